```python
import math
import jax, jax.numpy as jnp
from jax import lax
import numpy as np

D_MODEL = 1024
BATCH = 8
SEQ = 4096
DEPTH = 1

CHUNK = 64
EPS = 1e-6
HG_HEADS = 8
HG_DK = 128
HG_DV = D_MODEL // HG_HEADS
HG_WIDTH_K = HG_HEADS * HG_DK
HG_WIDTH_V = HG_HEADS * HG_DV
GDN_QK_HEADS = 8
GDN_V_HEADS = 16
GDN_DK = 128
GDN_DV = 128
GDN_WIDTH_K = GDN_QK_HEADS * GDN_DK
GDN_WIDTH_V = GDN_V_HEADS * GDN_DV
CONV_K = 4
D_FF = 2816
IN_SIZES = (HG_WIDTH_K, HG_WIDTH_K, HG_WIDTH_V, HG_WIDTH_V,
            GDN_WIDTH_K, GDN_WIDTH_K, GDN_WIDTH_V, GDN_V_HEADS, GDN_V_HEADS, GDN_WIDTH_V,
            D_MODEL, D_MODEL)
IN_WIDTH = sum(IN_SIZES)

kernel_name = "hgrn2_gdn_gated_macaron_block"


def rmsnorm(x, g):
    xf = x.astype(jnp.float32)
    y = xf * lax.rsqrt(jnp.mean(xf * xf, axis=-1, keepdims=True) + EPS)
    return (y * g).astype(x.dtype)


def l2norm(x):
    return x * lax.rsqrt(jnp.sum(x * x, axis=-1, keepdims=True) + EPS)


def swiglu(x, w_in, w_out):
    a, b = jnp.split(x @ w_in, 2, axis=-1)
    return (jax.nn.silu(a) * b) @ w_out


def to_chunks(t, n_heads):
    B, S = t.shape[:2]
    t = t.reshape(B, S // CHUNK, CHUNK, n_heads, -1)
    return jnp.transpose(t, (0, 3, 1, 2, 4))


def from_chunks(t):
    B, H, NC, C, d = t.shape
    return jnp.transpose(t, (0, 2, 3, 1, 4)).reshape(B, NC * C, H, d)


def causal_short_conv(x, w):
    K = w.shape[0]
    S = x.shape[1]
    xp = jnp.pad(x, ((0, 0), (K - 1, 0), (0, 0)))
    return sum(xp[:, j:j + S] * w[j] for j in range(K))


def hgrn2_chunked(q, k, v, log_f):
    B, H, NC, C, DK = q.shape
    DV = v.shape[-1]
    b_cum = jnp.cumsum(log_f, axis=3)
    causal = jnp.tril(jnp.ones((C, C), dtype=bool))[:, :, None]

    def step(S, inp):
        q_c, k_c, v_c, b_c = inp
        inter = jnp.einsum('bhtk,bhkv->bhtv', q_c * jnp.exp(b_c), S)
        diff = b_c[:, :, :, None, :] - b_c[:, :, None, :, :]
        decay = jnp.exp(jnp.where(causal, diff, -jnp.inf))
        scores = jnp.einsum('bhtk,bhsk,bhtsk->bhts', q_c, k_c, decay)
        intra = jnp.einsum('bhts,bhsv->bhtv', scores, v_c)
        b_end = b_c[:, :, -1, :]
        k_to_end = k_c * jnp.exp(b_end[:, :, None, :] - b_c)
        S = jnp.exp(b_end)[..., None] * S + jnp.einsum('bhsk,bhsv->bhkv', k_to_end, v_c)
        return S, inter + intra

    S0 = jnp.zeros((B, H, DK, DV), jnp.float32)
    xs = (jnp.moveaxis(q, 2, 0), jnp.moveaxis(k, 2, 0), jnp.moveaxis(v, 2, 0), jnp.moveaxis(b_cum, 2, 0))
    _, o = lax.scan(step, S0, xs)
    return jnp.moveaxis(o, 0, 2)


def gated_delta_chunked(q, k, v, beta, g):
    B, H, NC, C, DK = q.shape
    DV = v.shape[-1]
    gam = jnp.cumsum(g, axis=-1)
    incl = jnp.tril(jnp.ones((C, C), dtype=bool))
    strict = jnp.tril(jnp.ones((C, C), dtype=bool), -1)
    diff = gam[..., :, None] - gam[..., None, :]
    Lmat = jnp.exp(jnp.where(incl, diff, -jnp.inf))
    kb = k * beta[..., None]
    A = jnp.where(strict, jnp.einsum('bhntk,bhnsk->bhnts', kb, k) * Lmat, 0.0)
    eye = jnp.eye(C, dtype=A.dtype)
    T = lax.linalg.triangular_solve(eye + A, jnp.broadcast_to(eye, A.shape),
                                    left_side=True, lower=True, unit_diagonal=True)
    u = jnp.matmul(T, v * beta[..., None])
    w = jnp.matmul(T, kb * jnp.exp(gam)[..., None])
    qk = jnp.einsum('bhntk,bhnsk->bhnts', q, k) * Lmat

    def step(S, inp):
        q_c, k_c, u_c, w_c, qk_c, gam_c = inp
        v_new = u_c - jnp.einsum('bhtk,bhkv->bhtv', w_c, S)
        o = (jnp.einsum('bhtk,bhkv->bhtv', q_c * jnp.exp(gam_c)[..., None], S)
             + jnp.einsum('bhts,bhsv->bhtv', qk_c, v_new))
        g_end = gam_c[..., -1]
        k_to_end = k_c * jnp.exp(g_end[..., None] - gam_c)[..., None]
        S = S * jnp.exp(g_end)[..., None, None] + jnp.einsum('bhsk,bhsv->bhkv', k_to_end, v_new)
        return S, o

    S0 = jnp.zeros((B, H, DK, DV), jnp.float32)
    xs = tuple(jnp.moveaxis(t, 2, 0) for t in (q, k, u, w, qk, gam))
    _, o = lax.scan(step, S0, xs)
    return jnp.moveaxis(o, 0, 2)


def hybrid_mixer(u, w_in, lb, hgrn_out_norm, conv_w, a_log, dt_bias, gdn_out_norm,
                 w_branch_hgrn, w_branch_gdn, w_out):
    B, S, _ = u.shape
    f32 = jnp.float32
    proj = (u @ w_in).astype(f32)
    offsets = np.cumsum(IN_SIZES)[:-1].tolist()
    (hq, hf, hi, hg, gq, gk, gv, ga, gb, gz, gate_h, gate_g) = jnp.split(proj, offsets, axis=-1)

    lb = lb.astype(f32)
    log_f = jnp.logaddexp(jnp.log(lb), jnp.log1p(-lb) + jax.nn.log_sigmoid(hf))
    k_h = -jnp.expm1(log_f)
    q_h = jax.nn.silu(hq) * HG_DK ** -0.5
    o_h = hgrn2_chunked(to_chunks(q_h, HG_HEADS), to_chunks(k_h, HG_HEADS),
                        to_chunks(hi, HG_HEADS), to_chunks(log_f, HG_HEADS))
    o_h = rmsnorm(from_chunks(o_h), hgrn_out_norm) * jax.nn.silu(hg).reshape(B, S, HG_HEADS, HG_DV)
    y_h = o_h.reshape(B, S, HG_WIDTH_V) @ w_branch_hgrn

    qkv = jax.nn.silu(causal_short_conv(jnp.concatenate([gq, gk, gv], axis=-1), conv_w))
    cq, ck, cv = jnp.split(qkv, [GDN_WIDTH_K, 2 * GDN_WIDTH_K], axis=-1)
    rep = GDN_V_HEADS // GDN_QK_HEADS
    q_g = l2norm(cq.reshape(B, S, GDN_QK_HEADS, GDN_DK)) * GDN_DK ** -0.5
    k_g = l2norm(ck.reshape(B, S, GDN_QK_HEADS, GDN_DK))
    q_g = jnp.repeat(q_g, rep, axis=2).reshape(B, S, GDN_V_HEADS * GDN_DK)
    k_g = jnp.repeat(k_g, rep, axis=2).reshape(B, S, GDN_V_HEADS * GDN_DK)
    beta = jax.nn.sigmoid(gb)
    g = -jnp.exp(a_log.astype(f32)) * jax.nn.softplus(ga + dt_bias)
    o_g = gated_delta_chunked(to_chunks(q_g, GDN_V_HEADS), to_chunks(k_g, GDN_V_HEADS),
                              to_chunks(cv, GDN_V_HEADS),
                              to_chunks(beta[..., None], GDN_V_HEADS)[..., 0],
                              to_chunks(g[..., None], GDN_V_HEADS)[..., 0])
    o_g = rmsnorm(from_chunks(o_g), gdn_out_norm) * jax.nn.silu(gz).reshape(B, S, GDN_V_HEADS, GDN_DV)
    y_g = o_g.reshape(B, S, GDN_WIDTH_V) @ w_branch_gdn

    y = jax.nn.sigmoid(gate_h) * y_h + jax.nn.sigmoid(gate_g) * y_g
    return (y @ w_out).astype(u.dtype)


def _fwd_setup_inputs(seed: int = 0) -> dict:
    key = jax.random.key(seed)
    ks = jax.random.split(key, 20)
    f32 = jnp.float32
    L = DEPTH

    def dense(k, shape):
        return jax.random.normal(k, shape, f32) * shape[-2] ** -0.5

    def gain(k, shape):
        return 1.0 + 0.05 * jax.random.normal(k, shape, f32)

    A = jax.random.uniform(ks[8], (L, GDN_V_HEADS), f32, 1.0, 16.0)
    dt = jnp.exp(jax.random.uniform(ks[9], (L, GDN_V_HEADS), f32, math.log(1e-3), math.log(1e-1)))
    dt_bias = dt + jnp.log(-jnp.expm1(-dt))
    return {
        "x": jax.random.normal(ks[0], (BATCH, SEQ, D_MODEL), f32),
        "ffn1_norm": gain(ks[1], (L, D_MODEL)),
        "ffn1_w_in": dense(ks[2], (L, D_MODEL, 2 * D_FF)),
        "ffn1_w_out": dense(ks[3], (L, D_FF, D_MODEL)),
        "mix_norm": gain(ks[4], (L, D_MODEL)),
        "w_in": dense(ks[5], (L, D_MODEL, IN_WIDTH)),
        "hgrn_lb_logits": 0.5 * jax.random.normal(ks[6], (L + 1, HG_WIDTH_K), f32),
        "hgrn_out_norm": gain(ks[7], (L, HG_DV)),
        "gdn_conv_w": 0.5 * jax.random.normal(ks[10], (L, CONV_K, 2 * GDN_WIDTH_K + GDN_WIDTH_V), f32),
        "gdn_a_log": jnp.log(A),
        "gdn_dt_bias": dt_bias,
        "gdn_out_norm": gain(ks[11], (L, GDN_DV)),
        "w_branch_hgrn": dense(ks[12], (L, HG_WIDTH_V, D_MODEL)),
        "w_branch_gdn": dense(ks[13], (L, GDN_WIDTH_V, D_MODEL)),
        "w_out": dense(ks[14], (L, D_MODEL, D_MODEL)),
        "ffn2_norm": gain(ks[15], (L, D_MODEL)),
        "ffn2_w_in": dense(ks[16], (L, D_MODEL, 2 * D_FF)),
        "ffn2_w_out": dense(ks[17], (L, D_FF, D_MODEL)),
        "final_norm": gain(ks[18], (D_MODEL,)),
    }


def _fwd_reference(x, ffn1_norm, ffn1_w_in, ffn1_w_out, mix_norm, w_in, hgrn_lb_logits,
              hgrn_out_norm, gdn_conv_w, gdn_a_log, gdn_dt_bias, gdn_out_norm,
              w_branch_hgrn, w_branch_gdn, w_out, ffn2_norm, ffn2_w_in, ffn2_w_out,
              final_norm):
    lb_all = jnp.cumsum(jax.nn.softmax(hgrn_lb_logits.astype(jnp.float32), axis=0), axis=0)
    h = x
    for l in range(DEPTH):
        h = h + 0.5 * swiglu(rmsnorm(h, ffn1_norm[l]), ffn1_w_in[l], ffn1_w_out[l])
        h = h + hybrid_mixer(rmsnorm(h, mix_norm[l]), w_in[l], lb_all[l], hgrn_out_norm[l],
                             gdn_conv_w[l], gdn_a_log[l], gdn_dt_bias[l], gdn_out_norm[l],
                             w_branch_hgrn[l], w_branch_gdn[l], w_out[l])
        h = h + 0.5 * swiglu(rmsnorm(h, ffn2_norm[l]), ffn2_w_in[l], ffn2_w_out[l])
    return rmsnorm(h, final_norm)


import jax as _jax
import jax.numpy as _jnp

TWIN_FORMAT = 'train_step'
FWD_PARAMS = ['x', 'ffn1_norm', 'ffn1_w_in', 'ffn1_w_out', 'mix_norm', 'w_in', 'hgrn_lb_logits', 'hgrn_out_norm', 'gdn_conv_w', 'gdn_a_log', 'gdn_dt_bias', 'gdn_out_norm', 'w_branch_hgrn', 'w_branch_gdn', 'w_out', 'ffn2_norm', 'ffn2_w_in', 'ffn2_w_out', 'final_norm']
TWIN_WEIGHTS = ['ffn1_norm', 'ffn1_w_in', 'ffn1_w_out', 'mix_norm', 'w_in', 'hgrn_lb_logits', 'hgrn_out_norm', 'gdn_conv_w', 'gdn_a_log', 'gdn_dt_bias', 'gdn_out_norm', 'w_branch_hgrn', 'w_branch_gdn', 'w_out', 'ffn2_norm', 'ffn2_w_in', 'ffn2_w_out', 'final_norm']
TWIN_DIFF_INPUT = 'x'
TWIN_INPUTS = ['x', 'ffn1_norm', 'ffn1_w_in', 'ffn1_w_out', 'mix_norm', 'w_in', 'hgrn_lb_logits', 'hgrn_out_norm', 'gdn_conv_w', 'gdn_a_log', 'gdn_dt_bias', 'gdn_out_norm', 'w_branch_hgrn', 'w_branch_gdn', 'w_out', 'ffn2_norm', 'ffn2_w_in', 'ffn2_w_out', 'final_norm', 'loss_target', 'm_ffn1_norm', 'm_ffn1_w_in', 'm_ffn1_w_out', 'm_mix_norm', 'm_w_in', 'm_hgrn_lb_logits', 'm_hgrn_out_norm', 'm_gdn_conv_w', 'm_gdn_a_log', 'm_gdn_dt_bias', 'm_gdn_out_norm', 'm_w_branch_hgrn', 'm_w_branch_gdn', 'm_w_out', 'm_ffn2_norm', 'm_ffn2_w_in', 'm_ffn2_w_out', 'm_final_norm', 'v_ffn1_norm', 'v_ffn1_w_in', 'v_ffn1_w_out', 'v_mix_norm', 'v_w_in', 'v_hgrn_lb_logits', 'v_hgrn_out_norm', 'v_gdn_conv_w', 'v_gdn_a_log', 'v_gdn_dt_bias', 'v_gdn_out_norm', 'v_w_branch_hgrn', 'v_w_branch_gdn', 'v_w_out', 'v_ffn2_norm', 'v_ffn2_w_in', 'v_ffn2_w_out', 'v_final_norm']
TWIN_OUTPUTS = ['loss', 'grad_x', 'grad_ffn1_norm', 'grad_ffn1_w_in', 'grad_ffn1_w_out', 'grad_mix_norm', 'grad_w_in', 'grad_hgrn_lb_logits', 'grad_hgrn_out_norm', 'grad_gdn_conv_w', 'grad_gdn_a_log', 'grad_gdn_dt_bias', 'grad_gdn_out_norm', 'grad_w_branch_hgrn', 'grad_w_branch_gdn', 'grad_w_out', 'grad_ffn2_norm', 'grad_ffn2_w_in', 'grad_ffn2_w_out', 'grad_final_norm', 'delta_ffn1_norm', 'delta_ffn1_w_in', 'delta_ffn1_w_out', 'delta_mix_norm', 'delta_w_in', 'delta_hgrn_lb_logits', 'delta_hgrn_out_norm', 'delta_gdn_conv_w', 'delta_gdn_a_log', 'delta_gdn_dt_bias', 'delta_gdn_out_norm', 'delta_w_branch_hgrn', 'delta_w_branch_gdn', 'delta_w_out', 'delta_ffn2_norm', 'delta_ffn2_w_in', 'delta_ffn2_w_out', 'delta_final_norm', 'new_m_ffn1_norm', 'new_m_ffn1_w_in', 'new_m_ffn1_w_out', 'new_m_mix_norm', 'new_m_w_in', 'new_m_hgrn_lb_logits', 'new_m_hgrn_out_norm', 'new_m_gdn_conv_w', 'new_m_gdn_a_log', 'new_m_gdn_dt_bias', 'new_m_gdn_out_norm', 'new_m_w_branch_hgrn', 'new_m_w_branch_gdn', 'new_m_w_out', 'new_m_ffn2_norm', 'new_m_ffn2_w_in', 'new_m_ffn2_w_out', 'new_m_final_norm', 'new_v_ffn1_norm', 'new_v_ffn1_w_in', 'new_v_ffn1_w_out', 'new_v_mix_norm', 'new_v_w_in', 'new_v_hgrn_lb_logits', 'new_v_hgrn_out_norm', 'new_v_gdn_conv_w', 'new_v_gdn_a_log', 'new_v_gdn_dt_bias', 'new_v_gdn_out_norm', 'new_v_w_branch_hgrn', 'new_v_w_branch_gdn', 'new_v_w_out', 'new_v_ffn2_norm', 'new_v_ffn2_w_in', 'new_v_ffn2_w_out', 'new_v_final_norm']
TWIN_LEAF_KINDS = {'loss': 'loss', 'grad_x': 'grad_x', 'grad_ffn1_norm': 'grad_w', 'grad_ffn1_w_in': 'grad_w', 'grad_ffn1_w_out': 'grad_w', 'grad_mix_norm': 'grad_w', 'grad_w_in': 'grad_w', 'grad_hgrn_lb_logits': 'grad_w', 'grad_hgrn_out_norm': 'grad_w', 'grad_gdn_conv_w': 'grad_w', 'grad_gdn_a_log': 'grad_w', 'grad_gdn_dt_bias': 'grad_w', 'grad_gdn_out_norm': 'grad_w', 'grad_w_branch_hgrn': 'grad_w', 'grad_w_branch_gdn': 'grad_w', 'grad_w_out': 'grad_w', 'grad_ffn2_norm': 'grad_w', 'grad_ffn2_w_in': 'grad_w', 'grad_ffn2_w_out': 'grad_w', 'grad_final_norm': 'grad_w', 'delta_ffn1_norm': 'delta_w', 'delta_ffn1_w_in': 'delta_w', 'delta_ffn1_w_out': 'delta_w', 'delta_mix_norm': 'delta_w', 'delta_w_in': 'delta_w', 'delta_hgrn_lb_logits': 'delta_w', 'delta_hgrn_out_norm': 'delta_w', 'delta_gdn_conv_w': 'delta_w', 'delta_gdn_a_log': 'delta_w', 'delta_gdn_dt_bias': 'delta_w', 'delta_gdn_out_norm': 'delta_w', 'delta_w_branch_hgrn': 'delta_w', 'delta_w_branch_gdn': 'delta_w', 'delta_w_out': 'delta_w', 'delta_ffn2_norm': 'delta_w', 'delta_ffn2_w_in': 'delta_w', 'delta_ffn2_w_out': 'delta_w', 'delta_final_norm': 'delta_w', 'new_m_ffn1_norm': 'new_m', 'new_m_ffn1_w_in': 'new_m', 'new_m_ffn1_w_out': 'new_m', 'new_m_mix_norm': 'new_m', 'new_m_w_in': 'new_m', 'new_m_hgrn_lb_logits': 'new_m', 'new_m_hgrn_out_norm': 'new_m', 'new_m_gdn_conv_w': 'new_m', 'new_m_gdn_a_log': 'new_m', 'new_m_gdn_dt_bias': 'new_m', 'new_m_gdn_out_norm': 'new_m', 'new_m_w_branch_hgrn': 'new_m', 'new_m_w_branch_gdn': 'new_m', 'new_m_w_out': 'new_m', 'new_m_ffn2_norm': 'new_m', 'new_m_ffn2_w_in': 'new_m', 'new_m_ffn2_w_out': 'new_m', 'new_m_final_norm': 'new_m', 'new_v_ffn1_norm': 'new_v', 'new_v_ffn1_w_in': 'new_v', 'new_v_ffn1_w_out': 'new_v', 'new_v_mix_norm': 'new_v', 'new_v_w_in': 'new_v', 'new_v_hgrn_lb_logits': 'new_v', 'new_v_hgrn_out_norm': 'new_v', 'new_v_gdn_conv_w': 'new_v', 'new_v_gdn_a_log': 'new_v', 'new_v_gdn_dt_bias': 'new_v', 'new_v_gdn_out_norm': 'new_v', 'new_v_w_branch_hgrn': 'new_v', 'new_v_w_branch_gdn': 'new_v', 'new_v_w_out': 'new_v', 'new_v_ffn2_norm': 'new_v', 'new_v_ffn2_w_in': 'new_v', 'new_v_ffn2_w_out': 'new_v', 'new_v_final_norm': 'new_v'}


def _forward(args):
    return _fwd_reference(*[args[k] for k in FWD_PARAMS])


def _output_shape():
    def fwd():
        inp = _fwd_setup_inputs(0)
        return _fwd_reference(*[inp[k] for k in FWD_PARAMS])
    out = _jax.eval_shape(fwd)
    return out.shape, out.dtype

N_MICROBATCH = 1
ADAM_LR = 0.001
ADAM_B1 = 0.9
ADAM_B2 = 0.999
ADAM_EPS = 1e-08
ADAM_WD = 0.01
ADAM_STEP = 10
PER_EXAMPLE_BATCH_AXIS = {'x': 0, 'loss_target': 0}
SHARED_INPUTS = []
_WEIGHT_DTYPES = {'ffn1_norm': _jnp.float32, 'ffn1_w_in': _jnp.float32, 'ffn1_w_out': _jnp.float32, 'mix_norm': _jnp.float32, 'w_in': _jnp.float32, 'hgrn_lb_logits': _jnp.float32, 'hgrn_out_norm': _jnp.float32, 'gdn_conv_w': _jnp.float32, 'gdn_a_log': _jnp.float32, 'gdn_dt_bias': _jnp.float32, 'gdn_out_norm': _jnp.float32, 'w_branch_hgrn': _jnp.float32, 'w_branch_gdn': _jnp.float32, 'w_out': _jnp.float32, 'ffn2_norm': _jnp.float32, 'ffn2_w_in': _jnp.float32, 'ffn2_w_out': _jnp.float32, 'final_norm': _jnp.float32}
MOMENT_SCALE = {'ffn1_norm': 8.787137e-02, 'ffn1_w_in': 3.633944e-02, 'ffn1_w_out': 5.934493e-02, 'mix_norm': 1.199565e-01, 'w_in': 3.424675e-02, 'hgrn_lb_logits': 4.608224e-03, 'hgrn_out_norm': 1.821501e-01, 'gdn_conv_w': 3.447323e-02, 'gdn_a_log': 1.806275e-01, 'gdn_dt_bias': 1.741890e-01, 'gdn_out_norm': 1.468806e-01, 'w_branch_hgrn': 5.230461e-02, 'w_branch_gdn': 5.243915e-02, 'w_out': 7.476218e-02, 'ffn2_norm': 6.966704e-02, 'ffn2_w_in': 2.789464e-02, 'ffn2_w_out': 4.582551e-02, 'final_norm': 3.207083e+01}


def _to_microbatches(a, axis):
    t = _jnp.moveaxis(a, axis, 0)
    t = t.reshape((N_MICROBATCH, t.shape[0] // N_MICROBATCH) + t.shape[1:])
    return _jnp.moveaxis(t, 1, axis + 1)


def setup_inputs(seed: int = 0) -> dict:
    inp = _fwd_setup_inputs(seed)
    key = _jax.random.fold_in(_jax.random.key(seed), 7919)
    shape, _ = _output_shape()
    out = dict(inp)
    out["loss_target"] = _jax.random.normal(_jax.random.fold_in(key, 0), shape, _jnp.float32)
    for i, name in enumerate(TWIN_WEIGHTS):
        w = inp[name].astype(_jnp.float32)
        if MOMENT_SCALE is None:
            s = _jnp.sqrt(_jnp.mean(_jnp.square(w)) + 1e-30)
        else:
            s = MOMENT_SCALE[name]
        km, kv = _jax.random.split(_jax.random.fold_in(key, i + 1))
        out[name] = w
        out["m_" + name] = s * _jax.random.normal(km, w.shape, _jnp.float32)
        out["v_" + name] = (s * s) * _jax.random.uniform(kv, w.shape, _jnp.float32, 0.5, 1.5)
    if N_MICROBATCH > 1:
        for name, axis in PER_EXAMPLE_BATCH_AXIS.items():
            out[name] = _to_microbatches(out[name], axis)
    return {'x': out['x'], 'ffn1_norm': out['ffn1_norm'], 'ffn1_w_in': out['ffn1_w_in'], 'ffn1_w_out': out['ffn1_w_out'], 'mix_norm': out['mix_norm'], 'w_in': out['w_in'], 'hgrn_lb_logits': out['hgrn_lb_logits'], 'hgrn_out_norm': out['hgrn_out_norm'], 'gdn_conv_w': out['gdn_conv_w'], 'gdn_a_log': out['gdn_a_log'], 'gdn_dt_bias': out['gdn_dt_bias'], 'gdn_out_norm': out['gdn_out_norm'], 'w_branch_hgrn': out['w_branch_hgrn'], 'w_branch_gdn': out['w_branch_gdn'], 'w_out': out['w_out'], 'ffn2_norm': out['ffn2_norm'], 'ffn2_w_in': out['ffn2_w_in'], 'ffn2_w_out': out['ffn2_w_out'], 'final_norm': out['final_norm'], 'loss_target': out['loss_target'], 'm_ffn1_norm': out['m_ffn1_norm'], 'm_ffn1_w_in': out['m_ffn1_w_in'], 'm_ffn1_w_out': out['m_ffn1_w_out'], 'm_mix_norm': out['m_mix_norm'], 'm_w_in': out['m_w_in'], 'm_hgrn_lb_logits': out['m_hgrn_lb_logits'], 'm_hgrn_out_norm': out['m_hgrn_out_norm'], 'm_gdn_conv_w': out['m_gdn_conv_w'], 'm_gdn_a_log': out['m_gdn_a_log'], 'm_gdn_dt_bias': out['m_gdn_dt_bias'], 'm_gdn_out_norm': out['m_gdn_out_norm'], 'm_w_branch_hgrn': out['m_w_branch_hgrn'], 'm_w_branch_gdn': out['m_w_branch_gdn'], 'm_w_out': out['m_w_out'], 'm_ffn2_norm': out['m_ffn2_norm'], 'm_ffn2_w_in': out['m_ffn2_w_in'], 'm_ffn2_w_out': out['m_ffn2_w_out'], 'm_final_norm': out['m_final_norm'], 'v_ffn1_norm': out['v_ffn1_norm'], 'v_ffn1_w_in': out['v_ffn1_w_in'], 'v_ffn1_w_out': out['v_ffn1_w_out'], 'v_mix_norm': out['v_mix_norm'], 'v_w_in': out['v_w_in'], 'v_hgrn_lb_logits': out['v_hgrn_lb_logits'], 'v_hgrn_out_norm': out['v_hgrn_out_norm'], 'v_gdn_conv_w': out['v_gdn_conv_w'], 'v_gdn_a_log': out['v_gdn_a_log'], 'v_gdn_dt_bias': out['v_gdn_dt_bias'], 'v_gdn_out_norm': out['v_gdn_out_norm'], 'v_w_branch_hgrn': out['v_w_branch_hgrn'], 'v_w_branch_gdn': out['v_w_branch_gdn'], 'v_w_out': out['v_w_out'], 'v_ffn2_norm': out['v_ffn2_norm'], 'v_ffn2_w_in': out['v_ffn2_w_in'], 'v_ffn2_w_out': out['v_ffn2_w_out'], 'v_final_norm': out['v_final_norm']}


def _loss(weights, diff, rest, loss_target):
    with _jax.named_scope("forward"):
        args = {**rest, TWIN_DIFF_INPUT: diff, **{k: w.astype(_WEIGHT_DTYPES[k]) for k, w in weights.items()}}
        y = _forward(args)
    with _jax.named_scope("loss_head"):
        err = _jnp.square(y.astype(_jnp.float32) - loss_target)
        return 0.5 * _jnp.sum(_jnp.mean(err, axis=-1)) if err.ndim else 0.5 * err


def _adamw(w, g, m, v):
    m = ADAM_B1 * m + (1.0 - ADAM_B1) * g
    v = ADAM_B2 * v + (1.0 - ADAM_B2) * _jnp.square(g)
    m_hat = m / (1.0 - ADAM_B1 ** ADAM_STEP)
    v_hat = v / (1.0 - ADAM_B2 ** ADAM_STEP)
    delta = -ADAM_LR * (m_hat / (_jnp.sqrt(v_hat) + ADAM_EPS) + ADAM_WD * w)
    return delta, m, v


def reference(x, ffn1_norm, ffn1_w_in, ffn1_w_out, mix_norm, w_in, hgrn_lb_logits, hgrn_out_norm, gdn_conv_w, gdn_a_log, gdn_dt_bias, gdn_out_norm, w_branch_hgrn, w_branch_gdn, w_out, ffn2_norm, ffn2_w_in, ffn2_w_out, final_norm, loss_target, m_ffn1_norm, m_ffn1_w_in, m_ffn1_w_out, m_mix_norm, m_w_in, m_hgrn_lb_logits, m_hgrn_out_norm, m_gdn_conv_w, m_gdn_a_log, m_gdn_dt_bias, m_gdn_out_norm, m_w_branch_hgrn, m_w_branch_gdn, m_w_out, m_ffn2_norm, m_ffn2_w_in, m_ffn2_w_out, m_final_norm, v_ffn1_norm, v_ffn1_w_in, v_ffn1_w_out, v_mix_norm, v_w_in, v_hgrn_lb_logits, v_hgrn_out_norm, v_gdn_conv_w, v_gdn_a_log, v_gdn_dt_bias, v_gdn_out_norm, v_w_branch_hgrn, v_w_branch_gdn, v_w_out, v_ffn2_norm, v_ffn2_w_in, v_ffn2_w_out, v_final_norm):
    given = dict(x=x, ffn1_norm=ffn1_norm, ffn1_w_in=ffn1_w_in, ffn1_w_out=ffn1_w_out, mix_norm=mix_norm, w_in=w_in, hgrn_lb_logits=hgrn_lb_logits, hgrn_out_norm=hgrn_out_norm, gdn_conv_w=gdn_conv_w, gdn_a_log=gdn_a_log, gdn_dt_bias=gdn_dt_bias, gdn_out_norm=gdn_out_norm, w_branch_hgrn=w_branch_hgrn, w_branch_gdn=w_branch_gdn, w_out=w_out, ffn2_norm=ffn2_norm, ffn2_w_in=ffn2_w_in, ffn2_w_out=ffn2_w_out, final_norm=final_norm, loss_target=loss_target, m_ffn1_norm=m_ffn1_norm, m_ffn1_w_in=m_ffn1_w_in, m_ffn1_w_out=m_ffn1_w_out, m_mix_norm=m_mix_norm, m_w_in=m_w_in, m_hgrn_lb_logits=m_hgrn_lb_logits, m_hgrn_out_norm=m_hgrn_out_norm, m_gdn_conv_w=m_gdn_conv_w, m_gdn_a_log=m_gdn_a_log, m_gdn_dt_bias=m_gdn_dt_bias, m_gdn_out_norm=m_gdn_out_norm, m_w_branch_hgrn=m_w_branch_hgrn, m_w_branch_gdn=m_w_branch_gdn, m_w_out=m_w_out, m_ffn2_norm=m_ffn2_norm, m_ffn2_w_in=m_ffn2_w_in, m_ffn2_w_out=m_ffn2_w_out, m_final_norm=m_final_norm, v_ffn1_norm=v_ffn1_norm, v_ffn1_w_in=v_ffn1_w_in, v_ffn1_w_out=v_ffn1_w_out, v_mix_norm=v_mix_norm, v_w_in=v_w_in, v_hgrn_lb_logits=v_hgrn_lb_logits, v_hgrn_out_norm=v_hgrn_out_norm, v_gdn_conv_w=v_gdn_conv_w, v_gdn_a_log=v_gdn_a_log, v_gdn_dt_bias=v_gdn_dt_bias, v_gdn_out_norm=v_gdn_out_norm, v_w_branch_hgrn=v_w_branch_hgrn, v_w_branch_gdn=v_w_branch_gdn, v_w_out=v_w_out, v_ffn2_norm=v_ffn2_norm, v_ffn2_w_in=v_ffn2_w_in, v_ffn2_w_out=v_ffn2_w_out, v_final_norm=v_final_norm)
    weights = {n: given[n] for n in TWIN_WEIGHTS}
    shared = {n: given[n] for n in SHARED_INPUTS}
    per_example = {n: given[n] for n in ['x']}
    grad_fn = _jax.value_and_grad(_loss, argnums=(0, 1))

    def one_microbatch(ex, loss_target):
        ex = dict(ex)
        diff = ex.pop(TWIN_DIFF_INPUT)
        return grad_fn(weights, diff, {**shared, **ex}, loss_target)

    if N_MICROBATCH == 1:
        loss, (grad_w, grad_x) = one_microbatch(per_example, given["loss_target"])
    else:
        def body(carry, xs):
            loss_sum, grad_sum = carry
            l_k, (gw_k, gx_k) = one_microbatch(xs[0], xs[1])
            with _jax.named_scope("update"):
                return (loss_sum + l_k, _jax.tree.map(_jnp.add, grad_sum, gw_k)), gx_k

        init = (_jnp.zeros((), _jnp.float32), _jax.tree.map(_jnp.zeros_like, weights))
        (loss, grad_w), grad_x = _jax.lax.scan(body, init, (per_example, given["loss_target"]))
    with _jax.named_scope("update"):
        delta_w, new_m, new_v = {}, {}, {}
        for n in TWIN_WEIGHTS:
            delta_w[n], new_m[n], new_v[n] = _adamw(weights[n], grad_w[n], given["m_" + n], given["v_" + n])
    return (loss, grad_x, *[grad_w[n] for n in TWIN_WEIGHTS], *[delta_w[n] for n in TWIN_WEIGHTS],
            *[new_m[n] for n in TWIN_WEIGHTS], *[new_v[n] for n in TWIN_WEIGHTS])
```

```python
import numpy as np

import jax
import jax.numpy as jnp
from jax import lax
from jax.experimental import pallas as pl
from jax.experimental.pallas import tpu as pltpu

F32 = jnp.float32
BF16 = jnp.bfloat16

D_MODEL = 1024
D_FF = 2816
CHUNK = 64
HEAD = 128
HG_HEADS = 8
GD_HEADS = 16
EPS = 1e-6
CONV_K = 4
IN_NAMES = ("hq", "hf", "hi", "hg", "gq", "gk", "gv", "ga", "gb", "gz", "gate_h", "gate_g")
IN_SIZES = (1024, 1024, 1024, 1024, 1024, 1024, 2048, 16, 16, 2048, 1024, 1024)
IN_WIDTH = sum(IN_SIZES)

ADAM_LR = 0.001
ADAM_B1 = 0.9
ADAM_B2 = 0.999
ADAM_EPS = 1e-08
ADAM_WD = 0.01
ADAM_STEP = 10

MESH = pl.DeviceIdType.MESH
_ARB = "arbitrary"
_PAR = "parallel"


def _bf(x):
    return x.astype(BF16)


def _dot(a, b):
    return jnp.dot(_bf(a), _bf(b), preferred_element_type=F32)


def _dot_nt(a, b):
    return lax.dot_general(_bf(a), _bf(b), (((1,), (1,)), ((), ())), preferred_element_type=F32)


def _dot_tn(a, b):
    return lax.dot_general(_bf(a), _bf(b), (((0,), (0,)), ((), ())), preferred_element_type=F32)


def _split3(x):
    hi = _bf(x)
    r = x - hi.astype(F32)
    mid = _bf(r)
    lo = _bf(r - mid.astype(F32))
    return hi, mid, lo


def _dot_mx(m, x):
    hi, mid, lo = _split3(x)
    return (jnp.dot(m, hi, preferred_element_type=F32) + jnp.dot(m, mid, preferred_element_type=F32)
            + jnp.dot(m, lo, preferred_element_type=F32))


def _dot_xm(x, m):
    hi, mid, lo = _split3(x)
    return (jnp.dot(hi, m, preferred_element_type=F32) + jnp.dot(mid, m, preferred_element_type=F32)
            + jnp.dot(lo, m, preferred_element_type=F32))


def _dot_hp(a, b):
    ah = _bf(a)
    al = _bf(a - ah.astype(F32))
    bh = _bf(b)
    bl = _bf(b - bh.astype(F32))
    return (jnp.dot(ah, bh, preferred_element_type=F32) + jnp.dot(ah, bl, preferred_element_type=F32)
            + jnp.dot(al, bh, preferred_element_type=F32))


def _sigmoid(x):
    return jax.nn.sigmoid(x)


def _silu(x):
    return x * _sigmoid(x)


def _dsilu(x):
    s = _sigmoid(x)
    return s * (1.0 + x * (1.0 - s))


def _softplus(x):
    return jnp.maximum(x, 0.0) + jnp.log(1.0 + jnp.exp(-jnp.abs(x)))


def _rowsum(x):
    return jnp.sum(x, axis=1, keepdims=True)


def _col_to_row(col, eye):
    return jnp.sum(eye * col, axis=0, keepdims=True)


def _row_to_col(row, eye):
    return jnp.sum(eye * row, axis=1, keepdims=True)


def _pick(dim, pref, unit=128):
    if dim <= pref:
        return dim
    t = pref
    while t >= unit:
        if dim % t == 0:
            return t
        t -= unit
    return dim


def _params(*sem):
    return pltpu.CompilerParams(dimension_semantics=tuple(sem), vmem_limit_bytes=48 * 1024 * 1024)


def _mm(a, b, *, ta=False, tb=False, alpha=1.0, res=None, out_dtype=F32, name="mm"):
    m = a.shape[1] if ta else a.shape[0]
    k = a.shape[0] if ta else a.shape[1]
    n = b.shape[0] if tb else b.shape[1]
    assert k == (b.shape[1] if tb else b.shape[0])
    tm, tn, tk = _pick(m, 512), _pick(n, 512), _pick(k, 512)
    nk = k // tk
    a_spec = pl.BlockSpec((tk, tm), lambda i, j, l: (l, i)) if ta else pl.BlockSpec((tm, tk), lambda i, j, l: (i, l))
    b_spec = pl.BlockSpec((tn, tk), lambda i, j, l: (j, l)) if tb else pl.BlockSpec((tk, tn), lambda i, j, l: (l, j))
    o_spec = pl.BlockSpec((tm, tn), lambda i, j, l: (i, j))
    dims = (((0 if ta else 1,), (1 if tb else 0,)), ((), ()))
    has_res = res is not None

    def body(*refs):
        if has_res:
            a_ref, b_ref, r_ref, o_ref, acc = refs
        else:
            a_ref, b_ref, o_ref, acc = refs
        step = pl.program_id(2)

        @pl.when(step == 0)
        def _():
            acc[...] = jnp.zeros_like(acc)

        acc[...] += lax.dot_general(_bf(a_ref[...]), _bf(b_ref[...]), dims, preferred_element_type=F32)

        @pl.when(step == nk - 1)
        def _():
            r = acc[...]
            if alpha != 1.0:
                r = r * alpha
            if has_res:
                r = r + r_ref[...]
            o_ref[...] = r.astype(out_dtype)

    ins = [a, b] + ([res] if has_res else [])
    in_specs = [a_spec, b_spec] + ([o_spec] if has_res else [])
    return pl.pallas_call(
        body, name=name, grid=(m // tm, n // tn, nk), in_specs=in_specs, out_specs=o_spec,
        out_shape=jax.ShapeDtypeStruct((m, n), out_dtype), scratch_shapes=[pltpu.VMEM((tm, tn), F32)],
        compiler_params=_params(_PAR, _PAR, _ARB))(*ins)


def _row_spec(tr, w):
    return pl.BlockSpec((tr, w), lambda i: (i, 0))


def _full_spec(shape):
    return pl.BlockSpec(shape, lambda i: tuple(0 for _ in shape))


def _rmsnorm_fwd(x, g, name):
    t, d = x.shape
    tr = _pick(t, 256, 8)

    def body(x_ref, g_ref, o_ref):
        xv = x_ref[...]
        r = lax.rsqrt(jnp.mean(xv * xv, axis=1, keepdims=True) + EPS)
        o_ref[...] = (xv * r * g_ref[...]).astype(BF16)

    return pl.pallas_call(
        body, name=name, grid=(t // tr,), in_specs=[_row_spec(tr, d), _full_spec((1, d))],
        out_specs=_row_spec(tr, d), out_shape=jax.ShapeDtypeStruct((t, d), BF16),
        compiler_params=_params(_PAR))(x, g)


def _rmsnorm_bwd(x, g, dn, res, name):
    t, d = x.shape
    tr = _pick(t, 256, 8)

    def body(x_ref, g_ref, dn_ref, r_ref, dx_ref, dg_ref):
        @pl.when(pl.program_id(0) == 0)
        def _():
            dg_ref[...] = jnp.zeros_like(dg_ref)

        xv = x_ref[...]
        r = lax.rsqrt(jnp.mean(xv * xv, axis=1, keepdims=True) + EPS)
        xh = xv * r
        dy = dn_ref[...]
        dg_ref[...] += jnp.sum(dy * xh, axis=0, keepdims=True)
        dxh = dy * g_ref[...]
        dx_ref[...] = r_ref[...] + r * (dxh - xh * jnp.mean(dxh * xh, axis=1, keepdims=True))

    return pl.pallas_call(
        body, name=name, grid=(t // tr,),
        in_specs=[_row_spec(tr, d), _full_spec((1, d)), _row_spec(tr, d), _row_spec(tr, d)],
        out_specs=[_row_spec(tr, d), _full_spec((1, d))],
        out_shape=[jax.ShapeDtypeStruct((t, d), F32), jax.ShapeDtypeStruct((1, d), F32)],
        compiler_params=_params(_ARB))(x, g, dn, res)


def _swiglu_fwd(ab, name):
    t = ab.shape[0]
    tr = _pick(t, 256, 8)

    def body(ab_ref, o_ref):
        a = ab_ref[:, :D_FF].astype(F32)
        b = ab_ref[:, D_FF:].astype(F32)
        o_ref[...] = (_silu(a) * b).astype(BF16)

    return pl.pallas_call(
        body, name=name, grid=(t // tr,), in_specs=[_row_spec(tr, 2 * D_FF)], out_specs=_row_spec(tr, D_FF),
        out_shape=jax.ShapeDtypeStruct((t, D_FF), BF16), compiler_params=_params(_PAR))(ab)


def _swiglu_bwd(ab, dhm, name):
    t = ab.shape[0]
    tr = _pick(t, 256, 8)

    def body(ab_ref, dh_ref, dab_ref, hm_ref):
        a = ab_ref[:, :D_FF].astype(F32)
        b = ab_ref[:, D_FF:].astype(F32)
        dh = dh_ref[...]
        dab_ref[:, :D_FF] = (dh * b * _dsilu(a)).astype(BF16)
        sa = _silu(a)
        dab_ref[:, D_FF:] = (dh * sa).astype(BF16)
        hm_ref[...] = (sa * b).astype(BF16)

    return pl.pallas_call(
        body, name=name, grid=(t // tr,), in_specs=[_row_spec(tr, 2 * D_FF), _row_spec(tr, D_FF)],
        out_specs=[_row_spec(tr, 2 * D_FF), _row_spec(tr, D_FF)],
        out_shape=[jax.ShapeDtypeStruct((t, 2 * D_FF), BF16), jax.ShapeDtypeStruct((t, D_FF), BF16)],
        compiler_params=_params(_PAR))(ab, dhm)


def _merge_fwd(yh, yg, gh, gg):
    t, d = yh.shape
    tr = _pick(t, 256, 8)

    def body(yh_ref, yg_ref, gh_ref, gg_ref, o_ref):
        o_ref[...] = (_sigmoid(gh_ref[...]) * yh_ref[...] + _sigmoid(gg_ref[...]) * yg_ref[...]).astype(BF16)

    return pl.pallas_call(
        body, name="merge_fwd", grid=(t // tr,), in_specs=[_row_spec(tr, d)] * 4, out_specs=_row_spec(tr, d),
        out_shape=jax.ShapeDtypeStruct((t, d), BF16), compiler_params=_params(_PAR))(yh, yg, gh, gg)


def _merge_bwd(dy, yh, yg, gh, gg):
    t, d = yh.shape
    tr = _pick(t, 256, 8)

    def body(dy_ref, yh_ref, yg_ref, gh_ref, gg_ref, dyh_ref, dyg_ref, dgh_ref, dgg_ref):
        dyv = dy_ref[...]
        sh = _sigmoid(gh_ref[...])
        sg = _sigmoid(gg_ref[...])
        dyh_ref[...] = (dyv * sh).astype(BF16)
        dyg_ref[...] = (dyv * sg).astype(BF16)
        dgh_ref[...] = dyv * yh_ref[...] * sh * (1.0 - sh)
        dgg_ref[...] = dyv * yg_ref[...] * sg * (1.0 - sg)

    return pl.pallas_call(
        body, name="merge_bwd", grid=(t // tr,), in_specs=[_row_spec(tr, d)] * 5, out_specs=[_row_spec(tr, d)] * 4,
        out_shape=[jax.ShapeDtypeStruct((t, d), BF16), jax.ShapeDtypeStruct((t, d), BF16),
                   jax.ShapeDtypeStruct((t, d), F32), jax.ShapeDtypeStruct((t, d), F32)],
        compiler_params=_params(_PAR))(dy, yh, yg, gh, gg)


def _final_loss(h, g, tgt):
    t, d = h.shape
    tr = _pick(t, 256, 8)

    def body(h_ref, g_ref, t_ref, loss_ref, dh_ref, dg_ref):
        @pl.when(pl.program_id(0) == 0)
        def _():
            dg_ref[...] = jnp.zeros_like(dg_ref)
            loss_ref[...] = jnp.zeros_like(loss_ref)

        xv = h_ref[...]
        gv = g_ref[...]
        r = lax.rsqrt(jnp.mean(xv * xv, axis=1, keepdims=True) + EPS)
        xh = xv * r
        err = xh * gv - t_ref[...]
        loss_ref[...] += 0.5 * jnp.sum(jnp.mean(err * err, axis=1, keepdims=True), axis=0, keepdims=True)
        dy = err * (1.0 / d)
        dg_ref[...] += jnp.sum(dy * xh, axis=0, keepdims=True)
        dxh = dy * gv
        dh_ref[...] = r * (dxh - xh * jnp.mean(dxh * xh, axis=1, keepdims=True))

    return pl.pallas_call(
        body, name="final_loss", grid=(t // tr,),
        in_specs=[_row_spec(tr, d), _full_spec((1, d)), _row_spec(tr, d)],
        out_specs=[_full_spec((1, 128)), _row_spec(tr, d), _full_spec((1, d))],
        out_shape=[jax.ShapeDtypeStruct((1, 128), F32), jax.ShapeDtypeStruct((t, d), F32),
                   jax.ShapeDtypeStruct((1, d), F32)],
        compiler_params=_params(_ARB))(h, g, tgt)


def _hg_consts():
    c = CHUNK
    t = np.arange(c)
    mats, masks = [], []
    for lvl in range(6):
        m = 1 << lvl
        blk = t // m
        mat = np.zeros((c, c), np.float32)
        for tt in range(c):
            b = blk[tt]
            if b % 2 == 1:
                mat[tt, b * m:tt + 1] = 1.0
            else:
                mat[tt, tt + 1:(b + 1) * m] = 1.0
        mats.append(mat)
        same = (t[:, None] // (2 * m)) == (t[None, :] // (2 * m))
        masks.append((same & (blk[:, None] % 2 == 1) & (blk[None, :] % 2 == 0)).astype(np.float32))
    pre = np.tril(np.ones((c, c), np.float32))
    suf = np.triu(np.ones((c, c), np.float32), 1)
    mstack = np.concatenate(mats + [pre, suf], 0)
    masks.append(np.eye(c, dtype=np.float32))
    return (jnp.asarray(mstack, BF16), jnp.asarray(mstack.T.copy(), BF16), jnp.asarray(np.stack(masks), F32),
            jnp.asarray(np.eye(HEAD, dtype=np.float32)))


def _gd_consts():
    c = CHUNK
    incl = np.tril(np.ones((c, c), np.float32))
    strict = np.tril(np.ones((c, c), np.float32), -1)
    eye = np.eye(c, dtype=np.float32)
    masks = np.stack([incl, strict, eye, incl.T.copy()])
    sel = np.zeros((32, HEAD, HEAD), np.float32)
    for j in range(32):
        sel[j, j, :] = 1.0
    return (jnp.asarray(incl, BF16), jnp.asarray(incl.T.copy(), BF16), jnp.asarray(masks, F32), jnp.asarray(sel, BF16))


def _chunks_per_step(nc):
    for cb in (8, 4, 2, 1):
        if nc % cb == 0:
            return cb
    return 1


def _hg_prep(hq, hf, lg):
    lb = _sigmoid(lg[0:1, :] - lg[1:2, :])
    sg = _sigmoid(hf)
    sgn = _sigmoid(-hf)
    f = lb + (1.0 - lb) * sg
    lf = jnp.log(f)
    kk = (1.0 - lb) * sgn
    q = _silu(hq) * (HEAD ** -0.5)
    return lb, sg, sgn, f, lf, kk, q


def _hg_scores(q, kk, ex, mask_ref):
    p = mask_ref[6] * _rowsum(q * kk)
    for lvl in range(6):
        el = ex[lvl * CHUNK:(lvl + 1) * CHUNK]
        p = p + mask_ref[lvl] * _dot_nt(q * el, kk * el)
    return p


def _hgrn_fwd(hq, hf, hi, hg, logits, gain, consts):
    t = hq.shape[0]
    nc = t // CHUNK
    cb = _chunks_per_step(nc)
    rows = cb * CHUNK
    mstack, _, masks, eye = consts
    tile = pl.BlockSpec((rows, HEAD), lambda c, h: (c, h))

    def body(hq_ref, hf_ref, hi_ref, hg_ref, lg_ref, gain_ref, m_ref, mask_ref, eye_ref,
             oraw_ref, og_ref, ssave_ref, state):
        c = pl.program_id(0)
        h = pl.program_id(1)

        @pl.when(c == 0)
        def _():
            state[h] = jnp.zeros((HEAD, HEAD), F32)

        lg = lg_ref[...]
        gain_v = gain_ref[...]

        def one(i, carry):
            sl = pl.ds(pl.multiple_of(i * CHUNK, CHUNK), CHUNK)
            _, _, _, _, lf, kk, q = _hg_prep(hq_ref[sl, :], hf_ref[sl, :], lg)
            v = hi_ref[sl, :]
            ex = jnp.exp(_dot_mx(m_ref[...], lf))
            eb = ex[6 * CHUNK:7 * CHUNK]
            esfx = ex[7 * CHUNK:8 * CHUNK]
            p = _hg_scores(q, kk, ex, mask_ref)
            s0 = state[h]
            ssave_ref[i, 0] = s0
            o = _dot(q * eb, s0) + _dot(p, v)
            e_end = _row_to_col(eb[CHUNK - 1:CHUNK, :], eye_ref[...])
            state[h] = s0 * e_end + _dot_tn(kk * esfx, v)
            oraw_ref[sl, :] = o
            r = lax.rsqrt(jnp.mean(o * o, axis=1, keepdims=True) + EPS)
            og_ref[sl, :] = (o * r * gain_v * _silu(hg_ref[sl, :])).astype(BF16)
            return carry

        lax.fori_loop(0, cb, one, 0)

    return pl.pallas_call(
        body, name="hgrn_fwd", grid=(nc // cb, HG_HEADS),
        in_specs=[tile, tile, tile, tile, pl.BlockSpec((2, HEAD), lambda c, h: (0, h)),
                  pl.BlockSpec((1, HEAD), lambda c, h: (0, 0)),
                  pl.BlockSpec(mstack.shape, lambda c, h: (0, 0)),
                  pl.BlockSpec(masks.shape, lambda c, h: (0, 0, 0)),
                  pl.BlockSpec(eye.shape, lambda c, h: (0, 0))],
        out_specs=[tile, tile, pl.BlockSpec((cb, 1, HEAD, HEAD), lambda c, h: (c, h, 0, 0))],
        out_shape=[jax.ShapeDtypeStruct((t, HG_HEADS * HEAD), F32), jax.ShapeDtypeStruct((t, HG_HEADS * HEAD), BF16),
                   jax.ShapeDtypeStruct((nc, HG_HEADS, HEAD, HEAD), F32)],
        scratch_shapes=[pltpu.VMEM((HG_HEADS, HEAD, HEAD), F32)],
        compiler_params=_params(_ARB, _ARB))(hq, hf, hi, hg, logits, gain, mstack, masks, eye)


def _hgrn_bwd(hq, hf, hi, hg, logits, gain, oraw, ssave, dog, consts):
    t = hq.shape[0]
    nc = t // CHUNK
    cb = _chunks_per_step(nc)
    rows = cb * CHUNK
    nb = nc // cb
    mstack, mstack_t, masks, eye = consts
    tile = pl.BlockSpec((rows, HEAD), lambda c, h: (nb - 1 - c, h))

    def body(hq_ref, hf_ref, hi_ref, hg_ref, lg_ref, gain_ref, oraw_ref, ssave_ref, dog_ref, m_ref, mt_ref,
             mask_ref, eye_ref, dhq_ref, dhf_ref, dhi_ref, dhg_ref, dgain_ref, dlb_ref, dstate):
        c = pl.program_id(0)
        h = pl.program_id(1)

        @pl.when(c == 0)
        def _():
            dstate[h] = jnp.zeros((HEAD, HEAD), F32)

        @pl.when((c == 0) & (h == 0))
        def _():
            dgain_ref[...] = jnp.zeros_like(dgain_ref)
            dlb_ref[...] = jnp.zeros_like(dlb_ref)

        lg = lg_ref[...]
        gain_v = gain_ref[...]
        eye_v = eye_ref[...]
        last_row = (lax.broadcasted_iota(jnp.int32, (CHUNK, HEAD), 0) == CHUNK - 1).astype(F32)

        def one(j, carry):
            i = cb - 1 - j
            sl = pl.ds(pl.multiple_of(i * CHUNK, CHUNK), CHUNK)
            hqv = hq_ref[sl, :]
            hfv = hf_ref[sl, :]
            hgv = hg_ref[sl, :]
            lb, sg, sgn, f, lf, kk, q = _hg_prep(hqv, hfv, lg)
            v = hi_ref[sl, :]
            ex = jnp.exp(_dot_mx(m_ref[...], lf))
            eb = ex[6 * CHUNK:7 * CHUNK]
            esfx = ex[7 * CHUNK:8 * CHUNK]
            p = _hg_scores(q, kk, ex, mask_ref)
            s0 = ssave_ref[i, 0]
            ds = dstate[h]

            o = oraw_ref[sl, :]
            r = lax.rsqrt(jnp.mean(o * o, axis=1, keepdims=True) + EPS)
            on = o * r
            dg_out = dog_ref[sl, :]
            sgate = _silu(hgv)
            dhg_ref[sl, :] = dg_out * on * gain_v * _dsilu(hgv)
            dgain_ref[...] += jnp.sum(dg_out * sgate * on, axis=0, keepdims=True)
            don = dg_out * sgate * gain_v
            do = r * (don - on * jnp.mean(don * on, axis=1, keepdims=True))

            dp = _dot_nt(do, v)
            dv = _dot_tn(p, do) + _dot(kk * esfx, ds)
            dqb = _dot_nt(do, s0)
            dkx = _dot_nt(v, ds)
            diag = _rowsum(mask_ref[6] * dp)
            dq = dqb * eb + diag * kk
            dk = dkx * esfx + diag * q
            dxs = []
            for lvl in range(6):
                el = ex[lvl * CHUNK:(lvl + 1) * CHUNK]
                g = mask_ref[lvl] * dp
                a1 = _dot(g, kk * el)
                a2 = _dot_tn(g, q * el)
                dq = dq + a1 * el
                dk = dk + a2 * el
                dxs.append((a1 * q + a2 * kk) * el)
            e_end_row = eb[CHUNK - 1:CHUNK, :]
            e_end = _row_to_col(e_end_row, eye_v)
            dstate[h] = _dot_tn(q * eb, do) + e_end * ds
            dend_row = _col_to_row(_rowsum(s0 * ds), eye_v)
            dxs.append(dqb * q * eb + last_row * (e_end_row * dend_row))
            dxs.append(dkx * kk * esfx)
            dlf = _dot_mx(mt_ref[...], jnp.concatenate(dxs, axis=0))

            dhi_ref[sl, :] = dv
            dhq_ref[sl, :] = dq * (HEAD ** -0.5) * _dsilu(hqv)
            df = dlf / f
            dsig = (1.0 - lb) * sg * sgn
            dhf_ref[sl, :] = (df - dk) * dsig
            dlb_t = jnp.sum(df * sgn - dk * sgn, axis=0, keepdims=True)
            dlb_ref[pl.ds(h, 1), :] += dlb_t * lb * (1.0 - lb)
            return carry

        lax.fori_loop(0, cb, one, 0)

    outs = [jax.ShapeDtypeStruct((t, HG_HEADS * HEAD), F32)] * 4 + [
        jax.ShapeDtypeStruct((1, HEAD), F32), jax.ShapeDtypeStruct((HG_HEADS, HEAD), F32)]
    return pl.pallas_call(
        body, name="hgrn_bwd", grid=(nb, HG_HEADS),
        in_specs=[tile, tile, tile, tile, pl.BlockSpec((2, HEAD), lambda c, h: (0, h)),
                  pl.BlockSpec((1, HEAD), lambda c, h: (0, 0)), tile,
                  pl.BlockSpec((cb, 1, HEAD, HEAD), lambda c, h: (nb - 1 - c, h, 0, 0)), tile,
                  pl.BlockSpec(mstack.shape, lambda c, h: (0, 0)),
                  pl.BlockSpec(mstack_t.shape, lambda c, h: (0, 0)),
                  pl.BlockSpec(masks.shape, lambda c, h: (0, 0, 0)),
                  pl.BlockSpec(eye.shape, lambda c, h: (0, 0))],
        out_specs=[tile, tile, tile, tile, pl.BlockSpec((1, HEAD), lambda c, h: (0, 0)),
                   pl.BlockSpec((HG_HEADS, HEAD), lambda c, h: (0, 0))],
        out_shape=outs, scratch_shapes=[pltpu.VMEM((HG_HEADS, HEAD, HEAD), F32)],
        compiler_params=_params(_ARB, _ARB))(hq, hf, hi, hg, logits, gain, oraw, ssave, dog, mstack, mstack_t,
                                             masks, eye)


def _shift_down(xv, halo, d, top_rows):
    if d == 0:
        return xv, xv[0:8]
    main = pltpu.roll(xv, d, 0)
    top = jnp.where(top_rows < d, pltpu.roll(halo, d, 0), main[0:8])
    return main, top


def _conv_parts(x_ref, halo_ref, w_ref, first):
    xv = x_ref[...]
    halo = jnp.where(first, 0.0, halo_ref[...])
    top_rows = lax.broadcasted_iota(jnp.int32, (8, xv.shape[1]), 0)
    shifted = [_shift_down(xv, halo, CONV_K - 1 - j, top_rows) for j in range(CONV_K)]
    w = w_ref[...]
    acc = sum(shifted[j][0] * w[j:j + 1, :] for j in range(CONV_K))
    acc_top = sum(shifted[j][1] * w[j:j + 1, :] for j in range(CONV_K))
    return shifted, acc, acc_top


def _conv_fwd(x, w8, l2scale, name):
    t, width = x.shape
    tr = _pick(t, 512, 8)

    def post(cv):
        s = _silu(cv)
        if l2scale is not None:
            s = s * (lax.rsqrt(_rowsum(s * s) + EPS) * l2scale)
        return s

    def body(x_ref, halo_ref, w_ref, o_ref):
        _, acc, acc_top = _conv_parts(x_ref, halo_ref, w_ref, pl.program_id(1) == 0)
        o_ref[...] = post(acc)
        o_ref[0:8, :] = post(acc_top)

    return pl.pallas_call(
        body, name=name, grid=(width // HEAD, t // tr),
        in_specs=[pl.BlockSpec((tr, HEAD), lambda j, i: (i, j)),
                  pl.BlockSpec((8, HEAD), lambda j, i: (jnp.maximum(i * (tr // 8) - 1, 0), j)),
                  pl.BlockSpec((8, HEAD), lambda j, i: (0, j))],
        out_specs=pl.BlockSpec((tr, HEAD), lambda j, i: (i, j)),
        out_shape=jax.ShapeDtypeStruct((t, width), F32), compiler_params=_params(_PAR, _PAR))(x, x, w8)


def _conv_bwd_a(x, w8, dy, l2scale, name):
    t, width = x.shape
    tr = _pick(t, 512, 8)

    def to_dc(cv, dyv):
        if l2scale is not None:
            s = _silu(cv)
            r = lax.rsqrt(_rowsum(s * s) + EPS)
            y0 = s * r
            dy0 = dyv * l2scale
            dyv = r * (dy0 - y0 * _rowsum(dy0 * y0))
        return dyv * _dsilu(cv)

    def body(x_ref, halo_ref, w_ref, dy_ref, dc_ref, dw_ref):
        @pl.when(pl.program_id(1) == 0)
        def _():
            dw_ref[...] = jnp.zeros_like(dw_ref)

        shifted, acc, acc_top = _conv_parts(x_ref, halo_ref, w_ref, pl.program_id(1) == 0)
        dyv = dy_ref[...]
        dc = to_dc(acc, dyv)
        dc_top = to_dc(acc_top, dyv[0:8])
        dc_ref[...] = dc
        dc_ref[0:8, :] = dc_top
        rest = (lax.broadcasted_iota(jnp.int32, dc.shape, 0) >= 8).astype(F32)
        dc_rest = dc * rest
        for j in range(CONV_K):
            dw_ref[j:j + 1, :] += (jnp.sum(dc_rest * shifted[j][0], axis=0, keepdims=True)
                                   + jnp.sum(dc_top * shifted[j][1], axis=0, keepdims=True))

    return pl.pallas_call(
        body, name=name, grid=(width // HEAD, t // tr),
        in_specs=[pl.BlockSpec((tr, HEAD), lambda j, i: (i, j)),
                  pl.BlockSpec((8, HEAD), lambda j, i: (jnp.maximum(i * (tr // 8) - 1, 0), j)),
                  pl.BlockSpec((8, HEAD), lambda j, i: (0, j)),
                  pl.BlockSpec((tr, HEAD), lambda j, i: (i, j))],
        out_specs=[pl.BlockSpec((tr, HEAD), lambda j, i: (i, j)), pl.BlockSpec((8, HEAD), lambda j, i: (0, j))],
        out_shape=[jax.ShapeDtypeStruct((t, width), F32), jax.ShapeDtypeStruct((8, width), F32)],
        compiler_params=_params(_PAR, _ARB))(x, x, w8, dy)


def _conv_bwd_b(dc, w8, name):
    t, width = dc.shape
    tr = _pick(t, 512, 8)
    nt = t // tr

    def body(dc_ref, halo_ref, w_ref, dx_ref):
        dcv = dc_ref[...]
        halo = jnp.where(pl.program_id(1) == nt - 1, 0.0, halo_ref[...])
        w = w_ref[...]
        bot_rows = lax.broadcasted_iota(jnp.int32, (8, HEAD), 0)
        acc = dcv * w[CONV_K - 1:CONV_K, :]
        acc_bot = dcv[tr - 8:tr] * w[CONV_K - 1:CONV_K, :]
        for d in range(1, CONV_K):
            main = pltpu.roll(dcv, tr - d, 0)
            bot = jnp.where(bot_rows >= 8 - d, pltpu.roll(halo, 8 - d, 0), main[tr - 8:tr])
            wj = w[CONV_K - 1 - d:CONV_K - d, :]
            acc = acc + main * wj
            acc_bot = acc_bot + bot * wj
        dx_ref[...] = acc
        dx_ref[tr - 8:tr, :] = acc_bot

    return pl.pallas_call(
        body, name=name, grid=(width // HEAD, nt),
        in_specs=[pl.BlockSpec((tr, HEAD), lambda j, i: (i, j)),
                  pl.BlockSpec((8, HEAD), lambda j, i: (jnp.minimum((i + 1) * (tr // 8), t // 8 - 1), j)),
                  pl.BlockSpec((8, HEAD), lambda j, i: (0, j))],
        out_specs=pl.BlockSpec((tr, HEAD), lambda j, i: (i, j)),
        out_shape=jax.ShapeDtypeStruct((t, width), F32), compiler_params=_params(_PAR, _PAR))(dc, dc, w8)


def _tri_inv(a, eye):
    n = -a
    p = eye + n
    for _ in range(5):
        n = _dot_hp(n, n)
        p = p + _dot_hp(p, n)
    return p


def _gd_chunk(q, k, v, gab, alog, dtb, selg, selb, l_ref, mask_ref):
    incl, strict, eye, upper = mask_ref[0], mask_ref[1], mask_ref[2], mask_ref[3]
    sp_arg = gab + dtb
    g_all = -jnp.exp(alog) * _softplus(sp_arg)
    beta_all = _sigmoid(gab)
    gb = _dot_xm(g_all, selg)
    bb = _dot_xm(beta_all, selb)
    gam = _dot_mx(l_ref[...], gb)
    gam_row = jnp.sum(gb[:, :CHUNK] * upper, axis=0, keepdims=True)
    lm = incl * jnp.exp(jnp.minimum(gam[:, :CHUNK] - gam_row, 0.0))
    kb = k * bb
    a = strict * _dot_nt(kb, k) * lm
    tm = _tri_inv(a, eye)
    eg = jnp.exp(gam)
    vb = v * bb
    kbg = kb * eg
    u = _dot(tm, vb)
    w = _dot(tm, kbg)
    qk = _dot_nt(q, k) * lm
    g_end = gam[CHUNK - 1:CHUNK, :]
    ekg = jnp.exp(g_end - gam)
    ge = jnp.exp(g_end)
    return dict(g_all=g_all, beta_all=beta_all, sp_arg=sp_arg, gb=gb, bb=bb, gam=gam, lm=lm, kb=kb, a=a, tm=tm,
                eg=eg, vb=vb, kbg=kbg, u=u, w=w, qk=qk, ekg=ekg, ge=ge, kg=k * ekg, qg=q * eg)


def _gd_specs(rows, rev_nb=None):
    def cidx(c):
        return c if rev_nb is None else rev_nb - 1 - c

    qk_tile = pl.BlockSpec((rows, HEAD), lambda c, h: (cidx(c), h // 2))
    v_tile = pl.BlockSpec((rows, HEAD), lambda c, h: (cidx(c), h))
    gab_tile = pl.BlockSpec((rows, HEAD), lambda c, h: (cidx(c), 0))
    return qk_tile, v_tile, gab_tile


def _gdn_fwd(qn, kn, cv, gab, gz, alog, dtb, gain, consts):
    t = qn.shape[0]
    nc = t // CHUNK
    cb = _chunks_per_step(nc)
    rows = cb * CHUNK
    lmat, _, masks, sel = consts
    qk_tile, v_tile, gab_tile = _gd_specs(rows)
    row128 = pl.BlockSpec((1, HEAD), lambda c, h: (0, 0))

    def body(q_ref, k_ref, v_ref, gab_ref, gz_ref, alog_ref, dtb_ref, gain_ref, selg_ref, selb_ref, l_ref, mask_ref,
             oraw_ref, og_ref, ssave_ref, state):
        c = pl.program_id(0)
        h = pl.program_id(1)

        @pl.when(c == 0)
        def _():
            state[h] = jnp.zeros((HEAD, HEAD), F32)

        alog = alog_ref[...]
        dtb = dtb_ref[...]
        gain_v = gain_ref[...]

        def one(i, carry):
            sl = pl.ds(pl.multiple_of(i * CHUNK, CHUNK), CHUNK)
            q = q_ref[sl, :]
            k = k_ref[sl, :]
            v = v_ref[sl, :]
            ch = _gd_chunk(q, k, v, gab_ref[sl, :], alog, dtb, selg_ref[0], selb_ref[0], l_ref, mask_ref)
            s0 = state[h]
            ssave_ref[i, 0] = s0
            v_new = ch["u"] - _dot(ch["w"], s0)
            o = _dot(ch["qg"], s0) + _dot(ch["qk"], v_new)
            state[h] = s0 * ch["ge"] + _dot_tn(ch["kg"], v_new)
            oraw_ref[sl, :] = o
            r = lax.rsqrt(jnp.mean(o * o, axis=1, keepdims=True) + EPS)
            og_ref[sl, :] = (o * r * gain_v * _silu(gz_ref[sl, :])).astype(BF16)
            return carry

        lax.fori_loop(0, cb, one, 0)

    return pl.pallas_call(
        body, name="gdn_fwd", grid=(nc // cb, GD_HEADS),
        in_specs=[qk_tile, qk_tile, v_tile, gab_tile, v_tile, row128, row128, row128,
                  pl.BlockSpec((1, HEAD, HEAD), lambda c, h: (h, 0, 0)),
                  pl.BlockSpec((1, HEAD, HEAD), lambda c, h: (GD_HEADS + h, 0, 0)),
                  pl.BlockSpec(lmat.shape, lambda c, h: (0, 0)),
                  pl.BlockSpec(masks.shape, lambda c, h: (0, 0, 0))],
        out_specs=[v_tile, v_tile, pl.BlockSpec((cb, 1, HEAD, HEAD), lambda c, h: (c, h, 0, 0))],
        out_shape=[jax.ShapeDtypeStruct((t, GD_HEADS * HEAD), F32), jax.ShapeDtypeStruct((t, GD_HEADS * HEAD), BF16),
                   jax.ShapeDtypeStruct((nc, GD_HEADS, HEAD, HEAD), F32)],
        scratch_shapes=[pltpu.VMEM((GD_HEADS, HEAD, HEAD), F32)],
        compiler_params=_params(_ARB, _ARB))(qn, kn, cv, gab, gz, alog, dtb, gain, sel, sel, lmat, masks)


def _gdn_bwd(qn, kn, cv, gab, gz, alog, dtb, gain, oraw, ssave, dog, consts):
    t = qn.shape[0]
    nc = t // CHUNK
    cb = _chunks_per_step(nc)
    rows = cb * CHUNK
    nb = nc // cb
    lmat, lmat_t, masks, sel = consts
    qk_tile, v_tile, gab_tile = _gd_specs(rows, nb)
    row128 = pl.BlockSpec((1, HEAD), lambda c, h: (0, 0))

    def body(q_ref, k_ref, v_ref, gab_ref, gz_ref, alog_ref, dtb_ref, gain_ref, oraw_ref, ssave_ref, dog_ref,
             selg_ref, selb_ref, l_ref, lt_ref, mask_ref,
             dq_ref, dk_ref, dv_ref, dgab_ref, dgz_ref, small_ref, dstate):
        c = pl.program_id(0)
        h = pl.program_id(1)

        @pl.when(c == 0)
        def _():
            dstate[h] = jnp.zeros((HEAD, HEAD), F32)

        @pl.when((c == 0) & (h == 0))
        def _():
            small_ref[...] = jnp.zeros_like(small_ref)

        alog = alog_ref[...]
        dtb = dtb_ref[...]
        gain_v = gain_ref[...]
        lane = lax.broadcasted_iota(jnp.int32, (1, HEAD), 1)
        hot_g = (lane == h).astype(F32)
        hot_b = (lane == GD_HEADS + h).astype(F32)
        last_row = (lax.broadcasted_iota(jnp.int32, (CHUNK, HEAD), 0) == CHUNK - 1).astype(F32)
        first_of_pair = (h % 2) == 0

        def one(j, carry):
            i = cb - 1 - j
            sl = pl.ds(pl.multiple_of(i * CHUNK, CHUNK), CHUNK)
            q = q_ref[sl, :]
            k = k_ref[sl, :]
            v = v_ref[sl, :]
            gzv = gz_ref[sl, :]
            ch = _gd_chunk(q, k, v, gab_ref[sl, :], alog, dtb, selg_ref[0], selb_ref[0], l_ref, mask_ref)
            strict, eye = mask_ref[1], mask_ref[2]
            s0 = ssave_ref[i, 0]
            ds = dstate[h]
            tm, lm, eg, bb = ch["tm"], ch["lm"], ch["eg"], ch["bb"]
            v_new = ch["u"] - _dot(ch["w"], s0)

            o = oraw_ref[sl, :]
            r = lax.rsqrt(jnp.mean(o * o, axis=1, keepdims=True) + EPS)
            on = o * r
            dg_out = dog_ref[sl, :]
            sgate = _silu(gzv)
            dgz_ref[sl, :] = dg_out * on * gain_v * _dsilu(gzv)
            small_ref[0:1, :] += jnp.sum(dg_out * sgate * on, axis=0, keepdims=True)
            don = dg_out * sgate * gain_v
            do = r * (don - on * jnp.mean(don * on, axis=1, keepdims=True))

            dv_new = _dot_tn(ch["qk"], do) + _dot(ch["kg"], ds)
            dqk = _dot_nt(do, v_new)
            dqg = _dot_nt(do, s0)
            dkg = _dot_nt(v_new, ds)
            dge = jnp.sum(_rowsum(s0 * ds), axis=0, keepdims=True)
            dw = -_dot_nt(dv_new, s0)
            dstate[h] = _dot_tn(ch["qg"], do) + ch["ge"] * ds - _dot_tn(ch["w"], dv_new)

            dvb = _dot_tn(tm, dv_new)
            dkbg = _dot_tn(tm, dw)
            dtm = _dot_nt(dv_new, ch["vb"]) + _dot_nt(dw, ch["kbg"])
            da = -_dot_tn(tm, _dot_nt(dtm, tm)) * strict
            dal = da * lm
            dkb = _dot(dal, k) + dkbg * eg
            dqk_l = dqk * lm
            dq = _dot(dqk_l, k) + dqg * eg
            dk = _dot_tn(dal, ch["kb"]) + _dot_tn(dqk_l, q) + dkg * ch["ekg"] + dkb * bb
            gmat = da * ch["a"] + dqk * ch["qk"]
            t_kg = _rowsum(dkg * ch["kg"])
            dgam = (_rowsum(gmat) - _row_to_col(jnp.sum(gmat, axis=0, keepdims=True), eye)
                    + _rowsum(dqg * ch["qg"]) - t_kg + _rowsum(dkbg * ch["kbg"]))
            dg_end = jnp.sum(t_kg, axis=0, keepdims=True) + dge * ch["ge"][:, 0:1]
            dgam = dgam + last_row * dg_end
            dbeta = _rowsum(dkb * k) + _rowsum(dvb * v)
            dg = _dot_mx(lt_ref[...], jnp.broadcast_to(dgam, (CHUNK, HEAD)))

            dv_ref[sl, :] = dvb * bb

            @pl.when(first_of_pair)
            def _():
                dq_ref[sl, :] = dq
                dk_ref[sl, :] = dk

            @pl.when(jnp.logical_not(first_of_pair))
            def _():
                dq_ref[sl, :] += dq
                dk_ref[sl, :] += dk

            dga = dg * hot_g * (-jnp.exp(alog)) * _sigmoid(ch["sp_arg"])
            dgb = dbeta * hot_b * ch["beta_all"] * (1.0 - ch["beta_all"])
            contrib = dga + dgb

            @pl.when(h == 0)
            def _():
                dgab_ref[sl, :] = contrib

            @pl.when(h != 0)
            def _():
                dgab_ref[sl, :] += contrib

            small_ref[1:2, :] += jnp.sum(dga, axis=0, keepdims=True)
            small_ref[2:3, :] += jnp.sum(dg * hot_g * ch["g_all"], axis=0, keepdims=True)
            return carry

        lax.fori_loop(0, cb, one, 0)

    outs = [jax.ShapeDtypeStruct((t, 1024), F32), jax.ShapeDtypeStruct((t, 1024), F32),
            jax.ShapeDtypeStruct((t, 2048), F32), jax.ShapeDtypeStruct((t, HEAD), F32),
            jax.ShapeDtypeStruct((t, 2048), F32), jax.ShapeDtypeStruct((8, HEAD), F32)]
    return pl.pallas_call(
        body, name="gdn_bwd", grid=(nb, GD_HEADS),
        in_specs=[qk_tile, qk_tile, v_tile, gab_tile, v_tile, row128, row128, row128, v_tile,
                  pl.BlockSpec((cb, 1, HEAD, HEAD), lambda c, h: (nb - 1 - c, h, 0, 0)), v_tile,
                  pl.BlockSpec((1, HEAD, HEAD), lambda c, h: (h, 0, 0)),
                  pl.BlockSpec((1, HEAD, HEAD), lambda c, h: (GD_HEADS + h, 0, 0)),
                  pl.BlockSpec(lmat.shape, lambda c, h: (0, 0)),
                  pl.BlockSpec(lmat_t.shape, lambda c, h: (0, 0)),
                  pl.BlockSpec(masks.shape, lambda c, h: (0, 0, 0))],
        out_specs=[qk_tile, qk_tile, v_tile, gab_tile, v_tile, pl.BlockSpec((8, HEAD), lambda c, h: (0, 0))],
        out_shape=outs, scratch_shapes=[pltpu.VMEM((GD_HEADS, HEAD, HEAD), F32)],
        compiler_params=_params(_ARB, _ARB))(qn, kn, cv, gab, gz, alog, dtb, gain, oraw, ssave, dog, sel, sel,
                                             lmat, lmat_t, masks)


def _adam_math(w, g, m, v):
    m2 = ADAM_B1 * m + (1.0 - ADAM_B1) * g
    v2 = ADAM_B2 * v + (1.0 - ADAM_B2) * (g * g)
    m_hat = m2 / (1.0 - ADAM_B1 ** ADAM_STEP)
    v_hat = v2 / (1.0 - ADAM_B2 ** ADAM_STEP)
    delta = -ADAM_LR * (m_hat / (jnp.sqrt(v_hat) + ADAM_EPS) + ADAM_WD * w)
    return delta, m2, v2


def _adamw(w, g, m, v, name):
    r, c = w.shape
    tr = r
    for cand in range(8, r + 1, 8):
        if r % cand == 0 and cand * c * 4 <= (1 << 20):
            tr = cand
    if r % 8 != 0:
        tr = r

    def body(w_ref, g_ref, m_ref, v_ref, d_ref, m2_ref, v2_ref):
        d, m2, v2 = _adam_math(w_ref[...], g_ref[...], m_ref[...], v_ref[...])
        d_ref[...] = d
        m2_ref[...] = m2
        v2_ref[...] = v2

    spec = pl.BlockSpec((tr, c), lambda i: (i, 0))
    return pl.pallas_call(
        body, name=name, grid=(r // tr,), in_specs=[spec] * 4, out_specs=[spec] * 3,
        out_shape=[jax.ShapeDtypeStruct((r, c), F32)] * 3, compiler_params=_params(_PAR))(w, g, m, v)


_ANY = pl.BlockSpec(memory_space=pl.ANY)


def _place():
    return lax.axis_index("x"), lax.axis_index("y"), lax.axis_index("c")


def _gather_weights(packed):
    rows = packed.shape[0]
    half = rows // 2

    def body(p_ref, g_ref, send_sems, recv_sems, local_sem):
        x, y, c = _place()
        sibling = (x, y, 1 - c)
        chips = [(1 - x, y), (x, 1 - y), (1 - x, 1 - y)]

        def piece(px, py, pc):
            return g_ref.at[2 * px + py, pl.ds(pl.multiple_of(pc * half, 16), half), :]

        def copy(k, src, dst, to):
            return pltpu.make_async_remote_copy(src_ref=src, dst_ref=dst, send_sem=send_sems.at[k],
                                                recv_sem=recv_sems.at[k], device_id=to, device_id_type=MESH)

        mine = pltpu.make_async_copy(p_ref, g_ref.at[2 * x + y], local_sem)
        mine.start()
        my_half = p_ref.at[pl.ds(pl.multiple_of(c * half, 16), half), :]
        first = [copy(j, my_half, piece(x, y, c), (*chip, c)) for j, chip in enumerate(chips)]
        for cp in first:
            cp.start()
        passed = [copy(3 + j, piece(*chip, c), piece(*chip, c), sibling) for j, chip in enumerate(chips)]
        for j, chip in enumerate(chips):
            copy(j, my_half, piece(*chip, c), (*chip, c)).wait_recv()
            passed[j].start()
        for j, chip in enumerate(chips):
            copy(3 + j, piece(*chip, 1 - c), piece(*chip, 1 - c), sibling).wait_recv()
        for cp in first + passed:
            cp.wait_send()
        mine.wait()

    return pl.pallas_call(
        body, name="gather_weights", out_shape=jax.ShapeDtypeStruct((4, rows, 1024), packed.dtype),
        in_specs=[_ANY], out_specs=_ANY,
        scratch_shapes=[pltpu.SemaphoreType.DMA((6,)), pltpu.SemaphoreType.DMA((6,)), pltpu.SemaphoreType.DMA])(packed)


def _reduce_pair(grads):
    rows = grads.shape[1]
    half = rows // 2

    def body(g_ref, own_ref, got_ref, send_sem, recv_sem, local_sem):
        x, y, c = _place()
        sibling = (x, y, 1 - c)
        keep = g_ref.at[:, pl.ds(pl.multiple_of(c * half, 16), half), :]
        give = g_ref.at[:, pl.ds(pl.multiple_of((1 - c) * half, 16), half), :]
        mine = pltpu.make_async_copy(keep, own_ref, local_sem)
        mine.start()
        cp = pltpu.make_async_remote_copy(src_ref=give, dst_ref=got_ref, send_sem=send_sem, recv_sem=recv_sem,
                                          device_id=sibling, device_id_type=MESH)
        cp.start()
        cp.wait()
        mine.wait()

    shape = jax.ShapeDtypeStruct((4, half, 1024), grads.dtype)
    return pl.pallas_call(
        body, name="reduce_pair", out_shape=[shape, shape], in_specs=[_ANY], out_specs=[_ANY, _ANY],
        scratch_shapes=[pltpu.SemaphoreType.DMA, pltpu.SemaphoreType.DMA, pltpu.SemaphoreType.DMA])(grads)


def _add2(a, b):
    n, rows, w = a.shape
    tr = _pick(rows, 512, 16)
    if rows % tr:
        tr = 16 * max(d for d in range(1, 33) if (rows // 16) % d == 0)
    spec = pl.BlockSpec((1, tr, w), lambda i, j: (i, j, 0))

    def body(a_ref, b_ref, o_ref):
        o_ref[...] = (a_ref[...].astype(F32) + b_ref[...].astype(F32)).astype(BF16)

    return pl.pallas_call(
        body, name="add_pair", grid=(n, rows // tr), in_specs=[spec, spec], out_specs=spec,
        out_shape=jax.ShapeDtypeStruct(a.shape, BF16), compiler_params=_params(_PAR, _PAR))(a, b)


def _reduce_chips(partial):
    half = partial.shape[1]

    def body(p_ref, own_ref, got_ref, send_sems, recv_sems, local_sem):
        x, y, c = _place()
        chips = [(1 - x, y), (x, 1 - y), (1 - x, 1 - y)]
        mine = pltpu.make_async_copy(p_ref.at[2 * x + y], own_ref, local_sem)
        mine.start()
        copies = [pltpu.make_async_remote_copy(src_ref=p_ref.at[2 * px + py], dst_ref=got_ref.at[j],
                                               send_sem=send_sems.at[j], recv_sem=recv_sems.at[j],
                                               device_id=(px, py, c), device_id_type=MESH)
                  for j, (px, py) in enumerate(chips)]
        for cp in copies:
            cp.start()
        for cp in copies:
            cp.wait()
        mine.wait()

    return pl.pallas_call(
        body, name="reduce_chips",
        out_shape=[jax.ShapeDtypeStruct((half, 1024), partial.dtype), jax.ShapeDtypeStruct((3, half, 1024), partial.dtype)],
        in_specs=[_ANY], out_specs=[_ANY, _ANY],
        scratch_shapes=[pltpu.SemaphoreType.DMA((3,)), pltpu.SemaphoreType.DMA((3,)), pltpu.SemaphoreType.DMA])(partial)


def _add4(own, got):
    rows, w = own.shape
    tr = _pick(rows, 512, 16)
    if rows % tr:
        tr = 16 * max(d for d in range(1, 33) if (rows // 16) % d == 0)

    def body(a_ref, b_ref, o_ref):
        o_ref[...] = ((a_ref[...].astype(F32) + b_ref[0].astype(F32)) + b_ref[1].astype(F32)) + b_ref[2].astype(F32)

    return pl.pallas_call(
        body, name="add_chips", grid=(rows // tr,),
        in_specs=[pl.BlockSpec((tr, w), lambda i: (i, 0)), pl.BlockSpec((3, tr, w), lambda i: (0, i, 0))],
        out_specs=pl.BlockSpec((tr, w), lambda i: (i, 0)), out_shape=jax.ShapeDtypeStruct((rows, w), F32),
        compiler_params=_params(_PAR))(own, got)


def _share_pair(red):
    half = red.shape[0]

    def body(r_ref, full_ref, send_sem, recv_sem, local_sem):
        x, y, c = _place()
        sibling = (x, y, 1 - c)
        dst = full_ref.at[pl.ds(pl.multiple_of(c * half, 8), half), :]
        mine = pltpu.make_async_copy(r_ref, dst, local_sem)
        mine.start()
        cp = pltpu.make_async_remote_copy(src_ref=r_ref, dst_ref=dst, send_sem=send_sem, recv_sem=recv_sem,
                                          device_id=sibling, device_id_type=MESH)
        cp.start()
        cp.wait_send()
        other = full_ref.at[pl.ds(pl.multiple_of((1 - c) * half, 8), half), :]
        pltpu.make_async_remote_copy(src_ref=r_ref, dst_ref=other, send_sem=send_sem, recv_sem=recv_sem,
                                     device_id=sibling, device_id_type=MESH).wait_recv()
        mine.wait()

    return pl.pallas_call(
        body, name="share_pair", out_shape=jax.ShapeDtypeStruct((2 * half, 1024), red.dtype),
        in_specs=[_ANY], out_specs=_ANY,
        scratch_shapes=[pltpu.SemaphoreType.DMA, pltpu.SemaphoreType.DMA, pltpu.SemaphoreType.DMA])(red)


def _small_sync(gs, ws, ms, vs):
    rows = gs.shape[0]
    vmem = pl.BlockSpec(memory_space=pltpu.VMEM)

    def body(g_ref, w_ref, m_ref, v_ref, sum_ref, d_ref, m2_ref, v2_ref, buf, send_sems, recv_sems):
        x, y, c = _place()
        me = 4 * x + 2 * y + c
        buf[me] = g_ref[...]
        copies = []
        for k in range(1, 8):
            peer = (x ^ (k >> 2), y ^ ((k >> 1) & 1), c ^ (k & 1))
            copies.append(pltpu.make_async_remote_copy(
                src_ref=g_ref, dst_ref=buf.at[me], send_sem=send_sems.at[k - 1], recv_sem=recv_sems.at[k - 1],
                device_id=peer, device_id_type=MESH))
        for cp in copies:
            cp.start()
        for cp in copies:
            cp.wait()
        total = buf[0]
        for i in range(1, 8):
            total = total + buf[i]
        sum_ref[...] = total
        d, m2, v2 = _adam_math(w_ref[...], total, m_ref[...], v_ref[...])
        d_ref[...] = d
        m2_ref[...] = m2
        v2_ref[...] = v2

    shape = jax.ShapeDtypeStruct((rows, 128), F32)
    return pl.pallas_call(
        body, name="small_sync", out_shape=[shape] * 4, in_specs=[vmem] * 4, out_specs=[vmem] * 4,
        scratch_shapes=[pltpu.VMEM((8, rows, 128), F32), pltpu.SemaphoreType.DMA((7,)),
                        pltpu.SemaphoreType.DMA((7,))])(gs, ws, ms, vs)


_BIG = (("ffn1_w_in", 1408, 1408), ("ffn1_w_out", 704, 704), ("w_in", 3080, 3088), ("gdn_conv_w", 4, 16),
        ("w_branch_hgrn", 256, 256), ("w_branch_gdn", 512, 512), ("w_out", 256, 256),
        ("ffn2_w_in", 1408, 1408), ("ffn2_w_out", 704, 704))
_BIG_ROWS = sum(p for _, _, p in _BIG)
_COL_SHARDED = ("ffn1_w_in", "w_in", "gdn_conv_w", "ffn2_w_in")


def _pack_rows(parts, lead):
    out = []
    for name, rows, padded in _BIG:
        p = parts[name]
        if padded != rows:
            p = jnp.concatenate([p, jnp.zeros(lead + (padded - rows, 1024), p.dtype)], axis=len(lead))
        out.append(p)
    return jnp.concatenate(out, axis=len(lead))


def _unpack_rows(packed):
    out, off = {}, 0
    for name, rows, padded in _BIG:
        out[name] = packed[..., off:off + rows, :]
        off += padded
    return out


def _full_from_shards(name, g):
    if name in _COL_SHARDED:
        r = {"gdn_conv_w": CONV_K}.get(name, D_MODEL)
        return jnp.transpose(g.reshape(4, r, -1), (1, 0, 2)).reshape(r, -1)
    return g.reshape(-1, 1024)


def _shards_from_full(name, full):
    if name in _COL_SHARDED:
        r = full.shape[0]
        return jnp.transpose(full.reshape(r, 4, -1), (1, 0, 2)).reshape(4, -1, 1024)
    return full.reshape(4, -1, 1024)


_SMALL = (("ffn1_norm", 8), ("mix_norm", 8), ("hgrn_lb_logits", 16), ("hgrn_out_norm", 1), ("gdn_a_log", 1),
          ("gdn_dt_bias", 1), ("gdn_out_norm", 1), ("ffn2_norm", 8), ("final_norm", 8), ("loss", 1))
_SMALL_ROWS = 56


def _pack_small(parts):
    out = []
    for name, rows in _SMALL:
        p = parts[name].reshape(-1).astype(F32)
        p = jnp.concatenate([p, jnp.zeros((rows * 128 - p.shape[0],), F32)]) if p.shape[0] != rows * 128 else p
        out.append(p.reshape(rows, 128))
    used = sum(r for _, r in _SMALL)
    out.append(jnp.zeros((_SMALL_ROWS - used, 128), F32))
    return jnp.concatenate(out, axis=0)


def _unpack_small(packed, shapes):
    out, off = {}, 0
    for name, rows in _SMALL:
        n = int(np.prod(shapes[name]))
        out[name] = packed[off:off + rows].reshape(-1)[:n].reshape(shapes[name])
        off += rows
    return out


def _ffn_fwd(x, gain, w_in, w_out, tag):
    n = _rmsnorm_fwd(x, gain, tag + "_norm")
    ab = _mm(n, w_in, out_dtype=BF16, name=tag + "_in")
    hm = _swiglu_fwd(ab, tag + "_act")
    out = _mm(hm, w_out, alpha=0.5, res=x, name=tag + "_out")
    return out, (n, ab)


def _ffn_bwd(x, gain, w_in, w_out, saved, dout, tag):
    n, ab = saved
    dhm = _mm(dout, w_out, tb=True, alpha=0.5, name=tag + "_dact")
    dab, hm = _swiglu_bwd(ab, dhm, tag + "_dswiglu")
    dw_out = _mm(hm, dout, ta=True, alpha=0.5, name=tag + "_dwout")
    dw_in = _mm(n, dab, ta=True, name=tag + "_dwin")
    dn = _mm(dab, w_in, tb=True, name=tag + "_dnorm")
    dx, dgain = _rmsnorm_bwd(x, gain, dn, dout, tag + "_dx")
    return dx, dgain, dw_in, dw_out


def _pad_lanes(v):
    return jnp.concatenate([v.reshape(1, -1), jnp.zeros((1, HEAD - v.size), F32)], axis=1)


def _local_step(x, tgt, w, small):
    hg_c = _hg_consts()
    gd_c = _gd_consts()
    seg, off = {}, 0
    for name, size in zip(IN_NAMES, IN_SIZES):
        seg[name] = w["w_in"][:, off:off + size]
        off += size
    w_gab = jnp.concatenate([seg["ga"], seg["gb"], jnp.zeros((D_MODEL, HEAD - 32), BF16)], axis=1)
    big_segs = [n for n in IN_NAMES if n not in ("ga", "gb")]
    conv8 = jnp.concatenate([w["gdn_conv_w"].astype(F32), jnp.zeros((8 - CONV_K, 4096), F32)], axis=0)
    conv_q, conv_k, conv_v = conv8[:, :1024], conv8[:, 1024:2048], conv8[:, 2048:]
    alog = _pad_lanes(small["gdn_a_log"])
    dtb = _pad_lanes(small["gdn_dt_bias"])
    logits = small["hgrn_lb_logits"]
    hg_gain = small["hgrn_out_norm"].reshape(1, HEAD)
    gd_gain = small["gdn_out_norm"].reshape(1, HEAD)
    g1, gm, g2 = small["ffn1_norm"].reshape(1, -1), small["mix_norm"].reshape(1, -1), small["ffn2_norm"].reshape(1, -1)
    gf = small["final_norm"].reshape(1, -1)
    qscale = HEAD ** -0.5

    h1, ffn1_saved = _ffn_fwd(x, g1, w["ffn1_w_in"], w["ffn1_w_out"], "ffn1")
    u = _rmsnorm_fwd(h1, gm, "mix_norm")
    pr = {n: _mm(u, seg[n], name="proj_" + n) for n in big_segs}
    gab = _mm(u, w_gab, name="proj_gab")
    oh_raw, oh, s_h = _hgrn_fwd(pr["hq"], pr["hf"], pr["hi"], pr["hg"], logits, hg_gain, hg_c)
    qn = _conv_fwd(pr["gq"], conv_q, qscale, "conv_q")
    kn = _conv_fwd(pr["gk"], conv_k, 1.0, "conv_k")
    cv = _conv_fwd(pr["gv"], conv_v, None, "conv_v")
    og_raw, og, s_g = _gdn_fwd(qn, kn, cv, gab, pr["gz"], alog, dtb, gd_gain, gd_c)
    yh = _mm(oh, w["w_branch_hgrn"], name="branch_h")
    yg = _mm(og, w["w_branch_gdn"], name="branch_g")
    ym = _merge_fwd(yh, yg, pr["gate_h"], pr["gate_g"])
    h2 = _mm(ym, w["w_out"], res=h1, name="mix_out")
    h3, ffn2_saved = _ffn_fwd(h2, g2, w["ffn2_w_in"], w["ffn2_w_out"], "ffn2")
    loss, dh3, d_gf = _final_loss(h3, gf, tgt)

    dh2, d_g2, d_f2in, d_f2out = _ffn_bwd(h2, g2, w["ffn2_w_in"], w["ffn2_w_out"], ffn2_saved, dh3, "ffn2")
    dym = _mm(dh2, w["w_out"], tb=True, name="d_merge")
    d_wout = _mm(ym, dh2, ta=True, name="d_w_out")
    dyh, dyg, d_gate_h, d_gate_g = _merge_bwd(dym, yh, yg, pr["gate_h"], pr["gate_g"])
    d_wbh = _mm(oh, dyh, ta=True, name="d_w_branch_h")
    d_wbg = _mm(og, dyg, ta=True, name="d_w_branch_g")
    doh = _mm(dyh, w["w_branch_hgrn"], tb=True, name="d_oh")
    dog = _mm(dyg, w["w_branch_gdn"], tb=True, name="d_og")
    d_hq, d_hf, d_hi, d_hg, d_hg_gain, d_lb0 = _hgrn_bwd(pr["hq"], pr["hf"], pr["hi"], pr["hg"], logits, hg_gain,
                                                        oh_raw, s_h, doh, hg_c)
    d_qn, d_kn, d_cv, d_gab, d_gz, gd_small = _gdn_bwd(qn, kn, cv, gab, pr["gz"], alog, dtb, gd_gain, og_raw, s_g,
                                                       dog, gd_c)
    dc_q, dwc_q = _conv_bwd_a(pr["gq"], conv_q, d_qn, qscale, "dconv_q")
    dc_k, dwc_k = _conv_bwd_a(pr["gk"], conv_k, d_kn, 1.0, "dconv_k")
    dc_v, dwc_v = _conv_bwd_a(pr["gv"], conv_v, d_cv, None, "dconv_v")
    d_gq = _conv_bwd_b(dc_q, conv_q, "dconvx_q")
    d_gk = _conv_bwd_b(dc_k, conv_k, "dconvx_k")
    d_gv = _conv_bwd_b(dc_v, conv_v, "dconvx_v")
    dpr = {"hq": d_hq, "hf": d_hf, "hi": d_hi, "hg": d_hg, "gq": d_gq, "gk": d_gk, "gv": d_gv, "gz": d_gz,
           "gate_h": d_gate_h, "gate_g": d_gate_g}
    du = _mm(d_gab, w_gab, tb=True, name="du_gab")
    d_wseg = {}
    for n in big_segs:
        du = _mm(dpr[n], seg[n], tb=True, res=du, name="du_" + n)
        d_wseg[n] = _mm(u, dpr[n], ta=True, name="dw_" + n)
    d_wgab = _mm(u, d_gab, ta=True, name="dw_gab")
    d_wseg["ga"], d_wseg["gb"] = d_wgab[:, :16], d_wgab[:, 16:32]
    d_win = jnp.concatenate([d_wseg[n] for n in IN_NAMES], axis=1)
    dh1, d_gm = _rmsnorm_bwd(h1, gm, du, dh2, "mix_dnorm")
    dx, d_g1, d_f1in, d_f1out = _ffn_bwd(x, g1, w["ffn1_w_in"], w["ffn1_w_out"], ffn1_saved, dh1, "ffn1")

    d_conv = jnp.concatenate([dwc_q[:CONV_K], dwc_k[:CONV_K], dwc_v[:CONV_K]], axis=1)
    big = {"ffn1_w_in": d_f1in, "ffn1_w_out": d_f1out, "w_in": d_win, "gdn_conv_w": d_conv,
           "w_branch_hgrn": d_wbh, "w_branch_gdn": d_wbg, "w_out": d_wout, "ffn2_w_in": d_f2in, "ffn2_w_out": d_f2out}
    d_lb0 = d_lb0.reshape(1, -1)
    sm = {"ffn1_norm": d_g1, "mix_norm": d_gm, "hgrn_lb_logits": jnp.concatenate([d_lb0, -d_lb0], axis=0),
          "hgrn_out_norm": d_hg_gain, "gdn_a_log": gd_small[2, :16], "gdn_dt_bias": gd_small[1, :16],
          "gdn_out_norm": gd_small[0], "ffn2_norm": d_g2, "final_norm": d_gf, "loss": loss[0, :1]}
    return dx, big, sm


_WEIGHTS = ("ffn1_norm", "ffn1_w_in", "ffn1_w_out", "mix_norm", "w_in", "hgrn_lb_logits", "hgrn_out_norm",
            "gdn_conv_w", "gdn_a_log", "gdn_dt_bias", "gdn_out_norm", "w_branch_hgrn", "w_branch_gdn", "w_out",
            "ffn2_norm", "ffn2_w_in", "ffn2_w_out", "final_norm")
_BIG_NAMES = tuple(n for n, _, _ in _BIG)


def kernel(x, ffn1_norm, ffn1_w_in, ffn1_w_out, mix_norm, w_in, hgrn_lb_logits, hgrn_out_norm, gdn_conv_w, gdn_a_log, gdn_dt_bias, gdn_out_norm, w_branch_hgrn, w_branch_gdn, w_out, ffn2_norm, ffn2_w_in, ffn2_w_out, final_norm, loss_target, m_ffn1_norm, m_ffn1_w_in, m_ffn1_w_out, m_mix_norm, m_w_in, m_hgrn_lb_logits, m_hgrn_out_norm, m_gdn_conv_w, m_gdn_a_log, m_gdn_dt_bias, m_gdn_out_norm, m_w_branch_hgrn, m_w_branch_gdn, m_w_out, m_ffn2_norm, m_ffn2_w_in, m_ffn2_w_out, m_final_norm, v_ffn1_norm, v_ffn1_w_in, v_ffn1_w_out, v_mix_norm, v_w_in, v_hgrn_lb_logits, v_hgrn_out_norm, v_gdn_conv_w, v_gdn_a_log, v_gdn_dt_bias, v_gdn_out_norm, v_w_branch_hgrn, v_w_branch_gdn, v_w_out, v_ffn2_norm, v_ffn2_w_in, v_ffn2_w_out, v_final_norm):
    args = dict(locals())
    wts = {n: args[n] for n in _WEIGHTS}
    moms = {n: args["m_" + n] for n in _WEIGHTS}
    vars_ = {n: args["v_" + n] for n in _WEIGHTS}

    shard2d = {n: wts[n].reshape(-1, 1024) for n in _BIG_NAMES}
    packed = _pack_rows({n: shard2d[n].astype(BF16) for n in _BIG_NAMES}, ())
    gathered = _unpack_rows(_gather_weights(packed))
    full = {n: _full_from_shards(n, gathered[n]) for n in _BIG_NAMES}
    small = {n: wts[n].astype(F32) for n in _WEIGHTS if n not in _BIG_NAMES}

    dx, big_grads, small_grads = _local_step(x[0], loss_target[0], full, small)

    gpack = _pack_rows({n: _shards_from_full(n, big_grads[n]).astype(BF16) for n in _BIG_NAMES}, (4,))
    own, got = _reduce_pair(gpack)
    own_chip, got_chips = _reduce_chips(_add2(own, got))
    reduced = _unpack_rows(_share_pair(_add4(own_chip, got_chips)))

    out_g, out_d, out_m, out_v = {}, {}, {}, {}
    for n in _BIG_NAMES:
        shape = wts[n].shape
        w2 = wts[n].reshape(shape[-2], shape[-1])
        g2 = reduced[n].reshape(shape[-2], shape[-1])
        d, m2, v2 = _adamw(w2, g2, moms[n].reshape(w2.shape), vars_[n].reshape(w2.shape), "adamw_" + n)
        out_g[n], out_d[n], out_m[n], out_v[n] = g2.reshape(shape), d.reshape(shape), m2.reshape(shape), v2.reshape(shape)

    small_names = [n for n, _ in _SMALL]
    zero = jnp.zeros((1,), F32)
    shapes = {n: (wts[n].shape if n != "loss" else (1,)) for n in small_names}
    sums, sd, sm_, sv = _small_sync(
        _pack_small(small_grads),
        _pack_small({n: (wts[n] if n != "loss" else zero) for n in small_names}),
        _pack_small({n: (moms[n] if n != "loss" else zero) for n in small_names}),
        _pack_small({n: (vars_[n] if n != "loss" else zero) for n in small_names}))
    sg_u, sd_u, sm_u, sv_u = (_unpack_small(p, shapes) for p in (sums, sd, sm_, sv))
    for n in small_names:
        if n != "loss":
            out_g[n], out_d[n], out_m[n], out_v[n] = sg_u[n], sd_u[n], sm_u[n], sv_u[n]
    loss = sg_u["loss"].reshape(())

    return (loss, dx[None], *[out_g[n] for n in _WEIGHTS], *[out_d[n] for n in _WEIGHTS],
            *[out_m[n] for n in _WEIGHTS], *[out_v[n] for n in _WEIGHTS])
```

```python
import numpy as np

import jax
import jax.numpy as jnp
from jax import lax
from jax.experimental import pallas as pl
from jax.experimental.pallas import tpu as pltpu

F32 = jnp.float32
BF16 = jnp.bfloat16

D_MODEL = 1024
D_FF = 2816
CHUNK = 64
HEAD = 128
HG_HEADS = 8
GD_HEADS = 16
HPS = 4
COMM_CHUNKS = 9
MM_TM = 1408
MM_TN = 512
MM_TK = 1536
VMEM_LIMIT = 48 * 1024 * 1024
EPS = 1e-6
CONV_K = 4
IN_NAMES = ("hq", "hf", "hi", "hg", "gq", "gk", "gv", "ga", "gb", "gz", "gate_h", "gate_g")
IN_SIZES = (1024, 1024, 1024, 1024, 1024, 1024, 2048, 16, 16, 2048, 1024, 1024)
IN_WIDTH = sum(IN_SIZES)

ADAM_LR = 0.001
ADAM_B1 = 0.9
ADAM_B2 = 0.999
ADAM_EPS = 1e-08
ADAM_WD = 0.01
ADAM_STEP = 10

MESH = pl.DeviceIdType.MESH
_ARB = "arbitrary"
_PAR = "parallel"


def _bf(x):
    return x.astype(BF16)


def _dot(a, b):
    return jnp.dot(_bf(a), _bf(b), preferred_element_type=F32)


def _dot_nt(a, b):
    return lax.dot_general(_bf(a), _bf(b), (((1,), (1,)), ((), ())), preferred_element_type=F32)


def _dot_tn(a, b):
    return lax.dot_general(_bf(a), _bf(b), (((0,), (0,)), ((), ())), preferred_element_type=F32)


def _split3(x):
    hi = _bf(x)
    r = x - hi.astype(F32)
    mid = _bf(r)
    lo = _bf(r - mid.astype(F32))
    return hi, mid, lo


def _dot_mx(m, x):
    hi, mid, lo = _split3(x)
    return (jnp.dot(m, hi, preferred_element_type=F32) + jnp.dot(m, mid, preferred_element_type=F32)
            + jnp.dot(m, lo, preferred_element_type=F32))


def _dot_xm(x, m):
    hi, mid, lo = _split3(x)
    return (jnp.dot(hi, m, preferred_element_type=F32) + jnp.dot(mid, m, preferred_element_type=F32)
            + jnp.dot(lo, m, preferred_element_type=F32))


def _dot_hp(a, b):
    ah = _bf(a)
    al = _bf(a - ah.astype(F32))
    bh = _bf(b)
    bl = _bf(b - bh.astype(F32))
    return (jnp.dot(ah, bh, preferred_element_type=F32) + jnp.dot(ah, bl, preferred_element_type=F32)
            + jnp.dot(al, bh, preferred_element_type=F32))


def _sigmoid(x):
    return jax.nn.sigmoid(x)


def _silu(x):
    return x * _sigmoid(x)


def _dsilu(x):
    s = _sigmoid(x)
    return s * (1.0 + x * (1.0 - s))


def _softplus(x):
    return jnp.maximum(x, 0.0) + jnp.log(1.0 + jnp.exp(-jnp.abs(x)))


def _rowsum(x):
    return jnp.sum(x, axis=1, keepdims=True)


def _col_to_row(col, eye):
    return jnp.sum(eye * col, axis=0, keepdims=True)


def _row_to_col(row, eye):
    return jnp.sum(eye * row, axis=1, keepdims=True)


def _pick(dim, pref, unit=128):
    if dim <= pref:
        return dim
    t = pref
    while t >= unit:
        if dim % t == 0:
            return t
        t -= unit
    return dim


def _params(*sem):
    return pltpu.CompilerParams(dimension_semantics=tuple(sem), vmem_limit_bytes=VMEM_LIMIT)


def _mm(a, b, *, ta=False, tb=False, alpha=1.0, res=None, out_dtype=F32, name="mm"):
    m = a.shape[1] if ta else a.shape[0]
    k = a.shape[0] if ta else a.shape[1]
    n = b.shape[0] if tb else b.shape[1]
    assert k == (b.shape[1] if tb else b.shape[0])
    tm, tn, tk = _pick(m, MM_TM), _pick(n, MM_TN), _pick(k, MM_TK)
    nk = k // tk
    a_spec = pl.BlockSpec((tk, tm), lambda i, j, l: (l, i)) if ta else pl.BlockSpec((tm, tk), lambda i, j, l: (i, l))
    b_spec = pl.BlockSpec((tn, tk), lambda i, j, l: (j, l)) if tb else pl.BlockSpec((tk, tn), lambda i, j, l: (l, j))
    o_spec = pl.BlockSpec((tm, tn), lambda i, j, l: (i, j))
    dims = (((0 if ta else 1,), (1 if tb else 0,)), ((), ()))
    has_res = res is not None

    def finish(r, r_ref, o_ref):
        if alpha != 1.0:
            r = r * alpha
        if has_res:
            r = r + r_ref[...]
        o_ref[...] = r.astype(out_dtype)

    def body(*refs):
        a_ref, b_ref = refs[0], refs[1]
        r_ref = refs[2] if has_res else None
        o_ref = refs[3] if has_res else refs[2]
        part = lax.dot_general(_bf(a_ref[...]), _bf(b_ref[...]), dims, preferred_element_type=F32)
        if nk == 1:
            finish(part, r_ref, o_ref)
            return
        acc = refs[-1]
        step = pl.program_id(2)

        @pl.when(step == 0)
        def _():
            acc[...] = part

        @pl.when(step != 0)
        def _():
            acc[...] += part

        @pl.when(step == nk - 1)
        def _():
            finish(acc[...], r_ref, o_ref)

    ins = [a, b] + ([res] if has_res else [])
    in_specs = [a_spec, b_spec] + ([o_spec] if has_res else [])
    return pl.pallas_call(
        body, name=name, grid=(m // tm, n // tn, nk), in_specs=in_specs, out_specs=o_spec,
        out_shape=jax.ShapeDtypeStruct((m, n), out_dtype),
        scratch_shapes=[pltpu.VMEM((tm, tn), F32)] if nk > 1 else [],
        compiler_params=_params(_PAR, _PAR, _ARB))(*ins)


def _row_spec(tr, w):
    return pl.BlockSpec((tr, w), lambda i: (i, 0))


def _full_spec(shape):
    return pl.BlockSpec(shape, lambda i: tuple(0 for _ in shape))


def _rmsnorm_fwd(x, g, name):
    t, d = x.shape
    tr = _pick(t, 256, 8)

    def body(x_ref, g_ref, o_ref):
        xv = x_ref[...]
        r = lax.rsqrt(jnp.mean(xv * xv, axis=1, keepdims=True) + EPS)
        o_ref[...] = (xv * r * g_ref[...]).astype(BF16)

    return pl.pallas_call(
        body, name=name, grid=(t // tr,), in_specs=[_row_spec(tr, d), _full_spec((1, d))],
        out_specs=_row_spec(tr, d), out_shape=jax.ShapeDtypeStruct((t, d), BF16),
        compiler_params=_params(_PAR))(x, g)


def _rmsnorm_bwd(x, g, dn, res, name):
    t, d = x.shape
    tr = _pick(t, 256, 8)

    def body(x_ref, g_ref, dn_ref, r_ref, dx_ref, dxb_ref, dg_ref):
        @pl.when(pl.program_id(0) == 0)
        def _():
            dg_ref[...] = jnp.zeros_like(dg_ref)

        xv = x_ref[...]
        r = lax.rsqrt(jnp.mean(xv * xv, axis=1, keepdims=True) + EPS)
        xh = xv * r
        dy = dn_ref[...]
        dg_ref[...] += jnp.sum(dy * xh, axis=0, keepdims=True)
        dxh = dy * g_ref[...]
        dx = r_ref[...] + r * (dxh - xh * jnp.mean(dxh * xh, axis=1, keepdims=True))
        dx_ref[...] = dx
        dxb_ref[...] = dx.astype(BF16)

    return pl.pallas_call(
        body, name=name, grid=(t // tr,),
        in_specs=[_row_spec(tr, d), _full_spec((1, d)), _row_spec(tr, d), _row_spec(tr, d)],
        out_specs=[_row_spec(tr, d), _row_spec(tr, d), _full_spec((1, d))],
        out_shape=[jax.ShapeDtypeStruct((t, d), F32), jax.ShapeDtypeStruct((t, d), BF16),
                   jax.ShapeDtypeStruct((1, d), F32)],
        compiler_params=_params(_ARB))(x, g, dn, res)


def _swiglu_fwd(ab, name):
    t = ab.shape[0]
    tr = _pick(t, 256, 8)

    def body(ab_ref, o_ref):
        a = ab_ref[:, :D_FF].astype(F32)
        b = ab_ref[:, D_FF:].astype(F32)
        o_ref[...] = (_silu(a) * b).astype(BF16)

    return pl.pallas_call(
        body, name=name, grid=(t // tr,), in_specs=[_row_spec(tr, 2 * D_FF)], out_specs=_row_spec(tr, D_FF),
        out_shape=jax.ShapeDtypeStruct((t, D_FF), BF16), compiler_params=_params(_PAR))(ab)


def _swiglu_bwd(ab, dhm, name):
    t = ab.shape[0]
    tr = _pick(t, 256, 8)

    def body(ab_ref, dh_ref, dab_ref, hm_ref):
        a = ab_ref[:, :D_FF].astype(F32)
        b = ab_ref[:, D_FF:].astype(F32)
        dh = dh_ref[...]
        dab_ref[:, :D_FF] = (dh * b * _dsilu(a)).astype(BF16)
        sa = _silu(a)
        dab_ref[:, D_FF:] = (dh * sa).astype(BF16)
        hm_ref[...] = (sa * b).astype(BF16)

    return pl.pallas_call(
        body, name=name, grid=(t // tr,), in_specs=[_row_spec(tr, 2 * D_FF), _row_spec(tr, D_FF)],
        out_specs=[_row_spec(tr, 2 * D_FF), _row_spec(tr, D_FF)],
        out_shape=[jax.ShapeDtypeStruct((t, 2 * D_FF), BF16), jax.ShapeDtypeStruct((t, D_FF), BF16)],
        compiler_params=_params(_PAR))(ab, dhm)


def _merge_fwd(yh, yg, gh, gg):
    t, d = yh.shape
    tr = _pick(t, 256, 8)

    def body(yh_ref, yg_ref, gh_ref, gg_ref, o_ref):
        o_ref[...] = (_sigmoid(gh_ref[...]) * yh_ref[...] + _sigmoid(gg_ref[...]) * yg_ref[...]).astype(BF16)

    return pl.pallas_call(
        body, name="merge_fwd", grid=(t // tr,), in_specs=[_row_spec(tr, d)] * 4, out_specs=_row_spec(tr, d),
        out_shape=jax.ShapeDtypeStruct((t, d), BF16), compiler_params=_params(_PAR))(yh, yg, gh, gg)


def _merge_bwd(dy, yh, yg, gh, gg):
    t, d = yh.shape
    tr = _pick(t, 256, 8)

    def body(dy_ref, yh_ref, yg_ref, gh_ref, gg_ref, dyh_ref, dyg_ref, dgh_ref, dgg_ref):
        dyv = dy_ref[...]
        sh = _sigmoid(gh_ref[...])
        sg = _sigmoid(gg_ref[...])
        dyh_ref[...] = (dyv * sh).astype(BF16)
        dyg_ref[...] = (dyv * sg).astype(BF16)
        dgh_ref[...] = (dyv * yh_ref[...] * sh * (1.0 - sh)).astype(BF16)
        dgg_ref[...] = (dyv * yg_ref[...] * sg * (1.0 - sg)).astype(BF16)

    return pl.pallas_call(
        body, name="merge_bwd", grid=(t // tr,), in_specs=[_row_spec(tr, d)] * 5, out_specs=[_row_spec(tr, d)] * 4,
        out_shape=[jax.ShapeDtypeStruct((t, d), BF16)] * 4,
        compiler_params=_params(_PAR))(dy, yh, yg, gh, gg)


def _final_loss(h, g, tgt):
    t, d = h.shape
    tr = _pick(t, 256, 8)

    def body(h_ref, g_ref, t_ref, loss_ref, dh_ref, dhb_ref, dg_ref):
        @pl.when(pl.program_id(0) == 0)
        def _():
            dg_ref[...] = jnp.zeros_like(dg_ref)
            loss_ref[...] = jnp.zeros_like(loss_ref)

        xv = h_ref[...]
        gv = g_ref[...]
        r = lax.rsqrt(jnp.mean(xv * xv, axis=1, keepdims=True) + EPS)
        xh = xv * r
        err = xh * gv - t_ref[...]
        loss_ref[...] += 0.5 * jnp.sum(jnp.mean(err * err, axis=1, keepdims=True), axis=0, keepdims=True)
        dy = err * (1.0 / d)
        dg_ref[...] += jnp.sum(dy * xh, axis=0, keepdims=True)
        dxh = dy * gv
        dh = r * (dxh - xh * jnp.mean(dxh * xh, axis=1, keepdims=True))
        dh_ref[...] = dh
        dhb_ref[...] = dh.astype(BF16)

    return pl.pallas_call(
        body, name="final_loss", grid=(t // tr,),
        in_specs=[_row_spec(tr, d), _full_spec((1, d)), _row_spec(tr, d)],
        out_specs=[_full_spec((1, 128)), _row_spec(tr, d), _row_spec(tr, d), _full_spec((1, d))],
        out_shape=[jax.ShapeDtypeStruct((1, 128), F32), jax.ShapeDtypeStruct((t, d), F32),
                   jax.ShapeDtypeStruct((t, d), BF16), jax.ShapeDtypeStruct((1, d), F32)],
        compiler_params=_params(_ARB))(h, g, tgt)


def _hg_consts():
    c = CHUNK
    t = np.arange(c)
    mats, masks = [], []
    for lvl in range(6):
        m = 1 << lvl
        blk = t // m
        mat = np.zeros((c, c), np.float32)
        for tt in range(c):
            b = blk[tt]
            if b % 2 == 1:
                mat[tt, b * m:tt + 1] = 1.0
            else:
                mat[tt, tt + 1:(b + 1) * m] = 1.0
        mats.append(mat)
        same = (t[:, None] // (2 * m)) == (t[None, :] // (2 * m))
        masks.append((same & (blk[:, None] % 2 == 1) & (blk[None, :] % 2 == 0)).astype(np.float32))
    pre = np.tril(np.ones((c, c), np.float32))
    suf = np.triu(np.ones((c, c), np.float32), 1)
    mstack = np.concatenate(mats + [pre, suf], 0)
    masks.append(np.eye(c, dtype=np.float32))
    return (jnp.asarray(mstack, BF16), jnp.asarray(mstack.T.copy(), BF16), jnp.asarray(np.stack(masks), F32),
            jnp.asarray(np.eye(HEAD, dtype=np.float32)))


def _gd_consts():
    c = CHUNK
    incl = np.tril(np.ones((c, c), np.float32))
    strict = np.tril(np.ones((c, c), np.float32), -1)
    eye = np.eye(c, dtype=np.float32)
    masks = np.stack([incl, strict, eye, incl.T.copy()])
    sel = np.zeros((32, HEAD, HEAD), np.float32)
    for j in range(32):
        sel[j, j, :] = 1.0
    return (jnp.asarray(incl, BF16), jnp.asarray(incl.T.copy(), BF16), jnp.asarray(masks, F32), jnp.asarray(sel, BF16))


def _chunks_per_step(nc):
    for cb in (8, 4, 2, 1):
        if nc % cb == 0:
            return cb
    return 1


def _hg_prep(hq, hf, lg):
    lb = _sigmoid(lg[0:1, :] - lg[1:2, :])
    sg = _sigmoid(hf)
    sgn = _sigmoid(-hf)
    f = lb + (1.0 - lb) * sg
    lf = jnp.log(f)
    kk = (1.0 - lb) * sgn
    q = _silu(hq) * (HEAD ** -0.5)
    return lb, sg, sgn, f, lf, kk, q


def _mx_each(m, xs):
    parts = [_split3(x) for x in xs]
    out = None
    for p in range(3):
        d = [jnp.dot(m, s[p], preferred_element_type=F32) for s in parts]
        out = d if out is None else [a + b for a, b in zip(out, d)]
    return out


def _hg_scores(q, kk, ex, mask_ref):
    p = [mask_ref[6] * _rowsum(a * b) for a, b in zip(q, kk)]
    for lvl in range(6):
        el = [e[lvl * CHUNK:(lvl + 1) * CHUNK] for e in ex]
        d = [_dot_nt(a * e, b * e) for a, b, e in zip(q, kk, el)]
        p = [x + mask_ref[lvl] * y for x, y in zip(p, d)]
    return p


def _hgrn_fwd(hq, hf, hi, hg, logits, gain, consts):
    t = hq.shape[0]
    nc = t // CHUNK
    cb = _chunks_per_step(nc)
    rows = cb * CHUNK
    mstack, _, masks, eye = consts
    tile = pl.BlockSpec((rows, HPS * HEAD), lambda c, g: (c, g))

    def body(hq_ref, hf_ref, hi_ref, hg_ref, lg_ref, gain_ref, m_ref, mask_ref, eye_ref,
             oraw_ref, og_ref, ssave_ref, state):
        c = pl.program_id(0)
        g = pl.program_id(1)

        @pl.when(c == 0)
        def _():
            for hh in range(HPS):
                state[g * HPS + hh] = jnp.zeros((HEAD, HEAD), F32)

        lg_all = lg_ref[...]
        gain_v = gain_ref[...]

        def one(i, carry):
            sl = pl.ds(pl.multiple_of(i * CHUNK, CHUNK), CHUNK)
            hs = range(HPS)
            heads = [g * HPS + hh for hh in hs]
            ln = [slice(hh * HEAD, (hh + 1) * HEAD) for hh in hs]
            preps = [_hg_prep(hq_ref[sl, s], hf_ref[sl, s], lg_all[:, s]) for s in ln]
            lf, kk, q = [p[4] for p in preps], [p[5] for p in preps], [p[6] for p in preps]
            v = [hi_ref[sl, s] for s in ln]
            ex = [jnp.exp(x) for x in _mx_each(m_ref[...], lf)]
            eb = [e[6 * CHUNK:7 * CHUNK] for e in ex]
            esfx = [e[7 * CHUNK:8 * CHUNK] for e in ex]
            p = _hg_scores(q, kk, ex, mask_ref)
            s0 = [state[h] for h in heads]
            o = _each(lambda a, e, s, pp, vv: _dot(a * e, s) + _dot(pp, vv), q, eb, s0, p, v)
            eye_v = eye_ref[...]
            s1 = _each(lambda s, e, kx, ef, vv: s * _row_to_col(e[CHUNK - 1:CHUNK, :], eye_v) + _dot_tn(kx * ef, vv),
                       s0, eb, kk, esfx, v)
            for hh in hs:
                ssave_ref[i, hh] = s0[hh]
                state[heads[hh]] = s1[hh]
                oraw_ref[sl, ln[hh]] = o[hh]
                r = lax.rsqrt(jnp.mean(o[hh] * o[hh], axis=1, keepdims=True) + EPS)
                og_ref[sl, ln[hh]] = (o[hh] * r * gain_v * _silu(hg_ref[sl, ln[hh]])).astype(BF16)
            return carry

        lax.fori_loop(0, cb, one, 0)

    return pl.pallas_call(
        body, name="hgrn_fwd", grid=(nc // cb, HG_HEADS // HPS),
        in_specs=[tile, tile, tile, tile, pl.BlockSpec((2, HPS * HEAD), lambda c, g: (0, g)),
                  pl.BlockSpec((1, HEAD), lambda c, g: (0, 0)),
                  pl.BlockSpec(mstack.shape, lambda c, g: (0, 0)),
                  pl.BlockSpec(masks.shape, lambda c, g: (0, 0, 0)),
                  pl.BlockSpec(eye.shape, lambda c, g: (0, 0))],
        out_specs=[tile, tile, pl.BlockSpec((cb, HPS, HEAD, HEAD), lambda c, g: (c, g, 0, 0))],
        out_shape=[jax.ShapeDtypeStruct((t, HG_HEADS * HEAD), F32), jax.ShapeDtypeStruct((t, HG_HEADS * HEAD), BF16),
                   jax.ShapeDtypeStruct((nc, HG_HEADS, HEAD, HEAD), F32)],
        scratch_shapes=[pltpu.VMEM((HG_HEADS, HEAD, HEAD), F32)],
        compiler_params=_params(_ARB, _ARB))(hq, hf, hi, hg, logits, gain, mstack, masks, eye)


def _hgrn_bwd(hq, hf, hi, hg, logits, gain, oraw, ssave, dog, consts):
    t = hq.shape[0]
    nc = t // CHUNK
    cb = _chunks_per_step(nc)
    rows = cb * CHUNK
    nb = nc // cb
    mstack, mstack_t, masks, eye = consts
    tile = pl.BlockSpec((rows, HPS * HEAD), lambda c, g: (nb - 1 - c, g))

    def body(hq_ref, hf_ref, hi_ref, hg_ref, lg_ref, gain_ref, oraw_ref, ssave_ref, dog_ref, m_ref, mt_ref,
             mask_ref, eye_ref, dhq_ref, dhf_ref, dhi_ref, dhg_ref, dgain_ref, dlb_ref, dstate):
        c = pl.program_id(0)
        g = pl.program_id(1)

        @pl.when(c == 0)
        def _():
            for hh in range(HPS):
                dstate[g * HPS + hh] = jnp.zeros((HEAD, HEAD), F32)

        @pl.when((c == 0) & (g == 0))
        def _():
            dgain_ref[...] = jnp.zeros_like(dgain_ref)
            dlb_ref[...] = jnp.zeros_like(dlb_ref)

        lg_all = lg_ref[...]
        gain_v = gain_ref[...]
        eye_v = eye_ref[...]
        last_row = (lax.broadcasted_iota(jnp.int32, (CHUNK, HEAD), 0) == CHUNK - 1).astype(F32)

        def one(j, carry):
            i = cb - 1 - j
            sl = pl.ds(pl.multiple_of(i * CHUNK, CHUNK), CHUNK)
            hs = range(HPS)
            heads = [g * HPS + hh for hh in hs]
            ln = [slice(hh * HEAD, (hh + 1) * HEAD) for hh in hs]
            hqv = [hq_ref[sl, s] for s in ln]
            hgv = [hg_ref[sl, s] for s in ln]
            preps = [_hg_prep(a, hf_ref[sl, s], lg_all[:, s]) for a, s in zip(hqv, ln)]
            lb, sg, sgn, f, lf, kk, q = ([p[n] for p in preps] for n in range(7))
            v = [hi_ref[sl, s] for s in ln]
            ex = [jnp.exp(x) for x in _mx_each(m_ref[...], lf)]
            eb = [e[6 * CHUNK:7 * CHUNK] for e in ex]
            esfx = [e[7 * CHUNK:8 * CHUNK] for e in ex]
            p = _hg_scores(q, kk, ex, mask_ref)
            s0 = [ssave_ref[i, hh] for hh in hs]
            ds = [dstate[h] for h in heads]

            o = [oraw_ref[sl, s] for s in ln]
            r = [lax.rsqrt(jnp.mean(x * x, axis=1, keepdims=True) + EPS) for x in o]
            on = _each(lambda x, y: x * y, o, r)
            dg_out = [dog_ref[sl, s] for s in ln]
            sgate = [_silu(x) for x in hgv]
            for hh in hs:
                dhg_ref[sl, ln[hh]] = (dg_out[hh] * on[hh] * gain_v * _dsilu(hgv[hh])).astype(BF16)
            dgain_ref[...] += sum(jnp.sum(d * s * n, axis=0, keepdims=True) for d, s, n in zip(dg_out, sgate, on))
            don = _each(lambda d, s: d * s * gain_v, dg_out, sgate)
            do = _each(lambda rr, dn, n: rr * (dn - n * jnp.mean(dn * n, axis=1, keepdims=True)), r, don, on)

            dp = _each(_dot_nt, do, v)
            dv = _each(lambda pp, d, kx, ef, s: _dot_tn(pp, d) + _dot(kx * ef, s), p, do, kk, esfx, ds)
            dqb = _each(_dot_nt, do, s0)
            dkx = _each(_dot_nt, v, ds)
            diag = [_rowsum(mask_ref[6] * x) for x in dp]
            dq = _each(lambda a, e, d, kx: a * e + d * kx, dqb, eb, diag, kk)
            dk = _each(lambda a, e, d, qq: a * e + d * qq, dkx, esfx, diag, q)
            dxs = [[] for _ in hs]
            for lvl in range(6):
                el = [e[lvl * CHUNK:(lvl + 1) * CHUNK] for e in ex]
                gm = [mask_ref[lvl] * x for x in dp]
                a1 = _each(lambda m_, kx, e: _dot(m_, kx * e), gm, kk, el)
                a2 = _each(lambda m_, qq, e: _dot_tn(m_, qq * e), gm, q, el)
                dq = _each(lambda x, a, e: x + a * e, dq, a1, el)
                dk = _each(lambda x, a, e: x + a * e, dk, a2, el)
                for hh in hs:
                    dxs[hh].append((a1[hh] * q[hh] + a2[hh] * kk[hh]) * el[hh])
            e_end_row = [e[CHUNK - 1:CHUNK, :] for e in eb]
            ds_new = _each(lambda qq, e, d, er, s: _dot_tn(qq * e, d) + _row_to_col(er, eye_v) * s, q, eb, do, e_end_row, ds)
            for hh in hs:
                dstate[heads[hh]] = ds_new[hh]
                dend_row = _col_to_row(_rowsum(s0[hh] * ds[hh]), eye_v)
                dxs[hh].append(dqb[hh] * q[hh] * eb[hh] + last_row * (e_end_row[hh] * dend_row))
                dxs[hh].append(dkx[hh] * kk[hh] * esfx[hh])
            dlf = _mx_each(mt_ref[...], [jnp.concatenate(x, axis=0) for x in dxs])

            for hh in hs:
                dhi_ref[sl, ln[hh]] = dv[hh].astype(BF16)
                dhq_ref[sl, ln[hh]] = (dq[hh] * (HEAD ** -0.5) * _dsilu(hqv[hh])).astype(BF16)
                df = dlf[hh] / f[hh]
                dsig = (1.0 - lb[hh]) * sg[hh] * sgn[hh]
                dhf_ref[sl, ln[hh]] = ((df - dk[hh]) * dsig).astype(BF16)
                dlb_t = jnp.sum(df * sgn[hh] - dk[hh] * sgn[hh], axis=0, keepdims=True)
                dlb_ref[pl.ds(heads[hh], 1), :] += dlb_t * lb[hh] * (1.0 - lb[hh])
            return carry

        lax.fori_loop(0, cb, one, 0)

    outs = [jax.ShapeDtypeStruct((t, HG_HEADS * HEAD), BF16)] * 4 + [
        jax.ShapeDtypeStruct((1, HEAD), F32), jax.ShapeDtypeStruct((HG_HEADS, HEAD), F32)]
    return pl.pallas_call(
        body, name="hgrn_bwd", grid=(nb, HG_HEADS // HPS),
        in_specs=[tile, tile, tile, tile, pl.BlockSpec((2, HPS * HEAD), lambda c, g: (0, g)),
                  pl.BlockSpec((1, HEAD), lambda c, g: (0, 0)), tile,
                  pl.BlockSpec((cb, HPS, HEAD, HEAD), lambda c, g: (nb - 1 - c, g, 0, 0)), tile,
                  pl.BlockSpec(mstack.shape, lambda c, h: (0, 0)),
                  pl.BlockSpec(mstack_t.shape, lambda c, h: (0, 0)),
                  pl.BlockSpec(masks.shape, lambda c, h: (0, 0, 0)),
                  pl.BlockSpec(eye.shape, lambda c, h: (0, 0))],
        out_specs=[tile, tile, tile, tile, pl.BlockSpec((1, HEAD), lambda c, h: (0, 0)),
                   pl.BlockSpec((HG_HEADS, HEAD), lambda c, h: (0, 0))],
        out_shape=outs, scratch_shapes=[pltpu.VMEM((HG_HEADS, HEAD, HEAD), F32)],
        compiler_params=_params(_ARB, _ARB))(hq, hf, hi, hg, logits, gain, oraw, ssave, dog, mstack, mstack_t,
                                             masks, eye)


def _shift_down(xv, halo, d, top_rows):
    if d == 0:
        return xv, xv[0:8]
    main = pltpu.roll(xv, d, 0)
    top = jnp.where(top_rows < d, pltpu.roll(halo, d, 0), main[0:8])
    return main, top


def _conv_parts(x_ref, halo_ref, w_ref, first):
    xv = x_ref[...]
    halo = jnp.where(first, 0.0, halo_ref[...])
    top_rows = lax.broadcasted_iota(jnp.int32, (8, xv.shape[1]), 0)
    shifted = [_shift_down(xv, halo, CONV_K - 1 - j, top_rows) for j in range(CONV_K)]
    w = w_ref[...]
    acc = sum(shifted[j][0] * w[j:j + 1, :] for j in range(CONV_K))
    acc_top = sum(shifted[j][1] * w[j:j + 1, :] for j in range(CONV_K))
    return shifted, acc, acc_top


def _conv_fwd(x, w8, l2scale, name):
    t, width = x.shape
    tr = _pick(t, 512, 8)

    def post(cv):
        s = _silu(cv)
        if l2scale is not None:
            s = s * (lax.rsqrt(_rowsum(s * s) + EPS) * l2scale)
        return s

    def body(x_ref, halo_ref, w_ref, o_ref):
        _, acc, acc_top = _conv_parts(x_ref, halo_ref, w_ref, pl.program_id(1) == 0)
        o_ref[...] = post(acc)
        o_ref[0:8, :] = post(acc_top)

    return pl.pallas_call(
        body, name=name, grid=(width // HEAD, t // tr),
        in_specs=[pl.BlockSpec((tr, HEAD), lambda j, i: (i, j)),
                  pl.BlockSpec((8, HEAD), lambda j, i: (jnp.maximum(i * (tr // 8) - 1, 0), j)),
                  pl.BlockSpec((8, HEAD), lambda j, i: (0, j))],
        out_specs=pl.BlockSpec((tr, HEAD), lambda j, i: (i, j)),
        out_shape=jax.ShapeDtypeStruct((t, width), F32), compiler_params=_params(_PAR, _PAR))(x, x, w8)


def _conv_bwd_a(x, w8, dy, l2scale, name):
    t, width = x.shape
    tr = _pick(t, 512, 8)

    def to_dc(cv, dyv):
        if l2scale is not None:
            s = _silu(cv)
            r = lax.rsqrt(_rowsum(s * s) + EPS)
            y0 = s * r
            dy0 = dyv * l2scale
            dyv = r * (dy0 - y0 * _rowsum(dy0 * y0))
        return dyv * _dsilu(cv)

    def body(x_ref, halo_ref, w_ref, dy_ref, dc_ref, dw_ref):
        @pl.when(pl.program_id(1) == 0)
        def _():
            dw_ref[...] = jnp.zeros_like(dw_ref)

        shifted, acc, acc_top = _conv_parts(x_ref, halo_ref, w_ref, pl.program_id(1) == 0)
        dyv = dy_ref[...]
        dc = to_dc(acc, dyv)
        dc_top = to_dc(acc_top, dyv[0:8])
        dc_ref[...] = dc
        dc_ref[0:8, :] = dc_top
        rest = (lax.broadcasted_iota(jnp.int32, dc.shape, 0) >= 8).astype(F32)
        dc_rest = dc * rest
        for j in range(CONV_K):
            dw_ref[j:j + 1, :] += (jnp.sum(dc_rest * shifted[j][0], axis=0, keepdims=True)
                                   + jnp.sum(dc_top * shifted[j][1], axis=0, keepdims=True))

    return pl.pallas_call(
        body, name=name, grid=(width // HEAD, t // tr),
        in_specs=[pl.BlockSpec((tr, HEAD), lambda j, i: (i, j)),
                  pl.BlockSpec((8, HEAD), lambda j, i: (jnp.maximum(i * (tr // 8) - 1, 0), j)),
                  pl.BlockSpec((8, HEAD), lambda j, i: (0, j)),
                  pl.BlockSpec((tr, HEAD), lambda j, i: (i, j))],
        out_specs=[pl.BlockSpec((tr, HEAD), lambda j, i: (i, j)), pl.BlockSpec((8, HEAD), lambda j, i: (0, j))],
        out_shape=[jax.ShapeDtypeStruct((t, width), F32), jax.ShapeDtypeStruct((8, width), F32)],
        compiler_params=_params(_PAR, _ARB))(x, x, w8, dy)


def _conv_bwd_b(dc, w8, name):
    t, width = dc.shape
    tr = _pick(t, 512, 8)
    nt = t // tr

    def body(dc_ref, halo_ref, w_ref, dx_ref):
        dcv = dc_ref[...]
        halo = jnp.where(pl.program_id(1) == nt - 1, 0.0, halo_ref[...])
        w = w_ref[...]
        bot_rows = lax.broadcasted_iota(jnp.int32, (8, HEAD), 0)
        acc = dcv * w[CONV_K - 1:CONV_K, :]
        acc_bot = dcv[tr - 8:tr] * w[CONV_K - 1:CONV_K, :]
        for d in range(1, CONV_K):
            main = pltpu.roll(dcv, tr - d, 0)
            bot = jnp.where(bot_rows >= 8 - d, pltpu.roll(halo, 8 - d, 0), main[tr - 8:tr])
            wj = w[CONV_K - 1 - d:CONV_K - d, :]
            acc = acc + main * wj
            acc_bot = acc_bot + bot * wj
        dx_ref[...] = acc.astype(BF16)
        dx_ref[tr - 16:tr, :] = jnp.concatenate([acc[tr - 16:tr - 8], acc_bot], axis=0).astype(BF16)

    return pl.pallas_call(
        body, name=name, grid=(width // HEAD, nt),
        in_specs=[pl.BlockSpec((tr, HEAD), lambda j, i: (i, j)),
                  pl.BlockSpec((8, HEAD), lambda j, i: (jnp.minimum((i + 1) * (tr // 8), t // 8 - 1), j)),
                  pl.BlockSpec((8, HEAD), lambda j, i: (0, j))],
        out_specs=pl.BlockSpec((tr, HEAD), lambda j, i: (i, j)),
        out_shape=jax.ShapeDtypeStruct((t, width), BF16), compiler_params=_params(_PAR, _PAR))(dc, dc, w8)


def _each(f, *lists):
    return [f(*xs) for xs in zip(*lists)]


def _hp_each(a_list, b_list):
    ah = [_bf(a) for a in a_list]
    bh = [_bf(b) for b in b_list]
    al = [_bf(a - h.astype(F32)) for a, h in zip(a_list, ah)]
    bl = [_bf(b - h.astype(F32)) for b, h in zip(b_list, bh)]
    d1 = [jnp.dot(x, y, preferred_element_type=F32) for x, y in zip(ah, bh)]
    d2 = [jnp.dot(x, y, preferred_element_type=F32) for x, y in zip(ah, bl)]
    d3 = [jnp.dot(x, y, preferred_element_type=F32) for x, y in zip(al, bh)]
    return [x + y + z for x, y, z in zip(d1, d2, d3)]


def _tri_inv_each(a_list, eye):
    ns = [-a for a in a_list]
    ps = [eye + n for n in ns]
    for _ in range(5):
        ns = _hp_each(ns, ns)
        ps = [p + d for p, d in zip(ps, _hp_each(ps, ns))]
    return ps


def _gd_gates(gab, alog, dtb):
    sp_arg = gab + dtb
    return sp_arg, -jnp.exp(alog) * _softplus(sp_arg), _sigmoid(gab)


def _gd_chunks(q, k, v, g_all, beta_all, selg, selb, l_ref, mask_ref):
    incl, strict, eye, upper = mask_ref[0], mask_ref[1], mask_ref[2], mask_ref[3]
    lmat = l_ref[...]
    gb = [_dot_xm(g_all, s) for s in selg]
    bb = [_dot_xm(beta_all, s) for s in selb]
    gam = _mx_each(lmat, gb)
    gam_row = [jnp.sum(x[:, :CHUNK] * upper, axis=0, keepdims=True) for x in gb]
    lm = _each(lambda gm, gr: incl * jnp.exp(jnp.minimum(gm[:, :CHUNK] - gr, 0.0)), gam, gam_row)
    kb = _each(lambda x, b: x * b, k, bb)
    a = _each(lambda x, y, m: strict * _dot_nt(x, y) * m, kb, k, lm)
    tm = _tri_inv_each(a, eye)
    eg = [jnp.exp(x) for x in gam]
    vb = _each(lambda x, b: x * b, v, bb)
    kbg = _each(lambda x, e: x * e, kb, eg)
    u = _each(_dot, tm, vb)
    w = _each(_dot, tm, kbg)
    qk = _each(lambda x, y, m: _dot_nt(x, y) * m, q, k, lm)
    g_end = [x[CHUNK - 1:CHUNK, :] for x in gam]
    ekg = _each(lambda e, x: jnp.exp(e - x), g_end, gam)
    ge = [jnp.exp(e) for e in g_end]
    kg = _each(lambda x, e: x * e, k, ekg)
    qg = _each(lambda x, e: x * e, q, eg)
    names = ("bb", "lm", "kb", "a", "tm", "eg", "vb", "kbg", "u", "w", "qk", "ekg", "ge", "kg", "qg")
    cols = (bb, lm, kb, a, tm, eg, vb, kbg, u, w, qk, ekg, ge, kg, qg)
    return [dict(zip(names, vals)) for vals in zip(*cols)]


def _gd_specs(rows, rev_nb=None):
    def cidx(c):
        return c if rev_nb is None else rev_nb - 1 - c

    qk_tile = pl.BlockSpec((rows, HPS // 2 * HEAD), lambda c, g: (cidx(c), g))
    v_tile = pl.BlockSpec((rows, HPS * HEAD), lambda c, g: (cidx(c), g))
    gab_tile = pl.BlockSpec((rows, HEAD), lambda c, g: (cidx(c), 0))
    return qk_tile, v_tile, gab_tile


def _gdn_fwd(qn, kn, cv, gab, gz, alog, dtb, gain, consts):
    t = qn.shape[0]
    nc = t // CHUNK
    cb = _chunks_per_step(nc)
    rows = cb * CHUNK
    lmat, _, masks, sel = consts
    qk_tile, v_tile, gab_tile = _gd_specs(rows)
    row128 = pl.BlockSpec((1, HEAD), lambda c, h: (0, 0))

    def body(q_ref, k_ref, v_ref, gab_ref, gz_ref, alog_ref, dtb_ref, gain_ref, sel_ref, l_ref, mask_ref,
             oraw_ref, og_ref, ssave_ref, state):
        c = pl.program_id(0)
        g = pl.program_id(1)

        @pl.when(c == 0)
        def _():
            for hh in range(HPS):
                state[g * HPS + hh] = jnp.zeros((HEAD, HEAD), F32)

        alog = alog_ref[...]
        dtb = dtb_ref[...]
        gain_v = gain_ref[...]

        def one(i, carry):
            sl = pl.ds(pl.multiple_of(i * CHUNK, CHUNK), CHUNK)
            _, g_all, beta_all = _gd_gates(gab_ref[sl, :], alog, dtb)
            heads = [g * HPS + hh for hh in range(HPS)]
            lq = [slice(hh // 2 * HEAD, (hh // 2 + 1) * HEAD) for hh in range(HPS)]
            lv = [slice(hh * HEAD, (hh + 1) * HEAD) for hh in range(HPS)]
            chs = _gd_chunks([q_ref[sl, s] for s in lq], [k_ref[sl, s] for s in lq], [v_ref[sl, s] for s in lv],
                             g_all, beta_all, [sel_ref[h] for h in heads], [sel_ref[GD_HEADS + h] for h in heads],
                             l_ref, mask_ref)
            s0 = [state[h] for h in heads]
            v_new = _each(lambda ch, s: ch["u"] - _dot(ch["w"], s), chs, s0)
            o = _each(lambda ch, s, vn: _dot(ch["qg"], s) + _dot(ch["qk"], vn), chs, s0, v_new)
            s1 = _each(lambda ch, s, vn: s * ch["ge"] + _dot_tn(ch["kg"], vn), chs, s0, v_new)
            for hh in range(HPS):
                ssave_ref[i, hh] = s0[hh]
                state[heads[hh]] = s1[hh]
                oraw_ref[sl, lv[hh]] = o[hh]
                r = lax.rsqrt(jnp.mean(o[hh] * o[hh], axis=1, keepdims=True) + EPS)
                og_ref[sl, lv[hh]] = (o[hh] * r * gain_v * _silu(gz_ref[sl, lv[hh]])).astype(BF16)
            return carry

        lax.fori_loop(0, cb, one, 0)

    return pl.pallas_call(
        body, name="gdn_fwd", grid=(nc // cb, GD_HEADS // HPS),
        in_specs=[qk_tile, qk_tile, v_tile, gab_tile, v_tile, row128, row128, row128,
                  pl.BlockSpec(sel.shape, lambda c, g: (0, 0, 0)),
                  pl.BlockSpec(lmat.shape, lambda c, g: (0, 0)),
                  pl.BlockSpec(masks.shape, lambda c, g: (0, 0, 0))],
        out_specs=[v_tile, v_tile, pl.BlockSpec((cb, HPS, HEAD, HEAD), lambda c, g: (c, g, 0, 0))],
        out_shape=[jax.ShapeDtypeStruct((t, GD_HEADS * HEAD), F32), jax.ShapeDtypeStruct((t, GD_HEADS * HEAD), BF16),
                   jax.ShapeDtypeStruct((nc, GD_HEADS, HEAD, HEAD), F32)],
        scratch_shapes=[pltpu.VMEM((GD_HEADS, HEAD, HEAD), F32)],
        compiler_params=_params(_ARB, _ARB))(qn, kn, cv, gab, gz, alog, dtb, gain, sel, lmat, masks)


def _gdn_bwd(qn, kn, cv, gab, gz, alog, dtb, gain, oraw, ssave, dog, consts):
    t = qn.shape[0]
    nc = t // CHUNK
    cb = _chunks_per_step(nc)
    rows = cb * CHUNK
    nb = nc // cb
    lmat, lmat_t, masks, sel = consts
    qk_tile, v_tile, gab_tile = _gd_specs(rows, nb)
    row128 = pl.BlockSpec((1, HEAD), lambda c, h: (0, 0))

    def body(q_ref, k_ref, v_ref, gab_ref, gz_ref, alog_ref, dtb_ref, gain_ref, oraw_ref, ssave_ref, dog_ref,
             sel_ref, l_ref, lt_ref, mask_ref,
             dq_ref, dk_ref, dv_ref, dgab_ref, dgz_ref, small_ref, dstate):
        c = pl.program_id(0)
        g = pl.program_id(1)

        @pl.when(c == 0)
        def _():
            for hh in range(HPS):
                dstate[g * HPS + hh] = jnp.zeros((HEAD, HEAD), F32)

        @pl.when((c == 0) & (g == 0))
        def _():
            small_ref[...] = jnp.zeros_like(small_ref)

        alog = alog_ref[...]
        dtb = dtb_ref[...]
        gain_v = gain_ref[...]
        lane = lax.broadcasted_iota(jnp.int32, (1, HEAD), 1)
        last_row = (lax.broadcasted_iota(jnp.int32, (CHUNK, HEAD), 0) == CHUNK - 1).astype(F32)

        def one(j, carry):
            i = cb - 1 - j
            sl = pl.ds(pl.multiple_of(i * CHUNK, CHUNK), CHUNK)
            sp_arg, g_all, beta_all = _gd_gates(gab_ref[sl, :], alog, dtb)
            strict, eye = mask_ref[1], mask_ref[2]
            ltm = lt_ref[...]
            hs = range(HPS)
            heads = [g * HPS + hh for hh in hs]
            lq = [slice(hh // 2 * HEAD, (hh // 2 + 1) * HEAD) for hh in hs]
            lv = [slice(hh * HEAD, (hh + 1) * HEAD) for hh in hs]
            q = [q_ref[sl, s] for s in lq]
            k = [k_ref[sl, s] for s in lq]
            v = [v_ref[sl, s] for s in lv]
            gzv = [gz_ref[sl, s] for s in lv]
            chs = _gd_chunks(q, k, v, g_all, beta_all, [sel_ref[h] for h in heads],
                             [sel_ref[GD_HEADS + h] for h in heads], l_ref, mask_ref)

            def col(name):
                return [ch[name] for ch in chs]

            def mul(x, y):
                return x * y

            tm, lm, eg, bb = col("tm"), col("lm"), col("eg"), col("bb")
            s0 = [ssave_ref[i, hh] for hh in hs]
            ds = [dstate[h] for h in heads]
            v_new = _each(lambda u, w, s: u - _dot(w, s), col("u"), col("w"), s0)

            o = [oraw_ref[sl, s] for s in lv]
            r = [lax.rsqrt(jnp.mean(x * x, axis=1, keepdims=True) + EPS) for x in o]
            on = _each(mul, o, r)
            dg_out = [dog_ref[sl, s] for s in lv]
            sgate = [_silu(x) for x in gzv]
            for hh in hs:
                dgz_ref[sl, lv[hh]] = (dg_out[hh] * on[hh] * gain_v * _dsilu(gzv[hh])).astype(BF16)
            small_ref[0:1, :] += sum(jnp.sum(d * s * n, axis=0, keepdims=True) for d, s, n in zip(dg_out, sgate, on))
            don = _each(lambda d, s: d * s * gain_v, dg_out, sgate)
            do = _each(lambda rr, dn, n: rr * (dn - n * jnp.mean(dn * n, axis=1, keepdims=True)), r, don, on)

            dv_new = _each(lambda a, d, b, s: _dot_tn(a, d) + _dot(b, s), col("qk"), do, col("kg"), ds)
            dqk = _each(_dot_nt, do, v_new)
            dqg = _each(_dot_nt, do, s0)
            dkg = _each(_dot_nt, v_new, ds)
            dge = _each(lambda s, d: jnp.sum(_rowsum(s * d), axis=0, keepdims=True), s0, ds)
            dw = _each(lambda d, s: -_dot_nt(d, s), dv_new, s0)
            ds_new = _each(lambda qg, d, ge, s, w, dv: _dot_tn(qg, d) + ge * s - _dot_tn(w, dv),
                           col("qg"), do, col("ge"), ds, col("w"), dv_new)
            for hh in hs:
                dstate[heads[hh]] = ds_new[hh]

            dvb = _each(_dot_tn, tm, dv_new)
            dkbg = _each(_dot_tn, tm, dw)
            dtm = _each(lambda dv, vb, d, kbg: _dot_nt(dv, vb) + _dot_nt(d, kbg), dv_new, col("vb"), dw, col("kbg"))
            dtt = _each(_dot_nt, dtm, tm)
            da = _each(lambda t_, x: -_dot_tn(t_, x) * strict, tm, dtt)
            dal = _each(mul, da, lm)
            dkb = _each(lambda x, kk, y, e: _dot(x, kk) + y * e, dal, k, dkbg, eg)
            dqk_l = _each(mul, dqk, lm)
            dq = _each(lambda x, kk, y, e: _dot(x, kk) + y * e, dqk_l, k, dqg, eg)
            dk = _each(lambda x, kb, y, qq, z, ekg, w_, b: _dot_tn(x, kb) + _dot_tn(y, qq) + z * ekg + w_ * b,
                       dal, col("kb"), dqk_l, q, dkg, col("ekg"), dkb, bb)
            gmat = _each(lambda x, a, y, qk: x * a + y * qk, da, col("a"), dqk, col("qk"))
            t_kg = _each(lambda x, y: _rowsum(x * y), dkg, col("kg"))
            dgam = _each(lambda gm, x, qg, t_, y, kbg: (_rowsum(gm) - _row_to_col(jnp.sum(gm, axis=0, keepdims=True), eye)
                                                        + _rowsum(x * qg) - t_ + _rowsum(y * kbg)),
                         gmat, dqg, col("qg"), t_kg, dkbg, col("kbg"))
            dg_end = _each(lambda t_, e, ge: jnp.sum(t_, axis=0, keepdims=True) + e * ge[:, 0:1], t_kg, dge, col("ge"))
            dgam = _each(lambda x, e: x + last_row * e, dgam, dg_end)
            dbeta = _each(lambda x, kk, y, vv: _rowsum(x * kk) + _rowsum(y * vv), dkb, k, dvb, v)
            dg = _mx_each(ltm, dgam)

            for hh in hs:
                dv_ref[sl, lv[hh]] = dvb[hh] * bb[hh]
            fac_g = -jnp.exp(alog) * _sigmoid(sp_arg)
            fac_b = beta_all * (1.0 - beta_all)
            hot_g = [(lane == h).astype(F32) for h in heads]
            hot_b = [(lane == GD_HEADS + h).astype(F32) for h in heads]
            dga = _each(lambda x, hot: x * hot * fac_g, dg, hot_g)
            dgb = _each(lambda x, hot: x * hot * fac_b, dbeta, hot_b)
            small_ref[1:2, :] += sum(jnp.sum(x, axis=0, keepdims=True) for x in dga)
            small_ref[2:3, :] += sum(jnp.sum(x * hot * g_all, axis=0, keepdims=True) for x, hot in zip(dg, hot_g))
            for pair in range(HPS // 2):
                lqp = slice(pair * HEAD, (pair + 1) * HEAD)
                dq_ref[sl, lqp] = dq[2 * pair] + dq[2 * pair + 1]
                dk_ref[sl, lqp] = dk[2 * pair] + dk[2 * pair + 1]
            dgab_ref[sl, :] = sum(a + b for a, b in zip(dga, dgb))
            return carry

        lax.fori_loop(0, cb, one, 0)

    groups = GD_HEADS // HPS
    outs = [jax.ShapeDtypeStruct((t, 1024), F32), jax.ShapeDtypeStruct((t, 1024), F32),
            jax.ShapeDtypeStruct((t, 2048), F32), jax.ShapeDtypeStruct((t, groups * HEAD), F32),
            jax.ShapeDtypeStruct((t, 2048), BF16), jax.ShapeDtypeStruct((8, HEAD), F32)]
    return pl.pallas_call(
        body, name="gdn_bwd", grid=(nb, groups),
        in_specs=[qk_tile, qk_tile, v_tile, gab_tile, v_tile, row128, row128, row128, v_tile,
                  pl.BlockSpec((cb, HPS, HEAD, HEAD), lambda c, g: (nb - 1 - c, g, 0, 0)), v_tile,
                  pl.BlockSpec(sel.shape, lambda c, g: (0, 0, 0)),
                  pl.BlockSpec(lmat.shape, lambda c, g: (0, 0)),
                  pl.BlockSpec(lmat_t.shape, lambda c, g: (0, 0)),
                  pl.BlockSpec(masks.shape, lambda c, g: (0, 0, 0))],
        out_specs=[qk_tile, qk_tile, v_tile, pl.BlockSpec((rows, HEAD), lambda c, g: (nb - 1 - c, g)), v_tile,
                   pl.BlockSpec((8, HEAD), lambda c, g: (0, 0))],
        out_shape=outs, scratch_shapes=[pltpu.VMEM((GD_HEADS, HEAD, HEAD), F32)],
        compiler_params=_params(_ARB, _ARB))(qn, kn, cv, gab, gz, alog, dtb, gain, oraw, ssave, dog, sel,
                                             lmat, lmat_t, masks)


def _fold_groups(wide):
    t, width = wide.shape
    tr = _pick(t, 512, 8)

    def body(w_ref, o_ref):
        acc = w_ref[:, 0:HEAD]
        for j in range(1, width // HEAD):
            acc = acc + w_ref[:, j * HEAD:(j + 1) * HEAD]
        o_ref[...] = acc.astype(BF16)

    return pl.pallas_call(
        body, name="fold_gate_grads", grid=(t // tr,), in_specs=[_row_spec(tr, width)], out_specs=_row_spec(tr, HEAD),
        out_shape=jax.ShapeDtypeStruct((t, HEAD), BF16), compiler_params=_params(_PAR))(wide)


def _adam_math(w, g, m, v):
    m2 = ADAM_B1 * m + (1.0 - ADAM_B1) * g
    v2 = ADAM_B2 * v + (1.0 - ADAM_B2) * (g * g)
    m_hat = m2 / (1.0 - ADAM_B1 ** ADAM_STEP)
    v_hat = v2 / (1.0 - ADAM_B2 ** ADAM_STEP)
    delta = -ADAM_LR * (m_hat / (jnp.sqrt(v_hat) + ADAM_EPS) + ADAM_WD * w)
    return delta, m2, v2


def _adamw(w, g, m, v, name):
    r, c = w.shape
    tr = r
    for cand in range(8, r + 1, 8):
        if r % cand == 0 and cand * c * 4 <= (1 << 20):
            tr = cand
    if r % 8 != 0:
        tr = r

    def body(w_ref, g_ref, m_ref, v_ref, d_ref, m2_ref, v2_ref):
        d, m2, v2 = _adam_math(w_ref[...], g_ref[...], m_ref[...], v_ref[...])
        d_ref[...] = d
        m2_ref[...] = m2
        v2_ref[...] = v2

    spec = pl.BlockSpec((tr, c), lambda i: (i, 0))
    return pl.pallas_call(
        body, name=name, grid=(r // tr,), in_specs=[spec] * 4, out_specs=[spec] * 3,
        out_shape=[jax.ShapeDtypeStruct((r, c), F32)] * 3, compiler_params=_params(_PAR))(w, g, m, v)


_ANY = pl.BlockSpec(memory_space=pl.ANY)


def _place():
    return lax.axis_index("x"), lax.axis_index("y"), lax.axis_index("c")


def _gather_weights(packed):
    rows = packed.shape[0]
    half = rows // 2
    nch = COMM_CHUNKS
    ch = half // nch
    assert ch * nch == half and ch % 16 == 0

    def body(p_ref, g_ref, send_sems, recv_sems, local_sems):
        x, y, c = _place()
        sibling = (x, y, 1 - c)
        chips = [(1 - x, y), (x, 1 - y), (1 - x, 1 - y)]

        def rows_of(pc, q):
            return pl.ds(pl.multiple_of(pc * half + q * ch, 16), ch)

        def piece(px, py, pc, q):
            return g_ref.at[2 * px + py, rows_of(pc, q), :]

        def copy(k, src, dst, to):
            return pltpu.make_async_remote_copy(src_ref=src, dst_ref=dst, send_sem=send_sems.at[k],
                                                recv_sem=recv_sems.at[k], device_id=to, device_id_type=MESH)

        local = [pltpu.make_async_copy(p_ref.at[rows_of(pc, q), :], piece(x, y, pc, q), local_sems.at[pc * nch + q])
                 for pc in range(2) for q in range(nch)]
        for cp in local:
            cp.start()
        first = [[copy(j * nch + q, p_ref.at[rows_of(c, q), :], piece(x, y, c, q), (*chip, c)) for q in range(nch)]
                 for j, chip in enumerate(chips)]
        for q in range(nch):
            for j in range(3):
                first[j][q].start()
        passed = [[copy((3 + j) * nch + q, piece(*chip, c, q), piece(*chip, c, q), sibling) for q in range(nch)]
                  for j, chip in enumerate(chips)]
        for q in range(nch):
            for j, chip in enumerate(chips):
                copy(j * nch + q, p_ref.at[rows_of(c, q), :], piece(*chip, c, q), (*chip, c)).wait_recv()
                passed[j][q].start()
        for q in range(nch):
            for j, chip in enumerate(chips):
                copy((3 + j) * nch + q, piece(*chip, 1 - c, q), piece(*chip, 1 - c, q), sibling).wait_recv()
        for j in range(3):
            for q in range(nch):
                first[j][q].wait_send()
                passed[j][q].wait_send()
        for cp in local:
            cp.wait()

    return pl.pallas_call(
        body, name="gather_weights", out_shape=jax.ShapeDtypeStruct((4, rows, 1024), packed.dtype),
        in_specs=[_ANY], out_specs=_ANY,
        scratch_shapes=[pltpu.SemaphoreType.DMA((6 * nch,)), pltpu.SemaphoreType.DMA((6 * nch,)),
                        pltpu.SemaphoreType.DMA((2 * nch,))])(packed)


def _reduce_pair(give):
    half = give.shape[1]
    nch = COMM_CHUNKS
    ch = half // nch
    assert ch * nch == half and ch % 16 == 0

    def body(g_ref, got_ref, send_sems, recv_sems):
        x, y, c = _place()
        copies = [pltpu.make_async_remote_copy(
            src_ref=g_ref.at[s, pl.ds(q * ch, ch), :], dst_ref=got_ref.at[s, pl.ds(q * ch, ch), :],
            send_sem=send_sems.at[s * nch + q], recv_sem=recv_sems.at[s * nch + q],
            device_id=(x, y, 1 - c), device_id_type=MESH) for s in range(4) for q in range(nch)]
        for cp in copies:
            cp.start()
        for cp in copies:
            cp.wait()

    return pl.pallas_call(
        body, name="reduce_pair", out_shape=jax.ShapeDtypeStruct(give.shape, give.dtype), in_specs=[_ANY],
        out_specs=_ANY,
        scratch_shapes=[pltpu.SemaphoreType.DMA((4 * nch,)), pltpu.SemaphoreType.DMA((4 * nch,))])(give)


def _add2(a, b):
    n, rows, w = a.shape
    tr = _pick(rows, 512, 16)
    if rows % tr:
        tr = 16 * max(d for d in range(1, 33) if (rows // 16) % d == 0)
    spec = pl.BlockSpec((1, tr, w), lambda i, j: (i, j, 0))

    def body(a_ref, b_ref, o_ref):
        o_ref[...] = (a_ref[...].astype(F32) + b_ref[...].astype(F32)).astype(BF16)

    return pl.pallas_call(
        body, name="add_pair", grid=(n, rows // tr), in_specs=[spec, spec], out_specs=spec,
        out_shape=jax.ShapeDtypeStruct(a.shape, BF16), compiler_params=_params(_PAR, _PAR))(a, b)


def _reduce_chips(partial):
    half = partial.shape[1]
    nch = COMM_CHUNKS
    ch = half // nch
    assert ch * nch == half and ch % 16 == 0

    def body(p_ref, got_ref, send_sems, recv_sems):
        x, y, c = _place()
        chips = [(1 - x, y), (x, 1 - y), (1 - x, 1 - y)]
        copies = [pltpu.make_async_remote_copy(
            src_ref=p_ref.at[2 * px + py, pl.ds(q * ch, ch), :], dst_ref=got_ref.at[j, pl.ds(q * ch, ch), :],
            send_sem=send_sems.at[j * nch + q], recv_sem=recv_sems.at[j * nch + q],
            device_id=(px, py, c), device_id_type=MESH) for q in range(nch) for j, (px, py) in enumerate(chips)]
        for cp in copies:
            cp.start()
        for cp in copies:
            cp.wait()

    return pl.pallas_call(
        body, name="reduce_chips", out_shape=jax.ShapeDtypeStruct((3, half, 1024), partial.dtype),
        in_specs=[_ANY], out_specs=_ANY,
        scratch_shapes=[pltpu.SemaphoreType.DMA((3 * nch,)), pltpu.SemaphoreType.DMA((3 * nch,))])(partial)


def _add4(own, got):
    rows, w = own.shape
    tr = _pick(rows, 512, 16)
    if rows % tr:
        tr = 16 * max(d for d in range(1, 33) if (rows // 16) % d == 0)

    def body(a_ref, b_ref, o_ref):
        o_ref[...] = ((a_ref[...].astype(F32) + b_ref[0].astype(F32)) + b_ref[1].astype(F32)) + b_ref[2].astype(F32)

    return pl.pallas_call(
        body, name="add_chips", grid=(rows // tr,),
        in_specs=[pl.BlockSpec((tr, w), lambda i: (i, 0)), pl.BlockSpec((3, tr, w), lambda i: (0, i, 0))],
        out_specs=pl.BlockSpec((tr, w), lambda i: (i, 0)), out_shape=jax.ShapeDtypeStruct((rows, w), F32),
        compiler_params=_params(_PAR))(own, got)


def _share_pair(red):
    half = red.shape[0]
    nch = COMM_CHUNKS
    ch = half // nch
    assert ch * nch == half and ch % 8 == 0

    def body(r_ref, full_ref, send_sems, recv_sems, local_sems):
        x, y, c = _place()
        sibling = (x, y, 1 - c)

        def dst(pc, q):
            return full_ref.at[pl.ds(pl.multiple_of(pc * half + q * ch, 8), ch), :]

        local = [pltpu.make_async_copy(r_ref.at[pl.ds(q * ch, ch), :], dst(c, q), local_sems.at[q]) for q in range(nch)]
        remote = [pltpu.make_async_remote_copy(src_ref=r_ref.at[pl.ds(q * ch, ch), :], dst_ref=dst(c, q),
                                               send_sem=send_sems.at[q], recv_sem=recv_sems.at[q],
                                               device_id=sibling, device_id_type=MESH) for q in range(nch)]
        for cp in local + remote:
            cp.start()
        for q in range(nch):
            remote[q].wait_send()
            pltpu.make_async_remote_copy(src_ref=r_ref.at[pl.ds(q * ch, ch), :], dst_ref=dst(1 - c, q),
                                         send_sem=send_sems.at[q], recv_sem=recv_sems.at[q],
                                         device_id=sibling, device_id_type=MESH).wait_recv()
        for cp in local:
            cp.wait()

    return pl.pallas_call(
        body, name="share_pair", out_shape=jax.ShapeDtypeStruct((2 * half, 1024), red.dtype),
        in_specs=[_ANY], out_specs=_ANY,
        scratch_shapes=[pltpu.SemaphoreType.DMA((nch,)), pltpu.SemaphoreType.DMA((nch,)),
                        pltpu.SemaphoreType.DMA((nch,))])(red)


def _small_sync(gs, ws, ms, vs):
    rows = gs.shape[0]
    vmem = pl.BlockSpec(memory_space=pltpu.VMEM)

    def body(g_ref, w_ref, m_ref, v_ref, sum_ref, d_ref, m2_ref, v2_ref, buf, send_sems, recv_sems):
        x, y, c = _place()
        me = 4 * x + 2 * y + c
        buf[me] = g_ref[...]
        copies = []
        for k in range(1, 8):
            peer = (x ^ (k >> 2), y ^ ((k >> 1) & 1), c ^ (k & 1))
            copies.append(pltpu.make_async_remote_copy(
                src_ref=g_ref, dst_ref=buf.at[me], send_sem=send_sems.at[k - 1], recv_sem=recv_sems.at[k - 1],
                device_id=peer, device_id_type=MESH))
        for cp in copies:
            cp.start()
        for cp in copies:
            cp.wait()
        total = buf[0]
        for i in range(1, 8):
            total = total + buf[i]
        sum_ref[...] = total
        d, m2, v2 = _adam_math(w_ref[...], total, m_ref[...], v_ref[...])
        d_ref[...] = d
        m2_ref[...] = m2
        v2_ref[...] = v2

    shape = jax.ShapeDtypeStruct((rows, 128), F32)
    return pl.pallas_call(
        body, name="small_sync", out_shape=[shape] * 4, in_specs=[vmem] * 4, out_specs=[vmem] * 4,
        scratch_shapes=[pltpu.VMEM((8, rows, 128), F32), pltpu.SemaphoreType.DMA((7,)),
                        pltpu.SemaphoreType.DMA((7,))])(gs, ws, ms, vs)


_BIG = (("ffn1_w_in", 1408, 1408), ("ffn1_w_out", 704, 704), ("w_in", 3080, 3088), ("gdn_conv_w", 4, 16),
        ("w_branch_hgrn", 256, 256), ("w_branch_gdn", 512, 512), ("w_out", 256, 256),
        ("ffn2_w_in", 1408, 1408), ("ffn2_w_out", 704, 704))
_BIG_ROWS = sum(p for _, _, p in _BIG)
_COL_SHARDED = ("ffn1_w_in", "w_in", "gdn_conv_w", "ffn2_w_in")


def _pack_rows(parts, lead):
    out = []
    for name, rows, padded in _BIG:
        p = parts[name]
        if padded != rows:
            p = jnp.concatenate([p, jnp.zeros(lead + (padded - rows, 1024), p.dtype)], axis=len(lead))
        out.append(p)
    return jnp.concatenate(out, axis=len(lead))


def _unpack_rows(packed):
    out, off = {}, 0
    for name, rows, padded in _BIG:
        out[name] = packed[..., off:off + rows, :]
        off += padded
    return out


def _full_from_shards(name, g):
    if name in _COL_SHARDED:
        r = {"gdn_conv_w": CONV_K}.get(name, D_MODEL)
        return jnp.transpose(g.reshape(4, r, -1), (1, 0, 2)).reshape(r, -1)
    return g.reshape(-1, 1024)


def _shards_from_full(name, full):
    if name in _COL_SHARDED:
        r = full.shape[0]
        return jnp.transpose(full.reshape(r, 4, -1), (1, 0, 2)).reshape(4, -1, 1024)
    return full.reshape(4, -1, 1024)


_SMALL = (("ffn1_norm", 8), ("mix_norm", 8), ("hgrn_lb_logits", 16), ("hgrn_out_norm", 1), ("gdn_a_log", 1),
          ("gdn_dt_bias", 1), ("gdn_out_norm", 1), ("ffn2_norm", 8), ("final_norm", 8), ("loss", 1))
_SMALL_ROWS = 56


def _pack_small(parts):
    out = []
    for name, rows in _SMALL:
        p = parts[name].reshape(-1).astype(F32)
        p = jnp.concatenate([p, jnp.zeros((rows * 128 - p.shape[0],), F32)]) if p.shape[0] != rows * 128 else p
        out.append(p.reshape(rows, 128))
    used = sum(r for _, r in _SMALL)
    out.append(jnp.zeros((_SMALL_ROWS - used, 128), F32))
    return jnp.concatenate(out, axis=0)


def _unpack_small(packed, shapes):
    out, off = {}, 0
    for name, rows in _SMALL:
        n = int(np.prod(shapes[name]))
        out[name] = packed[off:off + rows].reshape(-1)[:n].reshape(shapes[name])
        off += rows
    return out


def _ffn_fwd(x, gain, w_in, w_out, tag):
    n = _rmsnorm_fwd(x, gain, tag + "_norm")
    ab = _mm(n, w_in, out_dtype=BF16, name=tag + "_in")
    hm = _swiglu_fwd(ab, tag + "_act")
    out = _mm(hm, w_out, alpha=0.5, res=x, name=tag + "_out")
    return out, (n, ab)


def _ffn_bwd(x, gain, w_in, w_out, saved, dout, dout_bf, tag):
    n, ab = saved
    dhm = _mm(dout_bf, w_out, tb=True, alpha=0.5, name=tag + "_dact")
    dab, hm = _swiglu_bwd(ab, dhm, tag + "_dswiglu")
    dw_out = _mm(hm, dout_bf, ta=True, alpha=0.5, name=tag + "_dwout")
    dw_in = _mm(n, dab, ta=True, name=tag + "_dwin")
    dn = _mm(dab, w_in, tb=True, name=tag + "_dnorm")
    dx, dx_bf, dgain = _rmsnorm_bwd(x, gain, dn, dout, tag + "_dx")
    return dx, dx_bf, dgain, dw_in, dw_out


def _pad_lanes(v):
    return jnp.concatenate([v.reshape(1, -1), jnp.zeros((1, HEAD - v.size), F32)], axis=1)


def _local_step(x, tgt, w, small):
    hg_c = _hg_consts()
    gd_c = _gd_consts()
    seg, off = {}, 0
    for name, size in zip(IN_NAMES, IN_SIZES):
        seg[name] = w["w_in"][:, off:off + size]
        off += size
    w_gab = jnp.concatenate([seg["ga"], seg["gb"], jnp.zeros((D_MODEL, HEAD - 32), BF16)], axis=1)
    big_segs = [n for n in IN_NAMES if n not in ("ga", "gb")]
    conv8 = jnp.concatenate([w["gdn_conv_w"].astype(F32), jnp.zeros((8 - CONV_K, 4096), F32)], axis=0)
    conv_q, conv_k, conv_v = conv8[:, :1024], conv8[:, 1024:2048], conv8[:, 2048:]
    alog = _pad_lanes(small["gdn_a_log"])
    dtb = _pad_lanes(small["gdn_dt_bias"])
    logits = small["hgrn_lb_logits"]
    hg_gain = small["hgrn_out_norm"].reshape(1, HEAD)
    gd_gain = small["gdn_out_norm"].reshape(1, HEAD)
    g1, gm, g2 = small["ffn1_norm"].reshape(1, -1), small["mix_norm"].reshape(1, -1), small["ffn2_norm"].reshape(1, -1)
    gf = small["final_norm"].reshape(1, -1)
    qscale = HEAD ** -0.5

    h1, ffn1_saved = _ffn_fwd(x, g1, w["ffn1_w_in"], w["ffn1_w_out"], "ffn1")
    u = _rmsnorm_fwd(h1, gm, "mix_norm")
    pr = {n: _mm(u, seg[n], name="proj_" + n) for n in big_segs}
    gab = _mm(u, w_gab, name="proj_gab")
    oh_raw, oh, s_h = _hgrn_fwd(pr["hq"], pr["hf"], pr["hi"], pr["hg"], logits, hg_gain, hg_c)
    qn = _conv_fwd(pr["gq"], conv_q, qscale, "conv_q")
    kn = _conv_fwd(pr["gk"], conv_k, 1.0, "conv_k")
    cv = _conv_fwd(pr["gv"], conv_v, None, "conv_v")
    og_raw, og, s_g = _gdn_fwd(qn, kn, cv, gab, pr["gz"], alog, dtb, gd_gain, gd_c)
    yh = _mm(oh, w["w_branch_hgrn"], name="branch_h")
    yg = _mm(og, w["w_branch_gdn"], name="branch_g")
    ym = _merge_fwd(yh, yg, pr["gate_h"], pr["gate_g"])
    h2 = _mm(ym, w["w_out"], res=h1, name="mix_out")
    h3, ffn2_saved = _ffn_fwd(h2, g2, w["ffn2_w_in"], w["ffn2_w_out"], "ffn2")
    loss, dh3, dh3_bf, d_gf = _final_loss(h3, gf, tgt)

    dh2, dh2_bf, d_g2, d_f2in, d_f2out = _ffn_bwd(h2, g2, w["ffn2_w_in"], w["ffn2_w_out"], ffn2_saved, dh3, dh3_bf,
                                                  "ffn2")
    dym = _mm(dh2_bf, w["w_out"], tb=True, name="d_merge")
    d_wout = _mm(ym, dh2_bf, ta=True, name="d_w_out")
    dyh, dyg, d_gate_h, d_gate_g = _merge_bwd(dym, yh, yg, pr["gate_h"], pr["gate_g"])
    d_wbh = _mm(oh, dyh, ta=True, name="d_w_branch_h")
    d_wbg = _mm(og, dyg, ta=True, name="d_w_branch_g")
    doh = _mm(dyh, w["w_branch_hgrn"], tb=True, name="d_oh")
    dog = _mm(dyg, w["w_branch_gdn"], tb=True, name="d_og")
    d_hq, d_hf, d_hi, d_hg, d_hg_gain, d_lb0 = _hgrn_bwd(pr["hq"], pr["hf"], pr["hi"], pr["hg"], logits, hg_gain,
                                                        oh_raw, s_h, doh, hg_c)
    d_qn, d_kn, d_cv, d_gab_wide, d_gz, gd_small = _gdn_bwd(qn, kn, cv, gab, pr["gz"], alog, dtb, gd_gain, og_raw,
                                                            s_g, dog, gd_c)
    d_gab = _fold_groups(d_gab_wide)
    dc_q, dwc_q = _conv_bwd_a(pr["gq"], conv_q, d_qn, qscale, "dconv_q")
    dc_k, dwc_k = _conv_bwd_a(pr["gk"], conv_k, d_kn, 1.0, "dconv_k")
    dc_v, dwc_v = _conv_bwd_a(pr["gv"], conv_v, d_cv, None, "dconv_v")
    d_gq = _conv_bwd_b(dc_q, conv_q, "dconvx_q")
    d_gk = _conv_bwd_b(dc_k, conv_k, "dconvx_k")
    d_gv = _conv_bwd_b(dc_v, conv_v, "dconvx_v")
    dpr = {"hq": d_hq, "hf": d_hf, "hi": d_hi, "hg": d_hg, "gq": d_gq, "gk": d_gk, "gv": d_gv, "gz": d_gz,
           "gate_h": d_gate_h, "gate_g": d_gate_g}
    du = _mm(d_gab, w_gab, tb=True, name="du_gab")
    d_wseg = {}
    for n in big_segs:
        du = _mm(dpr[n], seg[n], tb=True, res=du, name="du_" + n)
        d_wseg[n] = _mm(u, dpr[n], ta=True, name="dw_" + n)
    d_wgab = _mm(u, d_gab, ta=True, name="dw_gab")
    d_wseg["ga"], d_wseg["gb"] = d_wgab[:, :16], d_wgab[:, 16:32]
    d_win = jnp.concatenate([d_wseg[n] for n in IN_NAMES], axis=1)
    dh1, dh1_bf, d_gm = _rmsnorm_bwd(h1, gm, du, dh2, "mix_dnorm")
    dx, _, d_g1, d_f1in, d_f1out = _ffn_bwd(x, g1, w["ffn1_w_in"], w["ffn1_w_out"], ffn1_saved, dh1, dh1_bf, "ffn1")

    d_conv = jnp.concatenate([dwc_q[:CONV_K], dwc_k[:CONV_K], dwc_v[:CONV_K]], axis=1)
    big = {"ffn1_w_in": d_f1in, "ffn1_w_out": d_f1out, "w_in": d_win, "gdn_conv_w": d_conv,
           "w_branch_hgrn": d_wbh, "w_branch_gdn": d_wbg, "w_out": d_wout, "ffn2_w_in": d_f2in, "ffn2_w_out": d_f2out}
    d_lb0 = d_lb0.reshape(1, -1)
    sm = {"ffn1_norm": d_g1, "mix_norm": d_gm, "hgrn_lb_logits": jnp.concatenate([d_lb0, -d_lb0], axis=0),
          "hgrn_out_norm": d_hg_gain, "gdn_a_log": gd_small[2, :16], "gdn_dt_bias": gd_small[1, :16],
          "gdn_out_norm": gd_small[0], "ffn2_norm": d_g2, "final_norm": d_gf, "loss": loss[0, :1]}
    return dx, big, sm


_WEIGHTS = ("ffn1_norm", "ffn1_w_in", "ffn1_w_out", "mix_norm", "w_in", "hgrn_lb_logits", "hgrn_out_norm",
            "gdn_conv_w", "gdn_a_log", "gdn_dt_bias", "gdn_out_norm", "w_branch_hgrn", "w_branch_gdn", "w_out",
            "ffn2_norm", "ffn2_w_in", "ffn2_w_out", "final_norm")
_BIG_NAMES = tuple(n for n, _, _ in _BIG)


def kernel(x, ffn1_norm, ffn1_w_in, ffn1_w_out, mix_norm, w_in, hgrn_lb_logits, hgrn_out_norm, gdn_conv_w, gdn_a_log, gdn_dt_bias, gdn_out_norm, w_branch_hgrn, w_branch_gdn, w_out, ffn2_norm, ffn2_w_in, ffn2_w_out, final_norm, loss_target, m_ffn1_norm, m_ffn1_w_in, m_ffn1_w_out, m_mix_norm, m_w_in, m_hgrn_lb_logits, m_hgrn_out_norm, m_gdn_conv_w, m_gdn_a_log, m_gdn_dt_bias, m_gdn_out_norm, m_w_branch_hgrn, m_w_branch_gdn, m_w_out, m_ffn2_norm, m_ffn2_w_in, m_ffn2_w_out, m_final_norm, v_ffn1_norm, v_ffn1_w_in, v_ffn1_w_out, v_mix_norm, v_w_in, v_hgrn_lb_logits, v_hgrn_out_norm, v_gdn_conv_w, v_gdn_a_log, v_gdn_dt_bias, v_gdn_out_norm, v_w_branch_hgrn, v_w_branch_gdn, v_w_out, v_ffn2_norm, v_ffn2_w_in, v_ffn2_w_out, v_final_norm):
    args = dict(locals())
    wts = {n: args[n] for n in _WEIGHTS}
    moms = {n: args["m_" + n] for n in _WEIGHTS}
    vars_ = {n: args["v_" + n] for n in _WEIGHTS}

    shard2d = {n: wts[n].reshape(-1, 1024) for n in _BIG_NAMES}
    packed = _pack_rows({n: shard2d[n].astype(BF16) for n in _BIG_NAMES}, ())
    gathered = _unpack_rows(_gather_weights(packed))
    full = {n: _full_from_shards(n, gathered[n]) for n in _BIG_NAMES}
    small = {n: wts[n].astype(F32) for n in _WEIGHTS if n not in _BIG_NAMES}

    dx, big_grads, small_grads = _local_step(x[0], loss_target[0], full, small)

    gpack = _pack_rows({n: _shards_from_full(n, big_grads[n]).astype(BF16) for n in _BIG_NAMES}, (4,))
    xi, yi, ci = lax.axis_index("x"), lax.axis_index("y"), lax.axis_index("c")
    half = _BIG_ROWS // 2
    own = lax.dynamic_slice_in_dim(gpack, ci * half, half, axis=1)
    give = lax.dynamic_slice_in_dim(gpack, (1 - ci) * half, half, axis=1)
    chip_sums = _add2(own, _reduce_pair(give))
    own_chip = lax.dynamic_index_in_dim(chip_sums, 2 * xi + yi, axis=0, keepdims=False)
    reduced = _unpack_rows(_share_pair(_add4(own_chip, _reduce_chips(chip_sums))))

    out_g, out_d, out_m, out_v = {}, {}, {}, {}
    for n in _BIG_NAMES:
        shape = wts[n].shape
        w2 = wts[n].reshape(shape[-2], shape[-1])
        g2 = reduced[n].reshape(shape[-2], shape[-1])
        d, m2, v2 = _adamw(w2, g2, moms[n].reshape(w2.shape), vars_[n].reshape(w2.shape), "adamw_" + n)
        out_g[n], out_d[n], out_m[n], out_v[n] = g2.reshape(shape), d.reshape(shape), m2.reshape(shape), v2.reshape(shape)

    small_names = [n for n, _ in _SMALL]
    zero = jnp.zeros((1,), F32)
    shapes = {n: (wts[n].shape if n != "loss" else (1,)) for n in small_names}
    sums, sd, sm_, sv = _small_sync(
        _pack_small(small_grads),
        _pack_small({n: (wts[n] if n != "loss" else zero) for n in small_names}),
        _pack_small({n: (moms[n] if n != "loss" else zero) for n in small_names}),
        _pack_small({n: (vars_[n] if n != "loss" else zero) for n in small_names}))
    sg_u, sd_u, sm_u, sv_u = (_unpack_small(p, shapes) for p in (sums, sd, sm_, sv))
    for n in small_names:
        if n != "loss":
            out_g[n], out_d[n], out_m[n], out_v[n] = sg_u[n], sd_u[n], sm_u[n], sv_u[n]
    loss = sg_u["loss"].reshape(())

    return (loss, dx[None], *[out_g[n] for n in _WEIGHTS], *[out_d[n] for n in _WEIGHTS],
            *[out_m[n] for n in _WEIGHTS], *[out_v[n] for n in _WEIGHTS])
```

```python
import numpy as np

import jax
import jax.numpy as jnp
from jax import lax
from jax.experimental import pallas as pl
from jax.experimental.pallas import tpu as pltpu

F32 = jnp.float32
BF16 = jnp.bfloat16

D_MODEL = 1024
D_FF = 2816
CHUNK = 64
HEAD = 128
HG_HEADS = 8
GD_HEADS = 16
HPS = 8
COMM_CHUNKS = 9
MM_TM = 1408
MM_TN = 512
MM_TK = 1536
VMEM_LIMIT = 48 * 1024 * 1024
EPS = 1e-6
CONV_K = 4
IN_NAMES = ("hq", "hf", "hi", "hg", "gq", "gk", "gv", "ga", "gb", "gz", "gate_h", "gate_g")
IN_SIZES = (1024, 1024, 1024, 1024, 1024, 1024, 2048, 16, 16, 2048, 1024, 1024)
IN_WIDTH = sum(IN_SIZES)

ADAM_LR = 0.001
ADAM_B1 = 0.9
ADAM_B2 = 0.999
ADAM_EPS = 1e-08
ADAM_WD = 0.01
ADAM_STEP = 10

MESH = pl.DeviceIdType.MESH
_ARB = "arbitrary"
_PAR = "parallel"


def _bf(x):
    return x.astype(BF16)


def _dot(a, b):
    return jnp.dot(_bf(a), _bf(b), preferred_element_type=F32)


def _dot_nt(a, b):
    return lax.dot_general(_bf(a), _bf(b), (((1,), (1,)), ((), ())), preferred_element_type=F32)


def _dot_tn(a, b):
    return lax.dot_general(_bf(a), _bf(b), (((0,), (0,)), ((), ())), preferred_element_type=F32)


def _split3(x):
    hi = _bf(x)
    r = x - hi.astype(F32)
    mid = _bf(r)
    lo = _bf(r - mid.astype(F32))
    return hi, mid, lo


def _dot_mx(m, x):
    hi, mid, lo = _split3(x)
    return (jnp.dot(m, hi, preferred_element_type=F32) + jnp.dot(m, mid, preferred_element_type=F32)
            + jnp.dot(m, lo, preferred_element_type=F32))


def _dot_xm(x, m):
    hi, mid, lo = _split3(x)
    return (jnp.dot(hi, m, preferred_element_type=F32) + jnp.dot(mid, m, preferred_element_type=F32)
            + jnp.dot(lo, m, preferred_element_type=F32))


def _dot_hp(a, b):
    ah = _bf(a)
    al = _bf(a - ah.astype(F32))
    bh = _bf(b)
    bl = _bf(b - bh.astype(F32))
    return (jnp.dot(ah, bh, preferred_element_type=F32) + jnp.dot(ah, bl, preferred_element_type=F32)
            + jnp.dot(al, bh, preferred_element_type=F32))


def _sigmoid(x):
    return jax.nn.sigmoid(x)


def _silu(x):
    return x * _sigmoid(x)


def _dsilu(x):
    s = _sigmoid(x)
    return s * (1.0 + x * (1.0 - s))


def _softplus(x):
    return jnp.maximum(x, 0.0) + jnp.log(1.0 + jnp.exp(-jnp.abs(x)))


def _rowsum(x):
    return jnp.sum(x, axis=1, keepdims=True)


def _col_to_row(col, eye):
    return jnp.sum(eye * col, axis=0, keepdims=True)


def _row_to_col(row, eye):
    return jnp.sum(eye * row, axis=1, keepdims=True)


def _pick(dim, pref, unit=128):
    if dim <= pref:
        return dim
    t = pref
    while t >= unit:
        if dim % t == 0:
            return t
        t -= unit
    return dim


def _params(*sem):
    return pltpu.CompilerParams(dimension_semantics=tuple(sem), vmem_limit_bytes=VMEM_LIMIT)


def _mm(a, b, *, ta=False, tb=False, alpha=1.0, res=None, out_dtype=F32, name="mm"):
    m = a.shape[1] if ta else a.shape[0]
    k = a.shape[0] if ta else a.shape[1]
    n = b.shape[0] if tb else b.shape[1]
    assert k == (b.shape[1] if tb else b.shape[0])
    tm, tn, tk = _pick(m, MM_TM), _pick(n, MM_TN), _pick(k, MM_TK)
    nk = k // tk
    a_spec = pl.BlockSpec((tk, tm), lambda i, j, l: (l, i)) if ta else pl.BlockSpec((tm, tk), lambda i, j, l: (i, l))
    b_spec = pl.BlockSpec((tn, tk), lambda i, j, l: (j, l)) if tb else pl.BlockSpec((tk, tn), lambda i, j, l: (l, j))
    o_spec = pl.BlockSpec((tm, tn), lambda i, j, l: (i, j))
    dims = (((0 if ta else 1,), (1 if tb else 0,)), ((), ()))
    has_res = res is not None

    def finish(r, r_ref, o_ref):
        if alpha != 1.0:
            r = r * alpha
        if has_res:
            r = r + r_ref[...]
        o_ref[...] = r.astype(out_dtype)

    def body(*refs):
        a_ref, b_ref = refs[0], refs[1]
        r_ref = refs[2] if has_res else None
        o_ref = refs[3] if has_res else refs[2]
        part = lax.dot_general(_bf(a_ref[...]), _bf(b_ref[...]), dims, preferred_element_type=F32)
        if nk == 1:
            finish(part, r_ref, o_ref)
            return
        acc = refs[-1]
        step = pl.program_id(2)

        @pl.when(step == 0)
        def _():
            acc[...] = part

        @pl.when(step != 0)
        def _():
            acc[...] += part

        @pl.when(step == nk - 1)
        def _():
            finish(acc[...], r_ref, o_ref)

    ins = [a, b] + ([res] if has_res else [])
    in_specs = [a_spec, b_spec] + ([o_spec] if has_res else [])
    return pl.pallas_call(
        body, name=name, grid=(m // tm, n // tn, nk), in_specs=in_specs, out_specs=o_spec,
        out_shape=jax.ShapeDtypeStruct((m, n), out_dtype),
        scratch_shapes=[pltpu.VMEM((tm, tn), F32)] if nk > 1 else [],
        compiler_params=_params(_PAR, _PAR, _ARB))(*ins)


def _row_spec(tr, w):
    return pl.BlockSpec((tr, w), lambda i: (i, 0))


def _full_spec(shape):
    return pl.BlockSpec(shape, lambda i: tuple(0 for _ in shape))


def _rmsnorm_fwd(x, g, name):
    t, d = x.shape
    tr = _pick(t, 256, 8)

    def body(x_ref, g_ref, o_ref):
        xv = x_ref[...]
        r = lax.rsqrt(jnp.mean(xv * xv, axis=1, keepdims=True) + EPS)
        o_ref[...] = (xv * r * g_ref[...]).astype(BF16)

    return pl.pallas_call(
        body, name=name, grid=(t // tr,), in_specs=[_row_spec(tr, d), _full_spec((1, d))],
        out_specs=_row_spec(tr, d), out_shape=jax.ShapeDtypeStruct((t, d), BF16),
        compiler_params=_params(_PAR))(x, g)


def _rmsnorm_bwd(x, g, dn, res, name):
    t, d = x.shape
    tr = _pick(t, 256, 8)

    def body(x_ref, g_ref, dn_ref, r_ref, dx_ref, dxb_ref, dg_ref):
        @pl.when(pl.program_id(0) == 0)
        def _():
            dg_ref[...] = jnp.zeros_like(dg_ref)

        xv = x_ref[...]
        r = lax.rsqrt(jnp.mean(xv * xv, axis=1, keepdims=True) + EPS)
        xh = xv * r
        dy = dn_ref[...]
        dg_ref[...] += jnp.sum(dy * xh, axis=0, keepdims=True)
        dxh = dy * g_ref[...]
        dx = r_ref[...] + r * (dxh - xh * jnp.mean(dxh * xh, axis=1, keepdims=True))
        dx_ref[...] = dx
        dxb_ref[...] = dx.astype(BF16)

    return pl.pallas_call(
        body, name=name, grid=(t // tr,),
        in_specs=[_row_spec(tr, d), _full_spec((1, d)), _row_spec(tr, d), _row_spec(tr, d)],
        out_specs=[_row_spec(tr, d), _row_spec(tr, d), _full_spec((1, d))],
        out_shape=[jax.ShapeDtypeStruct((t, d), F32), jax.ShapeDtypeStruct((t, d), BF16),
                   jax.ShapeDtypeStruct((1, d), F32)],
        compiler_params=_params(_ARB))(x, g, dn, res)


def _swiglu_fwd(ab, name):
    t = ab.shape[0]
    tr = _pick(t, 256, 8)

    def body(ab_ref, o_ref):
        a = ab_ref[:, :D_FF].astype(F32)
        b = ab_ref[:, D_FF:].astype(F32)
        o_ref[...] = (_silu(a) * b).astype(BF16)

    return pl.pallas_call(
        body, name=name, grid=(t // tr,), in_specs=[_row_spec(tr, 2 * D_FF)], out_specs=_row_spec(tr, D_FF),
        out_shape=jax.ShapeDtypeStruct((t, D_FF), BF16), compiler_params=_params(_PAR))(ab)


def _swiglu_bwd(ab, dhm, name):
    t = ab.shape[0]
    tr = _pick(t, 256, 8)

    def body(ab_ref, dh_ref, dab_ref, hm_ref):
        a = ab_ref[:, :D_FF].astype(F32)
        b = ab_ref[:, D_FF:].astype(F32)
        dh = dh_ref[...]
        dab_ref[:, :D_FF] = (dh * b * _dsilu(a)).astype(BF16)
        sa = _silu(a)
        dab_ref[:, D_FF:] = (dh * sa).astype(BF16)
        hm_ref[...] = (sa * b).astype(BF16)

    return pl.pallas_call(
        body, name=name, grid=(t // tr,), in_specs=[_row_spec(tr, 2 * D_FF), _row_spec(tr, D_FF)],
        out_specs=[_row_spec(tr, 2 * D_FF), _row_spec(tr, D_FF)],
        out_shape=[jax.ShapeDtypeStruct((t, 2 * D_FF), BF16), jax.ShapeDtypeStruct((t, D_FF), BF16)],
        compiler_params=_params(_PAR))(ab, dhm)


def _merge_fwd(yh, yg, gh, gg):
    t, d = yh.shape
    tr = _pick(t, 256, 8)

    def body(yh_ref, yg_ref, gh_ref, gg_ref, o_ref):
        o_ref[...] = (_sigmoid(gh_ref[...]) * yh_ref[...] + _sigmoid(gg_ref[...]) * yg_ref[...]).astype(BF16)

    return pl.pallas_call(
        body, name="merge_fwd", grid=(t // tr,), in_specs=[_row_spec(tr, d)] * 4, out_specs=_row_spec(tr, d),
        out_shape=jax.ShapeDtypeStruct((t, d), BF16), compiler_params=_params(_PAR))(yh, yg, gh, gg)


def _merge_bwd(dy, yh, yg, gh, gg):
    t, d = yh.shape
    tr = _pick(t, 256, 8)

    def body(dy_ref, yh_ref, yg_ref, gh_ref, gg_ref, dyh_ref, dyg_ref, dgh_ref, dgg_ref):
        dyv = dy_ref[...]
        sh = _sigmoid(gh_ref[...])
        sg = _sigmoid(gg_ref[...])
        dyh_ref[...] = (dyv * sh).astype(BF16)
        dyg_ref[...] = (dyv * sg).astype(BF16)
        dgh_ref[...] = (dyv * yh_ref[...] * sh * (1.0 - sh)).astype(BF16)
        dgg_ref[...] = (dyv * yg_ref[...] * sg * (1.0 - sg)).astype(BF16)

    return pl.pallas_call(
        body, name="merge_bwd", grid=(t // tr,), in_specs=[_row_spec(tr, d)] * 5, out_specs=[_row_spec(tr, d)] * 4,
        out_shape=[jax.ShapeDtypeStruct((t, d), BF16)] * 4,
        compiler_params=_params(_PAR))(dy, yh, yg, gh, gg)


def _final_loss(h, g, tgt):
    t, d = h.shape
    tr = _pick(t, 256, 8)

    def body(h_ref, g_ref, t_ref, loss_ref, dh_ref, dhb_ref, dg_ref):
        @pl.when(pl.program_id(0) == 0)
        def _():
            dg_ref[...] = jnp.zeros_like(dg_ref)
            loss_ref[...] = jnp.zeros_like(loss_ref)

        xv = h_ref[...]
        gv = g_ref[...]
        r = lax.rsqrt(jnp.mean(xv * xv, axis=1, keepdims=True) + EPS)
        xh = xv * r
        err = xh * gv - t_ref[...]
        loss_ref[...] += 0.5 * jnp.sum(jnp.mean(err * err, axis=1, keepdims=True), axis=0, keepdims=True)
        dy = err * (1.0 / d)
        dg_ref[...] += jnp.sum(dy * xh, axis=0, keepdims=True)
        dxh = dy * gv
        dh = r * (dxh - xh * jnp.mean(dxh * xh, axis=1, keepdims=True))
        dh_ref[...] = dh
        dhb_ref[...] = dh.astype(BF16)

    return pl.pallas_call(
        body, name="final_loss", grid=(t // tr,),
        in_specs=[_row_spec(tr, d), _full_spec((1, d)), _row_spec(tr, d)],
        out_specs=[_full_spec((1, 128)), _row_spec(tr, d), _row_spec(tr, d), _full_spec((1, d))],
        out_shape=[jax.ShapeDtypeStruct((1, 128), F32), jax.ShapeDtypeStruct((t, d), F32),
                   jax.ShapeDtypeStruct((t, d), BF16), jax.ShapeDtypeStruct((1, d), F32)],
        compiler_params=_params(_ARB))(h, g, tgt)


def _hg_consts():
    c = CHUNK
    t = np.arange(c)
    mats, masks = [], []
    for lvl in range(6):
        m = 1 << lvl
        blk = t // m
        mat = np.zeros((c, c), np.float32)
        for tt in range(c):
            b = blk[tt]
            if b % 2 == 1:
                mat[tt, b * m:tt + 1] = 1.0
            else:
                mat[tt, tt + 1:(b + 1) * m] = 1.0
        mats.append(mat)
        same = (t[:, None] // (2 * m)) == (t[None, :] // (2 * m))
        masks.append((same & (blk[:, None] % 2 == 1) & (blk[None, :] % 2 == 0)).astype(np.float32))
    pre = np.tril(np.ones((c, c), np.float32))
    suf = np.triu(np.ones((c, c), np.float32), 1)
    mstack = np.concatenate(mats + [pre, suf], 0)
    masks.append(np.eye(c, dtype=np.float32))
    return (jnp.asarray(mstack, BF16), jnp.asarray(mstack.T.copy(), BF16), jnp.asarray(np.stack(masks), F32),
            jnp.asarray(np.eye(HEAD, dtype=np.float32)))


def _gd_consts():
    c = CHUNK
    incl = np.tril(np.ones((c, c), np.float32))
    strict = np.tril(np.ones((c, c), np.float32), -1)
    eye = np.eye(c, dtype=np.float32)
    masks = np.stack([incl, strict, eye, incl.T.copy()])
    sel = np.zeros((32, HEAD, HEAD), np.float32)
    for j in range(32):
        sel[j, j, :] = 1.0
    return (jnp.asarray(incl, BF16), jnp.asarray(incl.T.copy(), BF16), jnp.asarray(masks, F32), jnp.asarray(sel, BF16))


def _chunks_per_step(nc):
    for cb in (32 // HPS, 2, 1):
        if nc % cb == 0:
            return cb
    return 1


def _hg_prep(hq, hf, lg):
    lb = _sigmoid(lg[0:1, :] - lg[1:2, :])
    sg = _sigmoid(hf)
    sgn = _sigmoid(-hf)
    f = lb + (1.0 - lb) * sg
    lf = jnp.log(f)
    kk = (1.0 - lb) * sgn
    q = _silu(hq) * (HEAD ** -0.5)
    return lb, sg, sgn, f, lf, kk, q


def _mx_each(m, xs):
    parts = [_split3(x) for x in xs]
    out = None
    for p in range(3):
        d = [jnp.dot(m, s[p], preferred_element_type=F32) for s in parts]
        out = d if out is None else [a + b for a, b in zip(out, d)]
    return out


def _hg_scores(q, kk, ex, mask_ref):
    p = [mask_ref[6] * _rowsum(a * b) for a, b in zip(q, kk)]
    for lvl in range(6):
        el = [e[lvl * CHUNK:(lvl + 1) * CHUNK] for e in ex]
        d = [_dot_nt(a * e, b * e) for a, b, e in zip(q, kk, el)]
        p = [x + mask_ref[lvl] * y for x, y in zip(p, d)]
    return p


def _hgrn_fwd(hq, hf, hi, hg, logits, gain, consts):
    t = hq.shape[0]
    nc = t // CHUNK
    cb = _chunks_per_step(nc)
    rows = cb * CHUNK
    mstack, _, masks, eye = consts
    tile = pl.BlockSpec((rows, HPS * HEAD), lambda c, g: (c, g))

    def body(hq_ref, hf_ref, hi_ref, hg_ref, lg_ref, gain_ref, m_ref, mask_ref, eye_ref,
             oraw_ref, og_ref, ssave_ref, state):
        c = pl.program_id(0)
        g = pl.program_id(1)

        @pl.when(c == 0)
        def _():
            for hh in range(HPS):
                state[g * HPS + hh] = jnp.zeros((HEAD, HEAD), F32)

        lg_all = lg_ref[...]
        gain_v = gain_ref[...]

        def one(i, carry):
            sl = pl.ds(pl.multiple_of(i * CHUNK, CHUNK), CHUNK)
            hs = range(HPS)
            heads = [g * HPS + hh for hh in hs]
            ln = [slice(hh * HEAD, (hh + 1) * HEAD) for hh in hs]
            preps = [_hg_prep(hq_ref[sl, s], hf_ref[sl, s], lg_all[:, s]) for s in ln]
            lf, kk, q = [p[4] for p in preps], [p[5] for p in preps], [p[6] for p in preps]
            v = [hi_ref[sl, s] for s in ln]
            ex = [jnp.exp(x) for x in _mx_each(m_ref[...], lf)]
            eb = [e[6 * CHUNK:7 * CHUNK] for e in ex]
            esfx = [e[7 * CHUNK:8 * CHUNK] for e in ex]
            p = _hg_scores(q, kk, ex, mask_ref)
            s0 = [state[h] for h in heads]
            o = _each(lambda a, e, s, pp, vv: _dot(a * e, s) + _dot(pp, vv), q, eb, s0, p, v)
            eye_v = eye_ref[...]
            s1 = _each(lambda s, e, kx, ef, vv: s * _row_to_col(e[CHUNK - 1:CHUNK, :], eye_v) + _dot_tn(kx * ef, vv),
                       s0, eb, kk, esfx, v)
            for hh in hs:
                ssave_ref[i, hh] = s0[hh]
                state[heads[hh]] = s1[hh]
                oraw_ref[sl, ln[hh]] = o[hh]
                r = lax.rsqrt(jnp.mean(o[hh] * o[hh], axis=1, keepdims=True) + EPS)
                og_ref[sl, ln[hh]] = (o[hh] * r * gain_v * _silu(hg_ref[sl, ln[hh]])).astype(BF16)
            return carry

        lax.fori_loop(0, cb, one, 0)

    return pl.pallas_call(
        body, name="hgrn_fwd", grid=(nc // cb, HG_HEADS // HPS),
        in_specs=[tile, tile, tile, tile, pl.BlockSpec((2, HPS * HEAD), lambda c, g: (0, g)),
                  pl.BlockSpec((1, HEAD), lambda c, g: (0, 0)),
                  pl.BlockSpec(mstack.shape, lambda c, g: (0, 0)),
                  pl.BlockSpec(masks.shape, lambda c, g: (0, 0, 0)),
                  pl.BlockSpec(eye.shape, lambda c, g: (0, 0))],
        out_specs=[tile, tile, pl.BlockSpec((cb, HPS, HEAD, HEAD), lambda c, g: (c, g, 0, 0))],
        out_shape=[jax.ShapeDtypeStruct((t, HG_HEADS * HEAD), F32), jax.ShapeDtypeStruct((t, HG_HEADS * HEAD), BF16),
                   jax.ShapeDtypeStruct((nc, HG_HEADS, HEAD, HEAD), F32)],
        scratch_shapes=[pltpu.VMEM((HG_HEADS, HEAD, HEAD), F32)],
        compiler_params=_params(_ARB, _ARB))(hq, hf, hi, hg, logits, gain, mstack, masks, eye)


def _hgrn_bwd(hq, hf, hi, hg, logits, gain, oraw, ssave, dog, consts):
    t = hq.shape[0]
    nc = t // CHUNK
    cb = _chunks_per_step(nc)
    rows = cb * CHUNK
    nb = nc // cb
    mstack, mstack_t, masks, eye = consts
    tile = pl.BlockSpec((rows, HPS * HEAD), lambda c, g: (nb - 1 - c, g))

    def body(hq_ref, hf_ref, hi_ref, hg_ref, lg_ref, gain_ref, oraw_ref, ssave_ref, dog_ref, m_ref, mt_ref,
             mask_ref, eye_ref, dhq_ref, dhf_ref, dhi_ref, dhg_ref, dgain_ref, dlb_ref, dstate):
        c = pl.program_id(0)
        g = pl.program_id(1)

        @pl.when(c == 0)
        def _():
            for hh in range(HPS):
                dstate[g * HPS + hh] = jnp.zeros((HEAD, HEAD), F32)

        @pl.when((c == 0) & (g == 0))
        def _():
            dgain_ref[...] = jnp.zeros_like(dgain_ref)
            dlb_ref[...] = jnp.zeros_like(dlb_ref)

        lg_all = lg_ref[...]
        gain_v = gain_ref[...]
        eye_v = eye_ref[...]
        last_row = (lax.broadcasted_iota(jnp.int32, (CHUNK, HEAD), 0) == CHUNK - 1).astype(F32)

        def one(j, carry):
            i = cb - 1 - j
            sl = pl.ds(pl.multiple_of(i * CHUNK, CHUNK), CHUNK)
            hs = range(HPS)
            heads = [g * HPS + hh for hh in hs]
            ln = [slice(hh * HEAD, (hh + 1) * HEAD) for hh in hs]
            hqv = [hq_ref[sl, s] for s in ln]
            hgv = [hg_ref[sl, s] for s in ln]
            preps = [_hg_prep(a, hf_ref[sl, s], lg_all[:, s]) for a, s in zip(hqv, ln)]
            lb, sg, sgn, f, lf, kk, q = ([p[n] for p in preps] for n in range(7))
            v = [hi_ref[sl, s] for s in ln]
            ex = [jnp.exp(x) for x in _mx_each(m_ref[...], lf)]
            eb = [e[6 * CHUNK:7 * CHUNK] for e in ex]
            esfx = [e[7 * CHUNK:8 * CHUNK] for e in ex]
            p = _hg_scores(q, kk, ex, mask_ref)
            s0 = [ssave_ref[i, hh] for hh in hs]
            ds = [dstate[h] for h in heads]

            o = [oraw_ref[sl, s] for s in ln]
            r = [lax.rsqrt(jnp.mean(x * x, axis=1, keepdims=True) + EPS) for x in o]
            on = _each(lambda x, y: x * y, o, r)
            dg_out = [dog_ref[sl, s] for s in ln]
            sgate = [_silu(x) for x in hgv]
            for hh in hs:
                dhg_ref[sl, ln[hh]] = (dg_out[hh] * on[hh] * gain_v * _dsilu(hgv[hh])).astype(BF16)
            dgain_ref[...] += sum(jnp.sum(d * s * n, axis=0, keepdims=True) for d, s, n in zip(dg_out, sgate, on))
            don = _each(lambda d, s: d * s * gain_v, dg_out, sgate)
            do = _each(lambda rr, dn, n: rr * (dn - n * jnp.mean(dn * n, axis=1, keepdims=True)), r, don, on)

            dp = _each(_dot_nt, do, v)
            dv = _each(lambda pp, d, kx, ef, s: _dot_tn(pp, d) + _dot(kx * ef, s), p, do, kk, esfx, ds)
            dqb = _each(_dot_nt, do, s0)
            dkx = _each(_dot_nt, v, ds)
            diag = [_rowsum(mask_ref[6] * x) for x in dp]
            dq = _each(lambda a, e, d, kx: a * e + d * kx, dqb, eb, diag, kk)
            dk = _each(lambda a, e, d, qq: a * e + d * qq, dkx, esfx, diag, q)
            dxs = [[] for _ in hs]
            for lvl in range(6):
                el = [e[lvl * CHUNK:(lvl + 1) * CHUNK] for e in ex]
                gm = [mask_ref[lvl] * x for x in dp]
                a1 = _each(lambda m_, kx, e: _dot(m_, kx * e), gm, kk, el)
                a2 = _each(lambda m_, qq, e: _dot_tn(m_, qq * e), gm, q, el)
                dq = _each(lambda x, a, e: x + a * e, dq, a1, el)
                dk = _each(lambda x, a, e: x + a * e, dk, a2, el)
                for hh in hs:
                    dxs[hh].append((a1[hh] * q[hh] + a2[hh] * kk[hh]) * el[hh])
            e_end_row = [e[CHUNK - 1:CHUNK, :] for e in eb]
            ds_new = _each(lambda qq, e, d, er, s: _dot_tn(qq * e, d) + _row_to_col(er, eye_v) * s, q, eb, do, e_end_row, ds)
            for hh in hs:
                dstate[heads[hh]] = ds_new[hh]
                dend_row = _col_to_row(_rowsum(s0[hh] * ds[hh]), eye_v)
                dxs[hh].append(dqb[hh] * q[hh] * eb[hh] + last_row * (e_end_row[hh] * dend_row))
                dxs[hh].append(dkx[hh] * kk[hh] * esfx[hh])
            dlf = _mx_each(mt_ref[...], [jnp.concatenate(x, axis=0) for x in dxs])

            for hh in hs:
                dhi_ref[sl, ln[hh]] = dv[hh].astype(BF16)
                dhq_ref[sl, ln[hh]] = (dq[hh] * (HEAD ** -0.5) * _dsilu(hqv[hh])).astype(BF16)
                df = dlf[hh] / f[hh]
                dsig = (1.0 - lb[hh]) * sg[hh] * sgn[hh]
                dhf_ref[sl, ln[hh]] = ((df - dk[hh]) * dsig).astype(BF16)
                dlb_t = jnp.sum(df * sgn[hh] - dk[hh] * sgn[hh], axis=0, keepdims=True)
                dlb_ref[pl.ds(heads[hh], 1), :] += dlb_t * lb[hh] * (1.0 - lb[hh])
            return carry

        lax.fori_loop(0, cb, one, 0)

    outs = [jax.ShapeDtypeStruct((t, HG_HEADS * HEAD), BF16)] * 4 + [
        jax.ShapeDtypeStruct((1, HEAD), F32), jax.ShapeDtypeStruct((HG_HEADS, HEAD), F32)]
    return pl.pallas_call(
        body, name="hgrn_bwd", grid=(nb, HG_HEADS // HPS),
        in_specs=[tile, tile, tile, tile, pl.BlockSpec((2, HPS * HEAD), lambda c, g: (0, g)),
                  pl.BlockSpec((1, HEAD), lambda c, g: (0, 0)), tile,
                  pl.BlockSpec((cb, HPS, HEAD, HEAD), lambda c, g: (nb - 1 - c, g, 0, 0)), tile,
                  pl.BlockSpec(mstack.shape, lambda c, h: (0, 0)),
                  pl.BlockSpec(mstack_t.shape, lambda c, h: (0, 0)),
                  pl.BlockSpec(masks.shape, lambda c, h: (0, 0, 0)),
                  pl.BlockSpec(eye.shape, lambda c, h: (0, 0))],
        out_specs=[tile, tile, tile, tile, pl.BlockSpec((1, HEAD), lambda c, h: (0, 0)),
                   pl.BlockSpec((HG_HEADS, HEAD), lambda c, h: (0, 0))],
        out_shape=outs, scratch_shapes=[pltpu.VMEM((HG_HEADS, HEAD, HEAD), F32)],
        compiler_params=_params(_ARB, _ARB))(hq, hf, hi, hg, logits, gain, oraw, ssave, dog, mstack, mstack_t,
                                             masks, eye)


CONV_W = 512


def _per_head(fn, *arrs):
    width = arrs[0].shape[1]
    return jnp.concatenate([fn(*[a[:, j:j + HEAD] for a in arrs]) for j in range(0, width, HEAD)], axis=1)


def _shift_down(xv, halo, d, top_rows):
    if d == 0:
        return xv, xv[0:8]
    main = pltpu.roll(xv, d, 0)
    top = jnp.where(top_rows < d, pltpu.roll(halo, d, 0), main[0:8])
    return main, top


def _conv_parts(x_ref, halo_ref, w_ref, first):
    xv = x_ref[...]
    halo = jnp.where(first, 0.0, halo_ref[...])
    top_rows = lax.broadcasted_iota(jnp.int32, (8, xv.shape[1]), 0)
    shifted = [_shift_down(xv, halo, CONV_K - 1 - j, top_rows) for j in range(CONV_K)]
    w = w_ref[...]
    acc = sum(shifted[j][0] * w[j:j + 1, :] for j in range(CONV_K))
    acc_top = sum(shifted[j][1] * w[j:j + 1, :] for j in range(CONV_K))
    return shifted, acc, acc_top


def _conv_fwd(x, w8, l2scale, name):
    t, width = x.shape
    tr = _pick(t, 512, 8)

    def post(cv):
        s = _silu(cv)
        if l2scale is not None:
            s = _per_head(lambda sh: sh * (lax.rsqrt(_rowsum(sh * sh) + EPS) * l2scale), s)
        return s

    def body(x_ref, halo_ref, w_ref, o_ref):
        _, acc, acc_top = _conv_parts(x_ref, halo_ref, w_ref, pl.program_id(1) == 0)
        o_ref[...] = post(acc)
        o_ref[0:8, :] = post(acc_top)

    return pl.pallas_call(
        body, name=name, grid=(width // CONV_W,t // tr),
        in_specs=[pl.BlockSpec((tr, CONV_W), lambda j, i: (i, j)),
                  pl.BlockSpec((8, CONV_W), lambda j, i: (jnp.maximum(i * (tr // 8) - 1, 0), j)),
                  pl.BlockSpec((8, CONV_W), lambda j, i: (0, j))],
        out_specs=pl.BlockSpec((tr, CONV_W), lambda j, i: (i, j)),
        out_shape=jax.ShapeDtypeStruct((t, width), F32), compiler_params=_params(_PAR, _PAR))(x, x, w8)


def _conv_bwd_a(x, w8, dy, l2scale, name):
    t, width = x.shape
    tr = _pick(t, 512, 8)

    def l2_bwd(s, dyh):
        r = lax.rsqrt(_rowsum(s * s) + EPS)
        y0 = s * r
        dy0 = dyh * l2scale
        return r * (dy0 - y0 * _rowsum(dy0 * y0))

    def to_dc(cv, dyv):
        if l2scale is not None:
            dyv = _per_head(l2_bwd, _silu(cv), dyv)
        return dyv * _dsilu(cv)

    def body(x_ref, halo_ref, w_ref, dy_ref, dc_ref, dw_ref):
        @pl.when(pl.program_id(1) == 0)
        def _():
            dw_ref[...] = jnp.zeros_like(dw_ref)

        shifted, acc, acc_top = _conv_parts(x_ref, halo_ref, w_ref, pl.program_id(1) == 0)
        dyv = dy_ref[...]
        dc = to_dc(acc, dyv)
        dc_top = to_dc(acc_top, dyv[0:8])
        dc_ref[...] = dc
        dc_ref[0:8, :] = dc_top
        rest = (lax.broadcasted_iota(jnp.int32, dc.shape, 0) >= 8).astype(F32)
        dc_rest = dc * rest
        for j in range(CONV_K):
            dw_ref[j:j + 1, :] += (jnp.sum(dc_rest * shifted[j][0], axis=0, keepdims=True)
                                   + jnp.sum(dc_top * shifted[j][1], axis=0, keepdims=True))

    return pl.pallas_call(
        body, name=name, grid=(width // CONV_W,t // tr),
        in_specs=[pl.BlockSpec((tr, CONV_W), lambda j, i: (i, j)),
                  pl.BlockSpec((8, CONV_W), lambda j, i: (jnp.maximum(i * (tr // 8) - 1, 0), j)),
                  pl.BlockSpec((8, CONV_W), lambda j, i: (0, j)),
                  pl.BlockSpec((tr, CONV_W), lambda j, i: (i, j))],
        out_specs=[pl.BlockSpec((tr, CONV_W), lambda j, i: (i, j)), pl.BlockSpec((8, CONV_W), lambda j, i: (0, j))],
        out_shape=[jax.ShapeDtypeStruct((t, width), F32), jax.ShapeDtypeStruct((8, width), F32)],
        compiler_params=_params(_PAR, _ARB))(x, x, w8, dy)


def _conv_bwd_b(dc, w8, name):
    t, width = dc.shape
    tr = _pick(t, 512, 8)
    nt = t // tr

    def body(dc_ref, halo_ref, w_ref, dx_ref):
        dcv = dc_ref[...]
        halo = jnp.where(pl.program_id(1) == nt - 1, 0.0, halo_ref[...])
        w = w_ref[...]
        bot_rows = lax.broadcasted_iota(jnp.int32, (8, CONV_W), 0)
        acc = dcv * w[CONV_K - 1:CONV_K, :]
        acc_bot = dcv[tr - 8:tr] * w[CONV_K - 1:CONV_K, :]
        for d in range(1, CONV_K):
            main = pltpu.roll(dcv, tr - d, 0)
            bot = jnp.where(bot_rows >= 8 - d, pltpu.roll(halo, 8 - d, 0), main[tr - 8:tr])
            wj = w[CONV_K - 1 - d:CONV_K - d, :]
            acc = acc + main * wj
            acc_bot = acc_bot + bot * wj
        dx_ref[...] = acc.astype(BF16)
        dx_ref[tr - 16:tr, :] = jnp.concatenate([acc[tr - 16:tr - 8], acc_bot], axis=0).astype(BF16)

    return pl.pallas_call(
        body, name=name, grid=(width // CONV_W,nt),
        in_specs=[pl.BlockSpec((tr, CONV_W), lambda j, i: (i, j)),
                  pl.BlockSpec((8, CONV_W), lambda j, i: (jnp.minimum((i + 1) * (tr // 8), t // 8 - 1), j)),
                  pl.BlockSpec((8, CONV_W), lambda j, i: (0, j))],
        out_specs=pl.BlockSpec((tr, CONV_W), lambda j, i: (i, j)),
        out_shape=jax.ShapeDtypeStruct((t, width), BF16), compiler_params=_params(_PAR, _PAR))(dc, dc, w8)


def _each(f, *lists):
    return [f(*xs) for xs in zip(*lists)]


def _hp_each(a_list, b_list):
    ah = [_bf(a) for a in a_list]
    bh = [_bf(b) for b in b_list]
    al = [_bf(a - h.astype(F32)) for a, h in zip(a_list, ah)]
    bl = [_bf(b - h.astype(F32)) for b, h in zip(b_list, bh)]
    d1 = [jnp.dot(x, y, preferred_element_type=F32) for x, y in zip(ah, bh)]
    d2 = [jnp.dot(x, y, preferred_element_type=F32) for x, y in zip(ah, bl)]
    d3 = [jnp.dot(x, y, preferred_element_type=F32) for x, y in zip(al, bh)]
    return [x + y + z for x, y, z in zip(d1, d2, d3)]


def _tri_inv_each(a_list, eye):
    ns = [-a for a in a_list]
    ps = [eye + n for n in ns]
    for _ in range(5):
        ns = _hp_each(ns, ns)
        ps = [p + d for p, d in zip(ps, _hp_each(ps, ns))]
    return ps


def _gd_gates(gab, alog, dtb):
    sp_arg = gab + dtb
    return sp_arg, -jnp.exp(alog) * _softplus(sp_arg), _sigmoid(gab)


def _gd_chunks(q, k, v, g_all, beta_all, selg, selb, l_ref, mask_ref):
    incl, strict, eye, upper = mask_ref[0], mask_ref[1], mask_ref[2], mask_ref[3]
    lmat = l_ref[...]
    gb = [_dot_xm(g_all, s) for s in selg]
    bb = [_dot_xm(beta_all, s) for s in selb]
    gam = _mx_each(lmat, gb)
    gam_row = [jnp.sum(x[:, :CHUNK] * upper, axis=0, keepdims=True) for x in gb]
    lm = _each(lambda gm, gr: incl * jnp.exp(jnp.minimum(gm[:, :CHUNK] - gr, 0.0)), gam, gam_row)
    kb = _each(lambda x, b: x * b, k, bb)
    a = _each(lambda x, y, m: strict * _dot_nt(x, y) * m, kb, k, lm)
    tm = _tri_inv_each(a, eye)
    eg = [jnp.exp(x) for x in gam]
    vb = _each(lambda x, b: x * b, v, bb)
    kbg = _each(lambda x, e: x * e, kb, eg)
    u = _each(_dot, tm, vb)
    w = _each(_dot, tm, kbg)
    qk = _each(lambda x, y, m: _dot_nt(x, y) * m, q, k, lm)
    g_end = [x[CHUNK - 1:CHUNK, :] for x in gam]
    ekg = _each(lambda e, x: jnp.exp(e - x), g_end, gam)
    ge = [jnp.exp(e) for e in g_end]
    kg = _each(lambda x, e: x * e, k, ekg)
    qg = _each(lambda x, e: x * e, q, eg)
    names = ("bb", "lm", "kb", "a", "tm", "eg", "vb", "kbg", "u", "w", "qk", "ekg", "ge", "kg", "qg")
    cols = (bb, lm, kb, a, tm, eg, vb, kbg, u, w, qk, ekg, ge, kg, qg)
    return [dict(zip(names, vals)) for vals in zip(*cols)]


def _gd_specs(rows, rev_nb=None):
    def cidx(c):
        return c if rev_nb is None else rev_nb - 1 - c

    qk_tile = pl.BlockSpec((rows, HPS // 2 * HEAD), lambda c, g: (cidx(c), g))
    v_tile = pl.BlockSpec((rows, HPS * HEAD), lambda c, g: (cidx(c), g))
    gab_tile = pl.BlockSpec((rows, HEAD), lambda c, g: (cidx(c), 0))
    return qk_tile, v_tile, gab_tile


def _gdn_fwd(qn, kn, cv, gab, gz, alog, dtb, gain, consts):
    t = qn.shape[0]
    nc = t // CHUNK
    cb = _chunks_per_step(nc)
    rows = cb * CHUNK
    lmat, _, masks, sel = consts
    qk_tile, v_tile, gab_tile = _gd_specs(rows)
    row128 = pl.BlockSpec((1, HEAD), lambda c, h: (0, 0))

    def body(q_ref, k_ref, v_ref, gab_ref, gz_ref, alog_ref, dtb_ref, gain_ref, sel_ref, l_ref, mask_ref,
             oraw_ref, og_ref, ssave_ref, state):
        c = pl.program_id(0)
        g = pl.program_id(1)

        @pl.when(c == 0)
        def _():
            for hh in range(HPS):
                state[g * HPS + hh] = jnp.zeros((HEAD, HEAD), F32)

        alog = alog_ref[...]
        dtb = dtb_ref[...]
        gain_v = gain_ref[...]

        def one(i, carry):
            sl = pl.ds(pl.multiple_of(i * CHUNK, CHUNK), CHUNK)
            _, g_all, beta_all = _gd_gates(gab_ref[sl, :], alog, dtb)
            heads = [g * HPS + hh for hh in range(HPS)]
            lq = [slice(hh // 2 * HEAD, (hh // 2 + 1) * HEAD) for hh in range(HPS)]
            lv = [slice(hh * HEAD, (hh + 1) * HEAD) for hh in range(HPS)]
            chs = _gd_chunks([q_ref[sl, s] for s in lq], [k_ref[sl, s] for s in lq], [v_ref[sl, s] for s in lv],
                             g_all, beta_all, [sel_ref[h] for h in heads], [sel_ref[GD_HEADS + h] for h in heads],
                             l_ref, mask_ref)
            s0 = [state[h] for h in heads]
            v_new = _each(lambda ch, s: ch["u"] - _dot(ch["w"], s), chs, s0)
            o = _each(lambda ch, s, vn: _dot(ch["qg"], s) + _dot(ch["qk"], vn), chs, s0, v_new)
            s1 = _each(lambda ch, s, vn: s * ch["ge"] + _dot_tn(ch["kg"], vn), chs, s0, v_new)
            for hh in range(HPS):
                ssave_ref[i, hh] = s0[hh]
                state[heads[hh]] = s1[hh]
                oraw_ref[sl, lv[hh]] = o[hh]
                r = lax.rsqrt(jnp.mean(o[hh] * o[hh], axis=1, keepdims=True) + EPS)
                og_ref[sl, lv[hh]] = (o[hh] * r * gain_v * _silu(gz_ref[sl, lv[hh]])).astype(BF16)
            return carry

        lax.fori_loop(0, cb, one, 0)

    return pl.pallas_call(
        body, name="gdn_fwd", grid=(nc // cb, GD_HEADS // HPS),
        in_specs=[qk_tile, qk_tile, v_tile, gab_tile, v_tile, row128, row128, row128,
                  pl.BlockSpec(sel.shape, lambda c, g: (0, 0, 0)),
                  pl.BlockSpec(lmat.shape, lambda c, g: (0, 0)),
                  pl.BlockSpec(masks.shape, lambda c, g: (0, 0, 0))],
        out_specs=[v_tile, v_tile, pl.BlockSpec((cb, HPS, HEAD, HEAD), lambda c, g: (c, g, 0, 0))],
        out_shape=[jax.ShapeDtypeStruct((t, GD_HEADS * HEAD), F32), jax.ShapeDtypeStruct((t, GD_HEADS * HEAD), BF16),
                   jax.ShapeDtypeStruct((nc, GD_HEADS, HEAD, HEAD), F32)],
        scratch_shapes=[pltpu.VMEM((GD_HEADS, HEAD, HEAD), F32)],
        compiler_params=_params(_ARB, _ARB))(qn, kn, cv, gab, gz, alog, dtb, gain, sel, lmat, masks)


def _gdn_bwd(qn, kn, cv, gab, gz, alog, dtb, gain, oraw, ssave, dog, consts):
    t = qn.shape[0]
    nc = t // CHUNK
    cb = _chunks_per_step(nc)
    rows = cb * CHUNK
    nb = nc // cb
    lmat, lmat_t, masks, sel = consts
    qk_tile, v_tile, gab_tile = _gd_specs(rows, nb)
    row128 = pl.BlockSpec((1, HEAD), lambda c, h: (0, 0))

    def body(q_ref, k_ref, v_ref, gab_ref, gz_ref, alog_ref, dtb_ref, gain_ref, oraw_ref, ssave_ref, dog_ref,
             sel_ref, l_ref, lt_ref, mask_ref,
             dq_ref, dk_ref, dv_ref, dgab_ref, dgz_ref, small_ref, dstate):
        c = pl.program_id(0)
        g = pl.program_id(1)

        @pl.when(c == 0)
        def _():
            for hh in range(HPS):
                dstate[g * HPS + hh] = jnp.zeros((HEAD, HEAD), F32)

        @pl.when((c == 0) & (g == 0))
        def _():
            small_ref[...] = jnp.zeros_like(small_ref)

        alog = alog_ref[...]
        dtb = dtb_ref[...]
        gain_v = gain_ref[...]
        lane = lax.broadcasted_iota(jnp.int32, (1, HEAD), 1)
        last_row = (lax.broadcasted_iota(jnp.int32, (CHUNK, HEAD), 0) == CHUNK - 1).astype(F32)

        def one(j, carry):
            i = cb - 1 - j
            sl = pl.ds(pl.multiple_of(i * CHUNK, CHUNK), CHUNK)
            sp_arg, g_all, beta_all = _gd_gates(gab_ref[sl, :], alog, dtb)
            strict, eye = mask_ref[1], mask_ref[2]
            ltm = lt_ref[...]
            hs = range(HPS)
            heads = [g * HPS + hh for hh in hs]
            lq = [slice(hh // 2 * HEAD, (hh // 2 + 1) * HEAD) for hh in hs]
            lv = [slice(hh * HEAD, (hh + 1) * HEAD) for hh in hs]
            q = [q_ref[sl, s] for s in lq]
            k = [k_ref[sl, s] for s in lq]
            v = [v_ref[sl, s] for s in lv]
            gzv = [gz_ref[sl, s] for s in lv]
            chs = _gd_chunks(q, k, v, g_all, beta_all, [sel_ref[h] for h in heads],
                             [sel_ref[GD_HEADS + h] for h in heads], l_ref, mask_ref)

            def col(name):
                return [ch[name] for ch in chs]

            def mul(x, y):
                return x * y

            tm, lm, eg, bb = col("tm"), col("lm"), col("eg"), col("bb")
            s0 = [ssave_ref[i, hh] for hh in hs]
            ds = [dstate[h] for h in heads]
            v_new = _each(lambda u, w, s: u - _dot(w, s), col("u"), col("w"), s0)

            o = [oraw_ref[sl, s] for s in lv]
            r = [lax.rsqrt(jnp.mean(x * x, axis=1, keepdims=True) + EPS) for x in o]
            on = _each(mul, o, r)
            dg_out = [dog_ref[sl, s] for s in lv]
            sgate = [_silu(x) for x in gzv]
            for hh in hs:
                dgz_ref[sl, lv[hh]] = (dg_out[hh] * on[hh] * gain_v * _dsilu(gzv[hh])).astype(BF16)
            small_ref[0:1, :] += sum(jnp.sum(d * s * n, axis=0, keepdims=True) for d, s, n in zip(dg_out, sgate, on))
            don = _each(lambda d, s: d * s * gain_v, dg_out, sgate)
            do = _each(lambda rr, dn, n: rr * (dn - n * jnp.mean(dn * n, axis=1, keepdims=True)), r, don, on)

            dv_new = _each(lambda a, d, b, s: _dot_tn(a, d) + _dot(b, s), col("qk"), do, col("kg"), ds)
            dqk = _each(_dot_nt, do, v_new)
            dqg = _each(_dot_nt, do, s0)
            dkg = _each(_dot_nt, v_new, ds)
            dge = _each(lambda s, d: jnp.sum(_rowsum(s * d), axis=0, keepdims=True), s0, ds)
            dw = _each(lambda d, s: -_dot_nt(d, s), dv_new, s0)
            ds_new = _each(lambda qg, d, ge, s, w, dv: _dot_tn(qg, d) + ge * s - _dot_tn(w, dv),
                           col("qg"), do, col("ge"), ds, col("w"), dv_new)
            for hh in hs:
                dstate[heads[hh]] = ds_new[hh]

            dvb = _each(_dot_tn, tm, dv_new)
            dkbg = _each(_dot_tn, tm, dw)
            dtm = _each(lambda dv, vb, d, kbg: _dot_nt(dv, vb) + _dot_nt(d, kbg), dv_new, col("vb"), dw, col("kbg"))
            dtt = _each(_dot_nt, dtm, tm)
            da = _each(lambda t_, x: -_dot_tn(t_, x) * strict, tm, dtt)
            dal = _each(mul, da, lm)
            dkb = _each(lambda x, kk, y, e: _dot(x, kk) + y * e, dal, k, dkbg, eg)
            dqk_l = _each(mul, dqk, lm)
            dq = _each(lambda x, kk, y, e: _dot(x, kk) + y * e, dqk_l, k, dqg, eg)
            dk = _each(lambda x, kb, y, qq, z, ekg, w_, b: _dot_tn(x, kb) + _dot_tn(y, qq) + z * ekg + w_ * b,
                       dal, col("kb"), dqk_l, q, dkg, col("ekg"), dkb, bb)
            gmat = _each(lambda x, a, y, qk: x * a + y * qk, da, col("a"), dqk, col("qk"))
            t_kg = _each(lambda x, y: _rowsum(x * y), dkg, col("kg"))
            dgam = _each(lambda gm, x, qg, t_, y, kbg: (_rowsum(gm) - _row_to_col(jnp.sum(gm, axis=0, keepdims=True), eye)
                                                        + _rowsum(x * qg) - t_ + _rowsum(y * kbg)),
                         gmat, dqg, col("qg"), t_kg, dkbg, col("kbg"))
            dg_end = _each(lambda t_, e, ge: jnp.sum(t_, axis=0, keepdims=True) + e * ge[:, 0:1], t_kg, dge, col("ge"))
            dgam = _each(lambda x, e: x + last_row * e, dgam, dg_end)
            dbeta = _each(lambda x, kk, y, vv: _rowsum(x * kk) + _rowsum(y * vv), dkb, k, dvb, v)
            dg = _mx_each(ltm, dgam)

            for hh in hs:
                dv_ref[sl, lv[hh]] = dvb[hh] * bb[hh]
            fac_g = -jnp.exp(alog) * _sigmoid(sp_arg)
            fac_b = beta_all * (1.0 - beta_all)
            hot_g = [(lane == h).astype(F32) for h in heads]
            hot_b = [(lane == GD_HEADS + h).astype(F32) for h in heads]
            dga = _each(lambda x, hot: x * hot * fac_g, dg, hot_g)
            dgb = _each(lambda x, hot: x * hot * fac_b, dbeta, hot_b)
            small_ref[1:2, :] += sum(jnp.sum(x, axis=0, keepdims=True) for x in dga)
            small_ref[2:3, :] += sum(jnp.sum(x * hot * g_all, axis=0, keepdims=True) for x, hot in zip(dg, hot_g))
            for pair in range(HPS // 2):
                lqp = slice(pair * HEAD, (pair + 1) * HEAD)
                dq_ref[sl, lqp] = dq[2 * pair] + dq[2 * pair + 1]
                dk_ref[sl, lqp] = dk[2 * pair] + dk[2 * pair + 1]
            dgab_ref[sl, :] = sum(a + b for a, b in zip(dga, dgb))
            return carry

        lax.fori_loop(0, cb, one, 0)

    groups = GD_HEADS // HPS
    outs = [jax.ShapeDtypeStruct((t, 1024), F32), jax.ShapeDtypeStruct((t, 1024), F32),
            jax.ShapeDtypeStruct((t, 2048), F32), jax.ShapeDtypeStruct((t, groups * HEAD), F32),
            jax.ShapeDtypeStruct((t, 2048), BF16), jax.ShapeDtypeStruct((8, HEAD), F32)]
    return pl.pallas_call(
        body, name="gdn_bwd", grid=(nb, groups),
        in_specs=[qk_tile, qk_tile, v_tile, gab_tile, v_tile, row128, row128, row128, v_tile,
                  pl.BlockSpec((cb, HPS, HEAD, HEAD), lambda c, g: (nb - 1 - c, g, 0, 0)), v_tile,
                  pl.BlockSpec(sel.shape, lambda c, g: (0, 0, 0)),
                  pl.BlockSpec(lmat.shape, lambda c, g: (0, 0)),
                  pl.BlockSpec(lmat_t.shape, lambda c, g: (0, 0)),
                  pl.BlockSpec(masks.shape, lambda c, g: (0, 0, 0))],
        out_specs=[qk_tile, qk_tile, v_tile, pl.BlockSpec((rows, HEAD), lambda c, g: (nb - 1 - c, g)), v_tile,
                   pl.BlockSpec((8, HEAD), lambda c, g: (0, 0))],
        out_shape=outs, scratch_shapes=[pltpu.VMEM((GD_HEADS, HEAD, HEAD), F32)],
        compiler_params=_params(_ARB, _ARB))(qn, kn, cv, gab, gz, alog, dtb, gain, oraw, ssave, dog, sel,
                                             lmat, lmat_t, masks)


def _fold_groups(wide):
    t, width = wide.shape
    tr = _pick(t, 512, 8)

    def body(w_ref, o_ref):
        acc = w_ref[:, 0:HEAD]
        for j in range(1, width // HEAD):
            acc = acc + w_ref[:, j * HEAD:(j + 1) * HEAD]
        o_ref[...] = acc.astype(BF16)

    return pl.pallas_call(
        body, name="fold_gate_grads", grid=(t // tr,), in_specs=[_row_spec(tr, width)], out_specs=_row_spec(tr, HEAD),
        out_shape=jax.ShapeDtypeStruct((t, HEAD), BF16), compiler_params=_params(_PAR))(wide)


def _adam_math(w, g, m, v):
    m2 = ADAM_B1 * m + (1.0 - ADAM_B1) * g
    v2 = ADAM_B2 * v + (1.0 - ADAM_B2) * (g * g)
    m_hat = m2 / (1.0 - ADAM_B1 ** ADAM_STEP)
    v_hat = v2 / (1.0 - ADAM_B2 ** ADAM_STEP)
    delta = -ADAM_LR * (m_hat / (jnp.sqrt(v_hat) + ADAM_EPS) + ADAM_WD * w)
    return delta, m2, v2


def _adamw(w, g, m, v, name):
    r, c = w.shape
    tr = r
    for cand in range(8, r + 1, 8):
        if r % cand == 0 and cand * c * 4 <= (1 << 20):
            tr = cand
    if r % 8 != 0:
        tr = r

    def body(w_ref, g_ref, m_ref, v_ref, d_ref, m2_ref, v2_ref):
        d, m2, v2 = _adam_math(w_ref[...], g_ref[...], m_ref[...], v_ref[...])
        d_ref[...] = d
        m2_ref[...] = m2
        v2_ref[...] = v2

    spec = pl.BlockSpec((tr, c), lambda i: (i, 0))
    return pl.pallas_call(
        body, name=name, grid=(r // tr,), in_specs=[spec] * 4, out_specs=[spec] * 3,
        out_shape=[jax.ShapeDtypeStruct((r, c), F32)] * 3, compiler_params=_params(_PAR))(w, g, m, v)


_ANY = pl.BlockSpec(memory_space=pl.ANY)


def _place():
    return lax.axis_index("x"), lax.axis_index("y"), lax.axis_index("c")


def _gather_weights(packed):
    rows = packed.shape[0]
    half = rows // 2
    nch = COMM_CHUNKS
    ch = half // nch
    assert ch * nch == half and ch % 16 == 0

    def body(p_ref, g_ref, send_sems, recv_sems):
        x, y, c = _place()
        sibling = (x, y, 1 - c)
        chips = [(1 - x, y), (x, 1 - y), (1 - x, 1 - y)]

        def rows_of(pc, q):
            return pl.ds(pl.multiple_of(pc * half + q * ch, 16), ch)

        def piece(px, py, pc, q):
            return g_ref.at[2 * px + py, rows_of(pc, q), :]

        def copy(k, src, dst, to):
            return pltpu.make_async_remote_copy(src_ref=src, dst_ref=dst, send_sem=send_sems.at[k],
                                                recv_sem=recv_sems.at[k], device_id=to, device_id_type=MESH)

        first =[[copy(j * nch + q, p_ref.at[rows_of(c, q), :], piece(x, y, c, q), (*chip, c)) for q in range(nch)]
                 for j, chip in enumerate(chips)]
        for q in range(nch):
            for j in range(3):
                first[j][q].start()
        passed = [[copy((3 + j) * nch + q, piece(*chip, c, q), piece(*chip, c, q), sibling) for q in range(nch)]
                  for j, chip in enumerate(chips)]
        for q in range(nch):
            for j, chip in enumerate(chips):
                copy(j * nch + q, p_ref.at[rows_of(c, q), :], piece(*chip, c, q), (*chip, c)).wait_recv()
                passed[j][q].start()
        for q in range(nch):
            for j, chip in enumerate(chips):
                copy((3 + j) * nch + q, piece(*chip, 1 - c, q), piece(*chip, 1 - c, q), sibling).wait_recv()
        for j in range(3):
            for q in range(nch):
                first[j][q].wait_send()
                passed[j][q].wait_send()

    return pl.pallas_call(
        body, name="gather_weights", out_shape=jax.ShapeDtypeStruct((4, rows, 1024), packed.dtype),
        in_specs=[_ANY], out_specs=_ANY,
        scratch_shapes=[pltpu.SemaphoreType.DMA((6 * nch,)), pltpu.SemaphoreType.DMA((6 * nch,))])(packed)


def _reduce_pair(give):
    half = give.shape[1]
    nch = COMM_CHUNKS
    ch = half // nch
    assert ch * nch == half and ch % 16 == 0

    def body(g_ref, got_ref, send_sems, recv_sems):
        x, y, c = _place()
        copies = [pltpu.make_async_remote_copy(
            src_ref=g_ref.at[s, pl.ds(q * ch, ch), :], dst_ref=got_ref.at[s, pl.ds(q * ch, ch), :],
            send_sem=send_sems.at[s * nch + q], recv_sem=recv_sems.at[s * nch + q],
            device_id=(x, y, 1 - c), device_id_type=MESH) for s in range(4) for q in range(nch)]
        for cp in copies:
            cp.start()
        for cp in copies:
            cp.wait()

    return pl.pallas_call(
        body, name="reduce_pair", out_shape=jax.ShapeDtypeStruct(give.shape, give.dtype), in_specs=[_ANY],
        out_specs=_ANY,
        scratch_shapes=[pltpu.SemaphoreType.DMA((4 * nch,)), pltpu.SemaphoreType.DMA((4 * nch,))])(give)


def _add2(a, b):
    n, rows, w = a.shape
    tr = _pick(rows, 512, 16)
    if rows % tr:
        tr = 16 * max(d for d in range(1, 33) if (rows // 16) % d == 0)
    spec = pl.BlockSpec((1, tr, w), lambda i, j: (i, j, 0))

    def body(a_ref, b_ref, o_ref):
        o_ref[...] = (a_ref[...].astype(F32) + b_ref[...].astype(F32)).astype(BF16)

    return pl.pallas_call(
        body, name="add_pair", grid=(n, rows // tr), in_specs=[spec, spec], out_specs=spec,
        out_shape=jax.ShapeDtypeStruct(a.shape, BF16), compiler_params=_params(_PAR, _PAR))(a, b)


def _reduce_chips(partial):
    half = partial.shape[1]
    nch = COMM_CHUNKS
    ch = half // nch
    assert ch * nch == half and ch % 16 == 0

    def body(p_ref, got_ref, send_sems, recv_sems):
        x, y, c = _place()
        chips = [(1 - x, y), (x, 1 - y), (1 - x, 1 - y)]
        copies = [pltpu.make_async_remote_copy(
            src_ref=p_ref.at[2 * px + py, pl.ds(q * ch, ch), :], dst_ref=got_ref.at[j, pl.ds(q * ch, ch), :],
            send_sem=send_sems.at[j * nch + q], recv_sem=recv_sems.at[j * nch + q],
            device_id=(px, py, c), device_id_type=MESH) for q in range(nch) for j, (px, py) in enumerate(chips)]
        for cp in copies:
            cp.start()
        for cp in copies:
            cp.wait()

    return pl.pallas_call(
        body, name="reduce_chips", out_shape=jax.ShapeDtypeStruct((3, half, 1024), partial.dtype),
        in_specs=[_ANY], out_specs=_ANY,
        scratch_shapes=[pltpu.SemaphoreType.DMA((3 * nch,)), pltpu.SemaphoreType.DMA((3 * nch,))])(partial)


def _add4(own, got):
    rows, w = own.shape
    tr = _pick(rows, 512, 16)
    if rows % tr:
        tr = 16 * max(d for d in range(1, 33) if (rows // 16) % d == 0)

    def body(a_ref, b_ref, o_ref):
        o_ref[...] = ((a_ref[...].astype(F32) + b_ref[0].astype(F32)) + b_ref[1].astype(F32)) + b_ref[2].astype(F32)

    return pl.pallas_call(
        body, name="add_chips", grid=(rows // tr,),
        in_specs=[pl.BlockSpec((tr, w), lambda i: (i, 0)), pl.BlockSpec((3, tr, w), lambda i: (0, i, 0))],
        out_specs=pl.BlockSpec((tr, w), lambda i: (i, 0)), out_shape=jax.ShapeDtypeStruct((rows, w), F32),
        compiler_params=_params(_PAR))(own, got)


def _share_pair(red):
    half = red.shape[0]
    nch = COMM_CHUNKS
    ch = half // nch
    assert ch * nch == half and ch % 8 == 0

    def body(r_ref, got_ref, send_sems, recv_sems):
        x, y, c = _place()
        copies = [pltpu.make_async_remote_copy(
            src_ref=r_ref.at[pl.ds(q * ch, ch), :], dst_ref=got_ref.at[pl.ds(q * ch, ch), :],
            send_sem=send_sems.at[q], recv_sem=recv_sems.at[q], device_id=(x, y, 1 - c), device_id_type=MESH)
            for q in range(nch)]
        for cp in copies:
            cp.start()
        for cp in copies:
            cp.wait()

    return pl.pallas_call(
        body, name="share_pair", out_shape=jax.ShapeDtypeStruct(red.shape, red.dtype),
        in_specs=[_ANY], out_specs=_ANY,
        scratch_shapes=[pltpu.SemaphoreType.DMA((nch,)), pltpu.SemaphoreType.DMA((nch,))])(red)


def _small_sync(gs, ws, ms, vs):
    rows = gs.shape[0]
    vmem = pl.BlockSpec(memory_space=pltpu.VMEM)

    def body(g_ref, w_ref, m_ref, v_ref, sum_ref, d_ref, m2_ref, v2_ref, buf, send_sems, recv_sems):
        x, y, c = _place()
        me = 4 * x + 2 * y + c
        buf[me] = g_ref[...]
        copies = []
        for k in range(1, 8):
            peer = (x ^ (k >> 2), y ^ ((k >> 1) & 1), c ^ (k & 1))
            copies.append(pltpu.make_async_remote_copy(
                src_ref=g_ref, dst_ref=buf.at[me], send_sem=send_sems.at[k - 1], recv_sem=recv_sems.at[k - 1],
                device_id=peer, device_id_type=MESH))
        for cp in copies:
            cp.start()
        for cp in copies:
            cp.wait()
        total = buf[0]
        for i in range(1, 8):
            total = total + buf[i]
        sum_ref[...] = total
        d, m2, v2 = _adam_math(w_ref[...], total, m_ref[...], v_ref[...])
        d_ref[...] = d
        m2_ref[...] = m2
        v2_ref[...] = v2

    shape = jax.ShapeDtypeStruct((rows, 128), F32)
    return pl.pallas_call(
        body, name="small_sync", out_shape=[shape] * 4, in_specs=[vmem] * 4, out_specs=[vmem] * 4,
        scratch_shapes=[pltpu.VMEM((8, rows, 128), F32), pltpu.SemaphoreType.DMA((7,)),
                        pltpu.SemaphoreType.DMA((7,))])(gs, ws, ms, vs)


_BIG = (("ffn1_w_in", 1408, 1408), ("ffn1_w_out", 704, 704), ("w_in", 3080, 3088), ("gdn_conv_w", 4, 16),
        ("w_branch_hgrn", 256, 256), ("w_branch_gdn", 512, 512), ("w_out", 256, 256),
        ("ffn2_w_in", 1408, 1408), ("ffn2_w_out", 704, 704))
_BIG_ROWS = sum(p for _, _, p in _BIG)
_COL_SHARDED = ("ffn1_w_in", "w_in", "gdn_conv_w", "ffn2_w_in")


def _pack_rows(parts, lead):
    out = []
    for name, rows, padded in _BIG:
        p = parts[name]
        if padded != rows:
            p = jnp.concatenate([p, jnp.zeros(lead + (padded - rows, 1024), p.dtype)], axis=len(lead))
        out.append(p)
    return jnp.concatenate(out, axis=len(lead))


def _unpack_rows(packed):
    out, off = {}, 0
    for name, rows, padded in _BIG:
        out[name] = packed[..., off:off + rows, :]
        off += padded
    return out


def _full_from_shards(name, g):
    if name in _COL_SHARDED:
        r = {"gdn_conv_w": CONV_K}.get(name, D_MODEL)
        return jnp.transpose(g.reshape(4, r, -1), (1, 0, 2)).reshape(r, -1)
    return g.reshape(-1, 1024)


def _shards_from_full(name, full):
    if name in _COL_SHARDED:
        r = full.shape[0]
        return jnp.transpose(full.reshape(r, 4, -1), (1, 0, 2)).reshape(4, -1, 1024)
    return full.reshape(4, -1, 1024)


_SMALL = (("ffn1_norm", 8), ("mix_norm", 8), ("hgrn_lb_logits", 16), ("hgrn_out_norm", 1), ("gdn_a_log", 1),
          ("gdn_dt_bias", 1), ("gdn_out_norm", 1), ("ffn2_norm", 8), ("final_norm", 8), ("loss", 1))
_SMALL_ROWS = 56


def _pack_small(parts):
    out = []
    for name, rows in _SMALL:
        p = parts[name].reshape(-1).astype(F32)
        p = jnp.concatenate([p, jnp.zeros((rows * 128 - p.shape[0],), F32)]) if p.shape[0] != rows * 128 else p
        out.append(p.reshape(rows, 128))
    used = sum(r for _, r in _SMALL)
    out.append(jnp.zeros((_SMALL_ROWS - used, 128), F32))
    return jnp.concatenate(out, axis=0)


def _unpack_small(packed, shapes):
    out, off = {}, 0
    for name, rows in _SMALL:
        n = int(np.prod(shapes[name]))
        out[name] = packed[off:off + rows].reshape(-1)[:n].reshape(shapes[name])
        off += rows
    return out


def _ffn_fwd(x, gain, w_in, w_out, tag):
    n = _rmsnorm_fwd(x, gain, tag + "_norm")
    ab = _mm(n, w_in, out_dtype=BF16, name=tag + "_in")
    hm = _swiglu_fwd(ab, tag + "_act")
    out = _mm(hm, w_out, alpha=0.5, res=x, name=tag + "_out")
    return out, (n, ab)


def _ffn_bwd(x, gain, w_in, w_out, saved, dout, dout_bf, tag):
    n, ab = saved
    dhm = _mm(dout_bf, w_out, tb=True, alpha=0.5, name=tag + "_dact")
    dab, hm = _swiglu_bwd(ab, dhm, tag + "_dswiglu")
    dw_out = _mm(hm, dout_bf, ta=True, alpha=0.5, name=tag + "_dwout")
    dw_in = _mm(n, dab, ta=True, name=tag + "_dwin")
    dn = _mm(dab, w_in, tb=True, name=tag + "_dnorm")
    dx, dx_bf, dgain = _rmsnorm_bwd(x, gain, dn, dout, tag + "_dx")
    return dx, dx_bf, dgain, dw_in, dw_out


def _pad_lanes(v):
    return jnp.concatenate([v.reshape(1, -1), jnp.zeros((1, HEAD - v.size), F32)], axis=1)


def _local_step(x, tgt, w, small):
    hg_c = _hg_consts()
    gd_c = _gd_consts()
    seg, off = {}, 0
    for name, size in zip(IN_NAMES, IN_SIZES):
        seg[name] = w["w_in"][:, off:off + size]
        off += size
    w_gab = jnp.concatenate([seg["ga"], seg["gb"], jnp.zeros((D_MODEL, HEAD - 32), BF16)], axis=1)
    big_segs = [n for n in IN_NAMES if n not in ("ga", "gb")]
    conv8 = jnp.concatenate([w["gdn_conv_w"].astype(F32), jnp.zeros((8 - CONV_K, 4096), F32)], axis=0)
    conv_q, conv_k, conv_v = conv8[:, :1024], conv8[:, 1024:2048], conv8[:, 2048:]
    alog = _pad_lanes(small["gdn_a_log"])
    dtb = _pad_lanes(small["gdn_dt_bias"])
    logits = small["hgrn_lb_logits"]
    hg_gain = small["hgrn_out_norm"].reshape(1, HEAD)
    gd_gain = small["gdn_out_norm"].reshape(1, HEAD)
    g1, gm, g2 = small["ffn1_norm"].reshape(1, -1), small["mix_norm"].reshape(1, -1), small["ffn2_norm"].reshape(1, -1)
    gf = small["final_norm"].reshape(1, -1)
    qscale = HEAD ** -0.5

    h1, ffn1_saved = _ffn_fwd(x, g1, w["ffn1_w_in"], w["ffn1_w_out"], "ffn1")
    u = _rmsnorm_fwd(h1, gm, "mix_norm")
    pr = {n: _mm(u, seg[n], name="proj_" + n) for n in big_segs}
    gab = _mm(u, w_gab, name="proj_gab")
    oh_raw, oh, s_h = _hgrn_fwd(pr["hq"], pr["hf"], pr["hi"], pr["hg"], logits, hg_gain, hg_c)
    qn = _conv_fwd(pr["gq"], conv_q, qscale, "conv_q")
    kn = _conv_fwd(pr["gk"], conv_k, 1.0, "conv_k")
    cv = _conv_fwd(pr["gv"], conv_v, None, "conv_v")
    og_raw, og, s_g = _gdn_fwd(qn, kn, cv, gab, pr["gz"], alog, dtb, gd_gain, gd_c)
    yh = _mm(oh, w["w_branch_hgrn"], name="branch_h")
    yg = _mm(og, w["w_branch_gdn"], name="branch_g")
    ym = _merge_fwd(yh, yg, pr["gate_h"], pr["gate_g"])
    h2 = _mm(ym, w["w_out"], res=h1, name="mix_out")
    h3, ffn2_saved = _ffn_fwd(h2, g2, w["ffn2_w_in"], w["ffn2_w_out"], "ffn2")
    loss, dh3, dh3_bf, d_gf = _final_loss(h3, gf, tgt)

    dh2, dh2_bf, d_g2, d_f2in, d_f2out = _ffn_bwd(h2, g2, w["ffn2_w_in"], w["ffn2_w_out"], ffn2_saved, dh3, dh3_bf,
                                                  "ffn2")
    dym = _mm(dh2_bf, w["w_out"], tb=True, name="d_merge")
    d_wout = _mm(ym, dh2_bf, ta=True, name="d_w_out")
    dyh, dyg, d_gate_h, d_gate_g = _merge_bwd(dym, yh, yg, pr["gate_h"], pr["gate_g"])
    d_wbh = _mm(oh, dyh, ta=True, name="d_w_branch_h")
    d_wbg = _mm(og, dyg, ta=True, name="d_w_branch_g")
    doh = _mm(dyh, w["w_branch_hgrn"], tb=True, name="d_oh")
    dog = _mm(dyg, w["w_branch_gdn"], tb=True, name="d_og")
    d_hq, d_hf, d_hi, d_hg, d_hg_gain, d_lb0 = _hgrn_bwd(pr["hq"], pr["hf"], pr["hi"], pr["hg"], logits, hg_gain,
                                                        oh_raw, s_h, doh, hg_c)
    d_qn, d_kn, d_cv, d_gab_wide, d_gz, gd_small = _gdn_bwd(qn, kn, cv, gab, pr["gz"], alog, dtb, gd_gain, og_raw,
                                                            s_g, dog, gd_c)
    d_gab = _fold_groups(d_gab_wide)
    dc_q, dwc_q = _conv_bwd_a(pr["gq"], conv_q, d_qn, qscale, "dconv_q")
    dc_k, dwc_k = _conv_bwd_a(pr["gk"], conv_k, d_kn, 1.0, "dconv_k")
    dc_v, dwc_v = _conv_bwd_a(pr["gv"], conv_v, d_cv, None, "dconv_v")
    d_gq = _conv_bwd_b(dc_q, conv_q, "dconvx_q")
    d_gk = _conv_bwd_b(dc_k, conv_k, "dconvx_k")
    d_gv = _conv_bwd_b(dc_v, conv_v, "dconvx_v")
    dpr = {"hq": d_hq, "hf": d_hf, "hi": d_hi, "hg": d_hg, "gq": d_gq, "gk": d_gk, "gv": d_gv, "gz": d_gz,
           "gate_h": d_gate_h, "gate_g": d_gate_g}
    du = _mm(d_gab, w_gab, tb=True, name="du_gab")
    d_wseg = {}
    for n in big_segs:
        du = _mm(dpr[n], seg[n], tb=True, res=du, name="du_" + n)
        d_wseg[n] = _mm(u, dpr[n], ta=True, name="dw_" + n)
    d_wgab = _mm(u, d_gab, ta=True, name="dw_gab")
    d_wseg["ga"], d_wseg["gb"] = d_wgab[:, :16], d_wgab[:, 16:32]
    d_win = jnp.concatenate([d_wseg[n] for n in IN_NAMES], axis=1)
    dh1, dh1_bf, d_gm = _rmsnorm_bwd(h1, gm, du, dh2, "mix_dnorm")
    dx, _, d_g1, d_f1in, d_f1out = _ffn_bwd(x, g1, w["ffn1_w_in"], w["ffn1_w_out"], ffn1_saved, dh1, dh1_bf, "ffn1")

    d_conv = jnp.concatenate([dwc_q[:CONV_K], dwc_k[:CONV_K], dwc_v[:CONV_K]], axis=1)
    big = {"ffn1_w_in": d_f1in, "ffn1_w_out": d_f1out, "w_in": d_win, "gdn_conv_w": d_conv,
           "w_branch_hgrn": d_wbh, "w_branch_gdn": d_wbg, "w_out": d_wout, "ffn2_w_in": d_f2in, "ffn2_w_out": d_f2out}
    d_lb0 = d_lb0.reshape(1, -1)
    sm = {"ffn1_norm": d_g1, "mix_norm": d_gm, "hgrn_lb_logits": jnp.concatenate([d_lb0, -d_lb0], axis=0),
          "hgrn_out_norm": d_hg_gain, "gdn_a_log": gd_small[2, :16], "gdn_dt_bias": gd_small[1, :16],
          "gdn_out_norm": gd_small[0], "ffn2_norm": d_g2, "final_norm": d_gf, "loss": loss[0, :1]}
    return dx, big, sm


_WEIGHTS = ("ffn1_norm", "ffn1_w_in", "ffn1_w_out", "mix_norm", "w_in", "hgrn_lb_logits", "hgrn_out_norm",
            "gdn_conv_w", "gdn_a_log", "gdn_dt_bias", "gdn_out_norm", "w_branch_hgrn", "w_branch_gdn", "w_out",
            "ffn2_norm", "ffn2_w_in", "ffn2_w_out", "final_norm")
_BIG_NAMES = tuple(n for n, _, _ in _BIG)


def kernel(x, ffn1_norm, ffn1_w_in, ffn1_w_out, mix_norm, w_in, hgrn_lb_logits, hgrn_out_norm, gdn_conv_w, gdn_a_log, gdn_dt_bias, gdn_out_norm, w_branch_hgrn, w_branch_gdn, w_out, ffn2_norm, ffn2_w_in, ffn2_w_out, final_norm, loss_target, m_ffn1_norm, m_ffn1_w_in, m_ffn1_w_out, m_mix_norm, m_w_in, m_hgrn_lb_logits, m_hgrn_out_norm, m_gdn_conv_w, m_gdn_a_log, m_gdn_dt_bias, m_gdn_out_norm, m_w_branch_hgrn, m_w_branch_gdn, m_w_out, m_ffn2_norm, m_ffn2_w_in, m_ffn2_w_out, m_final_norm, v_ffn1_norm, v_ffn1_w_in, v_ffn1_w_out, v_mix_norm, v_w_in, v_hgrn_lb_logits, v_hgrn_out_norm, v_gdn_conv_w, v_gdn_a_log, v_gdn_dt_bias, v_gdn_out_norm, v_w_branch_hgrn, v_w_branch_gdn, v_w_out, v_ffn2_norm, v_ffn2_w_in, v_ffn2_w_out, v_final_norm):
    args = dict(locals())
    wts = {n: args[n] for n in _WEIGHTS}
    moms = {n: args["m_" + n] for n in _WEIGHTS}
    vars_ = {n: args["v_" + n] for n in _WEIGHTS}

    shard2d = {n: wts[n].reshape(-1, 1024) for n in _BIG_NAMES}
    packed = _pack_rows({n: shard2d[n].astype(BF16) for n in _BIG_NAMES}, ())
    xi, yi, ci = lax.axis_index("x"), lax.axis_index("y"), lax.axis_index("c")
    gathered = _unpack_rows(lax.dynamic_update_index_in_dim(_gather_weights(packed), packed, 2 * xi + yi, 0))
    full = {n: _full_from_shards(n, gathered[n]) for n in _BIG_NAMES}
    small = {n: wts[n].astype(F32) for n in _WEIGHTS if n not in _BIG_NAMES}

    dx, big_grads, small_grads = _local_step(x[0], loss_target[0], full, small)

    gpack = _pack_rows({n: _shards_from_full(n, big_grads[n]).astype(BF16) for n in _BIG_NAMES}, (4,))
    half = _BIG_ROWS // 2
    own = lax.dynamic_slice_in_dim(gpack, ci * half, half, axis=1)
    give = lax.dynamic_slice_in_dim(gpack, (1 - ci) * half, half, axis=1)
    chip_sums = _add2(own, _reduce_pair(give))
    own_chip = lax.dynamic_index_in_dim(chip_sums, 2 * xi + yi, axis=0, keepdims=False)
    mine = _add4(own_chip, _reduce_chips(chip_sums))
    theirs = _share_pair(mine)
    south = ci == 0
    reduced = _unpack_rows(jnp.concatenate([jnp.where(south, mine, theirs), jnp.where(south, theirs, mine)], axis=0))

    out_g, out_d, out_m, out_v = {}, {}, {}, {}
    for n in _BIG_NAMES:
        shape = wts[n].shape
        w2 = wts[n].reshape(shape[-2], shape[-1])
        g2 = reduced[n].reshape(shape[-2], shape[-1])
        d, m2, v2 = _adamw(w2, g2, moms[n].reshape(w2.shape), vars_[n].reshape(w2.shape), "adamw_" + n)
        out_g[n], out_d[n], out_m[n], out_v[n] = g2.reshape(shape), d.reshape(shape), m2.reshape(shape), v2.reshape(shape)

    small_names = [n for n, _ in _SMALL]
    zero = jnp.zeros((1,), F32)
    shapes = {n: (wts[n].shape if n != "loss" else (1,)) for n in small_names}
    sums, sd, sm_, sv = _small_sync(
        _pack_small(small_grads),
        _pack_small({n: (wts[n] if n != "loss" else zero) for n in small_names}),
        _pack_small({n: (moms[n] if n != "loss" else zero) for n in small_names}),
        _pack_small({n: (vars_[n] if n != "loss" else zero) for n in small_names}))
    sg_u, sd_u, sm_u, sv_u = (_unpack_small(p, shapes) for p in (sums, sd, sm_, sv))
    for n in small_names:
        if n != "loss":
            out_g[n], out_d[n], out_m[n], out_v[n] = sg_u[n], sd_u[n], sm_u[n], sv_u[n]
    loss = sg_u["loss"].reshape(())

    return (loss, dx[None], *[out_g[n] for n in _WEIGHTS], *[out_d[n] for n in _WEIGHTS],
            *[out_m[n] for n in _WEIGHTS], *[out_v[n] for n in _WEIGHTS])
```

```python
import numpy as np

import jax
import jax.numpy as jnp
from jax import lax
from jax.experimental import pallas as pl
from jax.experimental.pallas import tpu as pltpu

F32 = jnp.float32
BF16 = jnp.bfloat16

D_MODEL = 1024
D_FF = 2816
CHUNK = 64
HEAD = 128
HG_HEADS = 8
GD_HEADS = 16
HPS = 8
COMM_CHUNKS = 9
MM_TM = 1408
MM_TN = 512
MM_TK = 1536
VMEM_LIMIT = 48 * 1024 * 1024
EPS = 1e-6
CONV_K = 4
IN_NAMES = ("hq", "hf", "hi", "hg", "gq", "gk", "gv", "ga", "gb", "gz", "gate_h", "gate_g")
IN_SIZES = (1024, 1024, 1024, 1024, 1024, 1024, 2048, 16, 16, 2048, 1024, 1024)
IN_WIDTH = sum(IN_SIZES)

ADAM_LR = 0.001
ADAM_B1 = 0.9
ADAM_B2 = 0.999
ADAM_EPS = 1e-08
ADAM_WD = 0.01
ADAM_STEP = 10

MESH = pl.DeviceIdType.MESH
_ARB = "arbitrary"
_PAR = "parallel"


def _bf(x):
    return x.astype(BF16)


def _dot(a, b):
    return jnp.dot(_bf(a), _bf(b), preferred_element_type=F32)


def _dot_nt(a, b):
    return lax.dot_general(_bf(a), _bf(b), (((1,), (1,)), ((), ())), preferred_element_type=F32)


def _dot_tn(a, b):
    return lax.dot_general(_bf(a), _bf(b), (((0,), (0,)), ((), ())), preferred_element_type=F32)


def _split3(x):
    hi = _bf(x)
    r = x - hi.astype(F32)
    mid = _bf(r)
    lo = _bf(r - mid.astype(F32))
    return hi, mid, lo


def _dot_mx(m, x):
    hi, mid, lo = _split3(x)
    return (jnp.dot(m, hi, preferred_element_type=F32) + jnp.dot(m, mid, preferred_element_type=F32)
            + jnp.dot(m, lo, preferred_element_type=F32))


def _dot_xm(x, m):
    hi, mid, lo = _split3(x)
    return (jnp.dot(hi, m, preferred_element_type=F32) + jnp.dot(mid, m, preferred_element_type=F32)
            + jnp.dot(lo, m, preferred_element_type=F32))


def _dot_hp(a, b):
    ah = _bf(a)
    al = _bf(a - ah.astype(F32))
    bh = _bf(b)
    bl = _bf(b - bh.astype(F32))
    return (jnp.dot(ah, bh, preferred_element_type=F32) + jnp.dot(ah, bl, preferred_element_type=F32)
            + jnp.dot(al, bh, preferred_element_type=F32))


def _sigmoid(x):
    return jax.nn.sigmoid(x)


def _silu(x):
    return x * _sigmoid(x)


def _dsilu(x):
    s = _sigmoid(x)
    return s * (1.0 + x * (1.0 - s))


def _softplus(x):
    return jnp.maximum(x, 0.0) + jnp.log(1.0 + jnp.exp(-jnp.abs(x)))


def _rowsum(x):
    return jnp.sum(x, axis=1, keepdims=True)


def _col_to_row(col, eye):
    return jnp.sum(eye * col, axis=0, keepdims=True)


def _row_to_col(row, eye):
    return jnp.sum(eye * row, axis=1, keepdims=True)


def _pick(dim, pref, unit=128):
    if dim <= pref:
        return dim
    t = pref
    while t >= unit:
        if dim % t == 0:
            return t
        t -= unit
    return dim


def _params(*sem):
    return pltpu.CompilerParams(dimension_semantics=tuple(sem), vmem_limit_bytes=VMEM_LIMIT)


def _mm(a, b, *, ta=False, tb=False, alpha=1.0, res=None, out_dtype=F32, name="mm"):
    m = a.shape[1] if ta else a.shape[0]
    k = a.shape[0] if ta else a.shape[1]
    n = b.shape[0] if tb else b.shape[1]
    assert k == (b.shape[1] if tb else b.shape[0])
    tm, tn, tk = _pick(m, MM_TM), _pick(n, MM_TN), _pick(k, MM_TK)
    nk = k // tk
    a_spec = pl.BlockSpec((tk, tm), lambda i, j, l: (l, i)) if ta else pl.BlockSpec((tm, tk), lambda i, j, l: (i, l))
    b_spec = pl.BlockSpec((tn, tk), lambda i, j, l: (j, l)) if tb else pl.BlockSpec((tk, tn), lambda i, j, l: (l, j))
    o_spec = pl.BlockSpec((tm, tn), lambda i, j, l: (i, j))
    dims = (((0 if ta else 1,), (1 if tb else 0,)), ((), ()))
    has_res = res is not None

    def finish(r, r_ref, o_ref):
        if alpha != 1.0:
            r = r * alpha
        if has_res:
            r = r + r_ref[...]
        o_ref[...] = r.astype(out_dtype)

    def body(*refs):
        a_ref, b_ref = refs[0], refs[1]
        r_ref = refs[2] if has_res else None
        o_ref = refs[3] if has_res else refs[2]
        part = lax.dot_general(_bf(a_ref[...]), _bf(b_ref[...]), dims, preferred_element_type=F32)
        if nk == 1:
            finish(part, r_ref, o_ref)
            return
        acc = refs[-1]
        step = pl.program_id(2)

        @pl.when(step == 0)
        def _():
            acc[...] = part

        @pl.when(step != 0)
        def _():
            acc[...] += part

        @pl.when(step == nk - 1)
        def _():
            finish(acc[...], r_ref, o_ref)

    ins = [a, b] + ([res] if has_res else [])
    in_specs = [a_spec, b_spec] + ([o_spec] if has_res else [])
    return pl.pallas_call(
        body, name=name, grid=(m // tm, n // tn, nk), in_specs=in_specs, out_specs=o_spec,
        out_shape=jax.ShapeDtypeStruct((m, n), out_dtype),
        scratch_shapes=[pltpu.VMEM((tm, tn), F32)] if nk > 1 else [],
        compiler_params=_params(_PAR, _PAR, _ARB))(*ins)


def _row_spec(tr, w):
    return pl.BlockSpec((tr, w), lambda i: (i, 0))


def _full_spec(shape):
    return pl.BlockSpec(shape, lambda i: tuple(0 for _ in shape))


def _rmsnorm_fwd(x, g, name):
    t, d = x.shape
    tr = _pick(t, 256, 8)

    def body(x_ref, g_ref, o_ref):
        xv = x_ref[...]
        r = lax.rsqrt(jnp.mean(xv * xv, axis=1, keepdims=True) + EPS)
        o_ref[...] = (xv * r * g_ref[...]).astype(BF16)

    return pl.pallas_call(
        body, name=name, grid=(t // tr,), in_specs=[_row_spec(tr, d), _full_spec((1, d))],
        out_specs=_row_spec(tr, d), out_shape=jax.ShapeDtypeStruct((t, d), BF16),
        compiler_params=_params(_PAR))(x, g)


def _rmsnorm_bwd(x, g, dn, res, name):
    t, d = x.shape
    tr = _pick(t, 256, 8)

    def body(x_ref, g_ref, dn_ref, r_ref, dx_ref, dxb_ref, dg_ref):
        @pl.when(pl.program_id(0) == 0)
        def _():
            dg_ref[...] = jnp.zeros_like(dg_ref)

        xv = x_ref[...]
        r = lax.rsqrt(jnp.mean(xv * xv, axis=1, keepdims=True) + EPS)
        xh = xv * r
        dy = dn_ref[...]
        dg_ref[...] += jnp.sum(dy * xh, axis=0, keepdims=True)
        dxh = dy * g_ref[...]
        dx = r_ref[...] + r * (dxh - xh * jnp.mean(dxh * xh, axis=1, keepdims=True))
        dx_ref[...] = dx
        dxb_ref[...] = dx.astype(BF16)

    return pl.pallas_call(
        body, name=name, grid=(t // tr,),
        in_specs=[_row_spec(tr, d), _full_spec((1, d)), _row_spec(tr, d), _row_spec(tr, d)],
        out_specs=[_row_spec(tr, d), _row_spec(tr, d), _full_spec((1, d))],
        out_shape=[jax.ShapeDtypeStruct((t, d), F32), jax.ShapeDtypeStruct((t, d), BF16),
                   jax.ShapeDtypeStruct((1, d), F32)],
        compiler_params=_params(_ARB))(x, g, dn, res)


def _swiglu_fwd(ab, name):
    t = ab.shape[0]
    tr = _pick(t, 256, 8)

    def body(ab_ref, o_ref):
        a = ab_ref[:, :D_FF].astype(F32)
        b = ab_ref[:, D_FF:].astype(F32)
        o_ref[...] = (_silu(a) * b).astype(BF16)

    return pl.pallas_call(
        body, name=name, grid=(t // tr,), in_specs=[_row_spec(tr, 2 * D_FF)], out_specs=_row_spec(tr, D_FF),
        out_shape=jax.ShapeDtypeStruct((t, D_FF), BF16), compiler_params=_params(_PAR))(ab)


def _swiglu_bwd(ab, dhm, name):
    t = ab.shape[0]
    tr = _pick(t, 256, 8)

    def body(ab_ref, dh_ref, dab_ref, hm_ref):
        a = ab_ref[:, :D_FF].astype(F32)
        b = ab_ref[:, D_FF:].astype(F32)
        dh = dh_ref[...]
        dab_ref[:, :D_FF] = (dh * b * _dsilu(a)).astype(BF16)
        sa = _silu(a)
        dab_ref[:, D_FF:] = (dh * sa).astype(BF16)
        hm_ref[...] = (sa * b).astype(BF16)

    return pl.pallas_call(
        body, name=name, grid=(t // tr,), in_specs=[_row_spec(tr, 2 * D_FF), _row_spec(tr, D_FF)],
        out_specs=[_row_spec(tr, 2 * D_FF), _row_spec(tr, D_FF)],
        out_shape=[jax.ShapeDtypeStruct((t, 2 * D_FF), BF16), jax.ShapeDtypeStruct((t, D_FF), BF16)],
        compiler_params=_params(_PAR))(ab, dhm)


def _merge_fwd(yh, yg, gh, gg):
    t, d = yh.shape
    tr = _pick(t, 256, 8)

    def body(yh_ref, yg_ref, gh_ref, gg_ref, o_ref):
        o_ref[...] = (_sigmoid(gh_ref[...]) * yh_ref[...] + _sigmoid(gg_ref[...]) * yg_ref[...]).astype(BF16)

    return pl.pallas_call(
        body, name="merge_fwd", grid=(t // tr,), in_specs=[_row_spec(tr, d)] * 4, out_specs=_row_spec(tr, d),
        out_shape=jax.ShapeDtypeStruct((t, d), BF16), compiler_params=_params(_PAR))(yh, yg, gh, gg)


def _merge_bwd(dy, yh, yg, gh, gg):
    t, d = yh.shape
    tr = _pick(t, 256, 8)

    def body(dy_ref, yh_ref, yg_ref, gh_ref, gg_ref, dyh_ref, dyg_ref, dgh_ref, dgg_ref):
        dyv = dy_ref[...]
        sh = _sigmoid(gh_ref[...])
        sg = _sigmoid(gg_ref[...])
        dyh_ref[...] = (dyv * sh).astype(BF16)
        dyg_ref[...] = (dyv * sg).astype(BF16)
        dgh_ref[...] = (dyv * yh_ref[...] * sh * (1.0 - sh)).astype(BF16)
        dgg_ref[...] = (dyv * yg_ref[...] * sg * (1.0 - sg)).astype(BF16)

    return pl.pallas_call(
        body, name="merge_bwd", grid=(t // tr,), in_specs=[_row_spec(tr, d)] * 5, out_specs=[_row_spec(tr, d)] * 4,
        out_shape=[jax.ShapeDtypeStruct((t, d), BF16)] * 4,
        compiler_params=_params(_PAR))(dy, yh, yg, gh, gg)


def _final_loss(h, g, tgt):
    t, d = h.shape
    tr = _pick(t, 256, 8)

    def body(h_ref, g_ref, t_ref, loss_ref, dh_ref, dhb_ref, dg_ref):
        @pl.when(pl.program_id(0) == 0)
        def _():
            dg_ref[...] = jnp.zeros_like(dg_ref)
            loss_ref[...] = jnp.zeros_like(loss_ref)

        xv = h_ref[...]
        gv = g_ref[...]
        r = lax.rsqrt(jnp.mean(xv * xv, axis=1, keepdims=True) + EPS)
        xh = xv * r
        err = xh * gv - t_ref[...]
        loss_ref[...] += 0.5 * jnp.sum(jnp.mean(err * err, axis=1, keepdims=True), axis=0, keepdims=True)
        dy = err * (1.0 / d)
        dg_ref[...] += jnp.sum(dy * xh, axis=0, keepdims=True)
        dxh = dy * gv
        dh = r * (dxh - xh * jnp.mean(dxh * xh, axis=1, keepdims=True))
        dh_ref[...] = dh
        dhb_ref[...] = dh.astype(BF16)

    return pl.pallas_call(
        body, name="final_loss", grid=(t // tr,),
        in_specs=[_row_spec(tr, d), _full_spec((1, d)), _row_spec(tr, d)],
        out_specs=[_full_spec((1, 128)), _row_spec(tr, d), _row_spec(tr, d), _full_spec((1, d))],
        out_shape=[jax.ShapeDtypeStruct((1, 128), F32), jax.ShapeDtypeStruct((t, d), F32),
                   jax.ShapeDtypeStruct((t, d), BF16), jax.ShapeDtypeStruct((1, d), F32)],
        compiler_params=_params(_ARB))(h, g, tgt)


def _hg_consts():
    c = CHUNK
    t = np.arange(c)
    mats, masks = [], []
    for lvl in range(6):
        m = 1 << lvl
        blk = t // m
        mat = np.zeros((c, c), np.float32)
        for tt in range(c):
            b = blk[tt]
            if b % 2 == 1:
                mat[tt, b * m:tt + 1] = 1.0
            else:
                mat[tt, tt + 1:(b + 1) * m] = 1.0
        mats.append(mat)
        same = (t[:, None] // (2 * m)) == (t[None, :] // (2 * m))
        masks.append((same & (blk[:, None] % 2 == 1) & (blk[None, :] % 2 == 0)).astype(np.float32))
    pre = np.tril(np.ones((c, c), np.float32))
    suf = np.triu(np.ones((c, c), np.float32), 1)
    mstack = np.concatenate(mats + [pre, suf], 0)
    masks.append(np.eye(c, dtype=np.float32))
    return (jnp.asarray(mstack, BF16), jnp.asarray(mstack.T.copy(), BF16), jnp.asarray(np.stack(masks), F32),
            jnp.asarray(np.eye(HEAD, dtype=np.float32)))


def _gd_consts():
    c = CHUNK
    incl = np.tril(np.ones((c, c), np.float32))
    strict = np.tril(np.ones((c, c), np.float32), -1)
    eye = np.eye(c, dtype=np.float32)
    masks = np.stack([incl, strict, eye, incl.T.copy()])
    sel = np.zeros((32, HEAD, HEAD), np.float32)
    for j in range(32):
        sel[j, j, :] = 1.0
    return (jnp.asarray(incl, BF16), jnp.asarray(incl.T.copy(), BF16), jnp.asarray(masks, F32), jnp.asarray(sel, BF16))


def _chunks_per_step(nc):
    for cb in (32 // HPS, 2, 1):
        if nc % cb == 0:
            return cb
    return 1


def _hg_prep(hq, hf, lg):
    lb = _sigmoid(lg[0:1, :] - lg[1:2, :])
    sg = _sigmoid(hf)
    sgn = _sigmoid(-hf)
    f = lb + (1.0 - lb) * sg
    lf = jnp.log(f)
    kk = (1.0 - lb) * sgn
    q = _silu(hq) * (HEAD ** -0.5)
    return lb, sg, sgn, f, lf, kk, q


def _mx_each(m, xs):
    parts = [_split3(x) for x in xs]
    out = None
    for p in range(3):
        d = [jnp.dot(m, s[p], preferred_element_type=F32) for s in parts]
        out = d if out is None else [a + b for a, b in zip(out, d)]
    return out


def _hg_scores(q, kk, ex, mask_ref):
    p = [mask_ref[6] * _rowsum(a * b) for a, b in zip(q, kk)]
    for lvl in range(6):
        el = [e[lvl * CHUNK:(lvl + 1) * CHUNK] for e in ex]
        d = [_dot_nt(a * e, b * e) for a, b, e in zip(q, kk, el)]
        p = [x + mask_ref[lvl] * y for x, y in zip(p, d)]
    return p


def _hgrn_fwd(hq, hf, hi, hg, logits, gain, consts):
    t = hq.shape[0]
    nc = t // CHUNK
    cb = _chunks_per_step(nc)
    rows = cb * CHUNK
    mstack, _, masks, eye = consts
    tile = pl.BlockSpec((rows, HPS * HEAD), lambda c, g: (c, g))

    def body(hq_ref, hf_ref, hi_ref, hg_ref, lg_ref, gain_ref, m_ref, mask_ref, eye_ref,
             oraw_ref, og_ref, ssave_ref, state):
        c = pl.program_id(0)
        g = pl.program_id(1)

        @pl.when(c == 0)
        def _():
            for hh in range(HPS):
                state[g * HPS + hh] = jnp.zeros((HEAD, HEAD), F32)

        lg_all = lg_ref[...]
        gain_v = gain_ref[...]

        def one(i, carry):
            sl = pl.ds(pl.multiple_of(i * CHUNK, CHUNK), CHUNK)
            hs = range(HPS)
            heads = [g * HPS + hh for hh in hs]
            ln = [slice(hh * HEAD, (hh + 1) * HEAD) for hh in hs]
            preps = [_hg_prep(hq_ref[sl, s], hf_ref[sl, s], lg_all[:, s]) for s in ln]
            lf, kk, q = [p[4] for p in preps], [p[5] for p in preps], [p[6] for p in preps]
            v = [hi_ref[sl, s] for s in ln]
            ex = [jnp.exp(x) for x in _mx_each(m_ref[...], lf)]
            eb = [e[6 * CHUNK:7 * CHUNK] for e in ex]
            esfx = [e[7 * CHUNK:8 * CHUNK] for e in ex]
            p = _hg_scores(q, kk, ex, mask_ref)
            s0 = [state[h] for h in heads]
            o = _each(lambda a, e, s, pp, vv: _dot(a * e, s) + _dot(pp, vv), q, eb, s0, p, v)
            eye_v = eye_ref[...]
            s1 = _each(lambda s, e, kx, ef, vv: s * _row_to_col(e[CHUNK - 1:CHUNK, :], eye_v) + _dot_tn(kx * ef, vv),
                       s0, eb, kk, esfx, v)
            for hh in hs:
                ssave_ref[i, hh] = s0[hh]
                state[heads[hh]] = s1[hh]
                oraw_ref[sl, ln[hh]] = o[hh]
                r = lax.rsqrt(jnp.mean(o[hh] * o[hh], axis=1, keepdims=True) + EPS)
                og_ref[sl, ln[hh]] = (o[hh] * r * gain_v * _silu(hg_ref[sl, ln[hh]])).astype(BF16)
            return carry

        lax.fori_loop(0, cb, one, 0)

    return pl.pallas_call(
        body, name="hgrn_fwd", grid=(nc // cb, HG_HEADS // HPS),
        in_specs=[tile, tile, tile, tile, pl.BlockSpec((2, HPS * HEAD), lambda c, g: (0, g)),
                  pl.BlockSpec((1, HEAD), lambda c, g: (0, 0)),
                  pl.BlockSpec(mstack.shape, lambda c, g: (0, 0)),
                  pl.BlockSpec(masks.shape, lambda c, g: (0, 0, 0)),
                  pl.BlockSpec(eye.shape, lambda c, g: (0, 0))],
        out_specs=[tile, tile, pl.BlockSpec((cb, HPS, HEAD, HEAD), lambda c, g: (c, g, 0, 0))],
        out_shape=[jax.ShapeDtypeStruct((t, HG_HEADS * HEAD), F32), jax.ShapeDtypeStruct((t, HG_HEADS * HEAD), BF16),
                   jax.ShapeDtypeStruct((nc, HG_HEADS, HEAD, HEAD), F32)],
        scratch_shapes=[pltpu.VMEM((HG_HEADS, HEAD, HEAD), F32)],
        compiler_params=_params(_ARB, _ARB))(hq, hf, hi, hg, logits, gain, mstack, masks, eye)


def _hgrn_bwd(hq, hf, hi, hg, logits, gain, oraw, ssave, dog, consts):
    t = hq.shape[0]
    nc = t // CHUNK
    cb = _chunks_per_step(nc)
    rows = cb * CHUNK
    nb = nc // cb
    mstack, mstack_t, masks, eye = consts
    tile = pl.BlockSpec((rows, HPS * HEAD), lambda c, g: (nb - 1 - c, g))

    def body(hq_ref, hf_ref, hi_ref, hg_ref, lg_ref, gain_ref, oraw_ref, ssave_ref, dog_ref, m_ref, mt_ref,
             mask_ref, eye_ref, dhq_ref, dhf_ref, dhi_ref, dhg_ref, dgain_ref, dlb_ref, dstate):
        c = pl.program_id(0)
        g = pl.program_id(1)

        @pl.when(c == 0)
        def _():
            for hh in range(HPS):
                dstate[g * HPS + hh] = jnp.zeros((HEAD, HEAD), F32)

        @pl.when((c == 0) & (g == 0))
        def _():
            dgain_ref[...] = jnp.zeros_like(dgain_ref)
            dlb_ref[...] = jnp.zeros_like(dlb_ref)

        lg_all = lg_ref[...]
        gain_v = gain_ref[...]
        eye_v = eye_ref[...]
        last_row = (lax.broadcasted_iota(jnp.int32, (CHUNK, HEAD), 0) == CHUNK - 1).astype(F32)

        def one(j, carry):
            i = cb - 1 - j
            sl = pl.ds(pl.multiple_of(i * CHUNK, CHUNK), CHUNK)
            hs = range(HPS)
            heads = [g * HPS + hh for hh in hs]
            ln = [slice(hh * HEAD, (hh + 1) * HEAD) for hh in hs]
            hqv = [hq_ref[sl, s] for s in ln]
            hgv = [hg_ref[sl, s] for s in ln]
            preps = [_hg_prep(a, hf_ref[sl, s], lg_all[:, s]) for a, s in zip(hqv, ln)]
            lb, sg, sgn, f, lf, kk, q = ([p[n] for p in preps] for n in range(7))
            v = [hi_ref[sl, s] for s in ln]
            ex = [jnp.exp(x) for x in _mx_each(m_ref[...], lf)]
            eb = [e[6 * CHUNK:7 * CHUNK] for e in ex]
            esfx = [e[7 * CHUNK:8 * CHUNK] for e in ex]
            p = _hg_scores(q, kk, ex, mask_ref)
            s0 = [ssave_ref[i, hh] for hh in hs]
            ds = [dstate[h] for h in heads]

            o = [oraw_ref[sl, s] for s in ln]
            r = [lax.rsqrt(jnp.mean(x * x, axis=1, keepdims=True) + EPS) for x in o]
            on = _each(lambda x, y: x * y, o, r)
            dg_out = [dog_ref[sl, s] for s in ln]
            sgate = [_silu(x) for x in hgv]
            for hh in hs:
                dhg_ref[sl, ln[hh]] = (dg_out[hh] * on[hh] * gain_v * _dsilu(hgv[hh])).astype(BF16)
            dgain_ref[...] += sum(jnp.sum(d * s * n, axis=0, keepdims=True) for d, s, n in zip(dg_out, sgate, on))
            don = _each(lambda d, s: d * s * gain_v, dg_out, sgate)
            do = _each(lambda rr, dn, n: rr * (dn - n * jnp.mean(dn * n, axis=1, keepdims=True)), r, don, on)

            dp = _each(_dot_nt, do, v)
            dv = _each(lambda pp, d, kx, ef, s: _dot_tn(pp, d) + _dot(kx * ef, s), p, do, kk, esfx, ds)
            dqb = _each(_dot_nt, do, s0)
            dkx = _each(_dot_nt, v, ds)
            diag = [_rowsum(mask_ref[6] * x) for x in dp]
            dq = _each(lambda a, e, d, kx: a * e + d * kx, dqb, eb, diag, kk)
            dk = _each(lambda a, e, d, qq: a * e + d * qq, dkx, esfx, diag, q)
            dxs = [[] for _ in hs]
            for lvl in range(6):
                el = [e[lvl * CHUNK:(lvl + 1) * CHUNK] for e in ex]
                gm = [mask_ref[lvl] * x for x in dp]
                a1 = _each(lambda m_, kx, e: _dot(m_, kx * e), gm, kk, el)
                a2 = _each(lambda m_, qq, e: _dot_tn(m_, qq * e), gm, q, el)
                dq = _each(lambda x, a, e: x + a * e, dq, a1, el)
                dk = _each(lambda x, a, e: x + a * e, dk, a2, el)
                for hh in hs:
                    dxs[hh].append((a1[hh] * q[hh] + a2[hh] * kk[hh]) * el[hh])
            e_end_row = [e[CHUNK - 1:CHUNK, :] for e in eb]
            ds_new = _each(lambda qq, e, d, er, s: _dot_tn(qq * e, d) + _row_to_col(er, eye_v) * s, q, eb, do, e_end_row, ds)
            for hh in hs:
                dstate[heads[hh]] = ds_new[hh]
                dend_row = _col_to_row(_rowsum(s0[hh] * ds[hh]), eye_v)
                dxs[hh].append(dqb[hh] * q[hh] * eb[hh] + last_row * (e_end_row[hh] * dend_row))
                dxs[hh].append(dkx[hh] * kk[hh] * esfx[hh])
            dlf = _mx_each(mt_ref[...], [jnp.concatenate(x, axis=0) for x in dxs])

            for hh in hs:
                dhi_ref[sl, ln[hh]] = dv[hh].astype(BF16)
                dhq_ref[sl, ln[hh]] = (dq[hh] * (HEAD ** -0.5) * _dsilu(hqv[hh])).astype(BF16)
                df = dlf[hh] / f[hh]
                dsig = (1.0 - lb[hh]) * sg[hh] * sgn[hh]
                dhf_ref[sl, ln[hh]] = ((df - dk[hh]) * dsig).astype(BF16)
                dlb_t = jnp.sum(df * sgn[hh] - dk[hh] * sgn[hh], axis=0, keepdims=True)
                dlb_ref[pl.ds(heads[hh], 1), :] += dlb_t * lb[hh] * (1.0 - lb[hh])
            return carry

        lax.fori_loop(0, cb, one, 0)

    outs = [jax.ShapeDtypeStruct((t, HG_HEADS * HEAD), BF16)] * 4 + [
        jax.ShapeDtypeStruct((1, HEAD), F32), jax.ShapeDtypeStruct((HG_HEADS, HEAD), F32)]
    return pl.pallas_call(
        body, name="hgrn_bwd", grid=(nb, HG_HEADS // HPS),
        in_specs=[tile, tile, tile, tile, pl.BlockSpec((2, HPS * HEAD), lambda c, g: (0, g)),
                  pl.BlockSpec((1, HEAD), lambda c, g: (0, 0)), tile,
                  pl.BlockSpec((cb, HPS, HEAD, HEAD), lambda c, g: (nb - 1 - c, g, 0, 0)), tile,
                  pl.BlockSpec(mstack.shape, lambda c, h: (0, 0)),
                  pl.BlockSpec(mstack_t.shape, lambda c, h: (0, 0)),
                  pl.BlockSpec(masks.shape, lambda c, h: (0, 0, 0)),
                  pl.BlockSpec(eye.shape, lambda c, h: (0, 0))],
        out_specs=[tile, tile, tile, tile, pl.BlockSpec((1, HEAD), lambda c, h: (0, 0)),
                   pl.BlockSpec((HG_HEADS, HEAD), lambda c, h: (0, 0))],
        out_shape=outs, scratch_shapes=[pltpu.VMEM((HG_HEADS, HEAD, HEAD), F32)],
        compiler_params=_params(_ARB, _ARB))(hq, hf, hi, hg, logits, gain, oraw, ssave, dog, mstack, mstack_t,
                                             masks, eye)


CONV_W = 512


def _per_head(fn, *arrs):
    width = arrs[0].shape[1]
    return jnp.concatenate([fn(*[a[:, j:j + HEAD] for a in arrs]) for j in range(0, width, HEAD)], axis=1)


def _shift_down(xv, halo, d, top_rows):
    if d == 0:
        return xv, xv[0:8]
    main = pltpu.roll(xv, d, 0)
    top = jnp.where(top_rows < d, pltpu.roll(halo, d, 0), main[0:8])
    return main, top


def _conv_parts(x_ref, halo_ref, w_ref, first):
    xv = x_ref[...]
    halo = jnp.where(first, 0.0, halo_ref[...])
    top_rows = lax.broadcasted_iota(jnp.int32, (8, xv.shape[1]), 0)
    shifted = [_shift_down(xv, halo, CONV_K - 1 - j, top_rows) for j in range(CONV_K)]
    w = w_ref[...]
    acc = sum(shifted[j][0] * w[j:j + 1, :] for j in range(CONV_K))
    acc_top = sum(shifted[j][1] * w[j:j + 1, :] for j in range(CONV_K))
    return shifted, acc, acc_top


def _conv_fwd(x, w8, l2scale, name):
    t, width = x.shape
    tr = _pick(t, 512, 8)

    def post(cv):
        s = _silu(cv)
        if l2scale is not None:
            s = _per_head(lambda sh: sh * (lax.rsqrt(_rowsum(sh * sh) + EPS) * l2scale), s)
        return s

    def body(x_ref, halo_ref, w_ref, o_ref):
        _, acc, acc_top = _conv_parts(x_ref, halo_ref, w_ref, pl.program_id(1) == 0)
        o_ref[...] = post(acc)
        o_ref[0:8, :] = post(acc_top)

    return pl.pallas_call(
        body, name=name, grid=(width // CONV_W,t // tr),
        in_specs=[pl.BlockSpec((tr, CONV_W), lambda j, i: (i, j)),
                  pl.BlockSpec((8, CONV_W), lambda j, i: (jnp.maximum(i * (tr // 8) - 1, 0), j)),
                  pl.BlockSpec((8, CONV_W), lambda j, i: (0, j))],
        out_specs=pl.BlockSpec((tr, CONV_W), lambda j, i: (i, j)),
        out_shape=jax.ShapeDtypeStruct((t, width), F32), compiler_params=_params(_PAR, _PAR))(x, x, w8)


def _conv_bwd_a(x, w8, dy, l2scale, name):
    t, width = x.shape
    tr = _pick(t, 512, 8)

    def l2_bwd(s, dyh):
        r = lax.rsqrt(_rowsum(s * s) + EPS)
        y0 = s * r
        dy0 = dyh * l2scale
        return r * (dy0 - y0 * _rowsum(dy0 * y0))

    def to_dc(cv, dyv):
        if l2scale is not None:
            dyv = _per_head(l2_bwd, _silu(cv), dyv)
        return dyv * _dsilu(cv)

    def body(x_ref, halo_ref, w_ref, dy_ref, dc_ref, dw_ref):
        @pl.when(pl.program_id(1) == 0)
        def _():
            dw_ref[...] = jnp.zeros_like(dw_ref)

        shifted, acc, acc_top = _conv_parts(x_ref, halo_ref, w_ref, pl.program_id(1) == 0)
        dyv = dy_ref[...]
        dc = to_dc(acc, dyv)
        dc_top = to_dc(acc_top, dyv[0:8])
        dc_ref[...] = dc
        dc_ref[0:8, :] = dc_top
        rest = (lax.broadcasted_iota(jnp.int32, dc.shape, 0) >= 8).astype(F32)
        dc_rest = dc * rest
        for j in range(CONV_K):
            dw_ref[j:j + 1, :] += (jnp.sum(dc_rest * shifted[j][0], axis=0, keepdims=True)
                                   + jnp.sum(dc_top * shifted[j][1], axis=0, keepdims=True))

    return pl.pallas_call(
        body, name=name, grid=(width // CONV_W,t // tr),
        in_specs=[pl.BlockSpec((tr, CONV_W), lambda j, i: (i, j)),
                  pl.BlockSpec((8, CONV_W), lambda j, i: (jnp.maximum(i * (tr // 8) - 1, 0), j)),
                  pl.BlockSpec((8, CONV_W), lambda j, i: (0, j)),
                  pl.BlockSpec((tr, CONV_W), lambda j, i: (i, j))],
        out_specs=[pl.BlockSpec((tr, CONV_W), lambda j, i: (i, j)), pl.BlockSpec((8, CONV_W), lambda j, i: (0, j))],
        out_shape=[jax.ShapeDtypeStruct((t, width), F32), jax.ShapeDtypeStruct((8, width), F32)],
        compiler_params=_params(_PAR, _ARB))(x, x, w8, dy)


def _conv_bwd_b(dc, w8, name):
    t, width = dc.shape
    tr = _pick(t, 512, 8)
    nt = t // tr

    def body(dc_ref, halo_ref, w_ref, dx_ref):
        dcv = dc_ref[...]
        halo = jnp.where(pl.program_id(1) == nt - 1, 0.0, halo_ref[...])
        w = w_ref[...]
        bot_rows = lax.broadcasted_iota(jnp.int32, (8, CONV_W), 0)
        acc = dcv * w[CONV_K - 1:CONV_K, :]
        acc_bot = dcv[tr - 8:tr] * w[CONV_K - 1:CONV_K, :]
        for d in range(1, CONV_K):
            main = pltpu.roll(dcv, tr - d, 0)
            bot = jnp.where(bot_rows >= 8 - d, pltpu.roll(halo, 8 - d, 0), main[tr - 8:tr])
            wj = w[CONV_K - 1 - d:CONV_K - d, :]
            acc = acc + main * wj
            acc_bot = acc_bot + bot * wj
        dx_ref[...] = acc.astype(BF16)
        dx_ref[tr - 16:tr, :] = jnp.concatenate([acc[tr - 16:tr - 8], acc_bot], axis=0).astype(BF16)

    return pl.pallas_call(
        body, name=name, grid=(width // CONV_W,nt),
        in_specs=[pl.BlockSpec((tr, CONV_W), lambda j, i: (i, j)),
                  pl.BlockSpec((8, CONV_W), lambda j, i: (jnp.minimum((i + 1) * (tr // 8), t // 8 - 1), j)),
                  pl.BlockSpec((8, CONV_W), lambda j, i: (0, j))],
        out_specs=pl.BlockSpec((tr, CONV_W), lambda j, i: (i, j)),
        out_shape=jax.ShapeDtypeStruct((t, width), BF16), compiler_params=_params(_PAR, _PAR))(dc, dc, w8)


def _each(f, *lists):
    return [f(*xs) for xs in zip(*lists)]


def _hp_each(a_list, b_list):
    ah = [_bf(a) for a in a_list]
    bh = [_bf(b) for b in b_list]
    al = [_bf(a - h.astype(F32)) for a, h in zip(a_list, ah)]
    bl = [_bf(b - h.astype(F32)) for b, h in zip(b_list, bh)]
    d1 = [jnp.dot(x, y, preferred_element_type=F32) for x, y in zip(ah, bh)]
    d2 = [jnp.dot(x, y, preferred_element_type=F32) for x, y in zip(ah, bl)]
    d3 = [jnp.dot(x, y, preferred_element_type=F32) for x, y in zip(al, bh)]
    return [x + y + z for x, y, z in zip(d1, d2, d3)]


def _tri_inv_each(a_list, eye):
    ns = [-a for a in a_list]
    ps = [eye + n for n in ns]
    for _ in range(5):
        ns = _hp_each(ns, ns)
        ps = [p + d for p, d in zip(ps, _hp_each(ps, ns))]
    return ps


def _gd_gates(gab, alog, dtb):
    sp_arg = gab + dtb
    return sp_arg, -jnp.exp(alog) * _softplus(sp_arg), _sigmoid(gab)


def _gd_chunks(q, k, v, g_all, beta_all, selg, selb, l_ref, mask_ref):
    incl, strict, eye, upper = mask_ref[0], mask_ref[1], mask_ref[2], mask_ref[3]
    lmat = l_ref[...]
    gb = [_dot_xm(g_all, s) for s in selg]
    bb = [_dot_xm(beta_all, s) for s in selb]
    gam = _mx_each(lmat, gb)
    gam_row = [jnp.sum(x[:, :CHUNK] * upper, axis=0, keepdims=True) for x in gb]
    lm = _each(lambda gm, gr: incl * jnp.exp(jnp.minimum(gm[:, :CHUNK] - gr, 0.0)), gam, gam_row)
    kb = _each(lambda x, b: x * b, k, bb)
    a = _each(lambda x, y, m: strict * _dot_nt(x, y) * m, kb, k, lm)
    tm = _tri_inv_each(a, eye)
    eg = [jnp.exp(x) for x in gam]
    vb = _each(lambda x, b: x * b, v, bb)
    kbg = _each(lambda x, e: x * e, kb, eg)
    u = _each(_dot, tm, vb)
    w = _each(_dot, tm, kbg)
    qk = _each(lambda x, y, m: _dot_nt(x, y) * m, q, k, lm)
    g_end = [x[CHUNK - 1:CHUNK, :] for x in gam]
    ekg = _each(lambda e, x: jnp.exp(e - x), g_end, gam)
    ge = [jnp.exp(e) for e in g_end]
    kg = _each(lambda x, e: x * e, k, ekg)
    qg = _each(lambda x, e: x * e, q, eg)
    names = ("bb", "lm", "kb", "a", "tm", "eg", "vb", "kbg", "u", "w", "qk", "ekg", "ge", "kg", "qg")
    cols = (bb, lm, kb, a, tm, eg, vb, kbg, u, w, qk, ekg, ge, kg, qg)
    return [dict(zip(names, vals)) for vals in zip(*cols)]


def _gd_specs(rows, rev_nb=None):
    def cidx(c):
        return c if rev_nb is None else rev_nb - 1 - c

    qk_tile = pl.BlockSpec((rows, HPS // 2 * HEAD), lambda c, g: (cidx(c), g))
    v_tile = pl.BlockSpec((rows, HPS * HEAD), lambda c, g: (cidx(c), g))
    gab_tile = pl.BlockSpec((rows, HEAD), lambda c, g: (cidx(c), 0))
    return qk_tile, v_tile, gab_tile


def _gdn_fwd(qn, kn, cv, gab, gz, alog, dtb, gain, consts):
    t = qn.shape[0]
    nc = t // CHUNK
    cb = _chunks_per_step(nc)
    rows = cb * CHUNK
    lmat, _, masks, sel = consts
    qk_tile, v_tile, gab_tile = _gd_specs(rows)
    row128 = pl.BlockSpec((1, HEAD), lambda c, h: (0, 0))

    def body(q_ref, k_ref, v_ref, gab_ref, gz_ref, alog_ref, dtb_ref, gain_ref, sel_ref, l_ref, mask_ref,
             oraw_ref, og_ref, ssave_ref, state):
        c = pl.program_id(0)
        g = pl.program_id(1)

        @pl.when(c == 0)
        def _():
            for hh in range(HPS):
                state[g * HPS + hh] = jnp.zeros((HEAD, HEAD), F32)

        alog = alog_ref[...]
        dtb = dtb_ref[...]
        gain_v = gain_ref[...]

        def one(i, carry):
            sl = pl.ds(pl.multiple_of(i * CHUNK, CHUNK), CHUNK)
            _, g_all, beta_all = _gd_gates(gab_ref[sl, :], alog, dtb)
            heads = [g * HPS + hh for hh in range(HPS)]
            lq = [slice(hh // 2 * HEAD, (hh // 2 + 1) * HEAD) for hh in range(HPS)]
            lv = [slice(hh * HEAD, (hh + 1) * HEAD) for hh in range(HPS)]
            chs = _gd_chunks([q_ref[sl, s] for s in lq], [k_ref[sl, s] for s in lq], [v_ref[sl, s] for s in lv],
                             g_all, beta_all, [sel_ref[h] for h in heads], [sel_ref[GD_HEADS + h] for h in heads],
                             l_ref, mask_ref)
            s0 = [state[h] for h in heads]
            v_new = _each(lambda ch, s: ch["u"] - _dot(ch["w"], s), chs, s0)
            o = _each(lambda ch, s, vn: _dot(ch["qg"], s) + _dot(ch["qk"], vn), chs, s0, v_new)
            s1 = _each(lambda ch, s, vn: s * ch["ge"] + _dot_tn(ch["kg"], vn), chs, s0, v_new)
            for hh in range(HPS):
                ssave_ref[i, hh] = s0[hh]
                state[heads[hh]] = s1[hh]
                oraw_ref[sl, lv[hh]] = o[hh]
                r = lax.rsqrt(jnp.mean(o[hh] * o[hh], axis=1, keepdims=True) + EPS)
                og_ref[sl, lv[hh]] = (o[hh] * r * gain_v * _silu(gz_ref[sl, lv[hh]])).astype(BF16)
            return carry

        lax.fori_loop(0, cb, one, 0)

    return pl.pallas_call(
        body, name="gdn_fwd", grid=(nc // cb, GD_HEADS // HPS),
        in_specs=[qk_tile, qk_tile, v_tile, gab_tile, v_tile, row128, row128, row128,
                  pl.BlockSpec(sel.shape, lambda c, g: (0, 0, 0)),
                  pl.BlockSpec(lmat.shape, lambda c, g: (0, 0)),
                  pl.BlockSpec(masks.shape, lambda c, g: (0, 0, 0))],
        out_specs=[v_tile, v_tile, pl.BlockSpec((cb, HPS, HEAD, HEAD), lambda c, g: (c, g, 0, 0))],
        out_shape=[jax.ShapeDtypeStruct((t, GD_HEADS * HEAD), F32), jax.ShapeDtypeStruct((t, GD_HEADS * HEAD), BF16),
                   jax.ShapeDtypeStruct((nc, GD_HEADS, HEAD, HEAD), F32)],
        scratch_shapes=[pltpu.VMEM((GD_HEADS, HEAD, HEAD), F32)],
        compiler_params=_params(_ARB, _ARB))(qn, kn, cv, gab, gz, alog, dtb, gain, sel, lmat, masks)


def _gdn_bwd(qn, kn, cv, gab, gz, alog, dtb, gain, oraw, ssave, dog, consts):
    t = qn.shape[0]
    nc = t // CHUNK
    cb = _chunks_per_step(nc)
    rows = cb * CHUNK
    nb = nc // cb
    lmat, lmat_t, masks, sel = consts
    qk_tile, v_tile, gab_tile = _gd_specs(rows, nb)
    row128 = pl.BlockSpec((1, HEAD), lambda c, h: (0, 0))

    def body(q_ref, k_ref, v_ref, gab_ref, gz_ref, alog_ref, dtb_ref, gain_ref, oraw_ref, ssave_ref, dog_ref,
             sel_ref, l_ref, lt_ref, mask_ref,
             dq_ref, dk_ref, dv_ref, dgab_ref, dgz_ref, small_ref, dstate):
        c = pl.program_id(0)
        g = pl.program_id(1)

        @pl.when(c == 0)
        def _():
            for hh in range(HPS):
                dstate[g * HPS + hh] = jnp.zeros((HEAD, HEAD), F32)

        @pl.when((c == 0) & (g == 0))
        def _():
            small_ref[...] = jnp.zeros_like(small_ref)

        alog = alog_ref[...]
        dtb = dtb_ref[...]
        gain_v = gain_ref[...]
        lane = lax.broadcasted_iota(jnp.int32, (1, HEAD), 1)
        last_row = (lax.broadcasted_iota(jnp.int32, (CHUNK, HEAD), 0) == CHUNK - 1).astype(F32)

        def one(j, carry):
            i = cb - 1 - j
            sl = pl.ds(pl.multiple_of(i * CHUNK, CHUNK), CHUNK)
            sp_arg, g_all, beta_all = _gd_gates(gab_ref[sl, :], alog, dtb)
            strict, eye = mask_ref[1], mask_ref[2]
            ltm = lt_ref[...]
            hs = range(HPS)
            heads = [g * HPS + hh for hh in hs]
            lq = [slice(hh // 2 * HEAD, (hh // 2 + 1) * HEAD) for hh in hs]
            lv = [slice(hh * HEAD, (hh + 1) * HEAD) for hh in hs]
            q = [q_ref[sl, s] for s in lq]
            k = [k_ref[sl, s] for s in lq]
            v = [v_ref[sl, s] for s in lv]
            gzv = [gz_ref[sl, s] for s in lv]
            chs = _gd_chunks(q, k, v, g_all, beta_all, [sel_ref[h] for h in heads],
                             [sel_ref[GD_HEADS + h] for h in heads], l_ref, mask_ref)

            def col(name):
                return [ch[name] for ch in chs]

            def mul(x, y):
                return x * y

            tm, lm, eg, bb = col("tm"), col("lm"), col("eg"), col("bb")
            s0 = [ssave_ref[i, hh] for hh in hs]
            ds = [dstate[h] for h in heads]
            v_new = _each(lambda u, w, s: u - _dot(w, s), col("u"), col("w"), s0)

            o = [oraw_ref[sl, s] for s in lv]
            r = [lax.rsqrt(jnp.mean(x * x, axis=1, keepdims=True) + EPS) for x in o]
            on = _each(mul, o, r)
            dg_out = [dog_ref[sl, s] for s in lv]
            sgate = [_silu(x) for x in gzv]
            for hh in hs:
                dgz_ref[sl, lv[hh]] = (dg_out[hh] * on[hh] * gain_v * _dsilu(gzv[hh])).astype(BF16)
            small_ref[0:1, :] += sum(jnp.sum(d * s * n, axis=0, keepdims=True) for d, s, n in zip(dg_out, sgate, on))
            don = _each(lambda d, s: d * s * gain_v, dg_out, sgate)
            do = _each(lambda rr, dn, n: rr * (dn - n * jnp.mean(dn * n, axis=1, keepdims=True)), r, don, on)

            dv_new = _each(lambda a, d, b, s: _dot_tn(a, d) + _dot(b, s), col("qk"), do, col("kg"), ds)
            dqk = _each(_dot_nt, do, v_new)
            dqg = _each(_dot_nt, do, s0)
            dkg = _each(_dot_nt, v_new, ds)
            dge = _each(lambda s, d: jnp.sum(_rowsum(s * d), axis=0, keepdims=True), s0, ds)
            dw = _each(lambda d, s: -_dot_nt(d, s), dv_new, s0)
            ds_new = _each(lambda qg, d, ge, s, w, dv: _dot_tn(qg, d) + ge * s - _dot_tn(w, dv),
                           col("qg"), do, col("ge"), ds, col("w"), dv_new)
            for hh in hs:
                dstate[heads[hh]] = ds_new[hh]

            dvb = _each(_dot_tn, tm, dv_new)
            dkbg = _each(_dot_tn, tm, dw)
            dtm = _each(lambda dv, vb, d, kbg: _dot_nt(dv, vb) + _dot_nt(d, kbg), dv_new, col("vb"), dw, col("kbg"))
            dtt = _each(_dot_nt, dtm, tm)
            da = _each(lambda t_, x: -_dot_tn(t_, x) * strict, tm, dtt)
            dal = _each(mul, da, lm)
            dkb = _each(lambda x, kk, y, e: _dot(x, kk) + y * e, dal, k, dkbg, eg)
            dqk_l = _each(mul, dqk, lm)
            dq = _each(lambda x, kk, y, e: _dot(x, kk) + y * e, dqk_l, k, dqg, eg)
            dk = _each(lambda x, kb, y, qq, z, ekg, w_, b: _dot_tn(x, kb) + _dot_tn(y, qq) + z * ekg + w_ * b,
                       dal, col("kb"), dqk_l, q, dkg, col("ekg"), dkb, bb)
            gmat = _each(lambda x, a, y, qk: x * a + y * qk, da, col("a"), dqk, col("qk"))
            t_kg = _each(lambda x, y: _rowsum(x * y), dkg, col("kg"))
            dgam = _each(lambda gm, x, qg, t_, y, kbg: (_rowsum(gm) - _row_to_col(jnp.sum(gm, axis=0, keepdims=True), eye)
                                                        + _rowsum(x * qg) - t_ + _rowsum(y * kbg)),
                         gmat, dqg, col("qg"), t_kg, dkbg, col("kbg"))
            dg_end = _each(lambda t_, e, ge: jnp.sum(t_, axis=0, keepdims=True) + e * ge[:, 0:1], t_kg, dge, col("ge"))
            dgam = _each(lambda x, e: x + last_row * e, dgam, dg_end)
            dbeta = _each(lambda x, kk, y, vv: _rowsum(x * kk) + _rowsum(y * vv), dkb, k, dvb, v)
            dg = _mx_each(ltm, dgam)

            for hh in hs:
                dv_ref[sl, lv[hh]] = dvb[hh] * bb[hh]
            fac_g = -jnp.exp(alog) * _sigmoid(sp_arg)
            fac_b = beta_all * (1.0 - beta_all)
            hot_g = [(lane == h).astype(F32) for h in heads]
            hot_b = [(lane == GD_HEADS + h).astype(F32) for h in heads]
            dga = _each(lambda x, hot: x * hot * fac_g, dg, hot_g)
            dgb = _each(lambda x, hot: x * hot * fac_b, dbeta, hot_b)
            small_ref[1:2, :] += sum(jnp.sum(x, axis=0, keepdims=True) for x in dga)
            small_ref[2:3, :] += sum(jnp.sum(x * hot * g_all, axis=0, keepdims=True) for x, hot in zip(dg, hot_g))
            for pair in range(HPS // 2):
                lqp = slice(pair * HEAD, (pair + 1) * HEAD)
                dq_ref[sl, lqp] = dq[2 * pair] + dq[2 * pair + 1]
                dk_ref[sl, lqp] = dk[2 * pair] + dk[2 * pair + 1]
            dgab_ref[sl, :] = sum(a + b for a, b in zip(dga, dgb))
            return carry

        lax.fori_loop(0, cb, one, 0)

    groups = GD_HEADS // HPS
    outs = [jax.ShapeDtypeStruct((t, 1024), F32), jax.ShapeDtypeStruct((t, 1024), F32),
            jax.ShapeDtypeStruct((t, 2048), F32), jax.ShapeDtypeStruct((t, groups * HEAD), F32),
            jax.ShapeDtypeStruct((t, 2048), BF16), jax.ShapeDtypeStruct((8, HEAD), F32)]
    return pl.pallas_call(
        body, name="gdn_bwd", grid=(nb, groups),
        in_specs=[qk_tile, qk_tile, v_tile, gab_tile, v_tile, row128, row128, row128, v_tile,
                  pl.BlockSpec((cb, HPS, HEAD, HEAD), lambda c, g: (nb - 1 - c, g, 0, 0)), v_tile,
                  pl.BlockSpec(sel.shape, lambda c, g: (0, 0, 0)),
                  pl.BlockSpec(lmat.shape, lambda c, g: (0, 0)),
                  pl.BlockSpec(lmat_t.shape, lambda c, g: (0, 0)),
                  pl.BlockSpec(masks.shape, lambda c, g: (0, 0, 0))],
        out_specs=[qk_tile, qk_tile, v_tile, pl.BlockSpec((rows, HEAD), lambda c, g: (nb - 1 - c, g)), v_tile,
                   pl.BlockSpec((8, HEAD), lambda c, g: (0, 0))],
        out_shape=outs, scratch_shapes=[pltpu.VMEM((GD_HEADS, HEAD, HEAD), F32)],
        compiler_params=_params(_ARB, _ARB))(qn, kn, cv, gab, gz, alog, dtb, gain, oraw, ssave, dog, sel,
                                             lmat, lmat_t, masks)


def _fold_groups(wide):
    t, width = wide.shape
    tr = _pick(t, 512, 8)

    def body(w_ref, o_ref):
        acc = w_ref[:, 0:HEAD]
        for j in range(1, width // HEAD):
            acc = acc + w_ref[:, j * HEAD:(j + 1) * HEAD]
        o_ref[...] = acc.astype(BF16)

    return pl.pallas_call(
        body, name="fold_gate_grads", grid=(t // tr,), in_specs=[_row_spec(tr, width)], out_specs=_row_spec(tr, HEAD),
        out_shape=jax.ShapeDtypeStruct((t, HEAD), BF16), compiler_params=_params(_PAR))(wide)


def _adam_math(w, g, m, v):
    m2 = ADAM_B1 * m + (1.0 - ADAM_B1) * g
    v2 = ADAM_B2 * v + (1.0 - ADAM_B2) * (g * g)
    m_hat = m2 / (1.0 - ADAM_B1 ** ADAM_STEP)
    v_hat = v2 / (1.0 - ADAM_B2 ** ADAM_STEP)
    delta = -ADAM_LR * (m_hat / (jnp.sqrt(v_hat) + ADAM_EPS) + ADAM_WD * w)
    return delta, m2, v2


def _adamw(w, g, m, v, name):
    r, c = w.shape
    tr = r
    for cand in range(8, r + 1, 8):
        if r % cand == 0 and cand * c * 4 <= (1 << 20):
            tr = cand
    if r % 8 != 0:
        tr = r

    def body(w_ref, g_ref, m_ref, v_ref, d_ref, m2_ref, v2_ref):
        d, m2, v2 = _adam_math(w_ref[...], g_ref[...], m_ref[...], v_ref[...])
        d_ref[...] = d
        m2_ref[...] = m2
        v2_ref[...] = v2

    spec = pl.BlockSpec((tr, c), lambda i: (i, 0))
    return pl.pallas_call(
        body, name=name, grid=(r // tr,), in_specs=[spec] * 4, out_specs=[spec] * 3,
        out_shape=[jax.ShapeDtypeStruct((r, c), F32)] * 3, compiler_params=_params(_PAR))(w, g, m, v)


_ANY = pl.BlockSpec(memory_space=pl.ANY)


def _place():
    return lax.axis_index("x"), lax.axis_index("y"), lax.axis_index("c")


def _gather_weights(packs, nchs):
    n = len(packs)
    halves = [p.shape[0] // 2 for p in packs]
    base = [sum(nchs[:i]) for i in range(n)]
    total = sum(nchs)
    for p, h, k in zip(packs, halves, nchs):
        assert p.shape[0] == 2 * h and h % k == 0 and (h // k) % 16 == 0

    def body(*refs):
        p_refs, g_refs, (send_sems, recv_sems) = refs[:n], refs[n:2 * n], refs[2 * n:]
        x, y, c = _place()
        sibling = (x, y, 1 - c)
        chips = [(1 - x, y), (x, 1 - y), (1 - x, 1 - y)]
        chunks = [(a, q) for a in range(n) for q in range(nchs[a])]

        def rows_of(a, pc, q):
            ch = halves[a] // nchs[a]
            return pl.ds(pl.multiple_of(pc * halves[a] + q * ch, 16), ch)

        def piece(a, px, py, pc, q):
            return g_refs[a].at[2 * px + py, rows_of(a, pc, q), :]

        def copy(k, src, dst, to):
            return pltpu.make_async_remote_copy(src_ref=src, dst_ref=dst, send_sem=send_sems.at[k],
                                                recv_sem=recv_sems.at[k], device_id=to, device_id_type=MESH)

        def sem_of(j, a, q):
            return j * total + base[a] + q

        first = {(j, a, q): copy(sem_of(j, a, q), p_refs[a].at[rows_of(a, c, q), :], piece(a, x, y, c, q), (*chip, c))
                 for j, chip in enumerate(chips) for a, q in chunks}
        for a, q in chunks:
            for j in range(3):
                first[j, a, q].start()
        passed = {(j, a, q): copy(sem_of(3 + j, a, q), piece(a, *chip, c, q), piece(a, *chip, c, q), sibling)
                  for j, chip in enumerate(chips) for a, q in chunks}
        for a, q in chunks:
            for j, chip in enumerate(chips):
                copy(sem_of(j, a, q), p_refs[a].at[rows_of(a, c, q), :], piece(a, *chip, c, q), (*chip, c)).wait_recv()
                passed[j, a, q].start()
        for a, q in chunks:
            for j, chip in enumerate(chips):
                copy(sem_of(3 + j, a, q), piece(a, *chip, 1 - c, q), piece(a, *chip, 1 - c, q), sibling).wait_recv()
        for key in first:
            first[key].wait_send()
            passed[key].wait_send()

    return pl.pallas_call(
        body, name="gather_weights", out_shape=[jax.ShapeDtypeStruct((4,) + p.shape, p.dtype) for p in packs],
        in_specs=[_ANY] * n, out_specs=[_ANY] * n,
        scratch_shapes=[pltpu.SemaphoreType.DMA((6 * total,)), pltpu.SemaphoreType.DMA((6 * total,))])(*packs)


def _swap_with_sibling(arrs, nchs, lead, name):
    n = len(arrs)
    jobs = []
    for a, (arr, k) in enumerate(zip(arrs, nchs)):
        h = arr.shape[-2]
        assert h % k == 0 and (h // k) % 16 == 0
        for s in (range(lead) if lead else [None]):
            jobs += [(a, s, q * (h // k), h // k) for q in range(k)]

    def body(*refs):
        src, dst, (send_sems, recv_sems) = refs[:n], refs[n:2 * n], refs[2 * n:]
        x, y, c = _place()

        def at(ref, s, r0, rows):
            return ref.at[pl.ds(r0, rows), :] if s is None else ref.at[s, pl.ds(r0, rows), :]

        copies = [pltpu.make_async_remote_copy(
            src_ref=at(src[a], s, r0, rows), dst_ref=at(dst[a], s, r0, rows), send_sem=send_sems.at[k],
            recv_sem=recv_sems.at[k], device_id=(x, y, 1 - c), device_id_type=MESH)
            for k, (a, s, r0, rows) in enumerate(jobs)]
        for cp in copies:
            cp.start()
        for cp in copies:
            cp.wait()

    return pl.pallas_call(
        body, name=name, out_shape=[jax.ShapeDtypeStruct(a.shape, a.dtype) for a in arrs], in_specs=[_ANY] * n,
        out_specs=[_ANY] * n,
        scratch_shapes=[pltpu.SemaphoreType.DMA((len(jobs),)), pltpu.SemaphoreType.DMA((len(jobs),))])(*arrs)


def _add2(a, b, name):
    n, rows, w = a.shape
    tr = _pick(rows, 256, 16)
    spec = pl.BlockSpec((1, tr, w), lambda i, j: (i, j, 0))

    def body(a_ref, b_ref, o_ref):
        o_ref[...] = (a_ref[...].astype(F32) + b_ref[...].astype(F32)).astype(BF16)

    return pl.pallas_call(
        body, name=name, grid=(n, rows // tr), in_specs=[spec, spec], out_specs=spec,
        out_shape=jax.ShapeDtypeStruct(a.shape, BF16), compiler_params=_params(_PAR, _PAR))(a, b)


def _reduce_chips(partials, nchs):
    n = len(partials)
    jobs = []
    for a, (arr, k) in enumerate(zip(partials, nchs)):
        h = arr.shape[1]
        assert h % k == 0 and (h // k) % 16 == 0
        jobs += [(a, q * (h // k), h // k) for q in range(k)]

    def body(*refs):
        src, dst, (send_sems, recv_sems) = refs[:n], refs[n:2 * n], refs[2 * n:]
        x, y, c = _place()
        chips = [(1 - x, y), (x, 1 - y), (1 - x, 1 - y)]
        copies = [pltpu.make_async_remote_copy(
            src_ref=src[a].at[2 * px + py, pl.ds(r0, rows), :], dst_ref=dst[a].at[j, pl.ds(r0, rows), :],
            send_sem=send_sems.at[3 * k + j], recv_sem=recv_sems.at[3 * k + j],
            device_id=(px, py, c), device_id_type=MESH)
            for k, (a, r0, rows) in enumerate(jobs) for j, (px, py) in enumerate(chips)]
        for cp in copies:
            cp.start()
        for cp in copies:
            cp.wait()

    return pl.pallas_call(
        body, name="reduce_chips",
        out_shape=[jax.ShapeDtypeStruct((3,) + p.shape[1:], p.dtype) for p in partials],
        in_specs=[_ANY] * n, out_specs=[_ANY] * n,
        scratch_shapes=[pltpu.SemaphoreType.DMA((3 * len(jobs),)), pltpu.SemaphoreType.DMA((3 * len(jobs),))])(*partials)


def _add4(own, got, name):
    rows, w = own.shape
    tr = _pick(rows, 128, 16)

    def body(a_ref, b_ref, o_ref):
        o_ref[...] = ((a_ref[...].astype(F32) + b_ref[0].astype(F32)) + b_ref[1].astype(F32)) + b_ref[2].astype(F32)

    return pl.pallas_call(
        body, name=name, grid=(rows // tr,),
        in_specs=[pl.BlockSpec((tr, w), lambda i: (i, 0)), pl.BlockSpec((3, tr, w), lambda i: (0, i, 0))],
        out_specs=pl.BlockSpec((tr, w), lambda i: (i, 0)), out_shape=jax.ShapeDtypeStruct((rows, w), F32),
        compiler_params=_params(_PAR))(own, got)


def _small_sync(gs, ws, ms, vs):
    rows = gs.shape[0]
    vmem = pl.BlockSpec(memory_space=pltpu.VMEM)

    def body(g_ref, w_ref, m_ref, v_ref, sum_ref, d_ref, m2_ref, v2_ref, buf, send_sems, recv_sems):
        x, y, c = _place()
        me = 4 * x + 2 * y + c
        buf[me] = g_ref[...]
        copies = []
        for k in range(1, 8):
            peer = (x ^ (k >> 2), y ^ ((k >> 1) & 1), c ^ (k & 1))
            copies.append(pltpu.make_async_remote_copy(
                src_ref=g_ref, dst_ref=buf.at[me], send_sem=send_sems.at[k - 1], recv_sem=recv_sems.at[k - 1],
                device_id=peer, device_id_type=MESH))
        for cp in copies:
            cp.start()
        for cp in copies:
            cp.wait()
        total = buf[0]
        for i in range(1, 8):
            total = total + buf[i]
        sum_ref[...] = total
        d, m2, v2 = _adam_math(w_ref[...], total, m_ref[...], v_ref[...])
        d_ref[...] = d
        m2_ref[...] = m2
        v2_ref[...] = v2

    shape = jax.ShapeDtypeStruct((rows, 128), F32)
    return pl.pallas_call(
        body, name="small_sync", out_shape=[shape] * 4, in_specs=[vmem] * 4, out_specs=[vmem] * 4,
        scratch_shapes=[pltpu.VMEM((8, rows, 128), F32), pltpu.SemaphoreType.DMA((7,)),
                        pltpu.SemaphoreType.DMA((7,))])(gs, ws, ms, vs)


_PACK_COLS = (("ffn1_w_in", 1408), ("ffn2_w_in", 1408), ("w_in", 3080))
_PACK_ROWS = (("ffn1_w_out", 704, 704), ("ffn2_w_out", 704, 704), ("w_branch_hgrn", 256, 256),
              ("w_branch_gdn", 512, 512), ("w_out", 256, 256), ("gdn_conv_w", CONV_K, 128))
_PACK_CHUNKS = (8, 5)
_BIG_NAMES = tuple(n for n, _ in _PACK_COLS) + tuple(n for n, _, _ in _PACK_ROWS)


def _pack(parts, lead):
    ax = len(lead)
    cols = jnp.concatenate([parts[n] for n, _ in _PACK_COLS], axis=ax + 1)
    rows = []
    for n, r, padded in _PACK_ROWS:
        p = parts[n]
        if padded != r:
            p = jnp.concatenate([p, jnp.zeros(lead + (padded - r, p.shape[-1]), p.dtype)], axis=ax)
        rows.append(p)
    return cols, jnp.concatenate(rows, axis=ax)


def _unpack(cols, rows):
    out, off = {}, 0
    for n, w in _PACK_COLS:
        out[n] = cols[..., off:off + w]
        off += w
    off = 0
    for n, r, padded in _PACK_ROWS:
        out[n] = rows[..., off:off + r, :]
        off += padded
    return out


def _is_col_sharded(name):
    return name in ("ffn1_w_in", "ffn2_w_in", "w_in", "gdn_conv_w")


def _full_from_shards(name, g):
    if _is_col_sharded(name):
        return jnp.transpose(g, (1, 0, 2)).reshape(g.shape[1], -1)
    return g.reshape(-1, g.shape[2])


def _shards_from_full(name, full):
    if _is_col_sharded(name):
        return jnp.transpose(full.reshape(full.shape[0], 4, -1), (1, 0, 2))
    return full.reshape(4, -1, full.shape[1])


_SMALL = (("ffn1_norm", 8), ("mix_norm", 8), ("hgrn_lb_logits", 16), ("hgrn_out_norm", 1), ("gdn_a_log", 1),
          ("gdn_dt_bias", 1), ("gdn_out_norm", 1), ("ffn2_norm", 8), ("final_norm", 8), ("loss", 1))
_SMALL_ROWS = 56


def _pack_small(parts):
    out = []
    for name, rows in _SMALL:
        p = parts[name].reshape(-1).astype(F32)
        p = jnp.concatenate([p, jnp.zeros((rows * 128 - p.shape[0],), F32)]) if p.shape[0] != rows * 128 else p
        out.append(p.reshape(rows, 128))
    used = sum(r for _, r in _SMALL)
    out.append(jnp.zeros((_SMALL_ROWS - used, 128), F32))
    return jnp.concatenate(out, axis=0)


def _unpack_small(packed, shapes):
    out, off = {}, 0
    for name, rows in _SMALL:
        n = int(np.prod(shapes[name]))
        out[name] = packed[off:off + rows].reshape(-1)[:n].reshape(shapes[name])
        off += rows
    return out


def _ffn_fwd(x, gain, w_in, w_out, tag):
    n = _rmsnorm_fwd(x, gain, tag + "_norm")
    ab = _mm(n, w_in, out_dtype=BF16, name=tag + "_in")
    hm = _swiglu_fwd(ab, tag + "_act")
    out = _mm(hm, w_out, alpha=0.5, res=x, name=tag + "_out")
    return out, (n, ab)


def _ffn_bwd(x, gain, w_in, w_out, saved, dout, dout_bf, tag):
    n, ab = saved
    dhm = _mm(dout_bf, w_out, tb=True, alpha=0.5, name=tag + "_dact")
    dab, hm = _swiglu_bwd(ab, dhm, tag + "_dswiglu")
    dw_out = _mm(hm, dout_bf, ta=True, alpha=0.5, out_dtype=BF16, name=tag + "_dwout")
    dw_in = _mm(n, dab, ta=True, out_dtype=BF16, name=tag + "_dwin")
    dn = _mm(dab, w_in, tb=True, name=tag + "_dnorm")
    dx, dx_bf, dgain = _rmsnorm_bwd(x, gain, dn, dout, tag + "_dx")
    return dx, dx_bf, dgain, dw_in, dw_out


def _pad_lanes(v):
    return jnp.concatenate([v.reshape(1, -1), jnp.zeros((1, HEAD - v.size), F32)], axis=1)


def _local_step(x, tgt, w, small):
    hg_c = _hg_consts()
    gd_c = _gd_consts()
    seg, off = {}, 0
    for name, size in zip(IN_NAMES, IN_SIZES):
        seg[name] = w["w_in"][:, off:off + size]
        off += size
    w_gab = jnp.concatenate([seg["ga"], seg["gb"], jnp.zeros((D_MODEL, HEAD - 32), BF16)], axis=1)
    big_segs = [n for n in IN_NAMES if n not in ("ga", "gb")]
    conv8 = jnp.concatenate([w["gdn_conv_w"].astype(F32), jnp.zeros((8 - CONV_K, 4096), F32)], axis=0)
    conv_q, conv_k, conv_v = conv8[:, :1024], conv8[:, 1024:2048], conv8[:, 2048:]
    alog = _pad_lanes(small["gdn_a_log"])
    dtb = _pad_lanes(small["gdn_dt_bias"])
    logits = small["hgrn_lb_logits"]
    hg_gain = small["hgrn_out_norm"].reshape(1, HEAD)
    gd_gain = small["gdn_out_norm"].reshape(1, HEAD)
    g1, gm, g2 = small["ffn1_norm"].reshape(1, -1), small["mix_norm"].reshape(1, -1), small["ffn2_norm"].reshape(1, -1)
    gf = small["final_norm"].reshape(1, -1)
    qscale = HEAD ** -0.5

    h1, ffn1_saved = _ffn_fwd(x, g1, w["ffn1_w_in"], w["ffn1_w_out"], "ffn1")
    u = _rmsnorm_fwd(h1, gm, "mix_norm")
    pr = {n: _mm(u, seg[n], name="proj_" + n) for n in big_segs}
    gab = _mm(u, w_gab, name="proj_gab")
    oh_raw, oh, s_h = _hgrn_fwd(pr["hq"], pr["hf"], pr["hi"], pr["hg"], logits, hg_gain, hg_c)
    qn = _conv_fwd(pr["gq"], conv_q, qscale, "conv_q")
    kn = _conv_fwd(pr["gk"], conv_k, 1.0, "conv_k")
    cv = _conv_fwd(pr["gv"], conv_v, None, "conv_v")
    og_raw, og, s_g = _gdn_fwd(qn, kn, cv, gab, pr["gz"], alog, dtb, gd_gain, gd_c)
    yh = _mm(oh, w["w_branch_hgrn"], name="branch_h")
    yg = _mm(og, w["w_branch_gdn"], name="branch_g")
    ym = _merge_fwd(yh, yg, pr["gate_h"], pr["gate_g"])
    h2 = _mm(ym, w["w_out"], res=h1, name="mix_out")
    h3, ffn2_saved = _ffn_fwd(h2, g2, w["ffn2_w_in"], w["ffn2_w_out"], "ffn2")
    loss, dh3, dh3_bf, d_gf = _final_loss(h3, gf, tgt)

    dh2, dh2_bf, d_g2, d_f2in, d_f2out = _ffn_bwd(h2, g2, w["ffn2_w_in"], w["ffn2_w_out"], ffn2_saved, dh3, dh3_bf,
                                                  "ffn2")
    dym = _mm(dh2_bf, w["w_out"], tb=True, name="d_merge")
    d_wout = _mm(ym, dh2_bf, ta=True, out_dtype=BF16, name="d_w_out")
    dyh, dyg, d_gate_h, d_gate_g = _merge_bwd(dym, yh, yg, pr["gate_h"], pr["gate_g"])
    d_wbh = _mm(oh, dyh, ta=True, out_dtype=BF16, name="d_w_branch_h")
    d_wbg = _mm(og, dyg, ta=True, out_dtype=BF16, name="d_w_branch_g")
    doh = _mm(dyh, w["w_branch_hgrn"], tb=True, name="d_oh")
    dog = _mm(dyg, w["w_branch_gdn"], tb=True, name="d_og")
    d_hq, d_hf, d_hi, d_hg, d_hg_gain, d_lb0 = _hgrn_bwd(pr["hq"], pr["hf"], pr["hi"], pr["hg"], logits, hg_gain,
                                                        oh_raw, s_h, doh, hg_c)
    d_qn, d_kn, d_cv, d_gab_wide, d_gz, gd_small = _gdn_bwd(qn, kn, cv, gab, pr["gz"], alog, dtb, gd_gain, og_raw,
                                                            s_g, dog, gd_c)
    d_gab = _fold_groups(d_gab_wide)
    dc_q, dwc_q = _conv_bwd_a(pr["gq"], conv_q, d_qn, qscale, "dconv_q")
    dc_k, dwc_k = _conv_bwd_a(pr["gk"], conv_k, d_kn, 1.0, "dconv_k")
    dc_v, dwc_v = _conv_bwd_a(pr["gv"], conv_v, d_cv, None, "dconv_v")
    d_gq = _conv_bwd_b(dc_q, conv_q, "dconvx_q")
    d_gk = _conv_bwd_b(dc_k, conv_k, "dconvx_k")
    d_gv = _conv_bwd_b(dc_v, conv_v, "dconvx_v")
    dpr = {"hq": d_hq, "hf": d_hf, "hi": d_hi, "hg": d_hg, "gq": d_gq, "gk": d_gk, "gv": d_gv, "gz": d_gz,
           "gate_h": d_gate_h, "gate_g": d_gate_g}
    du = _mm(d_gab, w_gab, tb=True, name="du_gab")
    d_wseg = {}
    for n in big_segs:
        du = _mm(dpr[n], seg[n], tb=True, res=du, name="du_" + n)
        d_wseg[n] = _mm(u, dpr[n], ta=True, out_dtype=BF16, name="dw_" + n)
    d_wgab = _mm(u, d_gab, ta=True, out_dtype=BF16, name="dw_gab")
    d_wseg["ga"], d_wseg["gb"] = d_wgab[:, :16], d_wgab[:, 16:32]
    d_win = jnp.concatenate([d_wseg[n] for n in IN_NAMES], axis=1)
    dh1, dh1_bf, d_gm = _rmsnorm_bwd(h1, gm, du, dh2, "mix_dnorm")
    dx, _, d_g1, d_f1in, d_f1out = _ffn_bwd(x, g1, w["ffn1_w_in"], w["ffn1_w_out"], ffn1_saved, dh1, dh1_bf, "ffn1")

    d_conv = jnp.concatenate([dwc_q[:CONV_K], dwc_k[:CONV_K], dwc_v[:CONV_K]], axis=1).astype(BF16)
    big = {"ffn1_w_in": d_f1in, "ffn1_w_out": d_f1out, "w_in": d_win, "gdn_conv_w": d_conv,
           "w_branch_hgrn": d_wbh, "w_branch_gdn": d_wbg, "w_out": d_wout, "ffn2_w_in": d_f2in, "ffn2_w_out": d_f2out}
    d_lb0 = d_lb0.reshape(1, -1)
    sm = {"ffn1_norm": d_g1, "mix_norm": d_gm, "hgrn_lb_logits": jnp.concatenate([d_lb0, -d_lb0], axis=0),
          "hgrn_out_norm": d_hg_gain, "gdn_a_log": gd_small[2, :16], "gdn_dt_bias": gd_small[1, :16],
          "gdn_out_norm": gd_small[0], "ffn2_norm": d_g2, "final_norm": d_gf, "loss": loss[0, :1]}
    return dx, big, sm


_WEIGHTS = ("ffn1_norm", "ffn1_w_in", "ffn1_w_out", "mix_norm", "w_in", "hgrn_lb_logits", "hgrn_out_norm",
            "gdn_conv_w", "gdn_a_log", "gdn_dt_bias", "gdn_out_norm", "w_branch_hgrn", "w_branch_gdn", "w_out",
            "ffn2_norm", "ffn2_w_in", "ffn2_w_out", "final_norm")


def kernel(x, ffn1_norm, ffn1_w_in, ffn1_w_out, mix_norm, w_in, hgrn_lb_logits, hgrn_out_norm, gdn_conv_w, gdn_a_log, gdn_dt_bias, gdn_out_norm, w_branch_hgrn, w_branch_gdn, w_out, ffn2_norm, ffn2_w_in, ffn2_w_out, final_norm, loss_target, m_ffn1_norm, m_ffn1_w_in, m_ffn1_w_out, m_mix_norm, m_w_in, m_hgrn_lb_logits, m_hgrn_out_norm, m_gdn_conv_w, m_gdn_a_log, m_gdn_dt_bias, m_gdn_out_norm, m_w_branch_hgrn, m_w_branch_gdn, m_w_out, m_ffn2_norm, m_ffn2_w_in, m_ffn2_w_out, m_final_norm, v_ffn1_norm, v_ffn1_w_in, v_ffn1_w_out, v_mix_norm, v_w_in, v_hgrn_lb_logits, v_hgrn_out_norm, v_gdn_conv_w, v_gdn_a_log, v_gdn_dt_bias, v_gdn_out_norm, v_w_branch_hgrn, v_w_branch_gdn, v_w_out, v_ffn2_norm, v_ffn2_w_in, v_ffn2_w_out, v_final_norm):
    args = dict(locals())
    wts = {n: args[n] for n in _WEIGHTS}
    moms = {n: args["m_" + n] for n in _WEIGHTS}
    vars_ = {n: args["v_" + n] for n in _WEIGHTS}

    packs = _pack({n: wts[n][0].astype(BF16) for n in _BIG_NAMES}, ())
    xi, yi, ci = lax.axis_index("x"), lax.axis_index("y"), lax.axis_index("c")
    chip = 2 * xi + yi
    others = _gather_weights(packs, _PACK_CHUNKS)
    gathered = _unpack(*[lax.dynamic_update_index_in_dim(g, p, chip, 0) for g, p in zip(others, packs)])
    full = {n: _full_from_shards(n, gathered[n]) for n in _BIG_NAMES}
    small = {n: wts[n].astype(F32) for n in _WEIGHTS if n not in _BIG_NAMES}

    dx, big_grads, small_grads = _local_step(x[0], loss_target[0], full, small)

    gpacks = _pack({n: _shards_from_full(n, big_grads[n]) for n in _BIG_NAMES}, (4,))
    halves = [g.shape[1] // 2 for g in gpacks]
    own = [lax.dynamic_slice_in_dim(g, ci * h, h, axis=1) for g, h in zip(gpacks, halves)]
    give = [lax.dynamic_slice_in_dim(g, (1 - ci) * h, h, axis=1) for g, h in zip(gpacks, halves)]
    got = _swap_with_sibling(give, _PACK_CHUNKS, 4, "reduce_pair")
    chip_sums = [_add2(a, b, "add_pair_%d" % i) for i, (a, b) in enumerate(zip(own, got))]
    from_chips = _reduce_chips(chip_sums, _PACK_CHUNKS)
    mine = [_add4(lax.dynamic_index_in_dim(s, chip, axis=0, keepdims=False), f, "add_chips_%d" % i)
            for i, (s, f) in enumerate(zip(chip_sums, from_chips))]
    theirs = _swap_with_sibling(mine, _PACK_CHUNKS, 0, "share_pair")
    south = ci == 0
    reduced = _unpack(*[jnp.concatenate([jnp.where(south, a, b), jnp.where(south, b, a)], axis=0)
                        for a, b in zip(mine, theirs)])

    out_g, out_d, out_m, out_v = {}, {}, {}, {}
    for n in _BIG_NAMES:
        shape = wts[n].shape
        w2 = wts[n].reshape(shape[-2], shape[-1])
        g2 = reduced[n]
        d, m2, v2 = _adamw(w2, g2, moms[n].reshape(w2.shape), vars_[n].reshape(w2.shape), "adamw_" + n)
        out_g[n], out_d[n], out_m[n], out_v[n] = g2.reshape(shape), d.reshape(shape), m2.reshape(shape), v2.reshape(shape)

    small_names = [n for n, _ in _SMALL]
    zero = jnp.zeros((1,), F32)
    shapes = {n: (wts[n].shape if n != "loss" else (1,)) for n in small_names}
    sums, sd, sm_, sv = _small_sync(
        _pack_small(small_grads),
        _pack_small({n: (wts[n] if n != "loss" else zero) for n in small_names}),
        _pack_small({n: (moms[n] if n != "loss" else zero) for n in small_names}),
        _pack_small({n: (vars_[n] if n != "loss" else zero) for n in small_names}))
    sg_u, sd_u, sm_u, sv_u = (_unpack_small(p, shapes) for p in (sums, sd, sm_, sv))
    for n in small_names:
        if n != "loss":
            out_g[n], out_d[n], out_m[n], out_v[n] = sg_u[n], sd_u[n], sm_u[n], sv_u[n]
    loss = sg_u["loss"].reshape(())

    return (loss, dx[None], *[out_g[n] for n in _WEIGHTS], *[out_d[n] for n in _WEIGHTS],
            *[out_m[n] for n in _WEIGHTS], *[out_v[n] for n in _WEIGHTS])
```

```python
import numpy as np

import jax
import jax.numpy as jnp
from jax import lax
from jax.experimental import pallas as pl
from jax.experimental.pallas import tpu as pltpu

F32 = jnp.float32
BF16 = jnp.bfloat16

D_MODEL = 1024
D_FF = 2816
CHUNK = 64
HEAD = 128
HG_HEADS = 8
GD_HEADS = 16
HPS = 8
COMM_CHUNKS = 9
MM_TM = 1408
MM_TN = 512
MM_TK = 1536
VMEM_LIMIT = 48 * 1024 * 1024
EPS = 1e-6
CONV_K = 4
IN_NAMES = ("hq", "hf", "hi", "hg", "gq", "gk", "gv", "ga", "gb", "gz", "gate_h", "gate_g")
IN_SIZES = (1024, 1024, 1024, 1024, 1024, 1024, 2048, 16, 16, 2048, 1024, 1024)
IN_WIDTH = sum(IN_SIZES)

ADAM_LR = 0.001
ADAM_B1 = 0.9
ADAM_B2 = 0.999
ADAM_EPS = 1e-08
ADAM_WD = 0.01
ADAM_STEP = 10

MESH = pl.DeviceIdType.MESH
_ARB = "arbitrary"
_PAR = "parallel"


def _bf(x):
    return x.astype(BF16)


def _dot(a, b):
    return jnp.dot(_bf(a), _bf(b), preferred_element_type=F32)


def _dot_nt(a, b):
    return lax.dot_general(_bf(a), _bf(b), (((1,), (1,)), ((), ())), preferred_element_type=F32)


def _dot_tn(a, b):
    return lax.dot_general(_bf(a), _bf(b), (((0,), (0,)), ((), ())), preferred_element_type=F32)


def _split3(x):
    hi = _bf(x)
    r = x - hi.astype(F32)
    mid = _bf(r)
    lo = _bf(r - mid.astype(F32))
    return hi, mid, lo


def _dot_mx(m, x):
    hi, mid, lo = _split3(x)
    return (jnp.dot(m, hi, preferred_element_type=F32) + jnp.dot(m, mid, preferred_element_type=F32)
            + jnp.dot(m, lo, preferred_element_type=F32))


def _dot_xm(x, m):
    hi, mid, lo = _split3(x)
    return (jnp.dot(hi, m, preferred_element_type=F32) + jnp.dot(mid, m, preferred_element_type=F32)
            + jnp.dot(lo, m, preferred_element_type=F32))


def _dot_hp(a, b):
    ah = _bf(a)
    al = _bf(a - ah.astype(F32))
    bh = _bf(b)
    bl = _bf(b - bh.astype(F32))
    return (jnp.dot(ah, bh, preferred_element_type=F32) + jnp.dot(ah, bl, preferred_element_type=F32)
            + jnp.dot(al, bh, preferred_element_type=F32))


def _sigmoid(x):
    return jax.nn.sigmoid(x)


def _silu(x):
    return x * _sigmoid(x)


def _dsilu(x):
    s = _sigmoid(x)
    return s * (1.0 + x * (1.0 - s))


def _softplus(x):
    return jnp.maximum(x, 0.0) + jnp.log(1.0 + jnp.exp(-jnp.abs(x)))


def _rowsum(x):
    return jnp.sum(x, axis=1, keepdims=True)


def _col_to_row(col, eye):
    return jnp.sum(eye * col, axis=0, keepdims=True)


def _row_to_col(row, eye):
    return jnp.sum(eye * row, axis=1, keepdims=True)


def _pick(dim, pref, unit=128):
    if dim <= pref:
        return dim
    t = pref
    while t >= unit:
        if dim % t == 0:
            return t
        t -= unit
    return dim


def _params(*sem):
    return pltpu.CompilerParams(dimension_semantics=tuple(sem), vmem_limit_bytes=VMEM_LIMIT)


def _mm(a, b, *, ta=False, tb=False, alpha=1.0, res=None, out_dtype=F32, name="mm"):
    m = a.shape[1] if ta else a.shape[0]
    k = a.shape[0] if ta else a.shape[1]
    n = b.shape[0] if tb else b.shape[1]
    assert k == (b.shape[1] if tb else b.shape[0])
    tm, tn, tk = _pick(m, MM_TM), _pick(n, MM_TN), _pick(k, MM_TK)
    nk = k // tk
    a_spec = pl.BlockSpec((tk, tm), lambda i, j, l: (l, i)) if ta else pl.BlockSpec((tm, tk), lambda i, j, l: (i, l))
    b_spec = pl.BlockSpec((tn, tk), lambda i, j, l: (j, l)) if tb else pl.BlockSpec((tk, tn), lambda i, j, l: (l, j))
    o_spec = pl.BlockSpec((tm, tn), lambda i, j, l: (i, j))
    dims = (((0 if ta else 1,), (1 if tb else 0,)), ((), ()))
    has_res = res is not None

    def finish(r, r_ref, o_ref):
        if alpha != 1.0:
            r = r * alpha
        if has_res:
            r = r + r_ref[...]
        o_ref[...] = r.astype(out_dtype)

    def body(*refs):
        a_ref, b_ref = refs[0], refs[1]
        r_ref = refs[2] if has_res else None
        o_ref = refs[3] if has_res else refs[2]
        part = lax.dot_general(_bf(a_ref[...]), _bf(b_ref[...]), dims, preferred_element_type=F32)
        if nk == 1:
            finish(part, r_ref, o_ref)
            return
        acc = refs[-1]
        step = pl.program_id(2)

        @pl.when(step == 0)
        def _():
            acc[...] = part

        @pl.when(step != 0)
        def _():
            acc[...] += part

        @pl.when(step == nk - 1)
        def _():
            finish(acc[...], r_ref, o_ref)

    ins = [a, b] + ([res] if has_res else [])
    in_specs = [a_spec, b_spec] + ([o_spec] if has_res else [])
    return pl.pallas_call(
        body, name=name, grid=(m // tm, n // tn, nk), in_specs=in_specs, out_specs=o_spec,
        out_shape=jax.ShapeDtypeStruct((m, n), out_dtype),
        scratch_shapes=[pltpu.VMEM((tm, tn), F32)] if nk > 1 else [],
        compiler_params=_params(_PAR, _PAR, _ARB))(*ins)


def _row_spec(tr, w):
    return pl.BlockSpec((tr, w), lambda i: (i, 0))


def _full_spec(shape):
    return pl.BlockSpec(shape, lambda i: tuple(0 for _ in shape))


def _view(arr, off, width):
    return arr, off, width


def _view_rows(view, tr):
    _, off, width = view
    assert off % width == 0
    return pl.BlockSpec((tr, width), lambda i: (i, off // width))


def _view_tile(view, rows, bw, cidx=lambda c: c):
    _, off, width = view
    assert off % bw == 0 and width % bw == 0
    return pl.BlockSpec((rows, bw), lambda c, g: (cidx(c), off // bw + g))


def _rmsnorm_fwd(x, g, name):
    t, d = x.shape
    tr = _pick(t, 256, 8)

    def body(x_ref, g_ref, o_ref):
        xv = x_ref[...]
        r = lax.rsqrt(jnp.mean(xv * xv, axis=1, keepdims=True) + EPS)
        o_ref[...] = (xv * r * g_ref[...]).astype(BF16)

    return pl.pallas_call(
        body, name=name, grid=(t // tr,), in_specs=[_row_spec(tr, d), _full_spec((1, d))],
        out_specs=_row_spec(tr, d), out_shape=jax.ShapeDtypeStruct((t, d), BF16),
        compiler_params=_params(_PAR))(x, g)


def _rmsnorm_bwd(x, g, dn, res, name):
    t, d = x.shape
    tr = _pick(t, 256, 8)

    def body(x_ref, g_ref, dn_ref, r_ref, dx_ref, dxb_ref, dg_ref):
        @pl.when(pl.program_id(0) == 0)
        def _():
            dg_ref[...] = jnp.zeros_like(dg_ref)

        xv = x_ref[...]
        r = lax.rsqrt(jnp.mean(xv * xv, axis=1, keepdims=True) + EPS)
        xh = xv * r
        dy = dn_ref[...]
        dg_ref[...] += jnp.sum(dy * xh, axis=0, keepdims=True)
        dxh = dy * g_ref[...]
        dx = r_ref[...] + r * (dxh - xh * jnp.mean(dxh * xh, axis=1, keepdims=True))
        dx_ref[...] = dx
        dxb_ref[...] = dx.astype(BF16)

    return pl.pallas_call(
        body, name=name, grid=(t // tr,),
        in_specs=[_row_spec(tr, d), _full_spec((1, d)), _row_spec(tr, d), _row_spec(tr, d)],
        out_specs=[_row_spec(tr, d), _row_spec(tr, d), _full_spec((1, d))],
        out_shape=[jax.ShapeDtypeStruct((t, d), F32), jax.ShapeDtypeStruct((t, d), BF16),
                   jax.ShapeDtypeStruct((1, d), F32)],
        compiler_params=_params(_ARB))(x, g, dn, res)


def _swiglu_fwd(ab, name):
    t = ab.shape[0]
    tr = _pick(t, 256, 8)

    def body(ab_ref, o_ref):
        a = ab_ref[:, :D_FF].astype(F32)
        b = ab_ref[:, D_FF:].astype(F32)
        o_ref[...] = (_silu(a) * b).astype(BF16)

    return pl.pallas_call(
        body, name=name, grid=(t // tr,), in_specs=[_row_spec(tr, 2 * D_FF)], out_specs=_row_spec(tr, D_FF),
        out_shape=jax.ShapeDtypeStruct((t, D_FF), BF16), compiler_params=_params(_PAR))(ab)


def _swiglu_bwd(ab, dhm, name):
    t = ab.shape[0]
    tr = _pick(t, 256, 8)

    def body(ab_ref, dh_ref, dab_ref, hm_ref):
        a = ab_ref[:, :D_FF].astype(F32)
        b = ab_ref[:, D_FF:].astype(F32)
        dh = dh_ref[...]
        dab_ref[:, :D_FF] = (dh * b * _dsilu(a)).astype(BF16)
        sa = _silu(a)
        dab_ref[:, D_FF:] = (dh * sa).astype(BF16)
        hm_ref[...] = (sa * b).astype(BF16)

    return pl.pallas_call(
        body, name=name, grid=(t // tr,), in_specs=[_row_spec(tr, 2 * D_FF), _row_spec(tr, D_FF)],
        out_specs=[_row_spec(tr, 2 * D_FF), _row_spec(tr, D_FF)],
        out_shape=[jax.ShapeDtypeStruct((t, 2 * D_FF), BF16), jax.ShapeDtypeStruct((t, D_FF), BF16)],
        compiler_params=_params(_PAR))(ab, dhm)


def _merge_fwd(yh, yg, gh, gg):
    t, d = yh.shape
    tr = _pick(t, 256, 8)

    def body(yh_ref, yg_ref, gh_ref, gg_ref, o_ref):
        o_ref[...] = (_sigmoid(gh_ref[...]) * yh_ref[...] + _sigmoid(gg_ref[...]) * yg_ref[...]).astype(BF16)

    return pl.pallas_call(
        body, name="merge_fwd", grid=(t // tr,),
        in_specs=[_row_spec(tr, d), _row_spec(tr, d), _view_rows(gh, tr), _view_rows(gg, tr)],
        out_specs=_row_spec(tr, d),
        out_shape=jax.ShapeDtypeStruct((t, d), BF16), compiler_params=_params(_PAR))(yh, yg, gh[0], gg[0])


def _merge_bwd(dy, yh, yg, gh, gg):
    t, d = yh.shape
    tr = _pick(t, 256, 8)

    def body(dy_ref, yh_ref, yg_ref, gh_ref, gg_ref, dyh_ref, dyg_ref, dgh_ref, dgg_ref):
        dyv = dy_ref[...]
        sh = _sigmoid(gh_ref[...])
        sg = _sigmoid(gg_ref[...])
        dyh_ref[...] = (dyv * sh).astype(BF16)
        dyg_ref[...] = (dyv * sg).astype(BF16)
        dgh_ref[...] = (dyv * yh_ref[...] * sh * (1.0 - sh)).astype(BF16)
        dgg_ref[...] = (dyv * yg_ref[...] * sg * (1.0 - sg)).astype(BF16)

    return pl.pallas_call(
        body, name="merge_bwd", grid=(t // tr,),
        in_specs=[_row_spec(tr, d)] * 3 + [_view_rows(gh, tr), _view_rows(gg, tr)], out_specs=[_row_spec(tr, d)] * 4,
        out_shape=[jax.ShapeDtypeStruct((t, d), BF16)] * 4,
        compiler_params=_params(_PAR))(dy, yh, yg, gh[0], gg[0])


def _final_loss(h, g, tgt):
    t, d = h.shape
    tr = _pick(t, 256, 8)

    def body(h_ref, g_ref, t_ref, loss_ref, dh_ref, dhb_ref, dg_ref):
        @pl.when(pl.program_id(0) == 0)
        def _():
            dg_ref[...] = jnp.zeros_like(dg_ref)
            loss_ref[...] = jnp.zeros_like(loss_ref)

        xv = h_ref[...]
        gv = g_ref[...]
        r = lax.rsqrt(jnp.mean(xv * xv, axis=1, keepdims=True) + EPS)
        xh = xv * r
        err = xh * gv - t_ref[...]
        loss_ref[...] += 0.5 * jnp.sum(jnp.mean(err * err, axis=1, keepdims=True), axis=0, keepdims=True)
        dy = err * (1.0 / d)
        dg_ref[...] += jnp.sum(dy * xh, axis=0, keepdims=True)
        dxh = dy * gv
        dh = r * (dxh - xh * jnp.mean(dxh * xh, axis=1, keepdims=True))
        dh_ref[...] = dh
        dhb_ref[...] = dh.astype(BF16)

    return pl.pallas_call(
        body, name="final_loss", grid=(t // tr,),
        in_specs=[_row_spec(tr, d), _full_spec((1, d)), _row_spec(tr, d)],
        out_specs=[_full_spec((1, 128)), _row_spec(tr, d), _row_spec(tr, d), _full_spec((1, d))],
        out_shape=[jax.ShapeDtypeStruct((1, 128), F32), jax.ShapeDtypeStruct((t, d), F32),
                   jax.ShapeDtypeStruct((t, d), BF16), jax.ShapeDtypeStruct((1, d), F32)],
        compiler_params=_params(_ARB))(h, g, tgt)


def _hg_consts():
    c = CHUNK
    t = np.arange(c)
    mats, masks = [], []
    for lvl in range(6):
        m = 1 << lvl
        blk = t // m
        mat = np.zeros((c, c), np.float32)
        for tt in range(c):
            b = blk[tt]
            if b % 2 == 1:
                mat[tt, b * m:tt + 1] = 1.0
            else:
                mat[tt, tt + 1:(b + 1) * m] = 1.0
        mats.append(mat)
        same = (t[:, None] // (2 * m)) == (t[None, :] // (2 * m))
        masks.append((same & (blk[:, None] % 2 == 1) & (blk[None, :] % 2 == 0)).astype(np.float32))
    pre = np.tril(np.ones((c, c), np.float32))
    suf = np.triu(np.ones((c, c), np.float32), 1)
    mstack = np.concatenate(mats + [pre, suf], 0)
    masks.append(np.eye(c, dtype=np.float32))
    return (jnp.asarray(mstack, BF16), jnp.asarray(mstack.T.copy(), BF16), jnp.asarray(np.stack(masks), F32),
            jnp.asarray(np.eye(HEAD, dtype=np.float32)))


def _gd_consts():
    c = CHUNK
    incl = np.tril(np.ones((c, c), np.float32))
    strict = np.tril(np.ones((c, c), np.float32), -1)
    eye = np.eye(c, dtype=np.float32)
    masks = np.stack([incl, strict, eye, incl.T.copy()])
    sel = np.zeros((GD_HEADS, HEAD, 2 * HEAD), np.float32)
    for j in range(GD_HEADS):
        sel[j, j, :HEAD] = 1.0
        sel[j, GD_HEADS + j, HEAD:] = 1.0
    return (jnp.asarray(incl, BF16), jnp.asarray(incl.T.copy(), BF16), jnp.asarray(masks, F32), jnp.asarray(sel, BF16))


def _chunks_per_step(nc):
    for cb in (32 // HPS, 2, 1):
        if nc % cb == 0:
            return cb
    return 1


def _hg_prep(hq, hf, lg):
    lb = _sigmoid(lg[0:1, :] - lg[1:2, :])
    sg = _sigmoid(hf)
    sgn = _sigmoid(-hf)
    f = lb + (1.0 - lb) * sg
    lf = jnp.log(f)
    kk = (1.0 - lb) * sgn
    q = _silu(hq) * (HEAD ** -0.5)
    return lb, sg, sgn, f, lf, kk, q


def _mx_each(m, xs):
    wide = [jnp.concatenate(_split3(x), axis=1) for x in xs]
    prods = [jnp.dot(m, w, preferred_element_type=F32) for w in wide]
    return [p[:, :HEAD] + p[:, HEAD:2 * HEAD] + p[:, 2 * HEAD:] for p in prods]


def _hg_scores(q, kk, ex, mask_ref):
    p = [mask_ref[6] * _rowsum(a * b) for a, b in zip(q, kk)]
    for lvl in range(6):
        el = [e[lvl * CHUNK:(lvl + 1) * CHUNK] for e in ex]
        d = [_dot_nt(a * e, b * e) for a, b, e in zip(q, kk, el)]
        p = [x + mask_ref[lvl] * y for x, y in zip(p, d)]
    return p


def _hgrn_fwd(hq, hf, hi, hg, logits, gain, consts):
    t = hq[0].shape[0]
    nc = t // CHUNK
    cb = _chunks_per_step(nc)
    rows = cb * CHUNK
    mstack, _, masks, eye = consts
    tile = pl.BlockSpec((rows, HPS * HEAD), lambda c, g: (c, g))

    def body(hq_ref, hf_ref, hi_ref, hg_ref, lg_ref, gain_ref, m_ref, mask_ref, eye_ref,
             oraw_ref, og_ref, ssave_ref, state):
        c = pl.program_id(0)
        g = pl.program_id(1)

        @pl.when(c == 0)
        def _():
            for hh in range(HPS):
                state[g * HPS + hh] = jnp.zeros((HEAD, HEAD), F32)

        lg_all = lg_ref[...]
        gain_v = gain_ref[...]

        def one(i, carry):
            sl = pl.ds(pl.multiple_of(i * CHUNK, CHUNK), CHUNK)
            hs = range(HPS)
            heads = [g * HPS + hh for hh in hs]
            ln = [slice(hh * HEAD, (hh + 1) * HEAD) for hh in hs]
            preps = [_hg_prep(hq_ref[sl, s], hf_ref[sl, s], lg_all[:, s]) for s in ln]
            lf, kk, q = [p[4] for p in preps], [p[5] for p in preps], [p[6] for p in preps]
            v = [hi_ref[sl, s] for s in ln]
            ex = [jnp.exp(x) for x in _mx_each(m_ref[...], lf)]
            eb = [e[6 * CHUNK:7 * CHUNK] for e in ex]
            esfx = [e[7 * CHUNK:8 * CHUNK] for e in ex]
            p = _hg_scores(q, kk, ex, mask_ref)
            s0 = [state[h] for h in heads]
            o = _each(lambda a, e, s, pp, vv: _dot(a * e, s) + _dot(pp, vv), q, eb, s0, p, v)
            eye_v = eye_ref[...]
            s1 = _each(lambda s, e, kx, ef, vv: s * _row_to_col(e[CHUNK - 1:CHUNK, :], eye_v) + _dot_tn(kx * ef, vv),
                       s0, eb, kk, esfx, v)
            for hh in hs:
                ssave_ref[i, hh] = s0[hh]
                state[heads[hh]] = s1[hh]
                oraw_ref[sl, ln[hh]] = o[hh]
                r = lax.rsqrt(jnp.mean(o[hh] * o[hh], axis=1, keepdims=True) + EPS)
                og_ref[sl, ln[hh]] = (o[hh] * r * gain_v * _silu(hg_ref[sl, ln[hh]])).astype(BF16)
            return carry

        lax.fori_loop(0, cb, one, 0, unroll=2)

    return pl.pallas_call(
        body, name="hgrn_fwd", grid=(nc // cb, HG_HEADS // HPS),
        in_specs=[_view_tile(v, rows, HPS * HEAD) for v in (hq, hf, hi, hg)] + [
                  pl.BlockSpec((2, HPS * HEAD), lambda c, g: (0, g)),
                  pl.BlockSpec((1, HEAD), lambda c, g: (0, 0)),
                  pl.BlockSpec(mstack.shape, lambda c, g: (0, 0)),
                  pl.BlockSpec(masks.shape, lambda c, g: (0, 0, 0)),
                  pl.BlockSpec(eye.shape, lambda c, g: (0, 0))],
        out_specs=[tile, tile, pl.BlockSpec((cb, HPS, HEAD, HEAD), lambda c, g: (c, g, 0, 0))],
        out_shape=[jax.ShapeDtypeStruct((t, HG_HEADS * HEAD), F32), jax.ShapeDtypeStruct((t, HG_HEADS * HEAD), BF16),
                   jax.ShapeDtypeStruct((nc, HG_HEADS, HEAD, HEAD), F32)],
        scratch_shapes=[pltpu.VMEM((HG_HEADS, HEAD, HEAD), F32)],
        compiler_params=_params(_ARB, _ARB))(hq[0], hf[0], hi[0], hg[0], logits, gain, mstack, masks, eye)


def _hgrn_bwd(hq, hf, hi, hg, logits, gain, oraw, ssave, dog, consts):
    t = hq[0].shape[0]
    nc = t // CHUNK
    cb = _chunks_per_step(nc)
    rows = cb * CHUNK
    nb = nc // cb
    mstack, mstack_t, masks, eye = consts
    tile = pl.BlockSpec((rows, HPS * HEAD), lambda c, g: (nb - 1 - c, g))

    def body(hq_ref, hf_ref, hi_ref, hg_ref, lg_ref, gain_ref, oraw_ref, ssave_ref, dog_ref, m_ref, mt_ref,
             mask_ref, eye_ref, dhq_ref, dhf_ref, dhi_ref, dhg_ref, dgain_ref, dlb_ref, dstate):
        c = pl.program_id(0)
        g = pl.program_id(1)

        @pl.when(c == 0)
        def _():
            for hh in range(HPS):
                dstate[g * HPS + hh] = jnp.zeros((HEAD, HEAD), F32)

        @pl.when((c == 0) & (g == 0))
        def _():
            dgain_ref[...] = jnp.zeros_like(dgain_ref)
            dlb_ref[...] = jnp.zeros_like(dlb_ref)

        lg_all = lg_ref[...]
        gain_v = gain_ref[...]
        eye_v = eye_ref[...]
        last_row = (lax.broadcasted_iota(jnp.int32, (CHUNK, HEAD), 0) == CHUNK - 1).astype(F32)

        def one(j, carry):
            i = cb - 1 - j
            sl = pl.ds(pl.multiple_of(i * CHUNK, CHUNK), CHUNK)
            hs = range(HPS)
            heads = [g * HPS + hh for hh in hs]
            ln = [slice(hh * HEAD, (hh + 1) * HEAD) for hh in hs]
            hqv = [hq_ref[sl, s] for s in ln]
            hgv = [hg_ref[sl, s] for s in ln]
            preps = [_hg_prep(a, hf_ref[sl, s], lg_all[:, s]) for a, s in zip(hqv, ln)]
            lb, sg, sgn, f, lf, kk, q = ([p[n] for p in preps] for n in range(7))
            v = [hi_ref[sl, s] for s in ln]
            ex = [jnp.exp(x) for x in _mx_each(m_ref[...], lf)]
            eb = [e[6 * CHUNK:7 * CHUNK] for e in ex]
            esfx = [e[7 * CHUNK:8 * CHUNK] for e in ex]
            p = _hg_scores(q, kk, ex, mask_ref)
            s0 = [ssave_ref[i, hh] for hh in hs]
            ds = [dstate[h] for h in heads]

            o = [oraw_ref[sl, s] for s in ln]
            r = [lax.rsqrt(jnp.mean(x * x, axis=1, keepdims=True) + EPS) for x in o]
            on = _each(lambda x, y: x * y, o, r)
            dg_out = [dog_ref[sl, s] for s in ln]
            sgate = [_silu(x) for x in hgv]
            for hh in hs:
                dhg_ref[sl, ln[hh]] = (dg_out[hh] * on[hh] * gain_v * _dsilu(hgv[hh])).astype(BF16)
            dgain_ref[...] += sum(jnp.sum(d * s * n, axis=0, keepdims=True) for d, s, n in zip(dg_out, sgate, on))
            don = _each(lambda d, s: d * s * gain_v, dg_out, sgate)
            do = _each(lambda rr, dn, n: rr * (dn - n * jnp.mean(dn * n, axis=1, keepdims=True)), r, don, on)

            dp = _each(_dot_nt, do, v)
            dv = _each(lambda pp, d, kx, ef, s: _dot_tn(pp, d) + _dot(kx * ef, s), p, do, kk, esfx, ds)
            dqb = _each(_dot_nt, do, s0)
            dkx = _each(_dot_nt, v, ds)
            diag = [_rowsum(mask_ref[6] * x) for x in dp]
            dq = _each(lambda a, e, d, kx: a * e + d * kx, dqb, eb, diag, kk)
            dk = _each(lambda a, e, d, qq: a * e + d * qq, dkx, esfx, diag, q)
            dxs = [[] for _ in hs]
            for lvl in range(6):
                el = [e[lvl * CHUNK:(lvl + 1) * CHUNK] for e in ex]
                gm = [mask_ref[lvl] * x for x in dp]
                a1 = _each(lambda m_, kx, e: _dot(m_, kx * e), gm, kk, el)
                a2 = _each(lambda m_, qq, e: _dot_tn(m_, qq * e), gm, q, el)
                dq = _each(lambda x, a, e: x + a * e, dq, a1, el)
                dk = _each(lambda x, a, e: x + a * e, dk, a2, el)
                for hh in hs:
                    dxs[hh].append((a1[hh] * q[hh] + a2[hh] * kk[hh]) * el[hh])
            e_end_row = [e[CHUNK - 1:CHUNK, :] for e in eb]
            ds_new = _each(lambda qq, e, d, er, s: _dot_tn(qq * e, d) + _row_to_col(er, eye_v) * s, q, eb, do, e_end_row, ds)
            for hh in hs:
                dstate[heads[hh]] = ds_new[hh]
                dend_row = _col_to_row(_rowsum(s0[hh] * ds[hh]), eye_v)
                dxs[hh].append(dqb[hh] * q[hh] * eb[hh] + last_row * (e_end_row[hh] * dend_row))
                dxs[hh].append(dkx[hh] * kk[hh] * esfx[hh])
            dlf = _mx_each(mt_ref[...], [jnp.concatenate(x, axis=0) for x in dxs])

            for hh in hs:
                dhi_ref[sl, ln[hh]] = dv[hh].astype(BF16)
                dhq_ref[sl, ln[hh]] = (dq[hh] * (HEAD ** -0.5) * _dsilu(hqv[hh])).astype(BF16)
                df = dlf[hh] / f[hh]
                dsig = (1.0 - lb[hh]) * sg[hh] * sgn[hh]
                dhf_ref[sl, ln[hh]] = ((df - dk[hh]) * dsig).astype(BF16)
                dlb_t = jnp.sum(df * sgn[hh] - dk[hh] * sgn[hh], axis=0, keepdims=True)
                dlb_ref[pl.ds(heads[hh], 1), :] += dlb_t * lb[hh] * (1.0 - lb[hh])
            return carry

        lax.fori_loop(0, cb, one, 0, unroll=2)

    outs = [jax.ShapeDtypeStruct((t, HG_HEADS * HEAD), BF16)] * 4 + [
        jax.ShapeDtypeStruct((1, HEAD), F32), jax.ShapeDtypeStruct((HG_HEADS, HEAD), F32)]
    return pl.pallas_call(
        body, name="hgrn_bwd", grid=(nb, HG_HEADS // HPS),
        in_specs=[_view_tile(v, rows, HPS * HEAD, lambda c: nb - 1 - c) for v in (hq, hf, hi, hg)] + [
                  pl.BlockSpec((2, HPS * HEAD), lambda c, g: (0, g)),
                  pl.BlockSpec((1, HEAD), lambda c, g: (0, 0)), tile,
                  pl.BlockSpec((cb, HPS, HEAD, HEAD), lambda c, g: (nb - 1 - c, g, 0, 0)), tile,
                  pl.BlockSpec(mstack.shape, lambda c, h: (0, 0)),
                  pl.BlockSpec(mstack_t.shape, lambda c, h: (0, 0)),
                  pl.BlockSpec(masks.shape, lambda c, h: (0, 0, 0)),
                  pl.BlockSpec(eye.shape, lambda c, h: (0, 0))],
        out_specs=[tile, tile, tile, tile, pl.BlockSpec((1, HEAD), lambda c, h: (0, 0)),
                   pl.BlockSpec((HG_HEADS, HEAD), lambda c, h: (0, 0))],
        out_shape=outs, scratch_shapes=[pltpu.VMEM((HG_HEADS, HEAD, HEAD), F32)],
        compiler_params=_params(_ARB, _ARB))(hq[0], hf[0], hi[0], hg[0], logits, gain, oraw, ssave, dog, mstack,
                                             mstack_t, masks, eye)


CONV_W = 512


def _per_head(fn, *arrs):
    width = arrs[0].shape[1]
    return jnp.concatenate([fn(*[a[:, j:j + HEAD] for a in arrs]) for j in range(0, width, HEAD)], axis=1)


def _shift_down(xv, halo, d, top_rows):
    if d == 0:
        return xv, xv[0:8]
    main = pltpu.roll(xv, d, 0)
    top = jnp.where(top_rows < d, pltpu.roll(halo, d, 0), main[0:8])
    return main, top


def _conv_parts(x_ref, halo_ref, w_ref, first):
    xv = x_ref[...]
    halo = jnp.where(first, 0.0, halo_ref[...])
    top_rows = lax.broadcasted_iota(jnp.int32, (8, xv.shape[1]), 0)
    shifted = [_shift_down(xv, halo, CONV_K - 1 - j, top_rows) for j in range(CONV_K)]
    w = w_ref[...]
    acc = sum(shifted[j][0] * w[j:j + 1, :] for j in range(CONV_K))
    acc_top = sum(shifted[j][1] * w[j:j + 1, :] for j in range(CONV_K))
    return shifted, acc, acc_top


def _conv_fwd(x, w8, l2scale, name):
    x, off, width = x
    t = x.shape[0]
    o = off // CONV_W
    tr = _pick(t, 512, 8)

    def post(cv):
        s = _silu(cv)
        if l2scale is not None:
            s = _per_head(lambda sh: sh * (lax.rsqrt(_rowsum(sh * sh) + EPS) * l2scale), s)
        return s

    def body(x_ref, halo_ref, w_ref, o_ref):
        _, acc, acc_top = _conv_parts(x_ref, halo_ref, w_ref, pl.program_id(1) == 0)
        o_ref[...] = post(acc)
        o_ref[0:8, :] = post(acc_top)

    return pl.pallas_call(
        body, name=name, grid=(width // CONV_W,t // tr),
        in_specs=[pl.BlockSpec((tr, CONV_W), lambda j, i: (i, o + j)),
                  pl.BlockSpec((8, CONV_W), lambda j, i: (jnp.maximum(i * (tr // 8) - 1, 0), o + j)),
                  pl.BlockSpec((8, CONV_W), lambda j, i: (0, j))],
        out_specs=pl.BlockSpec((tr, CONV_W), lambda j, i: (i, j)),
        out_shape=jax.ShapeDtypeStruct((t, width), F32), compiler_params=_params(_PAR, _PAR))(x, x, w8)


def _conv_bwd_a(x, w8, dy, l2scale, name):
    x, off, width = x
    t = x.shape[0]
    o = off // CONV_W
    tr = _pick(t, 512, 8)

    def l2_bwd(s, dyh):
        r = lax.rsqrt(_rowsum(s * s) + EPS)
        y0 = s * r
        dy0 = dyh * l2scale
        return r * (dy0 - y0 * _rowsum(dy0 * y0))

    def to_dc(cv, dyv):
        if l2scale is not None:
            dyv = _per_head(l2_bwd, _silu(cv), dyv)
        return dyv * _dsilu(cv)

    def body(x_ref, halo_ref, w_ref, dy_ref, dc_ref, dw_ref):
        @pl.when(pl.program_id(1) == 0)
        def _():
            dw_ref[...] = jnp.zeros_like(dw_ref)

        shifted, acc, acc_top = _conv_parts(x_ref, halo_ref, w_ref, pl.program_id(1) == 0)
        dyv = dy_ref[...]
        dc = to_dc(acc, dyv)
        dc_top = to_dc(acc_top, dyv[0:8])
        dc_ref[...] = dc
        dc_ref[0:8, :] = dc_top
        rest = (lax.broadcasted_iota(jnp.int32, dc.shape, 0) >= 8).astype(F32)
        dc_rest = dc * rest
        for j in range(CONV_K):
            dw_ref[j:j + 1, :] += (jnp.sum(dc_rest * shifted[j][0], axis=0, keepdims=True)
                                   + jnp.sum(dc_top * shifted[j][1], axis=0, keepdims=True))

    return pl.pallas_call(
        body, name=name, grid=(width // CONV_W,t // tr),
        in_specs=[pl.BlockSpec((tr, CONV_W), lambda j, i: (i, o + j)),
                  pl.BlockSpec((8, CONV_W), lambda j, i: (jnp.maximum(i * (tr // 8) - 1, 0), o + j)),
                  pl.BlockSpec((8, CONV_W), lambda j, i: (0, j)),
                  pl.BlockSpec((tr, CONV_W), lambda j, i: (i, j))],
        out_specs=[pl.BlockSpec((tr, CONV_W), lambda j, i: (i, j)), pl.BlockSpec((8, CONV_W), lambda j, i: (0, j))],
        out_shape=[jax.ShapeDtypeStruct((t, width), F32), jax.ShapeDtypeStruct((8, width), F32)],
        compiler_params=_params(_PAR, _ARB))(x, x, w8, dy)


def _conv_bwd_b(dc, w8, name):
    t, width = dc.shape
    tr = _pick(t, 512, 8)
    nt = t // tr

    def body(dc_ref, halo_ref, w_ref, dx_ref):
        dcv = dc_ref[...]
        halo = jnp.where(pl.program_id(1) == nt - 1, 0.0, halo_ref[...])
        w = w_ref[...]
        bot_rows = lax.broadcasted_iota(jnp.int32, (8, CONV_W), 0)
        acc = dcv * w[CONV_K - 1:CONV_K, :]
        acc_bot = dcv[tr - 8:tr] * w[CONV_K - 1:CONV_K, :]
        for d in range(1, CONV_K):
            main = pltpu.roll(dcv, tr - d, 0)
            bot = jnp.where(bot_rows >= 8 - d, pltpu.roll(halo, 8 - d, 0), main[tr - 8:tr])
            wj = w[CONV_K - 1 - d:CONV_K - d, :]
            acc = acc + main * wj
            acc_bot = acc_bot + bot * wj
        dx_ref[...] = acc.astype(BF16)
        dx_ref[tr - 16:tr, :] = jnp.concatenate([acc[tr - 16:tr - 8], acc_bot], axis=0).astype(BF16)

    return pl.pallas_call(
        body, name=name, grid=(width // CONV_W,nt),
        in_specs=[pl.BlockSpec((tr, CONV_W), lambda j, i: (i, j)),
                  pl.BlockSpec((8, CONV_W), lambda j, i: (jnp.minimum((i + 1) * (tr // 8), t // 8 - 1), j)),
                  pl.BlockSpec((8, CONV_W), lambda j, i: (0, j))],
        out_specs=pl.BlockSpec((tr, CONV_W), lambda j, i: (i, j)),
        out_shape=jax.ShapeDtypeStruct((t, width), BF16), compiler_params=_params(_PAR, _PAR))(dc, dc, w8)


def _each(f, *lists):
    return [f(*xs) for xs in zip(*lists)]


def _split2_each(xs):
    hi = [_bf(x) for x in xs]
    lo = [_bf(x - h.astype(F32)) for x, h in zip(xs, hi)]
    return hi, lo


def _hp_each(a_split, b_split):
    (ah, al), (bh, bl) = a_split, b_split
    rows = ah[0].shape[0]
    d12 = [jnp.dot(jnp.concatenate([x, y], axis=0), z, preferred_element_type=F32) for x, y, z in zip(ah, al, bh)]
    d3 = [jnp.dot(x, y, preferred_element_type=F32) for x, y in zip(ah, bl)]
    return [d[:rows] + d[rows:] + e for d, e in zip(d12, d3)]


def _tri_inv_each(a_list, eye):
    ns = [-a for a in a_list]
    ps = [eye + n for n in ns]
    n_split = _split2_each(ns)
    for _ in range(5):
        ns = _hp_each(n_split, n_split)
        n_split = _split2_each(ns)
        ps = [p + d for p, d in zip(ps, _hp_each(_split2_each(ps), n_split))]
    return ps


def _gd_gates(gab, alog, dtb):
    sp_arg = gab + dtb
    return sp_arg, -jnp.exp(alog) * _softplus(sp_arg), _sigmoid(gab)


def _gd_chunks(q, k, v, g_all, beta_all, sel, l_ref, mask_ref):
    incl, strict, eye, upper = mask_ref[0], mask_ref[1], mask_ref[2], mask_ref[3]
    lmat = l_ref[...]
    gates = jnp.concatenate(_split3(g_all) + _split3(beta_all), axis=0)
    picked = [jnp.dot(gates, s, preferred_element_type=F32) for s in sel]
    c = CHUNK
    gb = [p[0:c, :HEAD] + p[c:2 * c, :HEAD] + p[2 * c:3 * c, :HEAD] for p in picked]
    bb = [p[3 * c:4 * c, HEAD:] + p[4 * c:5 * c, HEAD:] + p[5 * c:, HEAD:] for p in picked]
    gam = _mx_each(lmat, gb)
    gam_row = [jnp.sum(x[:, :CHUNK] * upper, axis=0, keepdims=True) for x in gb]
    lm = _each(lambda gm, gr: incl * jnp.exp(jnp.minimum(gm[:, :CHUNK] - gr, 0.0)), gam, gam_row)
    kb = _each(lambda x, b: x * b, k, bb)
    a = _each(lambda x, y, m: strict * _dot_nt(x, y) * m, kb, k, lm)
    tm = _tri_inv_each(a, eye)
    eg = [jnp.exp(x) for x in gam]
    vb = _each(lambda x, b: x * b, v, bb)
    kbg = _each(lambda x, e: x * e, kb, eg)
    uw = _each(lambda t_, x, y: _dot(t_, jnp.concatenate([x, y], axis=1)), tm, vb, kbg)
    u = [x[:, :HEAD] for x in uw]
    w = [x[:, HEAD:] for x in uw]
    qk = _each(lambda x, y, m: _dot_nt(x, y) * m, q, k, lm)
    g_end = [x[CHUNK - 1:CHUNK, :] for x in gam]
    ekg = _each(lambda e, x: jnp.exp(e - x), g_end, gam)
    ge = [jnp.exp(e) for e in g_end]
    kg = _each(lambda x, e: x * e, k, ekg)
    qg = _each(lambda x, e: x * e, q, eg)
    names = ("bb", "lm", "kb", "a", "tm", "eg", "vb", "kbg", "u", "w", "qk", "ekg", "ge", "kg", "qg")
    cols = (bb, lm, kb, a, tm, eg, vb, kbg, u, w, qk, ekg, ge, kg, qg)
    return [dict(zip(names, vals)) for vals in zip(*cols)]


def _gd_specs(rows, rev_nb=None):
    def cidx(c):
        return c if rev_nb is None else rev_nb - 1 - c

    qk_tile = pl.BlockSpec((rows, HPS // 2 * HEAD), lambda c, g: (cidx(c), g))
    v_tile = pl.BlockSpec((rows, HPS * HEAD), lambda c, g: (cidx(c), g))
    gab_tile = pl.BlockSpec((rows, HEAD), lambda c, g: (cidx(c), 0))
    return qk_tile, v_tile, gab_tile


def _gdn_fwd(qn, kn, cv, gab, gz, alog, dtb, gain, consts):
    t = qn.shape[0]
    nc = t // CHUNK
    cb = _chunks_per_step(nc)
    rows = cb * CHUNK
    lmat, _, masks, sel = consts
    qk_tile, v_tile, gab_tile = _gd_specs(rows)
    row128 = pl.BlockSpec((1, HEAD), lambda c, h: (0, 0))

    def body(q_ref, k_ref, v_ref, gab_ref, gz_ref, alog_ref, dtb_ref, gain_ref, sel_ref, l_ref, mask_ref,
             oraw_ref, og_ref, ssave_ref, state):
        c = pl.program_id(0)
        g = pl.program_id(1)

        @pl.when(c == 0)
        def _():
            for hh in range(HPS):
                state[g * HPS + hh] = jnp.zeros((HEAD, HEAD), F32)

        alog = alog_ref[...]
        dtb = dtb_ref[...]
        gain_v = gain_ref[...]

        def one(i, carry):
            sl = pl.ds(pl.multiple_of(i * CHUNK, CHUNK), CHUNK)
            _, g_all, beta_all = _gd_gates(gab_ref[sl, :], alog, dtb)
            heads = [g * HPS + hh for hh in range(HPS)]
            lq = [slice(hh // 2 * HEAD, (hh // 2 + 1) * HEAD) for hh in range(HPS)]
            lv = [slice(hh * HEAD, (hh + 1) * HEAD) for hh in range(HPS)]
            chs = _gd_chunks([q_ref[sl, s] for s in lq], [k_ref[sl, s] for s in lq], [v_ref[sl, s] for s in lv],
                             g_all, beta_all, [sel_ref[h] for h in heads], l_ref, mask_ref)
            s0 = [state[h] for h in heads]
            ws = _each(lambda ch, s: _dot(jnp.concatenate([ch["w"], ch["qg"]], axis=0), s), chs, s0)
            v_new = _each(lambda ch, x: ch["u"] - x[:CHUNK], chs, ws)
            o = _each(lambda ch, x, vn: x[CHUNK:] + _dot(ch["qk"], vn), chs, ws, v_new)
            s1 = _each(lambda ch, s, vn: s * ch["ge"] + _dot_tn(ch["kg"], vn), chs, s0, v_new)
            for hh in range(HPS):
                ssave_ref[i, hh] = s0[hh]
                state[heads[hh]] = s1[hh]
                oraw_ref[sl, lv[hh]] = o[hh]
                r = lax.rsqrt(jnp.mean(o[hh] * o[hh], axis=1, keepdims=True) + EPS)
                og_ref[sl, lv[hh]] = (o[hh] * r * gain_v * _silu(gz_ref[sl, lv[hh]])).astype(BF16)
            return carry

        lax.fori_loop(0, cb, one, 0, unroll=2)

    return pl.pallas_call(
        body, name="gdn_fwd", grid=(nc // cb, GD_HEADS // HPS),
        in_specs=[qk_tile, qk_tile, v_tile, gab_tile, _view_tile(gz, rows, HPS * HEAD), row128, row128, row128,
                  pl.BlockSpec(sel.shape, lambda c, g: (0, 0, 0)),
                  pl.BlockSpec(lmat.shape, lambda c, g: (0, 0)),
                  pl.BlockSpec(masks.shape, lambda c, g: (0, 0, 0))],
        out_specs=[v_tile, v_tile, pl.BlockSpec((cb, HPS, HEAD, HEAD), lambda c, g: (c, g, 0, 0))],
        out_shape=[jax.ShapeDtypeStruct((t, GD_HEADS * HEAD), F32), jax.ShapeDtypeStruct((t, GD_HEADS * HEAD), BF16),
                   jax.ShapeDtypeStruct((nc, GD_HEADS, HEAD, HEAD), F32)],
        scratch_shapes=[pltpu.VMEM((GD_HEADS, HEAD, HEAD), F32)],
        compiler_params=_params(_ARB, _ARB))(qn, kn, cv, gab, gz[0], alog, dtb, gain, sel, lmat, masks)


def _gdn_bwd(qn, kn, cv, gab, gz, alog, dtb, gain, oraw, ssave, dog, consts):
    t = qn.shape[0]
    nc = t // CHUNK
    cb = _chunks_per_step(nc)
    rows = cb * CHUNK
    nb = nc // cb
    lmat, lmat_t, masks, sel = consts
    qk_tile, v_tile, gab_tile = _gd_specs(rows, nb)
    row128 = pl.BlockSpec((1, HEAD), lambda c, h: (0, 0))

    def body(q_ref, k_ref, v_ref, gab_ref, gz_ref, alog_ref, dtb_ref, gain_ref, oraw_ref, ssave_ref, dog_ref,
             sel_ref, l_ref, lt_ref, mask_ref,
             dq_ref, dk_ref, dv_ref, dgab_ref, dgz_ref, small_ref, dstate):
        c = pl.program_id(0)
        g = pl.program_id(1)

        @pl.when(c == 0)
        def _():
            for hh in range(HPS):
                dstate[g * HPS + hh] = jnp.zeros((HEAD, HEAD), F32)

        @pl.when((c == 0) & (g == 0))
        def _():
            small_ref[...] = jnp.zeros_like(small_ref)

        alog = alog_ref[...]
        dtb = dtb_ref[...]
        gain_v = gain_ref[...]
        lane = lax.broadcasted_iota(jnp.int32, (1, HEAD), 1)
        last_row = (lax.broadcasted_iota(jnp.int32, (CHUNK, HEAD), 0) == CHUNK - 1).astype(F32)

        def one(j, carry):
            i = cb - 1 - j
            sl = pl.ds(pl.multiple_of(i * CHUNK, CHUNK), CHUNK)
            sp_arg, g_all, beta_all = _gd_gates(gab_ref[sl, :], alog, dtb)
            strict, eye = mask_ref[1], mask_ref[2]
            ltm = lt_ref[...]
            hs = range(HPS)
            heads = [g * HPS + hh for hh in hs]
            lq = [slice(hh // 2 * HEAD, (hh // 2 + 1) * HEAD) for hh in hs]
            lv = [slice(hh * HEAD, (hh + 1) * HEAD) for hh in hs]
            q = [q_ref[sl, s] for s in lq]
            k = [k_ref[sl, s] for s in lq]
            v = [v_ref[sl, s] for s in lv]
            gzv = [gz_ref[sl, s] for s in lv]
            chs = _gd_chunks(q, k, v, g_all, beta_all, [sel_ref[h] for h in heads], l_ref, mask_ref)

            def col(name):
                return [ch[name] for ch in chs]

            def mul(x, y):
                return x * y

            tm, lm, eg, bb = col("tm"), col("lm"), col("eg"), col("bb")
            s0 = [ssave_ref[i, hh] for hh in hs]
            ds = [dstate[h] for h in heads]
            v_new = _each(lambda u, w, s: u - _dot(w, s), col("u"), col("w"), s0)

            o = [oraw_ref[sl, s] for s in lv]
            r = [lax.rsqrt(jnp.mean(x * x, axis=1, keepdims=True) + EPS) for x in o]
            on = _each(mul, o, r)
            dg_out = [dog_ref[sl, s] for s in lv]
            sgate = [_silu(x) for x in gzv]
            for hh in hs:
                dgz_ref[sl, lv[hh]] = (dg_out[hh] * on[hh] * gain_v * _dsilu(gzv[hh])).astype(BF16)
            small_ref[0:1, :] += sum(jnp.sum(d * s * n, axis=0, keepdims=True) for d, s, n in zip(dg_out, sgate, on))
            don = _each(lambda d, s: d * s * gain_v, dg_out, sgate)
            do = _each(lambda rr, dn, n: rr * (dn - n * jnp.mean(dn * n, axis=1, keepdims=True)), r, don, on)

            dv_new = _each(lambda a, d, b, s: _dot_tn(a, d) + _dot(b, s), col("qk"), do, col("kg"), ds)
            dqk = _each(_dot_nt, do, v_new)
            dkg = _each(_dot_nt, v_new, ds)
            dge = _each(lambda s, d: jnp.sum(_rowsum(s * d), axis=0, keepdims=True), s0, ds)
            both = _each(lambda d, dv: jnp.concatenate([d, dv], axis=0), do, dv_new)
            from_s = _each(_dot_nt, both, s0)
            dqg = [x[:CHUNK] for x in from_s]
            dw = [-x[CHUNK:] for x in from_s]
            ds_new = _each(lambda qg, w, bo, ge, s: _dot_tn(jnp.concatenate([qg, -w], axis=0), bo) + ge * s,
                           col("qg"), col("w"), both, col("ge"), ds)
            for hh in hs:
                dstate[heads[hh]] = ds_new[hh]

            side = _each(lambda dv, d: jnp.concatenate([dv, d], axis=1), dv_new, dw)
            back = _each(_dot_tn, tm, side)
            dvb = [x[:, :HEAD] for x in back]
            dkbg = [x[:, HEAD:] for x in back]
            dtm = _each(lambda sd, vb, kbg: _dot_nt(sd, jnp.concatenate([vb, kbg], axis=1)), side, col("vb"), col("kbg"))
            dtt = _each(_dot_nt, dtm, tm)
            da = _each(lambda t_, x: -_dot_tn(t_, x) * strict, tm, dtt)
            dal = _each(mul, da, lm)
            dqk_l = _each(mul, dqk, lm)
            stack = _each(lambda x, y: jnp.concatenate([x, y], axis=0), dal, dqk_l)
            on_k = _each(_dot, stack, k)
            dkb = _each(lambda x, y, e: x[:CHUNK] + y * e, on_k, dkbg, eg)
            dq = _each(lambda x, y, e: x[CHUNK:] + y * e, on_k, dqg, eg)
            dk = _each(lambda st, kb, qq, z, ekg, w_, b: _dot_tn(st, jnp.concatenate([kb, qq], axis=0)) + z * ekg + w_ * b,
                       stack, col("kb"), q, dkg, col("ekg"), dkb, bb)
            gmat = _each(lambda x, a, y, qk: x * a + y * qk, da, col("a"), dqk, col("qk"))
            t_kg = _each(lambda x, y: _rowsum(x * y), dkg, col("kg"))
            dgam = _each(lambda gm, x, qg, t_, y, kbg: (_rowsum(gm) - _row_to_col(jnp.sum(gm, axis=0, keepdims=True), eye)
                                                        + _rowsum(x * qg) - t_ + _rowsum(y * kbg)),
                         gmat, dqg, col("qg"), t_kg, dkbg, col("kbg"))
            dg_end = _each(lambda t_, e, ge: jnp.sum(t_, axis=0, keepdims=True) + e * ge[:, 0:1], t_kg, dge, col("ge"))
            dgam = _each(lambda x, e: x + last_row * e, dgam, dg_end)
            dbeta = _each(lambda x, kk, y, vv: _rowsum(x * kk) + _rowsum(y * vv), dkb, k, dvb, v)
            dg = _mx_each(ltm, dgam)

            for hh in hs:
                dv_ref[sl, lv[hh]] = dvb[hh] * bb[hh]
            fac_g = -jnp.exp(alog) * _sigmoid(sp_arg)
            fac_b = beta_all * (1.0 - beta_all)
            hot_g = [(lane == h).astype(F32) for h in heads]
            hot_b = [(lane == GD_HEADS + h).astype(F32) for h in heads]
            dga = _each(lambda x, hot: x * hot * fac_g, dg, hot_g)
            dgb = _each(lambda x, hot: x * hot * fac_b, dbeta, hot_b)
            small_ref[1:2, :] += sum(jnp.sum(x, axis=0, keepdims=True) for x in dga)
            small_ref[2:3, :] += sum(jnp.sum(x * hot * g_all, axis=0, keepdims=True) for x, hot in zip(dg, hot_g))
            for pair in range(HPS // 2):
                lqp = slice(pair * HEAD, (pair + 1) * HEAD)
                dq_ref[sl, lqp] = dq[2 * pair] + dq[2 * pair + 1]
                dk_ref[sl, lqp] = dk[2 * pair] + dk[2 * pair + 1]
            dgab_ref[sl, :] = sum(a + b for a, b in zip(dga, dgb))
            return carry

        lax.fori_loop(0, cb, one, 0, unroll=2)

    groups = GD_HEADS // HPS
    outs = [jax.ShapeDtypeStruct((t, 1024), F32), jax.ShapeDtypeStruct((t, 1024), F32),
            jax.ShapeDtypeStruct((t, 2048), F32), jax.ShapeDtypeStruct((t, groups * HEAD), F32),
            jax.ShapeDtypeStruct((t, 2048), BF16), jax.ShapeDtypeStruct((8, HEAD), F32)]
    return pl.pallas_call(
        body, name="gdn_bwd", grid=(nb, groups),
        in_specs=[qk_tile, qk_tile, v_tile, gab_tile, _view_tile(gz, rows, HPS * HEAD, lambda c: nb - 1 - c),
                  row128, row128, row128, v_tile,
                  pl.BlockSpec((cb, HPS, HEAD, HEAD), lambda c, g: (nb - 1 - c, g, 0, 0)), v_tile,
                  pl.BlockSpec(sel.shape, lambda c, g: (0, 0, 0)),
                  pl.BlockSpec(lmat.shape, lambda c, g: (0, 0)),
                  pl.BlockSpec(lmat_t.shape, lambda c, g: (0, 0)),
                  pl.BlockSpec(masks.shape, lambda c, g: (0, 0, 0))],
        out_specs=[qk_tile, qk_tile, v_tile, pl.BlockSpec((rows, HEAD), lambda c, g: (nb - 1 - c, g)), v_tile,
                   pl.BlockSpec((8, HEAD), lambda c, g: (0, 0))],
        out_shape=outs, scratch_shapes=[pltpu.VMEM((GD_HEADS, HEAD, HEAD), F32)],
        compiler_params=_params(_ARB, _ARB))(qn, kn, cv, gab, gz[0], alog, dtb, gain, oraw, ssave, dog, sel,
                                             lmat, lmat_t, masks)


def _fold_groups(wide):
    t, width = wide.shape
    tr = _pick(t, 512, 8)

    def body(w_ref, o_ref):
        acc = w_ref[:, 0:HEAD]
        for j in range(1, width // HEAD):
            acc = acc + w_ref[:, j * HEAD:(j + 1) * HEAD]
        o_ref[...] = acc.astype(BF16)

    return pl.pallas_call(
        body, name="fold_gate_grads", grid=(t // tr,), in_specs=[_row_spec(tr, width)], out_specs=_row_spec(tr, HEAD),
        out_shape=jax.ShapeDtypeStruct((t, HEAD), BF16), compiler_params=_params(_PAR))(wide)


def _adam_math(w, g, m, v):
    m2 = ADAM_B1 * m + (1.0 - ADAM_B1) * g
    v2 = ADAM_B2 * v + (1.0 - ADAM_B2) * (g * g)
    m_hat = m2 / (1.0 - ADAM_B1 ** ADAM_STEP)
    v_hat = v2 / (1.0 - ADAM_B2 ** ADAM_STEP)
    delta = -ADAM_LR * (m_hat / (jnp.sqrt(v_hat) + ADAM_EPS) + ADAM_WD * w)
    return delta, m2, v2


def _adamw(w, g, m, v, name):
    r, c = w.shape
    tr = r
    for cand in range(8, r + 1, 8):
        if r % cand == 0 and cand * c * 4 <= (1 << 20):
            tr = cand
    if r % 8 != 0:
        tr = r

    def body(w_ref, g_ref, m_ref, v_ref, d_ref, m2_ref, v2_ref):
        d, m2, v2 = _adam_math(w_ref[...], g_ref[...], m_ref[...], v_ref[...])
        d_ref[...] = d
        m2_ref[...] = m2
        v2_ref[...] = v2

    spec = pl.BlockSpec((tr, c), lambda i: (i, 0))
    return pl.pallas_call(
        body, name=name, grid=(r // tr,), in_specs=[spec] * 4, out_specs=[spec] * 3,
        out_shape=[jax.ShapeDtypeStruct((r, c), F32)] * 3, compiler_params=_params(_PAR))(w, g, m, v)


_ANY = pl.BlockSpec(memory_space=pl.ANY)


def _place():
    return lax.axis_index("x"), lax.axis_index("y"), lax.axis_index("c")


def _gather_weights(packs, nchs):
    n = len(packs)
    halves = [p.shape[0] // 2 for p in packs]
    base = [sum(nchs[:i]) for i in range(n)]
    total = sum(nchs)
    for p, h, k in zip(packs, halves, nchs):
        assert p.shape[0] == 2 * h and h % k == 0 and (h // k) % 16 == 0

    def body(*refs):
        p_refs, g_refs, (send_sems, recv_sems) = refs[:n], refs[n:2 * n], refs[2 * n:]
        x, y, c = _place()
        sibling = (x, y, 1 - c)
        chips = [(1 - x, y), (x, 1 - y), (1 - x, 1 - y)]
        chunks = [(a, q) for a in range(n) for q in range(nchs[a])]

        def rows_of(a, pc, q):
            ch = halves[a] // nchs[a]
            return pl.ds(pl.multiple_of(pc * halves[a] + q * ch, 16), ch)

        def piece(a, px, py, pc, q):
            return g_refs[a].at[2 * px + py, rows_of(a, pc, q), :]

        def copy(k, src, dst, to):
            return pltpu.make_async_remote_copy(src_ref=src, dst_ref=dst, send_sem=send_sems.at[k],
                                                recv_sem=recv_sems.at[k], device_id=to, device_id_type=MESH)

        def sem_of(j, a, q):
            return j * total + base[a] + q

        first = {(j, a, q): copy(sem_of(j, a, q), p_refs[a].at[rows_of(a, c, q), :], piece(a, x, y, c, q), (*chip, c))
                 for j, chip in enumerate(chips) for a, q in chunks}
        for a, q in chunks:
            for j in range(3):
                first[j, a, q].start()
        passed = {(j, a, q): copy(sem_of(3 + j, a, q), piece(a, *chip, c, q), piece(a, *chip, c, q), sibling)
                  for j, chip in enumerate(chips) for a, q in chunks}
        for a, q in chunks:
            for j, chip in enumerate(chips):
                copy(sem_of(j, a, q), p_refs[a].at[rows_of(a, c, q), :], piece(a, *chip, c, q), (*chip, c)).wait_recv()
                passed[j, a, q].start()
        for a, q in chunks:
            for j, chip in enumerate(chips):
                copy(sem_of(3 + j, a, q), piece(a, *chip, 1 - c, q), piece(a, *chip, 1 - c, q), sibling).wait_recv()
        for key in first:
            first[key].wait_send()
            passed[key].wait_send()

    return pl.pallas_call(
        body, name="gather_weights", out_shape=[jax.ShapeDtypeStruct((4,) + p.shape, p.dtype) for p in packs],
        in_specs=[_ANY] * n, out_specs=[_ANY] * n,
        scratch_shapes=[pltpu.SemaphoreType.DMA((6 * total,)), pltpu.SemaphoreType.DMA((6 * total,))])(*packs)


def _swap_with_sibling(arrs, nchs, lead, name):
    n = len(arrs)
    jobs = []
    for a, (arr, k) in enumerate(zip(arrs, nchs)):
        h = arr.shape[-2]
        assert h % k == 0 and (h // k) % 16 == 0
        for s in (range(lead) if lead else [None]):
            jobs += [(a, s, q * (h // k), h // k) for q in range(k)]

    def body(*refs):
        src, dst, (send_sems, recv_sems) = refs[:n], refs[n:2 * n], refs[2 * n:]
        x, y, c = _place()

        def at(ref, s, r0, rows):
            return ref.at[pl.ds(r0, rows), :] if s is None else ref.at[s, pl.ds(r0, rows), :]

        copies = [pltpu.make_async_remote_copy(
            src_ref=at(src[a], s, r0, rows), dst_ref=at(dst[a], s, r0, rows), send_sem=send_sems.at[k],
            recv_sem=recv_sems.at[k], device_id=(x, y, 1 - c), device_id_type=MESH)
            for k, (a, s, r0, rows) in enumerate(jobs)]
        for cp in copies:
            cp.start()
        for cp in copies:
            cp.wait()

    return pl.pallas_call(
        body, name=name, out_shape=[jax.ShapeDtypeStruct(a.shape, a.dtype) for a in arrs], in_specs=[_ANY] * n,
        out_specs=[_ANY] * n,
        scratch_shapes=[pltpu.SemaphoreType.DMA((len(jobs),)), pltpu.SemaphoreType.DMA((len(jobs),))])(*arrs)


def _add2(a, b, name):
    n, rows, w = a.shape
    tr = _pick(rows, 256, 16)
    spec = pl.BlockSpec((1, tr, w), lambda i, j: (i, j, 0))

    def body(a_ref, b_ref, o_ref):
        o_ref[...] = (a_ref[...].astype(F32) + b_ref[...].astype(F32)).astype(BF16)

    return pl.pallas_call(
        body, name=name, grid=(n, rows // tr), in_specs=[spec, spec], out_specs=spec,
        out_shape=jax.ShapeDtypeStruct(a.shape, BF16), compiler_params=_params(_PAR, _PAR))(a, b)


def _reduce_chips(partials, nchs):
    n = len(partials)
    jobs = []
    for a, (arr, k) in enumerate(zip(partials, nchs)):
        h = arr.shape[1]
        assert h % k == 0 and (h // k) % 16 == 0
        jobs += [(a, q * (h // k), h // k) for q in range(k)]

    def body(*refs):
        src, dst, (send_sems, recv_sems) = refs[:n], refs[n:2 * n], refs[2 * n:]
        x, y, c = _place()
        chips = [(1 - x, y), (x, 1 - y), (1 - x, 1 - y)]
        copies = [pltpu.make_async_remote_copy(
            src_ref=src[a].at[2 * px + py, pl.ds(r0, rows), :], dst_ref=dst[a].at[j, pl.ds(r0, rows), :],
            send_sem=send_sems.at[3 * k + j], recv_sem=recv_sems.at[3 * k + j],
            device_id=(px, py, c), device_id_type=MESH)
            for k, (a, r0, rows) in enumerate(jobs) for j, (px, py) in enumerate(chips)]
        for cp in copies:
            cp.start()
        for cp in copies:
            cp.wait()

    return pl.pallas_call(
        body, name="reduce_chips",
        out_shape=[jax.ShapeDtypeStruct((3,) + p.shape[1:], p.dtype) for p in partials],
        in_specs=[_ANY] * n, out_specs=[_ANY] * n,
        scratch_shapes=[pltpu.SemaphoreType.DMA((3 * len(jobs),)), pltpu.SemaphoreType.DMA((3 * len(jobs),))])(*partials)


def _add4(own, got, name):
    rows, w = own.shape
    tr = _pick(rows, 128, 16)

    def body(a_ref, b_ref, o_ref):
        o_ref[...] = ((a_ref[...].astype(F32) + b_ref[0].astype(F32)) + b_ref[1].astype(F32)) + b_ref[2].astype(F32)

    return pl.pallas_call(
        body, name=name, grid=(rows // tr,),
        in_specs=[pl.BlockSpec((tr, w), lambda i: (i, 0)), pl.BlockSpec((3, tr, w), lambda i: (0, i, 0))],
        out_specs=pl.BlockSpec((tr, w), lambda i: (i, 0)), out_shape=jax.ShapeDtypeStruct((rows, w), F32),
        compiler_params=_params(_PAR))(own, got)


def _small_sync(gs, ws, ms, vs):
    rows = gs.shape[0]
    vmem = pl.BlockSpec(memory_space=pltpu.VMEM)

    def body(g_ref, w_ref, m_ref, v_ref, sum_ref, d_ref, m2_ref, v2_ref, buf, send_sems, recv_sems):
        x, y, c = _place()
        me = 4 * x + 2 * y + c
        buf[me] = g_ref[...]
        copies = []
        for k in range(1, 8):
            peer = (x ^ (k >> 2), y ^ ((k >> 1) & 1), c ^ (k & 1))
            copies.append(pltpu.make_async_remote_copy(
                src_ref=g_ref, dst_ref=buf.at[me], send_sem=send_sems.at[k - 1], recv_sem=recv_sems.at[k - 1],
                device_id=peer, device_id_type=MESH))
        for cp in copies:
            cp.start()
        for cp in copies:
            cp.wait()
        total = buf[0]
        for i in range(1, 8):
            total = total + buf[i]
        sum_ref[...] = total
        d, m2, v2 = _adam_math(w_ref[...], total, m_ref[...], v_ref[...])
        d_ref[...] = d
        m2_ref[...] = m2
        v2_ref[...] = v2

    shape = jax.ShapeDtypeStruct((rows, 128), F32)
    return pl.pallas_call(
        body, name="small_sync", out_shape=[shape] * 4, in_specs=[vmem] * 4, out_specs=[vmem] * 4,
        scratch_shapes=[pltpu.VMEM((8, rows, 128), F32), pltpu.SemaphoreType.DMA((7,)),
                        pltpu.SemaphoreType.DMA((7,))])(gs, ws, ms, vs)


_PACK_COLS = (("ffn1_w_in", 1408), ("ffn2_w_in", 1408), ("w_in", 3080))
_PACK_ROWS = (("ffn1_w_out", 704, 704), ("ffn2_w_out", 704, 704), ("w_branch_hgrn", 256, 256),
              ("w_branch_gdn", 512, 512), ("w_out", 256, 256), ("gdn_conv_w", CONV_K, 128))
_PACK_CHUNKS = (8, 5)
_BIG_NAMES = tuple(n for n, _ in _PACK_COLS) + tuple(n for n, _, _ in _PACK_ROWS)


def _pack(parts, lead):
    ax = len(lead)
    cols = jnp.concatenate([parts[n] for n, _ in _PACK_COLS], axis=ax + 1)
    rows = []
    for n, r, padded in _PACK_ROWS:
        p = parts[n]
        if padded != r:
            p = jnp.concatenate([p, jnp.zeros(lead + (padded - r, p.shape[-1]), p.dtype)], axis=ax)
        rows.append(p)
    return cols, jnp.concatenate(rows, axis=ax)


def _unpack(cols, rows):
    out, off = {}, 0
    for n, w in _PACK_COLS:
        out[n] = cols[..., off:off + w]
        off += w
    off = 0
    for n, r, padded in _PACK_ROWS:
        out[n] = rows[..., off:off + r, :]
        off += padded
    return out


def _is_col_sharded(name):
    return name in ("ffn1_w_in", "ffn2_w_in", "w_in", "gdn_conv_w")


def _full_from_shards(name, g):
    if _is_col_sharded(name):
        return jnp.transpose(g, (1, 0, 2)).reshape(g.shape[1], -1)
    return g.reshape(-1, g.shape[2])


def _shards_from_full(name, full):
    if _is_col_sharded(name):
        return jnp.transpose(full.reshape(full.shape[0], 4, -1), (1, 0, 2))
    return full.reshape(4, -1, full.shape[1])


_SMALL = (("ffn1_norm", 8), ("mix_norm", 8), ("hgrn_lb_logits", 16), ("hgrn_out_norm", 1), ("gdn_a_log", 1),
          ("gdn_dt_bias", 1), ("gdn_out_norm", 1), ("ffn2_norm", 8), ("final_norm", 8), ("loss", 1))
_SMALL_ROWS = 56


def _pack_small(parts):
    out = []
    for name, rows in _SMALL:
        p = parts[name].reshape(-1).astype(F32)
        p = jnp.concatenate([p, jnp.zeros((rows * 128 - p.shape[0],), F32)]) if p.shape[0] != rows * 128 else p
        out.append(p.reshape(rows, 128))
    used = sum(r for _, r in _SMALL)
    out.append(jnp.zeros((_SMALL_ROWS - used, 128), F32))
    return jnp.concatenate(out, axis=0)


def _unpack_small(packed, shapes):
    out, off = {}, 0
    for name, rows in _SMALL:
        n = int(np.prod(shapes[name]))
        out[name] = packed[off:off + rows].reshape(-1)[:n].reshape(shapes[name])
        off += rows
    return out


def _ffn_fwd(x, gain, w_in, w_out, tag):
    n = _rmsnorm_fwd(x, gain, tag + "_norm")
    ab = _mm(n, w_in, out_dtype=BF16, name=tag + "_in")
    hm = _swiglu_fwd(ab, tag + "_act")
    out = _mm(hm, w_out, alpha=0.5, res=x, name=tag + "_out")
    return out, (n, ab)


def _ffn_bwd(x, gain, w_in, w_out, saved, dout, dout_bf, tag):
    n, ab = saved
    dhm = _mm(dout_bf, w_out, tb=True, alpha=0.5, name=tag + "_dact")
    dab, hm = _swiglu_bwd(ab, dhm, tag + "_dswiglu")
    dw_out = _mm(hm, dout_bf, ta=True, alpha=0.5, out_dtype=BF16, name=tag + "_dwout")
    dw_in = _mm(n, dab, ta=True, out_dtype=BF16, name=tag + "_dwin")
    dn = _mm(dab, w_in, tb=True, name=tag + "_dnorm")
    dx, dx_bf, dgain = _rmsnorm_bwd(x, gain, dn, dout, tag + "_dx")
    return dx, dx_bf, dgain, dw_in, dw_out


def _pad_lanes(v):
    return jnp.concatenate([v.reshape(1, -1), jnp.zeros((1, HEAD - v.size), F32)], axis=1)


def _local_step(x, tgt, w, small):
    hg_c = _hg_consts()
    gd_c = _gd_consts()
    seg, off = {}, 0
    for name, size in zip(IN_NAMES, IN_SIZES):
        seg[name] = w["w_in"][:, off:off + size]
        off += size
    w_gab = jnp.concatenate([seg["ga"], seg["gb"], jnp.zeros((D_MODEL, HEAD - 32), BF16)], axis=1)
    big_segs = [n for n in IN_NAMES if n not in ("ga", "gb")]
    conv8 = jnp.concatenate([w["gdn_conv_w"].astype(F32), jnp.zeros((8 - CONV_K, 4096), F32)], axis=0)
    conv_q, conv_k, conv_v = conv8[:, :1024], conv8[:, 1024:2048], conv8[:, 2048:]
    alog = _pad_lanes(small["gdn_a_log"])
    dtb = _pad_lanes(small["gdn_dt_bias"])
    logits = small["hgrn_lb_logits"]
    hg_gain = small["hgrn_out_norm"].reshape(1, HEAD)
    gd_gain = small["gdn_out_norm"].reshape(1, HEAD)
    g1, gm, g2 = small["ffn1_norm"].reshape(1, -1), small["mix_norm"].reshape(1, -1), small["ffn2_norm"].reshape(1, -1)
    gf = small["final_norm"].reshape(1, -1)
    qscale = HEAD ** -0.5

    h1, ffn1_saved = _ffn_fwd(x, g1, w["ffn1_w_in"], w["ffn1_w_out"], "ffn1")
    u = _rmsnorm_fwd(h1, gm, "mix_norm")
    w_main = jnp.concatenate([seg[n] for n in big_segs], axis=1)
    proj = _mm(u, w_main, name="proj")
    pr, off = {}, 0
    for n in big_segs:
        pr[n] = _view(proj, off, seg[n].shape[1])
        off += seg[n].shape[1]
    gab = _mm(u, w_gab, name="proj_gab")
    oh_raw, oh, s_h = _hgrn_fwd(pr["hq"], pr["hf"], pr["hi"], pr["hg"], logits, hg_gain, hg_c)
    qn = _conv_fwd(pr["gq"], conv_q, qscale, "conv_q")
    kn = _conv_fwd(pr["gk"], conv_k, 1.0, "conv_k")
    cv = _conv_fwd(pr["gv"], conv_v, None, "conv_v")
    og_raw, og, s_g = _gdn_fwd(qn, kn, cv, gab, pr["gz"], alog, dtb, gd_gain, gd_c)
    yh = _mm(oh, w["w_branch_hgrn"], name="branch_h")
    yg = _mm(og, w["w_branch_gdn"], name="branch_g")
    ym = _merge_fwd(yh, yg, pr["gate_h"], pr["gate_g"])
    h2 = _mm(ym, w["w_out"], res=h1, name="mix_out")
    h3, ffn2_saved = _ffn_fwd(h2, g2, w["ffn2_w_in"], w["ffn2_w_out"], "ffn2")
    loss, dh3, dh3_bf, d_gf = _final_loss(h3, gf, tgt)

    dh2, dh2_bf, d_g2, d_f2in, d_f2out = _ffn_bwd(h2, g2, w["ffn2_w_in"], w["ffn2_w_out"], ffn2_saved, dh3, dh3_bf,
                                                  "ffn2")
    dym = _mm(dh2_bf, w["w_out"], tb=True, name="d_merge")
    d_wout = _mm(ym, dh2_bf, ta=True, out_dtype=BF16, name="d_w_out")
    dyh, dyg, d_gate_h, d_gate_g = _merge_bwd(dym, yh, yg, pr["gate_h"], pr["gate_g"])
    d_wbh = _mm(oh, dyh, ta=True, out_dtype=BF16, name="d_w_branch_h")
    d_wbg = _mm(og, dyg, ta=True, out_dtype=BF16, name="d_w_branch_g")
    doh = _mm(dyh, w["w_branch_hgrn"], tb=True, name="d_oh")
    dog = _mm(dyg, w["w_branch_gdn"], tb=True, name="d_og")
    d_hq, d_hf, d_hi, d_hg, d_hg_gain, d_lb0 = _hgrn_bwd(pr["hq"], pr["hf"], pr["hi"], pr["hg"], logits, hg_gain,
                                                        oh_raw, s_h, doh, hg_c)
    d_qn, d_kn, d_cv, d_gab_wide, d_gz, gd_small = _gdn_bwd(qn, kn, cv, gab, pr["gz"], alog, dtb, gd_gain, og_raw,
                                                            s_g, dog, gd_c)
    d_gab = _fold_groups(d_gab_wide)
    dc_q, dwc_q = _conv_bwd_a(pr["gq"], conv_q, d_qn, qscale, "dconv_q")
    dc_k, dwc_k = _conv_bwd_a(pr["gk"], conv_k, d_kn, 1.0, "dconv_k")
    dc_v, dwc_v = _conv_bwd_a(pr["gv"], conv_v, d_cv, None, "dconv_v")
    d_gq = _conv_bwd_b(dc_q, conv_q, "dconvx_q")
    d_gk = _conv_bwd_b(dc_k, conv_k, "dconvx_k")
    d_gv = _conv_bwd_b(dc_v, conv_v, "dconvx_v")
    dpr = {"hq": d_hq, "hf": d_hf, "hi": d_hi, "hg": d_hg, "gq": d_gq, "gk": d_gk, "gv": d_gv, "gz": d_gz,
           "gate_h": d_gate_h, "gate_g": d_gate_g}
    dproj = jnp.concatenate([dpr[n] for n in big_segs], axis=1)
    du = _mm(d_gab, w_gab, tb=True, name="du_gab")
    du = _mm(dproj, w_main, tb=True, res=du, name="du")
    d_wmain = _mm(u, dproj, ta=True, out_dtype=BF16, name="dw_main")
    d_wgab = _mm(u, d_gab, ta=True, out_dtype=BF16, name="dw_gab")
    d_win = jnp.concatenate([d_wmain[:, :8192], d_wgab[:, :32], d_wmain[:, 8192:]], axis=1)
    dh1, dh1_bf, d_gm = _rmsnorm_bwd(h1, gm, du, dh2, "mix_dnorm")
    dx, _, d_g1, d_f1in, d_f1out = _ffn_bwd(x, g1, w["ffn1_w_in"], w["ffn1_w_out"], ffn1_saved, dh1, dh1_bf, "ffn1")

    d_conv = jnp.concatenate([dwc_q[:CONV_K], dwc_k[:CONV_K], dwc_v[:CONV_K]], axis=1).astype(BF16)
    big = {"ffn1_w_in": d_f1in, "ffn1_w_out": d_f1out, "w_in": d_win, "gdn_conv_w": d_conv,
           "w_branch_hgrn": d_wbh, "w_branch_gdn": d_wbg, "w_out": d_wout, "ffn2_w_in": d_f2in, "ffn2_w_out": d_f2out}
    d_lb0 = d_lb0.reshape(1, -1)
    sm = {"ffn1_norm": d_g1, "mix_norm": d_gm, "hgrn_lb_logits": jnp.concatenate([d_lb0, -d_lb0], axis=0),
          "hgrn_out_norm": d_hg_gain, "gdn_a_log": gd_small[2, :16], "gdn_dt_bias": gd_small[1, :16],
          "gdn_out_norm": gd_small[0], "ffn2_norm": d_g2, "final_norm": d_gf, "loss": loss[0, :1]}
    return dx, big, sm


_WEIGHTS = ("ffn1_norm", "ffn1_w_in", "ffn1_w_out", "mix_norm", "w_in", "hgrn_lb_logits", "hgrn_out_norm",
            "gdn_conv_w", "gdn_a_log", "gdn_dt_bias", "gdn_out_norm", "w_branch_hgrn", "w_branch_gdn", "w_out",
            "ffn2_norm", "ffn2_w_in", "ffn2_w_out", "final_norm")


def kernel(x, ffn1_norm, ffn1_w_in, ffn1_w_out, mix_norm, w_in, hgrn_lb_logits, hgrn_out_norm, gdn_conv_w, gdn_a_log, gdn_dt_bias, gdn_out_norm, w_branch_hgrn, w_branch_gdn, w_out, ffn2_norm, ffn2_w_in, ffn2_w_out, final_norm, loss_target, m_ffn1_norm, m_ffn1_w_in, m_ffn1_w_out, m_mix_norm, m_w_in, m_hgrn_lb_logits, m_hgrn_out_norm, m_gdn_conv_w, m_gdn_a_log, m_gdn_dt_bias, m_gdn_out_norm, m_w_branch_hgrn, m_w_branch_gdn, m_w_out, m_ffn2_norm, m_ffn2_w_in, m_ffn2_w_out, m_final_norm, v_ffn1_norm, v_ffn1_w_in, v_ffn1_w_out, v_mix_norm, v_w_in, v_hgrn_lb_logits, v_hgrn_out_norm, v_gdn_conv_w, v_gdn_a_log, v_gdn_dt_bias, v_gdn_out_norm, v_w_branch_hgrn, v_w_branch_gdn, v_w_out, v_ffn2_norm, v_ffn2_w_in, v_ffn2_w_out, v_final_norm):
    args = dict(locals())
    wts = {n: args[n] for n in _WEIGHTS}
    moms = {n: args["m_" + n] for n in _WEIGHTS}
    vars_ = {n: args["v_" + n] for n in _WEIGHTS}

    packs = _pack({n: wts[n][0].astype(BF16) for n in _BIG_NAMES}, ())
    xi, yi, ci = lax.axis_index("x"), lax.axis_index("y"), lax.axis_index("c")
    chip = 2 * xi + yi
    others = _gather_weights(packs, _PACK_CHUNKS)
    gathered = _unpack(*[lax.dynamic_update_index_in_dim(g, p, chip, 0) for g, p in zip(others, packs)])
    full = {n: _full_from_shards(n, gathered[n]) for n in _BIG_NAMES}
    small = {n: wts[n].astype(F32) for n in _WEIGHTS if n not in _BIG_NAMES}

    dx, big_grads, small_grads = _local_step(x[0], loss_target[0], full, small)

    gpacks = _pack({n: _shards_from_full(n, big_grads[n]) for n in _BIG_NAMES}, (4,))
    halves = [g.shape[1] // 2 for g in gpacks]
    own = [lax.dynamic_slice_in_dim(g, ci * h, h, axis=1) for g, h in zip(gpacks, halves)]
    give = [lax.dynamic_slice_in_dim(g, (1 - ci) * h, h, axis=1) for g, h in zip(gpacks, halves)]
    got = _swap_with_sibling(give, _PACK_CHUNKS, 4, "reduce_pair")
    chip_sums = [_add2(a, b, "add_pair_%d" % i) for i, (a, b) in enumerate(zip(own, got))]
    from_chips = _reduce_chips(chip_sums, _PACK_CHUNKS)
    mine = [_add4(lax.dynamic_index_in_dim(s, chip, axis=0, keepdims=False), f, "add_chips_%d" % i)
            for i, (s, f) in enumerate(zip(chip_sums, from_chips))]
    theirs = _swap_with_sibling(mine, _PACK_CHUNKS, 0, "share_pair")
    south = ci == 0
    reduced = _unpack(*[jnp.concatenate([jnp.where(south, a, b), jnp.where(south, b, a)], axis=0)
                        for a, b in zip(mine, theirs)])

    out_g, out_d, out_m, out_v = {}, {}, {}, {}
    for n in _BIG_NAMES:
        shape = wts[n].shape
        w2 = wts[n].reshape(shape[-2], shape[-1])
        g2 = reduced[n]
        d, m2, v2 = _adamw(w2, g2, moms[n].reshape(w2.shape), vars_[n].reshape(w2.shape), "adamw_" + n)
        out_g[n], out_d[n], out_m[n], out_v[n] = g2.reshape(shape), d.reshape(shape), m2.reshape(shape), v2.reshape(shape)

    small_names = [n for n, _ in _SMALL]
    zero = jnp.zeros((1,), F32)
    shapes = {n: (wts[n].shape if n != "loss" else (1,)) for n in small_names}
    sums, sd, sm_, sv = _small_sync(
        _pack_small(small_grads),
        _pack_small({n: (wts[n] if n != "loss" else zero) for n in small_names}),
        _pack_small({n: (moms[n] if n != "loss" else zero) for n in small_names}),
        _pack_small({n: (vars_[n] if n != "loss" else zero) for n in small_names}))
    sg_u, sd_u, sm_u, sv_u = (_unpack_small(p, shapes) for p in (sums, sd, sm_, sv))
    for n in small_names:
        if n != "loss":
            out_g[n], out_d[n], out_m[n], out_v[n] = sg_u[n], sd_u[n], sm_u[n], sv_u[n]
    loss = sg_u["loss"].reshape(())

    return (loss, dx[None], *[out_g[n] for n in _WEIGHTS], *[out_d[n] for n in _WEIGHTS],
            *[out_m[n] for n in _WEIGHTS], *[out_v[n] for n in _WEIGHTS])
```

```python
import numpy as np

import jax
import jax.numpy as jnp
from jax import lax
from jax.experimental import pallas as pl
from jax.experimental.pallas import tpu as pltpu

F32 = jnp.float32
BF16 = jnp.bfloat16

D_MODEL = 1024
D_FF = 2816
CHUNK = 64
HEAD = 128
HG_HEADS = 8
GD_HEADS = 16
HPS = 8
COMM_CHUNKS = 9
MM_TM = 1408
MM_TN = 512
MM_TK = 1536
VMEM_LIMIT = 48 * 1024 * 1024
EPS = 1e-6
CONV_K = 4
IN_NAMES = ("hq", "hf", "hi", "hg", "gq", "gk", "gv", "ga", "gb", "gz", "gate_h", "gate_g")
IN_SIZES = (1024, 1024, 1024, 1024, 1024, 1024, 2048, 16, 16, 2048, 1024, 1024)
IN_WIDTH = sum(IN_SIZES)

ADAM_LR = 0.001
ADAM_B1 = 0.9
ADAM_B2 = 0.999
ADAM_EPS = 1e-08
ADAM_WD = 0.01
ADAM_STEP = 10

MESH = pl.DeviceIdType.MESH
_ARB = "arbitrary"
_PAR = "parallel"


def _bf(x):
    return x.astype(BF16)


def _dot(a, b):
    return jnp.dot(_bf(a), _bf(b), preferred_element_type=F32)


def _dot_nt(a, b):
    return lax.dot_general(_bf(a), _bf(b), (((1,), (1,)), ((), ())), preferred_element_type=F32)


def _dot_tn(a, b):
    return lax.dot_general(_bf(a), _bf(b), (((0,), (0,)), ((), ())), preferred_element_type=F32)


def _split3(x):
    hi = _bf(x)
    r = x - hi.astype(F32)
    mid = _bf(r)
    lo = _bf(r - mid.astype(F32))
    return hi, mid, lo


def _dot_mx(m, x):
    hi, mid, lo = _split3(x)
    return (jnp.dot(m, hi, preferred_element_type=F32) + jnp.dot(m, mid, preferred_element_type=F32)
            + jnp.dot(m, lo, preferred_element_type=F32))


def _dot_xm(x, m):
    hi, mid, lo = _split3(x)
    return (jnp.dot(hi, m, preferred_element_type=F32) + jnp.dot(mid, m, preferred_element_type=F32)
            + jnp.dot(lo, m, preferred_element_type=F32))


def _dot_hp(a, b):
    ah = _bf(a)
    al = _bf(a - ah.astype(F32))
    bh = _bf(b)
    bl = _bf(b - bh.astype(F32))
    return (jnp.dot(ah, bh, preferred_element_type=F32) + jnp.dot(ah, bl, preferred_element_type=F32)
            + jnp.dot(al, bh, preferred_element_type=F32))


def _sigmoid(x):
    return jax.nn.sigmoid(x)


def _silu(x):
    return x * _sigmoid(x)


def _dsilu(x):
    s = _sigmoid(x)
    return s * (1.0 + x * (1.0 - s))


def _softplus(x):
    return jnp.maximum(x, 0.0) + jnp.log(1.0 + jnp.exp(-jnp.abs(x)))


def _rowsum(x):
    return jnp.sum(x, axis=1, keepdims=True)


def _col_to_row(col, eye):
    return jnp.sum(eye * col, axis=0, keepdims=True)


def _row_to_col(row, eye):
    return jnp.sum(eye * row, axis=1, keepdims=True)


def _pick(dim, pref, unit=128):
    if dim <= pref:
        return dim
    t = pref
    while t >= unit:
        if dim % t == 0:
            return t
        t -= unit
    return dim


def _params(*sem):
    return pltpu.CompilerParams(dimension_semantics=tuple(sem), vmem_limit_bytes=VMEM_LIMIT)


def _mm(a, b, *, ta=False, tb=False, alpha=1.0, res=None, out_dtype=F32, name="mm"):
    m = a.shape[1] if ta else a.shape[0]
    k = a.shape[0] if ta else a.shape[1]
    n = b.shape[0] if tb else b.shape[1]
    assert k == (b.shape[1] if tb else b.shape[0])
    tm, tn, tk = _pick(m, MM_TM), _pick(n, MM_TN), _pick(k, MM_TK)
    nk = k // tk
    a_spec = pl.BlockSpec((tk, tm), lambda i, j, l: (l, i)) if ta else pl.BlockSpec((tm, tk), lambda i, j, l: (i, l))
    b_spec = pl.BlockSpec((tn, tk), lambda i, j, l: (j, l)) if tb else pl.BlockSpec((tk, tn), lambda i, j, l: (l, j))
    o_spec = pl.BlockSpec((tm, tn), lambda i, j, l: (i, j))
    dims = (((0 if ta else 1,), (1 if tb else 0,)), ((), ()))
    has_res = res is not None

    def finish(r, r_ref, o_ref):
        if alpha != 1.0:
            r = r * alpha
        if has_res:
            r = r + r_ref[...]
        o_ref[...] = r.astype(out_dtype)

    def body(*refs):
        a_ref, b_ref = refs[0], refs[1]
        r_ref = refs[2] if has_res else None
        o_ref = refs[3] if has_res else refs[2]
        part = lax.dot_general(_bf(a_ref[...]), _bf(b_ref[...]), dims, preferred_element_type=F32)
        if nk == 1:
            finish(part, r_ref, o_ref)
            return
        acc = refs[-1]
        step = pl.program_id(2)

        @pl.when(step == 0)
        def _():
            acc[...] = part

        @pl.when(step != 0)
        def _():
            acc[...] += part

        @pl.when(step == nk - 1)
        def _():
            finish(acc[...], r_ref, o_ref)

    ins = [a, b] + ([res] if has_res else [])
    in_specs = [a_spec, b_spec] + ([o_spec] if has_res else [])
    return pl.pallas_call(
        body, name=name, grid=(m // tm, n // tn, nk), in_specs=in_specs, out_specs=o_spec,
        out_shape=jax.ShapeDtypeStruct((m, n), out_dtype),
        scratch_shapes=[pltpu.VMEM((tm, tn), F32)] if nk > 1 else [],
        compiler_params=_params(_PAR, _PAR, _ARB))(*ins)


def _row_spec(tr, w):
    return pl.BlockSpec((tr, w), lambda i: (i, 0))


def _full_spec(shape):
    return pl.BlockSpec(shape, lambda i: tuple(0 for _ in shape))


def _view(arr, off, width):
    return arr, off, width


def _view_rows(view, tr):
    _, off, width = view
    assert off % width == 0
    return pl.BlockSpec((tr, width), lambda i: (i, off // width))


def _view_tile(view, rows, bw, cidx=lambda c: c):
    _, off, width = view
    assert off % bw == 0 and width % bw == 0
    return pl.BlockSpec((rows, bw), lambda c, g: (cidx(c), off // bw + g))


def _rmsnorm_fwd(x, g, name):
    t, d = x.shape
    tr = _pick(t, 256, 8)

    def body(x_ref, g_ref, o_ref):
        xv = x_ref[...]
        r = lax.rsqrt(jnp.mean(xv * xv, axis=1, keepdims=True) + EPS)
        o_ref[...] = (xv * r * g_ref[...]).astype(BF16)

    return pl.pallas_call(
        body, name=name, grid=(t // tr,), in_specs=[_row_spec(tr, d), _full_spec((1, d))],
        out_specs=_row_spec(tr, d), out_shape=jax.ShapeDtypeStruct((t, d), BF16),
        compiler_params=_params(_PAR))(x, g)


def _rmsnorm_bwd(x, g, dn, res, name):
    t, d = x.shape
    tr = _pick(t, 256, 8)

    def body(x_ref, g_ref, dn_ref, r_ref, dx_ref, dxb_ref, dg_ref):
        @pl.when(pl.program_id(0) == 0)
        def _():
            dg_ref[...] = jnp.zeros_like(dg_ref)

        xv = x_ref[...]
        r = lax.rsqrt(jnp.mean(xv * xv, axis=1, keepdims=True) + EPS)
        xh = xv * r
        dy = dn_ref[...]
        dg_ref[...] += jnp.sum(dy * xh, axis=0, keepdims=True)
        dxh = dy * g_ref[...]
        dx = r_ref[...] + r * (dxh - xh * jnp.mean(dxh * xh, axis=1, keepdims=True))
        dx_ref[...] = dx
        dxb_ref[...] = dx.astype(BF16)

    return pl.pallas_call(
        body, name=name, grid=(t // tr,),
        in_specs=[_row_spec(tr, d), _full_spec((1, d)), _row_spec(tr, d), _row_spec(tr, d)],
        out_specs=[_row_spec(tr, d), _row_spec(tr, d), _full_spec((1, d))],
        out_shape=[jax.ShapeDtypeStruct((t, d), F32), jax.ShapeDtypeStruct((t, d), BF16),
                   jax.ShapeDtypeStruct((1, d), F32)],
        compiler_params=_params(_ARB))(x, g, dn, res)


FFN_TN = 256


def _ffn_in_act(n, w_in, name):
    t, d = n.shape
    tm = _pick(t, MM_TM)
    nf = D_FF // FFN_TN

    def body(n_ref, wa_ref, wb_ref, a_ref, b_ref, hm_ref):
        nv = n_ref[...]
        a = jnp.dot(nv, wa_ref[...], preferred_element_type=F32)
        b = jnp.dot(nv, wb_ref[...], preferred_element_type=F32)
        a_ref[...] = a.astype(BF16)
        b_ref[...] = b.astype(BF16)
        hm_ref[...] = (_silu(a) * b).astype(BF16)

    tile = pl.BlockSpec((tm, FFN_TN), lambda i, j: (i, j))
    return pl.pallas_call(
        body, name=name, grid=(t // tm, nf),
        in_specs=[pl.BlockSpec((tm, d), lambda i, j: (i, 0)), pl.BlockSpec((d, FFN_TN), lambda i, j: (0, j)),
                  pl.BlockSpec((d, FFN_TN), lambda i, j: (0, nf + j))],
        out_specs=[tile, tile, tile], out_shape=[jax.ShapeDtypeStruct((t, D_FF), BF16)] * 3,
        compiler_params=_params(_PAR, _PAR))(n, w_in, w_in)


def _ffn_dact(dout, w_out, a, b, name):
    t, d = dout.shape
    tm = _pick(t, MM_TM)

    def body(do_ref, w_ref, a_ref, b_ref, da_ref, db_ref, hm_ref):
        dh = 0.5 * _dot_nt(do_ref[...], w_ref[...])
        av = a_ref[...].astype(F32)
        bv = b_ref[...].astype(F32)
        sa = _silu(av)
        da_ref[...] = (dh * bv * _dsilu(av)).astype(BF16)
        db_ref[...] = (dh * sa).astype(BF16)
        hm_ref[...] = (sa * bv).astype(BF16)

    tile = pl.BlockSpec((tm, FFN_TN), lambda i, j: (i, j))
    return pl.pallas_call(
        body, name=name, grid=(t // tm, D_FF // FFN_TN),
        in_specs=[pl.BlockSpec((tm, d), lambda i, j: (i, 0)), pl.BlockSpec((FFN_TN, d), lambda i, j: (j, 0)), tile, tile],
        out_specs=[tile, tile, tile], out_shape=[jax.ShapeDtypeStruct((t, D_FF), BF16)] * 3,
        compiler_params=_params(_PAR, _PAR))(dout, w_out, a, b)


def _merge_fwd(yh, yg, gh, gg):
    t, d = yh.shape
    tr = _pick(t, 256, 8)

    def body(yh_ref, yg_ref, gh_ref, gg_ref, o_ref):
        o_ref[...] = (_sigmoid(gh_ref[...]) * yh_ref[...] + _sigmoid(gg_ref[...]) * yg_ref[...]).astype(BF16)

    return pl.pallas_call(
        body, name="merge_fwd", grid=(t // tr,),
        in_specs=[_row_spec(tr, d), _row_spec(tr, d), _view_rows(gh, tr), _view_rows(gg, tr)],
        out_specs=_row_spec(tr, d),
        out_shape=jax.ShapeDtypeStruct((t, d), BF16), compiler_params=_params(_PAR))(yh, yg, gh[0], gg[0])


def _merge_bwd(dy, yh, yg, gh, gg):
    t, d = yh.shape
    tr = _pick(t, 256, 8)

    def body(dy_ref, yh_ref, yg_ref, gh_ref, gg_ref, dyh_ref, dyg_ref, dgh_ref, dgg_ref):
        dyv = dy_ref[...]
        sh = _sigmoid(gh_ref[...])
        sg = _sigmoid(gg_ref[...])
        dyh_ref[...] = (dyv * sh).astype(BF16)
        dyg_ref[...] = (dyv * sg).astype(BF16)
        dgh_ref[...] = (dyv * yh_ref[...] * sh * (1.0 - sh)).astype(BF16)
        dgg_ref[...] = (dyv * yg_ref[...] * sg * (1.0 - sg)).astype(BF16)

    return pl.pallas_call(
        body, name="merge_bwd", grid=(t // tr,),
        in_specs=[_row_spec(tr, d)] * 3 + [_view_rows(gh, tr), _view_rows(gg, tr)], out_specs=[_row_spec(tr, d)] * 4,
        out_shape=[jax.ShapeDtypeStruct((t, d), BF16)] * 4,
        compiler_params=_params(_PAR))(dy, yh, yg, gh[0], gg[0])


def _final_loss(h, g, tgt):
    t, d = h.shape
    tr = _pick(t, 256, 8)

    def body(h_ref, g_ref, t_ref, loss_ref, dh_ref, dhb_ref, dg_ref):
        @pl.when(pl.program_id(0) == 0)
        def _():
            dg_ref[...] = jnp.zeros_like(dg_ref)
            loss_ref[...] = jnp.zeros_like(loss_ref)

        xv = h_ref[...]
        gv = g_ref[...]
        r = lax.rsqrt(jnp.mean(xv * xv, axis=1, keepdims=True) + EPS)
        xh = xv * r
        err = xh * gv - t_ref[...]
        loss_ref[...] += 0.5 * jnp.sum(jnp.mean(err * err, axis=1, keepdims=True), axis=0, keepdims=True)
        dy = err * (1.0 / d)
        dg_ref[...] += jnp.sum(dy * xh, axis=0, keepdims=True)
        dxh = dy * gv
        dh = r * (dxh - xh * jnp.mean(dxh * xh, axis=1, keepdims=True))
        dh_ref[...] = dh
        dhb_ref[...] = dh.astype(BF16)

    return pl.pallas_call(
        body, name="final_loss", grid=(t // tr,),
        in_specs=[_row_spec(tr, d), _full_spec((1, d)), _row_spec(tr, d)],
        out_specs=[_full_spec((1, 128)), _row_spec(tr, d), _row_spec(tr, d), _full_spec((1, d))],
        out_shape=[jax.ShapeDtypeStruct((1, 128), F32), jax.ShapeDtypeStruct((t, d), F32),
                   jax.ShapeDtypeStruct((t, d), BF16), jax.ShapeDtypeStruct((1, d), F32)],
        compiler_params=_params(_ARB))(h, g, tgt)


def _hg_consts():
    c = CHUNK
    t = np.arange(c)
    mats, masks = [], []
    for lvl in range(6):
        m = 1 << lvl
        blk = t // m
        mat = np.zeros((c, c), np.float32)
        for tt in range(c):
            b = blk[tt]
            if b % 2 == 1:
                mat[tt, b * m:tt + 1] = 1.0
            else:
                mat[tt, tt + 1:(b + 1) * m] = 1.0
        mats.append(mat)
        same = (t[:, None] // (2 * m)) == (t[None, :] // (2 * m))
        masks.append((same & (blk[:, None] % 2 == 1) & (blk[None, :] % 2 == 0)).astype(np.float32))
    pre = np.tril(np.ones((c, c), np.float32))
    suf = np.triu(np.ones((c, c), np.float32), 1)
    mstack = np.concatenate(mats + [pre, suf], 0)
    masks.append(np.eye(c, dtype=np.float32))
    return (jnp.asarray(mstack, BF16), jnp.asarray(mstack.T.copy(), BF16), jnp.asarray(np.stack(masks), F32),
            jnp.asarray(np.eye(HEAD, dtype=np.float32)))


def _gd_consts():
    c = CHUNK
    incl = np.tril(np.ones((c, c), np.float32))
    strict = np.tril(np.ones((c, c), np.float32), -1)
    eye = np.eye(c, dtype=np.float32)
    masks = np.stack([incl, strict, eye, incl.T.copy()])
    sel = np.zeros((GD_HEADS, HEAD, 2 * HEAD), np.float32)
    for j in range(GD_HEADS):
        sel[j, j, :HEAD] = 1.0
        sel[j, GD_HEADS + j, HEAD:] = 1.0
    return (jnp.asarray(incl, BF16), jnp.asarray(incl.T.copy(), BF16), jnp.asarray(masks, F32), jnp.asarray(sel, BF16))


def _chunks_per_step(nc):
    for cb in (32 // HPS, 2, 1):
        if nc % cb == 0:
            return cb
    return 1


def _hg_prep(hq, hf, lg):
    lb = _sigmoid(lg[0:1, :] - lg[1:2, :])
    sg = _sigmoid(hf)
    sgn = _sigmoid(-hf)
    f = lb + (1.0 - lb) * sg
    lf = jnp.log(f)
    kk = (1.0 - lb) * sgn
    q = _silu(hq) * (HEAD ** -0.5)
    return lb, sg, sgn, f, lf, kk, q


def _mx_each(m, xs):
    wide = [jnp.concatenate(_split3(x), axis=1) for x in xs]
    prods = [jnp.dot(m, w, preferred_element_type=F32) for w in wide]
    return [p[:, :HEAD] + p[:, HEAD:2 * HEAD] + p[:, 2 * HEAD:] for p in prods]


def _hg_scores(q, kk, ex, mask_ref):
    p = [mask_ref[6] * _rowsum(a * b) for a, b in zip(q, kk)]
    for lvl in range(6):
        el = [e[lvl * CHUNK:(lvl + 1) * CHUNK] for e in ex]
        d = [_dot_nt(a * e, b * e) for a, b, e in zip(q, kk, el)]
        p = [x + mask_ref[lvl] * y for x, y in zip(p, d)]
    return p


def _hgrn_fwd(hq, hf, hi, hg, logits, gain, consts):
    t = hq[0].shape[0]
    nc = t // CHUNK
    cb = _chunks_per_step(nc)
    rows = cb * CHUNK
    mstack, _, masks, eye = consts
    tile = pl.BlockSpec((rows, HPS * HEAD), lambda c, g: (c, g))

    def body(hq_ref, hf_ref, hi_ref, hg_ref, lg_ref, gain_ref, m_ref, mask_ref, eye_ref,
             oraw_ref, og_ref, ssave_ref, state):
        c = pl.program_id(0)
        g = pl.program_id(1)

        @pl.when(c == 0)
        def _():
            for hh in range(HPS):
                state[g * HPS + hh] = jnp.zeros((HEAD, HEAD), F32)

        lg_all = lg_ref[...]
        gain_v = gain_ref[...]

        def one(i, carry):
            sl = pl.ds(pl.multiple_of(i * CHUNK, CHUNK), CHUNK)
            hs = range(HPS)
            heads = [g * HPS + hh for hh in hs]
            ln = [slice(hh * HEAD, (hh + 1) * HEAD) for hh in hs]
            preps = [_hg_prep(hq_ref[sl, s], hf_ref[sl, s], lg_all[:, s]) for s in ln]
            lf, kk, q = [p[4] for p in preps], [p[5] for p in preps], [p[6] for p in preps]
            v = [hi_ref[sl, s] for s in ln]
            ex = [jnp.exp(x) for x in _mx_each(m_ref[...], lf)]
            eb = [e[6 * CHUNK:7 * CHUNK] for e in ex]
            esfx = [e[7 * CHUNK:8 * CHUNK] for e in ex]
            p = _hg_scores(q, kk, ex, mask_ref)
            s0 = [state[h] for h in heads]
            o = _each(lambda a, e, s, pp, vv: _dot(a * e, s) + _dot(pp, vv), q, eb, s0, p, v)
            eye_v = eye_ref[...]
            s1 = _each(lambda s, e, kx, ef, vv: s * _row_to_col(e[CHUNK - 1:CHUNK, :], eye_v) + _dot_tn(kx * ef, vv),
                       s0, eb, kk, esfx, v)
            for hh in hs:
                ssave_ref[i, hh] = s0[hh]
                state[heads[hh]] = s1[hh]
                oraw_ref[sl, ln[hh]] = o[hh]
                r = lax.rsqrt(jnp.mean(o[hh] * o[hh], axis=1, keepdims=True) + EPS)
                og_ref[sl, ln[hh]] = (o[hh] * r * gain_v * _silu(hg_ref[sl, ln[hh]])).astype(BF16)
            return carry

        lax.fori_loop(0, cb, one, 0, unroll=2)

    return pl.pallas_call(
        body, name="hgrn_fwd", grid=(nc // cb, HG_HEADS // HPS),
        in_specs=[_view_tile(v, rows, HPS * HEAD) for v in (hq, hf, hi, hg)] + [
                  pl.BlockSpec((2, HPS * HEAD), lambda c, g: (0, g)),
                  pl.BlockSpec((1, HEAD), lambda c, g: (0, 0)),
                  pl.BlockSpec(mstack.shape, lambda c, g: (0, 0)),
                  pl.BlockSpec(masks.shape, lambda c, g: (0, 0, 0)),
                  pl.BlockSpec(eye.shape, lambda c, g: (0, 0))],
        out_specs=[tile, tile, pl.BlockSpec((cb, HPS, HEAD, HEAD), lambda c, g: (c, g, 0, 0))],
        out_shape=[jax.ShapeDtypeStruct((t, HG_HEADS * HEAD), F32), jax.ShapeDtypeStruct((t, HG_HEADS * HEAD), BF16),
                   jax.ShapeDtypeStruct((nc, HG_HEADS, HEAD, HEAD), F32)],
        scratch_shapes=[pltpu.VMEM((HG_HEADS, HEAD, HEAD), F32)],
        compiler_params=_params(_ARB, _ARB))(hq[0], hf[0], hi[0], hg[0], logits, gain, mstack, masks, eye)


def _hgrn_bwd(hq, hf, hi, hg, logits, gain, oraw, ssave, dog, consts):
    t = hq[0].shape[0]
    nc = t // CHUNK
    cb = _chunks_per_step(nc)
    rows = cb * CHUNK
    nb = nc // cb
    mstack, mstack_t, masks, eye = consts
    tile = pl.BlockSpec((rows, HPS * HEAD), lambda c, g: (nb - 1 - c, g))

    def body(hq_ref, hf_ref, hi_ref, hg_ref, lg_ref, gain_ref, oraw_ref, ssave_ref, dog_ref, m_ref, mt_ref,
             mask_ref, eye_ref, dhq_ref, dhf_ref, dhi_ref, dhg_ref, dgain_ref, dlb_ref, dstate):
        c = pl.program_id(0)
        g = pl.program_id(1)

        @pl.when(c == 0)
        def _():
            for hh in range(HPS):
                dstate[g * HPS + hh] = jnp.zeros((HEAD, HEAD), F32)

        @pl.when((c == 0) & (g == 0))
        def _():
            dgain_ref[...] = jnp.zeros_like(dgain_ref)
            dlb_ref[...] = jnp.zeros_like(dlb_ref)

        lg_all = lg_ref[...]
        gain_v = gain_ref[...]
        eye_v = eye_ref[...]
        last_row = (lax.broadcasted_iota(jnp.int32, (CHUNK, HEAD), 0) == CHUNK - 1).astype(F32)

        def one(j, carry):
            i = cb - 1 - j
            sl = pl.ds(pl.multiple_of(i * CHUNK, CHUNK), CHUNK)
            hs = range(HPS)
            heads = [g * HPS + hh for hh in hs]
            ln = [slice(hh * HEAD, (hh + 1) * HEAD) for hh in hs]
            hqv = [hq_ref[sl, s] for s in ln]
            hgv = [hg_ref[sl, s] for s in ln]
            preps = [_hg_prep(a, hf_ref[sl, s], lg_all[:, s]) for a, s in zip(hqv, ln)]
            lb, sg, sgn, f, lf, kk, q = ([p[n] for p in preps] for n in range(7))
            v = [hi_ref[sl, s] for s in ln]
            ex = [jnp.exp(x) for x in _mx_each(m_ref[...], lf)]
            eb = [e[6 * CHUNK:7 * CHUNK] for e in ex]
            esfx = [e[7 * CHUNK:8 * CHUNK] for e in ex]
            p = _hg_scores(q, kk, ex, mask_ref)
            s0 = [ssave_ref[i, hh] for hh in hs]
            ds = [dstate[h] for h in heads]

            o = [oraw_ref[sl, s] for s in ln]
            r = [lax.rsqrt(jnp.mean(x * x, axis=1, keepdims=True) + EPS) for x in o]
            on = _each(lambda x, y: x * y, o, r)
            dg_out = [dog_ref[sl, s] for s in ln]
            sgate = [_silu(x) for x in hgv]
            for hh in hs:
                dhg_ref[sl, ln[hh]] = (dg_out[hh] * on[hh] * gain_v * _dsilu(hgv[hh])).astype(BF16)
            dgain_ref[...] += sum(jnp.sum(d * s * n, axis=0, keepdims=True) for d, s, n in zip(dg_out, sgate, on))
            don = _each(lambda d, s: d * s * gain_v, dg_out, sgate)
            do = _each(lambda rr, dn, n: rr * (dn - n * jnp.mean(dn * n, axis=1, keepdims=True)), r, don, on)

            dp = _each(_dot_nt, do, v)
            dv = _each(lambda pp, d, kx, ef, s: _dot_tn(pp, d) + _dot(kx * ef, s), p, do, kk, esfx, ds)
            dqb = _each(_dot_nt, do, s0)
            dkx = _each(_dot_nt, v, ds)
            diag = [_rowsum(mask_ref[6] * x) for x in dp]
            dq = _each(lambda a, e, d, kx: a * e + d * kx, dqb, eb, diag, kk)
            dk = _each(lambda a, e, d, qq: a * e + d * qq, dkx, esfx, diag, q)
            dxs = [[] for _ in hs]
            for lvl in range(6):
                el = [e[lvl * CHUNK:(lvl + 1) * CHUNK] for e in ex]
                gm = [mask_ref[lvl] * x for x in dp]
                a1 = _each(lambda m_, kx, e: _dot(m_, kx * e), gm, kk, el)
                a2 = _each(lambda m_, qq, e: _dot_tn(m_, qq * e), gm, q, el)
                dq = _each(lambda x, a, e: x + a * e, dq, a1, el)
                dk = _each(lambda x, a, e: x + a * e, dk, a2, el)
                for hh in hs:
                    dxs[hh].append((a1[hh] * q[hh] + a2[hh] * kk[hh]) * el[hh])
            e_end_row = [e[CHUNK - 1:CHUNK, :] for e in eb]
            ds_new = _each(lambda qq, e, d, er, s: _dot_tn(qq * e, d) + _row_to_col(er, eye_v) * s, q, eb, do, e_end_row, ds)
            for hh in hs:
                dstate[heads[hh]] = ds_new[hh]
                dend_row = _col_to_row(_rowsum(s0[hh] * ds[hh]), eye_v)
                dxs[hh].append(dqb[hh] * q[hh] * eb[hh] + last_row * (e_end_row[hh] * dend_row))
                dxs[hh].append(dkx[hh] * kk[hh] * esfx[hh])
            dlf = _mx_each(mt_ref[...], [jnp.concatenate(x, axis=0) for x in dxs])

            for hh in hs:
                dhi_ref[sl, ln[hh]] = dv[hh].astype(BF16)
                dhq_ref[sl, ln[hh]] = (dq[hh] * (HEAD ** -0.5) * _dsilu(hqv[hh])).astype(BF16)
                df = dlf[hh] / f[hh]
                dsig = (1.0 - lb[hh]) * sg[hh] * sgn[hh]
                dhf_ref[sl, ln[hh]] = ((df - dk[hh]) * dsig).astype(BF16)
                dlb_t = jnp.sum(df * sgn[hh] - dk[hh] * sgn[hh], axis=0, keepdims=True)
                dlb_ref[pl.ds(heads[hh], 1), :] += dlb_t * lb[hh] * (1.0 - lb[hh])
            return carry

        lax.fori_loop(0, cb, one, 0, unroll=2)

    outs = [jax.ShapeDtypeStruct((t, HG_HEADS * HEAD), BF16)] * 4 + [
        jax.ShapeDtypeStruct((1, HEAD), F32), jax.ShapeDtypeStruct((HG_HEADS, HEAD), F32)]
    return pl.pallas_call(
        body, name="hgrn_bwd", grid=(nb, HG_HEADS // HPS),
        in_specs=[_view_tile(v, rows, HPS * HEAD, lambda c: nb - 1 - c) for v in (hq, hf, hi, hg)] + [
                  pl.BlockSpec((2, HPS * HEAD), lambda c, g: (0, g)),
                  pl.BlockSpec((1, HEAD), lambda c, g: (0, 0)), tile,
                  pl.BlockSpec((cb, HPS, HEAD, HEAD), lambda c, g: (nb - 1 - c, g, 0, 0)), tile,
                  pl.BlockSpec(mstack.shape, lambda c, h: (0, 0)),
                  pl.BlockSpec(mstack_t.shape, lambda c, h: (0, 0)),
                  pl.BlockSpec(masks.shape, lambda c, h: (0, 0, 0)),
                  pl.BlockSpec(eye.shape, lambda c, h: (0, 0))],
        out_specs=[tile, tile, tile, tile, pl.BlockSpec((1, HEAD), lambda c, h: (0, 0)),
                   pl.BlockSpec((HG_HEADS, HEAD), lambda c, h: (0, 0))],
        out_shape=outs, scratch_shapes=[pltpu.VMEM((HG_HEADS, HEAD, HEAD), F32)],
        compiler_params=_params(_ARB, _ARB))(hq[0], hf[0], hi[0], hg[0], logits, gain, oraw, ssave, dog, mstack,
                                             mstack_t, masks, eye)


CONV_W = 512


def _per_head(fn, *arrs):
    width = arrs[0].shape[1]
    return jnp.concatenate([fn(*[a[:, j:j + HEAD] for a in arrs]) for j in range(0, width, HEAD)], axis=1)


def _shift_down(xv, halo, d, top_rows):
    if d == 0:
        return xv, xv[0:8]
    main = pltpu.roll(xv, d, 0)
    top = jnp.where(top_rows < d, pltpu.roll(halo, d, 0), main[0:8])
    return main, top


def _conv_parts(x_ref, halo_ref, w_ref, first):
    xv = x_ref[...]
    halo = jnp.where(first, 0.0, halo_ref[...])
    top_rows = lax.broadcasted_iota(jnp.int32, (8, xv.shape[1]), 0)
    shifted = [_shift_down(xv, halo, CONV_K - 1 - j, top_rows) for j in range(CONV_K)]
    w = w_ref[...]
    acc = sum(shifted[j][0] * w[j:j + 1, :] for j in range(CONV_K))
    acc_top = sum(shifted[j][1] * w[j:j + 1, :] for j in range(CONV_K))
    return shifted, acc, acc_top


def _conv_fwd(x, w8, l2scale, name):
    x, off, width = x
    t = x.shape[0]
    o = off // CONV_W
    tr = _pick(t, 512, 8)

    def post(cv):
        s = _silu(cv)
        if l2scale is not None:
            s = _per_head(lambda sh: sh * (lax.rsqrt(_rowsum(sh * sh) + EPS) * l2scale), s)
        return s

    def body(x_ref, halo_ref, w_ref, o_ref):
        _, acc, acc_top = _conv_parts(x_ref, halo_ref, w_ref, pl.program_id(1) == 0)
        o_ref[...] = post(acc)
        o_ref[0:8, :] = post(acc_top)

    return pl.pallas_call(
        body, name=name, grid=(width // CONV_W,t // tr),
        in_specs=[pl.BlockSpec((tr, CONV_W), lambda j, i: (i, o + j)),
                  pl.BlockSpec((8, CONV_W), lambda j, i: (jnp.maximum(i * (tr // 8) - 1, 0), o + j)),
                  pl.BlockSpec((8, CONV_W), lambda j, i: (0, j))],
        out_specs=pl.BlockSpec((tr, CONV_W), lambda j, i: (i, j)),
        out_shape=jax.ShapeDtypeStruct((t, width), F32), compiler_params=_params(_PAR, _PAR))(x, x, w8)


def _conv_bwd_a(x, w8, dy, l2scale, name):
    x, off, width = x
    t = x.shape[0]
    o = off // CONV_W
    tr = _pick(t, 512, 8)

    def l2_bwd(s, dyh):
        r = lax.rsqrt(_rowsum(s * s) + EPS)
        y0 = s * r
        dy0 = dyh * l2scale
        return r * (dy0 - y0 * _rowsum(dy0 * y0))

    def to_dc(cv, dyv):
        if l2scale is not None:
            dyv = _per_head(l2_bwd, _silu(cv), dyv)
        return dyv * _dsilu(cv)

    def body(x_ref, halo_ref, w_ref, dy_ref, dc_ref, dw_ref):
        @pl.when(pl.program_id(1) == 0)
        def _():
            dw_ref[...] = jnp.zeros_like(dw_ref)

        shifted, acc, acc_top = _conv_parts(x_ref, halo_ref, w_ref, pl.program_id(1) == 0)
        dyv = dy_ref[...]
        dc = to_dc(acc, dyv)
        dc_top = to_dc(acc_top, dyv[0:8])
        dc_ref[...] = dc
        dc_ref[0:8, :] = dc_top
        rest = (lax.broadcasted_iota(jnp.int32, dc.shape, 0) >= 8).astype(F32)
        dc_rest = dc * rest
        for j in range(CONV_K):
            dw_ref[j:j + 1, :] += (jnp.sum(dc_rest * shifted[j][0], axis=0, keepdims=True)
                                   + jnp.sum(dc_top * shifted[j][1], axis=0, keepdims=True))

    return pl.pallas_call(
        body, name=name, grid=(width // CONV_W,t // tr),
        in_specs=[pl.BlockSpec((tr, CONV_W), lambda j, i: (i, o + j)),
                  pl.BlockSpec((8, CONV_W), lambda j, i: (jnp.maximum(i * (tr // 8) - 1, 0), o + j)),
                  pl.BlockSpec((8, CONV_W), lambda j, i: (0, j)),
                  pl.BlockSpec((tr, CONV_W), lambda j, i: (i, j))],
        out_specs=[pl.BlockSpec((tr, CONV_W), lambda j, i: (i, j)), pl.BlockSpec((8, CONV_W), lambda j, i: (0, j))],
        out_shape=[jax.ShapeDtypeStruct((t, width), F32), jax.ShapeDtypeStruct((8, width), F32)],
        compiler_params=_params(_PAR, _ARB))(x, x, w8, dy)


def _conv_bwd_b(dc, w8, name):
    t, width = dc.shape
    tr = _pick(t, 512, 8)
    nt = t // tr

    def body(dc_ref, halo_ref, w_ref, dx_ref):
        dcv = dc_ref[...]
        halo = jnp.where(pl.program_id(1) == nt - 1, 0.0, halo_ref[...])
        w = w_ref[...]
        bot_rows = lax.broadcasted_iota(jnp.int32, (8, CONV_W), 0)
        acc = dcv * w[CONV_K - 1:CONV_K, :]
        acc_bot = dcv[tr - 8:tr] * w[CONV_K - 1:CONV_K, :]
        for d in range(1, CONV_K):
            main = pltpu.roll(dcv, tr - d, 0)
            bot = jnp.where(bot_rows >= 8 - d, pltpu.roll(halo, 8 - d, 0), main[tr - 8:tr])
            wj = w[CONV_K - 1 - d:CONV_K - d, :]
            acc = acc + main * wj
            acc_bot = acc_bot + bot * wj
        dx_ref[...] = acc.astype(BF16)
        dx_ref[tr - 16:tr, :] = jnp.concatenate([acc[tr - 16:tr - 8], acc_bot], axis=0).astype(BF16)

    return pl.pallas_call(
        body, name=name, grid=(width // CONV_W,nt),
        in_specs=[pl.BlockSpec((tr, CONV_W), lambda j, i: (i, j)),
                  pl.BlockSpec((8, CONV_W), lambda j, i: (jnp.minimum((i + 1) * (tr // 8), t // 8 - 1), j)),
                  pl.BlockSpec((8, CONV_W), lambda j, i: (0, j))],
        out_specs=pl.BlockSpec((tr, CONV_W), lambda j, i: (i, j)),
        out_shape=jax.ShapeDtypeStruct((t, width), BF16), compiler_params=_params(_PAR, _PAR))(dc, dc, w8)


def _each(f, *lists):
    return [f(*xs) for xs in zip(*lists)]


def _split2_each(xs):
    hi = [_bf(x) for x in xs]
    lo = [_bf(x - h.astype(F32)) for x, h in zip(xs, hi)]
    return hi, lo


def _hp_each(a_split, b_split):
    (ah, al), (bh, bl) = a_split, b_split
    rows = ah[0].shape[0]
    d12 = [jnp.dot(jnp.concatenate([x, y], axis=0), z, preferred_element_type=F32) for x, y, z in zip(ah, al, bh)]
    d3 = [jnp.dot(x, y, preferred_element_type=F32) for x, y in zip(ah, bl)]
    return [d[:rows] + d[rows:] + e for d, e in zip(d12, d3)]


def _tri_inv_each(a_list, eye):
    ns = [-a for a in a_list]
    ps = [eye + n for n in ns]
    n_split = _split2_each(ns)
    for _ in range(5):
        ns = _hp_each(n_split, n_split)
        n_split = _split2_each(ns)
        ps = [p + d for p, d in zip(ps, _hp_each(_split2_each(ps), n_split))]
    return ps


def _gd_gates(gab, alog, dtb):
    sp_arg = gab + dtb
    return sp_arg, -jnp.exp(alog) * _softplus(sp_arg), _sigmoid(gab)


def _gd_chunks(q, k, v, g_all, beta_all, sel, l_ref, mask_ref):
    incl, strict, eye, upper = mask_ref[0], mask_ref[1], mask_ref[2], mask_ref[3]
    lmat = l_ref[...]
    gates = jnp.concatenate(_split3(g_all) + _split3(beta_all), axis=0)
    picked = [jnp.dot(gates, s, preferred_element_type=F32) for s in sel]
    c = CHUNK
    gb = [p[0:c, :HEAD] + p[c:2 * c, :HEAD] + p[2 * c:3 * c, :HEAD] for p in picked]
    bb = [p[3 * c:4 * c, HEAD:] + p[4 * c:5 * c, HEAD:] + p[5 * c:, HEAD:] for p in picked]
    gam = _mx_each(lmat, gb)
    gam_row = [jnp.sum(x[:, :CHUNK] * upper, axis=0, keepdims=True) for x in gb]
    lm = _each(lambda gm, gr: incl * jnp.exp(jnp.minimum(gm[:, :CHUNK] - gr, 0.0)), gam, gam_row)
    kb = _each(lambda x, b: x * b, k, bb)
    a = _each(lambda x, y, m: strict * _dot_nt(x, y) * m, kb, k, lm)
    tm = _tri_inv_each(a, eye)
    eg = [jnp.exp(x) for x in gam]
    vb = _each(lambda x, b: x * b, v, bb)
    kbg = _each(lambda x, e: x * e, kb, eg)
    uw = _each(lambda t_, x, y: _dot(t_, jnp.concatenate([x, y], axis=1)), tm, vb, kbg)
    u = [x[:, :HEAD] for x in uw]
    w = [x[:, HEAD:] for x in uw]
    qk = _each(lambda x, y, m: _dot_nt(x, y) * m, q, k, lm)
    g_end = [x[CHUNK - 1:CHUNK, :] for x in gam]
    ekg = _each(lambda e, x: jnp.exp(e - x), g_end, gam)
    ge = [jnp.exp(e) for e in g_end]
    kg = _each(lambda x, e: x * e, k, ekg)
    qg = _each(lambda x, e: x * e, q, eg)
    names = ("bb", "lm", "kb", "a", "tm", "eg", "vb", "kbg", "u", "w", "qk", "ekg", "ge", "kg", "qg")
    cols = (bb, lm, kb, a, tm, eg, vb, kbg, u, w, qk, ekg, ge, kg, qg)
    return [dict(zip(names, vals)) for vals in zip(*cols)]


def _gd_specs(rows, rev_nb=None):
    def cidx(c):
        return c if rev_nb is None else rev_nb - 1 - c

    qk_tile = pl.BlockSpec((rows, HPS // 2 * HEAD), lambda c, g: (cidx(c), g))
    v_tile = pl.BlockSpec((rows, HPS * HEAD), lambda c, g: (cidx(c), g))
    gab_tile = pl.BlockSpec((rows, HEAD), lambda c, g: (cidx(c), 0))
    return qk_tile, v_tile, gab_tile


def _gdn_fwd(qn, kn, cv, gab, gz, alog, dtb, gain, consts):
    t = qn.shape[0]
    nc = t // CHUNK
    cb = _chunks_per_step(nc)
    rows = cb * CHUNK
    lmat, _, masks, sel = consts
    qk_tile, v_tile, gab_tile = _gd_specs(rows)
    row128 = pl.BlockSpec((1, HEAD), lambda c, h: (0, 0))

    def body(q_ref, k_ref, v_ref, gab_ref, gz_ref, alog_ref, dtb_ref, gain_ref, sel_ref, l_ref, mask_ref,
             oraw_ref, og_ref, ssave_ref, state):
        c = pl.program_id(0)
        g = pl.program_id(1)

        @pl.when(c == 0)
        def _():
            for hh in range(HPS):
                state[g * HPS + hh] = jnp.zeros((HEAD, HEAD), F32)

        alog = alog_ref[...]
        dtb = dtb_ref[...]
        gain_v = gain_ref[...]

        def one(i, carry):
            sl = pl.ds(pl.multiple_of(i * CHUNK, CHUNK), CHUNK)
            _, g_all, beta_all = _gd_gates(gab_ref[sl, :], alog, dtb)
            heads = [g * HPS + hh for hh in range(HPS)]
            lq = [slice(hh // 2 * HEAD, (hh // 2 + 1) * HEAD) for hh in range(HPS)]
            lv = [slice(hh * HEAD, (hh + 1) * HEAD) for hh in range(HPS)]
            chs = _gd_chunks([q_ref[sl, s] for s in lq], [k_ref[sl, s] for s in lq], [v_ref[sl, s] for s in lv],
                             g_all, beta_all, [sel_ref[h] for h in heads], l_ref, mask_ref)
            s0 = [state[h] for h in heads]
            ws = _each(lambda ch, s: _dot(jnp.concatenate([ch["w"], ch["qg"]], axis=0), s), chs, s0)
            v_new = _each(lambda ch, x: ch["u"] - x[:CHUNK], chs, ws)
            o = _each(lambda ch, x, vn: x[CHUNK:] + _dot(ch["qk"], vn), chs, ws, v_new)
            s1 = _each(lambda ch, s, vn: s * ch["ge"] + _dot_tn(ch["kg"], vn), chs, s0, v_new)
            for hh in range(HPS):
                ssave_ref[i, hh] = s0[hh]
                state[heads[hh]] = s1[hh]
                oraw_ref[sl, lv[hh]] = o[hh]
                r = lax.rsqrt(jnp.mean(o[hh] * o[hh], axis=1, keepdims=True) + EPS)
                og_ref[sl, lv[hh]] = (o[hh] * r * gain_v * _silu(gz_ref[sl, lv[hh]])).astype(BF16)
            return carry

        lax.fori_loop(0, cb, one, 0, unroll=2)

    return pl.pallas_call(
        body, name="gdn_fwd", grid=(nc // cb, GD_HEADS // HPS),
        in_specs=[qk_tile, qk_tile, v_tile, gab_tile, _view_tile(gz, rows, HPS * HEAD), row128, row128, row128,
                  pl.BlockSpec(sel.shape, lambda c, g: (0, 0, 0)),
                  pl.BlockSpec(lmat.shape, lambda c, g: (0, 0)),
                  pl.BlockSpec(masks.shape, lambda c, g: (0, 0, 0))],
        out_specs=[v_tile, v_tile, pl.BlockSpec((cb, HPS, HEAD, HEAD), lambda c, g: (c, g, 0, 0))],
        out_shape=[jax.ShapeDtypeStruct((t, GD_HEADS * HEAD), F32), jax.ShapeDtypeStruct((t, GD_HEADS * HEAD), BF16),
                   jax.ShapeDtypeStruct((nc, GD_HEADS, HEAD, HEAD), F32)],
        scratch_shapes=[pltpu.VMEM((GD_HEADS, HEAD, HEAD), F32)],
        compiler_params=_params(_ARB, _ARB))(qn, kn, cv, gab, gz[0], alog, dtb, gain, sel, lmat, masks)


def _gdn_bwd(qn, kn, cv, gab, gz, alog, dtb, gain, oraw, ssave, dog, consts):
    t = qn.shape[0]
    nc = t // CHUNK
    cb = _chunks_per_step(nc)
    rows = cb * CHUNK
    nb = nc // cb
    lmat, lmat_t, masks, sel = consts
    qk_tile, v_tile, gab_tile = _gd_specs(rows, nb)
    row128 = pl.BlockSpec((1, HEAD), lambda c, h: (0, 0))

    def body(q_ref, k_ref, v_ref, gab_ref, gz_ref, alog_ref, dtb_ref, gain_ref, oraw_ref, ssave_ref, dog_ref,
             sel_ref, l_ref, lt_ref, mask_ref,
             dq_ref, dk_ref, dv_ref, dgab_ref, dgz_ref, small_ref, dstate):
        c = pl.program_id(0)
        g = pl.program_id(1)

        @pl.when(c == 0)
        def _():
            for hh in range(HPS):
                dstate[g * HPS + hh] = jnp.zeros((HEAD, HEAD), F32)

        @pl.when((c == 0) & (g == 0))
        def _():
            small_ref[...] = jnp.zeros_like(small_ref)

        alog = alog_ref[...]
        dtb = dtb_ref[...]
        gain_v = gain_ref[...]
        lane = lax.broadcasted_iota(jnp.int32, (1, HEAD), 1)
        last_row = (lax.broadcasted_iota(jnp.int32, (CHUNK, HEAD), 0) == CHUNK - 1).astype(F32)

        def one(j, carry):
            i = cb - 1 - j
            sl = pl.ds(pl.multiple_of(i * CHUNK, CHUNK), CHUNK)
            sp_arg, g_all, beta_all = _gd_gates(gab_ref[sl, :], alog, dtb)
            strict, eye = mask_ref[1], mask_ref[2]
            ltm = lt_ref[...]
            hs = range(HPS)
            heads = [g * HPS + hh for hh in hs]
            lq = [slice(hh // 2 * HEAD, (hh // 2 + 1) * HEAD) for hh in hs]
            lv = [slice(hh * HEAD, (hh + 1) * HEAD) for hh in hs]
            q = [q_ref[sl, s] for s in lq]
            k = [k_ref[sl, s] for s in lq]
            v = [v_ref[sl, s] for s in lv]
            gzv = [gz_ref[sl, s] for s in lv]
            chs = _gd_chunks(q, k, v, g_all, beta_all, [sel_ref[h] for h in heads], l_ref, mask_ref)

            def col(name):
                return [ch[name] for ch in chs]

            def mul(x, y):
                return x * y

            tm, lm, eg, bb = col("tm"), col("lm"), col("eg"), col("bb")
            s0 = [ssave_ref[i, hh] for hh in hs]
            ds = [dstate[h] for h in heads]
            v_new = _each(lambda u, w, s: u - _dot(w, s), col("u"), col("w"), s0)

            o = [oraw_ref[sl, s] for s in lv]
            r = [lax.rsqrt(jnp.mean(x * x, axis=1, keepdims=True) + EPS) for x in o]
            on = _each(mul, o, r)
            dg_out = [dog_ref[sl, s] for s in lv]
            sgate = [_silu(x) for x in gzv]
            for hh in hs:
                dgz_ref[sl, lv[hh]] = (dg_out[hh] * on[hh] * gain_v * _dsilu(gzv[hh])).astype(BF16)
            small_ref[0:1, :] += sum(jnp.sum(d * s * n, axis=0, keepdims=True) for d, s, n in zip(dg_out, sgate, on))
            don = _each(lambda d, s: d * s * gain_v, dg_out, sgate)
            do = _each(lambda rr, dn, n: rr * (dn - n * jnp.mean(dn * n, axis=1, keepdims=True)), r, don, on)

            dv_new = _each(lambda a, d, b, s: _dot_tn(a, d) + _dot(b, s), col("qk"), do, col("kg"), ds)
            dqk = _each(_dot_nt, do, v_new)
            dkg = _each(_dot_nt, v_new, ds)
            dge = _each(lambda s, d: jnp.sum(_rowsum(s * d), axis=0, keepdims=True), s0, ds)
            both = _each(lambda d, dv: jnp.concatenate([d, dv], axis=0), do, dv_new)
            from_s = _each(_dot_nt, both, s0)
            dqg = [x[:CHUNK] for x in from_s]
            dw = [-x[CHUNK:] for x in from_s]
            ds_new = _each(lambda qg, w, bo, ge, s: _dot_tn(jnp.concatenate([qg, -w], axis=0), bo) + ge * s,
                           col("qg"), col("w"), both, col("ge"), ds)
            for hh in hs:
                dstate[heads[hh]] = ds_new[hh]

            side = _each(lambda dv, d: jnp.concatenate([dv, d], axis=1), dv_new, dw)
            back = _each(_dot_tn, tm, side)
            dvb = [x[:, :HEAD] for x in back]
            dkbg = [x[:, HEAD:] for x in back]
            dtm = _each(lambda sd, vb, kbg: _dot_nt(sd, jnp.concatenate([vb, kbg], axis=1)), side, col("vb"), col("kbg"))
            dtt = _each(_dot_nt, dtm, tm)
            da = _each(lambda t_, x: -_dot_tn(t_, x) * strict, tm, dtt)
            dal = _each(mul, da, lm)
            dqk_l = _each(mul, dqk, lm)
            stack = _each(lambda x, y: jnp.concatenate([x, y], axis=0), dal, dqk_l)
            on_k = _each(_dot, stack, k)
            dkb = _each(lambda x, y, e: x[:CHUNK] + y * e, on_k, dkbg, eg)
            dq = _each(lambda x, y, e: x[CHUNK:] + y * e, on_k, dqg, eg)
            dk = _each(lambda st, kb, qq, z, ekg, w_, b: _dot_tn(st, jnp.concatenate([kb, qq], axis=0)) + z * ekg + w_ * b,
                       stack, col("kb"), q, dkg, col("ekg"), dkb, bb)
            gmat = _each(lambda x, a, y, qk: x * a + y * qk, da, col("a"), dqk, col("qk"))
            t_kg = _each(lambda x, y: _rowsum(x * y), dkg, col("kg"))
            dgam = _each(lambda gm, x, qg, t_, y, kbg: (_rowsum(gm) - _row_to_col(jnp.sum(gm, axis=0, keepdims=True), eye)
                                                        + _rowsum(x * qg) - t_ + _rowsum(y * kbg)),
                         gmat, dqg, col("qg"), t_kg, dkbg, col("kbg"))
            dg_end = _each(lambda t_, e, ge: jnp.sum(t_, axis=0, keepdims=True) + e * ge[:, 0:1], t_kg, dge, col("ge"))
            dgam = _each(lambda x, e: x + last_row * e, dgam, dg_end)
            dbeta = _each(lambda x, kk, y, vv: _rowsum(x * kk) + _rowsum(y * vv), dkb, k, dvb, v)
            dg = _mx_each(ltm, dgam)

            for hh in hs:
                dv_ref[sl, lv[hh]] = dvb[hh] * bb[hh]
            fac_g = -jnp.exp(alog) * _sigmoid(sp_arg)
            fac_b = beta_all * (1.0 - beta_all)
            hot_g = [(lane == h).astype(F32) for h in heads]
            hot_b = [(lane == GD_HEADS + h).astype(F32) for h in heads]
            dga = _each(lambda x, hot: x * hot * fac_g, dg, hot_g)
            dgb = _each(lambda x, hot: x * hot * fac_b, dbeta, hot_b)
            small_ref[1:2, :] += sum(jnp.sum(x, axis=0, keepdims=True) for x in dga)
            small_ref[2:3, :] += sum(jnp.sum(x * hot * g_all, axis=0, keepdims=True) for x, hot in zip(dg, hot_g))
            for pair in range(HPS // 2):
                lqp = slice(pair * HEAD, (pair + 1) * HEAD)
                dq_ref[sl, lqp] = dq[2 * pair] + dq[2 * pair + 1]
                dk_ref[sl, lqp] = dk[2 * pair] + dk[2 * pair + 1]
            dgab_ref[sl, :] = sum(a + b for a, b in zip(dga, dgb))
            return carry

        lax.fori_loop(0, cb, one, 0, unroll=2)

    groups = GD_HEADS // HPS
    outs = [jax.ShapeDtypeStruct((t, 1024), F32), jax.ShapeDtypeStruct((t, 1024), F32),
            jax.ShapeDtypeStruct((t, 2048), F32), jax.ShapeDtypeStruct((t, groups * HEAD), F32),
            jax.ShapeDtypeStruct((t, 2048), BF16), jax.ShapeDtypeStruct((8, HEAD), F32)]
    return pl.pallas_call(
        body, name="gdn_bwd", grid=(nb, groups),
        in_specs=[qk_tile, qk_tile, v_tile, gab_tile, _view_tile(gz, rows, HPS * HEAD, lambda c: nb - 1 - c),
                  row128, row128, row128, v_tile,
                  pl.BlockSpec((cb, HPS, HEAD, HEAD), lambda c, g: (nb - 1 - c, g, 0, 0)), v_tile,
                  pl.BlockSpec(sel.shape, lambda c, g: (0, 0, 0)),
                  pl.BlockSpec(lmat.shape, lambda c, g: (0, 0)),
                  pl.BlockSpec(lmat_t.shape, lambda c, g: (0, 0)),
                  pl.BlockSpec(masks.shape, lambda c, g: (0, 0, 0))],
        out_specs=[qk_tile, qk_tile, v_tile, pl.BlockSpec((rows, HEAD), lambda c, g: (nb - 1 - c, g)), v_tile,
                   pl.BlockSpec((8, HEAD), lambda c, g: (0, 0))],
        out_shape=outs, scratch_shapes=[pltpu.VMEM((GD_HEADS, HEAD, HEAD), F32)],
        compiler_params=_params(_ARB, _ARB))(qn, kn, cv, gab, gz[0], alog, dtb, gain, oraw, ssave, dog, sel,
                                             lmat, lmat_t, masks)


def _fold_groups(wide):
    t, width = wide.shape
    tr = _pick(t, 512, 8)

    def body(w_ref, o_ref):
        acc = w_ref[:, 0:HEAD]
        for j in range(1, width // HEAD):
            acc = acc + w_ref[:, j * HEAD:(j + 1) * HEAD]
        o_ref[...] = acc.astype(BF16)

    return pl.pallas_call(
        body, name="fold_gate_grads", grid=(t // tr,), in_specs=[_row_spec(tr, width)], out_specs=_row_spec(tr, HEAD),
        out_shape=jax.ShapeDtypeStruct((t, HEAD), BF16), compiler_params=_params(_PAR))(wide)


def _adam_math(w, g, m, v):
    m2 = ADAM_B1 * m + (1.0 - ADAM_B1) * g
    v2 = ADAM_B2 * v + (1.0 - ADAM_B2) * (g * g)
    m_hat = m2 / (1.0 - ADAM_B1 ** ADAM_STEP)
    v_hat = v2 / (1.0 - ADAM_B2 ** ADAM_STEP)
    delta = -ADAM_LR * (m_hat / (jnp.sqrt(v_hat) + ADAM_EPS) + ADAM_WD * w)
    return delta, m2, v2


def _adamw(w, g, m, v, name):
    r, c = w.shape
    tr = r
    for cand in range(8, r + 1, 8):
        if r % cand == 0 and cand * c * 4 <= (1 << 20):
            tr = cand
    if r % 8 != 0:
        tr = r

    def body(w_ref, g_ref, m_ref, v_ref, d_ref, m2_ref, v2_ref):
        d, m2, v2 = _adam_math(w_ref[...], g_ref[...], m_ref[...], v_ref[...])
        d_ref[...] = d
        m2_ref[...] = m2
        v2_ref[...] = v2

    spec = pl.BlockSpec((tr, c), lambda i: (i, 0))
    return pl.pallas_call(
        body, name=name, grid=(r // tr,), in_specs=[spec] * 4, out_specs=[spec] * 3,
        out_shape=[jax.ShapeDtypeStruct((r, c), F32)] * 3, compiler_params=_params(_PAR))(w, g, m, v)


_ANY = pl.BlockSpec(memory_space=pl.ANY)


def _place():
    return lax.axis_index("x"), lax.axis_index("y"), lax.axis_index("c")


def _gather_weights(packs, nchs):
    n = len(packs)
    halves = [p.shape[0] // 2 for p in packs]
    base = [sum(nchs[:i]) for i in range(n)]
    total = sum(nchs)
    for p, h, k in zip(packs, halves, nchs):
        assert p.shape[0] == 2 * h and h % k == 0 and (h // k) % 16 == 0

    def body(*refs):
        p_refs, g_refs, (send_sems, recv_sems) = refs[:n], refs[n:2 * n], refs[2 * n:]
        x, y, c = _place()
        sibling = (x, y, 1 - c)
        chips = [(1 - x, y), (x, 1 - y), (1 - x, 1 - y)]
        chunks = [(a, q) for a in range(n) for q in range(nchs[a])]

        def rows_of(a, pc, q):
            ch = halves[a] // nchs[a]
            return pl.ds(pl.multiple_of(pc * halves[a] + q * ch, 16), ch)

        def piece(a, px, py, pc, q):
            return g_refs[a].at[2 * px + py, rows_of(a, pc, q), :]

        def copy(k, src, dst, to):
            return pltpu.make_async_remote_copy(src_ref=src, dst_ref=dst, send_sem=send_sems.at[k],
                                                recv_sem=recv_sems.at[k], device_id=to, device_id_type=MESH)

        def sem_of(j, a, q):
            return j * total + base[a] + q

        first = {(j, a, q): copy(sem_of(j, a, q), p_refs[a].at[rows_of(a, c, q), :], piece(a, x, y, c, q), (*chip, c))
                 for j, chip in enumerate(chips) for a, q in chunks}
        for a, q in chunks:
            for j in range(3):
                first[j, a, q].start()
        passed = {(j, a, q): copy(sem_of(3 + j, a, q), piece(a, *chip, c, q), piece(a, *chip, c, q), sibling)
                  for j, chip in enumerate(chips) for a, q in chunks}
        for a, q in chunks:
            for j, chip in enumerate(chips):
                copy(sem_of(j, a, q), p_refs[a].at[rows_of(a, c, q), :], piece(a, *chip, c, q), (*chip, c)).wait_recv()
                passed[j, a, q].start()
        for a, q in chunks:
            for j, chip in enumerate(chips):
                copy(sem_of(3 + j, a, q), piece(a, *chip, 1 - c, q), piece(a, *chip, 1 - c, q), sibling).wait_recv()
        for key in first:
            first[key].wait_send()
            passed[key].wait_send()

    return pl.pallas_call(
        body, name="gather_weights", out_shape=[jax.ShapeDtypeStruct((4,) + p.shape, p.dtype) for p in packs],
        in_specs=[_ANY] * n, out_specs=[_ANY] * n,
        scratch_shapes=[pltpu.SemaphoreType.DMA((6 * total,)), pltpu.SemaphoreType.DMA((6 * total,))])(*packs)


def _swap_with_sibling(arrs, nchs, lead, name, halves=False):
    n = len(arrs)
    jobs = []
    hs = [arr.shape[-2] // (2 if halves else 1) for arr in arrs]
    for a, (h, k) in enumerate(zip(hs, nchs)):
        assert h % k == 0 and (h // k) % 16 == 0
        for s in (range(lead) if lead else [None]):
            jobs += [(a, s, q * (h // k), h // k) for q in range(k)]

    def body(*refs):
        src, dst, (send_sems, recv_sems) = refs[:n], refs[n:2 * n], refs[2 * n:]
        x, y, c = _place()

        def at(ref, s, r0, rows):
            return ref.at[pl.ds(r0, rows), :] if s is None else ref.at[s, pl.ds(r0, rows), :]

        def src_rows(a, r0):
            return pl.multiple_of((1 - c) * hs[a] + r0, 16) if halves else r0

        copies = [pltpu.make_async_remote_copy(
            src_ref=at(src[a], s, src_rows(a, r0), rows), dst_ref=at(dst[a], s, r0, rows), send_sem=send_sems.at[k],
            recv_sem=recv_sems.at[k], device_id=(x, y, 1 - c), device_id_type=MESH)
            for k, (a, s, r0, rows) in enumerate(jobs)]
        for cp in copies:
            cp.start()
        for cp in copies:
            cp.wait()

    shapes = [jax.ShapeDtypeStruct(arr.shape[:-2] + (h, arr.shape[-1]), arr.dtype) for arr, h in zip(arrs, hs)]
    return pl.pallas_call(
        body, name=name, out_shape=shapes, in_specs=[_ANY] * n, out_specs=[_ANY] * n,
        scratch_shapes=[pltpu.SemaphoreType.DMA((len(jobs),)), pltpu.SemaphoreType.DMA((len(jobs),))])(*arrs)


def _add2(full, b, core, name):
    n, rows, w = b.shape
    tr = _pick(rows, 256, 16)
    nblk = rows // tr

    def body(c_ref, a_ref, b_ref, o_ref):
        o_ref[...] = (a_ref[...].astype(F32) + b_ref[...].astype(F32)).astype(BF16)

    spec = pl.BlockSpec((1, tr, w), lambda i, j, c_ref: (i, j, 0))
    grid_spec = pltpu.PrefetchScalarGridSpec(
        num_scalar_prefetch=1, grid=(n, nblk),
        in_specs=[pl.BlockSpec((1, tr, w), lambda i, j, c_ref: (i, c_ref[0] * nblk + j, 0)), spec], out_specs=spec)
    return pl.pallas_call(
        body, name=name, grid_spec=grid_spec, out_shape=jax.ShapeDtypeStruct(b.shape, BF16),
        compiler_params=_params(_PAR, _PAR))(core, full, b)


def _reduce_chips(partials, nchs):
    n = len(partials)
    jobs = []
    for a, (arr, k) in enumerate(zip(partials, nchs)):
        h = arr.shape[1]
        assert h % k == 0 and (h // k) % 16 == 0
        jobs += [(a, q * (h // k), h // k) for q in range(k)]

    def body(*refs):
        src, dst, (send_sems, recv_sems) = refs[:n], refs[n:2 * n], refs[2 * n:]
        x, y, c = _place()
        chips = [(1 - x, y), (x, 1 - y), (1 - x, 1 - y)]
        copies = [pltpu.make_async_remote_copy(
            src_ref=src[a].at[2 * px + py, pl.ds(r0, rows), :], dst_ref=dst[a].at[j, pl.ds(r0, rows), :],
            send_sem=send_sems.at[3 * k + j], recv_sem=recv_sems.at[3 * k + j],
            device_id=(px, py, c), device_id_type=MESH)
            for k, (a, r0, rows) in enumerate(jobs) for j, (px, py) in enumerate(chips)]
        for cp in copies:
            cp.start()
        for cp in copies:
            cp.wait()

    return pl.pallas_call(
        body, name="reduce_chips",
        out_shape=[jax.ShapeDtypeStruct((3,) + p.shape[1:], p.dtype) for p in partials],
        in_specs=[_ANY] * n, out_specs=[_ANY] * n,
        scratch_shapes=[pltpu.SemaphoreType.DMA((3 * len(jobs),)), pltpu.SemaphoreType.DMA((3 * len(jobs),))])(*partials)


def _add4(own, got, name):
    rows, w = own.shape
    tr = _pick(rows, 128, 16)

    def body(a_ref, b_ref, o_ref):
        o_ref[...] = ((a_ref[...].astype(F32) + b_ref[0].astype(F32)) + b_ref[1].astype(F32)) + b_ref[2].astype(F32)

    return pl.pallas_call(
        body, name=name, grid=(rows // tr,),
        in_specs=[pl.BlockSpec((tr, w), lambda i: (i, 0)), pl.BlockSpec((3, tr, w), lambda i: (0, i, 0))],
        out_specs=pl.BlockSpec((tr, w), lambda i: (i, 0)), out_shape=jax.ShapeDtypeStruct((rows, w), F32),
        compiler_params=_params(_PAR))(own, got)


def _small_sync(gs, ws, ms, vs):
    rows = gs.shape[0]
    vmem = pl.BlockSpec(memory_space=pltpu.VMEM)

    def body(g_ref, w_ref, m_ref, v_ref, sum_ref, d_ref, m2_ref, v2_ref, buf, send_sems, recv_sems):
        x, y, c = _place()
        me = 4 * x + 2 * y + c
        buf[me] = g_ref[...]
        copies = []
        for k in range(1, 8):
            peer = (x ^ (k >> 2), y ^ ((k >> 1) & 1), c ^ (k & 1))
            copies.append(pltpu.make_async_remote_copy(
                src_ref=g_ref, dst_ref=buf.at[me], send_sem=send_sems.at[k - 1], recv_sem=recv_sems.at[k - 1],
                device_id=peer, device_id_type=MESH))
        for cp in copies:
            cp.start()
        for cp in copies:
            cp.wait()
        total = buf[0]
        for i in range(1, 8):
            total = total + buf[i]
        sum_ref[...] = total
        d, m2, v2 = _adam_math(w_ref[...], total, m_ref[...], v_ref[...])
        d_ref[...] = d
        m2_ref[...] = m2
        v2_ref[...] = v2

    shape = jax.ShapeDtypeStruct((rows, 128), F32)
    return pl.pallas_call(
        body, name="small_sync", out_shape=[shape] * 4, in_specs=[vmem] * 4, out_specs=[vmem] * 4,
        scratch_shapes=[pltpu.VMEM((8, rows, 128), F32), pltpu.SemaphoreType.DMA((7,)),
                        pltpu.SemaphoreType.DMA((7,))])(gs, ws, ms, vs)


_PACK_COLS = (("ffn1_w_in", 1408), ("ffn2_w_in", 1408), ("w_in", 3080))
_PACK_ROWS = (("ffn1_w_out", 704, 704), ("ffn2_w_out", 704, 704), ("w_branch_hgrn", 256, 256),
              ("w_branch_gdn", 512, 512), ("w_out", 256, 256), ("gdn_conv_w", CONV_K, 128))
_PACK_CHUNKS = (8, 5)
_BIG_NAMES = tuple(n for n, _ in _PACK_COLS) + tuple(n for n, _, _ in _PACK_ROWS)


def _pack(parts, lead):
    ax = len(lead)
    cols = jnp.concatenate([parts[n] for n, _ in _PACK_COLS], axis=ax + 1)
    rows = []
    for n, r, padded in _PACK_ROWS:
        p = parts[n]
        if padded != r:
            p = jnp.concatenate([p, jnp.zeros(lead + (padded - r, p.shape[-1]), p.dtype)], axis=ax)
        rows.append(p)
    return cols, jnp.concatenate(rows, axis=ax)


def _unpack(cols, rows):
    out, off = {}, 0
    for n, w in _PACK_COLS:
        out[n] = cols[..., off:off + w]
        off += w
    off = 0
    for n, r, padded in _PACK_ROWS:
        out[n] = rows[..., off:off + r, :]
        off += padded
    return out


def _is_col_sharded(name):
    return name in ("ffn1_w_in", "ffn2_w_in", "w_in", "gdn_conv_w")


def _full_from_shards(name, g):
    if _is_col_sharded(name):
        return jnp.transpose(g, (1, 0, 2)).reshape(g.shape[1], -1)
    return g.reshape(-1, g.shape[2])


def _shards_from_full(name, full):
    if _is_col_sharded(name):
        return jnp.transpose(full.reshape(full.shape[0], 4, -1), (1, 0, 2))
    return full.reshape(4, -1, full.shape[1])


_SMALL = (("ffn1_norm", 8), ("mix_norm", 8), ("hgrn_lb_logits", 16), ("hgrn_out_norm", 1), ("gdn_a_log", 1),
          ("gdn_dt_bias", 1), ("gdn_out_norm", 1), ("ffn2_norm", 8), ("final_norm", 8), ("loss", 1))
_SMALL_ROWS = 56


def _pack_small(parts):
    out = []
    for name, rows in _SMALL:
        p = parts[name].reshape(-1).astype(F32)
        p = jnp.concatenate([p, jnp.zeros((rows * 128 - p.shape[0],), F32)]) if p.shape[0] != rows * 128 else p
        out.append(p.reshape(rows, 128))
    used = sum(r for _, r in _SMALL)
    out.append(jnp.zeros((_SMALL_ROWS - used, 128), F32))
    return jnp.concatenate(out, axis=0)


def _unpack_small(packed, shapes):
    out, off = {}, 0
    for name, rows in _SMALL:
        n = int(np.prod(shapes[name]))
        out[name] = packed[off:off + rows].reshape(-1)[:n].reshape(shapes[name])
        off += rows
    return out


def _ffn_fwd(x, gain, w_in, w_out, tag):
    n = _rmsnorm_fwd(x, gain, tag + "_norm")
    a, b, hm = _ffn_in_act(n, w_in, tag + "_in")
    out = _mm(hm, w_out, alpha=0.5, res=x, name=tag + "_out")
    return out, (n, a, b)


def _ffn_bwd(x, gain, w_in, w_out, saved, dout, dout_bf, tag):
    n, a, b = saved
    da, db, hm = _ffn_dact(dout_bf, w_out, a, b, tag + "_dact")
    dw_out = _mm(hm, dout_bf, ta=True, alpha=0.5, out_dtype=BF16, name=tag + "_dwout")
    dw_in = jnp.concatenate([_mm(n, da, ta=True, out_dtype=BF16, name=tag + "_dwin_a"),
                             _mm(n, db, ta=True, out_dtype=BF16, name=tag + "_dwin_b")], axis=1)
    dn = _mm(da, w_in[:, :D_FF], tb=True, name=tag + "_dnorm_a")
    dn = _mm(db, w_in[:, D_FF:], tb=True, res=dn, name=tag + "_dnorm_b")
    dx, dx_bf, dgain = _rmsnorm_bwd(x, gain, dn, dout, tag + "_dx")
    return dx, dx_bf, dgain, dw_in, dw_out


def _pad_lanes(v):
    return jnp.concatenate([v.reshape(1, -1), jnp.zeros((1, HEAD - v.size), F32)], axis=1)


def _local_step(x, tgt, w, small):
    hg_c = _hg_consts()
    gd_c = _gd_consts()
    seg, off = {}, 0
    for name, size in zip(IN_NAMES, IN_SIZES):
        seg[name] = w["w_in"][:, off:off + size]
        off += size
    w_gab = jnp.concatenate([seg["ga"], seg["gb"], jnp.zeros((D_MODEL, HEAD - 32), BF16)], axis=1)
    big_segs = [n for n in IN_NAMES if n not in ("ga", "gb")]
    conv8 = jnp.concatenate([w["gdn_conv_w"].astype(F32), jnp.zeros((8 - CONV_K, 4096), F32)], axis=0)
    conv_q, conv_k, conv_v = conv8[:, :1024], conv8[:, 1024:2048], conv8[:, 2048:]
    alog = _pad_lanes(small["gdn_a_log"])
    dtb = _pad_lanes(small["gdn_dt_bias"])
    logits = small["hgrn_lb_logits"]
    hg_gain = small["hgrn_out_norm"].reshape(1, HEAD)
    gd_gain = small["gdn_out_norm"].reshape(1, HEAD)
    g1, gm, g2 = small["ffn1_norm"].reshape(1, -1), small["mix_norm"].reshape(1, -1), small["ffn2_norm"].reshape(1, -1)
    gf = small["final_norm"].reshape(1, -1)
    qscale = HEAD ** -0.5

    h1, ffn1_saved = _ffn_fwd(x, g1, w["ffn1_w_in"], w["ffn1_w_out"], "ffn1")
    u = _rmsnorm_fwd(h1, gm, "mix_norm")
    w_main = jnp.concatenate([seg[n] for n in big_segs], axis=1)
    proj = _mm(u, w_main, name="proj")
    pr, off = {}, 0
    for n in big_segs:
        pr[n] = _view(proj, off, seg[n].shape[1])
        off += seg[n].shape[1]
    gab = _mm(u, w_gab, name="proj_gab")
    oh_raw, oh, s_h = _hgrn_fwd(pr["hq"], pr["hf"], pr["hi"], pr["hg"], logits, hg_gain, hg_c)
    qn = _conv_fwd(pr["gq"], conv_q, qscale, "conv_q")
    kn = _conv_fwd(pr["gk"], conv_k, 1.0, "conv_k")
    cv = _conv_fwd(pr["gv"], conv_v, None, "conv_v")
    og_raw, og, s_g = _gdn_fwd(qn, kn, cv, gab, pr["gz"], alog, dtb, gd_gain, gd_c)
    yh = _mm(oh, w["w_branch_hgrn"], name="branch_h")
    yg = _mm(og, w["w_branch_gdn"], name="branch_g")
    ym = _merge_fwd(yh, yg, pr["gate_h"], pr["gate_g"])
    h2 = _mm(ym, w["w_out"], res=h1, name="mix_out")
    h3, ffn2_saved = _ffn_fwd(h2, g2, w["ffn2_w_in"], w["ffn2_w_out"], "ffn2")
    loss, dh3, dh3_bf, d_gf = _final_loss(h3, gf, tgt)

    dh2, dh2_bf, d_g2, d_f2in, d_f2out = _ffn_bwd(h2, g2, w["ffn2_w_in"], w["ffn2_w_out"], ffn2_saved, dh3, dh3_bf,
                                                  "ffn2")
    dym = _mm(dh2_bf, w["w_out"], tb=True, name="d_merge")
    d_wout = _mm(ym, dh2_bf, ta=True, out_dtype=BF16, name="d_w_out")
    dyh, dyg, d_gate_h, d_gate_g = _merge_bwd(dym, yh, yg, pr["gate_h"], pr["gate_g"])
    d_wbh = _mm(oh, dyh, ta=True, out_dtype=BF16, name="d_w_branch_h")
    d_wbg = _mm(og, dyg, ta=True, out_dtype=BF16, name="d_w_branch_g")
    doh = _mm(dyh, w["w_branch_hgrn"], tb=True, name="d_oh")
    dog = _mm(dyg, w["w_branch_gdn"], tb=True, name="d_og")
    d_hq, d_hf, d_hi, d_hg, d_hg_gain, d_lb0 = _hgrn_bwd(pr["hq"], pr["hf"], pr["hi"], pr["hg"], logits, hg_gain,
                                                        oh_raw, s_h, doh, hg_c)
    d_qn, d_kn, d_cv, d_gab_wide, d_gz, gd_small = _gdn_bwd(qn, kn, cv, gab, pr["gz"], alog, dtb, gd_gain, og_raw,
                                                            s_g, dog, gd_c)
    d_gab = _fold_groups(d_gab_wide)
    dc_q, dwc_q = _conv_bwd_a(pr["gq"], conv_q, d_qn, qscale, "dconv_q")
    dc_k, dwc_k = _conv_bwd_a(pr["gk"], conv_k, d_kn, 1.0, "dconv_k")
    dc_v, dwc_v = _conv_bwd_a(pr["gv"], conv_v, d_cv, None, "dconv_v")
    d_gq = _conv_bwd_b(dc_q, conv_q, "dconvx_q")
    d_gk = _conv_bwd_b(dc_k, conv_k, "dconvx_k")
    d_gv = _conv_bwd_b(dc_v, conv_v, "dconvx_v")
    dpr = {"hq": d_hq, "hf": d_hf, "hi": d_hi, "hg": d_hg, "gq": d_gq, "gk": d_gk, "gv": d_gv, "gz": d_gz,
           "gate_h": d_gate_h, "gate_g": d_gate_g}
    dproj = jnp.concatenate([dpr[n] for n in big_segs], axis=1)
    du = _mm(d_gab, w_gab, tb=True, name="du_gab")
    du = _mm(dproj, w_main, tb=True, res=du, name="du")
    d_wmain = _mm(u, dproj, ta=True, out_dtype=BF16, name="dw_main")
    d_wgab = _mm(u, d_gab, ta=True, out_dtype=BF16, name="dw_gab")
    d_win = jnp.concatenate([d_wmain[:, :8192], d_wgab[:, :32], d_wmain[:, 8192:]], axis=1)
    dh1, dh1_bf, d_gm = _rmsnorm_bwd(h1, gm, du, dh2, "mix_dnorm")
    dx, _, d_g1, d_f1in, d_f1out = _ffn_bwd(x, g1, w["ffn1_w_in"], w["ffn1_w_out"], ffn1_saved, dh1, dh1_bf, "ffn1")

    d_conv = jnp.concatenate([dwc_q[:CONV_K], dwc_k[:CONV_K], dwc_v[:CONV_K]], axis=1).astype(BF16)
    big = {"ffn1_w_in": d_f1in, "ffn1_w_out": d_f1out, "w_in": d_win, "gdn_conv_w": d_conv,
           "w_branch_hgrn": d_wbh, "w_branch_gdn": d_wbg, "w_out": d_wout, "ffn2_w_in": d_f2in, "ffn2_w_out": d_f2out}
    d_lb0 = d_lb0.reshape(1, -1)
    sm = {"ffn1_norm": d_g1, "mix_norm": d_gm, "hgrn_lb_logits": jnp.concatenate([d_lb0, -d_lb0], axis=0),
          "hgrn_out_norm": d_hg_gain, "gdn_a_log": gd_small[2, :16], "gdn_dt_bias": gd_small[1, :16],
          "gdn_out_norm": gd_small[0], "ffn2_norm": d_g2, "final_norm": d_gf, "loss": loss[0, :1]}
    return dx, big, sm


_WEIGHTS = ("ffn1_norm", "ffn1_w_in", "ffn1_w_out", "mix_norm", "w_in", "hgrn_lb_logits", "hgrn_out_norm",
            "gdn_conv_w", "gdn_a_log", "gdn_dt_bias", "gdn_out_norm", "w_branch_hgrn", "w_branch_gdn", "w_out",
            "ffn2_norm", "ffn2_w_in", "ffn2_w_out", "final_norm")


def kernel(x, ffn1_norm, ffn1_w_in, ffn1_w_out, mix_norm, w_in, hgrn_lb_logits, hgrn_out_norm, gdn_conv_w, gdn_a_log, gdn_dt_bias, gdn_out_norm, w_branch_hgrn, w_branch_gdn, w_out, ffn2_norm, ffn2_w_in, ffn2_w_out, final_norm, loss_target, m_ffn1_norm, m_ffn1_w_in, m_ffn1_w_out, m_mix_norm, m_w_in, m_hgrn_lb_logits, m_hgrn_out_norm, m_gdn_conv_w, m_gdn_a_log, m_gdn_dt_bias, m_gdn_out_norm, m_w_branch_hgrn, m_w_branch_gdn, m_w_out, m_ffn2_norm, m_ffn2_w_in, m_ffn2_w_out, m_final_norm, v_ffn1_norm, v_ffn1_w_in, v_ffn1_w_out, v_mix_norm, v_w_in, v_hgrn_lb_logits, v_hgrn_out_norm, v_gdn_conv_w, v_gdn_a_log, v_gdn_dt_bias, v_gdn_out_norm, v_w_branch_hgrn, v_w_branch_gdn, v_w_out, v_ffn2_norm, v_ffn2_w_in, v_ffn2_w_out, v_final_norm):
    args = dict(locals())
    wts = {n: args[n] for n in _WEIGHTS}
    moms = {n: args["m_" + n] for n in _WEIGHTS}
    vars_ = {n: args["v_" + n] for n in _WEIGHTS}

    packs = _pack({n: wts[n][0].astype(BF16) for n in _BIG_NAMES}, ())
    xi, yi, ci = lax.axis_index("x"), lax.axis_index("y"), lax.axis_index("c")
    chip = 2 * xi + yi
    others = _gather_weights(packs, _PACK_CHUNKS)
    gathered = _unpack(*[lax.dynamic_update_index_in_dim(g, p, chip, 0) for g, p in zip(others, packs)])
    full = {n: _full_from_shards(n, gathered[n]) for n in _BIG_NAMES}
    small = {n: wts[n].astype(F32) for n in _WEIGHTS if n not in _BIG_NAMES}

    dx, big_grads, small_grads = _local_step(x[0], loss_target[0], full, small)

    gpacks = _pack({n: _shards_from_full(n, big_grads[n]) for n in _BIG_NAMES}, (4,))
    got = _swap_with_sibling(gpacks, _PACK_CHUNKS, 4, "reduce_pair", halves=True)
    core = ci.reshape(1).astype(jnp.int32)
    chip_sums = [_add2(a, b, core, "add_pair_%d" % i) for i, (a, b) in enumerate(zip(gpacks, got))]
    from_chips = _reduce_chips(chip_sums, _PACK_CHUNKS)
    mine = [_add4(lax.dynamic_index_in_dim(s, chip, axis=0, keepdims=False), f, "add_chips_%d" % i)
            for i, (s, f) in enumerate(zip(chip_sums, from_chips))]
    theirs = _swap_with_sibling(mine, _PACK_CHUNKS, 0, "share_pair")
    south = ci == 0
    reduced = _unpack(*[jnp.concatenate([jnp.where(south, a, b), jnp.where(south, b, a)], axis=0)
                        for a, b in zip(mine, theirs)])

    out_g, out_d, out_m, out_v = {}, {}, {}, {}
    for n in _BIG_NAMES:
        shape = wts[n].shape
        w2 = wts[n].reshape(shape[-2], shape[-1])
        g2 = reduced[n]
        d, m2, v2 = _adamw(w2, g2, moms[n].reshape(w2.shape), vars_[n].reshape(w2.shape), "adamw_" + n)
        out_g[n], out_d[n], out_m[n], out_v[n] = g2.reshape(shape), d.reshape(shape), m2.reshape(shape), v2.reshape(shape)

    small_names = [n for n, _ in _SMALL]
    zero = jnp.zeros((1,), F32)
    shapes = {n: (wts[n].shape if n != "loss" else (1,)) for n in small_names}
    sums, sd, sm_, sv = _small_sync(
        _pack_small(small_grads),
        _pack_small({n: (wts[n] if n != "loss" else zero) for n in small_names}),
        _pack_small({n: (moms[n] if n != "loss" else zero) for n in small_names}),
        _pack_small({n: (vars_[n] if n != "loss" else zero) for n in small_names}))
    sg_u, sd_u, sm_u, sv_u = (_unpack_small(p, shapes) for p in (sums, sd, sm_, sv))
    for n in small_names:
        if n != "loss":
            out_g[n], out_d[n], out_m[n], out_v[n] = sg_u[n], sd_u[n], sm_u[n], sv_u[n]
    loss = sg_u["loss"].reshape(())

    return (loss, dx[None], *[out_g[n] for n in _WEIGHTS], *[out_d[n] for n in _WEIGHTS],
            *[out_m[n] for n in _WEIGHTS], *[out_v[n] for n in _WEIGHTS])
```

```python
import numpy as np

import jax
import jax.numpy as jnp
from jax import lax
from jax.experimental import pallas as pl
from jax.experimental.pallas import tpu as pltpu

F32 = jnp.float32
BF16 = jnp.bfloat16

D_MODEL = 1024
D_FF = 2816
CHUNK = 64
HEAD = 128
HG_HEADS = 8
GD_HEADS = 16
HPS = 8
COMM_CHUNKS = 9
MM_TM = 1408
MM_TN = 512
MM_TK = 1536
VMEM_LIMIT = 48 * 1024 * 1024
EPS = 1e-6
CONV_K = 4
IN_NAMES = ("hq", "hf", "hi", "hg", "gq", "gk", "gv", "ga", "gb", "gz", "gate_h", "gate_g")
IN_SIZES = (1024, 1024, 1024, 1024, 1024, 1024, 2048, 16, 16, 2048, 1024, 1024)
IN_WIDTH = sum(IN_SIZES)

ADAM_LR = 0.001
ADAM_B1 = 0.9
ADAM_B2 = 0.999
ADAM_EPS = 1e-08
ADAM_WD = 0.01
ADAM_STEP = 10

MESH = pl.DeviceIdType.MESH
_ARB = "arbitrary"
_PAR = "parallel"


def _bf(x):
    return x.astype(BF16)


def _dot(a, b):
    return jnp.dot(_bf(a), _bf(b), preferred_element_type=F32)


def _dot_nt(a, b):
    return lax.dot_general(_bf(a), _bf(b), (((1,), (1,)), ((), ())), preferred_element_type=F32)


def _dot_tn(a, b):
    return lax.dot_general(_bf(a), _bf(b), (((0,), (0,)), ((), ())), preferred_element_type=F32)


def _split3(x):
    hi = _bf(x)
    r = x - hi.astype(F32)
    mid = _bf(r)
    lo = _bf(r - mid.astype(F32))
    return hi, mid, lo


def _dot_mx(m, x):
    hi, mid, lo = _split3(x)
    return (jnp.dot(m, hi, preferred_element_type=F32) + jnp.dot(m, mid, preferred_element_type=F32)
            + jnp.dot(m, lo, preferred_element_type=F32))


def _dot_xm(x, m):
    hi, mid, lo = _split3(x)
    return (jnp.dot(hi, m, preferred_element_type=F32) + jnp.dot(mid, m, preferred_element_type=F32)
            + jnp.dot(lo, m, preferred_element_type=F32))


def _dot_hp(a, b):
    ah = _bf(a)
    al = _bf(a - ah.astype(F32))
    bh = _bf(b)
    bl = _bf(b - bh.astype(F32))
    return (jnp.dot(ah, bh, preferred_element_type=F32) + jnp.dot(ah, bl, preferred_element_type=F32)
            + jnp.dot(al, bh, preferred_element_type=F32))


def _sigmoid(x):
    return jax.nn.sigmoid(x)


def _silu(x):
    return x * _sigmoid(x)


def _dsilu(x):
    s = _sigmoid(x)
    return s * (1.0 + x * (1.0 - s))


def _softplus(x):
    return jnp.maximum(x, 0.0) + jnp.log(1.0 + jnp.exp(-jnp.abs(x)))


def _rowsum(x):
    return jnp.sum(x, axis=1, keepdims=True)


def _col_to_row(col, eye):
    return jnp.sum(eye * col, axis=0, keepdims=True)


def _row_to_col(row, eye):
    return jnp.sum(eye * row, axis=1, keepdims=True)


def _pick(dim, pref, unit=128):
    if dim <= pref:
        return dim
    t = pref
    while t >= unit:
        if dim % t == 0:
            return t
        t -= unit
    return dim


def _params(*sem):
    return pltpu.CompilerParams(dimension_semantics=tuple(sem), vmem_limit_bytes=VMEM_LIMIT)


def _mm(a, b, *, ta=False, tb=False, alpha=1.0, res=None, out_dtype=F32, name="mm"):
    m = a.shape[1] if ta else a.shape[0]
    k = a.shape[0] if ta else a.shape[1]
    n = b.shape[0] if tb else b.shape[1]
    assert k == (b.shape[1] if tb else b.shape[0])
    tm, tn, tk = _pick(m, MM_TM), _pick(n, MM_TN), _pick(k, MM_TK)
    if tn < MM_TN < n and n % MM_TM == 0:
        tn = MM_TM
    nk = k // tk
    a_spec = pl.BlockSpec((tk, tm), lambda i, j, l: (l, i)) if ta else pl.BlockSpec((tm, tk), lambda i, j, l: (i, l))
    b_spec = pl.BlockSpec((tn, tk), lambda i, j, l: (j, l)) if tb else pl.BlockSpec((tk, tn), lambda i, j, l: (l, j))
    o_spec = pl.BlockSpec((tm, tn), lambda i, j, l: (i, j))
    dims = (((0 if ta else 1,), (1 if tb else 0,)), ((), ()))
    has_res = res is not None

    def finish(r, r_ref, o_ref):
        if alpha != 1.0:
            r = r * alpha
        if has_res:
            r = r + r_ref[...]
        o_ref[...] = r.astype(out_dtype)

    def body(*refs):
        a_ref, b_ref = refs[0], refs[1]
        r_ref = refs[2] if has_res else None
        o_ref = refs[3] if has_res else refs[2]
        part = lax.dot_general(_bf(a_ref[...]), _bf(b_ref[...]), dims, preferred_element_type=F32)
        if nk == 1:
            finish(part, r_ref, o_ref)
            return
        acc = refs[-1]
        step = pl.program_id(2)

        @pl.when(step == 0)
        def _():
            acc[...] = part

        @pl.when(step != 0)
        def _():
            acc[...] += part

        @pl.when(step == nk - 1)
        def _():
            finish(acc[...], r_ref, o_ref)

    ins = [a, b] + ([res] if has_res else [])
    in_specs = [a_spec, b_spec] + ([o_spec] if has_res else [])
    return pl.pallas_call(
        body, name=name, grid=(m // tm, n // tn, nk), in_specs=in_specs, out_specs=o_spec,
        out_shape=jax.ShapeDtypeStruct((m, n), out_dtype),
        scratch_shapes=[pltpu.VMEM((tm, tn), F32)] if nk > 1 else [],
        compiler_params=_params(_PAR, _PAR, _ARB))(*ins)


def _row_spec(tr, w):
    return pl.BlockSpec((tr, w), lambda i: (i, 0))


def _full_spec(shape):
    return pl.BlockSpec(shape, lambda i: tuple(0 for _ in shape))


def _view(arr, off, width):
    return arr, off, width


def _view_rows(view, tr):
    _, off, width = view
    assert off % width == 0
    return pl.BlockSpec((tr, width), lambda i: (i, off // width))


def _view_tile(view, rows, bw, cidx=lambda c: c):
    _, off, width = view
    assert off % bw == 0 and width % bw == 0
    return pl.BlockSpec((rows, bw), lambda c, g: (cidx(c), off // bw + g))


def _rmsnorm_fwd(x, g, name):
    t, d = x.shape
    tr = _pick(t, 256, 8)

    def body(x_ref, g_ref, o_ref):
        xv = x_ref[...]
        r = lax.rsqrt(jnp.mean(xv * xv, axis=1, keepdims=True) + EPS)
        o_ref[...] = (xv * r * g_ref[...]).astype(BF16)

    return pl.pallas_call(
        body, name=name, grid=(t // tr,), in_specs=[_row_spec(tr, d), _full_spec((1, d))],
        out_specs=_row_spec(tr, d), out_shape=jax.ShapeDtypeStruct((t, d), BF16),
        compiler_params=_params(_PAR))(x, g)


def _rmsnorm_bwd(x, g, dn, res, name):
    t, d = x.shape
    tr = _pick(t, 256, 8)

    def body(x_ref, g_ref, dn_ref, r_ref, dx_ref, dxb_ref, dg_ref):
        @pl.when(pl.program_id(0) == 0)
        def _():
            dg_ref[...] = jnp.zeros_like(dg_ref)

        xv = x_ref[...]
        r = lax.rsqrt(jnp.mean(xv * xv, axis=1, keepdims=True) + EPS)
        xh = xv * r
        dy = dn_ref[...]
        dg_ref[...] += jnp.sum(dy * xh, axis=0, keepdims=True)
        dxh = dy * g_ref[...]
        dx = r_ref[...] + r * (dxh - xh * jnp.mean(dxh * xh, axis=1, keepdims=True))
        dx_ref[...] = dx
        dxb_ref[...] = dx.astype(BF16)

    return pl.pallas_call(
        body, name=name, grid=(t // tr,),
        in_specs=[_row_spec(tr, d), _full_spec((1, d)), _row_spec(tr, d), _row_spec(tr, d)],
        out_specs=[_row_spec(tr, d), _row_spec(tr, d), _full_spec((1, d))],
        out_shape=[jax.ShapeDtypeStruct((t, d), F32), jax.ShapeDtypeStruct((t, d), BF16),
                   jax.ShapeDtypeStruct((1, d), F32)],
        compiler_params=_params(_ARB))(x, g, dn, res)


FFN_TN = 256


def _ffn_in_act(n, w_in, name):
    t, d = n.shape
    tm = _pick(t, MM_TM)
    nf = D_FF // FFN_TN

    def body(n_ref, wa_ref, wb_ref, a_ref, b_ref, hm_ref):
        nv = n_ref[...]
        a = jnp.dot(nv, wa_ref[...], preferred_element_type=F32)
        b = jnp.dot(nv, wb_ref[...], preferred_element_type=F32)
        a_ref[...] = a.astype(BF16)
        b_ref[...] = b.astype(BF16)
        hm_ref[...] = (_silu(a) * b).astype(BF16)

    tile = pl.BlockSpec((tm, FFN_TN), lambda i, j: (i, j))
    return pl.pallas_call(
        body, name=name, grid=(t // tm, nf),
        in_specs=[pl.BlockSpec((tm, d), lambda i, j: (i, 0)), pl.BlockSpec((d, FFN_TN), lambda i, j: (0, j)),
                  pl.BlockSpec((d, FFN_TN), lambda i, j: (0, nf + j))],
        out_specs=[tile, tile, tile], out_shape=[jax.ShapeDtypeStruct((t, D_FF), BF16)] * 3,
        compiler_params=_params(_PAR, _PAR))(n, w_in, w_in)


def _ffn_dact(dout, w_out, a, b, name):
    t, d = dout.shape
    tm = _pick(t, MM_TM)

    def body(do_ref, w_ref, a_ref, b_ref, da_ref, db_ref, hm_ref):
        dh = 0.5 * _dot_nt(do_ref[...], w_ref[...])
        av = a_ref[...].astype(F32)
        bv = b_ref[...].astype(F32)
        sa = _silu(av)
        da_ref[...] = (dh * bv * _dsilu(av)).astype(BF16)
        db_ref[...] = (dh * sa).astype(BF16)
        hm_ref[...] = (sa * bv).astype(BF16)

    tile = pl.BlockSpec((tm, FFN_TN), lambda i, j: (i, j))
    return pl.pallas_call(
        body, name=name, grid=(t // tm, D_FF // FFN_TN),
        in_specs=[pl.BlockSpec((tm, d), lambda i, j: (i, 0)), pl.BlockSpec((FFN_TN, d), lambda i, j: (j, 0)), tile, tile],
        out_specs=[tile, tile, tile], out_shape=[jax.ShapeDtypeStruct((t, D_FF), BF16)] * 3,
        compiler_params=_params(_PAR, _PAR))(dout, w_out, a, b)


def _merge_fwd(yh, yg, gh, gg):
    t, d = yh.shape
    tr = _pick(t, 256, 8)

    def body(yh_ref, yg_ref, gh_ref, gg_ref, o_ref):
        o_ref[...] = (_sigmoid(gh_ref[...]) * yh_ref[...] + _sigmoid(gg_ref[...]) * yg_ref[...]).astype(BF16)

    return pl.pallas_call(
        body, name="merge_fwd", grid=(t // tr,),
        in_specs=[_row_spec(tr, d), _row_spec(tr, d), _view_rows(gh, tr), _view_rows(gg, tr)],
        out_specs=_row_spec(tr, d),
        out_shape=jax.ShapeDtypeStruct((t, d), BF16), compiler_params=_params(_PAR))(yh, yg, gh[0], gg[0])


def _merge_bwd(dy, yh, yg, gh, gg):
    t, d = yh.shape
    tr = _pick(t, 256, 8)

    def body(dy_ref, yh_ref, yg_ref, gh_ref, gg_ref, dyh_ref, dyg_ref, dgh_ref, dgg_ref):
        dyv = dy_ref[...]
        sh = _sigmoid(gh_ref[...])
        sg = _sigmoid(gg_ref[...])
        dyh_ref[...] = (dyv * sh).astype(BF16)
        dyg_ref[...] = (dyv * sg).astype(BF16)
        dgh_ref[...] = (dyv * yh_ref[...] * sh * (1.0 - sh)).astype(BF16)
        dgg_ref[...] = (dyv * yg_ref[...] * sg * (1.0 - sg)).astype(BF16)

    return pl.pallas_call(
        body, name="merge_bwd", grid=(t // tr,),
        in_specs=[_row_spec(tr, d)] * 3 + [_view_rows(gh, tr), _view_rows(gg, tr)], out_specs=[_row_spec(tr, d)] * 4,
        out_shape=[jax.ShapeDtypeStruct((t, d), BF16)] * 4,
        compiler_params=_params(_PAR))(dy, yh, yg, gh[0], gg[0])


def _final_loss(h, g, tgt):
    t, d = h.shape
    tr = _pick(t, 256, 8)

    def body(h_ref, g_ref, t_ref, loss_ref, dh_ref, dhb_ref, dg_ref):
        @pl.when(pl.program_id(0) == 0)
        def _():
            dg_ref[...] = jnp.zeros_like(dg_ref)
            loss_ref[...] = jnp.zeros_like(loss_ref)

        xv = h_ref[...]
        gv = g_ref[...]
        r = lax.rsqrt(jnp.mean(xv * xv, axis=1, keepdims=True) + EPS)
        xh = xv * r
        err = xh * gv - t_ref[...]
        loss_ref[...] += 0.5 * jnp.sum(jnp.mean(err * err, axis=1, keepdims=True), axis=0, keepdims=True)
        dy = err * (1.0 / d)
        dg_ref[...] += jnp.sum(dy * xh, axis=0, keepdims=True)
        dxh = dy * gv
        dh = r * (dxh - xh * jnp.mean(dxh * xh, axis=1, keepdims=True))
        dh_ref[...] = dh
        dhb_ref[...] = dh.astype(BF16)

    return pl.pallas_call(
        body, name="final_loss", grid=(t // tr,),
        in_specs=[_row_spec(tr, d), _full_spec((1, d)), _row_spec(tr, d)],
        out_specs=[_full_spec((1, 128)), _row_spec(tr, d), _row_spec(tr, d), _full_spec((1, d))],
        out_shape=[jax.ShapeDtypeStruct((1, 128), F32), jax.ShapeDtypeStruct((t, d), F32),
                   jax.ShapeDtypeStruct((t, d), BF16), jax.ShapeDtypeStruct((1, d), F32)],
        compiler_params=_params(_ARB))(h, g, tgt)


def _hg_consts():
    c = CHUNK
    t = np.arange(c)
    mats, masks = [], []
    for lvl in range(6):
        m = 1 << lvl
        blk = t // m
        mat = np.zeros((c, c), np.float32)
        for tt in range(c):
            b = blk[tt]
            if b % 2 == 1:
                mat[tt, b * m:tt + 1] = 1.0
            else:
                mat[tt, tt + 1:(b + 1) * m] = 1.0
        mats.append(mat)
        same = (t[:, None] // (2 * m)) == (t[None, :] // (2 * m))
        masks.append((same & (blk[:, None] % 2 == 1) & (blk[None, :] % 2 == 0)).astype(np.float32))
    pre = np.tril(np.ones((c, c), np.float32))
    suf = np.triu(np.ones((c, c), np.float32), 1)
    mstack = np.concatenate(mats + [pre, suf], 0)
    masks.append(np.eye(c, dtype=np.float32))
    return (jnp.asarray(mstack, BF16), jnp.asarray(mstack.T.copy(), BF16), jnp.asarray(np.stack(masks), F32),
            jnp.asarray(np.eye(HEAD, dtype=np.float32)))


def _gd_consts():
    c = CHUNK
    incl = np.tril(np.ones((c, c), np.float32))
    strict = np.tril(np.ones((c, c), np.float32), -1)
    eye = np.eye(c, dtype=np.float32)
    masks = np.stack([incl, strict, eye, incl.T.copy()])
    sel = np.zeros((GD_HEADS, HEAD, 2 * HEAD), np.float32)
    for j in range(GD_HEADS):
        sel[j, j, :HEAD] = 1.0
        sel[j, GD_HEADS + j, HEAD:] = 1.0
    return (jnp.asarray(incl, BF16), jnp.asarray(incl.T.copy(), BF16), jnp.asarray(masks, F32), jnp.asarray(sel, BF16))


def _chunks_per_step(nc):
    for cb in (32 // HPS, 2, 1):
        if nc % cb == 0:
            return cb
    return 1


def _hg_prep(hq, hf, lg):
    lb = _sigmoid(lg[0:1, :] - lg[1:2, :])
    sg = _sigmoid(hf)
    sgn = _sigmoid(-hf)
    f = lb + (1.0 - lb) * sg
    lf = jnp.log(f)
    kk = (1.0 - lb) * sgn
    q = _silu(hq) * (HEAD ** -0.5)
    return lb, sg, sgn, f, lf, kk, q


def _mx_each(m, xs):
    wide = [jnp.concatenate(_split3(x), axis=1) for x in xs]
    prods = [jnp.dot(m, w, preferred_element_type=F32) for w in wide]
    return [p[:, :HEAD] + p[:, HEAD:2 * HEAD] + p[:, 2 * HEAD:] for p in prods]


def _hg_scores(q, kk, ex, mask_ref):
    p = [mask_ref[6] * _rowsum(a * b) for a, b in zip(q, kk)]
    for lvl in range(6):
        el = [e[lvl * CHUNK:(lvl + 1) * CHUNK] for e in ex]
        d = [_dot_nt(a * e, b * e) for a, b, e in zip(q, kk, el)]
        p = [x + mask_ref[lvl] * y for x, y in zip(p, d)]
    return p


def _hgrn_fwd(hq, hf, hi, hg, logits, gain, consts):
    t = hq[0].shape[0]
    nc = t // CHUNK
    cb = _chunks_per_step(nc)
    rows = cb * CHUNK
    mstack, _, masks, eye = consts
    tile = pl.BlockSpec((rows, HPS * HEAD), lambda c, g: (c, g))

    def body(hq_ref, hf_ref, hi_ref, hg_ref, lg_ref, gain_ref, m_ref, mask_ref, eye_ref,
             oraw_ref, og_ref, ssave_ref, state):
        c = pl.program_id(0)
        g = pl.program_id(1)

        @pl.when(c == 0)
        def _():
            for hh in range(HPS):
                state[g * HPS + hh] = jnp.zeros((HEAD, HEAD), F32)

        lg_all = lg_ref[...]
        gain_v = gain_ref[...]

        def one(i, carry):
            sl = pl.ds(pl.multiple_of(i * CHUNK, CHUNK), CHUNK)
            hs = range(HPS)
            heads = [g * HPS + hh for hh in hs]
            ln = [slice(hh * HEAD, (hh + 1) * HEAD) for hh in hs]
            preps = [_hg_prep(hq_ref[sl, s], hf_ref[sl, s], lg_all[:, s]) for s in ln]
            lf, kk, q = [p[4] for p in preps], [p[5] for p in preps], [p[6] for p in preps]
            v = [hi_ref[sl, s] for s in ln]
            ex = [jnp.exp(x) for x in _mx_each(m_ref[...], lf)]
            eb = [e[6 * CHUNK:7 * CHUNK] for e in ex]
            esfx = [e[7 * CHUNK:8 * CHUNK] for e in ex]
            p = _hg_scores(q, kk, ex, mask_ref)
            s0 = [state[h] for h in heads]
            o = _each(lambda a, e, s, pp, vv: _dot(a * e, s) + _dot(pp, vv), q, eb, s0, p, v)
            eye_v = eye_ref[...]
            s1 = _each(lambda s, e, kx, ef, vv: s * _row_to_col(e[CHUNK - 1:CHUNK, :], eye_v) + _dot_tn(kx * ef, vv),
                       s0, eb, kk, esfx, v)
            for hh in hs:
                ssave_ref[i, hh] = s0[hh]
                state[heads[hh]] = s1[hh]
                oraw_ref[sl, ln[hh]] = o[hh]
                r = lax.rsqrt(jnp.mean(o[hh] * o[hh], axis=1, keepdims=True) + EPS)
                og_ref[sl, ln[hh]] = (o[hh] * r * gain_v * _silu(hg_ref[sl, ln[hh]])).astype(BF16)
            return carry

        lax.fori_loop(0, cb, one, 0, unroll=2)

    return pl.pallas_call(
        body, name="hgrn_fwd", grid=(nc // cb, HG_HEADS // HPS),
        in_specs=[_view_tile(v, rows, HPS * HEAD) for v in (hq, hf, hi, hg)] + [
                  pl.BlockSpec((2, HPS * HEAD), lambda c, g: (0, g)),
                  pl.BlockSpec((1, HEAD), lambda c, g: (0, 0)),
                  pl.BlockSpec(mstack.shape, lambda c, g: (0, 0)),
                  pl.BlockSpec(masks.shape, lambda c, g: (0, 0, 0)),
                  pl.BlockSpec(eye.shape, lambda c, g: (0, 0))],
        out_specs=[tile, tile, pl.BlockSpec((cb, HPS, HEAD, HEAD), lambda c, g: (c, g, 0, 0))],
        out_shape=[jax.ShapeDtypeStruct((t, HG_HEADS * HEAD), F32), jax.ShapeDtypeStruct((t, HG_HEADS * HEAD), BF16),
                   jax.ShapeDtypeStruct((nc, HG_HEADS, HEAD, HEAD), F32)],
        scratch_shapes=[pltpu.VMEM((HG_HEADS, HEAD, HEAD), F32)],
        compiler_params=_params(_ARB, _ARB))(hq[0], hf[0], hi[0], hg[0], logits, gain, mstack, masks, eye)


def _hgrn_bwd(hq, hf, hi, hg, logits, gain, oraw, ssave, dog, consts):
    t = hq[0].shape[0]
    nc = t // CHUNK
    cb = _chunks_per_step(nc)
    rows = cb * CHUNK
    nb = nc // cb
    mstack, mstack_t, masks, eye = consts
    tile = pl.BlockSpec((rows, HPS * HEAD), lambda c, g: (nb - 1 - c, g))

    def body(hq_ref, hf_ref, hi_ref, hg_ref, lg_ref, gain_ref, oraw_ref, ssave_ref, dog_ref, m_ref, mt_ref,
             mask_ref, eye_ref, dhq_ref, dhf_ref, dhi_ref, dhg_ref, dgain_ref, dlb_ref, dstate):
        c = pl.program_id(0)
        g = pl.program_id(1)

        @pl.when(c == 0)
        def _():
            for hh in range(HPS):
                dstate[g * HPS + hh] = jnp.zeros((HEAD, HEAD), F32)

        @pl.when((c == 0) & (g == 0))
        def _():
            dgain_ref[...] = jnp.zeros_like(dgain_ref)
            dlb_ref[...] = jnp.zeros_like(dlb_ref)

        lg_all = lg_ref[...]
        gain_v = gain_ref[...]
        eye_v = eye_ref[...]
        last_row = (lax.broadcasted_iota(jnp.int32, (CHUNK, HEAD), 0) == CHUNK - 1).astype(F32)

        def one(j, carry):
            i = cb - 1 - j
            sl = pl.ds(pl.multiple_of(i * CHUNK, CHUNK), CHUNK)
            hs = range(HPS)
            heads = [g * HPS + hh for hh in hs]
            ln = [slice(hh * HEAD, (hh + 1) * HEAD) for hh in hs]
            hqv = [hq_ref[sl, s] for s in ln]
            hgv = [hg_ref[sl, s] for s in ln]
            preps = [_hg_prep(a, hf_ref[sl, s], lg_all[:, s]) for a, s in zip(hqv, ln)]
            lb, sg, sgn, f, lf, kk, q = ([p[n] for p in preps] for n in range(7))
            v = [hi_ref[sl, s] for s in ln]
            ex = [jnp.exp(x) for x in _mx_each(m_ref[...], lf)]
            eb = [e[6 * CHUNK:7 * CHUNK] for e in ex]
            esfx = [e[7 * CHUNK:8 * CHUNK] for e in ex]
            p = _hg_scores(q, kk, ex, mask_ref)
            s0 = [ssave_ref[i, hh] for hh in hs]
            ds = [dstate[h] for h in heads]

            o = [oraw_ref[sl, s] for s in ln]
            r = [lax.rsqrt(jnp.mean(x * x, axis=1, keepdims=True) + EPS) for x in o]
            on = _each(lambda x, y: x * y, o, r)
            dg_out = [dog_ref[sl, s] for s in ln]
            sgate = [_silu(x) for x in hgv]
            for hh in hs:
                dhg_ref[sl, ln[hh]] = (dg_out[hh] * on[hh] * gain_v * _dsilu(hgv[hh])).astype(BF16)
            dgain_ref[...] += sum(jnp.sum(d * s * n, axis=0, keepdims=True) for d, s, n in zip(dg_out, sgate, on))
            don = _each(lambda d, s: d * s * gain_v, dg_out, sgate)
            do = _each(lambda rr, dn, n: rr * (dn - n * jnp.mean(dn * n, axis=1, keepdims=True)), r, don, on)

            dp = _each(_dot_nt, do, v)
            dv = _each(lambda pp, d, kx, ef, s: _dot_tn(pp, d) + _dot(kx * ef, s), p, do, kk, esfx, ds)
            dqb = _each(_dot_nt, do, s0)
            dkx = _each(_dot_nt, v, ds)
            diag = [_rowsum(mask_ref[6] * x) for x in dp]
            dq = _each(lambda a, e, d, kx: a * e + d * kx, dqb, eb, diag, kk)
            dk = _each(lambda a, e, d, qq: a * e + d * qq, dkx, esfx, diag, q)
            dxs = [[] for _ in hs]
            for lvl in range(6):
                el = [e[lvl * CHUNK:(lvl + 1) * CHUNK] for e in ex]
                gm = [mask_ref[lvl] * x for x in dp]
                a1 = _each(lambda m_, kx, e: _dot(m_, kx * e), gm, kk, el)
                a2 = _each(lambda m_, qq, e: _dot_tn(m_, qq * e), gm, q, el)
                dq = _each(lambda x, a, e: x + a * e, dq, a1, el)
                dk = _each(lambda x, a, e: x + a * e, dk, a2, el)
                for hh in hs:
                    dxs[hh].append((a1[hh] * q[hh] + a2[hh] * kk[hh]) * el[hh])
            e_end_row = [e[CHUNK - 1:CHUNK, :] for e in eb]
            ds_new = _each(lambda qq, e, d, er, s: _dot_tn(qq * e, d) + _row_to_col(er, eye_v) * s, q, eb, do, e_end_row, ds)
            for hh in hs:
                dstate[heads[hh]] = ds_new[hh]
                dend_row = _col_to_row(_rowsum(s0[hh] * ds[hh]), eye_v)
                dxs[hh].append(dqb[hh] * q[hh] * eb[hh] + last_row * (e_end_row[hh] * dend_row))
                dxs[hh].append(dkx[hh] * kk[hh] * esfx[hh])
            dlf = _mx_each(mt_ref[...], [jnp.concatenate(x, axis=0) for x in dxs])

            for hh in hs:
                dhi_ref[sl, ln[hh]] = dv[hh].astype(BF16)
                dhq_ref[sl, ln[hh]] = (dq[hh] * (HEAD ** -0.5) * _dsilu(hqv[hh])).astype(BF16)
                df = dlf[hh] / f[hh]
                dsig = (1.0 - lb[hh]) * sg[hh] * sgn[hh]
                dhf_ref[sl, ln[hh]] = ((df - dk[hh]) * dsig).astype(BF16)
                dlb_t = jnp.sum(df * sgn[hh] - dk[hh] * sgn[hh], axis=0, keepdims=True)
                dlb_ref[pl.ds(heads[hh], 1), :] += dlb_t * lb[hh] * (1.0 - lb[hh])
            return carry

        lax.fori_loop(0, cb, one, 0, unroll=2)

    outs = [jax.ShapeDtypeStruct((t, HG_HEADS * HEAD), BF16)] * 4 + [
        jax.ShapeDtypeStruct((1, HEAD), F32), jax.ShapeDtypeStruct((HG_HEADS, HEAD), F32)]
    return pl.pallas_call(
        body, name="hgrn_bwd", grid=(nb, HG_HEADS // HPS),
        in_specs=[_view_tile(v, rows, HPS * HEAD, lambda c: nb - 1 - c) for v in (hq, hf, hi, hg)] + [
                  pl.BlockSpec((2, HPS * HEAD), lambda c, g: (0, g)),
                  pl.BlockSpec((1, HEAD), lambda c, g: (0, 0)), tile,
                  pl.BlockSpec((cb, HPS, HEAD, HEAD), lambda c, g: (nb - 1 - c, g, 0, 0)), tile,
                  pl.BlockSpec(mstack.shape, lambda c, h: (0, 0)),
                  pl.BlockSpec(mstack_t.shape, lambda c, h: (0, 0)),
                  pl.BlockSpec(masks.shape, lambda c, h: (0, 0, 0)),
                  pl.BlockSpec(eye.shape, lambda c, h: (0, 0))],
        out_specs=[tile, tile, tile, tile, pl.BlockSpec((1, HEAD), lambda c, h: (0, 0)),
                   pl.BlockSpec((HG_HEADS, HEAD), lambda c, h: (0, 0))],
        out_shape=outs, scratch_shapes=[pltpu.VMEM((HG_HEADS, HEAD, HEAD), F32)],
        compiler_params=_params(_ARB, _ARB))(hq[0], hf[0], hi[0], hg[0], logits, gain, oraw, ssave, dog, mstack,
                                             mstack_t, masks, eye)


CONV_W = 512


def _per_head(fn, *arrs):
    width = arrs[0].shape[1]
    return jnp.concatenate([fn(*[a[:, j:j + HEAD] for a in arrs]) for j in range(0, width, HEAD)], axis=1)


def _shift_down(xv, halo, d, top_rows):
    if d == 0:
        return xv, xv[0:8]
    main = pltpu.roll(xv, d, 0)
    top = jnp.where(top_rows < d, pltpu.roll(halo, d, 0), main[0:8])
    return main, top


def _conv_parts(x_ref, halo_ref, w_ref, first):
    xv = x_ref[...]
    halo = jnp.where(first, 0.0, halo_ref[...])
    top_rows = lax.broadcasted_iota(jnp.int32, (8, xv.shape[1]), 0)
    shifted = [_shift_down(xv, halo, CONV_K - 1 - j, top_rows) for j in range(CONV_K)]
    w = w_ref[...]
    acc = sum(shifted[j][0] * w[j:j + 1, :] for j in range(CONV_K))
    acc_top = sum(shifted[j][1] * w[j:j + 1, :] for j in range(CONV_K))
    return shifted, acc, acc_top


def _conv_fwd(x, w8, l2scale, name):
    x, off, width = x
    t = x.shape[0]
    o = off // CONV_W
    tr = _pick(t, 512, 8)

    def post(cv):
        s = _silu(cv)
        if l2scale is not None:
            s = _per_head(lambda sh: sh * (lax.rsqrt(_rowsum(sh * sh) + EPS) * l2scale), s)
        return s

    def body(x_ref, halo_ref, w_ref, o_ref):
        _, acc, acc_top = _conv_parts(x_ref, halo_ref, w_ref, pl.program_id(1) == 0)
        o_ref[...] = post(acc)
        o_ref[0:8, :] = post(acc_top)

    return pl.pallas_call(
        body, name=name, grid=(width // CONV_W,t // tr),
        in_specs=[pl.BlockSpec((tr, CONV_W), lambda j, i: (i, o + j)),
                  pl.BlockSpec((8, CONV_W), lambda j, i: (jnp.maximum(i * (tr // 8) - 1, 0), o + j)),
                  pl.BlockSpec((8, CONV_W), lambda j, i: (0, j))],
        out_specs=pl.BlockSpec((tr, CONV_W), lambda j, i: (i, j)),
        out_shape=jax.ShapeDtypeStruct((t, width), F32), compiler_params=_params(_PAR, _PAR))(x, x, w8)


def _conv_bwd_a(x, w8, dy, l2scale, name):
    x, off, width = x
    t = x.shape[0]
    o = off // CONV_W
    tr = _pick(t, 512, 8)

    def l2_bwd(s, dyh):
        r = lax.rsqrt(_rowsum(s * s) + EPS)
        y0 = s * r
        dy0 = dyh * l2scale
        return r * (dy0 - y0 * _rowsum(dy0 * y0))

    def to_dc(cv, dyv):
        if l2scale is not None:
            dyv = _per_head(l2_bwd, _silu(cv), dyv)
        return dyv * _dsilu(cv)

    def body(x_ref, halo_ref, w_ref, dy_ref, dc_ref, dw_ref):
        @pl.when(pl.program_id(1) == 0)
        def _():
            dw_ref[...] = jnp.zeros_like(dw_ref)

        shifted, acc, acc_top = _conv_parts(x_ref, halo_ref, w_ref, pl.program_id(1) == 0)
        dyv = dy_ref[...]
        dc = to_dc(acc, dyv)
        dc_top = to_dc(acc_top, dyv[0:8])
        dc_ref[...] = dc
        dc_ref[0:8, :] = dc_top
        rest = (lax.broadcasted_iota(jnp.int32, dc.shape, 0) >= 8).astype(F32)
        dc_rest = dc * rest
        for j in range(CONV_K):
            dw_ref[j:j + 1, :] += (jnp.sum(dc_rest * shifted[j][0], axis=0, keepdims=True)
                                   + jnp.sum(dc_top * shifted[j][1], axis=0, keepdims=True))

    return pl.pallas_call(
        body, name=name, grid=(width // CONV_W,t // tr),
        in_specs=[pl.BlockSpec((tr, CONV_W), lambda j, i: (i, o + j)),
                  pl.BlockSpec((8, CONV_W), lambda j, i: (jnp.maximum(i * (tr // 8) - 1, 0), o + j)),
                  pl.BlockSpec((8, CONV_W), lambda j, i: (0, j)),
                  pl.BlockSpec((tr, CONV_W), lambda j, i: (i, j))],
        out_specs=[pl.BlockSpec((tr, CONV_W), lambda j, i: (i, j)), pl.BlockSpec((8, CONV_W), lambda j, i: (0, j))],
        out_shape=[jax.ShapeDtypeStruct((t, width), F32), jax.ShapeDtypeStruct((8, width), F32)],
        compiler_params=_params(_PAR, _ARB))(x, x, w8, dy)


def _conv_bwd_b(dc, w8, name):
    t, width = dc.shape
    tr = _pick(t, 512, 8)
    nt = t // tr

    def body(dc_ref, halo_ref, w_ref, dx_ref):
        dcv = dc_ref[...]
        halo = jnp.where(pl.program_id(1) == nt - 1, 0.0, halo_ref[...])
        w = w_ref[...]
        bot_rows = lax.broadcasted_iota(jnp.int32, (8, CONV_W), 0)
        acc = dcv * w[CONV_K - 1:CONV_K, :]
        acc_bot = dcv[tr - 8:tr] * w[CONV_K - 1:CONV_K, :]
        for d in range(1, CONV_K):
            main = pltpu.roll(dcv, tr - d, 0)
            bot = jnp.where(bot_rows >= 8 - d, pltpu.roll(halo, 8 - d, 0), main[tr - 8:tr])
            wj = w[CONV_K - 1 - d:CONV_K - d, :]
            acc = acc + main * wj
            acc_bot = acc_bot + bot * wj
        dx_ref[...] = acc.astype(BF16)
        dx_ref[tr - 16:tr, :] = jnp.concatenate([acc[tr - 16:tr - 8], acc_bot], axis=0).astype(BF16)

    return pl.pallas_call(
        body, name=name, grid=(width // CONV_W,nt),
        in_specs=[pl.BlockSpec((tr, CONV_W), lambda j, i: (i, j)),
                  pl.BlockSpec((8, CONV_W), lambda j, i: (jnp.minimum((i + 1) * (tr // 8), t // 8 - 1), j)),
                  pl.BlockSpec((8, CONV_W), lambda j, i: (0, j))],
        out_specs=pl.BlockSpec((tr, CONV_W), lambda j, i: (i, j)),
        out_shape=jax.ShapeDtypeStruct((t, width), BF16), compiler_params=_params(_PAR, _PAR))(dc, dc, w8)


def _each(f, *lists):
    return [f(*xs) for xs in zip(*lists)]


def _split2_each(xs):
    hi = [_bf(x) for x in xs]
    lo = [_bf(x - h.astype(F32)) for x, h in zip(xs, hi)]
    return hi, lo


def _hp_each(a_split, b_split):
    (ah, al), (bh, bl) = a_split, b_split
    rows = ah[0].shape[0]
    d12 = [jnp.dot(jnp.concatenate([x, y], axis=0), z, preferred_element_type=F32) for x, y, z in zip(ah, al, bh)]
    d3 = [jnp.dot(x, y, preferred_element_type=F32) for x, y in zip(ah, bl)]
    return [d[:rows] + d[rows:] + e for d, e in zip(d12, d3)]


def _tri_inv_each(a_list, eye):
    ns = [-a for a in a_list]
    ps = [eye + n for n in ns]
    n_split = _split2_each(ns)
    for _ in range(5):
        ns = _hp_each(n_split, n_split)
        n_split = _split2_each(ns)
        ps = [p + d for p, d in zip(ps, _hp_each(_split2_each(ps), n_split))]
    return ps


def _gd_gates(gab, alog, dtb):
    sp_arg = gab + dtb
    return sp_arg, -jnp.exp(alog) * _softplus(sp_arg), _sigmoid(gab)


def _gd_chunks(q, k, v, g_all, beta_all, sel, l_ref, mask_ref):
    incl, strict, eye, upper = mask_ref[0], mask_ref[1], mask_ref[2], mask_ref[3]
    lmat = l_ref[...]
    gates = jnp.concatenate(_split3(g_all) + _split3(beta_all), axis=0)
    picked = [jnp.dot(gates, s, preferred_element_type=F32) for s in sel]
    c = CHUNK
    gb = [p[0:c, :HEAD] + p[c:2 * c, :HEAD] + p[2 * c:3 * c, :HEAD] for p in picked]
    bb = [p[3 * c:4 * c, HEAD:] + p[4 * c:5 * c, HEAD:] + p[5 * c:, HEAD:] for p in picked]
    gam = _mx_each(lmat, gb)
    gam_row = [jnp.sum(x[:, :CHUNK] * upper, axis=0, keepdims=True) for x in gb]
    lm = _each(lambda gm, gr: incl * jnp.exp(jnp.minimum(gm[:, :CHUNK] - gr, 0.0)), gam, gam_row)
    kb = _each(lambda x, b: x * b, k, bb)
    a = _each(lambda x, y, m: strict * _dot_nt(x, y) * m, kb, k, lm)
    tm = _tri_inv_each(a, eye)
    eg = [jnp.exp(x) for x in gam]
    vb = _each(lambda x, b: x * b, v, bb)
    kbg = _each(lambda x, e: x * e, kb, eg)
    uw = _each(lambda t_, x, y: _dot(t_, jnp.concatenate([x, y], axis=1)), tm, vb, kbg)
    u = [x[:, :HEAD] for x in uw]
    w = [x[:, HEAD:] for x in uw]
    qk = _each(lambda x, y, m: _dot_nt(x, y) * m, q, k, lm)
    g_end = [x[CHUNK - 1:CHUNK, :] for x in gam]
    ekg = _each(lambda e, x: jnp.exp(e - x), g_end, gam)
    ge = [jnp.exp(e) for e in g_end]
    kg = _each(lambda x, e: x * e, k, ekg)
    qg = _each(lambda x, e: x * e, q, eg)
    names = ("bb", "lm", "kb", "a", "tm", "eg", "vb", "kbg", "u", "w", "qk", "ekg", "ge", "kg", "qg")
    cols = (bb, lm, kb, a, tm, eg, vb, kbg, u, w, qk, ekg, ge, kg, qg)
    return [dict(zip(names, vals)) for vals in zip(*cols)]


def _gd_specs(rows, rev_nb=None):
    def cidx(c):
        return c if rev_nb is None else rev_nb - 1 - c

    qk_tile = pl.BlockSpec((rows, HPS // 2 * HEAD), lambda c, g: (cidx(c), g))
    v_tile = pl.BlockSpec((rows, HPS * HEAD), lambda c, g: (cidx(c), g))
    gab_tile = pl.BlockSpec((rows, HEAD), lambda c, g: (cidx(c), 0))
    return qk_tile, v_tile, gab_tile


def _gdn_fwd(qn, kn, cv, gab, gz, alog, dtb, gain, consts):
    t = qn.shape[0]
    nc = t // CHUNK
    cb = _chunks_per_step(nc)
    rows = cb * CHUNK
    lmat, _, masks, sel = consts
    qk_tile, v_tile, gab_tile = _gd_specs(rows)
    row128 = pl.BlockSpec((1, HEAD), lambda c, h: (0, 0))

    def body(q_ref, k_ref, v_ref, gab_ref, gz_ref, alog_ref, dtb_ref, gain_ref, sel_ref, l_ref, mask_ref,
             oraw_ref, og_ref, ssave_ref, state):
        c = pl.program_id(0)
        g = pl.program_id(1)

        @pl.when(c == 0)
        def _():
            for hh in range(HPS):
                state[g * HPS + hh] = jnp.zeros((HEAD, HEAD), F32)

        alog = alog_ref[...]
        dtb = dtb_ref[...]
        gain_v = gain_ref[...]

        def one(i, carry):
            sl = pl.ds(pl.multiple_of(i * CHUNK, CHUNK), CHUNK)
            _, g_all, beta_all = _gd_gates(gab_ref[sl, :], alog, dtb)
            heads = [g * HPS + hh for hh in range(HPS)]
            lq = [slice(hh // 2 * HEAD, (hh // 2 + 1) * HEAD) for hh in range(HPS)]
            lv = [slice(hh * HEAD, (hh + 1) * HEAD) for hh in range(HPS)]
            chs = _gd_chunks([q_ref[sl, s] for s in lq], [k_ref[sl, s] for s in lq], [v_ref[sl, s] for s in lv],
                             g_all, beta_all, [sel_ref[h] for h in heads], l_ref, mask_ref)
            s0 = [state[h] for h in heads]
            ws = _each(lambda ch, s: _dot(jnp.concatenate([ch["w"], ch["qg"]], axis=0), s), chs, s0)
            v_new = _each(lambda ch, x: ch["u"] - x[:CHUNK], chs, ws)
            o = _each(lambda ch, x, vn: x[CHUNK:] + _dot(ch["qk"], vn), chs, ws, v_new)
            s1 = _each(lambda ch, s, vn: s * ch["ge"] + _dot_tn(ch["kg"], vn), chs, s0, v_new)
            for hh in range(HPS):
                ssave_ref[i, hh] = s0[hh]
                state[heads[hh]] = s1[hh]
                oraw_ref[sl, lv[hh]] = o[hh]
                r = lax.rsqrt(jnp.mean(o[hh] * o[hh], axis=1, keepdims=True) + EPS)
                og_ref[sl, lv[hh]] = (o[hh] * r * gain_v * _silu(gz_ref[sl, lv[hh]])).astype(BF16)
            return carry

        lax.fori_loop(0, cb, one, 0, unroll=2)

    return pl.pallas_call(
        body, name="gdn_fwd", grid=(nc // cb, GD_HEADS // HPS),
        in_specs=[qk_tile, qk_tile, v_tile, gab_tile, _view_tile(gz, rows, HPS * HEAD), row128, row128, row128,
                  pl.BlockSpec(sel.shape, lambda c, g: (0, 0, 0)),
                  pl.BlockSpec(lmat.shape, lambda c, g: (0, 0)),
                  pl.BlockSpec(masks.shape, lambda c, g: (0, 0, 0))],
        out_specs=[v_tile, v_tile, pl.BlockSpec((cb, HPS, HEAD, HEAD), lambda c, g: (c, g, 0, 0))],
        out_shape=[jax.ShapeDtypeStruct((t, GD_HEADS * HEAD), F32), jax.ShapeDtypeStruct((t, GD_HEADS * HEAD), BF16),
                   jax.ShapeDtypeStruct((nc, GD_HEADS, HEAD, HEAD), F32)],
        scratch_shapes=[pltpu.VMEM((GD_HEADS, HEAD, HEAD), F32)],
        compiler_params=_params(_ARB, _ARB))(qn, kn, cv, gab, gz[0], alog, dtb, gain, sel, lmat, masks)


def _gdn_bwd(qn, kn, cv, gab, gz, alog, dtb, gain, oraw, ssave, dog, consts):
    t = qn.shape[0]
    nc = t // CHUNK
    cb = _chunks_per_step(nc)
    rows = cb * CHUNK
    nb = nc // cb
    lmat, lmat_t, masks, sel = consts
    qk_tile, v_tile, gab_tile = _gd_specs(rows, nb)
    row128 = pl.BlockSpec((1, HEAD), lambda c, h: (0, 0))

    def body(q_ref, k_ref, v_ref, gab_ref, gz_ref, alog_ref, dtb_ref, gain_ref, oraw_ref, ssave_ref, dog_ref,
             sel_ref, l_ref, lt_ref, mask_ref,
             dq_ref, dk_ref, dv_ref, dgab_ref, dgz_ref, small_ref, dstate):
        c = pl.program_id(0)
        g = pl.program_id(1)

        @pl.when(c == 0)
        def _():
            for hh in range(HPS):
                dstate[g * HPS + hh] = jnp.zeros((HEAD, HEAD), F32)

        @pl.when((c == 0) & (g == 0))
        def _():
            small_ref[...] = jnp.zeros_like(small_ref)

        alog = alog_ref[...]
        dtb = dtb_ref[...]
        gain_v = gain_ref[...]
        lane = lax.broadcasted_iota(jnp.int32, (1, HEAD), 1)
        last_row = (lax.broadcasted_iota(jnp.int32, (CHUNK, HEAD), 0) == CHUNK - 1).astype(F32)

        def one(j, carry):
            i = cb - 1 - j
            sl = pl.ds(pl.multiple_of(i * CHUNK, CHUNK), CHUNK)
            sp_arg, g_all, beta_all = _gd_gates(gab_ref[sl, :], alog, dtb)
            strict, eye = mask_ref[1], mask_ref[2]
            ltm = lt_ref[...]
            hs = range(HPS)
            heads = [g * HPS + hh for hh in hs]
            lq = [slice(hh // 2 * HEAD, (hh // 2 + 1) * HEAD) for hh in hs]
            lv = [slice(hh * HEAD, (hh + 1) * HEAD) for hh in hs]
            q = [q_ref[sl, s] for s in lq]
            k = [k_ref[sl, s] for s in lq]
            v = [v_ref[sl, s] for s in lv]
            gzv = [gz_ref[sl, s] for s in lv]
            chs = _gd_chunks(q, k, v, g_all, beta_all, [sel_ref[h] for h in heads], l_ref, mask_ref)

            def col(name):
                return [ch[name] for ch in chs]

            def mul(x, y):
                return x * y

            tm, lm, eg, bb = col("tm"), col("lm"), col("eg"), col("bb")
            s0 = [ssave_ref[i, hh] for hh in hs]
            ds = [dstate[h] for h in heads]
            v_new = _each(lambda u, w, s: u - _dot(w, s), col("u"), col("w"), s0)

            o = [oraw_ref[sl, s] for s in lv]
            r = [lax.rsqrt(jnp.mean(x * x, axis=1, keepdims=True) + EPS) for x in o]
            on = _each(mul, o, r)
            dg_out = [dog_ref[sl, s] for s in lv]
            sgate = [_silu(x) for x in gzv]
            for hh in hs:
                dgz_ref[sl, lv[hh]] = (dg_out[hh] * on[hh] * gain_v * _dsilu(gzv[hh])).astype(BF16)
            small_ref[0:1, :] += sum(jnp.sum(d * s * n, axis=0, keepdims=True) for d, s, n in zip(dg_out, sgate, on))
            don = _each(lambda d, s: d * s * gain_v, dg_out, sgate)
            do = _each(lambda rr, dn, n: rr * (dn - n * jnp.mean(dn * n, axis=1, keepdims=True)), r, don, on)

            dv_new = _each(lambda a, d, b, s: _dot_tn(a, d) + _dot(b, s), col("qk"), do, col("kg"), ds)
            dqk = _each(_dot_nt, do, v_new)
            dkg = _each(_dot_nt, v_new, ds)
            dge = _each(lambda s, d: jnp.sum(_rowsum(s * d), axis=0, keepdims=True), s0, ds)
            both = _each(lambda d, dv: jnp.concatenate([d, dv], axis=0), do, dv_new)
            from_s = _each(_dot_nt, both, s0)
            dqg = [x[:CHUNK] for x in from_s]
            dw = [-x[CHUNK:] for x in from_s]
            ds_new = _each(lambda qg, w, bo, ge, s: _dot_tn(jnp.concatenate([qg, -w], axis=0), bo) + ge * s,
                           col("qg"), col("w"), both, col("ge"), ds)
            for hh in hs:
                dstate[heads[hh]] = ds_new[hh]

            side = _each(lambda dv, d: jnp.concatenate([dv, d], axis=1), dv_new, dw)
            back = _each(_dot_tn, tm, side)
            dvb = [x[:, :HEAD] for x in back]
            dkbg = [x[:, HEAD:] for x in back]
            dtm = _each(lambda sd, vb, kbg: _dot_nt(sd, jnp.concatenate([vb, kbg], axis=1)), side, col("vb"), col("kbg"))
            dtt = _each(_dot_nt, dtm, tm)
            da = _each(lambda t_, x: -_dot_tn(t_, x) * strict, tm, dtt)
            dal = _each(mul, da, lm)
            dqk_l = _each(mul, dqk, lm)
            stack = _each(lambda x, y: jnp.concatenate([x, y], axis=0), dal, dqk_l)
            on_k = _each(_dot, stack, k)
            dkb = _each(lambda x, y, e: x[:CHUNK] + y * e, on_k, dkbg, eg)
            dq = _each(lambda x, y, e: x[CHUNK:] + y * e, on_k, dqg, eg)
            dk = _each(lambda st, kb, qq, z, ekg, w_, b: _dot_tn(st, jnp.concatenate([kb, qq], axis=0)) + z * ekg + w_ * b,
                       stack, col("kb"), q, dkg, col("ekg"), dkb, bb)
            gmat = _each(lambda x, a, y, qk: x * a + y * qk, da, col("a"), dqk, col("qk"))
            t_kg = _each(lambda x, y: _rowsum(x * y), dkg, col("kg"))
            dgam = _each(lambda gm, x, qg, t_, y, kbg: (_rowsum(gm) - _row_to_col(jnp.sum(gm, axis=0, keepdims=True), eye)
                                                        + _rowsum(x * qg) - t_ + _rowsum(y * kbg)),
                         gmat, dqg, col("qg"), t_kg, dkbg, col("kbg"))
            dg_end = _each(lambda t_, e, ge: jnp.sum(t_, axis=0, keepdims=True) + e * ge[:, 0:1], t_kg, dge, col("ge"))
            dgam = _each(lambda x, e: x + last_row * e, dgam, dg_end)
            dbeta = _each(lambda x, kk, y, vv: _rowsum(x * kk) + _rowsum(y * vv), dkb, k, dvb, v)
            dg = _mx_each(ltm, dgam)

            for hh in hs:
                dv_ref[sl, lv[hh]] = dvb[hh] * bb[hh]
            fac_g = -jnp.exp(alog) * _sigmoid(sp_arg)
            fac_b = beta_all * (1.0 - beta_all)
            hot_g = [(lane == h).astype(F32) for h in heads]
            hot_b = [(lane == GD_HEADS + h).astype(F32) for h in heads]
            dga = _each(lambda x, hot: x * hot * fac_g, dg, hot_g)
            dgb = _each(lambda x, hot: x * hot * fac_b, dbeta, hot_b)
            small_ref[1:2, :] += sum(jnp.sum(x, axis=0, keepdims=True) for x in dga)
            small_ref[2:3, :] += sum(jnp.sum(x * hot * g_all, axis=0, keepdims=True) for x, hot in zip(dg, hot_g))
            for pair in range(HPS // 2):
                lqp = slice(pair * HEAD, (pair + 1) * HEAD)
                dq_ref[sl, lqp] = dq[2 * pair] + dq[2 * pair + 1]
                dk_ref[sl, lqp] = dk[2 * pair] + dk[2 * pair + 1]
            dgab_ref[sl, :] = sum(a + b for a, b in zip(dga, dgb))
            return carry

        lax.fori_loop(0, cb, one, 0, unroll=2)

    groups = GD_HEADS // HPS
    outs = [jax.ShapeDtypeStruct((t, 1024), F32), jax.ShapeDtypeStruct((t, 1024), F32),
            jax.ShapeDtypeStruct((t, 2048), F32), jax.ShapeDtypeStruct((t, groups * HEAD), F32),
            jax.ShapeDtypeStruct((t, 2048), BF16), jax.ShapeDtypeStruct((8, HEAD), F32)]
    return pl.pallas_call(
        body, name="gdn_bwd", grid=(nb, groups),
        in_specs=[qk_tile, qk_tile, v_tile, gab_tile, _view_tile(gz, rows, HPS * HEAD, lambda c: nb - 1 - c),
                  row128, row128, row128, v_tile,
                  pl.BlockSpec((cb, HPS, HEAD, HEAD), lambda c, g: (nb - 1 - c, g, 0, 0)), v_tile,
                  pl.BlockSpec(sel.shape, lambda c, g: (0, 0, 0)),
                  pl.BlockSpec(lmat.shape, lambda c, g: (0, 0)),
                  pl.BlockSpec(lmat_t.shape, lambda c, g: (0, 0)),
                  pl.BlockSpec(masks.shape, lambda c, g: (0, 0, 0))],
        out_specs=[qk_tile, qk_tile, v_tile, pl.BlockSpec((rows, HEAD), lambda c, g: (nb - 1 - c, g)), v_tile,
                   pl.BlockSpec((8, HEAD), lambda c, g: (0, 0))],
        out_shape=outs, scratch_shapes=[pltpu.VMEM((GD_HEADS, HEAD, HEAD), F32)],
        compiler_params=_params(_ARB, _ARB))(qn, kn, cv, gab, gz[0], alog, dtb, gain, oraw, ssave, dog, sel,
                                             lmat, lmat_t, masks)


def _fold_groups(wide):
    t, width = wide.shape
    tr = _pick(t, 512, 8)

    def body(w_ref, o_ref):
        acc = w_ref[:, 0:HEAD]
        for j in range(1, width // HEAD):
            acc = acc + w_ref[:, j * HEAD:(j + 1) * HEAD]
        o_ref[...] = acc.astype(BF16)

    return pl.pallas_call(
        body, name="fold_gate_grads", grid=(t // tr,), in_specs=[_row_spec(tr, width)], out_specs=_row_spec(tr, HEAD),
        out_shape=jax.ShapeDtypeStruct((t, HEAD), BF16), compiler_params=_params(_PAR))(wide)


def _adam_math(w, g, m, v):
    m2 = ADAM_B1 * m + (1.0 - ADAM_B1) * g
    v2 = ADAM_B2 * v + (1.0 - ADAM_B2) * (g * g)
    m_hat = m2 / (1.0 - ADAM_B1 ** ADAM_STEP)
    v_hat = v2 / (1.0 - ADAM_B2 ** ADAM_STEP)
    delta = -ADAM_LR * (m_hat / (jnp.sqrt(v_hat) + ADAM_EPS) + ADAM_WD * w)
    return delta, m2, v2


def _adamw(w, g, m, v, name):
    r, c = w.shape
    tr = r
    for cand in range(8, r + 1, 8):
        if r % cand == 0 and cand * c * 4 <= (1 << 20):
            tr = cand
    if r % 8 != 0:
        tr = r

    def body(w_ref, g_ref, m_ref, v_ref, d_ref, m2_ref, v2_ref):
        d, m2, v2 = _adam_math(w_ref[...], g_ref[...], m_ref[...], v_ref[...])
        d_ref[...] = d
        m2_ref[...] = m2
        v2_ref[...] = v2

    spec = pl.BlockSpec((tr, c), lambda i: (i, 0))
    return pl.pallas_call(
        body, name=name, grid=(r // tr,), in_specs=[spec] * 4, out_specs=[spec] * 3,
        out_shape=[jax.ShapeDtypeStruct((r, c), F32)] * 3, compiler_params=_params(_PAR))(w, g, m, v)


_ANY = pl.BlockSpec(memory_space=pl.ANY)


def _place():
    return lax.axis_index("x"), lax.axis_index("y"), lax.axis_index("c")


def _gather_weights(packs, nchs, name):
    n = len(packs)
    halves = [p.shape[0] // 2 for p in packs]
    base = [sum(nchs[:i]) for i in range(n)]
    total = sum(nchs)
    for p, h, k in zip(packs, halves, nchs):
        assert p.shape[0] == 2 * h and h % k == 0 and (h // k) % 16 == 0

    def body(*refs):
        p_refs, g_refs, (send_sems, recv_sems) = refs[:n], refs[n:2 * n], refs[2 * n:]
        x, y, c = _place()
        sibling = (x, y, 1 - c)
        chips = [(1 - x, y), (x, 1 - y), (1 - x, 1 - y)]
        chunks = [(a, q) for a in range(n) for q in range(nchs[a])]

        def rows_of(a, pc, q):
            ch = halves[a] // nchs[a]
            return pl.ds(pl.multiple_of(pc * halves[a] + q * ch, 16), ch)

        def piece(a, px, py, pc, q):
            return g_refs[a].at[2 * px + py, rows_of(a, pc, q), :]

        def copy(k, src, dst, to):
            return pltpu.make_async_remote_copy(src_ref=src, dst_ref=dst, send_sem=send_sems.at[k],
                                                recv_sem=recv_sems.at[k], device_id=to, device_id_type=MESH)

        def sem_of(j, a, q):
            return j * total + base[a] + q

        first = {(j, a, q): copy(sem_of(j, a, q), p_refs[a].at[rows_of(a, c, q), :], piece(a, x, y, c, q), (*chip, c))
                 for j, chip in enumerate(chips) for a, q in chunks}
        for a, q in chunks:
            for j in range(3):
                first[j, a, q].start()
        passed = {(j, a, q): copy(sem_of(3 + j, a, q), piece(a, *chip, c, q), piece(a, *chip, c, q), sibling)
                  for j, chip in enumerate(chips) for a, q in chunks}
        for a, q in chunks:
            for j, chip in enumerate(chips):
                copy(sem_of(j, a, q), p_refs[a].at[rows_of(a, c, q), :], piece(a, *chip, c, q), (*chip, c)).wait_recv()
                passed[j, a, q].start()
        for a, q in chunks:
            for j, chip in enumerate(chips):
                copy(sem_of(3 + j, a, q), piece(a, *chip, 1 - c, q), piece(a, *chip, 1 - c, q), sibling).wait_recv()
        for key in first:
            first[key].wait_send()
            passed[key].wait_send()

    return pl.pallas_call(
        body, name=name, out_shape=[jax.ShapeDtypeStruct((4,) + p.shape, p.dtype) for p in packs],
        in_specs=[_ANY] * n, out_specs=[_ANY] * n,
        scratch_shapes=[pltpu.SemaphoreType.DMA((6 * total,)), pltpu.SemaphoreType.DMA((6 * total,))])(*packs)


def _swap_with_sibling(arrs, nchs, lead, name, halves=False):
    n = len(arrs)
    jobs = []
    hs = [arr.shape[-2] // (2 if halves else 1) for arr in arrs]
    for a, (h, k) in enumerate(zip(hs, nchs)):
        assert h % k == 0 and (h // k) % 16 == 0
        for s in (range(lead) if lead else [None]):
            jobs += [(a, s, q * (h // k), h // k) for q in range(k)]

    def body(*refs):
        src, dst, (send_sems, recv_sems) = refs[:n], refs[n:2 * n], refs[2 * n:]
        x, y, c = _place()

        def at(ref, s, r0, rows):
            return ref.at[pl.ds(r0, rows), :] if s is None else ref.at[s, pl.ds(r0, rows), :]

        def src_rows(a, r0):
            return pl.multiple_of((1 - c) * hs[a] + r0, 16) if halves else r0

        copies = [pltpu.make_async_remote_copy(
            src_ref=at(src[a], s, src_rows(a, r0), rows), dst_ref=at(dst[a], s, r0, rows), send_sem=send_sems.at[k],
            recv_sem=recv_sems.at[k], device_id=(x, y, 1 - c), device_id_type=MESH)
            for k, (a, s, r0, rows) in enumerate(jobs)]
        for cp in copies:
            cp.start()
        for cp in copies:
            cp.wait()

    shapes = [jax.ShapeDtypeStruct(arr.shape[:-2] + (h, arr.shape[-1]), arr.dtype) for arr, h in zip(arrs, hs)]
    return pl.pallas_call(
        body, name=name, out_shape=shapes, in_specs=[_ANY] * n, out_specs=[_ANY] * n,
        scratch_shapes=[pltpu.SemaphoreType.DMA((len(jobs),)), pltpu.SemaphoreType.DMA((len(jobs),))])(*arrs)


def _add2(full, b, core, name):
    n, rows, w = b.shape
    tr = _pick(rows, 256, 16)
    nblk = rows // tr

    def body(c_ref, a_ref, b_ref, o_ref):
        o_ref[...] = (a_ref[...].astype(F32) + b_ref[...].astype(F32)).astype(BF16)

    spec = pl.BlockSpec((1, tr, w), lambda i, j, c_ref: (i, j, 0))
    grid_spec = pltpu.PrefetchScalarGridSpec(
        num_scalar_prefetch=1, grid=(n, nblk),
        in_specs=[pl.BlockSpec((1, tr, w), lambda i, j, c_ref: (i, c_ref[0] * nblk + j, 0)), spec], out_specs=spec)
    return pl.pallas_call(
        body, name=name, grid_spec=grid_spec, out_shape=jax.ShapeDtypeStruct(b.shape, BF16),
        compiler_params=_params(_PAR, _PAR))(core, full, b)


def _reduce_chips(partials, nchs, name):
    n = len(partials)
    jobs = []
    for a, (arr, k) in enumerate(zip(partials, nchs)):
        h = arr.shape[1]
        assert h % k == 0 and (h // k) % 16 == 0
        jobs += [(a, q * (h // k), h // k) for q in range(k)]

    def body(*refs):
        src, dst, (send_sems, recv_sems) = refs[:n], refs[n:2 * n], refs[2 * n:]
        x, y, c = _place()
        chips = [(1 - x, y), (x, 1 - y), (1 - x, 1 - y)]
        copies = [pltpu.make_async_remote_copy(
            src_ref=src[a].at[2 * px + py, pl.ds(r0, rows), :], dst_ref=dst[a].at[j, pl.ds(r0, rows), :],
            send_sem=send_sems.at[3 * k + j], recv_sem=recv_sems.at[3 * k + j],
            device_id=(px, py, c), device_id_type=MESH)
            for k, (a, r0, rows) in enumerate(jobs) for j, (px, py) in enumerate(chips)]
        for cp in copies:
            cp.start()
        for cp in copies:
            cp.wait()

    return pl.pallas_call(
        body, name=name,
        out_shape=[jax.ShapeDtypeStruct((3,) + p.shape[1:], p.dtype) for p in partials],
        in_specs=[_ANY] * n, out_specs=[_ANY] * n,
        scratch_shapes=[pltpu.SemaphoreType.DMA((3 * len(jobs),)), pltpu.SemaphoreType.DMA((3 * len(jobs),))])(*partials)


def _add4(own, got, name):
    rows, w = own.shape
    tr = _pick(rows, 128, 16)

    def body(a_ref, b_ref, o_ref):
        o_ref[...] = ((a_ref[...].astype(F32) + b_ref[0].astype(F32)) + b_ref[1].astype(F32)) + b_ref[2].astype(F32)

    return pl.pallas_call(
        body, name=name, grid=(rows // tr,),
        in_specs=[pl.BlockSpec((tr, w), lambda i: (i, 0)), pl.BlockSpec((3, tr, w), lambda i: (0, i, 0))],
        out_specs=pl.BlockSpec((tr, w), lambda i: (i, 0)), out_shape=jax.ShapeDtypeStruct((rows, w), F32),
        compiler_params=_params(_PAR))(own, got)


def _small_sync(gs, ws, ms, vs):
    rows = gs.shape[0]
    vmem = pl.BlockSpec(memory_space=pltpu.VMEM)

    def body(g_ref, w_ref, m_ref, v_ref, sum_ref, d_ref, m2_ref, v2_ref, buf, send_sems, recv_sems):
        x, y, c = _place()
        me = 4 * x + 2 * y + c
        buf[me] = g_ref[...]
        copies = []
        for k in range(1, 8):
            peer = (x ^ (k >> 2), y ^ ((k >> 1) & 1), c ^ (k & 1))
            copies.append(pltpu.make_async_remote_copy(
                src_ref=g_ref, dst_ref=buf.at[me], send_sem=send_sems.at[k - 1], recv_sem=recv_sems.at[k - 1],
                device_id=peer, device_id_type=MESH))
        for cp in copies:
            cp.start()
        for cp in copies:
            cp.wait()
        total = buf[0]
        for i in range(1, 8):
            total = total + buf[i]
        sum_ref[...] = total
        d, m2, v2 = _adam_math(w_ref[...], total, m_ref[...], v_ref[...])
        d_ref[...] = d
        m2_ref[...] = m2
        v2_ref[...] = v2

    shape = jax.ShapeDtypeStruct((rows, 128), F32)
    return pl.pallas_call(
        body, name="small_sync", out_shape=[shape] * 4, in_specs=[vmem] * 4, out_specs=[vmem] * 4,
        scratch_shapes=[pltpu.VMEM((8, rows, 128), F32), pltpu.SemaphoreType.DMA((7,)),
                        pltpu.SemaphoreType.DMA((7,))])(gs, ws, ms, vs)


_GROUPS = {
    "ffn1": dict(cols=("ffn1_w_in", 1408), rows=(("ffn1_w_out", 704, 704),), chunks=(8, 2)),
    "ffn2": dict(cols=("ffn2_w_in", 1408), rows=(("ffn2_w_out", 704, 704),), chunks=(8, 2)),
    "mixer": dict(cols=("w_in", 3080), chunks=(8, 4),
                  rows=(("w_branch_hgrn", 256, 256), ("w_branch_gdn", 512, 512), ("w_out", 256, 256),
                        ("gdn_conv_w", CONV_K, 128))),
}
_BIG_NAMES = tuple(n for g in _GROUPS.values() for n in (g["cols"][0],) + tuple(r[0] for r in g["rows"]))


def _group_names(group):
    return (group["cols"][0],) + tuple(r[0] for r in group["rows"])


def _pack(parts, lead, group):
    ax = len(lead)
    rows = []
    for n, r, padded in group["rows"]:
        p = parts[n]
        if padded != r:
            p = jnp.tile(p, (1,) * ax + (padded // r, 1))
        rows.append(p)
    return [parts[group["cols"][0]], rows[0] if len(rows) == 1 else jnp.concatenate(rows, axis=ax)]


def _unpack(cols, rows, group):
    out, off = {group["cols"][0]: cols}, 0
    for n, r, padded in group["rows"]:
        out[n] = rows[..., off:off + r, :]
        off += padded
    return out


def _is_col_sharded(name):
    return name in ("ffn1_w_in", "ffn2_w_in", "w_in", "gdn_conv_w")


def _full_from_shards(name, g):
    if _is_col_sharded(name):
        return jnp.transpose(g, (1, 0, 2)).reshape(g.shape[1], -1)
    return g.reshape(-1, g.shape[2])


def _shards_from_full(name, full):
    if _is_col_sharded(name):
        return jnp.transpose(full.reshape(full.shape[0], 4, -1), (1, 0, 2))
    return full.reshape(4, -1, full.shape[1])


_SMALL = (("ffn1_norm", 8), ("mix_norm", 8), ("hgrn_lb_logits", 16), ("hgrn_out_norm", 8), ("gdn_a_log", 8),
          ("gdn_dt_bias", 8), ("gdn_out_norm", 8), ("ffn2_norm", 8), ("final_norm", 8), ("loss", 8))
_SMALL_ROWS = sum(r for _, r in _SMALL)


def _pack_small(parts):
    out = []
    for name, rows in _SMALL:
        p = parts[name].reshape(-1).astype(F32)
        if p.shape[0] <= 128:
            if p.shape[0] < 128:
                p = jnp.concatenate([p, jnp.zeros((128 - p.shape[0],), F32)])
            p = jnp.broadcast_to(p.reshape(1, 128), (rows, 128))
        out.append(p.reshape(rows, 128))
    return jnp.concatenate(out, axis=0)


def _unpack_small(packed, shapes):
    out, off = {}, 0
    for name, rows in _SMALL:
        n = int(np.prod(shapes[name]))
        out[name] = packed[off:off + rows].reshape(-1)[:n].reshape(shapes[name])
        off += rows
    return out


def _ffn_fwd(x, gain, w_in, w_out, tag):
    n = _rmsnorm_fwd(x, gain, tag + "_norm")
    a, b, hm = _ffn_in_act(n, w_in, tag + "_in")
    out = _mm(hm, w_out, alpha=0.5, res=x, name=tag + "_out")
    return out, (n, a, b)


def _ffn_bwd(x, gain, w_in, w_out, saved, dout, dout_bf, tag):
    n, a, b = saved
    da, db, hm = _ffn_dact(dout_bf, w_out, a, b, tag + "_dact")
    dw_out = _mm(hm, dout_bf, ta=True, alpha=0.5, out_dtype=BF16, name=tag + "_dwout")
    dw_in = jnp.concatenate([_mm(n, da, ta=True, out_dtype=BF16, name=tag + "_dwin_a"),
                             _mm(n, db, ta=True, out_dtype=BF16, name=tag + "_dwin_b")], axis=1)
    dn = _mm(da, w_in[:, :D_FF], tb=True, name=tag + "_dnorm_a")
    dn = _mm(db, w_in[:, D_FF:], tb=True, res=dn, name=tag + "_dnorm_b")
    dx, dx_bf, dgain = _rmsnorm_bwd(x, gain, dn, dout, tag + "_dx")
    return dx, dx_bf, dgain, dw_in, dw_out


def _pad_lanes(v):
    return jnp.concatenate([v.reshape(1, -1), jnp.zeros((1, HEAD - v.size), F32)], axis=1)


def _local_step(x, tgt, small, exchange):
    hg_c = _hg_consts()
    gd_c = _gd_consts()
    alog = _pad_lanes(small["gdn_a_log"])
    dtb = _pad_lanes(small["gdn_dt_bias"])
    logits = small["hgrn_lb_logits"]
    hg_gain = small["hgrn_out_norm"].reshape(1, HEAD)
    gd_gain = small["gdn_out_norm"].reshape(1, HEAD)
    g1, gm, g2 = small["ffn1_norm"].reshape(1, -1), small["mix_norm"].reshape(1, -1), small["ffn2_norm"].reshape(1, -1)
    gf = small["final_norm"].reshape(1, -1)
    qscale = HEAD ** -0.5

    w1 = exchange.weights("ffn1")
    h1, ffn1_saved = _ffn_fwd(x, g1, w1["ffn1_w_in"], w1["ffn1_w_out"], "ffn1")
    u = _rmsnorm_fwd(h1, gm, "mix_norm")
    w = exchange.weights("mixer")
    seg, off = {}, 0
    for name, size in zip(IN_NAMES, IN_SIZES):
        seg[name] = w["w_in"][:, off:off + size]
        off += size
    w_gab = jnp.concatenate([seg["ga"], seg["gb"], jnp.zeros((D_MODEL, HEAD - 32), BF16)], axis=1)
    big_segs = [n for n in IN_NAMES if n not in ("ga", "gb")]
    conv8 = jnp.concatenate([w["gdn_conv_w"].astype(F32), jnp.zeros((8 - CONV_K, 4096), F32)], axis=0)
    conv_q, conv_k, conv_v = conv8[:, :1024], conv8[:, 1024:2048], conv8[:, 2048:]
    w_main = jnp.concatenate([seg[n] for n in big_segs], axis=1)
    proj = _mm(u, w_main, name="proj")
    pr, off = {}, 0
    for n in big_segs:
        pr[n] = _view(proj, off, seg[n].shape[1])
        off += seg[n].shape[1]
    gab = _mm(u, w_gab, name="proj_gab")
    oh_raw, oh, s_h = _hgrn_fwd(pr["hq"], pr["hf"], pr["hi"], pr["hg"], logits, hg_gain, hg_c)
    qn = _conv_fwd(pr["gq"], conv_q, qscale, "conv_q")
    kn = _conv_fwd(pr["gk"], conv_k, 1.0, "conv_k")
    cv = _conv_fwd(pr["gv"], conv_v, None, "conv_v")
    og_raw, og, s_g = _gdn_fwd(qn, kn, cv, gab, pr["gz"], alog, dtb, gd_gain, gd_c)
    yh = _mm(oh, w["w_branch_hgrn"], name="branch_h")
    yg = _mm(og, w["w_branch_gdn"], name="branch_g")
    ym = _merge_fwd(yh, yg, pr["gate_h"], pr["gate_g"])
    h2 = _mm(ym, w["w_out"], res=h1, name="mix_out")
    w2 = exchange.weights("ffn2")
    h3, ffn2_saved = _ffn_fwd(h2, g2, w2["ffn2_w_in"], w2["ffn2_w_out"], "ffn2")
    loss, dh3, dh3_bf, d_gf = _final_loss(h3, gf, tgt)

    dh2, dh2_bf, d_g2, d_f2in, d_f2out = _ffn_bwd(h2, g2, w2["ffn2_w_in"], w2["ffn2_w_out"], ffn2_saved, dh3, dh3_bf,
                                                  "ffn2")
    exchange.reduce("ffn2", {"ffn2_w_in": d_f2in, "ffn2_w_out": d_f2out})
    dym =_mm(dh2_bf, w["w_out"], tb=True, name="d_merge")
    d_wout = _mm(ym, dh2_bf, ta=True, out_dtype=BF16, name="d_w_out")
    dyh, dyg, d_gate_h, d_gate_g = _merge_bwd(dym, yh, yg, pr["gate_h"], pr["gate_g"])
    d_wbh = _mm(oh, dyh, ta=True, out_dtype=BF16, name="d_w_branch_h")
    d_wbg = _mm(og, dyg, ta=True, out_dtype=BF16, name="d_w_branch_g")
    doh = _mm(dyh, w["w_branch_hgrn"], tb=True, name="d_oh")
    dog = _mm(dyg, w["w_branch_gdn"], tb=True, name="d_og")
    d_hq, d_hf, d_hi, d_hg, d_hg_gain, d_lb0 = _hgrn_bwd(pr["hq"], pr["hf"], pr["hi"], pr["hg"], logits, hg_gain,
                                                        oh_raw, s_h, doh, hg_c)
    d_qn, d_kn, d_cv, d_gab_wide, d_gz, gd_small = _gdn_bwd(qn, kn, cv, gab, pr["gz"], alog, dtb, gd_gain, og_raw,
                                                            s_g, dog, gd_c)
    d_gab = _fold_groups(d_gab_wide)
    dc_q, dwc_q = _conv_bwd_a(pr["gq"], conv_q, d_qn, qscale, "dconv_q")
    dc_k, dwc_k = _conv_bwd_a(pr["gk"], conv_k, d_kn, 1.0, "dconv_k")
    dc_v, dwc_v = _conv_bwd_a(pr["gv"], conv_v, d_cv, None, "dconv_v")
    d_gq = _conv_bwd_b(dc_q, conv_q, "dconvx_q")
    d_gk = _conv_bwd_b(dc_k, conv_k, "dconvx_k")
    d_gv = _conv_bwd_b(dc_v, conv_v, "dconvx_v")
    dpr = {"hq": d_hq, "hf": d_hf, "hi": d_hi, "hg": d_hg, "gq": d_gq, "gk": d_gk, "gv": d_gv, "gz": d_gz,
           "gate_h": d_gate_h, "gate_g": d_gate_g}
    dproj = jnp.concatenate([dpr[n] for n in big_segs], axis=1)
    du = _mm(d_gab, w_gab, tb=True, name="du_gab")
    du = _mm(dproj, w_main, tb=True, res=du, name="du")
    d_wmain = _mm(u, dproj, ta=True, out_dtype=BF16, name="dw_main")
    d_wgab = _mm(u, d_gab, ta=True, out_dtype=BF16, name="dw_gab")
    d_win = jnp.concatenate([d_wmain[:, :8192], d_wgab[:, :32], d_wmain[:, 8192:]], axis=1)
    d_conv = jnp.concatenate([dwc_q[:CONV_K], dwc_k[:CONV_K], dwc_v[:CONV_K]], axis=1).astype(BF16)
    exchange.reduce("mixer", {"w_in": d_win, "gdn_conv_w": d_conv, "w_branch_hgrn": d_wbh, "w_branch_gdn": d_wbg,
                              "w_out": d_wout})
    dh1, dh1_bf, d_gm = _rmsnorm_bwd(h1, gm, du, dh2, "mix_dnorm")
    dx, _, d_g1, d_f1in, d_f1out = _ffn_bwd(x, g1, w1["ffn1_w_in"], w1["ffn1_w_out"], ffn1_saved, dh1, dh1_bf, "ffn1")
    exchange.reduce("ffn1", {"ffn1_w_in": d_f1in, "ffn1_w_out": d_f1out})
    d_lb0 = d_lb0.reshape(1, -1)
    sm = {"ffn1_norm": d_g1, "mix_norm": d_gm, "hgrn_lb_logits": jnp.concatenate([d_lb0, -d_lb0], axis=0),
          "hgrn_out_norm": d_hg_gain, "gdn_a_log": gd_small[2, :16], "gdn_dt_bias": gd_small[1, :16],
          "gdn_out_norm": gd_small[0], "ffn2_norm": d_g2, "final_norm": d_gf, "loss": loss[0, :1]}
    return dx, sm


class _Exchange:
    def __init__(self, wts):
        self.wts = wts
        xi, yi, ci = _place()
        self.chip = 2 * xi + yi
        self.south = ci == 0
        self.core = ci.reshape(1).astype(jnp.int32)
        self.mine = {}

    def weights(self, tag):
        group = _GROUPS[tag]
        names = _group_names(group)
        packs = _pack({n: self.wts[n][0].astype(BF16) for n in names}, (), group)
        others = _gather_weights(packs, group["chunks"], "gather_" + tag)
        whole = [lax.dynamic_update_index_in_dim(g, p, self.chip, 0) for g, p in zip(others, packs)]
        gathered = _unpack(*whole, group)
        return {n: _full_from_shards(n, gathered[n]) for n in names}

    def reduce(self, tag, grads):
        group = _GROUPS[tag]
        gpacks = _pack({n: _shards_from_full(n, grads[n]) for n in _group_names(group)}, (4,), group)
        got = _swap_with_sibling(gpacks, group["chunks"], 4, "reduce_pair_" + tag, halves=True)
        sums = [_add2(a, b, self.core, "add_pair_%s_%d" % (tag, i)) for i, (a, b) in enumerate(zip(gpacks, got))]
        from_chips = _reduce_chips(sums, group["chunks"], "reduce_chips_" + tag)
        self.mine[tag] = [_add4(lax.dynamic_index_in_dim(s, self.chip, axis=0, keepdims=False), f,
                                "add_chips_%s_%d" % (tag, i)) for i, (s, f) in enumerate(zip(sums, from_chips))]

    def finish(self):
        tags = list(self.mine)
        mine = [a for t in tags for a in self.mine[t]]
        nchs = [k for t in tags for k in _GROUPS[t]["chunks"]]
        theirs = _swap_with_sibling(mine, nchs, 0, "share_pair")
        whole = [jnp.concatenate([jnp.where(self.south, a, b), jnp.where(self.south, b, a)], axis=0)
                 for a, b in zip(mine, theirs)]
        reduced = {}
        for i, t in enumerate(tags):
            reduced.update(_unpack(whole[2 * i], whole[2 * i + 1], _GROUPS[t]))
        return reduced


_WEIGHTS = ("ffn1_norm", "ffn1_w_in", "ffn1_w_out", "mix_norm", "w_in", "hgrn_lb_logits", "hgrn_out_norm",
            "gdn_conv_w", "gdn_a_log", "gdn_dt_bias", "gdn_out_norm", "w_branch_hgrn", "w_branch_gdn", "w_out",
            "ffn2_norm", "ffn2_w_in", "ffn2_w_out", "final_norm")


def kernel(x, ffn1_norm, ffn1_w_in, ffn1_w_out, mix_norm, w_in, hgrn_lb_logits, hgrn_out_norm, gdn_conv_w, gdn_a_log, gdn_dt_bias, gdn_out_norm, w_branch_hgrn, w_branch_gdn, w_out, ffn2_norm, ffn2_w_in, ffn2_w_out, final_norm, loss_target, m_ffn1_norm, m_ffn1_w_in, m_ffn1_w_out, m_mix_norm, m_w_in, m_hgrn_lb_logits, m_hgrn_out_norm, m_gdn_conv_w, m_gdn_a_log, m_gdn_dt_bias, m_gdn_out_norm, m_w_branch_hgrn, m_w_branch_gdn, m_w_out, m_ffn2_norm, m_ffn2_w_in, m_ffn2_w_out, m_final_norm, v_ffn1_norm, v_ffn1_w_in, v_ffn1_w_out, v_mix_norm, v_w_in, v_hgrn_lb_logits, v_hgrn_out_norm, v_gdn_conv_w, v_gdn_a_log, v_gdn_dt_bias, v_gdn_out_norm, v_w_branch_hgrn, v_w_branch_gdn, v_w_out, v_ffn2_norm, v_ffn2_w_in, v_ffn2_w_out, v_final_norm):
    args = dict(locals())
    wts = {n: args[n] for n in _WEIGHTS}
    moms = {n: args["m_" + n] for n in _WEIGHTS}
    vars_ = {n: args["v_" + n] for n in _WEIGHTS}

    small = {n: wts[n].astype(F32) for n in _WEIGHTS if n not in _BIG_NAMES}
    exchange = _Exchange(wts)
    dx, small_grads = _local_step(x[0], loss_target[0], small, exchange)
    reduced = exchange.finish()

    out_g, out_d, out_m, out_v = {}, {}, {}, {}
    for n in _BIG_NAMES:
        shape = wts[n].shape
        w2 = wts[n].reshape(shape[-2], shape[-1])
        g2 = reduced[n]
        d, m2, v2 = _adamw(w2, g2, moms[n].reshape(w2.shape), vars_[n].reshape(w2.shape), "adamw_" + n)
        out_g[n], out_d[n], out_m[n], out_v[n] = g2.reshape(shape), d.reshape(shape), m2.reshape(shape), v2.reshape(shape)

    small_names = [n for n, _ in _SMALL]
    zero = jnp.zeros((1,), F32)
    shapes = {n: (wts[n].shape if n != "loss" else (1,)) for n in small_names}
    sums, sd, sm_, sv = _small_sync(
        _pack_small(small_grads),
        _pack_small({n: (wts[n] if n != "loss" else zero) for n in small_names}),
        _pack_small({n: (moms[n] if n != "loss" else zero) for n in small_names}),
        _pack_small({n: (vars_[n] if n != "loss" else zero) for n in small_names}))
    sg_u, sd_u, sm_u, sv_u = (_unpack_small(p, shapes) for p in (sums, sd, sm_, sv))
    for n in small_names:
        if n != "loss":
            out_g[n], out_d[n], out_m[n], out_v[n] = sg_u[n], sd_u[n], sm_u[n], sv_u[n]
    loss = sg_u["loss"].reshape(())

    return (loss, dx[None], *[out_g[n] for n in _WEIGHTS], *[out_d[n] for n in _WEIGHTS],
            *[out_m[n] for n in _WEIGHTS], *[out_v[n] for n in _WEIGHTS])
```

```python
import numpy as np

import jax
import jax.numpy as jnp
from jax import lax
from jax.experimental import pallas as pl
from jax.experimental.pallas import tpu as pltpu

F32 = jnp.float32
BF16 = jnp.bfloat16

D_MODEL = 1024
D_FF = 2816
CHUNK = 64
HEAD = 128
HG_HEADS = 8
GD_HEADS = 16
HPS = 8
COMM_CHUNKS = 9
MM_TM = 1408
MM_TN = 512
MM_TK = 1536
VMEM_LIMIT = 48 * 1024 * 1024
EPS = 1e-6
CONV_K = 4
IN_NAMES = ("hq", "hf", "hi", "hg", "gq", "gk", "gv", "ga", "gb", "gz", "gate_h", "gate_g")
IN_SIZES = (1024, 1024, 1024, 1024, 1024, 1024, 2048, 16, 16, 2048, 1024, 1024)
IN_WIDTH = sum(IN_SIZES)

ADAM_LR = 0.001
ADAM_B1 = 0.9
ADAM_B2 = 0.999
ADAM_EPS = 1e-08
ADAM_WD = 0.01
ADAM_STEP = 10

MESH = pl.DeviceIdType.MESH
_ARB = "arbitrary"
_PAR = "parallel"


def _bf(x):
    return x.astype(BF16)


def _dot(a, b):
    return jnp.dot(_bf(a), _bf(b), preferred_element_type=F32)


def _dot_nt(a, b):
    return lax.dot_general(_bf(a), _bf(b), (((1,), (1,)), ((), ())), preferred_element_type=F32)


def _dot_tn(a, b):
    return lax.dot_general(_bf(a), _bf(b), (((0,), (0,)), ((), ())), preferred_element_type=F32)


def _split3(x):
    hi = _bf(x)
    r = x - hi.astype(F32)
    mid = _bf(r)
    lo = _bf(r - mid.astype(F32))
    return hi, mid, lo


def _dot_mx(m, x):
    hi, mid, lo = _split3(x)
    return (jnp.dot(m, hi, preferred_element_type=F32) + jnp.dot(m, mid, preferred_element_type=F32)
            + jnp.dot(m, lo, preferred_element_type=F32))


def _dot_xm(x, m):
    hi, mid, lo = _split3(x)
    return (jnp.dot(hi, m, preferred_element_type=F32) + jnp.dot(mid, m, preferred_element_type=F32)
            + jnp.dot(lo, m, preferred_element_type=F32))


def _dot_hp(a, b):
    ah = _bf(a)
    al = _bf(a - ah.astype(F32))
    bh = _bf(b)
    bl = _bf(b - bh.astype(F32))
    return (jnp.dot(ah, bh, preferred_element_type=F32) + jnp.dot(ah, bl, preferred_element_type=F32)
            + jnp.dot(al, bh, preferred_element_type=F32))


def _sigmoid(x):
    return jax.nn.sigmoid(x)


def _silu(x):
    return x * _sigmoid(x)


def _dsilu(x):
    s = _sigmoid(x)
    return s * (1.0 + x * (1.0 - s))


def _softplus(x):
    return jnp.maximum(x, 0.0) + jnp.log(1.0 + jnp.exp(-jnp.abs(x)))


def _rowsum(x):
    return jnp.sum(x, axis=1, keepdims=True)


def _col_to_row(col, eye):
    return jnp.sum(eye * col, axis=0, keepdims=True)


def _row_to_col(row, eye):
    return jnp.sum(eye * row, axis=1, keepdims=True)


def _pick(dim, pref, unit=128):
    if dim <= pref:
        return dim
    t = pref
    while t >= unit:
        if dim % t == 0:
            return t
        t -= unit
    return dim


def _params(*sem):
    return pltpu.CompilerParams(dimension_semantics=tuple(sem), vmem_limit_bytes=VMEM_LIMIT)


def _mm(a, b, *, ta=False, tb=False, alpha=1.0, res=None, out_dtype=F32, name="mm"):
    m = a.shape[1] if ta else a.shape[0]
    k = a.shape[0] if ta else a.shape[1]
    n = b.shape[0] if tb else b.shape[1]
    assert k == (b.shape[1] if tb else b.shape[0])
    tm, tn, tk = _pick(m, MM_TM), _pick(n, MM_TN), _pick(k, MM_TK)
    if tn < MM_TN < n and n % MM_TM == 0:
        tn = MM_TM
    nk = k // tk
    a_spec = pl.BlockSpec((tk, tm), lambda i, j, l: (l, i)) if ta else pl.BlockSpec((tm, tk), lambda i, j, l: (i, l))
    b_spec = pl.BlockSpec((tn, tk), lambda i, j, l: (j, l)) if tb else pl.BlockSpec((tk, tn), lambda i, j, l: (l, j))
    o_spec = pl.BlockSpec((tm, tn), lambda i, j, l: (i, j))
    dims = (((0 if ta else 1,), (1 if tb else 0,)), ((), ()))
    has_res = res is not None

    def finish(r, r_ref, o_ref):
        if alpha != 1.0:
            r = r * alpha
        if has_res:
            r = r + r_ref[...]
        o_ref[...] = r.astype(out_dtype)

    def body(*refs):
        a_ref, b_ref = refs[0], refs[1]
        r_ref = refs[2] if has_res else None
        o_ref = refs[3] if has_res else refs[2]
        part = lax.dot_general(_bf(a_ref[...]), _bf(b_ref[...]), dims, preferred_element_type=F32)
        if nk == 1:
            finish(part, r_ref, o_ref)
            return
        acc = refs[-1]
        step = pl.program_id(2)

        @pl.when(step == 0)
        def _():
            acc[...] = part

        @pl.when(step != 0)
        def _():
            acc[...] += part

        @pl.when(step == nk - 1)
        def _():
            finish(acc[...], r_ref, o_ref)

    ins = [a, b] + ([res] if has_res else [])
    in_specs = [a_spec, b_spec] + ([o_spec] if has_res else [])
    return pl.pallas_call(
        body, name=name, grid=(m // tm, n // tn, nk), in_specs=in_specs, out_specs=o_spec,
        out_shape=jax.ShapeDtypeStruct((m, n), out_dtype),
        scratch_shapes=[pltpu.VMEM((tm, tn), F32)] if nk > 1 else [],
        compiler_params=_params(_PAR, _PAR, _ARB))(*ins)


def _row_spec(tr, w):
    return pl.BlockSpec((tr, w), lambda i: (i, 0))


def _full_spec(shape):
    return pl.BlockSpec(shape, lambda i: tuple(0 for _ in shape))


def _view(arr, off, width):
    return arr, off, width


def _view_rows(view, tr):
    _, off, width = view
    assert off % width == 0
    return pl.BlockSpec((tr, width), lambda i: (i, off // width))


def _view_tile(view, rows, bw, cidx=lambda c: c):
    _, off, width = view
    assert off % bw == 0 and width % bw == 0
    return pl.BlockSpec((rows, bw), lambda c, g: (cidx(c), off // bw + g))


def _rmsnorm_fwd(x, g, name):
    t, d = x.shape
    tr = _pick(t, 256, 8)

    def body(x_ref, g_ref, o_ref):
        xv = x_ref[...]
        r = lax.rsqrt(jnp.mean(xv * xv, axis=1, keepdims=True) + EPS)
        o_ref[...] = (xv * r * g_ref[...]).astype(BF16)

    return pl.pallas_call(
        body, name=name, grid=(t // tr,), in_specs=[_row_spec(tr, d), _full_spec((1, d))],
        out_specs=_row_spec(tr, d), out_shape=jax.ShapeDtypeStruct((t, d), BF16),
        compiler_params=_params(_PAR))(x, g)


def _rmsnorm_bwd(x, g, dn, res, name):
    t, d = x.shape
    tr = _pick(t, 256, 8)

    def body(x_ref, g_ref, dn_ref, r_ref, dx_ref, dxb_ref, dg_ref):
        @pl.when(pl.program_id(0) == 0)
        def _():
            dg_ref[...] = jnp.zeros_like(dg_ref)

        xv = x_ref[...]
        r = lax.rsqrt(jnp.mean(xv * xv, axis=1, keepdims=True) + EPS)
        xh = xv * r
        dy = dn_ref[...]
        dg_ref[...] += jnp.sum(dy * xh, axis=0, keepdims=True)
        dxh = dy * g_ref[...]
        dx = r_ref[...] + r * (dxh - xh * jnp.mean(dxh * xh, axis=1, keepdims=True))
        dx_ref[...] = dx
        dxb_ref[...] = dx.astype(BF16)

    return pl.pallas_call(
        body, name=name, grid=(t // tr,),
        in_specs=[_row_spec(tr, d), _full_spec((1, d)), _row_spec(tr, d), _row_spec(tr, d)],
        out_specs=[_row_spec(tr, d), _row_spec(tr, d), _full_spec((1, d))],
        out_shape=[jax.ShapeDtypeStruct((t, d), F32), jax.ShapeDtypeStruct((t, d), BF16),
                   jax.ShapeDtypeStruct((1, d), F32)],
        compiler_params=_params(_ARB))(x, g, dn, res)


FFN_TN = 256


def _ffn_in_act(n, w_in, name):
    t, d = n.shape
    tm = _pick(t, MM_TM)
    nf = D_FF // FFN_TN

    def body(n_ref, wa_ref, wb_ref, a_ref, b_ref, hm_ref):
        nv = n_ref[...]
        a = jnp.dot(nv, wa_ref[...], preferred_element_type=F32)
        b = jnp.dot(nv, wb_ref[...], preferred_element_type=F32)
        a_ref[...] = a.astype(BF16)
        b_ref[...] = b.astype(BF16)
        hm_ref[...] = (_silu(a) * b).astype(BF16)

    tile = pl.BlockSpec((tm, FFN_TN), lambda i, j: (i, j))
    return pl.pallas_call(
        body, name=name, grid=(t // tm, nf),
        in_specs=[pl.BlockSpec((tm, d), lambda i, j: (i, 0)), pl.BlockSpec((d, FFN_TN), lambda i, j: (0, j)),
                  pl.BlockSpec((d, FFN_TN), lambda i, j: (0, nf + j))],
        out_specs=[tile, tile, tile], out_shape=[jax.ShapeDtypeStruct((t, D_FF), BF16)] * 3,
        compiler_params=_params(_PAR, _PAR))(n, w_in, w_in)


def _ffn_dact(dout, w_out, a, b, name):
    t, d = dout.shape
    tm = _pick(t, MM_TM)

    def body(do_ref, w_ref, a_ref, b_ref, da_ref, db_ref, hm_ref):
        dh = 0.5 * _dot_nt(do_ref[...], w_ref[...])
        av = a_ref[...].astype(F32)
        bv = b_ref[...].astype(F32)
        sa = _silu(av)
        da_ref[...] = (dh * bv * _dsilu(av)).astype(BF16)
        db_ref[...] = (dh * sa).astype(BF16)
        hm_ref[...] = (sa * bv).astype(BF16)

    tile = pl.BlockSpec((tm, FFN_TN), lambda i, j: (i, j))
    return pl.pallas_call(
        body, name=name, grid=(t // tm, D_FF // FFN_TN),
        in_specs=[pl.BlockSpec((tm, d), lambda i, j: (i, 0)), pl.BlockSpec((FFN_TN, d), lambda i, j: (j, 0)), tile, tile],
        out_specs=[tile, tile, tile], out_shape=[jax.ShapeDtypeStruct((t, D_FF), BF16)] * 3,
        compiler_params=_params(_PAR, _PAR))(dout, w_out, a, b)


def _merge_fwd(yh, yg, gh, gg):
    t, d = yh.shape
    tr = _pick(t, 256, 8)

    def body(yh_ref, yg_ref, gh_ref, gg_ref, o_ref):
        o_ref[...] = (_sigmoid(gh_ref[...]) * yh_ref[...] + _sigmoid(gg_ref[...]) * yg_ref[...]).astype(BF16)

    return pl.pallas_call(
        body, name="merge_fwd", grid=(t // tr,),
        in_specs=[_row_spec(tr, d), _row_spec(tr, d), _view_rows(gh, tr), _view_rows(gg, tr)],
        out_specs=_row_spec(tr, d),
        out_shape=jax.ShapeDtypeStruct((t, d), BF16), compiler_params=_params(_PAR))(yh, yg, gh[0], gg[0])


def _merge_bwd(dy, yh, yg, gh, gg):
    t, d = yh.shape
    tr = _pick(t, 256, 8)

    def body(dy_ref, yh_ref, yg_ref, gh_ref, gg_ref, dyh_ref, dyg_ref, dgh_ref, dgg_ref):
        dyv = dy_ref[...]
        sh = _sigmoid(gh_ref[...])
        sg = _sigmoid(gg_ref[...])
        dyh_ref[...] = (dyv * sh).astype(BF16)
        dyg_ref[...] = (dyv * sg).astype(BF16)
        dgh_ref[...] = (dyv * yh_ref[...] * sh * (1.0 - sh)).astype(BF16)
        dgg_ref[...] = (dyv * yg_ref[...] * sg * (1.0 - sg)).astype(BF16)

    return pl.pallas_call(
        body, name="merge_bwd", grid=(t // tr,),
        in_specs=[_row_spec(tr, d)] * 3 + [_view_rows(gh, tr), _view_rows(gg, tr)], out_specs=[_row_spec(tr, d)] * 4,
        out_shape=[jax.ShapeDtypeStruct((t, d), BF16)] * 4,
        compiler_params=_params(_PAR))(dy, yh, yg, gh[0], gg[0])


def _final_loss(h, g, tgt):
    t, d = h.shape
    tr = _pick(t, 256, 8)

    def body(h_ref, g_ref, t_ref, loss_ref, dh_ref, dhb_ref, dg_ref):
        @pl.when(pl.program_id(0) == 0)
        def _():
            dg_ref[...] = jnp.zeros_like(dg_ref)
            loss_ref[...] = jnp.zeros_like(loss_ref)

        xv = h_ref[...]
        gv = g_ref[...]
        r = lax.rsqrt(jnp.mean(xv * xv, axis=1, keepdims=True) + EPS)
        xh = xv * r
        err = xh * gv - t_ref[...]
        loss_ref[...] += 0.5 * jnp.sum(jnp.mean(err * err, axis=1, keepdims=True), axis=0, keepdims=True)
        dy = err * (1.0 / d)
        dg_ref[...] += jnp.sum(dy * xh, axis=0, keepdims=True)
        dxh = dy * gv
        dh = r * (dxh - xh * jnp.mean(dxh * xh, axis=1, keepdims=True))
        dh_ref[...] = dh
        dhb_ref[...] = dh.astype(BF16)

    return pl.pallas_call(
        body, name="final_loss", grid=(t // tr,),
        in_specs=[_row_spec(tr, d), _full_spec((1, d)), _row_spec(tr, d)],
        out_specs=[_full_spec((1, 128)), _row_spec(tr, d), _row_spec(tr, d), _full_spec((1, d))],
        out_shape=[jax.ShapeDtypeStruct((1, 128), F32), jax.ShapeDtypeStruct((t, d), F32),
                   jax.ShapeDtypeStruct((t, d), BF16), jax.ShapeDtypeStruct((1, d), F32)],
        compiler_params=_params(_ARB))(h, g, tgt)


def _hg_consts():
    c = CHUNK
    t = np.arange(c)
    mats, masks = [], []
    for lvl in range(6):
        m = 1 << lvl
        blk = t // m
        mat = np.zeros((c, c), np.float32)
        for tt in range(c):
            b = blk[tt]
            if b % 2 == 1:
                mat[tt, b * m:tt + 1] = 1.0
            else:
                mat[tt, tt + 1:(b + 1) * m] = 1.0
        mats.append(mat)
        same = (t[:, None] // (2 * m)) == (t[None, :] // (2 * m))
        masks.append((same & (blk[:, None] % 2 == 1) & (blk[None, :] % 2 == 0)).astype(np.float32))
    pre = np.tril(np.ones((c, c), np.float32))
    suf = np.triu(np.ones((c, c), np.float32), 1)
    mstack = np.concatenate(mats + [pre, suf], 0)
    masks.append(np.eye(c, dtype=np.float32))
    return (jnp.asarray(mstack, BF16), jnp.asarray(mstack.T.copy(), BF16), jnp.asarray(np.stack(masks), F32),
            jnp.asarray(np.eye(HEAD, dtype=np.float32)))


def _gd_consts():
    c = CHUNK
    incl = np.tril(np.ones((c, c), np.float32))
    strict = np.tril(np.ones((c, c), np.float32), -1)
    eye = np.eye(c, dtype=np.float32)
    masks = np.stack([incl, strict, eye, incl.T.copy()])
    sel = np.zeros((GD_HEADS, HEAD, 2 * HEAD), np.float32)
    for j in range(GD_HEADS):
        sel[j, j, :HEAD] = 1.0
        sel[j, GD_HEADS + j, HEAD:] = 1.0
    return (jnp.asarray(incl, BF16), jnp.asarray(incl.T.copy(), BF16), jnp.asarray(masks, F32), jnp.asarray(sel, BF16))


def _chunks_per_step(nc):
    for cb in (32 // HPS, 2, 1):
        if nc % cb == 0:
            return cb
    return 1


def _hg_prep(hq, hf, lg):
    lb = _sigmoid(lg[0:1, :] - lg[1:2, :])
    sg = _sigmoid(hf)
    sgn = _sigmoid(-hf)
    f = lb + (1.0 - lb) * sg
    lf = jnp.log(f)
    kk = (1.0 - lb) * sgn
    q = _silu(hq) * (HEAD ** -0.5)
    return lb, sg, sgn, f, lf, kk, q


def _mx_each(m, xs):
    wide = [jnp.concatenate(_split3(x), axis=1) for x in xs]
    prods = [jnp.dot(m, w, preferred_element_type=F32) for w in wide]
    return [p[:, :HEAD] + p[:, HEAD:2 * HEAD] + p[:, 2 * HEAD:] for p in prods]


def _hg_scores(q, kk, ex, mask_ref):
    p = [mask_ref[6] * _rowsum(a * b) for a, b in zip(q, kk)]
    for lvl in range(6):
        el = [e[lvl * CHUNK:(lvl + 1) * CHUNK] for e in ex]
        d = [_dot_nt(a * e, b * e) for a, b, e in zip(q, kk, el)]
        p = [x + mask_ref[lvl] * y for x, y in zip(p, d)]
    return p


def _hgrn_fwd(hq, hf, hi, hg, logits, gain, consts):
    t = hq[0].shape[0]
    nc = t // CHUNK
    cb = _chunks_per_step(nc)
    rows = cb * CHUNK
    mstack, _, masks, eye = consts
    tile = pl.BlockSpec((rows, HPS * HEAD), lambda c, g: (c, g))

    def body(hq_ref, hf_ref, hi_ref, hg_ref, lg_ref, gain_ref, m_ref, mask_ref, eye_ref,
             oraw_ref, og_ref, ssave_ref, state):
        c = pl.program_id(0)
        g = pl.program_id(1)

        @pl.when(c == 0)
        def _():
            for hh in range(HPS):
                state[g * HPS + hh] = jnp.zeros((HEAD, HEAD), F32)

        lg_all = lg_ref[...]
        gain_v = gain_ref[...]

        def one(i, carry):
            sl = pl.ds(pl.multiple_of(i * CHUNK, CHUNK), CHUNK)
            hs = range(HPS)
            heads = [g * HPS + hh for hh in hs]
            ln = [slice(hh * HEAD, (hh + 1) * HEAD) for hh in hs]
            preps = [_hg_prep(hq_ref[sl, s], hf_ref[sl, s], lg_all[:, s]) for s in ln]
            lf, kk, q = [p[4] for p in preps], [p[5] for p in preps], [p[6] for p in preps]
            v = [hi_ref[sl, s] for s in ln]
            ex = [jnp.exp(x) for x in _mx_each(m_ref[...], lf)]
            eb = [e[6 * CHUNK:7 * CHUNK] for e in ex]
            esfx = [e[7 * CHUNK:8 * CHUNK] for e in ex]
            p = _hg_scores(q, kk, ex, mask_ref)
            s0 = [state[h] for h in heads]
            o = _each(lambda a, e, s, pp, vv: _dot(a * e, s) + _dot(pp, vv), q, eb, s0, p, v)
            eye_v = eye_ref[...]
            s1 = _each(lambda s, e, kx, ef, vv: s * _row_to_col(e[CHUNK - 1:CHUNK, :], eye_v) + _dot_tn(kx * ef, vv),
                       s0, eb, kk, esfx, v)
            for hh in hs:
                ssave_ref[i, hh] = s0[hh]
                state[heads[hh]] = s1[hh]
                oraw_ref[sl, ln[hh]] = o[hh]
                r = lax.rsqrt(jnp.mean(o[hh] * o[hh], axis=1, keepdims=True) + EPS)
                og_ref[sl, ln[hh]] = (o[hh] * r * gain_v * _silu(hg_ref[sl, ln[hh]])).astype(BF16)
            return carry

        lax.fori_loop(0, cb, one, 0, unroll=2)

    return pl.pallas_call(
        body, name="hgrn_fwd", grid=(nc // cb, HG_HEADS // HPS),
        in_specs=[_view_tile(v, rows, HPS * HEAD) for v in (hq, hf, hi, hg)] + [
                  pl.BlockSpec((2, HPS * HEAD), lambda c, g: (0, g)),
                  pl.BlockSpec((1, HEAD), lambda c, g: (0, 0)),
                  pl.BlockSpec(mstack.shape, lambda c, g: (0, 0)),
                  pl.BlockSpec(masks.shape, lambda c, g: (0, 0, 0)),
                  pl.BlockSpec(eye.shape, lambda c, g: (0, 0))],
        out_specs=[tile, tile, pl.BlockSpec((cb, HPS, HEAD, HEAD), lambda c, g: (c, g, 0, 0))],
        out_shape=[jax.ShapeDtypeStruct((t, HG_HEADS * HEAD), F32), jax.ShapeDtypeStruct((t, HG_HEADS * HEAD), BF16),
                   jax.ShapeDtypeStruct((nc, HG_HEADS, HEAD, HEAD), F32)],
        scratch_shapes=[pltpu.VMEM((HG_HEADS, HEAD, HEAD), F32)],
        compiler_params=_params(_ARB, _ARB))(hq[0], hf[0], hi[0], hg[0], logits, gain, mstack, masks, eye)


def _hgrn_bwd(hq, hf, hi, hg, logits, gain, oraw, ssave, dog, consts):
    t = hq[0].shape[0]
    nc = t // CHUNK
    cb = _chunks_per_step(nc)
    rows = cb * CHUNK
    nb = nc // cb
    mstack, mstack_t, masks, eye = consts
    tile = pl.BlockSpec((rows, HPS * HEAD), lambda c, g: (nb - 1 - c, g))

    def body(hq_ref, hf_ref, hi_ref, hg_ref, lg_ref, gain_ref, oraw_ref, ssave_ref, dog_ref, m_ref, mt_ref,
             mask_ref, eye_ref, dhq_ref, dhf_ref, dhi_ref, dhg_ref, dgain_ref, dlb_ref, dstate):
        c = pl.program_id(0)
        g = pl.program_id(1)

        @pl.when(c == 0)
        def _():
            for hh in range(HPS):
                dstate[g * HPS + hh] = jnp.zeros((HEAD, HEAD), F32)

        @pl.when((c == 0) & (g == 0))
        def _():
            dgain_ref[...] = jnp.zeros_like(dgain_ref)
            dlb_ref[...] = jnp.zeros_like(dlb_ref)

        lg_all = lg_ref[...]
        gain_v = gain_ref[...]
        eye_v = eye_ref[...]
        last_row = (lax.broadcasted_iota(jnp.int32, (CHUNK, HEAD), 0) == CHUNK - 1).astype(F32)

        def one(j, carry):
            i = cb - 1 - j
            sl = pl.ds(pl.multiple_of(i * CHUNK, CHUNK), CHUNK)
            hs = range(HPS)
            heads = [g * HPS + hh for hh in hs]
            ln = [slice(hh * HEAD, (hh + 1) * HEAD) for hh in hs]
            hqv = [hq_ref[sl, s] for s in ln]
            hgv = [hg_ref[sl, s] for s in ln]
            preps = [_hg_prep(a, hf_ref[sl, s], lg_all[:, s]) for a, s in zip(hqv, ln)]
            lb, sg, sgn, f, lf, kk, q = ([p[n] for p in preps] for n in range(7))
            v = [hi_ref[sl, s] for s in ln]
            ex = [jnp.exp(x) for x in _mx_each(m_ref[...], lf)]
            eb = [e[6 * CHUNK:7 * CHUNK] for e in ex]
            esfx = [e[7 * CHUNK:8 * CHUNK] for e in ex]
            p = _hg_scores(q, kk, ex, mask_ref)
            s0 = [ssave_ref[i, hh] for hh in hs]
            ds = [dstate[h] for h in heads]

            o = [oraw_ref[sl, s] for s in ln]
            r = [lax.rsqrt(jnp.mean(x * x, axis=1, keepdims=True) + EPS) for x in o]
            on = _each(lambda x, y: x * y, o, r)
            dg_out = [dog_ref[sl, s] for s in ln]
            sgate = [_silu(x) for x in hgv]
            for hh in hs:
                dhg_ref[sl, ln[hh]] = (dg_out[hh] * on[hh] * gain_v * _dsilu(hgv[hh])).astype(BF16)
            dgain_ref[...] += sum(jnp.sum(d * s * n, axis=0, keepdims=True) for d, s, n in zip(dg_out, sgate, on))
            don = _each(lambda d, s: d * s * gain_v, dg_out, sgate)
            do = _each(lambda rr, dn, n: rr * (dn - n * jnp.mean(dn * n, axis=1, keepdims=True)), r, don, on)

            dp = _each(_dot_nt, do, v)
            dv = _each(lambda pp, d, kx, ef, s: _dot_tn(pp, d) + _dot(kx * ef, s), p, do, kk, esfx, ds)
            dqb = _each(_dot_nt, do, s0)
            dkx = _each(_dot_nt, v, ds)
            diag = [_rowsum(mask_ref[6] * x) for x in dp]
            dq = _each(lambda a, e, d, kx: a * e + d * kx, dqb, eb, diag, kk)
            dk = _each(lambda a, e, d, qq: a * e + d * qq, dkx, esfx, diag, q)
            dxs = [[] for _ in hs]
            for lvl in range(6):
                el = [e[lvl * CHUNK:(lvl + 1) * CHUNK] for e in ex]
                gm = [mask_ref[lvl] * x for x in dp]
                a1 = _each(lambda m_, kx, e: _dot(m_, kx * e), gm, kk, el)
                a2 = _each(lambda m_, qq, e: _dot_tn(m_, qq * e), gm, q, el)
                dq = _each(lambda x, a, e: x + a * e, dq, a1, el)
                dk = _each(lambda x, a, e: x + a * e, dk, a2, el)
                for hh in hs:
                    dxs[hh].append((a1[hh] * q[hh] + a2[hh] * kk[hh]) * el[hh])
            e_end_row = [e[CHUNK - 1:CHUNK, :] for e in eb]
            ds_new = _each(lambda qq, e, d, er, s: _dot_tn(qq * e, d) + _row_to_col(er, eye_v) * s, q, eb, do, e_end_row, ds)
            for hh in hs:
                dstate[heads[hh]] = ds_new[hh]
                dend_row = _col_to_row(_rowsum(s0[hh] * ds[hh]), eye_v)
                dxs[hh].append(dqb[hh] * q[hh] * eb[hh] + last_row * (e_end_row[hh] * dend_row))
                dxs[hh].append(dkx[hh] * kk[hh] * esfx[hh])
            dlf = _mx_each(mt_ref[...], [jnp.concatenate(x, axis=0) for x in dxs])

            for hh in hs:
                dhi_ref[sl, ln[hh]] = dv[hh].astype(BF16)
                dhq_ref[sl, ln[hh]] = (dq[hh] * (HEAD ** -0.5) * _dsilu(hqv[hh])).astype(BF16)
                df = dlf[hh] / f[hh]
                dsig = (1.0 - lb[hh]) * sg[hh] * sgn[hh]
                dhf_ref[sl, ln[hh]] = ((df - dk[hh]) * dsig).astype(BF16)
                dlb_t = jnp.sum(df * sgn[hh] - dk[hh] * sgn[hh], axis=0, keepdims=True)
                dlb_ref[pl.ds(heads[hh], 1), :] += dlb_t * lb[hh] * (1.0 - lb[hh])
            return carry

        lax.fori_loop(0, cb, one, 0, unroll=2)

    outs = [jax.ShapeDtypeStruct((t, HG_HEADS * HEAD), BF16)] * 4 + [
        jax.ShapeDtypeStruct((1, HEAD), F32), jax.ShapeDtypeStruct((HG_HEADS, HEAD), F32)]
    return pl.pallas_call(
        body, name="hgrn_bwd", grid=(nb, HG_HEADS // HPS),
        in_specs=[_view_tile(v, rows, HPS * HEAD, lambda c: nb - 1 - c) for v in (hq, hf, hi, hg)] + [
                  pl.BlockSpec((2, HPS * HEAD), lambda c, g: (0, g)),
                  pl.BlockSpec((1, HEAD), lambda c, g: (0, 0)), tile,
                  pl.BlockSpec((cb, HPS, HEAD, HEAD), lambda c, g: (nb - 1 - c, g, 0, 0)), tile,
                  pl.BlockSpec(mstack.shape, lambda c, h: (0, 0)),
                  pl.BlockSpec(mstack_t.shape, lambda c, h: (0, 0)),
                  pl.BlockSpec(masks.shape, lambda c, h: (0, 0, 0)),
                  pl.BlockSpec(eye.shape, lambda c, h: (0, 0))],
        out_specs=[tile, tile, tile, tile, pl.BlockSpec((1, HEAD), lambda c, h: (0, 0)),
                   pl.BlockSpec((HG_HEADS, HEAD), lambda c, h: (0, 0))],
        out_shape=outs, scratch_shapes=[pltpu.VMEM((HG_HEADS, HEAD, HEAD), F32)],
        compiler_params=_params(_ARB, _ARB))(hq[0], hf[0], hi[0], hg[0], logits, gain, oraw, ssave, dog, mstack,
                                             mstack_t, masks, eye)


CONV_W = 512


def _per_head(fn, *arrs):
    width = arrs[0].shape[1]
    return jnp.concatenate([fn(*[a[:, j:j + HEAD] for a in arrs]) for j in range(0, width, HEAD)], axis=1)


def _shift_down(xv, halo, d, top_rows):
    if d == 0:
        return xv, xv[0:8]
    main = pltpu.roll(xv, d, 0)
    top = jnp.where(top_rows < d, pltpu.roll(halo, d, 0), main[0:8])
    return main, top


def _conv_parts(x_ref, halo_ref, w_ref, first):
    xv = x_ref[...]
    halo = jnp.where(first, 0.0, halo_ref[...])
    top_rows = lax.broadcasted_iota(jnp.int32, (8, xv.shape[1]), 0)
    shifted = [_shift_down(xv, halo, CONV_K - 1 - j, top_rows) for j in range(CONV_K)]
    w = w_ref[...]
    acc = sum(shifted[j][0] * w[j:j + 1, :] for j in range(CONV_K))
    acc_top = sum(shifted[j][1] * w[j:j + 1, :] for j in range(CONV_K))
    return shifted, acc, acc_top


def _conv_fwd(x, w8, l2scale, name):
    x, off, width = x
    t = x.shape[0]
    o = off // CONV_W
    tr = _pick(t, 512, 8)

    def post(cv):
        s = _silu(cv)
        if l2scale is not None:
            s = _per_head(lambda sh: sh * (lax.rsqrt(_rowsum(sh * sh) + EPS) * l2scale), s)
        return s

    def body(x_ref, halo_ref, w_ref, o_ref):
        _, acc, acc_top = _conv_parts(x_ref, halo_ref, w_ref, pl.program_id(1) == 0)
        o_ref[...] = post(acc)
        o_ref[0:8, :] = post(acc_top)

    return pl.pallas_call(
        body, name=name, grid=(width // CONV_W,t // tr),
        in_specs=[pl.BlockSpec((tr, CONV_W), lambda j, i: (i, o + j)),
                  pl.BlockSpec((8, CONV_W), lambda j, i: (jnp.maximum(i * (tr // 8) - 1, 0), o + j)),
                  pl.BlockSpec((8, CONV_W), lambda j, i: (0, j))],
        out_specs=pl.BlockSpec((tr, CONV_W), lambda j, i: (i, j)),
        out_shape=jax.ShapeDtypeStruct((t, width), F32), compiler_params=_params(_PAR, _PAR))(x, x, w8)


def _conv_bwd_a(x, w8, dy, l2scale, name):
    x, off, width = x
    t = x.shape[0]
    o = off // CONV_W
    tr = _pick(t, 512, 8)

    def l2_bwd(s, dyh):
        r = lax.rsqrt(_rowsum(s * s) + EPS)
        y0 = s * r
        dy0 = dyh * l2scale
        return r * (dy0 - y0 * _rowsum(dy0 * y0))

    def to_dc(cv, dyv):
        if l2scale is not None:
            dyv = _per_head(l2_bwd, _silu(cv), dyv)
        return dyv * _dsilu(cv)

    def body(x_ref, halo_ref, w_ref, dy_ref, dc_ref, dw_ref):
        @pl.when(pl.program_id(1) == 0)
        def _():
            dw_ref[...] = jnp.zeros_like(dw_ref)

        shifted, acc, acc_top = _conv_parts(x_ref, halo_ref, w_ref, pl.program_id(1) == 0)
        dyv = dy_ref[...]
        dc = to_dc(acc, dyv)
        dc_top = to_dc(acc_top, dyv[0:8])
        dc_ref[...] = dc
        dc_ref[0:8, :] = dc_top
        rest = (lax.broadcasted_iota(jnp.int32, dc.shape, 0) >= 8).astype(F32)
        dc_rest = dc * rest
        for j in range(CONV_K):
            dw_ref[j:j + 1, :] += (jnp.sum(dc_rest * shifted[j][0], axis=0, keepdims=True)
                                   + jnp.sum(dc_top * shifted[j][1], axis=0, keepdims=True))

    return pl.pallas_call(
        body, name=name, grid=(width // CONV_W,t // tr),
        in_specs=[pl.BlockSpec((tr, CONV_W), lambda j, i: (i, o + j)),
                  pl.BlockSpec((8, CONV_W), lambda j, i: (jnp.maximum(i * (tr // 8) - 1, 0), o + j)),
                  pl.BlockSpec((8, CONV_W), lambda j, i: (0, j)),
                  pl.BlockSpec((tr, CONV_W), lambda j, i: (i, j))],
        out_specs=[pl.BlockSpec((tr, CONV_W), lambda j, i: (i, j)), pl.BlockSpec((8, CONV_W), lambda j, i: (0, j))],
        out_shape=[jax.ShapeDtypeStruct((t, width), F32), jax.ShapeDtypeStruct((8, width), F32)],
        compiler_params=_params(_PAR, _ARB))(x, x, w8, dy)


def _conv_bwd_b(dc, w8, name):
    t, width = dc.shape
    tr = _pick(t, 512, 8)
    nt = t // tr

    def body(dc_ref, halo_ref, w_ref, dx_ref):
        dcv = dc_ref[...]
        halo = jnp.where(pl.program_id(1) == nt - 1, 0.0, halo_ref[...])
        w = w_ref[...]
        bot_rows = lax.broadcasted_iota(jnp.int32, (8, CONV_W), 0)
        acc = dcv * w[CONV_K - 1:CONV_K, :]
        acc_bot = dcv[tr - 8:tr] * w[CONV_K - 1:CONV_K, :]
        for d in range(1, CONV_K):
            main = pltpu.roll(dcv, tr - d, 0)
            bot = jnp.where(bot_rows >= 8 - d, pltpu.roll(halo, 8 - d, 0), main[tr - 8:tr])
            wj = w[CONV_K - 1 - d:CONV_K - d, :]
            acc = acc + main * wj
            acc_bot = acc_bot + bot * wj
        dx_ref[...] = acc.astype(BF16)
        dx_ref[tr - 16:tr, :] = jnp.concatenate([acc[tr - 16:tr - 8], acc_bot], axis=0).astype(BF16)

    return pl.pallas_call(
        body, name=name, grid=(width // CONV_W,nt),
        in_specs=[pl.BlockSpec((tr, CONV_W), lambda j, i: (i, j)),
                  pl.BlockSpec((8, CONV_W), lambda j, i: (jnp.minimum((i + 1) * (tr // 8), t // 8 - 1), j)),
                  pl.BlockSpec((8, CONV_W), lambda j, i: (0, j))],
        out_specs=pl.BlockSpec((tr, CONV_W), lambda j, i: (i, j)),
        out_shape=jax.ShapeDtypeStruct((t, width), BF16), compiler_params=_params(_PAR, _PAR))(dc, dc, w8)


def _each(f, *lists):
    return [f(*xs) for xs in zip(*lists)]


def _split2_each(xs):
    hi = [_bf(x) for x in xs]
    lo = [_bf(x - h.astype(F32)) for x, h in zip(xs, hi)]
    return hi, lo


def _hp_each(a_split, b_split):
    (ah, al), (bh, bl) = a_split, b_split
    rows = ah[0].shape[0]
    d12 = [jnp.dot(jnp.concatenate([x, y], axis=0), z, preferred_element_type=F32) for x, y, z in zip(ah, al, bh)]
    d3 = [jnp.dot(x, y, preferred_element_type=F32) for x, y in zip(ah, bl)]
    return [d[:rows] + d[rows:] + e for d, e in zip(d12, d3)]


def _tri_inv_each(a_list, eye):
    ns = [-a for a in a_list]
    ps = [eye + n for n in ns]
    n_split = _split2_each(ns)
    for _ in range(5):
        ns = _hp_each(n_split, n_split)
        n_split = _split2_each(ns)
        ps = [p + d for p, d in zip(ps, _hp_each(_split2_each(ps), n_split))]
    return ps


def _gd_gates(gab, alog, dtb):
    sp_arg = gab + dtb
    return sp_arg, -jnp.exp(alog) * _softplus(sp_arg), _sigmoid(gab)


def _gd_chunks(q, k, v, g_all, beta_all, sel, l_ref, mask_ref):
    incl, strict, eye, upper = mask_ref[0], mask_ref[1], mask_ref[2], mask_ref[3]
    lmat = l_ref[...]
    gates = jnp.concatenate(_split3(g_all) + _split3(beta_all), axis=0)
    picked = [jnp.dot(gates, s, preferred_element_type=F32) for s in sel]
    c = CHUNK
    gb = [p[0:c, :HEAD] + p[c:2 * c, :HEAD] + p[2 * c:3 * c, :HEAD] for p in picked]
    bb = [p[3 * c:4 * c, HEAD:] + p[4 * c:5 * c, HEAD:] + p[5 * c:, HEAD:] for p in picked]
    gam = _mx_each(lmat, gb)
    gam_row = [jnp.sum(x[:, :CHUNK] * upper, axis=0, keepdims=True) for x in gb]
    lm = _each(lambda gm, gr: incl * jnp.exp(jnp.minimum(gm[:, :CHUNK] - gr, 0.0)), gam, gam_row)
    kb = _each(lambda x, b: x * b, k, bb)
    a = _each(lambda x, y, m: strict * _dot_nt(x, y) * m, kb, k, lm)
    tm = _tri_inv_each(a, eye)
    eg = [jnp.exp(x) for x in gam]
    vb = _each(lambda x, b: x * b, v, bb)
    kbg = _each(lambda x, e: x * e, kb, eg)
    uw = _each(lambda t_, x, y: _dot(t_, jnp.concatenate([x, y], axis=1)), tm, vb, kbg)
    u = [x[:, :HEAD] for x in uw]
    w = [x[:, HEAD:] for x in uw]
    qk = _each(lambda x, y, m: _dot_nt(x, y) * m, q, k, lm)
    g_end = [x[CHUNK - 1:CHUNK, :] for x in gam]
    ekg = _each(lambda e, x: jnp.exp(e - x), g_end, gam)
    ge = [jnp.exp(e) for e in g_end]
    kg = _each(lambda x, e: x * e, k, ekg)
    qg = _each(lambda x, e: x * e, q, eg)
    names = ("bb", "lm", "kb", "a", "tm", "eg", "vb", "kbg", "u", "w", "qk", "ekg", "ge", "kg", "qg")
    cols = (bb, lm, kb, a, tm, eg, vb, kbg, u, w, qk, ekg, ge, kg, qg)
    return [dict(zip(names, vals)) for vals in zip(*cols)]


def _gd_specs(rows, rev_nb=None):
    def cidx(c):
        return c if rev_nb is None else rev_nb - 1 - c

    qk_tile = pl.BlockSpec((rows, HPS // 2 * HEAD), lambda c, g: (cidx(c), g))
    v_tile = pl.BlockSpec((rows, HPS * HEAD), lambda c, g: (cidx(c), g))
    gab_tile = pl.BlockSpec((rows, HEAD), lambda c, g: (cidx(c), 0))
    return qk_tile, v_tile, gab_tile


def _gdn_fwd(qn, kn, cv, gab, gz, alog, dtb, gain, consts):
    t = qn.shape[0]
    nc = t // CHUNK
    cb = _chunks_per_step(nc)
    rows = cb * CHUNK
    lmat, _, masks, sel = consts
    qk_tile, v_tile, gab_tile = _gd_specs(rows)
    row128 = pl.BlockSpec((1, HEAD), lambda c, h: (0, 0))

    def body(q_ref, k_ref, v_ref, gab_ref, gz_ref, alog_ref, dtb_ref, gain_ref, sel_ref, l_ref, mask_ref,
             oraw_ref, og_ref, ssave_ref, state):
        c = pl.program_id(0)
        g = pl.program_id(1)

        @pl.when(c == 0)
        def _():
            for hh in range(HPS):
                state[g * HPS + hh] = jnp.zeros((HEAD, HEAD), F32)

        alog = alog_ref[...]
        dtb = dtb_ref[...]
        gain_v = gain_ref[...]

        def one(i, carry):
            sl = pl.ds(pl.multiple_of(i * CHUNK, CHUNK), CHUNK)
            _, g_all, beta_all = _gd_gates(gab_ref[sl, :], alog, dtb)
            heads = [g * HPS + hh for hh in range(HPS)]
            lq = [slice(hh // 2 * HEAD, (hh // 2 + 1) * HEAD) for hh in range(HPS)]
            lv = [slice(hh * HEAD, (hh + 1) * HEAD) for hh in range(HPS)]
            chs = _gd_chunks([q_ref[sl, s] for s in lq], [k_ref[sl, s] for s in lq], [v_ref[sl, s] for s in lv],
                             g_all, beta_all, [sel_ref[h] for h in heads], l_ref, mask_ref)
            s0 = [state[h] for h in heads]
            ws = _each(lambda ch, s: _dot(jnp.concatenate([ch["w"], ch["qg"]], axis=0), s), chs, s0)
            v_new = _each(lambda ch, x: ch["u"] - x[:CHUNK], chs, ws)
            o = _each(lambda ch, x, vn: x[CHUNK:] + _dot(ch["qk"], vn), chs, ws, v_new)
            s1 = _each(lambda ch, s, vn: s * ch["ge"] + _dot_tn(ch["kg"], vn), chs, s0, v_new)
            for hh in range(HPS):
                ssave_ref[i, hh] = s0[hh]
                state[heads[hh]] = s1[hh]
                oraw_ref[sl, lv[hh]] = o[hh]
                r = lax.rsqrt(jnp.mean(o[hh] * o[hh], axis=1, keepdims=True) + EPS)
                og_ref[sl, lv[hh]] = (o[hh] * r * gain_v * _silu(gz_ref[sl, lv[hh]])).astype(BF16)
            return carry

        lax.fori_loop(0, cb, one, 0, unroll=2)

    return pl.pallas_call(
        body, name="gdn_fwd", grid=(nc // cb, GD_HEADS // HPS),
        in_specs=[qk_tile, qk_tile, v_tile, gab_tile, _view_tile(gz, rows, HPS * HEAD), row128, row128, row128,
                  pl.BlockSpec(sel.shape, lambda c, g: (0, 0, 0)),
                  pl.BlockSpec(lmat.shape, lambda c, g: (0, 0)),
                  pl.BlockSpec(masks.shape, lambda c, g: (0, 0, 0))],
        out_specs=[v_tile, v_tile, pl.BlockSpec((cb, HPS, HEAD, HEAD), lambda c, g: (c, g, 0, 0))],
        out_shape=[jax.ShapeDtypeStruct((t, GD_HEADS * HEAD), F32), jax.ShapeDtypeStruct((t, GD_HEADS * HEAD), BF16),
                   jax.ShapeDtypeStruct((nc, GD_HEADS, HEAD, HEAD), F32)],
        scratch_shapes=[pltpu.VMEM((GD_HEADS, HEAD, HEAD), F32)],
        compiler_params=_params(_ARB, _ARB))(qn, kn, cv, gab, gz[0], alog, dtb, gain, sel, lmat, masks)


def _gdn_bwd(qn, kn, cv, gab, gz, alog, dtb, gain, oraw, ssave, dog, consts):
    t = qn.shape[0]
    nc = t // CHUNK
    cb = _chunks_per_step(nc)
    rows = cb * CHUNK
    nb = nc // cb
    lmat, lmat_t, masks, sel = consts
    qk_tile, v_tile, gab_tile = _gd_specs(rows, nb)
    row128 = pl.BlockSpec((1, HEAD), lambda c, h: (0, 0))

    def body(q_ref, k_ref, v_ref, gab_ref, gz_ref, alog_ref, dtb_ref, gain_ref, oraw_ref, ssave_ref, dog_ref,
             sel_ref, l_ref, lt_ref, mask_ref,
             dq_ref, dk_ref, dv_ref, dgab_ref, dgz_ref, small_ref, dstate):
        c = pl.program_id(0)
        g = pl.program_id(1)

        @pl.when(c == 0)
        def _():
            for hh in range(HPS):
                dstate[g * HPS + hh] = jnp.zeros((HEAD, HEAD), F32)

        @pl.when((c == 0) & (g == 0))
        def _():
            small_ref[...] = jnp.zeros_like(small_ref)

        alog = alog_ref[...]
        dtb = dtb_ref[...]
        gain_v = gain_ref[...]
        lane = lax.broadcasted_iota(jnp.int32, (1, HEAD), 1)
        last_row = (lax.broadcasted_iota(jnp.int32, (CHUNK, HEAD), 0) == CHUNK - 1).astype(F32)

        def one(j, carry):
            i = cb - 1 - j
            sl = pl.ds(pl.multiple_of(i * CHUNK, CHUNK), CHUNK)
            sp_arg, g_all, beta_all = _gd_gates(gab_ref[sl, :], alog, dtb)
            strict, eye = mask_ref[1], mask_ref[2]
            ltm = lt_ref[...]
            hs = range(HPS)
            heads = [g * HPS + hh for hh in hs]
            lq = [slice(hh // 2 * HEAD, (hh // 2 + 1) * HEAD) for hh in hs]
            lv = [slice(hh * HEAD, (hh + 1) * HEAD) for hh in hs]
            q = [q_ref[sl, s] for s in lq]
            k = [k_ref[sl, s] for s in lq]
            v = [v_ref[sl, s] for s in lv]
            gzv = [gz_ref[sl, s] for s in lv]
            chs = _gd_chunks(q, k, v, g_all, beta_all, [sel_ref[h] for h in heads], l_ref, mask_ref)

            def col(name):
                return [ch[name] for ch in chs]

            def mul(x, y):
                return x * y

            tm, lm, eg, bb = col("tm"), col("lm"), col("eg"), col("bb")
            s0 = [ssave_ref[i, hh] for hh in hs]
            ds = [dstate[h] for h in heads]
            v_new = _each(lambda u, w, s: u - _dot(w, s), col("u"), col("w"), s0)

            o = [oraw_ref[sl, s] for s in lv]
            r = [lax.rsqrt(jnp.mean(x * x, axis=1, keepdims=True) + EPS) for x in o]
            on = _each(mul, o, r)
            dg_out = [dog_ref[sl, s] for s in lv]
            sgate = [_silu(x) for x in gzv]
            for hh in hs:
                dgz_ref[sl, lv[hh]] = (dg_out[hh] * on[hh] * gain_v * _dsilu(gzv[hh])).astype(BF16)
            small_ref[0:1, :] += sum(jnp.sum(d * s * n, axis=0, keepdims=True) for d, s, n in zip(dg_out, sgate, on))
            don = _each(lambda d, s: d * s * gain_v, dg_out, sgate)
            do = _each(lambda rr, dn, n: rr * (dn - n * jnp.mean(dn * n, axis=1, keepdims=True)), r, don, on)

            dv_new = _each(lambda a, d, b, s: _dot_tn(a, d) + _dot(b, s), col("qk"), do, col("kg"), ds)
            dqk = _each(_dot_nt, do, v_new)
            dkg = _each(_dot_nt, v_new, ds)
            dge = _each(lambda s, d: jnp.sum(_rowsum(s * d), axis=0, keepdims=True), s0, ds)
            both = _each(lambda d, dv: jnp.concatenate([d, dv], axis=0), do, dv_new)
            from_s = _each(_dot_nt, both, s0)
            dqg = [x[:CHUNK] for x in from_s]
            dw = [-x[CHUNK:] for x in from_s]
            ds_new = _each(lambda qg, w, bo, ge, s: _dot_tn(jnp.concatenate([qg, -w], axis=0), bo) + ge * s,
                           col("qg"), col("w"), both, col("ge"), ds)
            for hh in hs:
                dstate[heads[hh]] = ds_new[hh]

            side = _each(lambda dv, d: jnp.concatenate([dv, d], axis=1), dv_new, dw)
            back = _each(_dot_tn, tm, side)
            dvb = [x[:, :HEAD] for x in back]
            dkbg = [x[:, HEAD:] for x in back]
            dtm = _each(lambda sd, vb, kbg: _dot_nt(sd, jnp.concatenate([vb, kbg], axis=1)), side, col("vb"), col("kbg"))
            dtt = _each(_dot_nt, dtm, tm)
            da = _each(lambda t_, x: -_dot_tn(t_, x) * strict, tm, dtt)
            dal = _each(mul, da, lm)
            dqk_l = _each(mul, dqk, lm)
            stack = _each(lambda x, y: jnp.concatenate([x, y], axis=0), dal, dqk_l)
            on_k = _each(_dot, stack, k)
            dkb = _each(lambda x, y, e: x[:CHUNK] + y * e, on_k, dkbg, eg)
            dq = _each(lambda x, y, e: x[CHUNK:] + y * e, on_k, dqg, eg)
            dk = _each(lambda st, kb, qq, z, ekg, w_, b: _dot_tn(st, jnp.concatenate([kb, qq], axis=0)) + z * ekg + w_ * b,
                       stack, col("kb"), q, dkg, col("ekg"), dkb, bb)
            gmat = _each(lambda x, a, y, qk: x * a + y * qk, da, col("a"), dqk, col("qk"))
            t_kg = _each(lambda x, y: _rowsum(x * y), dkg, col("kg"))
            dgam = _each(lambda gm, x, qg, t_, y, kbg: (_rowsum(gm) - _row_to_col(jnp.sum(gm, axis=0, keepdims=True), eye)
                                                        + _rowsum(x * qg) - t_ + _rowsum(y * kbg)),
                         gmat, dqg, col("qg"), t_kg, dkbg, col("kbg"))
            dg_end = _each(lambda t_, e, ge: jnp.sum(t_, axis=0, keepdims=True) + e * ge[:, 0:1], t_kg, dge, col("ge"))
            dgam = _each(lambda x, e: x + last_row * e, dgam, dg_end)
            dbeta = _each(lambda x, kk, y, vv: _rowsum(x * kk) + _rowsum(y * vv), dkb, k, dvb, v)
            dg = _mx_each(ltm, dgam)

            for hh in hs:
                dv_ref[sl, lv[hh]] = dvb[hh] * bb[hh]
            fac_g = -jnp.exp(alog) * _sigmoid(sp_arg)
            fac_b = beta_all * (1.0 - beta_all)
            hot_g = [(lane == h).astype(F32) for h in heads]
            hot_b = [(lane == GD_HEADS + h).astype(F32) for h in heads]
            dga = _each(lambda x, hot: x * hot * fac_g, dg, hot_g)
            dgb = _each(lambda x, hot: x * hot * fac_b, dbeta, hot_b)
            small_ref[1:2, :] += sum(jnp.sum(x, axis=0, keepdims=True) for x in dga)
            small_ref[2:3, :] += sum(jnp.sum(x * hot * g_all, axis=0, keepdims=True) for x, hot in zip(dg, hot_g))
            for pair in range(HPS // 2):
                lqp = slice(pair * HEAD, (pair + 1) * HEAD)
                dq_ref[sl, lqp] = dq[2 * pair] + dq[2 * pair + 1]
                dk_ref[sl, lqp] = dk[2 * pair] + dk[2 * pair + 1]
            dgab_ref[sl, :] = sum(a + b for a, b in zip(dga, dgb))
            return carry

        lax.fori_loop(0, cb, one, 0, unroll=2)

    groups = GD_HEADS // HPS
    outs = [jax.ShapeDtypeStruct((t, 1024), F32), jax.ShapeDtypeStruct((t, 1024), F32),
            jax.ShapeDtypeStruct((t, 2048), F32), jax.ShapeDtypeStruct((t, groups * HEAD), F32),
            jax.ShapeDtypeStruct((t, 2048), BF16), jax.ShapeDtypeStruct((8, HEAD), F32)]
    return pl.pallas_call(
        body, name="gdn_bwd", grid=(nb, groups),
        in_specs=[qk_tile, qk_tile, v_tile, gab_tile, _view_tile(gz, rows, HPS * HEAD, lambda c: nb - 1 - c),
                  row128, row128, row128, v_tile,
                  pl.BlockSpec((cb, HPS, HEAD, HEAD), lambda c, g: (nb - 1 - c, g, 0, 0)), v_tile,
                  pl.BlockSpec(sel.shape, lambda c, g: (0, 0, 0)),
                  pl.BlockSpec(lmat.shape, lambda c, g: (0, 0)),
                  pl.BlockSpec(lmat_t.shape, lambda c, g: (0, 0)),
                  pl.BlockSpec(masks.shape, lambda c, g: (0, 0, 0))],
        out_specs=[qk_tile, qk_tile, v_tile, pl.BlockSpec((rows, HEAD), lambda c, g: (nb - 1 - c, g)), v_tile,
                   pl.BlockSpec((8, HEAD), lambda c, g: (0, 0))],
        out_shape=outs, scratch_shapes=[pltpu.VMEM((GD_HEADS, HEAD, HEAD), F32)],
        compiler_params=_params(_ARB, _ARB))(qn, kn, cv, gab, gz[0], alog, dtb, gain, oraw, ssave, dog, sel,
                                             lmat, lmat_t, masks)


def _fold_groups(wide):
    t, width = wide.shape
    tr = _pick(t, 512, 8)

    def body(w_ref, o_ref):
        acc = w_ref[:, 0:HEAD]
        for j in range(1, width // HEAD):
            acc = acc + w_ref[:, j * HEAD:(j + 1) * HEAD]
        o_ref[...] = acc.astype(BF16)

    return pl.pallas_call(
        body, name="fold_gate_grads", grid=(t // tr,), in_specs=[_row_spec(tr, width)], out_specs=_row_spec(tr, HEAD),
        out_shape=jax.ShapeDtypeStruct((t, HEAD), BF16), compiler_params=_params(_PAR))(wide)


def _adam_math(w, g, m, v):
    m2 = ADAM_B1 * m + (1.0 - ADAM_B1) * g
    v2 = ADAM_B2 * v + (1.0 - ADAM_B2) * (g * g)
    m_hat = m2 / (1.0 - ADAM_B1 ** ADAM_STEP)
    v_hat = v2 / (1.0 - ADAM_B2 ** ADAM_STEP)
    delta = -ADAM_LR * (m_hat / (jnp.sqrt(v_hat) + ADAM_EPS) + ADAM_WD * w)
    return delta, m2, v2


def _adamw(w, g, m, v, name):
    r, c = w.shape
    tr = r
    for cand in range(8, r + 1, 8):
        if r % cand == 0 and cand * c * 4 <= (1 << 20):
            tr = cand
    if r % 8 != 0:
        tr = r

    def body(w_ref, g_ref, m_ref, v_ref, d_ref, m2_ref, v2_ref):
        d, m2, v2 = _adam_math(w_ref[...], g_ref[...], m_ref[...], v_ref[...])
        d_ref[...] = d
        m2_ref[...] = m2
        v2_ref[...] = v2

    spec = pl.BlockSpec((tr, c), lambda i: (i, 0))
    return pl.pallas_call(
        body, name=name, grid=(r // tr,), in_specs=[spec] * 4, out_specs=[spec] * 3,
        out_shape=[jax.ShapeDtypeStruct((r, c), F32)] * 3, compiler_params=_params(_PAR))(w, g, m, v)


_ANY = pl.BlockSpec(memory_space=pl.ANY)


def _place():
    return lax.axis_index("x"), lax.axis_index("y"), lax.axis_index("c")


def _gather_weights(packs, nchs, name):
    n = len(packs)
    halves = [p.shape[0] // 2 for p in packs]
    base = [sum(nchs[:i]) for i in range(n)]
    total = sum(nchs)
    for p, h, k in zip(packs, halves, nchs):
        assert p.shape[0] == 2 * h and h % k == 0 and (h // k) % 16 == 0

    def body(*refs):
        p_refs, g_refs, (send_sems, recv_sems) = refs[:n], refs[n:2 * n], refs[2 * n:]
        x, y, c = _place()
        sibling = (x, y, 1 - c)
        chips = [(1 - x, y), (x, 1 - y), (1 - x, 1 - y)]
        chunks = [(a, q) for a in range(n) for q in range(nchs[a])]

        def rows_of(a, pc, q):
            ch = halves[a] // nchs[a]
            return pl.ds(pl.multiple_of(pc * halves[a] + q * ch, 16), ch)

        def piece(a, px, py, pc, q):
            return g_refs[a].at[2 * px + py, rows_of(a, pc, q), :]

        def copy(k, src, dst, to):
            return pltpu.make_async_remote_copy(src_ref=src, dst_ref=dst, send_sem=send_sems.at[k],
                                                recv_sem=recv_sems.at[k], device_id=to, device_id_type=MESH)

        def sem_of(j, a, q):
            return j * total + base[a] + q

        first = {(j, a, q): copy(sem_of(j, a, q), p_refs[a].at[rows_of(a, c, q), :], piece(a, x, y, c, q), (*chip, c))
                 for j, chip in enumerate(chips) for a, q in chunks}
        for a, q in chunks:
            for j in range(3):
                first[j, a, q].start()
        passed = {(j, a, q): copy(sem_of(3 + j, a, q), piece(a, *chip, c, q), piece(a, *chip, c, q), sibling)
                  for j, chip in enumerate(chips) for a, q in chunks}
        for a, q in chunks:
            for j, chip in enumerate(chips):
                copy(sem_of(j, a, q), p_refs[a].at[rows_of(a, c, q), :], piece(a, *chip, c, q), (*chip, c)).wait_recv()
                passed[j, a, q].start()
        for a, q in chunks:
            for j, chip in enumerate(chips):
                copy(sem_of(3 + j, a, q), piece(a, *chip, 1 - c, q), piece(a, *chip, 1 - c, q), sibling).wait_recv()
        for key in first:
            first[key].wait_send()
            passed[key].wait_send()

    return pl.pallas_call(
        body, name=name, out_shape=[jax.ShapeDtypeStruct((4,) + p.shape, p.dtype) for p in packs],
        in_specs=[_ANY] * n, out_specs=[_ANY] * n,
        scratch_shapes=[pltpu.SemaphoreType.DMA((6 * total,)), pltpu.SemaphoreType.DMA((6 * total,))])(*packs)


def _swap_with_sibling(arrs, nchs, lead, name, halves=False):
    n = len(arrs)
    jobs = []
    hs = [arr.shape[-2] // (2 if halves else 1) for arr in arrs]
    for a, (h, k) in enumerate(zip(hs, nchs)):
        assert h % k == 0 and (h // k) % 16 == 0
        for s in (range(lead) if lead else [None]):
            jobs += [(a, s, q * (h // k), h // k) for q in range(k)]

    def body(*refs):
        src, dst, (send_sems, recv_sems) = refs[:n], refs[n:2 * n], refs[2 * n:]
        x, y, c = _place()

        def at(ref, s, r0, rows):
            return ref.at[pl.ds(r0, rows), :] if s is None else ref.at[s, pl.ds(r0, rows), :]

        def src_rows(a, r0):
            return pl.multiple_of((1 - c) * hs[a] + r0, 16) if halves else r0

        copies = [pltpu.make_async_remote_copy(
            src_ref=at(src[a], s, src_rows(a, r0), rows), dst_ref=at(dst[a], s, r0, rows), send_sem=send_sems.at[k],
            recv_sem=recv_sems.at[k], device_id=(x, y, 1 - c), device_id_type=MESH)
            for k, (a, s, r0, rows) in enumerate(jobs)]
        for cp in copies:
            cp.start()
        for cp in copies:
            cp.wait()

    shapes = [jax.ShapeDtypeStruct(arr.shape[:-2] + (h, arr.shape[-1]), arr.dtype) for arr, h in zip(arrs, hs)]
    return pl.pallas_call(
        body, name=name, out_shape=shapes, in_specs=[_ANY] * n, out_specs=[_ANY] * n,
        scratch_shapes=[pltpu.SemaphoreType.DMA((len(jobs),)), pltpu.SemaphoreType.DMA((len(jobs),))])(*arrs)


def _add2(full, b, core, name):
    n, rows, w = b.shape
    tr = _pick(rows, 256, 16)
    nblk = rows // tr

    def body(c_ref, a_ref, b_ref, o_ref):
        o_ref[...] = (a_ref[...].astype(F32) + b_ref[...].astype(F32)).astype(BF16)

    spec = pl.BlockSpec((1, tr, w), lambda i, j, c_ref: (i, j, 0))
    grid_spec = pltpu.PrefetchScalarGridSpec(
        num_scalar_prefetch=1, grid=(n, nblk),
        in_specs=[pl.BlockSpec((1, tr, w), lambda i, j, c_ref: (i, c_ref[0] * nblk + j, 0)), spec], out_specs=spec)
    return pl.pallas_call(
        body, name=name, grid_spec=grid_spec, out_shape=jax.ShapeDtypeStruct(b.shape, BF16),
        compiler_params=_params(_PAR, _PAR))(core, full, b)


def _reduce_chips(partials, nchs, name):
    n = len(partials)
    jobs = []
    for a, (arr, k) in enumerate(zip(partials, nchs)):
        h = arr.shape[1]
        assert h % k == 0 and (h // k) % 16 == 0
        jobs += [(a, q * (h // k), h // k) for q in range(k)]

    def body(*refs):
        src, dst, (send_sems, recv_sems) = refs[:n], refs[n:2 * n], refs[2 * n:]
        x, y, c = _place()
        chips = [(1 - x, y), (x, 1 - y), (1 - x, 1 - y)]
        copies = [pltpu.make_async_remote_copy(
            src_ref=src[a].at[2 * px + py, pl.ds(r0, rows), :], dst_ref=dst[a].at[j, pl.ds(r0, rows), :],
            send_sem=send_sems.at[3 * k + j], recv_sem=recv_sems.at[3 * k + j],
            device_id=(px, py, c), device_id_type=MESH)
            for k, (a, r0, rows) in enumerate(jobs) for j, (px, py) in enumerate(chips)]
        for cp in copies:
            cp.start()
        for cp in copies:
            cp.wait()

    return pl.pallas_call(
        body, name=name,
        out_shape=[jax.ShapeDtypeStruct((3,) + p.shape[1:], p.dtype) for p in partials],
        in_specs=[_ANY] * n, out_specs=[_ANY] * n,
        scratch_shapes=[pltpu.SemaphoreType.DMA((3 * len(jobs),)), pltpu.SemaphoreType.DMA((3 * len(jobs),))])(*partials)


_HBM = pl.BlockSpec(memory_space=pltpu.HBM)
_SEM = pl.BlockSpec(memory_space=pltpu.SEMAPHORE)
_DATAFLOW = pltpu.SideEffectType.DATAFLOW_SIDE_EFFECTING


def _ici_jobs(srcs, nchs, kind):
    jobs = []
    for a, (arr, k) in enumerate(zip(srcs, nchs)):
        h = arr.shape[0] // 2 if kind == "gather" else arr.shape[1]
        assert h % k == 0 and (h // k) % 16 == 0
        jobs += [(a, h, q * (h // k), h // k) for q in range(k)]
    return jobs


def _ici_copies(src, land, send_sems, recv_sems, jobs, kind):
    x, y, c = _place()
    chips = [(1 - x, y), (x, 1 - y), (1 - x, 1 - y)]
    copies = []
    for k, (a, h, r0, rows) in enumerate(jobs):
        for j, (px, py) in enumerate(chips):
            if kind == "gather":
                at = pl.ds(pl.multiple_of(c * h + r0, 16), rows)
                s, d = src[a].at[at, :], land[a].at[2 * x + y, at, :]
            else:
                s, d = src[a].at[2 * px + py, pl.ds(r0, rows), :], land[a].at[j, pl.ds(r0, rows), :]
            copies.append(pltpu.make_async_remote_copy(
                src_ref=s, dst_ref=d, send_sem=send_sems.at[3 * k + j], recv_sem=recv_sems.at[3 * k + j],
                device_id=(px, py, c), device_id_type=MESH))
    return copies


def _ici_start(srcs, nchs, kind, name):
    n = len(srcs)
    jobs = _ici_jobs(srcs, nchs, kind)
    lead = (lambda s: (4,) + s.shape) if kind == "gather" else (lambda s: (3,) + s.shape[1:])
    lands = [lax.empty(lead(s), s.dtype) for s in srcs]

    def body(*refs):
        src, land = refs[:n], refs[n:2 * n]
        send_sems, recv_sems, token = refs[2 * n], refs[2 * n + 1], refs[-1]
        for cp in _ici_copies(src, land, send_sems, recv_sems, jobs, kind):
            cp.start()
        token[...] = jnp.zeros_like(token)

    hbm = [pltpu.HBM(a.shape, a.dtype) for a in srcs + lands]
    outs = pl.pallas_call(
        body, name=name,
        out_shape=[pltpu.SemaphoreType.DMA((3 * len(jobs),)), pltpu.SemaphoreType.DMA((3 * len(jobs),))] + hbm
        + [jax.ShapeDtypeStruct((8, 128), F32)],
        in_specs=[_HBM] * (2 * n), out_specs=[_SEM, _SEM] + [_HBM] * (2 * n) + [pl.BlockSpec(memory_space=pltpu.VMEM)],
        input_output_aliases={i: 2 + i for i in range(2 * n)},
        compiler_params=pltpu.CompilerParams(has_side_effects=_DATAFLOW),
    )(*[pltpu.with_memory_space_constraint(a, pltpu.HBM) for a in srcs + lands])
    return (outs[0], outs[1], list(outs[2:2 + n]), list(outs[2 + n:2 + 2 * n]), nchs, kind), outs[-1]


def _ici_wait(handle, after, name):
    send_sems, recv_sems, srcs, lands, nchs, kind = handle
    n = len(srcs)
    jobs = _ici_jobs(srcs, nchs, kind)

    def body(*refs):
        src, land = refs[:n], refs[n:2 * n]
        for cp in _ici_copies(src, land, refs[2 * n], refs[2 * n + 1], jobs, kind):
            cp.wait_send()
            cp.wait_recv()

    outs = pl.pallas_call(
        body, name=name, out_shape=[pltpu.HBM(a.shape, a.dtype) for a in srcs + lands],
        in_specs=[_HBM] * (2 * n) + [_SEM, _SEM, _ANY], out_specs=[_HBM] * (2 * n),
        input_output_aliases={i: i for i in range(2 * n)},
        compiler_params=pltpu.CompilerParams(has_side_effects=_DATAFLOW),
    )(*srcs, *lands, send_sems, recv_sems, after)
    return list(outs[:n]), list(outs[n:])


def _pass_to_sibling(gathered, nchs, name):
    n = len(gathered)
    jobs = _ici_jobs([jax.ShapeDtypeStruct(g.shape[1:], g.dtype) for g in gathered], nchs, "gather")

    def body(*refs):
        src, dst, (send_sems, recv_sems) = refs[:n], refs[n:2 * n], refs[2 * n:]
        x, y, c = _place()
        slots = [2 * (1 - x) + y, 2 * x + (1 - y), 2 * (1 - x) + (1 - y)]

        def copy(k, j, pc):
            a, h, r0, rows = jobs[k]
            at = pl.ds(pl.multiple_of(pc * h + r0, 16), rows)
            return pltpu.make_async_remote_copy(
                src_ref=src[a].at[slots[j], at, :], dst_ref=dst[a].at[slots[j], at, :], send_sem=send_sems.at[3 * k + j],
                recv_sem=recv_sems.at[3 * k + j], device_id=(x, y, 1 - c), device_id_type=MESH)

        pairs = [(k, j) for k in range(len(jobs)) for j in range(3)]
        for k, j in pairs:
            copy(k, j, c).start()
        for k, j in pairs:
            copy(k, j, c).wait_send()
            copy(k, j, 1 - c).wait_recv()

    return pl.pallas_call(
        body, name=name, out_shape=[jax.ShapeDtypeStruct(g.shape, g.dtype) for g in gathered],
        in_specs=[_ANY] * n, out_specs=[_ANY] * n, input_output_aliases={i: i for i in range(n)},
        scratch_shapes=[pltpu.SemaphoreType.DMA((3 * len(jobs),)), pltpu.SemaphoreType.DMA((3 * len(jobs),))])(*gathered)


def _add4(own, got, name):
    rows, w = own.shape
    tr = _pick(rows, 128, 16)

    def body(a_ref, b_ref, o_ref):
        o_ref[...] = ((a_ref[...].astype(F32) + b_ref[0].astype(F32)) + b_ref[1].astype(F32)) + b_ref[2].astype(F32)

    return pl.pallas_call(
        body, name=name, grid=(rows // tr,),
        in_specs=[pl.BlockSpec((tr, w), lambda i: (i, 0)), pl.BlockSpec((3, tr, w), lambda i: (0, i, 0))],
        out_specs=pl.BlockSpec((tr, w), lambda i: (i, 0)), out_shape=jax.ShapeDtypeStruct((rows, w), F32),
        compiler_params=_params(_PAR))(own, got)


def _small_sync(gs, ws, ms, vs):
    rows = gs.shape[0]
    vmem = pl.BlockSpec(memory_space=pltpu.VMEM)

    def body(g_ref, w_ref, m_ref, v_ref, sum_ref, d_ref, m2_ref, v2_ref, buf, send_sems, recv_sems):
        x, y, c = _place()
        me = 4 * x + 2 * y + c
        buf[me] = g_ref[...]
        copies = []
        for k in range(1, 8):
            peer = (x ^ (k >> 2), y ^ ((k >> 1) & 1), c ^ (k & 1))
            copies.append(pltpu.make_async_remote_copy(
                src_ref=g_ref, dst_ref=buf.at[me], send_sem=send_sems.at[k - 1], recv_sem=recv_sems.at[k - 1],
                device_id=peer, device_id_type=MESH))
        for cp in copies:
            cp.start()
        for cp in copies:
            cp.wait()
        total = buf[0]
        for i in range(1, 8):
            total = total + buf[i]
        sum_ref[...] = total
        d, m2, v2 = _adam_math(w_ref[...], total, m_ref[...], v_ref[...])
        d_ref[...] = d
        m2_ref[...] = m2
        v2_ref[...] = v2

    shape = jax.ShapeDtypeStruct((rows, 128), F32)
    return pl.pallas_call(
        body, name="small_sync", out_shape=[shape] * 4, in_specs=[vmem] * 4, out_specs=[vmem] * 4,
        scratch_shapes=[pltpu.VMEM((8, rows, 128), F32), pltpu.SemaphoreType.DMA((7,)),
                        pltpu.SemaphoreType.DMA((7,))])(gs, ws, ms, vs)


_GROUPS = {
    "ffn1": dict(cols=("ffn1_w_in", 1408), rows=(("ffn1_w_out", 704, 704),), chunks=(8, 2)),
    "ffn2": dict(cols=("ffn2_w_in", 1408), rows=(("ffn2_w_out", 704, 704),), chunks=(8, 2)),
    "mixer": dict(cols=("w_in", 3080), chunks=(8, 4),
                  rows=(("w_branch_hgrn", 256, 256), ("w_branch_gdn", 512, 512), ("w_out", 256, 256),
                        ("gdn_conv_w", CONV_K, 128))),
}
_BIG_NAMES = tuple(n for g in _GROUPS.values() for n in (g["cols"][0],) + tuple(r[0] for r in g["rows"]))


def _group_names(group):
    return (group["cols"][0],) + tuple(r[0] for r in group["rows"])


def _pack(parts, lead, group):
    ax = len(lead)
    rows = []
    for n, r, padded in group["rows"]:
        p = parts[n]
        if padded != r:
            p = jnp.tile(p, (1,) * ax + (padded // r, 1))
        rows.append(p)
    return [parts[group["cols"][0]], rows[0] if len(rows) == 1 else jnp.concatenate(rows, axis=ax)]


def _unpack(cols, rows, group):
    out, off = {group["cols"][0]: cols}, 0
    for n, r, padded in group["rows"]:
        out[n] = rows[..., off:off + r, :]
        off += padded
    return out


def _is_col_sharded(name):
    return name in ("ffn1_w_in", "ffn2_w_in", "w_in", "gdn_conv_w")


def _full_from_shards(name, g):
    if _is_col_sharded(name):
        return jnp.transpose(g, (1, 0, 2)).reshape(g.shape[1], -1)
    return g.reshape(-1, g.shape[2])


def _shards_from_full(name, full):
    if _is_col_sharded(name):
        return jnp.transpose(full.reshape(full.shape[0], 4, -1), (1, 0, 2))
    return full.reshape(4, -1, full.shape[1])


_SMALL = (("ffn1_norm", 8), ("mix_norm", 8), ("hgrn_lb_logits", 16), ("hgrn_out_norm", 8), ("gdn_a_log", 8),
          ("gdn_dt_bias", 8), ("gdn_out_norm", 8), ("ffn2_norm", 8), ("final_norm", 8), ("loss", 8))
_SMALL_ROWS = sum(r for _, r in _SMALL)


def _pack_small(parts):
    out = []
    for name, rows in _SMALL:
        p = parts[name].reshape(-1).astype(F32)
        if p.shape[0] <= 128:
            if p.shape[0] < 128:
                p = jnp.concatenate([p, jnp.zeros((128 - p.shape[0],), F32)])
            p = jnp.broadcast_to(p.reshape(1, 128), (rows, 128))
        out.append(p.reshape(rows, 128))
    return jnp.concatenate(out, axis=0)


def _unpack_small(packed, shapes):
    out, off = {}, 0
    for name, rows in _SMALL:
        n = int(np.prod(shapes[name]))
        out[name] = packed[off:off + rows].reshape(-1)[:n].reshape(shapes[name])
        off += rows
    return out


def _ffn_fwd(x, gain, w_in, w_out, tag):
    n = _rmsnorm_fwd(x, gain, tag + "_norm")
    a, b, hm = _ffn_in_act(n, w_in, tag + "_in")
    out = _mm(hm, w_out, alpha=0.5, res=x, name=tag + "_out")
    return out, (n, a, b)


def _ffn_bwd(x, gain, w_in, w_out, saved, dout, dout_bf, tag):
    n, a, b = saved
    da, db, hm = _ffn_dact(dout_bf, w_out, a, b, tag + "_dact")
    dw_out = _mm(hm, dout_bf, ta=True, alpha=0.5, out_dtype=BF16, name=tag + "_dwout")
    dw_in = jnp.concatenate([_mm(n, da, ta=True, out_dtype=BF16, name=tag + "_dwin_a"),
                             _mm(n, db, ta=True, out_dtype=BF16, name=tag + "_dwin_b")], axis=1)
    dn = _mm(da, w_in[:, :D_FF], tb=True, name=tag + "_dnorm_a")
    dn = _mm(db, w_in[:, D_FF:], tb=True, res=dn, name=tag + "_dnorm_b")
    dx, dx_bf, dgain = _rmsnorm_bwd(x, gain, dn, dout, tag + "_dx")
    return dx, dx_bf, dgain, dw_in, dw_out


def _pad_lanes(v):
    return jnp.concatenate([v.reshape(1, -1), jnp.zeros((1, HEAD - v.size), F32)], axis=1)


def _local_step(x, tgt, small, exchange):
    hg_c = _hg_consts()
    gd_c = _gd_consts()
    alog = _pad_lanes(small["gdn_a_log"])
    dtb = _pad_lanes(small["gdn_dt_bias"])
    logits = small["hgrn_lb_logits"]
    hg_gain = small["hgrn_out_norm"].reshape(1, HEAD)
    gd_gain = small["gdn_out_norm"].reshape(1, HEAD)
    g1, gm, g2 = small["ffn1_norm"].reshape(1, -1), small["mix_norm"].reshape(1, -1), small["ffn2_norm"].reshape(1, -1)
    gf = small["final_norm"].reshape(1, -1)
    qscale = HEAD ** -0.5

    w1 = exchange.weights("ffn1")
    started = exchange.prefetch("mixer")
    h1, ffn1_saved = _ffn_fwd(x, g1 + started, w1["ffn1_w_in"], w1["ffn1_w_out"], "ffn1")
    u = _rmsnorm_fwd(h1, gm, "mix_norm")
    w = exchange.weights("mixer", after=u)
    started = exchange.prefetch("ffn2")
    seg, off = {}, 0
    for name, size in zip(IN_NAMES, IN_SIZES):
        seg[name] = w["w_in"][:, off:off + size]
        off += size
    w_gab = jnp.concatenate([seg["ga"], seg["gb"], jnp.zeros((D_MODEL, HEAD - 32), BF16)], axis=1)
    big_segs = [n for n in IN_NAMES if n not in ("ga", "gb")]
    conv8 = jnp.concatenate([w["gdn_conv_w"].astype(F32), jnp.zeros((8 - CONV_K, 4096), F32)], axis=0)
    conv_q, conv_k, conv_v = conv8[:, :1024], conv8[:, 1024:2048], conv8[:, 2048:]
    w_main = jnp.concatenate([seg[n] for n in big_segs], axis=1)
    proj = _mm(u, w_main, name="proj")
    pr, off = {}, 0
    for n in big_segs:
        pr[n] = _view(proj, off, seg[n].shape[1])
        off += seg[n].shape[1]
    gab = _mm(u, w_gab, name="proj_gab")
    oh_raw, oh, s_h = _hgrn_fwd(pr["hq"], pr["hf"], pr["hi"], pr["hg"], logits, hg_gain + started, hg_c)
    qn = _conv_fwd(pr["gq"], conv_q, qscale, "conv_q")
    kn = _conv_fwd(pr["gk"], conv_k, 1.0, "conv_k")
    cv = _conv_fwd(pr["gv"], conv_v, None, "conv_v")
    og_raw, og, s_g = _gdn_fwd(qn, kn, cv, gab, pr["gz"], alog, dtb, gd_gain, gd_c)
    yh = _mm(oh, w["w_branch_hgrn"], name="branch_h")
    yg = _mm(og, w["w_branch_gdn"], name="branch_g")
    ym = _merge_fwd(yh, yg, pr["gate_h"], pr["gate_g"])
    h2 = _mm(ym, w["w_out"], res=h1, name="mix_out")
    w2 = exchange.weights("ffn2", after=h2)
    h3, ffn2_saved = _ffn_fwd(h2, g2, w2["ffn2_w_in"], w2["ffn2_w_out"], "ffn2")
    loss, dh3, dh3_bf, d_gf = _final_loss(h3, gf, tgt)

    dh2, dh2_bf, d_g2, d_f2in, d_f2out = _ffn_bwd(h2, g2, w2["ffn2_w_in"], w2["ffn2_w_out"], ffn2_saved, dh3, dh3_bf,
                                                  "ffn2")
    started = exchange.reduce("ffn2", {"ffn2_w_in": d_f2in, "ffn2_w_out": d_f2out}, behind=True)
    dym =_mm(dh2_bf, w["w_out"], tb=True, name="d_merge")
    d_wout = _mm(ym, dh2_bf, ta=True, out_dtype=BF16, name="d_w_out")
    dyh, dyg, d_gate_h, d_gate_g = _merge_bwd(dym, yh, yg, pr["gate_h"], pr["gate_g"])
    d_wbh = _mm(oh, dyh, ta=True, out_dtype=BF16, name="d_w_branch_h")
    d_wbg = _mm(og, dyg, ta=True, out_dtype=BF16, name="d_w_branch_g")
    doh = _mm(dyh, w["w_branch_hgrn"], tb=True, name="d_oh")
    dog = _mm(dyg, w["w_branch_gdn"], tb=True, name="d_og")
    d_hq, d_hf, d_hi, d_hg, d_hg_gain, d_lb0 = _hgrn_bwd(pr["hq"], pr["hf"], pr["hi"], pr["hg"], logits,
                                                        hg_gain + started, oh_raw, s_h, doh, hg_c)
    d_qn, d_kn, d_cv, d_gab_wide, d_gz, gd_small = _gdn_bwd(qn, kn, cv, gab, pr["gz"], alog, dtb, gd_gain, og_raw,
                                                            s_g, dog, gd_c)
    d_gab = _fold_groups(d_gab_wide)
    dc_q, dwc_q = _conv_bwd_a(pr["gq"], conv_q, d_qn, qscale, "dconv_q")
    dc_k, dwc_k = _conv_bwd_a(pr["gk"], conv_k, d_kn, 1.0, "dconv_k")
    dc_v, dwc_v = _conv_bwd_a(pr["gv"], conv_v, d_cv, None, "dconv_v")
    d_gq = _conv_bwd_b(dc_q, conv_q, "dconvx_q")
    d_gk = _conv_bwd_b(dc_k, conv_k, "dconvx_k")
    d_gv = _conv_bwd_b(dc_v, conv_v, "dconvx_v")
    dpr = {"hq": d_hq, "hf": d_hf, "hi": d_hi, "hg": d_hg, "gq": d_gq, "gk": d_gk, "gv": d_gv, "gz": d_gz,
           "gate_h": d_gate_h, "gate_g": d_gate_g}
    dproj = jnp.concatenate([dpr[n] for n in big_segs], axis=1)
    du = _mm(d_gab, w_gab, tb=True, name="du_gab")
    du = _mm(dproj, w_main, tb=True, res=du, name="du")
    d_wmain = _mm(u, dproj, ta=True, out_dtype=BF16, name="dw_main")
    d_wgab = _mm(u, d_gab, ta=True, out_dtype=BF16, name="dw_gab")
    d_win = jnp.concatenate([d_wmain[:, :8192], d_wgab[:, :32], d_wmain[:, 8192:]], axis=1)
    d_conv = jnp.concatenate([dwc_q[:CONV_K], dwc_k[:CONV_K], dwc_v[:CONV_K]], axis=1).astype(BF16)
    started = exchange.reduce("mixer", {"w_in": d_win, "gdn_conv_w": d_conv, "w_branch_hgrn": d_wbh,
                                        "w_branch_gdn": d_wbg, "w_out": d_wout}, behind=True)
    dh1, dh1_bf, d_gm = _rmsnorm_bwd(h1, gm + started, du, dh2, "mix_dnorm")
    dx, _, d_g1, d_f1in, d_f1out = _ffn_bwd(x, g1, w1["ffn1_w_in"], w1["ffn1_w_out"], ffn1_saved, dh1, dh1_bf, "ffn1")
    exchange.reduce("ffn1", {"ffn1_w_in": d_f1in, "ffn1_w_out": d_f1out})
    d_lb0 = d_lb0.reshape(1, -1)
    sm = {"ffn1_norm": d_g1, "mix_norm": d_gm, "hgrn_lb_logits": jnp.concatenate([d_lb0, -d_lb0], axis=0),
          "hgrn_out_norm": d_hg_gain, "gdn_a_log": gd_small[2, :16], "gdn_dt_bias": gd_small[1, :16],
          "gdn_out_norm": gd_small[0], "ffn2_norm": d_g2, "final_norm": d_gf, "loss": loss[0, :1]}
    return dx, sm


class _Exchange:
    def __init__(self, wts):
        self.wts = wts
        xi, yi, ci = _place()
        self.chip = 2 * xi + yi
        self.south = ci == 0
        self.core = ci.reshape(1).astype(jnp.int32)
        self.mine = {}
        self.coming = {}
        self.going = {}

    def _packs(self, tag):
        group = _GROUPS[tag]
        return _pack({n: self.wts[n][0].astype(BF16) for n in _group_names(group)}, (), group)

    def prefetch(self, tag):
        packs = self._packs(tag)
        handle, token = _ici_start(packs, _GROUPS[tag]["chunks"], "gather", "gather_start_" + tag)
        self.coming[tag] = handle
        return token[0:1, 0:1]

    def weights(self, tag, after=None):
        group = _GROUPS[tag]
        if tag in self.coming:
            packs, halves = _ici_wait(self.coming.pop(tag), after, "gather_wait_" + tag)
            others = _pass_to_sibling(halves, group["chunks"], "gather_pass_" + tag)
        else:
            packs = self._packs(tag)
            others = _gather_weights(packs, group["chunks"], "gather_" + tag)
        whole = [lax.dynamic_update_index_in_dim(g, p, self.chip, 0) for g, p in zip(others, packs)]
        gathered = _unpack(*whole, group)
        return {n: _full_from_shards(n, gathered[n]) for n in _group_names(group)}

    def reduce(self, tag, grads, behind=False):
        group = _GROUPS[tag]
        gpacks = _pack({n: _shards_from_full(n, grads[n]) for n in _group_names(group)}, (4,), group)
        got = _swap_with_sibling(gpacks, group["chunks"], 4, "reduce_pair_" + tag, halves=True)
        sums = [_add2(a, b, self.core, "add_pair_%s_%d" % (tag, i)) for i, (a, b) in enumerate(zip(gpacks, got))]
        if behind:
            handle, token = _ici_start(sums, group["chunks"], "reduce", "reduce_start_" + tag)
            self.going[tag] = handle
            return token[0:1, 0:1]
        self._add_chips(tag, sums, _reduce_chips(sums, group["chunks"], "reduce_chips_" + tag))
        return None

    def _add_chips(self, tag, sums, from_chips):
        self.mine[tag] = [_add4(lax.dynamic_index_in_dim(s, self.chip, axis=0, keepdims=False), f,
                                "add_chips_%s_%d" % (tag, i)) for i, (s, f) in enumerate(zip(sums, from_chips))]

    def finish(self, after):
        for tag in list(self.going):
            self._add_chips(tag, *_ici_wait(self.going.pop(tag), after, "reduce_wait_" + tag))
        tags = list(self.mine)
        mine = [a for t in tags for a in self.mine[t]]
        nchs = [k for t in tags for k in _GROUPS[t]["chunks"]]
        theirs = _swap_with_sibling(mine, nchs, 0, "share_pair")
        whole = [jnp.concatenate([jnp.where(self.south, a, b), jnp.where(self.south, b, a)], axis=0)
                 for a, b in zip(mine, theirs)]
        reduced = {}
        for i, t in enumerate(tags):
            reduced.update(_unpack(whole[2 * i], whole[2 * i + 1], _GROUPS[t]))
        return reduced


_WEIGHTS = ("ffn1_norm", "ffn1_w_in", "ffn1_w_out", "mix_norm", "w_in", "hgrn_lb_logits", "hgrn_out_norm",
            "gdn_conv_w", "gdn_a_log", "gdn_dt_bias", "gdn_out_norm", "w_branch_hgrn", "w_branch_gdn", "w_out",
            "ffn2_norm", "ffn2_w_in", "ffn2_w_out", "final_norm")


def kernel(x, ffn1_norm, ffn1_w_in, ffn1_w_out, mix_norm, w_in, hgrn_lb_logits, hgrn_out_norm, gdn_conv_w, gdn_a_log, gdn_dt_bias, gdn_out_norm, w_branch_hgrn, w_branch_gdn, w_out, ffn2_norm, ffn2_w_in, ffn2_w_out, final_norm, loss_target, m_ffn1_norm, m_ffn1_w_in, m_ffn1_w_out, m_mix_norm, m_w_in, m_hgrn_lb_logits, m_hgrn_out_norm, m_gdn_conv_w, m_gdn_a_log, m_gdn_dt_bias, m_gdn_out_norm, m_w_branch_hgrn, m_w_branch_gdn, m_w_out, m_ffn2_norm, m_ffn2_w_in, m_ffn2_w_out, m_final_norm, v_ffn1_norm, v_ffn1_w_in, v_ffn1_w_out, v_mix_norm, v_w_in, v_hgrn_lb_logits, v_hgrn_out_norm, v_gdn_conv_w, v_gdn_a_log, v_gdn_dt_bias, v_gdn_out_norm, v_w_branch_hgrn, v_w_branch_gdn, v_w_out, v_ffn2_norm, v_ffn2_w_in, v_ffn2_w_out, v_final_norm):
    args = dict(locals())
    wts = {n: args[n] for n in _WEIGHTS}
    moms = {n: args["m_" + n] for n in _WEIGHTS}
    vars_ = {n: args["v_" + n] for n in _WEIGHTS}

    small = {n: wts[n].astype(F32) for n in _WEIGHTS if n not in _BIG_NAMES}
    exchange = _Exchange(wts)
    dx, small_grads = _local_step(x[0], loss_target[0], small, exchange)
    reduced = exchange.finish(after=dx)

    out_g, out_d, out_m, out_v = {}, {}, {}, {}
    for n in _BIG_NAMES:
        shape = wts[n].shape
        w2 = wts[n].reshape(shape[-2], shape[-1])
        g2 = reduced[n]
        d, m2, v2 = _adamw(w2, g2, moms[n].reshape(w2.shape), vars_[n].reshape(w2.shape), "adamw_" + n)
        out_g[n], out_d[n], out_m[n], out_v[n] = g2.reshape(shape), d.reshape(shape), m2.reshape(shape), v2.reshape(shape)

    small_names = [n for n, _ in _SMALL]
    zero = jnp.zeros((1,), F32)
    shapes = {n: (wts[n].shape if n != "loss" else (1,)) for n in small_names}
    sums, sd, sm_, sv = _small_sync(
        _pack_small(small_grads),
        _pack_small({n: (wts[n] if n != "loss" else zero) for n in small_names}),
        _pack_small({n: (moms[n] if n != "loss" else zero) for n in small_names}),
        _pack_small({n: (vars_[n] if n != "loss" else zero) for n in small_names}))
    sg_u, sd_u, sm_u, sv_u = (_unpack_small(p, shapes) for p in (sums, sd, sm_, sv))
    for n in small_names:
        if n != "loss":
            out_g[n], out_d[n], out_m[n], out_v[n] = sg_u[n], sd_u[n], sm_u[n], sv_u[n]
    loss = sg_u["loss"].reshape(())

    return (loss, dx[None], *[out_g[n] for n in _WEIGHTS], *[out_d[n] for n in _WEIGHTS],
            *[out_m[n] for n in _WEIGHTS], *[out_v[n] for n in _WEIGHTS])
```

```python
import numpy as np

import jax
import jax.numpy as jnp
from jax import lax
from jax.experimental import pallas as pl
from jax.experimental.pallas import tpu as pltpu

F32 = jnp.float32
BF16 = jnp.bfloat16

D_MODEL = 1024
D_FF = 2816
CHUNK = 64
HEAD = 128
HG_HEADS = 8
GD_HEADS = 16
HPS = 8
COMM_CHUNKS = 9
MM_TM = 1408
MM_TN = 512
MM_TK = 1536
VMEM_LIMIT = 48 * 1024 * 1024
EPS = 1e-6
CONV_K = 4
IN_NAMES = ("hq", "hf", "hi", "hg", "gq", "gk", "gv", "ga", "gb", "gz", "gate_h", "gate_g")
IN_SIZES = (1024, 1024, 1024, 1024, 1024, 1024, 2048, 16, 16, 2048, 1024, 1024)
IN_WIDTH = sum(IN_SIZES)

ADAM_LR = 0.001
ADAM_B1 = 0.9
ADAM_B2 = 0.999
ADAM_EPS = 1e-08
ADAM_WD = 0.01
ADAM_STEP = 10

MESH = pl.DeviceIdType.MESH
_ARB = "arbitrary"
_PAR = "parallel"


def _bf(x):
    return x.astype(BF16)


def _dot(a, b):
    return jnp.dot(_bf(a), _bf(b), preferred_element_type=F32)


def _dot_nt(a, b):
    return lax.dot_general(_bf(a), _bf(b), (((1,), (1,)), ((), ())), preferred_element_type=F32)


def _dot_tn(a, b):
    return lax.dot_general(_bf(a), _bf(b), (((0,), (0,)), ((), ())), preferred_element_type=F32)


def _split3(x):
    hi = _bf(x)
    r = x - hi.astype(F32)
    mid = _bf(r)
    lo = _bf(r - mid.astype(F32))
    return hi, mid, lo


def _dot_mx(m, x):
    hi, mid, lo = _split3(x)
    return (jnp.dot(m, hi, preferred_element_type=F32) + jnp.dot(m, mid, preferred_element_type=F32)
            + jnp.dot(m, lo, preferred_element_type=F32))


def _dot_xm(x, m):
    hi, mid, lo = _split3(x)
    return (jnp.dot(hi, m, preferred_element_type=F32) + jnp.dot(mid, m, preferred_element_type=F32)
            + jnp.dot(lo, m, preferred_element_type=F32))


def _dot_hp(a, b):
    ah = _bf(a)
    al = _bf(a - ah.astype(F32))
    bh = _bf(b)
    bl = _bf(b - bh.astype(F32))
    return (jnp.dot(ah, bh, preferred_element_type=F32) + jnp.dot(ah, bl, preferred_element_type=F32)
            + jnp.dot(al, bh, preferred_element_type=F32))


def _sigmoid(x):
    return jax.nn.sigmoid(x)


def _silu(x):
    return x * _sigmoid(x)


def _dsilu(x):
    s = _sigmoid(x)
    return s * (1.0 + x * (1.0 - s))


def _softplus(x):
    return jnp.maximum(x, 0.0) + jnp.log(1.0 + jnp.exp(-jnp.abs(x)))


def _rowsum(x):
    return jnp.sum(x, axis=1, keepdims=True)


def _col_to_row(col, eye):
    return jnp.sum(eye * col, axis=0, keepdims=True)


def _row_to_col(row, eye):
    return jnp.sum(eye * row, axis=1, keepdims=True)


def _pick(dim, pref, unit=128):
    if dim <= pref:
        return dim
    t = pref
    while t >= unit:
        if dim % t == 0:
            return t
        t -= unit
    return dim


def _params(*sem):
    return pltpu.CompilerParams(dimension_semantics=tuple(sem), vmem_limit_bytes=VMEM_LIMIT)


def _mm(a, b, *, ta=False, tb=False, alpha=1.0, res=None, out_dtype=F32, name="mm"):
    m = a.shape[1] if ta else a.shape[0]
    k = a.shape[0] if ta else a.shape[1]
    n = b.shape[0] if tb else b.shape[1]
    assert k == (b.shape[1] if tb else b.shape[0])
    tm, tn, tk = _pick(m, MM_TM), _pick(n, MM_TN), _pick(k, MM_TK)
    if tn < MM_TN < n and n % MM_TM == 0:
        tn = MM_TM
    nk = k // tk
    a_spec = pl.BlockSpec((tk, tm), lambda i, j, l: (l, i)) if ta else pl.BlockSpec((tm, tk), lambda i, j, l: (i, l))
    b_spec = pl.BlockSpec((tn, tk), lambda i, j, l: (j, l)) if tb else pl.BlockSpec((tk, tn), lambda i, j, l: (l, j))
    o_spec = pl.BlockSpec((tm, tn), lambda i, j, l: (i, j))
    dims = (((0 if ta else 1,), (1 if tb else 0,)), ((), ()))
    has_res = res is not None

    def finish(r, r_ref, o_ref):
        if alpha != 1.0:
            r = r * alpha
        if has_res:
            r = r + r_ref[...]
        o_ref[...] = r.astype(out_dtype)

    def body(*refs):
        a_ref, b_ref = refs[0], refs[1]
        r_ref = refs[2] if has_res else None
        o_ref = refs[3] if has_res else refs[2]
        part = lax.dot_general(_bf(a_ref[...]), _bf(b_ref[...]), dims, preferred_element_type=F32)
        if nk == 1:
            finish(part, r_ref, o_ref)
            return
        acc = refs[-1]
        step = pl.program_id(2)

        @pl.when(step == 0)
        def _():
            acc[...] = part

        @pl.when(step != 0)
        def _():
            acc[...] += part

        @pl.when(step == nk - 1)
        def _():
            finish(acc[...], r_ref, o_ref)

    ins = [a, b] + ([res] if has_res else [])
    in_specs = [a_spec, b_spec] + ([o_spec] if has_res else [])
    return pl.pallas_call(
        body, name=name, grid=(m // tm, n // tn, nk), in_specs=in_specs, out_specs=o_spec,
        out_shape=jax.ShapeDtypeStruct((m, n), out_dtype),
        scratch_shapes=[pltpu.VMEM((tm, tn), F32)] if nk > 1 else [],
        compiler_params=_params(_PAR, _PAR, _ARB))(*ins)


def _row_spec(tr, w):
    return pl.BlockSpec((tr, w), lambda i: (i, 0))


def _full_spec(shape):
    return pl.BlockSpec(shape, lambda i: tuple(0 for _ in shape))


def _view(arr, off, width):
    return arr, off, width


def _view_rows(view, tr):
    _, off, width = view
    assert off % width == 0
    return pl.BlockSpec((tr, width), lambda i: (i, off // width))


def _view_tile(view, rows, bw, cidx=lambda c: c):
    _, off, width = view
    assert off % bw == 0 and width % bw == 0
    return pl.BlockSpec((rows, bw), lambda c, g: (cidx(c), off // bw + g))


def _rmsnorm_fwd(x, g, name):
    t, d = x.shape
    tr = _pick(t, 256, 8)

    def body(x_ref, g_ref, o_ref):
        xv = x_ref[...]
        r = lax.rsqrt(jnp.mean(xv * xv, axis=1, keepdims=True) + EPS)
        o_ref[...] = (xv * r * g_ref[...]).astype(BF16)

    return pl.pallas_call(
        body, name=name, grid=(t // tr,), in_specs=[_row_spec(tr, d), _full_spec((1, d))],
        out_specs=_row_spec(tr, d), out_shape=jax.ShapeDtypeStruct((t, d), BF16),
        compiler_params=_params(_PAR))(x, g)


def _rmsnorm_bwd(x, g, dn, res, name):
    t, d = x.shape
    tr = _pick(t, 256, 8)

    def body(x_ref, g_ref, dn_ref, r_ref, dx_ref, dxb_ref, dg_ref):
        @pl.when(pl.program_id(0) == 0)
        def _():
            dg_ref[...] = jnp.zeros_like(dg_ref)

        xv = x_ref[...]
        r = lax.rsqrt(jnp.mean(xv * xv, axis=1, keepdims=True) + EPS)
        xh = xv * r
        dy = dn_ref[...]
        dg_ref[...] += jnp.sum(dy * xh, axis=0, keepdims=True)
        dxh = dy * g_ref[...]
        dx = r_ref[...] + r * (dxh - xh * jnp.mean(dxh * xh, axis=1, keepdims=True))
        dx_ref[...] = dx
        dxb_ref[...] = dx.astype(BF16)

    return pl.pallas_call(
        body, name=name, grid=(t // tr,),
        in_specs=[_row_spec(tr, d), _full_spec((1, d)), _row_spec(tr, d), _row_spec(tr, d)],
        out_specs=[_row_spec(tr, d), _row_spec(tr, d), _full_spec((1, d))],
        out_shape=[jax.ShapeDtypeStruct((t, d), F32), jax.ShapeDtypeStruct((t, d), BF16),
                   jax.ShapeDtypeStruct((1, d), F32)],
        compiler_params=_params(_ARB))(x, g, dn, res)


FFN_TN = 256


def _ffn_in_act(n, w_in, name):
    t, d = n.shape
    tm = _pick(t, MM_TM)
    nf = D_FF // FFN_TN

    def body(n_ref, wa_ref, wb_ref, a_ref, b_ref, hm_ref):
        nv = n_ref[...]
        a = jnp.dot(nv, wa_ref[...], preferred_element_type=F32)
        b = jnp.dot(nv, wb_ref[...], preferred_element_type=F32)
        a_ref[...] = a.astype(BF16)
        b_ref[...] = b.astype(BF16)
        hm_ref[...] = (_silu(a) * b).astype(BF16)

    tile = pl.BlockSpec((tm, FFN_TN), lambda i, j: (i, j))
    return pl.pallas_call(
        body, name=name, grid=(t // tm, nf),
        in_specs=[pl.BlockSpec((tm, d), lambda i, j: (i, 0)), pl.BlockSpec((d, FFN_TN), lambda i, j: (0, j)),
                  pl.BlockSpec((d, FFN_TN), lambda i, j: (0, nf + j))],
        out_specs=[tile, tile, tile], out_shape=[jax.ShapeDtypeStruct((t, D_FF), BF16)] * 3,
        compiler_params=_params(_PAR, _PAR))(n, w_in, w_in)


def _ffn_dact(dout, w_out, a, b, name):
    t, d = dout.shape
    tm = _pick(t, MM_TM)

    def body(do_ref, w_ref, a_ref, b_ref, da_ref, db_ref, hm_ref):
        dh = 0.5 * _dot_nt(do_ref[...], w_ref[...])
        av = a_ref[...].astype(F32)
        bv = b_ref[...].astype(F32)
        sa = _silu(av)
        da_ref[...] = (dh * bv * _dsilu(av)).astype(BF16)
        db_ref[...] = (dh * sa).astype(BF16)
        hm_ref[...] = (sa * bv).astype(BF16)

    tile = pl.BlockSpec((tm, FFN_TN), lambda i, j: (i, j))
    return pl.pallas_call(
        body, name=name, grid=(t // tm, D_FF // FFN_TN),
        in_specs=[pl.BlockSpec((tm, d), lambda i, j: (i, 0)), pl.BlockSpec((FFN_TN, d), lambda i, j: (j, 0)), tile, tile],
        out_specs=[tile, tile, tile], out_shape=[jax.ShapeDtypeStruct((t, D_FF), BF16)] * 3,
        compiler_params=_params(_PAR, _PAR))(dout, w_out, a, b)


def _merge_fwd(yh, yg, gh, gg):
    t, d = yh.shape
    tr = _pick(t, 256, 8)

    def body(yh_ref, yg_ref, gh_ref, gg_ref, o_ref):
        o_ref[...] = (_sigmoid(gh_ref[...]) * yh_ref[...] + _sigmoid(gg_ref[...]) * yg_ref[...]).astype(BF16)

    return pl.pallas_call(
        body, name="merge_fwd", grid=(t // tr,),
        in_specs=[_row_spec(tr, d), _row_spec(tr, d), _view_rows(gh, tr), _view_rows(gg, tr)],
        out_specs=_row_spec(tr, d),
        out_shape=jax.ShapeDtypeStruct((t, d), BF16), compiler_params=_params(_PAR))(yh, yg, gh[0], gg[0])


def _merge_bwd(dy, yh, yg, gh, gg):
    t, d = yh.shape
    tr = _pick(t, 256, 8)

    def body(dy_ref, yh_ref, yg_ref, gh_ref, gg_ref, dyh_ref, dyg_ref, dgh_ref, dgg_ref):
        dyv = dy_ref[...]
        sh = _sigmoid(gh_ref[...])
        sg = _sigmoid(gg_ref[...])
        dyh_ref[...] = (dyv * sh).astype(BF16)
        dyg_ref[...] = (dyv * sg).astype(BF16)
        dgh_ref[...] = (dyv * yh_ref[...] * sh * (1.0 - sh)).astype(BF16)
        dgg_ref[...] = (dyv * yg_ref[...] * sg * (1.0 - sg)).astype(BF16)

    return pl.pallas_call(
        body, name="merge_bwd", grid=(t // tr,),
        in_specs=[_row_spec(tr, d)] * 3 + [_view_rows(gh, tr), _view_rows(gg, tr)], out_specs=[_row_spec(tr, d)] * 4,
        out_shape=[jax.ShapeDtypeStruct((t, d), BF16)] * 4,
        compiler_params=_params(_PAR))(dy, yh, yg, gh[0], gg[0])


def _final_loss(h, g, tgt):
    t, d = h.shape
    tr = _pick(t, 256, 8)

    def body(h_ref, g_ref, t_ref, loss_ref, dh_ref, dhb_ref, dg_ref):
        @pl.when(pl.program_id(0) == 0)
        def _():
            dg_ref[...] = jnp.zeros_like(dg_ref)
            loss_ref[...] = jnp.zeros_like(loss_ref)

        xv = h_ref[...]
        gv = g_ref[...]
        r = lax.rsqrt(jnp.mean(xv * xv, axis=1, keepdims=True) + EPS)
        xh = xv * r
        err = xh * gv - t_ref[...]
        loss_ref[...] += 0.5 * jnp.sum(jnp.mean(err * err, axis=1, keepdims=True), axis=0, keepdims=True)
        dy = err * (1.0 / d)
        dg_ref[...] += jnp.sum(dy * xh, axis=0, keepdims=True)
        dxh = dy * gv
        dh = r * (dxh - xh * jnp.mean(dxh * xh, axis=1, keepdims=True))
        dh_ref[...] = dh
        dhb_ref[...] = dh.astype(BF16)

    return pl.pallas_call(
        body, name="final_loss", grid=(t // tr,),
        in_specs=[_row_spec(tr, d), _full_spec((1, d)), _row_spec(tr, d)],
        out_specs=[_full_spec((1, 128)), _row_spec(tr, d), _row_spec(tr, d), _full_spec((1, d))],
        out_shape=[jax.ShapeDtypeStruct((1, 128), F32), jax.ShapeDtypeStruct((t, d), F32),
                   jax.ShapeDtypeStruct((t, d), BF16), jax.ShapeDtypeStruct((1, d), F32)],
        compiler_params=_params(_ARB))(h, g, tgt)


def _hg_consts():
    c = CHUNK
    t = np.arange(c)
    mats, masks = [], []
    for lvl in range(6):
        m = 1 << lvl
        blk = t // m
        mat = np.zeros((c, c), np.float32)
        for tt in range(c):
            b = blk[tt]
            if b % 2 == 1:
                mat[tt, b * m:tt + 1] = 1.0
            else:
                mat[tt, tt + 1:(b + 1) * m] = 1.0
        mats.append(mat)
        same = (t[:, None] // (2 * m)) == (t[None, :] // (2 * m))
        masks.append((same & (blk[:, None] % 2 == 1) & (blk[None, :] % 2 == 0)).astype(np.float32))
    pre = np.tril(np.ones((c, c), np.float32))
    suf = np.triu(np.ones((c, c), np.float32), 1)
    mstack = np.concatenate(mats + [pre, suf], 0)
    masks.append(np.eye(c, dtype=np.float32))
    return (jnp.asarray(mstack, BF16), jnp.asarray(mstack.T.copy(), BF16), jnp.asarray(np.stack(masks), F32),
            jnp.asarray(np.eye(HEAD, dtype=np.float32)))


def _gd_consts():
    c = CHUNK
    incl = np.tril(np.ones((c, c), np.float32))
    strict = np.tril(np.ones((c, c), np.float32), -1)
    eye = np.eye(c, dtype=np.float32)
    masks = np.stack([incl, strict, eye, incl.T.copy()])
    sel = np.zeros((GD_HEADS, HEAD, 2 * HEAD), np.float32)
    for j in range(GD_HEADS):
        sel[j, j, :HEAD] = 1.0
        sel[j, GD_HEADS + j, HEAD:] = 1.0
    return (jnp.asarray(incl, BF16), jnp.asarray(incl.T.copy(), BF16), jnp.asarray(masks, F32), jnp.asarray(sel, BF16))


def _chunks_per_step(nc):
    for cb in (32 // HPS, 2, 1):
        if nc % cb == 0:
            return cb
    return 1


def _hg_prep(hq, hf, lg):
    lb = _sigmoid(lg[0:1, :] - lg[1:2, :])
    sg = _sigmoid(hf)
    sgn = _sigmoid(-hf)
    f = lb + (1.0 - lb) * sg
    lf = jnp.log(f)
    kk = (1.0 - lb) * sgn
    q = _silu(hq) * (HEAD ** -0.5)
    return lb, sg, sgn, f, lf, kk, q


def _mx_each(m, xs):
    wide = [jnp.concatenate(_split3(x), axis=1) for x in xs]
    prods = [jnp.dot(m, w, preferred_element_type=F32) for w in wide]
    return [p[:, :HEAD] + p[:, HEAD:2 * HEAD] + p[:, 2 * HEAD:] for p in prods]


def _hg_scores(q, kk, ex, mask_ref):
    p = [mask_ref[6] * _rowsum(a * b) for a, b in zip(q, kk)]
    for lvl in range(6):
        el = [e[lvl * CHUNK:(lvl + 1) * CHUNK] for e in ex]
        d = [_dot_nt(a * e, b * e) for a, b, e in zip(q, kk, el)]
        p = [x + mask_ref[lvl] * y for x, y in zip(p, d)]
    return p


def _hgrn_fwd(hq, hf, hi, hg, logits, gain, consts):
    t = hq[0].shape[0]
    nc = t // CHUNK
    cb = _chunks_per_step(nc)
    rows = cb * CHUNK
    mstack, _, masks, eye = consts
    tile = pl.BlockSpec((rows, HPS * HEAD), lambda c, g: (c, g))

    def body(hq_ref, hf_ref, hi_ref, hg_ref, lg_ref, gain_ref, m_ref, mask_ref, eye_ref,
             oraw_ref, og_ref, ssave_ref, state):
        c = pl.program_id(0)
        g = pl.program_id(1)

        @pl.when(c == 0)
        def _():
            for hh in range(HPS):
                state[g * HPS + hh] = jnp.zeros((HEAD, HEAD), F32)

        lg_all = lg_ref[...]
        gain_v = gain_ref[...]

        def one(i, carry):
            sl = pl.ds(pl.multiple_of(i * CHUNK, CHUNK), CHUNK)
            hs = range(HPS)
            heads = [g * HPS + hh for hh in hs]
            ln = [slice(hh * HEAD, (hh + 1) * HEAD) for hh in hs]
            preps = [_hg_prep(hq_ref[sl, s], hf_ref[sl, s], lg_all[:, s]) for s in ln]
            lf, kk, q = [p[4] for p in preps], [p[5] for p in preps], [p[6] for p in preps]
            v = [hi_ref[sl, s] for s in ln]
            ex = [jnp.exp(x) for x in _mx_each(m_ref[...], lf)]
            eb = [e[6 * CHUNK:7 * CHUNK] for e in ex]
            esfx = [e[7 * CHUNK:8 * CHUNK] for e in ex]
            p = _hg_scores(q, kk, ex, mask_ref)
            s0 = [state[h] for h in heads]
            o = _each(lambda a, e, s, pp, vv: _dot(a * e, s) + _dot(pp, vv), q, eb, s0, p, v)
            eye_v = eye_ref[...]
            s1 = _each(lambda s, e, kx, ef, vv: s * _row_to_col(e[CHUNK - 1:CHUNK, :], eye_v) + _dot_tn(kx * ef, vv),
                       s0, eb, kk, esfx, v)
            for hh in hs:
                ssave_ref[i, hh] = s0[hh]
                state[heads[hh]] = s1[hh]
                oraw_ref[sl, ln[hh]] = o[hh]
                r = lax.rsqrt(jnp.mean(o[hh] * o[hh], axis=1, keepdims=True) + EPS)
                og_ref[sl, ln[hh]] = (o[hh] * r * gain_v * _silu(hg_ref[sl, ln[hh]])).astype(BF16)
            return carry

        lax.fori_loop(0, cb, one, 0, unroll=2)

    return pl.pallas_call(
        body, name="hgrn_fwd", grid=(nc // cb, HG_HEADS // HPS),
        in_specs=[_view_tile(v, rows, HPS * HEAD) for v in (hq, hf, hi, hg)] + [
                  pl.BlockSpec((2, HPS * HEAD), lambda c, g: (0, g)),
                  pl.BlockSpec((1, HEAD), lambda c, g: (0, 0)),
                  pl.BlockSpec(mstack.shape, lambda c, g: (0, 0)),
                  pl.BlockSpec(masks.shape, lambda c, g: (0, 0, 0)),
                  pl.BlockSpec(eye.shape, lambda c, g: (0, 0))],
        out_specs=[tile, tile, pl.BlockSpec((cb, HPS, HEAD, HEAD), lambda c, g: (c, g, 0, 0))],
        out_shape=[jax.ShapeDtypeStruct((t, HG_HEADS * HEAD), F32), jax.ShapeDtypeStruct((t, HG_HEADS * HEAD), BF16),
                   jax.ShapeDtypeStruct((nc, HG_HEADS, HEAD, HEAD), F32)],
        scratch_shapes=[pltpu.VMEM((HG_HEADS, HEAD, HEAD), F32)],
        compiler_params=_params(_ARB, _ARB))(hq[0], hf[0], hi[0], hg[0], logits, gain, mstack, masks, eye)


def _hgrn_bwd(hq, hf, hi, hg, logits, gain, oraw, ssave, dog, consts):
    t = hq[0].shape[0]
    nc = t // CHUNK
    cb = _chunks_per_step(nc)
    rows = cb * CHUNK
    nb = nc // cb
    mstack, mstack_t, masks, eye = consts
    tile = pl.BlockSpec((rows, HPS * HEAD), lambda c, g: (nb - 1 - c, g))

    def body(hq_ref, hf_ref, hi_ref, hg_ref, lg_ref, gain_ref, oraw_ref, ssave_ref, dog_ref, m_ref, mt_ref,
             mask_ref, eye_ref, dhq_ref, dhf_ref, dhi_ref, dhg_ref, dgain_ref, dlb_ref, dstate):
        c = pl.program_id(0)
        g = pl.program_id(1)

        @pl.when(c == 0)
        def _():
            for hh in range(HPS):
                dstate[g * HPS + hh] = jnp.zeros((HEAD, HEAD), F32)

        @pl.when((c == 0) & (g == 0))
        def _():
            dgain_ref[...] = jnp.zeros_like(dgain_ref)
            dlb_ref[...] = jnp.zeros_like(dlb_ref)

        lg_all = lg_ref[...]
        gain_v = gain_ref[...]
        eye_v = eye_ref[...]
        last_row = (lax.broadcasted_iota(jnp.int32, (CHUNK, HEAD), 0) == CHUNK - 1).astype(F32)

        def one(j, carry):
            i = cb - 1 - j
            sl = pl.ds(pl.multiple_of(i * CHUNK, CHUNK), CHUNK)
            hs = range(HPS)
            heads = [g * HPS + hh for hh in hs]
            ln = [slice(hh * HEAD, (hh + 1) * HEAD) for hh in hs]
            hqv = [hq_ref[sl, s] for s in ln]
            hgv = [hg_ref[sl, s] for s in ln]
            preps = [_hg_prep(a, hf_ref[sl, s], lg_all[:, s]) for a, s in zip(hqv, ln)]
            lb, sg, sgn, f, lf, kk, q = ([p[n] for p in preps] for n in range(7))
            v = [hi_ref[sl, s] for s in ln]
            ex = [jnp.exp(x) for x in _mx_each(m_ref[...], lf)]
            eb = [e[6 * CHUNK:7 * CHUNK] for e in ex]
            esfx = [e[7 * CHUNK:8 * CHUNK] for e in ex]
            p = _hg_scores(q, kk, ex, mask_ref)
            s0 = [ssave_ref[i, hh] for hh in hs]
            ds = [dstate[h] for h in heads]

            o = [oraw_ref[sl, s] for s in ln]
            r = [lax.rsqrt(jnp.mean(x * x, axis=1, keepdims=True) + EPS) for x in o]
            on = _each(lambda x, y: x * y, o, r)
            dg_out = [dog_ref[sl, s] for s in ln]
            sgate = [_silu(x) for x in hgv]
            for hh in hs:
                dhg_ref[sl, ln[hh]] = (dg_out[hh] * on[hh] * gain_v * _dsilu(hgv[hh])).astype(BF16)
            dgain_ref[...] += sum(jnp.sum(d * s * n, axis=0, keepdims=True) for d, s, n in zip(dg_out, sgate, on))
            don = _each(lambda d, s: d * s * gain_v, dg_out, sgate)
            do = _each(lambda rr, dn, n: rr * (dn - n * jnp.mean(dn * n, axis=1, keepdims=True)), r, don, on)

            dp = _each(_dot_nt, do, v)
            dv = _each(lambda pp, d, kx, ef, s: _dot_tn(pp, d) + _dot(kx * ef, s), p, do, kk, esfx, ds)
            dqb = _each(_dot_nt, do, s0)
            dkx = _each(_dot_nt, v, ds)
            diag = [_rowsum(mask_ref[6] * x) for x in dp]
            dq = _each(lambda a, e, d, kx: a * e + d * kx, dqb, eb, diag, kk)
            dk = _each(lambda a, e, d, qq: a * e + d * qq, dkx, esfx, diag, q)
            dxs = [[] for _ in hs]
            for lvl in range(6):
                el = [e[lvl * CHUNK:(lvl + 1) * CHUNK] for e in ex]
                gm = [mask_ref[lvl] * x for x in dp]
                a1 = _each(lambda m_, kx, e: _dot(m_, kx * e), gm, kk, el)
                a2 = _each(lambda m_, qq, e: _dot_tn(m_, qq * e), gm, q, el)
                dq = _each(lambda x, a, e: x + a * e, dq, a1, el)
                dk = _each(lambda x, a, e: x + a * e, dk, a2, el)
                for hh in hs:
                    dxs[hh].append((a1[hh] * q[hh] + a2[hh] * kk[hh]) * el[hh])
            e_end_row = [e[CHUNK - 1:CHUNK, :] for e in eb]
            ds_new = _each(lambda qq, e, d, er, s: _dot_tn(qq * e, d) + _row_to_col(er, eye_v) * s, q, eb, do, e_end_row, ds)
            for hh in hs:
                dstate[heads[hh]] = ds_new[hh]
                dend_row = _col_to_row(_rowsum(s0[hh] * ds[hh]), eye_v)
                dxs[hh].append(dqb[hh] * q[hh] * eb[hh] + last_row * (e_end_row[hh] * dend_row))
                dxs[hh].append(dkx[hh] * kk[hh] * esfx[hh])
            dlf = _mx_each(mt_ref[...], [jnp.concatenate(x, axis=0) for x in dxs])

            for hh in hs:
                dhi_ref[sl, ln[hh]] = dv[hh].astype(BF16)
                dhq_ref[sl, ln[hh]] = (dq[hh] * (HEAD ** -0.5) * _dsilu(hqv[hh])).astype(BF16)
                df = dlf[hh] / f[hh]
                dsig = (1.0 - lb[hh]) * sg[hh] * sgn[hh]
                dhf_ref[sl, ln[hh]] = ((df - dk[hh]) * dsig).astype(BF16)
                dlb_t = jnp.sum(df * sgn[hh] - dk[hh] * sgn[hh], axis=0, keepdims=True)
                dlb_ref[pl.ds(heads[hh], 1), :] += dlb_t * lb[hh] * (1.0 - lb[hh])
            return carry

        lax.fori_loop(0, cb, one, 0, unroll=2)

    outs = [jax.ShapeDtypeStruct((t, HG_HEADS * HEAD), BF16)] * 4 + [
        jax.ShapeDtypeStruct((1, HEAD), F32), jax.ShapeDtypeStruct((HG_HEADS, HEAD), F32)]
    return pl.pallas_call(
        body, name="hgrn_bwd", grid=(nb, HG_HEADS // HPS),
        in_specs=[_view_tile(v, rows, HPS * HEAD, lambda c: nb - 1 - c) for v in (hq, hf, hi, hg)] + [
                  pl.BlockSpec((2, HPS * HEAD), lambda c, g: (0, g)),
                  pl.BlockSpec((1, HEAD), lambda c, g: (0, 0)), tile,
                  pl.BlockSpec((cb, HPS, HEAD, HEAD), lambda c, g: (nb - 1 - c, g, 0, 0)), tile,
                  pl.BlockSpec(mstack.shape, lambda c, h: (0, 0)),
                  pl.BlockSpec(mstack_t.shape, lambda c, h: (0, 0)),
                  pl.BlockSpec(masks.shape, lambda c, h: (0, 0, 0)),
                  pl.BlockSpec(eye.shape, lambda c, h: (0, 0))],
        out_specs=[tile, tile, tile, tile, pl.BlockSpec((1, HEAD), lambda c, h: (0, 0)),
                   pl.BlockSpec((HG_HEADS, HEAD), lambda c, h: (0, 0))],
        out_shape=outs, scratch_shapes=[pltpu.VMEM((HG_HEADS, HEAD, HEAD), F32)],
        compiler_params=_params(_ARB, _ARB))(hq[0], hf[0], hi[0], hg[0], logits, gain, oraw, ssave, dog, mstack,
                                             mstack_t, masks, eye)


CONV_W = 512


def _per_head(fn, *arrs):
    width = arrs[0].shape[1]
    return jnp.concatenate([fn(*[a[:, j:j + HEAD] for a in arrs]) for j in range(0, width, HEAD)], axis=1)


def _shift_down(xv, halo, d, top_rows):
    if d == 0:
        return xv, xv[0:8]
    main = pltpu.roll(xv, d, 0)
    top = jnp.where(top_rows < d, pltpu.roll(halo, d, 0), main[0:8])
    return main, top


def _conv_parts(x_ref, halo_ref, w_ref, first):
    xv = x_ref[...]
    halo = jnp.where(first, 0.0, halo_ref[...])
    top_rows = lax.broadcasted_iota(jnp.int32, (8, xv.shape[1]), 0)
    shifted = [_shift_down(xv, halo, CONV_K - 1 - j, top_rows) for j in range(CONV_K)]
    w = w_ref[...]
    acc = sum(shifted[j][0] * w[j:j + 1, :] for j in range(CONV_K))
    acc_top = sum(shifted[j][1] * w[j:j + 1, :] for j in range(CONV_K))
    return shifted, acc, acc_top


def _conv_fwd(x, w8, l2scale, name):
    x, off, width = x
    t = x.shape[0]
    o = off // CONV_W
    tr = _pick(t, 512, 8)

    def post(cv):
        s = _silu(cv)
        if l2scale is not None:
            s = _per_head(lambda sh: sh * (lax.rsqrt(_rowsum(sh * sh) + EPS) * l2scale), s)
        return s

    def body(x_ref, halo_ref, w_ref, o_ref):
        _, acc, acc_top = _conv_parts(x_ref, halo_ref, w_ref, pl.program_id(1) == 0)
        o_ref[...] = post(acc)
        o_ref[0:8, :] = post(acc_top)

    return pl.pallas_call(
        body, name=name, grid=(width // CONV_W,t // tr),
        in_specs=[pl.BlockSpec((tr, CONV_W), lambda j, i: (i, o + j)),
                  pl.BlockSpec((8, CONV_W), lambda j, i: (jnp.maximum(i * (tr // 8) - 1, 0), o + j)),
                  pl.BlockSpec((8, CONV_W), lambda j, i: (0, j))],
        out_specs=pl.BlockSpec((tr, CONV_W), lambda j, i: (i, j)),
        out_shape=jax.ShapeDtypeStruct((t, width), F32), compiler_params=_params(_PAR, _PAR))(x, x, w8)


def _conv_bwd_a(x, w8, dy, l2scale, name):
    x, off, width = x
    t = x.shape[0]
    o = off // CONV_W
    tr = _pick(t, 512, 8)

    def l2_bwd(s, dyh):
        r = lax.rsqrt(_rowsum(s * s) + EPS)
        y0 = s * r
        dy0 = dyh * l2scale
        return r * (dy0 - y0 * _rowsum(dy0 * y0))

    def to_dc(cv, dyv):
        if l2scale is not None:
            dyv = _per_head(l2_bwd, _silu(cv), dyv)
        return dyv * _dsilu(cv)

    def body(x_ref, halo_ref, w_ref, dy_ref, dc_ref, dw_ref):
        @pl.when(pl.program_id(1) == 0)
        def _():
            dw_ref[...] = jnp.zeros_like(dw_ref)

        shifted, acc, acc_top = _conv_parts(x_ref, halo_ref, w_ref, pl.program_id(1) == 0)
        dyv = dy_ref[...]
        dc = to_dc(acc, dyv)
        dc_top = to_dc(acc_top, dyv[0:8])
        dc_ref[...] = dc
        dc_ref[0:8, :] = dc_top
        rest = (lax.broadcasted_iota(jnp.int32, dc.shape, 0) >= 8).astype(F32)
        dc_rest = dc * rest
        for j in range(CONV_K):
            dw_ref[j:j + 1, :] += (jnp.sum(dc_rest * shifted[j][0], axis=0, keepdims=True)
                                   + jnp.sum(dc_top * shifted[j][1], axis=0, keepdims=True))

    return pl.pallas_call(
        body, name=name, grid=(width // CONV_W,t // tr),
        in_specs=[pl.BlockSpec((tr, CONV_W), lambda j, i: (i, o + j)),
                  pl.BlockSpec((8, CONV_W), lambda j, i: (jnp.maximum(i * (tr // 8) - 1, 0), o + j)),
                  pl.BlockSpec((8, CONV_W), lambda j, i: (0, j)),
                  pl.BlockSpec((tr, CONV_W), lambda j, i: (i, j))],
        out_specs=[pl.BlockSpec((tr, CONV_W), lambda j, i: (i, j)), pl.BlockSpec((8, CONV_W), lambda j, i: (0, j))],
        out_shape=[jax.ShapeDtypeStruct((t, width), F32), jax.ShapeDtypeStruct((8, width), F32)],
        compiler_params=_params(_PAR, _ARB))(x, x, w8, dy)


def _conv_bwd_b(dc, w8, name):
    t, width = dc.shape
    tr = _pick(t, 512, 8)
    nt = t // tr

    def body(dc_ref, halo_ref, w_ref, dx_ref):
        dcv = dc_ref[...]
        halo = jnp.where(pl.program_id(1) == nt - 1, 0.0, halo_ref[...])
        w = w_ref[...]
        bot_rows = lax.broadcasted_iota(jnp.int32, (8, CONV_W), 0)
        acc = dcv * w[CONV_K - 1:CONV_K, :]
        acc_bot = dcv[tr - 8:tr] * w[CONV_K - 1:CONV_K, :]
        for d in range(1, CONV_K):
            main = pltpu.roll(dcv, tr - d, 0)
            bot = jnp.where(bot_rows >= 8 - d, pltpu.roll(halo, 8 - d, 0), main[tr - 8:tr])
            wj = w[CONV_K - 1 - d:CONV_K - d, :]
            acc = acc + main * wj
            acc_bot = acc_bot + bot * wj
        dx_ref[...] = acc.astype(BF16)
        dx_ref[tr - 16:tr, :] = jnp.concatenate([acc[tr - 16:tr - 8], acc_bot], axis=0).astype(BF16)

    return pl.pallas_call(
        body, name=name, grid=(width // CONV_W,nt),
        in_specs=[pl.BlockSpec((tr, CONV_W), lambda j, i: (i, j)),
                  pl.BlockSpec((8, CONV_W), lambda j, i: (jnp.minimum((i + 1) * (tr // 8), t // 8 - 1), j)),
                  pl.BlockSpec((8, CONV_W), lambda j, i: (0, j))],
        out_specs=pl.BlockSpec((tr, CONV_W), lambda j, i: (i, j)),
        out_shape=jax.ShapeDtypeStruct((t, width), BF16), compiler_params=_params(_PAR, _PAR))(dc, dc, w8)


def _each(f, *lists):
    return [f(*xs) for xs in zip(*lists)]


def _split2_each(xs):
    hi = [_bf(x) for x in xs]
    lo = [_bf(x - h.astype(F32)) for x, h in zip(xs, hi)]
    return hi, lo


def _hp_each(a_split, b_split):
    (ah, al), (bh, bl) = a_split, b_split
    rows = ah[0].shape[0]
    d12 = [jnp.dot(jnp.concatenate([x, y], axis=0), z, preferred_element_type=F32) for x, y, z in zip(ah, al, bh)]
    d3 = [jnp.dot(x, y, preferred_element_type=F32) for x, y in zip(ah, bl)]
    return [d[:rows] + d[rows:] + e for d, e in zip(d12, d3)]


def _tri_inv_each(a_list, eye):
    ns = [-a for a in a_list]
    ps = [eye + n for n in ns]
    n_split = _split2_each(ns)
    for _ in range(5):
        ns = _hp_each(n_split, n_split)
        n_split = _split2_each(ns)
        ps = [p + d for p, d in zip(ps, _hp_each(_split2_each(ps), n_split))]
    return ps


def _gd_gates(gab, alog, dtb):
    sp_arg = gab + dtb
    return sp_arg, -jnp.exp(alog) * _softplus(sp_arg), _sigmoid(gab)


def _gd_chunks(q, k, v, g_all, beta_all, sel, l_ref, mask_ref, tm=None):
    incl, strict, eye, upper = mask_ref[0], mask_ref[1], mask_ref[2], mask_ref[3]
    lmat = l_ref[...]
    gates = jnp.concatenate(_split3(g_all) + _split3(beta_all), axis=0)
    picked = [jnp.dot(gates, s, preferred_element_type=F32) for s in sel]
    c = CHUNK
    gb = [p[0:c, :HEAD] + p[c:2 * c, :HEAD] + p[2 * c:3 * c, :HEAD] for p in picked]
    bb = [p[3 * c:4 * c, HEAD:] + p[4 * c:5 * c, HEAD:] + p[5 * c:, HEAD:] for p in picked]
    gam = _mx_each(lmat, gb)
    gam_row = [jnp.sum(x[:, :CHUNK] * upper, axis=0, keepdims=True) for x in gb]
    lm = _each(lambda gm, gr: incl * jnp.exp(jnp.minimum(gm[:, :CHUNK] - gr, 0.0)), gam, gam_row)
    kb = _each(lambda x, b: x * b, k, bb)
    a = _each(lambda x, y, m: strict * _dot_nt(x, y) * m, kb, k, lm)
    if tm is None:
        tm = _tri_inv_each(a, eye)
    eg = [jnp.exp(x) for x in gam]
    vb = _each(lambda x, b: x * b, v, bb)
    kbg = _each(lambda x, e: x * e, kb, eg)
    uw = _each(lambda t_, x, y: _dot(t_, jnp.concatenate([x, y], axis=1)), tm, vb, kbg)
    u = [x[:, :HEAD] for x in uw]
    w = [x[:, HEAD:] for x in uw]
    qk = _each(lambda x, y, m: _dot_nt(x, y) * m, q, k, lm)
    g_end = [x[CHUNK - 1:CHUNK, :] for x in gam]
    ekg = _each(lambda e, x: jnp.exp(e - x), g_end, gam)
    ge = [jnp.exp(e) for e in g_end]
    kg = _each(lambda x, e: x * e, k, ekg)
    qg = _each(lambda x, e: x * e, q, eg)
    names = ("bb", "lm", "kb", "a", "tm", "eg", "vb", "kbg", "u", "w", "qk", "ekg", "ge", "kg", "qg")
    cols = (bb, lm, kb, a, tm, eg, vb, kbg, u, w, qk, ekg, ge, kg, qg)
    return [dict(zip(names, vals)) for vals in zip(*cols)]


def _gd_specs(rows, rev_nb=None):
    def cidx(c):
        return c if rev_nb is None else rev_nb - 1 - c

    qk_tile = pl.BlockSpec((rows, HPS // 2 * HEAD), lambda c, g: (cidx(c), g))
    v_tile = pl.BlockSpec((rows, HPS * HEAD), lambda c, g: (cidx(c), g))
    gab_tile = pl.BlockSpec((rows, HEAD), lambda c, g: (cidx(c), 0))
    return qk_tile, v_tile, gab_tile


def _gdn_fwd(qn, kn, cv, gab, gz, alog, dtb, gain, consts):
    t = qn.shape[0]
    nc = t // CHUNK
    cb = _chunks_per_step(nc)
    rows = cb * CHUNK
    lmat, _, masks, sel = consts
    qk_tile, v_tile, gab_tile = _gd_specs(rows)
    row128 = pl.BlockSpec((1, HEAD), lambda c, h: (0, 0))

    def body(q_ref, k_ref, v_ref, gab_ref, gz_ref, alog_ref, dtb_ref, gain_ref, sel_ref, l_ref, mask_ref,
             oraw_ref, og_ref, ssave_ref, tsave_ref, state):
        c = pl.program_id(0)
        g = pl.program_id(1)

        @pl.when(c == 0)
        def _():
            for hh in range(HPS):
                state[g * HPS + hh] = jnp.zeros((HEAD, HEAD), F32)

        alog = alog_ref[...]
        dtb = dtb_ref[...]
        gain_v = gain_ref[...]

        def one(i, carry):
            sl = pl.ds(pl.multiple_of(i * CHUNK, CHUNK), CHUNK)
            _, g_all, beta_all = _gd_gates(gab_ref[sl, :], alog, dtb)
            heads = [g * HPS + hh for hh in range(HPS)]
            lq = [slice(hh // 2 * HEAD, (hh // 2 + 1) * HEAD) for hh in range(HPS)]
            lv = [slice(hh * HEAD, (hh + 1) * HEAD) for hh in range(HPS)]
            chs = _gd_chunks([q_ref[sl, s] for s in lq], [k_ref[sl, s] for s in lq], [v_ref[sl, s] for s in lv],
                             g_all, beta_all, [sel_ref[h] for h in heads], l_ref, mask_ref)
            s0 = [state[h] for h in heads]
            ws = _each(lambda ch, s: _dot(jnp.concatenate([ch["w"], ch["qg"]], axis=0), s), chs, s0)
            v_new = _each(lambda ch, x: ch["u"] - x[:CHUNK], chs, ws)
            o = _each(lambda ch, x, vn: x[CHUNK:] + _dot(ch["qk"], vn), chs, ws, v_new)
            s1 = _each(lambda ch, s, vn: s * ch["ge"] + _dot_tn(ch["kg"], vn), chs, s0, v_new)
            for hh in range(HPS):
                ssave_ref[i, hh] = s0[hh]
                tsave_ref[i, hh] = chs[hh]["tm"]
                state[heads[hh]] = s1[hh]
                oraw_ref[sl, lv[hh]] = o[hh]
                r = lax.rsqrt(jnp.mean(o[hh] * o[hh], axis=1, keepdims=True) + EPS)
                og_ref[sl, lv[hh]] = (o[hh] * r * gain_v * _silu(gz_ref[sl, lv[hh]])).astype(BF16)
            return carry

        lax.fori_loop(0, cb, one, 0, unroll=2)

    return pl.pallas_call(
        body, name="gdn_fwd", grid=(nc // cb, GD_HEADS // HPS),
        in_specs=[qk_tile, qk_tile, v_tile, gab_tile, _view_tile(gz, rows, HPS * HEAD), row128, row128, row128,
                  pl.BlockSpec(sel.shape, lambda c, g: (0, 0, 0)),
                  pl.BlockSpec(lmat.shape, lambda c, g: (0, 0)),
                  pl.BlockSpec(masks.shape, lambda c, g: (0, 0, 0))],
        out_specs=[v_tile, v_tile, pl.BlockSpec((cb, HPS, HEAD, HEAD), lambda c, g: (c, g, 0, 0)),
                   pl.BlockSpec((cb, HPS, CHUNK, CHUNK), lambda c, g: (c, g, 0, 0))],
        out_shape=[jax.ShapeDtypeStruct((t, GD_HEADS * HEAD), F32), jax.ShapeDtypeStruct((t, GD_HEADS * HEAD), BF16),
                   jax.ShapeDtypeStruct((nc, GD_HEADS, HEAD, HEAD), F32),
                   jax.ShapeDtypeStruct((nc, GD_HEADS, CHUNK, CHUNK), F32)],
        scratch_shapes=[pltpu.VMEM((GD_HEADS, HEAD, HEAD), F32)],
        compiler_params=_params(_ARB, _ARB))(qn, kn, cv, gab, gz[0], alog, dtb, gain, sel, lmat, masks)


def _gdn_bwd(qn, kn, cv, gab, gz, alog, dtb, gain, oraw, ssave, tsave, dog, consts):
    t = qn.shape[0]
    nc = t // CHUNK
    cb = _chunks_per_step(nc)
    rows = cb * CHUNK
    nb = nc // cb
    lmat, lmat_t, masks, sel = consts
    qk_tile, v_tile, gab_tile = _gd_specs(rows, nb)
    row128 = pl.BlockSpec((1, HEAD), lambda c, h: (0, 0))

    def body(q_ref, k_ref, v_ref, gab_ref, gz_ref, alog_ref, dtb_ref, gain_ref, oraw_ref, ssave_ref, tsave_ref, dog_ref,
             sel_ref, l_ref, lt_ref, mask_ref,
             dq_ref, dk_ref, dv_ref, dgab_ref, dgz_ref, small_ref, dstate):
        c = pl.program_id(0)
        g = pl.program_id(1)

        @pl.when(c == 0)
        def _():
            for hh in range(HPS):
                dstate[g * HPS + hh] = jnp.zeros((HEAD, HEAD), F32)

        @pl.when((c == 0) & (g == 0))
        def _():
            small_ref[...] = jnp.zeros_like(small_ref)

        alog = alog_ref[...]
        dtb = dtb_ref[...]
        gain_v = gain_ref[...]
        lane = lax.broadcasted_iota(jnp.int32, (1, HEAD), 1)
        last_row = (lax.broadcasted_iota(jnp.int32, (CHUNK, HEAD), 0) == CHUNK - 1).astype(F32)

        def one(j, carry):
            i = cb - 1 - j
            sl = pl.ds(pl.multiple_of(i * CHUNK, CHUNK), CHUNK)
            sp_arg, g_all, beta_all = _gd_gates(gab_ref[sl, :], alog, dtb)
            strict, eye = mask_ref[1], mask_ref[2]
            ltm = lt_ref[...]
            hs = range(HPS)
            heads = [g * HPS + hh for hh in hs]
            lq = [slice(hh // 2 * HEAD, (hh // 2 + 1) * HEAD) for hh in hs]
            lv = [slice(hh * HEAD, (hh + 1) * HEAD) for hh in hs]
            q = [q_ref[sl, s] for s in lq]
            k = [k_ref[sl, s] for s in lq]
            v = [v_ref[sl, s] for s in lv]
            gzv = [gz_ref[sl, s] for s in lv]
            chs = _gd_chunks(q, k, v, g_all, beta_all, [sel_ref[h] for h in heads], l_ref, mask_ref,
                             tm=[tsave_ref[i, hh] for hh in hs])

            def col(name):
                return [ch[name] for ch in chs]

            def mul(x, y):
                return x * y

            tm, lm, eg, bb = col("tm"), col("lm"), col("eg"), col("bb")
            s0 = [ssave_ref[i, hh] for hh in hs]
            ds = [dstate[h] for h in heads]
            v_new = _each(lambda u, w, s: u - _dot(w, s), col("u"), col("w"), s0)

            o = [oraw_ref[sl, s] for s in lv]
            r = [lax.rsqrt(jnp.mean(x * x, axis=1, keepdims=True) + EPS) for x in o]
            on = _each(mul, o, r)
            dg_out = [dog_ref[sl, s] for s in lv]
            sgate = [_silu(x) for x in gzv]
            for hh in hs:
                dgz_ref[sl, lv[hh]] = (dg_out[hh] * on[hh] * gain_v * _dsilu(gzv[hh])).astype(BF16)
            small_ref[0:1, :] += sum(jnp.sum(d * s * n, axis=0, keepdims=True) for d, s, n in zip(dg_out, sgate, on))
            don = _each(lambda d, s: d * s * gain_v, dg_out, sgate)
            do = _each(lambda rr, dn, n: rr * (dn - n * jnp.mean(dn * n, axis=1, keepdims=True)), r, don, on)

            dv_new = _each(lambda a, d, b, s: _dot_tn(a, d) + _dot(b, s), col("qk"), do, col("kg"), ds)
            dqk = _each(_dot_nt, do, v_new)
            dkg = _each(_dot_nt, v_new, ds)
            dge = _each(lambda s, d: jnp.sum(_rowsum(s * d), axis=0, keepdims=True), s0, ds)
            both = _each(lambda d, dv: jnp.concatenate([d, dv], axis=0), do, dv_new)
            from_s = _each(_dot_nt, both, s0)
            dqg = [x[:CHUNK] for x in from_s]
            dw = [-x[CHUNK:] for x in from_s]
            ds_new = _each(lambda qg, w, bo, ge, s: _dot_tn(jnp.concatenate([qg, -w], axis=0), bo) + ge * s,
                           col("qg"), col("w"), both, col("ge"), ds)
            for hh in hs:
                dstate[heads[hh]] = ds_new[hh]

            side = _each(lambda dv, d: jnp.concatenate([dv, d], axis=1), dv_new, dw)
            back = _each(_dot_tn, tm, side)
            dvb = [x[:, :HEAD] for x in back]
            dkbg = [x[:, HEAD:] for x in back]
            dtm = _each(lambda sd, vb, kbg: _dot_nt(sd, jnp.concatenate([vb, kbg], axis=1)), side, col("vb"), col("kbg"))
            dtt = _each(_dot_nt, dtm, tm)
            da = _each(lambda t_, x: -_dot_tn(t_, x) * strict, tm, dtt)
            dal = _each(mul, da, lm)
            dqk_l = _each(mul, dqk, lm)
            stack = _each(lambda x, y: jnp.concatenate([x, y], axis=0), dal, dqk_l)
            on_k = _each(_dot, stack, k)
            dkb = _each(lambda x, y, e: x[:CHUNK] + y * e, on_k, dkbg, eg)
            dq = _each(lambda x, y, e: x[CHUNK:] + y * e, on_k, dqg, eg)
            dk = _each(lambda st, kb, qq, z, ekg, w_, b: _dot_tn(st, jnp.concatenate([kb, qq], axis=0)) + z * ekg + w_ * b,
                       stack, col("kb"), q, dkg, col("ekg"), dkb, bb)
            gmat = _each(lambda x, a, y, qk: x * a + y * qk, da, col("a"), dqk, col("qk"))
            t_kg = _each(lambda x, y: _rowsum(x * y), dkg, col("kg"))
            dgam = _each(lambda gm, x, qg, t_, y, kbg: (_rowsum(gm) - _row_to_col(jnp.sum(gm, axis=0, keepdims=True), eye)
                                                        + _rowsum(x * qg) - t_ + _rowsum(y * kbg)),
                         gmat, dqg, col("qg"), t_kg, dkbg, col("kbg"))
            dg_end = _each(lambda t_, e, ge: jnp.sum(t_, axis=0, keepdims=True) + e * ge[:, 0:1], t_kg, dge, col("ge"))
            dgam = _each(lambda x, e: x + last_row * e, dgam, dg_end)
            dbeta = _each(lambda x, kk, y, vv: _rowsum(x * kk) + _rowsum(y * vv), dkb, k, dvb, v)
            dg = _mx_each(ltm, dgam)

            for hh in hs:
                dv_ref[sl, lv[hh]] = dvb[hh] * bb[hh]
            fac_g = -jnp.exp(alog) * _sigmoid(sp_arg)
            fac_b = beta_all * (1.0 - beta_all)
            hot_g = [(lane == h).astype(F32) for h in heads]
            hot_b = [(lane == GD_HEADS + h).astype(F32) for h in heads]
            dga = _each(lambda x, hot: x * hot * fac_g, dg, hot_g)
            dgb = _each(lambda x, hot: x * hot * fac_b, dbeta, hot_b)
            small_ref[1:2, :] += sum(jnp.sum(x, axis=0, keepdims=True) for x in dga)
            small_ref[2:3, :] += sum(jnp.sum(x * hot * g_all, axis=0, keepdims=True) for x, hot in zip(dg, hot_g))
            for pair in range(HPS // 2):
                lqp = slice(pair * HEAD, (pair + 1) * HEAD)
                dq_ref[sl, lqp] = dq[2 * pair] + dq[2 * pair + 1]
                dk_ref[sl, lqp] = dk[2 * pair] + dk[2 * pair + 1]
            dgab_ref[sl, :] = sum(a + b for a, b in zip(dga, dgb))
            return carry

        lax.fori_loop(0, cb, one, 0, unroll=2)

    groups = GD_HEADS // HPS
    outs = [jax.ShapeDtypeStruct((t, 1024), F32), jax.ShapeDtypeStruct((t, 1024), F32),
            jax.ShapeDtypeStruct((t, 2048), F32), jax.ShapeDtypeStruct((t, groups * HEAD), F32),
            jax.ShapeDtypeStruct((t, 2048), BF16), jax.ShapeDtypeStruct((8, HEAD), F32)]
    return pl.pallas_call(
        body, name="gdn_bwd", grid=(nb, groups),
        in_specs=[qk_tile, qk_tile, v_tile, gab_tile, _view_tile(gz, rows, HPS * HEAD, lambda c: nb - 1 - c),
                  row128, row128, row128, v_tile,
                  pl.BlockSpec((cb, HPS, HEAD, HEAD), lambda c, g: (nb - 1 - c, g, 0, 0)),
                  pl.BlockSpec((cb, HPS, CHUNK, CHUNK), lambda c, g: (nb - 1 - c, g, 0, 0)), v_tile,
                  pl.BlockSpec(sel.shape, lambda c, g: (0, 0, 0)),
                  pl.BlockSpec(lmat.shape, lambda c, g: (0, 0)),
                  pl.BlockSpec(lmat_t.shape, lambda c, g: (0, 0)),
                  pl.BlockSpec(masks.shape, lambda c, g: (0, 0, 0))],
        out_specs=[qk_tile, qk_tile, v_tile, pl.BlockSpec((rows, HEAD), lambda c, g: (nb - 1 - c, g)), v_tile,
                   pl.BlockSpec((8, HEAD), lambda c, g: (0, 0))],
        out_shape=outs, scratch_shapes=[pltpu.VMEM((GD_HEADS, HEAD, HEAD), F32)],
        compiler_params=_params(_ARB, _ARB))(qn, kn, cv, gab, gz[0], alog, dtb, gain, oraw, ssave, tsave, dog, sel,
                                             lmat, lmat_t, masks)


def _fold_groups(wide):
    t, width = wide.shape
    tr = _pick(t, 512, 8)

    def body(w_ref, o_ref):
        acc = w_ref[:, 0:HEAD]
        for j in range(1, width // HEAD):
            acc = acc + w_ref[:, j * HEAD:(j + 1) * HEAD]
        o_ref[...] = acc.astype(BF16)

    return pl.pallas_call(
        body, name="fold_gate_grads", grid=(t // tr,), in_specs=[_row_spec(tr, width)], out_specs=_row_spec(tr, HEAD),
        out_shape=jax.ShapeDtypeStruct((t, HEAD), BF16), compiler_params=_params(_PAR))(wide)


def _adam_math(w, g, m, v):
    m2 = ADAM_B1 * m + (1.0 - ADAM_B1) * g
    v2 = ADAM_B2 * v + (1.0 - ADAM_B2) * (g * g)
    m_hat = m2 / (1.0 - ADAM_B1 ** ADAM_STEP)
    v_hat = v2 / (1.0 - ADAM_B2 ** ADAM_STEP)
    delta = -ADAM_LR * (m_hat / (jnp.sqrt(v_hat) + ADAM_EPS) + ADAM_WD * w)
    return delta, m2, v2


def _adamw(w, g, m, v, name, after=None):
    r, c = w.shape
    tr = r
    for cand in range(8, r + 1, 8):
        if r % cand == 0 and cand * c * 4 <= (1 << 20):
            tr = cand
    if r % 8 != 0:
        tr = r

    def body(w_ref, g_ref, m_ref, v_ref, *rest):
        d_ref, m2_ref, v2_ref = rest[-3:]
        d, m2, v2 = _adam_math(w_ref[...], g_ref[...], m_ref[...], v_ref[...])
        d_ref[...] = d
        m2_ref[...] = m2
        v2_ref[...] = v2

    spec = pl.BlockSpec((tr, c), lambda i: (i, 0))
    extra = [] if after is None else [after]
    return pl.pallas_call(
        body, name=name, grid=(r // tr,), in_specs=[spec] * 4 + [_ANY] * len(extra), out_specs=[spec] * 3,
        out_shape=[jax.ShapeDtypeStruct((r, c), F32)] * 3, compiler_params=_params(_PAR))(w, g, m, v, *extra)


_ANY = pl.BlockSpec(memory_space=pl.ANY)


def _place():
    return lax.axis_index("x"), lax.axis_index("y"), lax.axis_index("c")


def _gather_weights(packs, nchs, name):
    n = len(packs)
    halves = [p.shape[0] // 2 for p in packs]
    base = [sum(nchs[:i]) for i in range(n)]
    total = sum(nchs)
    for p, h, k in zip(packs, halves, nchs):
        assert p.shape[0] == 2 * h and h % k == 0 and (h // k) % 16 == 0

    def body(*refs):
        p_refs, g_refs, (send_sems, recv_sems) = refs[:n], refs[n:2 * n], refs[2 * n:]
        x, y, c = _place()
        sibling = (x, y, 1 - c)
        chips = [(1 - x, y), (x, 1 - y), (1 - x, 1 - y)]
        chunks = [(a, q) for a in range(n) for q in range(nchs[a])]

        def rows_of(a, pc, q):
            ch = halves[a] // nchs[a]
            return pl.ds(pl.multiple_of(pc * halves[a] + q * ch, 16), ch)

        def piece(a, px, py, pc, q):
            return g_refs[a].at[2 * px + py, rows_of(a, pc, q), :]

        def copy(k, src, dst, to):
            return pltpu.make_async_remote_copy(src_ref=src, dst_ref=dst, send_sem=send_sems.at[k],
                                                recv_sem=recv_sems.at[k], device_id=to, device_id_type=MESH)

        def sem_of(j, a, q):
            return j * total + base[a] + q

        first = {(j, a, q): copy(sem_of(j, a, q), p_refs[a].at[rows_of(a, c, q), :], piece(a, x, y, c, q), (*chip, c))
                 for j, chip in enumerate(chips) for a, q in chunks}
        for a, q in chunks:
            for j in range(3):
                first[j, a, q].start()
        passed = {(j, a, q): copy(sem_of(3 + j, a, q), piece(a, *chip, c, q), piece(a, *chip, c, q), sibling)
                  for j, chip in enumerate(chips) for a, q in chunks}
        for a, q in chunks:
            for j, chip in enumerate(chips):
                copy(sem_of(j, a, q), p_refs[a].at[rows_of(a, c, q), :], piece(a, *chip, c, q), (*chip, c)).wait_recv()
                passed[j, a, q].start()
        for a, q in chunks:
            for j, chip in enumerate(chips):
                copy(sem_of(3 + j, a, q), piece(a, *chip, 1 - c, q), piece(a, *chip, 1 - c, q), sibling).wait_recv()
        for key in first:
            first[key].wait_send()
            passed[key].wait_send()

    return pl.pallas_call(
        body, name=name, out_shape=[jax.ShapeDtypeStruct((4,) + p.shape, p.dtype) for p in packs],
        in_specs=[_ANY] * n, out_specs=[_ANY] * n,
        scratch_shapes=[pltpu.SemaphoreType.DMA((6 * total,)), pltpu.SemaphoreType.DMA((6 * total,))])(*packs)


def _swap_with_sibling(arrs, nchs, lead, name, halves=False):
    n = len(arrs)
    jobs = []
    hs = [arr.shape[-2] // (2 if halves else 1) for arr in arrs]
    for a, (h, k) in enumerate(zip(hs, nchs)):
        assert h % k == 0 and (h // k) % 16 == 0
        for s in (range(lead) if lead else [None]):
            jobs += [(a, s, q * (h // k), h // k) for q in range(k)]

    def body(*refs):
        src, dst, (send_sems, recv_sems) = refs[:n], refs[n:2 * n], refs[2 * n:]
        x, y, c = _place()

        def at(ref, s, r0, rows):
            return ref.at[pl.ds(r0, rows), :] if s is None else ref.at[s, pl.ds(r0, rows), :]

        def src_rows(a, r0):
            return pl.multiple_of((1 - c) * hs[a] + r0, 16) if halves else r0

        copies = [pltpu.make_async_remote_copy(
            src_ref=at(src[a], s, src_rows(a, r0), rows), dst_ref=at(dst[a], s, r0, rows), send_sem=send_sems.at[k],
            recv_sem=recv_sems.at[k], device_id=(x, y, 1 - c), device_id_type=MESH)
            for k, (a, s, r0, rows) in enumerate(jobs)]
        for cp in copies:
            cp.start()
        for cp in copies:
            cp.wait()

    shapes = [jax.ShapeDtypeStruct(arr.shape[:-2] + (h, arr.shape[-1]), arr.dtype) for arr, h in zip(arrs, hs)]
    return pl.pallas_call(
        body, name=name, out_shape=shapes, in_specs=[_ANY] * n, out_specs=[_ANY] * n,
        scratch_shapes=[pltpu.SemaphoreType.DMA((len(jobs),)), pltpu.SemaphoreType.DMA((len(jobs),))])(*arrs)


def _add2(full, b, core, name):
    n, rows, w = b.shape
    tr = _pick(rows, 256, 16)
    nblk = rows // tr

    def body(c_ref, a_ref, b_ref, o_ref):
        o_ref[...] = (a_ref[...].astype(F32) + b_ref[...].astype(F32)).astype(BF16)

    spec = pl.BlockSpec((1, tr, w), lambda i, j, c_ref: (i, j, 0))
    grid_spec = pltpu.PrefetchScalarGridSpec(
        num_scalar_prefetch=1, grid=(n, nblk),
        in_specs=[pl.BlockSpec((1, tr, w), lambda i, j, c_ref: (i, c_ref[0] * nblk + j, 0)), spec], out_specs=spec)
    return pl.pallas_call(
        body, name=name, grid_spec=grid_spec, out_shape=jax.ShapeDtypeStruct(b.shape, BF16),
        compiler_params=_params(_PAR, _PAR))(core, full, b)


def _reduce_chips(partials, nchs, name):
    n = len(partials)
    jobs = []
    for a, (arr, k) in enumerate(zip(partials, nchs)):
        h = arr.shape[1]
        assert h % k == 0 and (h // k) % 16 == 0
        jobs += [(a, q * (h // k), h // k) for q in range(k)]

    def body(*refs):
        src, dst, (send_sems, recv_sems) = refs[:n], refs[n:2 * n], refs[2 * n:]
        x, y, c = _place()
        chips = [(1 - x, y), (x, 1 - y), (1 - x, 1 - y)]
        copies = [pltpu.make_async_remote_copy(
            src_ref=src[a].at[2 * px + py, pl.ds(r0, rows), :], dst_ref=dst[a].at[j, pl.ds(r0, rows), :],
            send_sem=send_sems.at[3 * k + j], recv_sem=recv_sems.at[3 * k + j],
            device_id=(px, py, c), device_id_type=MESH)
            for k, (a, r0, rows) in enumerate(jobs) for j, (px, py) in enumerate(chips)]
        for cp in copies:
            cp.start()
        for cp in copies:
            cp.wait()

    return pl.pallas_call(
        body, name=name,
        out_shape=[jax.ShapeDtypeStruct((3,) + p.shape[1:], p.dtype) for p in partials],
        in_specs=[_ANY] * n, out_specs=[_ANY] * n,
        scratch_shapes=[pltpu.SemaphoreType.DMA((3 * len(jobs),)), pltpu.SemaphoreType.DMA((3 * len(jobs),))])(*partials)


_HBM = pl.BlockSpec(memory_space=pltpu.HBM)
_SEM = pl.BlockSpec(memory_space=pltpu.SEMAPHORE)
_DATAFLOW = pltpu.SideEffectType.DATAFLOW_SIDE_EFFECTING


def _ici_jobs(srcs, nchs, kind):
    jobs = []
    for a, (arr, k) in enumerate(zip(srcs, nchs)):
        h = arr.shape[0] // 2 if kind == "gather" else arr.shape[1]
        assert h % k == 0 and (h // k) % 16 == 0
        jobs += [(a, h, q * (h // k), h // k) for q in range(k)]
    return jobs


def _ici_copies(src, land, send_sems, recv_sems, jobs, kind):
    x, y, c = _place()
    chips = [(1 - x, y), (x, 1 - y), (1 - x, 1 - y)]
    copies = []
    for k, (a, h, r0, rows) in enumerate(jobs):
        for j, (px, py) in enumerate(chips):
            if kind == "gather":
                at = pl.ds(pl.multiple_of(c * h + r0, 16), rows)
                s, d = src[a].at[at, :], land[a].at[2 * x + y, at, :]
            else:
                s, d = src[a].at[2 * px + py, pl.ds(r0, rows), :], land[a].at[j, pl.ds(r0, rows), :]
            copies.append(pltpu.make_async_remote_copy(
                src_ref=s, dst_ref=d, send_sem=send_sems.at[3 * k + j], recv_sem=recv_sems.at[3 * k + j],
                device_id=(px, py, c), device_id_type=MESH))
    return copies


def _ici_start(srcs, nchs, kind, name):
    n = len(srcs)
    jobs = _ici_jobs(srcs, nchs, kind)
    lead = (lambda s: (4,) + s.shape) if kind == "gather" else (lambda s: (3,) + s.shape[1:])
    lands = [lax.empty(lead(s), s.dtype) for s in srcs]

    def body(*refs):
        src, land = refs[:n], refs[n:2 * n]
        send_sems, recv_sems, token = refs[2 * n], refs[2 * n + 1], refs[-1]
        for cp in _ici_copies(src, land, send_sems, recv_sems, jobs, kind):
            cp.start()
        token[...] = jnp.zeros_like(token)

    hbm = [pltpu.HBM(a.shape, a.dtype) for a in srcs + lands]
    outs = pl.pallas_call(
        body, name=name,
        out_shape=[pltpu.SemaphoreType.DMA((3 * len(jobs),)), pltpu.SemaphoreType.DMA((3 * len(jobs),))] + hbm
        + [jax.ShapeDtypeStruct((8, 128), F32)],
        in_specs=[_HBM] * (2 * n), out_specs=[_SEM, _SEM] + [_HBM] * (2 * n) + [pl.BlockSpec(memory_space=pltpu.VMEM)],
        input_output_aliases={i: 2 + i for i in range(2 * n)},
        compiler_params=pltpu.CompilerParams(has_side_effects=_DATAFLOW),
    )(*[pltpu.with_memory_space_constraint(a, pltpu.HBM) for a in srcs + lands])
    return (outs[0], outs[1], list(outs[2:2 + n]), list(outs[2 + n:2 + 2 * n]), nchs, kind), outs[-1]


def _ici_wait(handle, after, name):
    send_sems, recv_sems, srcs, lands, nchs, kind = handle
    n = len(srcs)
    jobs = _ici_jobs(srcs, nchs, kind)

    def body(*refs):
        src, land = refs[:n], refs[n:2 * n]
        for cp in _ici_copies(src, land, refs[2 * n], refs[2 * n + 1], jobs, kind):
            cp.wait_send()
            cp.wait_recv()

    outs = pl.pallas_call(
        body, name=name, out_shape=[pltpu.HBM(a.shape, a.dtype) for a in srcs + lands],
        in_specs=[_HBM] * (2 * n) + [_SEM, _SEM, _ANY], out_specs=[_HBM] * (2 * n),
        input_output_aliases={i: i for i in range(2 * n)},
        compiler_params=pltpu.CompilerParams(has_side_effects=_DATAFLOW),
    )(*srcs, *lands, send_sems, recv_sems, after)
    return list(outs[:n]), list(outs[n:])


def _pass_to_sibling(gathered, nchs, name):
    n = len(gathered)
    jobs = _ici_jobs([jax.ShapeDtypeStruct(g.shape[1:], g.dtype) for g in gathered], nchs, "gather")

    def body(*refs):
        src, dst, (send_sems, recv_sems) = refs[:n], refs[n:2 * n], refs[2 * n:]
        x, y, c = _place()
        slots = [2 * (1 - x) + y, 2 * x + (1 - y), 2 * (1 - x) + (1 - y)]

        def copy(k, j, pc):
            a, h, r0, rows = jobs[k]
            at = pl.ds(pl.multiple_of(pc * h + r0, 16), rows)
            return pltpu.make_async_remote_copy(
                src_ref=src[a].at[slots[j], at, :], dst_ref=dst[a].at[slots[j], at, :], send_sem=send_sems.at[3 * k + j],
                recv_sem=recv_sems.at[3 * k + j], device_id=(x, y, 1 - c), device_id_type=MESH)

        pairs = [(k, j) for k in range(len(jobs)) for j in range(3)]
        for k, j in pairs:
            copy(k, j, c).start()
        for k, j in pairs:
            copy(k, j, c).wait_send()
            copy(k, j, 1 - c).wait_recv()

    return pl.pallas_call(
        body, name=name, out_shape=[jax.ShapeDtypeStruct(g.shape, g.dtype) for g in gathered],
        in_specs=[_ANY] * n, out_specs=[_ANY] * n, input_output_aliases={i: i for i in range(n)},
        scratch_shapes=[pltpu.SemaphoreType.DMA((3 * len(jobs),)), pltpu.SemaphoreType.DMA((3 * len(jobs),))])(*gathered)


def _add4(own, got, name):
    rows, w = own.shape
    tr = _pick(rows, 128, 16)

    def body(a_ref, b_ref, o_ref):
        o_ref[...] = ((a_ref[...].astype(F32) + b_ref[0].astype(F32)) + b_ref[1].astype(F32)) + b_ref[2].astype(F32)

    return pl.pallas_call(
        body, name=name, grid=(rows // tr,),
        in_specs=[pl.BlockSpec((tr, w), lambda i: (i, 0)), pl.BlockSpec((3, tr, w), lambda i: (0, i, 0))],
        out_specs=pl.BlockSpec((tr, w), lambda i: (i, 0)), out_shape=jax.ShapeDtypeStruct((rows, w), F32),
        compiler_params=_params(_PAR))(own, got)


def _small_sync(gs, ws, ms, vs):
    rows = gs.shape[0]
    vmem = pl.BlockSpec(memory_space=pltpu.VMEM)

    def body(g_ref, w_ref, m_ref, v_ref, sum_ref, d_ref, m2_ref, v2_ref, buf, send_sems, recv_sems):
        x, y, c = _place()
        me = 4 * x + 2 * y + c
        buf[me] = g_ref[...]
        copies = []
        for k in range(1, 8):
            peer = (x ^ (k >> 2), y ^ ((k >> 1) & 1), c ^ (k & 1))
            copies.append(pltpu.make_async_remote_copy(
                src_ref=g_ref, dst_ref=buf.at[me], send_sem=send_sems.at[k - 1], recv_sem=recv_sems.at[k - 1],
                device_id=peer, device_id_type=MESH))
        for cp in copies:
            cp.start()
        for cp in copies:
            cp.wait()
        total = buf[0]
        for i in range(1, 8):
            total = total + buf[i]
        sum_ref[...] = total
        d, m2, v2 = _adam_math(w_ref[...], total, m_ref[...], v_ref[...])
        d_ref[...] = d
        m2_ref[...] = m2
        v2_ref[...] = v2

    shape = jax.ShapeDtypeStruct((rows, 128), F32)
    return pl.pallas_call(
        body, name="small_sync", out_shape=[shape] * 4, in_specs=[vmem] * 4, out_specs=[vmem] * 4,
        scratch_shapes=[pltpu.VMEM((8, rows, 128), F32), pltpu.SemaphoreType.DMA((7,)),
                        pltpu.SemaphoreType.DMA((7,))])(gs, ws, ms, vs)


_GROUPS = {
    "ffn1": dict(cols=("ffn1_w_in", 1408), rows=(("ffn1_w_out", 704, 704),), chunks=(8, 2)),
    "ffn2": dict(cols=("ffn2_w_in", 1408), rows=(("ffn2_w_out", 704, 704),), chunks=(8, 2)),
    "mixer": dict(cols=("w_in", 3080), chunks=(8, 4),
                  rows=(("w_branch_hgrn", 256, 256), ("w_branch_gdn", 512, 512), ("w_out", 256, 256),
                        ("gdn_conv_w", CONV_K, 128))),
}
_BIG_NAMES = tuple(n for g in _GROUPS.values() for n in (g["cols"][0],) + tuple(r[0] for r in g["rows"]))


def _group_names(group):
    return (group["cols"][0],) + tuple(r[0] for r in group["rows"])


def _pack(parts, lead, group):
    ax = len(lead)
    rows = []
    for n, r, padded in group["rows"]:
        p = parts[n]
        if padded != r:
            p = jnp.tile(p, (1,) * ax + (padded // r, 1))
        rows.append(p)
    return [parts[group["cols"][0]], rows[0] if len(rows) == 1 else jnp.concatenate(rows, axis=ax)]


def _unpack(cols, rows, group):
    out, off = {group["cols"][0]: cols}, 0
    for n, r, padded in group["rows"]:
        out[n] = rows[..., off:off + r, :]
        off += padded
    return out


def _is_col_sharded(name):
    return name in ("ffn1_w_in", "ffn2_w_in", "w_in", "gdn_conv_w")


def _full_from_shards(name, g):
    if _is_col_sharded(name):
        return jnp.transpose(g, (1, 0, 2)).reshape(g.shape[1], -1)
    return g.reshape(-1, g.shape[2])


def _shards_from_full(name, full):
    if _is_col_sharded(name):
        return jnp.transpose(full.reshape(full.shape[0], 4, -1), (1, 0, 2))
    return full.reshape(4, -1, full.shape[1])


_SMALL = (("ffn1_norm", 8), ("mix_norm", 8), ("hgrn_lb_logits", 16), ("hgrn_out_norm", 8), ("gdn_a_log", 8),
          ("gdn_dt_bias", 8), ("gdn_out_norm", 8), ("ffn2_norm", 8), ("final_norm", 8), ("loss", 8))
_SMALL_ROWS = sum(r for _, r in _SMALL)


def _pack_small(parts):
    out = []
    for name, rows in _SMALL:
        p = parts[name].reshape(-1).astype(F32)
        if p.shape[0] <= 128:
            if p.shape[0] < 128:
                p = jnp.concatenate([p, jnp.zeros((128 - p.shape[0],), F32)])
            p = jnp.broadcast_to(p.reshape(1, 128), (rows, 128))
        out.append(p.reshape(rows, 128))
    return jnp.concatenate(out, axis=0)


def _unpack_small(packed, shapes):
    out, off = {}, 0
    for name, rows in _SMALL:
        n = int(np.prod(shapes[name]))
        out[name] = packed[off:off + rows].reshape(-1)[:n].reshape(shapes[name])
        off += rows
    return out


def _ffn_fwd(x, gain, w_in, w_out, tag):
    n = _rmsnorm_fwd(x, gain, tag + "_norm")
    a, b, hm = _ffn_in_act(n, w_in, tag + "_in")
    out = _mm(hm, w_out, alpha=0.5, res=x, name=tag + "_out")
    return out, (n, a, b)


def _ffn_bwd(x, gain, w_in, w_out, saved, dout, dout_bf, tag):
    n, a, b = saved
    da, db, hm = _ffn_dact(dout_bf, w_out, a, b, tag + "_dact")
    dw_out = _mm(hm, dout_bf, ta=True, alpha=0.5, out_dtype=BF16, name=tag + "_dwout")
    dw_in = jnp.concatenate([_mm(n, da, ta=True, out_dtype=BF16, name=tag + "_dwin_a"),
                             _mm(n, db, ta=True, out_dtype=BF16, name=tag + "_dwin_b")], axis=1)
    dn = _mm(da, w_in[:, :D_FF], tb=True, name=tag + "_dnorm_a")
    dn = _mm(db, w_in[:, D_FF:], tb=True, res=dn, name=tag + "_dnorm_b")
    dx, dx_bf, dgain = _rmsnorm_bwd(x, gain, dn, dout, tag + "_dx")
    return dx, dx_bf, dgain, dw_in, dw_out


def _pad_lanes(v):
    return jnp.concatenate([v.reshape(1, -1), jnp.zeros((1, HEAD - v.size), F32)], axis=1)


def _local_step(x, tgt, small, exchange):
    hg_c = _hg_consts()
    gd_c = _gd_consts()
    alog = _pad_lanes(small["gdn_a_log"])
    dtb = _pad_lanes(small["gdn_dt_bias"])
    logits = small["hgrn_lb_logits"]
    hg_gain = small["hgrn_out_norm"].reshape(1, HEAD)
    gd_gain = small["gdn_out_norm"].reshape(1, HEAD)
    g1, gm, g2 = small["ffn1_norm"].reshape(1, -1), small["mix_norm"].reshape(1, -1), small["ffn2_norm"].reshape(1, -1)
    gf = small["final_norm"].reshape(1, -1)
    qscale = HEAD ** -0.5

    w1 = exchange.weights("ffn1")
    started = exchange.prefetch("mixer")
    h1, ffn1_saved = _ffn_fwd(x, g1 + started, w1["ffn1_w_in"], w1["ffn1_w_out"], "ffn1")
    u = _rmsnorm_fwd(h1, gm, "mix_norm")
    w = exchange.weights("mixer", after=u)
    started = exchange.prefetch("ffn2")
    seg, off = {}, 0
    for name, size in zip(IN_NAMES, IN_SIZES):
        seg[name] = w["w_in"][:, off:off + size]
        off += size
    w_gab = jnp.concatenate([seg["ga"], seg["gb"], jnp.zeros((D_MODEL, HEAD - 32), BF16)], axis=1)
    big_segs = [n for n in IN_NAMES if n not in ("ga", "gb")]
    conv8 = jnp.concatenate([w["gdn_conv_w"].astype(F32), jnp.zeros((8 - CONV_K, 4096), F32)], axis=0)
    conv_q, conv_k, conv_v = conv8[:, :1024], conv8[:, 1024:2048], conv8[:, 2048:]
    w_main = jnp.concatenate([seg[n] for n in big_segs], axis=1)
    proj = _mm(u, w_main, name="proj")
    pr, off = {}, 0
    for n in big_segs:
        pr[n] = _view(proj, off, seg[n].shape[1])
        off += seg[n].shape[1]
    gab = _mm(u, w_gab, name="proj_gab")
    oh_raw, oh, s_h = _hgrn_fwd(pr["hq"], pr["hf"], pr["hi"], pr["hg"], logits, hg_gain + started, hg_c)
    qn = _conv_fwd(pr["gq"], conv_q, qscale, "conv_q")
    kn = _conv_fwd(pr["gk"], conv_k, 1.0, "conv_k")
    cv = _conv_fwd(pr["gv"], conv_v, None, "conv_v")
    og_raw, og, s_g, t_g = _gdn_fwd(qn, kn, cv, gab, pr["gz"], alog, dtb, gd_gain, gd_c)
    yh = _mm(oh, w["w_branch_hgrn"], name="branch_h")
    yg = _mm(og, w["w_branch_gdn"], name="branch_g")
    ym = _merge_fwd(yh, yg, pr["gate_h"], pr["gate_g"])
    h2 = _mm(ym, w["w_out"], res=h1, name="mix_out")
    w2 = exchange.weights("ffn2", after=h2)
    h3, ffn2_saved = _ffn_fwd(h2, g2, w2["ffn2_w_in"], w2["ffn2_w_out"], "ffn2")
    loss, dh3, dh3_bf, d_gf = _final_loss(h3, gf, tgt)

    dh2, dh2_bf, d_g2, d_f2in, d_f2out = _ffn_bwd(h2, g2, w2["ffn2_w_in"], w2["ffn2_w_out"], ffn2_saved, dh3, dh3_bf,
                                                  "ffn2")
    started = exchange.reduce("ffn2", {"ffn2_w_in": d_f2in, "ffn2_w_out": d_f2out}, behind=True)
    dym =_mm(dh2_bf, w["w_out"], tb=True, name="d_merge")
    d_wout = _mm(ym, dh2_bf, ta=True, out_dtype=BF16, name="d_w_out")
    dyh, dyg, d_gate_h, d_gate_g = _merge_bwd(dym, yh, yg, pr["gate_h"], pr["gate_g"])
    d_wbh = _mm(oh, dyh, ta=True, out_dtype=BF16, name="d_w_branch_h")
    d_wbg = _mm(og, dyg, ta=True, out_dtype=BF16, name="d_w_branch_g")
    doh = _mm(dyh, w["w_branch_hgrn"], tb=True, name="d_oh")
    dog = _mm(dyg, w["w_branch_gdn"], tb=True, name="d_og")
    d_hq, d_hf, d_hi, d_hg, d_hg_gain, d_lb0 = _hgrn_bwd(pr["hq"], pr["hf"], pr["hi"], pr["hg"], logits,
                                                        hg_gain + started, oh_raw, s_h, doh, hg_c)
    d_qn, d_kn, d_cv, d_gab_wide, d_gz, gd_small = _gdn_bwd(qn, kn, cv, gab, pr["gz"], alog, dtb, gd_gain, og_raw,
                                                            s_g, t_g, dog, gd_c)
    d_gab = _fold_groups(d_gab_wide)
    dc_q, dwc_q = _conv_bwd_a(pr["gq"], conv_q, d_qn, qscale, "dconv_q")
    dc_k, dwc_k = _conv_bwd_a(pr["gk"], conv_k, d_kn, 1.0, "dconv_k")
    dc_v, dwc_v = _conv_bwd_a(pr["gv"], conv_v, d_cv, None, "dconv_v")
    d_gq = _conv_bwd_b(dc_q, conv_q, "dconvx_q")
    d_gk = _conv_bwd_b(dc_k, conv_k, "dconvx_k")
    d_gv = _conv_bwd_b(dc_v, conv_v, "dconvx_v")
    dpr = {"hq": d_hq, "hf": d_hf, "hi": d_hi, "hg": d_hg, "gq": d_gq, "gk": d_gk, "gv": d_gv, "gz": d_gz,
           "gate_h": d_gate_h, "gate_g": d_gate_g}
    dproj = jnp.concatenate([dpr[n] for n in big_segs], axis=1)
    du = _mm(d_gab, w_gab, tb=True, name="du_gab")
    du = _mm(dproj, w_main, tb=True, res=du, name="du")
    d_wmain = _mm(u, dproj, ta=True, out_dtype=BF16, name="dw_main")
    d_wgab = _mm(u, d_gab, ta=True, out_dtype=BF16, name="dw_gab")
    d_win = jnp.concatenate([d_wmain[:, :8192], d_wgab[:, :32], d_wmain[:, 8192:]], axis=1)
    d_conv = jnp.concatenate([dwc_q[:CONV_K], dwc_k[:CONV_K], dwc_v[:CONV_K]], axis=1).astype(BF16)
    started = exchange.reduce("mixer", {"w_in": d_win, "gdn_conv_w": d_conv, "w_branch_hgrn": d_wbh,
                                        "w_branch_gdn": d_wbg, "w_out": d_wout}, behind=True)
    dh1, dh1_bf, d_gm = _rmsnorm_bwd(h1, gm + started, du, dh2, "mix_dnorm")
    dx, _, d_g1, d_f1in, d_f1out = _ffn_bwd(x, g1, w1["ffn1_w_in"], w1["ffn1_w_out"], ffn1_saved, dh1, dh1_bf, "ffn1")
    exchange.reduce("ffn1", {"ffn1_w_in": d_f1in, "ffn1_w_out": d_f1out}, behind=True)
    d_lb0 = d_lb0.reshape(1, -1)
    sm = {"ffn1_norm": d_g1, "mix_norm": d_gm, "hgrn_lb_logits": jnp.concatenate([d_lb0, -d_lb0], axis=0),
          "hgrn_out_norm": d_hg_gain, "gdn_a_log": gd_small[2, :16], "gdn_dt_bias": gd_small[1, :16],
          "gdn_out_norm": gd_small[0], "ffn2_norm": d_g2, "final_norm": d_gf, "loss": loss[0, :1]}
    return dx, sm


class _Exchange:
    def __init__(self, wts):
        self.wts = wts
        xi, yi, ci = _place()
        self.chip = 2 * xi + yi
        self.south = ci == 0
        self.core = ci.reshape(1).astype(jnp.int32)
        self.mine = {}
        self.coming = {}
        self.going = {}

    def _packs(self, tag):
        group = _GROUPS[tag]
        return _pack({n: self.wts[n][0].astype(BF16) for n in _group_names(group)}, (), group)

    def prefetch(self, tag):
        packs = self._packs(tag)
        handle, token = _ici_start(packs, _GROUPS[tag]["chunks"], "gather", "gather_start_" + tag)
        self.coming[tag] = handle
        return token[0:1, 0:1]

    def weights(self, tag, after=None):
        group = _GROUPS[tag]
        if tag in self.coming:
            packs, halves = _ici_wait(self.coming.pop(tag), after, "gather_wait_" + tag)
            others = _pass_to_sibling(halves, group["chunks"], "gather_pass_" + tag)
        else:
            packs = self._packs(tag)
            others = _gather_weights(packs, group["chunks"], "gather_" + tag)
        whole = [lax.dynamic_update_index_in_dim(g, p, self.chip, 0) for g, p in zip(others, packs)]
        gathered = _unpack(*whole, group)
        return {n: _full_from_shards(n, gathered[n]) for n in _group_names(group)}

    def reduce(self, tag, grads, behind=False):
        group = _GROUPS[tag]
        gpacks = _pack({n: _shards_from_full(n, grads[n]) for n in _group_names(group)}, (4,), group)
        got = _swap_with_sibling(gpacks, group["chunks"], 4, "reduce_pair_" + tag, halves=True)
        sums = [_add2(a, b, self.core, "add_pair_%s_%d" % (tag, i)) for i, (a, b) in enumerate(zip(gpacks, got))]
        if behind:
            handle, token = _ici_start(sums, group["chunks"], "reduce", "reduce_start_" + tag)
            self.going[tag] = handle
            self.token = token
            return token[0:1, 0:1]
        self._add_chips(tag, sums, _reduce_chips(sums, group["chunks"], "reduce_chips_" + tag))
        return None

    def _add_chips(self, tag, sums, from_chips):
        self.mine[tag] = [_add4(lax.dynamic_index_in_dim(s, self.chip, axis=0, keepdims=False), f,
                                "add_chips_%s_%d" % (tag, i)) for i, (s, f) in enumerate(zip(sums, from_chips))]

    def finish(self, tags, after):
        for tag in tags:
            if tag in self.going:
                self._add_chips(tag, *_ici_wait(self.going.pop(tag), after, "reduce_wait_" + tag))
        mine = [a for t in tags for a in self.mine[t]]
        nchs = [k for t in tags for k in _GROUPS[t]["chunks"]]
        theirs = _swap_with_sibling(mine, nchs, 0, "share_pair_" + tags[0])
        whole = [jnp.concatenate([jnp.where(self.south, a, b), jnp.where(self.south, b, a)], axis=0)
                 for a, b in zip(mine, theirs)]
        reduced = {}
        for i, t in enumerate(tags):
            reduced.update(_unpack(whole[2 * i], whole[2 * i + 1], _GROUPS[t]))
        return reduced


_WEIGHTS = ("ffn1_norm", "ffn1_w_in", "ffn1_w_out", "mix_norm", "w_in", "hgrn_lb_logits", "hgrn_out_norm",
            "gdn_conv_w", "gdn_a_log", "gdn_dt_bias", "gdn_out_norm", "w_branch_hgrn", "w_branch_gdn", "w_out",
            "ffn2_norm", "ffn2_w_in", "ffn2_w_out", "final_norm")


def kernel(x, ffn1_norm, ffn1_w_in, ffn1_w_out, mix_norm, w_in, hgrn_lb_logits, hgrn_out_norm, gdn_conv_w, gdn_a_log, gdn_dt_bias, gdn_out_norm, w_branch_hgrn, w_branch_gdn, w_out, ffn2_norm, ffn2_w_in, ffn2_w_out, final_norm, loss_target, m_ffn1_norm, m_ffn1_w_in, m_ffn1_w_out, m_mix_norm, m_w_in, m_hgrn_lb_logits, m_hgrn_out_norm, m_gdn_conv_w, m_gdn_a_log, m_gdn_dt_bias, m_gdn_out_norm, m_w_branch_hgrn, m_w_branch_gdn, m_w_out, m_ffn2_norm, m_ffn2_w_in, m_ffn2_w_out, m_final_norm, v_ffn1_norm, v_ffn1_w_in, v_ffn1_w_out, v_mix_norm, v_w_in, v_hgrn_lb_logits, v_hgrn_out_norm, v_gdn_conv_w, v_gdn_a_log, v_gdn_dt_bias, v_gdn_out_norm, v_w_branch_hgrn, v_w_branch_gdn, v_w_out, v_ffn2_norm, v_ffn2_w_in, v_ffn2_w_out, v_final_norm):
    args = dict(locals())
    wts = {n: args[n] for n in _WEIGHTS}
    moms = {n: args["m_" + n] for n in _WEIGHTS}
    vars_ = {n: args["v_" + n] for n in _WEIGHTS}

    small = {n: wts[n].astype(F32) for n in _WEIGHTS if n not in _BIG_NAMES}
    exchange = _Exchange(wts)
    dx, small_grads = _local_step(x[0], loss_target[0], small, exchange)

    out_g, out_d, out_m, out_v = {}, {}, {}, {}

    def update(tags, reduced, after):
        for t in tags:
            for n in _group_names(_GROUPS[t]):
                shape = wts[n].shape
                w2 = wts[n].reshape(shape[-2], shape[-1])
                g2 = reduced[n]
                d, m2, v2 = _adamw(w2, g2, moms[n].reshape(w2.shape), vars_[n].reshape(w2.shape), "adamw_" + n, after)
                out_g[n], out_d[n], out_m[n], out_v[n] = (g2.reshape(shape), d.reshape(shape), m2.reshape(shape),
                                                          v2.reshape(shape))
                after = v2
        return after

    done = update(("ffn2", "mixer"), exchange.finish(("ffn2", "mixer"), after=dx), exchange.token)
    update(("ffn1",), exchange.finish(("ffn1",), after=done), None)

    small_names = [n for n, _ in _SMALL]
    zero = jnp.zeros((1,), F32)
    shapes = {n: (wts[n].shape if n != "loss" else (1,)) for n in small_names}
    sums, sd, sm_, sv = _small_sync(
        _pack_small(small_grads),
        _pack_small({n: (wts[n] if n != "loss" else zero) for n in small_names}),
        _pack_small({n: (moms[n] if n != "loss" else zero) for n in small_names}),
        _pack_small({n: (vars_[n] if n != "loss" else zero) for n in small_names}))
    sg_u, sd_u, sm_u, sv_u = (_unpack_small(p, shapes) for p in (sums, sd, sm_, sv))
    for n in small_names:
        if n != "loss":
            out_g[n], out_d[n], out_m[n], out_v[n] = sg_u[n], sd_u[n], sm_u[n], sv_u[n]
    loss = sg_u["loss"].reshape(())

    return (loss, dx[None], *[out_g[n] for n in _WEIGHTS], *[out_d[n] for n in _WEIGHTS],
            *[out_m[n] for n in _WEIGHTS], *[out_v[n] for n in _WEIGHTS])
```

```python
import numpy as np

import jax
import jax.numpy as jnp
from jax import lax
from jax.experimental import pallas as pl
from jax.experimental.pallas import tpu as pltpu

F32 = jnp.float32
BF16 = jnp.bfloat16

D_MODEL = 1024
D_FF = 2816
CHUNK = 64
HEAD = 128
HG_HEADS = 8
GD_HEADS = 16
HPS = 8
COMM_CHUNKS = 9
MM_TM = 1408
MM_TN = 1024
MM_TK = 1536
VMEM_LIMIT = 48 * 1024 * 1024
EPS = 1e-6
CONV_K = 4
IN_NAMES = ("hq", "hf", "hi", "hg", "gq", "gk", "gv", "ga", "gb", "gz", "gate_h", "gate_g")
IN_SIZES = (1024, 1024, 1024, 1024, 1024, 1024, 2048, 16, 16, 2048, 1024, 1024)
IN_WIDTH = sum(IN_SIZES)

ADAM_LR = 0.001
ADAM_B1 = 0.9
ADAM_B2 = 0.999
ADAM_EPS = 1e-08
ADAM_WD = 0.01
ADAM_STEP = 10

MESH = pl.DeviceIdType.MESH
_ARB = "arbitrary"
_PAR = "parallel"


def _bf(x):
    return x.astype(BF16)


def _dot(a, b):
    return jnp.dot(_bf(a), _bf(b), preferred_element_type=F32)


def _dot_nt(a, b):
    return lax.dot_general(_bf(a), _bf(b), (((1,), (1,)), ((), ())), preferred_element_type=F32)


def _dot_tn(a, b):
    return lax.dot_general(_bf(a), _bf(b), (((0,), (0,)), ((), ())), preferred_element_type=F32)


def _split3(x):
    hi = _bf(x)
    r = x - hi.astype(F32)
    mid = _bf(r)
    lo = _bf(r - mid.astype(F32))
    return hi, mid, lo


def _dot_mx(m, x):
    hi, mid, lo = _split3(x)
    return (jnp.dot(m, hi, preferred_element_type=F32) + jnp.dot(m, mid, preferred_element_type=F32)
            + jnp.dot(m, lo, preferred_element_type=F32))


def _dot_xm(x, m):
    hi, mid, lo = _split3(x)
    return (jnp.dot(hi, m, preferred_element_type=F32) + jnp.dot(mid, m, preferred_element_type=F32)
            + jnp.dot(lo, m, preferred_element_type=F32))


def _dot_hp(a, b):
    ah = _bf(a)
    al = _bf(a - ah.astype(F32))
    bh = _bf(b)
    bl = _bf(b - bh.astype(F32))
    return (jnp.dot(ah, bh, preferred_element_type=F32) + jnp.dot(ah, bl, preferred_element_type=F32)
            + jnp.dot(al, bh, preferred_element_type=F32))


def _sigmoid(x):
    return jax.nn.sigmoid(x)


def _silu(x):
    return x * _sigmoid(x)


def _dsilu(x):
    s = _sigmoid(x)
    return s * (1.0 + x * (1.0 - s))


def _softplus(x):
    return jnp.maximum(x, 0.0) + jnp.log(1.0 + jnp.exp(-jnp.abs(x)))


def _rowsum(x):
    return jnp.sum(x, axis=1, keepdims=True)


def _col_to_row(col, eye):
    return jnp.sum(eye * col, axis=0, keepdims=True)


def _row_to_col(row, eye):
    return jnp.sum(eye * row, axis=1, keepdims=True)


def _pick(dim, pref, unit=128):
    if dim <= pref:
        return dim
    t = pref
    while t >= unit:
        if dim % t == 0:
            return t
        t -= unit
    return dim


def _params(*sem):
    return pltpu.CompilerParams(dimension_semantics=tuple(sem), vmem_limit_bytes=VMEM_LIMIT)


def _mm(a, b, *, ta=False, tb=False, alpha=1.0, res=None, out_dtype=F32, name="mm"):
    m = a.shape[1] if ta else a.shape[0]
    k = a.shape[0] if ta else a.shape[1]
    n = b.shape[0] if tb else b.shape[1]
    assert k == (b.shape[1] if tb else b.shape[0])
    tm, tn, tk = _pick(m, MM_TM), _pick(n, MM_TN), _pick(k, MM_TK)
    if tn < MM_TN < n and n % MM_TM == 0:
        tn = MM_TM
    nk = k // tk
    a_spec = pl.BlockSpec((tk, tm), lambda i, j, l: (l, i)) if ta else pl.BlockSpec((tm, tk), lambda i, j, l: (i, l))
    b_spec = pl.BlockSpec((tn, tk), lambda i, j, l: (j, l)) if tb else pl.BlockSpec((tk, tn), lambda i, j, l: (l, j))
    o_spec = pl.BlockSpec((tm, tn), lambda i, j, l: (i, j))
    dims = (((0 if ta else 1,), (1 if tb else 0,)), ((), ()))
    has_res = res is not None

    def finish(r, r_ref, o_ref):
        if alpha != 1.0:
            r = r * alpha
        if has_res:
            r = r + r_ref[...]
        o_ref[...] = r.astype(out_dtype)

    def body(*refs):
        a_ref, b_ref = refs[0], refs[1]
        r_ref = refs[2] if has_res else None
        o_ref = refs[3] if has_res else refs[2]
        part = lax.dot_general(_bf(a_ref[...]), _bf(b_ref[...]), dims, preferred_element_type=F32)
        if nk == 1:
            finish(part, r_ref, o_ref)
            return
        acc = refs[-1]
        step = pl.program_id(2)

        @pl.when(step == 0)
        def _():
            acc[...] = part

        @pl.when(step != 0)
        def _():
            acc[...] += part

        @pl.when(step == nk - 1)
        def _():
            finish(acc[...], r_ref, o_ref)

    ins = [a, b] + ([res] if has_res else [])
    in_specs = [a_spec, b_spec] + ([o_spec] if has_res else [])
    return pl.pallas_call(
        body, name=name, grid=(m // tm, n // tn, nk), in_specs=in_specs, out_specs=o_spec,
        out_shape=jax.ShapeDtypeStruct((m, n), out_dtype),
        scratch_shapes=[pltpu.VMEM((tm, tn), F32)] if nk > 1 else [],
        compiler_params=_params(_PAR, _PAR, _ARB))(*ins)


def _row_spec(tr, w):
    return pl.BlockSpec((tr, w), lambda i: (i, 0))


def _full_spec(shape):
    return pl.BlockSpec(shape, lambda i: tuple(0 for _ in shape))


def _view(arr, off, width):
    return arr, off, width


def _view_rows(view, tr):
    _, off, width = view
    assert off % width == 0
    return pl.BlockSpec((tr, width), lambda i: (i, off // width))


def _view_tile(view, rows, bw, cidx=lambda c: c):
    _, off, width = view
    assert off % bw == 0 and width % bw == 0
    return pl.BlockSpec((rows, bw), lambda c, g: (cidx(c), off // bw + g))


def _rmsnorm_fwd(x, g, name):
    t, d = x.shape
    tr = _pick(t, 256, 8)

    def body(x_ref, g_ref, o_ref):
        xv = x_ref[...]
        r = lax.rsqrt(jnp.mean(xv * xv, axis=1, keepdims=True) + EPS)
        o_ref[...] = (xv * r * g_ref[...]).astype(BF16)

    return pl.pallas_call(
        body, name=name, grid=(t // tr,), in_specs=[_row_spec(tr, d), _full_spec((1, d))],
        out_specs=_row_spec(tr, d), out_shape=jax.ShapeDtypeStruct((t, d), BF16),
        compiler_params=_params(_PAR))(x, g)


def _rmsnorm_bwd(x, g, dn, res, name):
    t, d = x.shape
    tr = _pick(t, 256, 8)

    def body(x_ref, g_ref, dn_ref, r_ref, dx_ref, dxb_ref, dg_ref):
        @pl.when(pl.program_id(0) == 0)
        def _():
            dg_ref[...] = jnp.zeros_like(dg_ref)

        xv = x_ref[...]
        r = lax.rsqrt(jnp.mean(xv * xv, axis=1, keepdims=True) + EPS)
        xh = xv * r
        dy = dn_ref[...]
        dg_ref[...] += jnp.sum(dy * xh, axis=0, keepdims=True)
        dxh = dy * g_ref[...]
        dx = r_ref[...] + r * (dxh - xh * jnp.mean(dxh * xh, axis=1, keepdims=True))
        dx_ref[...] = dx
        dxb_ref[...] = dx.astype(BF16)

    return pl.pallas_call(
        body, name=name, grid=(t // tr,),
        in_specs=[_row_spec(tr, d), _full_spec((1, d)), _row_spec(tr, d), _row_spec(tr, d)],
        out_specs=[_row_spec(tr, d), _row_spec(tr, d), _full_spec((1, d))],
        out_shape=[jax.ShapeDtypeStruct((t, d), F32), jax.ShapeDtypeStruct((t, d), BF16),
                   jax.ShapeDtypeStruct((1, d), F32)],
        compiler_params=_params(_ARB))(x, g, dn, res)


FFN_TN = 256


def _ffn_in_act(n, w_in, name):
    t, d = n.shape
    tm = _pick(t, MM_TM)
    nf = D_FF // FFN_TN

    def body(n_ref, wa_ref, wb_ref, a_ref, b_ref, hm_ref):
        nv = n_ref[...]
        a = jnp.dot(nv, wa_ref[...], preferred_element_type=F32)
        b = jnp.dot(nv, wb_ref[...], preferred_element_type=F32)
        a_ref[...] = a.astype(BF16)
        b_ref[...] = b.astype(BF16)
        hm_ref[...] = (_silu(a) * b).astype(BF16)

    tile = pl.BlockSpec((tm, FFN_TN), lambda i, j: (i, j))
    return pl.pallas_call(
        body, name=name, grid=(t // tm, nf),
        in_specs=[pl.BlockSpec((tm, d), lambda i, j: (i, 0)), pl.BlockSpec((d, FFN_TN), lambda i, j: (0, j)),
                  pl.BlockSpec((d, FFN_TN), lambda i, j: (0, nf + j))],
        out_specs=[tile, tile, tile], out_shape=[jax.ShapeDtypeStruct((t, D_FF), BF16)] * 3,
        compiler_params=_params(_PAR, _PAR))(n, w_in, w_in)


def _ffn_dact(dout, w_out, a, b, name):
    t, d = dout.shape
    tm = _pick(t, MM_TM)

    def body(do_ref, w_ref, a_ref, b_ref, da_ref, db_ref, hm_ref):
        dh = 0.5 * _dot_nt(do_ref[...], w_ref[...])
        av = a_ref[...].astype(F32)
        bv = b_ref[...].astype(F32)
        sa = _silu(av)
        da_ref[...] = (dh * bv * _dsilu(av)).astype(BF16)
        db_ref[...] = (dh * sa).astype(BF16)
        hm_ref[...] = (sa * bv).astype(BF16)

    tile = pl.BlockSpec((tm, FFN_TN), lambda i, j: (i, j))
    return pl.pallas_call(
        body, name=name, grid=(t // tm, D_FF // FFN_TN),
        in_specs=[pl.BlockSpec((tm, d), lambda i, j: (i, 0)), pl.BlockSpec((FFN_TN, d), lambda i, j: (j, 0)), tile, tile],
        out_specs=[tile, tile, tile], out_shape=[jax.ShapeDtypeStruct((t, D_FF), BF16)] * 3,
        compiler_params=_params(_PAR, _PAR))(dout, w_out, a, b)


def _merge_fwd(yh, yg, gh, gg):
    t, d = yh.shape
    tr = _pick(t, 256, 8)

    def body(yh_ref, yg_ref, gh_ref, gg_ref, o_ref):
        o_ref[...] = (_sigmoid(gh_ref[...]) * yh_ref[...] + _sigmoid(gg_ref[...]) * yg_ref[...]).astype(BF16)

    return pl.pallas_call(
        body, name="merge_fwd", grid=(t // tr,),
        in_specs=[_row_spec(tr, d), _row_spec(tr, d), _view_rows(gh, tr), _view_rows(gg, tr)],
        out_specs=_row_spec(tr, d),
        out_shape=jax.ShapeDtypeStruct((t, d), BF16), compiler_params=_params(_PAR))(yh, yg, gh[0], gg[0])


def _merge_bwd(dy, yh, yg, gh, gg):
    t, d = yh.shape
    tr = _pick(t, 256, 8)

    def body(dy_ref, yh_ref, yg_ref, gh_ref, gg_ref, dyh_ref, dyg_ref, dgh_ref, dgg_ref):
        dyv = dy_ref[...]
        sh = _sigmoid(gh_ref[...])
        sg = _sigmoid(gg_ref[...])
        dyh_ref[...] = (dyv * sh).astype(BF16)
        dyg_ref[...] = (dyv * sg).astype(BF16)
        dgh_ref[...] = (dyv * yh_ref[...] * sh * (1.0 - sh)).astype(BF16)
        dgg_ref[...] = (dyv * yg_ref[...] * sg * (1.0 - sg)).astype(BF16)

    return pl.pallas_call(
        body, name="merge_bwd", grid=(t // tr,),
        in_specs=[_row_spec(tr, d)] * 3 + [_view_rows(gh, tr), _view_rows(gg, tr)], out_specs=[_row_spec(tr, d)] * 4,
        out_shape=[jax.ShapeDtypeStruct((t, d), BF16)] * 4,
        compiler_params=_params(_PAR))(dy, yh, yg, gh[0], gg[0])


def _final_loss(h, g, tgt):
    t, d = h.shape
    tr = _pick(t, 256, 8)

    def body(h_ref, g_ref, t_ref, loss_ref, dh_ref, dhb_ref, dg_ref):
        @pl.when(pl.program_id(0) == 0)
        def _():
            dg_ref[...] = jnp.zeros_like(dg_ref)
            loss_ref[...] = jnp.zeros_like(loss_ref)

        xv = h_ref[...]
        gv = g_ref[...]
        r = lax.rsqrt(jnp.mean(xv * xv, axis=1, keepdims=True) + EPS)
        xh = xv * r
        err = xh * gv - t_ref[...]
        loss_ref[...] += 0.5 * jnp.sum(jnp.mean(err * err, axis=1, keepdims=True), axis=0, keepdims=True)
        dy = err * (1.0 / d)
        dg_ref[...] += jnp.sum(dy * xh, axis=0, keepdims=True)
        dxh = dy * gv
        dh = r * (dxh - xh * jnp.mean(dxh * xh, axis=1, keepdims=True))
        dh_ref[...] = dh
        dhb_ref[...] = dh.astype(BF16)

    return pl.pallas_call(
        body, name="final_loss", grid=(t // tr,),
        in_specs=[_row_spec(tr, d), _full_spec((1, d)), _row_spec(tr, d)],
        out_specs=[_full_spec((1, 128)), _row_spec(tr, d), _row_spec(tr, d), _full_spec((1, d))],
        out_shape=[jax.ShapeDtypeStruct((1, 128), F32), jax.ShapeDtypeStruct((t, d), F32),
                   jax.ShapeDtypeStruct((t, d), BF16), jax.ShapeDtypeStruct((1, d), F32)],
        compiler_params=_params(_ARB))(h, g, tgt)


def _hg_consts():
    c = CHUNK
    t = np.arange(c)
    mats, masks = [], []
    for lvl in range(6):
        m = 1 << lvl
        blk = t // m
        mat = np.zeros((c, c), np.float32)
        for tt in range(c):
            b = blk[tt]
            if b % 2 == 1:
                mat[tt, b * m:tt + 1] = 1.0
            else:
                mat[tt, tt + 1:(b + 1) * m] = 1.0
        mats.append(mat)
        same = (t[:, None] // (2 * m)) == (t[None, :] // (2 * m))
        masks.append((same & (blk[:, None] % 2 == 1) & (blk[None, :] % 2 == 0)).astype(np.float32))
    pre = np.tril(np.ones((c, c), np.float32))
    suf = np.triu(np.ones((c, c), np.float32), 1)
    mstack = np.concatenate(mats + [pre, suf], 0)
    masks.append(np.eye(c, dtype=np.float32))
    return (jnp.asarray(mstack, BF16), jnp.asarray(mstack.T.copy(), BF16), jnp.asarray(np.stack(masks), F32),
            jnp.asarray(np.eye(HEAD, dtype=np.float32)))


def _gd_consts():
    c = CHUNK
    incl = np.tril(np.ones((c, c), np.float32))
    strict = np.tril(np.ones((c, c), np.float32), -1)
    eye = np.eye(c, dtype=np.float32)
    masks = np.stack([incl, strict, eye, incl.T.copy()])
    sel = np.zeros((GD_HEADS, HEAD, 2 * HEAD), np.float32)
    for j in range(GD_HEADS):
        sel[j, j, :HEAD] = 1.0
        sel[j, GD_HEADS + j, HEAD:] = 1.0
    return (jnp.asarray(incl, BF16), jnp.asarray(incl.T.copy(), BF16), jnp.asarray(masks, F32), jnp.asarray(sel, BF16))


def _chunks_per_step(nc):
    for cb in (32 // HPS, 2, 1):
        if nc % cb == 0:
            return cb
    return 1


def _hg_prep(hq, hf, lg):
    lb = _sigmoid(lg[0:1, :] - lg[1:2, :])
    sg = _sigmoid(hf)
    sgn = _sigmoid(-hf)
    f = lb + (1.0 - lb) * sg
    lf = jnp.log(f)
    kk = (1.0 - lb) * sgn
    q = _silu(hq) * (HEAD ** -0.5)
    return lb, sg, sgn, f, lf, kk, q


def _mx_each(m, xs):
    wide = [jnp.concatenate(_split3(x), axis=1) for x in xs]
    prods = [jnp.dot(m, w, preferred_element_type=F32) for w in wide]
    return [p[:, :HEAD] + p[:, HEAD:2 * HEAD] + p[:, 2 * HEAD:] for p in prods]


def _hg_scores(q, kk, ex, mask_ref):
    p = [mask_ref[6] * _rowsum(a * b) for a, b in zip(q, kk)]
    for lvl in range(6):
        el = [e[lvl * CHUNK:(lvl + 1) * CHUNK] for e in ex]
        d = [_dot_nt(a * e, b * e) for a, b, e in zip(q, kk, el)]
        p = [x + mask_ref[lvl] * y for x, y in zip(p, d)]
    return p


def _hgrn_fwd(hq, hf, hi, hg, logits, gain, consts):
    t = hq[0].shape[0]
    nc = t // CHUNK
    cb = _chunks_per_step(nc)
    rows = cb * CHUNK
    mstack, _, masks, eye = consts
    tile = pl.BlockSpec((rows, HPS * HEAD), lambda c, g: (c, g))

    def body(hq_ref, hf_ref, hi_ref, hg_ref, lg_ref, gain_ref, m_ref, mask_ref, eye_ref,
             oraw_ref, og_ref, ssave_ref, state):
        c = pl.program_id(0)
        g = pl.program_id(1)

        @pl.when(c == 0)
        def _():
            for hh in range(HPS):
                state[g * HPS + hh] = jnp.zeros((HEAD, HEAD), F32)

        lg_all = lg_ref[...]
        gain_v = gain_ref[...]

        def one(i, carry):
            sl = pl.ds(pl.multiple_of(i * CHUNK, CHUNK), CHUNK)
            hs = range(HPS)
            heads = [g * HPS + hh for hh in hs]
            ln = [slice(hh * HEAD, (hh + 1) * HEAD) for hh in hs]
            preps = [_hg_prep(hq_ref[sl, s], hf_ref[sl, s], lg_all[:, s]) for s in ln]
            lf, kk, q = [p[4] for p in preps], [p[5] for p in preps], [p[6] for p in preps]
            v = [hi_ref[sl, s] for s in ln]
            ex = [jnp.exp(x) for x in _mx_each(m_ref[...], lf)]
            eb = [e[6 * CHUNK:7 * CHUNK] for e in ex]
            esfx = [e[7 * CHUNK:8 * CHUNK] for e in ex]
            p = _hg_scores(q, kk, ex, mask_ref)
            s0 = [state[h] for h in heads]
            o = _each(lambda a, e, s, pp, vv: _dot(a * e, s) + _dot(pp, vv), q, eb, s0, p, v)
            eye_v = eye_ref[...]
            s1 = _each(lambda s, e, kx, ef, vv: s * _row_to_col(e[CHUNK - 1:CHUNK, :], eye_v) + _dot_tn(kx * ef, vv),
                       s0, eb, kk, esfx, v)
            for hh in hs:
                ssave_ref[i, hh] = s0[hh]
                state[heads[hh]] = s1[hh]
                oraw_ref[sl, ln[hh]] = o[hh]
                r = lax.rsqrt(jnp.mean(o[hh] * o[hh], axis=1, keepdims=True) + EPS)
                og_ref[sl, ln[hh]] = (o[hh] * r * gain_v * _silu(hg_ref[sl, ln[hh]])).astype(BF16)
            return carry

        lax.fori_loop(0, cb, one, 0, unroll=2)

    return pl.pallas_call(
        body, name="hgrn_fwd", grid=(nc // cb, HG_HEADS // HPS),
        in_specs=[_view_tile(v, rows, HPS * HEAD) for v in (hq, hf, hi, hg)] + [
                  pl.BlockSpec((2, HPS * HEAD), lambda c, g: (0, g)),
                  pl.BlockSpec((1, HEAD), lambda c, g: (0, 0)),
                  pl.BlockSpec(mstack.shape, lambda c, g: (0, 0)),
                  pl.BlockSpec(masks.shape, lambda c, g: (0, 0, 0)),
                  pl.BlockSpec(eye.shape, lambda c, g: (0, 0))],
        out_specs=[tile, tile, pl.BlockSpec((cb, HPS, HEAD, HEAD), lambda c, g: (c, g, 0, 0))],
        out_shape=[jax.ShapeDtypeStruct((t, HG_HEADS * HEAD), F32), jax.ShapeDtypeStruct((t, HG_HEADS * HEAD), BF16),
                   jax.ShapeDtypeStruct((nc, HG_HEADS, HEAD, HEAD), F32)],
        scratch_shapes=[pltpu.VMEM((HG_HEADS, HEAD, HEAD), F32)],
        compiler_params=_params(_ARB, _ARB))(hq[0], hf[0], hi[0], hg[0], logits, gain, mstack, masks, eye)


def _hgrn_bwd(hq, hf, hi, hg, logits, gain, oraw, ssave, dog, consts):
    t = hq[0].shape[0]
    nc = t // CHUNK
    cb = _chunks_per_step(nc)
    rows = cb * CHUNK
    nb = nc // cb
    mstack, mstack_t, masks, eye = consts
    tile = pl.BlockSpec((rows, HPS * HEAD), lambda c, g: (nb - 1 - c, g))

    def body(hq_ref, hf_ref, hi_ref, hg_ref, lg_ref, gain_ref, oraw_ref, ssave_ref, dog_ref, m_ref, mt_ref,
             mask_ref, eye_ref, dhq_ref, dhf_ref, dhi_ref, dhg_ref, dgain_ref, dlb_ref, dstate):
        c = pl.program_id(0)
        g = pl.program_id(1)

        @pl.when(c == 0)
        def _():
            for hh in range(HPS):
                dstate[g * HPS + hh] = jnp.zeros((HEAD, HEAD), F32)

        @pl.when((c == 0) & (g == 0))
        def _():
            dgain_ref[...] = jnp.zeros_like(dgain_ref)
            dlb_ref[...] = jnp.zeros_like(dlb_ref)

        lg_all = lg_ref[...]
        gain_v = gain_ref[...]
        eye_v = eye_ref[...]
        last_row = (lax.broadcasted_iota(jnp.int32, (CHUNK, HEAD), 0) == CHUNK - 1).astype(F32)

        def one(j, carry):
            i = cb - 1 - j
            sl = pl.ds(pl.multiple_of(i * CHUNK, CHUNK), CHUNK)
            hs = range(HPS)
            heads = [g * HPS + hh for hh in hs]
            ln = [slice(hh * HEAD, (hh + 1) * HEAD) for hh in hs]
            hqv = [hq_ref[sl, s] for s in ln]
            hgv = [hg_ref[sl, s] for s in ln]
            preps = [_hg_prep(a, hf_ref[sl, s], lg_all[:, s]) for a, s in zip(hqv, ln)]
            lb, sg, sgn, f, lf, kk, q = ([p[n] for p in preps] for n in range(7))
            v = [hi_ref[sl, s] for s in ln]
            ex = [jnp.exp(x) for x in _mx_each(m_ref[...], lf)]
            eb = [e[6 * CHUNK:7 * CHUNK] for e in ex]
            esfx = [e[7 * CHUNK:8 * CHUNK] for e in ex]
            p = _hg_scores(q, kk, ex, mask_ref)
            s0 = [ssave_ref[i, hh] for hh in hs]
            ds = [dstate[h] for h in heads]

            o = [oraw_ref[sl, s] for s in ln]
            r = [lax.rsqrt(jnp.mean(x * x, axis=1, keepdims=True) + EPS) for x in o]
            on = _each(lambda x, y: x * y, o, r)
            dg_out = [dog_ref[sl, s] for s in ln]
            sgate = [_silu(x) for x in hgv]
            for hh in hs:
                dhg_ref[sl, ln[hh]] = (dg_out[hh] * on[hh] * gain_v * _dsilu(hgv[hh])).astype(BF16)
            dgain_ref[...] += sum(jnp.sum(d * s * n, axis=0, keepdims=True) for d, s, n in zip(dg_out, sgate, on))
            don = _each(lambda d, s: d * s * gain_v, dg_out, sgate)
            do = _each(lambda rr, dn, n: rr * (dn - n * jnp.mean(dn * n, axis=1, keepdims=True)), r, don, on)

            dp = _each(_dot_nt, do, v)
            dv = _each(lambda pp, d, kx, ef, s: _dot_tn(pp, d) + _dot(kx * ef, s), p, do, kk, esfx, ds)
            dqb = _each(_dot_nt, do, s0)
            dkx = _each(_dot_nt, v, ds)
            diag = [_rowsum(mask_ref[6] * x) for x in dp]
            dq = _each(lambda a, e, d, kx: a * e + d * kx, dqb, eb, diag, kk)
            dk = _each(lambda a, e, d, qq: a * e + d * qq, dkx, esfx, diag, q)
            dxs = [[] for _ in hs]
            for lvl in range(6):
                el = [e[lvl * CHUNK:(lvl + 1) * CHUNK] for e in ex]
                gm = [mask_ref[lvl] * x for x in dp]
                a1 = _each(lambda m_, kx, e: _dot(m_, kx * e), gm, kk, el)
                a2 = _each(lambda m_, qq, e: _dot_tn(m_, qq * e), gm, q, el)
                dq = _each(lambda x, a, e: x + a * e, dq, a1, el)
                dk = _each(lambda x, a, e: x + a * e, dk, a2, el)
                for hh in hs:
                    dxs[hh].append((a1[hh] * q[hh] + a2[hh] * kk[hh]) * el[hh])
            e_end_row = [e[CHUNK - 1:CHUNK, :] for e in eb]
            ds_new = _each(lambda qq, e, d, er, s: _dot_tn(qq * e, d) + _row_to_col(er, eye_v) * s, q, eb, do, e_end_row, ds)
            for hh in hs:
                dstate[heads[hh]] = ds_new[hh]
                dend_row = _col_to_row(_rowsum(s0[hh] * ds[hh]), eye_v)
                dxs[hh].append(dqb[hh] * q[hh] * eb[hh] + last_row * (e_end_row[hh] * dend_row))
                dxs[hh].append(dkx[hh] * kk[hh] * esfx[hh])
            dlf = _mx_each(mt_ref[...], [jnp.concatenate(x, axis=0) for x in dxs])

            for hh in hs:
                dhi_ref[sl, ln[hh]] = dv[hh].astype(BF16)
                dhq_ref[sl, ln[hh]] = (dq[hh] * (HEAD ** -0.5) * _dsilu(hqv[hh])).astype(BF16)
                df = dlf[hh] / f[hh]
                dsig = (1.0 - lb[hh]) * sg[hh] * sgn[hh]
                dhf_ref[sl, ln[hh]] = ((df - dk[hh]) * dsig).astype(BF16)
                dlb_t = jnp.sum(df * sgn[hh] - dk[hh] * sgn[hh], axis=0, keepdims=True)
                dlb_ref[pl.ds(heads[hh], 1), :] += dlb_t * lb[hh] * (1.0 - lb[hh])
            return carry

        lax.fori_loop(0, cb, one, 0, unroll=2)

    outs = [jax.ShapeDtypeStruct((t, HG_HEADS * HEAD), BF16)] * 4 + [
        jax.ShapeDtypeStruct((1, HEAD), F32), jax.ShapeDtypeStruct((HG_HEADS, HEAD), F32)]
    return pl.pallas_call(
        body, name="hgrn_bwd", grid=(nb, HG_HEADS // HPS),
        in_specs=[_view_tile(v, rows, HPS * HEAD, lambda c: nb - 1 - c) for v in (hq, hf, hi, hg)] + [
                  pl.BlockSpec((2, HPS * HEAD), lambda c, g: (0, g)),
                  pl.BlockSpec((1, HEAD), lambda c, g: (0, 0)), tile,
                  pl.BlockSpec((cb, HPS, HEAD, HEAD), lambda c, g: (nb - 1 - c, g, 0, 0)), tile,
                  pl.BlockSpec(mstack.shape, lambda c, h: (0, 0)),
                  pl.BlockSpec(mstack_t.shape, lambda c, h: (0, 0)),
                  pl.BlockSpec(masks.shape, lambda c, h: (0, 0, 0)),
                  pl.BlockSpec(eye.shape, lambda c, h: (0, 0))],
        out_specs=[tile, tile, tile, tile, pl.BlockSpec((1, HEAD), lambda c, h: (0, 0)),
                   pl.BlockSpec((HG_HEADS, HEAD), lambda c, h: (0, 0))],
        out_shape=outs, scratch_shapes=[pltpu.VMEM((HG_HEADS, HEAD, HEAD), F32)],
        compiler_params=_params(_ARB, _ARB))(hq[0], hf[0], hi[0], hg[0], logits, gain, oraw, ssave, dog, mstack,
                                             mstack_t, masks, eye)


CONV_W = 512


def _per_head(fn, *arrs):
    width = arrs[0].shape[1]
    return jnp.concatenate([fn(*[a[:, j:j + HEAD] for a in arrs]) for j in range(0, width, HEAD)], axis=1)


def _shift_down(xv, halo, d, top_rows):
    if d == 0:
        return xv, xv[0:8]
    main = pltpu.roll(xv, d, 0)
    top = jnp.where(top_rows < d, pltpu.roll(halo, d, 0), main[0:8])
    return main, top


def _conv_parts(x_ref, halo_ref, w_ref, first):
    xv = x_ref[...]
    halo = jnp.where(first, 0.0, halo_ref[...])
    top_rows = lax.broadcasted_iota(jnp.int32, (8, xv.shape[1]), 0)
    shifted = [_shift_down(xv, halo, CONV_K - 1 - j, top_rows) for j in range(CONV_K)]
    w = w_ref[...]
    acc = sum(shifted[j][0] * w[j:j + 1, :] for j in range(CONV_K))
    acc_top = sum(shifted[j][1] * w[j:j + 1, :] for j in range(CONV_K))
    return shifted, acc, acc_top


def _conv_fwd(x, w8, l2scale, name):
    x, off, width = x
    t = x.shape[0]
    o = off // CONV_W
    tr = _pick(t, 512, 8)

    def post(cv):
        s = _silu(cv)
        if l2scale is not None:
            s = _per_head(lambda sh: sh * (lax.rsqrt(_rowsum(sh * sh) + EPS) * l2scale), s)
        return s

    def body(x_ref, halo_ref, w_ref, o_ref):
        _, acc, acc_top = _conv_parts(x_ref, halo_ref, w_ref, pl.program_id(1) == 0)
        o_ref[...] = post(acc)
        o_ref[0:8, :] = post(acc_top)

    return pl.pallas_call(
        body, name=name, grid=(width // CONV_W,t // tr),
        in_specs=[pl.BlockSpec((tr, CONV_W), lambda j, i: (i, o + j)),
                  pl.BlockSpec((8, CONV_W), lambda j, i: (jnp.maximum(i * (tr // 8) - 1, 0), o + j)),
                  pl.BlockSpec((8, CONV_W), lambda j, i: (0, j))],
        out_specs=pl.BlockSpec((tr, CONV_W), lambda j, i: (i, j)),
        out_shape=jax.ShapeDtypeStruct((t, width), F32), compiler_params=_params(_PAR, _PAR))(x, x, w8)


def _conv_bwd_a(x, w8, dy, l2scale, name):
    x, off, width = x
    t = x.shape[0]
    o = off // CONV_W
    tr = _pick(t, 512, 8)

    def l2_bwd(s, dyh):
        r = lax.rsqrt(_rowsum(s * s) + EPS)
        y0 = s * r
        dy0 = dyh * l2scale
        return r * (dy0 - y0 * _rowsum(dy0 * y0))

    def to_dc(cv, dyv):
        if l2scale is not None:
            dyv = _per_head(l2_bwd, _silu(cv), dyv)
        return dyv * _dsilu(cv)

    def body(x_ref, halo_ref, w_ref, dy_ref, dc_ref, dw_ref):
        @pl.when(pl.program_id(1) == 0)
        def _():
            dw_ref[...] = jnp.zeros_like(dw_ref)

        shifted, acc, acc_top = _conv_parts(x_ref, halo_ref, w_ref, pl.program_id(1) == 0)
        dyv = dy_ref[...]
        dc = to_dc(acc, dyv)
        dc_top = to_dc(acc_top, dyv[0:8])
        dc_ref[...] = dc
        dc_ref[0:8, :] = dc_top
        rest = (lax.broadcasted_iota(jnp.int32, dc.shape, 0) >= 8).astype(F32)
        dc_rest = dc * rest
        for j in range(CONV_K):
            dw_ref[j:j + 1, :] += (jnp.sum(dc_rest * shifted[j][0], axis=0, keepdims=True)
                                   + jnp.sum(dc_top * shifted[j][1], axis=0, keepdims=True))

    return pl.pallas_call(
        body, name=name, grid=(width // CONV_W,t // tr),
        in_specs=[pl.BlockSpec((tr, CONV_W), lambda j, i: (i, o + j)),
                  pl.BlockSpec((8, CONV_W), lambda j, i: (jnp.maximum(i * (tr // 8) - 1, 0), o + j)),
                  pl.BlockSpec((8, CONV_W), lambda j, i: (0, j)),
                  pl.BlockSpec((tr, CONV_W), lambda j, i: (i, j))],
        out_specs=[pl.BlockSpec((tr, CONV_W), lambda j, i: (i, j)), pl.BlockSpec((8, CONV_W), lambda j, i: (0, j))],
        out_shape=[jax.ShapeDtypeStruct((t, width), F32), jax.ShapeDtypeStruct((8, width), F32)],
        compiler_params=_params(_PAR, _ARB))(x, x, w8, dy)


def _conv_bwd_b(dc, w8, name):
    t, width = dc.shape
    tr = _pick(t, 512, 8)
    nt = t // tr

    def body(dc_ref, halo_ref, w_ref, dx_ref):
        dcv = dc_ref[...]
        halo = jnp.where(pl.program_id(1) == nt - 1, 0.0, halo_ref[...])
        w = w_ref[...]
        bot_rows = lax.broadcasted_iota(jnp.int32, (8, CONV_W), 0)
        acc = dcv * w[CONV_K - 1:CONV_K, :]
        acc_bot = dcv[tr - 8:tr] * w[CONV_K - 1:CONV_K, :]
        for d in range(1, CONV_K):
            main = pltpu.roll(dcv, tr - d, 0)
            bot = jnp.where(bot_rows >= 8 - d, pltpu.roll(halo, 8 - d, 0), main[tr - 8:tr])
            wj = w[CONV_K - 1 - d:CONV_K - d, :]
            acc = acc + main * wj
            acc_bot = acc_bot + bot * wj
        dx_ref[...] = acc.astype(BF16)
        dx_ref[tr - 16:tr, :] = jnp.concatenate([acc[tr - 16:tr - 8], acc_bot], axis=0).astype(BF16)

    return pl.pallas_call(
        body, name=name, grid=(width // CONV_W,nt),
        in_specs=[pl.BlockSpec((tr, CONV_W), lambda j, i: (i, j)),
                  pl.BlockSpec((8, CONV_W), lambda j, i: (jnp.minimum((i + 1) * (tr // 8), t // 8 - 1), j)),
                  pl.BlockSpec((8, CONV_W), lambda j, i: (0, j))],
        out_specs=pl.BlockSpec((tr, CONV_W), lambda j, i: (i, j)),
        out_shape=jax.ShapeDtypeStruct((t, width), BF16), compiler_params=_params(_PAR, _PAR))(dc, dc, w8)


def _each(f, *lists):
    return [f(*xs) for xs in zip(*lists)]


def _split2_each(xs):
    hi = [_bf(x) for x in xs]
    lo = [_bf(x - h.astype(F32)) for x, h in zip(xs, hi)]
    return hi, lo


def _hp_each(a_split, b_split):
    (ah, al), (bh, bl) = a_split, b_split
    rows = ah[0].shape[0]
    d12 = [jnp.dot(jnp.concatenate([x, y], axis=0), z, preferred_element_type=F32) for x, y, z in zip(ah, al, bh)]
    d3 = [jnp.dot(x, y, preferred_element_type=F32) for x, y in zip(ah, bl)]
    return [d[:rows] + d[rows:] + e for d, e in zip(d12, d3)]


def _tri_inv_each(a_list, eye):
    ns = [-a for a in a_list]
    ps = [eye + n for n in ns]
    n_split = _split2_each(ns)
    for _ in range(5):
        ns = _hp_each(n_split, n_split)
        n_split = _split2_each(ns)
        ps = [p + d for p, d in zip(ps, _hp_each(_split2_each(ps), n_split))]
    return ps


def _gd_gates(gab, alog, dtb):
    sp_arg = gab + dtb
    return sp_arg, -jnp.exp(alog) * _softplus(sp_arg), _sigmoid(gab)


def _gd_chunks(q, k, v, g_all, beta_all, sel, l_ref, mask_ref, tm=None):
    incl, strict, eye, upper = mask_ref[0], mask_ref[1], mask_ref[2], mask_ref[3]
    lmat = l_ref[...]
    gates = jnp.concatenate(_split3(g_all) + _split3(beta_all), axis=0)
    picked = [jnp.dot(gates, s, preferred_element_type=F32) for s in sel]
    c = CHUNK
    gb = [p[0:c, :HEAD] + p[c:2 * c, :HEAD] + p[2 * c:3 * c, :HEAD] for p in picked]
    bb = [p[3 * c:4 * c, HEAD:] + p[4 * c:5 * c, HEAD:] + p[5 * c:, HEAD:] for p in picked]
    gam = _mx_each(lmat, gb)
    gam_row = [jnp.sum(x[:, :CHUNK] * upper, axis=0, keepdims=True) for x in gb]
    lm = _each(lambda gm, gr: incl * jnp.exp(jnp.minimum(gm[:, :CHUNK] - gr, 0.0)), gam, gam_row)
    kb = _each(lambda x, b: x * b, k, bb)
    a = _each(lambda x, y, m: strict * _dot_nt(x, y) * m, kb, k, lm)
    if tm is None:
        tm = _tri_inv_each(a, eye)
    eg = [jnp.exp(x) for x in gam]
    vb = _each(lambda x, b: x * b, v, bb)
    kbg = _each(lambda x, e: x * e, kb, eg)
    uw = _each(lambda t_, x, y: _dot(t_, jnp.concatenate([x, y], axis=1)), tm, vb, kbg)
    u = [x[:, :HEAD] for x in uw]
    w = [x[:, HEAD:] for x in uw]
    qk = _each(lambda x, y, m: _dot_nt(x, y) * m, q, k, lm)
    g_end = [x[CHUNK - 1:CHUNK, :] for x in gam]
    ekg = _each(lambda e, x: jnp.exp(e - x), g_end, gam)
    ge = [jnp.exp(e) for e in g_end]
    kg = _each(lambda x, e: x * e, k, ekg)
    qg = _each(lambda x, e: x * e, q, eg)
    names = ("bb", "lm", "kb", "a", "tm", "eg", "vb", "kbg", "u", "w", "qk", "ekg", "ge", "kg", "qg")
    cols = (bb, lm, kb, a, tm, eg, vb, kbg, u, w, qk, ekg, ge, kg, qg)
    return [dict(zip(names, vals)) for vals in zip(*cols)]


def _gd_specs(rows, rev_nb=None):
    def cidx(c):
        return c if rev_nb is None else rev_nb - 1 - c

    qk_tile = pl.BlockSpec((rows, HPS // 2 * HEAD), lambda c, g: (cidx(c), g))
    v_tile = pl.BlockSpec((rows, HPS * HEAD), lambda c, g: (cidx(c), g))
    gab_tile = pl.BlockSpec((rows, HEAD), lambda c, g: (cidx(c), 0))
    return qk_tile, v_tile, gab_tile


def _gdn_fwd(qn, kn, cv, gab, gz, alog, dtb, gain, consts):
    t = qn.shape[0]
    nc = t // CHUNK
    cb = _chunks_per_step(nc)
    rows = cb * CHUNK
    lmat, _, masks, sel = consts
    qk_tile, v_tile, gab_tile = _gd_specs(rows)
    row128 = pl.BlockSpec((1, HEAD), lambda c, h: (0, 0))

    def body(q_ref, k_ref, v_ref, gab_ref, gz_ref, alog_ref, dtb_ref, gain_ref, sel_ref, l_ref, mask_ref,
             oraw_ref, og_ref, ssave_ref, tsave_ref, state):
        c = pl.program_id(0)
        g = pl.program_id(1)

        @pl.when(c == 0)
        def _():
            for hh in range(HPS):
                state[g * HPS + hh] = jnp.zeros((HEAD, HEAD), F32)

        alog = alog_ref[...]
        dtb = dtb_ref[...]
        gain_v = gain_ref[...]

        def one(i, carry):
            sl = pl.ds(pl.multiple_of(i * CHUNK, CHUNK), CHUNK)
            _, g_all, beta_all = _gd_gates(gab_ref[sl, :], alog, dtb)
            heads = [g * HPS + hh for hh in range(HPS)]
            lq = [slice(hh // 2 * HEAD, (hh // 2 + 1) * HEAD) for hh in range(HPS)]
            lv = [slice(hh * HEAD, (hh + 1) * HEAD) for hh in range(HPS)]
            chs = _gd_chunks([q_ref[sl, s] for s in lq], [k_ref[sl, s] for s in lq], [v_ref[sl, s] for s in lv],
                             g_all, beta_all, [sel_ref[h] for h in heads], l_ref, mask_ref)
            s0 = [state[h] for h in heads]
            ws = _each(lambda ch, s: _dot(jnp.concatenate([ch["w"], ch["qg"]], axis=0), s), chs, s0)
            v_new = _each(lambda ch, x: ch["u"] - x[:CHUNK], chs, ws)
            o = _each(lambda ch, x, vn: x[CHUNK:] + _dot(ch["qk"], vn), chs, ws, v_new)
            s1 = _each(lambda ch, s, vn: s * ch["ge"] + _dot_tn(ch["kg"], vn), chs, s0, v_new)
            for hh in range(HPS):
                ssave_ref[i, hh] = s0[hh]
                tsave_ref[i, hh] = chs[hh]["tm"]
                state[heads[hh]] = s1[hh]
                oraw_ref[sl, lv[hh]] = o[hh]
                r = lax.rsqrt(jnp.mean(o[hh] * o[hh], axis=1, keepdims=True) + EPS)
                og_ref[sl, lv[hh]] = (o[hh] * r * gain_v * _silu(gz_ref[sl, lv[hh]])).astype(BF16)
            return carry

        lax.fori_loop(0, cb, one, 0, unroll=2)

    return pl.pallas_call(
        body, name="gdn_fwd", grid=(nc // cb, GD_HEADS // HPS),
        in_specs=[qk_tile, qk_tile, v_tile, gab_tile, _view_tile(gz, rows, HPS * HEAD), row128, row128, row128,
                  pl.BlockSpec(sel.shape, lambda c, g: (0, 0, 0)),
                  pl.BlockSpec(lmat.shape, lambda c, g: (0, 0)),
                  pl.BlockSpec(masks.shape, lambda c, g: (0, 0, 0))],
        out_specs=[v_tile, v_tile, pl.BlockSpec((cb, HPS, HEAD, HEAD), lambda c, g: (c, g, 0, 0)),
                   pl.BlockSpec((cb, HPS, CHUNK, CHUNK), lambda c, g: (c, g, 0, 0))],
        out_shape=[jax.ShapeDtypeStruct((t, GD_HEADS * HEAD), F32), jax.ShapeDtypeStruct((t, GD_HEADS * HEAD), BF16),
                   jax.ShapeDtypeStruct((nc, GD_HEADS, HEAD, HEAD), F32),
                   jax.ShapeDtypeStruct((nc, GD_HEADS, CHUNK, CHUNK), F32)],
        scratch_shapes=[pltpu.VMEM((GD_HEADS, HEAD, HEAD), F32)],
        compiler_params=_params(_ARB, _ARB))(qn, kn, cv, gab, gz[0], alog, dtb, gain, sel, lmat, masks)


def _gdn_bwd(qn, kn, cv, gab, gz, alog, dtb, gain, oraw, ssave, tsave, dog, consts):
    t = qn.shape[0]
    nc = t // CHUNK
    cb = _chunks_per_step(nc)
    rows = cb * CHUNK
    nb = nc // cb
    lmat, lmat_t, masks, sel = consts
    qk_tile, v_tile, gab_tile = _gd_specs(rows, nb)
    row128 = pl.BlockSpec((1, HEAD), lambda c, h: (0, 0))

    def body(q_ref, k_ref, v_ref, gab_ref, gz_ref, alog_ref, dtb_ref, gain_ref, oraw_ref, ssave_ref, tsave_ref, dog_ref,
             sel_ref, l_ref, lt_ref, mask_ref,
             dq_ref, dk_ref, dv_ref, dgab_ref, dgz_ref, small_ref, dstate):
        c = pl.program_id(0)
        g = pl.program_id(1)

        @pl.when(c == 0)
        def _():
            for hh in range(HPS):
                dstate[g * HPS + hh] = jnp.zeros((HEAD, HEAD), F32)

        @pl.when((c == 0) & (g == 0))
        def _():
            small_ref[...] = jnp.zeros_like(small_ref)

        alog = alog_ref[...]
        dtb = dtb_ref[...]
        gain_v = gain_ref[...]
        lane = lax.broadcasted_iota(jnp.int32, (1, HEAD), 1)
        last_row = (lax.broadcasted_iota(jnp.int32, (CHUNK, HEAD), 0) == CHUNK - 1).astype(F32)

        def one(j, carry):
            i = cb - 1 - j
            sl = pl.ds(pl.multiple_of(i * CHUNK, CHUNK), CHUNK)
            sp_arg, g_all, beta_all = _gd_gates(gab_ref[sl, :], alog, dtb)
            strict, eye = mask_ref[1], mask_ref[2]
            ltm = lt_ref[...]
            hs = range(HPS)
            heads = [g * HPS + hh for hh in hs]
            lq = [slice(hh // 2 * HEAD, (hh // 2 + 1) * HEAD) for hh in hs]
            lv = [slice(hh * HEAD, (hh + 1) * HEAD) for hh in hs]
            q = [q_ref[sl, s] for s in lq]
            k = [k_ref[sl, s] for s in lq]
            v = [v_ref[sl, s] for s in lv]
            gzv = [gz_ref[sl, s] for s in lv]
            chs = _gd_chunks(q, k, v, g_all, beta_all, [sel_ref[h] for h in heads], l_ref, mask_ref,
                             tm=[tsave_ref[i, hh] for hh in hs])

            def col(name):
                return [ch[name] for ch in chs]

            def mul(x, y):
                return x * y

            tm, lm, eg, bb = col("tm"), col("lm"), col("eg"), col("bb")
            s0 = [ssave_ref[i, hh] for hh in hs]
            ds = [dstate[h] for h in heads]
            v_new = _each(lambda u, w, s: u - _dot(w, s), col("u"), col("w"), s0)

            o = [oraw_ref[sl, s] for s in lv]
            r = [lax.rsqrt(jnp.mean(x * x, axis=1, keepdims=True) + EPS) for x in o]
            on = _each(mul, o, r)
            dg_out = [dog_ref[sl, s] for s in lv]
            sgate = [_silu(x) for x in gzv]
            for hh in hs:
                dgz_ref[sl, lv[hh]] = (dg_out[hh] * on[hh] * gain_v * _dsilu(gzv[hh])).astype(BF16)
            small_ref[0:1, :] += sum(jnp.sum(d * s * n, axis=0, keepdims=True) for d, s, n in zip(dg_out, sgate, on))
            don = _each(lambda d, s: d * s * gain_v, dg_out, sgate)
            do = _each(lambda rr, dn, n: rr * (dn - n * jnp.mean(dn * n, axis=1, keepdims=True)), r, don, on)

            dv_new = _each(lambda a, d, b, s: _dot_tn(a, d) + _dot(b, s), col("qk"), do, col("kg"), ds)
            dqk = _each(_dot_nt, do, v_new)
            dkg = _each(_dot_nt, v_new, ds)
            dge = _each(lambda s, d: jnp.sum(_rowsum(s * d), axis=0, keepdims=True), s0, ds)
            both = _each(lambda d, dv: jnp.concatenate([d, dv], axis=0), do, dv_new)
            from_s = _each(_dot_nt, both, s0)
            dqg = [x[:CHUNK] for x in from_s]
            dw = [-x[CHUNK:] for x in from_s]
            ds_new = _each(lambda qg, w, bo, ge, s: _dot_tn(jnp.concatenate([qg, -w], axis=0), bo) + ge * s,
                           col("qg"), col("w"), both, col("ge"), ds)
            for hh in hs:
                dstate[heads[hh]] = ds_new[hh]

            side = _each(lambda dv, d: jnp.concatenate([dv, d], axis=1), dv_new, dw)
            back = _each(_dot_tn, tm, side)
            dvb = [x[:, :HEAD] for x in back]
            dkbg = [x[:, HEAD:] for x in back]
            dtm = _each(lambda sd, vb, kbg: _dot_nt(sd, jnp.concatenate([vb, kbg], axis=1)), side, col("vb"), col("kbg"))
            dtt = _each(_dot_nt, dtm, tm)
            da = _each(lambda t_, x: -_dot_tn(t_, x) * strict, tm, dtt)
            dal = _each(mul, da, lm)
            dqk_l = _each(mul, dqk, lm)
            stack = _each(lambda x, y: jnp.concatenate([x, y], axis=0), dal, dqk_l)
            on_k = _each(_dot, stack, k)
            dkb = _each(lambda x, y, e: x[:CHUNK] + y * e, on_k, dkbg, eg)
            dq = _each(lambda x, y, e: x[CHUNK:] + y * e, on_k, dqg, eg)
            dk = _each(lambda st, kb, qq, z, ekg, w_, b: _dot_tn(st, jnp.concatenate([kb, qq], axis=0)) + z * ekg + w_ * b,
                       stack, col("kb"), q, dkg, col("ekg"), dkb, bb)
            gmat = _each(lambda x, a, y, qk: x * a + y * qk, da, col("a"), dqk, col("qk"))
            t_kg = _each(lambda x, y: _rowsum(x * y), dkg, col("kg"))
            dgam = _each(lambda gm, x, qg, t_, y, kbg: (_rowsum(gm) - _row_to_col(jnp.sum(gm, axis=0, keepdims=True), eye)
                                                        + _rowsum(x * qg) - t_ + _rowsum(y * kbg)),
                         gmat, dqg, col("qg"), t_kg, dkbg, col("kbg"))
            dg_end = _each(lambda t_, e, ge: jnp.sum(t_, axis=0, keepdims=True) + e * ge[:, 0:1], t_kg, dge, col("ge"))
            dgam = _each(lambda x, e: x + last_row * e, dgam, dg_end)
            dbeta = _each(lambda x, kk, y, vv: _rowsum(x * kk) + _rowsum(y * vv), dkb, k, dvb, v)
            dg = _mx_each(ltm, dgam)

            for hh in hs:
                dv_ref[sl, lv[hh]] = dvb[hh] * bb[hh]
            fac_g = -jnp.exp(alog) * _sigmoid(sp_arg)
            fac_b = beta_all * (1.0 - beta_all)
            hot_g = [(lane == h).astype(F32) for h in heads]
            hot_b = [(lane == GD_HEADS + h).astype(F32) for h in heads]
            dga = _each(lambda x, hot: x * hot * fac_g, dg, hot_g)
            dgb = _each(lambda x, hot: x * hot * fac_b, dbeta, hot_b)
            small_ref[1:2, :] += sum(jnp.sum(x, axis=0, keepdims=True) for x in dga)
            small_ref[2:3, :] += sum(jnp.sum(x * hot * g_all, axis=0, keepdims=True) for x, hot in zip(dg, hot_g))
            for pair in range(HPS // 2):
                lqp = slice(pair * HEAD, (pair + 1) * HEAD)
                dq_ref[sl, lqp] = dq[2 * pair] + dq[2 * pair + 1]
                dk_ref[sl, lqp] = dk[2 * pair] + dk[2 * pair + 1]
            dgab_ref[sl, :] = sum(a + b for a, b in zip(dga, dgb))
            return carry

        lax.fori_loop(0, cb, one, 0, unroll=2)

    groups = GD_HEADS // HPS
    outs = [jax.ShapeDtypeStruct((t, 1024), F32), jax.ShapeDtypeStruct((t, 1024), F32),
            jax.ShapeDtypeStruct((t, 2048), F32), jax.ShapeDtypeStruct((t, groups * HEAD), F32),
            jax.ShapeDtypeStruct((t, 2048), BF16), jax.ShapeDtypeStruct((8, HEAD), F32)]
    return pl.pallas_call(
        body, name="gdn_bwd", grid=(nb, groups),
        in_specs=[qk_tile, qk_tile, v_tile, gab_tile, _view_tile(gz, rows, HPS * HEAD, lambda c: nb - 1 - c),
                  row128, row128, row128, v_tile,
                  pl.BlockSpec((cb, HPS, HEAD, HEAD), lambda c, g: (nb - 1 - c, g, 0, 0)),
                  pl.BlockSpec((cb, HPS, CHUNK, CHUNK), lambda c, g: (nb - 1 - c, g, 0, 0)), v_tile,
                  pl.BlockSpec(sel.shape, lambda c, g: (0, 0, 0)),
                  pl.BlockSpec(lmat.shape, lambda c, g: (0, 0)),
                  pl.BlockSpec(lmat_t.shape, lambda c, g: (0, 0)),
                  pl.BlockSpec(masks.shape, lambda c, g: (0, 0, 0))],
        out_specs=[qk_tile, qk_tile, v_tile, pl.BlockSpec((rows, HEAD), lambda c, g: (nb - 1 - c, g)), v_tile,
                   pl.BlockSpec((8, HEAD), lambda c, g: (0, 0))],
        out_shape=outs, scratch_shapes=[pltpu.VMEM((GD_HEADS, HEAD, HEAD), F32)],
        compiler_params=_params(_ARB, _ARB))(qn, kn, cv, gab, gz[0], alog, dtb, gain, oraw, ssave, tsave, dog, sel,
                                             lmat, lmat_t, masks)


def _fold_groups(wide):
    t, width = wide.shape
    tr = _pick(t, 512, 8)

    def body(w_ref, o_ref):
        acc = w_ref[:, 0:HEAD]
        for j in range(1, width // HEAD):
            acc = acc + w_ref[:, j * HEAD:(j + 1) * HEAD]
        o_ref[...] = acc.astype(BF16)

    return pl.pallas_call(
        body, name="fold_gate_grads", grid=(t // tr,), in_specs=[_row_spec(tr, width)], out_specs=_row_spec(tr, HEAD),
        out_shape=jax.ShapeDtypeStruct((t, HEAD), BF16), compiler_params=_params(_PAR))(wide)


def _adam_math(w, g, m, v):
    m2 = ADAM_B1 * m + (1.0 - ADAM_B1) * g
    v2 = ADAM_B2 * v + (1.0 - ADAM_B2) * (g * g)
    m_hat = m2 / (1.0 - ADAM_B1 ** ADAM_STEP)
    v_hat = v2 / (1.0 - ADAM_B2 ** ADAM_STEP)
    delta = -ADAM_LR * (m_hat / (jnp.sqrt(v_hat) + ADAM_EPS) + ADAM_WD * w)
    return delta, m2, v2


def _adamw(w, g, m, v, name, after=None):
    r, c = w.shape
    tr = r
    for cand in range(8, r + 1, 8):
        if r % cand == 0 and cand * c * 4 <= (1 << 20):
            tr = cand
    if r % 8 != 0:
        tr = r

    def body(w_ref, g_ref, m_ref, v_ref, *rest):
        d_ref, m2_ref, v2_ref = rest[-3:]
        d, m2, v2 = _adam_math(w_ref[...], g_ref[...], m_ref[...], v_ref[...])
        d_ref[...] = d
        m2_ref[...] = m2
        v2_ref[...] = v2

    spec = pl.BlockSpec((tr, c), lambda i: (i, 0))
    extra = [] if after is None else [after]
    return pl.pallas_call(
        body, name=name, grid=(r // tr,), in_specs=[spec] * 4 + [_ANY] * len(extra), out_specs=[spec] * 3,
        out_shape=[jax.ShapeDtypeStruct((r, c), F32)] * 3, compiler_params=_params(_PAR))(w, g, m, v, *extra)


_ANY = pl.BlockSpec(memory_space=pl.ANY)


def _place():
    return lax.axis_index("x"), lax.axis_index("y"), lax.axis_index("c")


def _gather_weights(packs, nchs, name):
    n = len(packs)
    halves = [p.shape[0] // 2 for p in packs]
    base = [sum(nchs[:i]) for i in range(n)]
    total = sum(nchs)
    for p, h, k in zip(packs, halves, nchs):
        assert p.shape[0] == 2 * h and h % k == 0 and (h // k) % 16 == 0

    def body(*refs):
        p_refs, g_refs, (send_sems, recv_sems) = refs[:n], refs[n:2 * n], refs[2 * n:]
        x, y, c = _place()
        sibling = (x, y, 1 - c)
        chips = [(1 - x, y), (x, 1 - y), (1 - x, 1 - y)]
        chunks = [(a, q) for a in range(n) for q in range(nchs[a])]

        def rows_of(a, pc, q):
            ch = halves[a] // nchs[a]
            return pl.ds(pl.multiple_of(pc * halves[a] + q * ch, 16), ch)

        def piece(a, px, py, pc, q):
            return g_refs[a].at[2 * px + py, rows_of(a, pc, q), :]

        def copy(k, src, dst, to):
            return pltpu.make_async_remote_copy(src_ref=src, dst_ref=dst, send_sem=send_sems.at[k],
                                                recv_sem=recv_sems.at[k], device_id=to, device_id_type=MESH)

        def sem_of(j, a, q):
            return j * total + base[a] + q

        first = {(j, a, q): copy(sem_of(j, a, q), p_refs[a].at[rows_of(a, c, q), :], piece(a, x, y, c, q), (*chip, c))
                 for j, chip in enumerate(chips) for a, q in chunks}
        for a, q in chunks:
            for j in range(3):
                first[j, a, q].start()
        passed = {(j, a, q): copy(sem_of(3 + j, a, q), piece(a, *chip, c, q), piece(a, *chip, c, q), sibling)
                  for j, chip in enumerate(chips) for a, q in chunks}
        for a, q in chunks:
            for j, chip in enumerate(chips):
                copy(sem_of(j, a, q), p_refs[a].at[rows_of(a, c, q), :], piece(a, *chip, c, q), (*chip, c)).wait_recv()
                passed[j, a, q].start()
        for a, q in chunks:
            for j, chip in enumerate(chips):
                copy(sem_of(3 + j, a, q), piece(a, *chip, 1 - c, q), piece(a, *chip, 1 - c, q), sibling).wait_recv()
        for key in first:
            first[key].wait_send()
            passed[key].wait_send()

    return pl.pallas_call(
        body, name=name, out_shape=[jax.ShapeDtypeStruct((4,) + p.shape, p.dtype) for p in packs],
        in_specs=[_ANY] * n, out_specs=[_ANY] * n,
        scratch_shapes=[pltpu.SemaphoreType.DMA((6 * total,)), pltpu.SemaphoreType.DMA((6 * total,))])(*packs)


def _swap_with_sibling(arrs, nchs, lead, name, halves=False):
    n = len(arrs)
    jobs = []
    hs = [arr.shape[-2] // (2 if halves else 1) for arr in arrs]
    for a, (h, k) in enumerate(zip(hs, nchs)):
        assert h % k == 0 and (h // k) % 16 == 0
        for s in (range(lead) if lead else [None]):
            jobs += [(a, s, q * (h // k), h // k) for q in range(k)]

    def body(*refs):
        src, dst, (send_sems, recv_sems) = refs[:n], refs[n:2 * n], refs[2 * n:]
        x, y, c = _place()

        def at(ref, s, r0, rows):
            return ref.at[pl.ds(r0, rows), :] if s is None else ref.at[s, pl.ds(r0, rows), :]

        def src_rows(a, r0):
            return pl.multiple_of((1 - c) * hs[a] + r0, 16) if halves else r0

        copies = [pltpu.make_async_remote_copy(
            src_ref=at(src[a], s, src_rows(a, r0), rows), dst_ref=at(dst[a], s, r0, rows), send_sem=send_sems.at[k],
            recv_sem=recv_sems.at[k], device_id=(x, y, 1 - c), device_id_type=MESH)
            for k, (a, s, r0, rows) in enumerate(jobs)]
        for cp in copies:
            cp.start()
        for cp in copies:
            cp.wait()

    shapes = [jax.ShapeDtypeStruct(arr.shape[:-2] + (h, arr.shape[-1]), arr.dtype) for arr, h in zip(arrs, hs)]
    return pl.pallas_call(
        body, name=name, out_shape=shapes, in_specs=[_ANY] * n, out_specs=[_ANY] * n,
        scratch_shapes=[pltpu.SemaphoreType.DMA((len(jobs),)), pltpu.SemaphoreType.DMA((len(jobs),))])(*arrs)


def _add2(full, b, core, name):
    n, rows, w = b.shape
    tr = _pick(rows, 256, 16)
    nblk = rows // tr

    def body(c_ref, a_ref, b_ref, o_ref):
        o_ref[...] = (a_ref[...].astype(F32) + b_ref[...].astype(F32)).astype(BF16)

    spec = pl.BlockSpec((1, tr, w), lambda i, j, c_ref: (i, j, 0))
    grid_spec = pltpu.PrefetchScalarGridSpec(
        num_scalar_prefetch=1, grid=(n, nblk),
        in_specs=[pl.BlockSpec((1, tr, w), lambda i, j, c_ref: (i, c_ref[0] * nblk + j, 0)), spec], out_specs=spec)
    return pl.pallas_call(
        body, name=name, grid_spec=grid_spec, out_shape=jax.ShapeDtypeStruct(b.shape, BF16),
        compiler_params=_params(_PAR, _PAR))(core, full, b)


def _reduce_chips(partials, nchs, name):
    n = len(partials)
    jobs = []
    for a, (arr, k) in enumerate(zip(partials, nchs)):
        h = arr.shape[1]
        assert h % k == 0 and (h // k) % 16 == 0
        jobs += [(a, q * (h // k), h // k) for q in range(k)]

    def body(*refs):
        src, dst, (send_sems, recv_sems) = refs[:n], refs[n:2 * n], refs[2 * n:]
        x, y, c = _place()
        chips = [(1 - x, y), (x, 1 - y), (1 - x, 1 - y)]
        copies = [pltpu.make_async_remote_copy(
            src_ref=src[a].at[2 * px + py, pl.ds(r0, rows), :], dst_ref=dst[a].at[j, pl.ds(r0, rows), :],
            send_sem=send_sems.at[3 * k + j], recv_sem=recv_sems.at[3 * k + j],
            device_id=(px, py, c), device_id_type=MESH)
            for k, (a, r0, rows) in enumerate(jobs) for j, (px, py) in enumerate(chips)]
        for cp in copies:
            cp.start()
        for cp in copies:
            cp.wait()

    return pl.pallas_call(
        body, name=name,
        out_shape=[jax.ShapeDtypeStruct((3,) + p.shape[1:], p.dtype) for p in partials],
        in_specs=[_ANY] * n, out_specs=[_ANY] * n,
        scratch_shapes=[pltpu.SemaphoreType.DMA((3 * len(jobs),)), pltpu.SemaphoreType.DMA((3 * len(jobs),))])(*partials)


_HBM = pl.BlockSpec(memory_space=pltpu.HBM)
_SEM = pl.BlockSpec(memory_space=pltpu.SEMAPHORE)
_DATAFLOW = pltpu.SideEffectType.DATAFLOW_SIDE_EFFECTING


def _ici_jobs(srcs, nchs, kind):
    jobs = []
    for a, (arr, k) in enumerate(zip(srcs, nchs)):
        h = arr.shape[0] // 2 if kind == "gather" else arr.shape[1]
        assert h % k == 0 and (h // k) % 16 == 0
        jobs += [(a, h, q * (h // k), h // k) for q in range(k)]
    return jobs


def _ici_copies(src, land, send_sems, recv_sems, jobs, kind):
    x, y, c = _place()
    chips = [(1 - x, y), (x, 1 - y), (1 - x, 1 - y)]
    copies = []
    for k, (a, h, r0, rows) in enumerate(jobs):
        for j, (px, py) in enumerate(chips):
            if kind == "gather":
                at = pl.ds(pl.multiple_of(c * h + r0, 16), rows)
                s, d = src[a].at[at, :], land[a].at[2 * x + y, at, :]
            else:
                s, d = src[a].at[2 * px + py, pl.ds(r0, rows), :], land[a].at[j, pl.ds(r0, rows), :]
            copies.append(pltpu.make_async_remote_copy(
                src_ref=s, dst_ref=d, send_sem=send_sems.at[3 * k + j], recv_sem=recv_sems.at[3 * k + j],
                device_id=(px, py, c), device_id_type=MESH))
    return copies


def _ici_start(srcs, nchs, kind, name):
    n = len(srcs)
    jobs = _ici_jobs(srcs, nchs, kind)
    lead = (lambda s: (4,) + s.shape) if kind == "gather" else (lambda s: (3,) + s.shape[1:])
    lands = [lax.empty(lead(s), s.dtype) for s in srcs]

    def body(*refs):
        src, land = refs[:n], refs[n:2 * n]
        send_sems, recv_sems, token = refs[2 * n], refs[2 * n + 1], refs[-1]
        for cp in _ici_copies(src, land, send_sems, recv_sems, jobs, kind):
            cp.start()
        token[...] = jnp.zeros_like(token)

    hbm = [pltpu.HBM(a.shape, a.dtype) for a in srcs + lands]
    outs = pl.pallas_call(
        body, name=name,
        out_shape=[pltpu.SemaphoreType.DMA((3 * len(jobs),)), pltpu.SemaphoreType.DMA((3 * len(jobs),))] + hbm
        + [jax.ShapeDtypeStruct((8, 128), F32)],
        in_specs=[_HBM] * (2 * n), out_specs=[_SEM, _SEM] + [_HBM] * (2 * n) + [pl.BlockSpec(memory_space=pltpu.VMEM)],
        input_output_aliases={i: 2 + i for i in range(2 * n)},
        compiler_params=pltpu.CompilerParams(has_side_effects=_DATAFLOW),
    )(*[pltpu.with_memory_space_constraint(a, pltpu.HBM) for a in srcs + lands])
    return (outs[0], outs[1], list(outs[2:2 + n]), list(outs[2 + n:2 + 2 * n]), nchs, kind), outs[-1]


def _ici_wait(handle, after, name):
    send_sems, recv_sems, srcs, lands, nchs, kind = handle
    n = len(srcs)
    jobs = _ici_jobs(srcs, nchs, kind)

    def body(*refs):
        src, land = refs[:n], refs[n:2 * n]
        for cp in _ici_copies(src, land, refs[2 * n], refs[2 * n + 1], jobs, kind):
            cp.wait_send()
            cp.wait_recv()

    outs = pl.pallas_call(
        body, name=name, out_shape=[pltpu.HBM(a.shape, a.dtype) for a in srcs + lands],
        in_specs=[_HBM] * (2 * n) + [_SEM, _SEM, _ANY], out_specs=[_HBM] * (2 * n),
        input_output_aliases={i: i for i in range(2 * n)},
        compiler_params=pltpu.CompilerParams(has_side_effects=_DATAFLOW),
    )(*srcs, *lands, send_sems, recv_sems, after)
    return list(outs[:n]), list(outs[n:])


def _pass_to_sibling(gathered, nchs, name):
    n = len(gathered)
    jobs = _ici_jobs([jax.ShapeDtypeStruct(g.shape[1:], g.dtype) for g in gathered], nchs, "gather")

    def body(*refs):
        src, dst, (send_sems, recv_sems) = refs[:n], refs[n:2 * n], refs[2 * n:]
        x, y, c = _place()
        slots = [2 * (1 - x) + y, 2 * x + (1 - y), 2 * (1 - x) + (1 - y)]

        def copy(k, j, pc):
            a, h, r0, rows = jobs[k]
            at = pl.ds(pl.multiple_of(pc * h + r0, 16), rows)
            return pltpu.make_async_remote_copy(
                src_ref=src[a].at[slots[j], at, :], dst_ref=dst[a].at[slots[j], at, :], send_sem=send_sems.at[3 * k + j],
                recv_sem=recv_sems.at[3 * k + j], device_id=(x, y, 1 - c), device_id_type=MESH)

        pairs = [(k, j) for k in range(len(jobs)) for j in range(3)]
        for k, j in pairs:
            copy(k, j, c).start()
        for k, j in pairs:
            copy(k, j, c).wait_send()
            copy(k, j, 1 - c).wait_recv()

    return pl.pallas_call(
        body, name=name, out_shape=[jax.ShapeDtypeStruct(g.shape, g.dtype) for g in gathered],
        in_specs=[_ANY] * n, out_specs=[_ANY] * n, input_output_aliases={i: i for i in range(n)},
        scratch_shapes=[pltpu.SemaphoreType.DMA((3 * len(jobs),)), pltpu.SemaphoreType.DMA((3 * len(jobs),))])(*gathered)


def _add4(own, got, name):
    rows, w = own.shape
    tr = _pick(rows, 128, 16)

    def body(a_ref, b_ref, o_ref):
        o_ref[...] = ((a_ref[...].astype(F32) + b_ref[0].astype(F32)) + b_ref[1].astype(F32)) + b_ref[2].astype(F32)

    return pl.pallas_call(
        body, name=name, grid=(rows // tr,),
        in_specs=[pl.BlockSpec((tr, w), lambda i: (i, 0)), pl.BlockSpec((3, tr, w), lambda i: (0, i, 0))],
        out_specs=pl.BlockSpec((tr, w), lambda i: (i, 0)), out_shape=jax.ShapeDtypeStruct((rows, w), F32),
        compiler_params=_params(_PAR))(own, got)


def _small_sync(gs, ws, ms, vs):
    rows = gs.shape[0]
    vmem = pl.BlockSpec(memory_space=pltpu.VMEM)

    def body(g_ref, w_ref, m_ref, v_ref, sum_ref, d_ref, m2_ref, v2_ref, buf, send_sems, recv_sems):
        x, y, c = _place()
        me = 4 * x + 2 * y + c
        buf[me] = g_ref[...]
        copies = []
        for k in range(1, 8):
            peer = (x ^ (k >> 2), y ^ ((k >> 1) & 1), c ^ (k & 1))
            copies.append(pltpu.make_async_remote_copy(
                src_ref=g_ref, dst_ref=buf.at[me], send_sem=send_sems.at[k - 1], recv_sem=recv_sems.at[k - 1],
                device_id=peer, device_id_type=MESH))
        for cp in copies:
            cp.start()
        for cp in copies:
            cp.wait()
        total = buf[0]
        for i in range(1, 8):
            total = total + buf[i]
        sum_ref[...] = total
        d, m2, v2 = _adam_math(w_ref[...], total, m_ref[...], v_ref[...])
        d_ref[...] = d
        m2_ref[...] = m2
        v2_ref[...] = v2

    shape = jax.ShapeDtypeStruct((rows, 128), F32)
    return pl.pallas_call(
        body, name="small_sync", out_shape=[shape] * 4, in_specs=[vmem] * 4, out_specs=[vmem] * 4,
        scratch_shapes=[pltpu.VMEM((8, rows, 128), F32), pltpu.SemaphoreType.DMA((7,)),
                        pltpu.SemaphoreType.DMA((7,))])(gs, ws, ms, vs)


_GROUPS = {
    "ffn1": dict(cols=("ffn1_w_in", 1408), rows=(("ffn1_w_out", 704, 704),), chunks=(8, 2)),
    "ffn2": dict(cols=("ffn2_w_in", 1408), rows=(("ffn2_w_out", 704, 704),), chunks=(8, 2)),
    "mixer": dict(cols=("w_in", 3080), chunks=(8, 4),
                  rows=(("w_branch_hgrn", 256, 256), ("w_branch_gdn", 512, 512), ("w_out", 256, 256),
                        ("gdn_conv_w", CONV_K, 128))),
}
_BIG_NAMES = tuple(n for g in _GROUPS.values() for n in (g["cols"][0],) + tuple(r[0] for r in g["rows"]))


def _group_names(group):
    return (group["cols"][0],) + tuple(r[0] for r in group["rows"])


def _pack(parts, lead, group):
    ax = len(lead)
    rows = []
    for n, r, padded in group["rows"]:
        p = parts[n]
        if padded != r:
            p = jnp.tile(p, (1,) * ax + (padded // r, 1))
        rows.append(p)
    return [parts[group["cols"][0]], rows[0] if len(rows) == 1 else jnp.concatenate(rows, axis=ax)]


def _unpack(cols, rows, group):
    out, off = {group["cols"][0]: cols}, 0
    for n, r, padded in group["rows"]:
        out[n] = rows[..., off:off + r, :]
        off += padded
    return out


def _is_col_sharded(name):
    return name in ("ffn1_w_in", "ffn2_w_in", "w_in", "gdn_conv_w")


def _full_from_shards(name, g):
    if _is_col_sharded(name):
        return jnp.transpose(g, (1, 0, 2)).reshape(g.shape[1], -1)
    return g.reshape(-1, g.shape[2])


def _shards_from_full(name, full):
    if _is_col_sharded(name):
        return jnp.transpose(full.reshape(full.shape[0], 4, -1), (1, 0, 2))
    return full.reshape(4, -1, full.shape[1])


_SMALL = (("ffn1_norm", 8), ("mix_norm", 8), ("hgrn_lb_logits", 16), ("hgrn_out_norm", 8), ("gdn_a_log", 8),
          ("gdn_dt_bias", 8), ("gdn_out_norm", 8), ("ffn2_norm", 8), ("final_norm", 8), ("loss", 8))
_SMALL_ROWS = sum(r for _, r in _SMALL)


def _pack_small(parts):
    out = []
    for name, rows in _SMALL:
        p = parts[name].reshape(-1).astype(F32)
        if p.shape[0] <= 128:
            if p.shape[0] < 128:
                p = jnp.concatenate([p, jnp.zeros((128 - p.shape[0],), F32)])
            p = jnp.broadcast_to(p.reshape(1, 128), (rows, 128))
        out.append(p.reshape(rows, 128))
    return jnp.concatenate(out, axis=0)


def _unpack_small(packed, shapes):
    out, off = {}, 0
    for name, rows in _SMALL:
        n = int(np.prod(shapes[name]))
        out[name] = packed[off:off + rows].reshape(-1)[:n].reshape(shapes[name])
        off += rows
    return out


def _ffn_fwd(x, gain, w_in, w_out, tag):
    n = _rmsnorm_fwd(x, gain, tag + "_norm")
    a, b, hm = _ffn_in_act(n, w_in, tag + "_in")
    out = _mm(hm, w_out, alpha=0.5, res=x, name=tag + "_out")
    return out, (n, a, b)


def _ffn_bwd(x, gain, w_in, w_out, saved, dout, dout_bf, tag):
    n, a, b = saved
    da, db, hm = _ffn_dact(dout_bf, w_out, a, b, tag + "_dact")
    dw_out = _mm(hm, dout_bf, ta=True, alpha=0.5, out_dtype=BF16, name=tag + "_dwout")
    dw_in = jnp.concatenate([_mm(n, da, ta=True, out_dtype=BF16, name=tag + "_dwin_a"),
                             _mm(n, db, ta=True, out_dtype=BF16, name=tag + "_dwin_b")], axis=1)
    dn = _mm(da, w_in[:, :D_FF], tb=True, name=tag + "_dnorm_a")
    dn = _mm(db, w_in[:, D_FF:], tb=True, res=dn, name=tag + "_dnorm_b")
    dx, dx_bf, dgain = _rmsnorm_bwd(x, gain, dn, dout, tag + "_dx")
    return dx, dx_bf, dgain, dw_in, dw_out


def _pad_lanes(v):
    return jnp.concatenate([v.reshape(1, -1), jnp.zeros((1, HEAD - v.size), F32)], axis=1)


def _local_step(x, tgt, small, exchange):
    hg_c = _hg_consts()
    gd_c = _gd_consts()
    alog = _pad_lanes(small["gdn_a_log"])
    dtb = _pad_lanes(small["gdn_dt_bias"])
    logits = small["hgrn_lb_logits"]
    hg_gain = small["hgrn_out_norm"].reshape(1, HEAD)
    gd_gain = small["gdn_out_norm"].reshape(1, HEAD)
    g1, gm, g2 = small["ffn1_norm"].reshape(1, -1), small["mix_norm"].reshape(1, -1), small["ffn2_norm"].reshape(1, -1)
    gf = small["final_norm"].reshape(1, -1)
    qscale = HEAD ** -0.5

    w1 = exchange.weights("ffn1")
    started = exchange.prefetch("mixer")
    h1, ffn1_saved = _ffn_fwd(x, g1 + started, w1["ffn1_w_in"], w1["ffn1_w_out"], "ffn1")
    u = _rmsnorm_fwd(h1, gm, "mix_norm")
    w = exchange.weights("mixer", after=u)
    started = exchange.prefetch("ffn2")
    seg, off = {}, 0
    for name, size in zip(IN_NAMES, IN_SIZES):
        seg[name] = w["w_in"][:, off:off + size]
        off += size
    w_gab = jnp.concatenate([seg["ga"], seg["gb"], jnp.zeros((D_MODEL, HEAD - 32), BF16)], axis=1)
    big_segs = [n for n in IN_NAMES if n not in ("ga", "gb")]
    conv8 = jnp.concatenate([w["gdn_conv_w"].astype(F32), jnp.zeros((8 - CONV_K, 4096), F32)], axis=0)
    conv_q, conv_k, conv_v = conv8[:, :1024], conv8[:, 1024:2048], conv8[:, 2048:]
    w_main = jnp.concatenate([seg[n] for n in big_segs], axis=1)
    proj = _mm(u, w_main, name="proj")
    pr, off = {}, 0
    for n in big_segs:
        pr[n] = _view(proj, off, seg[n].shape[1])
        off += seg[n].shape[1]
    gab = _mm(u, w_gab, name="proj_gab")
    oh_raw, oh, s_h = _hgrn_fwd(pr["hq"], pr["hf"], pr["hi"], pr["hg"], logits, hg_gain + started, hg_c)
    qn = _conv_fwd(pr["gq"], conv_q, qscale, "conv_q")
    kn = _conv_fwd(pr["gk"], conv_k, 1.0, "conv_k")
    cv = _conv_fwd(pr["gv"], conv_v, None, "conv_v")
    og_raw, og, s_g, t_g = _gdn_fwd(qn, kn, cv, gab, pr["gz"], alog, dtb, gd_gain, gd_c)
    yh = _mm(oh, w["w_branch_hgrn"], name="branch_h")
    yg = _mm(og, w["w_branch_gdn"], name="branch_g")
    ym = _merge_fwd(yh, yg, pr["gate_h"], pr["gate_g"])
    h2 = _mm(ym, w["w_out"], res=h1, name="mix_out")
    w2 = exchange.weights("ffn2", after=h2)
    h3, ffn2_saved = _ffn_fwd(h2, g2, w2["ffn2_w_in"], w2["ffn2_w_out"], "ffn2")
    loss, dh3, dh3_bf, d_gf = _final_loss(h3, gf, tgt)

    dh2, dh2_bf, d_g2, d_f2in, d_f2out = _ffn_bwd(h2, g2, w2["ffn2_w_in"], w2["ffn2_w_out"], ffn2_saved, dh3, dh3_bf,
                                                  "ffn2")
    started = exchange.reduce("ffn2", {"ffn2_w_in": d_f2in, "ffn2_w_out": d_f2out}, behind=True)
    dym =_mm(dh2_bf, w["w_out"], tb=True, name="d_merge")
    d_wout = _mm(ym, dh2_bf, ta=True, out_dtype=BF16, name="d_w_out")
    dyh, dyg, d_gate_h, d_gate_g = _merge_bwd(dym, yh, yg, pr["gate_h"], pr["gate_g"])
    d_wbh = _mm(oh, dyh, ta=True, out_dtype=BF16, name="d_w_branch_h")
    d_wbg = _mm(og, dyg, ta=True, out_dtype=BF16, name="d_w_branch_g")
    doh = _mm(dyh, w["w_branch_hgrn"], tb=True, name="d_oh")
    dog = _mm(dyg, w["w_branch_gdn"], tb=True, name="d_og")
    d_hq, d_hf, d_hi, d_hg, d_hg_gain, d_lb0 = _hgrn_bwd(pr["hq"], pr["hf"], pr["hi"], pr["hg"], logits,
                                                        hg_gain + started, oh_raw, s_h, doh, hg_c)
    d_qn, d_kn, d_cv, d_gab_wide, d_gz, gd_small = _gdn_bwd(qn, kn, cv, gab, pr["gz"], alog, dtb, gd_gain, og_raw,
                                                            s_g, t_g, dog, gd_c)
    d_gab = _fold_groups(d_gab_wide)
    dc_q, dwc_q = _conv_bwd_a(pr["gq"], conv_q, d_qn, qscale, "dconv_q")
    dc_k, dwc_k = _conv_bwd_a(pr["gk"], conv_k, d_kn, 1.0, "dconv_k")
    dc_v, dwc_v = _conv_bwd_a(pr["gv"], conv_v, d_cv, None, "dconv_v")
    d_gq = _conv_bwd_b(dc_q, conv_q, "dconvx_q")
    d_gk = _conv_bwd_b(dc_k, conv_k, "dconvx_k")
    d_gv = _conv_bwd_b(dc_v, conv_v, "dconvx_v")
    dpr = {"hq": d_hq, "hf": d_hf, "hi": d_hi, "hg": d_hg, "gq": d_gq, "gk": d_gk, "gv": d_gv, "gz": d_gz,
           "gate_h": d_gate_h, "gate_g": d_gate_g}
    dproj = jnp.concatenate([dpr[n] for n in big_segs], axis=1)
    du = _mm(d_gab, w_gab, tb=True, name="du_gab")
    du = _mm(dproj, w_main, tb=True, res=du, name="du")
    d_wmain = _mm(u, dproj, ta=True, out_dtype=BF16, name="dw_main")
    d_wgab = _mm(u, d_gab, ta=True, out_dtype=BF16, name="dw_gab")
    d_win = jnp.concatenate([d_wmain[:, :8192], d_wgab[:, :32], d_wmain[:, 8192:]], axis=1)
    d_conv = jnp.concatenate([dwc_q[:CONV_K], dwc_k[:CONV_K], dwc_v[:CONV_K]], axis=1).astype(BF16)
    started = exchange.reduce("mixer", {"w_in": d_win, "gdn_conv_w": d_conv, "w_branch_hgrn": d_wbh,
                                        "w_branch_gdn": d_wbg, "w_out": d_wout}, behind=True)
    dh1, dh1_bf, d_gm = _rmsnorm_bwd(h1, gm + started, du, dh2, "mix_dnorm")
    dx, _, d_g1, d_f1in, d_f1out = _ffn_bwd(x, g1, w1["ffn1_w_in"], w1["ffn1_w_out"], ffn1_saved, dh1, dh1_bf, "ffn1")
    exchange.reduce("ffn1", {"ffn1_w_in": d_f1in, "ffn1_w_out": d_f1out}, behind=True)
    d_lb0 = d_lb0.reshape(1, -1)
    sm = {"ffn1_norm": d_g1, "mix_norm": d_gm, "hgrn_lb_logits": jnp.concatenate([d_lb0, -d_lb0], axis=0),
          "hgrn_out_norm": d_hg_gain, "gdn_a_log": gd_small[2, :16], "gdn_dt_bias": gd_small[1, :16],
          "gdn_out_norm": gd_small[0], "ffn2_norm": d_g2, "final_norm": d_gf, "loss": loss[0, :1]}
    return dx, sm


class _Exchange:
    def __init__(self, wts):
        self.wts = wts
        xi, yi, ci = _place()
        self.chip = 2 * xi + yi
        self.south = ci == 0
        self.core = ci.reshape(1).astype(jnp.int32)
        self.mine = {}
        self.coming = {}
        self.going = {}

    def _packs(self, tag):
        group = _GROUPS[tag]
        return _pack({n: self.wts[n][0].astype(BF16) for n in _group_names(group)}, (), group)

    def prefetch(self, tag):
        packs = self._packs(tag)
        handle, token = _ici_start(packs, _GROUPS[tag]["chunks"], "gather", "gather_start_" + tag)
        self.coming[tag] = handle
        return token[0:1, 0:1]

    def weights(self, tag, after=None):
        group = _GROUPS[tag]
        if tag in self.coming:
            packs, halves = _ici_wait(self.coming.pop(tag), after, "gather_wait_" + tag)
            others = _pass_to_sibling(halves, group["chunks"], "gather_pass_" + tag)
        else:
            packs = self._packs(tag)
            others = _gather_weights(packs, group["chunks"], "gather_" + tag)
        whole = [lax.dynamic_update_index_in_dim(g, p, self.chip, 0) for g, p in zip(others, packs)]
        gathered = _unpack(*whole, group)
        return {n: _full_from_shards(n, gathered[n]) for n in _group_names(group)}

    def reduce(self, tag, grads, behind=False):
        group = _GROUPS[tag]
        gpacks = _pack({n: _shards_from_full(n, grads[n]) for n in _group_names(group)}, (4,), group)
        got = _swap_with_sibling(gpacks, group["chunks"], 4, "reduce_pair_" + tag, halves=True)
        sums = [_add2(a, b, self.core, "add_pair_%s_%d" % (tag, i)) for i, (a, b) in enumerate(zip(gpacks, got))]
        if behind:
            handle, token = _ici_start(sums, group["chunks"], "reduce", "reduce_start_" + tag)
            self.going[tag] = handle
            self.token = token
            return token[0:1, 0:1]
        self._add_chips(tag, sums, _reduce_chips(sums, group["chunks"], "reduce_chips_" + tag))
        return None

    def _add_chips(self, tag, sums, from_chips):
        self.mine[tag] = [_add4(lax.dynamic_index_in_dim(s, self.chip, axis=0, keepdims=False), f,
                                "add_chips_%s_%d" % (tag, i)) for i, (s, f) in enumerate(zip(sums, from_chips))]

    def finish(self, tags, after):
        for tag in tags:
            if tag in self.going:
                self._add_chips(tag, *_ici_wait(self.going.pop(tag), after, "reduce_wait_" + tag))
        mine = [a for t in tags for a in self.mine[t]]
        nchs = [k for t in tags for k in _GROUPS[t]["chunks"]]
        theirs = _swap_with_sibling(mine, nchs, 0, "share_pair_" + tags[0])
        whole = [jnp.concatenate([jnp.where(self.south, a, b), jnp.where(self.south, b, a)], axis=0)
                 for a, b in zip(mine, theirs)]
        reduced = {}
        for i, t in enumerate(tags):
            reduced.update(_unpack(whole[2 * i], whole[2 * i + 1], _GROUPS[t]))
        return reduced


_WEIGHTS = ("ffn1_norm", "ffn1_w_in", "ffn1_w_out", "mix_norm", "w_in", "hgrn_lb_logits", "hgrn_out_norm",
            "gdn_conv_w", "gdn_a_log", "gdn_dt_bias", "gdn_out_norm", "w_branch_hgrn", "w_branch_gdn", "w_out",
            "ffn2_norm", "ffn2_w_in", "ffn2_w_out", "final_norm")


def kernel(x, ffn1_norm, ffn1_w_in, ffn1_w_out, mix_norm, w_in, hgrn_lb_logits, hgrn_out_norm, gdn_conv_w, gdn_a_log, gdn_dt_bias, gdn_out_norm, w_branch_hgrn, w_branch_gdn, w_out, ffn2_norm, ffn2_w_in, ffn2_w_out, final_norm, loss_target, m_ffn1_norm, m_ffn1_w_in, m_ffn1_w_out, m_mix_norm, m_w_in, m_hgrn_lb_logits, m_hgrn_out_norm, m_gdn_conv_w, m_gdn_a_log, m_gdn_dt_bias, m_gdn_out_norm, m_w_branch_hgrn, m_w_branch_gdn, m_w_out, m_ffn2_norm, m_ffn2_w_in, m_ffn2_w_out, m_final_norm, v_ffn1_norm, v_ffn1_w_in, v_ffn1_w_out, v_mix_norm, v_w_in, v_hgrn_lb_logits, v_hgrn_out_norm, v_gdn_conv_w, v_gdn_a_log, v_gdn_dt_bias, v_gdn_out_norm, v_w_branch_hgrn, v_w_branch_gdn, v_w_out, v_ffn2_norm, v_ffn2_w_in, v_ffn2_w_out, v_final_norm):
    args = dict(locals())
    wts = {n: args[n] for n in _WEIGHTS}
    moms = {n: args["m_" + n] for n in _WEIGHTS}
    vars_ = {n: args["v_" + n] for n in _WEIGHTS}

    small = {n: wts[n].astype(F32) for n in _WEIGHTS if n not in _BIG_NAMES}
    exchange = _Exchange(wts)
    dx, small_grads = _local_step(x[0], loss_target[0], small, exchange)

    out_g, out_d, out_m, out_v = {}, {}, {}, {}

    def update(tags, reduced, after):
        for t in tags:
            for n in _group_names(_GROUPS[t]):
                shape = wts[n].shape
                w2 = wts[n].reshape(shape[-2], shape[-1])
                g2 = reduced[n]
                d, m2, v2 = _adamw(w2, g2, moms[n].reshape(w2.shape), vars_[n].reshape(w2.shape), "adamw_" + n, after)
                out_g[n], out_d[n], out_m[n], out_v[n] = (g2.reshape(shape), d.reshape(shape), m2.reshape(shape),
                                                          v2.reshape(shape))
                after = v2
        return after

    done = update(("ffn2", "mixer"), exchange.finish(("ffn2", "mixer"), after=dx), exchange.token)
    update(("ffn1",), exchange.finish(("ffn1",), after=done), None)

    small_names = [n for n, _ in _SMALL]
    zero = jnp.zeros((1,), F32)
    shapes = {n: (wts[n].shape if n != "loss" else (1,)) for n in small_names}
    sums, sd, sm_, sv = _small_sync(
        _pack_small(small_grads),
        _pack_small({n: (wts[n] if n != "loss" else zero) for n in small_names}),
        _pack_small({n: (moms[n] if n != "loss" else zero) for n in small_names}),
        _pack_small({n: (vars_[n] if n != "loss" else zero) for n in small_names}))
    sg_u, sd_u, sm_u, sv_u = (_unpack_small(p, shapes) for p in (sums, sd, sm_, sv))
    for n in small_names:
        if n != "loss":
            out_g[n], out_d[n], out_m[n], out_v[n] = sg_u[n], sd_u[n], sm_u[n], sv_u[n]
    loss = sg_u["loss"].reshape(())

    return (loss, dx[None], *[out_g[n] for n in _WEIGHTS], *[out_d[n] for n in _WEIGHTS],
            *[out_m[n] for n in _WEIGHTS], *[out_v[n] for n in _WEIGHTS])
```

```python
import numpy as np

import jax
import jax.numpy as jnp
from jax import lax
from jax.experimental import pallas as pl
from jax.experimental.pallas import tpu as pltpu

F32 = jnp.float32
BF16 = jnp.bfloat16

D_MODEL = 1024
D_FF = 2816
CHUNK = 64
HEAD = 128
HG_HEADS = 8
GD_HEADS = 16
HPS = 8
COMM_CHUNKS = 9
MM_TM = 1408
MM_TN = 1024
MM_TK = 1536
VMEM_LIMIT = 48 * 1024 * 1024
EPS = 1e-6
CONV_K = 4
IN_NAMES = ("hq", "hf", "hi", "hg", "gq", "gk", "gv", "ga", "gb", "gz", "gate_h", "gate_g")
IN_SIZES = (1024, 1024, 1024, 1024, 1024, 1024, 2048, 16, 16, 2048, 1024, 1024)
IN_WIDTH = sum(IN_SIZES)

ADAM_LR = 0.001
ADAM_B1 = 0.9
ADAM_B2 = 0.999
ADAM_EPS = 1e-08
ADAM_WD = 0.01
ADAM_STEP = 10

MESH = pl.DeviceIdType.MESH
_ARB = "arbitrary"
_PAR = "parallel"


def _bf(x):
    return x.astype(BF16)


def _dot(a, b):
    return jnp.dot(_bf(a), _bf(b), preferred_element_type=F32)


def _dot_nt(a, b):
    return lax.dot_general(_bf(a), _bf(b), (((1,), (1,)), ((), ())), preferred_element_type=F32)


def _dot_tn(a, b):
    return lax.dot_general(_bf(a), _bf(b), (((0,), (0,)), ((), ())), preferred_element_type=F32)


def _split3(x):
    hi = _bf(x)
    r = x - hi.astype(F32)
    mid = _bf(r)
    lo = _bf(r - mid.astype(F32))
    return hi, mid, lo


def _dot_mx(m, x):
    hi, mid, lo = _split3(x)
    return (jnp.dot(m, hi, preferred_element_type=F32) + jnp.dot(m, mid, preferred_element_type=F32)
            + jnp.dot(m, lo, preferred_element_type=F32))


def _dot_xm(x, m):
    hi, mid, lo = _split3(x)
    return (jnp.dot(hi, m, preferred_element_type=F32) + jnp.dot(mid, m, preferred_element_type=F32)
            + jnp.dot(lo, m, preferred_element_type=F32))


def _dot_hp(a, b):
    ah = _bf(a)
    al = _bf(a - ah.astype(F32))
    bh = _bf(b)
    bl = _bf(b - bh.astype(F32))
    return (jnp.dot(ah, bh, preferred_element_type=F32) + jnp.dot(ah, bl, preferred_element_type=F32)
            + jnp.dot(al, bh, preferred_element_type=F32))


def _sigmoid(x):
    return jax.nn.sigmoid(x)


def _silu(x):
    return x * _sigmoid(x)


def _dsilu(x):
    s = _sigmoid(x)
    return s * (1.0 + x * (1.0 - s))


def _softplus(x):
    return jnp.maximum(x, 0.0) + jnp.log(1.0 + jnp.exp(-jnp.abs(x)))


def _rowsum(x):
    return jnp.sum(x, axis=1, keepdims=True)


def _col_to_row(col, eye):
    return jnp.sum(eye * col, axis=0, keepdims=True)


def _row_to_col(row, eye):
    return jnp.sum(eye * row, axis=1, keepdims=True)


def _pick(dim, pref, unit=128):
    if dim <= pref:
        return dim
    t = pref
    while t >= unit:
        if dim % t == 0:
            return t
        t -= unit
    return dim


def _params(*sem):
    return pltpu.CompilerParams(dimension_semantics=tuple(sem), vmem_limit_bytes=VMEM_LIMIT)


def _mm(a, b, *, ta=False, tb=False, alpha=1.0, res=None, out_dtype=F32, name="mm"):
    m = a.shape[1] if ta else a.shape[0]
    k = a.shape[0] if ta else a.shape[1]
    n = b.shape[0] if tb else b.shape[1]
    assert k == (b.shape[1] if tb else b.shape[0])
    tm, tn, tk = _pick(m, MM_TM), _pick(n, MM_TN), _pick(k, MM_TK)
    if tn < MM_TN < n and n % MM_TM == 0:
        tn = MM_TM
    nk = k // tk
    a_spec = pl.BlockSpec((tk, tm), lambda i, j, l: (l, i)) if ta else pl.BlockSpec((tm, tk), lambda i, j, l: (i, l))
    b_spec = pl.BlockSpec((tn, tk), lambda i, j, l: (j, l)) if tb else pl.BlockSpec((tk, tn), lambda i, j, l: (l, j))
    o_spec = pl.BlockSpec((tm, tn), lambda i, j, l: (i, j))
    dims = (((0 if ta else 1,), (1 if tb else 0,)), ((), ()))
    has_res = res is not None

    def finish(r, r_ref, o_ref):
        if alpha != 1.0:
            r = r * alpha
        if has_res:
            r = r + r_ref[...]
        o_ref[...] = r.astype(out_dtype)

    def body(*refs):
        a_ref, b_ref = refs[0], refs[1]
        r_ref = refs[2] if has_res else None
        o_ref = refs[3] if has_res else refs[2]
        part = lax.dot_general(_bf(a_ref[...]), _bf(b_ref[...]), dims, preferred_element_type=F32)
        if nk == 1:
            finish(part, r_ref, o_ref)
            return
        acc = refs[-1]
        step = pl.program_id(2)

        @pl.when(step == 0)
        def _():
            acc[...] = part

        @pl.when(step != 0)
        def _():
            acc[...] += part

        @pl.when(step == nk - 1)
        def _():
            finish(acc[...], r_ref, o_ref)

    ins = [a, b] + ([res] if has_res else [])
    in_specs = [a_spec, b_spec] + ([o_spec] if has_res else [])
    return pl.pallas_call(
        body, name=name, grid=(m // tm, n // tn, nk), in_specs=in_specs, out_specs=o_spec,
        out_shape=jax.ShapeDtypeStruct((m, n), out_dtype),
        scratch_shapes=[pltpu.VMEM((tm, tn), F32)] if nk > 1 else [],
        compiler_params=_params(_PAR, _PAR, _ARB))(*ins)


def _row_spec(tr, w):
    return pl.BlockSpec((tr, w), lambda i: (i, 0))


def _full_spec(shape):
    return pl.BlockSpec(shape, lambda i: tuple(0 for _ in shape))


def _view(arr, off, width):
    return arr, off, width


def _view_rows(view, tr):
    _, off, width = view
    assert off % width == 0
    return pl.BlockSpec((tr, width), lambda i: (i, off // width))


def _view_tile(view, rows, bw, cidx=lambda c: c):
    _, off, width = view
    assert off % bw == 0 and width % bw == 0
    return pl.BlockSpec((rows, bw), lambda c, g: (cidx(c), off // bw + g))


def _rmsnorm_fwd(x, g, name):
    t, d = x.shape
    tr = _pick(t, 256, 8)

    def body(x_ref, g_ref, o_ref):
        xv = x_ref[...]
        r = lax.rsqrt(jnp.mean(xv * xv, axis=1, keepdims=True) + EPS)
        o_ref[...] = (xv * r * g_ref[...]).astype(BF16)

    return pl.pallas_call(
        body, name=name, grid=(t // tr,), in_specs=[_row_spec(tr, d), _full_spec((1, d))],
        out_specs=_row_spec(tr, d), out_shape=jax.ShapeDtypeStruct((t, d), BF16),
        compiler_params=_params(_PAR))(x, g)


def _rmsnorm_bwd(x, g, dn, res, name):
    t, d = x.shape
    tr = _pick(t, 256, 8)

    def body(x_ref, g_ref, dn_ref, r_ref, dx_ref, dxb_ref, dg_ref):
        @pl.when(pl.program_id(0) == 0)
        def _():
            dg_ref[...] = jnp.zeros_like(dg_ref)

        xv = x_ref[...]
        r = lax.rsqrt(jnp.mean(xv * xv, axis=1, keepdims=True) + EPS)
        xh = xv * r
        dy = dn_ref[...]
        dg_ref[...] += jnp.sum(dy * xh, axis=0, keepdims=True)
        dxh = dy * g_ref[...]
        dx = r_ref[...] + r * (dxh - xh * jnp.mean(dxh * xh, axis=1, keepdims=True))
        dx_ref[...] = dx
        dxb_ref[...] = dx.astype(BF16)

    return pl.pallas_call(
        body, name=name, grid=(t // tr,),
        in_specs=[_row_spec(tr, d), _full_spec((1, d)), _row_spec(tr, d), _row_spec(tr, d)],
        out_specs=[_row_spec(tr, d), _row_spec(tr, d), _full_spec((1, d))],
        out_shape=[jax.ShapeDtypeStruct((t, d), F32), jax.ShapeDtypeStruct((t, d), BF16),
                   jax.ShapeDtypeStruct((1, d), F32)],
        compiler_params=_params(_ARB))(x, g, dn, res)


FFN_TN = 1408
FFN_TM = 512


def _ffn_in_act(n, w_in, name):
    t, d = n.shape
    tm = _pick(t, FFN_TM)
    nf = D_FF // FFN_TN

    def body(n_ref, wa_ref, wb_ref, a_ref, b_ref, hm_ref):
        nv = n_ref[...]
        a = jnp.dot(nv, wa_ref[...], preferred_element_type=F32)
        b = jnp.dot(nv, wb_ref[...], preferred_element_type=F32)
        a_ref[...] = a.astype(BF16)
        b_ref[...] = b.astype(BF16)
        hm_ref[...] = (_silu(a) * b).astype(BF16)

    tile = pl.BlockSpec((tm, FFN_TN), lambda i, j: (i, j))
    return pl.pallas_call(
        body, name=name, grid=(t // tm, nf),
        in_specs=[pl.BlockSpec((tm, d), lambda i, j: (i, 0)), pl.BlockSpec((d, FFN_TN), lambda i, j: (0, j)),
                  pl.BlockSpec((d, FFN_TN), lambda i, j: (0, nf + j))],
        out_specs=[tile, tile, tile], out_shape=[jax.ShapeDtypeStruct((t, D_FF), BF16)] * 3,
        compiler_params=_params(_PAR, _PAR))(n, w_in, w_in)


def _ffn_dact(dout, w_out, a, b, name):
    t, d = dout.shape
    tm = _pick(t, FFN_TM)

    def body(do_ref, w_ref, a_ref, b_ref, da_ref, db_ref, hm_ref):
        dh = 0.5 * _dot_nt(do_ref[...], w_ref[...])
        av = a_ref[...].astype(F32)
        bv = b_ref[...].astype(F32)
        sa = _silu(av)
        da_ref[...] = (dh * bv * _dsilu(av)).astype(BF16)
        db_ref[...] = (dh * sa).astype(BF16)
        hm_ref[...] = (sa * bv).astype(BF16)

    tile = pl.BlockSpec((tm, FFN_TN), lambda i, j: (i, j))
    return pl.pallas_call(
        body, name=name, grid=(t // tm, D_FF // FFN_TN),
        in_specs=[pl.BlockSpec((tm, d), lambda i, j: (i, 0)), pl.BlockSpec((FFN_TN, d), lambda i, j: (j, 0)), tile, tile],
        out_specs=[tile, tile, tile], out_shape=[jax.ShapeDtypeStruct((t, D_FF), BF16)] * 3,
        compiler_params=_params(_PAR, _PAR))(dout, w_out, a, b)


def _merge_fwd(yh, yg, gh, gg):
    t, d = yh.shape
    tr = _pick(t, 256, 8)

    def body(yh_ref, yg_ref, gh_ref, gg_ref, o_ref):
        o_ref[...] = (_sigmoid(gh_ref[...]) * yh_ref[...] + _sigmoid(gg_ref[...]) * yg_ref[...]).astype(BF16)

    return pl.pallas_call(
        body, name="merge_fwd", grid=(t // tr,),
        in_specs=[_row_spec(tr, d), _row_spec(tr, d), _view_rows(gh, tr), _view_rows(gg, tr)],
        out_specs=_row_spec(tr, d),
        out_shape=jax.ShapeDtypeStruct((t, d), BF16), compiler_params=_params(_PAR))(yh, yg, gh[0], gg[0])


def _merge_bwd(dy, yh, yg, gh, gg):
    t, d = yh.shape
    tr = _pick(t, 256, 8)

    def body(dy_ref, yh_ref, yg_ref, gh_ref, gg_ref, dyh_ref, dyg_ref, dgh_ref, dgg_ref):
        dyv = dy_ref[...]
        sh = _sigmoid(gh_ref[...])
        sg = _sigmoid(gg_ref[...])
        dyh_ref[...] = (dyv * sh).astype(BF16)
        dyg_ref[...] = (dyv * sg).astype(BF16)
        dgh_ref[...] = (dyv * yh_ref[...] * sh * (1.0 - sh)).astype(BF16)
        dgg_ref[...] = (dyv * yg_ref[...] * sg * (1.0 - sg)).astype(BF16)

    return pl.pallas_call(
        body, name="merge_bwd", grid=(t // tr,),
        in_specs=[_row_spec(tr, d)] * 3 + [_view_rows(gh, tr), _view_rows(gg, tr)], out_specs=[_row_spec(tr, d)] * 4,
        out_shape=[jax.ShapeDtypeStruct((t, d), BF16)] * 4,
        compiler_params=_params(_PAR))(dy, yh, yg, gh[0], gg[0])


def _final_loss(h, g, tgt):
    t, d = h.shape
    tr = _pick(t, 256, 8)

    def body(h_ref, g_ref, t_ref, loss_ref, dh_ref, dhb_ref, dg_ref):
        @pl.when(pl.program_id(0) == 0)
        def _():
            dg_ref[...] = jnp.zeros_like(dg_ref)
            loss_ref[...] = jnp.zeros_like(loss_ref)

        xv = h_ref[...]
        gv = g_ref[...]
        r = lax.rsqrt(jnp.mean(xv * xv, axis=1, keepdims=True) + EPS)
        xh = xv * r
        err = xh * gv - t_ref[...]
        loss_ref[...] += 0.5 * jnp.sum(jnp.mean(err * err, axis=1, keepdims=True), axis=0, keepdims=True)
        dy = err * (1.0 / d)
        dg_ref[...] += jnp.sum(dy * xh, axis=0, keepdims=True)
        dxh = dy * gv
        dh = r * (dxh - xh * jnp.mean(dxh * xh, axis=1, keepdims=True))
        dh_ref[...] = dh
        dhb_ref[...] = dh.astype(BF16)

    return pl.pallas_call(
        body, name="final_loss", grid=(t // tr,),
        in_specs=[_row_spec(tr, d), _full_spec((1, d)), _row_spec(tr, d)],
        out_specs=[_full_spec((1, 128)), _row_spec(tr, d), _row_spec(tr, d), _full_spec((1, d))],
        out_shape=[jax.ShapeDtypeStruct((1, 128), F32), jax.ShapeDtypeStruct((t, d), F32),
                   jax.ShapeDtypeStruct((t, d), BF16), jax.ShapeDtypeStruct((1, d), F32)],
        compiler_params=_params(_ARB))(h, g, tgt)


def _hg_consts():
    c = CHUNK
    t = np.arange(c)
    mats, masks = [], []
    for lvl in range(6):
        m = 1 << lvl
        blk = t // m
        mat = np.zeros((c, c), np.float32)
        for tt in range(c):
            b = blk[tt]
            if b % 2 == 1:
                mat[tt, b * m:tt + 1] = 1.0
            else:
                mat[tt, tt + 1:(b + 1) * m] = 1.0
        mats.append(mat)
        same = (t[:, None] // (2 * m)) == (t[None, :] // (2 * m))
        masks.append((same & (blk[:, None] % 2 == 1) & (blk[None, :] % 2 == 0)).astype(np.float32))
    pre = np.tril(np.ones((c, c), np.float32))
    suf = np.triu(np.ones((c, c), np.float32), 1)
    mstack = np.concatenate(mats + [pre, suf], 0)
    masks.append(np.eye(c, dtype=np.float32))
    return (jnp.asarray(mstack, BF16), jnp.asarray(mstack.T.copy(), BF16), jnp.asarray(np.stack(masks), F32),
            jnp.asarray(np.eye(HEAD, dtype=np.float32)))


def _gd_consts():
    c = CHUNK
    incl = np.tril(np.ones((c, c), np.float32))
    strict = np.tril(np.ones((c, c), np.float32), -1)
    eye = np.eye(c, dtype=np.float32)
    masks = np.stack([incl, strict, eye, incl.T.copy()])
    return jnp.asarray(incl, BF16), jnp.asarray(incl.T.copy(), BF16), jnp.asarray(masks, F32)


def _chunks_per_step(nc):
    for cb in (32 // HPS, 2, 1):
        if nc % cb == 0:
            return cb
    return 1


def _hg_prep(hq, hf, lg):
    lb = _sigmoid(lg[0:1, :] - lg[1:2, :])
    sg = _sigmoid(hf)
    sgn = _sigmoid(-hf)
    f = lb + (1.0 - lb) * sg
    lf = jnp.log(f)
    kk = (1.0 - lb) * sgn
    q = _silu(hq) * (HEAD ** -0.5)
    return lb, sg, sgn, f, lf, kk, q


def _mx_each(m, xs):
    wide = [jnp.concatenate(_split3(x), axis=1) for x in xs]
    prods = [jnp.dot(m, w, preferred_element_type=F32) for w in wide]
    return [p[:, :HEAD] + p[:, HEAD:2 * HEAD] + p[:, 2 * HEAD:] for p in prods]


def _hg_scaled(x, ex):
    xb = [_bf(a) for a in x]
    eb = [_bf(e[:6 * CHUNK]) for e in ex]
    return [[a * e[lvl * CHUNK:(lvl + 1) * CHUNK] for lvl in range(6)] for a, e in zip(xb, eb)]


def _hg_scores(q, kk, qe, ke, mask_ref):
    p = [mask_ref[6] * _rowsum(a * b) for a, b in zip(q, kk)]
    for lvl in range(6):
        d = [_dot_nt(a[lvl], b[lvl]) for a, b in zip(qe, ke)]
        p = [x + mask_ref[lvl] * y for x, y in zip(p, d)]
    return p


def _hgrn_fwd(hq, hf, hi, hg, logits, gain, consts):
    t = hq[0].shape[0]
    nc = t // CHUNK
    cb = _chunks_per_step(nc)
    rows = cb * CHUNK
    mstack, _, masks, eye = consts
    tile = pl.BlockSpec((rows, HPS * HEAD), lambda c, g: (c, g))

    def body(hq_ref, hf_ref, hi_ref, hg_ref, lg_ref, gain_ref, m_ref, mask_ref, eye_ref,
             oraw_ref, og_ref, ssave_ref, state):
        c = pl.program_id(0)
        g = pl.program_id(1)

        @pl.when(c == 0)
        def _():
            for hh in range(HPS):
                state[g * HPS + hh] = jnp.zeros((HEAD, HEAD), F32)

        lg_all = lg_ref[...]
        gain_v = gain_ref[...]

        def one(i, carry):
            sl = pl.ds(pl.multiple_of(i * CHUNK, CHUNK), CHUNK)
            hs = range(HPS)
            heads = [g * HPS + hh for hh in hs]
            ln = [slice(hh * HEAD, (hh + 1) * HEAD) for hh in hs]
            preps = [_hg_prep(hq_ref[sl, s], hf_ref[sl, s], lg_all[:, s]) for s in ln]
            lf, kk, q = [p[4] for p in preps], [p[5] for p in preps], [p[6] for p in preps]
            v = [hi_ref[sl, s] for s in ln]
            ex = [jnp.exp(x) for x in _mx_each(m_ref[...], lf)]
            eb = [e[6 * CHUNK:7 * CHUNK] for e in ex]
            esfx = [e[7 * CHUNK:8 * CHUNK] for e in ex]
            qe, ke = _hg_scaled(q, ex), _hg_scaled(kk, ex)
            p = _hg_scores(q, kk, qe, ke, mask_ref)
            s0 = [state[h] for h in heads]
            o = _each(lambda a, e, s, pp, vv: _dot(a * e, s) + _dot(pp, vv), q, eb, s0, p, v)
            eye_v = eye_ref[...]
            s1 = _each(lambda s, e, kx, ef, vv: s * _row_to_col(e[CHUNK - 1:CHUNK, :], eye_v) + _dot_tn(kx * ef, vv),
                       s0, eb, kk, esfx, v)
            for hh in hs:
                ssave_ref[i, hh] = s0[hh]
                state[heads[hh]] = s1[hh]
                oraw_ref[sl, ln[hh]] = o[hh]
                r = lax.rsqrt(jnp.mean(o[hh] * o[hh], axis=1, keepdims=True) + EPS)
                og_ref[sl, ln[hh]] = (o[hh] * r * gain_v * _silu(hg_ref[sl, ln[hh]])).astype(BF16)
            return carry

        lax.fori_loop(0, cb, one, 0, unroll=2)

    return pl.pallas_call(
        body, name="hgrn_fwd", grid=(nc // cb, HG_HEADS // HPS),
        in_specs=[_view_tile(v, rows, HPS * HEAD) for v in (hq, hf, hi, hg)] + [
                  pl.BlockSpec((2, HPS * HEAD), lambda c, g: (0, g)),
                  pl.BlockSpec((1, HEAD), lambda c, g: (0, 0)),
                  pl.BlockSpec(mstack.shape, lambda c, g: (0, 0)),
                  pl.BlockSpec(masks.shape, lambda c, g: (0, 0, 0)),
                  pl.BlockSpec(eye.shape, lambda c, g: (0, 0))],
        out_specs=[tile, tile, pl.BlockSpec((cb, HPS, HEAD, HEAD), lambda c, g: (c, g, 0, 0))],
        out_shape=[jax.ShapeDtypeStruct((t, HG_HEADS * HEAD), F32), jax.ShapeDtypeStruct((t, HG_HEADS * HEAD), BF16),
                   jax.ShapeDtypeStruct((nc, HG_HEADS, HEAD, HEAD), F32)],
        scratch_shapes=[pltpu.VMEM((HG_HEADS, HEAD, HEAD), F32)],
        compiler_params=_params(_ARB, _ARB))(hq[0], hf[0], hi[0], hg[0], logits, gain, mstack, masks, eye)


def _hgrn_bwd(hq, hf, hi, hg, logits, gain, oraw, ssave, dog, consts):
    t = hq[0].shape[0]
    nc = t // CHUNK
    cb = _chunks_per_step(nc)
    rows = cb * CHUNK
    nb = nc // cb
    mstack, mstack_t, masks, eye = consts
    tile = pl.BlockSpec((rows, HPS * HEAD), lambda c, g: (nb - 1 - c, g))

    def body(hq_ref, hf_ref, hi_ref, hg_ref, lg_ref, gain_ref, oraw_ref, ssave_ref, dog_ref, m_ref, mt_ref,
             mask_ref, eye_ref, dhq_ref, dhf_ref, dhi_ref, dhg_ref, dgain_ref, dlb_ref, dstate):
        c = pl.program_id(0)
        g = pl.program_id(1)

        @pl.when(c == 0)
        def _():
            for hh in range(HPS):
                dstate[g * HPS + hh] = jnp.zeros((HEAD, HEAD), F32)

        @pl.when((c == 0) & (g == 0))
        def _():
            dgain_ref[...] = jnp.zeros_like(dgain_ref)
            dlb_ref[...] = jnp.zeros_like(dlb_ref)

        lg_all = lg_ref[...]
        gain_v = gain_ref[...]
        eye_v = eye_ref[...]
        last_row = (lax.broadcasted_iota(jnp.int32, (CHUNK, HEAD), 0) == CHUNK - 1).astype(F32)

        def one(j, carry):
            i = cb - 1 - j
            sl = pl.ds(pl.multiple_of(i * CHUNK, CHUNK), CHUNK)
            hs = range(HPS)
            heads = [g * HPS + hh for hh in hs]
            ln = [slice(hh * HEAD, (hh + 1) * HEAD) for hh in hs]
            hqv = [hq_ref[sl, s] for s in ln]
            hgv = [hg_ref[sl, s] for s in ln]
            preps = [_hg_prep(a, hf_ref[sl, s], lg_all[:, s]) for a, s in zip(hqv, ln)]
            lb, sg, sgn, f, lf, kk, q = ([p[n] for p in preps] for n in range(7))
            v = [hi_ref[sl, s] for s in ln]
            ex = [jnp.exp(x) for x in _mx_each(m_ref[...], lf)]
            eb = [e[6 * CHUNK:7 * CHUNK] for e in ex]
            esfx = [e[7 * CHUNK:8 * CHUNK] for e in ex]
            qe, ke = _hg_scaled(q, ex), _hg_scaled(kk, ex)
            p = _hg_scores(q, kk, qe, ke, mask_ref)
            s0 = [ssave_ref[i, hh] for hh in hs]
            ds = [dstate[h] for h in heads]

            o = [oraw_ref[sl, s] for s in ln]
            r = [lax.rsqrt(jnp.mean(x * x, axis=1, keepdims=True) + EPS) for x in o]
            on = _each(lambda x, y: x * y, o, r)
            dg_out = [dog_ref[sl, s] for s in ln]
            sgate = [_silu(x) for x in hgv]
            for hh in hs:
                dhg_ref[sl, ln[hh]] = (dg_out[hh] * on[hh] * gain_v * _dsilu(hgv[hh])).astype(BF16)
            dgain_ref[...] += sum(jnp.sum(d * s * n, axis=0, keepdims=True) for d, s, n in zip(dg_out, sgate, on))
            don = _each(lambda d, s: d * s * gain_v, dg_out, sgate)
            do = _each(lambda rr, dn, n: rr * (dn - n * jnp.mean(dn * n, axis=1, keepdims=True)), r, don, on)

            dp = _each(_dot_nt, do, v)
            dv = _each(lambda pp, d, kx, ef, s: _dot_tn(pp, d) + _dot(kx * ef, s), p, do, kk, esfx, ds)
            dqb = _each(_dot_nt, do, s0)
            dkx = _each(_dot_nt, v, ds)
            diag = [_rowsum(mask_ref[6] * x) for x in dp]
            dq = _each(lambda a, e, d, kx: a * e + d * kx, dqb, eb, diag, kk)
            dk = _each(lambda a, e, d, qq: a * e + d * qq, dkx, esfx, diag, q)
            dxs = [[] for _ in hs]
            for lvl in range(6):
                el = [e[lvl * CHUNK:(lvl + 1) * CHUNK] for e in ex]
                gm = [mask_ref[lvl] * x for x in dp]
                gm = [_bf(x) for x in gm]
                a1 = _each(lambda m_, kx: _dot(m_, kx[lvl]), gm, ke)
                a2 = _each(lambda m_, qq: _dot_tn(m_, qq[lvl]), gm, qe)
                dq = _each(lambda x, a, e: x + a * e, dq, a1, el)
                dk = _each(lambda x, a, e: x + a * e, dk, a2, el)
                for hh in hs:
                    dxs[hh].append((a1[hh] * q[hh] + a2[hh] * kk[hh]) * el[hh])
            e_end_row = [e[CHUNK - 1:CHUNK, :] for e in eb]
            ds_new = _each(lambda qq, e, d, er, s: _dot_tn(qq * e, d) + _row_to_col(er, eye_v) * s, q, eb, do, e_end_row, ds)
            for hh in hs:
                dstate[heads[hh]] = ds_new[hh]
                dend_row = _col_to_row(_rowsum(s0[hh] * ds[hh]), eye_v)
                dxs[hh].append(dqb[hh] * q[hh] * eb[hh] + last_row * (e_end_row[hh] * dend_row))
                dxs[hh].append(dkx[hh] * kk[hh] * esfx[hh])
            dlf = _mx_each(mt_ref[...], [jnp.concatenate(x, axis=0) for x in dxs])

            for hh in hs:
                dhi_ref[sl, ln[hh]] = dv[hh].astype(BF16)
                dhq_ref[sl, ln[hh]] = (dq[hh] * (HEAD ** -0.5) * _dsilu(hqv[hh])).astype(BF16)
                df = dlf[hh] / f[hh]
                dsig = (1.0 - lb[hh]) * sg[hh] * sgn[hh]
                dhf_ref[sl, ln[hh]] = ((df - dk[hh]) * dsig).astype(BF16)
                dlb_t = jnp.sum(df * sgn[hh] - dk[hh] * sgn[hh], axis=0, keepdims=True)
                dlb_ref[pl.ds(heads[hh], 1), :] += dlb_t * lb[hh] * (1.0 - lb[hh])
            return carry

        lax.fori_loop(0, cb, one, 0, unroll=2)

    outs = [jax.ShapeDtypeStruct((t, HG_HEADS * HEAD), BF16)] * 4 + [
        jax.ShapeDtypeStruct((1, HEAD), F32), jax.ShapeDtypeStruct((HG_HEADS, HEAD), F32)]
    return pl.pallas_call(
        body, name="hgrn_bwd", grid=(nb, HG_HEADS // HPS),
        in_specs=[_view_tile(v, rows, HPS * HEAD, lambda c: nb - 1 - c) for v in (hq, hf, hi, hg)] + [
                  pl.BlockSpec((2, HPS * HEAD), lambda c, g: (0, g)),
                  pl.BlockSpec((1, HEAD), lambda c, g: (0, 0)), tile,
                  pl.BlockSpec((cb, HPS, HEAD, HEAD), lambda c, g: (nb - 1 - c, g, 0, 0)), tile,
                  pl.BlockSpec(mstack.shape, lambda c, h: (0, 0)),
                  pl.BlockSpec(mstack_t.shape, lambda c, h: (0, 0)),
                  pl.BlockSpec(masks.shape, lambda c, h: (0, 0, 0)),
                  pl.BlockSpec(eye.shape, lambda c, h: (0, 0))],
        out_specs=[tile, tile, tile, tile, pl.BlockSpec((1, HEAD), lambda c, h: (0, 0)),
                   pl.BlockSpec((HG_HEADS, HEAD), lambda c, h: (0, 0))],
        out_shape=outs, scratch_shapes=[pltpu.VMEM((HG_HEADS, HEAD, HEAD), F32)],
        compiler_params=_params(_ARB, _ARB))(hq[0], hf[0], hi[0], hg[0], logits, gain, oraw, ssave, dog, mstack,
                                             mstack_t, masks, eye)


CONV_W = 512


def _per_head(fn, *arrs):
    width = arrs[0].shape[1]
    return jnp.concatenate([fn(*[a[:, j:j + HEAD] for a in arrs]) for j in range(0, width, HEAD)], axis=1)


def _shift_down(xv, halo, d, top_rows):
    if d == 0:
        return xv, xv[0:8]
    main = pltpu.roll(xv, d, 0)
    top = jnp.where(top_rows < d, pltpu.roll(halo, d, 0), main[0:8])
    return main, top


def _conv_parts(x_ref, halo_ref, w_ref, first):
    xv = x_ref[...]
    halo = jnp.where(first, 0.0, halo_ref[...])
    top_rows = lax.broadcasted_iota(jnp.int32, (8, xv.shape[1]), 0)
    shifted = [_shift_down(xv, halo, CONV_K - 1 - j, top_rows) for j in range(CONV_K)]
    w = w_ref[...]
    acc = sum(shifted[j][0] * w[j:j + 1, :] for j in range(CONV_K))
    acc_top = sum(shifted[j][1] * w[j:j + 1, :] for j in range(CONV_K))
    return shifted, acc, acc_top


def _conv_fwd(x, w8, l2scale, name):
    x, off, width = x
    t = x.shape[0]
    o = off // CONV_W
    tr = _pick(t, 512, 8)

    def post(cv):
        s = _silu(cv)
        if l2scale is not None:
            s = _per_head(lambda sh: sh * (lax.rsqrt(_rowsum(sh * sh) + EPS) * l2scale), s)
        return s

    def body(x_ref, halo_ref, w_ref, o_ref):
        _, acc, acc_top = _conv_parts(x_ref, halo_ref, w_ref, pl.program_id(1) == 0)
        o_ref[...] = post(acc)
        o_ref[0:8, :] = post(acc_top)

    return pl.pallas_call(
        body, name=name, grid=(width // CONV_W,t // tr),
        in_specs=[pl.BlockSpec((tr, CONV_W), lambda j, i: (i, o + j)),
                  pl.BlockSpec((8, CONV_W), lambda j, i: (jnp.maximum(i * (tr // 8) - 1, 0), o + j)),
                  pl.BlockSpec((8, CONV_W), lambda j, i: (0, j))],
        out_specs=pl.BlockSpec((tr, CONV_W), lambda j, i: (i, j)),
        out_shape=jax.ShapeDtypeStruct((t, width), F32), compiler_params=_params(_PAR, _PAR))(x, x, w8)


def _conv_bwd_a(x, w8, dy, l2scale, name):
    x, off, width = x
    t = x.shape[0]
    o = off // CONV_W
    tr = _pick(t, 512, 8)

    def l2_bwd(s, dyh):
        r = lax.rsqrt(_rowsum(s * s) + EPS)
        y0 = s * r
        dy0 = dyh * l2scale
        return r * (dy0 - y0 * _rowsum(dy0 * y0))

    def to_dc(cv, dyv):
        if l2scale is not None:
            dyv = _per_head(l2_bwd, _silu(cv), dyv)
        return dyv * _dsilu(cv)

    def body(x_ref, halo_ref, w_ref, dy_ref, dc_ref, dw_ref):
        @pl.when(pl.program_id(1) == 0)
        def _():
            dw_ref[...] = jnp.zeros_like(dw_ref)

        shifted, acc, acc_top = _conv_parts(x_ref, halo_ref, w_ref, pl.program_id(1) == 0)
        dyv = dy_ref[...]
        dc = to_dc(acc, dyv)
        dc_top = to_dc(acc_top, dyv[0:8])
        dc_ref[...] = dc
        dc_ref[0:8, :] = dc_top
        rest = (lax.broadcasted_iota(jnp.int32, dc.shape, 0) >= 8).astype(F32)
        dc_rest = dc * rest
        for j in range(CONV_K):
            dw_ref[j:j + 1, :] += (jnp.sum(dc_rest * shifted[j][0], axis=0, keepdims=True)
                                   + jnp.sum(dc_top * shifted[j][1], axis=0, keepdims=True))

    return pl.pallas_call(
        body, name=name, grid=(width // CONV_W,t // tr),
        in_specs=[pl.BlockSpec((tr, CONV_W), lambda j, i: (i, o + j)),
                  pl.BlockSpec((8, CONV_W), lambda j, i: (jnp.maximum(i * (tr // 8) - 1, 0), o + j)),
                  pl.BlockSpec((8, CONV_W), lambda j, i: (0, j)),
                  pl.BlockSpec((tr, CONV_W), lambda j, i: (i, j))],
        out_specs=[pl.BlockSpec((tr, CONV_W), lambda j, i: (i, j)), pl.BlockSpec((8, CONV_W), lambda j, i: (0, j))],
        out_shape=[jax.ShapeDtypeStruct((t, width), F32), jax.ShapeDtypeStruct((8, width), F32)],
        compiler_params=_params(_PAR, _ARB))(x, x, w8, dy)


def _conv_bwd_b(dc, w8, name):
    t, width = dc.shape
    tr = _pick(t, 512, 8)
    nt = t // tr

    def body(dc_ref, halo_ref, w_ref, dx_ref):
        dcv = dc_ref[...]
        halo = jnp.where(pl.program_id(1) == nt - 1, 0.0, halo_ref[...])
        w = w_ref[...]
        bot_rows = lax.broadcasted_iota(jnp.int32, (8, CONV_W), 0)
        acc = dcv * w[CONV_K - 1:CONV_K, :]
        acc_bot = dcv[tr - 8:tr] * w[CONV_K - 1:CONV_K, :]
        for d in range(1, CONV_K):
            main = pltpu.roll(dcv, tr - d, 0)
            bot = jnp.where(bot_rows >= 8 - d, pltpu.roll(halo, 8 - d, 0), main[tr - 8:tr])
            wj = w[CONV_K - 1 - d:CONV_K - d, :]
            acc = acc + main * wj
            acc_bot = acc_bot + bot * wj
        dx_ref[...] = acc.astype(BF16)
        dx_ref[tr - 16:tr, :] = jnp.concatenate([acc[tr - 16:tr - 8], acc_bot], axis=0).astype(BF16)

    return pl.pallas_call(
        body, name=name, grid=(width // CONV_W,nt),
        in_specs=[pl.BlockSpec((tr, CONV_W), lambda j, i: (i, j)),
                  pl.BlockSpec((8, CONV_W), lambda j, i: (jnp.minimum((i + 1) * (tr // 8), t // 8 - 1), j)),
                  pl.BlockSpec((8, CONV_W), lambda j, i: (0, j))],
        out_specs=pl.BlockSpec((tr, CONV_W), lambda j, i: (i, j)),
        out_shape=jax.ShapeDtypeStruct((t, width), BF16), compiler_params=_params(_PAR, _PAR))(dc, dc, w8)


def _each(f, *lists):
    return [f(*xs) for xs in zip(*lists)]


def _split2_each(xs):
    hi = [_bf(x) for x in xs]
    lo = [_bf(x - h.astype(F32)) for x, h in zip(xs, hi)]
    return hi, lo


def _hp_each(a_split, b_split):
    (ah, al), (bh, bl) = a_split, b_split
    rows = ah[0].shape[0]
    d12 = [jnp.dot(jnp.concatenate([x, y], axis=0), z, preferred_element_type=F32) for x, y, z in zip(ah, al, bh)]
    d3 = [jnp.dot(x, y, preferred_element_type=F32) for x, y in zip(ah, bl)]
    return [d[:rows] + d[rows:] + e for d, e in zip(d12, d3)]


def _tri_inv_each(a_list, eye):
    ns = [-a for a in a_list]
    ps = [eye + n for n in ns]
    n_split = _split2_each(ns)
    for _ in range(5):
        ns = _hp_each(n_split, n_split)
        n_split = _split2_each(ns)
        ps = [p + d for p, d in zip(ps, _hp_each(_split2_each(ps), n_split))]
    return ps


def _gd_gates(gab, alog, dtb):
    sp_arg = gab + dtb
    return sp_arg, -jnp.exp(alog) * _softplus(sp_arg), _sigmoid(gab)


def _pick_lane(tile, base, head):
    g, hh = head
    col = tile[:, base + hh:base + hh + 1]
    for gi in range(1, GD_HEADS // HPS):
        lane = base + gi * HPS + hh
        col = jnp.where(g == gi, tile[:, lane:lane + 1], col)
    return col


def _gd_chunks(q, k, v, g_all, beta_all, sel, l_ref, mask_ref, tm=None):
    incl, strict, eye, upper = mask_ref[0], mask_ref[1], mask_ref[2], mask_ref[3]
    lmat = l_ref[...]
    gb = [jnp.broadcast_to(_pick_lane(g_all, 0, s), (CHUNK, HEAD)) for s in sel]
    bb = [jnp.broadcast_to(_pick_lane(beta_all, GD_HEADS, s), (CHUNK, HEAD)) for s in sel]
    gam = _mx_each(lmat, gb)
    gam_row = [jnp.sum(x[:, :CHUNK] * upper, axis=0, keepdims=True) for x in gb]
    lm = _each(lambda gm, gr: incl * jnp.exp(jnp.minimum(gm[:, :CHUNK] - gr, 0.0)), gam, gam_row)
    kb = _each(lambda x, b: x * b, k, bb)
    a = _each(lambda x, y, m: strict * _dot_nt(x, y) * m, kb, k, lm)
    if tm is None:
        tm = _tri_inv_each(a, eye)
    eg = [jnp.exp(x) for x in gam]
    vb = _each(lambda x, b: x * b, v, bb)
    kbg = _each(lambda x, e: x * e, kb, eg)
    uw = _each(lambda t_, x, y: _dot(t_, jnp.concatenate([x, y], axis=1)), tm, vb, kbg)
    u = [x[:, :HEAD] for x in uw]
    w = [x[:, HEAD:] for x in uw]
    qk = _each(lambda x, y, m: _dot_nt(x, y) * m, q, k, lm)
    g_end = [x[CHUNK - 1:CHUNK, :] for x in gam]
    ekg = _each(lambda e, x: jnp.exp(e - x), g_end, gam)
    ge = [jnp.exp(e) for e in g_end]
    kg = _each(lambda x, e: x * e, k, ekg)
    qg = _each(lambda x, e: x * e, q, eg)
    names = ("bb", "lm", "kb", "a", "tm", "eg", "vb", "kbg", "u", "w", "qk", "ekg", "ge", "kg", "qg")
    cols = (bb, lm, kb, a, tm, eg, vb, kbg, u, w, qk, ekg, ge, kg, qg)
    return [dict(zip(names, vals)) for vals in zip(*cols)]


def _gd_specs(rows, rev_nb=None):
    def cidx(c):
        return c if rev_nb is None else rev_nb - 1 - c

    qk_tile = pl.BlockSpec((rows, HPS // 2 * HEAD), lambda c, g: (cidx(c), g))
    v_tile = pl.BlockSpec((rows, HPS * HEAD), lambda c, g: (cidx(c), g))
    gab_tile = pl.BlockSpec((rows, HEAD), lambda c, g: (cidx(c), 0))
    return qk_tile, v_tile, gab_tile


def _gdn_fwd(qn, kn, cv, gab, gz, alog, dtb, gain, consts):
    t = qn.shape[0]
    nc = t // CHUNK
    cb = _chunks_per_step(nc)
    rows = cb * CHUNK
    lmat, _, masks = consts
    qk_tile, v_tile, gab_tile = _gd_specs(rows)
    row128 = pl.BlockSpec((1, HEAD), lambda c, h: (0, 0))

    def body(q_ref, k_ref, v_ref, gab_ref, gz_ref, alog_ref, dtb_ref, gain_ref, l_ref, mask_ref,
             oraw_ref, og_ref, ssave_ref, tsave_ref, state):
        c = pl.program_id(0)
        g = pl.program_id(1)

        @pl.when(c == 0)
        def _():
            for hh in range(HPS):
                state[g * HPS + hh] = jnp.zeros((HEAD, HEAD), F32)

        alog = alog_ref[...]
        dtb = dtb_ref[...]
        gain_v = gain_ref[...]

        def one(i, carry):
            sl = pl.ds(pl.multiple_of(i * CHUNK, CHUNK), CHUNK)
            _, g_all, beta_all = _gd_gates(gab_ref[sl, :], alog, dtb)
            heads = [g * HPS + hh for hh in range(HPS)]
            lq = [slice(hh // 2 * HEAD, (hh // 2 + 1) * HEAD) for hh in range(HPS)]
            lv = [slice(hh * HEAD, (hh + 1) * HEAD) for hh in range(HPS)]
            chs = _gd_chunks([q_ref[sl, s] for s in lq], [k_ref[sl, s] for s in lq], [v_ref[sl, s] for s in lv],
                             g_all, beta_all, [(g, hh) for hh in range(HPS)], l_ref, mask_ref)
            s0 = [state[h] for h in heads]
            ws = _each(lambda ch, s: _dot(jnp.concatenate([ch["w"], ch["qg"]], axis=0), s), chs, s0)
            v_new = _each(lambda ch, x: ch["u"] - x[:CHUNK], chs, ws)
            o = _each(lambda ch, x, vn: x[CHUNK:] + _dot(ch["qk"], vn), chs, ws, v_new)
            s1 = _each(lambda ch, s, vn: s * ch["ge"] + _dot_tn(ch["kg"], vn), chs, s0, v_new)
            for hh in range(HPS):
                ssave_ref[i, hh] = s0[hh]
                tsave_ref[i, hh] = chs[hh]["tm"]
                state[heads[hh]] = s1[hh]
                oraw_ref[sl, lv[hh]] = o[hh]
                r = lax.rsqrt(jnp.mean(o[hh] * o[hh], axis=1, keepdims=True) + EPS)
                og_ref[sl, lv[hh]] = (o[hh] * r * gain_v * _silu(gz_ref[sl, lv[hh]])).astype(BF16)
            return carry

        lax.fori_loop(0, cb, one, 0, unroll=2)

    return pl.pallas_call(
        body, name="gdn_fwd", grid=(nc // cb, GD_HEADS // HPS),
        in_specs=[qk_tile, qk_tile, v_tile, gab_tile, _view_tile(gz, rows, HPS * HEAD), row128, row128, row128,
                  pl.BlockSpec(lmat.shape, lambda c, g: (0, 0)),
                  pl.BlockSpec(masks.shape, lambda c, g: (0, 0, 0))],
        out_specs=[v_tile, v_tile, pl.BlockSpec((cb, HPS, HEAD, HEAD), lambda c, g: (c, g, 0, 0)),
                   pl.BlockSpec((cb, HPS, CHUNK, CHUNK), lambda c, g: (c, g, 0, 0))],
        out_shape=[jax.ShapeDtypeStruct((t, GD_HEADS * HEAD), F32), jax.ShapeDtypeStruct((t, GD_HEADS * HEAD), BF16),
                   jax.ShapeDtypeStruct((nc, GD_HEADS, HEAD, HEAD), F32),
                   jax.ShapeDtypeStruct((nc, GD_HEADS, CHUNK, CHUNK), F32)],
        scratch_shapes=[pltpu.VMEM((GD_HEADS, HEAD, HEAD), F32)],
        compiler_params=_params(_ARB, _ARB))(qn, kn, cv, gab, gz[0], alog, dtb, gain, lmat, masks)


def _gdn_bwd(qn, kn, cv, gab, gz, alog, dtb, gain, oraw, ssave, tsave, dog, consts):
    t = qn.shape[0]
    nc = t // CHUNK
    cb = _chunks_per_step(nc)
    rows = cb * CHUNK
    nb = nc // cb
    lmat, lmat_t, masks = consts
    qk_tile, v_tile, gab_tile = _gd_specs(rows, nb)
    row128 = pl.BlockSpec((1, HEAD), lambda c, h: (0, 0))

    def body(q_ref, k_ref, v_ref, gab_ref, gz_ref, alog_ref, dtb_ref, gain_ref, oraw_ref, ssave_ref, tsave_ref, dog_ref,
             l_ref, lt_ref, mask_ref,
             dq_ref, dk_ref, dv_ref, dgab_ref, dgz_ref, small_ref, dstate):
        c = pl.program_id(0)
        g = pl.program_id(1)

        @pl.when(c == 0)
        def _():
            for hh in range(HPS):
                dstate[g * HPS + hh] = jnp.zeros((HEAD, HEAD), F32)

        @pl.when((c == 0) & (g == 0))
        def _():
            small_ref[...] = jnp.zeros_like(small_ref)

        alog = alog_ref[...]
        dtb = dtb_ref[...]
        gain_v = gain_ref[...]
        lane = lax.broadcasted_iota(jnp.int32, (1, HEAD), 1)
        last_row = (lax.broadcasted_iota(jnp.int32, (CHUNK, HEAD), 0) == CHUNK - 1).astype(F32)

        def one(j, carry):
            i = cb - 1 - j
            sl = pl.ds(pl.multiple_of(i * CHUNK, CHUNK), CHUNK)
            sp_arg, g_all, beta_all = _gd_gates(gab_ref[sl, :], alog, dtb)
            strict, eye = mask_ref[1], mask_ref[2]
            ltm = lt_ref[...]
            hs = range(HPS)
            heads = [g * HPS + hh for hh in hs]
            lq = [slice(hh // 2 * HEAD, (hh // 2 + 1) * HEAD) for hh in hs]
            lv = [slice(hh * HEAD, (hh + 1) * HEAD) for hh in hs]
            q = [q_ref[sl, s] for s in lq]
            k = [k_ref[sl, s] for s in lq]
            v = [v_ref[sl, s] for s in lv]
            gzv = [gz_ref[sl, s] for s in lv]
            chs = _gd_chunks(q, k, v, g_all, beta_all, [(g, hh) for hh in hs], l_ref, mask_ref,
                             tm=[tsave_ref[i, hh] for hh in hs])

            def col(name):
                return [ch[name] for ch in chs]

            def mul(x, y):
                return x * y

            tm, lm, eg, bb = col("tm"), col("lm"), col("eg"), col("bb")
            s0 = [ssave_ref[i, hh] for hh in hs]
            ds = [dstate[h] for h in heads]
            v_new = _each(lambda u, w, s: u - _dot(w, s), col("u"), col("w"), s0)

            o = [oraw_ref[sl, s] for s in lv]
            r = [lax.rsqrt(jnp.mean(x * x, axis=1, keepdims=True) + EPS) for x in o]
            on = _each(mul, o, r)
            dg_out = [dog_ref[sl, s] for s in lv]
            sgate = [_silu(x) for x in gzv]
            for hh in hs:
                dgz_ref[sl, lv[hh]] = (dg_out[hh] * on[hh] * gain_v * _dsilu(gzv[hh])).astype(BF16)
            small_ref[0:1, :] += sum(jnp.sum(d * s * n, axis=0, keepdims=True) for d, s, n in zip(dg_out, sgate, on))
            don = _each(lambda d, s: d * s * gain_v, dg_out, sgate)
            do = _each(lambda rr, dn, n: rr * (dn - n * jnp.mean(dn * n, axis=1, keepdims=True)), r, don, on)

            dv_new = _each(lambda a, d, b, s: _dot_tn(a, d) + _dot(b, s), col("qk"), do, col("kg"), ds)
            dqk = _each(_dot_nt, do, v_new)
            dkg = _each(_dot_nt, v_new, ds)
            dge = _each(lambda s, d: jnp.sum(_rowsum(s * d), axis=0, keepdims=True), s0, ds)
            both = _each(lambda d, dv: jnp.concatenate([d, dv], axis=0), do, dv_new)
            from_s = _each(_dot_nt, both, s0)
            dqg = [x[:CHUNK] for x in from_s]
            dw = [-x[CHUNK:] for x in from_s]
            ds_new = _each(lambda qg, w, bo, ge, s: _dot_tn(jnp.concatenate([qg, -w], axis=0), bo) + ge * s,
                           col("qg"), col("w"), both, col("ge"), ds)
            for hh in hs:
                dstate[heads[hh]] = ds_new[hh]

            side = _each(lambda dv, d: jnp.concatenate([dv, d], axis=1), dv_new, dw)
            back = _each(_dot_tn, tm, side)
            dvb = [x[:, :HEAD] for x in back]
            dkbg = [x[:, HEAD:] for x in back]
            dtm = _each(lambda sd, vb, kbg: _dot_nt(sd, jnp.concatenate([vb, kbg], axis=1)), side, col("vb"), col("kbg"))
            dtt = _each(_dot_nt, dtm, tm)
            da = _each(lambda t_, x: -_dot_tn(t_, x) * strict, tm, dtt)
            dal = _each(mul, da, lm)
            dqk_l = _each(mul, dqk, lm)
            stack = _each(lambda x, y: jnp.concatenate([x, y], axis=0), dal, dqk_l)
            on_k = _each(_dot, stack, k)
            dkb = _each(lambda x, y, e: x[:CHUNK] + y * e, on_k, dkbg, eg)
            dq = _each(lambda x, y, e: x[CHUNK:] + y * e, on_k, dqg, eg)
            dk = _each(lambda st, kb, qq, z, ekg, w_, b: _dot_tn(st, jnp.concatenate([kb, qq], axis=0)) + z * ekg + w_ * b,
                       stack, col("kb"), q, dkg, col("ekg"), dkb, bb)
            gmat = _each(lambda x, a, y, qk: x * a + y * qk, da, col("a"), dqk, col("qk"))
            t_kg = _each(lambda x, y: _rowsum(x * y), dkg, col("kg"))
            dgam = _each(lambda gm, x, qg, t_, y, kbg: (_rowsum(gm) - _row_to_col(jnp.sum(gm, axis=0, keepdims=True), eye)
                                                        + _rowsum(x * qg) - t_ + _rowsum(y * kbg)),
                         gmat, dqg, col("qg"), t_kg, dkbg, col("kbg"))
            dg_end = _each(lambda t_, e, ge: jnp.sum(t_, axis=0, keepdims=True) + e * ge[:, 0:1], t_kg, dge, col("ge"))
            dgam = _each(lambda x, e: x + last_row * e, dgam, dg_end)
            dbeta = _each(lambda x, kk, y, vv: _rowsum(x * kk) + _rowsum(y * vv), dkb, k, dvb, v)
            dg = _mx_each(ltm, dgam)

            for hh in hs:
                dv_ref[sl, lv[hh]] = dvb[hh] * bb[hh]
            fac_g = -jnp.exp(alog) * _sigmoid(sp_arg)
            fac_b = beta_all * (1.0 - beta_all)
            hot_g = [(lane == h).astype(F32) for h in heads]
            hot_b = [(lane == GD_HEADS + h).astype(F32) for h in heads]
            dga = _each(lambda x, hot: x * hot * fac_g, dg, hot_g)
            dgb = _each(lambda x, hot: x * hot * fac_b, dbeta, hot_b)
            small_ref[1:2, :] += sum(jnp.sum(x, axis=0, keepdims=True) for x in dga)
            small_ref[2:3, :] += sum(jnp.sum(x * hot * g_all, axis=0, keepdims=True) for x, hot in zip(dg, hot_g))
            for pair in range(HPS // 2):
                lqp = slice(pair * HEAD, (pair + 1) * HEAD)
                dq_ref[sl, lqp] = dq[2 * pair] + dq[2 * pair + 1]
                dk_ref[sl, lqp] = dk[2 * pair] + dk[2 * pair + 1]
            dgab_ref[sl, :] = sum(a + b for a, b in zip(dga, dgb))
            return carry

        lax.fori_loop(0, cb, one, 0, unroll=2)

    groups = GD_HEADS // HPS
    outs = [jax.ShapeDtypeStruct((t, 1024), F32), jax.ShapeDtypeStruct((t, 1024), F32),
            jax.ShapeDtypeStruct((t, 2048), F32), jax.ShapeDtypeStruct((t, groups * HEAD), F32),
            jax.ShapeDtypeStruct((t, 2048), BF16), jax.ShapeDtypeStruct((8, HEAD), F32)]
    return pl.pallas_call(
        body, name="gdn_bwd", grid=(nb, groups),
        in_specs=[qk_tile, qk_tile, v_tile, gab_tile, _view_tile(gz, rows, HPS * HEAD, lambda c: nb - 1 - c),
                  row128, row128, row128, v_tile,
                  pl.BlockSpec((cb, HPS, HEAD, HEAD), lambda c, g: (nb - 1 - c, g, 0, 0)),
                  pl.BlockSpec((cb, HPS, CHUNK, CHUNK), lambda c, g: (nb - 1 - c, g, 0, 0)), v_tile,
                  pl.BlockSpec(lmat.shape, lambda c, g: (0, 0)),
                  pl.BlockSpec(lmat_t.shape, lambda c, g: (0, 0)),
                  pl.BlockSpec(masks.shape, lambda c, g: (0, 0, 0))],
        out_specs=[qk_tile, qk_tile, v_tile, pl.BlockSpec((rows, HEAD), lambda c, g: (nb - 1 - c, g)), v_tile,
                   pl.BlockSpec((8, HEAD), lambda c, g: (0, 0))],
        out_shape=outs, scratch_shapes=[pltpu.VMEM((GD_HEADS, HEAD, HEAD), F32)],
        compiler_params=_params(_ARB, _ARB))(qn, kn, cv, gab, gz[0], alog, dtb, gain, oraw, ssave, tsave, dog,
                                             lmat, lmat_t, masks)


def _fold_groups(wide):
    t, width = wide.shape
    tr = _pick(t, 512, 8)

    def body(w_ref, o_ref):
        acc = w_ref[:, 0:HEAD]
        for j in range(1, width // HEAD):
            acc = acc + w_ref[:, j * HEAD:(j + 1) * HEAD]
        o_ref[...] = acc.astype(BF16)

    return pl.pallas_call(
        body, name="fold_gate_grads", grid=(t // tr,), in_specs=[_row_spec(tr, width)], out_specs=_row_spec(tr, HEAD),
        out_shape=jax.ShapeDtypeStruct((t, HEAD), BF16), compiler_params=_params(_PAR))(wide)


def _adam_math(w, g, m, v):
    m2 = ADAM_B1 * m + (1.0 - ADAM_B1) * g
    v2 = ADAM_B2 * v + (1.0 - ADAM_B2) * (g * g)
    m_hat = m2 / (1.0 - ADAM_B1 ** ADAM_STEP)
    v_hat = v2 / (1.0 - ADAM_B2 ** ADAM_STEP)
    delta = -ADAM_LR * (m_hat / (jnp.sqrt(v_hat) + ADAM_EPS) + ADAM_WD * w)
    return delta, m2, v2


def _adamw(w, g, m, v, name, after=None):
    r, c = w.shape
    tr = r
    for cand in range(8, r + 1, 8):
        if r % cand == 0 and cand * c * 4 <= (1 << 20):
            tr = cand
    if r % 8 != 0:
        tr = r

    def body(w_ref, g_ref, m_ref, v_ref, *rest):
        d_ref, m2_ref, v2_ref = rest[-3:]
        d, m2, v2 = _adam_math(w_ref[...], g_ref[...], m_ref[...], v_ref[...])
        d_ref[...] = d
        m2_ref[...] = m2
        v2_ref[...] = v2

    spec = pl.BlockSpec((tr, c), lambda i: (i, 0))
    extra = [] if after is None else [after]
    return pl.pallas_call(
        body, name=name, grid=(r // tr,), in_specs=[spec] * 4 + [_ANY] * len(extra), out_specs=[spec] * 3,
        out_shape=[jax.ShapeDtypeStruct((r, c), F32)] * 3, compiler_params=_params(_PAR))(w, g, m, v, *extra)


_ANY = pl.BlockSpec(memory_space=pl.ANY)


def _place():
    return lax.axis_index("x"), lax.axis_index("y"), lax.axis_index("c")


def _gather_weights(packs, nchs, name):
    n = len(packs)
    halves = [p.shape[0] // 2 for p in packs]
    base = [sum(nchs[:i]) for i in range(n)]
    total = sum(nchs)
    for p, h, k in zip(packs, halves, nchs):
        assert p.shape[0] == 2 * h and h % k == 0 and (h // k) % 16 == 0

    def body(*refs):
        p_refs, g_refs, (send_sems, recv_sems) = refs[:n], refs[n:2 * n], refs[2 * n:]
        x, y, c = _place()
        sibling = (x, y, 1 - c)
        chips = [(1 - x, y), (x, 1 - y), (1 - x, 1 - y)]
        chunks = [(a, q) for a in range(n) for q in range(nchs[a])]

        def rows_of(a, pc, q):
            ch = halves[a] // nchs[a]
            return pl.ds(pl.multiple_of(pc * halves[a] + q * ch, 16), ch)

        def piece(a, px, py, pc, q):
            return g_refs[a].at[2 * px + py, rows_of(a, pc, q), :]

        def copy(k, src, dst, to):
            return pltpu.make_async_remote_copy(src_ref=src, dst_ref=dst, send_sem=send_sems.at[k],
                                                recv_sem=recv_sems.at[k], device_id=to, device_id_type=MESH)

        def sem_of(j, a, q):
            return j * total + base[a] + q

        first = {(j, a, q): copy(sem_of(j, a, q), p_refs[a].at[rows_of(a, c, q), :], piece(a, x, y, c, q), (*chip, c))
                 for j, chip in enumerate(chips) for a, q in chunks}
        for a, q in chunks:
            for j in range(3):
                first[j, a, q].start()
        passed = {(j, a, q): copy(sem_of(3 + j, a, q), piece(a, *chip, c, q), piece(a, *chip, c, q), sibling)
                  for j, chip in enumerate(chips) for a, q in chunks}
        for a, q in chunks:
            for j, chip in enumerate(chips):
                copy(sem_of(j, a, q), p_refs[a].at[rows_of(a, c, q), :], piece(a, *chip, c, q), (*chip, c)).wait_recv()
                passed[j, a, q].start()
        for a, q in chunks:
            for j, chip in enumerate(chips):
                copy(sem_of(3 + j, a, q), piece(a, *chip, 1 - c, q), piece(a, *chip, 1 - c, q), sibling).wait_recv()
        for key in first:
            first[key].wait_send()
            passed[key].wait_send()

    return pl.pallas_call(
        body, name=name, out_shape=[jax.ShapeDtypeStruct((4,) + p.shape, p.dtype) for p in packs],
        in_specs=[_ANY] * n, out_specs=[_ANY] * n,
        scratch_shapes=[pltpu.SemaphoreType.DMA((6 * total,)), pltpu.SemaphoreType.DMA((6 * total,))])(*packs)


def _swap_with_sibling(arrs, nchs, lead, name, halves=False):
    n = len(arrs)
    jobs = []
    hs = [arr.shape[-2] // (2 if halves else 1) for arr in arrs]
    for a, (h, k) in enumerate(zip(hs, nchs)):
        assert h % k == 0 and (h // k) % 16 == 0
        for s in (range(lead) if lead else [None]):
            jobs += [(a, s, q * (h // k), h // k) for q in range(k)]

    def body(*refs):
        src, dst, (send_sems, recv_sems) = refs[:n], refs[n:2 * n], refs[2 * n:]
        x, y, c = _place()

        def at(ref, s, r0, rows):
            return ref.at[pl.ds(r0, rows), :] if s is None else ref.at[s, pl.ds(r0, rows), :]

        def src_rows(a, r0):
            return pl.multiple_of((1 - c) * hs[a] + r0, 16) if halves else r0

        copies = [pltpu.make_async_remote_copy(
            src_ref=at(src[a], s, src_rows(a, r0), rows), dst_ref=at(dst[a], s, r0, rows), send_sem=send_sems.at[k],
            recv_sem=recv_sems.at[k], device_id=(x, y, 1 - c), device_id_type=MESH)
            for k, (a, s, r0, rows) in enumerate(jobs)]
        for cp in copies:
            cp.start()
        for cp in copies:
            cp.wait()

    shapes = [jax.ShapeDtypeStruct(arr.shape[:-2] + (h, arr.shape[-1]), arr.dtype) for arr, h in zip(arrs, hs)]
    return pl.pallas_call(
        body, name=name, out_shape=shapes, in_specs=[_ANY] * n, out_specs=[_ANY] * n,
        scratch_shapes=[pltpu.SemaphoreType.DMA((len(jobs),)), pltpu.SemaphoreType.DMA((len(jobs),))])(*arrs)


def _add2(full, b, core, name):
    n, rows, w = b.shape
    tr = _pick(rows, 256, 16)
    nblk = rows // tr

    def body(c_ref, a_ref, b_ref, o_ref):
        o_ref[...] = (a_ref[...].astype(F32) + b_ref[...].astype(F32)).astype(BF16)

    spec = pl.BlockSpec((1, tr, w), lambda i, j, c_ref: (i, j, 0))
    grid_spec = pltpu.PrefetchScalarGridSpec(
        num_scalar_prefetch=1, grid=(n, nblk),
        in_specs=[pl.BlockSpec((1, tr, w), lambda i, j, c_ref: (i, c_ref[0] * nblk + j, 0)), spec], out_specs=spec)
    return pl.pallas_call(
        body, name=name, grid_spec=grid_spec, out_shape=jax.ShapeDtypeStruct(b.shape, BF16),
        compiler_params=_params(_PAR, _PAR))(core, full, b)


def _reduce_chips(partials, nchs, name):
    n = len(partials)
    jobs = []
    for a, (arr, k) in enumerate(zip(partials, nchs)):
        h = arr.shape[1]
        assert h % k == 0 and (h // k) % 16 == 0
        jobs += [(a, q * (h // k), h // k) for q in range(k)]

    def body(*refs):
        src, dst, (send_sems, recv_sems) = refs[:n], refs[n:2 * n], refs[2 * n:]
        x, y, c = _place()
        chips = [(1 - x, y), (x, 1 - y), (1 - x, 1 - y)]
        copies = [pltpu.make_async_remote_copy(
            src_ref=src[a].at[2 * px + py, pl.ds(r0, rows), :], dst_ref=dst[a].at[j, pl.ds(r0, rows), :],
            send_sem=send_sems.at[3 * k + j], recv_sem=recv_sems.at[3 * k + j],
            device_id=(px, py, c), device_id_type=MESH)
            for k, (a, r0, rows) in enumerate(jobs) for j, (px, py) in enumerate(chips)]
        for cp in copies:
            cp.start()
        for cp in copies:
            cp.wait()

    return pl.pallas_call(
        body, name=name,
        out_shape=[jax.ShapeDtypeStruct((3,) + p.shape[1:], p.dtype) for p in partials],
        in_specs=[_ANY] * n, out_specs=[_ANY] * n,
        scratch_shapes=[pltpu.SemaphoreType.DMA((3 * len(jobs),)), pltpu.SemaphoreType.DMA((3 * len(jobs),))])(*partials)


_HBM = pl.BlockSpec(memory_space=pltpu.HBM)
_SEM = pl.BlockSpec(memory_space=pltpu.SEMAPHORE)
_DATAFLOW = pltpu.SideEffectType.DATAFLOW_SIDE_EFFECTING


def _ici_jobs(srcs, nchs, kind):
    jobs = []
    for a, (arr, k) in enumerate(zip(srcs, nchs)):
        h = arr.shape[0] // 2 if kind == "gather" else arr.shape[1]
        assert h % k == 0 and (h // k) % 16 == 0
        jobs += [(a, h, q * (h // k), h // k) for q in range(k)]
    return jobs


def _ici_copies(src, land, send_sems, recv_sems, jobs, kind):
    x, y, c = _place()
    chips = [(1 - x, y), (x, 1 - y), (1 - x, 1 - y)]
    copies = []
    for k, (a, h, r0, rows) in enumerate(jobs):
        for j, (px, py) in enumerate(chips):
            if kind == "gather":
                at = pl.ds(pl.multiple_of(c * h + r0, 16), rows)
                s, d = src[a].at[at, :], land[a].at[2 * x + y, at, :]
            else:
                s, d = src[a].at[2 * px + py, pl.ds(r0, rows), :], land[a].at[j, pl.ds(r0, rows), :]
            copies.append(pltpu.make_async_remote_copy(
                src_ref=s, dst_ref=d, send_sem=send_sems.at[3 * k + j], recv_sem=recv_sems.at[3 * k + j],
                device_id=(px, py, c), device_id_type=MESH))
    return copies


def _ici_start(srcs, nchs, kind, name):
    n = len(srcs)
    jobs = _ici_jobs(srcs, nchs, kind)
    lead = (lambda s: (4,) + s.shape) if kind == "gather" else (lambda s: (3,) + s.shape[1:])
    lands = [lax.empty(lead(s), s.dtype) for s in srcs]

    def body(*refs):
        src, land = refs[:n], refs[n:2 * n]
        send_sems, recv_sems, token = refs[2 * n], refs[2 * n + 1], refs[-1]
        for cp in _ici_copies(src, land, send_sems, recv_sems, jobs, kind):
            cp.start()
        token[...] = jnp.zeros_like(token)

    hbm = [pltpu.HBM(a.shape, a.dtype) for a in srcs + lands]
    outs = pl.pallas_call(
        body, name=name,
        out_shape=[pltpu.SemaphoreType.DMA((3 * len(jobs),)), pltpu.SemaphoreType.DMA((3 * len(jobs),))] + hbm
        + [jax.ShapeDtypeStruct((8, 128), F32)],
        in_specs=[_HBM] * (2 * n), out_specs=[_SEM, _SEM] + [_HBM] * (2 * n) + [pl.BlockSpec(memory_space=pltpu.VMEM)],
        input_output_aliases={i: 2 + i for i in range(2 * n)},
        compiler_params=pltpu.CompilerParams(has_side_effects=_DATAFLOW),
    )(*[pltpu.with_memory_space_constraint(a, pltpu.HBM) for a in srcs + lands])
    return (outs[0], outs[1], list(outs[2:2 + n]), list(outs[2 + n:2 + 2 * n]), nchs, kind), outs[-1]


def _ici_wait(handle, after, name):
    send_sems, recv_sems, srcs, lands, nchs, kind = handle
    n = len(srcs)
    jobs = _ici_jobs(srcs, nchs, kind)

    def body(*refs):
        src, land = refs[:n], refs[n:2 * n]
        for cp in _ici_copies(src, land, refs[2 * n], refs[2 * n + 1], jobs, kind):
            cp.wait_send()
            cp.wait_recv()

    outs = pl.pallas_call(
        body, name=name, out_shape=[pltpu.HBM(a.shape, a.dtype) for a in srcs + lands],
        in_specs=[_HBM] * (2 * n) + [_SEM, _SEM, _ANY], out_specs=[_HBM] * (2 * n),
        input_output_aliases={i: i for i in range(2 * n)},
        compiler_params=pltpu.CompilerParams(has_side_effects=_DATAFLOW),
    )(*srcs, *lands, send_sems, recv_sems, after)
    return list(outs[:n]), list(outs[n:])


def _pass_to_sibling(gathered, nchs, name):
    n = len(gathered)
    jobs = _ici_jobs([jax.ShapeDtypeStruct(g.shape[1:], g.dtype) for g in gathered], nchs, "gather")

    def body(*refs):
        src, dst, (send_sems, recv_sems) = refs[:n], refs[n:2 * n], refs[2 * n:]
        x, y, c = _place()
        slots = [2 * (1 - x) + y, 2 * x + (1 - y), 2 * (1 - x) + (1 - y)]

        def copy(k, j, pc):
            a, h, r0, rows = jobs[k]
            at = pl.ds(pl.multiple_of(pc * h + r0, 16), rows)
            return pltpu.make_async_remote_copy(
                src_ref=src[a].at[slots[j], at, :], dst_ref=dst[a].at[slots[j], at, :], send_sem=send_sems.at[3 * k + j],
                recv_sem=recv_sems.at[3 * k + j], device_id=(x, y, 1 - c), device_id_type=MESH)

        pairs = [(k, j) for k in range(len(jobs)) for j in range(3)]
        for k, j in pairs:
            copy(k, j, c).start()
        for k, j in pairs:
            copy(k, j, c).wait_send()
            copy(k, j, 1 - c).wait_recv()

    return pl.pallas_call(
        body, name=name, out_shape=[jax.ShapeDtypeStruct(g.shape, g.dtype) for g in gathered],
        in_specs=[_ANY] * n, out_specs=[_ANY] * n, input_output_aliases={i: i for i in range(n)},
        scratch_shapes=[pltpu.SemaphoreType.DMA((3 * len(jobs),)), pltpu.SemaphoreType.DMA((3 * len(jobs),))])(*gathered)


def _add4(own, got, name):
    rows, w = own.shape
    tr = _pick(rows, 128, 16)

    def body(a_ref, b_ref, o_ref):
        o_ref[...] = ((a_ref[...].astype(F32) + b_ref[0].astype(F32)) + b_ref[1].astype(F32)) + b_ref[2].astype(F32)

    return pl.pallas_call(
        body, name=name, grid=(rows // tr,),
        in_specs=[pl.BlockSpec((tr, w), lambda i: (i, 0)), pl.BlockSpec((3, tr, w), lambda i: (0, i, 0))],
        out_specs=pl.BlockSpec((tr, w), lambda i: (i, 0)), out_shape=jax.ShapeDtypeStruct((rows, w), F32),
        compiler_params=_params(_PAR))(own, got)


def _small_sync(gs, ws, ms, vs):
    rows = gs.shape[0]
    vmem = pl.BlockSpec(memory_space=pltpu.VMEM)

    def body(g_ref, w_ref, m_ref, v_ref, sum_ref, d_ref, m2_ref, v2_ref, buf, send_sems, recv_sems):
        x, y, c = _place()
        me = 4 * x + 2 * y + c
        buf[me] = g_ref[...]
        copies = []
        for k in range(1, 8):
            peer = (x ^ (k >> 2), y ^ ((k >> 1) & 1), c ^ (k & 1))
            copies.append(pltpu.make_async_remote_copy(
                src_ref=g_ref, dst_ref=buf.at[me], send_sem=send_sems.at[k - 1], recv_sem=recv_sems.at[k - 1],
                device_id=peer, device_id_type=MESH))
        for cp in copies:
            cp.start()
        for cp in copies:
            cp.wait()
        total = buf[0]
        for i in range(1, 8):
            total = total + buf[i]
        sum_ref[...] = total
        d, m2, v2 = _adam_math(w_ref[...], total, m_ref[...], v_ref[...])
        d_ref[...] = d
        m2_ref[...] = m2
        v2_ref[...] = v2

    shape = jax.ShapeDtypeStruct((rows, 128), F32)
    return pl.pallas_call(
        body, name="small_sync", out_shape=[shape] * 4, in_specs=[vmem] * 4, out_specs=[vmem] * 4,
        scratch_shapes=[pltpu.VMEM((8, rows, 128), F32), pltpu.SemaphoreType.DMA((7,)),
                        pltpu.SemaphoreType.DMA((7,))])(gs, ws, ms, vs)


_GROUPS = {
    "ffn1": dict(cols=("ffn1_w_in", 1408), rows=(("ffn1_w_out", 704, 704),), chunks=(8, 2)),
    "ffn2": dict(cols=("ffn2_w_in", 1408), rows=(("ffn2_w_out", 704, 704),), chunks=(8, 2)),
    "mixer": dict(cols=("w_in", 3080), chunks=(8, 4),
                  rows=(("w_branch_hgrn", 256, 256), ("w_branch_gdn", 512, 512), ("w_out", 256, 256),
                        ("gdn_conv_w", CONV_K, 128))),
}
_BIG_NAMES = tuple(n for g in _GROUPS.values() for n in (g["cols"][0],) + tuple(r[0] for r in g["rows"]))


def _group_names(group):
    return (group["cols"][0],) + tuple(r[0] for r in group["rows"])


def _pack(parts, lead, group):
    ax = len(lead)
    rows = []
    for n, r, padded in group["rows"]:
        p = parts[n]
        if padded != r:
            p = jnp.tile(p, (1,) * ax + (padded // r, 1))
        rows.append(p)
    return [parts[group["cols"][0]], rows[0] if len(rows) == 1 else jnp.concatenate(rows, axis=ax)]


def _unpack(cols, rows, group):
    out, off = {group["cols"][0]: cols}, 0
    for n, r, padded in group["rows"]:
        out[n] = rows[..., off:off + r, :]
        off += padded
    return out


def _is_col_sharded(name):
    return name in ("ffn1_w_in", "ffn2_w_in", "w_in", "gdn_conv_w")


def _full_from_shards(name, g):
    if _is_col_sharded(name):
        return jnp.transpose(g, (1, 0, 2)).reshape(g.shape[1], -1)
    return g.reshape(-1, g.shape[2])


def _shards_from_full(name, full):
    if _is_col_sharded(name):
        return jnp.transpose(full.reshape(full.shape[0], 4, -1), (1, 0, 2))
    return full.reshape(4, -1, full.shape[1])


_SMALL = (("ffn1_norm", 8), ("mix_norm", 8), ("hgrn_lb_logits", 16), ("hgrn_out_norm", 8), ("gdn_a_log", 8),
          ("gdn_dt_bias", 8), ("gdn_out_norm", 8), ("ffn2_norm", 8), ("final_norm", 8), ("loss", 8))
_SMALL_ROWS = sum(r for _, r in _SMALL)


def _pack_small(parts):
    out = []
    for name, rows in _SMALL:
        p = parts[name].reshape(-1).astype(F32)
        if p.shape[0] <= 128:
            if p.shape[0] < 128:
                p = jnp.concatenate([p, jnp.zeros((128 - p.shape[0],), F32)])
            p = jnp.broadcast_to(p.reshape(1, 128), (rows, 128))
        out.append(p.reshape(rows, 128))
    return jnp.concatenate(out, axis=0)


def _unpack_small(packed, shapes):
    out, off = {}, 0
    for name, rows in _SMALL:
        n = int(np.prod(shapes[name]))
        out[name] = packed[off:off + rows].reshape(-1)[:n].reshape(shapes[name])
        off += rows
    return out


def _ffn_fwd(x, gain, w_in, w_out, tag):
    n = _rmsnorm_fwd(x, gain, tag + "_norm")
    a, b, hm = _ffn_in_act(n, w_in, tag + "_in")
    out = _mm(hm, w_out, alpha=0.5, res=x, name=tag + "_out")
    return out, (n, a, b)


def _ffn_bwd(x, gain, w_in, w_out, saved, dout, dout_bf, tag):
    n, a, b = saved
    da, db, hm = _ffn_dact(dout_bf, w_out, a, b, tag + "_dact")
    dw_out = _mm(hm, dout_bf, ta=True, alpha=0.5, out_dtype=BF16, name=tag + "_dwout")
    dw_in = jnp.concatenate([_mm(n, da, ta=True, out_dtype=BF16, name=tag + "_dwin_a"),
                             _mm(n, db, ta=True, out_dtype=BF16, name=tag + "_dwin_b")], axis=1)
    dn = _mm(da, w_in[:, :D_FF], tb=True, name=tag + "_dnorm_a")
    dn = _mm(db, w_in[:, D_FF:], tb=True, res=dn, name=tag + "_dnorm_b")
    dx, dx_bf, dgain = _rmsnorm_bwd(x, gain, dn, dout, tag + "_dx")
    return dx, dx_bf, dgain, dw_in, dw_out


def _pad_lanes(v):
    return jnp.concatenate([v.reshape(1, -1), jnp.zeros((1, HEAD - v.size), F32)], axis=1)


def _local_step(x, tgt, small, exchange):
    hg_c = _hg_consts()
    gd_c = _gd_consts()
    alog = _pad_lanes(small["gdn_a_log"])
    dtb = _pad_lanes(small["gdn_dt_bias"])
    logits = small["hgrn_lb_logits"]
    hg_gain = small["hgrn_out_norm"].reshape(1, HEAD)
    gd_gain = small["gdn_out_norm"].reshape(1, HEAD)
    g1, gm, g2 = small["ffn1_norm"].reshape(1, -1), small["mix_norm"].reshape(1, -1), small["ffn2_norm"].reshape(1, -1)
    gf = small["final_norm"].reshape(1, -1)
    qscale = HEAD ** -0.5

    w1 = exchange.weights("ffn1")
    started = exchange.prefetch("mixer")
    h1, ffn1_saved = _ffn_fwd(x, g1 + started, w1["ffn1_w_in"], w1["ffn1_w_out"], "ffn1")
    u = _rmsnorm_fwd(h1, gm, "mix_norm")
    w = exchange.weights("mixer", after=u)
    started = exchange.prefetch("ffn2")
    seg, off = {}, 0
    for name, size in zip(IN_NAMES, IN_SIZES):
        seg[name] = w["w_in"][:, off:off + size]
        off += size
    w_gab = jnp.concatenate([seg["ga"], seg["gb"], jnp.zeros((D_MODEL, HEAD - 32), BF16)], axis=1)
    big_segs = [n for n in IN_NAMES if n not in ("ga", "gb")]
    conv8 = jnp.concatenate([w["gdn_conv_w"].astype(F32), jnp.zeros((8 - CONV_K, 4096), F32)], axis=0)
    conv_q, conv_k, conv_v = conv8[:, :1024], conv8[:, 1024:2048], conv8[:, 2048:]
    w_main = jnp.concatenate([seg[n] for n in big_segs], axis=1)
    proj = _mm(u, w_main, name="proj")
    pr, off = {}, 0
    for n in big_segs:
        pr[n] = _view(proj, off, seg[n].shape[1])
        off += seg[n].shape[1]
    gab = _mm(u, w_gab, name="proj_gab")
    oh_raw, oh, s_h = _hgrn_fwd(pr["hq"], pr["hf"], pr["hi"], pr["hg"], logits, hg_gain + started, hg_c)
    qn = _conv_fwd(pr["gq"], conv_q, qscale, "conv_q")
    kn = _conv_fwd(pr["gk"], conv_k, 1.0, "conv_k")
    cv = _conv_fwd(pr["gv"], conv_v, None, "conv_v")
    og_raw, og, s_g, t_g = _gdn_fwd(qn, kn, cv, gab, pr["gz"], alog, dtb, gd_gain, gd_c)
    yh = _mm(oh, w["w_branch_hgrn"], name="branch_h")
    yg = _mm(og, w["w_branch_gdn"], name="branch_g")
    ym = _merge_fwd(yh, yg, pr["gate_h"], pr["gate_g"])
    h2 = _mm(ym, w["w_out"], res=h1, name="mix_out")
    w2 = exchange.weights("ffn2", after=h2)
    h3, ffn2_saved = _ffn_fwd(h2, g2, w2["ffn2_w_in"], w2["ffn2_w_out"], "ffn2")
    loss, dh3, dh3_bf, d_gf = _final_loss(h3, gf, tgt)

    dh2, dh2_bf, d_g2, d_f2in, d_f2out = _ffn_bwd(h2, g2, w2["ffn2_w_in"], w2["ffn2_w_out"], ffn2_saved, dh3, dh3_bf,
                                                  "ffn2")
    started = exchange.reduce("ffn2", {"ffn2_w_in": d_f2in, "ffn2_w_out": d_f2out}, behind=True)
    dym =_mm(dh2_bf, w["w_out"], tb=True, name="d_merge")
    d_wout = _mm(ym, dh2_bf, ta=True, out_dtype=BF16, name="d_w_out")
    dyh, dyg, d_gate_h, d_gate_g = _merge_bwd(dym, yh, yg, pr["gate_h"], pr["gate_g"])
    d_wbh = _mm(oh, dyh, ta=True, out_dtype=BF16, name="d_w_branch_h")
    d_wbg = _mm(og, dyg, ta=True, out_dtype=BF16, name="d_w_branch_g")
    doh = _mm(dyh, w["w_branch_hgrn"], tb=True, name="d_oh")
    dog = _mm(dyg, w["w_branch_gdn"], tb=True, name="d_og")
    d_hq, d_hf, d_hi, d_hg, d_hg_gain, d_lb0 = _hgrn_bwd(pr["hq"], pr["hf"], pr["hi"], pr["hg"], logits,
                                                        hg_gain + started, oh_raw, s_h, doh, hg_c)
    d_qn, d_kn, d_cv, d_gab_wide, d_gz, gd_small = _gdn_bwd(qn, kn, cv, gab, pr["gz"], alog, dtb, gd_gain, og_raw,
                                                            s_g, t_g, dog, gd_c)
    d_gab = _fold_groups(d_gab_wide)
    dc_q, dwc_q = _conv_bwd_a(pr["gq"], conv_q, d_qn, qscale, "dconv_q")
    dc_k, dwc_k = _conv_bwd_a(pr["gk"], conv_k, d_kn, 1.0, "dconv_k")
    dc_v, dwc_v = _conv_bwd_a(pr["gv"], conv_v, d_cv, None, "dconv_v")
    d_gq = _conv_bwd_b(dc_q, conv_q, "dconvx_q")
    d_gk = _conv_bwd_b(dc_k, conv_k, "dconvx_k")
    d_gv = _conv_bwd_b(dc_v, conv_v, "dconvx_v")
    dpr = {"hq": d_hq, "hf": d_hf, "hi": d_hi, "hg": d_hg, "gq": d_gq, "gk": d_gk, "gv": d_gv, "gz": d_gz,
           "gate_h": d_gate_h, "gate_g": d_gate_g}
    dproj = jnp.concatenate([dpr[n] for n in big_segs], axis=1)
    du = _mm(d_gab, w_gab, tb=True, name="du_gab")
    du = _mm(dproj, w_main, tb=True, res=du, name="du")
    d_wmain = _mm(u, dproj, ta=True, out_dtype=BF16, name="dw_main")
    d_wgab = _mm(u, d_gab, ta=True, out_dtype=BF16, name="dw_gab")
    d_win = jnp.concatenate([d_wmain[:, :8192], d_wgab[:, :32], d_wmain[:, 8192:]], axis=1)
    d_conv = jnp.concatenate([dwc_q[:CONV_K], dwc_k[:CONV_K], dwc_v[:CONV_K]], axis=1).astype(BF16)
    started = exchange.reduce("mixer", {"w_in": d_win, "gdn_conv_w": d_conv, "w_branch_hgrn": d_wbh,
                                        "w_branch_gdn": d_wbg, "w_out": d_wout}, behind=True)
    dh1, dh1_bf, d_gm = _rmsnorm_bwd(h1, gm + started, du, dh2, "mix_dnorm")
    dx, _, d_g1, d_f1in, d_f1out = _ffn_bwd(x, g1, w1["ffn1_w_in"], w1["ffn1_w_out"], ffn1_saved, dh1, dh1_bf, "ffn1")
    exchange.reduce("ffn1", {"ffn1_w_in": d_f1in, "ffn1_w_out": d_f1out}, behind=True)
    d_lb0 = d_lb0.reshape(1, -1)
    sm = {"ffn1_norm": d_g1, "mix_norm": d_gm, "hgrn_lb_logits": jnp.concatenate([d_lb0, -d_lb0], axis=0),
          "hgrn_out_norm": d_hg_gain, "gdn_a_log": gd_small[2, :16], "gdn_dt_bias": gd_small[1, :16],
          "gdn_out_norm": gd_small[0], "ffn2_norm": d_g2, "final_norm": d_gf, "loss": loss[0, :1]}
    return dx, sm


class _Exchange:
    def __init__(self, wts):
        self.wts = wts
        xi, yi, ci = _place()
        self.chip = 2 * xi + yi
        self.south = ci == 0
        self.core = ci.reshape(1).astype(jnp.int32)
        self.mine = {}
        self.coming = {}
        self.going = {}

    def _packs(self, tag):
        group = _GROUPS[tag]
        return _pack({n: self.wts[n][0].astype(BF16) for n in _group_names(group)}, (), group)

    def prefetch(self, tag):
        packs = self._packs(tag)
        handle, token = _ici_start(packs, _GROUPS[tag]["chunks"], "gather", "gather_start_" + tag)
        self.coming[tag] = handle
        return token[0:1, 0:1]

    def weights(self, tag, after=None):
        group = _GROUPS[tag]
        if tag in self.coming:
            packs, halves = _ici_wait(self.coming.pop(tag), after, "gather_wait_" + tag)
            others = _pass_to_sibling(halves, group["chunks"], "gather_pass_" + tag)
        else:
            packs = self._packs(tag)
            others = _gather_weights(packs, group["chunks"], "gather_" + tag)
        whole = [lax.dynamic_update_index_in_dim(g, p, self.chip, 0) for g, p in zip(others, packs)]
        gathered = _unpack(*whole, group)
        return {n: _full_from_shards(n, gathered[n]) for n in _group_names(group)}

    def reduce(self, tag, grads, behind=False):
        group = _GROUPS[tag]
        gpacks = _pack({n: _shards_from_full(n, grads[n]) for n in _group_names(group)}, (4,), group)
        got = _swap_with_sibling(gpacks, group["chunks"], 4, "reduce_pair_" + tag, halves=True)
        sums = [_add2(a, b, self.core, "add_pair_%s_%d" % (tag, i)) for i, (a, b) in enumerate(zip(gpacks, got))]
        if behind:
            handle, token = _ici_start(sums, group["chunks"], "reduce", "reduce_start_" + tag)
            self.going[tag] = handle
            self.token = token
            return token[0:1, 0:1]
        self._add_chips(tag, sums, _reduce_chips(sums, group["chunks"], "reduce_chips_" + tag))
        return None

    def _add_chips(self, tag, sums, from_chips):
        self.mine[tag] = [_add4(lax.dynamic_index_in_dim(s, self.chip, axis=0, keepdims=False), f,
                                "add_chips_%s_%d" % (tag, i)) for i, (s, f) in enumerate(zip(sums, from_chips))]

    def finish(self, tags, after):
        for tag in tags:
            if tag in self.going:
                self._add_chips(tag, *_ici_wait(self.going.pop(tag), after, "reduce_wait_" + tag))
        mine = [a for t in tags for a in self.mine[t]]
        nchs = [k for t in tags for k in _GROUPS[t]["chunks"]]
        theirs = _swap_with_sibling(mine, nchs, 0, "share_pair_" + tags[0])
        whole = [jnp.concatenate([jnp.where(self.south, a, b), jnp.where(self.south, b, a)], axis=0)
                 for a, b in zip(mine, theirs)]
        reduced = {}
        for i, t in enumerate(tags):
            reduced.update(_unpack(whole[2 * i], whole[2 * i + 1], _GROUPS[t]))
        return reduced


_WEIGHTS = ("ffn1_norm", "ffn1_w_in", "ffn1_w_out", "mix_norm", "w_in", "hgrn_lb_logits", "hgrn_out_norm",
            "gdn_conv_w", "gdn_a_log", "gdn_dt_bias", "gdn_out_norm", "w_branch_hgrn", "w_branch_gdn", "w_out",
            "ffn2_norm", "ffn2_w_in", "ffn2_w_out", "final_norm")


def kernel(x, ffn1_norm, ffn1_w_in, ffn1_w_out, mix_norm, w_in, hgrn_lb_logits, hgrn_out_norm, gdn_conv_w, gdn_a_log, gdn_dt_bias, gdn_out_norm, w_branch_hgrn, w_branch_gdn, w_out, ffn2_norm, ffn2_w_in, ffn2_w_out, final_norm, loss_target, m_ffn1_norm, m_ffn1_w_in, m_ffn1_w_out, m_mix_norm, m_w_in, m_hgrn_lb_logits, m_hgrn_out_norm, m_gdn_conv_w, m_gdn_a_log, m_gdn_dt_bias, m_gdn_out_norm, m_w_branch_hgrn, m_w_branch_gdn, m_w_out, m_ffn2_norm, m_ffn2_w_in, m_ffn2_w_out, m_final_norm, v_ffn1_norm, v_ffn1_w_in, v_ffn1_w_out, v_mix_norm, v_w_in, v_hgrn_lb_logits, v_hgrn_out_norm, v_gdn_conv_w, v_gdn_a_log, v_gdn_dt_bias, v_gdn_out_norm, v_w_branch_hgrn, v_w_branch_gdn, v_w_out, v_ffn2_norm, v_ffn2_w_in, v_ffn2_w_out, v_final_norm):
    args = dict(locals())
    wts = {n: args[n] for n in _WEIGHTS}
    moms = {n: args["m_" + n] for n in _WEIGHTS}
    vars_ = {n: args["v_" + n] for n in _WEIGHTS}

    small = {n: wts[n].astype(F32) for n in _WEIGHTS if n not in _BIG_NAMES}
    exchange = _Exchange(wts)
    dx, small_grads = _local_step(x[0], loss_target[0], small, exchange)

    out_g, out_d, out_m, out_v = {}, {}, {}, {}

    def update(tags, reduced, after):
        for t in tags:
            for n in _group_names(_GROUPS[t]):
                shape = wts[n].shape
                w2 = wts[n].reshape(shape[-2], shape[-1])
                g2 = reduced[n]
                d, m2, v2 = _adamw(w2, g2, moms[n].reshape(w2.shape), vars_[n].reshape(w2.shape), "adamw_" + n, after)
                out_g[n], out_d[n], out_m[n], out_v[n] = (g2.reshape(shape), d.reshape(shape), m2.reshape(shape),
                                                          v2.reshape(shape))
                after = v2
        return after

    done = update(("ffn2", "mixer"), exchange.finish(("ffn2", "mixer"), after=dx), exchange.token)
    update(("ffn1",), exchange.finish(("ffn1",), after=done), None)

    small_names = [n for n, _ in _SMALL]
    zero = jnp.zeros((1,), F32)
    shapes = {n: (wts[n].shape if n != "loss" else (1,)) for n in small_names}
    sums, sd, sm_, sv = _small_sync(
        _pack_small(small_grads),
        _pack_small({n: (wts[n] if n != "loss" else zero) for n in small_names}),
        _pack_small({n: (moms[n] if n != "loss" else zero) for n in small_names}),
        _pack_small({n: (vars_[n] if n != "loss" else zero) for n in small_names}))
    sg_u, sd_u, sm_u, sv_u = (_unpack_small(p, shapes) for p in (sums, sd, sm_, sv))
    for n in small_names:
        if n != "loss":
            out_g[n], out_d[n], out_m[n], out_v[n] = sg_u[n], sd_u[n], sm_u[n], sv_u[n]
    loss = sg_u["loss"].reshape(())

    return (loss, dx[None], *[out_g[n] for n in _WEIGHTS], *[out_d[n] for n in _WEIGHTS],
            *[out_m[n] for n in _WEIGHTS], *[out_v[n] for n in _WEIGHTS])
```

```python
import numpy as np

import jax
import jax.numpy as jnp
from jax import lax
from jax.experimental import pallas as pl
from jax.experimental.pallas import tpu as pltpu

F32 = jnp.float32
BF16 = jnp.bfloat16

D_MODEL = 1024
D_FF = 2816
CHUNK = 64
HEAD = 128
HG_HEADS = 8
GD_HEADS = 16
HPS = 8
COMM_CHUNKS = 9
MM_TM = 1408
MM_TN = 1024
MM_TK = 1536
VMEM_LIMIT = 48 * 1024 * 1024
EPS = 1e-6
CONV_K = 4
IN_NAMES = ("hq", "hf", "hi", "hg", "gq", "gk", "gv", "ga", "gb", "gz", "gate_h", "gate_g")
IN_SIZES = (1024, 1024, 1024, 1024, 1024, 1024, 2048, 16, 16, 2048, 1024, 1024)
IN_WIDTH = sum(IN_SIZES)

ADAM_LR = 0.001
ADAM_B1 = 0.9
ADAM_B2 = 0.999
ADAM_EPS = 1e-08
ADAM_WD = 0.01
ADAM_STEP = 10

MESH = pl.DeviceIdType.MESH
_ARB = "arbitrary"
_PAR = "parallel"


def _bf(x):
    return x.astype(BF16)


def _dot(a, b):
    return jnp.dot(_bf(a), _bf(b), preferred_element_type=F32)


def _dot_nt(a, b):
    return lax.dot_general(_bf(a), _bf(b), (((1,), (1,)), ((), ())), preferred_element_type=F32)


def _dot_tn(a, b):
    return lax.dot_general(_bf(a), _bf(b), (((0,), (0,)), ((), ())), preferred_element_type=F32)


def _split3(x):
    hi = _bf(x)
    r = x - hi.astype(F32)
    mid = _bf(r)
    lo = _bf(r - mid.astype(F32))
    return hi, mid, lo


def _dot_mx(m, x):
    hi, mid, lo = _split3(x)
    return (jnp.dot(m, hi, preferred_element_type=F32) + jnp.dot(m, mid, preferred_element_type=F32)
            + jnp.dot(m, lo, preferred_element_type=F32))


def _dot_xm(x, m):
    hi, mid, lo = _split3(x)
    return (jnp.dot(hi, m, preferred_element_type=F32) + jnp.dot(mid, m, preferred_element_type=F32)
            + jnp.dot(lo, m, preferred_element_type=F32))


def _dot_hp(a, b):
    ah = _bf(a)
    al = _bf(a - ah.astype(F32))
    bh = _bf(b)
    bl = _bf(b - bh.astype(F32))
    return (jnp.dot(ah, bh, preferred_element_type=F32) + jnp.dot(ah, bl, preferred_element_type=F32)
            + jnp.dot(al, bh, preferred_element_type=F32))


def _sigmoid(x):
    return jax.nn.sigmoid(x)


def _silu(x):
    return x * _sigmoid(x)


def _dsilu(x):
    s = _sigmoid(x)
    return s * (1.0 + x * (1.0 - s))


def _softplus(x):
    return jnp.maximum(x, 0.0) + jnp.log(1.0 + jnp.exp(-jnp.abs(x)))


def _rowsum(x):
    return jnp.sum(x, axis=1, keepdims=True)


def _col_to_row(col, eye):
    return jnp.sum(eye * col, axis=0, keepdims=True)


def _row_to_col(row, eye):
    return jnp.sum(eye * row, axis=1, keepdims=True)


def _pick(dim, pref, unit=128):
    if dim <= pref:
        return dim
    t = pref
    while t >= unit:
        if dim % t == 0:
            return t
        t -= unit
    return dim


def _params(*sem):
    return pltpu.CompilerParams(dimension_semantics=tuple(sem), vmem_limit_bytes=VMEM_LIMIT)


def _mm(a, b, *, ta=False, tb=False, alpha=1.0, res=None, out_dtype=F32, name="mm", b_from=0):
    m = a.shape[1] if ta else a.shape[0]
    k = a.shape[0] if ta else a.shape[1]
    n = b.shape[0] if tb else b.shape[1]
    assert b_from + k <= (b.shape[1] if tb else b.shape[0])
    tm, tn, tk = _pick(m, MM_TM), _pick(n, MM_TN), _pick(k, MM_TK)
    if tn < MM_TN < n and n % MM_TM == 0:
        tn = MM_TM
    nk = k // tk
    assert b_from % tk == 0
    b0 = b_from // tk
    a_spec = pl.BlockSpec((tk, tm), lambda i, j, l: (l, i)) if ta else pl.BlockSpec((tm, tk), lambda i, j, l: (i, l))
    b_spec = (pl.BlockSpec((tn, tk), lambda i, j, l: (j, b0 + l)) if tb
              else pl.BlockSpec((tk, tn), lambda i, j, l: (b0 + l, j)))
    o_spec = pl.BlockSpec((tm, tn), lambda i, j, l: (i, j))
    dims = (((0 if ta else 1,), (1 if tb else 0,)), ((), ()))
    has_res = res is not None

    def finish(r, r_ref, o_ref):
        if alpha != 1.0:
            r = r * alpha
        if has_res:
            r = r + r_ref[...]
        o_ref[...] = r.astype(out_dtype)

    def body(*refs):
        a_ref, b_ref = refs[0], refs[1]
        r_ref = refs[2] if has_res else None
        o_ref = refs[3] if has_res else refs[2]
        part = lax.dot_general(_bf(a_ref[...]), _bf(b_ref[...]), dims, preferred_element_type=F32)
        if nk == 1:
            finish(part, r_ref, o_ref)
            return
        acc = refs[-1]
        step = pl.program_id(2)

        @pl.when(step == 0)
        def _():
            acc[...] = part

        @pl.when(step != 0)
        def _():
            acc[...] += part

        @pl.when(step == nk - 1)
        def _():
            finish(acc[...], r_ref, o_ref)

    ins = [a, b] + ([res] if has_res else [])
    in_specs = [a_spec, b_spec] + ([o_spec] if has_res else [])
    return pl.pallas_call(
        body, name=name, grid=(m // tm, n // tn, nk), in_specs=in_specs, out_specs=o_spec,
        out_shape=jax.ShapeDtypeStruct((m, n), out_dtype),
        scratch_shapes=[pltpu.VMEM((tm, tn), F32)] if nk > 1 else [],
        compiler_params=_params(_PAR, _PAR, _ARB))(*ins)


def _row_spec(tr, w):
    return pl.BlockSpec((tr, w), lambda i: (i, 0))


def _full_spec(shape):
    return pl.BlockSpec(shape, lambda i: tuple(0 for _ in shape))


def _view(arr, off, width):
    return arr, off, width


def _view_rows(view, tr):
    _, off, width = view
    assert off % width == 0
    return pl.BlockSpec((tr, width), lambda i: (i, off // width))


def _view_tile(view, rows, bw, cidx=lambda c: c):
    _, off, width = view
    assert off % bw == 0 and width % bw == 0
    return pl.BlockSpec((rows, bw), lambda c, g: (cidx(c), off // bw + g))


def _rmsnorm_fwd(x, g, name):
    t, d = x.shape
    tr = _pick(t, 256, 8)

    def body(x_ref, g_ref, o_ref):
        xv = x_ref[...]
        r = lax.rsqrt(jnp.mean(xv * xv, axis=1, keepdims=True) + EPS)
        o_ref[...] = (xv * r * g_ref[...]).astype(BF16)

    return pl.pallas_call(
        body, name=name, grid=(t // tr,), in_specs=[_row_spec(tr, d), _full_spec((1, d))],
        out_specs=_row_spec(tr, d), out_shape=jax.ShapeDtypeStruct((t, d), BF16),
        compiler_params=_params(_PAR))(x, g)


def _rmsnorm_bwd(x, g, dn, res, name):
    t, d = x.shape
    tr = _pick(t, 256, 8)

    def body(x_ref, g_ref, dn_ref, r_ref, dx_ref, dxb_ref, dg_ref):
        @pl.when(pl.program_id(0) == 0)
        def _():
            dg_ref[...] = jnp.zeros_like(dg_ref)

        xv = x_ref[...]
        r = lax.rsqrt(jnp.mean(xv * xv, axis=1, keepdims=True) + EPS)
        xh = xv * r
        dy = dn_ref[...]
        dg_ref[...] += jnp.sum(dy * xh, axis=0, keepdims=True)
        dxh = dy * g_ref[...]
        dx = r_ref[...] + r * (dxh - xh * jnp.mean(dxh * xh, axis=1, keepdims=True))
        dx_ref[...] = dx
        dxb_ref[...] = dx.astype(BF16)

    return pl.pallas_call(
        body, name=name, grid=(t // tr,),
        in_specs=[_row_spec(tr, d), _full_spec((1, d)), _row_spec(tr, d), _row_spec(tr, d)],
        out_specs=[_row_spec(tr, d), _row_spec(tr, d), _full_spec((1, d))],
        out_shape=[jax.ShapeDtypeStruct((t, d), F32), jax.ShapeDtypeStruct((t, d), BF16),
                   jax.ShapeDtypeStruct((1, d), F32)],
        compiler_params=_params(_ARB))(x, g, dn, res)


FFN_TN = 1408
FFN_TM = 512


def _ffn_in_act(n, w_in, name):
    t, d = n.shape
    tm = _pick(t, FFN_TM)
    nf = D_FF // FFN_TN

    def body(n_ref, wa_ref, wb_ref, a_ref, b_ref, hm_ref):
        nv = n_ref[...]
        a = jnp.dot(nv, wa_ref[...], preferred_element_type=F32)
        b = jnp.dot(nv, wb_ref[...], preferred_element_type=F32)
        a_ref[...] = a.astype(BF16)
        b_ref[...] = b.astype(BF16)
        hm_ref[...] = (_silu(a) * b).astype(BF16)

    tile = pl.BlockSpec((tm, FFN_TN), lambda i, j: (i, j))
    return pl.pallas_call(
        body, name=name, grid=(t // tm, nf),
        in_specs=[pl.BlockSpec((tm, d), lambda i, j: (i, 0)), pl.BlockSpec((d, FFN_TN), lambda i, j: (0, j)),
                  pl.BlockSpec((d, FFN_TN), lambda i, j: (0, nf + j))],
        out_specs=[tile, tile, tile], out_shape=[jax.ShapeDtypeStruct((t, D_FF), BF16)] * 3,
        compiler_params=_params(_PAR, _PAR))(n, w_in, w_in)


def _ffn_dact(dout, w_out, a, b, name):
    t, d = dout.shape
    tm = _pick(t, FFN_TM)

    def body(do_ref, w_ref, a_ref, b_ref, da_ref, db_ref, hm_ref):
        dh = 0.5 * _dot_nt(do_ref[...], w_ref[...])
        av = a_ref[...].astype(F32)
        bv = b_ref[...].astype(F32)
        sa = _silu(av)
        da_ref[...] = (dh * bv * _dsilu(av)).astype(BF16)
        db_ref[...] = (dh * sa).astype(BF16)
        hm_ref[...] = (sa * bv).astype(BF16)

    tile = pl.BlockSpec((tm, FFN_TN), lambda i, j: (i, j))
    return pl.pallas_call(
        body, name=name, grid=(t // tm, D_FF // FFN_TN),
        in_specs=[pl.BlockSpec((tm, d), lambda i, j: (i, 0)), pl.BlockSpec((FFN_TN, d), lambda i, j: (j, 0)), tile, tile],
        out_specs=[tile, tile, tile], out_shape=[jax.ShapeDtypeStruct((t, D_FF), BF16)] * 3,
        compiler_params=_params(_PAR, _PAR))(dout, w_out, a, b)


def _merge_fwd(yh, yg, gh, gg):
    t, d = yh.shape
    tr = _pick(t, 256, 8)

    def body(yh_ref, yg_ref, gh_ref, gg_ref, o_ref):
        o_ref[...] = (_sigmoid(gh_ref[...]) * yh_ref[...] + _sigmoid(gg_ref[...]) * yg_ref[...]).astype(BF16)

    return pl.pallas_call(
        body, name="merge_fwd", grid=(t // tr,),
        in_specs=[_row_spec(tr, d), _row_spec(tr, d), _view_rows(gh, tr), _view_rows(gg, tr)],
        out_specs=_row_spec(tr, d),
        out_shape=jax.ShapeDtypeStruct((t, d), BF16), compiler_params=_params(_PAR))(yh, yg, gh[0], gg[0])


def _merge_bwd(dy, yh, yg, gh, gg):
    t, d = yh.shape
    tr = _pick(t, 256, 8)

    def body(dy_ref, yh_ref, yg_ref, gh_ref, gg_ref, dyh_ref, dyg_ref, dgh_ref, dgg_ref):
        dyv = dy_ref[...]
        sh = _sigmoid(gh_ref[...])
        sg = _sigmoid(gg_ref[...])
        dyh_ref[...] = (dyv * sh).astype(BF16)
        dyg_ref[...] = (dyv * sg).astype(BF16)
        dgh_ref[...] = (dyv * yh_ref[...] * sh * (1.0 - sh)).astype(BF16)
        dgg_ref[...] = (dyv * yg_ref[...] * sg * (1.0 - sg)).astype(BF16)

    return pl.pallas_call(
        body, name="merge_bwd", grid=(t // tr,),
        in_specs=[_row_spec(tr, d)] * 3 + [_view_rows(gh, tr), _view_rows(gg, tr)], out_specs=[_row_spec(tr, d)] * 4,
        out_shape=[jax.ShapeDtypeStruct((t, d), BF16)] * 4,
        compiler_params=_params(_PAR))(dy, yh, yg, gh[0], gg[0])


def _final_loss(h, g, tgt):
    t, d = h.shape
    tr = _pick(t, 256, 8)

    def body(h_ref, g_ref, t_ref, loss_ref, dh_ref, dhb_ref, dg_ref):
        @pl.when(pl.program_id(0) == 0)
        def _():
            dg_ref[...] = jnp.zeros_like(dg_ref)
            loss_ref[...] = jnp.zeros_like(loss_ref)

        xv = h_ref[...]
        gv = g_ref[...]
        r = lax.rsqrt(jnp.mean(xv * xv, axis=1, keepdims=True) + EPS)
        xh = xv * r
        err = xh * gv - t_ref[...]
        loss_ref[...] += 0.5 * jnp.sum(jnp.mean(err * err, axis=1, keepdims=True), axis=0, keepdims=True)
        dy = err * (1.0 / d)
        dg_ref[...] += jnp.sum(dy * xh, axis=0, keepdims=True)
        dxh = dy * gv
        dh = r * (dxh - xh * jnp.mean(dxh * xh, axis=1, keepdims=True))
        dh_ref[...] = dh
        dhb_ref[...] = dh.astype(BF16)

    return pl.pallas_call(
        body, name="final_loss", grid=(t // tr,),
        in_specs=[_row_spec(tr, d), _full_spec((1, d)), _row_spec(tr, d)],
        out_specs=[_full_spec((1, 128)), _row_spec(tr, d), _row_spec(tr, d), _full_spec((1, d))],
        out_shape=[jax.ShapeDtypeStruct((1, 128), F32), jax.ShapeDtypeStruct((t, d), F32),
                   jax.ShapeDtypeStruct((t, d), BF16), jax.ShapeDtypeStruct((1, d), F32)],
        compiler_params=_params(_ARB))(h, g, tgt)


def _hg_consts():
    c = CHUNK
    t = np.arange(c)
    mats, masks = [], []
    for lvl in range(6):
        m = 1 << lvl
        blk = t // m
        mat = np.zeros((c, c), np.float32)
        for tt in range(c):
            b = blk[tt]
            if b % 2 == 1:
                mat[tt, b * m:tt + 1] = 1.0
            else:
                mat[tt, tt + 1:(b + 1) * m] = 1.0
        mats.append(mat)
        same = (t[:, None] // (2 * m)) == (t[None, :] // (2 * m))
        masks.append((same & (blk[:, None] % 2 == 1) & (blk[None, :] % 2 == 0)).astype(np.float32))
    pre = np.tril(np.ones((c, c), np.float32))
    suf = np.triu(np.ones((c, c), np.float32), 1)
    mstack = np.concatenate(mats + [pre, suf], 0)
    masks.append(np.eye(c, dtype=np.float32))
    return (jnp.asarray(mstack, BF16), jnp.asarray(mstack.T.copy(), BF16), jnp.asarray(np.stack(masks), F32),
            jnp.asarray(np.eye(HEAD, dtype=np.float32)))


def _gd_consts():
    c = CHUNK
    incl = np.tril(np.ones((c, c), np.float32))
    strict = np.tril(np.ones((c, c), np.float32), -1)
    eye = np.eye(c, dtype=np.float32)
    masks = np.stack([incl, strict, eye, incl.T.copy()])
    return jnp.asarray(incl, BF16), jnp.asarray(incl.T.copy(), BF16), jnp.asarray(masks, F32)


def _chunks_per_step(nc):
    for cb in (32 // HPS, 2, 1):
        if nc % cb == 0:
            return cb
    return 1


def _hg_prep(hq, hf, lg):
    lb = _sigmoid(lg[0:1, :] - lg[1:2, :])
    sg = _sigmoid(hf)
    sgn = _sigmoid(-hf)
    f = lb + (1.0 - lb) * sg
    lf = jnp.log(f)
    kk = (1.0 - lb) * sgn
    q = _silu(hq) * (HEAD ** -0.5)
    return lb, sg, sgn, f, lf, kk, q


def _mx_each(m, xs):
    wide = [jnp.concatenate(_split3(x), axis=1) for x in xs]
    prods = [jnp.dot(m, w, preferred_element_type=F32) for w in wide]
    return [p[:, :HEAD] + p[:, HEAD:2 * HEAD] + p[:, 2 * HEAD:] for p in prods]


def _hg_scaled(x, ex):
    xb = [_bf(a) for a in x]
    eb = [_bf(e[:6 * CHUNK]) for e in ex]
    return [[a * e[lvl * CHUNK:(lvl + 1) * CHUNK] for lvl in range(6)] for a, e in zip(xb, eb)]


def _hg_scores(q, kk, qe, ke, mask_ref):
    p = [mask_ref[6] * _rowsum(a * b) for a, b in zip(q, kk)]
    for lvl in range(6):
        d = [_dot_nt(a[lvl], b[lvl]) for a, b in zip(qe, ke)]
        p = [x + mask_ref[lvl] * y for x, y in zip(p, d)]
    return p


def _hgrn_fwd(hq, hf, hi, hg, logits, gain, consts):
    t = hq[0].shape[0]
    nc = t // CHUNK
    cb = _chunks_per_step(nc)
    rows = cb * CHUNK
    mstack, _, masks, eye = consts
    tile = pl.BlockSpec((rows, HPS * HEAD), lambda c, g: (c, g))

    def body(hq_ref, hf_ref, hi_ref, hg_ref, lg_ref, gain_ref, m_ref, mask_ref, eye_ref,
             oraw_ref, og_ref, ssave_ref, state):
        c = pl.program_id(0)
        g = pl.program_id(1)

        @pl.when(c == 0)
        def _():
            for hh in range(HPS):
                state[g * HPS + hh] = jnp.zeros((HEAD, HEAD), F32)

        lg_all = lg_ref[...]
        gain_v = gain_ref[...]

        def one(i, carry):
            sl = pl.ds(pl.multiple_of(i * CHUNK, CHUNK), CHUNK)
            hs = range(HPS)
            heads = [g * HPS + hh for hh in hs]
            ln = [slice(hh * HEAD, (hh + 1) * HEAD) for hh in hs]
            preps = [_hg_prep(hq_ref[sl, s], hf_ref[sl, s], lg_all[:, s]) for s in ln]
            lf, kk, q = [p[4] for p in preps], [p[5] for p in preps], [p[6] for p in preps]
            v = [hi_ref[sl, s] for s in ln]
            ex = [jnp.exp(x) for x in _mx_each(m_ref[...], lf)]
            eb = [e[6 * CHUNK:7 * CHUNK] for e in ex]
            esfx = [e[7 * CHUNK:8 * CHUNK] for e in ex]
            qe, ke = _hg_scaled(q, ex), _hg_scaled(kk, ex)
            p = _hg_scores(q, kk, qe, ke, mask_ref)
            s0 = [state[h] for h in heads]
            o = _each(lambda a, e, s, pp, vv: _dot(a * e, s) + _dot(pp, vv), q, eb, s0, p, v)
            eye_v = eye_ref[...]
            s1 = _each(lambda s, e, kx, ef, vv: s * _row_to_col(e[CHUNK - 1:CHUNK, :], eye_v) + _dot_tn(kx * ef, vv),
                       s0, eb, kk, esfx, v)
            for hh in hs:
                ssave_ref[i, hh] = s0[hh]
                state[heads[hh]] = s1[hh]
                oraw_ref[sl, ln[hh]] = o[hh]
                r = lax.rsqrt(jnp.mean(o[hh] * o[hh], axis=1, keepdims=True) + EPS)
                og_ref[sl, ln[hh]] = (o[hh] * r * gain_v * _silu(hg_ref[sl, ln[hh]])).astype(BF16)
            return carry

        lax.fori_loop(0, cb, one, 0, unroll=2)

    return pl.pallas_call(
        body, name="hgrn_fwd", grid=(nc // cb, HG_HEADS // HPS),
        in_specs=[_view_tile(v, rows, HPS * HEAD) for v in (hq, hf, hi, hg)] + [
                  pl.BlockSpec((2, HPS * HEAD), lambda c, g: (0, g)),
                  pl.BlockSpec((1, HEAD), lambda c, g: (0, 0)),
                  pl.BlockSpec(mstack.shape, lambda c, g: (0, 0)),
                  pl.BlockSpec(masks.shape, lambda c, g: (0, 0, 0)),
                  pl.BlockSpec(eye.shape, lambda c, g: (0, 0))],
        out_specs=[tile, tile, pl.BlockSpec((cb, HPS, HEAD, HEAD), lambda c, g: (c, g, 0, 0))],
        out_shape=[jax.ShapeDtypeStruct((t, HG_HEADS * HEAD), F32), jax.ShapeDtypeStruct((t, HG_HEADS * HEAD), BF16),
                   jax.ShapeDtypeStruct((nc, HG_HEADS, HEAD, HEAD), F32)],
        scratch_shapes=[pltpu.VMEM((HG_HEADS, HEAD, HEAD), F32)],
        compiler_params=_params(_ARB, _ARB))(hq[0], hf[0], hi[0], hg[0], logits, gain, mstack, masks, eye)


def _hgrn_bwd(hq, hf, hi, hg, logits, gain, oraw, ssave, dog, consts):
    t = hq[0].shape[0]
    nc = t // CHUNK
    cb = _chunks_per_step(nc)
    rows = cb * CHUNK
    nb = nc // cb
    mstack, mstack_t, masks, eye = consts
    tile = pl.BlockSpec((rows, HPS * HEAD), lambda c, g: (nb - 1 - c, g))

    def body(hq_ref, hf_ref, hi_ref, hg_ref, lg_ref, gain_ref, oraw_ref, ssave_ref, dog_ref, m_ref, mt_ref,
             mask_ref, eye_ref, dhq_ref, dhf_ref, dhi_ref, dhg_ref, dgain_ref, dlb_ref, dstate):
        c = pl.program_id(0)
        g = pl.program_id(1)

        @pl.when(c == 0)
        def _():
            for hh in range(HPS):
                dstate[g * HPS + hh] = jnp.zeros((HEAD, HEAD), F32)

        @pl.when((c == 0) & (g == 0))
        def _():
            dgain_ref[...] = jnp.zeros_like(dgain_ref)
            dlb_ref[...] = jnp.zeros_like(dlb_ref)

        lg_all = lg_ref[...]
        gain_v = gain_ref[...]
        eye_v = eye_ref[...]
        last_row = (lax.broadcasted_iota(jnp.int32, (CHUNK, HEAD), 0) == CHUNK - 1).astype(F32)

        def one(j, carry):
            i = cb - 1 - j
            sl = pl.ds(pl.multiple_of(i * CHUNK, CHUNK), CHUNK)
            hs = range(HPS)
            heads = [g * HPS + hh for hh in hs]
            ln = [slice(hh * HEAD, (hh + 1) * HEAD) for hh in hs]
            hqv = [hq_ref[sl, s] for s in ln]
            hgv = [hg_ref[sl, s] for s in ln]
            preps = [_hg_prep(a, hf_ref[sl, s], lg_all[:, s]) for a, s in zip(hqv, ln)]
            lb, sg, sgn, f, lf, kk, q = ([p[n] for p in preps] for n in range(7))
            v = [hi_ref[sl, s] for s in ln]
            ex = [jnp.exp(x) for x in _mx_each(m_ref[...], lf)]
            eb = [e[6 * CHUNK:7 * CHUNK] for e in ex]
            esfx = [e[7 * CHUNK:8 * CHUNK] for e in ex]
            qe, ke = _hg_scaled(q, ex), _hg_scaled(kk, ex)
            p = _hg_scores(q, kk, qe, ke, mask_ref)
            s0 = [ssave_ref[i, hh] for hh in hs]
            ds = [dstate[h] for h in heads]

            o = [oraw_ref[sl, s] for s in ln]
            r = [lax.rsqrt(jnp.mean(x * x, axis=1, keepdims=True) + EPS) for x in o]
            on = _each(lambda x, y: x * y, o, r)
            dg_out = [dog_ref[sl, s] for s in ln]
            sgate = [_silu(x) for x in hgv]
            for hh in hs:
                dhg_ref[sl, ln[hh]] = (dg_out[hh] * on[hh] * gain_v * _dsilu(hgv[hh])).astype(BF16)
            dgain_ref[...] += sum(jnp.sum(d * s * n, axis=0, keepdims=True) for d, s, n in zip(dg_out, sgate, on))
            don = _each(lambda d, s: d * s * gain_v, dg_out, sgate)
            do = _each(lambda rr, dn, n: rr * (dn - n * jnp.mean(dn * n, axis=1, keepdims=True)), r, don, on)

            dp = _each(_dot_nt, do, v)
            dv = _each(lambda pp, d, kx, ef, s: _dot_tn(pp, d) + _dot(kx * ef, s), p, do, kk, esfx, ds)
            dqb = _each(_dot_nt, do, s0)
            dkx = _each(_dot_nt, v, ds)
            diag = [_rowsum(mask_ref[6] * x) for x in dp]
            dq = _each(lambda a, e, d, kx: a * e + d * kx, dqb, eb, diag, kk)
            dk = _each(lambda a, e, d, qq: a * e + d * qq, dkx, esfx, diag, q)
            dxs = [[] for _ in hs]
            for lvl in range(6):
                el = [e[lvl * CHUNK:(lvl + 1) * CHUNK] for e in ex]
                gm = [mask_ref[lvl] * x for x in dp]
                gm = [_bf(x) for x in gm]
                a1 = _each(lambda m_, kx: _dot(m_, kx[lvl]), gm, ke)
                a2 = _each(lambda m_, qq: _dot_tn(m_, qq[lvl]), gm, qe)
                dq = _each(lambda x, a, e: x + a * e, dq, a1, el)
                dk = _each(lambda x, a, e: x + a * e, dk, a2, el)
                for hh in hs:
                    dxs[hh].append((a1[hh] * q[hh] + a2[hh] * kk[hh]) * el[hh])
            e_end_row = [e[CHUNK - 1:CHUNK, :] for e in eb]
            ds_new = _each(lambda qq, e, d, er, s: _dot_tn(qq * e, d) + _row_to_col(er, eye_v) * s, q, eb, do, e_end_row, ds)
            for hh in hs:
                dstate[heads[hh]] = ds_new[hh]
                dend_row = _col_to_row(_rowsum(s0[hh] * ds[hh]), eye_v)
                dxs[hh].append(dqb[hh] * q[hh] * eb[hh] + last_row * (e_end_row[hh] * dend_row))
                dxs[hh].append(dkx[hh] * kk[hh] * esfx[hh])
            dlf = _mx_each(mt_ref[...], [jnp.concatenate(x, axis=0) for x in dxs])

            for hh in hs:
                dhi_ref[sl, ln[hh]] = dv[hh].astype(BF16)
                dhq_ref[sl, ln[hh]] = (dq[hh] * (HEAD ** -0.5) * _dsilu(hqv[hh])).astype(BF16)
                df = dlf[hh] / f[hh]
                dsig = (1.0 - lb[hh]) * sg[hh] * sgn[hh]
                dhf_ref[sl, ln[hh]] = ((df - dk[hh]) * dsig).astype(BF16)
                dlb_t = jnp.sum(df * sgn[hh] - dk[hh] * sgn[hh], axis=0, keepdims=True)
                dlb_ref[pl.ds(heads[hh], 1), :] += dlb_t * lb[hh] * (1.0 - lb[hh])
            return carry

        lax.fori_loop(0, cb, one, 0, unroll=2)

    outs = [jax.ShapeDtypeStruct((t, HG_HEADS * HEAD), BF16)] * 4 + [
        jax.ShapeDtypeStruct((1, HEAD), F32), jax.ShapeDtypeStruct((HG_HEADS, HEAD), F32)]
    return pl.pallas_call(
        body, name="hgrn_bwd", grid=(nb, HG_HEADS // HPS),
        in_specs=[_view_tile(v, rows, HPS * HEAD, lambda c: nb - 1 - c) for v in (hq, hf, hi, hg)] + [
                  pl.BlockSpec((2, HPS * HEAD), lambda c, g: (0, g)),
                  pl.BlockSpec((1, HEAD), lambda c, g: (0, 0)), tile,
                  pl.BlockSpec((cb, HPS, HEAD, HEAD), lambda c, g: (nb - 1 - c, g, 0, 0)), tile,
                  pl.BlockSpec(mstack.shape, lambda c, h: (0, 0)),
                  pl.BlockSpec(mstack_t.shape, lambda c, h: (0, 0)),
                  pl.BlockSpec(masks.shape, lambda c, h: (0, 0, 0)),
                  pl.BlockSpec(eye.shape, lambda c, h: (0, 0))],
        out_specs=[tile, tile, tile, tile, pl.BlockSpec((1, HEAD), lambda c, h: (0, 0)),
                   pl.BlockSpec((HG_HEADS, HEAD), lambda c, h: (0, 0))],
        out_shape=outs, scratch_shapes=[pltpu.VMEM((HG_HEADS, HEAD, HEAD), F32)],
        compiler_params=_params(_ARB, _ARB))(hq[0], hf[0], hi[0], hg[0], logits, gain, oraw, ssave, dog, mstack,
                                             mstack_t, masks, eye)


CONV_W = 512


def _per_head(fn, *arrs):
    width = arrs[0].shape[1]
    return jnp.concatenate([fn(*[a[:, j:j + HEAD] for a in arrs]) for j in range(0, width, HEAD)], axis=1)


def _shift_down(xv, halo, d, top_rows):
    if d == 0:
        return xv, xv[0:8]
    main = pltpu.roll(xv, d, 0)
    top = jnp.where(top_rows < d, pltpu.roll(halo, d, 0), main[0:8])
    return main, top


def _conv_parts(x_ref, halo_ref, w_ref, first):
    xv = x_ref[...]
    halo = jnp.where(first, 0.0, halo_ref[...])
    top_rows = lax.broadcasted_iota(jnp.int32, (8, xv.shape[1]), 0)
    shifted = [_shift_down(xv, halo, CONV_K - 1 - j, top_rows) for j in range(CONV_K)]
    w = w_ref[...]
    acc = sum(shifted[j][0] * w[j:j + 1, :] for j in range(CONV_K))
    acc_top = sum(shifted[j][1] * w[j:j + 1, :] for j in range(CONV_K))
    return shifted, acc, acc_top


def _conv_fwd(x, w8, l2scale, name):
    x, off, width = x
    t = x.shape[0]
    o = off // CONV_W
    tr = _pick(t, 512, 8)

    def post(cv):
        s = _silu(cv)
        if l2scale is not None:
            s = _per_head(lambda sh: sh * (lax.rsqrt(_rowsum(sh * sh) + EPS) * l2scale), s)
        return s

    def body(x_ref, halo_ref, w_ref, o_ref):
        _, acc, acc_top = _conv_parts(x_ref, halo_ref, w_ref, pl.program_id(1) == 0)
        o_ref[...] = post(acc)
        o_ref[0:8, :] = post(acc_top)

    return pl.pallas_call(
        body, name=name, grid=(width // CONV_W,t // tr),
        in_specs=[pl.BlockSpec((tr, CONV_W), lambda j, i: (i, o + j)),
                  pl.BlockSpec((8, CONV_W), lambda j, i: (jnp.maximum(i * (tr // 8) - 1, 0), o + j)),
                  pl.BlockSpec((8, CONV_W), lambda j, i: (0, j))],
        out_specs=pl.BlockSpec((tr, CONV_W), lambda j, i: (i, j)),
        out_shape=jax.ShapeDtypeStruct((t, width), F32), compiler_params=_params(_PAR, _PAR))(x, x, w8)


def _conv_bwd_a(x, w8, dy, l2scale, name):
    x, off, width = x
    t = x.shape[0]
    o = off // CONV_W
    tr = _pick(t, 512, 8)

    def l2_bwd(s, dyh):
        r = lax.rsqrt(_rowsum(s * s) + EPS)
        y0 = s * r
        dy0 = dyh * l2scale
        return r * (dy0 - y0 * _rowsum(dy0 * y0))

    def to_dc(cv, dyv):
        if l2scale is not None:
            dyv = _per_head(l2_bwd, _silu(cv), dyv)
        return dyv * _dsilu(cv)

    def body(x_ref, halo_ref, w_ref, dy_ref, dc_ref, dw_ref):
        @pl.when(pl.program_id(1) == 0)
        def _():
            dw_ref[...] = jnp.zeros_like(dw_ref)

        shifted, acc, acc_top = _conv_parts(x_ref, halo_ref, w_ref, pl.program_id(1) == 0)
        dyv = dy_ref[...]
        dc = to_dc(acc, dyv)
        dc_top = to_dc(acc_top, dyv[0:8])
        dc_ref[...] = dc
        dc_ref[0:8, :] = dc_top
        rest = (lax.broadcasted_iota(jnp.int32, dc.shape, 0) >= 8).astype(F32)
        dc_rest = dc * rest
        for j in range(CONV_K):
            dw_ref[j:j + 1, :] += (jnp.sum(dc_rest * shifted[j][0], axis=0, keepdims=True)
                                   + jnp.sum(dc_top * shifted[j][1], axis=0, keepdims=True))

    return pl.pallas_call(
        body, name=name, grid=(width // CONV_W,t // tr),
        in_specs=[pl.BlockSpec((tr, CONV_W), lambda j, i: (i, o + j)),
                  pl.BlockSpec((8, CONV_W), lambda j, i: (jnp.maximum(i * (tr // 8) - 1, 0), o + j)),
                  pl.BlockSpec((8, CONV_W), lambda j, i: (0, j)),
                  pl.BlockSpec((tr, CONV_W), lambda j, i: (i, j))],
        out_specs=[pl.BlockSpec((tr, CONV_W), lambda j, i: (i, j)), pl.BlockSpec((8, CONV_W), lambda j, i: (0, j))],
        out_shape=[jax.ShapeDtypeStruct((t, width), F32), jax.ShapeDtypeStruct((8, width), F32)],
        compiler_params=_params(_PAR, _ARB))(x, x, w8, dy)


def _conv_bwd_b(dc, w8, name):
    t, width = dc.shape
    tr = _pick(t, 512, 8)
    nt = t // tr

    def body(dc_ref, halo_ref, w_ref, dx_ref):
        dcv = dc_ref[...]
        halo = jnp.where(pl.program_id(1) == nt - 1, 0.0, halo_ref[...])
        w = w_ref[...]
        bot_rows = lax.broadcasted_iota(jnp.int32, (8, CONV_W), 0)
        acc = dcv * w[CONV_K - 1:CONV_K, :]
        acc_bot = dcv[tr - 8:tr] * w[CONV_K - 1:CONV_K, :]
        for d in range(1, CONV_K):
            main = pltpu.roll(dcv, tr - d, 0)
            bot = jnp.where(bot_rows >= 8 - d, pltpu.roll(halo, 8 - d, 0), main[tr - 8:tr])
            wj = w[CONV_K - 1 - d:CONV_K - d, :]
            acc = acc + main * wj
            acc_bot = acc_bot + bot * wj
        dx_ref[...] = acc.astype(BF16)
        dx_ref[tr - 16:tr, :] = jnp.concatenate([acc[tr - 16:tr - 8], acc_bot], axis=0).astype(BF16)

    return pl.pallas_call(
        body, name=name, grid=(width // CONV_W,nt),
        in_specs=[pl.BlockSpec((tr, CONV_W), lambda j, i: (i, j)),
                  pl.BlockSpec((8, CONV_W), lambda j, i: (jnp.minimum((i + 1) * (tr // 8), t // 8 - 1), j)),
                  pl.BlockSpec((8, CONV_W), lambda j, i: (0, j))],
        out_specs=pl.BlockSpec((tr, CONV_W), lambda j, i: (i, j)),
        out_shape=jax.ShapeDtypeStruct((t, width), BF16), compiler_params=_params(_PAR, _PAR))(dc, dc, w8)


def _each(f, *lists):
    return [f(*xs) for xs in zip(*lists)]


def _split2_each(xs):
    hi = [_bf(x) for x in xs]
    lo = [_bf(x - h.astype(F32)) for x, h in zip(xs, hi)]
    return hi, lo


def _hp_each(a_split, b_split):
    (ah, al), (bh, bl) = a_split, b_split
    rows = ah[0].shape[0]
    d12 = [jnp.dot(jnp.concatenate([x, y], axis=0), z, preferred_element_type=F32) for x, y, z in zip(ah, al, bh)]
    d3 = [jnp.dot(x, y, preferred_element_type=F32) for x, y in zip(ah, bl)]
    return [d[:rows] + d[rows:] + e for d, e in zip(d12, d3)]


def _tri_inv_each(a_list, eye):
    ns = [-a for a in a_list]
    ps = [eye + n for n in ns]
    n_split = _split2_each(ns)
    for _ in range(5):
        ns = _hp_each(n_split, n_split)
        n_split = _split2_each(ns)
        ps = [p + d for p, d in zip(ps, _hp_each(_split2_each(ps), n_split))]
    return ps


def _gd_gates(gab, alog, dtb):
    sp_arg = gab + dtb
    return sp_arg, -jnp.exp(alog) * _softplus(sp_arg), _sigmoid(gab)


def _pick_lane(tile, base, head):
    g, hh = head
    col = tile[:, base + hh:base + hh + 1]
    for gi in range(1, GD_HEADS // HPS):
        lane = base + gi * HPS + hh
        col = jnp.where(g == gi, tile[:, lane:lane + 1], col)
    return col


def _gd_chunks(q, k, v, g_all, beta_all, sel, l_ref, mask_ref, tm=None):
    incl, strict, eye, upper = mask_ref[0], mask_ref[1], mask_ref[2], mask_ref[3]
    lmat = l_ref[...]
    gb = [jnp.broadcast_to(_pick_lane(g_all, 0, s), (CHUNK, HEAD)) for s in sel]
    bb = [jnp.broadcast_to(_pick_lane(beta_all, GD_HEADS, s), (CHUNK, HEAD)) for s in sel]
    gam = _mx_each(lmat, gb)
    gam_row = [jnp.sum(x[:, :CHUNK] * upper, axis=0, keepdims=True) for x in gb]
    lm = _each(lambda gm, gr: incl * jnp.exp(jnp.minimum(gm[:, :CHUNK] - gr, 0.0)), gam, gam_row)
    kb = _each(lambda x, b: x * b, k, bb)
    a = _each(lambda x, y, m: strict * _dot_nt(x, y) * m, kb, k, lm)
    if tm is None:
        tm = _tri_inv_each(a, eye)
    eg = [jnp.exp(x) for x in gam]
    vb = _each(lambda x, b: x * b, v, bb)
    kbg = _each(lambda x, e: x * e, kb, eg)
    uw = _each(lambda t_, x, y: _dot(t_, jnp.concatenate([x, y], axis=1)), tm, vb, kbg)
    u = [x[:, :HEAD] for x in uw]
    w = [x[:, HEAD:] for x in uw]
    qk = _each(lambda x, y, m: _dot_nt(x, y) * m, q, k, lm)
    g_end = [x[CHUNK - 1:CHUNK, :] for x in gam]
    ekg = _each(lambda e, x: jnp.exp(e - x), g_end, gam)
    ge = [jnp.exp(e) for e in g_end]
    kg = _each(lambda x, e: x * e, k, ekg)
    qg = _each(lambda x, e: x * e, q, eg)
    names = ("bb", "lm", "kb", "a", "tm", "eg", "vb", "kbg", "u", "w", "qk", "ekg", "ge", "kg", "qg")
    cols = (bb, lm, kb, a, tm, eg, vb, kbg, u, w, qk, ekg, ge, kg, qg)
    return [dict(zip(names, vals)) for vals in zip(*cols)]


def _gd_specs(rows, rev_nb=None):
    def cidx(c):
        return c if rev_nb is None else rev_nb - 1 - c

    qk_tile = pl.BlockSpec((rows, HPS // 2 * HEAD), lambda c, g: (cidx(c), g))
    v_tile = pl.BlockSpec((rows, HPS * HEAD), lambda c, g: (cidx(c), g))
    gab_tile = pl.BlockSpec((rows, HEAD), lambda c, g: (cidx(c), 0))
    return qk_tile, v_tile, gab_tile


def _gdn_fwd(qn, kn, cv, gab, gz, alog, dtb, gain, consts):
    t = qn.shape[0]
    nc = t // CHUNK
    cb = _chunks_per_step(nc)
    rows = cb * CHUNK
    lmat, _, masks = consts
    qk_tile, v_tile, gab_tile = _gd_specs(rows)
    row128 = pl.BlockSpec((1, HEAD), lambda c, h: (0, 0))

    def body(q_ref, k_ref, v_ref, gab_ref, gz_ref, alog_ref, dtb_ref, gain_ref, l_ref, mask_ref,
             oraw_ref, og_ref, ssave_ref, tsave_ref, state):
        c = pl.program_id(0)
        g = pl.program_id(1)

        @pl.when(c == 0)
        def _():
            for hh in range(HPS):
                state[g * HPS + hh] = jnp.zeros((HEAD, HEAD), F32)

        alog = alog_ref[...]
        dtb = dtb_ref[...]
        gain_v = gain_ref[...]

        def one(i, carry):
            sl = pl.ds(pl.multiple_of(i * CHUNK, CHUNK), CHUNK)
            _, g_all, beta_all = _gd_gates(gab_ref[sl, :], alog, dtb)
            heads = [g * HPS + hh for hh in range(HPS)]
            lq = [slice(hh // 2 * HEAD, (hh // 2 + 1) * HEAD) for hh in range(HPS)]
            lv = [slice(hh * HEAD, (hh + 1) * HEAD) for hh in range(HPS)]
            chs = _gd_chunks([q_ref[sl, s] for s in lq], [k_ref[sl, s] for s in lq], [v_ref[sl, s] for s in lv],
                             g_all, beta_all, [(g, hh) for hh in range(HPS)], l_ref, mask_ref)
            s0 = [state[h] for h in heads]
            ws = _each(lambda ch, s: _dot(jnp.concatenate([ch["w"], ch["qg"]], axis=0), s), chs, s0)
            v_new = _each(lambda ch, x: ch["u"] - x[:CHUNK], chs, ws)
            o = _each(lambda ch, x, vn: x[CHUNK:] + _dot(ch["qk"], vn), chs, ws, v_new)
            s1 = _each(lambda ch, s, vn: s * ch["ge"] + _dot_tn(ch["kg"], vn), chs, s0, v_new)
            for hh in range(HPS):
                ssave_ref[i, hh] = s0[hh]
                tsave_ref[i, hh] = chs[hh]["tm"]
                state[heads[hh]] = s1[hh]
                oraw_ref[sl, lv[hh]] = o[hh]
                r = lax.rsqrt(jnp.mean(o[hh] * o[hh], axis=1, keepdims=True) + EPS)
                og_ref[sl, lv[hh]] = (o[hh] * r * gain_v * _silu(gz_ref[sl, lv[hh]])).astype(BF16)
            return carry

        lax.fori_loop(0, cb, one, 0, unroll=2)

    return pl.pallas_call(
        body, name="gdn_fwd", grid=(nc // cb, GD_HEADS // HPS),
        in_specs=[qk_tile, qk_tile, v_tile, gab_tile, _view_tile(gz, rows, HPS * HEAD), row128, row128, row128,
                  pl.BlockSpec(lmat.shape, lambda c, g: (0, 0)),
                  pl.BlockSpec(masks.shape, lambda c, g: (0, 0, 0))],
        out_specs=[v_tile, v_tile, pl.BlockSpec((cb, HPS, HEAD, HEAD), lambda c, g: (c, g, 0, 0)),
                   pl.BlockSpec((cb, HPS, CHUNK, CHUNK), lambda c, g: (c, g, 0, 0))],
        out_shape=[jax.ShapeDtypeStruct((t, GD_HEADS * HEAD), F32), jax.ShapeDtypeStruct((t, GD_HEADS * HEAD), BF16),
                   jax.ShapeDtypeStruct((nc, GD_HEADS, HEAD, HEAD), F32),
                   jax.ShapeDtypeStruct((nc, GD_HEADS, CHUNK, CHUNK), F32)],
        scratch_shapes=[pltpu.VMEM((GD_HEADS, HEAD, HEAD), F32)],
        compiler_params=_params(_ARB, _ARB))(qn, kn, cv, gab, gz[0], alog, dtb, gain, lmat, masks)


def _gdn_bwd(qn, kn, cv, gab, gz, alog, dtb, gain, oraw, ssave, tsave, dog, consts):
    t = qn.shape[0]
    nc = t // CHUNK
    cb = _chunks_per_step(nc)
    rows = cb * CHUNK
    nb = nc // cb
    lmat, lmat_t, masks = consts
    qk_tile, v_tile, gab_tile = _gd_specs(rows, nb)
    row128 = pl.BlockSpec((1, HEAD), lambda c, h: (0, 0))

    def body(q_ref, k_ref, v_ref, gab_ref, gz_ref, alog_ref, dtb_ref, gain_ref, oraw_ref, ssave_ref, tsave_ref, dog_ref,
             l_ref, lt_ref, mask_ref,
             dq_ref, dk_ref, dv_ref, dgab_ref, dgz_ref, small_ref, dstate):
        c = pl.program_id(0)
        g = pl.program_id(1)

        @pl.when(c == 0)
        def _():
            for hh in range(HPS):
                dstate[g * HPS + hh] = jnp.zeros((HEAD, HEAD), F32)

        @pl.when((c == 0) & (g == 0))
        def _():
            small_ref[...] = jnp.zeros_like(small_ref)

        alog = alog_ref[...]
        dtb = dtb_ref[...]
        gain_v = gain_ref[...]
        lane = lax.broadcasted_iota(jnp.int32, (1, HEAD), 1)
        last_row = (lax.broadcasted_iota(jnp.int32, (CHUNK, HEAD), 0) == CHUNK - 1).astype(F32)

        def one(j, carry):
            i = cb - 1 - j
            sl = pl.ds(pl.multiple_of(i * CHUNK, CHUNK), CHUNK)
            sp_arg, g_all, beta_all = _gd_gates(gab_ref[sl, :], alog, dtb)
            strict, eye = mask_ref[1], mask_ref[2]
            ltm = lt_ref[...]
            hs = range(HPS)
            heads = [g * HPS + hh for hh in hs]
            lq = [slice(hh // 2 * HEAD, (hh // 2 + 1) * HEAD) for hh in hs]
            lv = [slice(hh * HEAD, (hh + 1) * HEAD) for hh in hs]
            q = [q_ref[sl, s] for s in lq]
            k = [k_ref[sl, s] for s in lq]
            v = [v_ref[sl, s] for s in lv]
            gzv = [gz_ref[sl, s] for s in lv]
            chs = _gd_chunks(q, k, v, g_all, beta_all, [(g, hh) for hh in hs], l_ref, mask_ref,
                             tm=[tsave_ref[i, hh] for hh in hs])

            def col(name):
                return [ch[name] for ch in chs]

            def mul(x, y):
                return x * y

            tm, lm, eg, bb = col("tm"), col("lm"), col("eg"), col("bb")
            s0 = [ssave_ref[i, hh] for hh in hs]
            ds = [dstate[h] for h in heads]
            v_new = _each(lambda u, w, s: u - _dot(w, s), col("u"), col("w"), s0)

            o = [oraw_ref[sl, s] for s in lv]
            r = [lax.rsqrt(jnp.mean(x * x, axis=1, keepdims=True) + EPS) for x in o]
            on = _each(mul, o, r)
            dg_out = [dog_ref[sl, s] for s in lv]
            sgate = [_silu(x) for x in gzv]
            for hh in hs:
                dgz_ref[sl, lv[hh]] = (dg_out[hh] * on[hh] * gain_v * _dsilu(gzv[hh])).astype(BF16)
            small_ref[0:1, :] += sum(jnp.sum(d * s * n, axis=0, keepdims=True) for d, s, n in zip(dg_out, sgate, on))
            don = _each(lambda d, s: d * s * gain_v, dg_out, sgate)
            do = _each(lambda rr, dn, n: rr * (dn - n * jnp.mean(dn * n, axis=1, keepdims=True)), r, don, on)

            dv_new = _each(lambda a, d, b, s: _dot_tn(a, d) + _dot(b, s), col("qk"), do, col("kg"), ds)
            dqk = _each(_dot_nt, do, v_new)
            dkg = _each(_dot_nt, v_new, ds)
            dge = _each(lambda s, d: jnp.sum(_rowsum(s * d), axis=0, keepdims=True), s0, ds)
            both = _each(lambda d, dv: jnp.concatenate([d, dv], axis=0), do, dv_new)
            from_s = _each(_dot_nt, both, s0)
            dqg = [x[:CHUNK] for x in from_s]
            dw = [-x[CHUNK:] for x in from_s]
            ds_new = _each(lambda qg, w, bo, ge, s: _dot_tn(jnp.concatenate([qg, -w], axis=0), bo) + ge * s,
                           col("qg"), col("w"), both, col("ge"), ds)
            for hh in hs:
                dstate[heads[hh]] = ds_new[hh]

            side = _each(lambda dv, d: jnp.concatenate([dv, d], axis=1), dv_new, dw)
            back = _each(_dot_tn, tm, side)
            dvb = [x[:, :HEAD] for x in back]
            dkbg = [x[:, HEAD:] for x in back]
            dtm = _each(lambda sd, vb, kbg: _dot_nt(sd, jnp.concatenate([vb, kbg], axis=1)), side, col("vb"), col("kbg"))
            dtt = _each(_dot_nt, dtm, tm)
            da = _each(lambda t_, x: -_dot_tn(t_, x) * strict, tm, dtt)
            dal = _each(mul, da, lm)
            dqk_l = _each(mul, dqk, lm)
            stack = _each(lambda x, y: jnp.concatenate([x, y], axis=0), dal, dqk_l)
            on_k = _each(_dot, stack, k)
            dkb = _each(lambda x, y, e: x[:CHUNK] + y * e, on_k, dkbg, eg)
            dq = _each(lambda x, y, e: x[CHUNK:] + y * e, on_k, dqg, eg)
            dk = _each(lambda st, kb, qq, z, ekg, w_, b: _dot_tn(st, jnp.concatenate([kb, qq], axis=0)) + z * ekg + w_ * b,
                       stack, col("kb"), q, dkg, col("ekg"), dkb, bb)
            gmat = _each(lambda x, a, y, qk: x * a + y * qk, da, col("a"), dqk, col("qk"))
            t_kg = _each(lambda x, y: _rowsum(x * y), dkg, col("kg"))
            dgam = _each(lambda gm, x, qg, t_, y, kbg: (_rowsum(gm) - _row_to_col(jnp.sum(gm, axis=0, keepdims=True), eye)
                                                        + _rowsum(x * qg) - t_ + _rowsum(y * kbg)),
                         gmat, dqg, col("qg"), t_kg, dkbg, col("kbg"))
            dg_end = _each(lambda t_, e, ge: jnp.sum(t_, axis=0, keepdims=True) + e * ge[:, 0:1], t_kg, dge, col("ge"))
            dgam = _each(lambda x, e: x + last_row * e, dgam, dg_end)
            dbeta = _each(lambda x, kk, y, vv: _rowsum(x * kk) + _rowsum(y * vv), dkb, k, dvb, v)
            dg = _mx_each(ltm, dgam)

            for hh in hs:
                dv_ref[sl, lv[hh]] = dvb[hh] * bb[hh]
            fac_g = -jnp.exp(alog) * _sigmoid(sp_arg)
            fac_b = beta_all * (1.0 - beta_all)
            hot_g = [(lane == h).astype(F32) for h in heads]
            hot_b = [(lane == GD_HEADS + h).astype(F32) for h in heads]
            dga = _each(lambda x, hot: x * hot * fac_g, dg, hot_g)
            dgb = _each(lambda x, hot: x * hot * fac_b, dbeta, hot_b)
            small_ref[1:2, :] += sum(jnp.sum(x, axis=0, keepdims=True) for x in dga)
            small_ref[2:3, :] += sum(jnp.sum(x * hot * g_all, axis=0, keepdims=True) for x, hot in zip(dg, hot_g))
            for pair in range(HPS // 2):
                lqp = slice(pair * HEAD, (pair + 1) * HEAD)
                dq_ref[sl, lqp] = dq[2 * pair] + dq[2 * pair + 1]
                dk_ref[sl, lqp] = dk[2 * pair] + dk[2 * pair + 1]
            dgab_ref[sl, :] = sum(a + b for a, b in zip(dga, dgb))
            return carry

        lax.fori_loop(0, cb, one, 0, unroll=2)

    groups = GD_HEADS // HPS
    outs = [jax.ShapeDtypeStruct((t, 1024), F32), jax.ShapeDtypeStruct((t, 1024), F32),
            jax.ShapeDtypeStruct((t, 2048), F32), jax.ShapeDtypeStruct((t, groups * HEAD), F32),
            jax.ShapeDtypeStruct((t, 2048), BF16), jax.ShapeDtypeStruct((8, HEAD), F32)]
    return pl.pallas_call(
        body, name="gdn_bwd", grid=(nb, groups),
        in_specs=[qk_tile, qk_tile, v_tile, gab_tile, _view_tile(gz, rows, HPS * HEAD, lambda c: nb - 1 - c),
                  row128, row128, row128, v_tile,
                  pl.BlockSpec((cb, HPS, HEAD, HEAD), lambda c, g: (nb - 1 - c, g, 0, 0)),
                  pl.BlockSpec((cb, HPS, CHUNK, CHUNK), lambda c, g: (nb - 1 - c, g, 0, 0)), v_tile,
                  pl.BlockSpec(lmat.shape, lambda c, g: (0, 0)),
                  pl.BlockSpec(lmat_t.shape, lambda c, g: (0, 0)),
                  pl.BlockSpec(masks.shape, lambda c, g: (0, 0, 0))],
        out_specs=[qk_tile, qk_tile, v_tile, pl.BlockSpec((rows, HEAD), lambda c, g: (nb - 1 - c, g)), v_tile,
                   pl.BlockSpec((8, HEAD), lambda c, g: (0, 0))],
        out_shape=outs, scratch_shapes=[pltpu.VMEM((GD_HEADS, HEAD, HEAD), F32)],
        compiler_params=_params(_ARB, _ARB))(qn, kn, cv, gab, gz[0], alog, dtb, gain, oraw, ssave, tsave, dog,
                                             lmat, lmat_t, masks)


def _fold_groups(wide):
    t, width = wide.shape
    tr = _pick(t, 512, 8)

    def body(w_ref, o_ref):
        acc = w_ref[:, 0:HEAD]
        for j in range(1, width // HEAD):
            acc = acc + w_ref[:, j * HEAD:(j + 1) * HEAD]
        o_ref[...] = acc.astype(BF16)

    return pl.pallas_call(
        body, name="fold_gate_grads", grid=(t // tr,), in_specs=[_row_spec(tr, width)], out_specs=_row_spec(tr, HEAD),
        out_shape=jax.ShapeDtypeStruct((t, HEAD), BF16), compiler_params=_params(_PAR))(wide)


def _adam_math(w, g, m, v):
    m2 = ADAM_B1 * m + (1.0 - ADAM_B1) * g
    v2 = ADAM_B2 * v + (1.0 - ADAM_B2) * (g * g)
    m_hat = m2 / (1.0 - ADAM_B1 ** ADAM_STEP)
    v_hat = v2 / (1.0 - ADAM_B2 ** ADAM_STEP)
    delta = -ADAM_LR * (m_hat / (jnp.sqrt(v_hat) + ADAM_EPS) + ADAM_WD * w)
    return delta, m2, v2


def _adamw(w, g, m, v, name, after=None):
    r, c = w.shape
    tr = r
    for cand in range(8, r + 1, 8):
        if r % cand == 0 and cand * c * 4 <= (1 << 20):
            tr = cand
    if r % 8 != 0:
        tr = r

    def body(w_ref, g_ref, m_ref, v_ref, *rest):
        d_ref, m2_ref, v2_ref = rest[-3:]
        d, m2, v2 = _adam_math(w_ref[...], g_ref[...], m_ref[...], v_ref[...])
        d_ref[...] = d
        m2_ref[...] = m2
        v2_ref[...] = v2

    spec = pl.BlockSpec((tr, c), lambda i: (i, 0))
    extra = [] if after is None else [after]
    return pl.pallas_call(
        body, name=name, grid=(r // tr,), in_specs=[spec] * 4 + [_ANY] * len(extra), out_specs=[spec] * 3,
        out_shape=[jax.ShapeDtypeStruct((r, c), F32)] * 3, compiler_params=_params(_PAR))(w, g, m, v, *extra)


_ANY = pl.BlockSpec(memory_space=pl.ANY)


def _place():
    return lax.axis_index("x"), lax.axis_index("y"), lax.axis_index("c")


def _gather_weights(packs, nchs, name):
    n = len(packs)
    halves = [p.shape[0] // 2 for p in packs]
    base = [sum(nchs[:i]) for i in range(n)]
    total = sum(nchs)
    for p, h, k in zip(packs, halves, nchs):
        assert p.shape[0] == 2 * h and h % k == 0 and (h // k) % 16 == 0

    def body(*refs):
        p_refs, g_refs, (send_sems, recv_sems) = refs[:n], refs[n:2 * n], refs[2 * n:]
        x, y, c = _place()
        sibling = (x, y, 1 - c)
        chips = [(1 - x, y), (x, 1 - y), (1 - x, 1 - y)]
        chunks = [(a, q) for a in range(n) for q in range(nchs[a])]

        def rows_of(a, pc, q):
            ch = halves[a] // nchs[a]
            return pl.ds(pl.multiple_of(pc * halves[a] + q * ch, 16), ch)

        def piece(a, px, py, pc, q):
            return g_refs[a].at[2 * px + py, rows_of(a, pc, q), :]

        def copy(k, src, dst, to):
            return pltpu.make_async_remote_copy(src_ref=src, dst_ref=dst, send_sem=send_sems.at[k],
                                                recv_sem=recv_sems.at[k], device_id=to, device_id_type=MESH)

        def sem_of(j, a, q):
            return j * total + base[a] + q

        first = {(j, a, q): copy(sem_of(j, a, q), p_refs[a].at[rows_of(a, c, q), :], piece(a, x, y, c, q), (*chip, c))
                 for j, chip in enumerate(chips) for a, q in chunks}
        for a, q in chunks:
            for j in range(3):
                first[j, a, q].start()
        passed = {(j, a, q): copy(sem_of(3 + j, a, q), piece(a, *chip, c, q), piece(a, *chip, c, q), sibling)
                  for j, chip in enumerate(chips) for a, q in chunks}
        for a, q in chunks:
            for j, chip in enumerate(chips):
                copy(sem_of(j, a, q), p_refs[a].at[rows_of(a, c, q), :], piece(a, *chip, c, q), (*chip, c)).wait_recv()
                passed[j, a, q].start()
        for a, q in chunks:
            for j, chip in enumerate(chips):
                copy(sem_of(3 + j, a, q), piece(a, *chip, 1 - c, q), piece(a, *chip, 1 - c, q), sibling).wait_recv()
        for key in first:
            first[key].wait_send()
            passed[key].wait_send()

    return pl.pallas_call(
        body, name=name, out_shape=[jax.ShapeDtypeStruct((4,) + p.shape, p.dtype) for p in packs],
        in_specs=[_ANY] * n, out_specs=[_ANY] * n,
        scratch_shapes=[pltpu.SemaphoreType.DMA((6 * total,)), pltpu.SemaphoreType.DMA((6 * total,))])(*packs)


def _swap_with_sibling(arrs, nchs, lead, name, halves=False):
    n = len(arrs)
    jobs = []
    hs = [arr.shape[-2] // (2 if halves else 1) for arr in arrs]
    for a, (h, k) in enumerate(zip(hs, nchs)):
        assert h % k == 0 and (h // k) % 16 == 0
        for s in (range(lead) if lead else [None]):
            jobs += [(a, s, q * (h // k), h // k) for q in range(k)]

    def body(*refs):
        src, dst, (send_sems, recv_sems) = refs[:n], refs[n:2 * n], refs[2 * n:]
        x, y, c = _place()

        def at(ref, s, r0, rows):
            return ref.at[pl.ds(r0, rows), :] if s is None else ref.at[s, pl.ds(r0, rows), :]

        def src_rows(a, r0):
            return pl.multiple_of((1 - c) * hs[a] + r0, 16) if halves else r0

        copies = [pltpu.make_async_remote_copy(
            src_ref=at(src[a], s, src_rows(a, r0), rows), dst_ref=at(dst[a], s, r0, rows), send_sem=send_sems.at[k],
            recv_sem=recv_sems.at[k], device_id=(x, y, 1 - c), device_id_type=MESH)
            for k, (a, s, r0, rows) in enumerate(jobs)]
        for cp in copies:
            cp.start()
        for cp in copies:
            cp.wait()

    shapes = [jax.ShapeDtypeStruct(arr.shape[:-2] + (h, arr.shape[-1]), arr.dtype) for arr, h in zip(arrs, hs)]
    return pl.pallas_call(
        body, name=name, out_shape=shapes, in_specs=[_ANY] * n, out_specs=[_ANY] * n,
        scratch_shapes=[pltpu.SemaphoreType.DMA((len(jobs),)), pltpu.SemaphoreType.DMA((len(jobs),))])(*arrs)


def _add2(full, b, core, name):
    n, rows, w = b.shape
    tr = _pick(rows, 256, 16)
    nblk = rows // tr

    def body(c_ref, a_ref, b_ref, o_ref):
        o_ref[...] = (a_ref[...].astype(F32) + b_ref[...].astype(F32)).astype(BF16)

    spec = pl.BlockSpec((1, tr, w), lambda i, j, c_ref: (i, j, 0))
    grid_spec = pltpu.PrefetchScalarGridSpec(
        num_scalar_prefetch=1, grid=(n, nblk),
        in_specs=[pl.BlockSpec((1, tr, w), lambda i, j, c_ref: (i, c_ref[0] * nblk + j, 0)), spec], out_specs=spec)
    return pl.pallas_call(
        body, name=name, grid_spec=grid_spec, out_shape=jax.ShapeDtypeStruct(b.shape, BF16),
        compiler_params=_params(_PAR, _PAR))(core, full, b)


def _reduce_chips(partials, nchs, name):
    n = len(partials)
    jobs = []
    for a, (arr, k) in enumerate(zip(partials, nchs)):
        h = arr.shape[1]
        assert h % k == 0 and (h // k) % 16 == 0
        jobs += [(a, q * (h // k), h // k) for q in range(k)]

    def body(*refs):
        src, dst, (send_sems, recv_sems) = refs[:n], refs[n:2 * n], refs[2 * n:]
        x, y, c = _place()
        chips = [(1 - x, y), (x, 1 - y), (1 - x, 1 - y)]
        copies = [pltpu.make_async_remote_copy(
            src_ref=src[a].at[2 * px + py, pl.ds(r0, rows), :], dst_ref=dst[a].at[j, pl.ds(r0, rows), :],
            send_sem=send_sems.at[3 * k + j], recv_sem=recv_sems.at[3 * k + j],
            device_id=(px, py, c), device_id_type=MESH)
            for k, (a, r0, rows) in enumerate(jobs) for j, (px, py) in enumerate(chips)]
        for cp in copies:
            cp.start()
        for cp in copies:
            cp.wait()

    return pl.pallas_call(
        body, name=name,
        out_shape=[jax.ShapeDtypeStruct((3,) + p.shape[1:], p.dtype) for p in partials],
        in_specs=[_ANY] * n, out_specs=[_ANY] * n,
        scratch_shapes=[pltpu.SemaphoreType.DMA((3 * len(jobs),)), pltpu.SemaphoreType.DMA((3 * len(jobs),))])(*partials)


_HBM = pl.BlockSpec(memory_space=pltpu.HBM)
_SEM = pl.BlockSpec(memory_space=pltpu.SEMAPHORE)
_DATAFLOW = pltpu.SideEffectType.DATAFLOW_SIDE_EFFECTING


def _ici_jobs(srcs, nchs, kind):
    jobs = []
    for a, (arr, k) in enumerate(zip(srcs, nchs)):
        h = arr.shape[0] // 2 if kind == "gather" else arr.shape[1]
        assert h % k == 0 and (h // k) % 16 == 0
        jobs += [(a, h, q * (h // k), h // k) for q in range(k)]
    return jobs


def _ici_copies(src, land, send_sems, recv_sems, jobs, kind):
    x, y, c = _place()
    chips = [(1 - x, y), (x, 1 - y), (1 - x, 1 - y)]
    copies = []
    for k, (a, h, r0, rows) in enumerate(jobs):
        for j, (px, py) in enumerate(chips):
            if kind == "gather":
                at = pl.ds(pl.multiple_of(c * h + r0, 16), rows)
                s, d = src[a].at[at, :], land[a].at[2 * x + y, at, :]
            else:
                s, d = src[a].at[2 * px + py, pl.ds(r0, rows), :], land[a].at[j, pl.ds(r0, rows), :]
            copies.append(pltpu.make_async_remote_copy(
                src_ref=s, dst_ref=d, send_sem=send_sems.at[3 * k + j], recv_sem=recv_sems.at[3 * k + j],
                device_id=(px, py, c), device_id_type=MESH))
    return copies


def _ici_start(srcs, nchs, kind, name):
    n = len(srcs)
    jobs = _ici_jobs(srcs, nchs, kind)
    lead = (lambda s: (4,) + s.shape) if kind == "gather" else (lambda s: (3,) + s.shape[1:])
    lands = [lax.empty(lead(s), s.dtype) for s in srcs]

    def body(*refs):
        src, land = refs[:n], refs[n:2 * n]
        send_sems, recv_sems, token = refs[2 * n], refs[2 * n + 1], refs[-1]
        for cp in _ici_copies(src, land, send_sems, recv_sems, jobs, kind):
            cp.start()
        token[...] = jnp.zeros_like(token)

    hbm = [pltpu.HBM(a.shape, a.dtype) for a in srcs + lands]
    outs = pl.pallas_call(
        body, name=name,
        out_shape=[pltpu.SemaphoreType.DMA((3 * len(jobs),)), pltpu.SemaphoreType.DMA((3 * len(jobs),))] + hbm
        + [jax.ShapeDtypeStruct((8, 128), F32)],
        in_specs=[_HBM] * (2 * n), out_specs=[_SEM, _SEM] + [_HBM] * (2 * n) + [pl.BlockSpec(memory_space=pltpu.VMEM)],
        input_output_aliases={i: 2 + i for i in range(2 * n)},
        compiler_params=pltpu.CompilerParams(has_side_effects=_DATAFLOW),
    )(*[pltpu.with_memory_space_constraint(a, pltpu.HBM) for a in srcs + lands])
    return (outs[0], outs[1], list(outs[2:2 + n]), list(outs[2 + n:2 + 2 * n]), nchs, kind), outs[-1]


def _ici_wait(handle, after, name):
    send_sems, recv_sems, srcs, lands, nchs, kind = handle
    n = len(srcs)
    jobs = _ici_jobs(srcs, nchs, kind)

    def body(*refs):
        src, land = refs[:n], refs[n:2 * n]
        for cp in _ici_copies(src, land, refs[2 * n], refs[2 * n + 1], jobs, kind):
            cp.wait_send()
            cp.wait_recv()

    outs = pl.pallas_call(
        body, name=name, out_shape=[pltpu.HBM(a.shape, a.dtype) for a in srcs + lands],
        in_specs=[_HBM] * (2 * n) + [_SEM, _SEM, _ANY], out_specs=[_HBM] * (2 * n),
        input_output_aliases={i: i for i in range(2 * n)},
        compiler_params=pltpu.CompilerParams(has_side_effects=_DATAFLOW),
    )(*srcs, *lands, send_sems, recv_sems, after)
    return list(outs[:n]), list(outs[n:])


def _pass_to_sibling(gathered, nchs, name):
    n = len(gathered)
    jobs = _ici_jobs([jax.ShapeDtypeStruct(g.shape[1:], g.dtype) for g in gathered], nchs, "gather")

    def body(*refs):
        src, dst, (send_sems, recv_sems) = refs[:n], refs[n:2 * n], refs[2 * n:]
        x, y, c = _place()
        slots = [2 * (1 - x) + y, 2 * x + (1 - y), 2 * (1 - x) + (1 - y)]

        def copy(k, j, pc):
            a, h, r0, rows = jobs[k]
            at = pl.ds(pl.multiple_of(pc * h + r0, 16), rows)
            return pltpu.make_async_remote_copy(
                src_ref=src[a].at[slots[j], at, :], dst_ref=dst[a].at[slots[j], at, :], send_sem=send_sems.at[3 * k + j],
                recv_sem=recv_sems.at[3 * k + j], device_id=(x, y, 1 - c), device_id_type=MESH)

        pairs = [(k, j) for k in range(len(jobs)) for j in range(3)]
        for k, j in pairs:
            copy(k, j, c).start()
        for k, j in pairs:
            copy(k, j, c).wait_send()
            copy(k, j, 1 - c).wait_recv()

    return pl.pallas_call(
        body, name=name, out_shape=[jax.ShapeDtypeStruct(g.shape, g.dtype) for g in gathered],
        in_specs=[_ANY] * n, out_specs=[_ANY] * n, input_output_aliases={i: i for i in range(n)},
        scratch_shapes=[pltpu.SemaphoreType.DMA((3 * len(jobs),)), pltpu.SemaphoreType.DMA((3 * len(jobs),))])(*gathered)


def _add4(own, got, name):
    rows, w = own.shape
    tr = _pick(rows, 128, 16)

    def body(a_ref, b_ref, o_ref):
        o_ref[...] = ((a_ref[...].astype(F32) + b_ref[0].astype(F32)) + b_ref[1].astype(F32)) + b_ref[2].astype(F32)

    return pl.pallas_call(
        body, name=name, grid=(rows // tr,),
        in_specs=[pl.BlockSpec((tr, w), lambda i: (i, 0)), pl.BlockSpec((3, tr, w), lambda i: (0, i, 0))],
        out_specs=pl.BlockSpec((tr, w), lambda i: (i, 0)), out_shape=jax.ShapeDtypeStruct((rows, w), F32),
        compiler_params=_params(_PAR))(own, got)


def _small_sync(gs, ws, ms, vs):
    rows = gs.shape[0]
    vmem = pl.BlockSpec(memory_space=pltpu.VMEM)

    def body(g_ref, w_ref, m_ref, v_ref, sum_ref, d_ref, m2_ref, v2_ref, buf, send_sems, recv_sems):
        x, y, c = _place()
        me = 4 * x + 2 * y + c
        buf[me] = g_ref[...]
        copies = []
        for k in range(1, 8):
            peer = (x ^ (k >> 2), y ^ ((k >> 1) & 1), c ^ (k & 1))
            copies.append(pltpu.make_async_remote_copy(
                src_ref=g_ref, dst_ref=buf.at[me], send_sem=send_sems.at[k - 1], recv_sem=recv_sems.at[k - 1],
                device_id=peer, device_id_type=MESH))
        for cp in copies:
            cp.start()
        for cp in copies:
            cp.wait()
        total = buf[0]
        for i in range(1, 8):
            total = total + buf[i]
        sum_ref[...] = total
        d, m2, v2 = _adam_math(w_ref[...], total, m_ref[...], v_ref[...])
        d_ref[...] = d
        m2_ref[...] = m2
        v2_ref[...] = v2

    shape = jax.ShapeDtypeStruct((rows, 128), F32)
    return pl.pallas_call(
        body, name="small_sync", out_shape=[shape] * 4, in_specs=[vmem] * 4, out_specs=[vmem] * 4,
        scratch_shapes=[pltpu.VMEM((8, rows, 128), F32), pltpu.SemaphoreType.DMA((7,)),
                        pltpu.SemaphoreType.DMA((7,))])(gs, ws, ms, vs)


_GROUPS = {
    "ffn1": dict(cols=("ffn1_w_in", 1408), rows=(("ffn1_w_out", 704, 704),), chunks=(8, 2)),
    "ffn2": dict(cols=("ffn2_w_in", 1408), rows=(("ffn2_w_out", 704, 704),), chunks=(8, 2)),
    "mixer": dict(cols=("w_in", 3080), chunks=(8, 4),
                  rows=(("w_branch_hgrn", 256, 256), ("w_branch_gdn", 512, 512), ("w_out", 256, 256),
                        ("gdn_conv_w", CONV_K, 128))),
}
_BIG_NAMES = tuple(n for g in _GROUPS.values() for n in (g["cols"][0],) + tuple(r[0] for r in g["rows"]))


def _group_names(group):
    return (group["cols"][0],) + tuple(r[0] for r in group["rows"])


def _pack(parts, lead, group):
    ax = len(lead)
    rows = []
    for n, r, padded in group["rows"]:
        p = parts[n]
        if padded != r:
            p = jnp.tile(p, (1,) * ax + (padded // r, 1))
        rows.append(p)
    return [parts[group["cols"][0]], rows[0] if len(rows) == 1 else jnp.concatenate(rows, axis=ax)]


def _unpack(cols, rows, group):
    out, off = {group["cols"][0]: cols}, 0
    for n, r, padded in group["rows"]:
        out[n] = rows[..., off:off + r, :]
        off += padded
    return out


def _is_col_sharded(name):
    return name in ("ffn1_w_in", "ffn2_w_in", "w_in", "gdn_conv_w")


def _full_from_shards(name, g):
    if _is_col_sharded(name):
        return jnp.transpose(g, (1, 0, 2)).reshape(g.shape[1], -1)
    return g.reshape(-1, g.shape[2])


def _shards_from_full(name, full):
    if _is_col_sharded(name):
        return jnp.transpose(full.reshape(full.shape[0], 4, -1), (1, 0, 2))
    return full.reshape(4, -1, full.shape[1])


_SMALL = (("ffn1_norm", 8), ("mix_norm", 8), ("hgrn_lb_logits", 16), ("hgrn_out_norm", 8), ("gdn_a_log", 8),
          ("gdn_dt_bias", 8), ("gdn_out_norm", 8), ("ffn2_norm", 8), ("final_norm", 8), ("loss", 8))
_SMALL_ROWS = sum(r for _, r in _SMALL)


def _pack_small(parts):
    out = []
    for name, rows in _SMALL:
        p = parts[name].reshape(-1).astype(F32)
        if p.shape[0] <= 128:
            if p.shape[0] < 128:
                p = jnp.concatenate([p, jnp.zeros((128 - p.shape[0],), F32)])
            p = jnp.broadcast_to(p.reshape(1, 128), (rows, 128))
        out.append(p.reshape(rows, 128))
    return jnp.concatenate(out, axis=0)


def _unpack_small(packed, shapes):
    out, off = {}, 0
    for name, rows in _SMALL:
        n = int(np.prod(shapes[name]))
        out[name] = packed[off:off + rows].reshape(-1)[:n].reshape(shapes[name])
        off += rows
    return out


def _ffn_fwd(x, gain, w_in, w_out, tag):
    n = _rmsnorm_fwd(x, gain, tag + "_norm")
    a, b, hm = _ffn_in_act(n, w_in, tag + "_in")
    out = _mm(hm, w_out, alpha=0.5, res=x, name=tag + "_out")
    return out, (n, a, b)


def _ffn_bwd(x, gain, w_in, w_out, saved, dout, dout_bf, tag):
    n, a, b = saved
    da, db, hm = _ffn_dact(dout_bf, w_out, a, b, tag + "_dact")
    dw_out = _mm(hm, dout_bf, ta=True, alpha=0.5, out_dtype=BF16, name=tag + "_dwout")
    dwa = _mm(n, da, ta=True, out_dtype=BF16, name=tag + "_dwin_a")
    dwb = _mm(n, db, ta=True, out_dtype=BF16, name=tag + "_dwin_b")
    half = D_FF // 2
    dw_in = jnp.stack([dwa[:, :half], dwa[:, half:], dwb[:, :half], dwb[:, half:]])
    dn = _mm(da, w_in, tb=True, name=tag + "_dnorm_a")
    dn = _mm(db, w_in, tb=True, res=dn, b_from=D_FF, name=tag + "_dnorm_b")
    dx, dx_bf, dgain = _rmsnorm_bwd(x, gain, dn, dout, tag + "_dx")
    return dx, dx_bf, dgain, dw_in, dw_out


def _pad_lanes(v):
    return jnp.concatenate([v.reshape(1, -1), jnp.zeros((1, HEAD - v.size), F32)], axis=1)


def _local_step(x, tgt, small, exchange):
    hg_c = _hg_consts()
    gd_c = _gd_consts()
    alog = _pad_lanes(small["gdn_a_log"])
    dtb = _pad_lanes(small["gdn_dt_bias"])
    logits = small["hgrn_lb_logits"]
    hg_gain = small["hgrn_out_norm"].reshape(1, HEAD)
    gd_gain = small["gdn_out_norm"].reshape(1, HEAD)
    g1, gm, g2 = small["ffn1_norm"].reshape(1, -1), small["mix_norm"].reshape(1, -1), small["ffn2_norm"].reshape(1, -1)
    gf = small["final_norm"].reshape(1, -1)
    qscale = HEAD ** -0.5

    w1 = exchange.weights("ffn1")
    started = exchange.prefetch("mixer")
    h1, ffn1_saved = _ffn_fwd(x, g1 + started, w1["ffn1_w_in"], w1["ffn1_w_out"], "ffn1")
    u = _rmsnorm_fwd(h1, gm, "mix_norm")
    w = exchange.weights("mixer", after=u)
    started = exchange.prefetch("ffn2")
    seg, off = {}, 0
    for name, size in zip(IN_NAMES, IN_SIZES):
        seg[name] = w["w_in"][:, off:off + size]
        off += size
    w_gab = jnp.concatenate([seg["ga"], seg["gb"], jnp.zeros((D_MODEL, HEAD - 32), BF16)], axis=1)
    big_segs = [n for n in IN_NAMES if n not in ("ga", "gb")]
    conv8 = jnp.concatenate([w["gdn_conv_w"].astype(F32), jnp.zeros((8 - CONV_K, 4096), F32)], axis=0)
    conv_q, conv_k, conv_v = conv8[:, :1024], conv8[:, 1024:2048], conv8[:, 2048:]
    w_main = jnp.concatenate([seg[n] for n in big_segs], axis=1)
    proj = _mm(u, w_main, name="proj")
    pr, off = {}, 0
    for n in big_segs:
        pr[n] = _view(proj, off, seg[n].shape[1])
        off += seg[n].shape[1]
    gab = _mm(u, w_gab, name="proj_gab")
    oh_raw, oh, s_h = _hgrn_fwd(pr["hq"], pr["hf"], pr["hi"], pr["hg"], logits, hg_gain + started, hg_c)
    qn = _conv_fwd(pr["gq"], conv_q, qscale, "conv_q")
    kn = _conv_fwd(pr["gk"], conv_k, 1.0, "conv_k")
    cv = _conv_fwd(pr["gv"], conv_v, None, "conv_v")
    og_raw, og, s_g, t_g = _gdn_fwd(qn, kn, cv, gab, pr["gz"], alog, dtb, gd_gain, gd_c)
    yh = _mm(oh, w["w_branch_hgrn"], name="branch_h")
    yg = _mm(og, w["w_branch_gdn"], name="branch_g")
    ym = _merge_fwd(yh, yg, pr["gate_h"], pr["gate_g"])
    h2 = _mm(ym, w["w_out"], res=h1, name="mix_out")
    w2 = exchange.weights("ffn2", after=h2)
    h3, ffn2_saved = _ffn_fwd(h2, g2, w2["ffn2_w_in"], w2["ffn2_w_out"], "ffn2")
    loss, dh3, dh3_bf, d_gf = _final_loss(h3, gf, tgt)

    dh2, dh2_bf, d_g2, d_f2in, d_f2out = _ffn_bwd(h2, g2, w2["ffn2_w_in"], w2["ffn2_w_out"], ffn2_saved, dh3, dh3_bf,
                                                  "ffn2")
    started = exchange.reduce("ffn2", {"ffn2_w_in": d_f2in, "ffn2_w_out": d_f2out}, behind=True)
    dym =_mm(dh2_bf, w["w_out"], tb=True, name="d_merge")
    d_wout = _mm(ym, dh2_bf, ta=True, out_dtype=BF16, name="d_w_out")
    dyh, dyg, d_gate_h, d_gate_g = _merge_bwd(dym, yh, yg, pr["gate_h"], pr["gate_g"])
    d_wbh = _mm(oh, dyh, ta=True, out_dtype=BF16, name="d_w_branch_h")
    d_wbg = _mm(og, dyg, ta=True, out_dtype=BF16, name="d_w_branch_g")
    doh = _mm(dyh, w["w_branch_hgrn"], tb=True, name="d_oh")
    dog = _mm(dyg, w["w_branch_gdn"], tb=True, name="d_og")
    d_hq, d_hf, d_hi, d_hg, d_hg_gain, d_lb0 = _hgrn_bwd(pr["hq"], pr["hf"], pr["hi"], pr["hg"], logits,
                                                        hg_gain + started, oh_raw, s_h, doh, hg_c)
    d_qn, d_kn, d_cv, d_gab_wide, d_gz, gd_small = _gdn_bwd(qn, kn, cv, gab, pr["gz"], alog, dtb, gd_gain, og_raw,
                                                            s_g, t_g, dog, gd_c)
    d_gab = _fold_groups(d_gab_wide)
    dc_q, dwc_q = _conv_bwd_a(pr["gq"], conv_q, d_qn, qscale, "dconv_q")
    dc_k, dwc_k = _conv_bwd_a(pr["gk"], conv_k, d_kn, 1.0, "dconv_k")
    dc_v, dwc_v = _conv_bwd_a(pr["gv"], conv_v, d_cv, None, "dconv_v")
    d_gq = _conv_bwd_b(dc_q, conv_q, "dconvx_q")
    d_gk = _conv_bwd_b(dc_k, conv_k, "dconvx_k")
    d_gv = _conv_bwd_b(dc_v, conv_v, "dconvx_v")
    dpr = {"hq": d_hq, "hf": d_hf, "hi": d_hi, "hg": d_hg, "gq": d_gq, "gk": d_gk, "gv": d_gv, "gz": d_gz,
           "gate_h": d_gate_h, "gate_g": d_gate_g}
    dproj = jnp.concatenate([dpr[n] for n in big_segs], axis=1)
    du = _mm(d_gab, w_gab, tb=True, name="du_gab")
    du = _mm(dproj, w_main, tb=True, res=du, name="du")
    d_wmain = _mm(u, dproj, ta=True, out_dtype=BF16, name="dw_main")
    d_wgab = _mm(u, d_gab, ta=True, out_dtype=BF16, name="dw_gab")
    cut = IN_WIDTH // 4
    d_win = jnp.stack([d_wmain[:, :cut], d_wmain[:, cut:2 * cut],
                       jnp.concatenate([d_wmain[:, 2 * cut:8192], d_wgab[:, :32], d_wmain[:, 8192:3 * cut - 32]], axis=1),
                       d_wmain[:, 3 * cut - 32:]])
    d_conv = jnp.concatenate([dwc_q[:CONV_K], dwc_k[:CONV_K], dwc_v[:CONV_K]], axis=1).astype(BF16)
    started = exchange.reduce("mixer", {"w_in": d_win, "gdn_conv_w": d_conv, "w_branch_hgrn": d_wbh,
                                        "w_branch_gdn": d_wbg, "w_out": d_wout}, behind=True)
    dh1, dh1_bf, d_gm = _rmsnorm_bwd(h1, gm + started, du, dh2, "mix_dnorm")
    dx, _, d_g1, d_f1in, d_f1out = _ffn_bwd(x, g1, w1["ffn1_w_in"], w1["ffn1_w_out"], ffn1_saved, dh1, dh1_bf, "ffn1")
    exchange.reduce("ffn1", {"ffn1_w_in": d_f1in, "ffn1_w_out": d_f1out}, behind=True)
    d_lb0 = d_lb0.reshape(1, -1)
    sm = {"ffn1_norm": d_g1, "mix_norm": d_gm, "hgrn_lb_logits": jnp.concatenate([d_lb0, -d_lb0], axis=0),
          "hgrn_out_norm": d_hg_gain, "gdn_a_log": gd_small[2, :16], "gdn_dt_bias": gd_small[1, :16],
          "gdn_out_norm": gd_small[0], "ffn2_norm": d_g2, "final_norm": d_gf, "loss": loss[0, :1]}
    return dx, sm


class _Exchange:
    def __init__(self, wts):
        self.wts = wts
        xi, yi, ci = _place()
        self.chip = 2 * xi + yi
        self.south = ci == 0
        self.core = ci.reshape(1).astype(jnp.int32)
        self.mine = {}
        self.coming = {}
        self.going = {}

    def _packs(self, tag):
        group = _GROUPS[tag]
        return _pack({n: self.wts[n][0].astype(BF16) for n in _group_names(group)}, (), group)

    def prefetch(self, tag):
        packs = self._packs(tag)
        handle, token = _ici_start(packs, _GROUPS[tag]["chunks"], "gather", "gather_start_" + tag)
        self.coming[tag] = handle
        return token[0:1, 0:1]

    def weights(self, tag, after=None):
        group = _GROUPS[tag]
        if tag in self.coming:
            packs, halves = _ici_wait(self.coming.pop(tag), after, "gather_wait_" + tag)
            others = _pass_to_sibling(halves, group["chunks"], "gather_pass_" + tag)
        else:
            packs = self._packs(tag)
            others = _gather_weights(packs, group["chunks"], "gather_" + tag)
        whole = [lax.dynamic_update_index_in_dim(g, p, self.chip, 0) for g, p in zip(others, packs)]
        gathered = _unpack(*whole, group)
        return {n: _full_from_shards(n, gathered[n]) for n in _group_names(group)}

    def reduce(self, tag, grads, behind=False):
        group = _GROUPS[tag]
        shards = {n: (grads[n] if grads[n].ndim == 3 else _shards_from_full(n, grads[n])) for n in _group_names(group)}
        gpacks = _pack(shards, (4,), group)
        got = _swap_with_sibling(gpacks, group["chunks"], 4, "reduce_pair_" + tag, halves=True)
        sums = [_add2(a, b, self.core, "add_pair_%s_%d" % (tag, i)) for i, (a, b) in enumerate(zip(gpacks, got))]
        if behind:
            handle, token = _ici_start(sums, group["chunks"], "reduce", "reduce_start_" + tag)
            self.going[tag] = handle
            self.token = token
            return token[0:1, 0:1]
        self._add_chips(tag, sums, _reduce_chips(sums, group["chunks"], "reduce_chips_" + tag))
        return None

    def _add_chips(self, tag, sums, from_chips):
        self.mine[tag] = [_add4(lax.dynamic_index_in_dim(s, self.chip, axis=0, keepdims=False), f,
                                "add_chips_%s_%d" % (tag, i)) for i, (s, f) in enumerate(zip(sums, from_chips))]

    def finish(self, tags, after):
        for tag in tags:
            if tag in self.going:
                self._add_chips(tag, *_ici_wait(self.going.pop(tag), after, "reduce_wait_" + tag))
        mine = [a for t in tags for a in self.mine[t]]
        nchs = [k for t in tags for k in _GROUPS[t]["chunks"]]
        theirs = _swap_with_sibling(mine, nchs, 0, "share_pair_" + tags[0])
        whole = [jnp.concatenate([jnp.where(self.south, a, b), jnp.where(self.south, b, a)], axis=0)
                 for a, b in zip(mine, theirs)]
        reduced = {}
        for i, t in enumerate(tags):
            reduced.update(_unpack(whole[2 * i], whole[2 * i + 1], _GROUPS[t]))
        return reduced


_WEIGHTS = ("ffn1_norm", "ffn1_w_in", "ffn1_w_out", "mix_norm", "w_in", "hgrn_lb_logits", "hgrn_out_norm",
            "gdn_conv_w", "gdn_a_log", "gdn_dt_bias", "gdn_out_norm", "w_branch_hgrn", "w_branch_gdn", "w_out",
            "ffn2_norm", "ffn2_w_in", "ffn2_w_out", "final_norm")


def kernel(x, ffn1_norm, ffn1_w_in, ffn1_w_out, mix_norm, w_in, hgrn_lb_logits, hgrn_out_norm, gdn_conv_w, gdn_a_log, gdn_dt_bias, gdn_out_norm, w_branch_hgrn, w_branch_gdn, w_out, ffn2_norm, ffn2_w_in, ffn2_w_out, final_norm, loss_target, m_ffn1_norm, m_ffn1_w_in, m_ffn1_w_out, m_mix_norm, m_w_in, m_hgrn_lb_logits, m_hgrn_out_norm, m_gdn_conv_w, m_gdn_a_log, m_gdn_dt_bias, m_gdn_out_norm, m_w_branch_hgrn, m_w_branch_gdn, m_w_out, m_ffn2_norm, m_ffn2_w_in, m_ffn2_w_out, m_final_norm, v_ffn1_norm, v_ffn1_w_in, v_ffn1_w_out, v_mix_norm, v_w_in, v_hgrn_lb_logits, v_hgrn_out_norm, v_gdn_conv_w, v_gdn_a_log, v_gdn_dt_bias, v_gdn_out_norm, v_w_branch_hgrn, v_w_branch_gdn, v_w_out, v_ffn2_norm, v_ffn2_w_in, v_ffn2_w_out, v_final_norm):
    args = dict(locals())
    wts = {n: args[n] for n in _WEIGHTS}
    moms = {n: args["m_" + n] for n in _WEIGHTS}
    vars_ = {n: args["v_" + n] for n in _WEIGHTS}

    small = {n: wts[n].astype(F32) for n in _WEIGHTS if n not in _BIG_NAMES}
    exchange = _Exchange(wts)
    dx, small_grads = _local_step(x[0], loss_target[0], small, exchange)

    out_g, out_d, out_m, out_v = {}, {}, {}, {}

    def update(tags, reduced, after):
        for t in tags:
            for n in _group_names(_GROUPS[t]):
                shape = wts[n].shape
                w2 = wts[n].reshape(shape[-2], shape[-1])
                g2 = reduced[n]
                d, m2, v2 = _adamw(w2, g2, moms[n].reshape(w2.shape), vars_[n].reshape(w2.shape), "adamw_" + n, after)
                out_g[n], out_d[n], out_m[n], out_v[n] = (g2.reshape(shape), d.reshape(shape), m2.reshape(shape),
                                                          v2.reshape(shape))
                after = v2
        return after

    done = update(("ffn2", "mixer"), exchange.finish(("ffn2", "mixer"), after=dx), exchange.token)
    update(("ffn1",), exchange.finish(("ffn1",), after=done), None)

    small_names = [n for n, _ in _SMALL]
    zero = jnp.zeros((1,), F32)
    shapes = {n: (wts[n].shape if n != "loss" else (1,)) for n in small_names}
    sums, sd, sm_, sv = _small_sync(
        _pack_small(small_grads),
        _pack_small({n: (wts[n] if n != "loss" else zero) for n in small_names}),
        _pack_small({n: (moms[n] if n != "loss" else zero) for n in small_names}),
        _pack_small({n: (vars_[n] if n != "loss" else zero) for n in small_names}))
    sg_u, sd_u, sm_u, sv_u = (_unpack_small(p, shapes) for p in (sums, sd, sm_, sv))
    for n in small_names:
        if n != "loss":
            out_g[n], out_d[n], out_m[n], out_v[n] = sg_u[n], sd_u[n], sm_u[n], sv_u[n]
    loss = sg_u["loss"].reshape(())

    return (loss, dx[None], *[out_g[n] for n in _WEIGHTS], *[out_d[n] for n in _WEIGHTS],
            *[out_m[n] for n in _WEIGHTS], *[out_v[n] for n in _WEIGHTS])
```

```python
import numpy as np

import jax
import jax.numpy as jnp
from jax import lax
from jax.experimental import pallas as pl
from jax.experimental.pallas import tpu as pltpu

F32 = jnp.float32
BF16 = jnp.bfloat16

D_MODEL = 1024
D_FF = 2816
CHUNK = 64
HEAD = 128
HG_HEADS = 8
GD_HEADS = 16
HPS = 8
COMM_CHUNKS = 9
MM_TM = 1408
MM_TN = 1024
MM_TK = 1536
VMEM_LIMIT = 48 * 1024 * 1024
EPS = 1e-6
CONV_K = 4
IN_NAMES = ("hq", "hf", "hi", "hg", "gq", "gk", "gv", "ga", "gb", "gz", "gate_h", "gate_g")
IN_SIZES = (1024, 1024, 1024, 1024, 1024, 1024, 2048, 16, 16, 2048, 1024, 1024)
IN_WIDTH = sum(IN_SIZES)

ADAM_LR = 0.001
ADAM_B1 = 0.9
ADAM_B2 = 0.999
ADAM_EPS = 1e-08
ADAM_WD = 0.01
ADAM_STEP = 10

MESH = pl.DeviceIdType.MESH
_ARB = "arbitrary"
_PAR = "parallel"


def _bf(x):
    return x.astype(BF16)


def _dot(a, b):
    return jnp.dot(_bf(a), _bf(b), preferred_element_type=F32)


def _dot_nt(a, b):
    return lax.dot_general(_bf(a), _bf(b), (((1,), (1,)), ((), ())), preferred_element_type=F32)


def _dot_tn(a, b):
    return lax.dot_general(_bf(a), _bf(b), (((0,), (0,)), ((), ())), preferred_element_type=F32)


def _split3(x):
    hi = _bf(x)
    r = x - hi.astype(F32)
    mid = _bf(r)
    lo = _bf(r - mid.astype(F32))
    return hi, mid, lo


def _dot_mx(m, x):
    hi, mid, lo = _split3(x)
    return (jnp.dot(m, hi, preferred_element_type=F32) + jnp.dot(m, mid, preferred_element_type=F32)
            + jnp.dot(m, lo, preferred_element_type=F32))


def _dot_xm(x, m):
    hi, mid, lo = _split3(x)
    return (jnp.dot(hi, m, preferred_element_type=F32) + jnp.dot(mid, m, preferred_element_type=F32)
            + jnp.dot(lo, m, preferred_element_type=F32))


def _dot_hp(a, b):
    ah = _bf(a)
    al = _bf(a - ah.astype(F32))
    bh = _bf(b)
    bl = _bf(b - bh.astype(F32))
    return (jnp.dot(ah, bh, preferred_element_type=F32) + jnp.dot(ah, bl, preferred_element_type=F32)
            + jnp.dot(al, bh, preferred_element_type=F32))


def _sigmoid(x):
    return jax.nn.sigmoid(x)


def _silu(x):
    return x * _sigmoid(x)


def _dsilu(x):
    s = _sigmoid(x)
    return s * (1.0 + x * (1.0 - s))


def _softplus(x):
    return jnp.maximum(x, 0.0) + jnp.log(1.0 + jnp.exp(-jnp.abs(x)))


def _rowsum(x):
    return jnp.sum(x, axis=1, keepdims=True)


def _col_to_row(col, eye):
    return jnp.sum(eye * col, axis=0, keepdims=True)


def _row_to_col(row, eye):
    return jnp.sum(eye * row, axis=1, keepdims=True)


def _pick(dim, pref, unit=128):
    if dim <= pref:
        return dim
    t = pref
    while t >= unit:
        if dim % t == 0:
            return t
        t -= unit
    return dim


def _params(*sem):
    return pltpu.CompilerParams(dimension_semantics=tuple(sem), vmem_limit_bytes=VMEM_LIMIT)


def _mm(a, b, *, ta=False, tb=False, alpha=1.0, res=None, out_dtype=F32, name="mm", b_from=0):
    m = a.shape[1] if ta else a.shape[0]
    k = a.shape[0] if ta else a.shape[1]
    n = b.shape[0] if tb else b.shape[1]
    assert b_from + k <= (b.shape[1] if tb else b.shape[0])
    tm, tn, tk = _pick(m, MM_TM), _pick(n, MM_TN), _pick(k, MM_TK)
    if tn < MM_TN < n and n % MM_TM == 0:
        tn = MM_TM
    nk = k // tk
    assert b_from % tk == 0
    b0 = b_from // tk
    a_spec = pl.BlockSpec((tk, tm), lambda i, j, l: (l, i)) if ta else pl.BlockSpec((tm, tk), lambda i, j, l: (i, l))
    b_spec = (pl.BlockSpec((tn, tk), lambda i, j, l: (j, b0 + l)) if tb
              else pl.BlockSpec((tk, tn), lambda i, j, l: (b0 + l, j)))
    o_spec = pl.BlockSpec((tm, tn), lambda i, j, l: (i, j))
    dims = (((0 if ta else 1,), (1 if tb else 0,)), ((), ()))
    has_res = res is not None

    def finish(r, r_ref, o_ref):
        if alpha != 1.0:
            r = r * alpha
        if has_res:
            r = r + r_ref[...]
        o_ref[...] = r.astype(out_dtype)

    def body(*refs):
        a_ref, b_ref = refs[0], refs[1]
        r_ref = refs[2] if has_res else None
        o_ref = refs[3] if has_res else refs[2]
        part = lax.dot_general(_bf(a_ref[...]), _bf(b_ref[...]), dims, preferred_element_type=F32)
        if nk == 1:
            finish(part, r_ref, o_ref)
            return
        acc = refs[-1]
        step = pl.program_id(2)

        @pl.when(step == 0)
        def _():
            acc[...] = part

        @pl.when(step != 0)
        def _():
            acc[...] += part

        @pl.when(step == nk - 1)
        def _():
            finish(acc[...], r_ref, o_ref)

    ins = [a, b] + ([res] if has_res else [])
    in_specs = [a_spec, b_spec] + ([o_spec] if has_res else [])
    return pl.pallas_call(
        body, name=name, grid=(m // tm, n // tn, nk), in_specs=in_specs, out_specs=o_spec,
        out_shape=jax.ShapeDtypeStruct((m, n), out_dtype),
        scratch_shapes=[pltpu.VMEM((tm, tn), F32)] if nk > 1 else [],
        compiler_params=_params(_PAR, _PAR, _ARB))(*ins)


def _row_spec(tr, w):
    return pl.BlockSpec((tr, w), lambda i: (i, 0))


def _full_spec(shape):
    return pl.BlockSpec(shape, lambda i: tuple(0 for _ in shape))


def _view(arr, off, width):
    return arr, off, width


def _view_rows(view, tr):
    _, off, width = view
    assert off % width == 0
    return pl.BlockSpec((tr, width), lambda i: (i, off // width))


def _view_tile(view, rows, bw, cidx=lambda c: c):
    _, off, width = view
    assert off % bw == 0 and width % bw == 0
    return pl.BlockSpec((rows, bw), lambda c, g: (cidx(c), off // bw + g))


def _rmsnorm_fwd(x, g, name):
    t, d = x.shape
    tr = _pick(t, 256, 8)

    def body(x_ref, g_ref, o_ref):
        xv = x_ref[...]
        r = lax.rsqrt(jnp.mean(xv * xv, axis=1, keepdims=True) + EPS)
        o_ref[...] = (xv * r * g_ref[...]).astype(BF16)

    return pl.pallas_call(
        body, name=name, grid=(t // tr,), in_specs=[_row_spec(tr, d), _full_spec((1, d))],
        out_specs=_row_spec(tr, d), out_shape=jax.ShapeDtypeStruct((t, d), BF16),
        compiler_params=_params(_PAR))(x, g)


def _rmsnorm_bwd(x, g, dn, res, name):
    t, d = x.shape
    tr = _pick(t, 256, 8)

    def body(x_ref, g_ref, dn_ref, r_ref, dx_ref, dxb_ref, dg_ref):
        @pl.when(pl.program_id(0) == 0)
        def _():
            dg_ref[...] = jnp.zeros_like(dg_ref)

        xv = x_ref[...]
        r = lax.rsqrt(jnp.mean(xv * xv, axis=1, keepdims=True) + EPS)
        xh = xv * r
        dy = dn_ref[...]
        dg_ref[...] += jnp.sum(dy * xh, axis=0, keepdims=True)
        dxh = dy * g_ref[...]
        dx = r_ref[...] + r * (dxh - xh * jnp.mean(dxh * xh, axis=1, keepdims=True))
        dx_ref[...] = dx
        dxb_ref[...] = dx.astype(BF16)

    return pl.pallas_call(
        body, name=name, grid=(t // tr,),
        in_specs=[_row_spec(tr, d), _full_spec((1, d)), _row_spec(tr, d), _row_spec(tr, d)],
        out_specs=[_row_spec(tr, d), _row_spec(tr, d), _full_spec((1, d))],
        out_shape=[jax.ShapeDtypeStruct((t, d), F32), jax.ShapeDtypeStruct((t, d), BF16),
                   jax.ShapeDtypeStruct((1, d), F32)],
        compiler_params=_params(_ARB))(x, g, dn, res)


FFN_TN = 1408
FFN_TM = 512


def _ffn_pieces():
    return [slice(c, min(c + 256, FFN_TN)) for c in range(0, FFN_TN, 256)]


def _ffn_in_act(n, w_in, name):
    t, d = n.shape
    tm = _pick(t, FFN_TM)
    nf = D_FF // FFN_TN

    def body(n_ref, wa_ref, wb_ref, a_ref, b_ref, hm_ref):
        nv = n_ref[...]
        for cols in _ffn_pieces():
            a = jnp.dot(nv, wa_ref[:, cols], preferred_element_type=F32)
            b = jnp.dot(nv, wb_ref[:, cols], preferred_element_type=F32)
            a_ref[:, cols] = a.astype(BF16)
            b_ref[:, cols] = b.astype(BF16)
            hm_ref[:, cols] = (_silu(a) * b).astype(BF16)

    tile = pl.BlockSpec((tm, FFN_TN), lambda i, j: (i, j))
    return pl.pallas_call(
        body, name=name, grid=(t // tm, nf),
        in_specs=[pl.BlockSpec((tm, d), lambda i, j: (i, 0)), pl.BlockSpec((d, FFN_TN), lambda i, j: (0, j)),
                  pl.BlockSpec((d, FFN_TN), lambda i, j: (0, nf + j))],
        out_specs=[tile, tile, tile], out_shape=[jax.ShapeDtypeStruct((t, D_FF), BF16)] * 3,
        compiler_params=_params(_PAR, _PAR))(n, w_in, w_in)


def _ffn_dact(dout, w_out, a, b, name):
    t, d = dout.shape
    tm = _pick(t, FFN_TM)

    def body(do_ref, w_ref, a_ref, b_ref, da_ref, db_ref, hm_ref):
        dov = do_ref[...]
        for cols in _ffn_pieces():
            dh = 0.5 * _dot_nt(dov, w_ref[cols, :])
            av = a_ref[:, cols].astype(F32)
            bv = b_ref[:, cols].astype(F32)
            sg = _sigmoid(av)
            sa = av * sg
            da_ref[:, cols] = (dh * bv * (sg * (1.0 + av * (1.0 - sg)))).astype(BF16)
            db_ref[:, cols] = (dh * sa).astype(BF16)
            hm_ref[:, cols] = (sa * bv).astype(BF16)

    tile = pl.BlockSpec((tm, FFN_TN), lambda i, j: (i, j))
    return pl.pallas_call(
        body, name=name, grid=(t // tm, D_FF // FFN_TN),
        in_specs=[pl.BlockSpec((tm, d), lambda i, j: (i, 0)), pl.BlockSpec((FFN_TN, d), lambda i, j: (j, 0)), tile, tile],
        out_specs=[tile, tile, tile], out_shape=[jax.ShapeDtypeStruct((t, D_FF), BF16)] * 3,
        compiler_params=_params(_PAR, _PAR))(dout, w_out, a, b)


def _merge_fwd(yh, yg, gh, gg):
    t, d = yh.shape
    tr = _pick(t, 256, 8)

    def body(yh_ref, yg_ref, gh_ref, gg_ref, o_ref):
        o_ref[...] = (_sigmoid(gh_ref[...]) * yh_ref[...] + _sigmoid(gg_ref[...]) * yg_ref[...]).astype(BF16)

    return pl.pallas_call(
        body, name="merge_fwd", grid=(t // tr,),
        in_specs=[_row_spec(tr, d), _row_spec(tr, d), _view_rows(gh, tr), _view_rows(gg, tr)],
        out_specs=_row_spec(tr, d),
        out_shape=jax.ShapeDtypeStruct((t, d), BF16), compiler_params=_params(_PAR))(yh, yg, gh[0], gg[0])


def _merge_bwd(dy, yh, yg, gh, gg):
    t, d = yh.shape
    tr = _pick(t, 256, 8)

    def body(dy_ref, yh_ref, yg_ref, gh_ref, gg_ref, dyh_ref, dyg_ref, dgh_ref, dgg_ref):
        dyv = dy_ref[...]
        sh = _sigmoid(gh_ref[...])
        sg = _sigmoid(gg_ref[...])
        dyh_ref[...] = (dyv * sh).astype(BF16)
        dyg_ref[...] = (dyv * sg).astype(BF16)
        dgh_ref[...] = (dyv * yh_ref[...] * sh * (1.0 - sh)).astype(BF16)
        dgg_ref[...] = (dyv * yg_ref[...] * sg * (1.0 - sg)).astype(BF16)

    return pl.pallas_call(
        body, name="merge_bwd", grid=(t // tr,),
        in_specs=[_row_spec(tr, d)] * 3 + [_view_rows(gh, tr), _view_rows(gg, tr)], out_specs=[_row_spec(tr, d)] * 4,
        out_shape=[jax.ShapeDtypeStruct((t, d), BF16)] * 4,
        compiler_params=_params(_PAR))(dy, yh, yg, gh[0], gg[0])


def _final_loss(h, g, tgt):
    t, d = h.shape
    tr = _pick(t, 256, 8)

    def body(h_ref, g_ref, t_ref, loss_ref, dh_ref, dhb_ref, dg_ref):
        @pl.when(pl.program_id(0) == 0)
        def _():
            dg_ref[...] = jnp.zeros_like(dg_ref)
            loss_ref[...] = jnp.zeros_like(loss_ref)

        xv = h_ref[...]
        gv = g_ref[...]
        r = lax.rsqrt(jnp.mean(xv * xv, axis=1, keepdims=True) + EPS)
        xh = xv * r
        err = xh * gv - t_ref[...]
        loss_ref[...] += 0.5 * jnp.sum(jnp.mean(err * err, axis=1, keepdims=True), axis=0, keepdims=True)
        dy = err * (1.0 / d)
        dg_ref[...] += jnp.sum(dy * xh, axis=0, keepdims=True)
        dxh = dy * gv
        dh = r * (dxh - xh * jnp.mean(dxh * xh, axis=1, keepdims=True))
        dh_ref[...] = dh
        dhb_ref[...] = dh.astype(BF16)

    return pl.pallas_call(
        body, name="final_loss", grid=(t // tr,),
        in_specs=[_row_spec(tr, d), _full_spec((1, d)), _row_spec(tr, d)],
        out_specs=[_full_spec((1, 128)), _row_spec(tr, d), _row_spec(tr, d), _full_spec((1, d))],
        out_shape=[jax.ShapeDtypeStruct((1, 128), F32), jax.ShapeDtypeStruct((t, d), F32),
                   jax.ShapeDtypeStruct((t, d), BF16), jax.ShapeDtypeStruct((1, d), F32)],
        compiler_params=_params(_ARB))(h, g, tgt)


def _hg_consts():
    c = CHUNK
    t = np.arange(c)
    mats, masks = [], []
    for lvl in range(6):
        m = 1 << lvl
        blk = t // m
        mat = np.zeros((c, c), np.float32)
        for tt in range(c):
            b = blk[tt]
            if b % 2 == 1:
                mat[tt, b * m:tt + 1] = 1.0
            else:
                mat[tt, tt + 1:(b + 1) * m] = 1.0
        mats.append(mat)
        same = (t[:, None] // (2 * m)) == (t[None, :] // (2 * m))
        masks.append((same & (blk[:, None] % 2 == 1) & (blk[None, :] % 2 == 0)).astype(np.float32))
    pre = np.tril(np.ones((c, c), np.float32))
    suf = np.triu(np.ones((c, c), np.float32), 1)
    mstack = np.concatenate(mats + [pre, suf], 0)
    masks.append(np.eye(c, dtype=np.float32))
    return (jnp.asarray(mstack, BF16), jnp.asarray(mstack.T.copy(), BF16), jnp.asarray(np.stack(masks), F32),
            jnp.asarray(np.eye(HEAD, dtype=np.float32)))


def _gd_consts():
    c = CHUNK
    incl = np.tril(np.ones((c, c), np.float32))
    strict = np.tril(np.ones((c, c), np.float32), -1)
    eye = np.eye(c, dtype=np.float32)
    masks = np.stack([incl, strict, eye, incl.T.copy()])
    return jnp.asarray(incl, BF16), jnp.asarray(incl.T.copy(), BF16), jnp.asarray(masks, F32)


def _chunks_per_step(nc):
    for cb in (32 // HPS, 2, 1):
        if nc % cb == 0:
            return cb
    return 1


def _hg_prep(hq, hf, lg):
    lb = _sigmoid(lg[0:1, :] - lg[1:2, :])
    sg = _sigmoid(hf)
    sgn = _sigmoid(-hf)
    f = lb + (1.0 - lb) * sg
    lf = jnp.log(f)
    kk = (1.0 - lb) * sgn
    q = _silu(hq) * (HEAD ** -0.5)
    return lb, sg, sgn, f, lf, kk, q


def _mx_each(m, xs):
    wide = [jnp.concatenate(_split3(x), axis=1) for x in xs]
    prods = [jnp.dot(m, w, preferred_element_type=F32) for w in wide]
    return [p[:, :HEAD] + p[:, HEAD:2 * HEAD] + p[:, 2 * HEAD:] for p in prods]


def _hg_scaled(x, ex):
    xb = [_bf(a) for a in x]
    eb = [_bf(e[:6 * CHUNK]) for e in ex]
    return [[a * e[lvl * CHUNK:(lvl + 1) * CHUNK] for lvl in range(6)] for a, e in zip(xb, eb)]


def _hg_scores(q, kk, qe, ke, mask_ref):
    p = [mask_ref[6] * _rowsum(a * b) for a, b in zip(q, kk)]
    for lvl in range(6):
        d = [_dot_nt(a[lvl], b[lvl]) for a, b in zip(qe, ke)]
        p = [x + mask_ref[lvl] * y for x, y in zip(p, d)]
    return p


def _hgrn_fwd(hq, hf, hi, hg, logits, gain, consts):
    t = hq[0].shape[0]
    nc = t // CHUNK
    cb = _chunks_per_step(nc)
    rows = cb * CHUNK
    mstack, _, masks, eye = consts
    tile = pl.BlockSpec((rows, HPS * HEAD), lambda c, g: (c, g))

    def body(hq_ref, hf_ref, hi_ref, hg_ref, lg_ref, gain_ref, m_ref, mask_ref, eye_ref,
             oraw_ref, og_ref, ssave_ref, state):
        c = pl.program_id(0)
        g = pl.program_id(1)

        @pl.when(c == 0)
        def _():
            for hh in range(HPS):
                state[g * HPS + hh] = jnp.zeros((HEAD, HEAD), F32)

        lg_all = lg_ref[...]
        gain_v = gain_ref[...]

        def one(i, carry):
            sl = pl.ds(pl.multiple_of(i * CHUNK, CHUNK), CHUNK)
            hs = range(HPS)
            heads = [g * HPS + hh for hh in hs]
            ln = [slice(hh * HEAD, (hh + 1) * HEAD) for hh in hs]
            preps = [_hg_prep(hq_ref[sl, s], hf_ref[sl, s], lg_all[:, s]) for s in ln]
            lf, kk, q = [p[4] for p in preps], [p[5] for p in preps], [p[6] for p in preps]
            v = [hi_ref[sl, s] for s in ln]
            ex = [jnp.exp(x) for x in _mx_each(m_ref[...], lf)]
            eb = [e[6 * CHUNK:7 * CHUNK] for e in ex]
            esfx = [e[7 * CHUNK:8 * CHUNK] for e in ex]
            qe, ke = _hg_scaled(q, ex), _hg_scaled(kk, ex)
            p = _hg_scores(q, kk, qe, ke, mask_ref)
            s0 = [state[h] for h in heads]
            o = _each(lambda a, e, s, pp, vv: _dot(a * e, s) + _dot(pp, vv), q, eb, s0, p, v)
            eye_v = eye_ref[...]
            s1 = _each(lambda s, e, kx, ef, vv: s * _row_to_col(e[CHUNK - 1:CHUNK, :], eye_v) + _dot_tn(kx * ef, vv),
                       s0, eb, kk, esfx, v)
            for hh in hs:
                ssave_ref[i, hh] = s0[hh]
                state[heads[hh]] = s1[hh]
                oraw_ref[sl, ln[hh]] = o[hh]
                r = lax.rsqrt(jnp.mean(o[hh] * o[hh], axis=1, keepdims=True) + EPS)
                og_ref[sl, ln[hh]] = (o[hh] * r * gain_v * _silu(hg_ref[sl, ln[hh]])).astype(BF16)
            return carry

        lax.fori_loop(0, cb, one, 0, unroll=2)

    return pl.pallas_call(
        body, name="hgrn_fwd", grid=(nc // cb, HG_HEADS // HPS),
        in_specs=[_view_tile(v, rows, HPS * HEAD) for v in (hq, hf, hi, hg)] + [
                  pl.BlockSpec((2, HPS * HEAD), lambda c, g: (0, g)),
                  pl.BlockSpec((1, HEAD), lambda c, g: (0, 0)),
                  pl.BlockSpec(mstack.shape, lambda c, g: (0, 0)),
                  pl.BlockSpec(masks.shape, lambda c, g: (0, 0, 0)),
                  pl.BlockSpec(eye.shape, lambda c, g: (0, 0))],
        out_specs=[tile, tile, pl.BlockSpec((cb, HPS, HEAD, HEAD), lambda c, g: (c, g, 0, 0))],
        out_shape=[jax.ShapeDtypeStruct((t, HG_HEADS * HEAD), F32), jax.ShapeDtypeStruct((t, HG_HEADS * HEAD), BF16),
                   jax.ShapeDtypeStruct((nc, HG_HEADS, HEAD, HEAD), F32)],
        scratch_shapes=[pltpu.VMEM((HG_HEADS, HEAD, HEAD), F32)],
        compiler_params=_params(_ARB, _ARB))(hq[0], hf[0], hi[0], hg[0], logits, gain, mstack, masks, eye)


def _hgrn_bwd(hq, hf, hi, hg, logits, gain, oraw, ssave, dog, consts):
    t = hq[0].shape[0]
    nc = t // CHUNK
    cb = _chunks_per_step(nc)
    rows = cb * CHUNK
    nb = nc // cb
    mstack, mstack_t, masks, eye = consts
    tile = pl.BlockSpec((rows, HPS * HEAD), lambda c, g: (nb - 1 - c, g))

    def body(hq_ref, hf_ref, hi_ref, hg_ref, lg_ref, gain_ref, oraw_ref, ssave_ref, dog_ref, m_ref, mt_ref,
             mask_ref, eye_ref, dhq_ref, dhf_ref, dhi_ref, dhg_ref, dgain_ref, dlb_ref, dstate):
        c = pl.program_id(0)
        g = pl.program_id(1)

        @pl.when(c == 0)
        def _():
            for hh in range(HPS):
                dstate[g * HPS + hh] = jnp.zeros((HEAD, HEAD), F32)

        @pl.when((c == 0) & (g == 0))
        def _():
            dgain_ref[...] = jnp.zeros_like(dgain_ref)
            dlb_ref[...] = jnp.zeros_like(dlb_ref)

        lg_all = lg_ref[...]
        gain_v = gain_ref[...]
        eye_v = eye_ref[...]
        last_row = (lax.broadcasted_iota(jnp.int32, (CHUNK, HEAD), 0) == CHUNK - 1).astype(F32)

        def one(j, carry):
            i = cb - 1 - j
            sl = pl.ds(pl.multiple_of(i * CHUNK, CHUNK), CHUNK)
            hs = range(HPS)
            heads = [g * HPS + hh for hh in hs]
            ln = [slice(hh * HEAD, (hh + 1) * HEAD) for hh in hs]
            hqv = [hq_ref[sl, s] for s in ln]
            hgv = [hg_ref[sl, s] for s in ln]
            preps = [_hg_prep(a, hf_ref[sl, s], lg_all[:, s]) for a, s in zip(hqv, ln)]
            lb, sg, sgn, f, lf, kk, q = ([p[n] for p in preps] for n in range(7))
            v = [hi_ref[sl, s] for s in ln]
            ex = [jnp.exp(x) for x in _mx_each(m_ref[...], lf)]
            eb = [e[6 * CHUNK:7 * CHUNK] for e in ex]
            esfx = [e[7 * CHUNK:8 * CHUNK] for e in ex]
            qe, ke = _hg_scaled(q, ex), _hg_scaled(kk, ex)
            p = _hg_scores(q, kk, qe, ke, mask_ref)
            s0 = [ssave_ref[i, hh] for hh in hs]
            ds = [dstate[h] for h in heads]

            o = [oraw_ref[sl, s] for s in ln]
            r = [lax.rsqrt(jnp.mean(x * x, axis=1, keepdims=True) + EPS) for x in o]
            on = _each(lambda x, y: x * y, o, r)
            dg_out = [dog_ref[sl, s] for s in ln]
            sgate = [_silu(x) for x in hgv]
            for hh in hs:
                dhg_ref[sl, ln[hh]] = (dg_out[hh] * on[hh] * gain_v * _dsilu(hgv[hh])).astype(BF16)
            dgain_ref[...] += sum(jnp.sum(d * s * n, axis=0, keepdims=True) for d, s, n in zip(dg_out, sgate, on))
            don = _each(lambda d, s: d * s * gain_v, dg_out, sgate)
            do = _each(lambda rr, dn, n: rr * (dn - n * jnp.mean(dn * n, axis=1, keepdims=True)), r, don, on)

            dp = _each(_dot_nt, do, v)
            dv = _each(lambda pp, d, kx, ef, s: _dot_tn(pp, d) + _dot(kx * ef, s), p, do, kk, esfx, ds)
            dqb = _each(_dot_nt, do, s0)
            dkx = _each(_dot_nt, v, ds)
            diag = [_rowsum(mask_ref[6] * x) for x in dp]
            dq = _each(lambda a, e, d, kx: a * e + d * kx, dqb, eb, diag, kk)
            dk = _each(lambda a, e, d, qq: a * e + d * qq, dkx, esfx, diag, q)
            dxs = [[] for _ in hs]
            for lvl in range(6):
                el = [e[lvl * CHUNK:(lvl + 1) * CHUNK] for e in ex]
                gm = [mask_ref[lvl] * x for x in dp]
                gm = [_bf(x) for x in gm]
                a1 = _each(lambda m_, kx: _dot(m_, kx[lvl]), gm, ke)
                a2 = _each(lambda m_, qq: _dot_tn(m_, qq[lvl]), gm, qe)
                dq = _each(lambda x, a, e: x + a * e, dq, a1, el)
                dk = _each(lambda x, a, e: x + a * e, dk, a2, el)
                for hh in hs:
                    dxs[hh].append((a1[hh] * q[hh] + a2[hh] * kk[hh]) * el[hh])
            e_end_row = [e[CHUNK - 1:CHUNK, :] for e in eb]
            ds_new = _each(lambda qq, e, d, er, s: _dot_tn(qq * e, d) + _row_to_col(er, eye_v) * s, q, eb, do, e_end_row, ds)
            for hh in hs:
                dstate[heads[hh]] = ds_new[hh]
                dend_row = _col_to_row(_rowsum(s0[hh] * ds[hh]), eye_v)
                dxs[hh].append(dqb[hh] * q[hh] * eb[hh] + last_row * (e_end_row[hh] * dend_row))
                dxs[hh].append(dkx[hh] * kk[hh] * esfx[hh])
            dlf = _mx_each(mt_ref[...], [jnp.concatenate(x, axis=0) for x in dxs])

            for hh in hs:
                dhi_ref[sl, ln[hh]] = dv[hh].astype(BF16)
                dhq_ref[sl, ln[hh]] = (dq[hh] * (HEAD ** -0.5) * _dsilu(hqv[hh])).astype(BF16)
                df = dlf[hh] / f[hh]
                dsig = (1.0 - lb[hh]) * sg[hh] * sgn[hh]
                dhf_ref[sl, ln[hh]] = ((df - dk[hh]) * dsig).astype(BF16)
                dlb_t = jnp.sum(df * sgn[hh] - dk[hh] * sgn[hh], axis=0, keepdims=True)
                dlb_ref[pl.ds(heads[hh], 1), :] += dlb_t * lb[hh] * (1.0 - lb[hh])
            return carry

        lax.fori_loop(0, cb, one, 0, unroll=2)

    outs = [jax.ShapeDtypeStruct((t, HG_HEADS * HEAD), BF16)] * 4 + [
        jax.ShapeDtypeStruct((1, HEAD), F32), jax.ShapeDtypeStruct((HG_HEADS, HEAD), F32)]
    return pl.pallas_call(
        body, name="hgrn_bwd", grid=(nb, HG_HEADS // HPS),
        in_specs=[_view_tile(v, rows, HPS * HEAD, lambda c: nb - 1 - c) for v in (hq, hf, hi, hg)] + [
                  pl.BlockSpec((2, HPS * HEAD), lambda c, g: (0, g)),
                  pl.BlockSpec((1, HEAD), lambda c, g: (0, 0)), tile,
                  pl.BlockSpec((cb, HPS, HEAD, HEAD), lambda c, g: (nb - 1 - c, g, 0, 0)), tile,
                  pl.BlockSpec(mstack.shape, lambda c, h: (0, 0)),
                  pl.BlockSpec(mstack_t.shape, lambda c, h: (0, 0)),
                  pl.BlockSpec(masks.shape, lambda c, h: (0, 0, 0)),
                  pl.BlockSpec(eye.shape, lambda c, h: (0, 0))],
        out_specs=[tile, tile, tile, tile, pl.BlockSpec((1, HEAD), lambda c, h: (0, 0)),
                   pl.BlockSpec((HG_HEADS, HEAD), lambda c, h: (0, 0))],
        out_shape=outs, scratch_shapes=[pltpu.VMEM((HG_HEADS, HEAD, HEAD), F32)],
        compiler_params=_params(_ARB, _ARB))(hq[0], hf[0], hi[0], hg[0], logits, gain, oraw, ssave, dog, mstack,
                                             mstack_t, masks, eye)


CONV_W = 512


def _per_head(fn, *arrs):
    width = arrs[0].shape[1]
    return jnp.concatenate([fn(*[a[:, j:j + HEAD] for a in arrs]) for j in range(0, width, HEAD)], axis=1)


def _shift_down(xv, halo, d, top_rows):
    if d == 0:
        return xv, xv[0:8]
    main = pltpu.roll(xv, d, 0)
    top = jnp.where(top_rows < d, pltpu.roll(halo, d, 0), main[0:8])
    return main, top


def _conv_parts(x_ref, halo_ref, w_ref, first):
    xv = x_ref[...]
    halo = jnp.where(first, 0.0, halo_ref[...])
    top_rows = lax.broadcasted_iota(jnp.int32, (8, xv.shape[1]), 0)
    shifted = [_shift_down(xv, halo, CONV_K - 1 - j, top_rows) for j in range(CONV_K)]
    w = w_ref[...]
    acc = sum(shifted[j][0] * w[j:j + 1, :] for j in range(CONV_K))
    acc_top = sum(shifted[j][1] * w[j:j + 1, :] for j in range(CONV_K))
    return shifted, acc, acc_top


def _conv_fwd(x, w8, l2scale, name):
    x, off, width = x
    t = x.shape[0]
    o = off // CONV_W
    tr = _pick(t, 512, 8)

    def post(cv):
        s = _silu(cv)
        if l2scale is not None:
            s = _per_head(lambda sh: sh * (lax.rsqrt(_rowsum(sh * sh) + EPS) * l2scale), s)
        return s

    def body(x_ref, halo_ref, w_ref, o_ref):
        _, acc, acc_top = _conv_parts(x_ref, halo_ref, w_ref, pl.program_id(1) == 0)
        o_ref[...] = post(acc)
        o_ref[0:8, :] = post(acc_top)

    return pl.pallas_call(
        body, name=name, grid=(width // CONV_W,t // tr),
        in_specs=[pl.BlockSpec((tr, CONV_W), lambda j, i: (i, o + j)),
                  pl.BlockSpec((8, CONV_W), lambda j, i: (jnp.maximum(i * (tr // 8) - 1, 0), o + j)),
                  pl.BlockSpec((8, CONV_W), lambda j, i: (0, j))],
        out_specs=pl.BlockSpec((tr, CONV_W), lambda j, i: (i, j)),
        out_shape=jax.ShapeDtypeStruct((t, width), F32), compiler_params=_params(_PAR, _PAR))(x, x, w8)


def _conv_bwd_a(x, w8, dy, l2scale, name):
    x, off, width = x
    t = x.shape[0]
    o = off // CONV_W
    tr = _pick(t, 512, 8)

    def l2_bwd(s, dyh):
        r = lax.rsqrt(_rowsum(s * s) + EPS)
        y0 = s * r
        dy0 = dyh * l2scale
        return r * (dy0 - y0 * _rowsum(dy0 * y0))

    def to_dc(cv, dyv):
        if l2scale is not None:
            dyv = _per_head(l2_bwd, _silu(cv), dyv)
        return dyv * _dsilu(cv)

    def body(x_ref, halo_ref, w_ref, dy_ref, dc_ref, dw_ref):
        @pl.when(pl.program_id(1) == 0)
        def _():
            dw_ref[...] = jnp.zeros_like(dw_ref)

        shifted, acc, acc_top = _conv_parts(x_ref, halo_ref, w_ref, pl.program_id(1) == 0)
        dyv = dy_ref[...]
        dc = to_dc(acc, dyv)
        dc_top = to_dc(acc_top, dyv[0:8])
        dc_ref[...] = dc
        dc_ref[0:8, :] = dc_top
        rest = (lax.broadcasted_iota(jnp.int32, dc.shape, 0) >= 8).astype(F32)
        dc_rest = dc * rest
        for j in range(CONV_K):
            dw_ref[j:j + 1, :] += (jnp.sum(dc_rest * shifted[j][0], axis=0, keepdims=True)
                                   + jnp.sum(dc_top * shifted[j][1], axis=0, keepdims=True))

    return pl.pallas_call(
        body, name=name, grid=(width // CONV_W,t // tr),
        in_specs=[pl.BlockSpec((tr, CONV_W), lambda j, i: (i, o + j)),
                  pl.BlockSpec((8, CONV_W), lambda j, i: (jnp.maximum(i * (tr // 8) - 1, 0), o + j)),
                  pl.BlockSpec((8, CONV_W), lambda j, i: (0, j)),
                  pl.BlockSpec((tr, CONV_W), lambda j, i: (i, j))],
        out_specs=[pl.BlockSpec((tr, CONV_W), lambda j, i: (i, j)), pl.BlockSpec((8, CONV_W), lambda j, i: (0, j))],
        out_shape=[jax.ShapeDtypeStruct((t, width), F32), jax.ShapeDtypeStruct((8, width), F32)],
        compiler_params=_params(_PAR, _ARB))(x, x, w8, dy)


def _conv_bwd_b(dc, w8, name):
    t, width = dc.shape
    tr = _pick(t, 512, 8)
    nt = t // tr

    def body(dc_ref, halo_ref, w_ref, dx_ref):
        dcv = dc_ref[...]
        halo = jnp.where(pl.program_id(1) == nt - 1, 0.0, halo_ref[...])
        w = w_ref[...]
        bot_rows = lax.broadcasted_iota(jnp.int32, (8, CONV_W), 0)
        acc = dcv * w[CONV_K - 1:CONV_K, :]
        acc_bot = dcv[tr - 8:tr] * w[CONV_K - 1:CONV_K, :]
        for d in range(1, CONV_K):
            main = pltpu.roll(dcv, tr - d, 0)
            bot = jnp.where(bot_rows >= 8 - d, pltpu.roll(halo, 8 - d, 0), main[tr - 8:tr])
            wj = w[CONV_K - 1 - d:CONV_K - d, :]
            acc = acc + main * wj
            acc_bot = acc_bot + bot * wj
        dx_ref[...] = acc.astype(BF16)
        dx_ref[tr - 16:tr, :] = jnp.concatenate([acc[tr - 16:tr - 8], acc_bot], axis=0).astype(BF16)

    return pl.pallas_call(
        body, name=name, grid=(width // CONV_W,nt),
        in_specs=[pl.BlockSpec((tr, CONV_W), lambda j, i: (i, j)),
                  pl.BlockSpec((8, CONV_W), lambda j, i: (jnp.minimum((i + 1) * (tr // 8), t // 8 - 1), j)),
                  pl.BlockSpec((8, CONV_W), lambda j, i: (0, j))],
        out_specs=pl.BlockSpec((tr, CONV_W), lambda j, i: (i, j)),
        out_shape=jax.ShapeDtypeStruct((t, width), BF16), compiler_params=_params(_PAR, _PAR))(dc, dc, w8)


def _each(f, *lists):
    return [f(*xs) for xs in zip(*lists)]


def _split2_each(xs):
    hi = [_bf(x) for x in xs]
    lo = [_bf(x - h.astype(F32)) for x, h in zip(xs, hi)]
    return hi, lo


def _hp_each(a_split, b_split):
    (ah, al), (bh, bl) = a_split, b_split
    rows = ah[0].shape[0]
    d12 = [jnp.dot(jnp.concatenate([x, y], axis=0), z, preferred_element_type=F32) for x, y, z in zip(ah, al, bh)]
    d3 = [jnp.dot(x, y, preferred_element_type=F32) for x, y in zip(ah, bl)]
    return [d[:rows] + d[rows:] + e for d, e in zip(d12, d3)]


def _tri_inv_each(a_list, eye):
    ns = [-a for a in a_list]
    ps = [eye + n for n in ns]
    n_split = _split2_each(ns)
    for _ in range(5):
        ns = _hp_each(n_split, n_split)
        n_split = _split2_each(ns)
        ps = [p + d for p, d in zip(ps, _hp_each(_split2_each(ps), n_split))]
    return ps


def _gd_gates(gab, alog, dtb):
    sp_arg = gab + dtb
    return sp_arg, -jnp.exp(alog) * _softplus(sp_arg), _sigmoid(gab)


def _pick_lane(tile, base, head):
    g, hh = head
    col = tile[:, base + hh:base + hh + 1]
    for gi in range(1, GD_HEADS // HPS):
        lane = base + gi * HPS + hh
        col = jnp.where(g == gi, tile[:, lane:lane + 1], col)
    return col


def _gd_chunks(q, k, v, g_all, beta_all, sel, l_ref, mask_ref, tm=None):
    incl, strict, eye, upper = mask_ref[0], mask_ref[1], mask_ref[2], mask_ref[3]
    lmat = l_ref[...]
    gb = [jnp.broadcast_to(_pick_lane(g_all, 0, s), (CHUNK, HEAD)) for s in sel]
    bb = [jnp.broadcast_to(_pick_lane(beta_all, GD_HEADS, s), (CHUNK, HEAD)) for s in sel]
    gam = _mx_each(lmat, gb)
    gam_row = [jnp.sum(x[:, :CHUNK] * upper, axis=0, keepdims=True) for x in gb]
    lm = _each(lambda gm, gr: incl * jnp.exp(jnp.minimum(gm[:, :CHUNK] - gr, 0.0)), gam, gam_row)
    kb = _each(lambda x, b: x * b, k, bb)
    a = _each(lambda x, y, m: strict * _dot_nt(x, y) * m, kb, k, lm)
    if tm is None:
        tm = _tri_inv_each(a, eye)
    eg = [jnp.exp(x) for x in gam]
    vb = _each(lambda x, b: x * b, v, bb)
    kbg = _each(lambda x, e: x * e, kb, eg)
    uw = _each(lambda t_, x, y: _dot(t_, jnp.concatenate([x, y], axis=1)), tm, vb, kbg)
    u = [x[:, :HEAD] for x in uw]
    w = [x[:, HEAD:] for x in uw]
    qk = _each(lambda x, y, m: _dot_nt(x, y) * m, q, k, lm)
    g_end = [x[CHUNK - 1:CHUNK, :] for x in gam]
    ekg = _each(lambda e, x: jnp.exp(e - x), g_end, gam)
    ge = [jnp.exp(e) for e in g_end]
    kg = _each(lambda x, e: x * e, k, ekg)
    qg = _each(lambda x, e: x * e, q, eg)
    names = ("bb", "lm", "kb", "a", "tm", "eg", "vb", "kbg", "u", "w", "qk", "ekg", "ge", "kg", "qg")
    cols = (bb, lm, kb, a, tm, eg, vb, kbg, u, w, qk, ekg, ge, kg, qg)
    return [dict(zip(names, vals)) for vals in zip(*cols)]


def _gd_specs(rows, rev_nb=None):
    def cidx(c):
        return c if rev_nb is None else rev_nb - 1 - c

    qk_tile = pl.BlockSpec((rows, HPS // 2 * HEAD), lambda c, g: (cidx(c), g))
    v_tile = pl.BlockSpec((rows, HPS * HEAD), lambda c, g: (cidx(c), g))
    gab_tile = pl.BlockSpec((rows, HEAD), lambda c, g: (cidx(c), 0))
    return qk_tile, v_tile, gab_tile


def _gdn_fwd(qn, kn, cv, gab, gz, alog, dtb, gain, consts):
    t = qn.shape[0]
    nc = t // CHUNK
    cb = _chunks_per_step(nc)
    rows = cb * CHUNK
    lmat, _, masks = consts
    qk_tile, v_tile, gab_tile = _gd_specs(rows)
    row128 = pl.BlockSpec((1, HEAD), lambda c, h: (0, 0))

    def body(q_ref, k_ref, v_ref, gab_ref, gz_ref, alog_ref, dtb_ref, gain_ref, l_ref, mask_ref,
             oraw_ref, og_ref, ssave_ref, tsave_ref, state):
        c = pl.program_id(0)
        g = pl.program_id(1)

        @pl.when(c == 0)
        def _():
            for hh in range(HPS):
                state[g * HPS + hh] = jnp.zeros((HEAD, HEAD), F32)

        alog = alog_ref[...]
        dtb = dtb_ref[...]
        gain_v = gain_ref[...]

        def one(i, carry):
            sl = pl.ds(pl.multiple_of(i * CHUNK, CHUNK), CHUNK)
            _, g_all, beta_all = _gd_gates(gab_ref[sl, :], alog, dtb)
            heads = [g * HPS + hh for hh in range(HPS)]
            lq = [slice(hh // 2 * HEAD, (hh // 2 + 1) * HEAD) for hh in range(HPS)]
            lv = [slice(hh * HEAD, (hh + 1) * HEAD) for hh in range(HPS)]
            chs = _gd_chunks([q_ref[sl, s] for s in lq], [k_ref[sl, s] for s in lq], [v_ref[sl, s] for s in lv],
                             g_all, beta_all, [(g, hh) for hh in range(HPS)], l_ref, mask_ref)
            s0 = [state[h] for h in heads]
            ws = _each(lambda ch, s: _dot(jnp.concatenate([ch["w"], ch["qg"]], axis=0), s), chs, s0)
            v_new = _each(lambda ch, x: ch["u"] - x[:CHUNK], chs, ws)
            o = _each(lambda ch, x, vn: x[CHUNK:] + _dot(ch["qk"], vn), chs, ws, v_new)
            s1 = _each(lambda ch, s, vn: s * ch["ge"] + _dot_tn(ch["kg"], vn), chs, s0, v_new)
            for hh in range(HPS):
                ssave_ref[i, hh] = s0[hh]
                tsave_ref[i, hh] = chs[hh]["tm"]
                state[heads[hh]] = s1[hh]
                oraw_ref[sl, lv[hh]] = o[hh]
                r = lax.rsqrt(jnp.mean(o[hh] * o[hh], axis=1, keepdims=True) + EPS)
                og_ref[sl, lv[hh]] = (o[hh] * r * gain_v * _silu(gz_ref[sl, lv[hh]])).astype(BF16)
            return carry

        lax.fori_loop(0, cb, one, 0, unroll=2)

    return pl.pallas_call(
        body, name="gdn_fwd", grid=(nc // cb, GD_HEADS // HPS),
        in_specs=[qk_tile, qk_tile, v_tile, gab_tile, _view_tile(gz, rows, HPS * HEAD), row128, row128, row128,
                  pl.BlockSpec(lmat.shape, lambda c, g: (0, 0)),
                  pl.BlockSpec(masks.shape, lambda c, g: (0, 0, 0))],
        out_specs=[v_tile, v_tile, pl.BlockSpec((cb, HPS, HEAD, HEAD), lambda c, g: (c, g, 0, 0)),
                   pl.BlockSpec((cb, HPS, CHUNK, CHUNK), lambda c, g: (c, g, 0, 0))],
        out_shape=[jax.ShapeDtypeStruct((t, GD_HEADS * HEAD), F32), jax.ShapeDtypeStruct((t, GD_HEADS * HEAD), BF16),
                   jax.ShapeDtypeStruct((nc, GD_HEADS, HEAD, HEAD), F32),
                   jax.ShapeDtypeStruct((nc, GD_HEADS, CHUNK, CHUNK), F32)],
        scratch_shapes=[pltpu.VMEM((GD_HEADS, HEAD, HEAD), F32)],
        compiler_params=_params(_ARB, _ARB))(qn, kn, cv, gab, gz[0], alog, dtb, gain, lmat, masks)


def _gdn_bwd(qn, kn, cv, gab, gz, alog, dtb, gain, oraw, ssave, tsave, dog, consts):
    t = qn.shape[0]
    nc = t // CHUNK
    cb = _chunks_per_step(nc)
    rows = cb * CHUNK
    nb = nc // cb
    lmat, lmat_t, masks = consts
    qk_tile, v_tile, gab_tile = _gd_specs(rows, nb)
    row128 = pl.BlockSpec((1, HEAD), lambda c, h: (0, 0))

    def body(q_ref, k_ref, v_ref, gab_ref, gz_ref, alog_ref, dtb_ref, gain_ref, oraw_ref, ssave_ref, tsave_ref, dog_ref,
             l_ref, lt_ref, mask_ref,
             dq_ref, dk_ref, dv_ref, dgab_ref, dgz_ref, small_ref, dstate):
        c = pl.program_id(0)
        g = pl.program_id(1)

        @pl.when(c == 0)
        def _():
            for hh in range(HPS):
                dstate[g * HPS + hh] = jnp.zeros((HEAD, HEAD), F32)

        @pl.when((c == 0) & (g == 0))
        def _():
            small_ref[...] = jnp.zeros_like(small_ref)

        alog = alog_ref[...]
        dtb = dtb_ref[...]
        gain_v = gain_ref[...]
        lane = lax.broadcasted_iota(jnp.int32, (1, HEAD), 1)
        last_row = (lax.broadcasted_iota(jnp.int32, (CHUNK, HEAD), 0) == CHUNK - 1).astype(F32)

        def one(j, carry):
            i = cb - 1 - j
            sl = pl.ds(pl.multiple_of(i * CHUNK, CHUNK), CHUNK)
            sp_arg, g_all, beta_all = _gd_gates(gab_ref[sl, :], alog, dtb)
            strict, eye = mask_ref[1], mask_ref[2]
            ltm = lt_ref[...]
            hs = range(HPS)
            heads = [g * HPS + hh for hh in hs]
            lq = [slice(hh // 2 * HEAD, (hh // 2 + 1) * HEAD) for hh in hs]
            lv = [slice(hh * HEAD, (hh + 1) * HEAD) for hh in hs]
            q = [q_ref[sl, s] for s in lq]
            k = [k_ref[sl, s] for s in lq]
            v = [v_ref[sl, s] for s in lv]
            gzv = [gz_ref[sl, s] for s in lv]
            chs = _gd_chunks(q, k, v, g_all, beta_all, [(g, hh) for hh in hs], l_ref, mask_ref,
                             tm=[tsave_ref[i, hh] for hh in hs])

            def col(name):
                return [ch[name] for ch in chs]

            def mul(x, y):
                return x * y

            tm, lm, eg, bb = col("tm"), col("lm"), col("eg"), col("bb")
            s0 = [ssave_ref[i, hh] for hh in hs]
            ds = [dstate[h] for h in heads]
            v_new = _each(lambda u, w, s: u - _dot(w, s), col("u"), col("w"), s0)

            o = [oraw_ref[sl, s] for s in lv]
            r = [lax.rsqrt(jnp.mean(x * x, axis=1, keepdims=True) + EPS) for x in o]
            on = _each(mul, o, r)
            dg_out = [dog_ref[sl, s] for s in lv]
            sgate = [_silu(x) for x in gzv]
            for hh in hs:
                dgz_ref[sl, lv[hh]] = (dg_out[hh] * on[hh] * gain_v * _dsilu(gzv[hh])).astype(BF16)
            small_ref[0:1, :] += sum(jnp.sum(d * s * n, axis=0, keepdims=True) for d, s, n in zip(dg_out, sgate, on))
            don = _each(lambda d, s: d * s * gain_v, dg_out, sgate)
            do = _each(lambda rr, dn, n: rr * (dn - n * jnp.mean(dn * n, axis=1, keepdims=True)), r, don, on)

            dv_new = _each(lambda a, d, b, s: _dot_tn(a, d) + _dot(b, s), col("qk"), do, col("kg"), ds)
            dqk = _each(_dot_nt, do, v_new)
            dkg = _each(_dot_nt, v_new, ds)
            dge = _each(lambda s, d: jnp.sum(_rowsum(s * d), axis=0, keepdims=True), s0, ds)
            both = _each(lambda d, dv: jnp.concatenate([d, dv], axis=0), do, dv_new)
            from_s = _each(_dot_nt, both, s0)
            dqg = [x[:CHUNK] for x in from_s]
            dw = [-x[CHUNK:] for x in from_s]
            ds_new = _each(lambda qg, w, bo, ge, s: _dot_tn(jnp.concatenate([qg, -w], axis=0), bo) + ge * s,
                           col("qg"), col("w"), both, col("ge"), ds)
            for hh in hs:
                dstate[heads[hh]] = ds_new[hh]

            side = _each(lambda dv, d: jnp.concatenate([dv, d], axis=1), dv_new, dw)
            back = _each(_dot_tn, tm, side)
            dvb = [x[:, :HEAD] for x in back]
            dkbg = [x[:, HEAD:] for x in back]
            dtm = _each(lambda sd, vb, kbg: _dot_nt(sd, jnp.concatenate([vb, kbg], axis=1)), side, col("vb"), col("kbg"))
            dtt = _each(_dot_nt, dtm, tm)
            da = _each(lambda t_, x: -_dot_tn(t_, x) * strict, tm, dtt)
            dal = _each(mul, da, lm)
            dqk_l = _each(mul, dqk, lm)
            stack = _each(lambda x, y: jnp.concatenate([x, y], axis=0), dal, dqk_l)
            on_k = _each(_dot, stack, k)
            dkb = _each(lambda x, y, e: x[:CHUNK] + y * e, on_k, dkbg, eg)
            dq = _each(lambda x, y, e: x[CHUNK:] + y * e, on_k, dqg, eg)
            dk = _each(lambda st, kb, qq, z, ekg, w_, b: _dot_tn(st, jnp.concatenate([kb, qq], axis=0)) + z * ekg + w_ * b,
                       stack, col("kb"), q, dkg, col("ekg"), dkb, bb)
            gmat = _each(lambda x, a, y, qk: x * a + y * qk, da, col("a"), dqk, col("qk"))
            t_kg = _each(lambda x, y: _rowsum(x * y), dkg, col("kg"))
            dgam = _each(lambda gm, x, qg, t_, y, kbg: (_rowsum(gm) - _row_to_col(jnp.sum(gm, axis=0, keepdims=True), eye)
                                                        + _rowsum(x * qg) - t_ + _rowsum(y * kbg)),
                         gmat, dqg, col("qg"), t_kg, dkbg, col("kbg"))
            dg_end = _each(lambda t_, e, ge: jnp.sum(t_, axis=0, keepdims=True) + e * ge[:, 0:1], t_kg, dge, col("ge"))
            dgam = _each(lambda x, e: x + last_row * e, dgam, dg_end)
            dbeta = _each(lambda x, kk, y, vv: _rowsum(x * kk) + _rowsum(y * vv), dkb, k, dvb, v)
            dg = _mx_each(ltm, dgam)

            for hh in hs:
                dv_ref[sl, lv[hh]] = dvb[hh] * bb[hh]
            fac_g = -jnp.exp(alog) * _sigmoid(sp_arg)
            fac_b = beta_all * (1.0 - beta_all)
            hot_g = [(lane == h).astype(F32) for h in heads]
            hot_b = [(lane == GD_HEADS + h).astype(F32) for h in heads]
            dga = _each(lambda x, hot: x * hot * fac_g, dg, hot_g)
            dgb = _each(lambda x, hot: x * hot * fac_b, dbeta, hot_b)
            small_ref[1:2, :] += sum(jnp.sum(x, axis=0, keepdims=True) for x in dga)
            small_ref[2:3, :] += sum(jnp.sum(x * hot * g_all, axis=0, keepdims=True) for x, hot in zip(dg, hot_g))
            for pair in range(HPS // 2):
                lqp = slice(pair * HEAD, (pair + 1) * HEAD)
                dq_ref[sl, lqp] = dq[2 * pair] + dq[2 * pair + 1]
                dk_ref[sl, lqp] = dk[2 * pair] + dk[2 * pair + 1]
            dgab_ref[sl, :] = sum(a + b for a, b in zip(dga, dgb))
            return carry

        lax.fori_loop(0, cb, one, 0, unroll=2)

    groups = GD_HEADS // HPS
    outs = [jax.ShapeDtypeStruct((t, 1024), F32), jax.ShapeDtypeStruct((t, 1024), F32),
            jax.ShapeDtypeStruct((t, 2048), F32), jax.ShapeDtypeStruct((t, groups * HEAD), F32),
            jax.ShapeDtypeStruct((t, 2048), BF16), jax.ShapeDtypeStruct((8, HEAD), F32)]
    return pl.pallas_call(
        body, name="gdn_bwd", grid=(nb, groups),
        in_specs=[qk_tile, qk_tile, v_tile, gab_tile, _view_tile(gz, rows, HPS * HEAD, lambda c: nb - 1 - c),
                  row128, row128, row128, v_tile,
                  pl.BlockSpec((cb, HPS, HEAD, HEAD), lambda c, g: (nb - 1 - c, g, 0, 0)),
                  pl.BlockSpec((cb, HPS, CHUNK, CHUNK), lambda c, g: (nb - 1 - c, g, 0, 0)), v_tile,
                  pl.BlockSpec(lmat.shape, lambda c, g: (0, 0)),
                  pl.BlockSpec(lmat_t.shape, lambda c, g: (0, 0)),
                  pl.BlockSpec(masks.shape, lambda c, g: (0, 0, 0))],
        out_specs=[qk_tile, qk_tile, v_tile, pl.BlockSpec((rows, HEAD), lambda c, g: (nb - 1 - c, g)), v_tile,
                   pl.BlockSpec((8, HEAD), lambda c, g: (0, 0))],
        out_shape=outs, scratch_shapes=[pltpu.VMEM((GD_HEADS, HEAD, HEAD), F32)],
        compiler_params=_params(_ARB, _ARB))(qn, kn, cv, gab, gz[0], alog, dtb, gain, oraw, ssave, tsave, dog,
                                             lmat, lmat_t, masks)


def _fold_groups(wide):
    t, width = wide.shape
    tr = _pick(t, 512, 8)

    def body(w_ref, o_ref):
        acc = w_ref[:, 0:HEAD]
        for j in range(1, width // HEAD):
            acc = acc + w_ref[:, j * HEAD:(j + 1) * HEAD]
        o_ref[...] = acc.astype(BF16)

    return pl.pallas_call(
        body, name="fold_gate_grads", grid=(t // tr,), in_specs=[_row_spec(tr, width)], out_specs=_row_spec(tr, HEAD),
        out_shape=jax.ShapeDtypeStruct((t, HEAD), BF16), compiler_params=_params(_PAR))(wide)


def _adam_math(w, g, m, v):
    m2 = ADAM_B1 * m + (1.0 - ADAM_B1) * g
    v2 = ADAM_B2 * v + (1.0 - ADAM_B2) * (g * g)
    m_hat = m2 / (1.0 - ADAM_B1 ** ADAM_STEP)
    v_hat = v2 / (1.0 - ADAM_B2 ** ADAM_STEP)
    delta = -ADAM_LR * (m_hat / (jnp.sqrt(v_hat) + ADAM_EPS) + ADAM_WD * w)
    return delta, m2, v2


def _adamw(w, g, m, v, name, after=None):
    r, c = w.shape
    tr = r
    for cand in range(8, r + 1, 8):
        if r % cand == 0 and cand * c * 4 <= (1 << 20):
            tr = cand
    if r % 8 != 0:
        tr = r

    def body(w_ref, g_ref, m_ref, v_ref, *rest):
        d_ref, m2_ref, v2_ref = rest[-3:]
        d, m2, v2 = _adam_math(w_ref[...], g_ref[...], m_ref[...], v_ref[...])
        d_ref[...] = d
        m2_ref[...] = m2
        v2_ref[...] = v2

    spec = pl.BlockSpec((tr, c), lambda i: (i, 0))
    extra = [] if after is None else [after]
    return pl.pallas_call(
        body, name=name, grid=(r // tr,), in_specs=[spec] * 4 + [_ANY] * len(extra), out_specs=[spec] * 3,
        out_shape=[jax.ShapeDtypeStruct((r, c), F32)] * 3, compiler_params=_params(_PAR))(w, g, m, v, *extra)


_ANY = pl.BlockSpec(memory_space=pl.ANY)


def _place():
    return lax.axis_index("x"), lax.axis_index("y"), lax.axis_index("c")


def _gather_weights(packs, nchs, name):
    n = len(packs)
    halves = [p.shape[0] // 2 for p in packs]
    base = [sum(nchs[:i]) for i in range(n)]
    total = sum(nchs)
    for p, h, k in zip(packs, halves, nchs):
        assert p.shape[0] == 2 * h and h % k == 0 and (h // k) % 16 == 0

    def body(*refs):
        p_refs, g_refs, (send_sems, recv_sems) = refs[:n], refs[n:2 * n], refs[2 * n:]
        x, y, c = _place()
        sibling = (x, y, 1 - c)
        chips = [(1 - x, y), (x, 1 - y), (1 - x, 1 - y)]
        chunks = [(a, q) for a in range(n) for q in range(nchs[a])]

        def rows_of(a, pc, q):
            ch = halves[a] // nchs[a]
            return pl.ds(pl.multiple_of(pc * halves[a] + q * ch, 16), ch)

        def piece(a, px, py, pc, q):
            return g_refs[a].at[2 * px + py, rows_of(a, pc, q), :]

        def copy(k, src, dst, to):
            return pltpu.make_async_remote_copy(src_ref=src, dst_ref=dst, send_sem=send_sems.at[k],
                                                recv_sem=recv_sems.at[k], device_id=to, device_id_type=MESH)

        def sem_of(j, a, q):
            return j * total + base[a] + q

        first = {(j, a, q): copy(sem_of(j, a, q), p_refs[a].at[rows_of(a, c, q), :], piece(a, x, y, c, q), (*chip, c))
                 for j, chip in enumerate(chips) for a, q in chunks}
        for a, q in chunks:
            for j in range(3):
                first[j, a, q].start()
        passed = {(j, a, q): copy(sem_of(3 + j, a, q), piece(a, *chip, c, q), piece(a, *chip, c, q), sibling)
                  for j, chip in enumerate(chips) for a, q in chunks}
        for a, q in chunks:
            for j, chip in enumerate(chips):
                copy(sem_of(j, a, q), p_refs[a].at[rows_of(a, c, q), :], piece(a, *chip, c, q), (*chip, c)).wait_recv()
                passed[j, a, q].start()
        for a, q in chunks:
            for j, chip in enumerate(chips):
                copy(sem_of(3 + j, a, q), piece(a, *chip, 1 - c, q), piece(a, *chip, 1 - c, q), sibling).wait_recv()
        for key in first:
            first[key].wait_send()
            passed[key].wait_send()

    return pl.pallas_call(
        body, name=name, out_shape=[jax.ShapeDtypeStruct((4,) + p.shape, p.dtype) for p in packs],
        in_specs=[_ANY] * n, out_specs=[_ANY] * n,
        scratch_shapes=[pltpu.SemaphoreType.DMA((6 * total,)), pltpu.SemaphoreType.DMA((6 * total,))])(*packs)


def _swap_with_sibling(arrs, nchs, lead, name, halves=False):
    n = len(arrs)
    jobs = []
    hs = [arr.shape[-2] // (2 if halves else 1) for arr in arrs]
    for a, (h, k) in enumerate(zip(hs, nchs)):
        assert h % k == 0 and (h // k) % 16 == 0
        for s in (range(lead) if lead else [None]):
            jobs += [(a, s, q * (h // k), h // k) for q in range(k)]

    def body(*refs):
        src, dst, (send_sems, recv_sems) = refs[:n], refs[n:2 * n], refs[2 * n:]
        x, y, c = _place()

        def at(ref, s, r0, rows):
            return ref.at[pl.ds(r0, rows), :] if s is None else ref.at[s, pl.ds(r0, rows), :]

        def src_rows(a, r0):
            return pl.multiple_of((1 - c) * hs[a] + r0, 16) if halves else r0

        copies = [pltpu.make_async_remote_copy(
            src_ref=at(src[a], s, src_rows(a, r0), rows), dst_ref=at(dst[a], s, r0, rows), send_sem=send_sems.at[k],
            recv_sem=recv_sems.at[k], device_id=(x, y, 1 - c), device_id_type=MESH)
            for k, (a, s, r0, rows) in enumerate(jobs)]
        for cp in copies:
            cp.start()
        for cp in copies:
            cp.wait()

    shapes = [jax.ShapeDtypeStruct(arr.shape[:-2] + (h, arr.shape[-1]), arr.dtype) for arr, h in zip(arrs, hs)]
    return pl.pallas_call(
        body, name=name, out_shape=shapes, in_specs=[_ANY] * n, out_specs=[_ANY] * n,
        scratch_shapes=[pltpu.SemaphoreType.DMA((len(jobs),)), pltpu.SemaphoreType.DMA((len(jobs),))])(*arrs)


def _add2(full, b, core, name):
    n, rows, w = b.shape
    tr = _pick(rows, 256, 16)
    nblk = rows // tr

    def body(c_ref, a_ref, b_ref, o_ref):
        o_ref[...] = (a_ref[...].astype(F32) + b_ref[...].astype(F32)).astype(BF16)

    spec = pl.BlockSpec((1, tr, w), lambda i, j, c_ref: (i, j, 0))
    grid_spec = pltpu.PrefetchScalarGridSpec(
        num_scalar_prefetch=1, grid=(n, nblk),
        in_specs=[pl.BlockSpec((1, tr, w), lambda i, j, c_ref: (i, c_ref[0] * nblk + j, 0)), spec], out_specs=spec)
    return pl.pallas_call(
        body, name=name, grid_spec=grid_spec, out_shape=jax.ShapeDtypeStruct(b.shape, BF16),
        compiler_params=_params(_PAR, _PAR))(core, full, b)


def _reduce_chips(partials, nchs, name):
    n = len(partials)
    jobs = []
    for a, (arr, k) in enumerate(zip(partials, nchs)):
        h = arr.shape[1]
        assert h % k == 0 and (h // k) % 16 == 0
        jobs += [(a, q * (h // k), h // k) for q in range(k)]

    def body(*refs):
        src, dst, (send_sems, recv_sems) = refs[:n], refs[n:2 * n], refs[2 * n:]
        x, y, c = _place()
        chips = [(1 - x, y), (x, 1 - y), (1 - x, 1 - y)]
        copies = [pltpu.make_async_remote_copy(
            src_ref=src[a].at[2 * px + py, pl.ds(r0, rows), :], dst_ref=dst[a].at[j, pl.ds(r0, rows), :],
            send_sem=send_sems.at[3 * k + j], recv_sem=recv_sems.at[3 * k + j],
            device_id=(px, py, c), device_id_type=MESH)
            for k, (a, r0, rows) in enumerate(jobs) for j, (px, py) in enumerate(chips)]
        for cp in copies:
            cp.start()
        for cp in copies:
            cp.wait()

    return pl.pallas_call(
        body, name=name,
        out_shape=[jax.ShapeDtypeStruct((3,) + p.shape[1:], p.dtype) for p in partials],
        in_specs=[_ANY] * n, out_specs=[_ANY] * n,
        scratch_shapes=[pltpu.SemaphoreType.DMA((3 * len(jobs),)), pltpu.SemaphoreType.DMA((3 * len(jobs),))])(*partials)


_HBM = pl.BlockSpec(memory_space=pltpu.HBM)
_SEM = pl.BlockSpec(memory_space=pltpu.SEMAPHORE)
_DATAFLOW = pltpu.SideEffectType.DATAFLOW_SIDE_EFFECTING


def _ici_jobs(srcs, nchs, kind):
    jobs = []
    for a, (arr, k) in enumerate(zip(srcs, nchs)):
        h = arr.shape[0] // 2 if kind == "gather" else arr.shape[1]
        assert h % k == 0 and (h // k) % 16 == 0
        jobs += [(a, h, q * (h // k), h // k) for q in range(k)]
    return jobs


def _ici_copies(src, land, send_sems, recv_sems, jobs, kind):
    x, y, c = _place()
    chips = [(1 - x, y), (x, 1 - y), (1 - x, 1 - y)]
    copies = []
    for k, (a, h, r0, rows) in enumerate(jobs):
        for j, (px, py) in enumerate(chips):
            if kind == "gather":
                at = pl.ds(pl.multiple_of(c * h + r0, 16), rows)
                s, d = src[a].at[at, :], land[a].at[2 * x + y, at, :]
            else:
                s, d = src[a].at[2 * px + py, pl.ds(r0, rows), :], land[a].at[j, pl.ds(r0, rows), :]
            copies.append(pltpu.make_async_remote_copy(
                src_ref=s, dst_ref=d, send_sem=send_sems.at[3 * k + j], recv_sem=recv_sems.at[3 * k + j],
                device_id=(px, py, c), device_id_type=MESH))
    return copies


def _ici_start(srcs, nchs, kind, name):
    n = len(srcs)
    jobs = _ici_jobs(srcs, nchs, kind)
    lead = (lambda s: (4,) + s.shape) if kind == "gather" else (lambda s: (3,) + s.shape[1:])
    lands = [lax.empty(lead(s), s.dtype) for s in srcs]

    def body(*refs):
        src, land = refs[:n], refs[n:2 * n]
        send_sems, recv_sems, token = refs[2 * n], refs[2 * n + 1], refs[-1]
        for cp in _ici_copies(src, land, send_sems, recv_sems, jobs, kind):
            cp.start()
        token[...] = jnp.zeros_like(token)

    hbm = [pltpu.HBM(a.shape, a.dtype) for a in srcs + lands]
    outs = pl.pallas_call(
        body, name=name,
        out_shape=[pltpu.SemaphoreType.DMA((3 * len(jobs),)), pltpu.SemaphoreType.DMA((3 * len(jobs),))] + hbm
        + [jax.ShapeDtypeStruct((8, 128), F32)],
        in_specs=[_HBM] * (2 * n), out_specs=[_SEM, _SEM] + [_HBM] * (2 * n) + [pl.BlockSpec(memory_space=pltpu.VMEM)],
        input_output_aliases={i: 2 + i for i in range(2 * n)},
        compiler_params=pltpu.CompilerParams(has_side_effects=_DATAFLOW),
    )(*[pltpu.with_memory_space_constraint(a, pltpu.HBM) for a in srcs + lands])
    return (outs[0], outs[1], list(outs[2:2 + n]), list(outs[2 + n:2 + 2 * n]), nchs, kind), outs[-1]


def _ici_wait(handle, after, name):
    send_sems, recv_sems, srcs, lands, nchs, kind = handle
    n = len(srcs)
    jobs = _ici_jobs(srcs, nchs, kind)

    def body(*refs):
        src, land = refs[:n], refs[n:2 * n]
        for cp in _ici_copies(src, land, refs[2 * n], refs[2 * n + 1], jobs, kind):
            cp.wait_send()
            cp.wait_recv()

    outs = pl.pallas_call(
        body, name=name, out_shape=[pltpu.HBM(a.shape, a.dtype) for a in srcs + lands],
        in_specs=[_HBM] * (2 * n) + [_SEM, _SEM, _ANY], out_specs=[_HBM] * (2 * n),
        input_output_aliases={i: i for i in range(2 * n)},
        compiler_params=pltpu.CompilerParams(has_side_effects=_DATAFLOW),
    )(*srcs, *lands, send_sems, recv_sems, after)
    return list(outs[:n]), list(outs[n:])


def _pass_to_sibling(gathered, nchs, name):
    n = len(gathered)
    jobs = _ici_jobs([jax.ShapeDtypeStruct(g.shape[1:], g.dtype) for g in gathered], nchs, "gather")

    def body(*refs):
        src, dst, (send_sems, recv_sems) = refs[:n], refs[n:2 * n], refs[2 * n:]
        x, y, c = _place()
        slots = [2 * (1 - x) + y, 2 * x + (1 - y), 2 * (1 - x) + (1 - y)]

        def copy(k, j, pc):
            a, h, r0, rows = jobs[k]
            at = pl.ds(pl.multiple_of(pc * h + r0, 16), rows)
            return pltpu.make_async_remote_copy(
                src_ref=src[a].at[slots[j], at, :], dst_ref=dst[a].at[slots[j], at, :], send_sem=send_sems.at[3 * k + j],
                recv_sem=recv_sems.at[3 * k + j], device_id=(x, y, 1 - c), device_id_type=MESH)

        pairs = [(k, j) for k in range(len(jobs)) for j in range(3)]
        for k, j in pairs:
            copy(k, j, c).start()
        for k, j in pairs:
            copy(k, j, c).wait_send()
            copy(k, j, 1 - c).wait_recv()

    return pl.pallas_call(
        body, name=name, out_shape=[jax.ShapeDtypeStruct(g.shape, g.dtype) for g in gathered],
        in_specs=[_ANY] * n, out_specs=[_ANY] * n, input_output_aliases={i: i for i in range(n)},
        scratch_shapes=[pltpu.SemaphoreType.DMA((3 * len(jobs),)), pltpu.SemaphoreType.DMA((3 * len(jobs),))])(*gathered)


def _add4(own, got, name):
    rows, w = own.shape
    tr = _pick(rows, 128, 16)

    def body(a_ref, b_ref, o_ref):
        o_ref[...] = ((a_ref[...].astype(F32) + b_ref[0].astype(F32)) + b_ref[1].astype(F32)) + b_ref[2].astype(F32)

    return pl.pallas_call(
        body, name=name, grid=(rows // tr,),
        in_specs=[pl.BlockSpec((tr, w), lambda i: (i, 0)), pl.BlockSpec((3, tr, w), lambda i: (0, i, 0))],
        out_specs=pl.BlockSpec((tr, w), lambda i: (i, 0)), out_shape=jax.ShapeDtypeStruct((rows, w), F32),
        compiler_params=_params(_PAR))(own, got)


def _small_sync(gs, ws, ms, vs):
    rows = gs.shape[0]
    vmem = pl.BlockSpec(memory_space=pltpu.VMEM)

    def body(g_ref, w_ref, m_ref, v_ref, sum_ref, d_ref, m2_ref, v2_ref, buf, send_sems, recv_sems):
        x, y, c = _place()
        me = 4 * x + 2 * y + c
        buf[me] = g_ref[...]
        copies = []
        for k in range(1, 8):
            peer = (x ^ (k >> 2), y ^ ((k >> 1) & 1), c ^ (k & 1))
            copies.append(pltpu.make_async_remote_copy(
                src_ref=g_ref, dst_ref=buf.at[me], send_sem=send_sems.at[k - 1], recv_sem=recv_sems.at[k - 1],
                device_id=peer, device_id_type=MESH))
        for cp in copies:
            cp.start()
        for cp in copies:
            cp.wait()
        total = buf[0]
        for i in range(1, 8):
            total = total + buf[i]
        sum_ref[...] = total
        d, m2, v2 = _adam_math(w_ref[...], total, m_ref[...], v_ref[...])
        d_ref[...] = d
        m2_ref[...] = m2
        v2_ref[...] = v2

    shape = jax.ShapeDtypeStruct((rows, 128), F32)
    return pl.pallas_call(
        body, name="small_sync", out_shape=[shape] * 4, in_specs=[vmem] * 4, out_specs=[vmem] * 4,
        scratch_shapes=[pltpu.VMEM((8, rows, 128), F32), pltpu.SemaphoreType.DMA((7,)),
                        pltpu.SemaphoreType.DMA((7,))])(gs, ws, ms, vs)


_GROUPS = {
    "ffn1": dict(cols=("ffn1_w_in", 1408), rows=(("ffn1_w_out", 704, 704),), chunks=(8, 2)),
    "ffn2": dict(cols=("ffn2_w_in", 1408), rows=(("ffn2_w_out", 704, 704),), chunks=(8, 2)),
    "mixer": dict(cols=("w_in", 3080), chunks=(8, 4),
                  rows=(("w_branch_hgrn", 256, 256), ("w_branch_gdn", 512, 512), ("w_out", 256, 256),
                        ("gdn_conv_w", CONV_K, 128))),
}
_BIG_NAMES = tuple(n for g in _GROUPS.values() for n in (g["cols"][0],) + tuple(r[0] for r in g["rows"]))


def _group_names(group):
    return (group["cols"][0],) + tuple(r[0] for r in group["rows"])


def _pack(parts, lead, group):
    ax = len(lead)
    rows = []
    for n, r, padded in group["rows"]:
        p = parts[n]
        if padded != r:
            p = jnp.tile(p, (1,) * ax + (padded // r, 1))
        rows.append(p)
    return [parts[group["cols"][0]], rows[0] if len(rows) == 1 else jnp.concatenate(rows, axis=ax)]


def _unpack(cols, rows, group):
    out, off = {group["cols"][0]: cols}, 0
    for n, r, padded in group["rows"]:
        out[n] = rows[..., off:off + r, :]
        off += padded
    return out


def _is_col_sharded(name):
    return name in ("ffn1_w_in", "ffn2_w_in", "w_in", "gdn_conv_w")


def _full_from_shards(name, g):
    if _is_col_sharded(name):
        return jnp.transpose(g, (1, 0, 2)).reshape(g.shape[1], -1)
    return g.reshape(-1, g.shape[2])


def _shards_from_full(name, full):
    if _is_col_sharded(name):
        return jnp.transpose(full.reshape(full.shape[0], 4, -1), (1, 0, 2))
    return full.reshape(4, -1, full.shape[1])


_SMALL = (("ffn1_norm", 8), ("mix_norm", 8), ("hgrn_lb_logits", 16), ("hgrn_out_norm", 8), ("gdn_a_log", 8),
          ("gdn_dt_bias", 8), ("gdn_out_norm", 8), ("ffn2_norm", 8), ("final_norm", 8), ("loss", 8))
_SMALL_ROWS = sum(r for _, r in _SMALL)


def _pack_small(parts):
    out = []
    for name, rows in _SMALL:
        p = parts[name].reshape(-1).astype(F32)
        if p.shape[0] <= 128:
            if p.shape[0] < 128:
                p = jnp.concatenate([p, jnp.zeros((128 - p.shape[0],), F32)])
            p = jnp.broadcast_to(p.reshape(1, 128), (rows, 128))
        out.append(p.reshape(rows, 128))
    return jnp.concatenate(out, axis=0)


def _unpack_small(packed, shapes):
    out, off = {}, 0
    for name, rows in _SMALL:
        n = int(np.prod(shapes[name]))
        out[name] = packed[off:off + rows].reshape(-1)[:n].reshape(shapes[name])
        off += rows
    return out


def _ffn_fwd(x, gain, w_in, w_out, tag):
    n = _rmsnorm_fwd(x, gain, tag + "_norm")
    a, b, hm = _ffn_in_act(n, w_in, tag + "_in")
    out = _mm(hm, w_out, alpha=0.5, res=x, name=tag + "_out")
    return out, (n, a, b)


def _ffn_bwd(x, gain, w_in, w_out, saved, dout, dout_bf, tag):
    n, a, b = saved
    da, db, hm = _ffn_dact(dout_bf, w_out, a, b, tag + "_dact")
    dw_out = _mm(hm, dout_bf, ta=True, alpha=0.5, out_dtype=BF16, name=tag + "_dwout")
    dwa = _mm(n, da, ta=True, out_dtype=BF16, name=tag + "_dwin_a")
    dwb = _mm(n, db, ta=True, out_dtype=BF16, name=tag + "_dwin_b")
    half = D_FF // 2
    dw_in = jnp.stack([dwa[:, :half], dwa[:, half:], dwb[:, :half], dwb[:, half:]])
    dn = _mm(da, w_in, tb=True, name=tag + "_dnorm_a")
    dn = _mm(db, w_in, tb=True, res=dn, b_from=D_FF, name=tag + "_dnorm_b")
    dx, dx_bf, dgain = _rmsnorm_bwd(x, gain, dn, dout, tag + "_dx")
    return dx, dx_bf, dgain, dw_in, dw_out


def _pad_lanes(v):
    return jnp.concatenate([v.reshape(1, -1), jnp.zeros((1, HEAD - v.size), F32)], axis=1)


def _local_step(x, tgt, small, exchange):
    hg_c = _hg_consts()
    gd_c = _gd_consts()
    alog = _pad_lanes(small["gdn_a_log"])
    dtb = _pad_lanes(small["gdn_dt_bias"])
    logits = small["hgrn_lb_logits"]
    hg_gain = small["hgrn_out_norm"].reshape(1, HEAD)
    gd_gain = small["gdn_out_norm"].reshape(1, HEAD)
    g1, gm, g2 = small["ffn1_norm"].reshape(1, -1), small["mix_norm"].reshape(1, -1), small["ffn2_norm"].reshape(1, -1)
    gf = small["final_norm"].reshape(1, -1)
    qscale = HEAD ** -0.5

    w1 = exchange.weights("ffn1")
    started = exchange.prefetch("mixer")
    h1, ffn1_saved = _ffn_fwd(x, g1 + started, w1["ffn1_w_in"], w1["ffn1_w_out"], "ffn1")
    u = _rmsnorm_fwd(h1, gm, "mix_norm")
    w = exchange.weights("mixer", after=u)
    started = exchange.prefetch("ffn2")
    seg, off = {}, 0
    for name, size in zip(IN_NAMES, IN_SIZES):
        seg[name] = w["w_in"][:, off:off + size]
        off += size
    w_gab = jnp.concatenate([seg["ga"], seg["gb"], jnp.zeros((D_MODEL, HEAD - 32), BF16)], axis=1)
    big_segs = [n for n in IN_NAMES if n not in ("ga", "gb")]
    conv8 = jnp.concatenate([w["gdn_conv_w"].astype(F32), jnp.zeros((8 - CONV_K, 4096), F32)], axis=0)
    conv_q, conv_k, conv_v = conv8[:, :1024], conv8[:, 1024:2048], conv8[:, 2048:]
    w_main = jnp.concatenate([seg[n] for n in big_segs], axis=1)
    proj = _mm(u, w_main, name="proj")
    pr, off = {}, 0
    for n in big_segs:
        pr[n] = _view(proj, off, seg[n].shape[1])
        off += seg[n].shape[1]
    gab = _mm(u, w_gab, name="proj_gab")
    oh_raw, oh, s_h = _hgrn_fwd(pr["hq"], pr["hf"], pr["hi"], pr["hg"], logits, hg_gain + started, hg_c)
    qn = _conv_fwd(pr["gq"], conv_q, qscale, "conv_q")
    kn = _conv_fwd(pr["gk"], conv_k, 1.0, "conv_k")
    cv = _conv_fwd(pr["gv"], conv_v, None, "conv_v")
    og_raw, og, s_g, t_g = _gdn_fwd(qn, kn, cv, gab, pr["gz"], alog, dtb, gd_gain, gd_c)
    yh = _mm(oh, w["w_branch_hgrn"], name="branch_h")
    yg = _mm(og, w["w_branch_gdn"], name="branch_g")
    ym = _merge_fwd(yh, yg, pr["gate_h"], pr["gate_g"])
    h2 = _mm(ym, w["w_out"], res=h1, name="mix_out")
    w2 = exchange.weights("ffn2", after=h2)
    h3, ffn2_saved = _ffn_fwd(h2, g2, w2["ffn2_w_in"], w2["ffn2_w_out"], "ffn2")
    loss, dh3, dh3_bf, d_gf = _final_loss(h3, gf, tgt)

    dh2, dh2_bf, d_g2, d_f2in, d_f2out = _ffn_bwd(h2, g2, w2["ffn2_w_in"], w2["ffn2_w_out"], ffn2_saved, dh3, dh3_bf,
                                                  "ffn2")
    started = exchange.reduce("ffn2", {"ffn2_w_in": d_f2in, "ffn2_w_out": d_f2out}, behind=True)
    dym =_mm(dh2_bf, w["w_out"], tb=True, name="d_merge")
    d_wout = _mm(ym, dh2_bf, ta=True, out_dtype=BF16, name="d_w_out")
    dyh, dyg, d_gate_h, d_gate_g = _merge_bwd(dym, yh, yg, pr["gate_h"], pr["gate_g"])
    d_wbh = _mm(oh, dyh, ta=True, out_dtype=BF16, name="d_w_branch_h")
    d_wbg = _mm(og, dyg, ta=True, out_dtype=BF16, name="d_w_branch_g")
    doh = _mm(dyh, w["w_branch_hgrn"], tb=True, name="d_oh")
    dog = _mm(dyg, w["w_branch_gdn"], tb=True, name="d_og")
    d_hq, d_hf, d_hi, d_hg, d_hg_gain, d_lb0 = _hgrn_bwd(pr["hq"], pr["hf"], pr["hi"], pr["hg"], logits,
                                                        hg_gain + started, oh_raw, s_h, doh, hg_c)
    d_qn, d_kn, d_cv, d_gab_wide, d_gz, gd_small = _gdn_bwd(qn, kn, cv, gab, pr["gz"], alog, dtb, gd_gain, og_raw,
                                                            s_g, t_g, dog, gd_c)
    d_gab = _fold_groups(d_gab_wide)
    dc_q, dwc_q = _conv_bwd_a(pr["gq"], conv_q, d_qn, qscale, "dconv_q")
    dc_k, dwc_k = _conv_bwd_a(pr["gk"], conv_k, d_kn, 1.0, "dconv_k")
    dc_v, dwc_v = _conv_bwd_a(pr["gv"], conv_v, d_cv, None, "dconv_v")
    d_gq = _conv_bwd_b(dc_q, conv_q, "dconvx_q")
    d_gk = _conv_bwd_b(dc_k, conv_k, "dconvx_k")
    d_gv = _conv_bwd_b(dc_v, conv_v, "dconvx_v")
    dpr = {"hq": d_hq, "hf": d_hf, "hi": d_hi, "hg": d_hg, "gq": d_gq, "gk": d_gk, "gv": d_gv, "gz": d_gz,
           "gate_h": d_gate_h, "gate_g": d_gate_g}
    dproj = jnp.concatenate([dpr[n] for n in big_segs], axis=1)
    du = _mm(d_gab, w_gab, tb=True, name="du_gab")
    du = _mm(dproj, w_main, tb=True, res=du, name="du")
    d_wmain = _mm(u, dproj, ta=True, out_dtype=BF16, name="dw_main")
    d_wgab = _mm(u, d_gab, ta=True, out_dtype=BF16, name="dw_gab")
    cut = IN_WIDTH // 4
    d_win = jnp.stack([d_wmain[:, :cut], d_wmain[:, cut:2 * cut],
                       jnp.concatenate([d_wmain[:, 2 * cut:8192], d_wgab[:, :32], d_wmain[:, 8192:3 * cut - 32]], axis=1),
                       d_wmain[:, 3 * cut - 32:]])
    d_conv = jnp.concatenate([dwc_q[:CONV_K], dwc_k[:CONV_K], dwc_v[:CONV_K]], axis=1).astype(BF16)
    started = exchange.reduce("mixer", {"w_in": d_win, "gdn_conv_w": d_conv, "w_branch_hgrn": d_wbh,
                                        "w_branch_gdn": d_wbg, "w_out": d_wout}, behind=True)
    dh1, dh1_bf, d_gm = _rmsnorm_bwd(h1, gm + started, du, dh2, "mix_dnorm")
    dx, _, d_g1, d_f1in, d_f1out = _ffn_bwd(x, g1, w1["ffn1_w_in"], w1["ffn1_w_out"], ffn1_saved, dh1, dh1_bf, "ffn1")
    exchange.reduce("ffn1", {"ffn1_w_in": d_f1in, "ffn1_w_out": d_f1out}, behind=True)
    d_lb0 = d_lb0.reshape(1, -1)
    sm = {"ffn1_norm": d_g1, "mix_norm": d_gm, "hgrn_lb_logits": jnp.concatenate([d_lb0, -d_lb0], axis=0),
          "hgrn_out_norm": d_hg_gain, "gdn_a_log": gd_small[2, :16], "gdn_dt_bias": gd_small[1, :16],
          "gdn_out_norm": gd_small[0], "ffn2_norm": d_g2, "final_norm": d_gf, "loss": loss[0, :1]}
    return dx, sm


class _Exchange:
    def __init__(self, wts):
        self.wts = wts
        xi, yi, ci = _place()
        self.chip = 2 * xi + yi
        self.south = ci == 0
        self.core = ci.reshape(1).astype(jnp.int32)
        self.mine = {}
        self.coming = {}
        self.going = {}

    def _packs(self, tag):
        group = _GROUPS[tag]
        return _pack({n: self.wts[n][0].astype(BF16) for n in _group_names(group)}, (), group)

    def prefetch(self, tag):
        packs = self._packs(tag)
        handle, token = _ici_start(packs, _GROUPS[tag]["chunks"], "gather", "gather_start_" + tag)
        self.coming[tag] = handle
        return token[0:1, 0:1]

    def weights(self, tag, after=None):
        group = _GROUPS[tag]
        if tag in self.coming:
            packs, halves = _ici_wait(self.coming.pop(tag), after, "gather_wait_" + tag)
            others = _pass_to_sibling(halves, group["chunks"], "gather_pass_" + tag)
        else:
            packs = self._packs(tag)
            others = _gather_weights(packs, group["chunks"], "gather_" + tag)
        whole = [lax.dynamic_update_index_in_dim(g, p, self.chip, 0) for g, p in zip(others, packs)]
        gathered = _unpack(*whole, group)
        return {n: _full_from_shards(n, gathered[n]) for n in _group_names(group)}

    def reduce(self, tag, grads, behind=False):
        group = _GROUPS[tag]
        shards = {n: (grads[n] if grads[n].ndim == 3 else _shards_from_full(n, grads[n])) for n in _group_names(group)}
        gpacks = _pack(shards, (4,), group)
        got = _swap_with_sibling(gpacks, group["chunks"], 4, "reduce_pair_" + tag, halves=True)
        sums = [_add2(a, b, self.core, "add_pair_%s_%d" % (tag, i)) for i, (a, b) in enumerate(zip(gpacks, got))]
        if behind:
            handle, token = _ici_start(sums, group["chunks"], "reduce", "reduce_start_" + tag)
            self.going[tag] = handle
            self.token = token
            return token[0:1, 0:1]
        self._add_chips(tag, sums, _reduce_chips(sums, group["chunks"], "reduce_chips_" + tag))
        return None

    def _add_chips(self, tag, sums, from_chips):
        self.mine[tag] = [_add4(lax.dynamic_index_in_dim(s, self.chip, axis=0, keepdims=False), f,
                                "add_chips_%s_%d" % (tag, i)) for i, (s, f) in enumerate(zip(sums, from_chips))]

    def finish(self, tags, after):
        for tag in tags:
            if tag in self.going:
                self._add_chips(tag, *_ici_wait(self.going.pop(tag), after, "reduce_wait_" + tag))
        mine = [a for t in tags for a in self.mine[t]]
        nchs = [k for t in tags for k in _GROUPS[t]["chunks"]]
        theirs = _swap_with_sibling(mine, nchs, 0, "share_pair_" + tags[0])
        whole = [jnp.concatenate([jnp.where(self.south, a, b), jnp.where(self.south, b, a)], axis=0)
                 for a, b in zip(mine, theirs)]
        reduced = {}
        for i, t in enumerate(tags):
            reduced.update(_unpack(whole[2 * i], whole[2 * i + 1], _GROUPS[t]))
        return reduced


_WEIGHTS = ("ffn1_norm", "ffn1_w_in", "ffn1_w_out", "mix_norm", "w_in", "hgrn_lb_logits", "hgrn_out_norm",
            "gdn_conv_w", "gdn_a_log", "gdn_dt_bias", "gdn_out_norm", "w_branch_hgrn", "w_branch_gdn", "w_out",
            "ffn2_norm", "ffn2_w_in", "ffn2_w_out", "final_norm")


def kernel(x, ffn1_norm, ffn1_w_in, ffn1_w_out, mix_norm, w_in, hgrn_lb_logits, hgrn_out_norm, gdn_conv_w, gdn_a_log, gdn_dt_bias, gdn_out_norm, w_branch_hgrn, w_branch_gdn, w_out, ffn2_norm, ffn2_w_in, ffn2_w_out, final_norm, loss_target, m_ffn1_norm, m_ffn1_w_in, m_ffn1_w_out, m_mix_norm, m_w_in, m_hgrn_lb_logits, m_hgrn_out_norm, m_gdn_conv_w, m_gdn_a_log, m_gdn_dt_bias, m_gdn_out_norm, m_w_branch_hgrn, m_w_branch_gdn, m_w_out, m_ffn2_norm, m_ffn2_w_in, m_ffn2_w_out, m_final_norm, v_ffn1_norm, v_ffn1_w_in, v_ffn1_w_out, v_mix_norm, v_w_in, v_hgrn_lb_logits, v_hgrn_out_norm, v_gdn_conv_w, v_gdn_a_log, v_gdn_dt_bias, v_gdn_out_norm, v_w_branch_hgrn, v_w_branch_gdn, v_w_out, v_ffn2_norm, v_ffn2_w_in, v_ffn2_w_out, v_final_norm):
    args = dict(locals())
    wts = {n: args[n] for n in _WEIGHTS}
    moms = {n: args["m_" + n] for n in _WEIGHTS}
    vars_ = {n: args["v_" + n] for n in _WEIGHTS}

    small = {n: wts[n].astype(F32) for n in _WEIGHTS if n not in _BIG_NAMES}
    exchange = _Exchange(wts)
    dx, small_grads = _local_step(x[0], loss_target[0], small, exchange)

    out_g, out_d, out_m, out_v = {}, {}, {}, {}

    def update(tags, reduced, after):
        for t in tags:
            for n in _group_names(_GROUPS[t]):
                shape = wts[n].shape
                w2 = wts[n].reshape(shape[-2], shape[-1])
                g2 = reduced[n]
                d, m2, v2 = _adamw(w2, g2, moms[n].reshape(w2.shape), vars_[n].reshape(w2.shape), "adamw_" + n, after)
                out_g[n], out_d[n], out_m[n], out_v[n] = (g2.reshape(shape), d.reshape(shape), m2.reshape(shape),
                                                          v2.reshape(shape))
                after = v2
        return after

    done = update(("ffn2", "mixer"), exchange.finish(("ffn2", "mixer"), after=dx), exchange.token)
    update(("ffn1",), exchange.finish(("ffn1",), after=done), None)

    small_names = [n for n, _ in _SMALL]
    zero = jnp.zeros((1,), F32)
    shapes = {n: (wts[n].shape if n != "loss" else (1,)) for n in small_names}
    sums, sd, sm_, sv = _small_sync(
        _pack_small(small_grads),
        _pack_small({n: (wts[n] if n != "loss" else zero) for n in small_names}),
        _pack_small({n: (moms[n] if n != "loss" else zero) for n in small_names}),
        _pack_small({n: (vars_[n] if n != "loss" else zero) for n in small_names}))
    sg_u, sd_u, sm_u, sv_u = (_unpack_small(p, shapes) for p in (sums, sd, sm_, sv))
    for n in small_names:
        if n != "loss":
            out_g[n], out_d[n], out_m[n], out_v[n] = sg_u[n], sd_u[n], sm_u[n], sv_u[n]
    loss = sg_u["loss"].reshape(())

    return (loss, dx[None], *[out_g[n] for n in _WEIGHTS], *[out_d[n] for n in _WEIGHTS],
            *[out_m[n] for n in _WEIGHTS], *[out_v[n] for n in _WEIGHTS])
```

```python
import numpy as np

import jax
import jax.numpy as jnp
from jax import lax
from jax.experimental import pallas as pl
from jax.experimental.pallas import tpu as pltpu

F32 = jnp.float32
BF16 = jnp.bfloat16

D_MODEL = 1024
D_FF = 2816
CHUNK = 64
HEAD = 128
HG_HEADS = 8
GD_HEADS = 16
HPS = 8
COMM_CHUNKS = 9
MM_TM = 1408
MM_TN = 1024
MM_TK = 1536
VMEM_LIMIT = 48 * 1024 * 1024
EPS = 1e-6
CONV_K = 4
IN_NAMES = ("hq", "hf", "hi", "hg", "gq", "gk", "gv", "ga", "gb", "gz", "gate_h", "gate_g")
IN_SIZES = (1024, 1024, 1024, 1024, 1024, 1024, 2048, 16, 16, 2048, 1024, 1024)
IN_WIDTH = sum(IN_SIZES)

ADAM_LR = 0.001
ADAM_B1 = 0.9
ADAM_B2 = 0.999
ADAM_EPS = 1e-08
ADAM_WD = 0.01
ADAM_STEP = 10

MESH = pl.DeviceIdType.MESH
_ARB = "arbitrary"
_PAR = "parallel"


def _bf(x):
    return x.astype(BF16)


def _dot(a, b):
    return jnp.dot(_bf(a), _bf(b), preferred_element_type=F32)


def _dot_nt(a, b):
    return lax.dot_general(_bf(a), _bf(b), (((1,), (1,)), ((), ())), preferred_element_type=F32)


def _dot_tn(a, b):
    return lax.dot_general(_bf(a), _bf(b), (((0,), (0,)), ((), ())), preferred_element_type=F32)


def _split3(x):
    hi = _bf(x)
    r = x - hi.astype(F32)
    mid = _bf(r)
    lo = _bf(r - mid.astype(F32))
    return hi, mid, lo


def _dot_mx(m, x):
    hi, mid, lo = _split3(x)
    return (jnp.dot(m, hi, preferred_element_type=F32) + jnp.dot(m, mid, preferred_element_type=F32)
            + jnp.dot(m, lo, preferred_element_type=F32))


def _dot_xm(x, m):
    hi, mid, lo = _split3(x)
    return (jnp.dot(hi, m, preferred_element_type=F32) + jnp.dot(mid, m, preferred_element_type=F32)
            + jnp.dot(lo, m, preferred_element_type=F32))


def _dot_hp(a, b):
    ah = _bf(a)
    al = _bf(a - ah.astype(F32))
    bh = _bf(b)
    bl = _bf(b - bh.astype(F32))
    return (jnp.dot(ah, bh, preferred_element_type=F32) + jnp.dot(ah, bl, preferred_element_type=F32)
            + jnp.dot(al, bh, preferred_element_type=F32))


def _sigmoid(x):
    return jax.nn.sigmoid(x)


def _silu(x):
    return x * _sigmoid(x)


def _dsilu(x):
    s = _sigmoid(x)
    return s * (1.0 + x * (1.0 - s))


def _softplus(x):
    return jnp.maximum(x, 0.0) + jnp.log(1.0 + jnp.exp(-jnp.abs(x)))


def _rowsum(x):
    return jnp.sum(x, axis=1, keepdims=True)


def _col_to_row(col, eye):
    return jnp.sum(eye * col, axis=0, keepdims=True)


def _row_to_col(row, eye):
    return jnp.sum(eye * row, axis=1, keepdims=True)


def _pick(dim, pref, unit=128):
    if dim <= pref:
        return dim
    t = pref
    while t >= unit:
        if dim % t == 0:
            return t
        t -= unit
    return dim


def _params(*sem):
    return pltpu.CompilerParams(dimension_semantics=tuple(sem), vmem_limit_bytes=VMEM_LIMIT)


def _mm(a, b, *, ta=False, tb=False, alpha=1.0, res=None, out_dtype=F32, name="mm", b_from=0):
    m = a.shape[1] if ta else a.shape[0]
    k = a.shape[0] if ta else a.shape[1]
    n = b.shape[0] if tb else b.shape[1]
    assert b_from + k <= (b.shape[1] if tb else b.shape[0])
    tm, tn, tk = _pick(m, MM_TM), _pick(n, MM_TN), _pick(k, MM_TK)
    if tn < MM_TN < n and n % MM_TM == 0:
        tn = MM_TM
    nk = k // tk
    assert b_from % tk == 0
    b0 = b_from // tk
    a_spec = pl.BlockSpec((tk, tm), lambda i, j, l: (l, i)) if ta else pl.BlockSpec((tm, tk), lambda i, j, l: (i, l))
    b_spec = (pl.BlockSpec((tn, tk), lambda i, j, l: (j, b0 + l)) if tb
              else pl.BlockSpec((tk, tn), lambda i, j, l: (b0 + l, j)))
    o_spec = pl.BlockSpec((tm, tn), lambda i, j, l: (i, j))
    dims = (((0 if ta else 1,), (1 if tb else 0,)), ((), ()))
    has_res = res is not None

    def finish(r, r_ref, o_ref):
        if alpha != 1.0:
            r = r * alpha
        if has_res:
            r = r + r_ref[...]
        o_ref[...] = r.astype(out_dtype)

    def body(*refs):
        a_ref, b_ref = refs[0], refs[1]
        r_ref = refs[2] if has_res else None
        o_ref = refs[3] if has_res else refs[2]
        part = lax.dot_general(_bf(a_ref[...]), _bf(b_ref[...]), dims, preferred_element_type=F32)
        if nk == 1:
            finish(part, r_ref, o_ref)
            return
        acc = refs[-1]
        step = pl.program_id(2)

        @pl.when(step == 0)
        def _():
            acc[...] = part

        @pl.when(step != 0)
        def _():
            acc[...] += part

        @pl.when(step == nk - 1)
        def _():
            finish(acc[...], r_ref, o_ref)

    ins = [a, b] + ([res] if has_res else [])
    in_specs = [a_spec, b_spec] + ([o_spec] if has_res else [])
    return pl.pallas_call(
        body, name=name, grid=(m // tm, n // tn, nk), in_specs=in_specs, out_specs=o_spec,
        out_shape=jax.ShapeDtypeStruct((m, n), out_dtype),
        scratch_shapes=[pltpu.VMEM((tm, tn), F32)] if nk > 1 else [],
        compiler_params=_params(_PAR, _PAR, _ARB))(*ins)


def _row_spec(tr, w):
    return pl.BlockSpec((tr, w), lambda i: (i, 0))


def _full_spec(shape):
    return pl.BlockSpec(shape, lambda i: tuple(0 for _ in shape))


def _view(arr, off, width):
    return arr, off, width


def _view_rows(view, tr):
    _, off, width = view
    assert off % width == 0
    return pl.BlockSpec((tr, width), lambda i: (i, off // width))


def _view_tile(view, rows, bw, cidx=lambda c: c):
    _, off, width = view
    assert off % bw == 0 and width % bw == 0
    return pl.BlockSpec((rows, bw), lambda c, g: (cidx(c), off // bw + g))


def _rmsnorm_fwd(x, g, name):
    t, d = x.shape
    tr = _pick(t, 256, 8)

    def body(x_ref, g_ref, o_ref):
        xv = x_ref[...]
        r = lax.rsqrt(jnp.mean(xv * xv, axis=1, keepdims=True) + EPS)
        o_ref[...] = (xv * r * g_ref[...]).astype(BF16)

    return pl.pallas_call(
        body, name=name, grid=(t // tr,), in_specs=[_row_spec(tr, d), _full_spec((1, d))],
        out_specs=_row_spec(tr, d), out_shape=jax.ShapeDtypeStruct((t, d), BF16),
        compiler_params=_params(_PAR))(x, g)


def _rmsnorm_bwd(x, g, dn, res, name):
    t, d = x.shape
    tr = _pick(t, 256, 8)

    def body(x_ref, g_ref, dn_ref, r_ref, dx_ref, dxb_ref, dg_ref):
        @pl.when(pl.program_id(0) == 0)
        def _():
            dg_ref[...] = jnp.zeros_like(dg_ref)

        xv = x_ref[...]
        r = lax.rsqrt(jnp.mean(xv * xv, axis=1, keepdims=True) + EPS)
        xh = xv * r
        dy = dn_ref[...]
        dg_ref[...] += jnp.sum(dy * xh, axis=0, keepdims=True)
        dxh = dy * g_ref[...]
        dx = r_ref[...] + r * (dxh - xh * jnp.mean(dxh * xh, axis=1, keepdims=True))
        dx_ref[...] = dx
        dxb_ref[...] = dx.astype(BF16)

    return pl.pallas_call(
        body, name=name, grid=(t // tr,),
        in_specs=[_row_spec(tr, d), _full_spec((1, d)), _row_spec(tr, d), _row_spec(tr, d)],
        out_specs=[_row_spec(tr, d), _row_spec(tr, d), _full_spec((1, d))],
        out_shape=[jax.ShapeDtypeStruct((t, d), F32), jax.ShapeDtypeStruct((t, d), BF16),
                   jax.ShapeDtypeStruct((1, d), F32)],
        compiler_params=_params(_ARB))(x, g, dn, res)


FFN_TN = 1408
FFN_TM = 512


def _ffn_pieces():
    return [slice(c, min(c + 256, FFN_TN)) for c in range(0, FFN_TN, 256)]


def _ffn_in_act(n, w_in, name):
    t, d = n.shape
    tm = _pick(t, FFN_TM)
    nf = D_FF // FFN_TN

    def body(n_ref, wa_ref, wb_ref, a_ref, b_ref, hm_ref):
        nv = n_ref[...]
        for cols in _ffn_pieces():
            a = jnp.dot(nv, wa_ref[:, cols], preferred_element_type=F32)
            b = jnp.dot(nv, wb_ref[:, cols], preferred_element_type=F32)
            a_ref[:, cols] = a.astype(BF16)
            b_ref[:, cols] = b.astype(BF16)
            hm_ref[:, cols] = (_silu(a) * b).astype(BF16)

    tile = pl.BlockSpec((tm, FFN_TN), lambda i, j: (i, j))
    return pl.pallas_call(
        body, name=name, grid=(t // tm, nf),
        in_specs=[pl.BlockSpec((tm, d), lambda i, j: (i, 0)), pl.BlockSpec((d, FFN_TN), lambda i, j: (0, j)),
                  pl.BlockSpec((d, FFN_TN), lambda i, j: (0, nf + j))],
        out_specs=[tile, tile, tile], out_shape=[jax.ShapeDtypeStruct((t, D_FF), BF16)] * 3,
        compiler_params=_params(_PAR, _PAR))(n, w_in, w_in)


def _ffn_dact(dout, w_out, a, b, name):
    t, d = dout.shape
    tm = _pick(t, FFN_TM)

    def body(do_ref, w_ref, a_ref, b_ref, da_ref, db_ref, hm_ref):
        dov = do_ref[...]
        for cols in _ffn_pieces():
            dh = 0.5 * _dot_nt(dov, w_ref[cols, :])
            av = a_ref[:, cols].astype(F32)
            bv = b_ref[:, cols].astype(F32)
            sg = _sigmoid(av)
            sa = av * sg
            da_ref[:, cols] = (dh * bv * (sg * (1.0 + av * (1.0 - sg)))).astype(BF16)
            db_ref[:, cols] = (dh * sa).astype(BF16)
            hm_ref[:, cols] = (sa * bv).astype(BF16)

    tile = pl.BlockSpec((tm, FFN_TN), lambda i, j: (i, j))
    return pl.pallas_call(
        body, name=name, grid=(t // tm, D_FF // FFN_TN),
        in_specs=[pl.BlockSpec((tm, d), lambda i, j: (i, 0)), pl.BlockSpec((FFN_TN, d), lambda i, j: (j, 0)), tile, tile],
        out_specs=[tile, tile, tile], out_shape=[jax.ShapeDtypeStruct((t, D_FF), BF16)] * 3,
        compiler_params=_params(_PAR, _PAR))(dout, w_out, a, b)


def _merge_fwd(yh, yg, gh, gg):
    t, d = yh.shape
    tr = _pick(t, 256, 8)

    def body(yh_ref, yg_ref, gh_ref, gg_ref, o_ref):
        o_ref[...] = (_sigmoid(gh_ref[...]) * yh_ref[...] + _sigmoid(gg_ref[...]) * yg_ref[...]).astype(BF16)

    return pl.pallas_call(
        body, name="merge_fwd", grid=(t // tr,),
        in_specs=[_row_spec(tr, d), _row_spec(tr, d), _view_rows(gh, tr), _view_rows(gg, tr)],
        out_specs=_row_spec(tr, d),
        out_shape=jax.ShapeDtypeStruct((t, d), BF16), compiler_params=_params(_PAR))(yh, yg, gh[0], gg[0])


def _merge_bwd(dy, yh, yg, gh, gg):
    t, d = yh.shape
    tr = _pick(t, 256, 8)

    def body(dy_ref, yh_ref, yg_ref, gh_ref, gg_ref, dyh_ref, dyg_ref, dgh_ref, dgg_ref):
        dyv = dy_ref[...]
        sh = _sigmoid(gh_ref[...])
        sg = _sigmoid(gg_ref[...])
        dyh_ref[...] = (dyv * sh).astype(BF16)
        dyg_ref[...] = (dyv * sg).astype(BF16)
        dgh_ref[...] = (dyv * yh_ref[...] * sh * (1.0 - sh)).astype(BF16)
        dgg_ref[...] = (dyv * yg_ref[...] * sg * (1.0 - sg)).astype(BF16)

    return pl.pallas_call(
        body, name="merge_bwd", grid=(t // tr,),
        in_specs=[_row_spec(tr, d)] * 3 + [_view_rows(gh, tr), _view_rows(gg, tr)], out_specs=[_row_spec(tr, d)] * 4,
        out_shape=[jax.ShapeDtypeStruct((t, d), BF16)] * 4,
        compiler_params=_params(_PAR))(dy, yh, yg, gh[0], gg[0])


def _final_loss(h, g, tgt):
    t, d = h.shape
    tr = _pick(t, 256, 8)

    def body(h_ref, g_ref, t_ref, loss_ref, dh_ref, dhb_ref, dg_ref):
        @pl.when(pl.program_id(0) == 0)
        def _():
            dg_ref[...] = jnp.zeros_like(dg_ref)
            loss_ref[...] = jnp.zeros_like(loss_ref)

        xv = h_ref[...]
        gv = g_ref[...]
        r = lax.rsqrt(jnp.mean(xv * xv, axis=1, keepdims=True) + EPS)
        xh = xv * r
        err = xh * gv - t_ref[...]
        loss_ref[...] += 0.5 * jnp.sum(jnp.mean(err * err, axis=1, keepdims=True), axis=0, keepdims=True)
        dy = err * (1.0 / d)
        dg_ref[...] += jnp.sum(dy * xh, axis=0, keepdims=True)
        dxh = dy * gv
        dh = r * (dxh - xh * jnp.mean(dxh * xh, axis=1, keepdims=True))
        dh_ref[...] = dh
        dhb_ref[...] = dh.astype(BF16)

    return pl.pallas_call(
        body, name="final_loss", grid=(t // tr,),
        in_specs=[_row_spec(tr, d), _full_spec((1, d)), _row_spec(tr, d)],
        out_specs=[_full_spec((1, 128)), _row_spec(tr, d), _row_spec(tr, d), _full_spec((1, d))],
        out_shape=[jax.ShapeDtypeStruct((1, 128), F32), jax.ShapeDtypeStruct((t, d), F32),
                   jax.ShapeDtypeStruct((t, d), BF16), jax.ShapeDtypeStruct((1, d), F32)],
        compiler_params=_params(_ARB))(h, g, tgt)


def _hg_consts():
    c = CHUNK
    t = np.arange(c)
    mats, masks = [], []
    for lvl in range(6):
        m = 1 << lvl
        blk = t // m
        mat = np.zeros((c, c), np.float32)
        for tt in range(c):
            b = blk[tt]
            if b % 2 == 1:
                mat[tt, b * m:tt + 1] = 1.0
            else:
                mat[tt, tt + 1:(b + 1) * m] = 1.0
        mats.append(mat)
        same = (t[:, None] // (2 * m)) == (t[None, :] // (2 * m))
        masks.append((same & (blk[:, None] % 2 == 1) & (blk[None, :] % 2 == 0)).astype(np.float32))
    pre = np.tril(np.ones((c, c), np.float32))
    suf = np.triu(np.ones((c, c), np.float32), 1)
    mstack = np.concatenate(mats + [pre, suf], 0)
    masks.append(np.eye(c, dtype=np.float32))
    return (jnp.asarray(mstack, BF16), jnp.asarray(mstack.T.copy(), BF16), jnp.asarray(np.stack(masks), F32),
            jnp.asarray(np.eye(HEAD, dtype=np.float32)))


def _gd_consts():
    c = CHUNK
    incl = np.tril(np.ones((c, c), np.float32))
    strict = np.tril(np.ones((c, c), np.float32), -1)
    eye = np.eye(c, dtype=np.float32)
    masks = np.stack([incl, strict, eye, incl.T.copy()])
    return jnp.asarray(incl, BF16), jnp.asarray(incl.T.copy(), BF16), jnp.asarray(masks, F32)


def _chunks_per_step(nc):
    for cb in (32 // HPS, 2, 1):
        if nc % cb == 0:
            return cb
    return 1


def _hg_prep(hq, hf, lg):
    lb = _sigmoid(lg[0:1, :] - lg[1:2, :])
    sg = _sigmoid(hf)
    sgn = _sigmoid(-hf)
    f = lb + (1.0 - lb) * sg
    lf = jnp.log(f)
    kk = (1.0 - lb) * sgn
    q = _silu(hq) * (HEAD ** -0.5)
    return lb, sg, sgn, f, lf, kk, q


def _mx_each(m, xs):
    wide = [jnp.concatenate(_split3(x), axis=1) for x in xs]
    prods = [jnp.dot(m, w, preferred_element_type=F32) for w in wide]
    return [p[:, :HEAD] + p[:, HEAD:2 * HEAD] + p[:, 2 * HEAD:] for p in prods]


def _hg_scaled(x, ex):
    xb = [_bf(a) for a in x]
    eb = [_bf(e[:6 * CHUNK]) for e in ex]
    return [[a * e[lvl * CHUNK:(lvl + 1) * CHUNK] for lvl in range(6)] for a, e in zip(xb, eb)]


def _hg_scores(q, kk, qe, ke, mask_ref):
    p = [mask_ref[6] * _rowsum(a * b) for a, b in zip(q, kk)]
    for lvl in range(6):
        d = [_dot_nt(a[lvl], b[lvl]) for a, b in zip(qe, ke)]
        p = [x + mask_ref[lvl] * y for x, y in zip(p, d)]
    return p


def _hgrn_fwd(hq, hf, hi, hg, logits, gain, consts):
    t = hq[0].shape[0]
    nc = t // CHUNK
    cb = _chunks_per_step(nc)
    rows = cb * CHUNK
    mstack, _, masks, eye = consts
    tile = pl.BlockSpec((rows, HPS * HEAD), lambda c, g: (c, g))

    def body(hq_ref, hf_ref, hi_ref, hg_ref, lg_ref, gain_ref, m_ref, mask_ref, eye_ref,
             oraw_ref, og_ref, ssave_ref, state):
        c = pl.program_id(0)
        g = pl.program_id(1)

        @pl.when(c == 0)
        def _():
            for hh in range(HPS):
                state[g * HPS + hh] = jnp.zeros((HEAD, HEAD), F32)

        lg_all = lg_ref[...]
        gain_v = gain_ref[...]

        def one(i, carry):
            sl = pl.ds(pl.multiple_of(i * CHUNK, CHUNK), CHUNK)
            hs = range(HPS)
            heads = [g * HPS + hh for hh in hs]
            ln = [slice(hh * HEAD, (hh + 1) * HEAD) for hh in hs]
            preps = [_hg_prep(hq_ref[sl, s], hf_ref[sl, s], lg_all[:, s]) for s in ln]
            lf, kk, q = [p[4] for p in preps], [p[5] for p in preps], [p[6] for p in preps]
            v = [hi_ref[sl, s] for s in ln]
            ex = [jnp.exp(x) for x in _mx_each(m_ref[...], lf)]
            eb = [e[6 * CHUNK:7 * CHUNK] for e in ex]
            esfx = [e[7 * CHUNK:8 * CHUNK] for e in ex]
            qe, ke = _hg_scaled(q, ex), _hg_scaled(kk, ex)
            p = _hg_scores(q, kk, qe, ke, mask_ref)
            s0 = [state[h] for h in heads]
            o = _each(lambda a, e, s, pp, vv: _dot(a * e, s) + _dot(pp, vv), q, eb, s0, p, v)
            eye_v = eye_ref[...]
            s1 = _each(lambda s, e, kx, ef, vv: s * _row_to_col(e[CHUNK - 1:CHUNK, :], eye_v) + _dot_tn(kx * ef, vv),
                       s0, eb, kk, esfx, v)
            for hh in hs:
                ssave_ref[i, hh] = s0[hh]
                state[heads[hh]] = s1[hh]
                oraw_ref[sl, ln[hh]] = o[hh]
                r = lax.rsqrt(jnp.mean(o[hh] * o[hh], axis=1, keepdims=True) + EPS)
                og_ref[sl, ln[hh]] = (o[hh] * r * gain_v * _silu(hg_ref[sl, ln[hh]])).astype(BF16)
            return carry

        lax.fori_loop(0, cb, one, 0, unroll=4)

    return pl.pallas_call(
        body, name="hgrn_fwd", grid=(nc // cb, HG_HEADS // HPS),
        in_specs=[_view_tile(v, rows, HPS * HEAD) for v in (hq, hf, hi, hg)] + [
                  pl.BlockSpec((2, HPS * HEAD), lambda c, g: (0, g)),
                  pl.BlockSpec((1, HEAD), lambda c, g: (0, 0)),
                  pl.BlockSpec(mstack.shape, lambda c, g: (0, 0)),
                  pl.BlockSpec(masks.shape, lambda c, g: (0, 0, 0)),
                  pl.BlockSpec(eye.shape, lambda c, g: (0, 0))],
        out_specs=[tile, tile, pl.BlockSpec((cb, HPS, HEAD, HEAD), lambda c, g: (c, g, 0, 0))],
        out_shape=[jax.ShapeDtypeStruct((t, HG_HEADS * HEAD), F32), jax.ShapeDtypeStruct((t, HG_HEADS * HEAD), BF16),
                   jax.ShapeDtypeStruct((nc, HG_HEADS, HEAD, HEAD), F32)],
        scratch_shapes=[pltpu.VMEM((HG_HEADS, HEAD, HEAD), F32)],
        compiler_params=_params(_ARB, _ARB))(hq[0], hf[0], hi[0], hg[0], logits, gain, mstack, masks, eye)


def _hgrn_bwd(hq, hf, hi, hg, logits, gain, oraw, ssave, dog, consts):
    t = hq[0].shape[0]
    nc = t // CHUNK
    cb = _chunks_per_step(nc)
    rows = cb * CHUNK
    nb = nc // cb
    mstack, mstack_t, masks, eye = consts
    tile = pl.BlockSpec((rows, HPS * HEAD), lambda c, g: (nb - 1 - c, g))

    def body(hq_ref, hf_ref, hi_ref, hg_ref, lg_ref, gain_ref, oraw_ref, ssave_ref, dog_ref, m_ref, mt_ref,
             mask_ref, eye_ref, dhq_ref, dhf_ref, dhi_ref, dhg_ref, dgain_ref, dlb_ref, dstate):
        c = pl.program_id(0)
        g = pl.program_id(1)

        @pl.when(c == 0)
        def _():
            for hh in range(HPS):
                dstate[g * HPS + hh] = jnp.zeros((HEAD, HEAD), F32)

        @pl.when((c == 0) & (g == 0))
        def _():
            dgain_ref[...] = jnp.zeros_like(dgain_ref)
            dlb_ref[...] = jnp.zeros_like(dlb_ref)

        lg_all = lg_ref[...]
        gain_v = gain_ref[...]
        eye_v = eye_ref[...]
        last_row = (lax.broadcasted_iota(jnp.int32, (CHUNK, HEAD), 0) == CHUNK - 1).astype(F32)

        def one(j, carry):
            i = cb - 1 - j
            sl = pl.ds(pl.multiple_of(i * CHUNK, CHUNK), CHUNK)
            hs = range(HPS)
            heads = [g * HPS + hh for hh in hs]
            ln = [slice(hh * HEAD, (hh + 1) * HEAD) for hh in hs]
            hqv = [hq_ref[sl, s] for s in ln]
            hgv = [hg_ref[sl, s] for s in ln]
            preps = [_hg_prep(a, hf_ref[sl, s], lg_all[:, s]) for a, s in zip(hqv, ln)]
            lb, sg, sgn, f, lf, kk, q = ([p[n] for p in preps] for n in range(7))
            v = [hi_ref[sl, s] for s in ln]
            ex = [jnp.exp(x) for x in _mx_each(m_ref[...], lf)]
            eb = [e[6 * CHUNK:7 * CHUNK] for e in ex]
            esfx = [e[7 * CHUNK:8 * CHUNK] for e in ex]
            qe, ke = _hg_scaled(q, ex), _hg_scaled(kk, ex)
            p = _hg_scores(q, kk, qe, ke, mask_ref)
            s0 = [ssave_ref[i, hh] for hh in hs]
            ds = [dstate[h] for h in heads]

            o = [oraw_ref[sl, s] for s in ln]
            r = [lax.rsqrt(jnp.mean(x * x, axis=1, keepdims=True) + EPS) for x in o]
            on = _each(lambda x, y: x * y, o, r)
            dg_out = [dog_ref[sl, s] for s in ln]
            sgate = [_silu(x) for x in hgv]
            for hh in hs:
                dhg_ref[sl, ln[hh]] = (dg_out[hh] * on[hh] * gain_v * _dsilu(hgv[hh])).astype(BF16)
            dgain_ref[...] += sum(jnp.sum(d * s * n, axis=0, keepdims=True) for d, s, n in zip(dg_out, sgate, on))
            don = _each(lambda d, s: d * s * gain_v, dg_out, sgate)
            do = _each(lambda rr, dn, n: rr * (dn - n * jnp.mean(dn * n, axis=1, keepdims=True)), r, don, on)

            dp = _each(_dot_nt, do, v)
            dv = _each(lambda pp, d, kx, ef, s: _dot_tn(pp, d) + _dot(kx * ef, s), p, do, kk, esfx, ds)
            dqb = _each(_dot_nt, do, s0)
            dkx = _each(_dot_nt, v, ds)
            diag = [_rowsum(mask_ref[6] * x) for x in dp]
            dq = _each(lambda a, e, d, kx: a * e + d * kx, dqb, eb, diag, kk)
            dk = _each(lambda a, e, d, qq: a * e + d * qq, dkx, esfx, diag, q)
            dxs = [[] for _ in hs]
            for lvl in range(6):
                el = [e[lvl * CHUNK:(lvl + 1) * CHUNK] for e in ex]
                gm = [mask_ref[lvl] * x for x in dp]
                gm = [_bf(x) for x in gm]
                a1 = _each(lambda m_, kx: _dot(m_, kx[lvl]), gm, ke)
                a2 = _each(lambda m_, qq: _dot_tn(m_, qq[lvl]), gm, qe)
                dq = _each(lambda x, a, e: x + a * e, dq, a1, el)
                dk = _each(lambda x, a, e: x + a * e, dk, a2, el)
                for hh in hs:
                    dxs[hh].append((a1[hh] * q[hh] + a2[hh] * kk[hh]) * el[hh])
            e_end_row = [e[CHUNK - 1:CHUNK, :] for e in eb]
            ds_new = _each(lambda qq, e, d, er, s: _dot_tn(qq * e, d) + _row_to_col(er, eye_v) * s, q, eb, do, e_end_row, ds)
            for hh in hs:
                dstate[heads[hh]] = ds_new[hh]
                dend_row = _col_to_row(_rowsum(s0[hh] * ds[hh]), eye_v)
                dxs[hh].append(dqb[hh] * q[hh] * eb[hh] + last_row * (e_end_row[hh] * dend_row))
                dxs[hh].append(dkx[hh] * kk[hh] * esfx[hh])
            dlf = _mx_each(mt_ref[...], [jnp.concatenate(x, axis=0) for x in dxs])

            for hh in hs:
                dhi_ref[sl, ln[hh]] = dv[hh].astype(BF16)
                dhq_ref[sl, ln[hh]] = (dq[hh] * (HEAD ** -0.5) * _dsilu(hqv[hh])).astype(BF16)
                df = dlf[hh] / f[hh]
                dsig = (1.0 - lb[hh]) * sg[hh] * sgn[hh]
                dhf_ref[sl, ln[hh]] = ((df - dk[hh]) * dsig).astype(BF16)
                dlb_t = jnp.sum(df * sgn[hh] - dk[hh] * sgn[hh], axis=0, keepdims=True)
                dlb_ref[pl.ds(heads[hh], 1), :] += dlb_t * lb[hh] * (1.0 - lb[hh])
            return carry

        lax.fori_loop(0, cb, one, 0, unroll=2)

    outs = [jax.ShapeDtypeStruct((t, HG_HEADS * HEAD), BF16)] * 4 + [
        jax.ShapeDtypeStruct((1, HEAD), F32), jax.ShapeDtypeStruct((HG_HEADS, HEAD), F32)]
    return pl.pallas_call(
        body, name="hgrn_bwd", grid=(nb, HG_HEADS // HPS),
        in_specs=[_view_tile(v, rows, HPS * HEAD, lambda c: nb - 1 - c) for v in (hq, hf, hi, hg)] + [
                  pl.BlockSpec((2, HPS * HEAD), lambda c, g: (0, g)),
                  pl.BlockSpec((1, HEAD), lambda c, g: (0, 0)), tile,
                  pl.BlockSpec((cb, HPS, HEAD, HEAD), lambda c, g: (nb - 1 - c, g, 0, 0)), tile,
                  pl.BlockSpec(mstack.shape, lambda c, h: (0, 0)),
                  pl.BlockSpec(mstack_t.shape, lambda c, h: (0, 0)),
                  pl.BlockSpec(masks.shape, lambda c, h: (0, 0, 0)),
                  pl.BlockSpec(eye.shape, lambda c, h: (0, 0))],
        out_specs=[tile, tile, tile, tile, pl.BlockSpec((1, HEAD), lambda c, h: (0, 0)),
                   pl.BlockSpec((HG_HEADS, HEAD), lambda c, h: (0, 0))],
        out_shape=outs, scratch_shapes=[pltpu.VMEM((HG_HEADS, HEAD, HEAD), F32)],
        compiler_params=_params(_ARB, _ARB))(hq[0], hf[0], hi[0], hg[0], logits, gain, oraw, ssave, dog, mstack,
                                             mstack_t, masks, eye)


CONV_W = 512


def _per_head(fn, *arrs):
    width = arrs[0].shape[1]
    return jnp.concatenate([fn(*[a[:, j:j + HEAD] for a in arrs]) for j in range(0, width, HEAD)], axis=1)


def _shift_down(xv, halo, d, top_rows):
    if d == 0:
        return xv, xv[0:8]
    main = pltpu.roll(xv, d, 0)
    top = jnp.where(top_rows < d, pltpu.roll(halo, d, 0), main[0:8])
    return main, top


def _conv_parts(x_ref, halo_ref, w_ref, first):
    xv = x_ref[...]
    halo = jnp.where(first, 0.0, halo_ref[...])
    top_rows = lax.broadcasted_iota(jnp.int32, (8, xv.shape[1]), 0)
    shifted = [_shift_down(xv, halo, CONV_K - 1 - j, top_rows) for j in range(CONV_K)]
    w = w_ref[...]
    acc = sum(shifted[j][0] * w[j:j + 1, :] for j in range(CONV_K))
    acc_top = sum(shifted[j][1] * w[j:j + 1, :] for j in range(CONV_K))
    return shifted, acc, acc_top


def _conv_fwd(x, w8, l2scale, name):
    x, off, width = x
    t = x.shape[0]
    o = off // CONV_W
    tr = _pick(t, 512, 8)

    def post(cv):
        s = _silu(cv)
        if l2scale is not None:
            s = _per_head(lambda sh: sh * (lax.rsqrt(_rowsum(sh * sh) + EPS) * l2scale), s)
        return s

    def body(x_ref, halo_ref, w_ref, o_ref):
        _, acc, acc_top = _conv_parts(x_ref, halo_ref, w_ref, pl.program_id(1) == 0)
        o_ref[...] = post(acc)
        o_ref[0:8, :] = post(acc_top)

    return pl.pallas_call(
        body, name=name, grid=(width // CONV_W,t // tr),
        in_specs=[pl.BlockSpec((tr, CONV_W), lambda j, i: (i, o + j)),
                  pl.BlockSpec((8, CONV_W), lambda j, i: (jnp.maximum(i * (tr // 8) - 1, 0), o + j)),
                  pl.BlockSpec((8, CONV_W), lambda j, i: (0, j))],
        out_specs=pl.BlockSpec((tr, CONV_W), lambda j, i: (i, j)),
        out_shape=jax.ShapeDtypeStruct((t, width), F32), compiler_params=_params(_PAR, _PAR))(x, x, w8)


def _conv_bwd_a(x, w8, dy, l2scale, name):
    x, off, width = x
    t = x.shape[0]
    o = off // CONV_W
    tr = _pick(t, 512, 8)

    def l2_bwd(s, dyh):
        r = lax.rsqrt(_rowsum(s * s) + EPS)
        y0 = s * r
        dy0 = dyh * l2scale
        return r * (dy0 - y0 * _rowsum(dy0 * y0))

    def to_dc(cv, dyv):
        if l2scale is not None:
            dyv = _per_head(l2_bwd, _silu(cv), dyv)
        return dyv * _dsilu(cv)

    def body(x_ref, halo_ref, w_ref, dy_ref, dc_ref, dw_ref):
        @pl.when(pl.program_id(1) == 0)
        def _():
            dw_ref[...] = jnp.zeros_like(dw_ref)

        shifted, acc, acc_top = _conv_parts(x_ref, halo_ref, w_ref, pl.program_id(1) == 0)
        dyv = dy_ref[...]
        dc = to_dc(acc, dyv)
        dc_top = to_dc(acc_top, dyv[0:8])
        dc_ref[...] = dc
        dc_ref[0:8, :] = dc_top
        rest = (lax.broadcasted_iota(jnp.int32, dc.shape, 0) >= 8).astype(F32)
        dc_rest = dc * rest
        for j in range(CONV_K):
            dw_ref[j:j + 1, :] += (jnp.sum(dc_rest * shifted[j][0], axis=0, keepdims=True)
                                   + jnp.sum(dc_top * shifted[j][1], axis=0, keepdims=True))

    return pl.pallas_call(
        body, name=name, grid=(width // CONV_W,t // tr),
        in_specs=[pl.BlockSpec((tr, CONV_W), lambda j, i: (i, o + j)),
                  pl.BlockSpec((8, CONV_W), lambda j, i: (jnp.maximum(i * (tr // 8) - 1, 0), o + j)),
                  pl.BlockSpec((8, CONV_W), lambda j, i: (0, j)),
                  pl.BlockSpec((tr, CONV_W), lambda j, i: (i, j))],
        out_specs=[pl.BlockSpec((tr, CONV_W), lambda j, i: (i, j)), pl.BlockSpec((8, CONV_W), lambda j, i: (0, j))],
        out_shape=[jax.ShapeDtypeStruct((t, width), F32), jax.ShapeDtypeStruct((8, width), F32)],
        compiler_params=_params(_PAR, _ARB))(x, x, w8, dy)


def _conv_bwd_b(dc, w8, name):
    t, width = dc.shape
    tr = _pick(t, 512, 8)
    nt = t // tr

    def body(dc_ref, halo_ref, w_ref, dx_ref):
        dcv = dc_ref[...]
        halo = jnp.where(pl.program_id(1) == nt - 1, 0.0, halo_ref[...])
        w = w_ref[...]
        bot_rows = lax.broadcasted_iota(jnp.int32, (8, CONV_W), 0)
        acc = dcv * w[CONV_K - 1:CONV_K, :]
        acc_bot = dcv[tr - 8:tr] * w[CONV_K - 1:CONV_K, :]
        for d in range(1, CONV_K):
            main = pltpu.roll(dcv, tr - d, 0)
            bot = jnp.where(bot_rows >= 8 - d, pltpu.roll(halo, 8 - d, 0), main[tr - 8:tr])
            wj = w[CONV_K - 1 - d:CONV_K - d, :]
            acc = acc + main * wj
            acc_bot = acc_bot + bot * wj
        dx_ref[...] = acc.astype(BF16)
        dx_ref[tr - 16:tr, :] = jnp.concatenate([acc[tr - 16:tr - 8], acc_bot], axis=0).astype(BF16)

    return pl.pallas_call(
        body, name=name, grid=(width // CONV_W,nt),
        in_specs=[pl.BlockSpec((tr, CONV_W), lambda j, i: (i, j)),
                  pl.BlockSpec((8, CONV_W), lambda j, i: (jnp.minimum((i + 1) * (tr // 8), t // 8 - 1), j)),
                  pl.BlockSpec((8, CONV_W), lambda j, i: (0, j))],
        out_specs=pl.BlockSpec((tr, CONV_W), lambda j, i: (i, j)),
        out_shape=jax.ShapeDtypeStruct((t, width), BF16), compiler_params=_params(_PAR, _PAR))(dc, dc, w8)


def _each(f, *lists):
    return [f(*xs) for xs in zip(*lists)]


def _split2_each(xs):
    hi = [_bf(x) for x in xs]
    lo = [_bf(x - h.astype(F32)) for x, h in zip(xs, hi)]
    return hi, lo


def _hp_each(a_split, b_split):
    (ah, al), (bh, bl) = a_split, b_split
    rows = ah[0].shape[0]
    d12 = [jnp.dot(jnp.concatenate([x, y], axis=0), z, preferred_element_type=F32) for x, y, z in zip(ah, al, bh)]
    d3 = [jnp.dot(x, y, preferred_element_type=F32) for x, y in zip(ah, bl)]
    return [d[:rows] + d[rows:] + e for d, e in zip(d12, d3)]


INV_EXACT_STEPS = 2


def _tri_inv_each(a_list, eye):
    ns = [-a for a in a_list]
    ps = [eye + n for n in ns]
    n_split = _split2_each(ns)
    for step in range(5):
        if step < INV_EXACT_STEPS:
            ns = _hp_each(n_split, n_split)
            n_split = _split2_each(ns)
            ps = [p + d for p, d in zip(ps, _hp_each(_split2_each(ps), n_split))]
        else:
            nb = n_split[0] if step == INV_EXACT_STEPS else [_bf(n) for n in ns]
            ns = [jnp.dot(x, x, preferred_element_type=F32) for x in nb]
            nb2 = [_bf(n) for n in ns]
            ps = [p + jnp.dot(_bf(p), y, preferred_element_type=F32) for p, y in zip(ps, nb2)]
    return ps


def _gd_gates(gab, alog, dtb):
    sp_arg = gab + dtb
    return sp_arg, -jnp.exp(alog) * _softplus(sp_arg), _sigmoid(gab)


def _pick_lane(tile, base, head):
    g, hh = head
    col = tile[:, base + hh:base + hh + 1]
    for gi in range(1, GD_HEADS // HPS):
        lane = base + gi * HPS + hh
        col = jnp.where(g == gi, tile[:, lane:lane + 1], col)
    return col


def _gd_chunks(q, k, v, g_all, beta_all, sel, l_ref, mask_ref, tm=None):
    incl, strict, eye, upper = mask_ref[0], mask_ref[1], mask_ref[2], mask_ref[3]
    lmat = l_ref[...]
    gb = [jnp.broadcast_to(_pick_lane(g_all, 0, s), (CHUNK, HEAD)) for s in sel]
    bb = [jnp.broadcast_to(_pick_lane(beta_all, GD_HEADS, s), (CHUNK, HEAD)) for s in sel]
    gam = _mx_each(lmat, gb)
    gam_row = [jnp.sum(x[:, :CHUNK] * upper, axis=0, keepdims=True) for x in gb]
    lm = _each(lambda gm, gr: incl * jnp.exp(jnp.minimum(gm[:, :CHUNK] - gr, 0.0)), gam, gam_row)
    kb = _each(lambda x, b: x * b, k, bb)
    a = _each(lambda x, y, m: strict * _dot_nt(x, y) * m, kb, k, lm)
    if tm is None:
        tm = _tri_inv_each(a, eye)
    eg = [jnp.exp(x) for x in gam]
    vb = _each(lambda x, b: x * b, v, bb)
    kbg = _each(lambda x, e: x * e, kb, eg)
    uw = _each(lambda t_, x, y: _dot(t_, jnp.concatenate([x, y], axis=1)), tm, vb, kbg)
    u = [x[:, :HEAD] for x in uw]
    w = [x[:, HEAD:] for x in uw]
    qk = _each(lambda x, y, m: _dot_nt(x, y) * m, q, k, lm)
    g_end = [x[CHUNK - 1:CHUNK, :] for x in gam]
    ekg = _each(lambda e, x: jnp.exp(e - x), g_end, gam)
    ge = [jnp.exp(e) for e in g_end]
    kg = _each(lambda x, e: x * e, k, ekg)
    qg = _each(lambda x, e: x * e, q, eg)
    names = ("bb", "lm", "kb", "a", "tm", "eg", "vb", "kbg", "u", "w", "qk", "ekg", "ge", "kg", "qg")
    cols = (bb, lm, kb, a, tm, eg, vb, kbg, u, w, qk, ekg, ge, kg, qg)
    return [dict(zip(names, vals)) for vals in zip(*cols)]


def _gd_specs(rows, rev_nb=None):
    def cidx(c):
        return c if rev_nb is None else rev_nb - 1 - c

    qk_tile = pl.BlockSpec((rows, HPS // 2 * HEAD), lambda c, g: (cidx(c), g))
    v_tile = pl.BlockSpec((rows, HPS * HEAD), lambda c, g: (cidx(c), g))
    gab_tile = pl.BlockSpec((rows, HEAD), lambda c, g: (cidx(c), 0))
    return qk_tile, v_tile, gab_tile


def _gdn_fwd(qn, kn, cv, gab, gz, alog, dtb, gain, consts):
    t = qn.shape[0]
    nc = t // CHUNK
    cb = _chunks_per_step(nc)
    rows = cb * CHUNK
    lmat, _, masks = consts
    qk_tile, v_tile, gab_tile = _gd_specs(rows)
    row128 = pl.BlockSpec((1, HEAD), lambda c, h: (0, 0))

    def body(q_ref, k_ref, v_ref, gab_ref, gz_ref, alog_ref, dtb_ref, gain_ref, l_ref, mask_ref,
             oraw_ref, og_ref, ssave_ref, tsave_ref, state):
        c = pl.program_id(0)
        g = pl.program_id(1)

        @pl.when(c == 0)
        def _():
            for hh in range(HPS):
                state[g * HPS + hh] = jnp.zeros((HEAD, HEAD), F32)

        alog = alog_ref[...]
        dtb = dtb_ref[...]
        gain_v = gain_ref[...]

        def one(i, carry):
            sl = pl.ds(pl.multiple_of(i * CHUNK, CHUNK), CHUNK)
            _, g_all, beta_all = _gd_gates(gab_ref[sl, :], alog, dtb)
            heads = [g * HPS + hh for hh in range(HPS)]
            lq = [slice(hh // 2 * HEAD, (hh // 2 + 1) * HEAD) for hh in range(HPS)]
            lv = [slice(hh * HEAD, (hh + 1) * HEAD) for hh in range(HPS)]
            chs = _gd_chunks([q_ref[sl, s] for s in lq], [k_ref[sl, s] for s in lq], [v_ref[sl, s] for s in lv],
                             g_all, beta_all, [(g, hh) for hh in range(HPS)], l_ref, mask_ref)
            s0 = [state[h] for h in heads]
            ws = _each(lambda ch, s: _dot(jnp.concatenate([ch["w"], ch["qg"]], axis=0), s), chs, s0)
            v_new = _each(lambda ch, x: ch["u"] - x[:CHUNK], chs, ws)
            o = _each(lambda ch, x, vn: x[CHUNK:] + _dot(ch["qk"], vn), chs, ws, v_new)
            s1 = _each(lambda ch, s, vn: s * ch["ge"] + _dot_tn(ch["kg"], vn), chs, s0, v_new)
            for hh in range(HPS):
                ssave_ref[i, hh] = s0[hh]
                tsave_ref[i, hh] = chs[hh]["tm"]
                state[heads[hh]] = s1[hh]
                oraw_ref[sl, lv[hh]] = o[hh]
                r = lax.rsqrt(jnp.mean(o[hh] * o[hh], axis=1, keepdims=True) + EPS)
                og_ref[sl, lv[hh]] = (o[hh] * r * gain_v * _silu(gz_ref[sl, lv[hh]])).astype(BF16)
            return carry

        lax.fori_loop(0, cb, one, 0, unroll=4)

    return pl.pallas_call(
        body, name="gdn_fwd", grid=(nc // cb, GD_HEADS // HPS),
        in_specs=[qk_tile, qk_tile, v_tile, gab_tile, _view_tile(gz, rows, HPS * HEAD), row128, row128, row128,
                  pl.BlockSpec(lmat.shape, lambda c, g: (0, 0)),
                  pl.BlockSpec(masks.shape, lambda c, g: (0, 0, 0))],
        out_specs=[v_tile, v_tile, pl.BlockSpec((cb, HPS, HEAD, HEAD), lambda c, g: (c, g, 0, 0)),
                   pl.BlockSpec((cb, HPS, CHUNK, CHUNK), lambda c, g: (c, g, 0, 0))],
        out_shape=[jax.ShapeDtypeStruct((t, GD_HEADS * HEAD), F32), jax.ShapeDtypeStruct((t, GD_HEADS * HEAD), BF16),
                   jax.ShapeDtypeStruct((nc, GD_HEADS, HEAD, HEAD), F32),
                   jax.ShapeDtypeStruct((nc, GD_HEADS, CHUNK, CHUNK), F32)],
        scratch_shapes=[pltpu.VMEM((GD_HEADS, HEAD, HEAD), F32)],
        compiler_params=_params(_ARB, _ARB))(qn, kn, cv, gab, gz[0], alog, dtb, gain, lmat, masks)


def _gdn_bwd(qn, kn, cv, gab, gz, alog, dtb, gain, oraw, ssave, tsave, dog, consts):
    t = qn.shape[0]
    nc = t // CHUNK
    cb = _chunks_per_step(nc)
    rows = cb * CHUNK
    nb = nc // cb
    lmat, lmat_t, masks = consts
    qk_tile, v_tile, gab_tile = _gd_specs(rows, nb)
    row128 = pl.BlockSpec((1, HEAD), lambda c, h: (0, 0))

    def body(q_ref, k_ref, v_ref, gab_ref, gz_ref, alog_ref, dtb_ref, gain_ref, oraw_ref, ssave_ref, tsave_ref, dog_ref,
             l_ref, lt_ref, mask_ref,
             dq_ref, dk_ref, dv_ref, dgab_ref, dgz_ref, small_ref, dstate):
        c = pl.program_id(0)
        g = pl.program_id(1)

        @pl.when(c == 0)
        def _():
            for hh in range(HPS):
                dstate[g * HPS + hh] = jnp.zeros((HEAD, HEAD), F32)

        @pl.when((c == 0) & (g == 0))
        def _():
            small_ref[...] = jnp.zeros_like(small_ref)

        alog = alog_ref[...]
        dtb = dtb_ref[...]
        gain_v = gain_ref[...]
        lane = lax.broadcasted_iota(jnp.int32, (1, HEAD), 1)
        last_row = (lax.broadcasted_iota(jnp.int32, (CHUNK, HEAD), 0) == CHUNK - 1).astype(F32)

        def one(j, carry):
            i = cb - 1 - j
            sl = pl.ds(pl.multiple_of(i * CHUNK, CHUNK), CHUNK)
            sp_arg, g_all, beta_all = _gd_gates(gab_ref[sl, :], alog, dtb)
            strict, eye = mask_ref[1], mask_ref[2]
            ltm = lt_ref[...]
            hs = range(HPS)
            heads = [g * HPS + hh for hh in hs]
            lq = [slice(hh // 2 * HEAD, (hh // 2 + 1) * HEAD) for hh in hs]
            lv = [slice(hh * HEAD, (hh + 1) * HEAD) for hh in hs]
            q = [q_ref[sl, s] for s in lq]
            k = [k_ref[sl, s] for s in lq]
            v = [v_ref[sl, s] for s in lv]
            gzv = [gz_ref[sl, s] for s in lv]
            chs = _gd_chunks(q, k, v, g_all, beta_all, [(g, hh) for hh in hs], l_ref, mask_ref,
                             tm=[tsave_ref[i, hh] for hh in hs])

            def col(name):
                return [ch[name] for ch in chs]

            def mul(x, y):
                return x * y

            tm, lm, eg, bb = col("tm"), col("lm"), col("eg"), col("bb")
            s0 = [ssave_ref[i, hh] for hh in hs]
            ds = [dstate[h] for h in heads]
            v_new = _each(lambda u, w, s: u - _dot(w, s), col("u"), col("w"), s0)

            o = [oraw_ref[sl, s] for s in lv]
            r = [lax.rsqrt(jnp.mean(x * x, axis=1, keepdims=True) + EPS) for x in o]
            on = _each(mul, o, r)
            dg_out = [dog_ref[sl, s] for s in lv]
            sgate = [_silu(x) for x in gzv]
            for hh in hs:
                dgz_ref[sl, lv[hh]] = (dg_out[hh] * on[hh] * gain_v * _dsilu(gzv[hh])).astype(BF16)
            small_ref[0:1, :] += sum(jnp.sum(d * s * n, axis=0, keepdims=True) for d, s, n in zip(dg_out, sgate, on))
            don = _each(lambda d, s: d * s * gain_v, dg_out, sgate)
            do = _each(lambda rr, dn, n: rr * (dn - n * jnp.mean(dn * n, axis=1, keepdims=True)), r, don, on)

            dv_new = _each(lambda a, d, b, s: _dot_tn(a, d) + _dot(b, s), col("qk"), do, col("kg"), ds)
            dqk = _each(_dot_nt, do, v_new)
            dkg = _each(_dot_nt, v_new, ds)
            dge = _each(lambda s, d: jnp.sum(_rowsum(s * d), axis=0, keepdims=True), s0, ds)
            both = _each(lambda d, dv: jnp.concatenate([d, dv], axis=0), do, dv_new)
            from_s = _each(_dot_nt, both, s0)
            dqg = [x[:CHUNK] for x in from_s]
            dw = [-x[CHUNK:] for x in from_s]
            ds_new = _each(lambda qg, w, bo, ge, s: _dot_tn(jnp.concatenate([qg, -w], axis=0), bo) + ge * s,
                           col("qg"), col("w"), both, col("ge"), ds)
            for hh in hs:
                dstate[heads[hh]] = ds_new[hh]

            side = _each(lambda dv, d: jnp.concatenate([dv, d], axis=1), dv_new, dw)
            back = _each(_dot_tn, tm, side)
            dvb = [x[:, :HEAD] for x in back]
            dkbg = [x[:, HEAD:] for x in back]
            dtm = _each(lambda sd, vb, kbg: _dot_nt(sd, jnp.concatenate([vb, kbg], axis=1)), side, col("vb"), col("kbg"))
            dtt = _each(_dot_nt, dtm, tm)
            da = _each(lambda t_, x: -_dot_tn(t_, x) * strict, tm, dtt)
            dal = _each(mul, da, lm)
            dqk_l = _each(mul, dqk, lm)
            stack = _each(lambda x, y: jnp.concatenate([x, y], axis=0), dal, dqk_l)
            on_k = _each(_dot, stack, k)
            dkb = _each(lambda x, y, e: x[:CHUNK] + y * e, on_k, dkbg, eg)
            dq = _each(lambda x, y, e: x[CHUNK:] + y * e, on_k, dqg, eg)
            dk = _each(lambda st, kb, qq, z, ekg, w_, b: _dot_tn(st, jnp.concatenate([kb, qq], axis=0)) + z * ekg + w_ * b,
                       stack, col("kb"), q, dkg, col("ekg"), dkb, bb)
            gmat = _each(lambda x, a, y, qk: x * a + y * qk, da, col("a"), dqk, col("qk"))
            t_kg = _each(lambda x, y: _rowsum(x * y), dkg, col("kg"))
            dgam = _each(lambda gm, x, qg, t_, y, kbg: (_rowsum(gm) - _row_to_col(jnp.sum(gm, axis=0, keepdims=True), eye)
                                                        + _rowsum(x * qg) - t_ + _rowsum(y * kbg)),
                         gmat, dqg, col("qg"), t_kg, dkbg, col("kbg"))
            dg_end = _each(lambda t_, e, ge: jnp.sum(t_, axis=0, keepdims=True) + e * ge[:, 0:1], t_kg, dge, col("ge"))
            dgam = _each(lambda x, e: x + last_row * e, dgam, dg_end)
            dbeta = _each(lambda x, kk, y, vv: _rowsum(x * kk) + _rowsum(y * vv), dkb, k, dvb, v)
            dg = _mx_each(ltm, dgam)

            for hh in hs:
                dv_ref[sl, lv[hh]] = dvb[hh] * bb[hh]
            fac_g = -jnp.exp(alog) * _sigmoid(sp_arg)
            fac_b = beta_all * (1.0 - beta_all)
            hot_g = [(lane == h).astype(F32) for h in heads]
            hot_b = [(lane == GD_HEADS + h).astype(F32) for h in heads]
            dga = _each(lambda x, hot: x * hot * fac_g, dg, hot_g)
            dgb = _each(lambda x, hot: x * hot * fac_b, dbeta, hot_b)
            small_ref[1:2, :] += sum(jnp.sum(x, axis=0, keepdims=True) for x in dga)
            small_ref[2:3, :] += sum(jnp.sum(x * hot * g_all, axis=0, keepdims=True) for x, hot in zip(dg, hot_g))
            for pair in range(HPS // 2):
                lqp = slice(pair * HEAD, (pair + 1) * HEAD)
                dq_ref[sl, lqp] = dq[2 * pair] + dq[2 * pair + 1]
                dk_ref[sl, lqp] = dk[2 * pair] + dk[2 * pair + 1]
            dgab_ref[sl, :] = sum(a + b for a, b in zip(dga, dgb))
            return carry

        lax.fori_loop(0, cb, one, 0, unroll=4)

    groups = GD_HEADS // HPS
    outs = [jax.ShapeDtypeStruct((t, 1024), F32), jax.ShapeDtypeStruct((t, 1024), F32),
            jax.ShapeDtypeStruct((t, 2048), F32), jax.ShapeDtypeStruct((t, groups * HEAD), F32),
            jax.ShapeDtypeStruct((t, 2048), BF16), jax.ShapeDtypeStruct((8, HEAD), F32)]
    return pl.pallas_call(
        body, name="gdn_bwd", grid=(nb, groups),
        in_specs=[qk_tile, qk_tile, v_tile, gab_tile, _view_tile(gz, rows, HPS * HEAD, lambda c: nb - 1 - c),
                  row128, row128, row128, v_tile,
                  pl.BlockSpec((cb, HPS, HEAD, HEAD), lambda c, g: (nb - 1 - c, g, 0, 0)),
                  pl.BlockSpec((cb, HPS, CHUNK, CHUNK), lambda c, g: (nb - 1 - c, g, 0, 0)), v_tile,
                  pl.BlockSpec(lmat.shape, lambda c, g: (0, 0)),
                  pl.BlockSpec(lmat_t.shape, lambda c, g: (0, 0)),
                  pl.BlockSpec(masks.shape, lambda c, g: (0, 0, 0))],
        out_specs=[qk_tile, qk_tile, v_tile, pl.BlockSpec((rows, HEAD), lambda c, g: (nb - 1 - c, g)), v_tile,
                   pl.BlockSpec((8, HEAD), lambda c, g: (0, 0))],
        out_shape=outs, scratch_shapes=[pltpu.VMEM((GD_HEADS, HEAD, HEAD), F32)],
        compiler_params=_params(_ARB, _ARB))(qn, kn, cv, gab, gz[0], alog, dtb, gain, oraw, ssave, tsave, dog,
                                             lmat, lmat_t, masks)


def _fold_groups(wide):
    t, width = wide.shape
    tr = _pick(t, 512, 8)

    def body(w_ref, o_ref):
        acc = w_ref[:, 0:HEAD]
        for j in range(1, width // HEAD):
            acc = acc + w_ref[:, j * HEAD:(j + 1) * HEAD]
        o_ref[...] = acc.astype(BF16)

    return pl.pallas_call(
        body, name="fold_gate_grads", grid=(t // tr,), in_specs=[_row_spec(tr, width)], out_specs=_row_spec(tr, HEAD),
        out_shape=jax.ShapeDtypeStruct((t, HEAD), BF16), compiler_params=_params(_PAR))(wide)


def _adam_math(w, g, m, v):
    m2 = ADAM_B1 * m + (1.0 - ADAM_B1) * g
    v2 = ADAM_B2 * v + (1.0 - ADAM_B2) * (g * g)
    m_hat = m2 / (1.0 - ADAM_B1 ** ADAM_STEP)
    v_hat = v2 / (1.0 - ADAM_B2 ** ADAM_STEP)
    delta = -ADAM_LR * (m_hat / (jnp.sqrt(v_hat) + ADAM_EPS) + ADAM_WD * w)
    return delta, m2, v2


def _adamw(w, g, m, v, name, after=None):
    r, c = w.shape
    tr = r
    for cand in range(8, r + 1, 8):
        if r % cand == 0 and cand * c * 4 <= (1 << 20):
            tr = cand
    if r % 8 != 0:
        tr = r

    def body(w_ref, g_ref, m_ref, v_ref, *rest):
        d_ref, m2_ref, v2_ref = rest[-3:]
        d, m2, v2 = _adam_math(w_ref[...], g_ref[...], m_ref[...], v_ref[...])
        d_ref[...] = d
        m2_ref[...] = m2
        v2_ref[...] = v2

    spec = pl.BlockSpec((tr, c), lambda i: (i, 0))
    extra = [] if after is None else [after]
    return pl.pallas_call(
        body, name=name, grid=(r // tr,), in_specs=[spec] * 4 + [_ANY] * len(extra), out_specs=[spec] * 3,
        out_shape=[jax.ShapeDtypeStruct((r, c), F32)] * 3, compiler_params=_params(_PAR))(w, g, m, v, *extra)


_ANY = pl.BlockSpec(memory_space=pl.ANY)


def _place():
    return lax.axis_index("x"), lax.axis_index("y"), lax.axis_index("c")


def _gather_weights(packs, nchs, name):
    n = len(packs)
    halves = [p.shape[0] // 2 for p in packs]
    base = [sum(nchs[:i]) for i in range(n)]
    total = sum(nchs)
    for p, h, k in zip(packs, halves, nchs):
        assert p.shape[0] == 2 * h and h % k == 0 and (h // k) % 16 == 0

    def body(*refs):
        p_refs, g_refs, (send_sems, recv_sems) = refs[:n], refs[n:2 * n], refs[2 * n:]
        x, y, c = _place()
        sibling = (x, y, 1 - c)
        chips = [(1 - x, y), (x, 1 - y), (1 - x, 1 - y)]
        chunks = [(a, q) for a in range(n) for q in range(nchs[a])]

        def rows_of(a, pc, q):
            ch = halves[a] // nchs[a]
            return pl.ds(pl.multiple_of(pc * halves[a] + q * ch, 16), ch)

        def piece(a, px, py, pc, q):
            return g_refs[a].at[2 * px + py, rows_of(a, pc, q), :]

        def copy(k, src, dst, to):
            return pltpu.make_async_remote_copy(src_ref=src, dst_ref=dst, send_sem=send_sems.at[k],
                                                recv_sem=recv_sems.at[k], device_id=to, device_id_type=MESH)

        def sem_of(j, a, q):
            return j * total + base[a] + q

        first = {(j, a, q): copy(sem_of(j, a, q), p_refs[a].at[rows_of(a, c, q), :], piece(a, x, y, c, q), (*chip, c))
                 for j, chip in enumerate(chips) for a, q in chunks}
        for a, q in chunks:
            for j in range(3):
                first[j, a, q].start()
        passed = {(j, a, q): copy(sem_of(3 + j, a, q), piece(a, *chip, c, q), piece(a, *chip, c, q), sibling)
                  for j, chip in enumerate(chips) for a, q in chunks}
        for a, q in chunks:
            for j, chip in enumerate(chips):
                copy(sem_of(j, a, q), p_refs[a].at[rows_of(a, c, q), :], piece(a, *chip, c, q), (*chip, c)).wait_recv()
                passed[j, a, q].start()
        for a, q in chunks:
            for j, chip in enumerate(chips):
                copy(sem_of(3 + j, a, q), piece(a, *chip, 1 - c, q), piece(a, *chip, 1 - c, q), sibling).wait_recv()
        for key in first:
            first[key].wait_send()
            passed[key].wait_send()

    return pl.pallas_call(
        body, name=name, out_shape=[jax.ShapeDtypeStruct((4,) + p.shape, p.dtype) for p in packs],
        in_specs=[_ANY] * n, out_specs=[_ANY] * n,
        scratch_shapes=[pltpu.SemaphoreType.DMA((6 * total,)), pltpu.SemaphoreType.DMA((6 * total,))])(*packs)


def _swap_with_sibling(arrs, nchs, lead, name, halves=False):
    n = len(arrs)
    jobs = []
    hs = [arr.shape[-2] // (2 if halves else 1) for arr in arrs]
    for a, (h, k) in enumerate(zip(hs, nchs)):
        assert h % k == 0 and (h // k) % 16 == 0
        for s in (range(lead) if lead else [None]):
            jobs += [(a, s, q * (h // k), h // k) for q in range(k)]

    def body(*refs):
        src, dst, (send_sems, recv_sems) = refs[:n], refs[n:2 * n], refs[2 * n:]
        x, y, c = _place()

        def at(ref, s, r0, rows):
            return ref.at[pl.ds(r0, rows), :] if s is None else ref.at[s, pl.ds(r0, rows), :]

        def src_rows(a, r0):
            return pl.multiple_of((1 - c) * hs[a] + r0, 16) if halves else r0

        copies = [pltpu.make_async_remote_copy(
            src_ref=at(src[a], s, src_rows(a, r0), rows), dst_ref=at(dst[a], s, r0, rows), send_sem=send_sems.at[k],
            recv_sem=recv_sems.at[k], device_id=(x, y, 1 - c), device_id_type=MESH)
            for k, (a, s, r0, rows) in enumerate(jobs)]
        for cp in copies:
            cp.start()
        for cp in copies:
            cp.wait()

    shapes = [jax.ShapeDtypeStruct(arr.shape[:-2] + (h, arr.shape[-1]), arr.dtype) for arr, h in zip(arrs, hs)]
    return pl.pallas_call(
        body, name=name, out_shape=shapes, in_specs=[_ANY] * n, out_specs=[_ANY] * n,
        scratch_shapes=[pltpu.SemaphoreType.DMA((len(jobs),)), pltpu.SemaphoreType.DMA((len(jobs),))])(*arrs)


def _add2(full, b, core, name):
    n, rows, w = b.shape
    tr = _pick(rows, 256, 16)
    nblk = rows // tr

    def body(c_ref, a_ref, b_ref, o_ref):
        o_ref[...] = (a_ref[...].astype(F32) + b_ref[...].astype(F32)).astype(BF16)

    spec = pl.BlockSpec((1, tr, w), lambda i, j, c_ref: (i, j, 0))
    grid_spec = pltpu.PrefetchScalarGridSpec(
        num_scalar_prefetch=1, grid=(n, nblk),
        in_specs=[pl.BlockSpec((1, tr, w), lambda i, j, c_ref: (i, c_ref[0] * nblk + j, 0)), spec], out_specs=spec)
    return pl.pallas_call(
        body, name=name, grid_spec=grid_spec, out_shape=jax.ShapeDtypeStruct(b.shape, BF16),
        compiler_params=_params(_PAR, _PAR))(core, full, b)


def _reduce_chips(partials, nchs, name):
    n = len(partials)
    jobs = []
    for a, (arr, k) in enumerate(zip(partials, nchs)):
        h = arr.shape[1]
        assert h % k == 0 and (h // k) % 16 == 0
        jobs += [(a, q * (h // k), h // k) for q in range(k)]

    def body(*refs):
        src, dst, (send_sems, recv_sems) = refs[:n], refs[n:2 * n], refs[2 * n:]
        x, y, c = _place()
        chips = [(1 - x, y), (x, 1 - y), (1 - x, 1 - y)]
        copies = [pltpu.make_async_remote_copy(
            src_ref=src[a].at[2 * px + py, pl.ds(r0, rows), :], dst_ref=dst[a].at[j, pl.ds(r0, rows), :],
            send_sem=send_sems.at[3 * k + j], recv_sem=recv_sems.at[3 * k + j],
            device_id=(px, py, c), device_id_type=MESH)
            for k, (a, r0, rows) in enumerate(jobs) for j, (px, py) in enumerate(chips)]
        for cp in copies:
            cp.start()
        for cp in copies:
            cp.wait()

    return pl.pallas_call(
        body, name=name,
        out_shape=[jax.ShapeDtypeStruct((3,) + p.shape[1:], p.dtype) for p in partials],
        in_specs=[_ANY] * n, out_specs=[_ANY] * n,
        scratch_shapes=[pltpu.SemaphoreType.DMA((3 * len(jobs),)), pltpu.SemaphoreType.DMA((3 * len(jobs),))])(*partials)


_HBM = pl.BlockSpec(memory_space=pltpu.HBM)
_SEM = pl.BlockSpec(memory_space=pltpu.SEMAPHORE)
_DATAFLOW = pltpu.SideEffectType.DATAFLOW_SIDE_EFFECTING


def _ici_jobs(srcs, nchs, kind):
    jobs = []
    for a, (arr, k) in enumerate(zip(srcs, nchs)):
        h = arr.shape[0] // 2 if kind == "gather" else arr.shape[1]
        assert h % k == 0 and (h // k) % 16 == 0
        jobs += [(a, h, q * (h // k), h // k) for q in range(k)]
    return jobs


def _ici_copies(src, land, send_sems, recv_sems, jobs, kind):
    x, y, c = _place()
    chips = [(1 - x, y), (x, 1 - y), (1 - x, 1 - y)]
    copies = []
    for k, (a, h, r0, rows) in enumerate(jobs):
        for j, (px, py) in enumerate(chips):
            if kind == "gather":
                at = pl.ds(pl.multiple_of(c * h + r0, 16), rows)
                s, d = src[a].at[at, :], land[a].at[2 * x + y, at, :]
            else:
                s, d = src[a].at[2 * px + py, pl.ds(r0, rows), :], land[a].at[j, pl.ds(r0, rows), :]
            copies.append(pltpu.make_async_remote_copy(
                src_ref=s, dst_ref=d, send_sem=send_sems.at[3 * k + j], recv_sem=recv_sems.at[3 * k + j],
                device_id=(px, py, c), device_id_type=MESH))
    return copies


def _ici_start(srcs, nchs, kind, name):
    n = len(srcs)
    jobs = _ici_jobs(srcs, nchs, kind)
    lead = (lambda s: (4,) + s.shape) if kind == "gather" else (lambda s: (3,) + s.shape[1:])
    lands = [lax.empty(lead(s), s.dtype) for s in srcs]

    def body(*refs):
        src, land = refs[:n], refs[n:2 * n]
        send_sems, recv_sems, token = refs[2 * n], refs[2 * n + 1], refs[-1]
        for cp in _ici_copies(src, land, send_sems, recv_sems, jobs, kind):
            cp.start()
        token[...] = jnp.zeros_like(token)

    hbm = [pltpu.HBM(a.shape, a.dtype) for a in srcs + lands]
    outs = pl.pallas_call(
        body, name=name,
        out_shape=[pltpu.SemaphoreType.DMA((3 * len(jobs),)), pltpu.SemaphoreType.DMA((3 * len(jobs),))] + hbm
        + [jax.ShapeDtypeStruct((8, 128), F32)],
        in_specs=[_HBM] * (2 * n), out_specs=[_SEM, _SEM] + [_HBM] * (2 * n) + [pl.BlockSpec(memory_space=pltpu.VMEM)],
        input_output_aliases={i: 2 + i for i in range(2 * n)},
        compiler_params=pltpu.CompilerParams(has_side_effects=_DATAFLOW),
    )(*[pltpu.with_memory_space_constraint(a, pltpu.HBM) for a in srcs + lands])
    return (outs[0], outs[1], list(outs[2:2 + n]), list(outs[2 + n:2 + 2 * n]), nchs, kind), outs[-1]


def _ici_wait(handle, after, name):
    send_sems, recv_sems, srcs, lands, nchs, kind = handle
    n = len(srcs)
    jobs = _ici_jobs(srcs, nchs, kind)

    def body(*refs):
        src, land = refs[:n], refs[n:2 * n]
        for cp in _ici_copies(src, land, refs[2 * n], refs[2 * n + 1], jobs, kind):
            cp.wait_send()
            cp.wait_recv()

    outs = pl.pallas_call(
        body, name=name, out_shape=[pltpu.HBM(a.shape, a.dtype) for a in srcs + lands],
        in_specs=[_HBM] * (2 * n) + [_SEM, _SEM, _ANY], out_specs=[_HBM] * (2 * n),
        input_output_aliases={i: i for i in range(2 * n)},
        compiler_params=pltpu.CompilerParams(has_side_effects=_DATAFLOW),
    )(*srcs, *lands, send_sems, recv_sems, after)
    return list(outs[:n]), list(outs[n:])


def _pass_to_sibling(gathered, nchs, name):
    n = len(gathered)
    jobs = _ici_jobs([jax.ShapeDtypeStruct(g.shape[1:], g.dtype) for g in gathered], nchs, "gather")

    def body(*refs):
        src, dst, (send_sems, recv_sems) = refs[:n], refs[n:2 * n], refs[2 * n:]
        x, y, c = _place()
        slots = [2 * (1 - x) + y, 2 * x + (1 - y), 2 * (1 - x) + (1 - y)]

        def copy(k, j, pc):
            a, h, r0, rows = jobs[k]
            at = pl.ds(pl.multiple_of(pc * h + r0, 16), rows)
            return pltpu.make_async_remote_copy(
                src_ref=src[a].at[slots[j], at, :], dst_ref=dst[a].at[slots[j], at, :], send_sem=send_sems.at[3 * k + j],
                recv_sem=recv_sems.at[3 * k + j], device_id=(x, y, 1 - c), device_id_type=MESH)

        pairs = [(k, j) for k in range(len(jobs)) for j in range(3)]
        for k, j in pairs:
            copy(k, j, c).start()
        for k, j in pairs:
            copy(k, j, c).wait_send()
            copy(k, j, 1 - c).wait_recv()

    return pl.pallas_call(
        body, name=name, out_shape=[jax.ShapeDtypeStruct(g.shape, g.dtype) for g in gathered],
        in_specs=[_ANY] * n, out_specs=[_ANY] * n, input_output_aliases={i: i for i in range(n)},
        scratch_shapes=[pltpu.SemaphoreType.DMA((3 * len(jobs),)), pltpu.SemaphoreType.DMA((3 * len(jobs),))])(*gathered)


def _add4(own, got, name):
    rows, w = own.shape
    tr = _pick(rows, 128, 16)

    def body(a_ref, b_ref, o_ref):
        o_ref[...] = ((a_ref[...].astype(F32) + b_ref[0].astype(F32)) + b_ref[1].astype(F32)) + b_ref[2].astype(F32)

    return pl.pallas_call(
        body, name=name, grid=(rows // tr,),
        in_specs=[pl.BlockSpec((tr, w), lambda i: (i, 0)), pl.BlockSpec((3, tr, w), lambda i: (0, i, 0))],
        out_specs=pl.BlockSpec((tr, w), lambda i: (i, 0)), out_shape=jax.ShapeDtypeStruct((rows, w), F32),
        compiler_params=_params(_PAR))(own, got)


def _small_sync(gs, ws, ms, vs):
    rows = gs.shape[0]
    vmem = pl.BlockSpec(memory_space=pltpu.VMEM)

    def body(g_ref, w_ref, m_ref, v_ref, sum_ref, d_ref, m2_ref, v2_ref, buf, send_sems, recv_sems):
        x, y, c = _place()
        me = 4 * x + 2 * y + c
        buf[me] = g_ref[...]
        copies = []
        for k in range(1, 8):
            peer = (x ^ (k >> 2), y ^ ((k >> 1) & 1), c ^ (k & 1))
            copies.append(pltpu.make_async_remote_copy(
                src_ref=g_ref, dst_ref=buf.at[me], send_sem=send_sems.at[k - 1], recv_sem=recv_sems.at[k - 1],
                device_id=peer, device_id_type=MESH))
        for cp in copies:
            cp.start()
        for cp in copies:
            cp.wait()
        total = buf[0]
        for i in range(1, 8):
            total = total + buf[i]
        sum_ref[...] = total
        d, m2, v2 = _adam_math(w_ref[...], total, m_ref[...], v_ref[...])
        d_ref[...] = d
        m2_ref[...] = m2
        v2_ref[...] = v2

    shape = jax.ShapeDtypeStruct((rows, 128), F32)
    return pl.pallas_call(
        body, name="small_sync", out_shape=[shape] * 4, in_specs=[vmem] * 4, out_specs=[vmem] * 4,
        scratch_shapes=[pltpu.VMEM((8, rows, 128), F32), pltpu.SemaphoreType.DMA((7,)),
                        pltpu.SemaphoreType.DMA((7,))])(gs, ws, ms, vs)


_GROUPS = {
    "ffn1": dict(cols=("ffn1_w_in", 1408), rows=(("ffn1_w_out", 704, 704),), chunks=(8, 2)),
    "ffn2": dict(cols=("ffn2_w_in", 1408), rows=(("ffn2_w_out", 704, 704),), chunks=(8, 2)),
    "mixer": dict(cols=("w_in", 3080), chunks=(8, 4),
                  rows=(("w_branch_hgrn", 256, 256), ("w_branch_gdn", 512, 512), ("w_out", 256, 256),
                        ("gdn_conv_w", CONV_K, 128))),
}
_BIG_NAMES = tuple(n for g in _GROUPS.values() for n in (g["cols"][0],) + tuple(r[0] for r in g["rows"]))


def _group_names(group):
    return (group["cols"][0],) + tuple(r[0] for r in group["rows"])


def _pack(parts, lead, group):
    ax = len(lead)
    rows = []
    for n, r, padded in group["rows"]:
        p = parts[n]
        if padded != r:
            p = jnp.tile(p, (1,) * ax + (padded // r, 1))
        rows.append(p)
    return [parts[group["cols"][0]], rows[0] if len(rows) == 1 else jnp.concatenate(rows, axis=ax)]


def _unpack(cols, rows, group):
    out, off = {group["cols"][0]: cols}, 0
    for n, r, padded in group["rows"]:
        out[n] = rows[..., off:off + r, :]
        off += padded
    return out


def _is_col_sharded(name):
    return name in ("ffn1_w_in", "ffn2_w_in", "w_in", "gdn_conv_w")


def _full_from_shards(name, g):
    if _is_col_sharded(name):
        return jnp.transpose(g, (1, 0, 2)).reshape(g.shape[1], -1)
    return g.reshape(-1, g.shape[2])


def _shards_from_full(name, full):
    if _is_col_sharded(name):
        return jnp.transpose(full.reshape(full.shape[0], 4, -1), (1, 0, 2))
    return full.reshape(4, -1, full.shape[1])


_SMALL = (("ffn1_norm", 8), ("mix_norm", 8), ("hgrn_lb_logits", 16), ("hgrn_out_norm", 8), ("gdn_a_log", 8),
          ("gdn_dt_bias", 8), ("gdn_out_norm", 8), ("ffn2_norm", 8), ("final_norm", 8), ("loss", 8))
_SMALL_ROWS = sum(r for _, r in _SMALL)


def _pack_small(parts):
    out = []
    for name, rows in _SMALL:
        p = parts[name].reshape(-1).astype(F32)
        if p.shape[0] <= 128:
            if p.shape[0] < 128:
                p = jnp.concatenate([p, jnp.zeros((128 - p.shape[0],), F32)])
            p = jnp.broadcast_to(p.reshape(1, 128), (rows, 128))
        out.append(p.reshape(rows, 128))
    return jnp.concatenate(out, axis=0)


def _unpack_small(packed, shapes):
    out, off = {}, 0
    for name, rows in _SMALL:
        n = int(np.prod(shapes[name]))
        out[name] = packed[off:off + rows].reshape(-1)[:n].reshape(shapes[name])
        off += rows
    return out


def _ffn_fwd(x, gain, w_in, w_out, tag):
    n = _rmsnorm_fwd(x, gain, tag + "_norm")
    a, b, hm = _ffn_in_act(n, w_in, tag + "_in")
    out = _mm(hm, w_out, alpha=0.5, res=x, name=tag + "_out")
    return out, (n, a, b)


def _ffn_bwd(x, gain, w_in, w_out, saved, dout, dout_bf, tag):
    n, a, b = saved
    da, db, hm = _ffn_dact(dout_bf, w_out, a, b, tag + "_dact")
    dw_out = _mm(hm, dout_bf, ta=True, alpha=0.5, out_dtype=BF16, name=tag + "_dwout")
    dwa = _mm(n, da, ta=True, out_dtype=BF16, name=tag + "_dwin_a")
    dwb = _mm(n, db, ta=True, out_dtype=BF16, name=tag + "_dwin_b")
    half = D_FF // 2
    dw_in = jnp.stack([dwa[:, :half], dwa[:, half:], dwb[:, :half], dwb[:, half:]])
    dn = _mm(da, w_in, tb=True, name=tag + "_dnorm_a")
    dn = _mm(db, w_in, tb=True, res=dn, b_from=D_FF, name=tag + "_dnorm_b")
    dx, dx_bf, dgain = _rmsnorm_bwd(x, gain, dn, dout, tag + "_dx")
    return dx, dx_bf, dgain, dw_in, dw_out


def _pad_lanes(v):
    return jnp.concatenate([v.reshape(1, -1), jnp.zeros((1, HEAD - v.size), F32)], axis=1)


def _local_step(x, tgt, small, exchange):
    hg_c = _hg_consts()
    gd_c = _gd_consts()
    alog = _pad_lanes(small["gdn_a_log"])
    dtb = _pad_lanes(small["gdn_dt_bias"])
    logits = small["hgrn_lb_logits"]
    hg_gain = small["hgrn_out_norm"].reshape(1, HEAD)
    gd_gain = small["gdn_out_norm"].reshape(1, HEAD)
    g1, gm, g2 = small["ffn1_norm"].reshape(1, -1), small["mix_norm"].reshape(1, -1), small["ffn2_norm"].reshape(1, -1)
    gf = small["final_norm"].reshape(1, -1)
    qscale = HEAD ** -0.5

    w1 = exchange.weights("ffn1")
    started = exchange.prefetch("mixer")
    h1, ffn1_saved = _ffn_fwd(x, g1 + started, w1["ffn1_w_in"], w1["ffn1_w_out"], "ffn1")
    u = _rmsnorm_fwd(h1, gm, "mix_norm")
    w = exchange.weights("mixer", after=u)
    started = exchange.prefetch("ffn2")
    seg, off = {}, 0
    for name, size in zip(IN_NAMES, IN_SIZES):
        seg[name] = w["w_in"][:, off:off + size]
        off += size
    w_gab = jnp.concatenate([seg["ga"], seg["gb"], jnp.zeros((D_MODEL, HEAD - 32), BF16)], axis=1)
    big_segs = [n for n in IN_NAMES if n not in ("ga", "gb")]
    conv8 = jnp.concatenate([w["gdn_conv_w"].astype(F32), jnp.zeros((8 - CONV_K, 4096), F32)], axis=0)
    conv_q, conv_k, conv_v = conv8[:, :1024], conv8[:, 1024:2048], conv8[:, 2048:]
    w_main = jnp.concatenate([seg[n] for n in big_segs], axis=1)
    proj = _mm(u, w_main, name="proj")
    pr, off = {}, 0
    for n in big_segs:
        pr[n] = _view(proj, off, seg[n].shape[1])
        off += seg[n].shape[1]
    gab = _mm(u, w_gab, name="proj_gab")
    oh_raw, oh, s_h = _hgrn_fwd(pr["hq"], pr["hf"], pr["hi"], pr["hg"], logits, hg_gain + started, hg_c)
    qn = _conv_fwd(pr["gq"], conv_q, qscale, "conv_q")
    kn = _conv_fwd(pr["gk"], conv_k, 1.0, "conv_k")
    cv = _conv_fwd(pr["gv"], conv_v, None, "conv_v")
    og_raw, og, s_g, t_g = _gdn_fwd(qn, kn, cv, gab, pr["gz"], alog, dtb, gd_gain, gd_c)
    yh = _mm(oh, w["w_branch_hgrn"], name="branch_h")
    yg = _mm(og, w["w_branch_gdn"], name="branch_g")
    ym = _merge_fwd(yh, yg, pr["gate_h"], pr["gate_g"])
    h2 = _mm(ym, w["w_out"], res=h1, name="mix_out")
    w2 = exchange.weights("ffn2", after=h2)
    h3, ffn2_saved = _ffn_fwd(h2, g2, w2["ffn2_w_in"], w2["ffn2_w_out"], "ffn2")
    loss, dh3, dh3_bf, d_gf = _final_loss(h3, gf, tgt)

    dh2, dh2_bf, d_g2, d_f2in, d_f2out = _ffn_bwd(h2, g2, w2["ffn2_w_in"], w2["ffn2_w_out"], ffn2_saved, dh3, dh3_bf,
                                                  "ffn2")
    started = exchange.reduce("ffn2", {"ffn2_w_in": d_f2in, "ffn2_w_out": d_f2out}, behind=True)
    dym =_mm(dh2_bf, w["w_out"], tb=True, name="d_merge")
    d_wout = _mm(ym, dh2_bf, ta=True, out_dtype=BF16, name="d_w_out")
    dyh, dyg, d_gate_h, d_gate_g = _merge_bwd(dym, yh, yg, pr["gate_h"], pr["gate_g"])
    d_wbh = _mm(oh, dyh, ta=True, out_dtype=BF16, name="d_w_branch_h")
    d_wbg = _mm(og, dyg, ta=True, out_dtype=BF16, name="d_w_branch_g")
    doh = _mm(dyh, w["w_branch_hgrn"], tb=True, name="d_oh")
    dog = _mm(dyg, w["w_branch_gdn"], tb=True, name="d_og")
    d_hq, d_hf, d_hi, d_hg, d_hg_gain, d_lb0 = _hgrn_bwd(pr["hq"], pr["hf"], pr["hi"], pr["hg"], logits,
                                                        hg_gain + started, oh_raw, s_h, doh, hg_c)
    d_qn, d_kn, d_cv, d_gab_wide, d_gz, gd_small = _gdn_bwd(qn, kn, cv, gab, pr["gz"], alog, dtb, gd_gain, og_raw,
                                                            s_g, t_g, dog, gd_c)
    d_gab = _fold_groups(d_gab_wide)
    dc_q, dwc_q = _conv_bwd_a(pr["gq"], conv_q, d_qn, qscale, "dconv_q")
    dc_k, dwc_k = _conv_bwd_a(pr["gk"], conv_k, d_kn, 1.0, "dconv_k")
    dc_v, dwc_v = _conv_bwd_a(pr["gv"], conv_v, d_cv, None, "dconv_v")
    d_gq = _conv_bwd_b(dc_q, conv_q, "dconvx_q")
    d_gk = _conv_bwd_b(dc_k, conv_k, "dconvx_k")
    d_gv = _conv_bwd_b(dc_v, conv_v, "dconvx_v")
    dpr = {"hq": d_hq, "hf": d_hf, "hi": d_hi, "hg": d_hg, "gq": d_gq, "gk": d_gk, "gv": d_gv, "gz": d_gz,
           "gate_h": d_gate_h, "gate_g": d_gate_g}
    dproj = jnp.concatenate([dpr[n] for n in big_segs], axis=1)
    du = _mm(d_gab, w_gab, tb=True, name="du_gab")
    du = _mm(dproj, w_main, tb=True, res=du, name="du")
    d_wmain = _mm(u, dproj, ta=True, out_dtype=BF16, name="dw_main")
    d_wgab = _mm(u, d_gab, ta=True, out_dtype=BF16, name="dw_gab")
    cut = IN_WIDTH // 4
    d_win = jnp.stack([d_wmain[:, :cut], d_wmain[:, cut:2 * cut],
                       jnp.concatenate([d_wmain[:, 2 * cut:8192], d_wgab[:, :32], d_wmain[:, 8192:3 * cut - 32]], axis=1),
                       d_wmain[:, 3 * cut - 32:]])
    d_conv = jnp.concatenate([dwc_q[:CONV_K], dwc_k[:CONV_K], dwc_v[:CONV_K]], axis=1).astype(BF16)
    started = exchange.reduce("mixer", {"w_in": d_win, "gdn_conv_w": d_conv, "w_branch_hgrn": d_wbh,
                                        "w_branch_gdn": d_wbg, "w_out": d_wout}, behind=True)
    dh1, dh1_bf, d_gm = _rmsnorm_bwd(h1, gm + started, du, dh2, "mix_dnorm")
    dx, _, d_g1, d_f1in, d_f1out = _ffn_bwd(x, g1, w1["ffn1_w_in"], w1["ffn1_w_out"], ffn1_saved, dh1, dh1_bf, "ffn1")
    exchange.reduce("ffn1", {"ffn1_w_in": d_f1in, "ffn1_w_out": d_f1out}, behind=True)
    d_lb0 = d_lb0.reshape(1, -1)
    sm = {"ffn1_norm": d_g1, "mix_norm": d_gm, "hgrn_lb_logits": jnp.concatenate([d_lb0, -d_lb0], axis=0),
          "hgrn_out_norm": d_hg_gain, "gdn_a_log": gd_small[2, :16], "gdn_dt_bias": gd_small[1, :16],
          "gdn_out_norm": gd_small[0], "ffn2_norm": d_g2, "final_norm": d_gf, "loss": loss[0, :1]}
    return dx, sm


class _Exchange:
    def __init__(self, wts):
        self.wts = wts
        xi, yi, ci = _place()
        self.chip = 2 * xi + yi
        self.south = ci == 0
        self.core = ci.reshape(1).astype(jnp.int32)
        self.mine = {}
        self.coming = {}
        self.going = {}

    def _packs(self, tag):
        group = _GROUPS[tag]
        return _pack({n: self.wts[n][0].astype(BF16) for n in _group_names(group)}, (), group)

    def prefetch(self, tag):
        packs = self._packs(tag)
        handle, token = _ici_start(packs, _GROUPS[tag]["chunks"], "gather", "gather_start_" + tag)
        self.coming[tag] = handle
        return token[0:1, 0:1]

    def weights(self, tag, after=None):
        group = _GROUPS[tag]
        if tag in self.coming:
            packs, halves = _ici_wait(self.coming.pop(tag), after, "gather_wait_" + tag)
            others = _pass_to_sibling(halves, group["chunks"], "gather_pass_" + tag)
        else:
            packs = self._packs(tag)
            others = _gather_weights(packs, group["chunks"], "gather_" + tag)
        whole = [lax.dynamic_update_index_in_dim(g, p, self.chip, 0) for g, p in zip(others, packs)]
        gathered = _unpack(*whole, group)
        return {n: _full_from_shards(n, gathered[n]) for n in _group_names(group)}

    def reduce(self, tag, grads, behind=False):
        group = _GROUPS[tag]
        shards = {n: (grads[n] if grads[n].ndim == 3 else _shards_from_full(n, grads[n])) for n in _group_names(group)}
        gpacks = _pack(shards, (4,), group)
        got = _swap_with_sibling(gpacks, group["chunks"], 4, "reduce_pair_" + tag, halves=True)
        sums = [_add2(a, b, self.core, "add_pair_%s_%d" % (tag, i)) for i, (a, b) in enumerate(zip(gpacks, got))]
        if behind:
            handle, token = _ici_start(sums, group["chunks"], "reduce", "reduce_start_" + tag)
            self.going[tag] = handle
            self.token = token
            return token[0:1, 0:1]
        self._add_chips(tag, sums, _reduce_chips(sums, group["chunks"], "reduce_chips_" + tag))
        return None

    def _add_chips(self, tag, sums, from_chips):
        self.mine[tag] = [_add4(lax.dynamic_index_in_dim(s, self.chip, axis=0, keepdims=False), f,
                                "add_chips_%s_%d" % (tag, i)) for i, (s, f) in enumerate(zip(sums, from_chips))]

    def finish(self, tags, after):
        for tag in tags:
            if tag in self.going:
                self._add_chips(tag, *_ici_wait(self.going.pop(tag), after, "reduce_wait_" + tag))
        mine = [a for t in tags for a in self.mine[t]]
        nchs = [k for t in tags for k in _GROUPS[t]["chunks"]]
        theirs = _swap_with_sibling(mine, nchs, 0, "share_pair_" + tags[0])
        whole = [jnp.concatenate([jnp.where(self.south, a, b), jnp.where(self.south, b, a)], axis=0)
                 for a, b in zip(mine, theirs)]
        reduced = {}
        for i, t in enumerate(tags):
            reduced.update(_unpack(whole[2 * i], whole[2 * i + 1], _GROUPS[t]))
        return reduced


_WEIGHTS = ("ffn1_norm", "ffn1_w_in", "ffn1_w_out", "mix_norm", "w_in", "hgrn_lb_logits", "hgrn_out_norm",
            "gdn_conv_w", "gdn_a_log", "gdn_dt_bias", "gdn_out_norm", "w_branch_hgrn", "w_branch_gdn", "w_out",
            "ffn2_norm", "ffn2_w_in", "ffn2_w_out", "final_norm")


def kernel(x, ffn1_norm, ffn1_w_in, ffn1_w_out, mix_norm, w_in, hgrn_lb_logits, hgrn_out_norm, gdn_conv_w, gdn_a_log, gdn_dt_bias, gdn_out_norm, w_branch_hgrn, w_branch_gdn, w_out, ffn2_norm, ffn2_w_in, ffn2_w_out, final_norm, loss_target, m_ffn1_norm, m_ffn1_w_in, m_ffn1_w_out, m_mix_norm, m_w_in, m_hgrn_lb_logits, m_hgrn_out_norm, m_gdn_conv_w, m_gdn_a_log, m_gdn_dt_bias, m_gdn_out_norm, m_w_branch_hgrn, m_w_branch_gdn, m_w_out, m_ffn2_norm, m_ffn2_w_in, m_ffn2_w_out, m_final_norm, v_ffn1_norm, v_ffn1_w_in, v_ffn1_w_out, v_mix_norm, v_w_in, v_hgrn_lb_logits, v_hgrn_out_norm, v_gdn_conv_w, v_gdn_a_log, v_gdn_dt_bias, v_gdn_out_norm, v_w_branch_hgrn, v_w_branch_gdn, v_w_out, v_ffn2_norm, v_ffn2_w_in, v_ffn2_w_out, v_final_norm):
    args = dict(locals())
    wts = {n: args[n] for n in _WEIGHTS}
    moms = {n: args["m_" + n] for n in _WEIGHTS}
    vars_ = {n: args["v_" + n] for n in _WEIGHTS}

    small = {n: wts[n].astype(F32) for n in _WEIGHTS if n not in _BIG_NAMES}
    exchange = _Exchange(wts)
    dx, small_grads = _local_step(x[0], loss_target[0], small, exchange)

    out_g, out_d, out_m, out_v = {}, {}, {}, {}

    def update(tags, reduced, after):
        for t in tags:
            for n in _group_names(_GROUPS[t]):
                shape = wts[n].shape
                w2 = wts[n].reshape(shape[-2], shape[-1])
                g2 = reduced[n]
                d, m2, v2 = _adamw(w2, g2, moms[n].reshape(w2.shape), vars_[n].reshape(w2.shape), "adamw_" + n, after)
                out_g[n], out_d[n], out_m[n], out_v[n] = (g2.reshape(shape), d.reshape(shape), m2.reshape(shape),
                                                          v2.reshape(shape))
                after = v2
        return after

    done = update(("ffn2", "mixer"), exchange.finish(("ffn2", "mixer"), after=dx), exchange.token)
    update(("ffn1",), exchange.finish(("ffn1",), after=done), None)

    small_names = [n for n, _ in _SMALL]
    zero = jnp.zeros((1,), F32)
    shapes = {n: (wts[n].shape if n != "loss" else (1,)) for n in small_names}
    sums, sd, sm_, sv = _small_sync(
        _pack_small(small_grads),
        _pack_small({n: (wts[n] if n != "loss" else zero) for n in small_names}),
        _pack_small({n: (moms[n] if n != "loss" else zero) for n in small_names}),
        _pack_small({n: (vars_[n] if n != "loss" else zero) for n in small_names}))
    sg_u, sd_u, sm_u, sv_u = (_unpack_small(p, shapes) for p in (sums, sd, sm_, sv))
    for n in small_names:
        if n != "loss":
            out_g[n], out_d[n], out_m[n], out_v[n] = sg_u[n], sd_u[n], sm_u[n], sv_u[n]
    loss = sg_u["loss"].reshape(())

    return (loss, dx[None], *[out_g[n] for n in _WEIGHTS], *[out_d[n] for n in _WEIGHTS],
            *[out_m[n] for n in _WEIGHTS], *[out_v[n] for n in _WEIGHTS])
```

```python
import numpy as np

import jax
import jax.numpy as jnp
from jax import lax
from jax.experimental import pallas as pl
from jax.experimental.pallas import tpu as pltpu

F32 = jnp.float32
BF16 = jnp.bfloat16

D_MODEL = 1024
D_FF = 2816
CHUNK = 64
HEAD = 128
HG_HEADS = 8
GD_HEADS = 16
HPS = 8
COMM_CHUNKS = 9
MM_TM = 1408
MM_TN = 1024
MM_TK = 1536
VMEM_LIMIT = 48 * 1024 * 1024
EPS = 1e-6
CONV_K = 4
IN_NAMES = ("hq", "hf", "hi", "hg", "gq", "gk", "gv", "ga", "gb", "gz", "gate_h", "gate_g")
IN_SIZES = (1024, 1024, 1024, 1024, 1024, 1024, 2048, 16, 16, 2048, 1024, 1024)
IN_WIDTH = sum(IN_SIZES)

ADAM_LR = 0.001
ADAM_B1 = 0.9
ADAM_B2 = 0.999
ADAM_EPS = 1e-08
ADAM_WD = 0.01
ADAM_STEP = 10

MESH = pl.DeviceIdType.MESH
_ARB = "arbitrary"
_PAR = "parallel"


def _bf(x):
    return x.astype(BF16)


def _dot(a, b):
    return jnp.dot(_bf(a), _bf(b), preferred_element_type=F32)


def _dot_nt(a, b):
    return lax.dot_general(_bf(a), _bf(b), (((1,), (1,)), ((), ())), preferred_element_type=F32)


def _dot_tn(a, b):
    return lax.dot_general(_bf(a), _bf(b), (((0,), (0,)), ((), ())), preferred_element_type=F32)


def _split3(x):
    hi = _bf(x)
    r = x - hi.astype(F32)
    mid = _bf(r)
    lo = _bf(r - mid.astype(F32))
    return hi, mid, lo


def _dot_mx(m, x):
    hi, mid, lo = _split3(x)
    return (jnp.dot(m, hi, preferred_element_type=F32) + jnp.dot(m, mid, preferred_element_type=F32)
            + jnp.dot(m, lo, preferred_element_type=F32))


def _dot_xm(x, m):
    hi, mid, lo = _split3(x)
    return (jnp.dot(hi, m, preferred_element_type=F32) + jnp.dot(mid, m, preferred_element_type=F32)
            + jnp.dot(lo, m, preferred_element_type=F32))


def _dot_hp(a, b):
    ah = _bf(a)
    al = _bf(a - ah.astype(F32))
    bh = _bf(b)
    bl = _bf(b - bh.astype(F32))
    return (jnp.dot(ah, bh, preferred_element_type=F32) + jnp.dot(ah, bl, preferred_element_type=F32)
            + jnp.dot(al, bh, preferred_element_type=F32))


def _sigmoid(x):
    return jax.nn.sigmoid(x)


def _silu(x):
    return x * _sigmoid(x)


def _dsilu(x):
    s = _sigmoid(x)
    return s * (1.0 + x * (1.0 - s))


def _softplus(x):
    return jnp.maximum(x, 0.0) + jnp.log(1.0 + jnp.exp(-jnp.abs(x)))


def _rowsum(x):
    return jnp.sum(x, axis=1, keepdims=True)


def _col_to_row(col, eye):
    return jnp.sum(eye * col, axis=0, keepdims=True)


def _row_to_col(row, eye):
    return jnp.sum(eye * row, axis=1, keepdims=True)


def _pick(dim, pref, unit=128):
    if dim <= pref:
        return dim
    t = pref
    while t >= unit:
        if dim % t == 0:
            return t
        t -= unit
    return dim


def _params(*sem):
    return pltpu.CompilerParams(dimension_semantics=tuple(sem), vmem_limit_bytes=VMEM_LIMIT)


def _mm(a, b, *, ta=False, tb=False, alpha=1.0, res=None, out_dtype=F32, name="mm", b_from=0):
    m = a.shape[1] if ta else a.shape[0]
    k = a.shape[0] if ta else a.shape[1]
    n = b.shape[0] if tb else b.shape[1]
    assert b_from + k <= (b.shape[1] if tb else b.shape[0])
    tm, tn, tk = _pick(m, MM_TM), _pick(n, MM_TN), _pick(k, MM_TK)
    if tn < MM_TN < n and n % MM_TM == 0:
        tn = MM_TM
    nk = k // tk
    assert b_from % tk == 0
    b0 = b_from // tk
    a_spec = pl.BlockSpec((tk, tm), lambda i, j, l: (l, i)) if ta else pl.BlockSpec((tm, tk), lambda i, j, l: (i, l))
    b_spec = (pl.BlockSpec((tn, tk), lambda i, j, l: (j, b0 + l)) if tb
              else pl.BlockSpec((tk, tn), lambda i, j, l: (b0 + l, j)))
    o_spec = pl.BlockSpec((tm, tn), lambda i, j, l: (i, j))
    dims = (((0 if ta else 1,), (1 if tb else 0,)), ((), ()))
    has_res = res is not None

    def finish(r, r_ref, o_ref):
        if alpha != 1.0:
            r = r * alpha
        if has_res:
            r = r + r_ref[...]
        o_ref[...] = r.astype(out_dtype)

    def body(*refs):
        a_ref, b_ref = refs[0], refs[1]
        r_ref = refs[2] if has_res else None
        o_ref = refs[3] if has_res else refs[2]
        part = lax.dot_general(_bf(a_ref[...]), _bf(b_ref[...]), dims, preferred_element_type=F32)
        if nk == 1:
            finish(part, r_ref, o_ref)
            return
        acc = refs[-1]
        step = pl.program_id(2)

        @pl.when(step == 0)
        def _():
            acc[...] = part

        @pl.when(step != 0)
        def _():
            acc[...] += part

        @pl.when(step == nk - 1)
        def _():
            finish(acc[...], r_ref, o_ref)

    ins = [a, b] + ([res] if has_res else [])
    in_specs = [a_spec, b_spec] + ([o_spec] if has_res else [])
    return pl.pallas_call(
        body, name=name, grid=(m // tm, n // tn, nk), in_specs=in_specs, out_specs=o_spec,
        out_shape=jax.ShapeDtypeStruct((m, n), out_dtype),
        scratch_shapes=[pltpu.VMEM((tm, tn), F32)] if nk > 1 else [],
        compiler_params=_params(_PAR, _PAR, _ARB))(*ins)


def _row_spec(tr, w):
    return pl.BlockSpec((tr, w), lambda i: (i, 0))


def _full_spec(shape):
    return pl.BlockSpec(shape, lambda i: tuple(0 for _ in shape))


def _view(arr, off, width):
    return arr, off, width


def _view_rows(view, tr):
    _, off, width = view
    assert off % width == 0
    return pl.BlockSpec((tr, width), lambda i: (i, off // width))


def _view_tile(view, rows, bw, cidx=lambda c: c):
    _, off, width = view
    assert off % bw == 0 and width % bw == 0
    return pl.BlockSpec((rows, bw), lambda c, g: (cidx(c), off // bw + g))


def _rmsnorm_fwd(x, g, name):
    t, d = x.shape
    tr = _pick(t, 256, 8)

    def body(x_ref, g_ref, o_ref):
        xv = x_ref[...]
        r = lax.rsqrt(jnp.mean(xv * xv, axis=1, keepdims=True) + EPS)
        o_ref[...] = (xv * r * g_ref[...]).astype(BF16)

    return pl.pallas_call(
        body, name=name, grid=(t // tr,), in_specs=[_row_spec(tr, d), _full_spec((1, d))],
        out_specs=_row_spec(tr, d), out_shape=jax.ShapeDtypeStruct((t, d), BF16),
        compiler_params=_params(_PAR))(x, g)


def _rmsnorm_bwd(x, g, dn, res, name):
    t, d = x.shape
    tr = _pick(t, 256, 8)

    def body(x_ref, g_ref, dn_ref, r_ref, dx_ref, dxb_ref, dg_ref):
        @pl.when(pl.program_id(0) == 0)
        def _():
            dg_ref[...] = jnp.zeros_like(dg_ref)

        xv = x_ref[...]
        r = lax.rsqrt(jnp.mean(xv * xv, axis=1, keepdims=True) + EPS)
        xh = xv * r
        dy = dn_ref[...]
        dg_ref[...] += jnp.sum(dy * xh, axis=0, keepdims=True)
        dxh = dy * g_ref[...]
        dx = r_ref[...] + r * (dxh - xh * jnp.mean(dxh * xh, axis=1, keepdims=True))
        dx_ref[...] = dx
        dxb_ref[...] = dx.astype(BF16)

    return pl.pallas_call(
        body, name=name, grid=(t // tr,),
        in_specs=[_row_spec(tr, d), _full_spec((1, d)), _row_spec(tr, d), _row_spec(tr, d)],
        out_specs=[_row_spec(tr, d), _row_spec(tr, d), _full_spec((1, d))],
        out_shape=[jax.ShapeDtypeStruct((t, d), F32), jax.ShapeDtypeStruct((t, d), BF16),
                   jax.ShapeDtypeStruct((1, d), F32)],
        compiler_params=_params(_ARB))(x, g, dn, res)


FFN_TN = 1408
FFN_TM = 512


def _ffn_pieces():
    return [slice(c, min(c + 256, FFN_TN)) for c in range(0, FFN_TN, 256)]


def _ffn_in_act(n, w_in, name):
    t, d = n.shape
    tm = _pick(t, FFN_TM)
    nf = D_FF // FFN_TN

    def body(n_ref, wa_ref, wb_ref, a_ref, b_ref, hm_ref):
        nv = n_ref[...]
        for cols in _ffn_pieces():
            a = jnp.dot(nv, wa_ref[:, cols], preferred_element_type=F32)
            b = jnp.dot(nv, wb_ref[:, cols], preferred_element_type=F32)
            a_ref[:, cols] = a.astype(BF16)
            b_ref[:, cols] = b.astype(BF16)
            hm_ref[:, cols] = (_silu(a) * b).astype(BF16)

    tile = pl.BlockSpec((tm, FFN_TN), lambda i, j: (i, j))
    return pl.pallas_call(
        body, name=name, grid=(t // tm, nf),
        in_specs=[pl.BlockSpec((tm, d), lambda i, j: (i, 0)), pl.BlockSpec((d, FFN_TN), lambda i, j: (0, j)),
                  pl.BlockSpec((d, FFN_TN), lambda i, j: (0, nf + j))],
        out_specs=[tile, tile, tile], out_shape=[jax.ShapeDtypeStruct((t, D_FF), BF16)] * 3,
        compiler_params=_params(_PAR, _PAR))(n, w_in, w_in)


def _ffn_dact(dout, w_out, a, b, name):
    t, d = dout.shape
    tm = _pick(t, FFN_TM)

    def body(do_ref, w_ref, a_ref, b_ref, da_ref, db_ref, hm_ref):
        dov = do_ref[...]
        for cols in _ffn_pieces():
            dh = 0.5 * _dot_nt(dov, w_ref[cols, :])
            av = a_ref[:, cols].astype(F32)
            bv = b_ref[:, cols].astype(F32)
            sg = _sigmoid(av)
            sa = av * sg
            da_ref[:, cols] = (dh * bv * (sg * (1.0 + av * (1.0 - sg)))).astype(BF16)
            db_ref[:, cols] = (dh * sa).astype(BF16)
            hm_ref[:, cols] = (sa * bv).astype(BF16)

    tile = pl.BlockSpec((tm, FFN_TN), lambda i, j: (i, j))
    return pl.pallas_call(
        body, name=name, grid=(t // tm, D_FF // FFN_TN),
        in_specs=[pl.BlockSpec((tm, d), lambda i, j: (i, 0)), pl.BlockSpec((FFN_TN, d), lambda i, j: (j, 0)), tile, tile],
        out_specs=[tile, tile, tile], out_shape=[jax.ShapeDtypeStruct((t, D_FF), BF16)] * 3,
        compiler_params=_params(_PAR, _PAR))(dout, w_out, a, b)


def _merge_fwd(yh, yg, gh, gg):
    t, d = yh.shape
    tr = _pick(t, 256, 8)

    def body(yh_ref, yg_ref, gh_ref, gg_ref, o_ref):
        o_ref[...] = (_sigmoid(gh_ref[...]) * yh_ref[...] + _sigmoid(gg_ref[...]) * yg_ref[...]).astype(BF16)

    return pl.pallas_call(
        body, name="merge_fwd", grid=(t // tr,),
        in_specs=[_row_spec(tr, d), _row_spec(tr, d), _view_rows(gh, tr), _view_rows(gg, tr)],
        out_specs=_row_spec(tr, d),
        out_shape=jax.ShapeDtypeStruct((t, d), BF16), compiler_params=_params(_PAR))(yh, yg, gh[0], gg[0])


def _merge_bwd(dy, yh, yg, gh, gg):
    t, d = yh.shape
    tr = _pick(t, 256, 8)

    def body(dy_ref, yh_ref, yg_ref, gh_ref, gg_ref, dyh_ref, dyg_ref, dgh_ref, dgg_ref):
        dyv = dy_ref[...]
        sh = _sigmoid(gh_ref[...])
        sg = _sigmoid(gg_ref[...])
        dyh_ref[...] = (dyv * sh).astype(BF16)
        dyg_ref[...] = (dyv * sg).astype(BF16)
        dgh_ref[...] = (dyv * yh_ref[...] * sh * (1.0 - sh)).astype(BF16)
        dgg_ref[...] = (dyv * yg_ref[...] * sg * (1.0 - sg)).astype(BF16)

    return pl.pallas_call(
        body, name="merge_bwd", grid=(t // tr,),
        in_specs=[_row_spec(tr, d)] * 3 + [_view_rows(gh, tr), _view_rows(gg, tr)], out_specs=[_row_spec(tr, d)] * 4,
        out_shape=[jax.ShapeDtypeStruct((t, d), BF16)] * 4,
        compiler_params=_params(_PAR))(dy, yh, yg, gh[0], gg[0])


def _final_loss(h, g, tgt):
    t, d = h.shape
    tr = _pick(t, 256, 8)

    def body(h_ref, g_ref, t_ref, loss_ref, dh_ref, dhb_ref, dg_ref):
        @pl.when(pl.program_id(0) == 0)
        def _():
            dg_ref[...] = jnp.zeros_like(dg_ref)
            loss_ref[...] = jnp.zeros_like(loss_ref)

        xv = h_ref[...]
        gv = g_ref[...]
        r = lax.rsqrt(jnp.mean(xv * xv, axis=1, keepdims=True) + EPS)
        xh = xv * r
        err = xh * gv - t_ref[...]
        loss_ref[...] += 0.5 * jnp.sum(jnp.mean(err * err, axis=1, keepdims=True), axis=0, keepdims=True)
        dy = err * (1.0 / d)
        dg_ref[...] += jnp.sum(dy * xh, axis=0, keepdims=True)
        dxh = dy * gv
        dh = r * (dxh - xh * jnp.mean(dxh * xh, axis=1, keepdims=True))
        dh_ref[...] = dh
        dhb_ref[...] = dh.astype(BF16)

    return pl.pallas_call(
        body, name="final_loss", grid=(t // tr,),
        in_specs=[_row_spec(tr, d), _full_spec((1, d)), _row_spec(tr, d)],
        out_specs=[_full_spec((1, 128)), _row_spec(tr, d), _row_spec(tr, d), _full_spec((1, d))],
        out_shape=[jax.ShapeDtypeStruct((1, 128), F32), jax.ShapeDtypeStruct((t, d), F32),
                   jax.ShapeDtypeStruct((t, d), BF16), jax.ShapeDtypeStruct((1, d), F32)],
        compiler_params=_params(_ARB))(h, g, tgt)


def _hg_consts():
    c = CHUNK
    t = np.arange(c)
    mats, masks = [], []
    for lvl in range(6):
        m = 1 << lvl
        blk = t // m
        mat = np.zeros((c, c), np.float32)
        for tt in range(c):
            b = blk[tt]
            if b % 2 == 1:
                mat[tt, b * m:tt + 1] = 1.0
            else:
                mat[tt, tt + 1:(b + 1) * m] = 1.0
        mats.append(mat)
        same = (t[:, None] // (2 * m)) == (t[None, :] // (2 * m))
        masks.append((same & (blk[:, None] % 2 == 1) & (blk[None, :] % 2 == 0)).astype(np.float32))
    pre = np.tril(np.ones((c, c), np.float32))
    suf = np.triu(np.ones((c, c), np.float32), 1)
    mstack = np.concatenate(mats + [pre, suf], 0)
    masks.append(np.eye(c, dtype=np.float32))
    return (jnp.asarray(mstack, BF16), jnp.asarray(mstack.T.copy(), BF16), jnp.asarray(np.stack(masks), F32),
            jnp.asarray(np.eye(HEAD, dtype=np.float32)))


def _gd_consts():
    c = CHUNK
    incl = np.tril(np.ones((c, c), np.float32))
    strict = np.tril(np.ones((c, c), np.float32), -1)
    eye = np.eye(c, dtype=np.float32)
    masks = np.stack([incl, strict, eye, incl.T.copy()])
    return jnp.asarray(incl, BF16), jnp.asarray(incl.T.copy(), BF16), jnp.asarray(masks, F32)


def _chunks_per_step(nc):
    for cb in (32 // HPS, 2, 1):
        if nc % cb == 0:
            return cb
    return 1


def _hg_prep(hq, hf, lg):
    lb = _sigmoid(lg[0:1, :] - lg[1:2, :])
    sg = _sigmoid(hf)
    sgn = _sigmoid(-hf)
    f = lb + (1.0 - lb) * sg
    lf = jnp.log(f)
    kk = (1.0 - lb) * sgn
    q = _silu(hq) * (HEAD ** -0.5)
    return lb, sg, sgn, f, lf, kk, q


def _mx_each(m, xs):
    hi, lo = _split2_each(xs)
    prods = [jnp.dot(m, jnp.concatenate([h, l], axis=1), preferred_element_type=F32) for h, l in zip(hi, lo)]
    return [p[:, :HEAD] + p[:, HEAD:] for p in prods]


def _hg_scaled(x, ex):
    xb = [_bf(a) for a in x]
    eb = [_bf(e[:6 * CHUNK]) for e in ex]
    return [[a * e[lvl * CHUNK:(lvl + 1) * CHUNK] for lvl in range(6)] for a, e in zip(xb, eb)]


def _hg_scores(q, kk, qe, ke, mask_ref):
    p = [mask_ref[6] * _rowsum(a * b) for a, b in zip(q, kk)]
    for lvl in range(6):
        d = [_dot_nt(a[lvl], b[lvl]) for a, b in zip(qe, ke)]
        p = [x + mask_ref[lvl] * y for x, y in zip(p, d)]
    return p


def _hgrn_fwd(hq, hf, hi, hg, logits, gain, consts):
    t = hq[0].shape[0]
    nc = t // CHUNK
    cb = _chunks_per_step(nc)
    rows = cb * CHUNK
    mstack, _, masks, eye = consts
    tile = pl.BlockSpec((rows, HPS * HEAD), lambda c, g: (c, g))

    def body(hq_ref, hf_ref, hi_ref, hg_ref, lg_ref, gain_ref, m_ref, mask_ref, eye_ref,
             oraw_ref, og_ref, ssave_ref, state):
        c = pl.program_id(0)
        g = pl.program_id(1)

        @pl.when(c == 0)
        def _():
            for hh in range(HPS):
                state[g * HPS + hh] = jnp.zeros((HEAD, HEAD), F32)

        lg_all = lg_ref[...]
        gain_v = gain_ref[...]

        def one(i, carry):
            sl = pl.ds(pl.multiple_of(i * CHUNK, CHUNK), CHUNK)
            hs = range(HPS)
            heads = [g * HPS + hh for hh in hs]
            ln = [slice(hh * HEAD, (hh + 1) * HEAD) for hh in hs]
            preps = [_hg_prep(hq_ref[sl, s], hf_ref[sl, s], lg_all[:, s]) for s in ln]
            lf, kk, q = [p[4] for p in preps], [p[5] for p in preps], [p[6] for p in preps]
            v = [hi_ref[sl, s] for s in ln]
            ex = [jnp.exp(x) for x in _mx_each(m_ref[...], lf)]
            eb = [e[6 * CHUNK:7 * CHUNK] for e in ex]
            esfx = [e[7 * CHUNK:8 * CHUNK] for e in ex]
            qe, ke = _hg_scaled(q, ex), _hg_scaled(kk, ex)
            p = _hg_scores(q, kk, qe, ke, mask_ref)
            s0 = [state[h] for h in heads]
            o = _each(lambda a, e, s, pp, vv: _dot(a * e, s) + _dot(pp, vv), q, eb, s0, p, v)
            eye_v = eye_ref[...]
            s1 = _each(lambda s, e, kx, ef, vv: s * _row_to_col(e[CHUNK - 1:CHUNK, :], eye_v) + _dot_tn(kx * ef, vv),
                       s0, eb, kk, esfx, v)
            for hh in hs:
                ssave_ref[i, hh] = s0[hh]
                state[heads[hh]] = s1[hh]
                oraw_ref[sl, ln[hh]] = o[hh]
                r = lax.rsqrt(jnp.mean(o[hh] * o[hh], axis=1, keepdims=True) + EPS)
                og_ref[sl, ln[hh]] = (o[hh] * r * gain_v * _silu(hg_ref[sl, ln[hh]])).astype(BF16)
            return carry

        lax.fori_loop(0, cb, one, 0, unroll=4)

    return pl.pallas_call(
        body, name="hgrn_fwd", grid=(nc // cb, HG_HEADS // HPS),
        in_specs=[_view_tile(v, rows, HPS * HEAD) for v in (hq, hf, hi, hg)] + [
                  pl.BlockSpec((2, HPS * HEAD), lambda c, g: (0, g)),
                  pl.BlockSpec((1, HEAD), lambda c, g: (0, 0)),
                  pl.BlockSpec(mstack.shape, lambda c, g: (0, 0)),
                  pl.BlockSpec(masks.shape, lambda c, g: (0, 0, 0)),
                  pl.BlockSpec(eye.shape, lambda c, g: (0, 0))],
        out_specs=[tile, tile, pl.BlockSpec((cb, HPS, HEAD, HEAD), lambda c, g: (c, g, 0, 0))],
        out_shape=[jax.ShapeDtypeStruct((t, HG_HEADS * HEAD), F32), jax.ShapeDtypeStruct((t, HG_HEADS * HEAD), BF16),
                   jax.ShapeDtypeStruct((nc, HG_HEADS, HEAD, HEAD), F32)],
        scratch_shapes=[pltpu.VMEM((HG_HEADS, HEAD, HEAD), F32)],
        compiler_params=_params(_ARB, _ARB))(hq[0], hf[0], hi[0], hg[0], logits, gain, mstack, masks, eye)


def _hgrn_bwd(hq, hf, hi, hg, logits, gain, oraw, ssave, dog, consts):
    t = hq[0].shape[0]
    nc = t // CHUNK
    cb = _chunks_per_step(nc)
    rows = cb * CHUNK
    nb = nc // cb
    mstack, mstack_t, masks, eye = consts
    tile = pl.BlockSpec((rows, HPS * HEAD), lambda c, g: (nb - 1 - c, g))

    def body(hq_ref, hf_ref, hi_ref, hg_ref, lg_ref, gain_ref, oraw_ref, ssave_ref, dog_ref, m_ref, mt_ref,
             mask_ref, eye_ref, dhq_ref, dhf_ref, dhi_ref, dhg_ref, dgain_ref, dlb_ref, dstate):
        c = pl.program_id(0)
        g = pl.program_id(1)

        @pl.when(c == 0)
        def _():
            for hh in range(HPS):
                dstate[g * HPS + hh] = jnp.zeros((HEAD, HEAD), F32)

        @pl.when((c == 0) & (g == 0))
        def _():
            dgain_ref[...] = jnp.zeros_like(dgain_ref)
            dlb_ref[...] = jnp.zeros_like(dlb_ref)

        lg_all = lg_ref[...]
        gain_v = gain_ref[...]
        eye_v = eye_ref[...]
        last_row = (lax.broadcasted_iota(jnp.int32, (CHUNK, HEAD), 0) == CHUNK - 1).astype(F32)

        def one(j, carry):
            i = cb - 1 - j
            sl = pl.ds(pl.multiple_of(i * CHUNK, CHUNK), CHUNK)
            hs = range(HPS)
            heads = [g * HPS + hh for hh in hs]
            ln = [slice(hh * HEAD, (hh + 1) * HEAD) for hh in hs]
            hqv = [hq_ref[sl, s] for s in ln]
            hgv = [hg_ref[sl, s] for s in ln]
            preps = [_hg_prep(a, hf_ref[sl, s], lg_all[:, s]) for a, s in zip(hqv, ln)]
            lb, sg, sgn, f, lf, kk, q = ([p[n] for p in preps] for n in range(7))
            v = [hi_ref[sl, s] for s in ln]
            ex = [jnp.exp(x) for x in _mx_each(m_ref[...], lf)]
            eb = [e[6 * CHUNK:7 * CHUNK] for e in ex]
            esfx = [e[7 * CHUNK:8 * CHUNK] for e in ex]
            qe, ke = _hg_scaled(q, ex), _hg_scaled(kk, ex)
            p = _hg_scores(q, kk, qe, ke, mask_ref)
            s0 = [ssave_ref[i, hh] for hh in hs]
            ds = [dstate[h] for h in heads]

            o = [oraw_ref[sl, s] for s in ln]
            r = [lax.rsqrt(jnp.mean(x * x, axis=1, keepdims=True) + EPS) for x in o]
            on = _each(lambda x, y: x * y, o, r)
            dg_out = [dog_ref[sl, s] for s in ln]
            sgate = [_silu(x) for x in hgv]
            for hh in hs:
                dhg_ref[sl, ln[hh]] = (dg_out[hh] * on[hh] * gain_v * _dsilu(hgv[hh])).astype(BF16)
            dgain_ref[...] += sum(jnp.sum(d * s * n, axis=0, keepdims=True) for d, s, n in zip(dg_out, sgate, on))
            don = _each(lambda d, s: d * s * gain_v, dg_out, sgate)
            do = _each(lambda rr, dn, n: rr * (dn - n * jnp.mean(dn * n, axis=1, keepdims=True)), r, don, on)

            dp = _each(_dot_nt, do, v)
            dv = _each(lambda pp, d, kx, ef, s: _dot_tn(pp, d) + _dot(kx * ef, s), p, do, kk, esfx, ds)
            dqb = _each(_dot_nt, do, s0)
            dkx = _each(_dot_nt, v, ds)
            diag = [_rowsum(mask_ref[6] * x) for x in dp]
            dq = _each(lambda a, e, d, kx: a * e + d * kx, dqb, eb, diag, kk)
            dk = _each(lambda a, e, d, qq: a * e + d * qq, dkx, esfx, diag, q)
            dxs = [[] for _ in hs]
            for lvl in range(6):
                el = [e[lvl * CHUNK:(lvl + 1) * CHUNK] for e in ex]
                gm = [mask_ref[lvl] * x for x in dp]
                gm = [_bf(x) for x in gm]
                a1 = _each(lambda m_, kx: _dot(m_, kx[lvl]), gm, ke)
                a2 = _each(lambda m_, qq: _dot_tn(m_, qq[lvl]), gm, qe)
                dq = _each(lambda x, a, e: x + a * e, dq, a1, el)
                dk = _each(lambda x, a, e: x + a * e, dk, a2, el)
                for hh in hs:
                    dxs[hh].append((a1[hh] * q[hh] + a2[hh] * kk[hh]) * el[hh])
            e_end_row = [e[CHUNK - 1:CHUNK, :] for e in eb]
            ds_new = _each(lambda qq, e, d, er, s: _dot_tn(qq * e, d) + _row_to_col(er, eye_v) * s, q, eb, do, e_end_row, ds)
            for hh in hs:
                dstate[heads[hh]] = ds_new[hh]
                dend_row = _col_to_row(_rowsum(s0[hh] * ds[hh]), eye_v)
                dxs[hh].append(dqb[hh] * q[hh] * eb[hh] + last_row * (e_end_row[hh] * dend_row))
                dxs[hh].append(dkx[hh] * kk[hh] * esfx[hh])
            dlf = _mx_each(mt_ref[...], [jnp.concatenate(x, axis=0) for x in dxs])

            for hh in hs:
                dhi_ref[sl, ln[hh]] = dv[hh].astype(BF16)
                dhq_ref[sl, ln[hh]] = (dq[hh] * (HEAD ** -0.5) * _dsilu(hqv[hh])).astype(BF16)
                df = dlf[hh] / f[hh]
                dsig = (1.0 - lb[hh]) * sg[hh] * sgn[hh]
                dhf_ref[sl, ln[hh]] = ((df - dk[hh]) * dsig).astype(BF16)
                dlb_t = jnp.sum(df * sgn[hh] - dk[hh] * sgn[hh], axis=0, keepdims=True)
                dlb_ref[pl.ds(heads[hh], 1), :] += dlb_t * lb[hh] * (1.0 - lb[hh])
            return carry

        lax.fori_loop(0, cb, one, 0, unroll=2)

    outs = [jax.ShapeDtypeStruct((t, HG_HEADS * HEAD), BF16)] * 4 + [
        jax.ShapeDtypeStruct((1, HEAD), F32), jax.ShapeDtypeStruct((HG_HEADS, HEAD), F32)]
    return pl.pallas_call(
        body, name="hgrn_bwd", grid=(nb, HG_HEADS // HPS),
        in_specs=[_view_tile(v, rows, HPS * HEAD, lambda c: nb - 1 - c) for v in (hq, hf, hi, hg)] + [
                  pl.BlockSpec((2, HPS * HEAD), lambda c, g: (0, g)),
                  pl.BlockSpec((1, HEAD), lambda c, g: (0, 0)), tile,
                  pl.BlockSpec((cb, HPS, HEAD, HEAD), lambda c, g: (nb - 1 - c, g, 0, 0)), tile,
                  pl.BlockSpec(mstack.shape, lambda c, h: (0, 0)),
                  pl.BlockSpec(mstack_t.shape, lambda c, h: (0, 0)),
                  pl.BlockSpec(masks.shape, lambda c, h: (0, 0, 0)),
                  pl.BlockSpec(eye.shape, lambda c, h: (0, 0))],
        out_specs=[tile, tile, tile, tile, pl.BlockSpec((1, HEAD), lambda c, h: (0, 0)),
                   pl.BlockSpec((HG_HEADS, HEAD), lambda c, h: (0, 0))],
        out_shape=outs, scratch_shapes=[pltpu.VMEM((HG_HEADS, HEAD, HEAD), F32)],
        compiler_params=_params(_ARB, _ARB))(hq[0], hf[0], hi[0], hg[0], logits, gain, oraw, ssave, dog, mstack,
                                             mstack_t, masks, eye)


CONV_W = 512


def _per_head(fn, *arrs):
    width = arrs[0].shape[1]
    return jnp.concatenate([fn(*[a[:, j:j + HEAD] for a in arrs]) for j in range(0, width, HEAD)], axis=1)


def _shift_down(xv, halo, d, top_rows):
    if d == 0:
        return xv, xv[0:8]
    main = pltpu.roll(xv, d, 0)
    top = jnp.where(top_rows < d, pltpu.roll(halo, d, 0), main[0:8])
    return main, top


def _conv_parts(x_ref, halo_ref, w_ref, first):
    xv = x_ref[...]
    halo = jnp.where(first, 0.0, halo_ref[...])
    top_rows = lax.broadcasted_iota(jnp.int32, (8, xv.shape[1]), 0)
    shifted = [_shift_down(xv, halo, CONV_K - 1 - j, top_rows) for j in range(CONV_K)]
    w = w_ref[...]
    acc = sum(shifted[j][0] * w[j:j + 1, :] for j in range(CONV_K))
    acc_top = sum(shifted[j][1] * w[j:j + 1, :] for j in range(CONV_K))
    return shifted, acc, acc_top


def _conv_fwd(x, w8, l2scale, name):
    x, off, width = x
    t = x.shape[0]
    o = off // CONV_W
    tr = _pick(t, 512, 8)

    def post(cv):
        s = _silu(cv)
        if l2scale is not None:
            s = _per_head(lambda sh: sh * (lax.rsqrt(_rowsum(sh * sh) + EPS) * l2scale), s)
        return s

    def body(x_ref, halo_ref, w_ref, o_ref):
        _, acc, acc_top = _conv_parts(x_ref, halo_ref, w_ref, pl.program_id(1) == 0)
        o_ref[...] = post(acc)
        o_ref[0:8, :] = post(acc_top)

    return pl.pallas_call(
        body, name=name, grid=(width // CONV_W,t // tr),
        in_specs=[pl.BlockSpec((tr, CONV_W), lambda j, i: (i, o + j)),
                  pl.BlockSpec((8, CONV_W), lambda j, i: (jnp.maximum(i * (tr // 8) - 1, 0), o + j)),
                  pl.BlockSpec((8, CONV_W), lambda j, i: (0, j))],
        out_specs=pl.BlockSpec((tr, CONV_W), lambda j, i: (i, j)),
        out_shape=jax.ShapeDtypeStruct((t, width), F32), compiler_params=_params(_PAR, _PAR))(x, x, w8)


def _conv_bwd_a(x, w8, dy, l2scale, name):
    x, off, width = x
    t = x.shape[0]
    o = off // CONV_W
    tr = _pick(t, 512, 8)

    def l2_bwd(s, dyh):
        r = lax.rsqrt(_rowsum(s * s) + EPS)
        y0 = s * r
        dy0 = dyh * l2scale
        return r * (dy0 - y0 * _rowsum(dy0 * y0))

    def to_dc(cv, dyv):
        if l2scale is not None:
            dyv = _per_head(l2_bwd, _silu(cv), dyv)
        return dyv * _dsilu(cv)

    def body(x_ref, halo_ref, w_ref, dy_ref, dc_ref, dw_ref):
        @pl.when(pl.program_id(1) == 0)
        def _():
            dw_ref[...] = jnp.zeros_like(dw_ref)

        shifted, acc, acc_top = _conv_parts(x_ref, halo_ref, w_ref, pl.program_id(1) == 0)
        dyv = dy_ref[...]
        dc = to_dc(acc, dyv)
        dc_top = to_dc(acc_top, dyv[0:8])
        dc_ref[...] = dc
        dc_ref[0:8, :] = dc_top
        rest = (lax.broadcasted_iota(jnp.int32, dc.shape, 0) >= 8).astype(F32)
        dc_rest = dc * rest
        for j in range(CONV_K):
            dw_ref[j:j + 1, :] += (jnp.sum(dc_rest * shifted[j][0], axis=0, keepdims=True)
                                   + jnp.sum(dc_top * shifted[j][1], axis=0, keepdims=True))

    return pl.pallas_call(
        body, name=name, grid=(width // CONV_W,t // tr),
        in_specs=[pl.BlockSpec((tr, CONV_W), lambda j, i: (i, o + j)),
                  pl.BlockSpec((8, CONV_W), lambda j, i: (jnp.maximum(i * (tr // 8) - 1, 0), o + j)),
                  pl.BlockSpec((8, CONV_W), lambda j, i: (0, j)),
                  pl.BlockSpec((tr, CONV_W), lambda j, i: (i, j))],
        out_specs=[pl.BlockSpec((tr, CONV_W), lambda j, i: (i, j)), pl.BlockSpec((8, CONV_W), lambda j, i: (0, j))],
        out_shape=[jax.ShapeDtypeStruct((t, width), F32), jax.ShapeDtypeStruct((8, width), F32)],
        compiler_params=_params(_PAR, _ARB))(x, x, w8, dy)


def _conv_bwd_b(dc, w8, name):
    t, width = dc.shape
    tr = _pick(t, 512, 8)
    nt = t // tr

    def body(dc_ref, halo_ref, w_ref, dx_ref):
        dcv = dc_ref[...]
        halo = jnp.where(pl.program_id(1) == nt - 1, 0.0, halo_ref[...])
        w = w_ref[...]
        bot_rows = lax.broadcasted_iota(jnp.int32, (8, CONV_W), 0)
        acc = dcv * w[CONV_K - 1:CONV_K, :]
        acc_bot = dcv[tr - 8:tr] * w[CONV_K - 1:CONV_K, :]
        for d in range(1, CONV_K):
            main = pltpu.roll(dcv, tr - d, 0)
            bot = jnp.where(bot_rows >= 8 - d, pltpu.roll(halo, 8 - d, 0), main[tr - 8:tr])
            wj = w[CONV_K - 1 - d:CONV_K - d, :]
            acc = acc + main * wj
            acc_bot = acc_bot + bot * wj
        dx_ref[...] = acc.astype(BF16)
        dx_ref[tr - 16:tr, :] = jnp.concatenate([acc[tr - 16:tr - 8], acc_bot], axis=0).astype(BF16)

    return pl.pallas_call(
        body, name=name, grid=(width // CONV_W,nt),
        in_specs=[pl.BlockSpec((tr, CONV_W), lambda j, i: (i, j)),
                  pl.BlockSpec((8, CONV_W), lambda j, i: (jnp.minimum((i + 1) * (tr // 8), t // 8 - 1), j)),
                  pl.BlockSpec((8, CONV_W), lambda j, i: (0, j))],
        out_specs=pl.BlockSpec((tr, CONV_W), lambda j, i: (i, j)),
        out_shape=jax.ShapeDtypeStruct((t, width), BF16), compiler_params=_params(_PAR, _PAR))(dc, dc, w8)


def _each(f, *lists):
    return [f(*xs) for xs in zip(*lists)]


def _split2_each(xs):
    hi = [_bf(x) for x in xs]
    lo = [_bf(x - h.astype(F32)) for x, h in zip(xs, hi)]
    return hi, lo


def _hp_each(a_split, b_split):
    (ah, al), (bh, bl) = a_split, b_split
    rows = ah[0].shape[0]
    d12 = [jnp.dot(jnp.concatenate([x, y], axis=0), z, preferred_element_type=F32) for x, y, z in zip(ah, al, bh)]
    d3 = [jnp.dot(x, y, preferred_element_type=F32) for x, y in zip(ah, bl)]
    return [d[:rows] + d[rows:] + e for d, e in zip(d12, d3)]


INV_EXACT_STEPS = 2


def _tri_inv_each(a_list, eye):
    ns = [-a for a in a_list]
    ps = [eye + n for n in ns]
    n_split = _split2_each(ns)
    for step in range(5):
        if step < INV_EXACT_STEPS:
            ns = _hp_each(n_split, n_split)
            n_split = _split2_each(ns)
            ps = [p + d for p, d in zip(ps, _hp_each(_split2_each(ps), n_split))]
        else:
            nb = n_split[0] if step == INV_EXACT_STEPS else [_bf(n) for n in ns]
            ns = [jnp.dot(x, x, preferred_element_type=F32) for x in nb]
            nb2 = [_bf(n) for n in ns]
            ps = [p + jnp.dot(_bf(p), y, preferred_element_type=F32) for p, y in zip(ps, nb2)]
    return ps


def _gd_gates(gab, alog, dtb):
    sp_arg = gab + dtb
    return sp_arg, -jnp.exp(alog) * _softplus(sp_arg), _sigmoid(gab)


def _pick_lane(tile, base, head):
    g, hh = head
    col = tile[:, base + hh:base + hh + 1]
    for gi in range(1, GD_HEADS // HPS):
        lane = base + gi * HPS + hh
        col = jnp.where(g == gi, tile[:, lane:lane + 1], col)
    return col


def _gd_chunks(q, k, v, g_all, beta_all, sel, l_ref, mask_ref, tm=None):
    incl, strict, eye, upper = mask_ref[0], mask_ref[1], mask_ref[2], mask_ref[3]
    lmat = l_ref[...]
    gb = [jnp.broadcast_to(_pick_lane(g_all, 0, s), (CHUNK, HEAD)) for s in sel]
    bb = [jnp.broadcast_to(_pick_lane(beta_all, GD_HEADS, s), (CHUNK, HEAD)) for s in sel]
    gam = _mx_each(lmat, gb)
    gam_row = [jnp.sum(x[:, :CHUNK] * upper, axis=0, keepdims=True) for x in gb]
    lm = _each(lambda gm, gr: incl * jnp.exp(jnp.minimum(gm[:, :CHUNK] - gr, 0.0)), gam, gam_row)
    kb = _each(lambda x, b: x * b, k, bb)
    a = _each(lambda x, y, m: strict * _dot_nt(x, y) * m, kb, k, lm)
    if tm is None:
        tm = _tri_inv_each(a, eye)
    eg = [jnp.exp(x) for x in gam]
    vb = _each(lambda x, b: x * b, v, bb)
    kbg = _each(lambda x, e: x * e, kb, eg)
    uw = _each(lambda t_, x, y: _dot(t_, jnp.concatenate([x, y], axis=1)), tm, vb, kbg)
    u = [x[:, :HEAD] for x in uw]
    w = [x[:, HEAD:] for x in uw]
    qk = _each(lambda x, y, m: _dot_nt(x, y) * m, q, k, lm)
    g_end = [x[CHUNK - 1:CHUNK, :] for x in gam]
    ekg = _each(lambda e, x: jnp.exp(e - x), g_end, gam)
    ge = [jnp.exp(e) for e in g_end]
    kg = _each(lambda x, e: x * e, k, ekg)
    qg = _each(lambda x, e: x * e, q, eg)
    names = ("bb", "lm", "kb", "a", "tm", "eg", "vb", "kbg", "u", "w", "qk", "ekg", "ge", "kg", "qg")
    cols = (bb, lm, kb, a, tm, eg, vb, kbg, u, w, qk, ekg, ge, kg, qg)
    return [dict(zip(names, vals)) for vals in zip(*cols)]


def _gd_specs(rows, rev_nb=None):
    def cidx(c):
        return c if rev_nb is None else rev_nb - 1 - c

    qk_tile = pl.BlockSpec((rows, HPS // 2 * HEAD), lambda c, g: (cidx(c), g))
    v_tile = pl.BlockSpec((rows, HPS * HEAD), lambda c, g: (cidx(c), g))
    gab_tile = pl.BlockSpec((rows, HEAD), lambda c, g: (cidx(c), 0))
    return qk_tile, v_tile, gab_tile


def _gdn_fwd(qn, kn, cv, gab, gz, alog, dtb, gain, consts):
    t = qn.shape[0]
    nc = t // CHUNK
    cb = _chunks_per_step(nc)
    rows = cb * CHUNK
    lmat, _, masks = consts
    qk_tile, v_tile, gab_tile = _gd_specs(rows)
    row128 = pl.BlockSpec((1, HEAD), lambda c, h: (0, 0))

    def body(q_ref, k_ref, v_ref, gab_ref, gz_ref, alog_ref, dtb_ref, gain_ref, l_ref, mask_ref,
             oraw_ref, og_ref, ssave_ref, tsave_ref, state):
        c = pl.program_id(0)
        g = pl.program_id(1)

        @pl.when(c == 0)
        def _():
            for hh in range(HPS):
                state[g * HPS + hh] = jnp.zeros((HEAD, HEAD), F32)

        alog = alog_ref[...]
        dtb = dtb_ref[...]
        gain_v = gain_ref[...]

        def one(i, carry):
            sl = pl.ds(pl.multiple_of(i * CHUNK, CHUNK), CHUNK)
            _, g_all, beta_all = _gd_gates(gab_ref[sl, :], alog, dtb)
            heads = [g * HPS + hh for hh in range(HPS)]
            lq = [slice(hh // 2 * HEAD, (hh // 2 + 1) * HEAD) for hh in range(HPS)]
            lv = [slice(hh * HEAD, (hh + 1) * HEAD) for hh in range(HPS)]
            chs = _gd_chunks([q_ref[sl, s] for s in lq], [k_ref[sl, s] for s in lq], [v_ref[sl, s] for s in lv],
                             g_all, beta_all, [(g, hh) for hh in range(HPS)], l_ref, mask_ref)
            s0 = [state[h] for h in heads]
            ws = _each(lambda ch, s: _dot(jnp.concatenate([ch["w"], ch["qg"]], axis=0), s), chs, s0)
            v_new = _each(lambda ch, x: ch["u"] - x[:CHUNK], chs, ws)
            o = _each(lambda ch, x, vn: x[CHUNK:] + _dot(ch["qk"], vn), chs, ws, v_new)
            s1 = _each(lambda ch, s, vn: s * ch["ge"] + _dot_tn(ch["kg"], vn), chs, s0, v_new)
            for hh in range(HPS):
                ssave_ref[i, hh] = s0[hh]
                tsave_ref[i, hh] = chs[hh]["tm"]
                state[heads[hh]] = s1[hh]
                oraw_ref[sl, lv[hh]] = o[hh]
                r = lax.rsqrt(jnp.mean(o[hh] * o[hh], axis=1, keepdims=True) + EPS)
                og_ref[sl, lv[hh]] = (o[hh] * r * gain_v * _silu(gz_ref[sl, lv[hh]])).astype(BF16)
            return carry

        lax.fori_loop(0, cb, one, 0, unroll=4)

    return pl.pallas_call(
        body, name="gdn_fwd", grid=(nc // cb, GD_HEADS // HPS),
        in_specs=[qk_tile, qk_tile, v_tile, gab_tile, _view_tile(gz, rows, HPS * HEAD), row128, row128, row128,
                  pl.BlockSpec(lmat.shape, lambda c, g: (0, 0)),
                  pl.BlockSpec(masks.shape, lambda c, g: (0, 0, 0))],
        out_specs=[v_tile, v_tile, pl.BlockSpec((cb, HPS, HEAD, HEAD), lambda c, g: (c, g, 0, 0)),
                   pl.BlockSpec((cb, HPS, CHUNK, CHUNK), lambda c, g: (c, g, 0, 0))],
        out_shape=[jax.ShapeDtypeStruct((t, GD_HEADS * HEAD), F32), jax.ShapeDtypeStruct((t, GD_HEADS * HEAD), BF16),
                   jax.ShapeDtypeStruct((nc, GD_HEADS, HEAD, HEAD), F32),
                   jax.ShapeDtypeStruct((nc, GD_HEADS, CHUNK, CHUNK), F32)],
        scratch_shapes=[pltpu.VMEM((GD_HEADS, HEAD, HEAD), F32)],
        compiler_params=_params(_ARB, _ARB))(qn, kn, cv, gab, gz[0], alog, dtb, gain, lmat, masks)


def _gdn_bwd(qn, kn, cv, gab, gz, alog, dtb, gain, oraw, ssave, tsave, dog, consts):
    t = qn.shape[0]
    nc = t // CHUNK
    cb = _chunks_per_step(nc)
    rows = cb * CHUNK
    nb = nc // cb
    lmat, lmat_t, masks = consts
    qk_tile, v_tile, gab_tile = _gd_specs(rows, nb)
    row128 = pl.BlockSpec((1, HEAD), lambda c, h: (0, 0))

    def body(q_ref, k_ref, v_ref, gab_ref, gz_ref, alog_ref, dtb_ref, gain_ref, oraw_ref, ssave_ref, tsave_ref, dog_ref,
             l_ref, lt_ref, mask_ref,
             dq_ref, dk_ref, dv_ref, dgab_ref, dgz_ref, small_ref, dstate):
        c = pl.program_id(0)
        g = pl.program_id(1)

        @pl.when(c == 0)
        def _():
            for hh in range(HPS):
                dstate[g * HPS + hh] = jnp.zeros((HEAD, HEAD), F32)

        @pl.when((c == 0) & (g == 0))
        def _():
            small_ref[...] = jnp.zeros_like(small_ref)

        alog = alog_ref[...]
        dtb = dtb_ref[...]
        gain_v = gain_ref[...]
        lane = lax.broadcasted_iota(jnp.int32, (1, HEAD), 1)
        last_row = (lax.broadcasted_iota(jnp.int32, (CHUNK, HEAD), 0) == CHUNK - 1).astype(F32)

        def one(j, carry):
            i = cb - 1 - j
            sl = pl.ds(pl.multiple_of(i * CHUNK, CHUNK), CHUNK)
            sp_arg, g_all, beta_all = _gd_gates(gab_ref[sl, :], alog, dtb)
            strict, eye = mask_ref[1], mask_ref[2]
            ltm = lt_ref[...]
            hs = range(HPS)
            heads = [g * HPS + hh for hh in hs]
            lq = [slice(hh // 2 * HEAD, (hh // 2 + 1) * HEAD) for hh in hs]
            lv = [slice(hh * HEAD, (hh + 1) * HEAD) for hh in hs]
            q = [q_ref[sl, s] for s in lq]
            k = [k_ref[sl, s] for s in lq]
            v = [v_ref[sl, s] for s in lv]
            gzv = [gz_ref[sl, s] for s in lv]
            chs = _gd_chunks(q, k, v, g_all, beta_all, [(g, hh) for hh in hs], l_ref, mask_ref,
                             tm=[tsave_ref[i, hh] for hh in hs])

            def col(name):
                return [ch[name] for ch in chs]

            def mul(x, y):
                return x * y

            tm, lm, eg, bb = col("tm"), col("lm"), col("eg"), col("bb")
            s0 = [ssave_ref[i, hh] for hh in hs]
            ds = [dstate[h] for h in heads]
            v_new = _each(lambda u, w, s: u - _dot(w, s), col("u"), col("w"), s0)

            o = [oraw_ref[sl, s] for s in lv]
            r = [lax.rsqrt(jnp.mean(x * x, axis=1, keepdims=True) + EPS) for x in o]
            on = _each(mul, o, r)
            dg_out = [dog_ref[sl, s] for s in lv]
            sgate = [_silu(x) for x in gzv]
            for hh in hs:
                dgz_ref[sl, lv[hh]] = (dg_out[hh] * on[hh] * gain_v * _dsilu(gzv[hh])).astype(BF16)
            small_ref[0:1, :] += sum(jnp.sum(d * s * n, axis=0, keepdims=True) for d, s, n in zip(dg_out, sgate, on))
            don = _each(lambda d, s: d * s * gain_v, dg_out, sgate)
            do = _each(lambda rr, dn, n: rr * (dn - n * jnp.mean(dn * n, axis=1, keepdims=True)), r, don, on)

            dv_new = _each(lambda a, d, b, s: _dot_tn(a, d) + _dot(b, s), col("qk"), do, col("kg"), ds)
            dqk = _each(_dot_nt, do, v_new)
            dkg = _each(_dot_nt, v_new, ds)
            dge = _each(lambda s, d: jnp.sum(_rowsum(s * d), axis=0, keepdims=True), s0, ds)
            both = _each(lambda d, dv: jnp.concatenate([d, dv], axis=0), do, dv_new)
            from_s = _each(_dot_nt, both, s0)
            dqg = [x[:CHUNK] for x in from_s]
            dw = [-x[CHUNK:] for x in from_s]
            ds_new = _each(lambda qg, w, bo, ge, s: _dot_tn(jnp.concatenate([qg, -w], axis=0), bo) + ge * s,
                           col("qg"), col("w"), both, col("ge"), ds)
            for hh in hs:
                dstate[heads[hh]] = ds_new[hh]

            side = _each(lambda dv, d: jnp.concatenate([dv, d], axis=1), dv_new, dw)
            back = _each(_dot_tn, tm, side)
            dvb = [x[:, :HEAD] for x in back]
            dkbg = [x[:, HEAD:] for x in back]
            dtm = _each(lambda sd, vb, kbg: _dot_nt(sd, jnp.concatenate([vb, kbg], axis=1)), side, col("vb"), col("kbg"))
            dtt = _each(_dot_nt, dtm, tm)
            da = _each(lambda t_, x: -_dot_tn(t_, x) * strict, tm, dtt)
            dal = _each(mul, da, lm)
            dqk_l = _each(mul, dqk, lm)
            stack = _each(lambda x, y: jnp.concatenate([x, y], axis=0), dal, dqk_l)
            on_k = _each(_dot, stack, k)
            dkb = _each(lambda x, y, e: x[:CHUNK] + y * e, on_k, dkbg, eg)
            dq = _each(lambda x, y, e: x[CHUNK:] + y * e, on_k, dqg, eg)
            dk = _each(lambda st, kb, qq, z, ekg, w_, b: _dot_tn(st, jnp.concatenate([kb, qq], axis=0)) + z * ekg + w_ * b,
                       stack, col("kb"), q, dkg, col("ekg"), dkb, bb)
            gmat = _each(lambda x, a, y, qk: x * a + y * qk, da, col("a"), dqk, col("qk"))
            t_kg = _each(lambda x, y: _rowsum(x * y), dkg, col("kg"))
            dgam = _each(lambda gm, x, qg, t_, y, kbg: (_rowsum(gm) - _row_to_col(jnp.sum(gm, axis=0, keepdims=True), eye)
                                                        + _rowsum(x * qg) - t_ + _rowsum(y * kbg)),
                         gmat, dqg, col("qg"), t_kg, dkbg, col("kbg"))
            dg_end = _each(lambda t_, e, ge: jnp.sum(t_, axis=0, keepdims=True) + e * ge[:, 0:1], t_kg, dge, col("ge"))
            dgam = _each(lambda x, e: x + last_row * e, dgam, dg_end)
            dbeta = _each(lambda x, kk, y, vv: _rowsum(x * kk) + _rowsum(y * vv), dkb, k, dvb, v)
            dg = _mx_each(ltm, dgam)

            for hh in hs:
                dv_ref[sl, lv[hh]] = dvb[hh] * bb[hh]
            fac_g = -jnp.exp(alog) * _sigmoid(sp_arg)
            fac_b = beta_all * (1.0 - beta_all)
            hot_g = [(lane == h).astype(F32) for h in heads]
            hot_b = [(lane == GD_HEADS + h).astype(F32) for h in heads]
            dga = _each(lambda x, hot: x * hot * fac_g, dg, hot_g)
            dgb = _each(lambda x, hot: x * hot * fac_b, dbeta, hot_b)
            small_ref[1:2, :] += sum(jnp.sum(x, axis=0, keepdims=True) for x in dga)
            small_ref[2:3, :] += sum(jnp.sum(x * hot * g_all, axis=0, keepdims=True) for x, hot in zip(dg, hot_g))
            for pair in range(HPS // 2):
                lqp = slice(pair * HEAD, (pair + 1) * HEAD)
                dq_ref[sl, lqp] = dq[2 * pair] + dq[2 * pair + 1]
                dk_ref[sl, lqp] = dk[2 * pair] + dk[2 * pair + 1]
            dgab_ref[sl, :] = sum(a + b for a, b in zip(dga, dgb))
            return carry

        lax.fori_loop(0, cb, one, 0, unroll=4)

    groups = GD_HEADS // HPS
    outs = [jax.ShapeDtypeStruct((t, 1024), F32), jax.ShapeDtypeStruct((t, 1024), F32),
            jax.ShapeDtypeStruct((t, 2048), F32), jax.ShapeDtypeStruct((t, groups * HEAD), F32),
            jax.ShapeDtypeStruct((t, 2048), BF16), jax.ShapeDtypeStruct((8, HEAD), F32)]
    return pl.pallas_call(
        body, name="gdn_bwd", grid=(nb, groups),
        in_specs=[qk_tile, qk_tile, v_tile, gab_tile, _view_tile(gz, rows, HPS * HEAD, lambda c: nb - 1 - c),
                  row128, row128, row128, v_tile,
                  pl.BlockSpec((cb, HPS, HEAD, HEAD), lambda c, g: (nb - 1 - c, g, 0, 0)),
                  pl.BlockSpec((cb, HPS, CHUNK, CHUNK), lambda c, g: (nb - 1 - c, g, 0, 0)), v_tile,
                  pl.BlockSpec(lmat.shape, lambda c, g: (0, 0)),
                  pl.BlockSpec(lmat_t.shape, lambda c, g: (0, 0)),
                  pl.BlockSpec(masks.shape, lambda c, g: (0, 0, 0))],
        out_specs=[qk_tile, qk_tile, v_tile, pl.BlockSpec((rows, HEAD), lambda c, g: (nb - 1 - c, g)), v_tile,
                   pl.BlockSpec((8, HEAD), lambda c, g: (0, 0))],
        out_shape=outs, scratch_shapes=[pltpu.VMEM((GD_HEADS, HEAD, HEAD), F32)],
        compiler_params=_params(_ARB, _ARB))(qn, kn, cv, gab, gz[0], alog, dtb, gain, oraw, ssave, tsave, dog,
                                             lmat, lmat_t, masks)


def _fold_groups(wide):
    t, width = wide.shape
    tr = _pick(t, 512, 8)

    def body(w_ref, o_ref):
        acc = w_ref[:, 0:HEAD]
        for j in range(1, width // HEAD):
            acc = acc + w_ref[:, j * HEAD:(j + 1) * HEAD]
        o_ref[...] = acc.astype(BF16)

    return pl.pallas_call(
        body, name="fold_gate_grads", grid=(t // tr,), in_specs=[_row_spec(tr, width)], out_specs=_row_spec(tr, HEAD),
        out_shape=jax.ShapeDtypeStruct((t, HEAD), BF16), compiler_params=_params(_PAR))(wide)


def _adam_math(w, g, m, v):
    m2 = ADAM_B1 * m + (1.0 - ADAM_B1) * g
    v2 = ADAM_B2 * v + (1.0 - ADAM_B2) * (g * g)
    m_hat = m2 / (1.0 - ADAM_B1 ** ADAM_STEP)
    v_hat = v2 / (1.0 - ADAM_B2 ** ADAM_STEP)
    delta = -ADAM_LR * (m_hat / (jnp.sqrt(v_hat) + ADAM_EPS) + ADAM_WD * w)
    return delta, m2, v2


def _adamw(w, g, m, v, name, after=None):
    r, c = w.shape
    tr = r
    for cand in range(8, r + 1, 8):
        if r % cand == 0 and cand * c * 4 <= (1 << 20):
            tr = cand
    if r % 8 != 0:
        tr = r

    def body(w_ref, g_ref, m_ref, v_ref, *rest):
        d_ref, m2_ref, v2_ref = rest[-3:]
        d, m2, v2 = _adam_math(w_ref[...], g_ref[...], m_ref[...], v_ref[...])
        d_ref[...] = d
        m2_ref[...] = m2
        v2_ref[...] = v2

    spec = pl.BlockSpec((tr, c), lambda i: (i, 0))
    extra = [] if after is None else [after]
    return pl.pallas_call(
        body, name=name, grid=(r // tr,), in_specs=[spec] * 4 + [_ANY] * len(extra), out_specs=[spec] * 3,
        out_shape=[jax.ShapeDtypeStruct((r, c), F32)] * 3, compiler_params=_params(_PAR))(w, g, m, v, *extra)


_ANY = pl.BlockSpec(memory_space=pl.ANY)


def _place():
    return lax.axis_index("x"), lax.axis_index("y"), lax.axis_index("c")


def _gather_weights(packs, nchs, name):
    n = len(packs)
    halves = [p.shape[0] // 2 for p in packs]
    base = [sum(nchs[:i]) for i in range(n)]
    total = sum(nchs)
    for p, h, k in zip(packs, halves, nchs):
        assert p.shape[0] == 2 * h and h % k == 0 and (h // k) % 16 == 0

    def body(*refs):
        p_refs, g_refs, (send_sems, recv_sems) = refs[:n], refs[n:2 * n], refs[2 * n:]
        x, y, c = _place()
        sibling = (x, y, 1 - c)
        chips = [(1 - x, y), (x, 1 - y), (1 - x, 1 - y)]
        chunks = [(a, q) for a in range(n) for q in range(nchs[a])]

        def rows_of(a, pc, q):
            ch = halves[a] // nchs[a]
            return pl.ds(pl.multiple_of(pc * halves[a] + q * ch, 16), ch)

        def piece(a, px, py, pc, q):
            return g_refs[a].at[2 * px + py, rows_of(a, pc, q), :]

        def copy(k, src, dst, to):
            return pltpu.make_async_remote_copy(src_ref=src, dst_ref=dst, send_sem=send_sems.at[k],
                                                recv_sem=recv_sems.at[k], device_id=to, device_id_type=MESH)

        def sem_of(j, a, q):
            return j * total + base[a] + q

        first = {(j, a, q): copy(sem_of(j, a, q), p_refs[a].at[rows_of(a, c, q), :], piece(a, x, y, c, q), (*chip, c))
                 for j, chip in enumerate(chips) for a, q in chunks}
        for a, q in chunks:
            for j in range(3):
                first[j, a, q].start()
        passed = {(j, a, q): copy(sem_of(3 + j, a, q), piece(a, *chip, c, q), piece(a, *chip, c, q), sibling)
                  for j, chip in enumerate(chips) for a, q in chunks}
        for a, q in chunks:
            for j, chip in enumerate(chips):
                copy(sem_of(j, a, q), p_refs[a].at[rows_of(a, c, q), :], piece(a, *chip, c, q), (*chip, c)).wait_recv()
                passed[j, a, q].start()
        for a, q in chunks:
            for j, chip in enumerate(chips):
                copy(sem_of(3 + j, a, q), piece(a, *chip, 1 - c, q), piece(a, *chip, 1 - c, q), sibling).wait_recv()
        for key in first:
            first[key].wait_send()
            passed[key].wait_send()

    return pl.pallas_call(
        body, name=name, out_shape=[jax.ShapeDtypeStruct((4,) + p.shape, p.dtype) for p in packs],
        in_specs=[_ANY] * n, out_specs=[_ANY] * n,
        scratch_shapes=[pltpu.SemaphoreType.DMA((6 * total,)), pltpu.SemaphoreType.DMA((6 * total,))])(*packs)


def _swap_with_sibling(arrs, nchs, lead, name, halves=False):
    n = len(arrs)
    jobs = []
    hs = [arr.shape[-2] // (2 if halves else 1) for arr in arrs]
    for a, (h, k) in enumerate(zip(hs, nchs)):
        assert h % k == 0 and (h // k) % 16 == 0
        for s in (range(lead) if lead else [None]):
            jobs += [(a, s, q * (h // k), h // k) for q in range(k)]

    def body(*refs):
        src, dst, (send_sems, recv_sems) = refs[:n], refs[n:2 * n], refs[2 * n:]
        x, y, c = _place()

        def at(ref, s, r0, rows):
            return ref.at[pl.ds(r0, rows), :] if s is None else ref.at[s, pl.ds(r0, rows), :]

        def src_rows(a, r0):
            return pl.multiple_of((1 - c) * hs[a] + r0, 16) if halves else r0

        copies = [pltpu.make_async_remote_copy(
            src_ref=at(src[a], s, src_rows(a, r0), rows), dst_ref=at(dst[a], s, r0, rows), send_sem=send_sems.at[k],
            recv_sem=recv_sems.at[k], device_id=(x, y, 1 - c), device_id_type=MESH)
            for k, (a, s, r0, rows) in enumerate(jobs)]
        for cp in copies:
            cp.start()
        for cp in copies:
            cp.wait()

    shapes = [jax.ShapeDtypeStruct(arr.shape[:-2] + (h, arr.shape[-1]), arr.dtype) for arr, h in zip(arrs, hs)]
    return pl.pallas_call(
        body, name=name, out_shape=shapes, in_specs=[_ANY] * n, out_specs=[_ANY] * n,
        scratch_shapes=[pltpu.SemaphoreType.DMA((len(jobs),)), pltpu.SemaphoreType.DMA((len(jobs),))])(*arrs)


def _add2(full, b, core, name):
    n, rows, w = b.shape
    tr = _pick(rows, 256, 16)
    nblk = rows // tr

    def body(c_ref, a_ref, b_ref, o_ref):
        o_ref[...] = (a_ref[...].astype(F32) + b_ref[...].astype(F32)).astype(BF16)

    spec = pl.BlockSpec((1, tr, w), lambda i, j, c_ref: (i, j, 0))
    grid_spec = pltpu.PrefetchScalarGridSpec(
        num_scalar_prefetch=1, grid=(n, nblk),
        in_specs=[pl.BlockSpec((1, tr, w), lambda i, j, c_ref: (i, c_ref[0] * nblk + j, 0)), spec], out_specs=spec)
    return pl.pallas_call(
        body, name=name, grid_spec=grid_spec, out_shape=jax.ShapeDtypeStruct(b.shape, BF16),
        compiler_params=_params(_PAR, _PAR))(core, full, b)


def _reduce_chips(partials, nchs, name):
    n = len(partials)
    jobs = []
    for a, (arr, k) in enumerate(zip(partials, nchs)):
        h = arr.shape[1]
        assert h % k == 0 and (h // k) % 16 == 0
        jobs += [(a, q * (h // k), h // k) for q in range(k)]

    def body(*refs):
        src, dst, (send_sems, recv_sems) = refs[:n], refs[n:2 * n], refs[2 * n:]
        x, y, c = _place()
        chips = [(1 - x, y), (x, 1 - y), (1 - x, 1 - y)]
        copies = [pltpu.make_async_remote_copy(
            src_ref=src[a].at[2 * px + py, pl.ds(r0, rows), :], dst_ref=dst[a].at[j, pl.ds(r0, rows), :],
            send_sem=send_sems.at[3 * k + j], recv_sem=recv_sems.at[3 * k + j],
            device_id=(px, py, c), device_id_type=MESH)
            for k, (a, r0, rows) in enumerate(jobs) for j, (px, py) in enumerate(chips)]
        for cp in copies:
            cp.start()
        for cp in copies:
            cp.wait()

    return pl.pallas_call(
        body, name=name,
        out_shape=[jax.ShapeDtypeStruct((3,) + p.shape[1:], p.dtype) for p in partials],
        in_specs=[_ANY] * n, out_specs=[_ANY] * n,
        scratch_shapes=[pltpu.SemaphoreType.DMA((3 * len(jobs),)), pltpu.SemaphoreType.DMA((3 * len(jobs),))])(*partials)


_HBM = pl.BlockSpec(memory_space=pltpu.HBM)
_SEM = pl.BlockSpec(memory_space=pltpu.SEMAPHORE)
_DATAFLOW = pltpu.SideEffectType.DATAFLOW_SIDE_EFFECTING


def _ici_jobs(srcs, nchs, kind):
    jobs = []
    for a, (arr, k) in enumerate(zip(srcs, nchs)):
        h = arr.shape[0] // 2 if kind == "gather" else arr.shape[1]
        assert h % k == 0 and (h // k) % 16 == 0
        jobs += [(a, h, q * (h // k), h // k) for q in range(k)]
    return jobs


def _ici_copies(src, land, send_sems, recv_sems, jobs, kind):
    x, y, c = _place()
    chips = [(1 - x, y), (x, 1 - y), (1 - x, 1 - y)]
    copies = []
    for k, (a, h, r0, rows) in enumerate(jobs):
        for j, (px, py) in enumerate(chips):
            if kind == "gather":
                at = pl.ds(pl.multiple_of(c * h + r0, 16), rows)
                s, d = src[a].at[at, :], land[a].at[2 * x + y, at, :]
            else:
                s, d = src[a].at[2 * px + py, pl.ds(r0, rows), :], land[a].at[j, pl.ds(r0, rows), :]
            copies.append(pltpu.make_async_remote_copy(
                src_ref=s, dst_ref=d, send_sem=send_sems.at[3 * k + j], recv_sem=recv_sems.at[3 * k + j],
                device_id=(px, py, c), device_id_type=MESH))
    return copies


def _ici_start(srcs, nchs, kind, name):
    n = len(srcs)
    jobs = _ici_jobs(srcs, nchs, kind)
    lead = (lambda s: (4,) + s.shape) if kind == "gather" else (lambda s: (3,) + s.shape[1:])
    lands = [lax.empty(lead(s), s.dtype) for s in srcs]

    def body(*refs):
        src, land = refs[:n], refs[n:2 * n]
        send_sems, recv_sems, token = refs[2 * n], refs[2 * n + 1], refs[-1]
        for cp in _ici_copies(src, land, send_sems, recv_sems, jobs, kind):
            cp.start()
        token[...] = jnp.zeros_like(token)

    hbm = [pltpu.HBM(a.shape, a.dtype) for a in srcs + lands]
    outs = pl.pallas_call(
        body, name=name,
        out_shape=[pltpu.SemaphoreType.DMA((3 * len(jobs),)), pltpu.SemaphoreType.DMA((3 * len(jobs),))] + hbm
        + [jax.ShapeDtypeStruct((8, 128), F32)],
        in_specs=[_HBM] * (2 * n), out_specs=[_SEM, _SEM] + [_HBM] * (2 * n) + [pl.BlockSpec(memory_space=pltpu.VMEM)],
        input_output_aliases={i: 2 + i for i in range(2 * n)},
        compiler_params=pltpu.CompilerParams(has_side_effects=_DATAFLOW),
    )(*[pltpu.with_memory_space_constraint(a, pltpu.HBM) for a in srcs + lands])
    return (outs[0], outs[1], list(outs[2:2 + n]), list(outs[2 + n:2 + 2 * n]), nchs, kind), outs[-1]


def _ici_wait(handle, after, name):
    send_sems, recv_sems, srcs, lands, nchs, kind = handle
    n = len(srcs)
    jobs = _ici_jobs(srcs, nchs, kind)

    def body(*refs):
        src, land = refs[:n], refs[n:2 * n]
        for cp in _ici_copies(src, land, refs[2 * n], refs[2 * n + 1], jobs, kind):
            cp.wait_send()
            cp.wait_recv()

    outs = pl.pallas_call(
        body, name=name, out_shape=[pltpu.HBM(a.shape, a.dtype) for a in srcs + lands],
        in_specs=[_HBM] * (2 * n) + [_SEM, _SEM, _ANY], out_specs=[_HBM] * (2 * n),
        input_output_aliases={i: i for i in range(2 * n)},
        compiler_params=pltpu.CompilerParams(has_side_effects=_DATAFLOW),
    )(*srcs, *lands, send_sems, recv_sems, after)
    return list(outs[:n]), list(outs[n:])


def _pass_to_sibling(gathered, nchs, name):
    n = len(gathered)
    jobs = _ici_jobs([jax.ShapeDtypeStruct(g.shape[1:], g.dtype) for g in gathered], nchs, "gather")

    def body(*refs):
        src, dst, (send_sems, recv_sems) = refs[:n], refs[n:2 * n], refs[2 * n:]
        x, y, c = _place()
        slots = [2 * (1 - x) + y, 2 * x + (1 - y), 2 * (1 - x) + (1 - y)]

        def copy(k, j, pc):
            a, h, r0, rows = jobs[k]
            at = pl.ds(pl.multiple_of(pc * h + r0, 16), rows)
            return pltpu.make_async_remote_copy(
                src_ref=src[a].at[slots[j], at, :], dst_ref=dst[a].at[slots[j], at, :], send_sem=send_sems.at[3 * k + j],
                recv_sem=recv_sems.at[3 * k + j], device_id=(x, y, 1 - c), device_id_type=MESH)

        pairs = [(k, j) for k in range(len(jobs)) for j in range(3)]
        for k, j in pairs:
            copy(k, j, c).start()
        for k, j in pairs:
            copy(k, j, c).wait_send()
            copy(k, j, 1 - c).wait_recv()

    return pl.pallas_call(
        body, name=name, out_shape=[jax.ShapeDtypeStruct(g.shape, g.dtype) for g in gathered],
        in_specs=[_ANY] * n, out_specs=[_ANY] * n, input_output_aliases={i: i for i in range(n)},
        scratch_shapes=[pltpu.SemaphoreType.DMA((3 * len(jobs),)), pltpu.SemaphoreType.DMA((3 * len(jobs),))])(*gathered)


def _add4(own, got, name):
    rows, w = own.shape
    tr = _pick(rows, 128, 16)

    def body(a_ref, b_ref, o_ref):
        o_ref[...] = ((a_ref[...].astype(F32) + b_ref[0].astype(F32)) + b_ref[1].astype(F32)) + b_ref[2].astype(F32)

    return pl.pallas_call(
        body, name=name, grid=(rows // tr,),
        in_specs=[pl.BlockSpec((tr, w), lambda i: (i, 0)), pl.BlockSpec((3, tr, w), lambda i: (0, i, 0))],
        out_specs=pl.BlockSpec((tr, w), lambda i: (i, 0)), out_shape=jax.ShapeDtypeStruct((rows, w), F32),
        compiler_params=_params(_PAR))(own, got)


def _small_sync(gs, ws, ms, vs):
    rows = gs.shape[0]
    vmem = pl.BlockSpec(memory_space=pltpu.VMEM)

    def body(g_ref, w_ref, m_ref, v_ref, sum_ref, d_ref, m2_ref, v2_ref, buf, send_sems, recv_sems):
        x, y, c = _place()
        me = 4 * x + 2 * y + c
        buf[me] = g_ref[...]
        copies = []
        for k in range(1, 8):
            peer = (x ^ (k >> 2), y ^ ((k >> 1) & 1), c ^ (k & 1))
            copies.append(pltpu.make_async_remote_copy(
                src_ref=g_ref, dst_ref=buf.at[me], send_sem=send_sems.at[k - 1], recv_sem=recv_sems.at[k - 1],
                device_id=peer, device_id_type=MESH))
        for cp in copies:
            cp.start()
        for cp in copies:
            cp.wait()
        total = buf[0]
        for i in range(1, 8):
            total = total + buf[i]
        sum_ref[...] = total
        d, m2, v2 = _adam_math(w_ref[...], total, m_ref[...], v_ref[...])
        d_ref[...] = d
        m2_ref[...] = m2
        v2_ref[...] = v2

    shape = jax.ShapeDtypeStruct((rows, 128), F32)
    return pl.pallas_call(
        body, name="small_sync", out_shape=[shape] * 4, in_specs=[vmem] * 4, out_specs=[vmem] * 4,
        scratch_shapes=[pltpu.VMEM((8, rows, 128), F32), pltpu.SemaphoreType.DMA((7,)),
                        pltpu.SemaphoreType.DMA((7,))])(gs, ws, ms, vs)


_GROUPS = {
    "ffn1": dict(cols=("ffn1_w_in", 1408), rows=(("ffn1_w_out", 704, 704),), chunks=(8, 2)),
    "ffn2": dict(cols=("ffn2_w_in", 1408), rows=(("ffn2_w_out", 704, 704),), chunks=(8, 2)),
    "mixer": dict(cols=("w_in", 3080), chunks=(8, 4),
                  rows=(("w_branch_hgrn", 256, 256), ("w_branch_gdn", 512, 512), ("w_out", 256, 256),
                        ("gdn_conv_w", CONV_K, 128))),
}
_BIG_NAMES = tuple(n for g in _GROUPS.values() for n in (g["cols"][0],) + tuple(r[0] for r in g["rows"]))


def _group_names(group):
    return (group["cols"][0],) + tuple(r[0] for r in group["rows"])


def _pack(parts, lead, group):
    ax = len(lead)
    rows = []
    for n, r, padded in group["rows"]:
        p = parts[n]
        if padded != r:
            p = jnp.tile(p, (1,) * ax + (padded // r, 1))
        rows.append(p)
    return [parts[group["cols"][0]], rows[0] if len(rows) == 1 else jnp.concatenate(rows, axis=ax)]


def _unpack(cols, rows, group):
    out, off = {group["cols"][0]: cols}, 0
    for n, r, padded in group["rows"]:
        out[n] = rows[..., off:off + r, :]
        off += padded
    return out


def _is_col_sharded(name):
    return name in ("ffn1_w_in", "ffn2_w_in", "w_in", "gdn_conv_w")


def _full_from_shards(name, g):
    if _is_col_sharded(name):
        return jnp.transpose(g, (1, 0, 2)).reshape(g.shape[1], -1)
    return g.reshape(-1, g.shape[2])


def _shards_from_full(name, full):
    if _is_col_sharded(name):
        return jnp.transpose(full.reshape(full.shape[0], 4, -1), (1, 0, 2))
    return full.reshape(4, -1, full.shape[1])


_SMALL = (("ffn1_norm", 8), ("mix_norm", 8), ("hgrn_lb_logits", 16), ("hgrn_out_norm", 8), ("gdn_a_log", 8),
          ("gdn_dt_bias", 8), ("gdn_out_norm", 8), ("ffn2_norm", 8), ("final_norm", 8), ("loss", 8))
_SMALL_ROWS = sum(r for _, r in _SMALL)


def _pack_small(parts):
    out = []
    for name, rows in _SMALL:
        p = parts[name].reshape(-1).astype(F32)
        if p.shape[0] <= 128:
            if p.shape[0] < 128:
                p = jnp.concatenate([p, jnp.zeros((128 - p.shape[0],), F32)])
            p = jnp.broadcast_to(p.reshape(1, 128), (rows, 128))
        out.append(p.reshape(rows, 128))
    return jnp.concatenate(out, axis=0)


def _unpack_small(packed, shapes):
    out, off = {}, 0
    for name, rows in _SMALL:
        n = int(np.prod(shapes[name]))
        out[name] = packed[off:off + rows].reshape(-1)[:n].reshape(shapes[name])
        off += rows
    return out


def _ffn_fwd(x, gain, w_in, w_out, tag):
    n = _rmsnorm_fwd(x, gain, tag + "_norm")
    a, b, hm = _ffn_in_act(n, w_in, tag + "_in")
    out = _mm(hm, w_out, alpha=0.5, res=x, name=tag + "_out")
    return out, (n, a, b)


def _ffn_bwd(x, gain, w_in, w_out, saved, dout, dout_bf, tag):
    n, a, b = saved
    da, db, hm = _ffn_dact(dout_bf, w_out, a, b, tag + "_dact")
    dw_out = _mm(hm, dout_bf, ta=True, alpha=0.5, out_dtype=BF16, name=tag + "_dwout")
    dwa = _mm(n, da, ta=True, out_dtype=BF16, name=tag + "_dwin_a")
    dwb = _mm(n, db, ta=True, out_dtype=BF16, name=tag + "_dwin_b")
    half = D_FF // 2
    dw_in = jnp.stack([dwa[:, :half], dwa[:, half:], dwb[:, :half], dwb[:, half:]])
    dn = _mm(da, w_in, tb=True, name=tag + "_dnorm_a")
    dn = _mm(db, w_in, tb=True, res=dn, b_from=D_FF, name=tag + "_dnorm_b")
    dx, dx_bf, dgain = _rmsnorm_bwd(x, gain, dn, dout, tag + "_dx")
    return dx, dx_bf, dgain, dw_in, dw_out


def _pad_lanes(v):
    return jnp.concatenate([v.reshape(1, -1), jnp.zeros((1, HEAD - v.size), F32)], axis=1)


def _local_step(x, tgt, small, exchange):
    hg_c = _hg_consts()
    gd_c = _gd_consts()
    alog = _pad_lanes(small["gdn_a_log"])
    dtb = _pad_lanes(small["gdn_dt_bias"])
    logits = small["hgrn_lb_logits"]
    hg_gain = small["hgrn_out_norm"].reshape(1, HEAD)
    gd_gain = small["gdn_out_norm"].reshape(1, HEAD)
    g1, gm, g2 = small["ffn1_norm"].reshape(1, -1), small["mix_norm"].reshape(1, -1), small["ffn2_norm"].reshape(1, -1)
    gf = small["final_norm"].reshape(1, -1)
    qscale = HEAD ** -0.5

    w1 = exchange.weights("ffn1")
    started = exchange.prefetch("mixer")
    h1, ffn1_saved = _ffn_fwd(x, g1 + started, w1["ffn1_w_in"], w1["ffn1_w_out"], "ffn1")
    u = _rmsnorm_fwd(h1, gm, "mix_norm")
    w = exchange.weights("mixer", after=u)
    started = exchange.prefetch("ffn2")
    seg, off = {}, 0
    for name, size in zip(IN_NAMES, IN_SIZES):
        seg[name] = w["w_in"][:, off:off + size]
        off += size
    w_gab = jnp.concatenate([seg["ga"], seg["gb"], jnp.zeros((D_MODEL, HEAD - 32), BF16)], axis=1)
    big_segs = [n for n in IN_NAMES if n not in ("ga", "gb")]
    conv8 = jnp.concatenate([w["gdn_conv_w"].astype(F32), jnp.zeros((8 - CONV_K, 4096), F32)], axis=0)
    conv_q, conv_k, conv_v = conv8[:, :1024], conv8[:, 1024:2048], conv8[:, 2048:]
    w_main = jnp.concatenate([seg[n] for n in big_segs], axis=1)
    proj = _mm(u, w_main, name="proj")
    pr, off = {}, 0
    for n in big_segs:
        pr[n] = _view(proj, off, seg[n].shape[1])
        off += seg[n].shape[1]
    gab = _mm(u, w_gab, name="proj_gab")
    oh_raw, oh, s_h = _hgrn_fwd(pr["hq"], pr["hf"], pr["hi"], pr["hg"], logits, hg_gain + started, hg_c)
    qn = _conv_fwd(pr["gq"], conv_q, qscale, "conv_q")
    kn = _conv_fwd(pr["gk"], conv_k, 1.0, "conv_k")
    cv = _conv_fwd(pr["gv"], conv_v, None, "conv_v")
    og_raw, og, s_g, t_g = _gdn_fwd(qn, kn, cv, gab, pr["gz"], alog, dtb, gd_gain, gd_c)
    yh = _mm(oh, w["w_branch_hgrn"], name="branch_h")
    yg = _mm(og, w["w_branch_gdn"], name="branch_g")
    ym = _merge_fwd(yh, yg, pr["gate_h"], pr["gate_g"])
    h2 = _mm(ym, w["w_out"], res=h1, name="mix_out")
    w2 = exchange.weights("ffn2", after=h2)
    h3, ffn2_saved = _ffn_fwd(h2, g2, w2["ffn2_w_in"], w2["ffn2_w_out"], "ffn2")
    loss, dh3, dh3_bf, d_gf = _final_loss(h3, gf, tgt)

    dh2, dh2_bf, d_g2, d_f2in, d_f2out = _ffn_bwd(h2, g2, w2["ffn2_w_in"], w2["ffn2_w_out"], ffn2_saved, dh3, dh3_bf,
                                                  "ffn2")
    started = exchange.reduce("ffn2", {"ffn2_w_in": d_f2in, "ffn2_w_out": d_f2out}, behind=True)
    dym =_mm(dh2_bf, w["w_out"], tb=True, name="d_merge")
    d_wout = _mm(ym, dh2_bf, ta=True, out_dtype=BF16, name="d_w_out")
    dyh, dyg, d_gate_h, d_gate_g = _merge_bwd(dym, yh, yg, pr["gate_h"], pr["gate_g"])
    d_wbh = _mm(oh, dyh, ta=True, out_dtype=BF16, name="d_w_branch_h")
    d_wbg = _mm(og, dyg, ta=True, out_dtype=BF16, name="d_w_branch_g")
    doh = _mm(dyh, w["w_branch_hgrn"], tb=True, name="d_oh")
    dog = _mm(dyg, w["w_branch_gdn"], tb=True, name="d_og")
    d_hq, d_hf, d_hi, d_hg, d_hg_gain, d_lb0 = _hgrn_bwd(pr["hq"], pr["hf"], pr["hi"], pr["hg"], logits,
                                                        hg_gain + started, oh_raw, s_h, doh, hg_c)
    d_qn, d_kn, d_cv, d_gab_wide, d_gz, gd_small = _gdn_bwd(qn, kn, cv, gab, pr["gz"], alog, dtb, gd_gain, og_raw,
                                                            s_g, t_g, dog, gd_c)
    d_gab = _fold_groups(d_gab_wide)
    dc_q, dwc_q = _conv_bwd_a(pr["gq"], conv_q, d_qn, qscale, "dconv_q")
    dc_k, dwc_k = _conv_bwd_a(pr["gk"], conv_k, d_kn, 1.0, "dconv_k")
    dc_v, dwc_v = _conv_bwd_a(pr["gv"], conv_v, d_cv, None, "dconv_v")
    d_gq = _conv_bwd_b(dc_q, conv_q, "dconvx_q")
    d_gk = _conv_bwd_b(dc_k, conv_k, "dconvx_k")
    d_gv = _conv_bwd_b(dc_v, conv_v, "dconvx_v")
    dpr = {"hq": d_hq, "hf": d_hf, "hi": d_hi, "hg": d_hg, "gq": d_gq, "gk": d_gk, "gv": d_gv, "gz": d_gz,
           "gate_h": d_gate_h, "gate_g": d_gate_g}
    dproj = jnp.concatenate([dpr[n] for n in big_segs], axis=1)
    du = _mm(d_gab, w_gab, tb=True, name="du_gab")
    du = _mm(dproj, w_main, tb=True, res=du, name="du")
    d_wmain = _mm(u, dproj, ta=True, out_dtype=BF16, name="dw_main")
    d_wgab = _mm(u, d_gab, ta=True, out_dtype=BF16, name="dw_gab")
    cut = IN_WIDTH // 4
    d_win = jnp.stack([d_wmain[:, :cut], d_wmain[:, cut:2 * cut],
                       jnp.concatenate([d_wmain[:, 2 * cut:8192], d_wgab[:, :32], d_wmain[:, 8192:3 * cut - 32]], axis=1),
                       d_wmain[:, 3 * cut - 32:]])
    d_conv = jnp.concatenate([dwc_q[:CONV_K], dwc_k[:CONV_K], dwc_v[:CONV_K]], axis=1).astype(BF16)
    started = exchange.reduce("mixer", {"w_in": d_win, "gdn_conv_w": d_conv, "w_branch_hgrn": d_wbh,
                                        "w_branch_gdn": d_wbg, "w_out": d_wout}, behind=True)
    dh1, dh1_bf, d_gm = _rmsnorm_bwd(h1, gm + started, du, dh2, "mix_dnorm")
    dx, _, d_g1, d_f1in, d_f1out = _ffn_bwd(x, g1, w1["ffn1_w_in"], w1["ffn1_w_out"], ffn1_saved, dh1, dh1_bf, "ffn1")
    exchange.reduce("ffn1", {"ffn1_w_in": d_f1in, "ffn1_w_out": d_f1out}, behind=True)
    d_lb0 = d_lb0.reshape(1, -1)
    sm = {"ffn1_norm": d_g1, "mix_norm": d_gm, "hgrn_lb_logits": jnp.concatenate([d_lb0, -d_lb0], axis=0),
          "hgrn_out_norm": d_hg_gain, "gdn_a_log": gd_small[2, :16], "gdn_dt_bias": gd_small[1, :16],
          "gdn_out_norm": gd_small[0], "ffn2_norm": d_g2, "final_norm": d_gf, "loss": loss[0, :1]}
    return dx, sm


class _Exchange:
    def __init__(self, wts):
        self.wts = wts
        xi, yi, ci = _place()
        self.chip = 2 * xi + yi
        self.south = ci == 0
        self.core = ci.reshape(1).astype(jnp.int32)
        self.mine = {}
        self.coming = {}
        self.going = {}

    def _packs(self, tag):
        group = _GROUPS[tag]
        return _pack({n: self.wts[n][0].astype(BF16) for n in _group_names(group)}, (), group)

    def prefetch(self, tag):
        packs = self._packs(tag)
        handle, token = _ici_start(packs, _GROUPS[tag]["chunks"], "gather", "gather_start_" + tag)
        self.coming[tag] = handle
        return token[0:1, 0:1]

    def weights(self, tag, after=None):
        group = _GROUPS[tag]
        if tag in self.coming:
            packs, halves = _ici_wait(self.coming.pop(tag), after, "gather_wait_" + tag)
            others = _pass_to_sibling(halves, group["chunks"], "gather_pass_" + tag)
        else:
            packs = self._packs(tag)
            others = _gather_weights(packs, group["chunks"], "gather_" + tag)
        whole = [lax.dynamic_update_index_in_dim(g, p, self.chip, 0) for g, p in zip(others, packs)]
        gathered = _unpack(*whole, group)
        return {n: _full_from_shards(n, gathered[n]) for n in _group_names(group)}

    def reduce(self, tag, grads, behind=False):
        group = _GROUPS[tag]
        shards = {n: (grads[n] if grads[n].ndim == 3 else _shards_from_full(n, grads[n])) for n in _group_names(group)}
        gpacks = _pack(shards, (4,), group)
        got = _swap_with_sibling(gpacks, group["chunks"], 4, "reduce_pair_" + tag, halves=True)
        sums = [_add2(a, b, self.core, "add_pair_%s_%d" % (tag, i)) for i, (a, b) in enumerate(zip(gpacks, got))]
        if behind:
            handle, token = _ici_start(sums, group["chunks"], "reduce", "reduce_start_" + tag)
            self.going[tag] = handle
            self.token = token
            return token[0:1, 0:1]
        self._add_chips(tag, sums, _reduce_chips(sums, group["chunks"], "reduce_chips_" + tag))
        return None

    def _add_chips(self, tag, sums, from_chips):
        self.mine[tag] = [_add4(lax.dynamic_index_in_dim(s, self.chip, axis=0, keepdims=False), f,
                                "add_chips_%s_%d" % (tag, i)) for i, (s, f) in enumerate(zip(sums, from_chips))]

    def finish(self, tags, after):
        for tag in tags:
            if tag in self.going:
                self._add_chips(tag, *_ici_wait(self.going.pop(tag), after, "reduce_wait_" + tag))
        mine = [a for t in tags for a in self.mine[t]]
        nchs = [k for t in tags for k in _GROUPS[t]["chunks"]]
        theirs = _swap_with_sibling(mine, nchs, 0, "share_pair_" + tags[0])
        whole = [jnp.concatenate([jnp.where(self.south, a, b), jnp.where(self.south, b, a)], axis=0)
                 for a, b in zip(mine, theirs)]
        reduced = {}
        for i, t in enumerate(tags):
            reduced.update(_unpack(whole[2 * i], whole[2 * i + 1], _GROUPS[t]))
        return reduced


_WEIGHTS = ("ffn1_norm", "ffn1_w_in", "ffn1_w_out", "mix_norm", "w_in", "hgrn_lb_logits", "hgrn_out_norm",
            "gdn_conv_w", "gdn_a_log", "gdn_dt_bias", "gdn_out_norm", "w_branch_hgrn", "w_branch_gdn", "w_out",
            "ffn2_norm", "ffn2_w_in", "ffn2_w_out", "final_norm")


def kernel(x, ffn1_norm, ffn1_w_in, ffn1_w_out, mix_norm, w_in, hgrn_lb_logits, hgrn_out_norm, gdn_conv_w, gdn_a_log, gdn_dt_bias, gdn_out_norm, w_branch_hgrn, w_branch_gdn, w_out, ffn2_norm, ffn2_w_in, ffn2_w_out, final_norm, loss_target, m_ffn1_norm, m_ffn1_w_in, m_ffn1_w_out, m_mix_norm, m_w_in, m_hgrn_lb_logits, m_hgrn_out_norm, m_gdn_conv_w, m_gdn_a_log, m_gdn_dt_bias, m_gdn_out_norm, m_w_branch_hgrn, m_w_branch_gdn, m_w_out, m_ffn2_norm, m_ffn2_w_in, m_ffn2_w_out, m_final_norm, v_ffn1_norm, v_ffn1_w_in, v_ffn1_w_out, v_mix_norm, v_w_in, v_hgrn_lb_logits, v_hgrn_out_norm, v_gdn_conv_w, v_gdn_a_log, v_gdn_dt_bias, v_gdn_out_norm, v_w_branch_hgrn, v_w_branch_gdn, v_w_out, v_ffn2_norm, v_ffn2_w_in, v_ffn2_w_out, v_final_norm):
    args = dict(locals())
    wts = {n: args[n] for n in _WEIGHTS}
    moms = {n: args["m_" + n] for n in _WEIGHTS}
    vars_ = {n: args["v_" + n] for n in _WEIGHTS}

    small = {n: wts[n].astype(F32) for n in _WEIGHTS if n not in _BIG_NAMES}
    exchange = _Exchange(wts)
    dx, small_grads = _local_step(x[0], loss_target[0], small, exchange)

    out_g, out_d, out_m, out_v = {}, {}, {}, {}

    def update(tags, reduced, after):
        for t in tags:
            for n in _group_names(_GROUPS[t]):
                shape = wts[n].shape
                w2 = wts[n].reshape(shape[-2], shape[-1])
                g2 = reduced[n]
                d, m2, v2 = _adamw(w2, g2, moms[n].reshape(w2.shape), vars_[n].reshape(w2.shape), "adamw_" + n, after)
                out_g[n], out_d[n], out_m[n], out_v[n] = (g2.reshape(shape), d.reshape(shape), m2.reshape(shape),
                                                          v2.reshape(shape))
                after = v2
        return after

    done = update(("ffn2", "mixer"), exchange.finish(("ffn2", "mixer"), after=dx), exchange.token)
    update(("ffn1",), exchange.finish(("ffn1",), after=done), None)

    small_names = [n for n, _ in _SMALL]
    zero = jnp.zeros((1,), F32)
    shapes = {n: (wts[n].shape if n != "loss" else (1,)) for n in small_names}
    sums, sd, sm_, sv = _small_sync(
        _pack_small(small_grads),
        _pack_small({n: (wts[n] if n != "loss" else zero) for n in small_names}),
        _pack_small({n: (moms[n] if n != "loss" else zero) for n in small_names}),
        _pack_small({n: (vars_[n] if n != "loss" else zero) for n in small_names}))
    sg_u, sd_u, sm_u, sv_u = (_unpack_small(p, shapes) for p in (sums, sd, sm_, sv))
    for n in small_names:
        if n != "loss":
            out_g[n], out_d[n], out_m[n], out_v[n] = sg_u[n], sd_u[n], sm_u[n], sv_u[n]
    loss = sg_u["loss"].reshape(())

    return (loss, dx[None], *[out_g[n] for n in _WEIGHTS], *[out_d[n] for n in _WEIGHTS],
            *[out_m[n] for n in _WEIGHTS], *[out_v[n] for n in _WEIGHTS])
```

```python
import numpy as np

import jax
import jax.numpy as jnp
from jax import lax
from jax.experimental import pallas as pl
from jax.experimental.pallas import tpu as pltpu

F32 = jnp.float32
BF16 = jnp.bfloat16

D_MODEL = 1024
D_FF = 2816
CHUNK = 64
HEAD = 128
HG_HEADS = 8
GD_HEADS = 16
HPS = 8
COMM_CHUNKS = 9
MM_TM = 1408
MM_TN = 1024
MM_TK = 1536
VMEM_LIMIT = 48 * 1024 * 1024
EPS = 1e-6
CONV_K = 4
IN_NAMES = ("hq", "hf", "hi", "hg", "gq", "gk", "gv", "ga", "gb", "gz", "gate_h", "gate_g")
IN_SIZES = (1024, 1024, 1024, 1024, 1024, 1024, 2048, 16, 16, 2048, 1024, 1024)
IN_WIDTH = sum(IN_SIZES)

ADAM_LR = 0.001
ADAM_B1 = 0.9
ADAM_B2 = 0.999
ADAM_EPS = 1e-08
ADAM_WD = 0.01
ADAM_STEP = 10

MESH = pl.DeviceIdType.MESH
_ARB = "arbitrary"
_PAR = "parallel"


def _bf(x):
    return x.astype(BF16)


def _dot(a, b):
    return jnp.dot(_bf(a), _bf(b), preferred_element_type=F32)


def _dot_nt(a, b):
    return lax.dot_general(_bf(a), _bf(b), (((1,), (1,)), ((), ())), preferred_element_type=F32)


def _dot_tn(a, b):
    return lax.dot_general(_bf(a), _bf(b), (((0,), (0,)), ((), ())), preferred_element_type=F32)


def _sigmoid(x):
    return jax.nn.sigmoid(x)


def _silu(x):
    return x * _sigmoid(x)


def _dsilu(x):
    s = _sigmoid(x)
    return s * (1.0 + x * (1.0 - s))


def _softplus(x):
    return jnp.maximum(x, 0.0) + jnp.log(1.0 + jnp.exp(-jnp.abs(x)))


def _rowsum(x):
    return jnp.sum(x, axis=1, keepdims=True)


def _col_to_row(col, eye):
    return jnp.sum(eye * col, axis=0, keepdims=True)


def _row_to_col(row, eye):
    return jnp.sum(eye * row, axis=1, keepdims=True)


def _pick(dim, pref, unit=128):
    if dim <= pref:
        return dim
    t = pref
    while t >= unit:
        if dim % t == 0:
            return t
        t -= unit
    return dim


def _params(*sem):
    return pltpu.CompilerParams(dimension_semantics=tuple(sem), vmem_limit_bytes=VMEM_LIMIT)


def _mm(a, b, *, ta=False, tb=False, alpha=1.0, res=None, out_dtype=F32, name="mm", b_from=0):
    m = a.shape[1] if ta else a.shape[0]
    k = a.shape[0] if ta else a.shape[1]
    n = b.shape[0] if tb else b.shape[1]
    assert b_from + k <= (b.shape[1] if tb else b.shape[0])
    tm, tn, tk = _pick(m, MM_TM), _pick(n, MM_TN), _pick(k, MM_TK)
    if tn < MM_TN < n and n % MM_TM == 0:
        tn = MM_TM
    nk = k // tk
    assert b_from % tk == 0
    b0 = b_from // tk
    a_spec = pl.BlockSpec((tk, tm), lambda i, j, l: (l, i)) if ta else pl.BlockSpec((tm, tk), lambda i, j, l: (i, l))
    b_spec = (pl.BlockSpec((tn, tk), lambda i, j, l: (j, b0 + l)) if tb
              else pl.BlockSpec((tk, tn), lambda i, j, l: (b0 + l, j)))
    o_spec = pl.BlockSpec((tm, tn), lambda i, j, l: (i, j))
    dims = (((0 if ta else 1,), (1 if tb else 0,)), ((), ()))
    has_res = res is not None

    def finish(r, r_ref, o_ref):
        if alpha != 1.0:
            r = r * alpha
        if has_res:
            r = r + r_ref[...]
        o_ref[...] = r.astype(out_dtype)

    def body(*refs):
        a_ref, b_ref = refs[0], refs[1]
        r_ref = refs[2] if has_res else None
        o_ref = refs[3] if has_res else refs[2]
        part = lax.dot_general(_bf(a_ref[...]), _bf(b_ref[...]), dims, preferred_element_type=F32)
        if nk == 1:
            finish(part, r_ref, o_ref)
            return
        acc = refs[-1]
        step = pl.program_id(2)

        @pl.when(step == 0)
        def _():
            acc[...] = part

        @pl.when(step != 0)
        def _():
            acc[...] += part

        @pl.when(step == nk - 1)
        def _():
            finish(acc[...], r_ref, o_ref)

    ins = [a, b] + ([res] if has_res else [])
    in_specs = [a_spec, b_spec] + ([o_spec] if has_res else [])
    return pl.pallas_call(
        body, name=name, grid=(m // tm, n // tn, nk), in_specs=in_specs, out_specs=o_spec,
        out_shape=jax.ShapeDtypeStruct((m, n), out_dtype),
        scratch_shapes=[pltpu.VMEM((tm, tn), F32)] if nk > 1 else [],
        compiler_params=_params(_PAR, _PAR, _ARB))(*ins)


def _row_spec(tr, w):
    return pl.BlockSpec((tr, w), lambda i: (i, 0))


def _full_spec(shape):
    return pl.BlockSpec(shape, lambda i: tuple(0 for _ in shape))


def _view(arr, off, width):
    return arr, off, width


def _view_rows(view, tr):
    _, off, width = view
    assert off % width == 0
    return pl.BlockSpec((tr, width), lambda i: (i, off // width))


def _view_tile(view, rows, bw, cidx=lambda c: c):
    _, off, width = view
    assert off % bw == 0 and width % bw == 0
    return pl.BlockSpec((rows, bw), lambda c, g: (cidx(c), off // bw + g))


def _rmsnorm_fwd(x, g, name):
    t, d = x.shape
    tr = _pick(t, 256, 8)

    def body(x_ref, g_ref, o_ref):
        xv = x_ref[...]
        r = lax.rsqrt(jnp.mean(xv * xv, axis=1, keepdims=True) + EPS)
        o_ref[...] = (xv * r * g_ref[...]).astype(BF16)

    return pl.pallas_call(
        body, name=name, grid=(t // tr,), in_specs=[_row_spec(tr, d), _full_spec((1, d))],
        out_specs=_row_spec(tr, d), out_shape=jax.ShapeDtypeStruct((t, d), BF16),
        compiler_params=_params(_PAR))(x, g)


def _rmsnorm_bwd(x, g, dn, res, name):
    t, d = x.shape
    tr = _pick(t, 256, 8)

    def body(x_ref, g_ref, dn_ref, r_ref, dx_ref, dxb_ref, dg_ref):
        @pl.when(pl.program_id(0) == 0)
        def _():
            dg_ref[...] = jnp.zeros_like(dg_ref)

        xv = x_ref[...]
        r = lax.rsqrt(jnp.mean(xv * xv, axis=1, keepdims=True) + EPS)
        xh = xv * r
        dy = dn_ref[...]
        dg_ref[...] += jnp.sum(dy * xh, axis=0, keepdims=True)
        dxh = dy * g_ref[...]
        dx = r_ref[...] + r * (dxh - xh * jnp.mean(dxh * xh, axis=1, keepdims=True))
        dx_ref[...] = dx
        dxb_ref[...] = dx.astype(BF16)

    return pl.pallas_call(
        body, name=name, grid=(t // tr,),
        in_specs=[_row_spec(tr, d), _full_spec((1, d)), _row_spec(tr, d), _row_spec(tr, d)],
        out_specs=[_row_spec(tr, d), _row_spec(tr, d), _full_spec((1, d))],
        out_shape=[jax.ShapeDtypeStruct((t, d), F32), jax.ShapeDtypeStruct((t, d), BF16),
                   jax.ShapeDtypeStruct((1, d), F32)],
        compiler_params=_params(_ARB))(x, g, dn, res)


FFN_TN = 1408
FFN_TM = 512


def _ffn_pieces():
    return [slice(c, min(c + 256, FFN_TN)) for c in range(0, FFN_TN, 256)]


def _ffn_in_act(n, w_in, name):
    t, d = n.shape
    tm = _pick(t, FFN_TM)
    nf = D_FF // FFN_TN

    def body(n_ref, wa_ref, wb_ref, a_ref, b_ref, hm_ref):
        nv = n_ref[...]
        for cols in _ffn_pieces():
            a = jnp.dot(nv, wa_ref[:, cols], preferred_element_type=F32)
            b = jnp.dot(nv, wb_ref[:, cols], preferred_element_type=F32)
            a_ref[:, cols] = a.astype(BF16)
            b_ref[:, cols] = b.astype(BF16)
            hm_ref[:, cols] = (_silu(a) * b).astype(BF16)

    tile = pl.BlockSpec((tm, FFN_TN), lambda i, j: (i, j))
    return pl.pallas_call(
        body, name=name, grid=(t // tm, nf),
        in_specs=[pl.BlockSpec((tm, d), lambda i, j: (i, 0)), pl.BlockSpec((d, FFN_TN), lambda i, j: (0, j)),
                  pl.BlockSpec((d, FFN_TN), lambda i, j: (0, nf + j))],
        out_specs=[tile, tile, tile], out_shape=[jax.ShapeDtypeStruct((t, D_FF), BF16)] * 3,
        compiler_params=_params(_PAR, _PAR))(n, w_in, w_in)


def _ffn_dact(dout, w_out, a, b, name):
    t, d = dout.shape
    tm = _pick(t, FFN_TM)

    def body(do_ref, w_ref, a_ref, b_ref, da_ref, db_ref, hm_ref):
        dov = do_ref[...]
        for cols in _ffn_pieces():
            dh = 0.5 * _dot_nt(dov, w_ref[cols, :])
            av = a_ref[:, cols].astype(F32)
            bv = b_ref[:, cols].astype(F32)
            sg = _sigmoid(av)
            sa = av * sg
            da_ref[:, cols] = (dh * bv * (sg * (1.0 + av * (1.0 - sg)))).astype(BF16)
            db_ref[:, cols] = (dh * sa).astype(BF16)
            hm_ref[:, cols] = (sa * bv).astype(BF16)

    tile = pl.BlockSpec((tm, FFN_TN), lambda i, j: (i, j))
    return pl.pallas_call(
        body, name=name, grid=(t // tm, D_FF // FFN_TN),
        in_specs=[pl.BlockSpec((tm, d), lambda i, j: (i, 0)), pl.BlockSpec((FFN_TN, d), lambda i, j: (j, 0)), tile, tile],
        out_specs=[tile, tile, tile], out_shape=[jax.ShapeDtypeStruct((t, D_FF), BF16)] * 3,
        compiler_params=_params(_PAR, _PAR))(dout, w_out, a, b)


def _merge_fwd(yh, yg, gh, gg):
    t, d = yh.shape
    tr = _pick(t, 256, 8)

    def body(yh_ref, yg_ref, gh_ref, gg_ref, o_ref):
        o_ref[...] = (_sigmoid(gh_ref[...]) * yh_ref[...] + _sigmoid(gg_ref[...]) * yg_ref[...]).astype(BF16)

    return pl.pallas_call(
        body, name="merge_fwd", grid=(t // tr,),
        in_specs=[_row_spec(tr, d), _row_spec(tr, d), _view_rows(gh, tr), _view_rows(gg, tr)],
        out_specs=_row_spec(tr, d),
        out_shape=jax.ShapeDtypeStruct((t, d), BF16), compiler_params=_params(_PAR))(yh, yg, gh[0], gg[0])


def _into(dproj, off, width):
    return dproj, off, width


def _merge_bwd(dy, yh, yg, gh, gg, into):
    t, d = yh.shape
    tr = _pick(t, 256, 8)
    dproj, off, width = into
    assert width == 2 * d and off % width == 0

    def body(dy_ref, yh_ref, yg_ref, gh_ref, gg_ref, _, dyh_ref, dyg_ref, dg_ref):
        dyv = dy_ref[...]
        sh = _sigmoid(gh_ref[...])
        sg = _sigmoid(gg_ref[...])
        dyh_ref[...] = (dyv * sh).astype(BF16)
        dyg_ref[...] = (dyv * sg).astype(BF16)
        dg_ref[:, :d] = (dyv * yh_ref[...] * sh * (1.0 - sh)).astype(BF16)
        dg_ref[:, d:] = (dyv * yg_ref[...] * sg * (1.0 - sg)).astype(BF16)

    return pl.pallas_call(
        body, name="merge_bwd", grid=(t // tr,),
        in_specs=[_row_spec(tr, d)] * 3 + [_view_rows(gh, tr), _view_rows(gg, tr), _ANY],
        out_specs=[_row_spec(tr, d)] * 2 + [pl.BlockSpec((tr, width), lambda i: (i, off // width))],
        out_shape=[jax.ShapeDtypeStruct((t, d), BF16)] * 2 + [jax.ShapeDtypeStruct(dproj.shape, dproj.dtype)],
        input_output_aliases={5: 2},
        compiler_params=_params(_PAR))(dy, yh, yg, gh[0], gg[0], dproj)


def _final_loss(h, g, tgt):
    t, d = h.shape
    tr = _pick(t, 256, 8)

    def body(h_ref, g_ref, t_ref, loss_ref, dh_ref, dhb_ref, dg_ref):
        @pl.when(pl.program_id(0) == 0)
        def _():
            dg_ref[...] = jnp.zeros_like(dg_ref)
            loss_ref[...] = jnp.zeros_like(loss_ref)

        xv = h_ref[...]
        gv = g_ref[...]
        r = lax.rsqrt(jnp.mean(xv * xv, axis=1, keepdims=True) + EPS)
        xh = xv * r
        err = xh * gv - t_ref[...]
        loss_ref[...] += 0.5 * jnp.sum(jnp.mean(err * err, axis=1, keepdims=True), axis=0, keepdims=True)
        dy = err * (1.0 / d)
        dg_ref[...] += jnp.sum(dy * xh, axis=0, keepdims=True)
        dxh = dy * gv
        dh = r * (dxh - xh * jnp.mean(dxh * xh, axis=1, keepdims=True))
        dh_ref[...] = dh
        dhb_ref[...] = dh.astype(BF16)

    return pl.pallas_call(
        body, name="final_loss", grid=(t // tr,),
        in_specs=[_row_spec(tr, d), _full_spec((1, d)), _row_spec(tr, d)],
        out_specs=[_full_spec((1, 128)), _row_spec(tr, d), _row_spec(tr, d), _full_spec((1, d))],
        out_shape=[jax.ShapeDtypeStruct((1, 128), F32), jax.ShapeDtypeStruct((t, d), F32),
                   jax.ShapeDtypeStruct((t, d), BF16), jax.ShapeDtypeStruct((1, d), F32)],
        compiler_params=_params(_ARB))(h, g, tgt)


def _hg_consts():
    c = CHUNK
    t = np.arange(c)
    mats, masks = [], []
    for lvl in range(6):
        m = 1 << lvl
        blk = t // m
        mat = np.zeros((c, c), np.float32)
        for tt in range(c):
            b = blk[tt]
            if b % 2 == 1:
                mat[tt, b * m:tt + 1] = 1.0
            else:
                mat[tt, tt + 1:(b + 1) * m] = 1.0
        mats.append(mat)
        same = (t[:, None] // (2 * m)) == (t[None, :] // (2 * m))
        masks.append((same & (blk[:, None] % 2 == 1) & (blk[None, :] % 2 == 0)).astype(np.float32))
    pre = np.tril(np.ones((c, c), np.float32))
    suf = np.triu(np.ones((c, c), np.float32), 1)
    mstack = np.concatenate(mats + [pre, suf], 0)
    masks.append(np.eye(c, dtype=np.float32))
    return (jnp.asarray(mstack, BF16), jnp.asarray(mstack.T.copy(), BF16), jnp.asarray(np.stack(masks), F32),
            jnp.asarray(np.eye(HEAD, dtype=np.float32)))


def _gd_consts():
    c = CHUNK
    incl = np.tril(np.ones((c, c), np.float32))
    strict = np.tril(np.ones((c, c), np.float32), -1)
    eye = np.eye(c, dtype=np.float32)
    masks = np.stack([incl, strict, eye, incl.T.copy()])
    return jnp.asarray(incl, BF16), jnp.asarray(incl.T.copy(), BF16), jnp.asarray(masks, F32)


def _chunks_per_step(nc):
    for cb in (32 // HPS, 2, 1):
        if nc % cb == 0:
            return cb
    return 1


def _hg_prep(hq, hf, lg):
    lb = _sigmoid(lg[0:1, :] - lg[1:2, :])
    sg = _sigmoid(hf)
    sgn = _sigmoid(-hf)
    f = lb + (1.0 - lb) * sg
    lf = jnp.log(f)
    kk = (1.0 - lb) * sgn
    q = _silu(hq) * (HEAD ** -0.5)
    return lb, sg, sgn, f, lf, kk, q


def _mx_each(m, xs):
    hi, lo = _split2_each(xs)
    prods = [jnp.dot(m, jnp.concatenate([h, l], axis=1), preferred_element_type=F32) for h, l in zip(hi, lo)]
    return [p[:, :HEAD] + p[:, HEAD:] for p in prods]


def _hg_scaled(x, ex):
    xb = [_bf(a) for a in x]
    eb = [_bf(e[:6 * CHUNK]) for e in ex]
    return [[a * e[lvl * CHUNK:(lvl + 1) * CHUNK] for lvl in range(6)] for a, e in zip(xb, eb)]


def _hg_scores(q, kk, qe, ke, mask_ref):
    p = [mask_ref[6] * _rowsum(a * b) for a, b in zip(q, kk)]
    for lvl in range(6):
        d = [_dot_nt(a[lvl], b[lvl]) for a, b in zip(qe, ke)]
        p = [x + mask_ref[lvl] * y for x, y in zip(p, d)]
    return p


def _hgrn_fwd(hq, hf, hi, hg, logits, gain, consts):
    t = hq[0].shape[0]
    nc = t // CHUNK
    cb = _chunks_per_step(nc)
    rows = cb * CHUNK
    mstack, _, masks, eye = consts
    tile = pl.BlockSpec((rows, HPS * HEAD), lambda c, g: (c, g))

    def body(hq_ref, hf_ref, hi_ref, hg_ref, lg_ref, gain_ref, m_ref, mask_ref, eye_ref,
             oraw_ref, og_ref, ssave_ref, state):
        c = pl.program_id(0)
        g = pl.program_id(1)

        @pl.when(c == 0)
        def _():
            for hh in range(HPS):
                state[g * HPS + hh] = jnp.zeros((HEAD, HEAD), F32)

        lg_all = lg_ref[...]
        gain_v = gain_ref[...]

        def one(i, carry):
            sl = pl.ds(pl.multiple_of(i * CHUNK, CHUNK), CHUNK)
            hs = range(HPS)
            heads = [g * HPS + hh for hh in hs]
            ln = [slice(hh * HEAD, (hh + 1) * HEAD) for hh in hs]
            preps = [_hg_prep(hq_ref[sl, s], hf_ref[sl, s], lg_all[:, s]) for s in ln]
            lf, kk, q = [p[4] for p in preps], [p[5] for p in preps], [p[6] for p in preps]
            v = [hi_ref[sl, s] for s in ln]
            ex = [jnp.exp(x) for x in _mx_each(m_ref[...], lf)]
            eb = [e[6 * CHUNK:7 * CHUNK] for e in ex]
            esfx = [e[7 * CHUNK:8 * CHUNK] for e in ex]
            qe, ke = _hg_scaled(q, ex), _hg_scaled(kk, ex)
            p = _hg_scores(q, kk, qe, ke, mask_ref)
            s0 = [state[h] for h in heads]
            o = _each(lambda a, e, s, pp, vv: _dot(a * e, s) + _dot(pp, vv), q, eb, s0, p, v)
            eye_v = eye_ref[...]
            s1 = _each(lambda s, e, kx, ef, vv: s * _row_to_col(e[CHUNK - 1:CHUNK, :], eye_v) + _dot_tn(kx * ef, vv),
                       s0, eb, kk, esfx, v)
            for hh in hs:
                ssave_ref[i, hh] = s0[hh]
                state[heads[hh]] = s1[hh]
                oraw_ref[sl, ln[hh]] = o[hh]
                r = lax.rsqrt(jnp.mean(o[hh] * o[hh], axis=1, keepdims=True) + EPS)
                og_ref[sl, ln[hh]] = (o[hh] * r * gain_v * _silu(hg_ref[sl, ln[hh]])).astype(BF16)
            return carry

        lax.fori_loop(0, cb, one, 0, unroll=4)

    return pl.pallas_call(
        body, name="hgrn_fwd", grid=(nc // cb, HG_HEADS // HPS),
        in_specs=[_view_tile(v, rows, HPS * HEAD) for v in (hq, hf, hi, hg)] + [
                  pl.BlockSpec((2, HPS * HEAD), lambda c, g: (0, g)),
                  pl.BlockSpec((1, HEAD), lambda c, g: (0, 0)),
                  pl.BlockSpec(mstack.shape, lambda c, g: (0, 0)),
                  pl.BlockSpec(masks.shape, lambda c, g: (0, 0, 0)),
                  pl.BlockSpec(eye.shape, lambda c, g: (0, 0))],
        out_specs=[tile, tile, pl.BlockSpec((cb, HPS, HEAD, HEAD), lambda c, g: (c, g, 0, 0))],
        out_shape=[jax.ShapeDtypeStruct((t, HG_HEADS * HEAD), F32), jax.ShapeDtypeStruct((t, HG_HEADS * HEAD), BF16),
                   jax.ShapeDtypeStruct((nc, HG_HEADS, HEAD, HEAD), F32)],
        scratch_shapes=[pltpu.VMEM((HG_HEADS, HEAD, HEAD), F32)],
        compiler_params=_params(_ARB, _ARB))(hq[0], hf[0], hi[0], hg[0], logits, gain, mstack, masks, eye)


def _hgrn_bwd(hq, hf, hi, hg, logits, gain, oraw, ssave, dog, consts, into):
    t = hq[0].shape[0]
    dproj, off, width = into
    seg = HG_HEADS * HEAD
    assert HPS == HG_HEADS and width == 4 * seg and off % width == 0
    nc = t // CHUNK
    cb = _chunks_per_step(nc)
    rows = cb * CHUNK
    nb = nc // cb
    mstack, mstack_t, masks, eye = consts
    tile = pl.BlockSpec((rows, HPS * HEAD), lambda c, g: (nb - 1 - c, g))

    def body(hq_ref, hf_ref, hi_ref, hg_ref, lg_ref, gain_ref, oraw_ref, ssave_ref, dog_ref, m_ref, mt_ref,
             mask_ref, eye_ref, _, d_ref, dgain_ref, dlb_ref, dstate):
        c = pl.program_id(0)
        g = pl.program_id(1)

        @pl.when(c == 0)
        def _():
            for hh in range(HPS):
                dstate[g * HPS + hh] = jnp.zeros((HEAD, HEAD), F32)

        @pl.when((c == 0) & (g == 0))
        def _():
            dgain_ref[...] = jnp.zeros_like(dgain_ref)
            dlb_ref[...] = jnp.zeros_like(dlb_ref)

        lg_all = lg_ref[...]
        gain_v = gain_ref[...]
        eye_v = eye_ref[...]
        last_row = (lax.broadcasted_iota(jnp.int32, (CHUNK, HEAD), 0) == CHUNK - 1).astype(F32)

        def one(j, carry):
            i = cb - 1 - j
            sl = pl.ds(pl.multiple_of(i * CHUNK, CHUNK), CHUNK)
            hs = range(HPS)
            heads = [g * HPS + hh for hh in hs]
            ln = [slice(hh * HEAD, (hh + 1) * HEAD) for hh in hs]
            hqv = [hq_ref[sl, s] for s in ln]
            hgv = [hg_ref[sl, s] for s in ln]
            preps = [_hg_prep(a, hf_ref[sl, s], lg_all[:, s]) for a, s in zip(hqv, ln)]
            lb, sg, sgn, f, lf, kk, q = ([p[n] for p in preps] for n in range(7))
            v = [hi_ref[sl, s] for s in ln]
            ex = [jnp.exp(x) for x in _mx_each(m_ref[...], lf)]
            eb = [e[6 * CHUNK:7 * CHUNK] for e in ex]
            esfx = [e[7 * CHUNK:8 * CHUNK] for e in ex]
            qe, ke = _hg_scaled(q, ex), _hg_scaled(kk, ex)
            p = _hg_scores(q, kk, qe, ke, mask_ref)
            s0 = [ssave_ref[i, hh] for hh in hs]
            ds = [dstate[h] for h in heads]

            o = [oraw_ref[sl, s] for s in ln]
            r = [lax.rsqrt(jnp.mean(x * x, axis=1, keepdims=True) + EPS) for x in o]
            on = _each(lambda x, y: x * y, o, r)
            dg_out = [dog_ref[sl, s] for s in ln]
            sgate = [_silu(x) for x in hgv]
            for hh in hs:
                d_ref[sl, slice(3 * seg + hh * HEAD, 3 * seg + (hh + 1) * HEAD)] =(dg_out[hh] * on[hh] * gain_v * _dsilu(hgv[hh])).astype(BF16)
            dgain_ref[...] += sum(jnp.sum(d * s * n, axis=0, keepdims=True) for d, s, n in zip(dg_out, sgate, on))
            don = _each(lambda d, s: d * s * gain_v, dg_out, sgate)
            do = _each(lambda rr, dn, n: rr * (dn - n * jnp.mean(dn * n, axis=1, keepdims=True)), r, don, on)

            dp = _each(_dot_nt, do, v)
            dv = _each(lambda pp, d, kx, ef, s: _dot_tn(pp, d) + _dot(kx * ef, s), p, do, kk, esfx, ds)
            dqb = _each(_dot_nt, do, s0)
            dkx = _each(_dot_nt, v, ds)
            diag = [_rowsum(mask_ref[6] * x) for x in dp]
            dq = _each(lambda a, e, d, kx: a * e + d * kx, dqb, eb, diag, kk)
            dk = _each(lambda a, e, d, qq: a * e + d * qq, dkx, esfx, diag, q)
            dxs = [[] for _ in hs]
            for lvl in range(6):
                el = [e[lvl * CHUNK:(lvl + 1) * CHUNK] for e in ex]
                gm = [mask_ref[lvl] * x for x in dp]
                gm = [_bf(x) for x in gm]
                a1 = _each(lambda m_, kx: _dot(m_, kx[lvl]), gm, ke)
                a2 = _each(lambda m_, qq: _dot_tn(m_, qq[lvl]), gm, qe)
                dq = _each(lambda x, a, e: x + a * e, dq, a1, el)
                dk = _each(lambda x, a, e: x + a * e, dk, a2, el)
                for hh in hs:
                    dxs[hh].append((a1[hh] * q[hh] + a2[hh] * kk[hh]) * el[hh])
            e_end_row = [e[CHUNK - 1:CHUNK, :] for e in eb]
            ds_new = _each(lambda qq, e, d, er, s: _dot_tn(qq * e, d) + _row_to_col(er, eye_v) * s, q, eb, do, e_end_row, ds)
            for hh in hs:
                dstate[heads[hh]] = ds_new[hh]
                dend_row = _col_to_row(_rowsum(s0[hh] * ds[hh]), eye_v)
                dxs[hh].append(dqb[hh] * q[hh] * eb[hh] + last_row * (e_end_row[hh] * dend_row))
                dxs[hh].append(dkx[hh] * kk[hh] * esfx[hh])
            dlf = _mx_each(mt_ref[...], [jnp.concatenate(x, axis=0) for x in dxs])

            for hh in hs:
                d_ref[sl, slice(2 * seg + hh * HEAD, 2 * seg + (hh + 1) * HEAD)] =dv[hh].astype(BF16)
                d_ref[sl, ln[hh]] =(dq[hh] * (HEAD ** -0.5) * _dsilu(hqv[hh])).astype(BF16)
                df = dlf[hh] / f[hh]
                dsig = (1.0 - lb[hh]) * sg[hh] * sgn[hh]
                d_ref[sl, slice(seg + hh * HEAD, seg + (hh + 1) * HEAD)] =((df - dk[hh]) * dsig).astype(BF16)
                dlb_t = jnp.sum(df * sgn[hh] - dk[hh] * sgn[hh], axis=0, keepdims=True)
                dlb_ref[pl.ds(heads[hh], 1), :] += dlb_t * lb[hh] * (1.0 - lb[hh])
            return carry

        lax.fori_loop(0, cb, one, 0, unroll=2)

    outs = [jax.ShapeDtypeStruct(dproj.shape, dproj.dtype),
            jax.ShapeDtypeStruct((1, HEAD), F32), jax.ShapeDtypeStruct((HG_HEADS, HEAD), F32)]
    return pl.pallas_call(
        body, name="hgrn_bwd", grid=(nb, HG_HEADS // HPS),
        in_specs=[_view_tile(v, rows, HPS * HEAD, lambda c: nb - 1 - c) for v in (hq, hf, hi, hg)] + [
                  pl.BlockSpec((2, HPS * HEAD), lambda c, g: (0, g)),
                  pl.BlockSpec((1, HEAD), lambda c, g: (0, 0)), tile,
                  pl.BlockSpec((cb, HPS, HEAD, HEAD), lambda c, g: (nb - 1 - c, g, 0, 0)), tile,
                  pl.BlockSpec(mstack.shape, lambda c, h: (0, 0)),
                  pl.BlockSpec(mstack_t.shape, lambda c, h: (0, 0)),
                  pl.BlockSpec(masks.shape, lambda c, h: (0, 0, 0)),
                  pl.BlockSpec(eye.shape, lambda c, h: (0, 0)), _ANY],
        out_specs=[pl.BlockSpec((rows, width), lambda c, h: (nb - 1 - c, off // width)),
                   pl.BlockSpec((1, HEAD), lambda c, h: (0, 0)),
                   pl.BlockSpec((HG_HEADS, HEAD), lambda c, h: (0, 0))],
        out_shape=outs, scratch_shapes=[pltpu.VMEM((HG_HEADS, HEAD, HEAD), F32)], input_output_aliases={13: 0},
        compiler_params=_params(_ARB, _ARB))(hq[0], hf[0], hi[0], hg[0], logits, gain, oraw, ssave, dog, mstack,
                                             mstack_t, masks, eye, dproj)


CONV_W = 512


def _per_head(fn, *arrs):
    width = arrs[0].shape[1]
    return jnp.concatenate([fn(*[a[:, j:j + HEAD] for a in arrs]) for j in range(0, width, HEAD)], axis=1)


def _shift_down(xv, halo, d, top_rows):
    if d == 0:
        return xv, xv[0:8]
    main = pltpu.roll(xv, d, 0)
    top = jnp.where(top_rows < d, pltpu.roll(halo, d, 0), main[0:8])
    return main, top


def _conv_parts(x_ref, halo_ref, w_ref, first):
    xv = x_ref[...]
    halo = jnp.where(first, 0.0, halo_ref[...])
    top_rows = lax.broadcasted_iota(jnp.int32, (8, xv.shape[1]), 0)
    shifted = [_shift_down(xv, halo, CONV_K - 1 - j, top_rows) for j in range(CONV_K)]
    w = w_ref[...]
    acc = sum(shifted[j][0] * w[j:j + 1, :] for j in range(CONV_K))
    acc_top = sum(shifted[j][1] * w[j:j + 1, :] for j in range(CONV_K))
    return shifted, acc, acc_top


def _conv_fwd(x, w8, l2scale, name):
    x, off, width = x
    t = x.shape[0]
    o = off // CONV_W
    tr = _pick(t, 512, 8)

    def post(cv):
        s = _silu(cv)
        if l2scale is not None:
            s = _per_head(lambda sh: sh * (lax.rsqrt(_rowsum(sh * sh) + EPS) * l2scale), s)
        return s

    def body(x_ref, halo_ref, w_ref, o_ref):
        _, acc, acc_top = _conv_parts(x_ref, halo_ref, w_ref, pl.program_id(1) == 0)
        o_ref[...] = post(acc)
        o_ref[0:8, :] = post(acc_top)

    return pl.pallas_call(
        body, name=name, grid=(width // CONV_W,t // tr),
        in_specs=[pl.BlockSpec((tr, CONV_W), lambda j, i: (i, o + j)),
                  pl.BlockSpec((8, CONV_W), lambda j, i: (jnp.maximum(i * (tr // 8) - 1, 0), o + j)),
                  pl.BlockSpec((8, CONV_W), lambda j, i: (0, j))],
        out_specs=pl.BlockSpec((tr, CONV_W), lambda j, i: (i, j)),
        out_shape=jax.ShapeDtypeStruct((t, width), F32), compiler_params=_params(_PAR, _PAR))(x, x, w8)


def _conv_bwd_a(x, w8, dy, l2scale, name):
    x, off, width = x
    t = x.shape[0]
    o = off // CONV_W
    tr = _pick(t, 512, 8)

    def l2_bwd(s, dyh):
        r = lax.rsqrt(_rowsum(s * s) + EPS)
        y0 = s * r
        dy0 = dyh * l2scale
        return r * (dy0 - y0 * _rowsum(dy0 * y0))

    def to_dc(cv, dyv):
        if l2scale is not None:
            dyv = _per_head(l2_bwd, _silu(cv), dyv)
        return dyv * _dsilu(cv)

    def body(x_ref, halo_ref, w_ref, dy_ref, dc_ref, dw_ref):
        @pl.when(pl.program_id(1) == 0)
        def _():
            dw_ref[...] = jnp.zeros_like(dw_ref)

        shifted, acc, acc_top = _conv_parts(x_ref, halo_ref, w_ref, pl.program_id(1) == 0)
        dyv = dy_ref[...]
        dc = to_dc(acc, dyv)
        dc_top = to_dc(acc_top, dyv[0:8])
        dc_ref[...] = dc
        dc_ref[0:8, :] = dc_top
        rest = (lax.broadcasted_iota(jnp.int32, dc.shape, 0) >= 8).astype(F32)
        dc_rest = dc * rest
        for j in range(CONV_K):
            dw_ref[j:j + 1, :] += (jnp.sum(dc_rest * shifted[j][0], axis=0, keepdims=True)
                                   + jnp.sum(dc_top * shifted[j][1], axis=0, keepdims=True))

    return pl.pallas_call(
        body, name=name, grid=(width // CONV_W,t // tr),
        in_specs=[pl.BlockSpec((tr, CONV_W), lambda j, i: (i, o + j)),
                  pl.BlockSpec((8, CONV_W), lambda j, i: (jnp.maximum(i * (tr // 8) - 1, 0), o + j)),
                  pl.BlockSpec((8, CONV_W), lambda j, i: (0, j)),
                  pl.BlockSpec((tr, CONV_W), lambda j, i: (i, j))],
        out_specs=[pl.BlockSpec((tr, CONV_W), lambda j, i: (i, j)), pl.BlockSpec((8, CONV_W), lambda j, i: (0, j))],
        out_shape=[jax.ShapeDtypeStruct((t, width), F32), jax.ShapeDtypeStruct((8, width), F32)],
        compiler_params=_params(_PAR, _ARB))(x, x, w8, dy)


def _conv_bwd_b(dc, w8, name, into):
    t, width = dc.shape
    tr = _pick(t, 512, 8)
    nt = t // tr

    dproj, off, into_width = into
    assert into_width == width and off % CONV_W == 0
    o = off // CONV_W

    def body(dc_ref, halo_ref, w_ref, _, dx_ref):
        dcv = dc_ref[...]
        halo = jnp.where(pl.program_id(1) == nt - 1, 0.0, halo_ref[...])
        w = w_ref[...]
        bot_rows = lax.broadcasted_iota(jnp.int32, (8, CONV_W), 0)
        acc = dcv * w[CONV_K - 1:CONV_K, :]
        acc_bot = dcv[tr - 8:tr] * w[CONV_K - 1:CONV_K, :]
        for d in range(1, CONV_K):
            main = pltpu.roll(dcv, tr - d, 0)
            bot = jnp.where(bot_rows >= 8 - d, pltpu.roll(halo, 8 - d, 0), main[tr - 8:tr])
            wj = w[CONV_K - 1 - d:CONV_K - d, :]
            acc = acc + main * wj
            acc_bot = acc_bot + bot * wj
        dx_ref[...] = acc.astype(BF16)
        dx_ref[tr - 16:tr, :] = jnp.concatenate([acc[tr - 16:tr - 8], acc_bot], axis=0).astype(BF16)

    return pl.pallas_call(
        body, name=name, grid=(width // CONV_W,nt),
        in_specs=[pl.BlockSpec((tr, CONV_W), lambda j, i: (i, j)),
                  pl.BlockSpec((8, CONV_W), lambda j, i: (jnp.minimum((i + 1) * (tr // 8), t // 8 - 1), j)),
                  pl.BlockSpec((8, CONV_W), lambda j, i: (0, j)), _ANY],
        out_specs=pl.BlockSpec((tr, CONV_W), lambda j, i: (i, o + j)),
        out_shape=jax.ShapeDtypeStruct(dproj.shape, dproj.dtype), input_output_aliases={3: 0},
        compiler_params=_params(_PAR, _PAR))(dc, dc, w8, dproj)


def _each(f, *lists):
    return [f(*xs) for xs in zip(*lists)]


def _split2_each(xs):
    hi = [_bf(x) for x in xs]
    lo = [_bf(x - h.astype(F32)) for x, h in zip(xs, hi)]
    return hi, lo


def _hp_each(a_split, b_split):
    (ah, al), (bh, bl) = a_split, b_split
    rows = ah[0].shape[0]
    d12 = [jnp.dot(jnp.concatenate([x, y], axis=0), z, preferred_element_type=F32) for x, y, z in zip(ah, al, bh)]
    d3 = [jnp.dot(x, y, preferred_element_type=F32) for x, y in zip(ah, bl)]
    return [d[:rows] + d[rows:] + e for d, e in zip(d12, d3)]


INV_EXACT_STEPS = 2


def _tri_inv_each(a_list, eye):
    ns = [-a for a in a_list]
    ps = [eye + n for n in ns]
    n_split = _split2_each(ns)
    for step in range(5):
        if step < INV_EXACT_STEPS:
            ns = _hp_each(n_split, n_split)
            n_split = _split2_each(ns)
            ps = [p + d for p, d in zip(ps, _hp_each(_split2_each(ps), n_split))]
        else:
            nb = n_split[0] if step == INV_EXACT_STEPS else [_bf(n) for n in ns]
            ns = [jnp.dot(x, x, preferred_element_type=F32) for x in nb]
            nb2 = [_bf(n) for n in ns]
            ps = [p + jnp.dot(_bf(p), y, preferred_element_type=F32) for p, y in zip(ps, nb2)]
    return ps


def _gd_gates(gab, alog, dtb):
    sp_arg = gab + dtb
    return sp_arg, -jnp.exp(alog) * _softplus(sp_arg), _sigmoid(gab)


def _pick_lane(tile, base, head):
    g, hh = head
    col = tile[:, base + hh:base + hh + 1]
    for gi in range(1, GD_HEADS // HPS):
        lane = base + gi * HPS + hh
        col = jnp.where(g == gi, tile[:, lane:lane + 1], col)
    return col


def _gd_chunks(q, k, v, g_all, beta_all, heads, l_ref, mask_ref, tm=None):
    incl, strict, eye, upper = mask_ref[0], mask_ref[1], mask_ref[2], mask_ref[3]
    lmat = l_ref[...]
    gb = [jnp.broadcast_to(_pick_lane(g_all, 0, s), (CHUNK, HEAD)) for s in heads]
    bb = [jnp.broadcast_to(_pick_lane(beta_all, GD_HEADS, s), (CHUNK, HEAD)) for s in heads]
    gam = _mx_each(lmat, gb)
    gam_row = [jnp.sum(x[:, :CHUNK] * upper, axis=0, keepdims=True) for x in gb]
    lm = _each(lambda gm, gr: incl * jnp.exp(jnp.minimum(gm[:, :CHUNK] - gr, 0.0)), gam, gam_row)
    kb = _each(lambda x, b: x * b, k, bb)
    a = _each(lambda x, y, m: strict * _dot_nt(x, y) * m, kb, k, lm)
    if tm is None:
        tm = _tri_inv_each(a, eye)
    eg = [jnp.exp(x) for x in gam]
    vb = _each(lambda x, b: x * b, v, bb)
    kbg = _each(lambda x, e: x * e, kb, eg)
    uw = _each(lambda t_, x, y: _dot(t_, jnp.concatenate([x, y], axis=1)), tm, vb, kbg)
    u = [x[:, :HEAD] for x in uw]
    w = [x[:, HEAD:] for x in uw]
    qk = _each(lambda x, y, m: _dot_nt(x, y) * m, q, k, lm)
    g_end = [x[CHUNK - 1:CHUNK, :] for x in gam]
    ekg = _each(lambda e, x: jnp.exp(e - x), g_end, gam)
    ge = [jnp.exp(e) for e in g_end]
    kg = _each(lambda x, e: x * e, k, ekg)
    qg = _each(lambda x, e: x * e, q, eg)
    names = ("bb", "lm", "kb", "a", "tm", "eg", "vb", "kbg", "u", "w", "qk", "ekg", "ge", "kg", "qg")
    cols = (bb, lm, kb, a, tm, eg, vb, kbg, u, w, qk, ekg, ge, kg, qg)
    return [dict(zip(names, vals)) for vals in zip(*cols)]


def _gd_specs(rows, rev_nb=None):
    def cidx(c):
        return c if rev_nb is None else rev_nb - 1 - c

    qk_tile = pl.BlockSpec((rows, HPS // 2 * HEAD), lambda c, g: (cidx(c), g))
    v_tile = pl.BlockSpec((rows, HPS * HEAD), lambda c, g: (cidx(c), g))
    gab_tile = pl.BlockSpec((rows, HEAD), lambda c, g: (cidx(c), 0))
    return qk_tile, v_tile, gab_tile


def _gdn_fwd(qn, kn, cv, gab, gz, alog, dtb, gain, consts):
    t = qn.shape[0]
    nc = t // CHUNK
    cb = _chunks_per_step(nc)
    rows = cb * CHUNK
    lmat, _, masks = consts
    qk_tile, v_tile, gab_tile = _gd_specs(rows)
    row128 = pl.BlockSpec((1, HEAD), lambda c, h: (0, 0))

    def body(q_ref, k_ref, v_ref, gab_ref, gz_ref, alog_ref, dtb_ref, gain_ref, l_ref, mask_ref,
             oraw_ref, og_ref, ssave_ref, tsave_ref, state):
        c = pl.program_id(0)
        g = pl.program_id(1)

        @pl.when(c == 0)
        def _():
            for hh in range(HPS):
                state[g * HPS + hh] = jnp.zeros((HEAD, HEAD), F32)

        alog = alog_ref[...]
        dtb = dtb_ref[...]
        gain_v = gain_ref[...]

        def one(i, carry):
            sl = pl.ds(pl.multiple_of(i * CHUNK, CHUNK), CHUNK)
            _, g_all, beta_all = _gd_gates(gab_ref[sl, :], alog, dtb)
            heads = [g * HPS + hh for hh in range(HPS)]
            lq = [slice(hh // 2 * HEAD, (hh // 2 + 1) * HEAD) for hh in range(HPS)]
            lv = [slice(hh * HEAD, (hh + 1) * HEAD) for hh in range(HPS)]
            chs = _gd_chunks([q_ref[sl, s] for s in lq], [k_ref[sl, s] for s in lq], [v_ref[sl, s] for s in lv],
                             g_all, beta_all, [(g, hh) for hh in range(HPS)], l_ref, mask_ref)
            s0 = [state[h] for h in heads]
            ws = _each(lambda ch, s: _dot(jnp.concatenate([ch["w"], ch["qg"]], axis=0), s), chs, s0)
            v_new = _each(lambda ch, x: ch["u"] - x[:CHUNK], chs, ws)
            o = _each(lambda ch, x, vn: x[CHUNK:] + _dot(ch["qk"], vn), chs, ws, v_new)
            s1 = _each(lambda ch, s, vn: s * ch["ge"] + _dot_tn(ch["kg"], vn), chs, s0, v_new)
            for hh in range(HPS):
                ssave_ref[i, hh] = s0[hh]
                tsave_ref[i, hh] = chs[hh]["tm"]
                state[heads[hh]] = s1[hh]
                oraw_ref[sl, lv[hh]] = o[hh]
                r = lax.rsqrt(jnp.mean(o[hh] * o[hh], axis=1, keepdims=True) + EPS)
                og_ref[sl, lv[hh]] = (o[hh] * r * gain_v * _silu(gz_ref[sl, lv[hh]])).astype(BF16)
            return carry

        lax.fori_loop(0, cb, one, 0, unroll=4)

    return pl.pallas_call(
        body, name="gdn_fwd", grid=(nc // cb, GD_HEADS // HPS),
        in_specs=[qk_tile, qk_tile, v_tile, gab_tile, _view_tile(gz, rows, HPS * HEAD), row128, row128, row128,
                  pl.BlockSpec(lmat.shape, lambda c, g: (0, 0)),
                  pl.BlockSpec(masks.shape, lambda c, g: (0, 0, 0))],
        out_specs=[v_tile, v_tile, pl.BlockSpec((cb, HPS, HEAD, HEAD), lambda c, g: (c, g, 0, 0)),
                   pl.BlockSpec((cb, HPS, CHUNK, CHUNK), lambda c, g: (c, g, 0, 0))],
        out_shape=[jax.ShapeDtypeStruct((t, GD_HEADS * HEAD), F32), jax.ShapeDtypeStruct((t, GD_HEADS * HEAD), BF16),
                   jax.ShapeDtypeStruct((nc, GD_HEADS, HEAD, HEAD), F32),
                   jax.ShapeDtypeStruct((nc, GD_HEADS, CHUNK, CHUNK), F32)],
        scratch_shapes=[pltpu.VMEM((GD_HEADS, HEAD, HEAD), F32)],
        compiler_params=_params(_ARB, _ARB))(qn, kn, cv, gab, gz[0], alog, dtb, gain, lmat, masks)


def _gdn_bwd(qn, kn, cv, gab, gz, alog, dtb, gain, oraw, ssave, tsave, dog, consts, into):
    t = qn.shape[0]
    dproj, off, width = into
    assert width == GD_HEADS * HEAD and off % (HPS * HEAD) == 0
    nc = t // CHUNK
    cb = _chunks_per_step(nc)
    rows = cb * CHUNK
    nb = nc // cb
    lmat, lmat_t, masks = consts
    qk_tile, v_tile, gab_tile = _gd_specs(rows, nb)
    row128 = pl.BlockSpec((1, HEAD), lambda c, h: (0, 0))

    def body(q_ref, k_ref, v_ref, gab_ref, gz_ref, alog_ref, dtb_ref, gain_ref, oraw_ref, ssave_ref, tsave_ref, dog_ref,
             l_ref, lt_ref, mask_ref, _,
             dq_ref, dk_ref, dv_ref, dgab_ref, dgz_ref, small_ref, dstate):
        c = pl.program_id(0)
        g = pl.program_id(1)

        @pl.when(c == 0)
        def _():
            for hh in range(HPS):
                dstate[g * HPS + hh] = jnp.zeros((HEAD, HEAD), F32)

        @pl.when((c == 0) & (g == 0))
        def _():
            small_ref[...] = jnp.zeros_like(small_ref)

        alog = alog_ref[...]
        dtb = dtb_ref[...]
        gain_v = gain_ref[...]
        lane = lax.broadcasted_iota(jnp.int32, (1, HEAD), 1)
        last_row = (lax.broadcasted_iota(jnp.int32, (CHUNK, HEAD), 0) == CHUNK - 1).astype(F32)

        def one(j, carry):
            i = cb - 1 - j
            sl = pl.ds(pl.multiple_of(i * CHUNK, CHUNK), CHUNK)
            sp_arg, g_all, beta_all = _gd_gates(gab_ref[sl, :], alog, dtb)
            strict, eye = mask_ref[1], mask_ref[2]
            ltm = lt_ref[...]
            hs = range(HPS)
            heads = [g * HPS + hh for hh in hs]
            lq = [slice(hh // 2 * HEAD, (hh // 2 + 1) * HEAD) for hh in hs]
            lv = [slice(hh * HEAD, (hh + 1) * HEAD) for hh in hs]
            q = [q_ref[sl, s] for s in lq]
            k = [k_ref[sl, s] for s in lq]
            v = [v_ref[sl, s] for s in lv]
            gzv = [gz_ref[sl, s] for s in lv]
            chs = _gd_chunks(q, k, v, g_all, beta_all, [(g, hh) for hh in hs], l_ref, mask_ref,
                             tm=[tsave_ref[i, hh] for hh in hs])

            def col(name):
                return [ch[name] for ch in chs]

            def mul(x, y):
                return x * y

            tm, lm, eg, bb = col("tm"), col("lm"), col("eg"), col("bb")
            s0 = [ssave_ref[i, hh] for hh in hs]
            ds = [dstate[h] for h in heads]
            v_new = _each(lambda u, w, s: u - _dot(w, s), col("u"), col("w"), s0)

            o = [oraw_ref[sl, s] for s in lv]
            r = [lax.rsqrt(jnp.mean(x * x, axis=1, keepdims=True) + EPS) for x in o]
            on = _each(mul, o, r)
            dg_out = [dog_ref[sl, s] for s in lv]
            sgate = [_silu(x) for x in gzv]
            for hh in hs:
                dgz_ref[sl, lv[hh]] = (dg_out[hh] * on[hh] * gain_v * _dsilu(gzv[hh])).astype(BF16)
            small_ref[0:1, :] += sum(jnp.sum(d * s * n, axis=0, keepdims=True) for d, s, n in zip(dg_out, sgate, on))
            don = _each(lambda d, s: d * s * gain_v, dg_out, sgate)
            do = _each(lambda rr, dn, n: rr * (dn - n * jnp.mean(dn * n, axis=1, keepdims=True)), r, don, on)

            dv_new = _each(lambda a, d, b, s: _dot_tn(a, d) + _dot(b, s), col("qk"), do, col("kg"), ds)
            dqk = _each(_dot_nt, do, v_new)
            dkg = _each(_dot_nt, v_new, ds)
            dge = _each(lambda s, d: jnp.sum(_rowsum(s * d), axis=0, keepdims=True), s0, ds)
            both = _each(lambda d, dv: jnp.concatenate([d, dv], axis=0), do, dv_new)
            from_s = _each(_dot_nt, both, s0)
            dqg = [x[:CHUNK] for x in from_s]
            dw = [-x[CHUNK:] for x in from_s]
            ds_new = _each(lambda qg, w, bo, ge, s: _dot_tn(jnp.concatenate([qg, -w], axis=0), bo) + ge * s,
                           col("qg"), col("w"), both, col("ge"), ds)
            for hh in hs:
                dstate[heads[hh]] = ds_new[hh]

            side = _each(lambda dv, d: jnp.concatenate([dv, d], axis=1), dv_new, dw)
            back = _each(_dot_tn, tm, side)
            dvb = [x[:, :HEAD] for x in back]
            dkbg = [x[:, HEAD:] for x in back]
            dtm = _each(lambda sd, vb, kbg: _dot_nt(sd, jnp.concatenate([vb, kbg], axis=1)), side, col("vb"), col("kbg"))
            dtt = _each(_dot_nt, dtm, tm)
            da = _each(lambda t_, x: -_dot_tn(t_, x) * strict, tm, dtt)
            dal = _each(mul, da, lm)
            dqk_l = _each(mul, dqk, lm)
            stack = _each(lambda x, y: jnp.concatenate([x, y], axis=0), dal, dqk_l)
            on_k = _each(_dot, stack, k)
            dkb = _each(lambda x, y, e: x[:CHUNK] + y * e, on_k, dkbg, eg)
            dq = _each(lambda x, y, e: x[CHUNK:] + y * e, on_k, dqg, eg)
            dk = _each(lambda st, kb, qq, z, ekg, w_, b: _dot_tn(st, jnp.concatenate([kb, qq], axis=0)) + z * ekg + w_ * b,
                       stack, col("kb"), q, dkg, col("ekg"), dkb, bb)
            gmat = _each(lambda x, a, y, qk: x * a + y * qk, da, col("a"), dqk, col("qk"))
            t_kg = _each(lambda x, y: _rowsum(x * y), dkg, col("kg"))
            dgam = _each(lambda gm, x, qg, t_, y, kbg: (_rowsum(gm) - _row_to_col(jnp.sum(gm, axis=0, keepdims=True), eye)
                                                        + _rowsum(x * qg) - t_ + _rowsum(y * kbg)),
                         gmat, dqg, col("qg"), t_kg, dkbg, col("kbg"))
            dg_end = _each(lambda t_, e, ge: jnp.sum(t_, axis=0, keepdims=True) + e * ge[:, 0:1], t_kg, dge, col("ge"))
            dgam = _each(lambda x, e: x + last_row * e, dgam, dg_end)
            dbeta = _each(lambda x, kk, y, vv: _rowsum(x * kk) + _rowsum(y * vv), dkb, k, dvb, v)
            dg = _mx_each(ltm, dgam)

            for hh in hs:
                dv_ref[sl, lv[hh]] = dvb[hh] * bb[hh]
            fac_g = -jnp.exp(alog) * _sigmoid(sp_arg)
            fac_b = beta_all * (1.0 - beta_all)
            hot_g = [(lane == h).astype(F32) for h in heads]
            hot_b = [(lane == GD_HEADS + h).astype(F32) for h in heads]
            dga = _each(lambda x, hot: x * hot * fac_g, dg, hot_g)
            dgb = _each(lambda x, hot: x * hot * fac_b, dbeta, hot_b)
            small_ref[1:2, :] += sum(jnp.sum(x, axis=0, keepdims=True) for x in dga)
            small_ref[2:3, :] += sum(jnp.sum(x * hot * g_all, axis=0, keepdims=True) for x, hot in zip(dg, hot_g))
            for pair in range(HPS // 2):
                lqp = slice(pair * HEAD, (pair + 1) * HEAD)
                dq_ref[sl, lqp] = dq[2 * pair] + dq[2 * pair + 1]
                dk_ref[sl, lqp] = dk[2 * pair] + dk[2 * pair + 1]
            dgab_ref[sl, :] = sum(a + b for a, b in zip(dga, dgb))
            return carry

        lax.fori_loop(0, cb, one, 0, unroll=4)

    groups = GD_HEADS // HPS
    outs = [jax.ShapeDtypeStruct((t, 1024), F32), jax.ShapeDtypeStruct((t, 1024), F32),
            jax.ShapeDtypeStruct((t, 2048), F32), jax.ShapeDtypeStruct((t, groups * HEAD), F32),
            jax.ShapeDtypeStruct(dproj.shape, dproj.dtype), jax.ShapeDtypeStruct((8, HEAD), F32)]
    dgz_tile = pl.BlockSpec((rows, HPS * HEAD), lambda c, g: (nb - 1 - c, off // (HPS * HEAD) + g))
    return pl.pallas_call(
        body, name="gdn_bwd", grid=(nb, groups),
        in_specs=[qk_tile, qk_tile, v_tile, gab_tile, _view_tile(gz, rows, HPS * HEAD, lambda c: nb - 1 - c),
                  row128, row128, row128, v_tile,
                  pl.BlockSpec((cb, HPS, HEAD, HEAD), lambda c, g: (nb - 1 - c, g, 0, 0)),
                  pl.BlockSpec((cb, HPS, CHUNK, CHUNK), lambda c, g: (nb - 1 - c, g, 0, 0)), v_tile,
                  pl.BlockSpec(lmat.shape, lambda c, g: (0, 0)),
                  pl.BlockSpec(lmat_t.shape, lambda c, g: (0, 0)),
                  pl.BlockSpec(masks.shape, lambda c, g: (0, 0, 0)), _ANY],
        out_specs=[qk_tile, qk_tile, v_tile, pl.BlockSpec((rows, HEAD), lambda c, g: (nb - 1 - c, g)), dgz_tile,
                   pl.BlockSpec((8, HEAD), lambda c, g: (0, 0))],
        out_shape=outs, scratch_shapes=[pltpu.VMEM((GD_HEADS, HEAD, HEAD), F32)], input_output_aliases={15: 4},
        compiler_params=_params(_ARB, _ARB))(qn, kn, cv, gab, gz[0], alog, dtb, gain, oraw, ssave, tsave, dog,
                                             lmat, lmat_t, masks, dproj)


def _fold_groups(wide):
    t, width = wide.shape
    tr = _pick(t, 512, 8)

    def body(w_ref, o_ref):
        acc = w_ref[:, 0:HEAD]
        for j in range(1, width // HEAD):
            acc = acc + w_ref[:, j * HEAD:(j + 1) * HEAD]
        o_ref[...] = acc.astype(BF16)

    return pl.pallas_call(
        body, name="fold_gate_grads", grid=(t // tr,), in_specs=[_row_spec(tr, width)], out_specs=_row_spec(tr, HEAD),
        out_shape=jax.ShapeDtypeStruct((t, HEAD), BF16), compiler_params=_params(_PAR))(wide)


def _adam_math(w, g, m, v):
    m2 = ADAM_B1 * m + (1.0 - ADAM_B1) * g
    v2 = ADAM_B2 * v + (1.0 - ADAM_B2) * (g * g)
    m_hat = m2 / (1.0 - ADAM_B1 ** ADAM_STEP)
    v_hat = v2 / (1.0 - ADAM_B2 ** ADAM_STEP)
    delta = -ADAM_LR * (m_hat / (jnp.sqrt(v_hat) + ADAM_EPS) + ADAM_WD * w)
    return delta, m2, v2


def _adamw(w, g, m, v, name, after=None):
    r, c = w.shape
    tr = r
    for cand in range(8, r + 1, 8):
        if r % cand == 0 and cand * c * 4 <= (1 << 20):
            tr = cand
    if r % 8 != 0:
        tr = r

    def body(w_ref, g_ref, m_ref, v_ref, *rest):
        d_ref, m2_ref, v2_ref = rest[-3:]
        d, m2, v2 = _adam_math(w_ref[...], g_ref[...], m_ref[...], v_ref[...])
        d_ref[...] = d
        m2_ref[...] = m2
        v2_ref[...] = v2

    spec = pl.BlockSpec((tr, c), lambda i: (i, 0))
    extra = [] if after is None else [after]
    return pl.pallas_call(
        body, name=name, grid=(r // tr,), in_specs=[spec] * 4 + [_ANY] * len(extra), out_specs=[spec] * 3,
        out_shape=[jax.ShapeDtypeStruct((r, c), F32)] * 3, compiler_params=_params(_PAR))(w, g, m, v, *extra)


_ANY = pl.BlockSpec(memory_space=pl.ANY)


def _place():
    return lax.axis_index("x"), lax.axis_index("y"), lax.axis_index("c")


def _gather_weights(packs, nchs, name):
    n = len(packs)
    halves = [p.shape[0] // 2 for p in packs]
    base = [sum(nchs[:i]) for i in range(n)]
    total = sum(nchs)
    for p, h, k in zip(packs, halves, nchs):
        assert p.shape[0] == 2 * h and h % k == 0 and (h // k) % 16 == 0

    def body(*refs):
        p_refs, g_refs, (send_sems, recv_sems) = refs[:n], refs[n:2 * n], refs[2 * n:]
        x, y, c = _place()
        sibling = (x, y, 1 - c)
        chips = [(1 - x, y), (x, 1 - y), (1 - x, 1 - y)]
        chunks = [(a, q) for a in range(n) for q in range(nchs[a])]

        def rows_of(a, pc, q):
            ch = halves[a] // nchs[a]
            return pl.ds(pl.multiple_of(pc * halves[a] + q * ch, 16), ch)

        def piece(a, px, py, pc, q):
            return g_refs[a].at[2 * px + py, rows_of(a, pc, q), :]

        def copy(k, src, dst, to):
            return pltpu.make_async_remote_copy(src_ref=src, dst_ref=dst, send_sem=send_sems.at[k],
                                                recv_sem=recv_sems.at[k], device_id=to, device_id_type=MESH)

        def sem_of(j, a, q):
            return j * total + base[a] + q

        first = {(j, a, q): copy(sem_of(j, a, q), p_refs[a].at[rows_of(a, c, q), :], piece(a, x, y, c, q), (*chip, c))
                 for j, chip in enumerate(chips) for a, q in chunks}
        for a, q in chunks:
            for j in range(3):
                first[j, a, q].start()
        passed = {(j, a, q): copy(sem_of(3 + j, a, q), piece(a, *chip, c, q), piece(a, *chip, c, q), sibling)
                  for j, chip in enumerate(chips) for a, q in chunks}
        for a, q in chunks:
            for j, chip in enumerate(chips):
                copy(sem_of(j, a, q), p_refs[a].at[rows_of(a, c, q), :], piece(a, *chip, c, q), (*chip, c)).wait_recv()
                passed[j, a, q].start()
        for a, q in chunks:
            for j, chip in enumerate(chips):
                copy(sem_of(3 + j, a, q), piece(a, *chip, 1 - c, q), piece(a, *chip, 1 - c, q), sibling).wait_recv()
        for key in first:
            first[key].wait_send()
            passed[key].wait_send()

    return pl.pallas_call(
        body, name=name, out_shape=[jax.ShapeDtypeStruct((4,) + p.shape, p.dtype) for p in packs],
        in_specs=[_ANY] * n, out_specs=[_ANY] * n,
        scratch_shapes=[pltpu.SemaphoreType.DMA((6 * total,)), pltpu.SemaphoreType.DMA((6 * total,))])(*packs)


def _swap_with_sibling(arrs, nchs, lead, name, halves=False):
    n = len(arrs)
    jobs = []
    hs = [arr.shape[-2] // (2 if halves else 1) for arr in arrs]
    for a, (h, k) in enumerate(zip(hs, nchs)):
        assert h % k == 0 and (h // k) % 16 == 0
        for s in (range(lead) if lead else [None]):
            jobs += [(a, s, q * (h // k), h // k) for q in range(k)]

    def body(*refs):
        src, dst, (send_sems, recv_sems) = refs[:n], refs[n:2 * n], refs[2 * n:]
        x, y, c = _place()

        def at(ref, s, r0, rows):
            return ref.at[pl.ds(r0, rows), :] if s is None else ref.at[s, pl.ds(r0, rows), :]

        def src_rows(a, r0):
            return pl.multiple_of((1 - c) * hs[a] + r0, 16) if halves else r0

        copies = [pltpu.make_async_remote_copy(
            src_ref=at(src[a], s, src_rows(a, r0), rows), dst_ref=at(dst[a], s, r0, rows), send_sem=send_sems.at[k],
            recv_sem=recv_sems.at[k], device_id=(x, y, 1 - c), device_id_type=MESH)
            for k, (a, s, r0, rows) in enumerate(jobs)]
        for cp in copies:
            cp.start()
        for cp in copies:
            cp.wait()

    shapes = [jax.ShapeDtypeStruct(arr.shape[:-2] + (h, arr.shape[-1]), arr.dtype) for arr, h in zip(arrs, hs)]
    return pl.pallas_call(
        body, name=name, out_shape=shapes, in_specs=[_ANY] * n, out_specs=[_ANY] * n,
        scratch_shapes=[pltpu.SemaphoreType.DMA((len(jobs),)), pltpu.SemaphoreType.DMA((len(jobs),))])(*arrs)


def _add2(full, b, core, name):
    n, rows, w = b.shape
    tr = _pick(rows, 256, 16)
    nblk = rows // tr

    def body(c_ref, a_ref, b_ref, o_ref):
        o_ref[...] = (a_ref[...].astype(F32) + b_ref[...].astype(F32)).astype(BF16)

    spec = pl.BlockSpec((1, tr, w), lambda i, j, c_ref: (i, j, 0))
    grid_spec = pltpu.PrefetchScalarGridSpec(
        num_scalar_prefetch=1, grid=(n, nblk),
        in_specs=[pl.BlockSpec((1, tr, w), lambda i, j, c_ref: (i, c_ref[0] * nblk + j, 0)), spec], out_specs=spec)
    return pl.pallas_call(
        body, name=name, grid_spec=grid_spec, out_shape=jax.ShapeDtypeStruct(b.shape, BF16),
        compiler_params=_params(_PAR, _PAR))(core, full, b)


def _reduce_chips(partials, nchs, name):
    n = len(partials)
    jobs = []
    for a, (arr, k) in enumerate(zip(partials, nchs)):
        h = arr.shape[1]
        assert h % k == 0 and (h // k) % 16 == 0
        jobs += [(a, q * (h // k), h // k) for q in range(k)]

    def body(*refs):
        src, dst, (send_sems, recv_sems) = refs[:n], refs[n:2 * n], refs[2 * n:]
        x, y, c = _place()
        chips = [(1 - x, y), (x, 1 - y), (1 - x, 1 - y)]
        copies = [pltpu.make_async_remote_copy(
            src_ref=src[a].at[2 * px + py, pl.ds(r0, rows), :], dst_ref=dst[a].at[j, pl.ds(r0, rows), :],
            send_sem=send_sems.at[3 * k + j], recv_sem=recv_sems.at[3 * k + j],
            device_id=(px, py, c), device_id_type=MESH)
            for k, (a, r0, rows) in enumerate(jobs) for j, (px, py) in enumerate(chips)]
        for cp in copies:
            cp.start()
        for cp in copies:
            cp.wait()

    return pl.pallas_call(
        body, name=name,
        out_shape=[jax.ShapeDtypeStruct((3,) + p.shape[1:], p.dtype) for p in partials],
        in_specs=[_ANY] * n, out_specs=[_ANY] * n,
        scratch_shapes=[pltpu.SemaphoreType.DMA((3 * len(jobs),)), pltpu.SemaphoreType.DMA((3 * len(jobs),))])(*partials)


_HBM = pl.BlockSpec(memory_space=pltpu.HBM)
_SEM = pl.BlockSpec(memory_space=pltpu.SEMAPHORE)
_DATAFLOW = pltpu.SideEffectType.DATAFLOW_SIDE_EFFECTING


def _ici_jobs(srcs, nchs, kind):
    jobs = []
    for a, (arr, k) in enumerate(zip(srcs, nchs)):
        h = arr.shape[0] // 2 if kind == "gather" else arr.shape[1]
        assert h % k == 0 and (h // k) % 16 == 0
        jobs += [(a, h, q * (h // k), h // k) for q in range(k)]
    return jobs


def _ici_copies(src, land, send_sems, recv_sems, jobs, kind):
    x, y, c = _place()
    chips = [(1 - x, y), (x, 1 - y), (1 - x, 1 - y)]
    copies = []
    for k, (a, h, r0, rows) in enumerate(jobs):
        for j, (px, py) in enumerate(chips):
            if kind == "gather":
                at = pl.ds(pl.multiple_of(c * h + r0, 16), rows)
                s, d = src[a].at[at, :], land[a].at[2 * x + y, at, :]
            else:
                s, d = src[a].at[2 * px + py, pl.ds(r0, rows), :], land[a].at[j, pl.ds(r0, rows), :]
            copies.append(pltpu.make_async_remote_copy(
                src_ref=s, dst_ref=d, send_sem=send_sems.at[3 * k + j], recv_sem=recv_sems.at[3 * k + j],
                device_id=(px, py, c), device_id_type=MESH))
    return copies


def _ici_start(srcs, nchs, kind, name):
    n = len(srcs)
    jobs = _ici_jobs(srcs, nchs, kind)
    lead = (lambda s: (4,) + s.shape) if kind == "gather" else (lambda s: (3,) + s.shape[1:])
    lands = [lax.empty(lead(s), s.dtype) for s in srcs]

    def body(*refs):
        src, land = refs[:n], refs[n:2 * n]
        send_sems, recv_sems, token = refs[2 * n], refs[2 * n + 1], refs[-1]
        for cp in _ici_copies(src, land, send_sems, recv_sems, jobs, kind):
            cp.start()
        token[...] = jnp.zeros_like(token)

    hbm = [pltpu.HBM(a.shape, a.dtype) for a in srcs + lands]
    outs = pl.pallas_call(
        body, name=name,
        out_shape=[pltpu.SemaphoreType.DMA((3 * len(jobs),)), pltpu.SemaphoreType.DMA((3 * len(jobs),))] + hbm
        + [jax.ShapeDtypeStruct((8, 128), F32)],
        in_specs=[_HBM] * (2 * n), out_specs=[_SEM, _SEM] + [_HBM] * (2 * n) + [pl.BlockSpec(memory_space=pltpu.VMEM)],
        input_output_aliases={i: 2 + i for i in range(2 * n)},
        compiler_params=pltpu.CompilerParams(has_side_effects=_DATAFLOW),
    )(*[pltpu.with_memory_space_constraint(a, pltpu.HBM) for a in srcs + lands])
    return (outs[0], outs[1], list(outs[2:2 + n]), list(outs[2 + n:2 + 2 * n]), nchs, kind), outs[-1]


def _ici_wait(handle, after, name):
    send_sems, recv_sems, srcs, lands, nchs, kind = handle
    n = len(srcs)
    jobs = _ici_jobs(srcs, nchs, kind)

    def body(*refs):
        src, land = refs[:n], refs[n:2 * n]
        for cp in _ici_copies(src, land, refs[2 * n], refs[2 * n + 1], jobs, kind):
            cp.wait_send()
            cp.wait_recv()

    outs = pl.pallas_call(
        body, name=name, out_shape=[pltpu.HBM(a.shape, a.dtype) for a in srcs + lands],
        in_specs=[_HBM] * (2 * n) + [_SEM, _SEM, _ANY], out_specs=[_HBM] * (2 * n),
        input_output_aliases={i: i for i in range(2 * n)},
        compiler_params=pltpu.CompilerParams(has_side_effects=_DATAFLOW),
    )(*srcs, *lands, send_sems, recv_sems, after)
    return list(outs[:n]), list(outs[n:])


def _pass_to_sibling(gathered, nchs, name):
    n = len(gathered)
    jobs = _ici_jobs([jax.ShapeDtypeStruct(g.shape[1:], g.dtype) for g in gathered], nchs, "gather")

    def body(*refs):
        src, dst, (send_sems, recv_sems) = refs[:n], refs[n:2 * n], refs[2 * n:]
        x, y, c = _place()
        slots = [2 * (1 - x) + y, 2 * x + (1 - y), 2 * (1 - x) + (1 - y)]

        def copy(k, j, pc):
            a, h, r0, rows = jobs[k]
            at = pl.ds(pl.multiple_of(pc * h + r0, 16), rows)
            return pltpu.make_async_remote_copy(
                src_ref=src[a].at[slots[j], at, :], dst_ref=dst[a].at[slots[j], at, :], send_sem=send_sems.at[3 * k + j],
                recv_sem=recv_sems.at[3 * k + j], device_id=(x, y, 1 - c), device_id_type=MESH)

        pairs = [(k, j) for k in range(len(jobs)) for j in range(3)]
        for k, j in pairs:
            copy(k, j, c).start()
        for k, j in pairs:
            copy(k, j, c).wait_send()
            copy(k, j, 1 - c).wait_recv()

    return pl.pallas_call(
        body, name=name, out_shape=[jax.ShapeDtypeStruct(g.shape, g.dtype) for g in gathered],
        in_specs=[_ANY] * n, out_specs=[_ANY] * n, input_output_aliases={i: i for i in range(n)},
        scratch_shapes=[pltpu.SemaphoreType.DMA((3 * len(jobs),)), pltpu.SemaphoreType.DMA((3 * len(jobs),))])(*gathered)


def _add4(own, got, name):
    rows, w = own.shape
    tr = _pick(rows, 128, 16)

    def body(a_ref, b_ref, o_ref):
        o_ref[...] = ((a_ref[...].astype(F32) + b_ref[0].astype(F32)) + b_ref[1].astype(F32)) + b_ref[2].astype(F32)

    return pl.pallas_call(
        body, name=name, grid=(rows // tr,),
        in_specs=[pl.BlockSpec((tr, w), lambda i: (i, 0)), pl.BlockSpec((3, tr, w), lambda i: (0, i, 0))],
        out_specs=pl.BlockSpec((tr, w), lambda i: (i, 0)), out_shape=jax.ShapeDtypeStruct((rows, w), F32),
        compiler_params=_params(_PAR))(own, got)


def _small_sync(gs, ws, ms, vs):
    rows = gs.shape[0]
    vmem = pl.BlockSpec(memory_space=pltpu.VMEM)

    def body(g_ref, w_ref, m_ref, v_ref, sum_ref, d_ref, m2_ref, v2_ref, buf, send_sems, recv_sems):
        x, y, c = _place()
        me = 4 * x + 2 * y + c
        buf[me] = g_ref[...]
        copies = []
        for k in range(1, 8):
            peer = (x ^ (k >> 2), y ^ ((k >> 1) & 1), c ^ (k & 1))
            copies.append(pltpu.make_async_remote_copy(
                src_ref=g_ref, dst_ref=buf.at[me], send_sem=send_sems.at[k - 1], recv_sem=recv_sems.at[k - 1],
                device_id=peer, device_id_type=MESH))
        for cp in copies:
            cp.start()
        for cp in copies:
            cp.wait()
        total = buf[0]
        for i in range(1, 8):
            total = total + buf[i]
        sum_ref[...] = total
        d, m2, v2 = _adam_math(w_ref[...], total, m_ref[...], v_ref[...])
        d_ref[...] = d
        m2_ref[...] = m2
        v2_ref[...] = v2

    shape = jax.ShapeDtypeStruct((rows, 128), F32)
    return pl.pallas_call(
        body, name="small_sync", out_shape=[shape] * 4, in_specs=[vmem] * 4, out_specs=[vmem] * 4,
        scratch_shapes=[pltpu.VMEM((8, rows, 128), F32), pltpu.SemaphoreType.DMA((7,)),
                        pltpu.SemaphoreType.DMA((7,))])(gs, ws, ms, vs)


_GROUPS = {
    "ffn1": dict(cols=("ffn1_w_in", 1408), rows=(("ffn1_w_out", 704, 704),), chunks=(8, 2)),
    "ffn2": dict(cols=("ffn2_w_in", 1408), rows=(("ffn2_w_out", 704, 704),), chunks=(8, 2)),
    "mixer": dict(cols=("w_in", 3080), chunks=(8, 4),
                  rows=(("w_branch_hgrn", 256, 256), ("w_branch_gdn", 512, 512), ("w_out", 256, 256),
                        ("gdn_conv_w", CONV_K, 128))),
}
_BIG_NAMES = tuple(n for g in _GROUPS.values() for n in (g["cols"][0],) + tuple(r[0] for r in g["rows"]))


def _group_names(group):
    return (group["cols"][0],) + tuple(r[0] for r in group["rows"])


def _pack(parts, lead, group):
    ax = len(lead)
    rows = []
    for n, r, padded in group["rows"]:
        p = parts[n]
        if padded != r:
            p = jnp.tile(p, (1,) * ax + (padded // r, 1))
        rows.append(p)
    return [parts[group["cols"][0]], rows[0] if len(rows) == 1 else jnp.concatenate(rows, axis=ax)]


def _unpack(cols, rows, group):
    out, off = {group["cols"][0]: cols}, 0
    for n, r, padded in group["rows"]:
        out[n] = rows[..., off:off + r, :]
        off += padded
    return out


def _is_col_sharded(name):
    return name in ("ffn1_w_in", "ffn2_w_in", "w_in", "gdn_conv_w")


def _full_from_shards(name, g):
    if _is_col_sharded(name):
        return jnp.transpose(g, (1, 0, 2)).reshape(g.shape[1], -1)
    return g.reshape(-1, g.shape[2])


def _shards_from_full(name, full):
    if _is_col_sharded(name):
        return jnp.transpose(full.reshape(full.shape[0], 4, -1), (1, 0, 2))
    return full.reshape(4, -1, full.shape[1])


_SMALL = (("ffn1_norm", 8), ("mix_norm", 8), ("hgrn_lb_logits", 16), ("hgrn_out_norm", 8), ("gdn_a_log", 8),
          ("gdn_dt_bias", 8), ("gdn_out_norm", 8), ("ffn2_norm", 8), ("final_norm", 8), ("loss", 8))
_SMALL_ROWS = sum(r for _, r in _SMALL)


def _pack_small(parts):
    out = []
    for name, rows in _SMALL:
        p = parts[name].reshape(-1).astype(F32)
        if p.shape[0] <= 128:
            if p.shape[0] < 128:
                p = jnp.concatenate([p, jnp.zeros((128 - p.shape[0],), F32)])
            p = jnp.broadcast_to(p.reshape(1, 128), (rows, 128))
        out.append(p.reshape(rows, 128))
    return jnp.concatenate(out, axis=0)


def _unpack_small(packed, shapes):
    out, off = {}, 0
    for name, rows in _SMALL:
        n = int(np.prod(shapes[name]))
        out[name] = packed[off:off + rows].reshape(-1)[:n].reshape(shapes[name])
        off += rows
    return out


def _ffn_fwd(x, gain, w_in, w_out, tag):
    n = _rmsnorm_fwd(x, gain, tag + "_norm")
    a, b, hm = _ffn_in_act(n, w_in, tag + "_in")
    out = _mm(hm, w_out, alpha=0.5, res=x, name=tag + "_out")
    return out, (n, a, b)


def _ffn_bwd(x, gain, w_in, w_out, saved, dout, dout_bf, tag):
    n, a, b = saved
    da, db, hm = _ffn_dact(dout_bf, w_out, a, b, tag + "_dact")
    dw_out = _mm(hm, dout_bf, ta=True, alpha=0.5, out_dtype=BF16, name=tag + "_dwout")
    dwa = _mm(n, da, ta=True, out_dtype=BF16, name=tag + "_dwin_a")
    dwb = _mm(n, db, ta=True, out_dtype=BF16, name=tag + "_dwin_b")
    half = D_FF // 2
    dw_in = jnp.stack([dwa[:, :half], dwa[:, half:], dwb[:, :half], dwb[:, half:]])
    dn = _mm(da, w_in, tb=True, name=tag + "_dnorm_a")
    dn = _mm(db, w_in, tb=True, res=dn, b_from=D_FF, name=tag + "_dnorm_b")
    dx, dx_bf, dgain = _rmsnorm_bwd(x, gain, dn, dout, tag + "_dx")
    return dx, dx_bf, dgain, dw_in, dw_out


def _pad_lanes(v):
    return jnp.concatenate([v.reshape(1, -1), jnp.zeros((1, HEAD - v.size), F32)], axis=1)


def _local_step(x, tgt, small, exchange):
    hg_c = _hg_consts()
    gd_c = _gd_consts()
    alog = _pad_lanes(small["gdn_a_log"])
    dtb = _pad_lanes(small["gdn_dt_bias"])
    logits = small["hgrn_lb_logits"]
    hg_gain = small["hgrn_out_norm"].reshape(1, HEAD)
    gd_gain = small["gdn_out_norm"].reshape(1, HEAD)
    g1, gm, g2 = small["ffn1_norm"].reshape(1, -1), small["mix_norm"].reshape(1, -1), small["ffn2_norm"].reshape(1, -1)
    gf = small["final_norm"].reshape(1, -1)
    qscale = HEAD ** -0.5

    w1 = exchange.weights("ffn1")
    started = exchange.prefetch("mixer")
    h1, ffn1_saved = _ffn_fwd(x, g1 + started, w1["ffn1_w_in"], w1["ffn1_w_out"], "ffn1")
    u = _rmsnorm_fwd(h1, gm, "mix_norm")
    w = exchange.weights("mixer", after=u)
    started = exchange.prefetch("ffn2")
    seg, off = {}, 0
    for name, size in zip(IN_NAMES, IN_SIZES):
        seg[name] = w["w_in"][:, off:off + size]
        off += size
    w_gab = jnp.concatenate([seg["ga"], seg["gb"], jnp.zeros((D_MODEL, HEAD - 32), BF16)], axis=1)
    big_segs = [n for n in IN_NAMES if n not in ("ga", "gb")]
    conv8 = jnp.concatenate([w["gdn_conv_w"].astype(F32), jnp.zeros((8 - CONV_K, 4096), F32)], axis=0)
    conv_q, conv_k, conv_v = conv8[:, :1024], conv8[:, 1024:2048], conv8[:, 2048:]
    w_main = jnp.concatenate([seg[n] for n in big_segs], axis=1)
    proj = _mm(u, w_main, name="proj")
    pr, off = {}, 0
    for n in big_segs:
        pr[n] = _view(proj, off, seg[n].shape[1])
        off += seg[n].shape[1]
    gab = _mm(u, w_gab, name="proj_gab")
    oh_raw, oh, s_h = _hgrn_fwd(pr["hq"], pr["hf"], pr["hi"], pr["hg"], logits, hg_gain + started, hg_c)
    qn = _conv_fwd(pr["gq"], conv_q, qscale, "conv_q")
    kn = _conv_fwd(pr["gk"], conv_k, 1.0, "conv_k")
    cv = _conv_fwd(pr["gv"], conv_v, None, "conv_v")
    og_raw, og, s_g, t_g = _gdn_fwd(qn, kn, cv, gab, pr["gz"], alog, dtb, gd_gain, gd_c)
    yh = _mm(oh, w["w_branch_hgrn"], name="branch_h")
    yg = _mm(og, w["w_branch_gdn"], name="branch_g")
    ym = _merge_fwd(yh, yg, pr["gate_h"], pr["gate_g"])
    h2 = _mm(ym, w["w_out"], res=h1, name="mix_out")
    w2 = exchange.weights("ffn2", after=h2)
    h3, ffn2_saved = _ffn_fwd(h2, g2, w2["ffn2_w_in"], w2["ffn2_w_out"], "ffn2")
    loss, dh3, dh3_bf, d_gf = _final_loss(h3, gf, tgt)

    dh2, dh2_bf, d_g2, d_f2in, d_f2out = _ffn_bwd(h2, g2, w2["ffn2_w_in"], w2["ffn2_w_out"], ffn2_saved, dh3, dh3_bf,
                                                  "ffn2")
    started = exchange.reduce("ffn2", {"ffn2_w_in": d_f2in, "ffn2_w_out": d_f2out}, behind=True)
    dym =_mm(dh2_bf, w["w_out"], tb=True, name="d_merge")
    d_wout = _mm(ym, dh2_bf, ta=True, out_dtype=BF16, name="d_w_out")
    dproj = lax.empty((x.shape[0], w_main.shape[1]), BF16)
    dyh, dyg, dproj = _merge_bwd(dym, yh, yg, pr["gate_h"], pr["gate_g"], _into(dproj, pr["gate_h"][1], 2 * D_MODEL))
    d_wbh = _mm(oh, dyh, ta=True, out_dtype=BF16, name="d_w_branch_h")
    d_wbg = _mm(og, dyg, ta=True, out_dtype=BF16, name="d_w_branch_g")
    doh = _mm(dyh, w["w_branch_hgrn"], tb=True, name="d_oh")
    dog = _mm(dyg, w["w_branch_gdn"], tb=True, name="d_og")
    dproj, d_hg_gain, d_lb0 = _hgrn_bwd(pr["hq"], pr["hf"], pr["hi"], pr["hg"], logits, hg_gain + started, oh_raw,
                                        s_h, doh, hg_c, _into(dproj, pr["hq"][1], 4 * D_MODEL))
    d_qn, d_kn, d_cv, d_gab_wide, dproj, gd_small = _gdn_bwd(qn, kn, cv, gab, pr["gz"], alog, dtb, gd_gain, og_raw,
                                                             s_g, t_g, dog, gd_c, _into(dproj, *pr["gz"][1:]))
    d_gab = _fold_groups(d_gab_wide)
    dc_q, dwc_q = _conv_bwd_a(pr["gq"], conv_q, d_qn, qscale, "dconv_q")
    dc_k, dwc_k = _conv_bwd_a(pr["gk"], conv_k, d_kn, 1.0, "dconv_k")
    dc_v, dwc_v = _conv_bwd_a(pr["gv"], conv_v, d_cv, None, "dconv_v")
    dproj = _conv_bwd_b(dc_q, conv_q, "dconvx_q", _into(dproj, *pr["gq"][1:]))
    dproj = _conv_bwd_b(dc_k, conv_k, "dconvx_k", _into(dproj, *pr["gk"][1:]))
    dproj = _conv_bwd_b(dc_v, conv_v, "dconvx_v", _into(dproj, *pr["gv"][1:]))
    du =_mm(d_gab, w_gab, tb=True, name="du_gab")
    du = _mm(dproj, w_main, tb=True, res=du, name="du")
    d_wmain = _mm(u, dproj, ta=True, out_dtype=BF16, name="dw_main")
    d_wgab = _mm(u, d_gab, ta=True, out_dtype=BF16, name="dw_gab")
    cut = IN_WIDTH // 4
    d_win = jnp.stack([d_wmain[:, :cut], d_wmain[:, cut:2 * cut],
                       jnp.concatenate([d_wmain[:, 2 * cut:8192], d_wgab[:, :32], d_wmain[:, 8192:3 * cut - 32]], axis=1),
                       d_wmain[:, 3 * cut - 32:]])
    d_conv = jnp.concatenate([dwc_q[:CONV_K], dwc_k[:CONV_K], dwc_v[:CONV_K]], axis=1).astype(BF16)
    started = exchange.reduce("mixer", {"w_in": d_win, "gdn_conv_w": d_conv, "w_branch_hgrn": d_wbh,
                                        "w_branch_gdn": d_wbg, "w_out": d_wout}, behind=True)
    dh1, dh1_bf, d_gm = _rmsnorm_bwd(h1, gm + started, du, dh2, "mix_dnorm")
    dx, _, d_g1, d_f1in, d_f1out = _ffn_bwd(x, g1, w1["ffn1_w_in"], w1["ffn1_w_out"], ffn1_saved, dh1, dh1_bf, "ffn1")
    exchange.reduce("ffn1", {"ffn1_w_in": d_f1in, "ffn1_w_out": d_f1out}, behind=True)
    d_lb0 = d_lb0.reshape(1, -1)
    sm = {"ffn1_norm": d_g1, "mix_norm": d_gm, "hgrn_lb_logits": jnp.concatenate([d_lb0, -d_lb0], axis=0),
          "hgrn_out_norm": d_hg_gain, "gdn_a_log": gd_small[2, :16], "gdn_dt_bias": gd_small[1, :16],
          "gdn_out_norm": gd_small[0], "ffn2_norm": d_g2, "final_norm": d_gf, "loss": loss[0, :1]}
    return dx, sm


class _Exchange:
    def __init__(self, wts):
        self.wts = wts
        xi, yi, ci = _place()
        self.chip = 2 * xi + yi
        self.south = ci == 0
        self.core = ci.reshape(1).astype(jnp.int32)
        self.mine = {}
        self.coming = {}
        self.going = {}

    def _packs(self, tag):
        group = _GROUPS[tag]
        return _pack({n: self.wts[n][0].astype(BF16) for n in _group_names(group)}, (), group)

    def prefetch(self, tag):
        packs = self._packs(tag)
        handle, token = _ici_start(packs, _GROUPS[tag]["chunks"], "gather", "gather_start_" + tag)
        self.coming[tag] = handle
        return token[0:1, 0:1]

    def weights(self, tag, after=None):
        group = _GROUPS[tag]
        if tag in self.coming:
            packs, halves = _ici_wait(self.coming.pop(tag), after, "gather_wait_" + tag)
            others = _pass_to_sibling(halves, group["chunks"], "gather_pass_" + tag)
        else:
            packs = self._packs(tag)
            others = _gather_weights(packs, group["chunks"], "gather_" + tag)
        whole = [lax.dynamic_update_index_in_dim(g, p, self.chip, 0) for g, p in zip(others, packs)]
        gathered = _unpack(*whole, group)
        return {n: _full_from_shards(n, gathered[n]) for n in _group_names(group)}

    def reduce(self, tag, grads, behind=False):
        group = _GROUPS[tag]
        shards = {n: (grads[n] if grads[n].ndim == 3 else _shards_from_full(n, grads[n])) for n in _group_names(group)}
        gpacks = _pack(shards, (4,), group)
        got = _swap_with_sibling(gpacks, group["chunks"], 4, "reduce_pair_" + tag, halves=True)
        sums = [_add2(a, b, self.core, "add_pair_%s_%d" % (tag, i)) for i, (a, b) in enumerate(zip(gpacks, got))]
        if behind:
            handle, token = _ici_start(sums, group["chunks"], "reduce", "reduce_start_" + tag)
            self.going[tag] = handle
            self.token = token
            return token[0:1, 0:1]
        self._add_chips(tag, sums, _reduce_chips(sums, group["chunks"], "reduce_chips_" + tag))
        return None

    def _add_chips(self, tag, sums, from_chips):
        self.mine[tag] = [_add4(lax.dynamic_index_in_dim(s, self.chip, axis=0, keepdims=False), f,
                                "add_chips_%s_%d" % (tag, i)) for i, (s, f) in enumerate(zip(sums, from_chips))]

    def finish(self, tags, after):
        for tag in tags:
            if tag in self.going:
                self._add_chips(tag, *_ici_wait(self.going.pop(tag), after, "reduce_wait_" + tag))
        mine = [a for t in tags for a in self.mine[t]]
        nchs = [k for t in tags for k in _GROUPS[t]["chunks"]]
        theirs = _swap_with_sibling(mine, nchs, 0, "share_pair_" + tags[0])
        whole = [jnp.concatenate([jnp.where(self.south, a, b), jnp.where(self.south, b, a)], axis=0)
                 for a, b in zip(mine, theirs)]
        reduced = {}
        for i, t in enumerate(tags):
            reduced.update(_unpack(whole[2 * i], whole[2 * i + 1], _GROUPS[t]))
        return reduced


_WEIGHTS = ("ffn1_norm", "ffn1_w_in", "ffn1_w_out", "mix_norm", "w_in", "hgrn_lb_logits", "hgrn_out_norm",
            "gdn_conv_w", "gdn_a_log", "gdn_dt_bias", "gdn_out_norm", "w_branch_hgrn", "w_branch_gdn", "w_out",
            "ffn2_norm", "ffn2_w_in", "ffn2_w_out", "final_norm")


def kernel(x, ffn1_norm, ffn1_w_in, ffn1_w_out, mix_norm, w_in, hgrn_lb_logits, hgrn_out_norm, gdn_conv_w, gdn_a_log, gdn_dt_bias, gdn_out_norm, w_branch_hgrn, w_branch_gdn, w_out, ffn2_norm, ffn2_w_in, ffn2_w_out, final_norm, loss_target, m_ffn1_norm, m_ffn1_w_in, m_ffn1_w_out, m_mix_norm, m_w_in, m_hgrn_lb_logits, m_hgrn_out_norm, m_gdn_conv_w, m_gdn_a_log, m_gdn_dt_bias, m_gdn_out_norm, m_w_branch_hgrn, m_w_branch_gdn, m_w_out, m_ffn2_norm, m_ffn2_w_in, m_ffn2_w_out, m_final_norm, v_ffn1_norm, v_ffn1_w_in, v_ffn1_w_out, v_mix_norm, v_w_in, v_hgrn_lb_logits, v_hgrn_out_norm, v_gdn_conv_w, v_gdn_a_log, v_gdn_dt_bias, v_gdn_out_norm, v_w_branch_hgrn, v_w_branch_gdn, v_w_out, v_ffn2_norm, v_ffn2_w_in, v_ffn2_w_out, v_final_norm):
    args = dict(locals())
    wts = {n: args[n] for n in _WEIGHTS}
    moms = {n: args["m_" + n] for n in _WEIGHTS}
    vars_ = {n: args["v_" + n] for n in _WEIGHTS}

    small = {n: wts[n].astype(F32) for n in _WEIGHTS if n not in _BIG_NAMES}
    exchange = _Exchange(wts)
    dx, small_grads = _local_step(x[0], loss_target[0], small, exchange)

    out_g, out_d, out_m, out_v = {}, {}, {}, {}

    def update(tags, reduced, after):
        for t in tags:
            for n in _group_names(_GROUPS[t]):
                shape = wts[n].shape
                w2 = wts[n].reshape(shape[-2], shape[-1])
                g2 = reduced[n]
                d, m2, v2 = _adamw(w2, g2, moms[n].reshape(w2.shape), vars_[n].reshape(w2.shape), "adamw_" + n, after)
                out_g[n], out_d[n], out_m[n], out_v[n] = (g2.reshape(shape), d.reshape(shape), m2.reshape(shape),
                                                          v2.reshape(shape))
                after = v2
        return after

    done = update(("ffn2", "mixer"), exchange.finish(("ffn2", "mixer"), after=dx), exchange.token)
    update(("ffn1",), exchange.finish(("ffn1",), after=done), None)

    small_names = [n for n, _ in _SMALL]
    zero = jnp.zeros((1,), F32)
    shapes = {n: (wts[n].shape if n != "loss" else (1,)) for n in small_names}
    sums, sd, sm_, sv = _small_sync(
        _pack_small(small_grads),
        _pack_small({n: (wts[n] if n != "loss" else zero) for n in small_names}),
        _pack_small({n: (moms[n] if n != "loss" else zero) for n in small_names}),
        _pack_small({n: (vars_[n] if n != "loss" else zero) for n in small_names}))
    sg_u, sd_u, sm_u, sv_u = (_unpack_small(p, shapes) for p in (sums, sd, sm_, sv))
    for n in small_names:
        if n != "loss":
            out_g[n], out_d[n], out_m[n], out_v[n] = sg_u[n], sd_u[n], sm_u[n], sv_u[n]
    loss = sg_u["loss"].reshape(())

    return (loss, dx[None], *[out_g[n] for n in _WEIGHTS], *[out_d[n] for n in _WEIGHTS],
            *[out_m[n] for n in _WEIGHTS], *[out_v[n] for n in _WEIGHTS])
```

```python
import numpy as np

import jax
import jax.numpy as jnp
from jax import lax
from jax.experimental import pallas as pl
from jax.experimental.pallas import tpu as pltpu

F32 = jnp.float32
BF16 = jnp.bfloat16

D_MODEL = 1024
D_FF = 2816
CHUNK = 64
HEAD = 128
HG_HEADS = 8
GD_HEADS = 16
HPS = 8
MM_TM = 1408
MM_TN = 1024
MM_TK = 1536
VMEM_LIMIT = 48 * 1024 * 1024
EPS = 1e-6
CONV_K = 4
IN_NAMES = ("hq", "hf", "hi", "hg", "gq", "gk", "gv", "ga", "gb", "gz", "gate_h", "gate_g")
IN_SIZES = (1024, 1024, 1024, 1024, 1024, 1024, 2048, 16, 16, 2048, 1024, 1024)
IN_WIDTH = sum(IN_SIZES)

ADAM_LR = 0.001
ADAM_B1 = 0.9
ADAM_B2 = 0.999
ADAM_EPS = 1e-08
ADAM_WD = 0.01
ADAM_STEP = 10

MESH = pl.DeviceIdType.MESH
_ARB = "arbitrary"
_PAR = "parallel"


def _bf(x):
    return x.astype(BF16)


def _dot(a, b):
    return jnp.dot(_bf(a), _bf(b), preferred_element_type=F32)


def _dot_nt(a, b):
    return lax.dot_general(_bf(a), _bf(b), (((1,), (1,)), ((), ())), preferred_element_type=F32)


def _dot_tn(a, b):
    return lax.dot_general(_bf(a), _bf(b), (((0,), (0,)), ((), ())), preferred_element_type=F32)


def _sigmoid(x):
    return jax.nn.sigmoid(x)


def _silu(x):
    return x * _sigmoid(x)


def _dsilu(x):
    s = _sigmoid(x)
    return s * (1.0 + x * (1.0 - s))


def _softplus(x):
    return jnp.maximum(x, 0.0) + jnp.log(1.0 + jnp.exp(-jnp.abs(x)))


def _rowsum(x):
    return jnp.sum(x, axis=1, keepdims=True)


def _col_to_row(col, eye):
    return jnp.sum(eye * col, axis=0, keepdims=True)


def _row_to_col(row, eye):
    return jnp.sum(eye * row, axis=1, keepdims=True)


def _pick(dim, pref, unit=128):
    if dim <= pref:
        return dim
    t = pref
    while t >= unit:
        if dim % t == 0:
            return t
        t -= unit
    return dim


def _params(*sem):
    return pltpu.CompilerParams(dimension_semantics=tuple(sem), vmem_limit_bytes=VMEM_LIMIT)


def _mm(a, b, *, ta=False, tb=False, alpha=1.0, res=None, out_dtype=F32, name="mm", b_from=0):
    m = a.shape[1] if ta else a.shape[0]
    k = a.shape[0] if ta else a.shape[1]
    n = b.shape[0] if tb else b.shape[1]
    assert b_from + k <= (b.shape[1] if tb else b.shape[0])
    tm, tn, tk = _pick(m, MM_TM), _pick(n, MM_TN), _pick(k, MM_TK)
    if tn < MM_TN < n and n % MM_TM == 0:
        tn = MM_TM
    nk = k // tk
    assert b_from % tk == 0
    b0 = b_from // tk
    a_spec = pl.BlockSpec((tk, tm), lambda i, j, l: (l, i)) if ta else pl.BlockSpec((tm, tk), lambda i, j, l: (i, l))
    b_spec = (pl.BlockSpec((tn, tk), lambda i, j, l: (j, b0 + l)) if tb
              else pl.BlockSpec((tk, tn), lambda i, j, l: (b0 + l, j)))
    o_spec = pl.BlockSpec((tm, tn), lambda i, j, l: (i, j))
    dims = (((0 if ta else 1,), (1 if tb else 0,)), ((), ()))
    has_res = res is not None

    def finish(r, r_ref, o_ref):
        if alpha != 1.0:
            r = r * alpha
        if has_res:
            r = r + r_ref[...]
        o_ref[...] = r.astype(out_dtype)

    def body(*refs):
        a_ref, b_ref = refs[0], refs[1]
        r_ref = refs[2] if has_res else None
        o_ref = refs[3] if has_res else refs[2]
        part = lax.dot_general(_bf(a_ref[...]), _bf(b_ref[...]), dims, preferred_element_type=F32)
        if nk == 1:
            finish(part, r_ref, o_ref)
            return
        acc = refs[-1]
        step = pl.program_id(2)

        @pl.when(step == 0)
        def _():
            acc[...] = part

        @pl.when(step != 0)
        def _():
            acc[...] += part

        @pl.when(step == nk - 1)
        def _():
            finish(acc[...], r_ref, o_ref)

    ins = [a, b] + ([res] if has_res else [])
    in_specs = [a_spec, b_spec] + ([o_spec] if has_res else [])
    return pl.pallas_call(
        body, name=name, grid=(m // tm, n // tn, nk), in_specs=in_specs, out_specs=o_spec,
        out_shape=jax.ShapeDtypeStruct((m, n), out_dtype),
        scratch_shapes=[pltpu.VMEM((tm, tn), F32)] if nk > 1 else [],
        compiler_params=_params(_PAR, _PAR, _ARB))(*ins)


def _row_spec(tr, w):
    return pl.BlockSpec((tr, w), lambda i: (i, 0))


def _full_spec(shape):
    return pl.BlockSpec(shape, lambda i: tuple(0 for _ in shape))


def _view(arr, off, width):
    return arr, off, width


def _view_rows(view, tr):
    _, off, width = view
    assert off % width == 0
    return pl.BlockSpec((tr, width), lambda i: (i, off // width))


def _view_tile(view, rows, bw, cidx=lambda c: c):
    _, off, width = view
    assert off % bw == 0 and width % bw == 0
    return pl.BlockSpec((rows, bw), lambda c, g: (cidx(c), off // bw + g))


def _rmsnorm_fwd(x, g, name):
    t, d = x.shape
    tr = _pick(t, 256, 8)

    def body(x_ref, g_ref, o_ref):
        xv = x_ref[...]
        r = lax.rsqrt(jnp.mean(xv * xv, axis=1, keepdims=True) + EPS)
        o_ref[...] = (xv * r * g_ref[...]).astype(BF16)

    return pl.pallas_call(
        body, name=name, grid=(t // tr,), in_specs=[_row_spec(tr, d), _full_spec((1, d))],
        out_specs=_row_spec(tr, d), out_shape=jax.ShapeDtypeStruct((t, d), BF16),
        compiler_params=_params(_PAR))(x, g)


def _rmsnorm_bwd(x, g, dn, res, name):
    t, d = x.shape
    tr = _pick(t, 256, 8)

    def body(x_ref, g_ref, dn_ref, r_ref, dx_ref, dxb_ref, dg_ref):
        @pl.when(pl.program_id(0) == 0)
        def _():
            dg_ref[...] = jnp.zeros_like(dg_ref)

        xv = x_ref[...]
        r = lax.rsqrt(jnp.mean(xv * xv, axis=1, keepdims=True) + EPS)
        xh = xv * r
        dy = dn_ref[...]
        dg_ref[...] += jnp.sum(dy * xh, axis=0, keepdims=True)
        dxh = dy * g_ref[...]
        dx = r_ref[...] + r * (dxh - xh * jnp.mean(dxh * xh, axis=1, keepdims=True))
        dx_ref[...] = dx
        dxb_ref[...] = dx.astype(BF16)

    return pl.pallas_call(
        body, name=name, grid=(t // tr,),
        in_specs=[_row_spec(tr, d), _full_spec((1, d)), _row_spec(tr, d), _row_spec(tr, d)],
        out_specs=[_row_spec(tr, d), _row_spec(tr, d), _full_spec((1, d))],
        out_shape=[jax.ShapeDtypeStruct((t, d), F32), jax.ShapeDtypeStruct((t, d), BF16),
                   jax.ShapeDtypeStruct((1, d), F32)],
        compiler_params=_params(_ARB))(x, g, dn, res)


FFN_TN = 1408
FFN_TM = 512


def _ffn_in_act(n, w_in, name):
    t, d = n.shape
    tm = _pick(t, FFN_TM)
    nf = D_FF // FFN_TN

    def body(n_ref, wa_ref, wb_ref, a_ref, b_ref, hm_ref):
        nv = n_ref[...]
        a = jnp.dot(nv, wa_ref[...], preferred_element_type=F32)
        b = jnp.dot(nv, wb_ref[...], preferred_element_type=F32)
        a_ref[...] = a.astype(BF16)
        b_ref[...] = b.astype(BF16)
        hm_ref[...] = (_silu(a) * b).astype(BF16)

    tile = pl.BlockSpec((tm, FFN_TN), lambda i, j: (i, j))
    return pl.pallas_call(
        body, name=name, grid=(t // tm, nf),
        in_specs=[pl.BlockSpec((tm, d), lambda i, j: (i, 0)), pl.BlockSpec((d, FFN_TN), lambda i, j: (0, j)),
                  pl.BlockSpec((d, FFN_TN), lambda i, j: (0, nf + j))],
        out_specs=[tile, tile, tile], out_shape=[jax.ShapeDtypeStruct((t, D_FF), BF16)] * 3,
        compiler_params=_params(_PAR, _PAR))(n, w_in, w_in)


def _ffn_dact(dout, w_out, a, b, name):
    t, d = dout.shape
    tm = _pick(t, FFN_TM)

    def body(do_ref, w_ref, a_ref, b_ref, da_ref, db_ref, hm_ref):
        dh = 0.5 * _dot_nt(do_ref[...], w_ref[...])
        av = a_ref[...].astype(F32)
        bv = b_ref[...].astype(F32)
        sg = _sigmoid(av)
        sa = av * sg
        da_ref[...] = (dh * bv * (sg * (1.0 + av * (1.0 - sg)))).astype(BF16)
        db_ref[...] = (dh * sa).astype(BF16)
        hm_ref[...] = (sa * bv).astype(BF16)

    tile = pl.BlockSpec((tm, FFN_TN), lambda i, j: (i, j))
    return pl.pallas_call(
        body, name=name, grid=(t // tm, D_FF // FFN_TN),
        in_specs=[pl.BlockSpec((tm, d), lambda i, j: (i, 0)), pl.BlockSpec((FFN_TN, d), lambda i, j: (j, 0)), tile, tile],
        out_specs=[tile, tile, tile], out_shape=[jax.ShapeDtypeStruct((t, D_FF), BF16)] * 3,
        compiler_params=_params(_PAR, _PAR))(dout, w_out, a, b)


def _merge_fwd(yh, yg, gh, gg):
    t, d = yh.shape
    tr = _pick(t, 256, 8)

    def body(yh_ref, yg_ref, gh_ref, gg_ref, o_ref):
        o_ref[...] = (_sigmoid(gh_ref[...]) * yh_ref[...] + _sigmoid(gg_ref[...]) * yg_ref[...]).astype(BF16)

    return pl.pallas_call(
        body, name="merge_fwd", grid=(t // tr,),
        in_specs=[_row_spec(tr, d), _row_spec(tr, d), _view_rows(gh, tr), _view_rows(gg, tr)],
        out_specs=_row_spec(tr, d),
        out_shape=jax.ShapeDtypeStruct((t, d), BF16), compiler_params=_params(_PAR))(yh, yg, gh[0], gg[0])


def _into(dproj, off, width):
    return dproj, off, width


def _merge_bwd(dy, yh, yg, gh, gg, into):
    t, d = yh.shape
    tr = _pick(t, 256, 8)
    dproj, off, width = into
    assert width == 2 * d and off % width == 0

    def body(dy_ref, yh_ref, yg_ref, gh_ref, gg_ref, _, dyh_ref, dyg_ref, dg_ref):
        dyv = dy_ref[...]
        sh = _sigmoid(gh_ref[...])
        sg = _sigmoid(gg_ref[...])
        dyh_ref[...] = (dyv * sh).astype(BF16)
        dyg_ref[...] = (dyv * sg).astype(BF16)
        dg_ref[:, :d] = (dyv * yh_ref[...] * sh * (1.0 - sh)).astype(BF16)
        dg_ref[:, d:] = (dyv * yg_ref[...] * sg * (1.0 - sg)).astype(BF16)

    return pl.pallas_call(
        body, name="merge_bwd", grid=(t // tr,),
        in_specs=[_row_spec(tr, d)] * 3 + [_view_rows(gh, tr), _view_rows(gg, tr), _ANY],
        out_specs=[_row_spec(tr, d)] * 2 + [pl.BlockSpec((tr, width), lambda i: (i, off // width))],
        out_shape=[jax.ShapeDtypeStruct((t, d), BF16)] * 2 + [jax.ShapeDtypeStruct(dproj.shape, dproj.dtype)],
        input_output_aliases={5: 2},
        compiler_params=_params(_PAR))(dy, yh, yg, gh[0], gg[0], dproj)


def _final_loss(h, g, tgt):
    t, d = h.shape
    tr = _pick(t, 256, 8)

    def body(h_ref, g_ref, t_ref, loss_ref, dh_ref, dhb_ref, dg_ref):
        @pl.when(pl.program_id(0) == 0)
        def _():
            dg_ref[...] = jnp.zeros_like(dg_ref)
            loss_ref[...] = jnp.zeros_like(loss_ref)

        xv = h_ref[...]
        gv = g_ref[...]
        r = lax.rsqrt(jnp.mean(xv * xv, axis=1, keepdims=True) + EPS)
        xh = xv * r
        err = xh * gv - t_ref[...]
        loss_ref[...] += 0.5 * jnp.sum(jnp.mean(err * err, axis=1, keepdims=True), axis=0, keepdims=True)
        dy = err * (1.0 / d)
        dg_ref[...] += jnp.sum(dy * xh, axis=0, keepdims=True)
        dxh = dy * gv
        dh = r * (dxh - xh * jnp.mean(dxh * xh, axis=1, keepdims=True))
        dh_ref[...] = dh
        dhb_ref[...] = dh.astype(BF16)

    return pl.pallas_call(
        body, name="final_loss", grid=(t // tr,),
        in_specs=[_row_spec(tr, d), _full_spec((1, d)), _row_spec(tr, d)],
        out_specs=[_full_spec((1, 128)), _row_spec(tr, d), _row_spec(tr, d), _full_spec((1, d))],
        out_shape=[jax.ShapeDtypeStruct((1, 128), F32), jax.ShapeDtypeStruct((t, d), F32),
                   jax.ShapeDtypeStruct((t, d), BF16), jax.ShapeDtypeStruct((1, d), F32)],
        compiler_params=_params(_ARB))(h, g, tgt)


def _hg_consts():
    c = CHUNK
    t = np.arange(c)
    mats, masks = [], []
    for lvl in range(6):
        m = 1 << lvl
        blk = t // m
        mat = np.zeros((c, c), np.float32)
        for tt in range(c):
            b = blk[tt]
            if b % 2 == 1:
                mat[tt, b * m:tt + 1] = 1.0
            else:
                mat[tt, tt + 1:(b + 1) * m] = 1.0
        mats.append(mat)
        same = (t[:, None] // (2 * m)) == (t[None, :] // (2 * m))
        masks.append((same & (blk[:, None] % 2 == 1) & (blk[None, :] % 2 == 0)).astype(np.float32))
    pre = np.tril(np.ones((c, c), np.float32))
    suf = np.triu(np.ones((c, c), np.float32), 1)
    mstack = np.concatenate(mats + [pre, suf], 0)
    masks.append(np.eye(c, dtype=np.float32))
    return (jnp.asarray(mstack, BF16), jnp.asarray(mstack.T.copy(), BF16), jnp.asarray(np.stack(masks), F32),
            jnp.asarray(np.eye(HEAD, dtype=np.float32)))


def _gd_consts():
    c = CHUNK
    incl = np.tril(np.ones((c, c), np.float32))
    strict = np.tril(np.ones((c, c), np.float32), -1)
    eye = np.eye(c, dtype=np.float32)
    masks = np.stack([incl, strict, eye, incl.T.copy()])
    return jnp.asarray(incl, BF16), jnp.asarray(incl.T.copy(), BF16), jnp.asarray(masks, F32)


def _chunks_per_step(nc):
    for cb in (32 // HPS, 2, 1):
        if nc % cb == 0:
            return cb
    return 1


def _hg_prep(hq, hf, lg):
    lb = _sigmoid(lg[0:1, :] - lg[1:2, :])
    sg = _sigmoid(hf)
    sgn = _sigmoid(-hf)
    f = lb + (1.0 - lb) * sg
    lf = jnp.log(f)
    kk = (1.0 - lb) * sgn
    q = _silu(hq) * (HEAD ** -0.5)
    return lb, sg, sgn, f, lf, kk, q


def _mx_each(m, xs):
    hi, lo = _split2_each(xs)
    prods = [jnp.dot(m, jnp.concatenate([h, l], axis=1), preferred_element_type=F32) for h, l in zip(hi, lo)]
    return [p[:, :HEAD] + p[:, HEAD:] for p in prods]


def _hg_scaled(x, ex):
    xb = [_bf(a) for a in x]
    eb = [_bf(e[:6 * CHUNK]) for e in ex]
    return [[a * e[lvl * CHUNK:(lvl + 1) * CHUNK] for lvl in range(6)] for a, e in zip(xb, eb)]


def _hg_scores(q, kk, qe, ke, mask_ref):
    p = [mask_ref[6] * _rowsum(a * b) for a, b in zip(q, kk)]
    for lvl in range(6):
        d = [_dot_nt(a[lvl], b[lvl]) for a, b in zip(qe, ke)]
        p = [x + mask_ref[lvl] * y for x, y in zip(p, d)]
    return p


def _hgrn_fwd(hq, hf, hi, hg, logits, gain, consts):
    t = hq[0].shape[0]
    nc = t // CHUNK
    cb = _chunks_per_step(nc)
    rows = cb * CHUNK
    mstack, _, masks, eye = consts
    tile = pl.BlockSpec((rows, HPS * HEAD), lambda c, g: (c, g))

    def body(hq_ref, hf_ref, hi_ref, hg_ref, lg_ref, gain_ref, m_ref, mask_ref, eye_ref,
             oraw_ref, og_ref, ssave_ref, state):
        c = pl.program_id(0)
        g = pl.program_id(1)

        @pl.when(c == 0)
        def _():
            for hh in range(HPS):
                state[g * HPS + hh] = jnp.zeros((HEAD, HEAD), F32)

        lg_all = lg_ref[...]
        gain_v = gain_ref[...]

        def one(i, carry):
            sl = pl.ds(pl.multiple_of(i * CHUNK, CHUNK), CHUNK)
            hs = range(HPS)
            heads = [g * HPS + hh for hh in hs]
            ln = [slice(hh * HEAD, (hh + 1) * HEAD) for hh in hs]
            preps = [_hg_prep(hq_ref[sl, s], hf_ref[sl, s], lg_all[:, s]) for s in ln]
            lf, kk, q = [p[4] for p in preps], [p[5] for p in preps], [p[6] for p in preps]
            v = [hi_ref[sl, s] for s in ln]
            ex = [jnp.exp(x) for x in _mx_each(m_ref[...], lf)]
            eb = [e[6 * CHUNK:7 * CHUNK] for e in ex]
            esfx = [e[7 * CHUNK:8 * CHUNK] for e in ex]
            qe, ke = _hg_scaled(q, ex), _hg_scaled(kk, ex)
            p = _hg_scores(q, kk, qe, ke, mask_ref)
            s0 = [state[h] for h in heads]
            o = _each(lambda a, e, s, pp, vv: _dot(a * e, s) + _dot(pp, vv), q, eb, s0, p, v)
            eye_v = eye_ref[...]
            s1 = _each(lambda s, e, kx, ef, vv: s * _row_to_col(e[CHUNK - 1:CHUNK, :], eye_v) + _dot_tn(kx * ef, vv),
                       s0, eb, kk, esfx, v)
            for hh in hs:
                ssave_ref[i, hh] = s0[hh]
                state[heads[hh]] = s1[hh]
                oraw_ref[sl, ln[hh]] = o[hh]
                r = lax.rsqrt(jnp.mean(o[hh] * o[hh], axis=1, keepdims=True) + EPS)
                og_ref[sl, ln[hh]] = (o[hh] * r * gain_v * _silu(hg_ref[sl, ln[hh]])).astype(BF16)
            return carry

        lax.fori_loop(0, cb, one, 0, unroll=4)

    return pl.pallas_call(
        body, name="hgrn_fwd", grid=(nc // cb, HG_HEADS // HPS),
        in_specs=[_view_tile(v, rows, HPS * HEAD) for v in (hq, hf, hi, hg)] + [
                  pl.BlockSpec((2, HPS * HEAD), lambda c, g: (0, g)),
                  pl.BlockSpec((1, HEAD), lambda c, g: (0, 0)),
                  pl.BlockSpec(mstack.shape, lambda c, g: (0, 0)),
                  pl.BlockSpec(masks.shape, lambda c, g: (0, 0, 0)),
                  pl.BlockSpec(eye.shape, lambda c, g: (0, 0))],
        out_specs=[tile, tile, pl.BlockSpec((cb, HPS, HEAD, HEAD), lambda c, g: (c, g, 0, 0))],
        out_shape=[jax.ShapeDtypeStruct((t, HG_HEADS * HEAD), F32), jax.ShapeDtypeStruct((t, HG_HEADS * HEAD), BF16),
                   jax.ShapeDtypeStruct((nc, HG_HEADS, HEAD, HEAD), F32)],
        scratch_shapes=[pltpu.VMEM((HG_HEADS, HEAD, HEAD), F32)],
        compiler_params=_params(_ARB, _ARB))(hq[0], hf[0], hi[0], hg[0], logits, gain, mstack, masks, eye)


def _hgrn_bwd(hq, hf, hi, hg, logits, gain, oraw, ssave, dog, consts, into):
    t = hq[0].shape[0]
    dproj, off, width = into
    seg = HG_HEADS * HEAD
    assert HPS == HG_HEADS and width == 4 * seg and off % width == 0
    nc = t // CHUNK
    cb = _chunks_per_step(nc)
    rows = cb * CHUNK
    nb = nc // cb
    mstack, mstack_t, masks, eye = consts
    tile = pl.BlockSpec((rows, HPS * HEAD), lambda c, g: (nb - 1 - c, g))

    def body(hq_ref, hf_ref, hi_ref, hg_ref, lg_ref, gain_ref, oraw_ref, ssave_ref, dog_ref, m_ref, mt_ref,
             mask_ref, eye_ref, _, d_ref, dgain_ref, dlb_ref, dstate):
        c = pl.program_id(0)
        g = pl.program_id(1)

        @pl.when(c == 0)
        def _():
            for hh in range(HPS):
                dstate[g * HPS + hh] = jnp.zeros((HEAD, HEAD), F32)

        @pl.when((c == 0) & (g == 0))
        def _():
            dgain_ref[...] = jnp.zeros_like(dgain_ref)
            dlb_ref[...] = jnp.zeros_like(dlb_ref)

        lg_all = lg_ref[...]
        gain_v = gain_ref[...]
        eye_v = eye_ref[...]
        last_row = (lax.broadcasted_iota(jnp.int32, (CHUNK, HEAD), 0) == CHUNK - 1).astype(F32)

        def one(j, carry):
            i = cb - 1 - j
            sl = pl.ds(pl.multiple_of(i * CHUNK, CHUNK), CHUNK)
            hs = range(HPS)
            heads = [g * HPS + hh for hh in hs]
            ln = [slice(hh * HEAD, (hh + 1) * HEAD) for hh in hs]
            hqv = [hq_ref[sl, s] for s in ln]
            hgv = [hg_ref[sl, s] for s in ln]
            preps = [_hg_prep(a, hf_ref[sl, s], lg_all[:, s]) for a, s in zip(hqv, ln)]
            lb, sg, sgn, f, lf, kk, q = ([p[n] for p in preps] for n in range(7))
            v = [hi_ref[sl, s] for s in ln]
            ex = [jnp.exp(x) for x in _mx_each(m_ref[...], lf)]
            eb = [e[6 * CHUNK:7 * CHUNK] for e in ex]
            esfx = [e[7 * CHUNK:8 * CHUNK] for e in ex]
            qe, ke = _hg_scaled(q, ex), _hg_scaled(kk, ex)
            p = _hg_scores(q, kk, qe, ke, mask_ref)
            s0 = [ssave_ref[i, hh] for hh in hs]
            ds = [dstate[h] for h in heads]

            o = [oraw_ref[sl, s] for s in ln]
            r = [lax.rsqrt(jnp.mean(x * x, axis=1, keepdims=True) + EPS) for x in o]
            on = _each(lambda x, y: x * y, o, r)
            dg_out = [dog_ref[sl, s] for s in ln]
            sgate = [_silu(x) for x in hgv]
            for hh in hs:
                d_ref[sl, slice(3 * seg + hh * HEAD, 3 * seg + (hh + 1) * HEAD)] =(dg_out[hh] * on[hh] * gain_v * _dsilu(hgv[hh])).astype(BF16)
            dgain_ref[...] += sum(jnp.sum(d * s * n, axis=0, keepdims=True) for d, s, n in zip(dg_out, sgate, on))
            don = _each(lambda d, s: d * s * gain_v, dg_out, sgate)
            do = _each(lambda rr, dn, n: rr * (dn - n * jnp.mean(dn * n, axis=1, keepdims=True)), r, don, on)

            dp = _each(_dot_nt, do, v)
            dv = _each(lambda pp, d, kx, ef, s: _dot_tn(pp, d) + _dot(kx * ef, s), p, do, kk, esfx, ds)
            dqb = _each(_dot_nt, do, s0)
            dkx = _each(_dot_nt, v, ds)
            diag = [_rowsum(mask_ref[6] * x) for x in dp]
            dq = _each(lambda a, e, d, kx: a * e + d * kx, dqb, eb, diag, kk)
            dk = _each(lambda a, e, d, qq: a * e + d * qq, dkx, esfx, diag, q)
            dxs = [[] for _ in hs]
            for lvl in range(6):
                el = [e[lvl * CHUNK:(lvl + 1) * CHUNK] for e in ex]
                gm = [mask_ref[lvl] * x for x in dp]
                gm = [_bf(x) for x in gm]
                a1 = _each(lambda m_, kx: _dot(m_, kx[lvl]), gm, ke)
                a2 = _each(lambda m_, qq: _dot_tn(m_, qq[lvl]), gm, qe)
                dq = _each(lambda x, a, e: x + a * e, dq, a1, el)
                dk = _each(lambda x, a, e: x + a * e, dk, a2, el)
                for hh in hs:
                    dxs[hh].append((a1[hh] * q[hh] + a2[hh] * kk[hh]) * el[hh])
            e_end_row = [e[CHUNK - 1:CHUNK, :] for e in eb]
            ds_new = _each(lambda qq, e, d, er, s: _dot_tn(qq * e, d) + _row_to_col(er, eye_v) * s, q, eb, do, e_end_row, ds)
            for hh in hs:
                dstate[heads[hh]] = ds_new[hh]
                dend_row = _col_to_row(_rowsum(s0[hh] * ds[hh]), eye_v)
                dxs[hh].append(dqb[hh] * q[hh] * eb[hh] + last_row * (e_end_row[hh] * dend_row))
                dxs[hh].append(dkx[hh] * kk[hh] * esfx[hh])
            dlf = _mx_each(mt_ref[...], [jnp.concatenate(x, axis=0) for x in dxs])

            for hh in hs:
                d_ref[sl, slice(2 * seg + hh * HEAD, 2 * seg + (hh + 1) * HEAD)] =dv[hh].astype(BF16)
                d_ref[sl, ln[hh]] =(dq[hh] * (HEAD ** -0.5) * _dsilu(hqv[hh])).astype(BF16)
                df = dlf[hh] / f[hh]
                dsig = (1.0 - lb[hh]) * sg[hh] * sgn[hh]
                d_ref[sl, slice(seg + hh * HEAD, seg + (hh + 1) * HEAD)] =((df - dk[hh]) * dsig).astype(BF16)
                dlb_t = jnp.sum(df * sgn[hh] - dk[hh] * sgn[hh], axis=0, keepdims=True)
                dlb_ref[pl.ds(heads[hh], 1), :] += dlb_t * lb[hh] * (1.0 - lb[hh])
            return carry

        lax.fori_loop(0, cb, one, 0, unroll=2)

    outs = [jax.ShapeDtypeStruct(dproj.shape, dproj.dtype),
            jax.ShapeDtypeStruct((1, HEAD), F32), jax.ShapeDtypeStruct((HG_HEADS, HEAD), F32)]
    return pl.pallas_call(
        body, name="hgrn_bwd", grid=(nb, HG_HEADS // HPS),
        in_specs=[_view_tile(v, rows, HPS * HEAD, lambda c: nb - 1 - c) for v in (hq, hf, hi, hg)] + [
                  pl.BlockSpec((2, HPS * HEAD), lambda c, g: (0, g)),
                  pl.BlockSpec((1, HEAD), lambda c, g: (0, 0)), tile,
                  pl.BlockSpec((cb, HPS, HEAD, HEAD), lambda c, g: (nb - 1 - c, g, 0, 0)), tile,
                  pl.BlockSpec(mstack.shape, lambda c, h: (0, 0)),
                  pl.BlockSpec(mstack_t.shape, lambda c, h: (0, 0)),
                  pl.BlockSpec(masks.shape, lambda c, h: (0, 0, 0)),
                  pl.BlockSpec(eye.shape, lambda c, h: (0, 0)), _ANY],
        out_specs=[pl.BlockSpec((rows, width), lambda c, h: (nb - 1 - c, off // width)),
                   pl.BlockSpec((1, HEAD), lambda c, h: (0, 0)),
                   pl.BlockSpec((HG_HEADS, HEAD), lambda c, h: (0, 0))],
        out_shape=outs, scratch_shapes=[pltpu.VMEM((HG_HEADS, HEAD, HEAD), F32)], input_output_aliases={13: 0},
        compiler_params=_params(_ARB, _ARB))(hq[0], hf[0], hi[0], hg[0], logits, gain, oraw, ssave, dog, mstack,
                                             mstack_t, masks, eye, dproj)


CONV_W = 512
CONV_ROWS = 1024


def _per_head(fn, *arrs):
    width = arrs[0].shape[1]
    return jnp.concatenate([fn(*[a[:, j:j + HEAD] for a in arrs]) for j in range(0, width, HEAD)], axis=1)


def _shift_down(xv, halo, d, top_rows):
    if d == 0:
        return xv, xv[0:8]
    main = pltpu.roll(xv, d, 0)
    top = jnp.where(top_rows < d, pltpu.roll(halo, d, 0), main[0:8])
    return main, top


def _conv_parts(x_ref, halo_ref, w_ref, first):
    xv = x_ref[...]
    halo = jnp.where(first, 0.0, halo_ref[...])
    top_rows = lax.broadcasted_iota(jnp.int32, (8, xv.shape[1]), 0)
    shifted = [_shift_down(xv, halo, CONV_K - 1 - j, top_rows) for j in range(CONV_K)]
    w = w_ref[...]
    acc = sum(shifted[j][0] * w[j:j + 1, :] for j in range(CONV_K))
    acc_top = sum(shifted[j][1] * w[j:j + 1, :] for j in range(CONV_K))
    return shifted, acc, acc_top


def _conv_fwd(x, w8, l2scale, name):
    x, off, width = x
    t = x.shape[0]
    o = off // CONV_W
    tr = _pick(t, CONV_ROWS, 8)

    def post(cv):
        s = _silu(cv)
        if l2scale is not None:
            s = _per_head(lambda sh: sh * (lax.rsqrt(_rowsum(sh * sh) + EPS) * l2scale), s)
        return s

    def body(x_ref, halo_ref, w_ref, o_ref):
        _, acc, acc_top = _conv_parts(x_ref, halo_ref, w_ref, pl.program_id(1) == 0)
        o_ref[...] = post(acc)
        o_ref[0:8, :] = post(acc_top)

    return pl.pallas_call(
        body, name=name, grid=(width // CONV_W,t // tr),
        in_specs=[pl.BlockSpec((tr, CONV_W), lambda j, i: (i, o + j)),
                  pl.BlockSpec((8, CONV_W), lambda j, i: (jnp.maximum(i * (tr // 8) - 1, 0), o + j)),
                  pl.BlockSpec((8, CONV_W), lambda j, i: (0, j))],
        out_specs=pl.BlockSpec((tr, CONV_W), lambda j, i: (i, j)),
        out_shape=jax.ShapeDtypeStruct((t, width), F32), compiler_params=_params(_PAR, _PAR))(x, x, w8)


def _conv_bwd_a(x, w8, dy, l2scale, name):
    x, off, width = x
    t = x.shape[0]
    o = off // CONV_W
    tr = _pick(t, CONV_ROWS, 8)

    def l2_bwd(s, dyh):
        r = lax.rsqrt(_rowsum(s * s) + EPS)
        y0 = s * r
        dy0 = dyh * l2scale
        return r * (dy0 - y0 * _rowsum(dy0 * y0))

    def to_dc(cv, dyv):
        if l2scale is not None:
            dyv = _per_head(l2_bwd, _silu(cv), dyv)
        return dyv * _dsilu(cv)

    def body(x_ref, halo_ref, w_ref, dy_ref, dc_ref, dw_ref):
        @pl.when(pl.program_id(1) == 0)
        def _():
            dw_ref[...] = jnp.zeros_like(dw_ref)

        shifted, acc, acc_top = _conv_parts(x_ref, halo_ref, w_ref, pl.program_id(1) == 0)
        dyv = dy_ref[...]
        dc = to_dc(acc, dyv)
        dc_top = to_dc(acc_top, dyv[0:8])
        dc_ref[...] = dc
        dc_ref[0:8, :] = dc_top
        rest = (lax.broadcasted_iota(jnp.int32, dc.shape, 0) >= 8).astype(F32)
        dc_rest = dc * rest
        for j in range(CONV_K):
            dw_ref[j:j + 1, :] += (jnp.sum(dc_rest * shifted[j][0], axis=0, keepdims=True)
                                   + jnp.sum(dc_top * shifted[j][1], axis=0, keepdims=True))

    return pl.pallas_call(
        body, name=name, grid=(width // CONV_W,t // tr),
        in_specs=[pl.BlockSpec((tr, CONV_W), lambda j, i: (i, o + j)),
                  pl.BlockSpec((8, CONV_W), lambda j, i: (jnp.maximum(i * (tr // 8) - 1, 0), o + j)),
                  pl.BlockSpec((8, CONV_W), lambda j, i: (0, j)),
                  pl.BlockSpec((tr, CONV_W), lambda j, i: (i, j))],
        out_specs=[pl.BlockSpec((tr, CONV_W), lambda j, i: (i, j)), pl.BlockSpec((8, CONV_W), lambda j, i: (0, j))],
        out_shape=[jax.ShapeDtypeStruct((t, width), F32), jax.ShapeDtypeStruct((8, width), F32)],
        compiler_params=_params(_PAR, _ARB))(x, x, w8, dy)


def _conv_bwd_b(dc, w8, name, into):
    t, width = dc.shape
    tr = _pick(t, CONV_ROWS, 8)
    nt = t // tr

    dproj, off, into_width = into
    assert into_width == width and off % CONV_W == 0
    o = off // CONV_W

    def body(dc_ref, halo_ref, w_ref, _, dx_ref):
        dcv = dc_ref[...]
        halo = jnp.where(pl.program_id(1) == nt - 1, 0.0, halo_ref[...])
        w = w_ref[...]
        bot_rows = lax.broadcasted_iota(jnp.int32, (8, CONV_W), 0)
        acc = dcv * w[CONV_K - 1:CONV_K, :]
        acc_bot = dcv[tr - 8:tr] * w[CONV_K - 1:CONV_K, :]
        for d in range(1, CONV_K):
            main = pltpu.roll(dcv, tr - d, 0)
            bot = jnp.where(bot_rows >= 8 - d, pltpu.roll(halo, 8 - d, 0), main[tr - 8:tr])
            wj = w[CONV_K - 1 - d:CONV_K - d, :]
            acc = acc + main * wj
            acc_bot = acc_bot + bot * wj
        dx_ref[...] = acc.astype(BF16)
        dx_ref[tr - 16:tr, :] = jnp.concatenate([acc[tr - 16:tr - 8], acc_bot], axis=0).astype(BF16)

    return pl.pallas_call(
        body, name=name, grid=(width // CONV_W,nt),
        in_specs=[pl.BlockSpec((tr, CONV_W), lambda j, i: (i, j)),
                  pl.BlockSpec((8, CONV_W), lambda j, i: (jnp.minimum((i + 1) * (tr // 8), t // 8 - 1), j)),
                  pl.BlockSpec((8, CONV_W), lambda j, i: (0, j)), _ANY],
        out_specs=pl.BlockSpec((tr, CONV_W), lambda j, i: (i, o + j)),
        out_shape=jax.ShapeDtypeStruct(dproj.shape, dproj.dtype), input_output_aliases={3: 0},
        compiler_params=_params(_PAR, _PAR))(dc, dc, w8, dproj)


def _each(f, *lists):
    return [f(*xs) for xs in zip(*lists)]


def _split2_each(xs):
    hi = [_bf(x) for x in xs]
    lo = [_bf(x - h.astype(F32)) for x, h in zip(xs, hi)]
    return hi, lo


def _hp_each(a_split, b_split):
    (ah, al), (bh, bl) = a_split, b_split
    rows = ah[0].shape[0]
    d12 = [jnp.dot(jnp.concatenate([x, y], axis=0), z, preferred_element_type=F32) for x, y, z in zip(ah, al, bh)]
    d3 = [jnp.dot(x, y, preferred_element_type=F32) for x, y in zip(ah, bl)]
    return [d[:rows] + d[rows:] + e for d, e in zip(d12, d3)]


INV_EXACT_STEPS = 2


def _tri_inv_each(a_list, eye):
    ns = [-a for a in a_list]
    ps = [eye + n for n in ns]
    n_split = _split2_each(ns)
    for step in range(5):
        if step < INV_EXACT_STEPS:
            ns = _hp_each(n_split, n_split)
            n_split = _split2_each(ns)
            ps = [p + d for p, d in zip(ps, _hp_each(_split2_each(ps), n_split))]
        else:
            nb = n_split[0] if step == INV_EXACT_STEPS else [_bf(n) for n in ns]
            ns = [jnp.dot(x, x, preferred_element_type=F32) for x in nb]
            nb2 = [_bf(n) for n in ns]
            ps = [p + jnp.dot(_bf(p), y, preferred_element_type=F32) for p, y in zip(ps, nb2)]
    return ps


def _gd_gates(gab, alog, dtb):
    sp_arg = gab + dtb
    return sp_arg, -jnp.exp(alog) * _softplus(sp_arg), _sigmoid(gab)


def _pick_lane(tile, base, head):
    g, hh = head
    col = tile[:, base + hh:base + hh + 1]
    for gi in range(1, GD_HEADS // HPS):
        lane = base + gi * HPS + hh
        col = jnp.where(g == gi, tile[:, lane:lane + 1], col)
    return col


def _gd_chunks(q, k, v, g_all, beta_all, heads, l_ref, mask_ref, tm=None):
    incl, strict, eye, upper = mask_ref[0], mask_ref[1], mask_ref[2], mask_ref[3]
    lmat = l_ref[...]
    gb = [jnp.broadcast_to(_pick_lane(g_all, 0, s), (CHUNK, HEAD)) for s in heads]
    bb = [jnp.broadcast_to(_pick_lane(beta_all, GD_HEADS, s), (CHUNK, HEAD)) for s in heads]
    gam = _mx_each(lmat, gb)
    gam_row = [jnp.sum(x[:, :CHUNK] * upper, axis=0, keepdims=True) for x in gb]
    lm = _each(lambda gm, gr: incl * jnp.exp(jnp.minimum(gm[:, :CHUNK] - gr, 0.0)), gam, gam_row)
    kb = _each(lambda x, b: x * b, k, bb)
    a = _each(lambda x, y, m: strict * _dot_nt(x, y) * m, kb, k, lm)
    if tm is None:
        tm = _tri_inv_each(a, eye)
    eg = [jnp.exp(x) for x in gam]
    vb = _each(lambda x, b: x * b, v, bb)
    kbg = _each(lambda x, e: x * e, kb, eg)
    uw = _each(lambda t_, x, y: _dot(t_, jnp.concatenate([x, y], axis=1)), tm, vb, kbg)
    u = [x[:, :HEAD] for x in uw]
    w = [x[:, HEAD:] for x in uw]
    qk = _each(lambda x, y, m: _dot_nt(x, y) * m, q, k, lm)
    g_end = [x[CHUNK - 1:CHUNK, :] for x in gam]
    ekg = _each(lambda e, x: jnp.exp(e - x), g_end, gam)
    ge = [jnp.exp(e) for e in g_end]
    kg = _each(lambda x, e: x * e, k, ekg)
    qg = _each(lambda x, e: x * e, q, eg)
    names = ("bb", "lm", "kb", "a", "tm", "eg", "vb", "kbg", "u", "w", "qk", "ekg", "ge", "kg", "qg")
    cols = (bb, lm, kb, a, tm, eg, vb, kbg, u, w, qk, ekg, ge, kg, qg)
    return [dict(zip(names, vals)) for vals in zip(*cols)]


def _gd_specs(rows, rev_nb=None):
    def cidx(c):
        return c if rev_nb is None else rev_nb - 1 - c

    qk_tile = pl.BlockSpec((rows, HPS // 2 * HEAD), lambda c, g: (cidx(c), g))
    v_tile = pl.BlockSpec((rows, HPS * HEAD), lambda c, g: (cidx(c), g))
    gab_tile = pl.BlockSpec((rows, HEAD), lambda c, g: (cidx(c), 0))
    return qk_tile, v_tile, gab_tile


def _gdn_fwd(qn, kn, cv, gab, gz, alog, dtb, gain, consts):
    t = qn.shape[0]
    nc = t // CHUNK
    cb = _chunks_per_step(nc)
    rows = cb * CHUNK
    lmat, _, masks = consts
    qk_tile, v_tile, gab_tile = _gd_specs(rows)
    row128 = pl.BlockSpec((1, HEAD), lambda c, h: (0, 0))

    def body(q_ref, k_ref, v_ref, gab_ref, gz_ref, alog_ref, dtb_ref, gain_ref, l_ref, mask_ref,
             oraw_ref, og_ref, ssave_ref, tsave_ref, state):
        c = pl.program_id(0)
        g = pl.program_id(1)

        @pl.when(c == 0)
        def _():
            for hh in range(HPS):
                state[g * HPS + hh] = jnp.zeros((HEAD, HEAD), F32)

        alog = alog_ref[...]
        dtb = dtb_ref[...]
        gain_v = gain_ref[...]

        def one(i, carry):
            sl = pl.ds(pl.multiple_of(i * CHUNK, CHUNK), CHUNK)
            _, g_all, beta_all = _gd_gates(gab_ref[sl, :], alog, dtb)
            heads = [g * HPS + hh for hh in range(HPS)]
            lq = [slice(hh // 2 * HEAD, (hh // 2 + 1) * HEAD) for hh in range(HPS)]
            lv = [slice(hh * HEAD, (hh + 1) * HEAD) for hh in range(HPS)]
            chs = _gd_chunks([q_ref[sl, s] for s in lq], [k_ref[sl, s] for s in lq], [v_ref[sl, s] for s in lv],
                             g_all, beta_all, [(g, hh) for hh in range(HPS)], l_ref, mask_ref)
            s0 = [state[h] for h in heads]
            ws = _each(lambda ch, s: _dot(jnp.concatenate([ch["w"], ch["qg"]], axis=0), s), chs, s0)
            v_new = _each(lambda ch, x: ch["u"] - x[:CHUNK], chs, ws)
            o = _each(lambda ch, x, vn: x[CHUNK:] + _dot(ch["qk"], vn), chs, ws, v_new)
            s1 = _each(lambda ch, s, vn: s * ch["ge"] + _dot_tn(ch["kg"], vn), chs, s0, v_new)
            for hh in range(HPS):
                ssave_ref[i, hh] = s0[hh]
                tsave_ref[i, hh] = chs[hh]["tm"]
                state[heads[hh]] = s1[hh]
                oraw_ref[sl, lv[hh]] = o[hh]
                r = lax.rsqrt(jnp.mean(o[hh] * o[hh], axis=1, keepdims=True) + EPS)
                og_ref[sl, lv[hh]] = (o[hh] * r * gain_v * _silu(gz_ref[sl, lv[hh]])).astype(BF16)
            return carry

        lax.fori_loop(0, cb, one, 0, unroll=4)

    return pl.pallas_call(
        body, name="gdn_fwd", grid=(nc // cb, GD_HEADS // HPS),
        in_specs=[qk_tile, qk_tile, v_tile, gab_tile, _view_tile(gz, rows, HPS * HEAD), row128, row128, row128,
                  pl.BlockSpec(lmat.shape, lambda c, g: (0, 0)),
                  pl.BlockSpec(masks.shape, lambda c, g: (0, 0, 0))],
        out_specs=[v_tile, v_tile, pl.BlockSpec((cb, HPS, HEAD, HEAD), lambda c, g: (c, g, 0, 0)),
                   pl.BlockSpec((cb, HPS, CHUNK, CHUNK), lambda c, g: (c, g, 0, 0))],
        out_shape=[jax.ShapeDtypeStruct((t, GD_HEADS * HEAD), F32), jax.ShapeDtypeStruct((t, GD_HEADS * HEAD), BF16),
                   jax.ShapeDtypeStruct((nc, GD_HEADS, HEAD, HEAD), F32),
                   jax.ShapeDtypeStruct((nc, GD_HEADS, CHUNK, CHUNK), F32)],
        scratch_shapes=[pltpu.VMEM((GD_HEADS, HEAD, HEAD), F32)],
        compiler_params=_params(_ARB, _ARB))(qn, kn, cv, gab, gz[0], alog, dtb, gain, lmat, masks)


def _gdn_bwd(qn, kn, cv, gab, gz, alog, dtb, gain, oraw, ssave, tsave, dog, consts, into):
    t = qn.shape[0]
    dproj, off, width = into
    assert width == GD_HEADS * HEAD and off % (HPS * HEAD) == 0
    nc = t // CHUNK
    cb = _chunks_per_step(nc)
    rows = cb * CHUNK
    nb = nc // cb
    lmat, lmat_t, masks = consts
    qk_tile, v_tile, gab_tile = _gd_specs(rows, nb)
    row128 = pl.BlockSpec((1, HEAD), lambda c, h: (0, 0))

    def body(q_ref, k_ref, v_ref, gab_ref, gz_ref, alog_ref, dtb_ref, gain_ref, oraw_ref, ssave_ref, tsave_ref, dog_ref,
             l_ref, lt_ref, mask_ref, _,
             dq_ref, dk_ref, dv_ref, dgab_ref, dgz_ref, small_ref, dstate):
        c = pl.program_id(0)
        g = pl.program_id(1)

        @pl.when(c == 0)
        def _():
            for hh in range(HPS):
                dstate[g * HPS + hh] = jnp.zeros((HEAD, HEAD), F32)

        @pl.when((c == 0) & (g == 0))
        def _():
            small_ref[...] = jnp.zeros_like(small_ref)

        alog = alog_ref[...]
        dtb = dtb_ref[...]
        gain_v = gain_ref[...]
        lane = lax.broadcasted_iota(jnp.int32, (1, HEAD), 1)
        last_row = (lax.broadcasted_iota(jnp.int32, (CHUNK, HEAD), 0) == CHUNK - 1).astype(F32)

        def one(j, carry):
            i = cb - 1 - j
            sl = pl.ds(pl.multiple_of(i * CHUNK, CHUNK), CHUNK)
            sp_arg, g_all, beta_all = _gd_gates(gab_ref[sl, :], alog, dtb)
            strict, eye = mask_ref[1], mask_ref[2]
            ltm = lt_ref[...]
            hs = range(HPS)
            heads = [g * HPS + hh for hh in hs]
            lq = [slice(hh // 2 * HEAD, (hh // 2 + 1) * HEAD) for hh in hs]
            lv = [slice(hh * HEAD, (hh + 1) * HEAD) for hh in hs]
            q = [q_ref[sl, s] for s in lq]
            k = [k_ref[sl, s] for s in lq]
            v = [v_ref[sl, s] for s in lv]
            gzv = [gz_ref[sl, s] for s in lv]
            chs = _gd_chunks(q, k, v, g_all, beta_all, [(g, hh) for hh in hs], l_ref, mask_ref,
                             tm=[tsave_ref[i, hh] for hh in hs])

            def col(name):
                return [ch[name] for ch in chs]

            def mul(x, y):
                return x * y

            tm, lm, eg, bb = col("tm"), col("lm"), col("eg"), col("bb")
            s0 = [ssave_ref[i, hh] for hh in hs]
            ds = [dstate[h] for h in heads]
            v_new = _each(lambda u, w, s: u - _dot(w, s), col("u"), col("w"), s0)

            o = [oraw_ref[sl, s] for s in lv]
            r = [lax.rsqrt(jnp.mean(x * x, axis=1, keepdims=True) + EPS) for x in o]
            on = _each(mul, o, r)
            dg_out = [dog_ref[sl, s] for s in lv]
            sgate = [_silu(x) for x in gzv]
            for hh in hs:
                dgz_ref[sl, lv[hh]] = (dg_out[hh] * on[hh] * gain_v * _dsilu(gzv[hh])).astype(BF16)
            small_ref[0:1, :] += sum(jnp.sum(d * s * n, axis=0, keepdims=True) for d, s, n in zip(dg_out, sgate, on))
            don = _each(lambda d, s: d * s * gain_v, dg_out, sgate)
            do = _each(lambda rr, dn, n: rr * (dn - n * jnp.mean(dn * n, axis=1, keepdims=True)), r, don, on)

            dv_new = _each(lambda a, d, b, s: _dot_tn(a, d) + _dot(b, s), col("qk"), do, col("kg"), ds)
            dqk = _each(_dot_nt, do, v_new)
            dkg = _each(_dot_nt, v_new, ds)
            dge = _each(lambda s, d: jnp.sum(_rowsum(s * d), axis=0, keepdims=True), s0, ds)
            both = _each(lambda d, dv: jnp.concatenate([d, dv], axis=0), do, dv_new)
            from_s = _each(_dot_nt, both, s0)
            dqg = [x[:CHUNK] for x in from_s]
            dw = [-x[CHUNK:] for x in from_s]
            ds_new = _each(lambda qg, w, bo, ge, s: _dot_tn(jnp.concatenate([qg, -w], axis=0), bo) + ge * s,
                           col("qg"), col("w"), both, col("ge"), ds)
            for hh in hs:
                dstate[heads[hh]] = ds_new[hh]

            side = _each(lambda dv, d: jnp.concatenate([dv, d], axis=1), dv_new, dw)
            back = _each(_dot_tn, tm, side)
            dvb = [x[:, :HEAD] for x in back]
            dkbg = [x[:, HEAD:] for x in back]
            dtm = _each(lambda sd, vb, kbg: _dot_nt(sd, jnp.concatenate([vb, kbg], axis=1)), side, col("vb"), col("kbg"))
            dtt = _each(_dot_nt, dtm, tm)
            da = _each(lambda t_, x: -_dot_tn(t_, x) * strict, tm, dtt)
            dal = _each(mul, da, lm)
            dqk_l = _each(mul, dqk, lm)
            stack = _each(lambda x, y: jnp.concatenate([x, y], axis=0), dal, dqk_l)
            on_k = _each(_dot, stack, k)
            dkb = _each(lambda x, y, e: x[:CHUNK] + y * e, on_k, dkbg, eg)
            dq = _each(lambda x, y, e: x[CHUNK:] + y * e, on_k, dqg, eg)
            dk = _each(lambda st, kb, qq, z, ekg, w_, b: _dot_tn(st, jnp.concatenate([kb, qq], axis=0)) + z * ekg + w_ * b,
                       stack, col("kb"), q, dkg, col("ekg"), dkb, bb)
            gmat = _each(lambda x, a, y, qk: x * a + y * qk, da, col("a"), dqk, col("qk"))
            t_kg = _each(lambda x, y: _rowsum(x * y), dkg, col("kg"))
            dgam = _each(lambda gm, x, qg, t_, y, kbg: (_rowsum(gm) - _row_to_col(jnp.sum(gm, axis=0, keepdims=True), eye)
                                                        + _rowsum(x * qg) - t_ + _rowsum(y * kbg)),
                         gmat, dqg, col("qg"), t_kg, dkbg, col("kbg"))
            dg_end = _each(lambda t_, e, ge: jnp.sum(t_, axis=0, keepdims=True) + e * ge[:, 0:1], t_kg, dge, col("ge"))
            dgam = _each(lambda x, e: x + last_row * e, dgam, dg_end)
            dbeta = _each(lambda x, kk, y, vv: _rowsum(x * kk) + _rowsum(y * vv), dkb, k, dvb, v)
            dg = _mx_each(ltm, dgam)

            for hh in hs:
                dv_ref[sl, lv[hh]] = dvb[hh] * bb[hh]
            fac_g = -jnp.exp(alog) * _sigmoid(sp_arg)
            fac_b = beta_all * (1.0 - beta_all)
            hot_g = [(lane == h).astype(F32) for h in heads]
            hot_b = [(lane == GD_HEADS + h).astype(F32) for h in heads]
            dga = _each(lambda x, hot: x * hot * fac_g, dg, hot_g)
            dgb = _each(lambda x, hot: x * hot * fac_b, dbeta, hot_b)
            small_ref[1:2, :] += sum(jnp.sum(x, axis=0, keepdims=True) for x in dga)
            small_ref[2:3, :] += sum(jnp.sum(x * hot * g_all, axis=0, keepdims=True) for x, hot in zip(dg, hot_g))
            for pair in range(HPS // 2):
                lqp = slice(pair * HEAD, (pair + 1) * HEAD)
                dq_ref[sl, lqp] = dq[2 * pair] + dq[2 * pair + 1]
                dk_ref[sl, lqp] = dk[2 * pair] + dk[2 * pair + 1]
            dgab_ref[sl, :] = sum(a + b for a, b in zip(dga, dgb))
            return carry

        lax.fori_loop(0, cb, one, 0, unroll=4)

    groups = GD_HEADS // HPS
    outs = [jax.ShapeDtypeStruct((t, 1024), F32), jax.ShapeDtypeStruct((t, 1024), F32),
            jax.ShapeDtypeStruct((t, 2048), F32), jax.ShapeDtypeStruct((t, groups * HEAD), F32),
            jax.ShapeDtypeStruct(dproj.shape, dproj.dtype), jax.ShapeDtypeStruct((8, HEAD), F32)]
    dgz_tile = pl.BlockSpec((rows, HPS * HEAD), lambda c, g: (nb - 1 - c, off // (HPS * HEAD) + g))
    return pl.pallas_call(
        body, name="gdn_bwd", grid=(nb, groups),
        in_specs=[qk_tile, qk_tile, v_tile, gab_tile, _view_tile(gz, rows, HPS * HEAD, lambda c: nb - 1 - c),
                  row128, row128, row128, v_tile,
                  pl.BlockSpec((cb, HPS, HEAD, HEAD), lambda c, g: (nb - 1 - c, g, 0, 0)),
                  pl.BlockSpec((cb, HPS, CHUNK, CHUNK), lambda c, g: (nb - 1 - c, g, 0, 0)), v_tile,
                  pl.BlockSpec(lmat.shape, lambda c, g: (0, 0)),
                  pl.BlockSpec(lmat_t.shape, lambda c, g: (0, 0)),
                  pl.BlockSpec(masks.shape, lambda c, g: (0, 0, 0)), _ANY],
        out_specs=[qk_tile, qk_tile, v_tile, pl.BlockSpec((rows, HEAD), lambda c, g: (nb - 1 - c, g)), dgz_tile,
                   pl.BlockSpec((8, HEAD), lambda c, g: (0, 0))],
        out_shape=outs, scratch_shapes=[pltpu.VMEM((GD_HEADS, HEAD, HEAD), F32)], input_output_aliases={15: 4},
        compiler_params=_params(_ARB, _ARB))(qn, kn, cv, gab, gz[0], alog, dtb, gain, oraw, ssave, tsave, dog,
                                             lmat, lmat_t, masks, dproj)


def _fold_groups(wide):
    t, width = wide.shape
    tr = _pick(t, CONV_ROWS, 8)

    def body(w_ref, o_ref):
        acc = w_ref[:, 0:HEAD]
        for j in range(1, width // HEAD):
            acc = acc + w_ref[:, j * HEAD:(j + 1) * HEAD]
        o_ref[...] = acc.astype(BF16)

    return pl.pallas_call(
        body, name="fold_gate_grads", grid=(t // tr,), in_specs=[_row_spec(tr, width)], out_specs=_row_spec(tr, HEAD),
        out_shape=jax.ShapeDtypeStruct((t, HEAD), BF16), compiler_params=_params(_PAR))(wide)


def _adam_math(w, g, m, v):
    m2 = ADAM_B1 * m + (1.0 - ADAM_B1) * g
    v2 = ADAM_B2 * v + (1.0 - ADAM_B2) * (g * g)
    m_hat = m2 / (1.0 - ADAM_B1 ** ADAM_STEP)
    v_hat = v2 / (1.0 - ADAM_B2 ** ADAM_STEP)
    delta = -ADAM_LR * (m_hat / (jnp.sqrt(v_hat) + ADAM_EPS) + ADAM_WD * w)
    return delta, m2, v2


def _adamw(w, g, m, v, name, after=None):
    r, c = w.shape
    tr = r
    for cand in range(8, r + 1, 8):
        if r % cand == 0 and cand * c * 4 <= (1 << 20):
            tr = cand
    if r % 8 != 0:
        tr = r

    def body(w_ref, g_ref, m_ref, v_ref, *rest):
        d_ref, m2_ref, v2_ref = rest[-3:]
        d, m2, v2 = _adam_math(w_ref[...], g_ref[...], m_ref[...], v_ref[...])
        d_ref[...] = d
        m2_ref[...] = m2
        v2_ref[...] = v2

    spec = pl.BlockSpec((tr, c), lambda i: (i, 0))
    extra = [] if after is None else [after]
    return pl.pallas_call(
        body, name=name, grid=(r // tr,), in_specs=[spec] * 4 + [_ANY] * len(extra), out_specs=[spec] * 3,
        out_shape=[jax.ShapeDtypeStruct((r, c), F32)] * 3, compiler_params=_params(_PAR))(w, g, m, v, *extra)


_ANY = pl.BlockSpec(memory_space=pl.ANY)


def _place():
    return lax.axis_index("x"), lax.axis_index("y"), lax.axis_index("c")


def _gather_weights(packs, nchs, name):
    n = len(packs)
    halves = [p.shape[0] // 2 for p in packs]
    base = [sum(nchs[:i]) for i in range(n)]
    total = sum(nchs)
    for p, h, k in zip(packs, halves, nchs):
        assert p.shape[0] == 2 * h and h % k == 0 and (h // k) % 16 == 0

    def body(*refs):
        p_refs, g_refs, (send_sems, recv_sems) = refs[:n], refs[n:2 * n], refs[2 * n:]
        x, y, c = _place()
        sibling = (x, y, 1 - c)
        chips = [(1 - x, y), (x, 1 - y), (1 - x, 1 - y)]
        chunks = [(a, q) for a in range(n) for q in range(nchs[a])]

        def rows_of(a, pc, q):
            ch = halves[a] // nchs[a]
            return pl.ds(pl.multiple_of(pc * halves[a] + q * ch, 16), ch)

        def piece(a, px, py, pc, q):
            return g_refs[a].at[2 * px + py, rows_of(a, pc, q), :]

        def copy(k, src, dst, to):
            return pltpu.make_async_remote_copy(src_ref=src, dst_ref=dst, send_sem=send_sems.at[k],
                                                recv_sem=recv_sems.at[k], device_id=to, device_id_type=MESH)

        def sem_of(j, a, q):
            return j * total + base[a] + q

        first = {(j, a, q): copy(sem_of(j, a, q), p_refs[a].at[rows_of(a, c, q), :], piece(a, x, y, c, q), (*chip, c))
                 for j, chip in enumerate(chips) for a, q in chunks}
        for a, q in chunks:
            for j in range(3):
                first[j, a, q].start()
        passed = {(j, a, q): copy(sem_of(3 + j, a, q), piece(a, *chip, c, q), piece(a, *chip, c, q), sibling)
                  for j, chip in enumerate(chips) for a, q in chunks}
        for a, q in chunks:
            for j, chip in enumerate(chips):
                copy(sem_of(j, a, q), p_refs[a].at[rows_of(a, c, q), :], piece(a, *chip, c, q), (*chip, c)).wait_recv()
                passed[j, a, q].start()
        for a, q in chunks:
            for j, chip in enumerate(chips):
                copy(sem_of(3 + j, a, q), piece(a, *chip, 1 - c, q), piece(a, *chip, 1 - c, q), sibling).wait_recv()
        for key in first:
            first[key].wait_send()
            passed[key].wait_send()

    return pl.pallas_call(
        body, name=name, out_shape=[jax.ShapeDtypeStruct((4,) + p.shape, p.dtype) for p in packs],
        in_specs=[_ANY] * n, out_specs=[_ANY] * n,
        scratch_shapes=[pltpu.SemaphoreType.DMA((6 * total,)), pltpu.SemaphoreType.DMA((6 * total,))])(*packs)


def _swap_with_sibling(arrs, nchs, lead, name, halves=False):
    n = len(arrs)
    jobs = []
    hs = [arr.shape[-2] // (2 if halves else 1) for arr in arrs]
    for a, (h, k) in enumerate(zip(hs, nchs)):
        assert h % k == 0 and (h // k) % 16 == 0
        for s in (range(lead) if lead else [None]):
            jobs += [(a, s, q * (h // k), h // k) for q in range(k)]

    def body(*refs):
        src, dst, (send_sems, recv_sems) = refs[:n], refs[n:2 * n], refs[2 * n:]
        x, y, c = _place()

        def at(ref, s, r0, rows):
            return ref.at[pl.ds(r0, rows), :] if s is None else ref.at[s, pl.ds(r0, rows), :]

        def src_rows(a, r0):
            return pl.multiple_of((1 - c) * hs[a] + r0, 16) if halves else r0

        copies = [pltpu.make_async_remote_copy(
            src_ref=at(src[a], s, src_rows(a, r0), rows), dst_ref=at(dst[a], s, r0, rows), send_sem=send_sems.at[k],
            recv_sem=recv_sems.at[k], device_id=(x, y, 1 - c), device_id_type=MESH)
            for k, (a, s, r0, rows) in enumerate(jobs)]
        for cp in copies:
            cp.start()
        for cp in copies:
            cp.wait()

    shapes = [jax.ShapeDtypeStruct(arr.shape[:-2] + (h, arr.shape[-1]), arr.dtype) for arr, h in zip(arrs, hs)]
    return pl.pallas_call(
        body, name=name, out_shape=shapes, in_specs=[_ANY] * n, out_specs=[_ANY] * n,
        scratch_shapes=[pltpu.SemaphoreType.DMA((len(jobs),)), pltpu.SemaphoreType.DMA((len(jobs),))])(*arrs)


def _add2(full, b, core, name):
    n, rows, w = b.shape
    tr = _pick(rows, 256, 16)
    nblk = rows // tr

    def body(c_ref, a_ref, b_ref, o_ref):
        o_ref[...] = (a_ref[...].astype(F32) + b_ref[...].astype(F32)).astype(BF16)

    spec = pl.BlockSpec((1, tr, w), lambda i, j, c_ref: (i, j, 0))
    grid_spec = pltpu.PrefetchScalarGridSpec(
        num_scalar_prefetch=1, grid=(n, nblk),
        in_specs=[pl.BlockSpec((1, tr, w), lambda i, j, c_ref: (i, c_ref[0] * nblk + j, 0)), spec], out_specs=spec)
    return pl.pallas_call(
        body, name=name, grid_spec=grid_spec, out_shape=jax.ShapeDtypeStruct(b.shape, BF16),
        compiler_params=_params(_PAR, _PAR))(core, full, b)


def _reduce_chips(partials, nchs, name):
    n = len(partials)
    jobs = []
    for a, (arr, k) in enumerate(zip(partials, nchs)):
        h = arr.shape[1]
        assert h % k == 0 and (h // k) % 16 == 0
        jobs += [(a, q * (h // k), h // k) for q in range(k)]

    def body(*refs):
        src, dst, (send_sems, recv_sems) = refs[:n], refs[n:2 * n], refs[2 * n:]
        x, y, c = _place()
        chips = [(1 - x, y), (x, 1 - y), (1 - x, 1 - y)]
        copies = [pltpu.make_async_remote_copy(
            src_ref=src[a].at[2 * px + py, pl.ds(r0, rows), :], dst_ref=dst[a].at[j, pl.ds(r0, rows), :],
            send_sem=send_sems.at[3 * k + j], recv_sem=recv_sems.at[3 * k + j],
            device_id=(px, py, c), device_id_type=MESH)
            for k, (a, r0, rows) in enumerate(jobs) for j, (px, py) in enumerate(chips)]
        for cp in copies:
            cp.start()
        for cp in copies:
            cp.wait()

    return pl.pallas_call(
        body, name=name,
        out_shape=[jax.ShapeDtypeStruct((3,) + p.shape[1:], p.dtype) for p in partials],
        in_specs=[_ANY] * n, out_specs=[_ANY] * n,
        scratch_shapes=[pltpu.SemaphoreType.DMA((3 * len(jobs),)), pltpu.SemaphoreType.DMA((3 * len(jobs),))])(*partials)


_HBM = pl.BlockSpec(memory_space=pltpu.HBM)
_SEM = pl.BlockSpec(memory_space=pltpu.SEMAPHORE)
_DATAFLOW = pltpu.SideEffectType.DATAFLOW_SIDE_EFFECTING


def _ici_jobs(srcs, nchs, kind):
    jobs = []
    for a, (arr, k) in enumerate(zip(srcs, nchs)):
        h = arr.shape[0] // 2 if kind == "gather" else arr.shape[1]
        assert h % k == 0 and (h // k) % 16 == 0
        jobs += [(a, h, q * (h // k), h // k) for q in range(k)]
    return jobs


def _ici_copies(src, land, send_sems, recv_sems, jobs, kind):
    x, y, c = _place()
    chips = [(1 - x, y), (x, 1 - y), (1 - x, 1 - y)]
    copies = []
    for k, (a, h, r0, rows) in enumerate(jobs):
        for j, (px, py) in enumerate(chips):
            if kind == "gather":
                at = pl.ds(pl.multiple_of(c * h + r0, 16), rows)
                s, d = src[a].at[at, :], land[a].at[2 * x + y, at, :]
            else:
                s, d = src[a].at[2 * px + py, pl.ds(r0, rows), :], land[a].at[j, pl.ds(r0, rows), :]
            copies.append(pltpu.make_async_remote_copy(
                src_ref=s, dst_ref=d, send_sem=send_sems.at[3 * k + j], recv_sem=recv_sems.at[3 * k + j],
                device_id=(px, py, c), device_id_type=MESH))
    return copies


def _ici_start(srcs, nchs, kind, name):
    n = len(srcs)
    jobs = _ici_jobs(srcs, nchs, kind)
    lead = (lambda s: (4,) + s.shape) if kind == "gather" else (lambda s: (3,) + s.shape[1:])
    lands = [lax.empty(lead(s), s.dtype) for s in srcs]

    def body(*refs):
        src, land = refs[:n], refs[n:2 * n]
        send_sems, recv_sems, token = refs[2 * n], refs[2 * n + 1], refs[-1]
        for cp in _ici_copies(src, land, send_sems, recv_sems, jobs, kind):
            cp.start()
        token[...] = jnp.zeros_like(token)

    hbm = [pltpu.HBM(a.shape, a.dtype) for a in srcs + lands]
    outs = pl.pallas_call(
        body, name=name,
        out_shape=[pltpu.SemaphoreType.DMA((3 * len(jobs),)), pltpu.SemaphoreType.DMA((3 * len(jobs),))] + hbm
        + [jax.ShapeDtypeStruct((8, 128), F32)],
        in_specs=[_HBM] * (2 * n), out_specs=[_SEM, _SEM] + [_HBM] * (2 * n) + [pl.BlockSpec(memory_space=pltpu.VMEM)],
        input_output_aliases={i: 2 + i for i in range(2 * n)},
        compiler_params=pltpu.CompilerParams(has_side_effects=_DATAFLOW),
    )(*[pltpu.with_memory_space_constraint(a, pltpu.HBM) for a in srcs + lands])
    return (outs[0], outs[1], list(outs[2:2 + n]), list(outs[2 + n:2 + 2 * n]), nchs, kind), outs[-1]


def _ici_wait(handle, after, name):
    send_sems, recv_sems, srcs, lands, nchs, kind = handle
    n = len(srcs)
    jobs = _ici_jobs(srcs, nchs, kind)

    def body(*refs):
        src, land = refs[:n], refs[n:2 * n]
        for cp in _ici_copies(src, land, refs[2 * n], refs[2 * n + 1], jobs, kind):
            cp.wait_send()
            cp.wait_recv()

    outs = pl.pallas_call(
        body, name=name, out_shape=[pltpu.HBM(a.shape, a.dtype) for a in srcs + lands],
        in_specs=[_HBM] * (2 * n) + [_SEM, _SEM, _ANY], out_specs=[_HBM] * (2 * n),
        input_output_aliases={i: i for i in range(2 * n)},
        compiler_params=pltpu.CompilerParams(has_side_effects=_DATAFLOW),
    )(*srcs, *lands, send_sems, recv_sems, after)
    return list(outs[:n]), list(outs[n:])


def _pass_to_sibling(gathered, nchs, name):
    n = len(gathered)
    jobs = _ici_jobs([jax.ShapeDtypeStruct(g.shape[1:], g.dtype) for g in gathered], nchs, "gather")

    def body(*refs):
        src, dst, (send_sems, recv_sems) = refs[:n], refs[n:2 * n], refs[2 * n:]
        x, y, c = _place()
        slots = [2 * (1 - x) + y, 2 * x + (1 - y), 2 * (1 - x) + (1 - y)]

        def copy(k, j, pc):
            a, h, r0, rows = jobs[k]
            at = pl.ds(pl.multiple_of(pc * h + r0, 16), rows)
            return pltpu.make_async_remote_copy(
                src_ref=src[a].at[slots[j], at, :], dst_ref=dst[a].at[slots[j], at, :], send_sem=send_sems.at[3 * k + j],
                recv_sem=recv_sems.at[3 * k + j], device_id=(x, y, 1 - c), device_id_type=MESH)

        pairs = [(k, j) for k in range(len(jobs)) for j in range(3)]
        for k, j in pairs:
            copy(k, j, c).start()
        for k, j in pairs:
            copy(k, j, c).wait_send()
            copy(k, j, 1 - c).wait_recv()

    return pl.pallas_call(
        body, name=name, out_shape=[jax.ShapeDtypeStruct(g.shape, g.dtype) for g in gathered],
        in_specs=[_ANY] * n, out_specs=[_ANY] * n, input_output_aliases={i: i for i in range(n)},
        scratch_shapes=[pltpu.SemaphoreType.DMA((3 * len(jobs),)), pltpu.SemaphoreType.DMA((3 * len(jobs),))])(*gathered)


def _add4(own, got, name):
    rows, w = own.shape
    tr = _pick(rows, 128, 16)

    def body(a_ref, b_ref, o_ref):
        o_ref[...] = ((a_ref[...].astype(F32) + b_ref[0].astype(F32)) + b_ref[1].astype(F32)) + b_ref[2].astype(F32)

    return pl.pallas_call(
        body, name=name, grid=(rows // tr,),
        in_specs=[pl.BlockSpec((tr, w), lambda i: (i, 0)), pl.BlockSpec((3, tr, w), lambda i: (0, i, 0))],
        out_specs=pl.BlockSpec((tr, w), lambda i: (i, 0)), out_shape=jax.ShapeDtypeStruct((rows, w), F32),
        compiler_params=_params(_PAR))(own, got)


def _small_sync(gs, ws, ms, vs):
    rows = gs.shape[0]
    vmem = pl.BlockSpec(memory_space=pltpu.VMEM)

    def body(g_ref, w_ref, m_ref, v_ref, sum_ref, d_ref, m2_ref, v2_ref, buf, send_sems, recv_sems):
        x, y, c = _place()
        me = 4 * x + 2 * y + c
        buf[me] = g_ref[...]
        copies = []
        for k in range(1, 8):
            peer = (x ^ (k >> 2), y ^ ((k >> 1) & 1), c ^ (k & 1))
            copies.append(pltpu.make_async_remote_copy(
                src_ref=g_ref, dst_ref=buf.at[me], send_sem=send_sems.at[k - 1], recv_sem=recv_sems.at[k - 1],
                device_id=peer, device_id_type=MESH))
        for cp in copies:
            cp.start()
        for cp in copies:
            cp.wait()
        total = buf[0]
        for i in range(1, 8):
            total = total + buf[i]
        sum_ref[...] = total
        d, m2, v2 = _adam_math(w_ref[...], total, m_ref[...], v_ref[...])
        d_ref[...] = d
        m2_ref[...] = m2
        v2_ref[...] = v2

    shape = jax.ShapeDtypeStruct((rows, 128), F32)
    return pl.pallas_call(
        body, name="small_sync", out_shape=[shape] * 4, in_specs=[vmem] * 4, out_specs=[vmem] * 4,
        scratch_shapes=[pltpu.VMEM((8, rows, 128), F32), pltpu.SemaphoreType.DMA((7,)),
                        pltpu.SemaphoreType.DMA((7,))])(gs, ws, ms, vs)


_GROUPS = {
    "ffn1": dict(cols=("ffn1_w_in", 1408), rows=(("ffn1_w_out", 704, 704),), chunks=(8, 2)),
    "ffn2": dict(cols=("ffn2_w_in", 1408), rows=(("ffn2_w_out", 704, 704),), chunks=(8, 2)),
    "mixer": dict(cols=("w_in", 3080), chunks=(8, 4),
                  rows=(("w_branch_hgrn", 256, 256), ("w_branch_gdn", 512, 512), ("w_out", 256, 256),
                        ("gdn_conv_w", CONV_K, 128))),
}
_BIG_NAMES = tuple(n for g in _GROUPS.values() for n in (g["cols"][0],) + tuple(r[0] for r in g["rows"]))


def _group_names(group):
    return (group["cols"][0],) + tuple(r[0] for r in group["rows"])


def _pack(parts, lead, group):
    ax = len(lead)
    rows = []
    for n, r, padded in group["rows"]:
        p = parts[n]
        if padded != r:
            p = jnp.tile(p, (1,) * ax + (padded // r, 1))
        rows.append(p)
    return [parts[group["cols"][0]], rows[0] if len(rows) == 1 else jnp.concatenate(rows, axis=ax)]


def _unpack(cols, rows, group):
    out, off = {group["cols"][0]: cols}, 0
    for n, r, padded in group["rows"]:
        out[n] = rows[..., off:off + r, :]
        off += padded
    return out


def _is_col_sharded(name):
    return name in ("ffn1_w_in", "ffn2_w_in", "w_in", "gdn_conv_w")


def _full_from_shards(name, g):
    if _is_col_sharded(name):
        return jnp.transpose(g, (1, 0, 2)).reshape(g.shape[1], -1)
    return g.reshape(-1, g.shape[2])


def _shards_from_full(name, full):
    if _is_col_sharded(name):
        return jnp.transpose(full.reshape(full.shape[0], 4, -1), (1, 0, 2))
    return full.reshape(4, -1, full.shape[1])


_SMALL = (("ffn1_norm", 8), ("mix_norm", 8), ("hgrn_lb_logits", 16), ("hgrn_out_norm", 8), ("gdn_a_log", 8),
          ("gdn_dt_bias", 8), ("gdn_out_norm", 8), ("ffn2_norm", 8), ("final_norm", 8), ("loss", 8))
_SMALL_ROWS = sum(r for _, r in _SMALL)


def _pack_small(parts):
    out = []
    for name, rows in _SMALL:
        p = parts[name].reshape(-1).astype(F32)
        if p.shape[0] <= 128:
            if p.shape[0] < 128:
                p = jnp.concatenate([p, jnp.zeros((128 - p.shape[0],), F32)])
            p = jnp.broadcast_to(p.reshape(1, 128), (rows, 128))
        out.append(p.reshape(rows, 128))
    return jnp.concatenate(out, axis=0)


def _unpack_small(packed, shapes):
    out, off = {}, 0
    for name, rows in _SMALL:
        n = int(np.prod(shapes[name]))
        out[name] = packed[off:off + rows].reshape(-1)[:n].reshape(shapes[name])
        off += rows
    return out


def _ffn_fwd(x, gain, w_in, w_out, tag):
    n = _rmsnorm_fwd(x, gain, tag + "_norm")
    a, b, hm = _ffn_in_act(n, w_in, tag + "_in")
    out = _mm(hm, w_out, alpha=0.5, res=x, name=tag + "_out")
    return out, (n, a, b)


def _ffn_bwd(x, gain, w_in, w_out, saved, dout, dout_bf, tag):
    n, a, b = saved
    da, db, hm = _ffn_dact(dout_bf, w_out, a, b, tag + "_dact")
    dw_out = _mm(hm, dout_bf, ta=True, alpha=0.5, out_dtype=BF16, name=tag + "_dwout")
    dwa = _mm(n, da, ta=True, out_dtype=BF16, name=tag + "_dwin_a")
    dwb = _mm(n, db, ta=True, out_dtype=BF16, name=tag + "_dwin_b")
    half = D_FF // 2
    dw_in = jnp.stack([dwa[:, :half], dwa[:, half:], dwb[:, :half], dwb[:, half:]])
    dn = _mm(da, w_in, tb=True, name=tag + "_dnorm_a")
    dn = _mm(db, w_in, tb=True, res=dn, b_from=D_FF, name=tag + "_dnorm_b")
    dx, dx_bf, dgain = _rmsnorm_bwd(x, gain, dn, dout, tag + "_dx")
    return dx, dx_bf, dgain, dw_in, dw_out


def _pad_lanes(v):
    return jnp.concatenate([v.reshape(1, -1), jnp.zeros((1, HEAD - v.size), F32)], axis=1)


def _local_step(x, tgt, small, exchange):
    hg_c = _hg_consts()
    gd_c = _gd_consts()
    alog = _pad_lanes(small["gdn_a_log"])
    dtb = _pad_lanes(small["gdn_dt_bias"])
    logits = small["hgrn_lb_logits"]
    hg_gain = small["hgrn_out_norm"].reshape(1, HEAD)
    gd_gain = small["gdn_out_norm"].reshape(1, HEAD)
    g1, gm, g2 = small["ffn1_norm"].reshape(1, -1), small["mix_norm"].reshape(1, -1), small["ffn2_norm"].reshape(1, -1)
    gf = small["final_norm"].reshape(1, -1)
    qscale = HEAD ** -0.5

    w1 = exchange.weights("ffn1")
    started = exchange.prefetch("mixer")
    h1, ffn1_saved = _ffn_fwd(x, g1 + started, w1["ffn1_w_in"], w1["ffn1_w_out"], "ffn1")
    u = _rmsnorm_fwd(h1, gm, "mix_norm")
    w = exchange.weights("mixer", after=u)
    started = exchange.prefetch("ffn2")
    seg, off = {}, 0
    for name, size in zip(IN_NAMES, IN_SIZES):
        seg[name] = w["w_in"][:, off:off + size]
        off += size
    w_gab = jnp.concatenate([seg["ga"], seg["gb"], jnp.zeros((D_MODEL, HEAD - 32), BF16)], axis=1)
    big_segs = [n for n in IN_NAMES if n not in ("ga", "gb")]
    conv8 = jnp.concatenate([w["gdn_conv_w"].astype(F32), jnp.zeros((8 - CONV_K, 4096), F32)], axis=0)
    conv_q, conv_k, conv_v = conv8[:, :1024], conv8[:, 1024:2048], conv8[:, 2048:]
    w_main = jnp.concatenate([seg[n] for n in big_segs], axis=1)
    proj = _mm(u, w_main, name="proj")
    pr, off = {}, 0
    for n in big_segs:
        pr[n] = _view(proj, off, seg[n].shape[1])
        off += seg[n].shape[1]
    gab = _mm(u, w_gab, name="proj_gab")
    oh_raw, oh, s_h = _hgrn_fwd(pr["hq"], pr["hf"], pr["hi"], pr["hg"], logits, hg_gain + started, hg_c)
    qn = _conv_fwd(pr["gq"], conv_q, qscale, "conv_q")
    kn = _conv_fwd(pr["gk"], conv_k, 1.0, "conv_k")
    cv = _conv_fwd(pr["gv"], conv_v, None, "conv_v")
    og_raw, og, s_g, t_g = _gdn_fwd(qn, kn, cv, gab, pr["gz"], alog, dtb, gd_gain, gd_c)
    yh = _mm(oh, w["w_branch_hgrn"], name="branch_h")
    yg = _mm(og, w["w_branch_gdn"], name="branch_g")
    ym = _merge_fwd(yh, yg, pr["gate_h"], pr["gate_g"])
    h2 = _mm(ym, w["w_out"], res=h1, name="mix_out")
    w2 = exchange.weights("ffn2", after=h2)
    h3, ffn2_saved = _ffn_fwd(h2, g2, w2["ffn2_w_in"], w2["ffn2_w_out"], "ffn2")
    loss, dh3, dh3_bf, d_gf = _final_loss(h3, gf, tgt)

    dh2, dh2_bf, d_g2, d_f2in, d_f2out = _ffn_bwd(h2, g2, w2["ffn2_w_in"], w2["ffn2_w_out"], ffn2_saved, dh3, dh3_bf,
                                                  "ffn2")
    started = exchange.reduce("ffn2", {"ffn2_w_in": d_f2in, "ffn2_w_out": d_f2out}, behind=True)
    dym =_mm(dh2_bf, w["w_out"], tb=True, name="d_merge")
    d_wout = _mm(ym, dh2_bf, ta=True, out_dtype=BF16, name="d_w_out")
    dproj = lax.empty((x.shape[0], w_main.shape[1]), BF16)
    dyh, dyg, dproj = _merge_bwd(dym, yh, yg, pr["gate_h"], pr["gate_g"], _into(dproj, pr["gate_h"][1], 2 * D_MODEL))
    d_wbh = _mm(oh, dyh, ta=True, out_dtype=BF16, name="d_w_branch_h")
    d_wbg = _mm(og, dyg, ta=True, out_dtype=BF16, name="d_w_branch_g")
    doh = _mm(dyh, w["w_branch_hgrn"], tb=True, name="d_oh")
    dog = _mm(dyg, w["w_branch_gdn"], tb=True, name="d_og")
    dproj, d_hg_gain, d_lb0 = _hgrn_bwd(pr["hq"], pr["hf"], pr["hi"], pr["hg"], logits, hg_gain + started, oh_raw,
                                        s_h, doh, hg_c, _into(dproj, pr["hq"][1], 4 * D_MODEL))
    d_qn, d_kn, d_cv, d_gab_wide, dproj, gd_small = _gdn_bwd(qn, kn, cv, gab, pr["gz"], alog, dtb, gd_gain, og_raw,
                                                             s_g, t_g, dog, gd_c, _into(dproj, *pr["gz"][1:]))
    d_gab = _fold_groups(d_gab_wide)
    dc_q, dwc_q = _conv_bwd_a(pr["gq"], conv_q, d_qn, qscale, "dconv_q")
    dc_k, dwc_k = _conv_bwd_a(pr["gk"], conv_k, d_kn, 1.0, "dconv_k")
    dc_v, dwc_v = _conv_bwd_a(pr["gv"], conv_v, d_cv, None, "dconv_v")
    dproj = _conv_bwd_b(dc_q, conv_q, "dconvx_q", _into(dproj, *pr["gq"][1:]))
    dproj = _conv_bwd_b(dc_k, conv_k, "dconvx_k", _into(dproj, *pr["gk"][1:]))
    dproj = _conv_bwd_b(dc_v, conv_v, "dconvx_v", _into(dproj, *pr["gv"][1:]))
    du =_mm(d_gab, w_gab, tb=True, name="du_gab")
    du = _mm(dproj, w_main, tb=True, res=du, name="du")
    d_wmain = _mm(u, dproj, ta=True, out_dtype=BF16, name="dw_main")
    d_wgab = _mm(u, d_gab, ta=True, out_dtype=BF16, name="dw_gab")
    cut = IN_WIDTH // 4
    d_win = jnp.stack([d_wmain[:, :cut], d_wmain[:, cut:2 * cut],
                       jnp.concatenate([d_wmain[:, 2 * cut:8192], d_wgab[:, :32], d_wmain[:, 8192:3 * cut - 32]], axis=1),
                       d_wmain[:, 3 * cut - 32:]])
    d_conv = jnp.concatenate([dwc_q[:CONV_K], dwc_k[:CONV_K], dwc_v[:CONV_K]], axis=1).astype(BF16)
    started = exchange.reduce("mixer", {"w_in": d_win, "gdn_conv_w": d_conv, "w_branch_hgrn": d_wbh,
                                        "w_branch_gdn": d_wbg, "w_out": d_wout}, behind=True)
    dh1, dh1_bf, d_gm = _rmsnorm_bwd(h1, gm + started, du, dh2, "mix_dnorm")
    dx, _, d_g1, d_f1in, d_f1out = _ffn_bwd(x, g1, w1["ffn1_w_in"], w1["ffn1_w_out"], ffn1_saved, dh1, dh1_bf, "ffn1")
    exchange.reduce("ffn1", {"ffn1_w_in": d_f1in, "ffn1_w_out": d_f1out}, behind=True)
    d_lb0 = d_lb0.reshape(1, -1)
    sm = {"ffn1_norm": d_g1, "mix_norm": d_gm, "hgrn_lb_logits": jnp.concatenate([d_lb0, -d_lb0], axis=0),
          "hgrn_out_norm": d_hg_gain, "gdn_a_log": gd_small[2, :16], "gdn_dt_bias": gd_small[1, :16],
          "gdn_out_norm": gd_small[0], "ffn2_norm": d_g2, "final_norm": d_gf, "loss": loss[0, :1]}
    return dx, sm


class _Exchange:
    def __init__(self, wts):
        self.wts = wts
        xi, yi, ci = _place()
        self.chip = 2 * xi + yi
        self.south = ci == 0
        self.core = ci.reshape(1).astype(jnp.int32)
        self.mine = {}
        self.coming = {}
        self.going = {}

    def _packs(self, tag):
        group = _GROUPS[tag]
        return _pack({n: self.wts[n][0].astype(BF16) for n in _group_names(group)}, (), group)

    def prefetch(self, tag):
        packs = self._packs(tag)
        handle, token = _ici_start(packs, _GROUPS[tag]["chunks"], "gather", "gather_start_" + tag)
        self.coming[tag] = handle
        return token[0:1, 0:1]

    def weights(self, tag, after=None):
        group = _GROUPS[tag]
        if tag in self.coming:
            packs, halves = _ici_wait(self.coming.pop(tag), after, "gather_wait_" + tag)
            others = _pass_to_sibling(halves, group["chunks"], "gather_pass_" + tag)
        else:
            packs = self._packs(tag)
            others = _gather_weights(packs, group["chunks"], "gather_" + tag)
        whole = [lax.dynamic_update_index_in_dim(g, p, self.chip, 0) for g, p in zip(others, packs)]
        gathered = _unpack(*whole, group)
        return {n: _full_from_shards(n, gathered[n]) for n in _group_names(group)}

    def reduce(self, tag, grads, behind=False):
        group = _GROUPS[tag]
        shards = {n: (grads[n] if grads[n].ndim == 3 else _shards_from_full(n, grads[n])) for n in _group_names(group)}
        gpacks = _pack(shards, (4,), group)
        got = _swap_with_sibling(gpacks, group["chunks"], 4, "reduce_pair_" + tag, halves=True)
        sums = [_add2(a, b, self.core, "add_pair_%s_%d" % (tag, i)) for i, (a, b) in enumerate(zip(gpacks, got))]
        if behind:
            handle, token = _ici_start(sums, group["chunks"], "reduce", "reduce_start_" + tag)
            self.going[tag] = handle
            self.token = token
            return token[0:1, 0:1]
        self._add_chips(tag, sums, _reduce_chips(sums, group["chunks"], "reduce_chips_" + tag))
        return None

    def _add_chips(self, tag, sums, from_chips):
        self.mine[tag] = [_add4(lax.dynamic_index_in_dim(s, self.chip, axis=0, keepdims=False), f,
                                "add_chips_%s_%d" % (tag, i)) for i, (s, f) in enumerate(zip(sums, from_chips))]

    def finish(self, tags, after):
        for tag in tags:
            if tag in self.going:
                self._add_chips(tag, *_ici_wait(self.going.pop(tag), after, "reduce_wait_" + tag))
        mine = [a for t in tags for a in self.mine[t]]
        nchs = [k for t in tags for k in _GROUPS[t]["chunks"]]
        theirs = _swap_with_sibling(mine, nchs, 0, "share_pair_" + tags[0])
        whole = [jnp.concatenate([jnp.where(self.south, a, b), jnp.where(self.south, b, a)], axis=0)
                 for a, b in zip(mine, theirs)]
        reduced = {}
        for i, t in enumerate(tags):
            reduced.update(_unpack(whole[2 * i], whole[2 * i + 1], _GROUPS[t]))
        return reduced


_WEIGHTS = ("ffn1_norm", "ffn1_w_in", "ffn1_w_out", "mix_norm", "w_in", "hgrn_lb_logits", "hgrn_out_norm",
            "gdn_conv_w", "gdn_a_log", "gdn_dt_bias", "gdn_out_norm", "w_branch_hgrn", "w_branch_gdn", "w_out",
            "ffn2_norm", "ffn2_w_in", "ffn2_w_out", "final_norm")


def kernel(x, ffn1_norm, ffn1_w_in, ffn1_w_out, mix_norm, w_in, hgrn_lb_logits, hgrn_out_norm, gdn_conv_w, gdn_a_log, gdn_dt_bias, gdn_out_norm, w_branch_hgrn, w_branch_gdn, w_out, ffn2_norm, ffn2_w_in, ffn2_w_out, final_norm, loss_target, m_ffn1_norm, m_ffn1_w_in, m_ffn1_w_out, m_mix_norm, m_w_in, m_hgrn_lb_logits, m_hgrn_out_norm, m_gdn_conv_w, m_gdn_a_log, m_gdn_dt_bias, m_gdn_out_norm, m_w_branch_hgrn, m_w_branch_gdn, m_w_out, m_ffn2_norm, m_ffn2_w_in, m_ffn2_w_out, m_final_norm, v_ffn1_norm, v_ffn1_w_in, v_ffn1_w_out, v_mix_norm, v_w_in, v_hgrn_lb_logits, v_hgrn_out_norm, v_gdn_conv_w, v_gdn_a_log, v_gdn_dt_bias, v_gdn_out_norm, v_w_branch_hgrn, v_w_branch_gdn, v_w_out, v_ffn2_norm, v_ffn2_w_in, v_ffn2_w_out, v_final_norm):
    args = dict(locals())
    wts = {n: args[n] for n in _WEIGHTS}
    moms = {n: args["m_" + n] for n in _WEIGHTS}
    vars_ = {n: args["v_" + n] for n in _WEIGHTS}

    small = {n: wts[n].astype(F32) for n in _WEIGHTS if n not in _BIG_NAMES}
    exchange = _Exchange(wts)
    dx, small_grads = _local_step(x[0], loss_target[0], small, exchange)

    out_g, out_d, out_m, out_v = {}, {}, {}, {}

    def update(tags, reduced, after):
        for t in tags:
            for n in _group_names(_GROUPS[t]):
                shape = wts[n].shape
                w2 = wts[n].reshape(shape[-2], shape[-1])
                g2 = reduced[n]
                d, m2, v2 = _adamw(w2, g2, moms[n].reshape(w2.shape), vars_[n].reshape(w2.shape), "adamw_" + n, after)
                out_g[n], out_d[n], out_m[n], out_v[n] = (g2.reshape(shape), d.reshape(shape), m2.reshape(shape),
                                                          v2.reshape(shape))
                after = v2
        return after

    done = update(("ffn2", "mixer"), exchange.finish(("ffn2", "mixer"), after=dx), exchange.token)
    update(("ffn1",), exchange.finish(("ffn1",), after=done), None)

    small_names = [n for n, _ in _SMALL]
    zero = jnp.zeros((1,), F32)
    shapes = {n: (wts[n].shape if n != "loss" else (1,)) for n in small_names}
    sums, sd, sm_, sv = _small_sync(
        _pack_small(small_grads),
        _pack_small({n: (wts[n] if n != "loss" else zero) for n in small_names}),
        _pack_small({n: (moms[n] if n != "loss" else zero) for n in small_names}),
        _pack_small({n: (vars_[n] if n != "loss" else zero) for n in small_names}))
    sg_u, sd_u, sm_u, sv_u = (_unpack_small(p, shapes) for p in (sums, sd, sm_, sv))
    for n in small_names:
        if n != "loss":
            out_g[n], out_d[n], out_m[n], out_v[n] = sg_u[n], sd_u[n], sm_u[n], sv_u[n]
    loss = sg_u["loss"].reshape(())

    return (loss, dx[None], *[out_g[n] for n in _WEIGHTS], *[out_d[n] for n in _WEIGHTS],
            *[out_m[n] for n in _WEIGHTS], *[out_v[n] for n in _WEIGHTS])
```

```python
import numpy as np

import jax
import jax.numpy as jnp
from jax import lax
from jax.experimental import pallas as pl
from jax.experimental.pallas import tpu as pltpu

F32 = jnp.float32
BF16 = jnp.bfloat16

D_MODEL = 1024
D_FF = 2816
CHUNK = 64
HEAD = 128
HG_HEADS = 8
GD_HEADS = 16
HPS = 8
MM_TM = 1408
MM_TN = 1024
MM_TK = 1536
VMEM_LIMIT = 48 * 1024 * 1024
ROW_TILE = 512
EPS = 1e-6
CONV_K = 4
IN_NAMES = ("hq", "hf", "hi", "hg", "gq", "gk", "gv", "ga", "gb", "gz", "gate_h", "gate_g")
IN_SIZES = (1024, 1024, 1024, 1024, 1024, 1024, 2048, 16, 16, 2048, 1024, 1024)
IN_WIDTH = sum(IN_SIZES)

ADAM_LR = 0.001
ADAM_B1 = 0.9
ADAM_B2 = 0.999
ADAM_EPS = 1e-08
ADAM_WD = 0.01
ADAM_STEP = 10

MESH = pl.DeviceIdType.MESH
_ARB = "arbitrary"
_PAR = "parallel"


def _bf(x):
    return x.astype(BF16)


def _dot(a, b):
    return jnp.dot(_bf(a), _bf(b), preferred_element_type=F32)


def _dot_nt(a, b):
    return lax.dot_general(_bf(a), _bf(b), (((1,), (1,)), ((), ())), preferred_element_type=F32)


def _dot_tn(a, b):
    return lax.dot_general(_bf(a), _bf(b), (((0,), (0,)), ((), ())), preferred_element_type=F32)


def _sigmoid(x):
    return jax.nn.sigmoid(x)


def _silu(x):
    return x * _sigmoid(x)


def _dsilu(x):
    s = _sigmoid(x)
    return s * (1.0 + x * (1.0 - s))


def _softplus(x):
    return jnp.maximum(x, 0.0) + jnp.log(1.0 + jnp.exp(-jnp.abs(x)))


def _rowsum(x):
    return jnp.sum(x, axis=1, keepdims=True)


def _col_to_row(col, eye):
    return jnp.sum(eye * col, axis=0, keepdims=True)


def _row_to_col(row, eye):
    return jnp.sum(eye * row, axis=1, keepdims=True)


def _pick(dim, pref, unit=128):
    if dim <= pref:
        return dim
    t = pref
    while t >= unit:
        if dim % t == 0:
            return t
        t -= unit
    return dim


def _params(*sem):
    return pltpu.CompilerParams(dimension_semantics=tuple(sem), vmem_limit_bytes=VMEM_LIMIT)


def _mm(a, b, *, ta=False, tb=False, alpha=1.0, res=None, out_dtype=F32, name="mm", b_from=0):
    m = a.shape[1] if ta else a.shape[0]
    k = a.shape[0] if ta else a.shape[1]
    n = b.shape[0] if tb else b.shape[1]
    assert b_from + k <= (b.shape[1] if tb else b.shape[0])
    tm, tn, tk = _pick(m, MM_TM), _pick(n, MM_TN), _pick(k, MM_TK)
    if tn < MM_TN < n and n % MM_TM == 0:
        tn = MM_TM
    nk = k // tk
    assert b_from % tk == 0
    b0 = b_from // tk
    a_spec = pl.BlockSpec((tk, tm), lambda i, j, l: (l, i)) if ta else pl.BlockSpec((tm, tk), lambda i, j, l: (i, l))
    b_spec = (pl.BlockSpec((tn, tk), lambda i, j, l: (j, b0 + l)) if tb
              else pl.BlockSpec((tk, tn), lambda i, j, l: (b0 + l, j)))
    o_spec = pl.BlockSpec((tm, tn), lambda i, j, l: (i, j))
    dims = (((0 if ta else 1,), (1 if tb else 0,)), ((), ()))
    has_res = res is not None

    def finish(r, r_ref, o_ref):
        if alpha != 1.0:
            r = r * alpha
        if has_res:
            r = r + r_ref[...]
        o_ref[...] = r.astype(out_dtype)

    def body(*refs):
        a_ref, b_ref = refs[0], refs[1]
        r_ref = refs[2] if has_res else None
        o_ref = refs[3] if has_res else refs[2]
        part = lax.dot_general(_bf(a_ref[...]), _bf(b_ref[...]), dims, preferred_element_type=F32)
        if nk == 1:
            finish(part, r_ref, o_ref)
            return
        acc = refs[-1]
        step = pl.program_id(2)

        @pl.when(step == 0)
        def _():
            acc[...] = part

        @pl.when(step != 0)
        def _():
            acc[...] += part

        @pl.when(step == nk - 1)
        def _():
            finish(acc[...], r_ref, o_ref)

    ins = [a, b] + ([res] if has_res else [])
    in_specs = [a_spec, b_spec] + ([o_spec] if has_res else [])
    return pl.pallas_call(
        body, name=name, grid=(m // tm, n // tn, nk), in_specs=in_specs, out_specs=o_spec,
        out_shape=jax.ShapeDtypeStruct((m, n), out_dtype),
        scratch_shapes=[pltpu.VMEM((tm, tn), F32)] if nk > 1 else [],
        compiler_params=_params(_PAR, _PAR, _ARB))(*ins)


def _row_spec(tr, w):
    return pl.BlockSpec((tr, w), lambda i: (i, 0))


def _full_spec(shape):
    return pl.BlockSpec(shape, lambda i: tuple(0 for _ in shape))


def _view(arr, off, width):
    return arr, off, width


def _view_rows(view, tr):
    _, off, width = view
    assert off % width == 0
    return pl.BlockSpec((tr, width), lambda i: (i, off // width))


def _view_tile(view, rows, bw, cidx=lambda c: c):
    _, off, width = view
    assert off % bw == 0 and width % bw == 0
    return pl.BlockSpec((rows, bw), lambda c, g: (cidx(c), off // bw + g))


def _rmsnorm_fwd(x, g, name):
    t, d = x.shape
    tr = _pick(t, ROW_TILE, 8)

    def body(x_ref, g_ref, o_ref):
        xv = x_ref[...]
        r = lax.rsqrt(jnp.mean(xv * xv, axis=1, keepdims=True) + EPS)
        o_ref[...] = (xv * r * g_ref[...]).astype(BF16)

    return pl.pallas_call(
        body, name=name, grid=(t // tr,), in_specs=[_row_spec(tr, d), _full_spec((1, d))],
        out_specs=_row_spec(tr, d), out_shape=jax.ShapeDtypeStruct((t, d), BF16),
        compiler_params=_params(_PAR))(x, g)


def _rmsnorm_bwd(x, g, dn, res, name):
    t, d = x.shape
    tr = _pick(t, ROW_TILE, 8)

    def body(x_ref, g_ref, dn_ref, r_ref, dx_ref, dxb_ref, dg_ref):
        @pl.when(pl.program_id(0) == 0)
        def _():
            dg_ref[...] = jnp.zeros_like(dg_ref)

        xv = x_ref[...]
        r = lax.rsqrt(jnp.mean(xv * xv, axis=1, keepdims=True) + EPS)
        xh = xv * r
        dy = dn_ref[...]
        dg_ref[...] += jnp.sum(dy * xh, axis=0, keepdims=True)
        dxh = dy * g_ref[...]
        dx = r_ref[...] + r * (dxh - xh * jnp.mean(dxh * xh, axis=1, keepdims=True))
        dx_ref[...] = dx
        dxb_ref[...] = dx.astype(BF16)

    return pl.pallas_call(
        body, name=name, grid=(t // tr,),
        in_specs=[_row_spec(tr, d), _full_spec((1, d)), _row_spec(tr, d), _row_spec(tr, d)],
        out_specs=[_row_spec(tr, d), _row_spec(tr, d), _full_spec((1, d))],
        out_shape=[jax.ShapeDtypeStruct((t, d), F32), jax.ShapeDtypeStruct((t, d), BF16),
                   jax.ShapeDtypeStruct((1, d), F32)],
        compiler_params=_params(_ARB))(x, g, dn, res)


FFN_TN = 1408
FFN_TM = 512


def _ffn_in_act(n, w_in, name):
    t, d = n.shape
    tm = _pick(t, FFN_TM)
    nf = D_FF // FFN_TN

    def body(n_ref, wa_ref, wb_ref, a_ref, b_ref, hm_ref):
        nv = n_ref[...]
        a = jnp.dot(nv, wa_ref[...], preferred_element_type=F32)
        b = jnp.dot(nv, wb_ref[...], preferred_element_type=F32)
        a_ref[...] = a.astype(BF16)
        b_ref[...] = b.astype(BF16)
        hm_ref[...] = (_silu(a) * b).astype(BF16)

    tile = pl.BlockSpec((tm, FFN_TN), lambda i, j: (i, j))
    return pl.pallas_call(
        body, name=name, grid=(t // tm, nf),
        in_specs=[pl.BlockSpec((tm, d), lambda i, j: (i, 0)), pl.BlockSpec((d, FFN_TN), lambda i, j: (0, j)),
                  pl.BlockSpec((d, FFN_TN), lambda i, j: (0, nf + j))],
        out_specs=[tile, tile, tile], out_shape=[jax.ShapeDtypeStruct((t, D_FF), BF16)] * 3,
        compiler_params=_params(_PAR, _PAR))(n, w_in, w_in)


def _ffn_dact(dout, w_out, a, b, name):
    t, d = dout.shape
    tm = _pick(t, FFN_TM)

    def body(do_ref, w_ref, a_ref, b_ref, da_ref, db_ref, hm_ref):
        dh = 0.5 * _dot_nt(do_ref[...], w_ref[...])
        av = a_ref[...].astype(F32)
        bv = b_ref[...].astype(F32)
        sg = _sigmoid(av)
        sa = av * sg
        da_ref[...] = (dh * bv * (sg * (1.0 + av * (1.0 - sg)))).astype(BF16)
        db_ref[...] = (dh * sa).astype(BF16)
        hm_ref[...] = (sa * bv).astype(BF16)

    tile = pl.BlockSpec((tm, FFN_TN), lambda i, j: (i, j))
    return pl.pallas_call(
        body, name=name, grid=(t // tm, D_FF // FFN_TN),
        in_specs=[pl.BlockSpec((tm, d), lambda i, j: (i, 0)), pl.BlockSpec((FFN_TN, d), lambda i, j: (j, 0)), tile, tile],
        out_specs=[tile, tile, tile], out_shape=[jax.ShapeDtypeStruct((t, D_FF), BF16)] * 3,
        compiler_params=_params(_PAR, _PAR))(dout, w_out, a, b)


def _merge_fwd(yh, yg, gh, gg):
    t, d = yh.shape
    tr = _pick(t, ROW_TILE, 8)

    def body(yh_ref, yg_ref, gh_ref, gg_ref, o_ref):
        o_ref[...] = (_sigmoid(gh_ref[...]) * yh_ref[...] + _sigmoid(gg_ref[...]) * yg_ref[...]).astype(BF16)

    return pl.pallas_call(
        body, name="merge_fwd", grid=(t // tr,),
        in_specs=[_row_spec(tr, d), _row_spec(tr, d), _view_rows(gh, tr), _view_rows(gg, tr)],
        out_specs=_row_spec(tr, d),
        out_shape=jax.ShapeDtypeStruct((t, d), BF16), compiler_params=_params(_PAR))(yh, yg, gh[0], gg[0])


def _into(dproj, off, width):
    return dproj, off, width


def _merge_bwd(dy, yh, yg, gh, gg, into):
    t, d = yh.shape
    tr = _pick(t, ROW_TILE, 8)
    dproj, off, width = into
    assert width == 2 * d and off % width == 0

    def body(dy_ref, yh_ref, yg_ref, gh_ref, gg_ref, _, dyh_ref, dyg_ref, dg_ref):
        dyv = dy_ref[...]
        sh = _sigmoid(gh_ref[...])
        sg = _sigmoid(gg_ref[...])
        dyh_ref[...] = (dyv * sh).astype(BF16)
        dyg_ref[...] = (dyv * sg).astype(BF16)
        dg_ref[:, :d] = (dyv * yh_ref[...] * sh * (1.0 - sh)).astype(BF16)
        dg_ref[:, d:] = (dyv * yg_ref[...] * sg * (1.0 - sg)).astype(BF16)

    return pl.pallas_call(
        body, name="merge_bwd", grid=(t // tr,),
        in_specs=[_row_spec(tr, d)] * 3 + [_view_rows(gh, tr), _view_rows(gg, tr), _ANY],
        out_specs=[_row_spec(tr, d)] * 2 + [pl.BlockSpec((tr, width), lambda i: (i, off // width))],
        out_shape=[jax.ShapeDtypeStruct((t, d), BF16)] * 2 + [jax.ShapeDtypeStruct(dproj.shape, dproj.dtype)],
        input_output_aliases={5: 2},
        compiler_params=_params(_PAR))(dy, yh, yg, gh[0], gg[0], dproj)


def _final_loss(h, g, tgt):
    t, d = h.shape
    tr = _pick(t, ROW_TILE, 8)

    def body(h_ref, g_ref, t_ref, loss_ref, dh_ref, dhb_ref, dg_ref):
        @pl.when(pl.program_id(0) == 0)
        def _():
            dg_ref[...] = jnp.zeros_like(dg_ref)
            loss_ref[...] = jnp.zeros_like(loss_ref)

        xv = h_ref[...]
        gv = g_ref[...]
        r = lax.rsqrt(jnp.mean(xv * xv, axis=1, keepdims=True) + EPS)
        xh = xv * r
        err = xh * gv - t_ref[...]
        loss_ref[...] += 0.5 * jnp.sum(jnp.mean(err * err, axis=1, keepdims=True), axis=0, keepdims=True)
        dy = err * (1.0 / d)
        dg_ref[...] += jnp.sum(dy * xh, axis=0, keepdims=True)
        dxh = dy * gv
        dh = r * (dxh - xh * jnp.mean(dxh * xh, axis=1, keepdims=True))
        dh_ref[...] = dh
        dhb_ref[...] = dh.astype(BF16)

    return pl.pallas_call(
        body, name="final_loss", grid=(t // tr,),
        in_specs=[_row_spec(tr, d), _full_spec((1, d)), _row_spec(tr, d)],
        out_specs=[_full_spec((1, 128)), _row_spec(tr, d), _row_spec(tr, d), _full_spec((1, d))],
        out_shape=[jax.ShapeDtypeStruct((1, 128), F32), jax.ShapeDtypeStruct((t, d), F32),
                   jax.ShapeDtypeStruct((t, d), BF16), jax.ShapeDtypeStruct((1, d), F32)],
        compiler_params=_params(_ARB))(h, g, tgt)


def _hg_consts():
    c = CHUNK
    t = np.arange(c)
    mats, masks = [], []
    for lvl in range(6):
        m = 1 << lvl
        blk = t // m
        mat = np.zeros((c, c), np.float32)
        for tt in range(c):
            b = blk[tt]
            if b % 2 == 1:
                mat[tt, b * m:tt + 1] = 1.0
            else:
                mat[tt, tt + 1:(b + 1) * m] = 1.0
        mats.append(mat)
        same = (t[:, None] // (2 * m)) == (t[None, :] // (2 * m))
        masks.append((same & (blk[:, None] % 2 == 1) & (blk[None, :] % 2 == 0)).astype(np.float32))
    pre = np.tril(np.ones((c, c), np.float32))
    suf = np.triu(np.ones((c, c), np.float32), 1)
    mstack = np.concatenate(mats + [pre, suf], 0)
    masks.append(np.eye(c, dtype=np.float32))
    return (jnp.asarray(mstack, BF16), jnp.asarray(mstack.T.copy(), BF16), jnp.asarray(np.stack(masks), F32),
            jnp.asarray(np.eye(HEAD, dtype=np.float32)))


def _gd_consts():
    c = CHUNK
    incl = np.tril(np.ones((c, c), np.float32))
    strict = np.tril(np.ones((c, c), np.float32), -1)
    eye = np.eye(c, dtype=np.float32)
    masks = np.stack([incl, strict, eye, incl.T.copy()])
    return jnp.asarray(incl, BF16), jnp.asarray(incl.T.copy(), BF16), jnp.asarray(masks, F32)


def _chunks_per_step(nc):
    for cb in (32 // HPS, 2, 1):
        if nc % cb == 0:
            return cb
    return 1


def _hg_prep(hq, hf, lg):
    lb = _sigmoid(lg[0:1, :] - lg[1:2, :])
    sg = _sigmoid(hf)
    sgn = _sigmoid(-hf)
    f = lb + (1.0 - lb) * sg
    lf = jnp.log(f)
    kk = (1.0 - lb) * sgn
    q = _silu(hq) * (HEAD ** -0.5)
    return lb, sg, sgn, f, lf, kk, q


def _mx_each(m, xs):
    hi, lo = _split2_each(xs)
    prods = [jnp.dot(m, jnp.concatenate([h, l], axis=1), preferred_element_type=F32) for h, l in zip(hi, lo)]
    return [p[:, :HEAD] + p[:, HEAD:] for p in prods]


def _hg_scaled(x, ex):
    xb = [_bf(a) for a in x]
    eb = [_bf(e[:6 * CHUNK]) for e in ex]
    return [[a * e[lvl * CHUNK:(lvl + 1) * CHUNK] for lvl in range(6)] for a, e in zip(xb, eb)]


def _hg_scores(q, kk, qe, ke, mask_ref):
    p = [mask_ref[6] * _rowsum(a * b) for a, b in zip(q, kk)]
    for lvl in range(6):
        d = [_dot_nt(a[lvl], b[lvl]) for a, b in zip(qe, ke)]
        p = [x + mask_ref[lvl] * y for x, y in zip(p, d)]
    return p


def _hgrn_fwd(hq, hf, hi, hg, logits, gain, consts):
    t = hq[0].shape[0]
    nc = t // CHUNK
    cb = _chunks_per_step(nc)
    rows = cb * CHUNK
    mstack, _, masks, eye = consts
    tile = pl.BlockSpec((rows, HPS * HEAD), lambda c, g: (c, g))

    def body(hq_ref, hf_ref, hi_ref, hg_ref, lg_ref, gain_ref, m_ref, mask_ref, eye_ref,
             oraw_ref, og_ref, ssave_ref, state):
        c = pl.program_id(0)
        g = pl.program_id(1)

        @pl.when(c == 0)
        def _():
            for hh in range(HPS):
                state[g * HPS + hh] = jnp.zeros((HEAD, HEAD), F32)

        lg_all = lg_ref[...]
        gain_v = gain_ref[...]

        def one(i, carry):
            sl = pl.ds(pl.multiple_of(i * CHUNK, CHUNK), CHUNK)
            hs = range(HPS)
            heads = [g * HPS + hh for hh in hs]
            ln = [slice(hh * HEAD, (hh + 1) * HEAD) for hh in hs]
            preps = [_hg_prep(hq_ref[sl, s], hf_ref[sl, s], lg_all[:, s]) for s in ln]
            lf, kk, q = [p[4] for p in preps], [p[5] for p in preps], [p[6] for p in preps]
            v = [hi_ref[sl, s] for s in ln]
            ex = [jnp.exp(x) for x in _mx_each(m_ref[...], lf)]
            eb = [e[6 * CHUNK:7 * CHUNK] for e in ex]
            esfx = [e[7 * CHUNK:8 * CHUNK] for e in ex]
            qe, ke = _hg_scaled(q, ex), _hg_scaled(kk, ex)
            p = _hg_scores(q, kk, qe, ke, mask_ref)
            s0 = [state[h] for h in heads]
            o = _each(lambda a, e, s, pp, vv: _dot(a * e, s) + _dot(pp, vv), q, eb, s0, p, v)
            eye_v = eye_ref[...]
            s1 = _each(lambda s, e, kx, ef, vv: s * _row_to_col(e[CHUNK - 1:CHUNK, :], eye_v) + _dot_tn(kx * ef, vv),
                       s0, eb, kk, esfx, v)
            for hh in hs:
                ssave_ref[i, hh] = s0[hh]
                state[heads[hh]] = s1[hh]
                oraw_ref[sl, ln[hh]] = o[hh]
                r = lax.rsqrt(jnp.mean(o[hh] * o[hh], axis=1, keepdims=True) + EPS)
                og_ref[sl, ln[hh]] = (o[hh] * r * gain_v * _silu(hg_ref[sl, ln[hh]])).astype(BF16)
            return carry

        lax.fori_loop(0, cb, one, 0, unroll=4)

    return pl.pallas_call(
        body, name="hgrn_fwd", grid=(nc // cb, HG_HEADS // HPS),
        in_specs=[_view_tile(v, rows, HPS * HEAD) for v in (hq, hf, hi, hg)] + [
                  pl.BlockSpec((2, HPS * HEAD), lambda c, g: (0, g)),
                  pl.BlockSpec((1, HEAD), lambda c, g: (0, 0)),
                  pl.BlockSpec(mstack.shape, lambda c, g: (0, 0)),
                  pl.BlockSpec(masks.shape, lambda c, g: (0, 0, 0)),
                  pl.BlockSpec(eye.shape, lambda c, g: (0, 0))],
        out_specs=[tile, tile, pl.BlockSpec((cb, HPS, HEAD, HEAD), lambda c, g: (c, g, 0, 0))],
        out_shape=[jax.ShapeDtypeStruct((t, HG_HEADS * HEAD), F32), jax.ShapeDtypeStruct((t, HG_HEADS * HEAD), BF16),
                   jax.ShapeDtypeStruct((nc, HG_HEADS, HEAD, HEAD), F32)],
        scratch_shapes=[pltpu.VMEM((HG_HEADS, HEAD, HEAD), F32)],
        compiler_params=_params(_ARB, _ARB))(hq[0], hf[0], hi[0], hg[0], logits, gain, mstack, masks, eye)


def _hgrn_bwd(hq, hf, hi, hg, logits, gain, oraw, ssave, dog, consts, into):
    t = hq[0].shape[0]
    dproj, off, width = into
    seg = HG_HEADS * HEAD
    assert HPS == HG_HEADS and width == 4 * seg and off % width == 0
    nc = t // CHUNK
    cb = _chunks_per_step(nc)
    rows = cb * CHUNK
    nb = nc // cb
    mstack, mstack_t, masks, eye = consts
    tile = pl.BlockSpec((rows, HPS * HEAD), lambda c, g: (nb - 1 - c, g))

    def body(hq_ref, hf_ref, hi_ref, hg_ref, lg_ref, gain_ref, oraw_ref, ssave_ref, dog_ref, m_ref, mt_ref,
             mask_ref, eye_ref, _, d_ref, dgain_ref, dlb_ref, dstate):
        c = pl.program_id(0)
        g = pl.program_id(1)

        @pl.when(c == 0)
        def _():
            for hh in range(HPS):
                dstate[g * HPS + hh] = jnp.zeros((HEAD, HEAD), F32)

        @pl.when((c == 0) & (g == 0))
        def _():
            dgain_ref[...] = jnp.zeros_like(dgain_ref)
            dlb_ref[...] = jnp.zeros_like(dlb_ref)

        lg_all = lg_ref[...]
        gain_v = gain_ref[...]
        eye_v = eye_ref[...]
        last_row = (lax.broadcasted_iota(jnp.int32, (CHUNK, HEAD), 0) == CHUNK - 1).astype(F32)

        def one(j, carry):
            i = cb - 1 - j
            sl = pl.ds(pl.multiple_of(i * CHUNK, CHUNK), CHUNK)
            hs = range(HPS)
            heads = [g * HPS + hh for hh in hs]
            ln = [slice(hh * HEAD, (hh + 1) * HEAD) for hh in hs]
            hqv = [hq_ref[sl, s] for s in ln]
            hgv = [hg_ref[sl, s] for s in ln]
            preps = [_hg_prep(a, hf_ref[sl, s], lg_all[:, s]) for a, s in zip(hqv, ln)]
            lb, sg, sgn, f, lf, kk, q = ([p[n] for p in preps] for n in range(7))
            v = [hi_ref[sl, s] for s in ln]
            ex = [jnp.exp(x) for x in _mx_each(m_ref[...], lf)]
            eb = [e[6 * CHUNK:7 * CHUNK] for e in ex]
            esfx = [e[7 * CHUNK:8 * CHUNK] for e in ex]
            qe, ke = _hg_scaled(q, ex), _hg_scaled(kk, ex)
            p = _hg_scores(q, kk, qe, ke, mask_ref)
            s0 = [ssave_ref[i, hh] for hh in hs]
            ds = [dstate[h] for h in heads]

            o = [oraw_ref[sl, s] for s in ln]
            r = [lax.rsqrt(jnp.mean(x * x, axis=1, keepdims=True) + EPS) for x in o]
            on = _each(lambda x, y: x * y, o, r)
            dg_out = [dog_ref[sl, s] for s in ln]
            sgate = [_silu(x) for x in hgv]
            for hh in hs:
                d_ref[sl, slice(3 * seg + hh * HEAD, 3 * seg + (hh + 1) * HEAD)] =(dg_out[hh] * on[hh] * gain_v * _dsilu(hgv[hh])).astype(BF16)
            dgain_ref[...] += sum(jnp.sum(d * s * n, axis=0, keepdims=True) for d, s, n in zip(dg_out, sgate, on))
            don = _each(lambda d, s: d * s * gain_v, dg_out, sgate)
            do = _each(lambda rr, dn, n: rr * (dn - n * jnp.mean(dn * n, axis=1, keepdims=True)), r, don, on)

            dp = _each(_dot_nt, do, v)
            dv = _each(lambda pp, d, kx, ef, s: _dot_tn(pp, d) + _dot(kx * ef, s), p, do, kk, esfx, ds)
            dqb = _each(_dot_nt, do, s0)
            dkx = _each(_dot_nt, v, ds)
            diag = [_rowsum(mask_ref[6] * x) for x in dp]
            dq = _each(lambda a, e, d, kx: a * e + d * kx, dqb, eb, diag, kk)
            dk = _each(lambda a, e, d, qq: a * e + d * qq, dkx, esfx, diag, q)
            dxs = [[] for _ in hs]
            for lvl in range(6):
                el = [e[lvl * CHUNK:(lvl + 1) * CHUNK] for e in ex]
                gm = [mask_ref[lvl] * x for x in dp]
                gm = [_bf(x) for x in gm]
                a1 = _each(lambda m_, kx: _dot(m_, kx[lvl]), gm, ke)
                a2 = _each(lambda m_, qq: _dot_tn(m_, qq[lvl]), gm, qe)
                dq = _each(lambda x, a, e: x + a * e, dq, a1, el)
                dk = _each(lambda x, a, e: x + a * e, dk, a2, el)
                for hh in hs:
                    dxs[hh].append((a1[hh] * q[hh] + a2[hh] * kk[hh]) * el[hh])
            e_end_row = [e[CHUNK - 1:CHUNK, :] for e in eb]
            ds_new = _each(lambda qq, e, d, er, s: _dot_tn(qq * e, d) + _row_to_col(er, eye_v) * s, q, eb, do, e_end_row, ds)
            for hh in hs:
                dstate[heads[hh]] = ds_new[hh]
                dend_row = _col_to_row(_rowsum(s0[hh] * ds[hh]), eye_v)
                dxs[hh].append(dqb[hh] * q[hh] * eb[hh] + last_row * (e_end_row[hh] * dend_row))
                dxs[hh].append(dkx[hh] * kk[hh] * esfx[hh])
            dlf = _mx_each(mt_ref[...], [jnp.concatenate(x, axis=0) for x in dxs])

            for hh in hs:
                d_ref[sl, slice(2 * seg + hh * HEAD, 2 * seg + (hh + 1) * HEAD)] =dv[hh].astype(BF16)
                d_ref[sl, ln[hh]] =(dq[hh] * (HEAD ** -0.5) * _dsilu(hqv[hh])).astype(BF16)
                df = dlf[hh] / f[hh]
                dsig = (1.0 - lb[hh]) * sg[hh] * sgn[hh]
                d_ref[sl, slice(seg + hh * HEAD, seg + (hh + 1) * HEAD)] =((df - dk[hh]) * dsig).astype(BF16)
                dlb_t = jnp.sum(df * sgn[hh] - dk[hh] * sgn[hh], axis=0, keepdims=True)
                dlb_ref[pl.ds(heads[hh], 1), :] += dlb_t * lb[hh] * (1.0 - lb[hh])
            return carry

        lax.fori_loop(0, cb, one, 0, unroll=2)

    outs = [jax.ShapeDtypeStruct(dproj.shape, dproj.dtype),
            jax.ShapeDtypeStruct((1, HEAD), F32), jax.ShapeDtypeStruct((HG_HEADS, HEAD), F32)]
    return pl.pallas_call(
        body, name="hgrn_bwd", grid=(nb, HG_HEADS // HPS),
        in_specs=[_view_tile(v, rows, HPS * HEAD, lambda c: nb - 1 - c) for v in (hq, hf, hi, hg)] + [
                  pl.BlockSpec((2, HPS * HEAD), lambda c, g: (0, g)),
                  pl.BlockSpec((1, HEAD), lambda c, g: (0, 0)), tile,
                  pl.BlockSpec((cb, HPS, HEAD, HEAD), lambda c, g: (nb - 1 - c, g, 0, 0)), tile,
                  pl.BlockSpec(mstack.shape, lambda c, h: (0, 0)),
                  pl.BlockSpec(mstack_t.shape, lambda c, h: (0, 0)),
                  pl.BlockSpec(masks.shape, lambda c, h: (0, 0, 0)),
                  pl.BlockSpec(eye.shape, lambda c, h: (0, 0)), _ANY],
        out_specs=[pl.BlockSpec((rows, width), lambda c, h: (nb - 1 - c, off // width)),
                   pl.BlockSpec((1, HEAD), lambda c, h: (0, 0)),
                   pl.BlockSpec((HG_HEADS, HEAD), lambda c, h: (0, 0))],
        out_shape=outs, scratch_shapes=[pltpu.VMEM((HG_HEADS, HEAD, HEAD), F32)], input_output_aliases={13: 0},
        compiler_params=_params(_ARB, _ARB))(hq[0], hf[0], hi[0], hg[0], logits, gain, oraw, ssave, dog, mstack,
                                             mstack_t, masks, eye, dproj)


CONV_W = 512
CONV_ROWS = 1024


def _per_head(fn, *arrs):
    width = arrs[0].shape[1]
    return jnp.concatenate([fn(*[a[:, j:j + HEAD] for a in arrs]) for j in range(0, width, HEAD)], axis=1)


def _shift_down(xv, halo, d, top_rows):
    if d == 0:
        return xv, xv[0:8]
    main = pltpu.roll(xv, d, 0)
    top = jnp.where(top_rows < d, pltpu.roll(halo, d, 0), main[0:8])
    return main, top


def _conv_parts(x_ref, halo_ref, w_ref, first):
    xv = x_ref[...]
    halo = jnp.where(first, 0.0, halo_ref[...])
    top_rows = lax.broadcasted_iota(jnp.int32, (8, xv.shape[1]), 0)
    shifted = [_shift_down(xv, halo, CONV_K - 1 - j, top_rows) for j in range(CONV_K)]
    w = w_ref[...]
    acc = sum(shifted[j][0] * w[j:j + 1, :] for j in range(CONV_K))
    acc_top = sum(shifted[j][1] * w[j:j + 1, :] for j in range(CONV_K))
    return shifted, acc, acc_top


def _conv_fwd(x, w8, l2scale, name):
    x, off, width = x
    t = x.shape[0]
    o = off // CONV_W
    tr = _pick(t, CONV_ROWS, 8)

    def post(cv):
        s = _silu(cv)
        if l2scale is not None:
            s = _per_head(lambda sh: sh * (lax.rsqrt(_rowsum(sh * sh) + EPS) * l2scale), s)
        return s

    def body(x_ref, halo_ref, w_ref, o_ref):
        _, acc, acc_top = _conv_parts(x_ref, halo_ref, w_ref, pl.program_id(1) == 0)
        o_ref[...] = post(acc)
        o_ref[0:8, :] = post(acc_top)

    return pl.pallas_call(
        body, name=name, grid=(width // CONV_W,t // tr),
        in_specs=[pl.BlockSpec((tr, CONV_W), lambda j, i: (i, o + j)),
                  pl.BlockSpec((8, CONV_W), lambda j, i: (jnp.maximum(i * (tr // 8) - 1, 0), o + j)),
                  pl.BlockSpec((8, CONV_W), lambda j, i: (0, j))],
        out_specs=pl.BlockSpec((tr, CONV_W), lambda j, i: (i, j)),
        out_shape=jax.ShapeDtypeStruct((t, width), F32), compiler_params=_params(_PAR, _PAR))(x, x, w8)


def _conv_bwd_a(x, w8, dy, l2scale, name):
    x, off, width = x
    t = x.shape[0]
    o = off // CONV_W
    tr = _pick(t, CONV_ROWS, 8)

    def l2_bwd(s, dyh):
        r = lax.rsqrt(_rowsum(s * s) + EPS)
        y0 = s * r
        dy0 = dyh * l2scale
        return r * (dy0 - y0 * _rowsum(dy0 * y0))

    def to_dc(cv, dyv):
        if l2scale is not None:
            dyv = _per_head(l2_bwd, _silu(cv), dyv)
        return dyv * _dsilu(cv)

    def body(x_ref, halo_ref, w_ref, dy_ref, dc_ref, dw_ref):
        @pl.when(pl.program_id(1) == 0)
        def _():
            dw_ref[...] = jnp.zeros_like(dw_ref)

        shifted, acc, acc_top = _conv_parts(x_ref, halo_ref, w_ref, pl.program_id(1) == 0)
        dyv = dy_ref[...]
        dc = to_dc(acc, dyv)
        dc_top = to_dc(acc_top, dyv[0:8])
        dc_ref[...] = dc
        dc_ref[0:8, :] = dc_top
        rest = (lax.broadcasted_iota(jnp.int32, dc.shape, 0) >= 8).astype(F32)
        dc_rest = dc * rest
        for j in range(CONV_K):
            dw_ref[j:j + 1, :] += (jnp.sum(dc_rest * shifted[j][0], axis=0, keepdims=True)
                                   + jnp.sum(dc_top * shifted[j][1], axis=0, keepdims=True))

    return pl.pallas_call(
        body, name=name, grid=(width // CONV_W,t // tr),
        in_specs=[pl.BlockSpec((tr, CONV_W), lambda j, i: (i, o + j)),
                  pl.BlockSpec((8, CONV_W), lambda j, i: (jnp.maximum(i * (tr // 8) - 1, 0), o + j)),
                  pl.BlockSpec((8, CONV_W), lambda j, i: (0, j)),
                  pl.BlockSpec((tr, CONV_W), lambda j, i: (i, j))],
        out_specs=[pl.BlockSpec((tr, CONV_W), lambda j, i: (i, j)), pl.BlockSpec((8, CONV_W), lambda j, i: (0, j))],
        out_shape=[jax.ShapeDtypeStruct((t, width), F32), jax.ShapeDtypeStruct((8, width), F32)],
        compiler_params=_params(_PAR, _ARB))(x, x, w8, dy)


def _conv_bwd_b(dc, w8, name, into):
    t, width = dc.shape
    tr = _pick(t, CONV_ROWS, 8)
    nt = t // tr

    dproj, off, into_width = into
    assert into_width == width and off % CONV_W == 0
    o = off // CONV_W

    def body(dc_ref, halo_ref, w_ref, _, dx_ref):
        dcv = dc_ref[...]
        halo = jnp.where(pl.program_id(1) == nt - 1, 0.0, halo_ref[...])
        w = w_ref[...]
        bot_rows = lax.broadcasted_iota(jnp.int32, (8, CONV_W), 0)
        acc = dcv * w[CONV_K - 1:CONV_K, :]
        acc_bot = dcv[tr - 8:tr] * w[CONV_K - 1:CONV_K, :]
        for d in range(1, CONV_K):
            main = pltpu.roll(dcv, tr - d, 0)
            bot = jnp.where(bot_rows >= 8 - d, pltpu.roll(halo, 8 - d, 0), main[tr - 8:tr])
            wj = w[CONV_K - 1 - d:CONV_K - d, :]
            acc = acc + main * wj
            acc_bot = acc_bot + bot * wj
        dx_ref[...] = acc.astype(BF16)
        dx_ref[tr - 16:tr, :] = jnp.concatenate([acc[tr - 16:tr - 8], acc_bot], axis=0).astype(BF16)

    return pl.pallas_call(
        body, name=name, grid=(width // CONV_W,nt),
        in_specs=[pl.BlockSpec((tr, CONV_W), lambda j, i: (i, j)),
                  pl.BlockSpec((8, CONV_W), lambda j, i: (jnp.minimum((i + 1) * (tr // 8), t // 8 - 1), j)),
                  pl.BlockSpec((8, CONV_W), lambda j, i: (0, j)), _ANY],
        out_specs=pl.BlockSpec((tr, CONV_W), lambda j, i: (i, o + j)),
        out_shape=jax.ShapeDtypeStruct(dproj.shape, dproj.dtype), input_output_aliases={3: 0},
        compiler_params=_params(_PAR, _PAR))(dc, dc, w8, dproj)


def _each(f, *lists):
    return [f(*xs) for xs in zip(*lists)]


def _split2_each(xs):
    hi = [_bf(x) for x in xs]
    lo = [_bf(x - h.astype(F32)) for x, h in zip(xs, hi)]
    return hi, lo


def _hp_each(a_split, b_split):
    (ah, al), (bh, bl) = a_split, b_split
    rows = ah[0].shape[0]
    d12 = [jnp.dot(jnp.concatenate([x, y], axis=0), z, preferred_element_type=F32) for x, y, z in zip(ah, al, bh)]
    d3 = [jnp.dot(x, y, preferred_element_type=F32) for x, y in zip(ah, bl)]
    return [d[:rows] + d[rows:] + e for d, e in zip(d12, d3)]


INV_EXACT_STEPS = 2


def _tri_inv_each(a_list, eye):
    ns = [-a for a in a_list]
    ps = [eye + n for n in ns]
    n_split = _split2_each(ns)
    for step in range(5):
        if step < INV_EXACT_STEPS:
            ns = _hp_each(n_split, n_split)
            n_split = _split2_each(ns)
            ps = [p + d for p, d in zip(ps, _hp_each(_split2_each(ps), n_split))]
        else:
            nb = n_split[0] if step == INV_EXACT_STEPS else [_bf(n) for n in ns]
            ns = [jnp.dot(x, x, preferred_element_type=F32) for x in nb]
            nb2 = [_bf(n) for n in ns]
            ps = [p + jnp.dot(_bf(p), y, preferred_element_type=F32) for p, y in zip(ps, nb2)]
    return ps


def _gd_gates(gab, alog, dtb):
    sp_arg = gab + dtb
    return sp_arg, -jnp.exp(alog) * _softplus(sp_arg), _sigmoid(gab)


def _pick_lane(tile, base, head):
    g, hh = head
    col = tile[:, base + hh:base + hh + 1]
    for gi in range(1, GD_HEADS // HPS):
        lane = base + gi * HPS + hh
        col = jnp.where(g == gi, tile[:, lane:lane + 1], col)
    return col


def _gd_chunks(q, k, v, g_all, beta_all, heads, l_ref, mask_ref, tm=None):
    incl, strict, eye, upper = mask_ref[0], mask_ref[1], mask_ref[2], mask_ref[3]
    lmat = l_ref[...]
    gb = [jnp.broadcast_to(_pick_lane(g_all, 0, s), (CHUNK, HEAD)) for s in heads]
    bb = [jnp.broadcast_to(_pick_lane(beta_all, GD_HEADS, s), (CHUNK, HEAD)) for s in heads]
    gam = _mx_each(lmat, gb)
    gam_row = [jnp.sum(x[:, :CHUNK] * upper, axis=0, keepdims=True) for x in gb]
    lm = _each(lambda gm, gr: incl * jnp.exp(jnp.minimum(gm[:, :CHUNK] - gr, 0.0)), gam, gam_row)
    kb = _each(lambda x, b: x * b, k, bb)
    a = _each(lambda x, y, m: strict * _dot_nt(x, y) * m, kb, k, lm)
    if tm is None:
        tm = _tri_inv_each(a, eye)
    eg = [jnp.exp(x) for x in gam]
    vb = _each(lambda x, b: x * b, v, bb)
    kbg = _each(lambda x, e: x * e, kb, eg)
    uw = _each(lambda t_, x, y: _dot(t_, jnp.concatenate([x, y], axis=1)), tm, vb, kbg)
    u = [x[:, :HEAD] for x in uw]
    w = [x[:, HEAD:] for x in uw]
    qk = _each(lambda x, y, m: _dot_nt(x, y) * m, q, k, lm)
    g_end = [x[CHUNK - 1:CHUNK, :] for x in gam]
    ekg = _each(lambda e, x: jnp.exp(e - x), g_end, gam)
    ge = [jnp.exp(e) for e in g_end]
    kg = _each(lambda x, e: x * e, k, ekg)
    qg = _each(lambda x, e: x * e, q, eg)
    names = ("bb", "lm", "kb", "a", "tm", "eg", "vb", "kbg", "u", "w", "qk", "ekg", "ge", "kg", "qg")
    cols = (bb, lm, kb, a, tm, eg, vb, kbg, u, w, qk, ekg, ge, kg, qg)
    return [dict(zip(names, vals)) for vals in zip(*cols)]


def _gd_specs(rows, rev_nb=None):
    def cidx(c):
        return c if rev_nb is None else rev_nb - 1 - c

    qk_tile = pl.BlockSpec((rows, HPS // 2 * HEAD), lambda c, g: (cidx(c), g))
    v_tile = pl.BlockSpec((rows, HPS * HEAD), lambda c, g: (cidx(c), g))
    gab_tile = pl.BlockSpec((rows, HEAD), lambda c, g: (cidx(c), 0))
    return qk_tile, v_tile, gab_tile


def _gdn_fwd(qn, kn, cv, gab, gz, alog, dtb, gain, consts):
    t = qn.shape[0]
    nc = t // CHUNK
    cb = _chunks_per_step(nc)
    rows = cb * CHUNK
    lmat, _, masks = consts
    qk_tile, v_tile, gab_tile = _gd_specs(rows)
    row128 = pl.BlockSpec((1, HEAD), lambda c, h: (0, 0))

    def body(q_ref, k_ref, v_ref, gab_ref, gz_ref, alog_ref, dtb_ref, gain_ref, l_ref, mask_ref,
             oraw_ref, og_ref, ssave_ref, tsave_ref, state):
        c = pl.program_id(0)
        g = pl.program_id(1)

        @pl.when(c == 0)
        def _():
            for hh in range(HPS):
                state[g * HPS + hh] = jnp.zeros((HEAD, HEAD), F32)

        alog = alog_ref[...]
        dtb = dtb_ref[...]
        gain_v = gain_ref[...]

        def one(i, carry):
            sl = pl.ds(pl.multiple_of(i * CHUNK, CHUNK), CHUNK)
            _, g_all, beta_all = _gd_gates(gab_ref[sl, :], alog, dtb)
            heads = [g * HPS + hh for hh in range(HPS)]
            lq = [slice(hh // 2 * HEAD, (hh // 2 + 1) * HEAD) for hh in range(HPS)]
            lv = [slice(hh * HEAD, (hh + 1) * HEAD) for hh in range(HPS)]
            chs = _gd_chunks([q_ref[sl, s] for s in lq], [k_ref[sl, s] for s in lq], [v_ref[sl, s] for s in lv],
                             g_all, beta_all, [(g, hh) for hh in range(HPS)], l_ref, mask_ref)
            s0 = [state[h] for h in heads]
            ws = _each(lambda ch, s: _dot(jnp.concatenate([ch["w"], ch["qg"]], axis=0), s), chs, s0)
            v_new = _each(lambda ch, x: ch["u"] - x[:CHUNK], chs, ws)
            o = _each(lambda ch, x, vn: x[CHUNK:] + _dot(ch["qk"], vn), chs, ws, v_new)
            s1 = _each(lambda ch, s, vn: s * ch["ge"] + _dot_tn(ch["kg"], vn), chs, s0, v_new)
            for hh in range(HPS):
                ssave_ref[i, hh] = s0[hh]
                tsave_ref[i, hh] = chs[hh]["tm"]
                state[heads[hh]] = s1[hh]
                oraw_ref[sl, lv[hh]] = o[hh]
                r = lax.rsqrt(jnp.mean(o[hh] * o[hh], axis=1, keepdims=True) + EPS)
                og_ref[sl, lv[hh]] = (o[hh] * r * gain_v * _silu(gz_ref[sl, lv[hh]])).astype(BF16)
            return carry

        lax.fori_loop(0, cb, one, 0, unroll=4)

    return pl.pallas_call(
        body, name="gdn_fwd", grid=(nc // cb, GD_HEADS // HPS),
        in_specs=[qk_tile, qk_tile, v_tile, gab_tile, _view_tile(gz, rows, HPS * HEAD), row128, row128, row128,
                  pl.BlockSpec(lmat.shape, lambda c, g: (0, 0)),
                  pl.BlockSpec(masks.shape, lambda c, g: (0, 0, 0))],
        out_specs=[v_tile, v_tile, pl.BlockSpec((cb, HPS, HEAD, HEAD), lambda c, g: (c, g, 0, 0)),
                   pl.BlockSpec((cb, HPS, CHUNK, CHUNK), lambda c, g: (c, g, 0, 0))],
        out_shape=[jax.ShapeDtypeStruct((t, GD_HEADS * HEAD), F32), jax.ShapeDtypeStruct((t, GD_HEADS * HEAD), BF16),
                   jax.ShapeDtypeStruct((nc, GD_HEADS, HEAD, HEAD), F32),
                   jax.ShapeDtypeStruct((nc, GD_HEADS, CHUNK, CHUNK), F32)],
        scratch_shapes=[pltpu.VMEM((GD_HEADS, HEAD, HEAD), F32)],
        compiler_params=_params(_ARB, _ARB))(qn, kn, cv, gab, gz[0], alog, dtb, gain, lmat, masks)


def _gdn_bwd(qn, kn, cv, gab, gz, alog, dtb, gain, oraw, ssave, tsave, dog, consts, into):
    t = qn.shape[0]
    dproj, off, width = into
    assert width == GD_HEADS * HEAD and off % (HPS * HEAD) == 0
    nc = t // CHUNK
    cb = _chunks_per_step(nc)
    rows = cb * CHUNK
    nb = nc // cb
    lmat, lmat_t, masks = consts
    qk_tile, v_tile, gab_tile = _gd_specs(rows, nb)
    row128 = pl.BlockSpec((1, HEAD), lambda c, h: (0, 0))

    def body(q_ref, k_ref, v_ref, gab_ref, gz_ref, alog_ref, dtb_ref, gain_ref, oraw_ref, ssave_ref, tsave_ref, dog_ref,
             l_ref, lt_ref, mask_ref, _,
             dq_ref, dk_ref, dv_ref, dgab_ref, dgz_ref, small_ref, dstate):
        c = pl.program_id(0)
        g = pl.program_id(1)

        @pl.when(c == 0)
        def _():
            for hh in range(HPS):
                dstate[g * HPS + hh] = jnp.zeros((HEAD, HEAD), F32)

        @pl.when((c == 0) & (g == 0))
        def _():
            small_ref[...] = jnp.zeros_like(small_ref)

        alog = alog_ref[...]
        dtb = dtb_ref[...]
        gain_v = gain_ref[...]
        lane = lax.broadcasted_iota(jnp.int32, (1, HEAD), 1)
        last_row = (lax.broadcasted_iota(jnp.int32, (CHUNK, HEAD), 0) == CHUNK - 1).astype(F32)

        def one(j, carry):
            i = cb - 1 - j
            sl = pl.ds(pl.multiple_of(i * CHUNK, CHUNK), CHUNK)
            sp_arg, g_all, beta_all = _gd_gates(gab_ref[sl, :], alog, dtb)
            strict, eye = mask_ref[1], mask_ref[2]
            ltm = lt_ref[...]
            hs = range(HPS)
            heads = [g * HPS + hh for hh in hs]
            lq = [slice(hh // 2 * HEAD, (hh // 2 + 1) * HEAD) for hh in hs]
            lv = [slice(hh * HEAD, (hh + 1) * HEAD) for hh in hs]
            q = [q_ref[sl, s] for s in lq]
            k = [k_ref[sl, s] for s in lq]
            v = [v_ref[sl, s] for s in lv]
            gzv = [gz_ref[sl, s] for s in lv]
            chs = _gd_chunks(q, k, v, g_all, beta_all, [(g, hh) for hh in hs], l_ref, mask_ref,
                             tm=[tsave_ref[i, hh] for hh in hs])

            def col(name):
                return [ch[name] for ch in chs]

            def mul(x, y):
                return x * y

            tm, lm, eg, bb = col("tm"), col("lm"), col("eg"), col("bb")
            s0 = [ssave_ref[i, hh] for hh in hs]
            ds = [dstate[h] for h in heads]
            v_new = _each(lambda u, w, s: u - _dot(w, s), col("u"), col("w"), s0)

            o = [oraw_ref[sl, s] for s in lv]
            r = [lax.rsqrt(jnp.mean(x * x, axis=1, keepdims=True) + EPS) for x in o]
            on = _each(mul, o, r)
            dg_out = [dog_ref[sl, s] for s in lv]
            sgate = [_silu(x) for x in gzv]
            for hh in hs:
                dgz_ref[sl, lv[hh]] = (dg_out[hh] * on[hh] * gain_v * _dsilu(gzv[hh])).astype(BF16)
            small_ref[0:1, :] += sum(jnp.sum(d * s * n, axis=0, keepdims=True) for d, s, n in zip(dg_out, sgate, on))
            don = _each(lambda d, s: d * s * gain_v, dg_out, sgate)
            do = _each(lambda rr, dn, n: rr * (dn - n * jnp.mean(dn * n, axis=1, keepdims=True)), r, don, on)

            dv_new = _each(lambda a, d, b, s: _dot_tn(a, d) + _dot(b, s), col("qk"), do, col("kg"), ds)
            dqk = _each(_dot_nt, do, v_new)
            dkg = _each(_dot_nt, v_new, ds)
            dge = _each(lambda s, d: jnp.sum(_rowsum(s * d), axis=0, keepdims=True), s0, ds)
            both = _each(lambda d, dv: jnp.concatenate([d, dv], axis=0), do, dv_new)
            from_s = _each(_dot_nt, both, s0)
            dqg = [x[:CHUNK] for x in from_s]
            dw = [-x[CHUNK:] for x in from_s]
            ds_new = _each(lambda qg, w, bo, ge, s: _dot_tn(jnp.concatenate([qg, -w], axis=0), bo) + ge * s,
                           col("qg"), col("w"), both, col("ge"), ds)
            for hh in hs:
                dstate[heads[hh]] = ds_new[hh]

            side = _each(lambda dv, d: jnp.concatenate([dv, d], axis=1), dv_new, dw)
            back = _each(_dot_tn, tm, side)
            dvb = [x[:, :HEAD] for x in back]
            dkbg = [x[:, HEAD:] for x in back]
            dtm = _each(lambda sd, vb, kbg: _dot_nt(sd, jnp.concatenate([vb, kbg], axis=1)), side, col("vb"), col("kbg"))
            dtt = _each(_dot_nt, dtm, tm)
            da = _each(lambda t_, x: -_dot_tn(t_, x) * strict, tm, dtt)
            dal = _each(mul, da, lm)
            dqk_l = _each(mul, dqk, lm)
            stack = _each(lambda x, y: jnp.concatenate([x, y], axis=0), dal, dqk_l)
            on_k = _each(_dot, stack, k)
            dkb = _each(lambda x, y, e: x[:CHUNK] + y * e, on_k, dkbg, eg)
            dq = _each(lambda x, y, e: x[CHUNK:] + y * e, on_k, dqg, eg)
            dk = _each(lambda st, kb, qq, z, ekg, w_, b: _dot_tn(st, jnp.concatenate([kb, qq], axis=0)) + z * ekg + w_ * b,
                       stack, col("kb"), q, dkg, col("ekg"), dkb, bb)
            gmat = _each(lambda x, a, y, qk: x * a + y * qk, da, col("a"), dqk, col("qk"))
            t_kg = _each(lambda x, y: _rowsum(x * y), dkg, col("kg"))
            dgam = _each(lambda gm, x, qg, t_, y, kbg: (_rowsum(gm) - _row_to_col(jnp.sum(gm, axis=0, keepdims=True), eye)
                                                        + _rowsum(x * qg) - t_ + _rowsum(y * kbg)),
                         gmat, dqg, col("qg"), t_kg, dkbg, col("kbg"))
            dg_end = _each(lambda t_, e, ge: jnp.sum(t_, axis=0, keepdims=True) + e * ge[:, 0:1], t_kg, dge, col("ge"))
            dgam = _each(lambda x, e: x + last_row * e, dgam, dg_end)
            dbeta = _each(lambda x, kk, y, vv: _rowsum(x * kk) + _rowsum(y * vv), dkb, k, dvb, v)
            dg = _mx_each(ltm, dgam)

            for hh in hs:
                dv_ref[sl, lv[hh]] = dvb[hh] * bb[hh]
            fac_g = -jnp.exp(alog) * _sigmoid(sp_arg)
            fac_b = beta_all * (1.0 - beta_all)
            hot_g = [(lane == h).astype(F32) for h in heads]
            hot_b = [(lane == GD_HEADS + h).astype(F32) for h in heads]
            dga = _each(lambda x, hot: x * hot * fac_g, dg, hot_g)
            dgb = _each(lambda x, hot: x * hot * fac_b, dbeta, hot_b)
            small_ref[1:2, :] += sum(jnp.sum(x, axis=0, keepdims=True) for x in dga)
            small_ref[2:3, :] += sum(jnp.sum(x * hot * g_all, axis=0, keepdims=True) for x, hot in zip(dg, hot_g))
            for pair in range(HPS // 2):
                lqp = slice(pair * HEAD, (pair + 1) * HEAD)
                dq_ref[sl, lqp] = dq[2 * pair] + dq[2 * pair + 1]
                dk_ref[sl, lqp] = dk[2 * pair] + dk[2 * pair + 1]
            dgab_ref[sl, :] = sum(a + b for a, b in zip(dga, dgb))
            return carry

        lax.fori_loop(0, cb, one, 0, unroll=4)

    groups = GD_HEADS // HPS
    outs = [jax.ShapeDtypeStruct((t, 1024), F32), jax.ShapeDtypeStruct((t, 1024), F32),
            jax.ShapeDtypeStruct((t, 2048), F32), jax.ShapeDtypeStruct((t, groups * HEAD), F32),
            jax.ShapeDtypeStruct(dproj.shape, dproj.dtype), jax.ShapeDtypeStruct((8, HEAD), F32)]
    dgz_tile = pl.BlockSpec((rows, HPS * HEAD), lambda c, g: (nb - 1 - c, off // (HPS * HEAD) + g))
    return pl.pallas_call(
        body, name="gdn_bwd", grid=(nb, groups),
        in_specs=[qk_tile, qk_tile, v_tile, gab_tile, _view_tile(gz, rows, HPS * HEAD, lambda c: nb - 1 - c),
                  row128, row128, row128, v_tile,
                  pl.BlockSpec((cb, HPS, HEAD, HEAD), lambda c, g: (nb - 1 - c, g, 0, 0)),
                  pl.BlockSpec((cb, HPS, CHUNK, CHUNK), lambda c, g: (nb - 1 - c, g, 0, 0)), v_tile,
                  pl.BlockSpec(lmat.shape, lambda c, g: (0, 0)),
                  pl.BlockSpec(lmat_t.shape, lambda c, g: (0, 0)),
                  pl.BlockSpec(masks.shape, lambda c, g: (0, 0, 0)), _ANY],
        out_specs=[qk_tile, qk_tile, v_tile, pl.BlockSpec((rows, HEAD), lambda c, g: (nb - 1 - c, g)), dgz_tile,
                   pl.BlockSpec((8, HEAD), lambda c, g: (0, 0))],
        out_shape=outs, scratch_shapes=[pltpu.VMEM((GD_HEADS, HEAD, HEAD), F32)], input_output_aliases={15: 4},
        compiler_params=_params(_ARB, _ARB))(qn, kn, cv, gab, gz[0], alog, dtb, gain, oraw, ssave, tsave, dog,
                                             lmat, lmat_t, masks, dproj)


def _fold_groups(wide):
    t, width = wide.shape
    tr = _pick(t, CONV_ROWS, 8)

    def body(w_ref, o_ref):
        acc = w_ref[:, 0:HEAD]
        for j in range(1, width // HEAD):
            acc = acc + w_ref[:, j * HEAD:(j + 1) * HEAD]
        o_ref[...] = acc.astype(BF16)

    return pl.pallas_call(
        body, name="fold_gate_grads", grid=(t // tr,), in_specs=[_row_spec(tr, width)], out_specs=_row_spec(tr, HEAD),
        out_shape=jax.ShapeDtypeStruct((t, HEAD), BF16), compiler_params=_params(_PAR))(wide)


def _adam_math(w, g, m, v):
    m2 = ADAM_B1 * m + (1.0 - ADAM_B1) * g
    v2 = ADAM_B2 * v + (1.0 - ADAM_B2) * (g * g)
    m_hat = m2 / (1.0 - ADAM_B1 ** ADAM_STEP)
    v_hat = v2 / (1.0 - ADAM_B2 ** ADAM_STEP)
    delta = -ADAM_LR * (m_hat / (jnp.sqrt(v_hat) + ADAM_EPS) + ADAM_WD * w)
    return delta, m2, v2


def _adamw(w, g, m, v, name, after=None):
    r, c = w.shape
    tr = r
    for cand in range(8, r + 1, 8):
        if r % cand == 0 and cand * c * 4 <= (2 << 20):
            tr = cand
    if r % 8 != 0:
        tr = r

    def body(w_ref, g_ref, m_ref, v_ref, *rest):
        d_ref, m2_ref, v2_ref = rest[-3:]
        d, m2, v2 = _adam_math(w_ref[...], g_ref[...], m_ref[...], v_ref[...])
        d_ref[...] = d
        m2_ref[...] = m2
        v2_ref[...] = v2

    spec = pl.BlockSpec((tr, c), lambda i: (i, 0))
    extra = [] if after is None else [after]
    return pl.pallas_call(
        body, name=name, grid=(r // tr,), in_specs=[spec] * 4 + [_ANY] * len(extra), out_specs=[spec] * 3,
        out_shape=[jax.ShapeDtypeStruct((r, c), F32)] * 3, compiler_params=_params(_PAR))(w, g, m, v, *extra)


_ANY = pl.BlockSpec(memory_space=pl.ANY)


def _place():
    return lax.axis_index("x"), lax.axis_index("y"), lax.axis_index("c")


def _gather_weights(packs, nchs, name):
    n = len(packs)
    halves = [p.shape[0] // 2 for p in packs]
    base = [sum(nchs[:i]) for i in range(n)]
    total = sum(nchs)
    for p, h, k in zip(packs, halves, nchs):
        assert p.shape[0] == 2 * h and h % k == 0 and (h // k) % 16 == 0

    def body(*refs):
        p_refs, g_refs, (send_sems, recv_sems) = refs[:n], refs[n:2 * n], refs[2 * n:]
        x, y, c = _place()
        sibling = (x, y, 1 - c)
        chips = [(1 - x, y), (x, 1 - y), (1 - x, 1 - y)]
        chunks = [(a, q) for a in range(n) for q in range(nchs[a])]

        def rows_of(a, pc, q):
            ch = halves[a] // nchs[a]
            return pl.ds(pl.multiple_of(pc * halves[a] + q * ch, 16), ch)

        def piece(a, px, py, pc, q):
            return g_refs[a].at[2 * px + py, rows_of(a, pc, q), :]

        def copy(k, src, dst, to):
            return pltpu.make_async_remote_copy(src_ref=src, dst_ref=dst, send_sem=send_sems.at[k],
                                                recv_sem=recv_sems.at[k], device_id=to, device_id_type=MESH)

        def sem_of(j, a, q):
            return j * total + base[a] + q

        first = {(j, a, q): copy(sem_of(j, a, q), p_refs[a].at[rows_of(a, c, q), :], piece(a, x, y, c, q), (*chip, c))
                 for j, chip in enumerate(chips) for a, q in chunks}
        for a, q in chunks:
            for j in range(3):
                first[j, a, q].start()
        passed = {(j, a, q): copy(sem_of(3 + j, a, q), piece(a, *chip, c, q), piece(a, *chip, c, q), sibling)
                  for j, chip in enumerate(chips) for a, q in chunks}
        for a, q in chunks:
            for j, chip in enumerate(chips):
                copy(sem_of(j, a, q), p_refs[a].at[rows_of(a, c, q), :], piece(a, *chip, c, q), (*chip, c)).wait_recv()
                passed[j, a, q].start()
        for a, q in chunks:
            for j, chip in enumerate(chips):
                copy(sem_of(3 + j, a, q), piece(a, *chip, 1 - c, q), piece(a, *chip, 1 - c, q), sibling).wait_recv()
        for key in first:
            first[key].wait_send()
            passed[key].wait_send()

    return pl.pallas_call(
        body, name=name, out_shape=[jax.ShapeDtypeStruct((4,) + p.shape, p.dtype) for p in packs],
        in_specs=[_ANY] * n, out_specs=[_ANY] * n,
        scratch_shapes=[pltpu.SemaphoreType.DMA((6 * total,)), pltpu.SemaphoreType.DMA((6 * total,))])(*packs)


def _swap_with_sibling(arrs, nchs, lead, name, halves=False):
    n = len(arrs)
    jobs = []
    hs = [arr.shape[-2] // (2 if halves else 1) for arr in arrs]
    for a, (h, k) in enumerate(zip(hs, nchs)):
        assert h % k == 0 and (h // k) % 16 == 0
        for s in (range(lead) if lead else [None]):
            jobs += [(a, s, q * (h // k), h // k) for q in range(k)]

    def body(*refs):
        src, dst, (send_sems, recv_sems) = refs[:n], refs[n:2 * n], refs[2 * n:]
        x, y, c = _place()

        def at(ref, s, r0, rows):
            return ref.at[pl.ds(r0, rows), :] if s is None else ref.at[s, pl.ds(r0, rows), :]

        def src_rows(a, r0):
            return pl.multiple_of((1 - c) * hs[a] + r0, 16) if halves else r0

        copies = [pltpu.make_async_remote_copy(
            src_ref=at(src[a], s, src_rows(a, r0), rows), dst_ref=at(dst[a], s, r0, rows), send_sem=send_sems.at[k],
            recv_sem=recv_sems.at[k], device_id=(x, y, 1 - c), device_id_type=MESH)
            for k, (a, s, r0, rows) in enumerate(jobs)]
        for cp in copies:
            cp.start()
        for cp in copies:
            cp.wait()

    shapes = [jax.ShapeDtypeStruct(arr.shape[:-2] + (h, arr.shape[-1]), arr.dtype) for arr, h in zip(arrs, hs)]
    return pl.pallas_call(
        body, name=name, out_shape=shapes, in_specs=[_ANY] * n, out_specs=[_ANY] * n,
        scratch_shapes=[pltpu.SemaphoreType.DMA((len(jobs),)), pltpu.SemaphoreType.DMA((len(jobs),))])(*arrs)


def _add2(full, b, core, name):
    n, rows, w = b.shape
    tr = _pick(rows, 256, 16)
    nblk = rows // tr

    def body(c_ref, a_ref, b_ref, o_ref):
        o_ref[...] = (a_ref[...].astype(F32) + b_ref[...].astype(F32)).astype(BF16)

    spec = pl.BlockSpec((1, tr, w), lambda i, j, c_ref: (i, j, 0))
    grid_spec = pltpu.PrefetchScalarGridSpec(
        num_scalar_prefetch=1, grid=(n, nblk),
        in_specs=[pl.BlockSpec((1, tr, w), lambda i, j, c_ref: (i, c_ref[0] * nblk + j, 0)), spec], out_specs=spec)
    return pl.pallas_call(
        body, name=name, grid_spec=grid_spec, out_shape=jax.ShapeDtypeStruct(b.shape, BF16),
        compiler_params=_params(_PAR, _PAR))(core, full, b)


def _reduce_chips(partials, nchs, name):
    n = len(partials)
    jobs = []
    for a, (arr, k) in enumerate(zip(partials, nchs)):
        h = arr.shape[1]
        assert h % k == 0 and (h // k) % 16 == 0
        jobs += [(a, q * (h // k), h // k) for q in range(k)]

    def body(*refs):
        src, dst, (send_sems, recv_sems) = refs[:n], refs[n:2 * n], refs[2 * n:]
        x, y, c = _place()
        chips = [(1 - x, y), (x, 1 - y), (1 - x, 1 - y)]
        copies = [pltpu.make_async_remote_copy(
            src_ref=src[a].at[2 * px + py, pl.ds(r0, rows), :], dst_ref=dst[a].at[j, pl.ds(r0, rows), :],
            send_sem=send_sems.at[3 * k + j], recv_sem=recv_sems.at[3 * k + j],
            device_id=(px, py, c), device_id_type=MESH)
            for k, (a, r0, rows) in enumerate(jobs) for j, (px, py) in enumerate(chips)]
        for cp in copies:
            cp.start()
        for cp in copies:
            cp.wait()

    return pl.pallas_call(
        body, name=name,
        out_shape=[jax.ShapeDtypeStruct((3,) + p.shape[1:], p.dtype) for p in partials],
        in_specs=[_ANY] * n, out_specs=[_ANY] * n,
        scratch_shapes=[pltpu.SemaphoreType.DMA((3 * len(jobs),)), pltpu.SemaphoreType.DMA((3 * len(jobs),))])(*partials)


_HBM = pl.BlockSpec(memory_space=pltpu.HBM)
_SEM = pl.BlockSpec(memory_space=pltpu.SEMAPHORE)
_DATAFLOW = pltpu.SideEffectType.DATAFLOW_SIDE_EFFECTING


def _ici_jobs(srcs, nchs, kind):
    jobs = []
    for a, (arr, k) in enumerate(zip(srcs, nchs)):
        h = arr.shape[0] // 2 if kind == "gather" else arr.shape[1]
        assert h % k == 0 and (h // k) % 16 == 0
        jobs += [(a, h, q * (h // k), h // k) for q in range(k)]
    return jobs


def _ici_copies(src, land, send_sems, recv_sems, jobs, kind):
    x, y, c = _place()
    chips = [(1 - x, y), (x, 1 - y), (1 - x, 1 - y)]
    copies = []
    for k, (a, h, r0, rows) in enumerate(jobs):
        for j, (px, py) in enumerate(chips):
            if kind == "gather":
                at = pl.ds(pl.multiple_of(c * h + r0, 16), rows)
                s, d = src[a].at[at, :], land[a].at[2 * x + y, at, :]
            else:
                s, d = src[a].at[2 * px + py, pl.ds(r0, rows), :], land[a].at[j, pl.ds(r0, rows), :]
            copies.append(pltpu.make_async_remote_copy(
                src_ref=s, dst_ref=d, send_sem=send_sems.at[3 * k + j], recv_sem=recv_sems.at[3 * k + j],
                device_id=(px, py, c), device_id_type=MESH))
    return copies


def _ici_start(srcs, nchs, kind, name):
    n = len(srcs)
    jobs = _ici_jobs(srcs, nchs, kind)
    lead = (lambda s: (4,) + s.shape) if kind == "gather" else (lambda s: (3,) + s.shape[1:])
    lands = [lax.empty(lead(s), s.dtype) for s in srcs]

    def body(*refs):
        src, land = refs[:n], refs[n:2 * n]
        send_sems, recv_sems, token = refs[2 * n], refs[2 * n + 1], refs[-1]
        for cp in _ici_copies(src, land, send_sems, recv_sems, jobs, kind):
            cp.start()
        token[...] = jnp.zeros_like(token)

    hbm = [pltpu.HBM(a.shape, a.dtype) for a in srcs + lands]
    outs = pl.pallas_call(
        body, name=name,
        out_shape=[pltpu.SemaphoreType.DMA((3 * len(jobs),)), pltpu.SemaphoreType.DMA((3 * len(jobs),))] + hbm
        + [jax.ShapeDtypeStruct((8, 128), F32)],
        in_specs=[_HBM] * (2 * n), out_specs=[_SEM, _SEM] + [_HBM] * (2 * n) + [pl.BlockSpec(memory_space=pltpu.VMEM)],
        input_output_aliases={i: 2 + i for i in range(2 * n)},
        compiler_params=pltpu.CompilerParams(has_side_effects=_DATAFLOW),
    )(*[pltpu.with_memory_space_constraint(a, pltpu.HBM) for a in srcs + lands])
    return (outs[0], outs[1], list(outs[2:2 + n]), list(outs[2 + n:2 + 2 * n]), nchs, kind), outs[-1]


def _ici_wait(handle, after, name):
    send_sems, recv_sems, srcs, lands, nchs, kind = handle
    n = len(srcs)
    jobs = _ici_jobs(srcs, nchs, kind)

    def body(*refs):
        src, land = refs[:n], refs[n:2 * n]
        for cp in _ici_copies(src, land, refs[2 * n], refs[2 * n + 1], jobs, kind):
            cp.wait_send()
            cp.wait_recv()

    outs = pl.pallas_call(
        body, name=name, out_shape=[pltpu.HBM(a.shape, a.dtype) for a in srcs + lands],
        in_specs=[_HBM] * (2 * n) + [_SEM, _SEM, _ANY], out_specs=[_HBM] * (2 * n),
        input_output_aliases={i: i for i in range(2 * n)},
        compiler_params=pltpu.CompilerParams(has_side_effects=_DATAFLOW),
    )(*srcs, *lands, send_sems, recv_sems, after)
    return list(outs[:n]), list(outs[n:])


def _pass_to_sibling(gathered, nchs, name):
    n = len(gathered)
    jobs = _ici_jobs([jax.ShapeDtypeStruct(g.shape[1:], g.dtype) for g in gathered], nchs, "gather")

    def body(*refs):
        src, dst, (send_sems, recv_sems) = refs[:n], refs[n:2 * n], refs[2 * n:]
        x, y, c = _place()
        slots = [2 * (1 - x) + y, 2 * x + (1 - y), 2 * (1 - x) + (1 - y)]

        def copy(k, j, pc):
            a, h, r0, rows = jobs[k]
            at = pl.ds(pl.multiple_of(pc * h + r0, 16), rows)
            return pltpu.make_async_remote_copy(
                src_ref=src[a].at[slots[j], at, :], dst_ref=dst[a].at[slots[j], at, :], send_sem=send_sems.at[3 * k + j],
                recv_sem=recv_sems.at[3 * k + j], device_id=(x, y, 1 - c), device_id_type=MESH)

        pairs = [(k, j) for k in range(len(jobs)) for j in range(3)]
        for k, j in pairs:
            copy(k, j, c).start()
        for k, j in pairs:
            copy(k, j, c).wait_send()
            copy(k, j, 1 - c).wait_recv()

    return pl.pallas_call(
        body, name=name, out_shape=[jax.ShapeDtypeStruct(g.shape, g.dtype) for g in gathered],
        in_specs=[_ANY] * n, out_specs=[_ANY] * n, input_output_aliases={i: i for i in range(n)},
        scratch_shapes=[pltpu.SemaphoreType.DMA((3 * len(jobs),)), pltpu.SemaphoreType.DMA((3 * len(jobs),))])(*gathered)


def _add4(own, got, name):
    rows, w = own.shape
    tr = _pick(rows, 128, 16)

    def body(a_ref, b_ref, o_ref):
        o_ref[...] = ((a_ref[...].astype(F32) + b_ref[0].astype(F32)) + b_ref[1].astype(F32)) + b_ref[2].astype(F32)

    return pl.pallas_call(
        body, name=name, grid=(rows // tr,),
        in_specs=[pl.BlockSpec((tr, w), lambda i: (i, 0)), pl.BlockSpec((3, tr, w), lambda i: (0, i, 0))],
        out_specs=pl.BlockSpec((tr, w), lambda i: (i, 0)), out_shape=jax.ShapeDtypeStruct((rows, w), F32),
        compiler_params=_params(_PAR))(own, got)


def _small_sync(gs, ws, ms, vs):
    rows = gs.shape[0]
    vmem = pl.BlockSpec(memory_space=pltpu.VMEM)

    def body(g_ref, w_ref, m_ref, v_ref, sum_ref, d_ref, m2_ref, v2_ref, buf, send_sems, recv_sems):
        x, y, c = _place()
        me = 4 * x + 2 * y + c
        buf[me] = g_ref[...]
        copies = []
        for k in range(1, 8):
            peer = (x ^ (k >> 2), y ^ ((k >> 1) & 1), c ^ (k & 1))
            copies.append(pltpu.make_async_remote_copy(
                src_ref=g_ref, dst_ref=buf.at[me], send_sem=send_sems.at[k - 1], recv_sem=recv_sems.at[k - 1],
                device_id=peer, device_id_type=MESH))
        for cp in copies:
            cp.start()
        for cp in copies:
            cp.wait()
        total = buf[0]
        for i in range(1, 8):
            total = total + buf[i]
        sum_ref[...] = total
        d, m2, v2 = _adam_math(w_ref[...], total, m_ref[...], v_ref[...])
        d_ref[...] = d
        m2_ref[...] = m2
        v2_ref[...] = v2

    shape = jax.ShapeDtypeStruct((rows, 128), F32)
    return pl.pallas_call(
        body, name="small_sync", out_shape=[shape] * 4, in_specs=[vmem] * 4, out_specs=[vmem] * 4,
        scratch_shapes=[pltpu.VMEM((8, rows, 128), F32), pltpu.SemaphoreType.DMA((7,)),
                        pltpu.SemaphoreType.DMA((7,))])(gs, ws, ms, vs)


_GROUPS = {
    "ffn1": dict(cols=("ffn1_w_in", 1408), rows=(("ffn1_w_out", 704, 704),), chunks=(8, 2)),
    "ffn2": dict(cols=("ffn2_w_in", 1408), rows=(("ffn2_w_out", 704, 704),), chunks=(8, 2)),
    "mixer": dict(cols=("w_in", 3080), chunks=(8, 4),
                  rows=(("w_branch_hgrn", 256, 256), ("w_branch_gdn", 512, 512), ("w_out", 256, 256),
                        ("gdn_conv_w", CONV_K, 128))),
}
_BIG_NAMES = tuple(n for g in _GROUPS.values() for n in (g["cols"][0],) + tuple(r[0] for r in g["rows"]))


def _group_names(group):
    return (group["cols"][0],) + tuple(r[0] for r in group["rows"])


def _pack(parts, lead, group):
    ax = len(lead)
    rows = []
    for n, r, padded in group["rows"]:
        p = parts[n]
        if padded != r:
            p = jnp.tile(p, (1,) * ax + (padded // r, 1))
        rows.append(p)
    return [parts[group["cols"][0]], rows[0] if len(rows) == 1 else jnp.concatenate(rows, axis=ax)]


def _unpack(cols, rows, group):
    out, off = {group["cols"][0]: cols}, 0
    for n, r, padded in group["rows"]:
        out[n] = rows[..., off:off + r, :]
        off += padded
    return out


def _is_col_sharded(name):
    return name in ("ffn1_w_in", "ffn2_w_in", "w_in", "gdn_conv_w")


def _full_from_shards(name, g):
    if _is_col_sharded(name):
        return jnp.transpose(g, (1, 0, 2)).reshape(g.shape[1], -1)
    return g.reshape(-1, g.shape[2])


def _shards_from_full(name, full):
    if _is_col_sharded(name):
        return jnp.transpose(full.reshape(full.shape[0], 4, -1), (1, 0, 2))
    return full.reshape(4, -1, full.shape[1])


_SMALL = (("ffn1_norm", 8), ("mix_norm", 8), ("hgrn_lb_logits", 16), ("hgrn_out_norm", 8), ("gdn_a_log", 8),
          ("gdn_dt_bias", 8), ("gdn_out_norm", 8), ("ffn2_norm", 8), ("final_norm", 8), ("loss", 8))
_SMALL_ROWS = sum(r for _, r in _SMALL)


def _pack_small(parts):
    out = []
    for name, rows in _SMALL:
        p = parts[name].reshape(-1).astype(F32)
        if p.shape[0] <= 128:
            if p.shape[0] < 128:
                p = jnp.concatenate([p, jnp.zeros((128 - p.shape[0],), F32)])
            p = jnp.broadcast_to(p.reshape(1, 128), (rows, 128))
        out.append(p.reshape(rows, 128))
    return jnp.concatenate(out, axis=0)


def _unpack_small(packed, shapes):
    out, off = {}, 0
    for name, rows in _SMALL:
        n = int(np.prod(shapes[name]))
        out[name] = packed[off:off + rows].reshape(-1)[:n].reshape(shapes[name])
        off += rows
    return out


def _ffn_fwd(x, gain, w_in, w_out, tag):
    n = _rmsnorm_fwd(x, gain, tag + "_norm")
    a, b, hm = _ffn_in_act(n, w_in, tag + "_in")
    out = _mm(hm, w_out, alpha=0.5, res=x, name=tag + "_out")
    return out, (n, a, b)


def _ffn_bwd(x, gain, w_in, w_out, saved, dout, dout_bf, tag):
    n, a, b = saved
    da, db, hm = _ffn_dact(dout_bf, w_out, a, b, tag + "_dact")
    dw_out = _mm(hm, dout_bf, ta=True, alpha=0.5, out_dtype=BF16, name=tag + "_dwout")
    dwa = _mm(n, da, ta=True, out_dtype=BF16, name=tag + "_dwin_a")
    dwb = _mm(n, db, ta=True, out_dtype=BF16, name=tag + "_dwin_b")
    half = D_FF // 2
    dw_in = jnp.stack([dwa[:, :half], dwa[:, half:], dwb[:, :half], dwb[:, half:]])
    dn = _mm(da, w_in, tb=True, name=tag + "_dnorm_a")
    dn = _mm(db, w_in, tb=True, res=dn, b_from=D_FF, name=tag + "_dnorm_b")
    dx, dx_bf, dgain = _rmsnorm_bwd(x, gain, dn, dout, tag + "_dx")
    return dx, dx_bf, dgain, dw_in, dw_out


def _pad_lanes(v):
    return jnp.concatenate([v.reshape(1, -1), jnp.zeros((1, HEAD - v.size), F32)], axis=1)


def _local_step(x, tgt, small, exchange):
    hg_c = _hg_consts()
    gd_c = _gd_consts()
    alog = _pad_lanes(small["gdn_a_log"])
    dtb = _pad_lanes(small["gdn_dt_bias"])
    logits = small["hgrn_lb_logits"]
    hg_gain = small["hgrn_out_norm"].reshape(1, HEAD)
    gd_gain = small["gdn_out_norm"].reshape(1, HEAD)
    g1, gm, g2 = small["ffn1_norm"].reshape(1, -1), small["mix_norm"].reshape(1, -1), small["ffn2_norm"].reshape(1, -1)
    gf = small["final_norm"].reshape(1, -1)
    qscale = HEAD ** -0.5

    w1 = exchange.weights("ffn1")
    started = exchange.prefetch("mixer")
    h1, ffn1_saved = _ffn_fwd(x, g1 + started, w1["ffn1_w_in"], w1["ffn1_w_out"], "ffn1")
    u = _rmsnorm_fwd(h1, gm, "mix_norm")
    w = exchange.weights("mixer", after=u)
    started = exchange.prefetch("ffn2")
    seg, off = {}, 0
    for name, size in zip(IN_NAMES, IN_SIZES):
        seg[name] = w["w_in"][:, off:off + size]
        off += size
    w_gab = jnp.concatenate([seg["ga"], seg["gb"], jnp.zeros((D_MODEL, HEAD - 32), BF16)], axis=1)
    big_segs = [n for n in IN_NAMES if n not in ("ga", "gb")]
    conv8 = jnp.concatenate([w["gdn_conv_w"].astype(F32), jnp.zeros((8 - CONV_K, 4096), F32)], axis=0)
    conv_q, conv_k, conv_v = conv8[:, :1024], conv8[:, 1024:2048], conv8[:, 2048:]
    w_main = jnp.concatenate([seg[n] for n in big_segs], axis=1)
    proj = _mm(u, w_main, name="proj")
    pr, off = {}, 0
    for n in big_segs:
        pr[n] = _view(proj, off, seg[n].shape[1])
        off += seg[n].shape[1]
    gab = _mm(u, w_gab, name="proj_gab")
    oh_raw, oh, s_h = _hgrn_fwd(pr["hq"], pr["hf"], pr["hi"], pr["hg"], logits, hg_gain + started, hg_c)
    qn = _conv_fwd(pr["gq"], conv_q, qscale, "conv_q")
    kn = _conv_fwd(pr["gk"], conv_k, 1.0, "conv_k")
    cv = _conv_fwd(pr["gv"], conv_v, None, "conv_v")
    og_raw, og, s_g, t_g = _gdn_fwd(qn, kn, cv, gab, pr["gz"], alog, dtb, gd_gain, gd_c)
    yh = _mm(oh, w["w_branch_hgrn"], name="branch_h")
    yg = _mm(og, w["w_branch_gdn"], name="branch_g")
    ym = _merge_fwd(yh, yg, pr["gate_h"], pr["gate_g"])
    h2 = _mm(ym, w["w_out"], res=h1, name="mix_out")
    w2 = exchange.weights("ffn2", after=h2)
    h3, ffn2_saved = _ffn_fwd(h2, g2, w2["ffn2_w_in"], w2["ffn2_w_out"], "ffn2")
    loss, dh3, dh3_bf, d_gf = _final_loss(h3, gf, tgt)

    dh2, dh2_bf, d_g2, d_f2in, d_f2out = _ffn_bwd(h2, g2, w2["ffn2_w_in"], w2["ffn2_w_out"], ffn2_saved, dh3, dh3_bf,
                                                  "ffn2")
    started = exchange.reduce("ffn2", {"ffn2_w_in": d_f2in, "ffn2_w_out": d_f2out}, behind=True)
    dym =_mm(dh2_bf, w["w_out"], tb=True, name="d_merge")
    d_wout = _mm(ym, dh2_bf, ta=True, out_dtype=BF16, name="d_w_out")
    dproj = lax.empty((x.shape[0], w_main.shape[1]), BF16)
    dyh, dyg, dproj = _merge_bwd(dym, yh, yg, pr["gate_h"], pr["gate_g"], _into(dproj, pr["gate_h"][1], 2 * D_MODEL))
    d_wbh = _mm(oh, dyh, ta=True, out_dtype=BF16, name="d_w_branch_h")
    d_wbg = _mm(og, dyg, ta=True, out_dtype=BF16, name="d_w_branch_g")
    doh = _mm(dyh, w["w_branch_hgrn"], tb=True, name="d_oh")
    dog = _mm(dyg, w["w_branch_gdn"], tb=True, name="d_og")
    dproj, d_hg_gain, d_lb0 = _hgrn_bwd(pr["hq"], pr["hf"], pr["hi"], pr["hg"], logits, hg_gain + started, oh_raw,
                                        s_h, doh, hg_c, _into(dproj, pr["hq"][1], 4 * D_MODEL))
    d_qn, d_kn, d_cv, d_gab_wide, dproj, gd_small = _gdn_bwd(qn, kn, cv, gab, pr["gz"], alog, dtb, gd_gain, og_raw,
                                                             s_g, t_g, dog, gd_c, _into(dproj, *pr["gz"][1:]))
    d_gab = _fold_groups(d_gab_wide)
    dc_q, dwc_q = _conv_bwd_a(pr["gq"], conv_q, d_qn, qscale, "dconv_q")
    dc_k, dwc_k = _conv_bwd_a(pr["gk"], conv_k, d_kn, 1.0, "dconv_k")
    dc_v, dwc_v = _conv_bwd_a(pr["gv"], conv_v, d_cv, None, "dconv_v")
    dproj = _conv_bwd_b(dc_q, conv_q, "dconvx_q", _into(dproj, *pr["gq"][1:]))
    dproj = _conv_bwd_b(dc_k, conv_k, "dconvx_k", _into(dproj, *pr["gk"][1:]))
    dproj = _conv_bwd_b(dc_v, conv_v, "dconvx_v", _into(dproj, *pr["gv"][1:]))
    du =_mm(d_gab, w_gab, tb=True, name="du_gab")
    du = _mm(dproj, w_main, tb=True, res=du, name="du")
    d_wmain = _mm(u, dproj, ta=True, out_dtype=BF16, name="dw_main")
    d_wgab = _mm(u, d_gab, ta=True, out_dtype=BF16, name="dw_gab")
    cut = IN_WIDTH // 4
    d_win = jnp.stack([d_wmain[:, :cut], d_wmain[:, cut:2 * cut],
                       jnp.concatenate([d_wmain[:, 2 * cut:8192], d_wgab[:, :32], d_wmain[:, 8192:3 * cut - 32]], axis=1),
                       d_wmain[:, 3 * cut - 32:]])
    d_conv = jnp.concatenate([dwc_q[:CONV_K], dwc_k[:CONV_K], dwc_v[:CONV_K]], axis=1).astype(BF16)
    started = exchange.reduce("mixer", {"w_in": d_win, "gdn_conv_w": d_conv, "w_branch_hgrn": d_wbh,
                                        "w_branch_gdn": d_wbg, "w_out": d_wout}, behind=True)
    dh1, dh1_bf, d_gm = _rmsnorm_bwd(h1, gm + started, du, dh2, "mix_dnorm")
    dx, _, d_g1, d_f1in, d_f1out = _ffn_bwd(x, g1, w1["ffn1_w_in"], w1["ffn1_w_out"], ffn1_saved, dh1, dh1_bf, "ffn1")
    exchange.reduce("ffn1", {"ffn1_w_in": d_f1in, "ffn1_w_out": d_f1out}, behind=True)
    d_lb0 = d_lb0.reshape(1, -1)
    sm = {"ffn1_norm": d_g1, "mix_norm": d_gm, "hgrn_lb_logits": jnp.concatenate([d_lb0, -d_lb0], axis=0),
          "hgrn_out_norm": d_hg_gain, "gdn_a_log": gd_small[2, :16], "gdn_dt_bias": gd_small[1, :16],
          "gdn_out_norm": gd_small[0], "ffn2_norm": d_g2, "final_norm": d_gf, "loss": loss[0, :1]}
    return dx, sm


class _Exchange:
    def __init__(self, wts):
        self.wts = wts
        xi, yi, ci = _place()
        self.chip = 2 * xi + yi
        self.south = ci == 0
        self.core = ci.reshape(1).astype(jnp.int32)
        self.mine = {}
        self.coming = {}
        self.going = {}

    def _packs(self, tag):
        group = _GROUPS[tag]
        return _pack({n: self.wts[n][0].astype(BF16) for n in _group_names(group)}, (), group)

    def prefetch(self, tag):
        packs = self._packs(tag)
        handle, token = _ici_start(packs, _GROUPS[tag]["chunks"], "gather", "gather_start_" + tag)
        self.coming[tag] = handle
        return token[0:1, 0:1]

    def weights(self, tag, after=None):
        group = _GROUPS[tag]
        if tag in self.coming:
            packs, halves = _ici_wait(self.coming.pop(tag), after, "gather_wait_" + tag)
            others = _pass_to_sibling(halves, group["chunks"], "gather_pass_" + tag)
        else:
            packs = self._packs(tag)
            others = _gather_weights(packs, group["chunks"], "gather_" + tag)
        whole = [lax.dynamic_update_index_in_dim(g, p, self.chip, 0) for g, p in zip(others, packs)]
        gathered = _unpack(*whole, group)
        return {n: _full_from_shards(n, gathered[n]) for n in _group_names(group)}

    def reduce(self, tag, grads, behind=False):
        group = _GROUPS[tag]
        shards = {n: (grads[n] if grads[n].ndim == 3 else _shards_from_full(n, grads[n])) for n in _group_names(group)}
        gpacks = _pack(shards, (4,), group)
        got = _swap_with_sibling(gpacks, group["chunks"], 4, "reduce_pair_" + tag, halves=True)
        sums = [_add2(a, b, self.core, "add_pair_%s_%d" % (tag, i)) for i, (a, b) in enumerate(zip(gpacks, got))]
        if behind:
            handle, token = _ici_start(sums, group["chunks"], "reduce", "reduce_start_" + tag)
            self.going[tag] = handle
            self.token = token
            return token[0:1, 0:1]
        self._add_chips(tag, sums, _reduce_chips(sums, group["chunks"], "reduce_chips_" + tag))
        return None

    def _add_chips(self, tag, sums, from_chips):
        self.mine[tag] = [_add4(lax.dynamic_index_in_dim(s, self.chip, axis=0, keepdims=False), f,
                                "add_chips_%s_%d" % (tag, i)) for i, (s, f) in enumerate(zip(sums, from_chips))]

    def finish(self, tags, after):
        for tag in tags:
            if tag in self.going:
                self._add_chips(tag, *_ici_wait(self.going.pop(tag), after, "reduce_wait_" + tag))
        mine = [a for t in tags for a in self.mine[t]]
        nchs = [k for t in tags for k in _GROUPS[t]["chunks"]]
        theirs = _swap_with_sibling(mine, nchs, 0, "share_pair_" + tags[0])
        whole = [jnp.concatenate([jnp.where(self.south, a, b), jnp.where(self.south, b, a)], axis=0)
                 for a, b in zip(mine, theirs)]
        reduced = {}
        for i, t in enumerate(tags):
            reduced.update(_unpack(whole[2 * i], whole[2 * i + 1], _GROUPS[t]))
        return reduced


_WEIGHTS = ("ffn1_norm", "ffn1_w_in", "ffn1_w_out", "mix_norm", "w_in", "hgrn_lb_logits", "hgrn_out_norm",
            "gdn_conv_w", "gdn_a_log", "gdn_dt_bias", "gdn_out_norm", "w_branch_hgrn", "w_branch_gdn", "w_out",
            "ffn2_norm", "ffn2_w_in", "ffn2_w_out", "final_norm")


def kernel(x, ffn1_norm, ffn1_w_in, ffn1_w_out, mix_norm, w_in, hgrn_lb_logits, hgrn_out_norm, gdn_conv_w, gdn_a_log, gdn_dt_bias, gdn_out_norm, w_branch_hgrn, w_branch_gdn, w_out, ffn2_norm, ffn2_w_in, ffn2_w_out, final_norm, loss_target, m_ffn1_norm, m_ffn1_w_in, m_ffn1_w_out, m_mix_norm, m_w_in, m_hgrn_lb_logits, m_hgrn_out_norm, m_gdn_conv_w, m_gdn_a_log, m_gdn_dt_bias, m_gdn_out_norm, m_w_branch_hgrn, m_w_branch_gdn, m_w_out, m_ffn2_norm, m_ffn2_w_in, m_ffn2_w_out, m_final_norm, v_ffn1_norm, v_ffn1_w_in, v_ffn1_w_out, v_mix_norm, v_w_in, v_hgrn_lb_logits, v_hgrn_out_norm, v_gdn_conv_w, v_gdn_a_log, v_gdn_dt_bias, v_gdn_out_norm, v_w_branch_hgrn, v_w_branch_gdn, v_w_out, v_ffn2_norm, v_ffn2_w_in, v_ffn2_w_out, v_final_norm):
    args = dict(locals())
    wts = {n: args[n] for n in _WEIGHTS}
    moms = {n: args["m_" + n] for n in _WEIGHTS}
    vars_ = {n: args["v_" + n] for n in _WEIGHTS}

    small = {n: wts[n].astype(F32) for n in _WEIGHTS if n not in _BIG_NAMES}
    exchange = _Exchange(wts)
    dx, small_grads = _local_step(x[0], loss_target[0], small, exchange)

    out_g, out_d, out_m, out_v = {}, {}, {}, {}

    def update(tags, reduced, after):
        for t in tags:
            for n in _group_names(_GROUPS[t]):
                shape = wts[n].shape
                w2 = wts[n].reshape(shape[-2], shape[-1])
                g2 = reduced[n]
                d, m2, v2 = _adamw(w2, g2, moms[n].reshape(w2.shape), vars_[n].reshape(w2.shape), "adamw_" + n, after)
                out_g[n], out_d[n], out_m[n], out_v[n] = (g2.reshape(shape), d.reshape(shape), m2.reshape(shape),
                                                          v2.reshape(shape))
                after = v2
        return after

    done = update(("ffn2", "mixer"), exchange.finish(("ffn2", "mixer"), after=dx), exchange.token)
    update(("ffn1",), exchange.finish(("ffn1",), after=done), None)

    small_names = [n for n, _ in _SMALL]
    zero = jnp.zeros((1,), F32)
    shapes = {n: (wts[n].shape if n != "loss" else (1,)) for n in small_names}
    sums, sd, sm_, sv = _small_sync(
        _pack_small(small_grads),
        _pack_small({n: (wts[n] if n != "loss" else zero) for n in small_names}),
        _pack_small({n: (moms[n] if n != "loss" else zero) for n in small_names}),
        _pack_small({n: (vars_[n] if n != "loss" else zero) for n in small_names}))
    sg_u, sd_u, sm_u, sv_u = (_unpack_small(p, shapes) for p in (sums, sd, sm_, sv))
    for n in small_names:
        if n != "loss":
            out_g[n], out_d[n], out_m[n], out_v[n] = sg_u[n], sd_u[n], sm_u[n], sv_u[n]
    loss = sg_u["loss"].reshape(())

    return (loss, dx[None], *[out_g[n] for n in _WEIGHTS], *[out_d[n] for n in _WEIGHTS],
            *[out_m[n] for n in _WEIGHTS], *[out_v[n] for n in _WEIGHTS])
```

```python
import numpy as np

import jax
import jax.numpy as jnp
from jax import lax
from jax.experimental import pallas as pl
from jax.experimental.pallas import tpu as pltpu

F32 = jnp.float32
BF16 = jnp.bfloat16

D_MODEL = 1024
D_FF = 2816
CHUNK = 64
HEAD = 128
HG_HEADS = 8
GD_HEADS = 16
HPS = 8
MM_TM = 1408
MM_TN = 1024
MM_TK = 1536
VMEM_LIMIT = 48 * 1024 * 1024
ROW_TILE = 512
EPS = 1e-6
CONV_K = 4
IN_NAMES = ("hq", "hf", "hi", "hg", "gq", "gk", "gv", "ga", "gb", "gz", "gate_h", "gate_g")
IN_SIZES = (1024, 1024, 1024, 1024, 1024, 1024, 2048, 16, 16, 2048, 1024, 1024)
IN_WIDTH = sum(IN_SIZES)

ADAM_LR = 0.001
ADAM_B1 = 0.9
ADAM_B2 = 0.999
ADAM_EPS = 1e-08
ADAM_WD = 0.01
ADAM_STEP = 10

MESH = pl.DeviceIdType.MESH
_ARB = "arbitrary"
_PAR = "parallel"


def _bf(x):
    return x.astype(BF16)


def _dot(a, b):
    return jnp.dot(_bf(a), _bf(b), preferred_element_type=F32)


def _dot_nt(a, b):
    return lax.dot_general(_bf(a), _bf(b), (((1,), (1,)), ((), ())), preferred_element_type=F32)


def _dot_tn(a, b):
    return lax.dot_general(_bf(a), _bf(b), (((0,), (0,)), ((), ())), preferred_element_type=F32)


def _sigmoid(x):
    return jax.nn.sigmoid(x)


def _silu(x):
    return x * _sigmoid(x)


def _dsilu(x):
    s = _sigmoid(x)
    return s * (1.0 + x * (1.0 - s))


def _softplus(x):
    return jnp.maximum(x, 0.0) + jnp.log(1.0 + jnp.exp(-jnp.abs(x)))


def _rowsum(x):
    return jnp.sum(x, axis=1, keepdims=True)


def _col_to_row(col, eye):
    return jnp.sum(eye * col, axis=0, keepdims=True)


def _row_to_col(row, eye):
    return jnp.sum(eye * row, axis=1, keepdims=True)


def _pick(dim, pref, unit=128):
    if dim <= pref:
        return dim
    t = pref
    while t >= unit:
        if dim % t == 0:
            return t
        t -= unit
    return dim


def _params(*sem):
    return pltpu.CompilerParams(dimension_semantics=tuple(sem), vmem_limit_bytes=VMEM_LIMIT)


def _mm(a, b, *, ta=False, tb=False, alpha=1.0, res=None, out_dtype=F32, name="mm", b_from=0, tm_max=MM_TM):
    m = a.shape[1] if ta else a.shape[0]
    k = a.shape[0] if ta else a.shape[1]
    n = b.shape[0] if tb else b.shape[1]
    assert b_from + k <= (b.shape[1] if tb else b.shape[0])
    tm, tn, tk = _pick(m, tm_max), _pick(n, MM_TN), _pick(k, MM_TK)
    if tn < MM_TN < n and n % MM_TM == 0:
        tn = MM_TM
    nk = k // tk
    assert b_from % tk == 0
    b0 = b_from // tk
    a_spec = pl.BlockSpec((tk, tm), lambda i, j, l: (l, i)) if ta else pl.BlockSpec((tm, tk), lambda i, j, l: (i, l))
    b_spec = (pl.BlockSpec((tn, tk), lambda i, j, l: (j, b0 + l)) if tb
              else pl.BlockSpec((tk, tn), lambda i, j, l: (b0 + l, j)))
    o_spec = pl.BlockSpec((tm, tn), lambda i, j, l: (i, j))
    dims = (((0 if ta else 1,), (1 if tb else 0,)), ((), ()))
    has_res = res is not None

    def finish(r, r_ref, o_ref):
        if alpha != 1.0:
            r = r * alpha
        if has_res:
            r = r + r_ref[...]
        o_ref[...] = r.astype(out_dtype)

    def body(*refs):
        a_ref, b_ref = refs[0], refs[1]
        r_ref = refs[2] if has_res else None
        o_ref = refs[3] if has_res else refs[2]
        part = lax.dot_general(_bf(a_ref[...]), _bf(b_ref[...]), dims, preferred_element_type=F32)
        if nk == 1:
            finish(part, r_ref, o_ref)
            return
        acc = refs[-1]
        step = pl.program_id(2)

        @pl.when(step == 0)
        def _():
            acc[...] = part

        @pl.when(step != 0)
        def _():
            acc[...] += part

        @pl.when(step == nk - 1)
        def _():
            finish(acc[...], r_ref, o_ref)

    ins = [a, b] + ([res] if has_res else [])
    in_specs = [a_spec, b_spec] + ([o_spec] if has_res else [])
    return pl.pallas_call(
        body, name=name, grid=(m // tm, n // tn, nk), in_specs=in_specs, out_specs=o_spec,
        out_shape=jax.ShapeDtypeStruct((m, n), out_dtype),
        scratch_shapes=[pltpu.VMEM((tm, tn), F32)] if nk > 1 else [],
        compiler_params=_params(_PAR, _PAR, _ARB))(*ins)


def _row_spec(tr, w):
    return pl.BlockSpec((tr, w), lambda i: (i, 0))


def _full_spec(shape):
    return pl.BlockSpec(shape, lambda i: tuple(0 for _ in shape))


def _view(arr, off, width):
    return arr, off, width


def _view_rows(view, tr):
    _, off, width = view
    assert off % width == 0
    return pl.BlockSpec((tr, width), lambda i: (i, off // width))


def _view_tile(view, rows, bw, cidx=lambda c: c):
    _, off, width = view
    assert off % bw == 0 and width % bw == 0
    return pl.BlockSpec((rows, bw), lambda c, g: (cidx(c), off // bw + g))


def _rmsnorm_fwd(x, g, name):
    t, d = x.shape
    tr = _pick(t, ROW_TILE, 8)

    def body(x_ref, g_ref, o_ref):
        xv = x_ref[...]
        r = lax.rsqrt(jnp.mean(xv * xv, axis=1, keepdims=True) + EPS)
        o_ref[...] = (xv * r * g_ref[...]).astype(BF16)

    return pl.pallas_call(
        body, name=name, grid=(t // tr,), in_specs=[_row_spec(tr, d), _full_spec((1, d))],
        out_specs=_row_spec(tr, d), out_shape=jax.ShapeDtypeStruct((t, d), BF16),
        compiler_params=_params(_PAR))(x, g)


def _rmsnorm_bwd(x, g, dn, res, name):
    t, d = x.shape
    tr = _pick(t, ROW_TILE, 8)

    def body(x_ref, g_ref, dn_ref, r_ref, dx_ref, dxb_ref, dg_ref):
        @pl.when(pl.program_id(0) == 0)
        def _():
            dg_ref[...] = jnp.zeros_like(dg_ref)

        xv = x_ref[...]
        r = lax.rsqrt(jnp.mean(xv * xv, axis=1, keepdims=True) + EPS)
        xh = xv * r
        dy = dn_ref[...]
        dg_ref[...] += jnp.sum(dy * xh, axis=0, keepdims=True)
        dxh = dy * g_ref[...]
        dx = r_ref[...] + r * (dxh - xh * jnp.mean(dxh * xh, axis=1, keepdims=True))
        dx_ref[...] = dx
        dxb_ref[...] = dx.astype(BF16)

    return pl.pallas_call(
        body, name=name, grid=(t // tr,),
        in_specs=[_row_spec(tr, d), _full_spec((1, d)), _row_spec(tr, d), _row_spec(tr, d)],
        out_specs=[_row_spec(tr, d), _row_spec(tr, d), _full_spec((1, d))],
        out_shape=[jax.ShapeDtypeStruct((t, d), F32), jax.ShapeDtypeStruct((t, d), BF16),
                   jax.ShapeDtypeStruct((1, d), F32)],
        compiler_params=_params(_ARB))(x, g, dn, res)


FFN_TN = 1408
FFN_TM = 512


def _ffn_in_act(n, w_in, name):
    t, d = n.shape
    tm = _pick(t, FFN_TM)
    nf = D_FF // FFN_TN

    def body(n_ref, wa_ref, wb_ref, a_ref, b_ref, hm_ref):
        nv = n_ref[...]
        a = jnp.dot(nv, wa_ref[...], preferred_element_type=F32)
        b = jnp.dot(nv, wb_ref[...], preferred_element_type=F32)
        a_ref[...] = a.astype(BF16)
        b_ref[...] = b.astype(BF16)
        hm_ref[...] = (_silu(a) * b).astype(BF16)

    tile = pl.BlockSpec((tm, FFN_TN), lambda i, j: (i, j))
    return pl.pallas_call(
        body, name=name, grid=(t // tm, nf),
        in_specs=[pl.BlockSpec((tm, d), lambda i, j: (i, 0)), pl.BlockSpec((d, FFN_TN), lambda i, j: (0, j)),
                  pl.BlockSpec((d, FFN_TN), lambda i, j: (0, nf + j))],
        out_specs=[tile, tile, tile], out_shape=[jax.ShapeDtypeStruct((t, D_FF), BF16)] * 3,
        compiler_params=_params(_PAR, _PAR))(n, w_in, w_in)


def _ffn_dact(dout, w_out, a, b, name):
    t, d = dout.shape
    tm = _pick(t, FFN_TM)

    def body(do_ref, w_ref, a_ref, b_ref, da_ref, db_ref, hm_ref):
        dh = 0.5 * _dot_nt(do_ref[...], w_ref[...])
        av = a_ref[...].astype(F32)
        bv = b_ref[...].astype(F32)
        sg = _sigmoid(av)
        sa = av * sg
        da_ref[...] = (dh * bv * (sg * (1.0 + av * (1.0 - sg)))).astype(BF16)
        db_ref[...] = (dh * sa).astype(BF16)
        hm_ref[...] = (sa * bv).astype(BF16)

    tile = pl.BlockSpec((tm, FFN_TN), lambda i, j: (i, j))
    return pl.pallas_call(
        body, name=name, grid=(t // tm, D_FF // FFN_TN),
        in_specs=[pl.BlockSpec((tm, d), lambda i, j: (i, 0)), pl.BlockSpec((FFN_TN, d), lambda i, j: (j, 0)), tile, tile],
        out_specs=[tile, tile, tile], out_shape=[jax.ShapeDtypeStruct((t, D_FF), BF16)] * 3,
        compiler_params=_params(_PAR, _PAR))(dout, w_out, a, b)


def _merge_fwd(yh, yg, gh, gg):
    t, d = yh.shape
    tr = _pick(t, ROW_TILE, 8)

    def body(yh_ref, yg_ref, gh_ref, gg_ref, o_ref):
        o_ref[...] = (_sigmoid(gh_ref[...]) * yh_ref[...] + _sigmoid(gg_ref[...]) * yg_ref[...]).astype(BF16)

    return pl.pallas_call(
        body, name="merge_fwd", grid=(t // tr,),
        in_specs=[_row_spec(tr, d), _row_spec(tr, d), _view_rows(gh, tr), _view_rows(gg, tr)],
        out_specs=_row_spec(tr, d),
        out_shape=jax.ShapeDtypeStruct((t, d), BF16), compiler_params=_params(_PAR))(yh, yg, gh[0], gg[0])


def _into(dproj, off, width):
    return dproj, off, width


def _merge_bwd(dy, yh, yg, gh, gg, into):
    t, d = yh.shape
    tr = _pick(t, ROW_TILE, 8)
    dproj, off, width = into
    assert width == 2 * d and off % width == 0

    def body(dy_ref, yh_ref, yg_ref, gh_ref, gg_ref, _, dyh_ref, dyg_ref, dg_ref):
        dyv = dy_ref[...]
        sh = _sigmoid(gh_ref[...])
        sg = _sigmoid(gg_ref[...])
        dyh_ref[...] = (dyv * sh).astype(BF16)
        dyg_ref[...] = (dyv * sg).astype(BF16)
        dg_ref[:, :d] = (dyv * yh_ref[...] * sh * (1.0 - sh)).astype(BF16)
        dg_ref[:, d:] = (dyv * yg_ref[...] * sg * (1.0 - sg)).astype(BF16)

    return pl.pallas_call(
        body, name="merge_bwd", grid=(t // tr,),
        in_specs=[_row_spec(tr, d)] * 3 + [_view_rows(gh, tr), _view_rows(gg, tr), _ANY],
        out_specs=[_row_spec(tr, d)] * 2 + [pl.BlockSpec((tr, width), lambda i: (i, off // width))],
        out_shape=[jax.ShapeDtypeStruct((t, d), BF16)] * 2 + [jax.ShapeDtypeStruct(dproj.shape, dproj.dtype)],
        input_output_aliases={5: 2},
        compiler_params=_params(_PAR))(dy, yh, yg, gh[0], gg[0], dproj)


def _final_loss(h, g, tgt):
    t, d = h.shape
    tr = _pick(t, ROW_TILE, 8)

    def body(h_ref, g_ref, t_ref, loss_ref, dh_ref, dhb_ref, dg_ref):
        @pl.when(pl.program_id(0) == 0)
        def _():
            dg_ref[...] = jnp.zeros_like(dg_ref)
            loss_ref[...] = jnp.zeros_like(loss_ref)

        xv = h_ref[...]
        gv = g_ref[...]
        r = lax.rsqrt(jnp.mean(xv * xv, axis=1, keepdims=True) + EPS)
        xh = xv * r
        err = xh * gv - t_ref[...]
        loss_ref[...] += 0.5 * jnp.sum(jnp.mean(err * err, axis=1, keepdims=True), axis=0, keepdims=True)
        dy = err * (1.0 / d)
        dg_ref[...] += jnp.sum(dy * xh, axis=0, keepdims=True)
        dxh = dy * gv
        dh = r * (dxh - xh * jnp.mean(dxh * xh, axis=1, keepdims=True))
        dh_ref[...] = dh
        dhb_ref[...] = dh.astype(BF16)

    return pl.pallas_call(
        body, name="final_loss", grid=(t // tr,),
        in_specs=[_row_spec(tr, d), _full_spec((1, d)), _row_spec(tr, d)],
        out_specs=[_full_spec((1, 128)), _row_spec(tr, d), _row_spec(tr, d), _full_spec((1, d))],
        out_shape=[jax.ShapeDtypeStruct((1, 128), F32), jax.ShapeDtypeStruct((t, d), F32),
                   jax.ShapeDtypeStruct((t, d), BF16), jax.ShapeDtypeStruct((1, d), F32)],
        compiler_params=_params(_ARB))(h, g, tgt)


def _hg_consts():
    c = CHUNK
    t = np.arange(c)
    mats, masks = [], []
    for lvl in range(6):
        m = 1 << lvl
        blk = t // m
        mat = np.zeros((c, c), np.float32)
        for tt in range(c):
            b = blk[tt]
            if b % 2 == 1:
                mat[tt, b * m:tt + 1] = 1.0
            else:
                mat[tt, tt + 1:(b + 1) * m] = 1.0
        mats.append(mat)
        same = (t[:, None] // (2 * m)) == (t[None, :] // (2 * m))
        masks.append((same & (blk[:, None] % 2 == 1) & (blk[None, :] % 2 == 0)).astype(np.float32))
    pre = np.tril(np.ones((c, c), np.float32))
    suf = np.triu(np.ones((c, c), np.float32), 1)
    mstack = np.concatenate(mats + [pre, suf], 0)
    masks.append(np.eye(c, dtype=np.float32))
    return (jnp.asarray(mstack, BF16), jnp.asarray(mstack.T.copy(), BF16), jnp.asarray(np.stack(masks), F32),
            jnp.asarray(np.eye(HEAD, dtype=np.float32)))


def _gd_consts():
    c = CHUNK
    incl = np.tril(np.ones((c, c), np.float32))
    strict = np.tril(np.ones((c, c), np.float32), -1)
    eye = np.eye(c, dtype=np.float32)
    masks = np.stack([incl, strict, eye, incl.T.copy()])
    return jnp.asarray(incl, BF16), jnp.asarray(incl.T.copy(), BF16), jnp.asarray(masks, F32)


def _chunks_per_step(nc):
    for cb in (32 // HPS, 2, 1):
        if nc % cb == 0:
            return cb
    return 1


def _hg_prep(hq, hf, lg):
    lb = _sigmoid(lg[0:1, :] - lg[1:2, :])
    sg = _sigmoid(hf)
    sgn = _sigmoid(-hf)
    f = lb + (1.0 - lb) * sg
    lf = jnp.log(f)
    kk = (1.0 - lb) * sgn
    q = _silu(hq) * (HEAD ** -0.5)
    return lb, sg, sgn, f, lf, kk, q


def _mx_each(m, xs):
    hi, lo = _split2_each(xs)
    prods = [jnp.dot(m, jnp.concatenate([h, l], axis=1), preferred_element_type=F32) for h, l in zip(hi, lo)]
    return [p[:, :HEAD] + p[:, HEAD:] for p in prods]


def _hg_scaled(x, ex):
    xb = [_bf(a) for a in x]
    eb = [_bf(e[:6 * CHUNK]) for e in ex]
    return [[a * e[lvl * CHUNK:(lvl + 1) * CHUNK] for lvl in range(6)] for a, e in zip(xb, eb)]


def _hg_scores(q, kk, qe, ke, mask_ref):
    p = [mask_ref[6] * _rowsum(a * b) for a, b in zip(q, kk)]
    for lvl in range(6):
        d = [_dot_nt(a[lvl], b[lvl]) for a, b in zip(qe, ke)]
        p = [x + mask_ref[lvl] * y for x, y in zip(p, d)]
    return p


def _hgrn_fwd(hq, hf, hi, hg, logits, gain, consts):
    t = hq[0].shape[0]
    nc = t // CHUNK
    cb = _chunks_per_step(nc)
    rows = cb * CHUNK
    mstack, _, masks, eye = consts
    tile = pl.BlockSpec((rows, HPS * HEAD), lambda c, g: (c, g))

    def body(hq_ref, hf_ref, hi_ref, hg_ref, lg_ref, gain_ref, m_ref, mask_ref, eye_ref,
             oraw_ref, og_ref, ssave_ref, state):
        c = pl.program_id(0)
        g = pl.program_id(1)

        @pl.when(c == 0)
        def _():
            for hh in range(HPS):
                state[g * HPS + hh] = jnp.zeros((HEAD, HEAD), F32)

        lg_all = lg_ref[...]
        gain_v = gain_ref[...]

        def one(i, carry):
            sl = pl.ds(pl.multiple_of(i * CHUNK, CHUNK), CHUNK)
            hs = range(HPS)
            heads = [g * HPS + hh for hh in hs]
            ln = [slice(hh * HEAD, (hh + 1) * HEAD) for hh in hs]
            preps = [_hg_prep(hq_ref[sl, s], hf_ref[sl, s], lg_all[:, s]) for s in ln]
            lf, kk, q = [p[4] for p in preps], [p[5] for p in preps], [p[6] for p in preps]
            v = [hi_ref[sl, s] for s in ln]
            ex = [jnp.exp(x) for x in _mx_each(m_ref[...], lf)]
            eb = [e[6 * CHUNK:7 * CHUNK] for e in ex]
            esfx = [e[7 * CHUNK:8 * CHUNK] for e in ex]
            qe, ke = _hg_scaled(q, ex), _hg_scaled(kk, ex)
            p = _hg_scores(q, kk, qe, ke, mask_ref)
            s0 = [state[h] for h in heads]
            o = _each(lambda a, e, s, pp, vv: _dot(a * e, s) + _dot(pp, vv), q, eb, s0, p, v)
            eye_v = eye_ref[...]
            s1 = _each(lambda s, e, kx, ef, vv: s * _row_to_col(e[CHUNK - 1:CHUNK, :], eye_v) + _dot_tn(kx * ef, vv),
                       s0, eb, kk, esfx, v)
            for hh in hs:
                ssave_ref[i, hh] = s0[hh]
                state[heads[hh]] = s1[hh]
                oraw_ref[sl, ln[hh]] = o[hh]
                r = lax.rsqrt(jnp.mean(o[hh] * o[hh], axis=1, keepdims=True) + EPS)
                og_ref[sl, ln[hh]] = (o[hh] * r * gain_v * _silu(hg_ref[sl, ln[hh]])).astype(BF16)
            return carry

        lax.fori_loop(0, cb, one, 0, unroll=4)

    return pl.pallas_call(
        body, name="hgrn_fwd", grid=(nc // cb, HG_HEADS // HPS),
        in_specs=[_view_tile(v, rows, HPS * HEAD) for v in (hq, hf, hi, hg)] + [
                  pl.BlockSpec((2, HPS * HEAD), lambda c, g: (0, g)),
                  pl.BlockSpec((1, HEAD), lambda c, g: (0, 0)),
                  pl.BlockSpec(mstack.shape, lambda c, g: (0, 0)),
                  pl.BlockSpec(masks.shape, lambda c, g: (0, 0, 0)),
                  pl.BlockSpec(eye.shape, lambda c, g: (0, 0))],
        out_specs=[tile, tile, pl.BlockSpec((cb, HPS, HEAD, HEAD), lambda c, g: (c, g, 0, 0))],
        out_shape=[jax.ShapeDtypeStruct((t, HG_HEADS * HEAD), F32), jax.ShapeDtypeStruct((t, HG_HEADS * HEAD), BF16),
                   jax.ShapeDtypeStruct((nc, HG_HEADS, HEAD, HEAD), F32)],
        scratch_shapes=[pltpu.VMEM((HG_HEADS, HEAD, HEAD), F32)],
        compiler_params=_params(_ARB, _ARB))(hq[0], hf[0], hi[0], hg[0], logits, gain, mstack, masks, eye)


def _hgrn_bwd(hq, hf, hi, hg, logits, gain, oraw, ssave, dog, consts, into):
    t = hq[0].shape[0]
    dproj, off, width = into
    seg = HG_HEADS * HEAD
    assert HPS == HG_HEADS and width == 4 * seg and off % width == 0
    nc = t // CHUNK
    cb = _chunks_per_step(nc)
    rows = cb * CHUNK
    nb = nc // cb
    mstack, mstack_t, masks, eye = consts
    tile = pl.BlockSpec((rows, HPS * HEAD), lambda c, g: (nb - 1 - c, g))

    def body(hq_ref, hf_ref, hi_ref, hg_ref, lg_ref, gain_ref, oraw_ref, ssave_ref, dog_ref, m_ref, mt_ref,
             mask_ref, eye_ref, _, d_ref, dgain_ref, dlb_ref, dstate):
        c = pl.program_id(0)
        g = pl.program_id(1)

        @pl.when(c == 0)
        def _():
            for hh in range(HPS):
                dstate[g * HPS + hh] = jnp.zeros((HEAD, HEAD), F32)

        @pl.when((c == 0) & (g == 0))
        def _():
            dgain_ref[...] = jnp.zeros_like(dgain_ref)
            dlb_ref[...] = jnp.zeros_like(dlb_ref)

        lg_all = lg_ref[...]
        gain_v = gain_ref[...]
        eye_v = eye_ref[...]
        last_row = (lax.broadcasted_iota(jnp.int32, (CHUNK, HEAD), 0) == CHUNK - 1).astype(F32)

        def one(j, carry):
            i = cb - 1 - j
            sl = pl.ds(pl.multiple_of(i * CHUNK, CHUNK), CHUNK)
            hs = range(HPS)
            heads = [g * HPS + hh for hh in hs]
            ln = [slice(hh * HEAD, (hh + 1) * HEAD) for hh in hs]
            hqv = [hq_ref[sl, s] for s in ln]
            hgv = [hg_ref[sl, s] for s in ln]
            preps = [_hg_prep(a, hf_ref[sl, s], lg_all[:, s]) for a, s in zip(hqv, ln)]
            lb, sg, sgn, f, lf, kk, q = ([p[n] for p in preps] for n in range(7))
            v = [hi_ref[sl, s] for s in ln]
            ex = [jnp.exp(x) for x in _mx_each(m_ref[...], lf)]
            eb = [e[6 * CHUNK:7 * CHUNK] for e in ex]
            esfx = [e[7 * CHUNK:8 * CHUNK] for e in ex]
            qe, ke = _hg_scaled(q, ex), _hg_scaled(kk, ex)
            p = _hg_scores(q, kk, qe, ke, mask_ref)
            s0 = [ssave_ref[i, hh] for hh in hs]
            ds = [dstate[h] for h in heads]

            o = [oraw_ref[sl, s] for s in ln]
            r = [lax.rsqrt(jnp.mean(x * x, axis=1, keepdims=True) + EPS) for x in o]
            on = _each(lambda x, y: x * y, o, r)
            dg_out = [dog_ref[sl, s] for s in ln]
            sgate = [_silu(x) for x in hgv]
            for hh in hs:
                d_ref[sl, slice(3 * seg + hh * HEAD, 3 * seg + (hh + 1) * HEAD)] =(dg_out[hh] * on[hh] * gain_v * _dsilu(hgv[hh])).astype(BF16)
            dgain_ref[...] += sum(jnp.sum(d * s * n, axis=0, keepdims=True) for d, s, n in zip(dg_out, sgate, on))
            don = _each(lambda d, s: d * s * gain_v, dg_out, sgate)
            do = _each(lambda rr, dn, n: rr * (dn - n * jnp.mean(dn * n, axis=1, keepdims=True)), r, don, on)

            dp = _each(_dot_nt, do, v)
            dv = _each(lambda pp, d, kx, ef, s: _dot_tn(pp, d) + _dot(kx * ef, s), p, do, kk, esfx, ds)
            dqb = _each(_dot_nt, do, s0)
            dkx = _each(_dot_nt, v, ds)
            diag = [_rowsum(mask_ref[6] * x) for x in dp]
            dq = _each(lambda a, e, d, kx: a * e + d * kx, dqb, eb, diag, kk)
            dk = _each(lambda a, e, d, qq: a * e + d * qq, dkx, esfx, diag, q)
            dxs = [[] for _ in hs]
            for lvl in range(6):
                el = [e[lvl * CHUNK:(lvl + 1) * CHUNK] for e in ex]
                gm = [mask_ref[lvl] * x for x in dp]
                gm = [_bf(x) for x in gm]
                a1 = _each(lambda m_, kx: _dot(m_, kx[lvl]), gm, ke)
                a2 = _each(lambda m_, qq: _dot_tn(m_, qq[lvl]), gm, qe)
                dq = _each(lambda x, a, e: x + a * e, dq, a1, el)
                dk = _each(lambda x, a, e: x + a * e, dk, a2, el)
                for hh in hs:
                    dxs[hh].append((a1[hh] * q[hh] + a2[hh] * kk[hh]) * el[hh])
            e_end_row = [e[CHUNK - 1:CHUNK, :] for e in eb]
            ds_new = _each(lambda qq, e, d, er, s: _dot_tn(qq * e, d) + _row_to_col(er, eye_v) * s, q, eb, do, e_end_row, ds)
            for hh in hs:
                dstate[heads[hh]] = ds_new[hh]
                dend_row = _col_to_row(_rowsum(s0[hh] * ds[hh]), eye_v)
                dxs[hh].append(dqb[hh] * q[hh] * eb[hh] + last_row * (e_end_row[hh] * dend_row))
                dxs[hh].append(dkx[hh] * kk[hh] * esfx[hh])
            dlf = _mx_each(mt_ref[...], [jnp.concatenate(x, axis=0) for x in dxs])

            for hh in hs:
                d_ref[sl, slice(2 * seg + hh * HEAD, 2 * seg + (hh + 1) * HEAD)] =dv[hh].astype(BF16)
                d_ref[sl, ln[hh]] =(dq[hh] * (HEAD ** -0.5) * _dsilu(hqv[hh])).astype(BF16)
                df = dlf[hh] / f[hh]
                dsig = (1.0 - lb[hh]) * sg[hh] * sgn[hh]
                d_ref[sl, slice(seg + hh * HEAD, seg + (hh + 1) * HEAD)] =((df - dk[hh]) * dsig).astype(BF16)
                dlb_t = jnp.sum(df * sgn[hh] - dk[hh] * sgn[hh], axis=0, keepdims=True)
                dlb_ref[pl.ds(heads[hh], 1), :] += dlb_t * lb[hh] * (1.0 - lb[hh])
            return carry

        lax.fori_loop(0, cb, one, 0, unroll=2)

    outs = [jax.ShapeDtypeStruct(dproj.shape, dproj.dtype),
            jax.ShapeDtypeStruct((1, HEAD), F32), jax.ShapeDtypeStruct((HG_HEADS, HEAD), F32)]
    return pl.pallas_call(
        body, name="hgrn_bwd", grid=(nb, HG_HEADS // HPS),
        in_specs=[_view_tile(v, rows, HPS * HEAD, lambda c: nb - 1 - c) for v in (hq, hf, hi, hg)] + [
                  pl.BlockSpec((2, HPS * HEAD), lambda c, g: (0, g)),
                  pl.BlockSpec((1, HEAD), lambda c, g: (0, 0)), tile,
                  pl.BlockSpec((cb, HPS, HEAD, HEAD), lambda c, g: (nb - 1 - c, g, 0, 0)), tile,
                  pl.BlockSpec(mstack.shape, lambda c, h: (0, 0)),
                  pl.BlockSpec(mstack_t.shape, lambda c, h: (0, 0)),
                  pl.BlockSpec(masks.shape, lambda c, h: (0, 0, 0)),
                  pl.BlockSpec(eye.shape, lambda c, h: (0, 0)), _ANY],
        out_specs=[pl.BlockSpec((rows, width), lambda c, h: (nb - 1 - c, off // width)),
                   pl.BlockSpec((1, HEAD), lambda c, h: (0, 0)),
                   pl.BlockSpec((HG_HEADS, HEAD), lambda c, h: (0, 0))],
        out_shape=outs, scratch_shapes=[pltpu.VMEM((HG_HEADS, HEAD, HEAD), F32)], input_output_aliases={13: 0},
        compiler_params=_params(_ARB, _ARB))(hq[0], hf[0], hi[0], hg[0], logits, gain, oraw, ssave, dog, mstack,
                                             mstack_t, masks, eye, dproj)


CONV_W = 512
CONV_ROWS = 1024


def _per_head(fn, *arrs):
    width = arrs[0].shape[1]
    return jnp.concatenate([fn(*[a[:, j:j + HEAD] for a in arrs]) for j in range(0, width, HEAD)], axis=1)


def _shift_down(xv, halo, d, top_rows):
    if d == 0:
        return xv, xv[0:8]
    main = pltpu.roll(xv, d, 0)
    top = jnp.where(top_rows < d, pltpu.roll(halo, d, 0), main[0:8])
    return main, top


def _conv_parts(x_ref, halo_ref, w_ref, first):
    xv = x_ref[...]
    halo = jnp.where(first, 0.0, halo_ref[...])
    top_rows = lax.broadcasted_iota(jnp.int32, (8, xv.shape[1]), 0)
    shifted = [_shift_down(xv, halo, CONV_K - 1 - j, top_rows) for j in range(CONV_K)]
    w = w_ref[...]
    acc = sum(shifted[j][0] * w[j:j + 1, :] for j in range(CONV_K))
    acc_top = sum(shifted[j][1] * w[j:j + 1, :] for j in range(CONV_K))
    return shifted, acc, acc_top


def _conv_fwd(x, w8, l2scale, name):
    x, off, width = x
    t = x.shape[0]
    o = off // CONV_W
    tr = _pick(t, CONV_ROWS, 8)

    def post(cv):
        s = _silu(cv)
        if l2scale is not None:
            s = _per_head(lambda sh: sh * (lax.rsqrt(_rowsum(sh * sh) + EPS) * l2scale), s)
        return s

    def body(x_ref, halo_ref, w_ref, o_ref):
        _, acc, acc_top = _conv_parts(x_ref, halo_ref, w_ref, pl.program_id(1) == 0)
        o_ref[...] = post(acc)
        o_ref[0:8, :] = post(acc_top)

    return pl.pallas_call(
        body, name=name, grid=(width // CONV_W,t // tr),
        in_specs=[pl.BlockSpec((tr, CONV_W), lambda j, i: (i, o + j)),
                  pl.BlockSpec((8, CONV_W), lambda j, i: (jnp.maximum(i * (tr // 8) - 1, 0), o + j)),
                  pl.BlockSpec((8, CONV_W), lambda j, i: (0, j))],
        out_specs=pl.BlockSpec((tr, CONV_W), lambda j, i: (i, j)),
        out_shape=jax.ShapeDtypeStruct((t, width), F32), compiler_params=_params(_PAR, _PAR))(x, x, w8)


def _conv_bwd_a(x, w8, dy, l2scale, name):
    x, off, width = x
    t = x.shape[0]
    o = off // CONV_W
    tr = _pick(t, CONV_ROWS, 8)

    def l2_bwd(s, dyh):
        r = lax.rsqrt(_rowsum(s * s) + EPS)
        y0 = s * r
        dy0 = dyh * l2scale
        return r * (dy0 - y0 * _rowsum(dy0 * y0))

    def to_dc(cv, dyv):
        if l2scale is not None:
            dyv = _per_head(l2_bwd, _silu(cv), dyv)
        return dyv * _dsilu(cv)

    def body(x_ref, halo_ref, w_ref, dy_ref, dc_ref, dw_ref):
        @pl.when(pl.program_id(1) == 0)
        def _():
            dw_ref[...] = jnp.zeros_like(dw_ref)

        shifted, acc, acc_top = _conv_parts(x_ref, halo_ref, w_ref, pl.program_id(1) == 0)
        dyv = dy_ref[...]
        dc = to_dc(acc, dyv)
        dc_top = to_dc(acc_top, dyv[0:8])
        dc_ref[...] = dc
        dc_ref[0:8, :] = dc_top
        rest = (lax.broadcasted_iota(jnp.int32, dc.shape, 0) >= 8).astype(F32)
        dc_rest = dc * rest
        for j in range(CONV_K):
            dw_ref[j:j + 1, :] += (jnp.sum(dc_rest * shifted[j][0], axis=0, keepdims=True)
                                   + jnp.sum(dc_top * shifted[j][1], axis=0, keepdims=True))

    return pl.pallas_call(
        body, name=name, grid=(width // CONV_W,t // tr),
        in_specs=[pl.BlockSpec((tr, CONV_W), lambda j, i: (i, o + j)),
                  pl.BlockSpec((8, CONV_W), lambda j, i: (jnp.maximum(i * (tr // 8) - 1, 0), o + j)),
                  pl.BlockSpec((8, CONV_W), lambda j, i: (0, j)),
                  pl.BlockSpec((tr, CONV_W), lambda j, i: (i, j))],
        out_specs=[pl.BlockSpec((tr, CONV_W), lambda j, i: (i, j)), pl.BlockSpec((8, CONV_W), lambda j, i: (0, j))],
        out_shape=[jax.ShapeDtypeStruct((t, width), F32), jax.ShapeDtypeStruct((8, width), F32)],
        compiler_params=_params(_PAR, _ARB))(x, x, w8, dy)


def _conv_bwd_b(dc, w8, name, into):
    t, width = dc.shape
    tr = _pick(t, CONV_ROWS, 8)
    nt = t // tr

    dproj, off, into_width = into
    assert into_width == width and off % CONV_W == 0
    o = off // CONV_W

    def body(dc_ref, halo_ref, w_ref, _, dx_ref):
        dcv = dc_ref[...]
        halo = jnp.where(pl.program_id(1) == nt - 1, 0.0, halo_ref[...])
        w = w_ref[...]
        bot_rows = lax.broadcasted_iota(jnp.int32, (8, CONV_W), 0)
        acc = dcv * w[CONV_K - 1:CONV_K, :]
        acc_bot = dcv[tr - 8:tr] * w[CONV_K - 1:CONV_K, :]
        for d in range(1, CONV_K):
            main = pltpu.roll(dcv, tr - d, 0)
            bot = jnp.where(bot_rows >= 8 - d, pltpu.roll(halo, 8 - d, 0), main[tr - 8:tr])
            wj = w[CONV_K - 1 - d:CONV_K - d, :]
            acc = acc + main * wj
            acc_bot = acc_bot + bot * wj
        dx_ref[...] = acc.astype(BF16)
        dx_ref[tr - 16:tr, :] = jnp.concatenate([acc[tr - 16:tr - 8], acc_bot], axis=0).astype(BF16)

    return pl.pallas_call(
        body, name=name, grid=(width // CONV_W,nt),
        in_specs=[pl.BlockSpec((tr, CONV_W), lambda j, i: (i, j)),
                  pl.BlockSpec((8, CONV_W), lambda j, i: (jnp.minimum((i + 1) * (tr // 8), t // 8 - 1), j)),
                  pl.BlockSpec((8, CONV_W), lambda j, i: (0, j)), _ANY],
        out_specs=pl.BlockSpec((tr, CONV_W), lambda j, i: (i, o + j)),
        out_shape=jax.ShapeDtypeStruct(dproj.shape, dproj.dtype), input_output_aliases={3: 0},
        compiler_params=_params(_PAR, _PAR))(dc, dc, w8, dproj)


def _each(f, *lists):
    return [f(*xs) for xs in zip(*lists)]


def _split2_each(xs):
    hi = [_bf(x) for x in xs]
    lo = [_bf(x - h.astype(F32)) for x, h in zip(xs, hi)]
    return hi, lo


def _hp_each(a_split, b_split):
    (ah, al), (bh, bl) = a_split, b_split
    rows = ah[0].shape[0]
    d12 = [jnp.dot(jnp.concatenate([x, y], axis=0), z, preferred_element_type=F32) for x, y, z in zip(ah, al, bh)]
    d3 = [jnp.dot(x, y, preferred_element_type=F32) for x, y in zip(ah, bl)]
    return [d[:rows] + d[rows:] + e for d, e in zip(d12, d3)]


INV_EXACT_STEPS = 1


def _tri_inv_each(a_list, eye):
    ns = [-a for a in a_list]
    ps = [eye + n for n in ns]
    n_split = _split2_each(ns)
    for step in range(5):
        if step < INV_EXACT_STEPS:
            ns = _hp_each(n_split, n_split)
            n_split = _split2_each(ns)
            ps = [p + d for p, d in zip(ps, _hp_each(_split2_each(ps), n_split))]
        else:
            nb = n_split[0] if step == INV_EXACT_STEPS else [_bf(n) for n in ns]
            ns = [jnp.dot(x, x, preferred_element_type=F32) for x in nb]
            nb2 = [_bf(n) for n in ns]
            ps = [p + jnp.dot(_bf(p), y, preferred_element_type=F32) for p, y in zip(ps, nb2)]
    return ps


def _gd_gates(gab, alog, dtb):
    sp_arg = gab + dtb
    return sp_arg, -jnp.exp(alog) * _softplus(sp_arg), _sigmoid(gab)


def _pick_lane(tile, base, head):
    g, hh = head
    col = tile[:, base + hh:base + hh + 1]
    for gi in range(1, GD_HEADS // HPS):
        lane = base + gi * HPS + hh
        col = jnp.where(g == gi, tile[:, lane:lane + 1], col)
    return col


def _gd_chunks(q, k, v, g_all, beta_all, heads, l_ref, mask_ref, tm=None):
    incl, strict, eye, upper = mask_ref[0], mask_ref[1], mask_ref[2], mask_ref[3]
    lmat = l_ref[...]
    gb = [jnp.broadcast_to(_pick_lane(g_all, 0, s), (CHUNK, HEAD)) for s in heads]
    bb = [jnp.broadcast_to(_pick_lane(beta_all, GD_HEADS, s), (CHUNK, HEAD)) for s in heads]
    gam = _mx_each(lmat, gb)
    gam_row = [jnp.sum(x[:, :CHUNK] * upper, axis=0, keepdims=True) for x in gb]
    lm = _each(lambda gm, gr: incl * jnp.exp(jnp.minimum(gm[:, :CHUNK] - gr, 0.0)), gam, gam_row)
    kb = _each(lambda x, b: x * b, k, bb)
    a = _each(lambda x, y, m: strict * _dot_nt(x, y) * m, kb, k, lm)
    if tm is None:
        tm = _tri_inv_each(a, eye)
    eg = [jnp.exp(x) for x in gam]
    vb = _each(lambda x, b: x * b, v, bb)
    kbg = _each(lambda x, e: x * e, kb, eg)
    uw = _each(lambda t_, x, y: _dot(t_, jnp.concatenate([x, y], axis=1)), tm, vb, kbg)
    u = [x[:, :HEAD] for x in uw]
    w = [x[:, HEAD:] for x in uw]
    qk = _each(lambda x, y, m: _dot_nt(x, y) * m, q, k, lm)
    g_end = [x[CHUNK - 1:CHUNK, :] for x in gam]
    ekg = _each(lambda e, x: jnp.exp(e - x), g_end, gam)
    ge = [jnp.exp(e) for e in g_end]
    kg = _each(lambda x, e: x * e, k, ekg)
    qg = _each(lambda x, e: x * e, q, eg)
    names = ("bb", "lm", "kb", "a", "tm", "eg", "vb", "kbg", "u", "w", "qk", "ekg", "ge", "kg", "qg")
    cols = (bb, lm, kb, a, tm, eg, vb, kbg, u, w, qk, ekg, ge, kg, qg)
    return [dict(zip(names, vals)) for vals in zip(*cols)]


def _gd_specs(rows, rev_nb=None):
    def cidx(c):
        return c if rev_nb is None else rev_nb - 1 - c

    qk_tile = pl.BlockSpec((rows, HPS // 2 * HEAD), lambda c, g: (cidx(c), g))
    v_tile = pl.BlockSpec((rows, HPS * HEAD), lambda c, g: (cidx(c), g))
    gab_tile = pl.BlockSpec((rows, HEAD), lambda c, g: (cidx(c), 0))
    return qk_tile, v_tile, gab_tile


def _gdn_fwd(qn, kn, cv, gab, gz, alog, dtb, gain, consts):
    t = qn.shape[0]
    nc = t // CHUNK
    cb = _chunks_per_step(nc)
    rows = cb * CHUNK
    lmat, _, masks = consts
    qk_tile, v_tile, gab_tile = _gd_specs(rows)
    row128 = pl.BlockSpec((1, HEAD), lambda c, h: (0, 0))

    def body(q_ref, k_ref, v_ref, gab_ref, gz_ref, alog_ref, dtb_ref, gain_ref, l_ref, mask_ref,
             oraw_ref, og_ref, ssave_ref, tsave_ref, state):
        c = pl.program_id(0)
        g = pl.program_id(1)

        @pl.when(c == 0)
        def _():
            for hh in range(HPS):
                state[g * HPS + hh] = jnp.zeros((HEAD, HEAD), F32)

        alog = alog_ref[...]
        dtb = dtb_ref[...]
        gain_v = gain_ref[...]

        def one(i, carry):
            sl = pl.ds(pl.multiple_of(i * CHUNK, CHUNK), CHUNK)
            _, g_all, beta_all = _gd_gates(gab_ref[sl, :], alog, dtb)
            heads = [g * HPS + hh for hh in range(HPS)]
            lq = [slice(hh // 2 * HEAD, (hh // 2 + 1) * HEAD) for hh in range(HPS)]
            lv = [slice(hh * HEAD, (hh + 1) * HEAD) for hh in range(HPS)]
            chs = _gd_chunks([q_ref[sl, s] for s in lq], [k_ref[sl, s] for s in lq], [v_ref[sl, s] for s in lv],
                             g_all, beta_all, [(g, hh) for hh in range(HPS)], l_ref, mask_ref)
            s0 = [state[h] for h in heads]
            ws = _each(lambda ch, s: _dot(jnp.concatenate([ch["w"], ch["qg"]], axis=0), s), chs, s0)
            v_new = _each(lambda ch, x: ch["u"] - x[:CHUNK], chs, ws)
            o = _each(lambda ch, x, vn: x[CHUNK:] + _dot(ch["qk"], vn), chs, ws, v_new)
            s1 = _each(lambda ch, s, vn: s * ch["ge"] + _dot_tn(ch["kg"], vn), chs, s0, v_new)
            for hh in range(HPS):
                ssave_ref[i, hh] = s0[hh]
                tsave_ref[i, hh] = chs[hh]["tm"]
                state[heads[hh]] = s1[hh]
                oraw_ref[sl, lv[hh]] = o[hh]
                r = lax.rsqrt(jnp.mean(o[hh] * o[hh], axis=1, keepdims=True) + EPS)
                og_ref[sl, lv[hh]] = (o[hh] * r * gain_v * _silu(gz_ref[sl, lv[hh]])).astype(BF16)
            return carry

        lax.fori_loop(0, cb, one, 0, unroll=4)

    return pl.pallas_call(
        body, name="gdn_fwd", grid=(nc // cb, GD_HEADS // HPS),
        in_specs=[qk_tile, qk_tile, v_tile, gab_tile, _view_tile(gz, rows, HPS * HEAD), row128, row128, row128,
                  pl.BlockSpec(lmat.shape, lambda c, g: (0, 0)),
                  pl.BlockSpec(masks.shape, lambda c, g: (0, 0, 0))],
        out_specs=[v_tile, v_tile, pl.BlockSpec((cb, HPS, HEAD, HEAD), lambda c, g: (c, g, 0, 0)),
                   pl.BlockSpec((cb, HPS, CHUNK, CHUNK), lambda c, g: (c, g, 0, 0))],
        out_shape=[jax.ShapeDtypeStruct((t, GD_HEADS * HEAD), F32), jax.ShapeDtypeStruct((t, GD_HEADS * HEAD), BF16),
                   jax.ShapeDtypeStruct((nc, GD_HEADS, HEAD, HEAD), F32),
                   jax.ShapeDtypeStruct((nc, GD_HEADS, CHUNK, CHUNK), F32)],
        scratch_shapes=[pltpu.VMEM((GD_HEADS, HEAD, HEAD), F32)],
        compiler_params=_params(_ARB, _ARB))(qn, kn, cv, gab, gz[0], alog, dtb, gain, lmat, masks)


def _gdn_bwd(qn, kn, cv, gab, gz, alog, dtb, gain, oraw, ssave, tsave, dog, consts, into):
    t = qn.shape[0]
    dproj, off, width = into
    assert width == GD_HEADS * HEAD and off % (HPS * HEAD) == 0
    nc = t // CHUNK
    cb = _chunks_per_step(nc)
    rows = cb * CHUNK
    nb = nc // cb
    lmat, lmat_t, masks = consts
    qk_tile, v_tile, gab_tile = _gd_specs(rows, nb)
    row128 = pl.BlockSpec((1, HEAD), lambda c, h: (0, 0))

    def body(q_ref, k_ref, v_ref, gab_ref, gz_ref, alog_ref, dtb_ref, gain_ref, oraw_ref, ssave_ref, tsave_ref, dog_ref,
             l_ref, lt_ref, mask_ref, _,
             dq_ref, dk_ref, dv_ref, dgab_ref, dgz_ref, small_ref, dstate):
        c = pl.program_id(0)
        g = pl.program_id(1)

        @pl.when(c == 0)
        def _():
            for hh in range(HPS):
                dstate[g * HPS + hh] = jnp.zeros((HEAD, HEAD), F32)

        @pl.when((c == 0) & (g == 0))
        def _():
            small_ref[...] = jnp.zeros_like(small_ref)

        alog = alog_ref[...]
        dtb = dtb_ref[...]
        gain_v = gain_ref[...]
        lane = lax.broadcasted_iota(jnp.int32, (1, HEAD), 1)
        last_row = (lax.broadcasted_iota(jnp.int32, (CHUNK, HEAD), 0) == CHUNK - 1).astype(F32)

        def one(j, carry):
            i = cb - 1 - j
            sl = pl.ds(pl.multiple_of(i * CHUNK, CHUNK), CHUNK)
            sp_arg, g_all, beta_all = _gd_gates(gab_ref[sl, :], alog, dtb)
            strict, eye = mask_ref[1], mask_ref[2]
            ltm = lt_ref[...]
            hs = range(HPS)
            heads = [g * HPS + hh for hh in hs]
            lq = [slice(hh // 2 * HEAD, (hh // 2 + 1) * HEAD) for hh in hs]
            lv = [slice(hh * HEAD, (hh + 1) * HEAD) for hh in hs]
            q = [q_ref[sl, s] for s in lq]
            k = [k_ref[sl, s] for s in lq]
            v = [v_ref[sl, s] for s in lv]
            gzv = [gz_ref[sl, s] for s in lv]
            chs = _gd_chunks(q, k, v, g_all, beta_all, [(g, hh) for hh in hs], l_ref, mask_ref,
                             tm=[tsave_ref[i, hh] for hh in hs])

            def col(name):
                return [ch[name] for ch in chs]

            def mul(x, y):
                return x * y

            tm, lm, eg, bb = col("tm"), col("lm"), col("eg"), col("bb")
            s0 = [ssave_ref[i, hh] for hh in hs]
            ds = [dstate[h] for h in heads]
            v_new = _each(lambda u, w, s: u - _dot(w, s), col("u"), col("w"), s0)

            o = [oraw_ref[sl, s] for s in lv]
            r = [lax.rsqrt(jnp.mean(x * x, axis=1, keepdims=True) + EPS) for x in o]
            on = _each(mul, o, r)
            dg_out = [dog_ref[sl, s] for s in lv]
            sgate = [_silu(x) for x in gzv]
            for hh in hs:
                dgz_ref[sl, lv[hh]] = (dg_out[hh] * on[hh] * gain_v * _dsilu(gzv[hh])).astype(BF16)
            small_ref[0:1, :] += sum(jnp.sum(d * s * n, axis=0, keepdims=True) for d, s, n in zip(dg_out, sgate, on))
            don = _each(lambda d, s: d * s * gain_v, dg_out, sgate)
            do = _each(lambda rr, dn, n: rr * (dn - n * jnp.mean(dn * n, axis=1, keepdims=True)), r, don, on)

            dv_new = _each(lambda a, d, b, s: _dot_tn(a, d) + _dot(b, s), col("qk"), do, col("kg"), ds)
            dqk = _each(_dot_nt, do, v_new)
            dkg = _each(_dot_nt, v_new, ds)
            dge = _each(lambda s, d: jnp.sum(_rowsum(s * d), axis=0, keepdims=True), s0, ds)
            both = _each(lambda d, dv: jnp.concatenate([d, dv], axis=0), do, dv_new)
            from_s = _each(_dot_nt, both, s0)
            dqg = [x[:CHUNK] for x in from_s]
            dw = [-x[CHUNK:] for x in from_s]
            ds_new = _each(lambda qg, w, bo, ge, s: _dot_tn(jnp.concatenate([qg, -w], axis=0), bo) + ge * s,
                           col("qg"), col("w"), both, col("ge"), ds)
            for hh in hs:
                dstate[heads[hh]] = ds_new[hh]

            side = _each(lambda dv, d: jnp.concatenate([dv, d], axis=1), dv_new, dw)
            back = _each(_dot_tn, tm, side)
            dvb = [x[:, :HEAD] for x in back]
            dkbg = [x[:, HEAD:] for x in back]
            dtm = _each(lambda sd, vb, kbg: _dot_nt(sd, jnp.concatenate([vb, kbg], axis=1)), side, col("vb"), col("kbg"))
            dtt = _each(_dot_nt, dtm, tm)
            da = _each(lambda t_, x: -_dot_tn(t_, x) * strict, tm, dtt)
            dal = _each(mul, da, lm)
            dqk_l = _each(mul, dqk, lm)
            stack = _each(lambda x, y: jnp.concatenate([x, y], axis=0), dal, dqk_l)
            on_k = _each(_dot, stack, k)
            dkb = _each(lambda x, y, e: x[:CHUNK] + y * e, on_k, dkbg, eg)
            dq = _each(lambda x, y, e: x[CHUNK:] + y * e, on_k, dqg, eg)
            dk = _each(lambda st, kb, qq, z, ekg, w_, b: _dot_tn(st, jnp.concatenate([kb, qq], axis=0)) + z * ekg + w_ * b,
                       stack, col("kb"), q, dkg, col("ekg"), dkb, bb)
            gmat = _each(lambda x, a, y, qk: x * a + y * qk, da, col("a"), dqk, col("qk"))
            t_kg = _each(lambda x, y: _rowsum(x * y), dkg, col("kg"))
            dgam = _each(lambda gm, x, qg, t_, y, kbg: (_rowsum(gm) - _row_to_col(jnp.sum(gm, axis=0, keepdims=True), eye)
                                                        + _rowsum(x * qg) - t_ + _rowsum(y * kbg)),
                         gmat, dqg, col("qg"), t_kg, dkbg, col("kbg"))
            dg_end = _each(lambda t_, e, ge: jnp.sum(t_, axis=0, keepdims=True) + e * ge[:, 0:1], t_kg, dge, col("ge"))
            dgam = _each(lambda x, e: x + last_row * e, dgam, dg_end)
            dbeta = _each(lambda x, kk, y, vv: _rowsum(x * kk) + _rowsum(y * vv), dkb, k, dvb, v)
            dg = _mx_each(ltm, dgam)

            for hh in hs:
                dv_ref[sl, lv[hh]] = dvb[hh] * bb[hh]
            fac_g = -jnp.exp(alog) * _sigmoid(sp_arg)
            fac_b = beta_all * (1.0 - beta_all)
            hot_g = [(lane == h).astype(F32) for h in heads]
            hot_b = [(lane == GD_HEADS + h).astype(F32) for h in heads]
            dga = _each(lambda x, hot: x * hot * fac_g, dg, hot_g)
            dgb = _each(lambda x, hot: x * hot * fac_b, dbeta, hot_b)
            small_ref[1:2, :] += sum(jnp.sum(x, axis=0, keepdims=True) for x in dga)
            small_ref[2:3, :] += sum(jnp.sum(x * hot * g_all, axis=0, keepdims=True) for x, hot in zip(dg, hot_g))
            for pair in range(HPS // 2):
                lqp = slice(pair * HEAD, (pair + 1) * HEAD)
                dq_ref[sl, lqp] = dq[2 * pair] + dq[2 * pair + 1]
                dk_ref[sl, lqp] = dk[2 * pair] + dk[2 * pair + 1]
            dgab_ref[sl, :] = sum(a + b for a, b in zip(dga, dgb))
            return carry

        lax.fori_loop(0, cb, one, 0, unroll=4)

    groups = GD_HEADS // HPS
    outs = [jax.ShapeDtypeStruct((t, 1024), F32), jax.ShapeDtypeStruct((t, 1024), F32),
            jax.ShapeDtypeStruct((t, 2048), F32), jax.ShapeDtypeStruct((t, groups * HEAD), F32),
            jax.ShapeDtypeStruct(dproj.shape, dproj.dtype), jax.ShapeDtypeStruct((8, HEAD), F32)]
    dgz_tile = pl.BlockSpec((rows, HPS * HEAD), lambda c, g: (nb - 1 - c, off // (HPS * HEAD) + g))
    return pl.pallas_call(
        body, name="gdn_bwd", grid=(nb, groups),
        in_specs=[qk_tile, qk_tile, v_tile, gab_tile, _view_tile(gz, rows, HPS * HEAD, lambda c: nb - 1 - c),
                  row128, row128, row128, v_tile,
                  pl.BlockSpec((cb, HPS, HEAD, HEAD), lambda c, g: (nb - 1 - c, g, 0, 0)),
                  pl.BlockSpec((cb, HPS, CHUNK, CHUNK), lambda c, g: (nb - 1 - c, g, 0, 0)), v_tile,
                  pl.BlockSpec(lmat.shape, lambda c, g: (0, 0)),
                  pl.BlockSpec(lmat_t.shape, lambda c, g: (0, 0)),
                  pl.BlockSpec(masks.shape, lambda c, g: (0, 0, 0)), _ANY],
        out_specs=[qk_tile, qk_tile, v_tile, pl.BlockSpec((rows, HEAD), lambda c, g: (nb - 1 - c, g)), dgz_tile,
                   pl.BlockSpec((8, HEAD), lambda c, g: (0, 0))],
        out_shape=outs, scratch_shapes=[pltpu.VMEM((GD_HEADS, HEAD, HEAD), F32)], input_output_aliases={15: 4},
        compiler_params=_params(_ARB, _ARB))(qn, kn, cv, gab, gz[0], alog, dtb, gain, oraw, ssave, tsave, dog,
                                             lmat, lmat_t, masks, dproj)


def _fold_groups(wide):
    t, width = wide.shape
    tr = _pick(t, CONV_ROWS, 8)

    def body(w_ref, o_ref):
        acc = w_ref[:, 0:HEAD]
        for j in range(1, width // HEAD):
            acc = acc + w_ref[:, j * HEAD:(j + 1) * HEAD]
        o_ref[...] = acc.astype(BF16)

    return pl.pallas_call(
        body, name="fold_gate_grads", grid=(t // tr,), in_specs=[_row_spec(tr, width)], out_specs=_row_spec(tr, HEAD),
        out_shape=jax.ShapeDtypeStruct((t, HEAD), BF16), compiler_params=_params(_PAR))(wide)


def _adam_math(w, g, m, v):
    m2 = ADAM_B1 * m + (1.0 - ADAM_B1) * g
    v2 = ADAM_B2 * v + (1.0 - ADAM_B2) * (g * g)
    m_hat = m2 / (1.0 - ADAM_B1 ** ADAM_STEP)
    v_hat = v2 / (1.0 - ADAM_B2 ** ADAM_STEP)
    delta = -ADAM_LR * (m_hat / (jnp.sqrt(v_hat) + ADAM_EPS) + ADAM_WD * w)
    return delta, m2, v2


def _adamw(w, g, m, v, name, after=None):
    r, c = w.shape
    tr = r
    for cand in range(8, r + 1, 8):
        if r % cand == 0 and cand * c * 4 <= (2 << 20):
            tr = cand
    if r % 8 != 0:
        tr = r

    def body(w_ref, g_ref, m_ref, v_ref, *rest):
        d_ref, m2_ref, v2_ref = rest[-3:]
        d, m2, v2 = _adam_math(w_ref[...], g_ref[...], m_ref[...], v_ref[...])
        d_ref[...] = d
        m2_ref[...] = m2
        v2_ref[...] = v2

    spec = pl.BlockSpec((tr, c), lambda i: (i, 0))
    extra = [] if after is None else [after]
    return pl.pallas_call(
        body, name=name, grid=(r // tr,), in_specs=[spec] * 4 + [_ANY] * len(extra), out_specs=[spec] * 3,
        out_shape=[jax.ShapeDtypeStruct((r, c), F32)] * 3, compiler_params=_params(_PAR))(w, g, m, v, *extra)


_ANY = pl.BlockSpec(memory_space=pl.ANY)


def _place():
    return lax.axis_index("x"), lax.axis_index("y"), lax.axis_index("c")


def _gather_weights(packs, nchs, name):
    n = len(packs)
    halves = [p.shape[0] // 2 for p in packs]
    base = [sum(nchs[:i]) for i in range(n)]
    total = sum(nchs)
    for p, h, k in zip(packs, halves, nchs):
        assert p.shape[0] == 2 * h and h % k == 0 and (h // k) % 16 == 0

    def body(*refs):
        p_refs, g_refs, (send_sems, recv_sems) = refs[:n], refs[n:2 * n], refs[2 * n:]
        x, y, c = _place()
        sibling = (x, y, 1 - c)
        chips = [(1 - x, y), (x, 1 - y), (1 - x, 1 - y)]
        chunks = [(a, q) for a in range(n) for q in range(nchs[a])]

        def rows_of(a, pc, q):
            ch = halves[a] // nchs[a]
            return pl.ds(pl.multiple_of(pc * halves[a] + q * ch, 16), ch)

        def piece(a, px, py, pc, q):
            return g_refs[a].at[2 * px + py, rows_of(a, pc, q), :]

        def copy(k, src, dst, to):
            return pltpu.make_async_remote_copy(src_ref=src, dst_ref=dst, send_sem=send_sems.at[k],
                                                recv_sem=recv_sems.at[k], device_id=to, device_id_type=MESH)

        def sem_of(j, a, q):
            return j * total + base[a] + q

        first = {(j, a, q): copy(sem_of(j, a, q), p_refs[a].at[rows_of(a, c, q), :], piece(a, x, y, c, q), (*chip, c))
                 for j, chip in enumerate(chips) for a, q in chunks}
        for a, q in chunks:
            for j in range(3):
                first[j, a, q].start()
        passed = {(j, a, q): copy(sem_of(3 + j, a, q), piece(a, *chip, c, q), piece(a, *chip, c, q), sibling)
                  for j, chip in enumerate(chips) for a, q in chunks}
        for a, q in chunks:
            for j, chip in enumerate(chips):
                copy(sem_of(j, a, q), p_refs[a].at[rows_of(a, c, q), :], piece(a, *chip, c, q), (*chip, c)).wait_recv()
                passed[j, a, q].start()
        for a, q in chunks:
            for j, chip in enumerate(chips):
                copy(sem_of(3 + j, a, q), piece(a, *chip, 1 - c, q), piece(a, *chip, 1 - c, q), sibling).wait_recv()
        for key in first:
            first[key].wait_send()
            passed[key].wait_send()

    return pl.pallas_call(
        body, name=name, out_shape=[jax.ShapeDtypeStruct((4,) + p.shape, p.dtype) for p in packs],
        in_specs=[_ANY] * n, out_specs=[_ANY] * n,
        scratch_shapes=[pltpu.SemaphoreType.DMA((6 * total,)), pltpu.SemaphoreType.DMA((6 * total,))])(*packs)


def _swap_with_sibling(arrs, nchs, lead, name, halves=False):
    n = len(arrs)
    jobs = []
    hs = [arr.shape[-2] // (2 if halves else 1) for arr in arrs]
    for a, (h, k) in enumerate(zip(hs, nchs)):
        assert h % k == 0 and (h // k) % 16 == 0
        for s in (range(lead) if lead else [None]):
            jobs += [(a, s, q * (h // k), h // k) for q in range(k)]

    def body(*refs):
        src, dst, (send_sems, recv_sems) = refs[:n], refs[n:2 * n], refs[2 * n:]
        x, y, c = _place()

        def at(ref, s, r0, rows):
            return ref.at[pl.ds(r0, rows), :] if s is None else ref.at[s, pl.ds(r0, rows), :]

        def src_rows(a, r0):
            return pl.multiple_of((1 - c) * hs[a] + r0, 16) if halves else r0

        copies = [pltpu.make_async_remote_copy(
            src_ref=at(src[a], s, src_rows(a, r0), rows), dst_ref=at(dst[a], s, r0, rows), send_sem=send_sems.at[k],
            recv_sem=recv_sems.at[k], device_id=(x, y, 1 - c), device_id_type=MESH)
            for k, (a, s, r0, rows) in enumerate(jobs)]
        for cp in copies:
            cp.start()
        for cp in copies:
            cp.wait()

    shapes = [jax.ShapeDtypeStruct(arr.shape[:-2] + (h, arr.shape[-1]), arr.dtype) for arr, h in zip(arrs, hs)]
    return pl.pallas_call(
        body, name=name, out_shape=shapes, in_specs=[_ANY] * n, out_specs=[_ANY] * n,
        scratch_shapes=[pltpu.SemaphoreType.DMA((len(jobs),)), pltpu.SemaphoreType.DMA((len(jobs),))])(*arrs)


def _add2(full, b, core, name):
    n, rows, w = b.shape
    tr = _pick(rows, 256, 16)
    nblk = rows // tr

    def body(c_ref, a_ref, b_ref, o_ref):
        o_ref[...] = (a_ref[...].astype(F32) + b_ref[...].astype(F32)).astype(BF16)

    spec = pl.BlockSpec((1, tr, w), lambda i, j, c_ref: (i, j, 0))
    grid_spec = pltpu.PrefetchScalarGridSpec(
        num_scalar_prefetch=1, grid=(n, nblk),
        in_specs=[pl.BlockSpec((1, tr, w), lambda i, j, c_ref: (i, c_ref[0] * nblk + j, 0)), spec], out_specs=spec)
    return pl.pallas_call(
        body, name=name, grid_spec=grid_spec, out_shape=jax.ShapeDtypeStruct(b.shape, BF16),
        compiler_params=_params(_PAR, _PAR))(core, full, b)


def _reduce_chips(partials, nchs, name):
    n = len(partials)
    jobs = []
    for a, (arr, k) in enumerate(zip(partials, nchs)):
        h = arr.shape[1]
        assert h % k == 0 and (h // k) % 16 == 0
        jobs += [(a, q * (h // k), h // k) for q in range(k)]

    def body(*refs):
        src, dst, (send_sems, recv_sems) = refs[:n], refs[n:2 * n], refs[2 * n:]
        x, y, c = _place()
        chips = [(1 - x, y), (x, 1 - y), (1 - x, 1 - y)]
        copies = [pltpu.make_async_remote_copy(
            src_ref=src[a].at[2 * px + py, pl.ds(r0, rows), :], dst_ref=dst[a].at[j, pl.ds(r0, rows), :],
            send_sem=send_sems.at[3 * k + j], recv_sem=recv_sems.at[3 * k + j],
            device_id=(px, py, c), device_id_type=MESH)
            for k, (a, r0, rows) in enumerate(jobs) for j, (px, py) in enumerate(chips)]
        for cp in copies:
            cp.start()
        for cp in copies:
            cp.wait()

    return pl.pallas_call(
        body, name=name,
        out_shape=[jax.ShapeDtypeStruct((3,) + p.shape[1:], p.dtype) for p in partials],
        in_specs=[_ANY] * n, out_specs=[_ANY] * n,
        scratch_shapes=[pltpu.SemaphoreType.DMA((3 * len(jobs),)), pltpu.SemaphoreType.DMA((3 * len(jobs),))])(*partials)


_HBM = pl.BlockSpec(memory_space=pltpu.HBM)
_SEM = pl.BlockSpec(memory_space=pltpu.SEMAPHORE)
_DATAFLOW = pltpu.SideEffectType.DATAFLOW_SIDE_EFFECTING


def _ici_jobs(srcs, nchs, kind):
    jobs = []
    for a, (arr, k) in enumerate(zip(srcs, nchs)):
        h = arr.shape[0] // 2 if kind == "gather" else arr.shape[1]
        assert h % k == 0 and (h // k) % 16 == 0
        jobs += [(a, h, q * (h // k), h // k) for q in range(k)]
    return jobs


def _ici_copies(src, land, send_sems, recv_sems, jobs, kind):
    x, y, c = _place()
    chips = [(1 - x, y), (x, 1 - y), (1 - x, 1 - y)]
    copies = []
    for k, (a, h, r0, rows) in enumerate(jobs):
        for j, (px, py) in enumerate(chips):
            if kind == "gather":
                at = pl.ds(pl.multiple_of(c * h + r0, 16), rows)
                s, d = src[a].at[at, :], land[a].at[2 * x + y, at, :]
            else:
                s, d = src[a].at[2 * px + py, pl.ds(r0, rows), :], land[a].at[j, pl.ds(r0, rows), :]
            copies.append(pltpu.make_async_remote_copy(
                src_ref=s, dst_ref=d, send_sem=send_sems.at[3 * k + j], recv_sem=recv_sems.at[3 * k + j],
                device_id=(px, py, c), device_id_type=MESH))
    return copies


def _ici_start(srcs, nchs, kind, name):
    n = len(srcs)
    jobs = _ici_jobs(srcs, nchs, kind)
    lead = (lambda s: (4,) + s.shape) if kind == "gather" else (lambda s: (3,) + s.shape[1:])
    lands = [lax.empty(lead(s), s.dtype) for s in srcs]

    def body(*refs):
        src, land = refs[:n], refs[n:2 * n]
        send_sems, recv_sems, token = refs[2 * n], refs[2 * n + 1], refs[-1]
        for cp in _ici_copies(src, land, send_sems, recv_sems, jobs, kind):
            cp.start()
        token[...] = jnp.zeros_like(token)

    hbm = [pltpu.HBM(a.shape, a.dtype) for a in srcs + lands]
    outs = pl.pallas_call(
        body, name=name,
        out_shape=[pltpu.SemaphoreType.DMA((3 * len(jobs),)), pltpu.SemaphoreType.DMA((3 * len(jobs),))] + hbm
        + [jax.ShapeDtypeStruct((8, 128), F32)],
        in_specs=[_HBM] * (2 * n), out_specs=[_SEM, _SEM] + [_HBM] * (2 * n) + [pl.BlockSpec(memory_space=pltpu.VMEM)],
        input_output_aliases={i: 2 + i for i in range(2 * n)},
        compiler_params=pltpu.CompilerParams(has_side_effects=_DATAFLOW),
    )(*[pltpu.with_memory_space_constraint(a, pltpu.HBM) for a in srcs + lands])
    return (outs[0], outs[1], list(outs[2:2 + n]), list(outs[2 + n:2 + 2 * n]), nchs, kind), outs[-1]


def _ici_wait(handle, after, name):
    send_sems, recv_sems, srcs, lands, nchs, kind = handle
    n = len(srcs)
    jobs = _ici_jobs(srcs, nchs, kind)

    def body(*refs):
        src, land = refs[:n], refs[n:2 * n]
        for cp in _ici_copies(src, land, refs[2 * n], refs[2 * n + 1], jobs, kind):
            cp.wait_send()
            cp.wait_recv()

    outs = pl.pallas_call(
        body, name=name, out_shape=[pltpu.HBM(a.shape, a.dtype) for a in srcs + lands],
        in_specs=[_HBM] * (2 * n) + [_SEM, _SEM, _ANY], out_specs=[_HBM] * (2 * n),
        input_output_aliases={i: i for i in range(2 * n)},
        compiler_params=pltpu.CompilerParams(has_side_effects=_DATAFLOW),
    )(*srcs, *lands, send_sems, recv_sems, after)
    return list(outs[:n]), list(outs[n:])


def _pass_to_sibling(gathered, nchs, name):
    n = len(gathered)
    jobs = _ici_jobs([jax.ShapeDtypeStruct(g.shape[1:], g.dtype) for g in gathered], nchs, "gather")

    def body(*refs):
        src, dst, (send_sems, recv_sems) = refs[:n], refs[n:2 * n], refs[2 * n:]
        x, y, c = _place()
        slots = [2 * (1 - x) + y, 2 * x + (1 - y), 2 * (1 - x) + (1 - y)]

        def copy(k, j, pc):
            a, h, r0, rows = jobs[k]
            at = pl.ds(pl.multiple_of(pc * h + r0, 16), rows)
            return pltpu.make_async_remote_copy(
                src_ref=src[a].at[slots[j], at, :], dst_ref=dst[a].at[slots[j], at, :], send_sem=send_sems.at[3 * k + j],
                recv_sem=recv_sems.at[3 * k + j], device_id=(x, y, 1 - c), device_id_type=MESH)

        pairs = [(k, j) for k in range(len(jobs)) for j in range(3)]
        for k, j in pairs:
            copy(k, j, c).start()
        for k, j in pairs:
            copy(k, j, c).wait_send()
            copy(k, j, 1 - c).wait_recv()

    return pl.pallas_call(
        body, name=name, out_shape=[jax.ShapeDtypeStruct(g.shape, g.dtype) for g in gathered],
        in_specs=[_ANY] * n, out_specs=[_ANY] * n, input_output_aliases={i: i for i in range(n)},
        scratch_shapes=[pltpu.SemaphoreType.DMA((3 * len(jobs),)), pltpu.SemaphoreType.DMA((3 * len(jobs),))])(*gathered)


def _add4(own, got, name):
    rows, w = own.shape
    tr = _pick(rows, 128, 16)

    def body(a_ref, b_ref, o_ref):
        o_ref[...] = ((a_ref[...].astype(F32) + b_ref[0].astype(F32)) + b_ref[1].astype(F32)) + b_ref[2].astype(F32)

    return pl.pallas_call(
        body, name=name, grid=(rows // tr,),
        in_specs=[pl.BlockSpec((tr, w), lambda i: (i, 0)), pl.BlockSpec((3, tr, w), lambda i: (0, i, 0))],
        out_specs=pl.BlockSpec((tr, w), lambda i: (i, 0)), out_shape=jax.ShapeDtypeStruct((rows, w), F32),
        compiler_params=_params(_PAR))(own, got)


def _small_sync(gs, ws, ms, vs):
    rows = gs.shape[0]
    vmem = pl.BlockSpec(memory_space=pltpu.VMEM)

    def body(g_ref, w_ref, m_ref, v_ref, sum_ref, d_ref, m2_ref, v2_ref, buf, send_sems, recv_sems):
        x, y, c = _place()
        me = 4 * x + 2 * y + c
        buf[me] = g_ref[...]
        copies = []
        for k in range(1, 8):
            peer = (x ^ (k >> 2), y ^ ((k >> 1) & 1), c ^ (k & 1))
            copies.append(pltpu.make_async_remote_copy(
                src_ref=g_ref, dst_ref=buf.at[me], send_sem=send_sems.at[k - 1], recv_sem=recv_sems.at[k - 1],
                device_id=peer, device_id_type=MESH))
        for cp in copies:
            cp.start()
        for cp in copies:
            cp.wait()
        total = buf[0]
        for i in range(1, 8):
            total = total + buf[i]
        sum_ref[...] = total
        d, m2, v2 = _adam_math(w_ref[...], total, m_ref[...], v_ref[...])
        d_ref[...] = d
        m2_ref[...] = m2
        v2_ref[...] = v2

    shape = jax.ShapeDtypeStruct((rows, 128), F32)
    return pl.pallas_call(
        body, name="small_sync", out_shape=[shape] * 4, in_specs=[vmem] * 4, out_specs=[vmem] * 4,
        scratch_shapes=[pltpu.VMEM((8, rows, 128), F32), pltpu.SemaphoreType.DMA((7,)),
                        pltpu.SemaphoreType.DMA((7,))])(gs, ws, ms, vs)


_GROUPS = {
    "ffn1": dict(cols=("ffn1_w_in", 1408), rows=(("ffn1_w_out", 704, 704),), chunks=(8, 2)),
    "ffn2": dict(cols=("ffn2_w_in", 1408), rows=(("ffn2_w_out", 704, 704),), chunks=(8, 2)),
    "mixer": dict(cols=("w_in", 3080), chunks=(8, 4),
                  rows=(("w_branch_hgrn", 256, 256), ("w_branch_gdn", 512, 512), ("w_out", 256, 256),
                        ("gdn_conv_w", CONV_K, 128))),
}
_BIG_NAMES = tuple(n for g in _GROUPS.values() for n in (g["cols"][0],) + tuple(r[0] for r in g["rows"]))


def _group_names(group):
    return (group["cols"][0],) + tuple(r[0] for r in group["rows"])


def _pack(parts, lead, group):
    ax = len(lead)
    rows = []
    for n, r, padded in group["rows"]:
        p = parts[n]
        if padded != r:
            p = jnp.tile(p, (1,) * ax + (padded // r, 1))
        rows.append(p)
    return [parts[group["cols"][0]], rows[0] if len(rows) == 1 else jnp.concatenate(rows, axis=ax)]


def _unpack(cols, rows, group):
    out, off = {group["cols"][0]: cols}, 0
    for n, r, padded in group["rows"]:
        out[n] = rows[..., off:off + r, :]
        off += padded
    return out


def _is_col_sharded(name):
    return name in ("ffn1_w_in", "ffn2_w_in", "w_in", "gdn_conv_w")


def _full_from_shards(name, g):
    if _is_col_sharded(name):
        return jnp.transpose(g, (1, 0, 2)).reshape(g.shape[1], -1)
    return g.reshape(-1, g.shape[2])


def _shards_from_full(name, full):
    if _is_col_sharded(name):
        return jnp.transpose(full.reshape(full.shape[0], 4, -1), (1, 0, 2))
    return full.reshape(4, -1, full.shape[1])


_SMALL = (("ffn1_norm", 8), ("mix_norm", 8), ("hgrn_lb_logits", 16), ("hgrn_out_norm", 8), ("gdn_a_log", 8),
          ("gdn_dt_bias", 8), ("gdn_out_norm", 8), ("ffn2_norm", 8), ("final_norm", 8), ("loss", 8))
_SMALL_ROWS = sum(r for _, r in _SMALL)


def _pack_small(parts):
    out = []
    for name, rows in _SMALL:
        p = parts[name].reshape(-1).astype(F32)
        if p.shape[0] <= 128:
            if p.shape[0] < 128:
                p = jnp.concatenate([p, jnp.zeros((128 - p.shape[0],), F32)])
            p = jnp.broadcast_to(p.reshape(1, 128), (rows, 128))
        out.append(p.reshape(rows, 128))
    return jnp.concatenate(out, axis=0)


def _unpack_small(packed, shapes):
    out, off = {}, 0
    for name, rows in _SMALL:
        n = int(np.prod(shapes[name]))
        out[name] = packed[off:off + rows].reshape(-1)[:n].reshape(shapes[name])
        off += rows
    return out


def _ffn_fwd(x, gain, w_in, w_out, tag):
    n = _rmsnorm_fwd(x, gain, tag + "_norm")
    a, b, hm = _ffn_in_act(n, w_in, tag + "_in")
    out = _mm(hm, w_out, alpha=0.5, res=x, name=tag + "_out")
    return out, (n, a, b)


def _ffn_bwd(x, gain, w_in, w_out, saved, dout, dout_bf, tag):
    n, a, b = saved
    da, db, hm = _ffn_dact(dout_bf, w_out, a, b, tag + "_dact")
    dw_out = _mm(hm, dout_bf, ta=True, alpha=0.5, out_dtype=BF16, name=tag + "_dwout")
    dwa = _mm(n, da, ta=True, out_dtype=BF16, name=tag + "_dwin_a")
    dwb = _mm(n, db, ta=True, out_dtype=BF16, name=tag + "_dwin_b")
    half = D_FF // 2
    dw_in = jnp.stack([dwa[:, :half], dwa[:, half:], dwb[:, :half], dwb[:, half:]])
    dn = _mm(da, w_in, tb=True, name=tag + "_dnorm_a")
    dn = _mm(db, w_in, tb=True, res=dn, b_from=D_FF, name=tag + "_dnorm_b")
    dx, dx_bf, dgain = _rmsnorm_bwd(x, gain, dn, dout, tag + "_dx")
    return dx, dx_bf, dgain, dw_in, dw_out


def _pad_lanes(v):
    return jnp.concatenate([v.reshape(1, -1), jnp.zeros((1, HEAD - v.size), F32)], axis=1)


def _local_step(x, tgt, small, exchange):
    hg_c = _hg_consts()
    gd_c = _gd_consts()
    alog = _pad_lanes(small["gdn_a_log"])
    dtb = _pad_lanes(small["gdn_dt_bias"])
    logits = small["hgrn_lb_logits"]
    hg_gain = small["hgrn_out_norm"].reshape(1, HEAD)
    gd_gain = small["gdn_out_norm"].reshape(1, HEAD)
    g1, gm, g2 = small["ffn1_norm"].reshape(1, -1), small["mix_norm"].reshape(1, -1), small["ffn2_norm"].reshape(1, -1)
    gf = small["final_norm"].reshape(1, -1)
    qscale = HEAD ** -0.5

    w1 = exchange.weights("ffn1")
    started = exchange.prefetch("mixer")
    h1, ffn1_saved = _ffn_fwd(x, g1 + started, w1["ffn1_w_in"], w1["ffn1_w_out"], "ffn1")
    u = _rmsnorm_fwd(h1, gm, "mix_norm")
    w = exchange.weights("mixer", after=u)
    started = exchange.prefetch("ffn2")
    seg, off = {}, 0
    for name, size in zip(IN_NAMES, IN_SIZES):
        seg[name] = w["w_in"][:, off:off + size]
        off += size
    w_gab = jnp.concatenate([seg["ga"], seg["gb"], jnp.zeros((D_MODEL, HEAD - 32), BF16)], axis=1)
    big_segs = [n for n in IN_NAMES if n not in ("ga", "gb")]
    conv8 = jnp.concatenate([w["gdn_conv_w"].astype(F32), jnp.zeros((8 - CONV_K, 4096), F32)], axis=0)
    conv_q, conv_k, conv_v = conv8[:, :1024], conv8[:, 1024:2048], conv8[:, 2048:]
    w_main = jnp.concatenate([seg[n] for n in big_segs], axis=1)
    proj = _mm(u, w_main, name="proj", tm_max=2048)
    pr, off = {}, 0
    for n in big_segs:
        pr[n] = _view(proj, off, seg[n].shape[1])
        off += seg[n].shape[1]
    gab = _mm(u, w_gab, name="proj_gab")
    oh_raw, oh, s_h = _hgrn_fwd(pr["hq"], pr["hf"], pr["hi"], pr["hg"], logits, hg_gain + started, hg_c)
    qn = _conv_fwd(pr["gq"], conv_q, qscale, "conv_q")
    kn = _conv_fwd(pr["gk"], conv_k, 1.0, "conv_k")
    cv = _conv_fwd(pr["gv"], conv_v, None, "conv_v")
    og_raw, og, s_g, t_g = _gdn_fwd(qn, kn, cv, gab, pr["gz"], alog, dtb, gd_gain, gd_c)
    yh = _mm(oh, w["w_branch_hgrn"], name="branch_h")
    yg = _mm(og, w["w_branch_gdn"], name="branch_g")
    ym = _merge_fwd(yh, yg, pr["gate_h"], pr["gate_g"])
    h2 = _mm(ym, w["w_out"], res=h1, name="mix_out")
    w2 = exchange.weights("ffn2", after=h2)
    h3, ffn2_saved = _ffn_fwd(h2, g2, w2["ffn2_w_in"], w2["ffn2_w_out"], "ffn2")
    loss, dh3, dh3_bf, d_gf = _final_loss(h3, gf, tgt)

    dh2, dh2_bf, d_g2, d_f2in, d_f2out = _ffn_bwd(h2, g2, w2["ffn2_w_in"], w2["ffn2_w_out"], ffn2_saved, dh3, dh3_bf,
                                                  "ffn2")
    started = exchange.reduce("ffn2", {"ffn2_w_in": d_f2in, "ffn2_w_out": d_f2out}, behind=True)
    dym =_mm(dh2_bf, w["w_out"], tb=True, name="d_merge")
    d_wout = _mm(ym, dh2_bf, ta=True, out_dtype=BF16, name="d_w_out")
    dproj = lax.empty((x.shape[0], w_main.shape[1]), BF16)
    dyh, dyg, dproj = _merge_bwd(dym, yh, yg, pr["gate_h"], pr["gate_g"], _into(dproj, pr["gate_h"][1], 2 * D_MODEL))
    d_wbh = _mm(oh, dyh, ta=True, out_dtype=BF16, name="d_w_branch_h")
    d_wbg = _mm(og, dyg, ta=True, out_dtype=BF16, name="d_w_branch_g")
    doh = _mm(dyh, w["w_branch_hgrn"], tb=True, name="d_oh")
    dog = _mm(dyg, w["w_branch_gdn"], tb=True, name="d_og")
    dproj, d_hg_gain, d_lb0 = _hgrn_bwd(pr["hq"], pr["hf"], pr["hi"], pr["hg"], logits, hg_gain + started, oh_raw,
                                        s_h, doh, hg_c, _into(dproj, pr["hq"][1], 4 * D_MODEL))
    d_qn, d_kn, d_cv, d_gab_wide, dproj, gd_small = _gdn_bwd(qn, kn, cv, gab, pr["gz"], alog, dtb, gd_gain, og_raw,
                                                             s_g, t_g, dog, gd_c, _into(dproj, *pr["gz"][1:]))
    d_gab = _fold_groups(d_gab_wide)
    dc_q, dwc_q = _conv_bwd_a(pr["gq"], conv_q, d_qn, qscale, "dconv_q")
    dc_k, dwc_k = _conv_bwd_a(pr["gk"], conv_k, d_kn, 1.0, "dconv_k")
    dc_v, dwc_v = _conv_bwd_a(pr["gv"], conv_v, d_cv, None, "dconv_v")
    dproj = _conv_bwd_b(dc_q, conv_q, "dconvx_q", _into(dproj, *pr["gq"][1:]))
    dproj = _conv_bwd_b(dc_k, conv_k, "dconvx_k", _into(dproj, *pr["gk"][1:]))
    dproj = _conv_bwd_b(dc_v, conv_v, "dconvx_v", _into(dproj, *pr["gv"][1:]))
    du =_mm(d_gab, w_gab, tb=True, name="du_gab")
    du = _mm(dproj, w_main, tb=True, res=du, name="du")
    d_wmain = _mm(u, dproj, ta=True, out_dtype=BF16, name="dw_main")
    d_wgab = _mm(u, d_gab, ta=True, out_dtype=BF16, name="dw_gab")
    cut = IN_WIDTH // 4
    d_win = jnp.stack([d_wmain[:, :cut], d_wmain[:, cut:2 * cut],
                       jnp.concatenate([d_wmain[:, 2 * cut:8192], d_wgab[:, :32], d_wmain[:, 8192:3 * cut - 32]], axis=1),
                       d_wmain[:, 3 * cut - 32:]])
    d_conv = jnp.concatenate([dwc_q[:CONV_K], dwc_k[:CONV_K], dwc_v[:CONV_K]], axis=1).astype(BF16)
    started = exchange.reduce("mixer", {"w_in": d_win, "gdn_conv_w": d_conv, "w_branch_hgrn": d_wbh,
                                        "w_branch_gdn": d_wbg, "w_out": d_wout}, behind=True)
    dh1, dh1_bf, d_gm = _rmsnorm_bwd(h1, gm + started, du, dh2, "mix_dnorm")
    dx, _, d_g1, d_f1in, d_f1out = _ffn_bwd(x, g1, w1["ffn1_w_in"], w1["ffn1_w_out"], ffn1_saved, dh1, dh1_bf, "ffn1")
    exchange.reduce("ffn1", {"ffn1_w_in": d_f1in, "ffn1_w_out": d_f1out}, behind=True)
    d_lb0 = d_lb0.reshape(1, -1)
    sm = {"ffn1_norm": d_g1, "mix_norm": d_gm, "hgrn_lb_logits": jnp.concatenate([d_lb0, -d_lb0], axis=0),
          "hgrn_out_norm": d_hg_gain, "gdn_a_log": gd_small[2, :16], "gdn_dt_bias": gd_small[1, :16],
          "gdn_out_norm": gd_small[0], "ffn2_norm": d_g2, "final_norm": d_gf, "loss": loss[0, :1]}
    return dx, sm


class _Exchange:
    def __init__(self, wts):
        self.wts = wts
        xi, yi, ci = _place()
        self.chip = 2 * xi + yi
        self.south = ci == 0
        self.core = ci.reshape(1).astype(jnp.int32)
        self.mine = {}
        self.coming = {}
        self.going = {}

    def _packs(self, tag):
        group = _GROUPS[tag]
        return _pack({n: self.wts[n][0].astype(BF16) for n in _group_names(group)}, (), group)

    def prefetch(self, tag):
        packs = self._packs(tag)
        handle, token = _ici_start(packs, _GROUPS[tag]["chunks"], "gather", "gather_start_" + tag)
        self.coming[tag] = handle
        return token[0:1, 0:1]

    def weights(self, tag, after=None):
        group = _GROUPS[tag]
        if tag in self.coming:
            packs, halves = _ici_wait(self.coming.pop(tag), after, "gather_wait_" + tag)
            others = _pass_to_sibling(halves, group["chunks"], "gather_pass_" + tag)
        else:
            packs = self._packs(tag)
            others = _gather_weights(packs, group["chunks"], "gather_" + tag)
        whole = [lax.dynamic_update_index_in_dim(g, p, self.chip, 0) for g, p in zip(others, packs)]
        gathered = _unpack(*whole, group)
        return {n: _full_from_shards(n, gathered[n]) for n in _group_names(group)}

    def reduce(self, tag, grads, behind=False):
        group = _GROUPS[tag]
        shards = {n: (grads[n] if grads[n].ndim == 3 else _shards_from_full(n, grads[n])) for n in _group_names(group)}
        gpacks = _pack(shards, (4,), group)
        got = _swap_with_sibling(gpacks, group["chunks"], 4, "reduce_pair_" + tag, halves=True)
        sums = [_add2(a, b, self.core, "add_pair_%s_%d" % (tag, i)) for i, (a, b) in enumerate(zip(gpacks, got))]
        if behind:
            handle, token = _ici_start(sums, group["chunks"], "reduce", "reduce_start_" + tag)
            self.going[tag] = handle
            self.token = token
            return token[0:1, 0:1]
        self._add_chips(tag, sums, _reduce_chips(sums, group["chunks"], "reduce_chips_" + tag))
        return None

    def _add_chips(self, tag, sums, from_chips):
        self.mine[tag] = [_add4(lax.dynamic_index_in_dim(s, self.chip, axis=0, keepdims=False), f,
                                "add_chips_%s_%d" % (tag, i)) for i, (s, f) in enumerate(zip(sums, from_chips))]

    def finish(self, tags, after):
        for tag in tags:
            if tag in self.going:
                self._add_chips(tag, *_ici_wait(self.going.pop(tag), after, "reduce_wait_" + tag))
        mine = [a for t in tags for a in self.mine[t]]
        nchs = [k for t in tags for k in _GROUPS[t]["chunks"]]
        theirs = _swap_with_sibling(mine, nchs, 0, "share_pair_" + tags[0])
        whole = [jnp.concatenate([jnp.where(self.south, a, b), jnp.where(self.south, b, a)], axis=0)
                 for a, b in zip(mine, theirs)]
        reduced = {}
        for i, t in enumerate(tags):
            reduced.update(_unpack(whole[2 * i], whole[2 * i + 1], _GROUPS[t]))
        return reduced


_WEIGHTS = ("ffn1_norm", "ffn1_w_in", "ffn1_w_out", "mix_norm", "w_in", "hgrn_lb_logits", "hgrn_out_norm",
            "gdn_conv_w", "gdn_a_log", "gdn_dt_bias", "gdn_out_norm", "w_branch_hgrn", "w_branch_gdn", "w_out",
            "ffn2_norm", "ffn2_w_in", "ffn2_w_out", "final_norm")


def kernel(x, ffn1_norm, ffn1_w_in, ffn1_w_out, mix_norm, w_in, hgrn_lb_logits, hgrn_out_norm, gdn_conv_w, gdn_a_log, gdn_dt_bias, gdn_out_norm, w_branch_hgrn, w_branch_gdn, w_out, ffn2_norm, ffn2_w_in, ffn2_w_out, final_norm, loss_target, m_ffn1_norm, m_ffn1_w_in, m_ffn1_w_out, m_mix_norm, m_w_in, m_hgrn_lb_logits, m_hgrn_out_norm, m_gdn_conv_w, m_gdn_a_log, m_gdn_dt_bias, m_gdn_out_norm, m_w_branch_hgrn, m_w_branch_gdn, m_w_out, m_ffn2_norm, m_ffn2_w_in, m_ffn2_w_out, m_final_norm, v_ffn1_norm, v_ffn1_w_in, v_ffn1_w_out, v_mix_norm, v_w_in, v_hgrn_lb_logits, v_hgrn_out_norm, v_gdn_conv_w, v_gdn_a_log, v_gdn_dt_bias, v_gdn_out_norm, v_w_branch_hgrn, v_w_branch_gdn, v_w_out, v_ffn2_norm, v_ffn2_w_in, v_ffn2_w_out, v_final_norm):
    args = dict(locals())
    wts = {n: args[n] for n in _WEIGHTS}
    moms = {n: args["m_" + n] for n in _WEIGHTS}
    vars_ = {n: args["v_" + n] for n in _WEIGHTS}

    small = {n: wts[n].astype(F32) for n in _WEIGHTS if n not in _BIG_NAMES}
    exchange = _Exchange(wts)
    dx, small_grads = _local_step(x[0], loss_target[0], small, exchange)

    out_g, out_d, out_m, out_v = {}, {}, {}, {}

    def update(tags, reduced, after):
        for t in tags:
            for n in _group_names(_GROUPS[t]):
                shape = wts[n].shape
                w2 = wts[n].reshape(shape[-2], shape[-1])
                g2 = reduced[n]
                d, m2, v2 = _adamw(w2, g2, moms[n].reshape(w2.shape), vars_[n].reshape(w2.shape), "adamw_" + n, after)
                out_g[n], out_d[n], out_m[n], out_v[n] = (g2.reshape(shape), d.reshape(shape), m2.reshape(shape),
                                                          v2.reshape(shape))
                after = v2
        return after

    done = update(("ffn2", "mixer"), exchange.finish(("ffn2", "mixer"), after=dx), exchange.token)
    update(("ffn1",), exchange.finish(("ffn1",), after=done), None)

    small_names = [n for n, _ in _SMALL]
    zero = jnp.zeros((1,), F32)
    shapes = {n: (wts[n].shape if n != "loss" else (1,)) for n in small_names}
    sums, sd, sm_, sv = _small_sync(
        _pack_small(small_grads),
        _pack_small({n: (wts[n] if n != "loss" else zero) for n in small_names}),
        _pack_small({n: (moms[n] if n != "loss" else zero) for n in small_names}),
        _pack_small({n: (vars_[n] if n != "loss" else zero) for n in small_names}))
    sg_u, sd_u, sm_u, sv_u = (_unpack_small(p, shapes) for p in (sums, sd, sm_, sv))
    for n in small_names:
        if n != "loss":
            out_g[n], out_d[n], out_m[n], out_v[n] = sg_u[n], sd_u[n], sm_u[n], sv_u[n]
    loss = sg_u["loss"].reshape(())

    return (loss, dx[None], *[out_g[n] for n in _WEIGHTS], *[out_d[n] for n in _WEIGHTS],
            *[out_m[n] for n in _WEIGHTS], *[out_v[n] for n in _WEIGHTS])
```

```python
import numpy as np

import jax
import jax.numpy as jnp
from jax import lax
from jax.experimental import pallas as pl
from jax.experimental.pallas import tpu as pltpu

F32 = jnp.float32
BF16 = jnp.bfloat16

D_MODEL = 1024
D_FF = 2816
CHUNK = 64
HEAD = 128
HG_HEADS = 8
GD_HEADS = 16
HPS = 8
MM_TM = 1408
MM_TN = 1024
MM_TK = 2048
VMEM_LIMIT = 48 * 1024 * 1024
ROW_TILE = 512
EPS = 1e-6
CONV_K = 4
IN_NAMES = ("hq", "hf", "hi", "hg", "gq", "gk", "gv", "ga", "gb", "gz", "gate_h", "gate_g")
IN_SIZES = (1024, 1024, 1024, 1024, 1024, 1024, 2048, 16, 16, 2048, 1024, 1024)
IN_WIDTH = sum(IN_SIZES)

ADAM_LR = 0.001
ADAM_B1 = 0.9
ADAM_B2 = 0.999
ADAM_EPS = 1e-08
ADAM_WD = 0.01
ADAM_STEP = 10

MESH = pl.DeviceIdType.MESH
_ARB = "arbitrary"
_PAR = "parallel"


def _bf(x):
    return x.astype(BF16)


def _dot(a, b):
    return jnp.dot(_bf(a), _bf(b), preferred_element_type=F32)


def _dot_nt(a, b):
    return lax.dot_general(_bf(a), _bf(b), (((1,), (1,)), ((), ())), preferred_element_type=F32)


def _dot_tn(a, b):
    return lax.dot_general(_bf(a), _bf(b), (((0,), (0,)), ((), ())), preferred_element_type=F32)


def _sigmoid(x):
    return jax.nn.sigmoid(x)


def _silu(x):
    return x * _sigmoid(x)


def _dsilu(x):
    s = _sigmoid(x)
    return s * (1.0 + x * (1.0 - s))


def _softplus(x):
    return jnp.maximum(x, 0.0) + jnp.log(1.0 + jnp.exp(-jnp.abs(x)))


def _rowsum(x):
    return jnp.sum(x, axis=1, keepdims=True)


def _col_to_row(col, eye):
    return jnp.sum(eye * col, axis=0, keepdims=True)


def _row_to_col(row, eye):
    return jnp.sum(eye * row, axis=1, keepdims=True)


def _pick(dim, pref, unit=128):
    if dim <= pref:
        return dim
    t = pref
    while t >= unit:
        if dim % t == 0:
            return t
        t -= unit
    return dim


def _params(*sem):
    return pltpu.CompilerParams(dimension_semantics=tuple(sem), vmem_limit_bytes=VMEM_LIMIT)


def _mm(a, b, *, ta=False, tb=False, alpha=1.0, res=None, out_dtype=F32, name="mm", b_from=0, tm_max=MM_TM):
    m = a.shape[1] if ta else a.shape[0]
    k = a.shape[0] if ta else a.shape[1]
    n = b.shape[0] if tb else b.shape[1]
    assert b_from + k <= (b.shape[1] if tb else b.shape[0])
    tm, tn, tk = _pick(m, tm_max), _pick(n, MM_TN), _pick(k, MM_TK)
    if tn < MM_TN < n and n % MM_TM == 0:
        tn = MM_TM
    nk = k // tk
    assert b_from % tk == 0
    b0 = b_from // tk
    a_spec = pl.BlockSpec((tk, tm), lambda i, j, l: (l, i)) if ta else pl.BlockSpec((tm, tk), lambda i, j, l: (i, l))
    b_spec = (pl.BlockSpec((tn, tk), lambda i, j, l: (j, b0 + l)) if tb
              else pl.BlockSpec((tk, tn), lambda i, j, l: (b0 + l, j)))
    o_spec = pl.BlockSpec((tm, tn), lambda i, j, l: (i, j))
    dims = (((0 if ta else 1,), (1 if tb else 0,)), ((), ()))
    has_res = res is not None

    def finish(r, r_ref, o_ref):
        if alpha != 1.0:
            r = r * alpha
        if has_res:
            r = r + r_ref[...]
        o_ref[...] = r.astype(out_dtype)

    def body(*refs):
        a_ref, b_ref = refs[0], refs[1]
        r_ref = refs[2] if has_res else None
        o_ref = refs[3] if has_res else refs[2]
        part = lax.dot_general(_bf(a_ref[...]), _bf(b_ref[...]), dims, preferred_element_type=F32)
        if nk == 1:
            finish(part, r_ref, o_ref)
            return
        acc = refs[-1]
        step = pl.program_id(2)

        @pl.when(step == 0)
        def _():
            acc[...] = part

        @pl.when(step != 0)
        def _():
            acc[...] += part

        @pl.when(step == nk - 1)
        def _():
            finish(acc[...], r_ref, o_ref)

    ins = [a, b] + ([res] if has_res else [])
    in_specs = [a_spec, b_spec] + ([o_spec] if has_res else [])
    return pl.pallas_call(
        body, name=name, grid=(m // tm, n // tn, nk), in_specs=in_specs, out_specs=o_spec,
        out_shape=jax.ShapeDtypeStruct((m, n), out_dtype),
        scratch_shapes=[pltpu.VMEM((tm, tn), F32)] if nk > 1 else [],
        compiler_params=_params(_PAR, _PAR, _ARB))(*ins)


def _row_spec(tr, w):
    return pl.BlockSpec((tr, w), lambda i: (i, 0))


def _full_spec(shape):
    return pl.BlockSpec(shape, lambda i: tuple(0 for _ in shape))


def _view(arr, off, width):
    return arr, off, width


def _view_rows(view, tr):
    _, off, width = view
    assert off % width == 0
    return pl.BlockSpec((tr, width), lambda i: (i, off // width))


def _view_tile(view, rows, bw, cidx=lambda c: c):
    _, off, width = view
    assert off % bw == 0 and width % bw == 0
    return pl.BlockSpec((rows, bw), lambda c, g: (cidx(c), off // bw + g))


def _rmsnorm_fwd(x, g, name):
    t, d = x.shape
    tr = _pick(t, ROW_TILE, 8)

    def body(x_ref, g_ref, o_ref):
        xv = x_ref[...]
        r = lax.rsqrt(jnp.mean(xv * xv, axis=1, keepdims=True) + EPS)
        o_ref[...] = (xv * r * g_ref[...]).astype(BF16)

    return pl.pallas_call(
        body, name=name, grid=(t // tr,), in_specs=[_row_spec(tr, d), _full_spec((1, d))],
        out_specs=_row_spec(tr, d), out_shape=jax.ShapeDtypeStruct((t, d), BF16),
        compiler_params=_params(_PAR))(x, g)


def _rmsnorm_bwd(x, g, dn, res, name):
    t, d = x.shape
    tr = _pick(t, ROW_TILE, 8)

    def body(x_ref, g_ref, dn_ref, r_ref, dx_ref, dxb_ref, dg_ref):
        @pl.when(pl.program_id(0) == 0)
        def _():
            dg_ref[...] = jnp.zeros_like(dg_ref)

        xv = x_ref[...]
        r = lax.rsqrt(jnp.mean(xv * xv, axis=1, keepdims=True) + EPS)
        xh = xv * r
        dy = dn_ref[...]
        dg_ref[...] += jnp.sum(dy * xh, axis=0, keepdims=True)
        dxh = dy * g_ref[...]
        dx = r_ref[...] + r * (dxh - xh * jnp.mean(dxh * xh, axis=1, keepdims=True))
        dx_ref[...] = dx
        dxb_ref[...] = dx.astype(BF16)

    return pl.pallas_call(
        body, name=name, grid=(t // tr,),
        in_specs=[_row_spec(tr, d), _full_spec((1, d)), _row_spec(tr, d), _row_spec(tr, d)],
        out_specs=[_row_spec(tr, d), _row_spec(tr, d), _full_spec((1, d))],
        out_shape=[jax.ShapeDtypeStruct((t, d), F32), jax.ShapeDtypeStruct((t, d), BF16),
                   jax.ShapeDtypeStruct((1, d), F32)],
        compiler_params=_params(_ARB))(x, g, dn, res)


FFN_TN = 1408
FFN_TM = 512


def _ffn_in_act(n, w_in, name):
    t, d = n.shape
    tm = _pick(t, FFN_TM)
    nf = D_FF // FFN_TN

    def body(n_ref, wa_ref, wb_ref, a_ref, b_ref, hm_ref):
        nv = n_ref[...]
        a = jnp.dot(nv, wa_ref[...], preferred_element_type=F32)
        b = jnp.dot(nv, wb_ref[...], preferred_element_type=F32)
        a_ref[...] = a.astype(BF16)
        b_ref[...] = b.astype(BF16)
        hm_ref[...] = (_silu(a) * b).astype(BF16)

    tile = pl.BlockSpec((tm, FFN_TN), lambda i, j: (i, j))
    return pl.pallas_call(
        body, name=name, grid=(t // tm, nf),
        in_specs=[pl.BlockSpec((tm, d), lambda i, j: (i, 0)), pl.BlockSpec((d, FFN_TN), lambda i, j: (0, j)),
                  pl.BlockSpec((d, FFN_TN), lambda i, j: (0, nf + j))],
        out_specs=[tile, tile, tile], out_shape=[jax.ShapeDtypeStruct((t, D_FF), BF16)] * 3,
        compiler_params=_params(_PAR, _PAR))(n, w_in, w_in)


def _ffn_dact(dout, w_out, a, b, name):
    t, d = dout.shape
    tm = _pick(t, FFN_TM)

    def body(do_ref, w_ref, a_ref, b_ref, da_ref, db_ref, hm_ref):
        dh = 0.5 * _dot_nt(do_ref[...], w_ref[...])
        av = a_ref[...].astype(F32)
        bv = b_ref[...].astype(F32)
        sg = _sigmoid(av)
        sa = av * sg
        da_ref[...] = (dh * bv * (sg * (1.0 + av * (1.0 - sg)))).astype(BF16)
        db_ref[...] = (dh * sa).astype(BF16)
        hm_ref[...] = (sa * bv).astype(BF16)

    tile = pl.BlockSpec((tm, FFN_TN), lambda i, j: (i, j))
    return pl.pallas_call(
        body, name=name, grid=(t // tm, D_FF // FFN_TN),
        in_specs=[pl.BlockSpec((tm, d), lambda i, j: (i, 0)), pl.BlockSpec((FFN_TN, d), lambda i, j: (j, 0)), tile, tile],
        out_specs=[tile, tile, tile], out_shape=[jax.ShapeDtypeStruct((t, D_FF), BF16)] * 3,
        compiler_params=_params(_PAR, _PAR))(dout, w_out, a, b)


def _merge_fwd(yh, yg, gh, gg):
    t, d = yh.shape
    tr = _pick(t, ROW_TILE, 8)

    def body(yh_ref, yg_ref, gh_ref, gg_ref, o_ref):
        o_ref[...] = (_sigmoid(gh_ref[...]) * yh_ref[...] + _sigmoid(gg_ref[...]) * yg_ref[...]).astype(BF16)

    return pl.pallas_call(
        body, name="merge_fwd", grid=(t // tr,),
        in_specs=[_row_spec(tr, d), _row_spec(tr, d), _view_rows(gh, tr), _view_rows(gg, tr)],
        out_specs=_row_spec(tr, d),
        out_shape=jax.ShapeDtypeStruct((t, d), BF16), compiler_params=_params(_PAR))(yh, yg, gh[0], gg[0])


def _into(dproj, off, width):
    return dproj, off, width


def _merge_bwd(dy, yh, yg, gh, gg, into):
    t, d = yh.shape
    tr = _pick(t, ROW_TILE, 8)
    dproj, off, width = into
    assert width == 2 * d and off % width == 0

    def body(dy_ref, yh_ref, yg_ref, gh_ref, gg_ref, _, dyh_ref, dyg_ref, dg_ref):
        dyv = dy_ref[...]
        sh = _sigmoid(gh_ref[...])
        sg = _sigmoid(gg_ref[...])
        dyh_ref[...] = (dyv * sh).astype(BF16)
        dyg_ref[...] = (dyv * sg).astype(BF16)
        dg_ref[:, :d] = (dyv * yh_ref[...] * sh * (1.0 - sh)).astype(BF16)
        dg_ref[:, d:] = (dyv * yg_ref[...] * sg * (1.0 - sg)).astype(BF16)

    return pl.pallas_call(
        body, name="merge_bwd", grid=(t // tr,),
        in_specs=[_row_spec(tr, d)] * 3 + [_view_rows(gh, tr), _view_rows(gg, tr), _ANY],
        out_specs=[_row_spec(tr, d)] * 2 + [pl.BlockSpec((tr, width), lambda i: (i, off // width))],
        out_shape=[jax.ShapeDtypeStruct((t, d), BF16)] * 2 + [jax.ShapeDtypeStruct(dproj.shape, dproj.dtype)],
        input_output_aliases={5: 2},
        compiler_params=_params(_PAR))(dy, yh, yg, gh[0], gg[0], dproj)


def _final_loss(h, g, tgt):
    t, d = h.shape
    tr = _pick(t, ROW_TILE, 8)

    def body(h_ref, g_ref, t_ref, loss_ref, dh_ref, dhb_ref, dg_ref):
        @pl.when(pl.program_id(0) == 0)
        def _():
            dg_ref[...] = jnp.zeros_like(dg_ref)
            loss_ref[...] = jnp.zeros_like(loss_ref)

        xv = h_ref[...]
        gv = g_ref[...]
        r = lax.rsqrt(jnp.mean(xv * xv, axis=1, keepdims=True) + EPS)
        xh = xv * r
        err = xh * gv - t_ref[...]
        loss_ref[...] += 0.5 * jnp.sum(jnp.mean(err * err, axis=1, keepdims=True), axis=0, keepdims=True)
        dy = err * (1.0 / d)
        dg_ref[...] += jnp.sum(dy * xh, axis=0, keepdims=True)
        dxh = dy * gv
        dh = r * (dxh - xh * jnp.mean(dxh * xh, axis=1, keepdims=True))
        dh_ref[...] = dh
        dhb_ref[...] = dh.astype(BF16)

    return pl.pallas_call(
        body, name="final_loss", grid=(t // tr,),
        in_specs=[_row_spec(tr, d), _full_spec((1, d)), _row_spec(tr, d)],
        out_specs=[_full_spec((1, 128)), _row_spec(tr, d), _row_spec(tr, d), _full_spec((1, d))],
        out_shape=[jax.ShapeDtypeStruct((1, 128), F32), jax.ShapeDtypeStruct((t, d), F32),
                   jax.ShapeDtypeStruct((t, d), BF16), jax.ShapeDtypeStruct((1, d), F32)],
        compiler_params=_params(_ARB))(h, g, tgt)


def _hg_consts():
    c = CHUNK
    t = np.arange(c)
    mats, masks = [], []
    for lvl in range(6):
        m = 1 << lvl
        blk = t // m
        mat = np.zeros((c, c), np.float32)
        for tt in range(c):
            b = blk[tt]
            if b % 2 == 1:
                mat[tt, b * m:tt + 1] = 1.0
            else:
                mat[tt, tt + 1:(b + 1) * m] = 1.0
        mats.append(mat)
        same = (t[:, None] // (2 * m)) == (t[None, :] // (2 * m))
        masks.append((same & (blk[:, None] % 2 == 1) & (blk[None, :] % 2 == 0)).astype(np.float32))
    pre = np.tril(np.ones((c, c), np.float32))
    suf = np.triu(np.ones((c, c), np.float32), 1)
    mstack = np.concatenate(mats + [pre, suf], 0)
    masks.append(np.eye(c, dtype=np.float32))
    return (jnp.asarray(mstack, BF16), jnp.asarray(mstack.T.copy(), BF16), jnp.asarray(np.stack(masks), F32),
            jnp.asarray(np.eye(HEAD, dtype=np.float32)))


def _gd_consts():
    c = CHUNK
    incl = np.tril(np.ones((c, c), np.float32))
    strict = np.tril(np.ones((c, c), np.float32), -1)
    eye = np.eye(c, dtype=np.float32)
    masks = np.stack([incl, strict, eye, incl.T.copy()])
    return jnp.asarray(incl, BF16), jnp.asarray(incl.T.copy(), BF16), jnp.asarray(masks, F32)


def _chunks_per_step(nc):
    for cb in (32 // HPS, 2, 1):
        if nc % cb == 0:
            return cb
    return 1


def _hg_prep(hq, hf, lg):
    lb = _sigmoid(lg[0:1, :] - lg[1:2, :])
    sg = _sigmoid(hf)
    sgn = _sigmoid(-hf)
    f = lb + (1.0 - lb) * sg
    lf = jnp.log(f)
    kk = (1.0 - lb) * sgn
    q = _silu(hq) * (HEAD ** -0.5)
    return lb, sg, sgn, f, lf, kk, q


def _mx_each(m, xs):
    hi, lo = _split2_each(xs)
    prods = [jnp.dot(m, jnp.concatenate([h, l], axis=1), preferred_element_type=F32) for h, l in zip(hi, lo)]
    return [p[:, :HEAD] + p[:, HEAD:] for p in prods]


def _hg_scaled(x, ex):
    xb = [_bf(a) for a in x]
    eb = [_bf(e[:6 * CHUNK]) for e in ex]
    return [[a * e[lvl * CHUNK:(lvl + 1) * CHUNK] for lvl in range(6)] for a, e in zip(xb, eb)]


def _hg_scores(q, kk, qe, ke, mask_ref):
    p = [mask_ref[6] * _rowsum(a * b) for a, b in zip(q, kk)]
    for lvl in range(6):
        d = [_dot_nt(a[lvl], b[lvl]) for a, b in zip(qe, ke)]
        p = [x + mask_ref[lvl] * y for x, y in zip(p, d)]
    return p


def _hgrn_fwd(hq, hf, hi, hg, logits, gain, consts):
    t = hq[0].shape[0]
    nc = t // CHUNK
    cb = _chunks_per_step(nc)
    rows = cb * CHUNK
    mstack, _, masks, eye = consts
    tile = pl.BlockSpec((rows, HPS * HEAD), lambda c, g: (c, g))

    def body(hq_ref, hf_ref, hi_ref, hg_ref, lg_ref, gain_ref, m_ref, mask_ref, eye_ref,
             oraw_ref, og_ref, ssave_ref, state):
        c = pl.program_id(0)
        g = pl.program_id(1)

        @pl.when(c == 0)
        def _():
            for hh in range(HPS):
                state[g * HPS + hh] = jnp.zeros((HEAD, HEAD), F32)

        lg_all = lg_ref[...]
        gain_v = gain_ref[...]

        def one(i, carry):
            sl = pl.ds(pl.multiple_of(i * CHUNK, CHUNK), CHUNK)
            hs = range(HPS)
            heads = [g * HPS + hh for hh in hs]
            ln = [slice(hh * HEAD, (hh + 1) * HEAD) for hh in hs]
            preps = [_hg_prep(hq_ref[sl, s], hf_ref[sl, s], lg_all[:, s]) for s in ln]
            lf, kk, q = [p[4] for p in preps], [p[5] for p in preps], [p[6] for p in preps]
            v = [hi_ref[sl, s] for s in ln]
            ex = [jnp.exp(x) for x in _mx_each(m_ref[...], lf)]
            eb = [e[6 * CHUNK:7 * CHUNK] for e in ex]
            esfx = [e[7 * CHUNK:8 * CHUNK] for e in ex]
            qe, ke = _hg_scaled(q, ex), _hg_scaled(kk, ex)
            p = _hg_scores(q, kk, qe, ke, mask_ref)
            s0 = [state[h] for h in heads]
            o = _each(lambda a, e, s, pp, vv: _dot(a * e, s) + _dot(pp, vv), q, eb, s0, p, v)
            eye_v = eye_ref[...]
            s1 = _each(lambda s, e, kx, ef, vv: s * _row_to_col(e[CHUNK - 1:CHUNK, :], eye_v) + _dot_tn(kx * ef, vv),
                       s0, eb, kk, esfx, v)
            for hh in hs:
                ssave_ref[i, hh] = s0[hh]
                state[heads[hh]] = s1[hh]
                oraw_ref[sl, ln[hh]] = o[hh]
                r = lax.rsqrt(jnp.mean(o[hh] * o[hh], axis=1, keepdims=True) + EPS)
                og_ref[sl, ln[hh]] = (o[hh] * r * gain_v * _silu(hg_ref[sl, ln[hh]])).astype(BF16)
            return carry

        lax.fori_loop(0, cb, one, 0, unroll=4)

    return pl.pallas_call(
        body, name="hgrn_fwd", grid=(nc // cb, HG_HEADS // HPS),
        in_specs=[_view_tile(v, rows, HPS * HEAD) for v in (hq, hf, hi, hg)] + [
                  pl.BlockSpec((2, HPS * HEAD), lambda c, g: (0, g)),
                  pl.BlockSpec((1, HEAD), lambda c, g: (0, 0)),
                  pl.BlockSpec(mstack.shape, lambda c, g: (0, 0)),
                  pl.BlockSpec(masks.shape, lambda c, g: (0, 0, 0)),
                  pl.BlockSpec(eye.shape, lambda c, g: (0, 0))],
        out_specs=[tile, tile, pl.BlockSpec((cb, HPS, HEAD, HEAD), lambda c, g: (c, g, 0, 0))],
        out_shape=[jax.ShapeDtypeStruct((t, HG_HEADS * HEAD), F32), jax.ShapeDtypeStruct((t, HG_HEADS * HEAD), BF16),
                   jax.ShapeDtypeStruct((nc, HG_HEADS, HEAD, HEAD), F32)],
        scratch_shapes=[pltpu.VMEM((HG_HEADS, HEAD, HEAD), F32)],
        compiler_params=_params(_ARB, _ARB))(hq[0], hf[0], hi[0], hg[0], logits, gain, mstack, masks, eye)


def _hgrn_bwd(hq, hf, hi, hg, logits, gain, oraw, ssave, dog, consts, into):
    t = hq[0].shape[0]
    dproj, off, width = into
    seg = HG_HEADS * HEAD
    assert HPS == HG_HEADS and width == 4 * seg and off % width == 0
    nc = t // CHUNK
    cb = _chunks_per_step(nc)
    rows = cb * CHUNK
    nb = nc // cb
    mstack, mstack_t, masks, eye = consts
    tile = pl.BlockSpec((rows, HPS * HEAD), lambda c, g: (nb - 1 - c, g))

    def body(hq_ref, hf_ref, hi_ref, hg_ref, lg_ref, gain_ref, oraw_ref, ssave_ref, dog_ref, m_ref, mt_ref,
             mask_ref, eye_ref, _, d_ref, dgain_ref, dlb_ref, dstate):
        c = pl.program_id(0)
        g = pl.program_id(1)

        @pl.when(c == 0)
        def _():
            for hh in range(HPS):
                dstate[g * HPS + hh] = jnp.zeros((HEAD, HEAD), F32)

        @pl.when((c == 0) & (g == 0))
        def _():
            dgain_ref[...] = jnp.zeros_like(dgain_ref)
            dlb_ref[...] = jnp.zeros_like(dlb_ref)

        lg_all = lg_ref[...]
        gain_v = gain_ref[...]
        eye_v = eye_ref[...]
        last_row = (lax.broadcasted_iota(jnp.int32, (CHUNK, HEAD), 0) == CHUNK - 1).astype(F32)

        def one(j, carry):
            i = cb - 1 - j
            sl = pl.ds(pl.multiple_of(i * CHUNK, CHUNK), CHUNK)
            hs = range(HPS)
            heads = [g * HPS + hh for hh in hs]
            ln = [slice(hh * HEAD, (hh + 1) * HEAD) for hh in hs]
            hqv = [hq_ref[sl, s] for s in ln]
            hgv = [hg_ref[sl, s] for s in ln]
            preps = [_hg_prep(a, hf_ref[sl, s], lg_all[:, s]) for a, s in zip(hqv, ln)]
            lb, sg, sgn, f, lf, kk, q = ([p[n] for p in preps] for n in range(7))
            v = [hi_ref[sl, s] for s in ln]
            ex = [jnp.exp(x) for x in _mx_each(m_ref[...], lf)]
            eb = [e[6 * CHUNK:7 * CHUNK] for e in ex]
            esfx = [e[7 * CHUNK:8 * CHUNK] for e in ex]
            qe, ke = _hg_scaled(q, ex), _hg_scaled(kk, ex)
            p = _hg_scores(q, kk, qe, ke, mask_ref)
            s0 = [ssave_ref[i, hh] for hh in hs]
            ds = [dstate[h] for h in heads]

            o = [oraw_ref[sl, s] for s in ln]
            r = [lax.rsqrt(jnp.mean(x * x, axis=1, keepdims=True) + EPS) for x in o]
            on = _each(lambda x, y: x * y, o, r)
            dg_out = [dog_ref[sl, s] for s in ln]
            sgate = [_silu(x) for x in hgv]
            for hh in hs:
                d_ref[sl, slice(3 * seg + hh * HEAD, 3 * seg + (hh + 1) * HEAD)] =(dg_out[hh] * on[hh] * gain_v * _dsilu(hgv[hh])).astype(BF16)
            dgain_ref[...] += sum(jnp.sum(d * s * n, axis=0, keepdims=True) for d, s, n in zip(dg_out, sgate, on))
            don = _each(lambda d, s: d * s * gain_v, dg_out, sgate)
            do = _each(lambda rr, dn, n: rr * (dn - n * jnp.mean(dn * n, axis=1, keepdims=True)), r, don, on)

            dp = _each(_dot_nt, do, v)
            dv = _each(lambda pp, d, kx, ef, s: _dot_tn(pp, d) + _dot(kx * ef, s), p, do, kk, esfx, ds)
            dqb = _each(_dot_nt, do, s0)
            dkx = _each(_dot_nt, v, ds)
            diag = [_rowsum(mask_ref[6] * x) for x in dp]
            dq = _each(lambda a, e, d, kx: a * e + d * kx, dqb, eb, diag, kk)
            dk = _each(lambda a, e, d, qq: a * e + d * qq, dkx, esfx, diag, q)
            dxs = [[] for _ in hs]
            for lvl in range(6):
                el = [e[lvl * CHUNK:(lvl + 1) * CHUNK] for e in ex]
                gm = [mask_ref[lvl] * x for x in dp]
                gm = [_bf(x) for x in gm]
                a1 = _each(lambda m_, kx: _dot(m_, kx[lvl]), gm, ke)
                a2 = _each(lambda m_, qq: _dot_tn(m_, qq[lvl]), gm, qe)
                dq = _each(lambda x, a, e: x + a * e, dq, a1, el)
                dk = _each(lambda x, a, e: x + a * e, dk, a2, el)
                for hh in hs:
                    dxs[hh].append((a1[hh] * q[hh] + a2[hh] * kk[hh]) * el[hh])
            e_end_row = [e[CHUNK - 1:CHUNK, :] for e in eb]
            ds_new = _each(lambda qq, e, d, er, s: _dot_tn(qq * e, d) + _row_to_col(er, eye_v) * s, q, eb, do, e_end_row, ds)
            for hh in hs:
                dstate[heads[hh]] = ds_new[hh]
                dend_row = _col_to_row(_rowsum(s0[hh] * ds[hh]), eye_v)
                dxs[hh].append(dqb[hh] * q[hh] * eb[hh] + last_row * (e_end_row[hh] * dend_row))
                dxs[hh].append(dkx[hh] * kk[hh] * esfx[hh])
            dlf = _mx_each(mt_ref[...], [jnp.concatenate(x, axis=0) for x in dxs])

            for hh in hs:
                d_ref[sl, slice(2 * seg + hh * HEAD, 2 * seg + (hh + 1) * HEAD)] =dv[hh].astype(BF16)
                d_ref[sl, ln[hh]] =(dq[hh] * (HEAD ** -0.5) * _dsilu(hqv[hh])).astype(BF16)
                df = dlf[hh] / f[hh]
                dsig = (1.0 - lb[hh]) * sg[hh] * sgn[hh]
                d_ref[sl, slice(seg + hh * HEAD, seg + (hh + 1) * HEAD)] =((df - dk[hh]) * dsig).astype(BF16)
                dlb_t = jnp.sum(df * sgn[hh] - dk[hh] * sgn[hh], axis=0, keepdims=True)
                dlb_ref[pl.ds(heads[hh], 1), :] += dlb_t * lb[hh] * (1.0 - lb[hh])
            return carry

        lax.fori_loop(0, cb, one, 0, unroll=2)

    outs = [jax.ShapeDtypeStruct(dproj.shape, dproj.dtype),
            jax.ShapeDtypeStruct((1, HEAD), F32), jax.ShapeDtypeStruct((HG_HEADS, HEAD), F32)]
    return pl.pallas_call(
        body, name="hgrn_bwd", grid=(nb, HG_HEADS // HPS),
        in_specs=[_view_tile(v, rows, HPS * HEAD, lambda c: nb - 1 - c) for v in (hq, hf, hi, hg)] + [
                  pl.BlockSpec((2, HPS * HEAD), lambda c, g: (0, g)),
                  pl.BlockSpec((1, HEAD), lambda c, g: (0, 0)), tile,
                  pl.BlockSpec((cb, HPS, HEAD, HEAD), lambda c, g: (nb - 1 - c, g, 0, 0)), tile,
                  pl.BlockSpec(mstack.shape, lambda c, h: (0, 0)),
                  pl.BlockSpec(mstack_t.shape, lambda c, h: (0, 0)),
                  pl.BlockSpec(masks.shape, lambda c, h: (0, 0, 0)),
                  pl.BlockSpec(eye.shape, lambda c, h: (0, 0)), _ANY],
        out_specs=[pl.BlockSpec((rows, width), lambda c, h: (nb - 1 - c, off // width)),
                   pl.BlockSpec((1, HEAD), lambda c, h: (0, 0)),
                   pl.BlockSpec((HG_HEADS, HEAD), lambda c, h: (0, 0))],
        out_shape=outs, scratch_shapes=[pltpu.VMEM((HG_HEADS, HEAD, HEAD), F32)], input_output_aliases={13: 0},
        compiler_params=_params(_ARB, _ARB))(hq[0], hf[0], hi[0], hg[0], logits, gain, oraw, ssave, dog, mstack,
                                             mstack_t, masks, eye, dproj)


CONV_W = 512
CONV_ROWS = 1024


def _per_head(fn, *arrs):
    width = arrs[0].shape[1]
    return jnp.concatenate([fn(*[a[:, j:j + HEAD] for a in arrs]) for j in range(0, width, HEAD)], axis=1)


def _shift_down(xv, halo, d, top_rows):
    if d == 0:
        return xv, xv[0:8]
    main = pltpu.roll(xv, d, 0)
    top = jnp.where(top_rows < d, pltpu.roll(halo, d, 0), main[0:8])
    return main, top


def _conv_parts(x_ref, halo_ref, w_ref, first):
    xv = x_ref[...]
    halo = jnp.where(first, 0.0, halo_ref[...])
    top_rows = lax.broadcasted_iota(jnp.int32, (8, xv.shape[1]), 0)
    shifted = [_shift_down(xv, halo, CONV_K - 1 - j, top_rows) for j in range(CONV_K)]
    w = w_ref[...]
    acc = sum(shifted[j][0] * w[j:j + 1, :] for j in range(CONV_K))
    acc_top = sum(shifted[j][1] * w[j:j + 1, :] for j in range(CONV_K))
    return shifted, acc, acc_top


def _conv_fwd(x, w8, l2scale, name):
    x, off, width = x
    t = x.shape[0]
    o = off // CONV_W
    tr = _pick(t, CONV_ROWS, 8)

    def post(cv):
        s = _silu(cv)
        if l2scale is not None:
            s = _per_head(lambda sh: sh * (lax.rsqrt(_rowsum(sh * sh) + EPS) * l2scale), s)
        return s

    def body(x_ref, halo_ref, w_ref, o_ref):
        _, acc, acc_top = _conv_parts(x_ref, halo_ref, w_ref, pl.program_id(1) == 0)
        o_ref[...] = post(acc)
        o_ref[0:8, :] = post(acc_top)

    return pl.pallas_call(
        body, name=name, grid=(width // CONV_W,t // tr),
        in_specs=[pl.BlockSpec((tr, CONV_W), lambda j, i: (i, o + j)),
                  pl.BlockSpec((8, CONV_W), lambda j, i: (jnp.maximum(i * (tr // 8) - 1, 0), o + j)),
                  pl.BlockSpec((8, CONV_W), lambda j, i: (0, j))],
        out_specs=pl.BlockSpec((tr, CONV_W), lambda j, i: (i, j)),
        out_shape=jax.ShapeDtypeStruct((t, width), F32), compiler_params=_params(_PAR, _PAR))(x, x, w8)


def _conv_bwd_a(x, w8, dy, l2scale, name):
    x, off, width = x
    t = x.shape[0]
    o = off // CONV_W
    tr = _pick(t, CONV_ROWS, 8)

    def l2_bwd(s, dyh):
        r = lax.rsqrt(_rowsum(s * s) + EPS)
        y0 = s * r
        dy0 = dyh * l2scale
        return r * (dy0 - y0 * _rowsum(dy0 * y0))

    def to_dc(cv, dyv):
        if l2scale is not None:
            dyv = _per_head(l2_bwd, _silu(cv), dyv)
        return dyv * _dsilu(cv)

    def body(x_ref, halo_ref, w_ref, dy_ref, dc_ref, dw_ref):
        @pl.when(pl.program_id(1) == 0)
        def _():
            dw_ref[...] = jnp.zeros_like(dw_ref)

        shifted, acc, acc_top = _conv_parts(x_ref, halo_ref, w_ref, pl.program_id(1) == 0)
        dyv = dy_ref[...]
        dc = to_dc(acc, dyv)
        dc_top = to_dc(acc_top, dyv[0:8])
        dc_ref[...] = dc
        dc_ref[0:8, :] = dc_top
        rest = (lax.broadcasted_iota(jnp.int32, dc.shape, 0) >= 8).astype(F32)
        dc_rest = dc * rest
        for j in range(CONV_K):
            dw_ref[j:j + 1, :] += (jnp.sum(dc_rest * shifted[j][0], axis=0, keepdims=True)
                                   + jnp.sum(dc_top * shifted[j][1], axis=0, keepdims=True))

    return pl.pallas_call(
        body, name=name, grid=(width // CONV_W,t // tr),
        in_specs=[pl.BlockSpec((tr, CONV_W), lambda j, i: (i, o + j)),
                  pl.BlockSpec((8, CONV_W), lambda j, i: (jnp.maximum(i * (tr // 8) - 1, 0), o + j)),
                  pl.BlockSpec((8, CONV_W), lambda j, i: (0, j)),
                  pl.BlockSpec((tr, CONV_W), lambda j, i: (i, j))],
        out_specs=[pl.BlockSpec((tr, CONV_W), lambda j, i: (i, j)), pl.BlockSpec((8, CONV_W), lambda j, i: (0, j))],
        out_shape=[jax.ShapeDtypeStruct((t, width), F32), jax.ShapeDtypeStruct((8, width), F32)],
        compiler_params=_params(_PAR, _ARB))(x, x, w8, dy)


def _conv_bwd_b(dc, w8, name, into):
    t, width = dc.shape
    tr = _pick(t, CONV_ROWS, 8)
    nt = t // tr

    dproj, off, into_width = into
    assert into_width == width and off % CONV_W == 0
    o = off // CONV_W

    def body(dc_ref, halo_ref, w_ref, _, dx_ref):
        dcv = dc_ref[...]
        halo = jnp.where(pl.program_id(1) == nt - 1, 0.0, halo_ref[...])
        w = w_ref[...]
        bot_rows = lax.broadcasted_iota(jnp.int32, (8, CONV_W), 0)
        acc = dcv * w[CONV_K - 1:CONV_K, :]
        acc_bot = dcv[tr - 8:tr] * w[CONV_K - 1:CONV_K, :]
        for d in range(1, CONV_K):
            main = pltpu.roll(dcv, tr - d, 0)
            bot = jnp.where(bot_rows >= 8 - d, pltpu.roll(halo, 8 - d, 0), main[tr - 8:tr])
            wj = w[CONV_K - 1 - d:CONV_K - d, :]
            acc = acc + main * wj
            acc_bot = acc_bot + bot * wj
        dx_ref[...] = acc.astype(BF16)
        dx_ref[tr - 16:tr, :] = jnp.concatenate([acc[tr - 16:tr - 8], acc_bot], axis=0).astype(BF16)

    return pl.pallas_call(
        body, name=name, grid=(width // CONV_W,nt),
        in_specs=[pl.BlockSpec((tr, CONV_W), lambda j, i: (i, j)),
                  pl.BlockSpec((8, CONV_W), lambda j, i: (jnp.minimum((i + 1) * (tr // 8), t // 8 - 1), j)),
                  pl.BlockSpec((8, CONV_W), lambda j, i: (0, j)), _ANY],
        out_specs=pl.BlockSpec((tr, CONV_W), lambda j, i: (i, o + j)),
        out_shape=jax.ShapeDtypeStruct(dproj.shape, dproj.dtype), input_output_aliases={3: 0},
        compiler_params=_params(_PAR, _PAR))(dc, dc, w8, dproj)


def _each(f, *lists):
    return [f(*xs) for xs in zip(*lists)]


def _split2_each(xs):
    hi = [_bf(x) for x in xs]
    lo = [_bf(x - h.astype(F32)) for x, h in zip(xs, hi)]
    return hi, lo


def _hp_each(a_split, b_split):
    (ah, al), (bh, bl) = a_split, b_split
    rows = ah[0].shape[0]
    d12 = [jnp.dot(jnp.concatenate([x, y], axis=0), z, preferred_element_type=F32) for x, y, z in zip(ah, al, bh)]
    d3 = [jnp.dot(x, y, preferred_element_type=F32) for x, y in zip(ah, bl)]
    return [d[:rows] + d[rows:] + e for d, e in zip(d12, d3)]


INV_EXACT_STEPS = 1


def _tri_inv_each(a_list, eye):
    ns = [-a for a in a_list]
    ps = [eye + n for n in ns]
    n_split = _split2_each(ns)
    for step in range(5):
        if step < INV_EXACT_STEPS:
            ns = _hp_each(n_split, n_split)
            n_split = _split2_each(ns)
            ps = [p + d for p, d in zip(ps, _hp_each(_split2_each(ps), n_split))]
        else:
            nb = n_split[0] if step == INV_EXACT_STEPS else [_bf(n) for n in ns]
            ns = [jnp.dot(x, x, preferred_element_type=F32) for x in nb]
            nb2 = [_bf(n) for n in ns]
            ps = [p + jnp.dot(_bf(p), y, preferred_element_type=F32) for p, y in zip(ps, nb2)]
    return ps


def _gd_gates(gab, alog, dtb):
    sp_arg = gab + dtb
    return sp_arg, -jnp.exp(alog) * _softplus(sp_arg), _sigmoid(gab)


def _pick_lane(tile, base, head):
    g, hh = head
    col = tile[:, base + hh:base + hh + 1]
    for gi in range(1, GD_HEADS // HPS):
        lane = base + gi * HPS + hh
        col = jnp.where(g == gi, tile[:, lane:lane + 1], col)
    return col


def _gd_chunks(q, k, v, g_all, beta_all, heads, l_ref, mask_ref, tm=None):
    incl, strict, eye, upper = mask_ref[0], mask_ref[1], mask_ref[2], mask_ref[3]
    lmat = l_ref[...]
    gb = [jnp.broadcast_to(_pick_lane(g_all, 0, s), (CHUNK, HEAD)) for s in heads]
    bb = [jnp.broadcast_to(_pick_lane(beta_all, GD_HEADS, s), (CHUNK, HEAD)) for s in heads]
    gam = _mx_each(lmat, gb)
    gam_row = [jnp.sum(x[:, :CHUNK] * upper, axis=0, keepdims=True) for x in gb]
    lm = _each(lambda gm, gr: incl * jnp.exp(jnp.minimum(gm[:, :CHUNK] - gr, 0.0)), gam, gam_row)
    kb = _each(lambda x, b: x * b, k, bb)
    a = _each(lambda x, y, m: strict * _dot_nt(x, y) * m, kb, k, lm)
    if tm is None:
        tm = _tri_inv_each(a, eye)
    eg = [jnp.exp(x) for x in gam]
    vb = _each(lambda x, b: x * b, v, bb)
    kbg = _each(lambda x, e: x * e, kb, eg)
    uw = _each(lambda t_, x, y: _dot(t_, jnp.concatenate([x, y], axis=1)), tm, vb, kbg)
    u = [x[:, :HEAD] for x in uw]
    w = [x[:, HEAD:] for x in uw]
    qk = _each(lambda x, y, m: _dot_nt(x, y) * m, q, k, lm)
    g_end = [x[CHUNK - 1:CHUNK, :] for x in gam]
    ekg = _each(lambda e, x: jnp.exp(e - x), g_end, gam)
    ge = [jnp.exp(e) for e in g_end]
    kg = _each(lambda x, e: x * e, k, ekg)
    qg = _each(lambda x, e: x * e, q, eg)
    names = ("bb", "lm", "kb", "a", "tm", "eg", "vb", "kbg", "u", "w", "qk", "ekg", "ge", "kg", "qg")
    cols = (bb, lm, kb, a, tm, eg, vb, kbg, u, w, qk, ekg, ge, kg, qg)
    return [dict(zip(names, vals)) for vals in zip(*cols)]


def _gd_specs(rows, rev_nb=None):
    def cidx(c):
        return c if rev_nb is None else rev_nb - 1 - c

    qk_tile = pl.BlockSpec((rows, HPS // 2 * HEAD), lambda c, g: (cidx(c), g))
    v_tile = pl.BlockSpec((rows, HPS * HEAD), lambda c, g: (cidx(c), g))
    gab_tile = pl.BlockSpec((rows, HEAD), lambda c, g: (cidx(c), 0))
    return qk_tile, v_tile, gab_tile


def _gdn_fwd(qn, kn, cv, gab, gz, alog, dtb, gain, consts):
    t = qn.shape[0]
    nc = t // CHUNK
    cb = _chunks_per_step(nc)
    rows = cb * CHUNK
    lmat, _, masks = consts
    qk_tile, v_tile, gab_tile = _gd_specs(rows)
    row128 = pl.BlockSpec((1, HEAD), lambda c, h: (0, 0))

    def body(q_ref, k_ref, v_ref, gab_ref, gz_ref, alog_ref, dtb_ref, gain_ref, l_ref, mask_ref,
             oraw_ref, og_ref, ssave_ref, tsave_ref, state):
        c = pl.program_id(0)
        g = pl.program_id(1)

        @pl.when(c == 0)
        def _():
            for hh in range(HPS):
                state[g * HPS + hh] = jnp.zeros((HEAD, HEAD), F32)

        alog = alog_ref[...]
        dtb = dtb_ref[...]
        gain_v = gain_ref[...]

        def one(i, carry):
            sl = pl.ds(pl.multiple_of(i * CHUNK, CHUNK), CHUNK)
            _, g_all, beta_all = _gd_gates(gab_ref[sl, :], alog, dtb)
            heads = [g * HPS + hh for hh in range(HPS)]
            lq = [slice(hh // 2 * HEAD, (hh // 2 + 1) * HEAD) for hh in range(HPS)]
            lv = [slice(hh * HEAD, (hh + 1) * HEAD) for hh in range(HPS)]
            chs = _gd_chunks([q_ref[sl, s] for s in lq], [k_ref[sl, s] for s in lq], [v_ref[sl, s] for s in lv],
                             g_all, beta_all, [(g, hh) for hh in range(HPS)], l_ref, mask_ref)
            s0 = [state[h] for h in heads]
            ws = _each(lambda ch, s: _dot(jnp.concatenate([ch["w"], ch["qg"]], axis=0), s), chs, s0)
            v_new = _each(lambda ch, x: ch["u"] - x[:CHUNK], chs, ws)
            o = _each(lambda ch, x, vn: x[CHUNK:] + _dot(ch["qk"], vn), chs, ws, v_new)
            s1 = _each(lambda ch, s, vn: s * ch["ge"] + _dot_tn(ch["kg"], vn), chs, s0, v_new)
            for hh in range(HPS):
                ssave_ref[i, hh] = s0[hh]
                tsave_ref[i, hh] = chs[hh]["tm"]
                state[heads[hh]] = s1[hh]
                oraw_ref[sl, lv[hh]] = o[hh]
                r = lax.rsqrt(jnp.mean(o[hh] * o[hh], axis=1, keepdims=True) + EPS)
                og_ref[sl, lv[hh]] = (o[hh] * r * gain_v * _silu(gz_ref[sl, lv[hh]])).astype(BF16)
            return carry

        lax.fori_loop(0, cb, one, 0, unroll=4)

    return pl.pallas_call(
        body, name="gdn_fwd", grid=(nc // cb, GD_HEADS // HPS),
        in_specs=[qk_tile, qk_tile, v_tile, gab_tile, _view_tile(gz, rows, HPS * HEAD), row128, row128, row128,
                  pl.BlockSpec(lmat.shape, lambda c, g: (0, 0)),
                  pl.BlockSpec(masks.shape, lambda c, g: (0, 0, 0))],
        out_specs=[v_tile, v_tile, pl.BlockSpec((cb, HPS, HEAD, HEAD), lambda c, g: (c, g, 0, 0)),
                   pl.BlockSpec((cb, HPS, CHUNK, CHUNK), lambda c, g: (c, g, 0, 0))],
        out_shape=[jax.ShapeDtypeStruct((t, GD_HEADS * HEAD), F32), jax.ShapeDtypeStruct((t, GD_HEADS * HEAD), BF16),
                   jax.ShapeDtypeStruct((nc, GD_HEADS, HEAD, HEAD), F32),
                   jax.ShapeDtypeStruct((nc, GD_HEADS, CHUNK, CHUNK), F32)],
        scratch_shapes=[pltpu.VMEM((GD_HEADS, HEAD, HEAD), F32)],
        compiler_params=_params(_ARB, _ARB))(qn, kn, cv, gab, gz[0], alog, dtb, gain, lmat, masks)


def _gdn_bwd(qn, kn, cv, gab, gz, alog, dtb, gain, oraw, ssave, tsave, dog, consts, into):
    t = qn.shape[0]
    dproj, off, width = into
    assert width == GD_HEADS * HEAD and off % (HPS * HEAD) == 0
    nc = t // CHUNK
    cb = _chunks_per_step(nc)
    rows = cb * CHUNK
    nb = nc // cb
    lmat, lmat_t, masks = consts
    qk_tile, v_tile, gab_tile = _gd_specs(rows, nb)
    row128 = pl.BlockSpec((1, HEAD), lambda c, h: (0, 0))

    def body(q_ref, k_ref, v_ref, gab_ref, gz_ref, alog_ref, dtb_ref, gain_ref, oraw_ref, ssave_ref, tsave_ref, dog_ref,
             l_ref, lt_ref, mask_ref, _,
             dq_ref, dk_ref, dv_ref, dgab_ref, dgz_ref, small_ref, dstate):
        c = pl.program_id(0)
        g = pl.program_id(1)

        @pl.when(c == 0)
        def _():
            for hh in range(HPS):
                dstate[g * HPS + hh] = jnp.zeros((HEAD, HEAD), F32)

        @pl.when((c == 0) & (g == 0))
        def _():
            small_ref[...] = jnp.zeros_like(small_ref)

        alog = alog_ref[...]
        dtb = dtb_ref[...]
        gain_v = gain_ref[...]
        lane = lax.broadcasted_iota(jnp.int32, (1, HEAD), 1)
        last_row = (lax.broadcasted_iota(jnp.int32, (CHUNK, HEAD), 0) == CHUNK - 1).astype(F32)

        def one(j, carry):
            i = cb - 1 - j
            sl = pl.ds(pl.multiple_of(i * CHUNK, CHUNK), CHUNK)
            sp_arg, g_all, beta_all = _gd_gates(gab_ref[sl, :], alog, dtb)
            strict, eye = mask_ref[1], mask_ref[2]
            ltm = lt_ref[...]
            hs = range(HPS)
            heads = [g * HPS + hh for hh in hs]
            lq = [slice(hh // 2 * HEAD, (hh // 2 + 1) * HEAD) for hh in hs]
            lv = [slice(hh * HEAD, (hh + 1) * HEAD) for hh in hs]
            q = [q_ref[sl, s] for s in lq]
            k = [k_ref[sl, s] for s in lq]
            v = [v_ref[sl, s] for s in lv]
            gzv = [gz_ref[sl, s] for s in lv]
            chs = _gd_chunks(q, k, v, g_all, beta_all, [(g, hh) for hh in hs], l_ref, mask_ref,
                             tm=[tsave_ref[i, hh] for hh in hs])

            def col(name):
                return [ch[name] for ch in chs]

            def mul(x, y):
                return x * y

            tm, lm, eg, bb = col("tm"), col("lm"), col("eg"), col("bb")
            s0 = [ssave_ref[i, hh] for hh in hs]
            ds = [dstate[h] for h in heads]
            v_new = _each(lambda u, w, s: u - _dot(w, s), col("u"), col("w"), s0)

            o = [oraw_ref[sl, s] for s in lv]
            r = [lax.rsqrt(jnp.mean(x * x, axis=1, keepdims=True) + EPS) for x in o]
            on = _each(mul, o, r)
            dg_out = [dog_ref[sl, s] for s in lv]
            sgate = [_silu(x) for x in gzv]
            for hh in hs:
                dgz_ref[sl, lv[hh]] = (dg_out[hh] * on[hh] * gain_v * _dsilu(gzv[hh])).astype(BF16)
            small_ref[0:1, :] += sum(jnp.sum(d * s * n, axis=0, keepdims=True) for d, s, n in zip(dg_out, sgate, on))
            don = _each(lambda d, s: d * s * gain_v, dg_out, sgate)
            do = _each(lambda rr, dn, n: rr * (dn - n * jnp.mean(dn * n, axis=1, keepdims=True)), r, don, on)

            dv_new = _each(lambda a, d, b, s: _dot_tn(a, d) + _dot(b, s), col("qk"), do, col("kg"), ds)
            dqk = _each(_dot_nt, do, v_new)
            dkg = _each(_dot_nt, v_new, ds)
            dge = _each(lambda s, d: jnp.sum(_rowsum(s * d), axis=0, keepdims=True), s0, ds)
            both = _each(lambda d, dv: jnp.concatenate([d, dv], axis=0), do, dv_new)
            from_s = _each(_dot_nt, both, s0)
            dqg = [x[:CHUNK] for x in from_s]
            dw = [-x[CHUNK:] for x in from_s]
            ds_new = _each(lambda qg, w, bo, ge, s: _dot_tn(jnp.concatenate([qg, -w], axis=0), bo) + ge * s,
                           col("qg"), col("w"), both, col("ge"), ds)
            for hh in hs:
                dstate[heads[hh]] = ds_new[hh]

            side = _each(lambda dv, d: jnp.concatenate([dv, d], axis=1), dv_new, dw)
            back = _each(_dot_tn, tm, side)
            dvb = [x[:, :HEAD] for x in back]
            dkbg = [x[:, HEAD:] for x in back]
            dtm = _each(lambda sd, vb, kbg: _dot_nt(sd, jnp.concatenate([vb, kbg], axis=1)), side, col("vb"), col("kbg"))
            dtt = _each(_dot_nt, dtm, tm)
            da = _each(lambda t_, x: -_dot_tn(t_, x) * strict, tm, dtt)
            dal = _each(mul, da, lm)
            dqk_l = _each(mul, dqk, lm)
            stack = _each(lambda x, y: jnp.concatenate([x, y], axis=0), dal, dqk_l)
            on_k = _each(_dot, stack, k)
            dkb = _each(lambda x, y, e: x[:CHUNK] + y * e, on_k, dkbg, eg)
            dq = _each(lambda x, y, e: x[CHUNK:] + y * e, on_k, dqg, eg)
            dk = _each(lambda st, kb, qq, z, ekg, w_, b: _dot_tn(st, jnp.concatenate([kb, qq], axis=0)) + z * ekg + w_ * b,
                       stack, col("kb"), q, dkg, col("ekg"), dkb, bb)
            gmat = _each(lambda x, a, y, qk: x * a + y * qk, da, col("a"), dqk, col("qk"))
            t_kg = _each(lambda x, y: _rowsum(x * y), dkg, col("kg"))
            dgam = _each(lambda gm, x, qg, t_, y, kbg: (_rowsum(gm) - _row_to_col(jnp.sum(gm, axis=0, keepdims=True), eye)
                                                        + _rowsum(x * qg) - t_ + _rowsum(y * kbg)),
                         gmat, dqg, col("qg"), t_kg, dkbg, col("kbg"))
            dg_end = _each(lambda t_, e, ge: jnp.sum(t_, axis=0, keepdims=True) + e * ge[:, 0:1], t_kg, dge, col("ge"))
            dgam = _each(lambda x, e: x + last_row * e, dgam, dg_end)
            dbeta = _each(lambda x, kk, y, vv: _rowsum(x * kk) + _rowsum(y * vv), dkb, k, dvb, v)
            dg = _mx_each(ltm, dgam)

            for hh in hs:
                dv_ref[sl, lv[hh]] = dvb[hh] * bb[hh]
            fac_g = -jnp.exp(alog) * _sigmoid(sp_arg)
            fac_b = beta_all * (1.0 - beta_all)
            hot_g = [(lane == h).astype(F32) for h in heads]
            hot_b = [(lane == GD_HEADS + h).astype(F32) for h in heads]
            dga = _each(lambda x, hot: x * hot * fac_g, dg, hot_g)
            dgb = _each(lambda x, hot: x * hot * fac_b, dbeta, hot_b)
            small_ref[1:2, :] += sum(jnp.sum(x, axis=0, keepdims=True) for x in dga)
            small_ref[2:3, :] += sum(jnp.sum(x * hot * g_all, axis=0, keepdims=True) for x, hot in zip(dg, hot_g))
            for pair in range(HPS // 2):
                lqp = slice(pair * HEAD, (pair + 1) * HEAD)
                dq_ref[sl, lqp] = dq[2 * pair] + dq[2 * pair + 1]
                dk_ref[sl, lqp] = dk[2 * pair] + dk[2 * pair + 1]
            dgab_ref[sl, :] = sum(a + b for a, b in zip(dga, dgb))
            return carry

        lax.fori_loop(0, cb, one, 0, unroll=4)

    groups = GD_HEADS // HPS
    outs = [jax.ShapeDtypeStruct((t, 1024), F32), jax.ShapeDtypeStruct((t, 1024), F32),
            jax.ShapeDtypeStruct((t, 2048), F32), jax.ShapeDtypeStruct((t, groups * HEAD), F32),
            jax.ShapeDtypeStruct(dproj.shape, dproj.dtype), jax.ShapeDtypeStruct((8, HEAD), F32)]
    dgz_tile = pl.BlockSpec((rows, HPS * HEAD), lambda c, g: (nb - 1 - c, off // (HPS * HEAD) + g))
    return pl.pallas_call(
        body, name="gdn_bwd", grid=(nb, groups),
        in_specs=[qk_tile, qk_tile, v_tile, gab_tile, _view_tile(gz, rows, HPS * HEAD, lambda c: nb - 1 - c),
                  row128, row128, row128, v_tile,
                  pl.BlockSpec((cb, HPS, HEAD, HEAD), lambda c, g: (nb - 1 - c, g, 0, 0)),
                  pl.BlockSpec((cb, HPS, CHUNK, CHUNK), lambda c, g: (nb - 1 - c, g, 0, 0)), v_tile,
                  pl.BlockSpec(lmat.shape, lambda c, g: (0, 0)),
                  pl.BlockSpec(lmat_t.shape, lambda c, g: (0, 0)),
                  pl.BlockSpec(masks.shape, lambda c, g: (0, 0, 0)), _ANY],
        out_specs=[qk_tile, qk_tile, v_tile, pl.BlockSpec((rows, HEAD), lambda c, g: (nb - 1 - c, g)), dgz_tile,
                   pl.BlockSpec((8, HEAD), lambda c, g: (0, 0))],
        out_shape=outs, scratch_shapes=[pltpu.VMEM((GD_HEADS, HEAD, HEAD), F32)], input_output_aliases={15: 4},
        compiler_params=_params(_ARB, _ARB))(qn, kn, cv, gab, gz[0], alog, dtb, gain, oraw, ssave, tsave, dog,
                                             lmat, lmat_t, masks, dproj)


def _fold_groups(wide):
    t, width = wide.shape
    tr = _pick(t, CONV_ROWS, 8)

    def body(w_ref, o_ref):
        acc = w_ref[:, 0:HEAD]
        for j in range(1, width // HEAD):
            acc = acc + w_ref[:, j * HEAD:(j + 1) * HEAD]
        o_ref[...] = acc.astype(BF16)

    return pl.pallas_call(
        body, name="fold_gate_grads", grid=(t // tr,), in_specs=[_row_spec(tr, width)], out_specs=_row_spec(tr, HEAD),
        out_shape=jax.ShapeDtypeStruct((t, HEAD), BF16), compiler_params=_params(_PAR))(wide)


def _adam_math(w, g, m, v):
    m2 = ADAM_B1 * m + (1.0 - ADAM_B1) * g
    v2 = ADAM_B2 * v + (1.0 - ADAM_B2) * (g * g)
    m_hat = m2 / (1.0 - ADAM_B1 ** ADAM_STEP)
    v_hat = v2 / (1.0 - ADAM_B2 ** ADAM_STEP)
    delta = -ADAM_LR * (m_hat / (jnp.sqrt(v_hat) + ADAM_EPS) + ADAM_WD * w)
    return delta, m2, v2


def _adamw(w, g, m, v, name, after=None):
    r, c = w.shape
    tr = r
    for cand in range(8, r + 1, 8):
        if r % cand == 0 and cand * c * 4 <= (2 << 20):
            tr = cand
    if r % 8 != 0:
        tr = r

    def body(w_ref, g_ref, m_ref, v_ref, *rest):
        d_ref, m2_ref, v2_ref = rest[-3:]
        d, m2, v2 = _adam_math(w_ref[...], g_ref[...], m_ref[...], v_ref[...])
        d_ref[...] = d
        m2_ref[...] = m2
        v2_ref[...] = v2

    spec = pl.BlockSpec((tr, c), lambda i: (i, 0))
    extra = [] if after is None else [after]
    return pl.pallas_call(
        body, name=name, grid=(r // tr,), in_specs=[spec] * 4 + [_ANY] * len(extra), out_specs=[spec] * 3,
        out_shape=[jax.ShapeDtypeStruct((r, c), F32)] * 3, compiler_params=_params(_PAR))(w, g, m, v, *extra)


_ANY = pl.BlockSpec(memory_space=pl.ANY)


def _place():
    return lax.axis_index("x"), lax.axis_index("y"), lax.axis_index("c")


def _gather_weights(packs, nchs, name):
    n = len(packs)
    halves = [p.shape[0] // 2 for p in packs]
    base = [sum(nchs[:i]) for i in range(n)]
    total = sum(nchs)
    for p, h, k in zip(packs, halves, nchs):
        assert p.shape[0] == 2 * h and h % k == 0 and (h // k) % 16 == 0

    def body(*refs):
        p_refs, g_refs, (send_sems, recv_sems) = refs[:n], refs[n:2 * n], refs[2 * n:]
        x, y, c = _place()
        sibling = (x, y, 1 - c)
        chips = [(1 - x, y), (x, 1 - y), (1 - x, 1 - y)]
        chunks = [(a, q) for a in range(n) for q in range(nchs[a])]

        def rows_of(a, pc, q):
            ch = halves[a] // nchs[a]
            return pl.ds(pl.multiple_of(pc * halves[a] + q * ch, 16), ch)

        def piece(a, px, py, pc, q):
            return g_refs[a].at[2 * px + py, rows_of(a, pc, q), :]

        def copy(k, src, dst, to):
            return pltpu.make_async_remote_copy(src_ref=src, dst_ref=dst, send_sem=send_sems.at[k],
                                                recv_sem=recv_sems.at[k], device_id=to, device_id_type=MESH)

        def sem_of(j, a, q):
            return j * total + base[a] + q

        first = {(j, a, q): copy(sem_of(j, a, q), p_refs[a].at[rows_of(a, c, q), :], piece(a, x, y, c, q), (*chip, c))
                 for j, chip in enumerate(chips) for a, q in chunks}
        for a, q in chunks:
            for j in range(3):
                first[j, a, q].start()
        passed = {(j, a, q): copy(sem_of(3 + j, a, q), piece(a, *chip, c, q), piece(a, *chip, c, q), sibling)
                  for j, chip in enumerate(chips) for a, q in chunks}
        for a, q in chunks:
            for j, chip in enumerate(chips):
                copy(sem_of(j, a, q), p_refs[a].at[rows_of(a, c, q), :], piece(a, *chip, c, q), (*chip, c)).wait_recv()
                passed[j, a, q].start()
        for a, q in chunks:
            for j, chip in enumerate(chips):
                copy(sem_of(3 + j, a, q), piece(a, *chip, 1 - c, q), piece(a, *chip, 1 - c, q), sibling).wait_recv()
        for key in first:
            first[key].wait_send()
            passed[key].wait_send()

    return pl.pallas_call(
        body, name=name, out_shape=[jax.ShapeDtypeStruct((4,) + p.shape, p.dtype) for p in packs],
        in_specs=[_ANY] * n, out_specs=[_ANY] * n,
        scratch_shapes=[pltpu.SemaphoreType.DMA((6 * total,)), pltpu.SemaphoreType.DMA((6 * total,))])(*packs)


def _swap_with_sibling(arrs, nchs, lead, name, halves=False):
    n = len(arrs)
    jobs = []
    hs = [arr.shape[-2] // (2 if halves else 1) for arr in arrs]
    for a, (h, k) in enumerate(zip(hs, nchs)):
        assert h % k == 0 and (h // k) % 16 == 0
        for s in (range(lead) if lead else [None]):
            jobs += [(a, s, q * (h // k), h // k) for q in range(k)]

    def body(*refs):
        src, dst, (send_sems, recv_sems) = refs[:n], refs[n:2 * n], refs[2 * n:]
        x, y, c = _place()

        def at(ref, s, r0, rows):
            return ref.at[pl.ds(r0, rows), :] if s is None else ref.at[s, pl.ds(r0, rows), :]

        def src_rows(a, r0):
            return pl.multiple_of((1 - c) * hs[a] + r0, 16) if halves else r0

        copies = [pltpu.make_async_remote_copy(
            src_ref=at(src[a], s, src_rows(a, r0), rows), dst_ref=at(dst[a], s, r0, rows), send_sem=send_sems.at[k],
            recv_sem=recv_sems.at[k], device_id=(x, y, 1 - c), device_id_type=MESH)
            for k, (a, s, r0, rows) in enumerate(jobs)]
        for cp in copies:
            cp.start()
        for cp in copies:
            cp.wait()

    shapes = [jax.ShapeDtypeStruct(arr.shape[:-2] + (h, arr.shape[-1]), arr.dtype) for arr, h in zip(arrs, hs)]
    return pl.pallas_call(
        body, name=name, out_shape=shapes, in_specs=[_ANY] * n, out_specs=[_ANY] * n,
        scratch_shapes=[pltpu.SemaphoreType.DMA((len(jobs),)), pltpu.SemaphoreType.DMA((len(jobs),))])(*arrs)


def _add2(full, b, core, name):
    n, rows, w = b.shape
    tr = _pick(rows, 256, 16)
    nblk = rows // tr

    def body(c_ref, a_ref, b_ref, o_ref):
        o_ref[...] = (a_ref[...].astype(F32) + b_ref[...].astype(F32)).astype(BF16)

    spec = pl.BlockSpec((1, tr, w), lambda i, j, c_ref: (i, j, 0))
    grid_spec = pltpu.PrefetchScalarGridSpec(
        num_scalar_prefetch=1, grid=(n, nblk),
        in_specs=[pl.BlockSpec((1, tr, w), lambda i, j, c_ref: (i, c_ref[0] * nblk + j, 0)), spec], out_specs=spec)
    return pl.pallas_call(
        body, name=name, grid_spec=grid_spec, out_shape=jax.ShapeDtypeStruct(b.shape, BF16),
        compiler_params=_params(_PAR, _PAR))(core, full, b)


def _reduce_chips(partials, nchs, name):
    n = len(partials)
    jobs = []
    for a, (arr, k) in enumerate(zip(partials, nchs)):
        h = arr.shape[1]
        assert h % k == 0 and (h // k) % 16 == 0
        jobs += [(a, q * (h // k), h // k) for q in range(k)]

    def body(*refs):
        src, dst, (send_sems, recv_sems) = refs[:n], refs[n:2 * n], refs[2 * n:]
        x, y, c = _place()
        chips = [(1 - x, y), (x, 1 - y), (1 - x, 1 - y)]
        copies = [pltpu.make_async_remote_copy(
            src_ref=src[a].at[2 * px + py, pl.ds(r0, rows), :], dst_ref=dst[a].at[j, pl.ds(r0, rows), :],
            send_sem=send_sems.at[3 * k + j], recv_sem=recv_sems.at[3 * k + j],
            device_id=(px, py, c), device_id_type=MESH)
            for k, (a, r0, rows) in enumerate(jobs) for j, (px, py) in enumerate(chips)]
        for cp in copies:
            cp.start()
        for cp in copies:
            cp.wait()

    return pl.pallas_call(
        body, name=name,
        out_shape=[jax.ShapeDtypeStruct((3,) + p.shape[1:], p.dtype) for p in partials],
        in_specs=[_ANY] * n, out_specs=[_ANY] * n,
        scratch_shapes=[pltpu.SemaphoreType.DMA((3 * len(jobs),)), pltpu.SemaphoreType.DMA((3 * len(jobs),))])(*partials)


_HBM = pl.BlockSpec(memory_space=pltpu.HBM)
_SEM = pl.BlockSpec(memory_space=pltpu.SEMAPHORE)
_DATAFLOW = pltpu.SideEffectType.DATAFLOW_SIDE_EFFECTING


def _ici_jobs(srcs, nchs, kind):
    jobs = []
    for a, (arr, k) in enumerate(zip(srcs, nchs)):
        h = arr.shape[0] // 2 if kind == "gather" else arr.shape[1]
        assert h % k == 0 and (h // k) % 16 == 0
        jobs += [(a, h, q * (h // k), h // k) for q in range(k)]
    return jobs


def _ici_copies(src, land, send_sems, recv_sems, jobs, kind):
    x, y, c = _place()
    chips = [(1 - x, y), (x, 1 - y), (1 - x, 1 - y)]
    copies = []
    for k, (a, h, r0, rows) in enumerate(jobs):
        for j, (px, py) in enumerate(chips):
            if kind == "gather":
                at = pl.ds(pl.multiple_of(c * h + r0, 16), rows)
                s, d = src[a].at[at, :], land[a].at[2 * x + y, at, :]
            else:
                s, d = src[a].at[2 * px + py, pl.ds(r0, rows), :], land[a].at[j, pl.ds(r0, rows), :]
            copies.append(pltpu.make_async_remote_copy(
                src_ref=s, dst_ref=d, send_sem=send_sems.at[3 * k + j], recv_sem=recv_sems.at[3 * k + j],
                device_id=(px, py, c), device_id_type=MESH))
    return copies


def _ici_start(srcs, nchs, kind, name):
    n = len(srcs)
    jobs = _ici_jobs(srcs, nchs, kind)
    lead = (lambda s: (4,) + s.shape) if kind == "gather" else (lambda s: (3,) + s.shape[1:])
    lands = [lax.empty(lead(s), s.dtype) for s in srcs]

    def body(*refs):
        src, land = refs[:n], refs[n:2 * n]
        send_sems, recv_sems, token = refs[2 * n], refs[2 * n + 1], refs[-1]
        for cp in _ici_copies(src, land, send_sems, recv_sems, jobs, kind):
            cp.start()
        token[...] = jnp.zeros_like(token)

    hbm = [pltpu.HBM(a.shape, a.dtype) for a in srcs + lands]
    outs = pl.pallas_call(
        body, name=name,
        out_shape=[pltpu.SemaphoreType.DMA((3 * len(jobs),)), pltpu.SemaphoreType.DMA((3 * len(jobs),))] + hbm
        + [jax.ShapeDtypeStruct((8, 128), F32)],
        in_specs=[_HBM] * (2 * n), out_specs=[_SEM, _SEM] + [_HBM] * (2 * n) + [pl.BlockSpec(memory_space=pltpu.VMEM)],
        input_output_aliases={i: 2 + i for i in range(2 * n)},
        compiler_params=pltpu.CompilerParams(has_side_effects=_DATAFLOW),
    )(*[pltpu.with_memory_space_constraint(a, pltpu.HBM) for a in srcs + lands])
    return (outs[0], outs[1], list(outs[2:2 + n]), list(outs[2 + n:2 + 2 * n]), nchs, kind), outs[-1]


def _ici_wait(handle, after, name):
    send_sems, recv_sems, srcs, lands, nchs, kind = handle
    n = len(srcs)
    jobs = _ici_jobs(srcs, nchs, kind)

    def body(*refs):
        src, land = refs[:n], refs[n:2 * n]
        for cp in _ici_copies(src, land, refs[2 * n], refs[2 * n + 1], jobs, kind):
            cp.wait_send()
            cp.wait_recv()

    outs = pl.pallas_call(
        body, name=name, out_shape=[pltpu.HBM(a.shape, a.dtype) for a in srcs + lands],
        in_specs=[_HBM] * (2 * n) + [_SEM, _SEM, _ANY], out_specs=[_HBM] * (2 * n),
        input_output_aliases={i: i for i in range(2 * n)},
        compiler_params=pltpu.CompilerParams(has_side_effects=_DATAFLOW),
    )(*srcs, *lands, send_sems, recv_sems, after)
    return list(outs[:n]), list(outs[n:])


def _pass_to_sibling(gathered, nchs, name):
    n = len(gathered)
    jobs = _ici_jobs([jax.ShapeDtypeStruct(g.shape[1:], g.dtype) for g in gathered], nchs, "gather")

    def body(*refs):
        src, dst, (send_sems, recv_sems) = refs[:n], refs[n:2 * n], refs[2 * n:]
        x, y, c = _place()
        slots = [2 * (1 - x) + y, 2 * x + (1 - y), 2 * (1 - x) + (1 - y)]

        def copy(k, j, pc):
            a, h, r0, rows = jobs[k]
            at = pl.ds(pl.multiple_of(pc * h + r0, 16), rows)
            return pltpu.make_async_remote_copy(
                src_ref=src[a].at[slots[j], at, :], dst_ref=dst[a].at[slots[j], at, :], send_sem=send_sems.at[3 * k + j],
                recv_sem=recv_sems.at[3 * k + j], device_id=(x, y, 1 - c), device_id_type=MESH)

        pairs = [(k, j) for k in range(len(jobs)) for j in range(3)]
        for k, j in pairs:
            copy(k, j, c).start()
        for k, j in pairs:
            copy(k, j, c).wait_send()
            copy(k, j, 1 - c).wait_recv()

    return pl.pallas_call(
        body, name=name, out_shape=[jax.ShapeDtypeStruct(g.shape, g.dtype) for g in gathered],
        in_specs=[_ANY] * n, out_specs=[_ANY] * n, input_output_aliases={i: i for i in range(n)},
        scratch_shapes=[pltpu.SemaphoreType.DMA((3 * len(jobs),)), pltpu.SemaphoreType.DMA((3 * len(jobs),))])(*gathered)


def _add4(own, got, name):
    rows, w = own.shape
    tr = _pick(rows, 128, 16)

    def body(a_ref, b_ref, o_ref):
        o_ref[...] = ((a_ref[...].astype(F32) + b_ref[0].astype(F32)) + b_ref[1].astype(F32)) + b_ref[2].astype(F32)

    return pl.pallas_call(
        body, name=name, grid=(rows // tr,),
        in_specs=[pl.BlockSpec((tr, w), lambda i: (i, 0)), pl.BlockSpec((3, tr, w), lambda i: (0, i, 0))],
        out_specs=pl.BlockSpec((tr, w), lambda i: (i, 0)), out_shape=jax.ShapeDtypeStruct((rows, w), F32),
        compiler_params=_params(_PAR))(own, got)


def _small_sync(gs, ws, ms, vs):
    rows = gs.shape[0]
    vmem = pl.BlockSpec(memory_space=pltpu.VMEM)

    def body(g_ref, w_ref, m_ref, v_ref, sum_ref, d_ref, m2_ref, v2_ref, buf, send_sems, recv_sems):
        x, y, c = _place()
        me = 4 * x + 2 * y + c
        buf[me] = g_ref[...]
        copies = []
        for k in range(1, 8):
            peer = (x ^ (k >> 2), y ^ ((k >> 1) & 1), c ^ (k & 1))
            copies.append(pltpu.make_async_remote_copy(
                src_ref=g_ref, dst_ref=buf.at[me], send_sem=send_sems.at[k - 1], recv_sem=recv_sems.at[k - 1],
                device_id=peer, device_id_type=MESH))
        for cp in copies:
            cp.start()
        for cp in copies:
            cp.wait()
        total = buf[0]
        for i in range(1, 8):
            total = total + buf[i]
        sum_ref[...] = total
        d, m2, v2 = _adam_math(w_ref[...], total, m_ref[...], v_ref[...])
        d_ref[...] = d
        m2_ref[...] = m2
        v2_ref[...] = v2

    shape = jax.ShapeDtypeStruct((rows, 128), F32)
    return pl.pallas_call(
        body, name="small_sync", out_shape=[shape] * 4, in_specs=[vmem] * 4, out_specs=[vmem] * 4,
        scratch_shapes=[pltpu.VMEM((8, rows, 128), F32), pltpu.SemaphoreType.DMA((7,)),
                        pltpu.SemaphoreType.DMA((7,))])(gs, ws, ms, vs)


_GROUPS = {
    "ffn1": dict(cols=("ffn1_w_in", 1408), rows=(("ffn1_w_out", 704, 704),), chunks=(8, 2)),
    "ffn2": dict(cols=("ffn2_w_in", 1408), rows=(("ffn2_w_out", 704, 704),), chunks=(8, 2)),
    "mixer": dict(cols=("w_in", 3080), chunks=(8, 4),
                  rows=(("w_branch_hgrn", 256, 256), ("w_branch_gdn", 512, 512), ("w_out", 256, 256),
                        ("gdn_conv_w", CONV_K, 128))),
}
_BIG_NAMES = tuple(n for g in _GROUPS.values() for n in (g["cols"][0],) + tuple(r[0] for r in g["rows"]))


def _group_names(group):
    return (group["cols"][0],) + tuple(r[0] for r in group["rows"])


def _pack(parts, lead, group):
    ax = len(lead)
    rows = []
    for n, r, padded in group["rows"]:
        p = parts[n]
        if padded != r:
            p = jnp.tile(p, (1,) * ax + (padded // r, 1))
        rows.append(p)
    return [parts[group["cols"][0]], rows[0] if len(rows) == 1 else jnp.concatenate(rows, axis=ax)]


def _unpack(cols, rows, group):
    out, off = {group["cols"][0]: cols}, 0
    for n, r, padded in group["rows"]:
        out[n] = rows[..., off:off + r, :]
        off += padded
    return out


def _is_col_sharded(name):
    return name in ("ffn1_w_in", "ffn2_w_in", "w_in", "gdn_conv_w")


def _full_from_shards(name, g):
    if _is_col_sharded(name):
        return jnp.transpose(g, (1, 0, 2)).reshape(g.shape[1], -1)
    return g.reshape(-1, g.shape[2])


def _shards_from_full(name, full):
    if _is_col_sharded(name):
        return jnp.transpose(full.reshape(full.shape[0], 4, -1), (1, 0, 2))
    return full.reshape(4, -1, full.shape[1])


_SMALL = (("ffn1_norm", 8), ("mix_norm", 8), ("hgrn_lb_logits", 16), ("hgrn_out_norm", 8), ("gdn_a_log", 8),
          ("gdn_dt_bias", 8), ("gdn_out_norm", 8), ("ffn2_norm", 8), ("final_norm", 8), ("loss", 8))
_SMALL_ROWS = sum(r for _, r in _SMALL)


def _pack_small(parts):
    out = []
    for name, rows in _SMALL:
        p = parts[name].reshape(-1).astype(F32)
        if p.shape[0] <= 128:
            if p.shape[0] < 128:
                p = jnp.concatenate([p, jnp.zeros((128 - p.shape[0],), F32)])
            p = jnp.broadcast_to(p.reshape(1, 128), (rows, 128))
        out.append(p.reshape(rows, 128))
    return jnp.concatenate(out, axis=0)


def _unpack_small(packed, shapes):
    out, off = {}, 0
    for name, rows in _SMALL:
        n = int(np.prod(shapes[name]))
        out[name] = packed[off:off + rows].reshape(-1)[:n].reshape(shapes[name])
        off += rows
    return out


def _ffn_fwd(x, gain, w_in, w_out, tag):
    n = _rmsnorm_fwd(x, gain, tag + "_norm")
    a, b, hm = _ffn_in_act(n, w_in, tag + "_in")
    out = _mm(hm, w_out, alpha=0.5, res=x, name=tag + "_out")
    return out, (n, a, b)


def _ffn_bwd(x, gain, w_in, w_out, saved, dout, dout_bf, tag):
    n, a, b = saved
    da, db, hm = _ffn_dact(dout_bf, w_out, a, b, tag + "_dact")
    dw_out = _mm(hm, dout_bf, ta=True, alpha=0.5, out_dtype=BF16, name=tag + "_dwout")
    dwa = _mm(n, da, ta=True, out_dtype=BF16, name=tag + "_dwin_a")
    dwb = _mm(n, db, ta=True, out_dtype=BF16, name=tag + "_dwin_b")
    half = D_FF // 2
    dw_in = jnp.stack([dwa[:, :half], dwa[:, half:], dwb[:, :half], dwb[:, half:]])
    dn = _mm(da, w_in, tb=True, name=tag + "_dnorm_a")
    dn = _mm(db, w_in, tb=True, res=dn, b_from=D_FF, name=tag + "_dnorm_b")
    dx, dx_bf, dgain = _rmsnorm_bwd(x, gain, dn, dout, tag + "_dx")
    return dx, dx_bf, dgain, dw_in, dw_out


def _pad_lanes(v):
    return jnp.concatenate([v.reshape(1, -1), jnp.zeros((1, HEAD - v.size), F32)], axis=1)


def _local_step(x, tgt, small, exchange):
    hg_c = _hg_consts()
    gd_c = _gd_consts()
    alog = _pad_lanes(small["gdn_a_log"])
    dtb = _pad_lanes(small["gdn_dt_bias"])
    logits = small["hgrn_lb_logits"]
    hg_gain = small["hgrn_out_norm"].reshape(1, HEAD)
    gd_gain = small["gdn_out_norm"].reshape(1, HEAD)
    g1, gm, g2 = small["ffn1_norm"].reshape(1, -1), small["mix_norm"].reshape(1, -1), small["ffn2_norm"].reshape(1, -1)
    gf = small["final_norm"].reshape(1, -1)
    qscale = HEAD ** -0.5

    w1 = exchange.weights("ffn1")
    started = exchange.prefetch("mixer")
    h1, ffn1_saved = _ffn_fwd(x, g1 + started, w1["ffn1_w_in"], w1["ffn1_w_out"], "ffn1")
    u = _rmsnorm_fwd(h1, gm, "mix_norm")
    w = exchange.weights("mixer", after=u)
    started = exchange.prefetch("ffn2")
    seg, off = {}, 0
    for name, size in zip(IN_NAMES, IN_SIZES):
        seg[name] = w["w_in"][:, off:off + size]
        off += size
    w_gab = jnp.concatenate([seg["ga"], seg["gb"], jnp.zeros((D_MODEL, HEAD - 32), BF16)], axis=1)
    big_segs = [n for n in IN_NAMES if n not in ("ga", "gb")]
    conv8 = jnp.concatenate([w["gdn_conv_w"].astype(F32), jnp.zeros((8 - CONV_K, 4096), F32)], axis=0)
    conv_q, conv_k, conv_v = conv8[:, :1024], conv8[:, 1024:2048], conv8[:, 2048:]
    w_main = jnp.concatenate([seg[n] for n in big_segs], axis=1)
    proj = _mm(u, w_main, name="proj", tm_max=2048)
    pr, off = {}, 0
    for n in big_segs:
        pr[n] = _view(proj, off, seg[n].shape[1])
        off += seg[n].shape[1]
    gab = _mm(u, w_gab, name="proj_gab")
    oh_raw, oh, s_h = _hgrn_fwd(pr["hq"], pr["hf"], pr["hi"], pr["hg"], logits, hg_gain + started, hg_c)
    qn = _conv_fwd(pr["gq"], conv_q, qscale, "conv_q")
    kn = _conv_fwd(pr["gk"], conv_k, 1.0, "conv_k")
    cv = _conv_fwd(pr["gv"], conv_v, None, "conv_v")
    og_raw, og, s_g, t_g = _gdn_fwd(qn, kn, cv, gab, pr["gz"], alog, dtb, gd_gain, gd_c)
    yh = _mm(oh, w["w_branch_hgrn"], out_dtype=BF16, name="branch_h")
    yg = _mm(og, w["w_branch_gdn"], out_dtype=BF16, name="branch_g")
    ym = _merge_fwd(yh, yg, pr["gate_h"], pr["gate_g"])
    h2 = _mm(ym, w["w_out"], res=h1, name="mix_out")
    w2 = exchange.weights("ffn2", after=h2)
    h3, ffn2_saved = _ffn_fwd(h2, g2, w2["ffn2_w_in"], w2["ffn2_w_out"], "ffn2")
    loss, dh3, dh3_bf, d_gf = _final_loss(h3, gf, tgt)

    dh2, dh2_bf, d_g2, d_f2in, d_f2out = _ffn_bwd(h2, g2, w2["ffn2_w_in"], w2["ffn2_w_out"], ffn2_saved, dh3, dh3_bf,
                                                  "ffn2")
    started = exchange.reduce("ffn2", {"ffn2_w_in": d_f2in, "ffn2_w_out": d_f2out}, behind=True)
    dym =_mm(dh2_bf, w["w_out"], tb=True, name="d_merge")
    d_wout = _mm(ym, dh2_bf, ta=True, out_dtype=BF16, name="d_w_out")
    dproj = lax.empty((x.shape[0], w_main.shape[1]), BF16)
    dyh, dyg, dproj = _merge_bwd(dym, yh, yg, pr["gate_h"], pr["gate_g"], _into(dproj, pr["gate_h"][1], 2 * D_MODEL))
    d_wbh = _mm(oh, dyh, ta=True, out_dtype=BF16, name="d_w_branch_h")
    d_wbg = _mm(og, dyg, ta=True, out_dtype=BF16, name="d_w_branch_g")
    doh = _mm(dyh, w["w_branch_hgrn"], tb=True, name="d_oh")
    dog = _mm(dyg, w["w_branch_gdn"], tb=True, name="d_og")
    dproj, d_hg_gain, d_lb0 = _hgrn_bwd(pr["hq"], pr["hf"], pr["hi"], pr["hg"], logits, hg_gain + started, oh_raw,
                                        s_h, doh, hg_c, _into(dproj, pr["hq"][1], 4 * D_MODEL))
    d_qn, d_kn, d_cv, d_gab_wide, dproj, gd_small = _gdn_bwd(qn, kn, cv, gab, pr["gz"], alog, dtb, gd_gain, og_raw,
                                                             s_g, t_g, dog, gd_c, _into(dproj, *pr["gz"][1:]))
    d_gab = _fold_groups(d_gab_wide)
    dc_q, dwc_q = _conv_bwd_a(pr["gq"], conv_q, d_qn, qscale, "dconv_q")
    dc_k, dwc_k = _conv_bwd_a(pr["gk"], conv_k, d_kn, 1.0, "dconv_k")
    dc_v, dwc_v = _conv_bwd_a(pr["gv"], conv_v, d_cv, None, "dconv_v")
    dproj = _conv_bwd_b(dc_q, conv_q, "dconvx_q", _into(dproj, *pr["gq"][1:]))
    dproj = _conv_bwd_b(dc_k, conv_k, "dconvx_k", _into(dproj, *pr["gk"][1:]))
    dproj = _conv_bwd_b(dc_v, conv_v, "dconvx_v", _into(dproj, *pr["gv"][1:]))
    du =_mm(d_gab, w_gab, tb=True, name="du_gab")
    du = _mm(dproj, w_main, tb=True, res=du, name="du")
    d_wmain = _mm(u, dproj, ta=True, out_dtype=BF16, name="dw_main")
    d_wgab = _mm(u, d_gab, ta=True, out_dtype=BF16, name="dw_gab")
    cut = IN_WIDTH // 4
    d_win = jnp.stack([d_wmain[:, :cut], d_wmain[:, cut:2 * cut],
                       jnp.concatenate([d_wmain[:, 2 * cut:8192], d_wgab[:, :32], d_wmain[:, 8192:3 * cut - 32]], axis=1),
                       d_wmain[:, 3 * cut - 32:]])
    d_conv = jnp.concatenate([dwc_q[:CONV_K], dwc_k[:CONV_K], dwc_v[:CONV_K]], axis=1).astype(BF16)
    started = exchange.reduce("mixer", {"w_in": d_win, "gdn_conv_w": d_conv, "w_branch_hgrn": d_wbh,
                                        "w_branch_gdn": d_wbg, "w_out": d_wout}, behind=True)
    dh1, dh1_bf, d_gm = _rmsnorm_bwd(h1, gm + started, du, dh2, "mix_dnorm")
    dx, _, d_g1, d_f1in, d_f1out = _ffn_bwd(x, g1, w1["ffn1_w_in"], w1["ffn1_w_out"], ffn1_saved, dh1, dh1_bf, "ffn1")
    exchange.reduce("ffn1", {"ffn1_w_in": d_f1in, "ffn1_w_out": d_f1out}, behind=True)
    d_lb0 = d_lb0.reshape(1, -1)
    sm = {"ffn1_norm": d_g1, "mix_norm": d_gm, "hgrn_lb_logits": jnp.concatenate([d_lb0, -d_lb0], axis=0),
          "hgrn_out_norm": d_hg_gain, "gdn_a_log": gd_small[2, :16], "gdn_dt_bias": gd_small[1, :16],
          "gdn_out_norm": gd_small[0], "ffn2_norm": d_g2, "final_norm": d_gf, "loss": loss[0, :1]}
    return dx, sm


class _Exchange:
    def __init__(self, wts):
        self.wts = wts
        xi, yi, ci = _place()
        self.chip = 2 * xi + yi
        self.south = ci == 0
        self.core = ci.reshape(1).astype(jnp.int32)
        self.mine = {}
        self.coming = {}
        self.going = {}

    def _packs(self, tag):
        group = _GROUPS[tag]
        return _pack({n: self.wts[n][0].astype(BF16) for n in _group_names(group)}, (), group)

    def prefetch(self, tag):
        packs = self._packs(tag)
        handle, token = _ici_start(packs, _GROUPS[tag]["chunks"], "gather", "gather_start_" + tag)
        self.coming[tag] = handle
        return token[0:1, 0:1]

    def weights(self, tag, after=None):
        group = _GROUPS[tag]
        if tag in self.coming:
            packs, halves = _ici_wait(self.coming.pop(tag), after, "gather_wait_" + tag)
            others = _pass_to_sibling(halves, group["chunks"], "gather_pass_" + tag)
        else:
            packs = self._packs(tag)
            others = _gather_weights(packs, group["chunks"], "gather_" + tag)
        whole = [lax.dynamic_update_index_in_dim(g, p, self.chip, 0) for g, p in zip(others, packs)]
        gathered = _unpack(*whole, group)
        return {n: _full_from_shards(n, gathered[n]) for n in _group_names(group)}

    def reduce(self, tag, grads, behind=False):
        group = _GROUPS[tag]
        shards = {n: (grads[n] if grads[n].ndim == 3 else _shards_from_full(n, grads[n])) for n in _group_names(group)}
        gpacks = _pack(shards, (4,), group)
        got = _swap_with_sibling(gpacks, group["chunks"], 4, "reduce_pair_" + tag, halves=True)
        sums = [_add2(a, b, self.core, "add_pair_%s_%d" % (tag, i)) for i, (a, b) in enumerate(zip(gpacks, got))]
        if behind:
            handle, token = _ici_start(sums, group["chunks"], "reduce", "reduce_start_" + tag)
            self.going[tag] = handle
            self.token = token
            return token[0:1, 0:1]
        self._add_chips(tag, sums, _reduce_chips(sums, group["chunks"], "reduce_chips_" + tag))
        return None

    def _add_chips(self, tag, sums, from_chips):
        self.mine[tag] = [_add4(lax.dynamic_index_in_dim(s, self.chip, axis=0, keepdims=False), f,
                                "add_chips_%s_%d" % (tag, i)) for i, (s, f) in enumerate(zip(sums, from_chips))]

    def finish(self, tags, after):
        for tag in tags:
            if tag in self.going:
                self._add_chips(tag, *_ici_wait(self.going.pop(tag), after, "reduce_wait_" + tag))
        mine = [a for t in tags for a in self.mine[t]]
        nchs = [k for t in tags for k in _GROUPS[t]["chunks"]]
        theirs = _swap_with_sibling(mine, nchs, 0, "share_pair_" + tags[0])
        whole = [jnp.concatenate([jnp.where(self.south, a, b), jnp.where(self.south, b, a)], axis=0)
                 for a, b in zip(mine, theirs)]
        reduced = {}
        for i, t in enumerate(tags):
            reduced.update(_unpack(whole[2 * i], whole[2 * i + 1], _GROUPS[t]))
        return reduced


_WEIGHTS = ("ffn1_norm", "ffn1_w_in", "ffn1_w_out", "mix_norm", "w_in", "hgrn_lb_logits", "hgrn_out_norm",
            "gdn_conv_w", "gdn_a_log", "gdn_dt_bias", "gdn_out_norm", "w_branch_hgrn", "w_branch_gdn", "w_out",
            "ffn2_norm", "ffn2_w_in", "ffn2_w_out", "final_norm")


def kernel(x, ffn1_norm, ffn1_w_in, ffn1_w_out, mix_norm, w_in, hgrn_lb_logits, hgrn_out_norm, gdn_conv_w, gdn_a_log, gdn_dt_bias, gdn_out_norm, w_branch_hgrn, w_branch_gdn, w_out, ffn2_norm, ffn2_w_in, ffn2_w_out, final_norm, loss_target, m_ffn1_norm, m_ffn1_w_in, m_ffn1_w_out, m_mix_norm, m_w_in, m_hgrn_lb_logits, m_hgrn_out_norm, m_gdn_conv_w, m_gdn_a_log, m_gdn_dt_bias, m_gdn_out_norm, m_w_branch_hgrn, m_w_branch_gdn, m_w_out, m_ffn2_norm, m_ffn2_w_in, m_ffn2_w_out, m_final_norm, v_ffn1_norm, v_ffn1_w_in, v_ffn1_w_out, v_mix_norm, v_w_in, v_hgrn_lb_logits, v_hgrn_out_norm, v_gdn_conv_w, v_gdn_a_log, v_gdn_dt_bias, v_gdn_out_norm, v_w_branch_hgrn, v_w_branch_gdn, v_w_out, v_ffn2_norm, v_ffn2_w_in, v_ffn2_w_out, v_final_norm):
    args = dict(locals())
    wts = {n: args[n] for n in _WEIGHTS}
    moms = {n: args["m_" + n] for n in _WEIGHTS}
    vars_ = {n: args["v_" + n] for n in _WEIGHTS}

    small = {n: wts[n].astype(F32) for n in _WEIGHTS if n not in _BIG_NAMES}
    exchange = _Exchange(wts)
    dx, small_grads = _local_step(x[0], loss_target[0], small, exchange)

    out_g, out_d, out_m, out_v = {}, {}, {}, {}

    def update(tags, reduced, after):
        for t in tags:
            for n in _group_names(_GROUPS[t]):
                shape = wts[n].shape
                w2 = wts[n].reshape(shape[-2], shape[-1])
                g2 = reduced[n]
                d, m2, v2 = _adamw(w2, g2, moms[n].reshape(w2.shape), vars_[n].reshape(w2.shape), "adamw_" + n, after)
                out_g[n], out_d[n], out_m[n], out_v[n] = (g2.reshape(shape), d.reshape(shape), m2.reshape(shape),
                                                          v2.reshape(shape))
                after = v2
        return after

    done = update(("ffn2", "mixer"), exchange.finish(("ffn2", "mixer"), after=dx), exchange.token)
    update(("ffn1",), exchange.finish(("ffn1",), after=done), None)

    small_names = [n for n, _ in _SMALL]
    zero = jnp.zeros((1,), F32)
    shapes = {n: (wts[n].shape if n != "loss" else (1,)) for n in small_names}
    sums, sd, sm_, sv = _small_sync(
        _pack_small(small_grads),
        _pack_small({n: (wts[n] if n != "loss" else zero) for n in small_names}),
        _pack_small({n: (moms[n] if n != "loss" else zero) for n in small_names}),
        _pack_small({n: (vars_[n] if n != "loss" else zero) for n in small_names}))
    sg_u, sd_u, sm_u, sv_u = (_unpack_small(p, shapes) for p in (sums, sd, sm_, sv))
    for n in small_names:
        if n != "loss":
            out_g[n], out_d[n], out_m[n], out_v[n] = sg_u[n], sd_u[n], sm_u[n], sv_u[n]
    loss = sg_u["loss"].reshape(())

    return (loss, dx[None], *[out_g[n] for n in _WEIGHTS], *[out_d[n] for n in _WEIGHTS],
            *[out_m[n] for n in _WEIGHTS], *[out_v[n] for n in _WEIGHTS])
```

```python
import numpy as np

import jax
import jax.numpy as jnp
from jax import lax
from jax.experimental import pallas as pl
from jax.experimental.pallas import tpu as pltpu

F32 = jnp.float32
BF16 = jnp.bfloat16

D_MODEL = 1024
D_FF = 2816
CHUNK = 64
HEAD = 128
HG_HEADS = 8
GD_HEADS = 16
HPS = 8
MM_TM = 1408
MM_TN = 1024
MM_TK = 2048
VMEM_LIMIT = 48 * 1024 * 1024
ROW_TILE = 512
EPS = 1e-6
CONV_K = 4
IN_NAMES = ("hq", "hf", "hi", "hg", "gq", "gk", "gv", "ga", "gb", "gz", "gate_h", "gate_g")
IN_SIZES = (1024, 1024, 1024, 1024, 1024, 1024, 2048, 16, 16, 2048, 1024, 1024)
IN_WIDTH = sum(IN_SIZES)

ADAM_LR = 0.001
ADAM_B1 = 0.9
ADAM_B2 = 0.999
ADAM_EPS = 1e-08
ADAM_WD = 0.01
ADAM_STEP = 10

MESH = pl.DeviceIdType.MESH
_ARB = "arbitrary"
_PAR = "parallel"


def _bf(x):
    return x.astype(BF16)


def _dot(a, b):
    return jnp.dot(_bf(a), _bf(b), preferred_element_type=F32)


def _dot_nt(a, b):
    return lax.dot_general(_bf(a), _bf(b), (((1,), (1,)), ((), ())), preferred_element_type=F32)


def _dot_tn(a, b):
    return lax.dot_general(_bf(a), _bf(b), (((0,), (0,)), ((), ())), preferred_element_type=F32)


def _sigmoid(x):
    return jax.nn.sigmoid(x)


def _silu(x):
    return x * _sigmoid(x)


def _dsilu(x):
    s = _sigmoid(x)
    return s * (1.0 + x * (1.0 - s))


def _softplus(x):
    return jnp.maximum(x, 0.0) + jnp.log(1.0 + jnp.exp(-jnp.abs(x)))


def _rowsum(x):
    return jnp.sum(x, axis=1, keepdims=True)


def _col_to_row(col, eye):
    return jnp.sum(eye * col, axis=0, keepdims=True)


def _row_to_col(row, eye):
    return jnp.sum(eye * row, axis=1, keepdims=True)


def _pick(dim, pref, unit=128):
    if dim <= pref:
        return dim
    t = pref
    while t >= unit:
        if dim % t == 0:
            return t
        t -= unit
    return dim


def _params(*sem):
    return pltpu.CompilerParams(dimension_semantics=tuple(sem), vmem_limit_bytes=VMEM_LIMIT)


def _mm(a, b, *, ta=False, tb=False, alpha=1.0, res=None, out_dtype=F32, name="mm", b_from=0, tm_max=MM_TM):
    m = a.shape[1] if ta else a.shape[0]
    k = a.shape[0] if ta else a.shape[1]
    n = b.shape[0] if tb else b.shape[1]
    assert b_from + k <= (b.shape[1] if tb else b.shape[0])
    tm, tn, tk = _pick(m, tm_max), _pick(n, MM_TN), _pick(k, MM_TK)
    if tn < MM_TN < n and n % MM_TM == 0:
        tn = MM_TM
    nk = k // tk
    assert b_from % tk == 0
    b0 = b_from // tk
    a_spec = pl.BlockSpec((tk, tm), lambda i, j, l: (l, i)) if ta else pl.BlockSpec((tm, tk), lambda i, j, l: (i, l))
    b_spec = (pl.BlockSpec((tn, tk), lambda i, j, l: (j, b0 + l)) if tb
              else pl.BlockSpec((tk, tn), lambda i, j, l: (b0 + l, j)))
    o_spec = pl.BlockSpec((tm, tn), lambda i, j, l: (i, j))
    dims = (((0 if ta else 1,), (1 if tb else 0,)), ((), ()))
    has_res = res is not None

    def finish(r, r_ref, o_ref):
        if alpha != 1.0:
            r = r * alpha
        if has_res:
            r = r + r_ref[...]
        o_ref[...] = r.astype(out_dtype)

    def body(*refs):
        a_ref, b_ref = refs[0], refs[1]
        r_ref = refs[2] if has_res else None
        o_ref = refs[3] if has_res else refs[2]
        part = lax.dot_general(_bf(a_ref[...]), _bf(b_ref[...]), dims, preferred_element_type=F32)
        if nk == 1:
            finish(part, r_ref, o_ref)
            return
        acc = refs[-1]
        step = pl.program_id(2)

        @pl.when(step == 0)
        def _():
            acc[...] = part

        @pl.when(step != 0)
        def _():
            acc[...] += part

        @pl.when(step == nk - 1)
        def _():
            finish(acc[...], r_ref, o_ref)

    ins = [a, b] + ([res] if has_res else [])
    in_specs = [a_spec, b_spec] + ([o_spec] if has_res else [])
    return pl.pallas_call(
        body, name=name, grid=(m // tm, n // tn, nk), in_specs=in_specs, out_specs=o_spec,
        out_shape=jax.ShapeDtypeStruct((m, n), out_dtype),
        scratch_shapes=[pltpu.VMEM((tm, tn), F32)] if nk > 1 else [],
        compiler_params=_params(_PAR, _PAR, _ARB))(*ins)


def _row_spec(tr, w):
    return pl.BlockSpec((tr, w), lambda i: (i, 0))


def _full_spec(shape):
    return pl.BlockSpec(shape, lambda i: tuple(0 for _ in shape))


def _view(arr, off, width):
    return arr, off, width


def _view_rows(view, tr):
    _, off, width = view
    assert off % width == 0
    return pl.BlockSpec((tr, width), lambda i: (i, off // width))


def _view_tile(view, rows, bw, cidx=lambda c: c):
    _, off, width = view
    assert off % bw == 0 and width % bw == 0
    return pl.BlockSpec((rows, bw), lambda c, g: (cidx(c), off // bw + g))


def _rmsnorm_fwd(x, g, name):
    t, d = x.shape
    tr = _pick(t, ROW_TILE, 8)

    def body(x_ref, g_ref, o_ref):
        xv = x_ref[...]
        r = lax.rsqrt(jnp.mean(xv * xv, axis=1, keepdims=True) + EPS)
        o_ref[...] = (xv * r * g_ref[...]).astype(BF16)

    return pl.pallas_call(
        body, name=name, grid=(t // tr,), in_specs=[_row_spec(tr, d), _full_spec((1, d))],
        out_specs=_row_spec(tr, d), out_shape=jax.ShapeDtypeStruct((t, d), BF16),
        compiler_params=_params(_PAR))(x, g)


def _rmsnorm_bwd(x, g, dn, res, name):
    t, d = x.shape
    tr = _pick(t, ROW_TILE, 8)

    def body(x_ref, g_ref, dn_ref, r_ref, dx_ref, dxb_ref, dg_ref):
        @pl.when(pl.program_id(0) == 0)
        def _():
            dg_ref[...] = jnp.zeros_like(dg_ref)

        xv = x_ref[...]
        r = lax.rsqrt(jnp.mean(xv * xv, axis=1, keepdims=True) + EPS)
        xh = xv * r
        dy = dn_ref[...]
        dg_ref[...] += jnp.sum(dy * xh, axis=0, keepdims=True)
        dxh = dy * g_ref[...]
        dx = r_ref[...] + r * (dxh - xh * jnp.mean(dxh * xh, axis=1, keepdims=True))
        dx_ref[...] = dx
        dxb_ref[...] = dx.astype(BF16)

    return pl.pallas_call(
        body, name=name, grid=(t // tr,),
        in_specs=[_row_spec(tr, d), _full_spec((1, d)), _row_spec(tr, d), _row_spec(tr, d)],
        out_specs=[_row_spec(tr, d), _row_spec(tr, d), _full_spec((1, d))],
        out_shape=[jax.ShapeDtypeStruct((t, d), F32), jax.ShapeDtypeStruct((t, d), BF16),
                   jax.ShapeDtypeStruct((1, d), F32)],
        compiler_params=_params(_ARB))(x, g, dn, res)


FFN_TN = 1408
FFN_TM = 512


def _ffn_in_act(n, w_in, name):
    t, d = n.shape
    tm = _pick(t, FFN_TM)
    nf = D_FF // FFN_TN

    def body(n_ref, wa_ref, wb_ref, a_ref, b_ref, hm_ref):
        nv = n_ref[...]
        a = jnp.dot(nv, wa_ref[...], preferred_element_type=F32)
        b = jnp.dot(nv, wb_ref[...], preferred_element_type=F32)
        a_ref[...] = a.astype(BF16)
        b_ref[...] = b.astype(BF16)
        hm_ref[...] = (_silu(a) * b).astype(BF16)

    tile = pl.BlockSpec((tm, FFN_TN), lambda i, j: (i, j))
    return pl.pallas_call(
        body, name=name, grid=(t // tm, nf),
        in_specs=[pl.BlockSpec((tm, d), lambda i, j: (i, 0)), pl.BlockSpec((d, FFN_TN), lambda i, j: (0, j)),
                  pl.BlockSpec((d, FFN_TN), lambda i, j: (0, nf + j))],
        out_specs=[tile, tile, tile], out_shape=[jax.ShapeDtypeStruct((t, D_FF), BF16)] * 3,
        compiler_params=_params(_PAR, _PAR))(n, w_in, w_in)


def _ffn_dact(dout, w_out, a, b, name):
    t, d = dout.shape
    tm = _pick(t, FFN_TM)

    def body(do_ref, w_ref, a_ref, b_ref, da_ref, db_ref, hm_ref):
        dh = 0.5 * _dot_nt(do_ref[...], w_ref[...])
        av = a_ref[...].astype(F32)
        bv = b_ref[...].astype(F32)
        sg = _sigmoid(av)
        sa = av * sg
        da_ref[...] = (dh * bv * (sg * (1.0 + av * (1.0 - sg)))).astype(BF16)
        db_ref[...] = (dh * sa).astype(BF16)
        hm_ref[...] = (sa * bv).astype(BF16)

    tile = pl.BlockSpec((tm, FFN_TN), lambda i, j: (i, j))
    return pl.pallas_call(
        body, name=name, grid=(t // tm, D_FF // FFN_TN),
        in_specs=[pl.BlockSpec((tm, d), lambda i, j: (i, 0)), pl.BlockSpec((FFN_TN, d), lambda i, j: (j, 0)), tile, tile],
        out_specs=[tile, tile, tile], out_shape=[jax.ShapeDtypeStruct((t, D_FF), BF16)] * 3,
        compiler_params=_params(_PAR, _PAR))(dout, w_out, a, b)


def _merge_fwd(yh, yg, gh, gg):
    t, d = yh.shape
    tr = _pick(t, ROW_TILE, 8)

    def body(yh_ref, yg_ref, gh_ref, gg_ref, o_ref):
        o_ref[...] = (_sigmoid(gh_ref[...]) * yh_ref[...] + _sigmoid(gg_ref[...]) * yg_ref[...]).astype(BF16)

    return pl.pallas_call(
        body, name="merge_fwd", grid=(t // tr,),
        in_specs=[_row_spec(tr, d), _row_spec(tr, d), _view_rows(gh, tr), _view_rows(gg, tr)],
        out_specs=_row_spec(tr, d),
        out_shape=jax.ShapeDtypeStruct((t, d), BF16), compiler_params=_params(_PAR))(yh, yg, gh[0], gg[0])


def _into(dproj, off, width):
    return dproj, off, width


def _merge_bwd(dy, yh, yg, gh, gg, into):
    t, d = yh.shape
    tr = _pick(t, ROW_TILE, 8)
    dproj, off, width = into
    assert width == 2 * d and off % width == 0

    def body(dy_ref, yh_ref, yg_ref, gh_ref, gg_ref, _, dyh_ref, dyg_ref, dg_ref):
        dyv = dy_ref[...]
        sh = _sigmoid(gh_ref[...])
        sg = _sigmoid(gg_ref[...])
        dyh_ref[...] = (dyv * sh).astype(BF16)
        dyg_ref[...] = (dyv * sg).astype(BF16)
        dg_ref[:, :d] = (dyv * yh_ref[...] * sh * (1.0 - sh)).astype(BF16)
        dg_ref[:, d:] = (dyv * yg_ref[...] * sg * (1.0 - sg)).astype(BF16)

    return pl.pallas_call(
        body, name="merge_bwd", grid=(t // tr,),
        in_specs=[_row_spec(tr, d)] * 3 + [_view_rows(gh, tr), _view_rows(gg, tr), _ANY],
        out_specs=[_row_spec(tr, d)] * 2 + [pl.BlockSpec((tr, width), lambda i: (i, off // width))],
        out_shape=[jax.ShapeDtypeStruct((t, d), BF16)] * 2 + [jax.ShapeDtypeStruct(dproj.shape, dproj.dtype)],
        input_output_aliases={5: 2},
        compiler_params=_params(_PAR))(dy, yh, yg, gh[0], gg[0], dproj)


def _final_loss(h, g, tgt):
    t, d = h.shape
    tr = _pick(t, ROW_TILE, 8)

    def body(h_ref, g_ref, t_ref, loss_ref, dh_ref, dhb_ref, dg_ref):
        @pl.when(pl.program_id(0) == 0)
        def _():
            dg_ref[...] = jnp.zeros_like(dg_ref)
            loss_ref[...] = jnp.zeros_like(loss_ref)

        xv = h_ref[...]
        gv = g_ref[...]
        r = lax.rsqrt(jnp.mean(xv * xv, axis=1, keepdims=True) + EPS)
        xh = xv * r
        err = xh * gv - t_ref[...]
        loss_ref[...] += 0.5 * jnp.sum(jnp.mean(err * err, axis=1, keepdims=True), axis=0, keepdims=True)
        dy = err * (1.0 / d)
        dg_ref[...] += jnp.sum(dy * xh, axis=0, keepdims=True)
        dxh = dy * gv
        dh = r * (dxh - xh * jnp.mean(dxh * xh, axis=1, keepdims=True))
        dh_ref[...] = dh
        dhb_ref[...] = dh.astype(BF16)

    return pl.pallas_call(
        body, name="final_loss", grid=(t // tr,),
        in_specs=[_row_spec(tr, d), _full_spec((1, d)), _row_spec(tr, d)],
        out_specs=[_full_spec((1, 128)), _row_spec(tr, d), _row_spec(tr, d), _full_spec((1, d))],
        out_shape=[jax.ShapeDtypeStruct((1, 128), F32), jax.ShapeDtypeStruct((t, d), F32),
                   jax.ShapeDtypeStruct((t, d), BF16), jax.ShapeDtypeStruct((1, d), F32)],
        compiler_params=_params(_ARB))(h, g, tgt)


def _hg_consts():
    c = CHUNK
    t = np.arange(c)
    mats, masks = [], []
    for lvl in range(6):
        m = 1 << lvl
        blk = t // m
        mat = np.zeros((c, c), np.float32)
        for tt in range(c):
            b = blk[tt]
            if b % 2 == 1:
                mat[tt, b * m:tt + 1] = 1.0
            else:
                mat[tt, tt + 1:(b + 1) * m] = 1.0
        mats.append(mat)
        same = (t[:, None] // (2 * m)) == (t[None, :] // (2 * m))
        masks.append((same & (blk[:, None] % 2 == 1) & (blk[None, :] % 2 == 0)).astype(np.float32))
    pre = np.tril(np.ones((c, c), np.float32))
    suf = np.triu(np.ones((c, c), np.float32), 1)
    mstack = np.concatenate(mats + [pre, suf], 0)
    masks.append(np.eye(c, dtype=np.float32))
    return (jnp.asarray(mstack, BF16), jnp.asarray(mstack.T.copy(), BF16), jnp.asarray(np.stack(masks), F32),
            jnp.asarray(np.eye(HEAD, dtype=np.float32)))


def _gd_consts():
    c = CHUNK
    incl = np.tril(np.ones((c, c), np.float32))
    strict = np.tril(np.ones((c, c), np.float32), -1)
    eye = np.eye(c, dtype=np.float32)
    masks = np.stack([incl, strict, eye, incl.T.copy()])
    return jnp.asarray(incl, BF16), jnp.asarray(incl.T.copy(), BF16), jnp.asarray(masks, F32)


def _chunks_per_step(nc):
    for cb in (32 // HPS, 2, 1):
        if nc % cb == 0:
            return cb
    return 1


def _hg_prep(hq, hf, lg):
    lb = _sigmoid(lg[0:1, :] - lg[1:2, :])
    sg = _sigmoid(hf)
    sgn = _sigmoid(-hf)
    f = lb + (1.0 - lb) * sg
    lf = jnp.log(f)
    kk = (1.0 - lb) * sgn
    q = _silu(hq) * (HEAD ** -0.5)
    return lb, sg, sgn, f, lf, kk, q


def _mx_each(m, xs):
    hi, lo = _split2_each(xs)
    prods = [jnp.dot(m, jnp.concatenate([h, l], axis=1), preferred_element_type=F32) for h, l in zip(hi, lo)]
    return [p[:, :HEAD] + p[:, HEAD:] for p in prods]


def _hg_scaled(x, ex):
    xb = [_bf(a) for a in x]
    eb = [_bf(e[:6 * CHUNK]) for e in ex]
    return [[a * e[lvl * CHUNK:(lvl + 1) * CHUNK] for lvl in range(6)] for a, e in zip(xb, eb)]


def _hg_scores(q, kk, qe, ke, mask_ref):
    p = [mask_ref[6] * _rowsum(a * b) for a, b in zip(q, kk)]
    for lvl in range(6):
        d = [_dot_nt(a[lvl], b[lvl]) for a, b in zip(qe, ke)]
        p = [x + mask_ref[lvl] * y for x, y in zip(p, d)]
    return p


def _hgrn_fwd(hq, hf, hi, hg, logits, gain, consts):
    t = hq[0].shape[0]
    nc = t // CHUNK
    cb = _chunks_per_step(nc)
    rows = cb * CHUNK
    mstack, _, masks, eye = consts
    tile = pl.BlockSpec((rows, HPS * HEAD), lambda c, g: (c, g))

    def body(hq_ref, hf_ref, hi_ref, hg_ref, lg_ref, gain_ref, m_ref, mask_ref, eye_ref,
             oraw_ref, og_ref, ssave_ref, state):
        c = pl.program_id(0)
        g = pl.program_id(1)

        @pl.when(c == 0)
        def _():
            for hh in range(HPS):
                state[g * HPS + hh] = jnp.zeros((HEAD, HEAD), F32)

        lg_all = lg_ref[...]
        gain_v = gain_ref[...]

        def one(i, carry):
            sl = pl.ds(pl.multiple_of(i * CHUNK, CHUNK), CHUNK)
            hs = range(HPS)
            heads = [g * HPS + hh for hh in hs]
            ln = [slice(hh * HEAD, (hh + 1) * HEAD) for hh in hs]
            preps = [_hg_prep(hq_ref[sl, s], hf_ref[sl, s], lg_all[:, s]) for s in ln]
            lf, kk, q = [p[4] for p in preps], [p[5] for p in preps], [p[6] for p in preps]
            v = [hi_ref[sl, s] for s in ln]
            ex = [jnp.exp(x) for x in _mx_each(m_ref[...], lf)]
            eb = [e[6 * CHUNK:7 * CHUNK] for e in ex]
            esfx = [e[7 * CHUNK:8 * CHUNK] for e in ex]
            qe, ke = _hg_scaled(q, ex), _hg_scaled(kk, ex)
            p = _hg_scores(q, kk, qe, ke, mask_ref)
            s0 = [state[h] for h in heads]
            o = _each(lambda a, e, s, pp, vv: _dot(a * e, s) + _dot(pp, vv), q, eb, s0, p, v)
            eye_v = eye_ref[...]
            s1 = _each(lambda s, e, kx, ef, vv: s * _row_to_col(e[CHUNK - 1:CHUNK, :], eye_v) + _dot_tn(kx * ef, vv),
                       s0, eb, kk, esfx, v)
            for hh in hs:
                ssave_ref[i, hh] = s0[hh]
                state[heads[hh]] = s1[hh]
                oraw_ref[sl, ln[hh]] = o[hh]
                r = lax.rsqrt(jnp.mean(o[hh] * o[hh], axis=1, keepdims=True) + EPS)
                og_ref[sl, ln[hh]] = (o[hh] * r * gain_v * _silu(hg_ref[sl, ln[hh]])).astype(BF16)
            return carry

        lax.fori_loop(0, cb, one, 0, unroll=4)

    return pl.pallas_call(
        body, name="hgrn_fwd", grid=(nc // cb, HG_HEADS // HPS),
        in_specs=[_view_tile(v, rows, HPS * HEAD) for v in (hq, hf, hi, hg)] + [
                  pl.BlockSpec((2, HPS * HEAD), lambda c, g: (0, g)),
                  pl.BlockSpec((1, HEAD), lambda c, g: (0, 0)),
                  pl.BlockSpec(mstack.shape, lambda c, g: (0, 0)),
                  pl.BlockSpec(masks.shape, lambda c, g: (0, 0, 0)),
                  pl.BlockSpec(eye.shape, lambda c, g: (0, 0))],
        out_specs=[tile, tile, pl.BlockSpec((cb, HPS, HEAD, HEAD), lambda c, g: (c, g, 0, 0))],
        out_shape=[jax.ShapeDtypeStruct((t, HG_HEADS * HEAD), F32), jax.ShapeDtypeStruct((t, HG_HEADS * HEAD), BF16),
                   jax.ShapeDtypeStruct((nc, HG_HEADS, HEAD, HEAD), F32)],
        scratch_shapes=[pltpu.VMEM((HG_HEADS, HEAD, HEAD), F32)],
        compiler_params=_params(_ARB, _ARB))(hq[0], hf[0], hi[0], hg[0], logits, gain, mstack, masks, eye)


def _hgrn_bwd(hq, hf, hi, hg, logits, gain, oraw, ssave, dog, consts, into):
    t = hq[0].shape[0]
    dproj, off, width = into
    seg = HG_HEADS * HEAD
    assert HPS == HG_HEADS and width == 4 * seg and off % width == 0
    nc = t // CHUNK
    cb = _chunks_per_step(nc)
    rows = cb * CHUNK
    nb = nc // cb
    mstack, mstack_t, masks, eye = consts
    tile = pl.BlockSpec((rows, HPS * HEAD), lambda c, g: (nb - 1 - c, g))

    def body(hq_ref, hf_ref, hi_ref, hg_ref, lg_ref, gain_ref, oraw_ref, ssave_ref, dog_ref, m_ref, mt_ref,
             mask_ref, eye_ref, _, d_ref, dgain_ref, dlb_ref, dstate):
        c = pl.program_id(0)
        g = pl.program_id(1)

        @pl.when(c == 0)
        def _():
            for hh in range(HPS):
                dstate[g * HPS + hh] = jnp.zeros((HEAD, HEAD), F32)

        @pl.when((c == 0) & (g == 0))
        def _():
            dgain_ref[...] = jnp.zeros_like(dgain_ref)
            dlb_ref[...] = jnp.zeros_like(dlb_ref)

        lg_all = lg_ref[...]
        gain_v = gain_ref[...]
        eye_v = eye_ref[...]
        last_row = (lax.broadcasted_iota(jnp.int32, (CHUNK, HEAD), 0) == CHUNK - 1).astype(F32)

        def one(j, carry):
            i = cb - 1 - j
            sl = pl.ds(pl.multiple_of(i * CHUNK, CHUNK), CHUNK)
            hs = range(HPS)
            heads = [g * HPS + hh for hh in hs]
            ln = [slice(hh * HEAD, (hh + 1) * HEAD) for hh in hs]
            hqv = [hq_ref[sl, s] for s in ln]
            hgv = [hg_ref[sl, s] for s in ln]
            preps = [_hg_prep(a, hf_ref[sl, s], lg_all[:, s]) for a, s in zip(hqv, ln)]
            lb, sg, sgn, f, lf, kk, q = ([p[n] for p in preps] for n in range(7))
            v = [hi_ref[sl, s] for s in ln]
            ex = [jnp.exp(x) for x in _mx_each(m_ref[...], lf)]
            eb = [e[6 * CHUNK:7 * CHUNK] for e in ex]
            esfx = [e[7 * CHUNK:8 * CHUNK] for e in ex]
            qe, ke = _hg_scaled(q, ex), _hg_scaled(kk, ex)
            p = _hg_scores(q, kk, qe, ke, mask_ref)
            s0 = [ssave_ref[i, hh] for hh in hs]
            ds = [dstate[h] for h in heads]

            o = [oraw_ref[sl, s] for s in ln]
            r = [lax.rsqrt(jnp.mean(x * x, axis=1, keepdims=True) + EPS) for x in o]
            on = _each(lambda x, y: x * y, o, r)
            dg_out = [dog_ref[sl, s] for s in ln]
            sgate = [_silu(x) for x in hgv]
            for hh in hs:
                d_ref[sl, slice(3 * seg + hh * HEAD, 3 * seg + (hh + 1) * HEAD)] =(dg_out[hh] * on[hh] * gain_v * _dsilu(hgv[hh])).astype(BF16)
            dgain_ref[...] += sum(jnp.sum(d * s * n, axis=0, keepdims=True) for d, s, n in zip(dg_out, sgate, on))
            don = _each(lambda d, s: d * s * gain_v, dg_out, sgate)
            do = _each(lambda rr, dn, n: rr * (dn - n * jnp.mean(dn * n, axis=1, keepdims=True)), r, don, on)

            dp = _each(_dot_nt, do, v)
            dv = _each(lambda pp, d, kx, ef, s: _dot_tn(pp, d) + _dot(kx * ef, s), p, do, kk, esfx, ds)
            dqb = _each(_dot_nt, do, s0)
            dkx = _each(_dot_nt, v, ds)
            diag = [_rowsum(mask_ref[6] * x) for x in dp]
            dq = _each(lambda a, e, d, kx: a * e + d * kx, dqb, eb, diag, kk)
            dk = _each(lambda a, e, d, qq: a * e + d * qq, dkx, esfx, diag, q)
            dxs = [[] for _ in hs]
            for lvl in range(6):
                el = [e[lvl * CHUNK:(lvl + 1) * CHUNK] for e in ex]
                gm = [mask_ref[lvl] * x for x in dp]
                gm = [_bf(x) for x in gm]
                a1 = _each(lambda m_, kx: _dot(m_, kx[lvl]), gm, ke)
                a2 = _each(lambda m_, qq: _dot_tn(m_, qq[lvl]), gm, qe)
                dq = _each(lambda x, a, e: x + a * e, dq, a1, el)
                dk = _each(lambda x, a, e: x + a * e, dk, a2, el)
                for hh in hs:
                    dxs[hh].append((a1[hh] * q[hh] + a2[hh] * kk[hh]) * el[hh])
            e_end_row = [e[CHUNK - 1:CHUNK, :] for e in eb]
            ds_new = _each(lambda qq, e, d, er, s: _dot_tn(qq * e, d) + _row_to_col(er, eye_v) * s, q, eb, do, e_end_row, ds)
            for hh in hs:
                dstate[heads[hh]] = ds_new[hh]
                dend_row = _col_to_row(_rowsum(s0[hh] * ds[hh]), eye_v)
                dxs[hh].append(dqb[hh] * q[hh] * eb[hh] + last_row * (e_end_row[hh] * dend_row))
                dxs[hh].append(dkx[hh] * kk[hh] * esfx[hh])
            dlf = _mx_each(mt_ref[...], [jnp.concatenate(x, axis=0) for x in dxs])

            for hh in hs:
                d_ref[sl, slice(2 * seg + hh * HEAD, 2 * seg + (hh + 1) * HEAD)] =dv[hh].astype(BF16)
                d_ref[sl, ln[hh]] =(dq[hh] * (HEAD ** -0.5) * _dsilu(hqv[hh])).astype(BF16)
                df = dlf[hh] / f[hh]
                dsig = (1.0 - lb[hh]) * sg[hh] * sgn[hh]
                d_ref[sl, slice(seg + hh * HEAD, seg + (hh + 1) * HEAD)] =((df - dk[hh]) * dsig).astype(BF16)
                dlb_t = jnp.sum(df * sgn[hh] - dk[hh] * sgn[hh], axis=0, keepdims=True)
                dlb_ref[pl.ds(heads[hh], 1), :] += dlb_t * lb[hh] * (1.0 - lb[hh])
            return carry

        lax.fori_loop(0, cb, one, 0, unroll=2)

    outs = [jax.ShapeDtypeStruct(dproj.shape, dproj.dtype),
            jax.ShapeDtypeStruct((1, HEAD), F32), jax.ShapeDtypeStruct((HG_HEADS, HEAD), F32)]
    return pl.pallas_call(
        body, name="hgrn_bwd", grid=(nb, HG_HEADS // HPS),
        in_specs=[_view_tile(v, rows, HPS * HEAD, lambda c: nb - 1 - c) for v in (hq, hf, hi, hg)] + [
                  pl.BlockSpec((2, HPS * HEAD), lambda c, g: (0, g)),
                  pl.BlockSpec((1, HEAD), lambda c, g: (0, 0)), tile,
                  pl.BlockSpec((cb, HPS, HEAD, HEAD), lambda c, g: (nb - 1 - c, g, 0, 0)), tile,
                  pl.BlockSpec(mstack.shape, lambda c, h: (0, 0)),
                  pl.BlockSpec(mstack_t.shape, lambda c, h: (0, 0)),
                  pl.BlockSpec(masks.shape, lambda c, h: (0, 0, 0)),
                  pl.BlockSpec(eye.shape, lambda c, h: (0, 0)), _ANY],
        out_specs=[pl.BlockSpec((rows, width), lambda c, h: (nb - 1 - c, off // width)),
                   pl.BlockSpec((1, HEAD), lambda c, h: (0, 0)),
                   pl.BlockSpec((HG_HEADS, HEAD), lambda c, h: (0, 0))],
        out_shape=outs, scratch_shapes=[pltpu.VMEM((HG_HEADS, HEAD, HEAD), F32)], input_output_aliases={13: 0},
        compiler_params=_params(_ARB, _ARB))(hq[0], hf[0], hi[0], hg[0], logits, gain, oraw, ssave, dog, mstack,
                                             mstack_t, masks, eye, dproj)


CONV_W = 512
CONV_ROWS = 1024


def _per_head(fn, *arrs):
    width = arrs[0].shape[1]
    return jnp.concatenate([fn(*[a[:, j:j + HEAD] for a in arrs]) for j in range(0, width, HEAD)], axis=1)


def _shift_down(xv, halo, d, top_rows):
    if d == 0:
        return xv, xv[0:8]
    main = pltpu.roll(xv, d, 0)
    top = jnp.where(top_rows < d, pltpu.roll(halo, d, 0), main[0:8])
    return main, top


def _conv_parts(x_ref, halo_ref, w_ref, first):
    xv = x_ref[...]
    halo = jnp.where(first, 0.0, halo_ref[...])
    top_rows = lax.broadcasted_iota(jnp.int32, (8, xv.shape[1]), 0)
    shifted = [_shift_down(xv, halo, CONV_K - 1 - j, top_rows) for j in range(CONV_K)]
    w = w_ref[...]
    acc = sum(shifted[j][0] * w[j:j + 1, :] for j in range(CONV_K))
    acc_top = sum(shifted[j][1] * w[j:j + 1, :] for j in range(CONV_K))
    return shifted, acc, acc_top


def _conv_fwd(x, w8, l2scale, name):
    x, off, width = x
    t = x.shape[0]
    o = off // CONV_W
    tr = _pick(t, CONV_ROWS, 8)

    def post(cv):
        s = _silu(cv)
        if l2scale is not None:
            s = _per_head(lambda sh: sh * (lax.rsqrt(_rowsum(sh * sh) + EPS) * l2scale), s)
        return s

    def body(x_ref, halo_ref, w_ref, o_ref):
        _, acc, acc_top = _conv_parts(x_ref, halo_ref, w_ref, pl.program_id(1) == 0)
        o_ref[...] = post(acc)
        o_ref[0:8, :] = post(acc_top)

    return pl.pallas_call(
        body, name=name, grid=(width // CONV_W,t // tr),
        in_specs=[pl.BlockSpec((tr, CONV_W), lambda j, i: (i, o + j)),
                  pl.BlockSpec((8, CONV_W), lambda j, i: (jnp.maximum(i * (tr // 8) - 1, 0), o + j)),
                  pl.BlockSpec((8, CONV_W), lambda j, i: (0, j))],
        out_specs=pl.BlockSpec((tr, CONV_W), lambda j, i: (i, j)),
        out_shape=jax.ShapeDtypeStruct((t, width), F32), compiler_params=_params(_PAR, _PAR))(x, x, w8)


def _conv_bwd_a(x, w8, dy, l2scale, name):
    x, off, width = x
    t = x.shape[0]
    o = off // CONV_W
    tr = _pick(t, CONV_ROWS, 8)

    def l2_bwd(s, dyh):
        r = lax.rsqrt(_rowsum(s * s) + EPS)
        y0 = s * r
        dy0 = dyh * l2scale
        return r * (dy0 - y0 * _rowsum(dy0 * y0))

    def to_dc(cv, dyv):
        if l2scale is not None:
            dyv = _per_head(l2_bwd, _silu(cv), dyv)
        return dyv * _dsilu(cv)

    def body(x_ref, halo_ref, w_ref, dy_ref, dc_ref, dw_ref):
        @pl.when(pl.program_id(1) == 0)
        def _():
            dw_ref[...] = jnp.zeros_like(dw_ref)

        shifted, acc, acc_top = _conv_parts(x_ref, halo_ref, w_ref, pl.program_id(1) == 0)
        dyv = dy_ref[...]
        dc = to_dc(acc, dyv)
        dc_top = to_dc(acc_top, dyv[0:8])
        dc_ref[...] = dc
        dc_ref[0:8, :] = dc_top
        rest = (lax.broadcasted_iota(jnp.int32, dc.shape, 0) >= 8).astype(F32)
        dc_rest = dc * rest
        for j in range(CONV_K):
            dw_ref[j:j + 1, :] += (jnp.sum(dc_rest * shifted[j][0], axis=0, keepdims=True)
                                   + jnp.sum(dc_top * shifted[j][1], axis=0, keepdims=True))

    return pl.pallas_call(
        body, name=name, grid=(width // CONV_W,t // tr),
        in_specs=[pl.BlockSpec((tr, CONV_W), lambda j, i: (i, o + j)),
                  pl.BlockSpec((8, CONV_W), lambda j, i: (jnp.maximum(i * (tr // 8) - 1, 0), o + j)),
                  pl.BlockSpec((8, CONV_W), lambda j, i: (0, j)),
                  pl.BlockSpec((tr, CONV_W), lambda j, i: (i, j))],
        out_specs=[pl.BlockSpec((tr, CONV_W), lambda j, i: (i, j)), pl.BlockSpec((8, CONV_W), lambda j, i: (0, j))],
        out_shape=[jax.ShapeDtypeStruct((t, width), F32), jax.ShapeDtypeStruct((8, width), F32)],
        compiler_params=_params(_PAR, _ARB))(x, x, w8, dy)


def _conv_bwd_b(dc, w8, name, into):
    t, width = dc.shape
    tr = _pick(t, CONV_ROWS, 8)
    nt = t // tr

    dproj, off, into_width = into
    assert into_width == width and off % CONV_W == 0
    o = off // CONV_W

    def body(dc_ref, halo_ref, w_ref, _, dx_ref):
        dcv = dc_ref[...]
        halo = jnp.where(pl.program_id(1) == nt - 1, 0.0, halo_ref[...])
        w = w_ref[...]
        bot_rows = lax.broadcasted_iota(jnp.int32, (8, CONV_W), 0)
        acc = dcv * w[CONV_K - 1:CONV_K, :]
        acc_bot = dcv[tr - 8:tr] * w[CONV_K - 1:CONV_K, :]
        for d in range(1, CONV_K):
            main = pltpu.roll(dcv, tr - d, 0)
            bot = jnp.where(bot_rows >= 8 - d, pltpu.roll(halo, 8 - d, 0), main[tr - 8:tr])
            wj = w[CONV_K - 1 - d:CONV_K - d, :]
            acc = acc + main * wj
            acc_bot = acc_bot + bot * wj
        dx_ref[...] = acc.astype(BF16)
        dx_ref[tr - 16:tr, :] = jnp.concatenate([acc[tr - 16:tr - 8], acc_bot], axis=0).astype(BF16)

    return pl.pallas_call(
        body, name=name, grid=(width // CONV_W,nt),
        in_specs=[pl.BlockSpec((tr, CONV_W), lambda j, i: (i, j)),
                  pl.BlockSpec((8, CONV_W), lambda j, i: (jnp.minimum((i + 1) * (tr // 8), t // 8 - 1), j)),
                  pl.BlockSpec((8, CONV_W), lambda j, i: (0, j)), _ANY],
        out_specs=pl.BlockSpec((tr, CONV_W), lambda j, i: (i, o + j)),
        out_shape=jax.ShapeDtypeStruct(dproj.shape, dproj.dtype), input_output_aliases={3: 0},
        compiler_params=_params(_PAR, _PAR))(dc, dc, w8, dproj)


def _each(f, *lists):
    return [f(*xs) for xs in zip(*lists)]


def _split2_each(xs):
    hi = [_bf(x) for x in xs]
    lo = [_bf(x - h.astype(F32)) for x, h in zip(xs, hi)]
    return hi, lo


def _hp_each(a_split, b_split):
    (ah, al), (bh, bl) = a_split, b_split
    rows = ah[0].shape[0]
    d12 = [jnp.dot(jnp.concatenate([x, y], axis=0), z, preferred_element_type=F32) for x, y, z in zip(ah, al, bh)]
    d3 = [jnp.dot(x, y, preferred_element_type=F32) for x, y in zip(ah, bl)]
    return [d[:rows] + d[rows:] + e for d, e in zip(d12, d3)]


INV_EXACT_STEPS = 1


def _tri_inv_each(a_list, eye):
    ns = [-a for a in a_list]
    ps = [eye + n for n in ns]
    n_split = _split2_each(ns)
    for step in range(5):
        if step < INV_EXACT_STEPS:
            ns = _hp_each(n_split, n_split)
            n_split = _split2_each(ns)
            ps = [p + d for p, d in zip(ps, _hp_each(_split2_each(ps), n_split))]
        else:
            nb = n_split[0] if step == INV_EXACT_STEPS else [_bf(n) for n in ns]
            ns = [jnp.dot(x, x, preferred_element_type=F32) for x in nb]
            nb2 = [_bf(n) for n in ns]
            ps = [p + jnp.dot(_bf(p), y, preferred_element_type=F32) for p, y in zip(ps, nb2)]
    return ps


def _gd_gates(gab, alog, dtb):
    sp_arg = gab + dtb
    return sp_arg, -jnp.exp(alog) * _softplus(sp_arg), _sigmoid(gab)


def _pick_lane(tile, base, head):
    g, hh = head
    col = tile[:, base + hh:base + hh + 1]
    for gi in range(1, GD_HEADS // HPS):
        lane = base + gi * HPS + hh
        col = jnp.where(g == gi, tile[:, lane:lane + 1], col)
    return col


def _gd_chunks(q, k, v, g_all, beta_all, heads, l_ref, mask_ref, tm=None):
    incl, strict, eye, upper = mask_ref[0], mask_ref[1], mask_ref[2], mask_ref[3]
    lmat = l_ref[...]
    gb = [jnp.broadcast_to(_pick_lane(g_all, 0, s), (CHUNK, HEAD)) for s in heads]
    bb = [jnp.broadcast_to(_pick_lane(beta_all, GD_HEADS, s), (CHUNK, HEAD)) for s in heads]
    gam = _mx_each(lmat, gb)
    gam_row = [jnp.sum(x[:, :CHUNK] * upper, axis=0, keepdims=True) for x in gb]
    lm = _each(lambda gm, gr: incl * jnp.exp(jnp.minimum(gm[:, :CHUNK] - gr, 0.0)), gam, gam_row)
    kb = _each(lambda x, b: x * b, k, bb)
    a = _each(lambda x, y, m: strict * _dot_nt(x, y) * m, kb, k, lm)
    if tm is None:
        tm = _tri_inv_each(a, eye)
    eg = [jnp.exp(x) for x in gam]
    vb = _each(lambda x, b: x * b, v, bb)
    kbg = _each(lambda x, e: x * e, kb, eg)
    uw = _each(lambda t_, x, y: _dot(t_, jnp.concatenate([x, y], axis=1)), tm, vb, kbg)
    u = [x[:, :HEAD] for x in uw]
    w = [x[:, HEAD:] for x in uw]
    qk = _each(lambda x, y, m: _dot_nt(x, y) * m, q, k, lm)
    g_end = [x[CHUNK - 1:CHUNK, :] for x in gam]
    ekg = _each(lambda e, x: jnp.exp(e - x), g_end, gam)
    ge = [jnp.exp(e) for e in g_end]
    kg = _each(lambda x, e: x * e, k, ekg)
    qg = _each(lambda x, e: x * e, q, eg)
    names = ("bb", "lm", "kb", "a", "tm", "eg", "vb", "kbg", "u", "w", "qk", "ekg", "ge", "kg", "qg")
    cols = (bb, lm, kb, a, tm, eg, vb, kbg, u, w, qk, ekg, ge, kg, qg)
    return [dict(zip(names, vals)) for vals in zip(*cols)]


def _gd_specs(rows, rev_nb=None):
    def cidx(c):
        return c if rev_nb is None else rev_nb - 1 - c

    qk_tile = pl.BlockSpec((rows, HPS // 2 * HEAD), lambda c, g: (cidx(c), g))
    v_tile = pl.BlockSpec((rows, HPS * HEAD), lambda c, g: (cidx(c), g))
    gab_tile = pl.BlockSpec((rows, HEAD), lambda c, g: (cidx(c), 0))
    return qk_tile, v_tile, gab_tile


def _gdn_fwd(qn, kn, cv, gab, gz, alog, dtb, gain, consts):
    t = qn.shape[0]
    nc = t // CHUNK
    cb = _chunks_per_step(nc)
    rows = cb * CHUNK
    lmat, _, masks = consts
    qk_tile, v_tile, gab_tile = _gd_specs(rows)
    row128 = pl.BlockSpec((1, HEAD), lambda c, h: (0, 0))

    def body(q_ref, k_ref, v_ref, gab_ref, gz_ref, alog_ref, dtb_ref, gain_ref, l_ref, mask_ref,
             oraw_ref, og_ref, ssave_ref, tsave_ref, state):
        c = pl.program_id(0)
        g = pl.program_id(1)

        @pl.when(c == 0)
        def _():
            for hh in range(HPS):
                state[g * HPS + hh] = jnp.zeros((HEAD, HEAD), F32)

        alog = alog_ref[...]
        dtb = dtb_ref[...]
        gain_v = gain_ref[...]

        def one(i, carry):
            sl = pl.ds(pl.multiple_of(i * CHUNK, CHUNK), CHUNK)
            _, g_all, beta_all = _gd_gates(gab_ref[sl, :], alog, dtb)
            heads = [g * HPS + hh for hh in range(HPS)]
            lq = [slice(hh // 2 * HEAD, (hh // 2 + 1) * HEAD) for hh in range(HPS)]
            lv = [slice(hh * HEAD, (hh + 1) * HEAD) for hh in range(HPS)]
            chs = _gd_chunks([q_ref[sl, s] for s in lq], [k_ref[sl, s] for s in lq], [v_ref[sl, s] for s in lv],
                             g_all, beta_all, [(g, hh) for hh in range(HPS)], l_ref, mask_ref)
            s0 = [state[h] for h in heads]
            ws = _each(lambda ch, s: _dot(jnp.concatenate([ch["w"], ch["qg"]], axis=0), s), chs, s0)
            v_new = _each(lambda ch, x: ch["u"] - x[:CHUNK], chs, ws)
            o = _each(lambda ch, x, vn: x[CHUNK:] + _dot(ch["qk"], vn), chs, ws, v_new)
            s1 = _each(lambda ch, s, vn: s * ch["ge"] + _dot_tn(ch["kg"], vn), chs, s0, v_new)
            for hh in range(HPS):
                ssave_ref[i, hh] = s0[hh]
                tsave_ref[i, hh] = chs[hh]["tm"]
                state[heads[hh]] = s1[hh]
                oraw_ref[sl, lv[hh]] = o[hh]
                r = lax.rsqrt(jnp.mean(o[hh] * o[hh], axis=1, keepdims=True) + EPS)
                og_ref[sl, lv[hh]] = (o[hh] * r * gain_v * _silu(gz_ref[sl, lv[hh]])).astype(BF16)
            return carry

        lax.fori_loop(0, cb, one, 0, unroll=4)

    return pl.pallas_call(
        body, name="gdn_fwd", grid=(nc // cb, GD_HEADS // HPS),
        in_specs=[qk_tile, qk_tile, v_tile, gab_tile, _view_tile(gz, rows, HPS * HEAD), row128, row128, row128,
                  pl.BlockSpec(lmat.shape, lambda c, g: (0, 0)),
                  pl.BlockSpec(masks.shape, lambda c, g: (0, 0, 0))],
        out_specs=[v_tile, v_tile, pl.BlockSpec((cb, HPS, HEAD, HEAD), lambda c, g: (c, g, 0, 0)),
                   pl.BlockSpec((cb, HPS, CHUNK, CHUNK), lambda c, g: (c, g, 0, 0))],
        out_shape=[jax.ShapeDtypeStruct((t, GD_HEADS * HEAD), F32), jax.ShapeDtypeStruct((t, GD_HEADS * HEAD), BF16),
                   jax.ShapeDtypeStruct((nc, GD_HEADS, HEAD, HEAD), F32),
                   jax.ShapeDtypeStruct((nc, GD_HEADS, CHUNK, CHUNK), F32)],
        scratch_shapes=[pltpu.VMEM((GD_HEADS, HEAD, HEAD), F32)],
        compiler_params=_params(_ARB, _ARB))(qn, kn, cv, gab, gz[0], alog, dtb, gain, lmat, masks)


def _gdn_bwd(qn, kn, cv, gab, gz, alog, dtb, gain, oraw, ssave, tsave, dog, consts, into):
    t = qn.shape[0]
    dproj, off, width = into
    assert width == GD_HEADS * HEAD and off % (HPS * HEAD) == 0
    nc = t // CHUNK
    cb = _chunks_per_step(nc)
    rows = cb * CHUNK
    nb = nc // cb
    lmat, lmat_t, masks = consts
    qk_tile, v_tile, gab_tile = _gd_specs(rows, nb)
    row128 = pl.BlockSpec((1, HEAD), lambda c, h: (0, 0))

    def body(q_ref, k_ref, v_ref, gab_ref, gz_ref, alog_ref, dtb_ref, gain_ref, oraw_ref, ssave_ref, tsave_ref, dog_ref,
             l_ref, lt_ref, mask_ref, _,
             dq_ref, dk_ref, dv_ref, dgab_ref, dgz_ref, small_ref, dstate):
        c = pl.program_id(0)
        g = pl.program_id(1)

        @pl.when(c == 0)
        def _():
            for hh in range(HPS):
                dstate[g * HPS + hh] = jnp.zeros((HEAD, HEAD), F32)

        @pl.when((c == 0) & (g == 0))
        def _():
            small_ref[...] = jnp.zeros_like(small_ref)

        alog = alog_ref[...]
        dtb = dtb_ref[...]
        gain_v = gain_ref[...]
        lane = lax.broadcasted_iota(jnp.int32, (1, HEAD), 1)
        last_row = (lax.broadcasted_iota(jnp.int32, (CHUNK, HEAD), 0) == CHUNK - 1).astype(F32)

        def one(j, carry):
            i = cb - 1 - j
            sl = pl.ds(pl.multiple_of(i * CHUNK, CHUNK), CHUNK)
            sp_arg, g_all, beta_all = _gd_gates(gab_ref[sl, :], alog, dtb)
            strict, eye = mask_ref[1], mask_ref[2]
            ltm = lt_ref[...]
            hs = range(HPS)
            heads = [g * HPS + hh for hh in hs]
            lq = [slice(hh // 2 * HEAD, (hh // 2 + 1) * HEAD) for hh in hs]
            lv = [slice(hh * HEAD, (hh + 1) * HEAD) for hh in hs]
            q = [q_ref[sl, s] for s in lq]
            k = [k_ref[sl, s] for s in lq]
            v = [v_ref[sl, s] for s in lv]
            gzv = [gz_ref[sl, s] for s in lv]
            chs = _gd_chunks(q, k, v, g_all, beta_all, [(g, hh) for hh in hs], l_ref, mask_ref,
                             tm=[tsave_ref[i, hh] for hh in hs])

            def col(name):
                return [ch[name] for ch in chs]

            def mul(x, y):
                return x * y

            tm, lm, eg, bb = col("tm"), col("lm"), col("eg"), col("bb")
            s0 = [ssave_ref[i, hh] for hh in hs]
            ds = [dstate[h] for h in heads]
            v_new = _each(lambda u, w, s: u - _dot(w, s), col("u"), col("w"), s0)

            o = [oraw_ref[sl, s] for s in lv]
            r = [lax.rsqrt(jnp.mean(x * x, axis=1, keepdims=True) + EPS) for x in o]
            on = _each(mul, o, r)
            dg_out = [dog_ref[sl, s] for s in lv]
            sgate = [_silu(x) for x in gzv]
            for hh in hs:
                dgz_ref[sl, lv[hh]] = (dg_out[hh] * on[hh] * gain_v * _dsilu(gzv[hh])).astype(BF16)
            small_ref[0:1, :] += sum(jnp.sum(d * s * n, axis=0, keepdims=True) for d, s, n in zip(dg_out, sgate, on))
            don = _each(lambda d, s: d * s * gain_v, dg_out, sgate)
            do = _each(lambda rr, dn, n: rr * (dn - n * jnp.mean(dn * n, axis=1, keepdims=True)), r, don, on)

            dv_new = _each(lambda a, d, b, s: _dot_tn(a, d) + _dot(b, s), col("qk"), do, col("kg"), ds)
            dqk = _each(_dot_nt, do, v_new)
            dkg = _each(_dot_nt, v_new, ds)
            dge = _each(lambda s, d: jnp.sum(_rowsum(s * d), axis=0, keepdims=True), s0, ds)
            both = _each(lambda d, dv: jnp.concatenate([d, dv], axis=0), do, dv_new)
            from_s = _each(_dot_nt, both, s0)
            dqg = [x[:CHUNK] for x in from_s]
            dw = [-x[CHUNK:] for x in from_s]
            ds_new = _each(lambda qg, w, bo, ge, s: _dot_tn(jnp.concatenate([qg, -w], axis=0), bo) + ge * s,
                           col("qg"), col("w"), both, col("ge"), ds)
            for hh in hs:
                dstate[heads[hh]] = ds_new[hh]

            side = _each(lambda dv, d: jnp.concatenate([dv, d], axis=1), dv_new, dw)
            back = _each(_dot_tn, tm, side)
            dvb = [x[:, :HEAD] for x in back]
            dkbg = [x[:, HEAD:] for x in back]
            dtm = _each(lambda sd, vb, kbg: _dot_nt(sd, jnp.concatenate([vb, kbg], axis=1)), side, col("vb"), col("kbg"))
            dtt = _each(_dot_nt, dtm, tm)
            da = _each(lambda t_, x: -_dot_tn(t_, x) * strict, tm, dtt)
            dal = _each(mul, da, lm)
            dqk_l = _each(mul, dqk, lm)
            stack = _each(lambda x, y: jnp.concatenate([x, y], axis=0), dal, dqk_l)
            on_k = _each(_dot, stack, k)
            dkb = _each(lambda x, y, e: x[:CHUNK] + y * e, on_k, dkbg, eg)
            dq = _each(lambda x, y, e: x[CHUNK:] + y * e, on_k, dqg, eg)
            dk = _each(lambda st, kb, qq, z, ekg, w_, b: _dot_tn(st, jnp.concatenate([kb, qq], axis=0)) + z * ekg + w_ * b,
                       stack, col("kb"), q, dkg, col("ekg"), dkb, bb)
            gmat = _each(lambda x, a, y, qk: x * a + y * qk, da, col("a"), dqk, col("qk"))
            t_kg = _each(lambda x, y: _rowsum(x * y), dkg, col("kg"))
            dgam = _each(lambda gm, x, qg, t_, y, kbg: (_rowsum(gm) - _row_to_col(jnp.sum(gm, axis=0, keepdims=True), eye)
                                                        + _rowsum(x * qg) - t_ + _rowsum(y * kbg)),
                         gmat, dqg, col("qg"), t_kg, dkbg, col("kbg"))
            dg_end = _each(lambda t_, e, ge: jnp.sum(t_, axis=0, keepdims=True) + e * ge[:, 0:1], t_kg, dge, col("ge"))
            dgam = _each(lambda x, e: x + last_row * e, dgam, dg_end)
            dbeta = _each(lambda x, kk, y, vv: _rowsum(x * kk) + _rowsum(y * vv), dkb, k, dvb, v)
            dg = _mx_each(ltm, dgam)

            for hh in hs:
                dv_ref[sl, lv[hh]] = dvb[hh] * bb[hh]
            fac_g = -jnp.exp(alog) * _sigmoid(sp_arg)
            fac_b = beta_all * (1.0 - beta_all)
            hot_g = [(lane == h).astype(F32) for h in heads]
            hot_b = [(lane == GD_HEADS + h).astype(F32) for h in heads]
            dga = _each(lambda x, hot: x * hot * fac_g, dg, hot_g)
            dgb = _each(lambda x, hot: x * hot * fac_b, dbeta, hot_b)
            small_ref[1:2, :] += sum(jnp.sum(x, axis=0, keepdims=True) for x in dga)
            small_ref[2:3, :] += sum(jnp.sum(x * hot * g_all, axis=0, keepdims=True) for x, hot in zip(dg, hot_g))
            for pair in range(HPS // 2):
                lqp = slice(pair * HEAD, (pair + 1) * HEAD)
                dq_ref[sl, lqp] = dq[2 * pair] + dq[2 * pair + 1]
                dk_ref[sl, lqp] = dk[2 * pair] + dk[2 * pair + 1]
            dgab_ref[sl, :] = sum(a + b for a, b in zip(dga, dgb))
            return carry

        lax.fori_loop(0, cb, one, 0, unroll=4)

    groups = GD_HEADS // HPS
    outs = [jax.ShapeDtypeStruct((t, 1024), F32), jax.ShapeDtypeStruct((t, 1024), F32),
            jax.ShapeDtypeStruct((t, 2048), F32), jax.ShapeDtypeStruct((t, groups * HEAD), F32),
            jax.ShapeDtypeStruct(dproj.shape, dproj.dtype), jax.ShapeDtypeStruct((8, HEAD), F32)]
    dgz_tile = pl.BlockSpec((rows, HPS * HEAD), lambda c, g: (nb - 1 - c, off // (HPS * HEAD) + g))
    return pl.pallas_call(
        body, name="gdn_bwd", grid=(nb, groups),
        in_specs=[qk_tile, qk_tile, v_tile, gab_tile, _view_tile(gz, rows, HPS * HEAD, lambda c: nb - 1 - c),
                  row128, row128, row128, v_tile,
                  pl.BlockSpec((cb, HPS, HEAD, HEAD), lambda c, g: (nb - 1 - c, g, 0, 0)),
                  pl.BlockSpec((cb, HPS, CHUNK, CHUNK), lambda c, g: (nb - 1 - c, g, 0, 0)), v_tile,
                  pl.BlockSpec(lmat.shape, lambda c, g: (0, 0)),
                  pl.BlockSpec(lmat_t.shape, lambda c, g: (0, 0)),
                  pl.BlockSpec(masks.shape, lambda c, g: (0, 0, 0)), _ANY],
        out_specs=[qk_tile, qk_tile, v_tile, pl.BlockSpec((rows, HEAD), lambda c, g: (nb - 1 - c, g)), dgz_tile,
                   pl.BlockSpec((8, HEAD), lambda c, g: (0, 0))],
        out_shape=outs, scratch_shapes=[pltpu.VMEM((GD_HEADS, HEAD, HEAD), F32)], input_output_aliases={15: 4},
        compiler_params=_params(_ARB, _ARB))(qn, kn, cv, gab, gz[0], alog, dtb, gain, oraw, ssave, tsave, dog,
                                             lmat, lmat_t, masks, dproj)


def _fold_groups(wide):
    t, width = wide.shape
    tr = _pick(t, CONV_ROWS, 8)

    def body(w_ref, o_ref):
        acc = w_ref[:, 0:HEAD]
        for j in range(1, width // HEAD):
            acc = acc + w_ref[:, j * HEAD:(j + 1) * HEAD]
        o_ref[...] = acc.astype(BF16)

    return pl.pallas_call(
        body, name="fold_gate_grads", grid=(t // tr,), in_specs=[_row_spec(tr, width)], out_specs=_row_spec(tr, HEAD),
        out_shape=jax.ShapeDtypeStruct((t, HEAD), BF16), compiler_params=_params(_PAR))(wide)


def _adam_math(w, g, m, v):
    m2 = ADAM_B1 * m + (1.0 - ADAM_B1) * g
    v2 = ADAM_B2 * v + (1.0 - ADAM_B2) * (g * g)
    m_hat = m2 / (1.0 - ADAM_B1 ** ADAM_STEP)
    v_hat = v2 / (1.0 - ADAM_B2 ** ADAM_STEP)
    delta = -ADAM_LR * (m_hat / (jnp.sqrt(v_hat) + ADAM_EPS) + ADAM_WD * w)
    return delta, m2, v2


def _adamw(w, g, m, v, name, after=None):
    r, c = w.shape
    tr = r
    for cand in range(8, r + 1, 8):
        if r % cand == 0 and cand * c * 4 <= (2 << 20):
            tr = cand
    if r % 8 != 0:
        tr = r

    def body(w_ref, g_ref, m_ref, v_ref, *rest):
        d_ref, m2_ref, v2_ref = rest[-3:]
        d, m2, v2 = _adam_math(w_ref[...], g_ref[...], m_ref[...], v_ref[...])
        d_ref[...] = d
        m2_ref[...] = m2
        v2_ref[...] = v2

    spec = pl.BlockSpec((tr, c), lambda i: (i, 0))
    extra = [] if after is None else [after]
    return pl.pallas_call(
        body, name=name, grid=(r // tr,), in_specs=[spec] * 4 + [_ANY] * len(extra), out_specs=[spec] * 3,
        out_shape=[jax.ShapeDtypeStruct((r, c), F32)] * 3, compiler_params=_params(_PAR))(w, g, m, v, *extra)


_ANY = pl.BlockSpec(memory_space=pl.ANY)


def _place():
    return lax.axis_index("x"), lax.axis_index("y"), lax.axis_index("c")


def _gather_weights(packs, nchs, name):
    n = len(packs)
    halves = [p.shape[0] // 2 for p in packs]
    base = [sum(nchs[:i]) for i in range(n)]
    total = sum(nchs)
    for p, h, k in zip(packs, halves, nchs):
        assert p.shape[0] == 2 * h and h % k == 0 and (h // k) % 16 == 0

    def body(*refs):
        p_refs, g_refs, (send_sems, recv_sems) = refs[:n], refs[n:2 * n], refs[2 * n:]
        x, y, c = _place()
        sibling = (x, y, 1 - c)
        chips = [(1 - x, y), (x, 1 - y), (1 - x, 1 - y)]
        chunks = [(a, q) for a in range(n) for q in range(nchs[a])]

        def rows_of(a, pc, q):
            ch = halves[a] // nchs[a]
            return pl.ds(pl.multiple_of(pc * halves[a] + q * ch, 16), ch)

        def piece(a, px, py, pc, q):
            return g_refs[a].at[2 * px + py, rows_of(a, pc, q), :]

        def copy(k, src, dst, to):
            return pltpu.make_async_remote_copy(src_ref=src, dst_ref=dst, send_sem=send_sems.at[k],
                                                recv_sem=recv_sems.at[k], device_id=to, device_id_type=MESH)

        def sem_of(j, a, q):
            return j * total + base[a] + q

        first = {(j, a, q): copy(sem_of(j, a, q), p_refs[a].at[rows_of(a, c, q), :], piece(a, x, y, c, q), (*chip, c))
                 for j, chip in enumerate(chips) for a, q in chunks}
        for a, q in chunks:
            for j in range(3):
                first[j, a, q].start()
        passed = {(j, a, q): copy(sem_of(3 + j, a, q), piece(a, *chip, c, q), piece(a, *chip, c, q), sibling)
                  for j, chip in enumerate(chips) for a, q in chunks}
        for a, q in chunks:
            for j, chip in enumerate(chips):
                copy(sem_of(j, a, q), p_refs[a].at[rows_of(a, c, q), :], piece(a, *chip, c, q), (*chip, c)).wait_recv()
                passed[j, a, q].start()
        for a, q in chunks:
            for j, chip in enumerate(chips):
                copy(sem_of(3 + j, a, q), piece(a, *chip, 1 - c, q), piece(a, *chip, 1 - c, q), sibling).wait_recv()
        for key in first:
            first[key].wait_send()
            passed[key].wait_send()

    return pl.pallas_call(
        body, name=name, out_shape=[jax.ShapeDtypeStruct((4,) + p.shape, p.dtype) for p in packs],
        in_specs=[_ANY] * n, out_specs=[_ANY] * n,
        scratch_shapes=[pltpu.SemaphoreType.DMA((6 * total,)), pltpu.SemaphoreType.DMA((6 * total,))])(*packs)


def _swap_with_sibling(arrs, nchs, lead, name, halves=False):
    n = len(arrs)
    jobs = []
    hs = [arr.shape[-2] // (2 if halves else 1) for arr in arrs]
    for a, (h, k) in enumerate(zip(hs, nchs)):
        assert h % k == 0 and (h // k) % 16 == 0
        for s in (range(lead) if lead else [None]):
            jobs += [(a, s, q * (h // k), h // k) for q in range(k)]

    def body(*refs):
        src, dst, (send_sems, recv_sems) = refs[:n], refs[n:2 * n], refs[2 * n:]
        x, y, c = _place()

        def at(ref, s, r0, rows):
            return ref.at[pl.ds(r0, rows), :] if s is None else ref.at[s, pl.ds(r0, rows), :]

        def src_rows(a, r0):
            return pl.multiple_of((1 - c) * hs[a] + r0, 16) if halves else r0

        copies = [pltpu.make_async_remote_copy(
            src_ref=at(src[a], s, src_rows(a, r0), rows), dst_ref=at(dst[a], s, r0, rows), send_sem=send_sems.at[k],
            recv_sem=recv_sems.at[k], device_id=(x, y, 1 - c), device_id_type=MESH)
            for k, (a, s, r0, rows) in enumerate(jobs)]
        for cp in copies:
            cp.start()
        for cp in copies:
            cp.wait()

    shapes = [jax.ShapeDtypeStruct(arr.shape[:-2] + (h, arr.shape[-1]), arr.dtype) for arr, h in zip(arrs, hs)]
    return pl.pallas_call(
        body, name=name, out_shape=shapes, in_specs=[_ANY] * n, out_specs=[_ANY] * n,
        scratch_shapes=[pltpu.SemaphoreType.DMA((len(jobs),)), pltpu.SemaphoreType.DMA((len(jobs),))])(*arrs)


def _add2(full, b, core, name):
    n, rows, w = b.shape
    tr = _pick(rows, 256, 16)
    nblk = rows // tr

    def body(c_ref, a_ref, b_ref, o_ref):
        o_ref[...] = (a_ref[...].astype(F32) + b_ref[...].astype(F32)).astype(BF16)

    spec = pl.BlockSpec((1, tr, w), lambda i, j, c_ref: (i, j, 0))
    grid_spec = pltpu.PrefetchScalarGridSpec(
        num_scalar_prefetch=1, grid=(n, nblk),
        in_specs=[pl.BlockSpec((1, tr, w), lambda i, j, c_ref: (i, c_ref[0] * nblk + j, 0)), spec], out_specs=spec)
    return pl.pallas_call(
        body, name=name, grid_spec=grid_spec, out_shape=jax.ShapeDtypeStruct(b.shape, BF16),
        compiler_params=_params(_PAR, _PAR))(core, full, b)


def _reduce_chips(partials, nchs, name):
    n = len(partials)
    jobs = []
    for a, (arr, k) in enumerate(zip(partials, nchs)):
        h = arr.shape[1]
        assert h % k == 0 and (h // k) % 16 == 0
        jobs += [(a, q * (h // k), h // k) for q in range(k)]

    def body(*refs):
        src, dst, (send_sems, recv_sems) = refs[:n], refs[n:2 * n], refs[2 * n:]
        x, y, c = _place()
        chips = [(1 - x, y), (x, 1 - y), (1 - x, 1 - y)]
        copies = [pltpu.make_async_remote_copy(
            src_ref=src[a].at[2 * px + py, pl.ds(r0, rows), :], dst_ref=dst[a].at[j, pl.ds(r0, rows), :],
            send_sem=send_sems.at[3 * k + j], recv_sem=recv_sems.at[3 * k + j],
            device_id=(px, py, c), device_id_type=MESH)
            for k, (a, r0, rows) in enumerate(jobs) for j, (px, py) in enumerate(chips)]
        for cp in copies:
            cp.start()
        for cp in copies:
            cp.wait()

    return pl.pallas_call(
        body, name=name,
        out_shape=[jax.ShapeDtypeStruct((3,) + p.shape[1:], p.dtype) for p in partials],
        in_specs=[_ANY] * n, out_specs=[_ANY] * n,
        scratch_shapes=[pltpu.SemaphoreType.DMA((3 * len(jobs),)), pltpu.SemaphoreType.DMA((3 * len(jobs),))])(*partials)


_HBM = pl.BlockSpec(memory_space=pltpu.HBM)
_SEM = pl.BlockSpec(memory_space=pltpu.SEMAPHORE)
_DATAFLOW = pltpu.SideEffectType.DATAFLOW_SIDE_EFFECTING


def _ici_jobs(srcs, nchs, kind):
    jobs = []
    for a, (arr, k) in enumerate(zip(srcs, nchs)):
        h = arr.shape[0] // 2 if kind == "gather" else arr.shape[1]
        assert h % k == 0 and (h // k) % 16 == 0
        jobs += [(a, h, q * (h // k), h // k) for q in range(k)]
    return jobs


def _ici_copies(src, land, send_sems, recv_sems, jobs, kind):
    x, y, c = _place()
    chips = [(1 - x, y), (x, 1 - y), (1 - x, 1 - y)]
    copies = []
    for k, (a, h, r0, rows) in enumerate(jobs):
        for j, (px, py) in enumerate(chips):
            if kind == "gather":
                at = pl.ds(pl.multiple_of(c * h + r0, 16), rows)
                s, d = src[a].at[at, :], land[a].at[2 * x + y, at, :]
            else:
                s, d = src[a].at[2 * px + py, pl.ds(r0, rows), :], land[a].at[j, pl.ds(r0, rows), :]
            copies.append(pltpu.make_async_remote_copy(
                src_ref=s, dst_ref=d, send_sem=send_sems.at[3 * k + j], recv_sem=recv_sems.at[3 * k + j],
                device_id=(px, py, c), device_id_type=MESH))
    return copies


def _ici_start(srcs, nchs, kind, name):
    n = len(srcs)
    jobs = _ici_jobs(srcs, nchs, kind)
    lead = (lambda s: (4,) + s.shape) if kind == "gather" else (lambda s: (3,) + s.shape[1:])
    lands = [lax.empty(lead(s), s.dtype) for s in srcs]

    def body(*refs):
        src, land = refs[:n], refs[n:2 * n]
        send_sems, recv_sems, token = refs[2 * n], refs[2 * n + 1], refs[-1]
        for cp in _ici_copies(src, land, send_sems, recv_sems, jobs, kind):
            cp.start()
        token[...] = jnp.zeros_like(token)

    hbm = [pltpu.HBM(a.shape, a.dtype) for a in srcs + lands]
    outs = pl.pallas_call(
        body, name=name,
        out_shape=[pltpu.SemaphoreType.DMA((3 * len(jobs),)), pltpu.SemaphoreType.DMA((3 * len(jobs),))] + hbm
        + [jax.ShapeDtypeStruct((8, 128), F32)],
        in_specs=[_HBM] * (2 * n), out_specs=[_SEM, _SEM] + [_HBM] * (2 * n) + [pl.BlockSpec(memory_space=pltpu.VMEM)],
        input_output_aliases={i: 2 + i for i in range(2 * n)},
        compiler_params=pltpu.CompilerParams(has_side_effects=_DATAFLOW),
    )(*[pltpu.with_memory_space_constraint(a, pltpu.HBM) for a in srcs + lands])
    return (outs[0], outs[1], list(outs[2:2 + n]), list(outs[2 + n:2 + 2 * n]), nchs, kind), outs[-1]


def _ici_wait(handle, after, name):
    send_sems, recv_sems, srcs, lands, nchs, kind = handle
    n = len(srcs)
    jobs = _ici_jobs(srcs, nchs, kind)

    def body(*refs):
        src, land = refs[:n], refs[n:2 * n]
        for cp in _ici_copies(src, land, refs[2 * n], refs[2 * n + 1], jobs, kind):
            cp.wait_send()
            cp.wait_recv()

    outs = pl.pallas_call(
        body, name=name, out_shape=[pltpu.HBM(a.shape, a.dtype) for a in srcs + lands],
        in_specs=[_HBM] * (2 * n) + [_SEM, _SEM, _ANY], out_specs=[_HBM] * (2 * n),
        input_output_aliases={i: i for i in range(2 * n)},
        compiler_params=pltpu.CompilerParams(has_side_effects=_DATAFLOW),
    )(*srcs, *lands, send_sems, recv_sems, after)
    return list(outs[:n]), list(outs[n:])


def _pass_to_sibling(gathered, nchs, name):
    n = len(gathered)
    jobs = _ici_jobs([jax.ShapeDtypeStruct(g.shape[1:], g.dtype) for g in gathered], nchs, "gather")

    def body(*refs):
        src, dst, (send_sems, recv_sems) = refs[:n], refs[n:2 * n], refs[2 * n:]
        x, y, c = _place()
        slots = [2 * (1 - x) + y, 2 * x + (1 - y), 2 * (1 - x) + (1 - y)]

        def copy(k, j, pc):
            a, h, r0, rows = jobs[k]
            at = pl.ds(pl.multiple_of(pc * h + r0, 16), rows)
            return pltpu.make_async_remote_copy(
                src_ref=src[a].at[slots[j], at, :], dst_ref=dst[a].at[slots[j], at, :], send_sem=send_sems.at[3 * k + j],
                recv_sem=recv_sems.at[3 * k + j], device_id=(x, y, 1 - c), device_id_type=MESH)

        pairs = [(k, j) for k in range(len(jobs)) for j in range(3)]
        for k, j in pairs:
            copy(k, j, c).start()
        for k, j in pairs:
            copy(k, j, c).wait_send()
            copy(k, j, 1 - c).wait_recv()

    return pl.pallas_call(
        body, name=name, out_shape=[jax.ShapeDtypeStruct(g.shape, g.dtype) for g in gathered],
        in_specs=[_ANY] * n, out_specs=[_ANY] * n, input_output_aliases={i: i for i in range(n)},
        scratch_shapes=[pltpu.SemaphoreType.DMA((3 * len(jobs),)), pltpu.SemaphoreType.DMA((3 * len(jobs),))])(*gathered)


def _add4(own, got, name):
    rows, w = own.shape
    tr = _pick(rows, 128, 16)

    def body(a_ref, b_ref, o_ref):
        o_ref[...] = ((a_ref[...].astype(F32) + b_ref[0].astype(F32)) + b_ref[1].astype(F32)) + b_ref[2].astype(F32)

    return pl.pallas_call(
        body, name=name, grid=(rows // tr,),
        in_specs=[pl.BlockSpec((tr, w), lambda i: (i, 0)), pl.BlockSpec((3, tr, w), lambda i: (0, i, 0))],
        out_specs=pl.BlockSpec((tr, w), lambda i: (i, 0)), out_shape=jax.ShapeDtypeStruct((rows, w), F32),
        compiler_params=_params(_PAR))(own, got)


def _small_sync(gs, ws, ms, vs):
    rows = gs.shape[0]
    vmem = pl.BlockSpec(memory_space=pltpu.VMEM)

    def body(g_ref, w_ref, m_ref, v_ref, sum_ref, d_ref, m2_ref, v2_ref, buf, send_sems, recv_sems):
        x, y, c = _place()
        me = 4 * x + 2 * y + c
        buf[me] = g_ref[...]
        copies = []
        for k in range(1, 8):
            peer = (x ^ (k >> 2), y ^ ((k >> 1) & 1), c ^ (k & 1))
            copies.append(pltpu.make_async_remote_copy(
                src_ref=g_ref, dst_ref=buf.at[me], send_sem=send_sems.at[k - 1], recv_sem=recv_sems.at[k - 1],
                device_id=peer, device_id_type=MESH))
        for cp in copies:
            cp.start()
        for cp in copies:
            cp.wait()
        total = buf[0]
        for i in range(1, 8):
            total = total + buf[i]
        sum_ref[...] = total
        d, m2, v2 = _adam_math(w_ref[...], total, m_ref[...], v_ref[...])
        d_ref[...] = d
        m2_ref[...] = m2
        v2_ref[...] = v2

    shape = jax.ShapeDtypeStruct((rows, 128), F32)
    return pl.pallas_call(
        body, name="small_sync", out_shape=[shape] * 4, in_specs=[vmem] * 4, out_specs=[vmem] * 4,
        scratch_shapes=[pltpu.VMEM((8, rows, 128), F32), pltpu.SemaphoreType.DMA((7,)),
                        pltpu.SemaphoreType.DMA((7,))])(gs, ws, ms, vs)


_GROUPS = {
    "ffn1": dict(cols=("ffn1_w_in", 1408), rows=(("ffn1_w_out", 704, 704),), chunks=(8, 2)),
    "ffn2": dict(cols=("ffn2_w_in", 1408), rows=(("ffn2_w_out", 704, 704),), chunks=(8, 2)),
    "mixer_in": dict(cols=("w_in", 3080), rows=(("gdn_conv_w", CONV_K, 128),), chunks=(8, 1)),
    "mixer_out": dict(cols=None, chunks=(4,),
                      rows=(("w_branch_hgrn", 256, 256), ("w_branch_gdn", 512, 512), ("w_out", 256, 256))),
}


def _group_names(group):
    return ((group["cols"][0],) if group["cols"] else ()) + tuple(r[0] for r in group["rows"])


_BIG_NAMES = tuple(n for g in _GROUPS.values() for n in _group_names(g))


def _pack(parts, lead, group):
    ax = len(lead)
    rows = []
    for n, r, padded in group["rows"]:
        p = parts[n]
        if padded != r:
            p = jnp.tile(p, (1,) * ax + (padded // r, 1))
        rows.append(p)
    stacked = rows[0] if len(rows) == 1 else jnp.concatenate(rows, axis=ax)
    return ([parts[group["cols"][0]]] if group["cols"] else []) + [stacked]


def _unpack(packs, group):
    out, off = ({group["cols"][0]: packs[0]} if group["cols"] else {}), 0
    for n, r, padded in group["rows"]:
        out[n] = packs[-1][..., off:off + r, :]
        off += padded
    return out


def _is_col_sharded(name):
    return name in ("ffn1_w_in", "ffn2_w_in", "w_in", "gdn_conv_w")


def _full_from_shards(name, g):
    if _is_col_sharded(name):
        return jnp.transpose(g, (1, 0, 2)).reshape(g.shape[1], -1)
    return g.reshape(-1, g.shape[2])


def _shards_from_full(name, full):
    if _is_col_sharded(name):
        return jnp.transpose(full.reshape(full.shape[0], 4, -1), (1, 0, 2))
    return full.reshape(4, -1, full.shape[1])


_SMALL = (("ffn1_norm", 8), ("mix_norm", 8), ("hgrn_lb_logits", 16), ("hgrn_out_norm", 8), ("gdn_a_log", 8),
          ("gdn_dt_bias", 8), ("gdn_out_norm", 8), ("ffn2_norm", 8), ("final_norm", 8), ("loss", 8))
_SMALL_ROWS = sum(r for _, r in _SMALL)


def _pack_small(parts):
    out = []
    for name, rows in _SMALL:
        p = parts[name].reshape(-1).astype(F32)
        if p.shape[0] <= 128:
            if p.shape[0] < 128:
                p = jnp.concatenate([p, jnp.zeros((128 - p.shape[0],), F32)])
            p = jnp.broadcast_to(p.reshape(1, 128), (rows, 128))
        out.append(p.reshape(rows, 128))
    return jnp.concatenate(out, axis=0)


def _unpack_small(packed, shapes):
    out, off = {}, 0
    for name, rows in _SMALL:
        n = int(np.prod(shapes[name]))
        out[name] = packed[off:off + rows].reshape(-1)[:n].reshape(shapes[name])
        off += rows
    return out


def _ffn_fwd(x, gain, w_in, w_out, tag):
    n = _rmsnorm_fwd(x, gain, tag + "_norm")
    a, b, hm = _ffn_in_act(n, w_in, tag + "_in")
    out = _mm(hm, w_out, alpha=0.5, res=x, name=tag + "_out")
    return out, (n, a, b)


def _ffn_bwd(x, gain, w_in, w_out, saved, dout, dout_bf, tag):
    n, a, b = saved
    da, db, hm = _ffn_dact(dout_bf, w_out, a, b, tag + "_dact")
    dw_out = _mm(hm, dout_bf, ta=True, alpha=0.5, out_dtype=BF16, name=tag + "_dwout")
    dwa = _mm(n, da, ta=True, out_dtype=BF16, name=tag + "_dwin_a")
    dwb = _mm(n, db, ta=True, out_dtype=BF16, name=tag + "_dwin_b")
    half = D_FF // 2
    dw_in = jnp.stack([dwa[:, :half], dwa[:, half:], dwb[:, :half], dwb[:, half:]])
    dn = _mm(da, w_in, tb=True, name=tag + "_dnorm_a")
    dn = _mm(db, w_in, tb=True, res=dn, b_from=D_FF, name=tag + "_dnorm_b")
    dx, dx_bf, dgain = _rmsnorm_bwd(x, gain, dn, dout, tag + "_dx")
    return dx, dx_bf, dgain, dw_in, dw_out


def _pad_lanes(v):
    return jnp.concatenate([v.reshape(1, -1), jnp.zeros((1, HEAD - v.size), F32)], axis=1)


def _local_step(x, tgt, small, exchange):
    hg_c = _hg_consts()
    gd_c = _gd_consts()
    alog = _pad_lanes(small["gdn_a_log"])
    dtb = _pad_lanes(small["gdn_dt_bias"])
    logits = small["hgrn_lb_logits"]
    hg_gain = small["hgrn_out_norm"].reshape(1, HEAD)
    gd_gain = small["gdn_out_norm"].reshape(1, HEAD)
    g1, gm, g2 = small["ffn1_norm"].reshape(1, -1), small["mix_norm"].reshape(1, -1), small["ffn2_norm"].reshape(1, -1)
    gf = small["final_norm"].reshape(1, -1)
    qscale = HEAD ** -0.5

    w1 = exchange.weights("ffn1")
    started = exchange.prefetch("mixer_in")
    h1, ffn1_saved = _ffn_fwd(x, g1 + started, w1["ffn1_w_in"], w1["ffn1_w_out"], "ffn1")
    u = _rmsnorm_fwd(h1, gm, "mix_norm")
    w = exchange.weights("mixer_in", after=u)
    started = exchange.prefetch("mixer_out") + exchange.prefetch("ffn2")
    seg, off = {}, 0
    for name, size in zip(IN_NAMES, IN_SIZES):
        seg[name] = w["w_in"][:, off:off + size]
        off += size
    w_gab = jnp.concatenate([seg["ga"], seg["gb"], jnp.zeros((D_MODEL, HEAD - 32), BF16)], axis=1)
    big_segs = [n for n in IN_NAMES if n not in ("ga", "gb")]
    conv8 = jnp.concatenate([w["gdn_conv_w"].astype(F32), jnp.zeros((8 - CONV_K, 4096), F32)], axis=0)
    conv_q, conv_k, conv_v = conv8[:, :1024], conv8[:, 1024:2048], conv8[:, 2048:]
    w_main = jnp.concatenate([seg[n] for n in big_segs], axis=1)
    proj = _mm(u, w_main, name="proj", tm_max=2048)
    pr, off = {}, 0
    for n in big_segs:
        pr[n] = _view(proj, off, seg[n].shape[1])
        off += seg[n].shape[1]
    gab = _mm(u, w_gab, name="proj_gab")
    oh_raw, oh, s_h = _hgrn_fwd(pr["hq"], pr["hf"], pr["hi"], pr["hg"], logits, hg_gain + started, hg_c)
    qn = _conv_fwd(pr["gq"], conv_q, qscale, "conv_q")
    kn = _conv_fwd(pr["gk"], conv_k, 1.0, "conv_k")
    cv = _conv_fwd(pr["gv"], conv_v, None, "conv_v")
    og_raw, og, s_g, t_g = _gdn_fwd(qn, kn, cv, gab, pr["gz"], alog, dtb, gd_gain, gd_c)
    wo = exchange.weights("mixer_out", after=og)
    yh = _mm(oh, wo["w_branch_hgrn"], out_dtype=BF16, name="branch_h")
    yg = _mm(og, wo["w_branch_gdn"], out_dtype=BF16, name="branch_g")
    ym = _merge_fwd(yh, yg, pr["gate_h"], pr["gate_g"])
    h2 = _mm(ym, wo["w_out"], res=h1, name="mix_out")
    w2 = exchange.weights("ffn2", after=h2)
    h3, ffn2_saved = _ffn_fwd(h2, g2, w2["ffn2_w_in"], w2["ffn2_w_out"], "ffn2")
    loss, dh3, dh3_bf, d_gf = _final_loss(h3, gf, tgt)

    dh2, dh2_bf, d_g2, d_f2in, d_f2out = _ffn_bwd(h2, g2, w2["ffn2_w_in"], w2["ffn2_w_out"], ffn2_saved, dh3, dh3_bf,
                                                  "ffn2")
    started = exchange.reduce("ffn2", {"ffn2_w_in": d_f2in, "ffn2_w_out": d_f2out}, behind=True)
    dym = _mm(dh2_bf, wo["w_out"], tb=True, name="d_merge")
    d_wout = _mm(ym, dh2_bf, ta=True, out_dtype=BF16, name="d_w_out")
    dproj = lax.empty((x.shape[0], w_main.shape[1]), BF16)
    dyh, dyg, dproj = _merge_bwd(dym, yh, yg, pr["gate_h"], pr["gate_g"], _into(dproj, pr["gate_h"][1], 2 * D_MODEL))
    d_wbh = _mm(oh, dyh, ta=True, out_dtype=BF16, name="d_w_branch_h")
    d_wbg = _mm(og, dyg, ta=True, out_dtype=BF16, name="d_w_branch_g")
    started = started + exchange.reduce("mixer_out", {"w_branch_hgrn": d_wbh, "w_branch_gdn": d_wbg, "w_out": d_wout},
                                        behind=True)
    doh = _mm(dyh, wo["w_branch_hgrn"], tb=True, name="d_oh")
    dog = _mm(dyg, wo["w_branch_gdn"], tb=True, name="d_og")
    dproj, d_hg_gain, d_lb0 = _hgrn_bwd(pr["hq"], pr["hf"], pr["hi"], pr["hg"], logits, hg_gain + started, oh_raw,
                                        s_h, doh, hg_c, _into(dproj, pr["hq"][1], 4 * D_MODEL))
    d_qn, d_kn, d_cv, d_gab_wide, dproj, gd_small = _gdn_bwd(qn, kn, cv, gab, pr["gz"], alog, dtb, gd_gain, og_raw,
                                                             s_g, t_g, dog, gd_c, _into(dproj, *pr["gz"][1:]))
    d_gab = _fold_groups(d_gab_wide)
    dc_q, dwc_q = _conv_bwd_a(pr["gq"], conv_q, d_qn, qscale, "dconv_q")
    dc_k, dwc_k = _conv_bwd_a(pr["gk"], conv_k, d_kn, 1.0, "dconv_k")
    dc_v, dwc_v = _conv_bwd_a(pr["gv"], conv_v, d_cv, None, "dconv_v")
    dproj = _conv_bwd_b(dc_q, conv_q, "dconvx_q", _into(dproj, *pr["gq"][1:]))
    dproj = _conv_bwd_b(dc_k, conv_k, "dconvx_k", _into(dproj, *pr["gk"][1:]))
    dproj = _conv_bwd_b(dc_v, conv_v, "dconvx_v", _into(dproj, *pr["gv"][1:]))
    du =_mm(d_gab, w_gab, tb=True, name="du_gab")
    du = _mm(dproj, w_main, tb=True, res=du, name="du")
    d_wmain = _mm(u, dproj, ta=True, out_dtype=BF16, name="dw_main")
    d_wgab = _mm(u, d_gab, ta=True, out_dtype=BF16, name="dw_gab")
    cut = IN_WIDTH // 4
    d_win = jnp.stack([d_wmain[:, :cut], d_wmain[:, cut:2 * cut],
                       jnp.concatenate([d_wmain[:, 2 * cut:8192], d_wgab[:, :32], d_wmain[:, 8192:3 * cut - 32]], axis=1),
                       d_wmain[:, 3 * cut - 32:]])
    d_conv = jnp.concatenate([dwc_q[:CONV_K], dwc_k[:CONV_K], dwc_v[:CONV_K]], axis=1).astype(BF16)
    started = exchange.reduce("mixer_in", {"w_in": d_win, "gdn_conv_w": d_conv}, behind=True)
    dh1, dh1_bf, d_gm = _rmsnorm_bwd(h1, gm + started, du, dh2, "mix_dnorm")
    dx, _, d_g1, d_f1in, d_f1out = _ffn_bwd(x, g1, w1["ffn1_w_in"], w1["ffn1_w_out"], ffn1_saved, dh1, dh1_bf, "ffn1")
    exchange.reduce("ffn1", {"ffn1_w_in": d_f1in, "ffn1_w_out": d_f1out}, behind=True)
    d_lb0 = d_lb0.reshape(1, -1)
    sm = {"ffn1_norm": d_g1, "mix_norm": d_gm, "hgrn_lb_logits": jnp.concatenate([d_lb0, -d_lb0], axis=0),
          "hgrn_out_norm": d_hg_gain, "gdn_a_log": gd_small[2, :16], "gdn_dt_bias": gd_small[1, :16],
          "gdn_out_norm": gd_small[0], "ffn2_norm": d_g2, "final_norm": d_gf, "loss": loss[0, :1]}
    return dx, sm


class _Exchange:
    def __init__(self, wts):
        self.wts = wts
        xi, yi, ci = _place()
        self.chip = 2 * xi + yi
        self.south = ci == 0
        self.core = ci.reshape(1).astype(jnp.int32)
        self.mine = {}
        self.coming = {}
        self.going = {}

    def _packs(self, tag):
        group = _GROUPS[tag]
        return _pack({n: self.wts[n][0].astype(BF16) for n in _group_names(group)}, (), group)

    def prefetch(self, tag):
        packs = self._packs(tag)
        handle, token = _ici_start(packs, _GROUPS[tag]["chunks"], "gather", "gather_start_" + tag)
        self.coming[tag] = handle
        return token[0:1, 0:1]

    def weights(self, tag, after=None):
        group = _GROUPS[tag]
        if tag in self.coming:
            packs, halves = _ici_wait(self.coming.pop(tag), after, "gather_wait_" + tag)
            others = _pass_to_sibling(halves, group["chunks"], "gather_pass_" + tag)
        else:
            packs = self._packs(tag)
            others = _gather_weights(packs, group["chunks"], "gather_" + tag)
        whole = [lax.dynamic_update_index_in_dim(g, p, self.chip, 0) for g, p in zip(others, packs)]
        gathered = _unpack(whole, group)
        return {n: _full_from_shards(n, gathered[n]) for n in _group_names(group)}

    def reduce(self, tag, grads, behind=False):
        group = _GROUPS[tag]
        shards = {n: (grads[n] if grads[n].ndim == 3 else _shards_from_full(n, grads[n])) for n in _group_names(group)}
        gpacks = _pack(shards, (4,), group)
        got = _swap_with_sibling(gpacks, group["chunks"], 4, "reduce_pair_" + tag, halves=True)
        sums = [_add2(a, b, self.core, "add_pair_%s_%d" % (tag, i)) for i, (a, b) in enumerate(zip(gpacks, got))]
        if behind:
            handle, token = _ici_start(sums, group["chunks"], "reduce", "reduce_start_" + tag)
            self.going[tag] = handle
            self.token = token
            return token[0:1, 0:1]
        self._add_chips(tag, sums, _reduce_chips(sums, group["chunks"], "reduce_chips_" + tag))
        return None

    def _add_chips(self, tag, sums, from_chips):
        self.mine[tag] = [_add4(lax.dynamic_index_in_dim(s, self.chip, axis=0, keepdims=False), f,
                                "add_chips_%s_%d" % (tag, i)) for i, (s, f) in enumerate(zip(sums, from_chips))]

    def finish(self, tags, after):
        for tag in tags:
            if tag in self.going:
                self._add_chips(tag, *_ici_wait(self.going.pop(tag), after, "reduce_wait_" + tag))
        mine = [a for t in tags for a in self.mine[t]]
        nchs = [k for t in tags for k in _GROUPS[t]["chunks"]]
        theirs = _swap_with_sibling(mine, nchs, 0, "share_pair_" + tags[0])
        whole = [jnp.concatenate([jnp.where(self.south, a, b), jnp.where(self.south, b, a)], axis=0)
                 for a, b in zip(mine, theirs)]
        reduced, at = {}, 0
        for t in tags:
            n = len(self.mine[t])
            reduced.update(_unpack(whole[at:at + n], _GROUPS[t]))
            at += n
        return reduced


_WEIGHTS = ("ffn1_norm", "ffn1_w_in", "ffn1_w_out", "mix_norm", "w_in", "hgrn_lb_logits", "hgrn_out_norm",
            "gdn_conv_w", "gdn_a_log", "gdn_dt_bias", "gdn_out_norm", "w_branch_hgrn", "w_branch_gdn", "w_out",
            "ffn2_norm", "ffn2_w_in", "ffn2_w_out", "final_norm")


def kernel(x, ffn1_norm, ffn1_w_in, ffn1_w_out, mix_norm, w_in, hgrn_lb_logits, hgrn_out_norm, gdn_conv_w, gdn_a_log, gdn_dt_bias, gdn_out_norm, w_branch_hgrn, w_branch_gdn, w_out, ffn2_norm, ffn2_w_in, ffn2_w_out, final_norm, loss_target, m_ffn1_norm, m_ffn1_w_in, m_ffn1_w_out, m_mix_norm, m_w_in, m_hgrn_lb_logits, m_hgrn_out_norm, m_gdn_conv_w, m_gdn_a_log, m_gdn_dt_bias, m_gdn_out_norm, m_w_branch_hgrn, m_w_branch_gdn, m_w_out, m_ffn2_norm, m_ffn2_w_in, m_ffn2_w_out, m_final_norm, v_ffn1_norm, v_ffn1_w_in, v_ffn1_w_out, v_mix_norm, v_w_in, v_hgrn_lb_logits, v_hgrn_out_norm, v_gdn_conv_w, v_gdn_a_log, v_gdn_dt_bias, v_gdn_out_norm, v_w_branch_hgrn, v_w_branch_gdn, v_w_out, v_ffn2_norm, v_ffn2_w_in, v_ffn2_w_out, v_final_norm):
    args = dict(locals())
    wts = {n: args[n] for n in _WEIGHTS}
    moms = {n: args["m_" + n] for n in _WEIGHTS}
    vars_ = {n: args["v_" + n] for n in _WEIGHTS}

    small = {n: wts[n].astype(F32) for n in _WEIGHTS if n not in _BIG_NAMES}
    exchange = _Exchange(wts)
    dx, small_grads = _local_step(x[0], loss_target[0], small, exchange)

    out_g, out_d, out_m, out_v = {}, {}, {}, {}

    def update(tags, reduced, after):
        for t in tags:
            for n in _group_names(_GROUPS[t]):
                shape = wts[n].shape
                w2 = wts[n].reshape(shape[-2], shape[-1])
                g2 = reduced[n]
                d, m2, v2 = _adamw(w2, g2, moms[n].reshape(w2.shape), vars_[n].reshape(w2.shape), "adamw_" + n, after)
                out_g[n], out_d[n], out_m[n], out_v[n] = (g2.reshape(shape), d.reshape(shape), m2.reshape(shape),
                                                          v2.reshape(shape))
                after = v2
        return after

    early = ("ffn2", "mixer_out", "mixer_in")
    done = update(early, exchange.finish(early, after=dx), exchange.token)
    update(("ffn1",), exchange.finish(("ffn1",), after=done), None)

    small_names = [n for n, _ in _SMALL]
    zero = jnp.zeros((1,), F32)
    shapes = {n: (wts[n].shape if n != "loss" else (1,)) for n in small_names}
    sums, sd, sm_, sv = _small_sync(
        _pack_small(small_grads),
        _pack_small({n: (wts[n] if n != "loss" else zero) for n in small_names}),
        _pack_small({n: (moms[n] if n != "loss" else zero) for n in small_names}),
        _pack_small({n: (vars_[n] if n != "loss" else zero) for n in small_names}))
    sg_u, sd_u, sm_u, sv_u = (_unpack_small(p, shapes) for p in (sums, sd, sm_, sv))
    for n in small_names:
        if n != "loss":
            out_g[n], out_d[n], out_m[n], out_v[n] = sg_u[n], sd_u[n], sm_u[n], sv_u[n]
    loss = sg_u["loss"].reshape(())

    return (loss, dx[None], *[out_g[n] for n in _WEIGHTS], *[out_d[n] for n in _WEIGHTS],
            *[out_m[n] for n in _WEIGHTS], *[out_v[n] for n in _WEIGHTS])
```

```python
import numpy as np

import jax
import jax.numpy as jnp
from jax import lax
from jax.experimental import pallas as pl
from jax.experimental.pallas import tpu as pltpu

F32 = jnp.float32
BF16 = jnp.bfloat16

D_MODEL = 1024
D_FF = 2816
CHUNK = 64
HEAD = 128
HG_HEADS = 8
GD_HEADS = 16
HPS = 8
MM_TM = 1408
MM_TN = 1024
MM_TK = 2048
VMEM_LIMIT = 48 * 1024 * 1024
ROW_TILE = 512
EPS = 1e-6
CONV_K = 4
IN_NAMES = ("hq", "hf", "hi", "hg", "gq", "gk", "gv", "ga", "gb", "gz", "gate_h", "gate_g")
IN_SIZES = (1024, 1024, 1024, 1024, 1024, 1024, 2048, 16, 16, 2048, 1024, 1024)
IN_WIDTH = sum(IN_SIZES)

ADAM_LR = 0.001
ADAM_B1 = 0.9
ADAM_B2 = 0.999
ADAM_EPS = 1e-08
ADAM_WD = 0.01
ADAM_STEP = 10

MESH = pl.DeviceIdType.MESH
_ARB = "arbitrary"
_PAR = "parallel"


def _bf(x):
    return x.astype(BF16)


def _dot(a, b):
    return jnp.dot(_bf(a), _bf(b), preferred_element_type=F32)


def _dot_nt(a, b):
    return lax.dot_general(_bf(a), _bf(b), (((1,), (1,)), ((), ())), preferred_element_type=F32)


def _dot_tn(a, b):
    return lax.dot_general(_bf(a), _bf(b), (((0,), (0,)), ((), ())), preferred_element_type=F32)


def _sigmoid(x):
    return jax.nn.sigmoid(x)


def _silu(x):
    return x * _sigmoid(x)


def _dsilu(x):
    s = _sigmoid(x)
    return s * (1.0 + x * (1.0 - s))


def _softplus(x):
    return jnp.maximum(x, 0.0) + jnp.log(1.0 + jnp.exp(-jnp.abs(x)))


def _rowsum(x):
    return jnp.sum(x, axis=1, keepdims=True)


def _col_to_row(col, eye):
    return jnp.sum(eye * col, axis=0, keepdims=True)


def _row_to_col(row, eye):
    return jnp.sum(eye * row, axis=1, keepdims=True)


def _pick(dim, pref, unit=128):
    if dim <= pref:
        return dim
    t = pref
    while t >= unit:
        if dim % t == 0:
            return t
        t -= unit
    return dim


def _params(*sem):
    return pltpu.CompilerParams(dimension_semantics=tuple(sem), vmem_limit_bytes=VMEM_LIMIT)


def _mm(a, b, *, ta=False, tb=False, alpha=1.0, res=None, out_dtype=F32, name="mm", b_from=0, tm_max=MM_TM):
    m = a.shape[1] if ta else a.shape[0]
    k = a.shape[0] if ta else a.shape[1]
    n = b.shape[0] if tb else b.shape[1]
    assert b_from + k <= (b.shape[1] if tb else b.shape[0])
    tm, tn, tk = _pick(m, tm_max), _pick(n, MM_TN), _pick(k, MM_TK)
    if tn < MM_TN < n and n % MM_TM == 0:
        tn = MM_TM
    nk = k // tk
    assert b_from % tk == 0
    b0 = b_from // tk
    a_spec = pl.BlockSpec((tk, tm), lambda i, j, l: (l, i)) if ta else pl.BlockSpec((tm, tk), lambda i, j, l: (i, l))
    b_spec = (pl.BlockSpec((tn, tk), lambda i, j, l: (j, b0 + l)) if tb
              else pl.BlockSpec((tk, tn), lambda i, j, l: (b0 + l, j)))
    o_spec = pl.BlockSpec((tm, tn), lambda i, j, l: (i, j))
    dims = (((0 if ta else 1,), (1 if tb else 0,)), ((), ()))
    has_res = res is not None

    def finish(r, r_ref, o_ref):
        if alpha != 1.0:
            r = r * alpha
        if has_res:
            r = r + r_ref[...]
        o_ref[...] = r.astype(out_dtype)

    def body(*refs):
        a_ref, b_ref = refs[0], refs[1]
        r_ref = refs[2] if has_res else None
        o_ref = refs[3] if has_res else refs[2]
        part = lax.dot_general(_bf(a_ref[...]), _bf(b_ref[...]), dims, preferred_element_type=F32)
        if nk == 1:
            finish(part, r_ref, o_ref)
            return
        acc = refs[-1]
        step = pl.program_id(2)

        @pl.when(step == 0)
        def _():
            acc[...] = part

        @pl.when(step != 0)
        def _():
            acc[...] += part

        @pl.when(step == nk - 1)
        def _():
            finish(acc[...], r_ref, o_ref)

    ins = [a, b] + ([res] if has_res else [])
    in_specs = [a_spec, b_spec] + ([o_spec] if has_res else [])
    return pl.pallas_call(
        body, name=name, grid=(m // tm, n // tn, nk), in_specs=in_specs, out_specs=o_spec,
        out_shape=jax.ShapeDtypeStruct((m, n), out_dtype),
        scratch_shapes=[pltpu.VMEM((tm, tn), F32)] if nk > 1 else [],
        compiler_params=_params(_PAR, _PAR, _ARB))(*ins)


def _row_spec(tr, w):
    return pl.BlockSpec((tr, w), lambda i: (i, 0))


def _full_spec(shape):
    return pl.BlockSpec(shape, lambda i: tuple(0 for _ in shape))


def _view(arr, off, width):
    return arr, off, width


def _view_rows(view, tr):
    _, off, width = view
    assert off % width == 0
    return pl.BlockSpec((tr, width), lambda i: (i, off // width))


def _view_tile(view, rows, bw, cidx=lambda c: c):
    _, off, width = view
    assert off % bw == 0 and width % bw == 0
    return pl.BlockSpec((rows, bw), lambda c, g: (cidx(c), off // bw + g))


def _rmsnorm_fwd(x, g, name):
    t, d = x.shape
    tr = _pick(t, ROW_TILE, 8)

    def body(x_ref, g_ref, o_ref):
        xv = x_ref[...]
        r = lax.rsqrt(jnp.mean(xv * xv, axis=1, keepdims=True) + EPS)
        o_ref[...] = (xv * r * g_ref[...]).astype(BF16)

    return pl.pallas_call(
        body, name=name, grid=(t // tr,), in_specs=[_row_spec(tr, d), _full_spec((1, d))],
        out_specs=_row_spec(tr, d), out_shape=jax.ShapeDtypeStruct((t, d), BF16),
        compiler_params=_params(_PAR))(x, g)


def _rmsnorm_bwd(x, g, dn, res, name):
    t, d = x.shape
    tr = _pick(t, ROW_TILE, 8)

    def body(x_ref, g_ref, dn_ref, r_ref, dx_ref, dxb_ref, dg_ref):
        @pl.when(pl.program_id(0) == 0)
        def _():
            dg_ref[...] = jnp.zeros_like(dg_ref)

        xv = x_ref[...]
        r = lax.rsqrt(jnp.mean(xv * xv, axis=1, keepdims=True) + EPS)
        xh = xv * r
        dy = dn_ref[...]
        dg_ref[...] += jnp.sum(dy * xh, axis=0, keepdims=True)
        dxh = dy * g_ref[...]
        dx = r_ref[...] + r * (dxh - xh * jnp.mean(dxh * xh, axis=1, keepdims=True))
        dx_ref[...] = dx
        dxb_ref[...] = dx.astype(BF16)

    return pl.pallas_call(
        body, name=name, grid=(t // tr,),
        in_specs=[_row_spec(tr, d), _full_spec((1, d)), _row_spec(tr, d), _row_spec(tr, d)],
        out_specs=[_row_spec(tr, d), _row_spec(tr, d), _full_spec((1, d))],
        out_shape=[jax.ShapeDtypeStruct((t, d), F32), jax.ShapeDtypeStruct((t, d), BF16),
                   jax.ShapeDtypeStruct((1, d), F32)],
        compiler_params=_params(_ARB))(x, g, dn, res)


FFN_TN = 1408
FFN_TM = 512


def _ffn_in_act(n, w_in, name):
    t, d = n.shape
    tm = _pick(t, FFN_TM)
    nf = D_FF // FFN_TN

    def body(n_ref, wa_ref, wb_ref, a_ref, b_ref, hm_ref):
        nv = n_ref[...]
        a = jnp.dot(nv, wa_ref[...], preferred_element_type=F32)
        b = jnp.dot(nv, wb_ref[...], preferred_element_type=F32)
        a_ref[...] = a.astype(BF16)
        b_ref[...] = b.astype(BF16)
        hm_ref[...] = (_silu(a) * b).astype(BF16)

    tile = pl.BlockSpec((tm, FFN_TN), lambda i, j: (i, j))
    return pl.pallas_call(
        body, name=name, grid=(t // tm, nf),
        in_specs=[pl.BlockSpec((tm, d), lambda i, j: (i, 0)), pl.BlockSpec((d, FFN_TN), lambda i, j: (0, j)),
                  pl.BlockSpec((d, FFN_TN), lambda i, j: (0, nf + j))],
        out_specs=[tile, tile, tile], out_shape=[jax.ShapeDtypeStruct((t, D_FF), BF16)] * 3,
        compiler_params=_params(_PAR, _PAR))(n, w_in, w_in)


def _ffn_dact(dout, w_out, a, b, name):
    t, d = dout.shape
    tm = _pick(t, FFN_TM)

    def body(do_ref, w_ref, a_ref, b_ref, da_ref, db_ref, hm_ref):
        dh = 0.5 * _dot_nt(do_ref[...], w_ref[...])
        av = a_ref[...].astype(F32)
        bv = b_ref[...].astype(F32)
        sg = _sigmoid(av)
        sa = av * sg
        da_ref[...] = (dh * bv * (sg * (1.0 + av * (1.0 - sg)))).astype(BF16)
        db_ref[...] = (dh * sa).astype(BF16)
        hm_ref[...] = (sa * bv).astype(BF16)

    tile = pl.BlockSpec((tm, FFN_TN), lambda i, j: (i, j))
    return pl.pallas_call(
        body, name=name, grid=(t // tm, D_FF // FFN_TN),
        in_specs=[pl.BlockSpec((tm, d), lambda i, j: (i, 0)), pl.BlockSpec((FFN_TN, d), lambda i, j: (j, 0)), tile, tile],
        out_specs=[tile, tile, tile], out_shape=[jax.ShapeDtypeStruct((t, D_FF), BF16)] * 3,
        compiler_params=_params(_PAR, _PAR))(dout, w_out, a, b)


def _merge_fwd(yh, yg, gh, gg):
    t, d = yh.shape
    tr = _pick(t, ROW_TILE, 8)

    def body(yh_ref, yg_ref, gh_ref, gg_ref, o_ref):
        o_ref[...] = (_sigmoid(gh_ref[...]) * yh_ref[...] + _sigmoid(gg_ref[...]) * yg_ref[...]).astype(BF16)

    return pl.pallas_call(
        body, name="merge_fwd", grid=(t // tr,),
        in_specs=[_row_spec(tr, d), _row_spec(tr, d), _view_rows(gh, tr), _view_rows(gg, tr)],
        out_specs=_row_spec(tr, d),
        out_shape=jax.ShapeDtypeStruct((t, d), BF16), compiler_params=_params(_PAR))(yh, yg, gh[0], gg[0])


def _into(dproj, off, width):
    return dproj, off, width


def _merge_bwd(dy, yh, yg, gh, gg, into):
    t, d = yh.shape
    tr = _pick(t, ROW_TILE, 8)
    dproj, off, width = into
    assert width == 2 * d and off % width == 0

    def body(dy_ref, yh_ref, yg_ref, gh_ref, gg_ref, _, dyh_ref, dyg_ref, dg_ref):
        dyv = dy_ref[...]
        sh = _sigmoid(gh_ref[...])
        sg = _sigmoid(gg_ref[...])
        dyh_ref[...] = (dyv * sh).astype(BF16)
        dyg_ref[...] = (dyv * sg).astype(BF16)
        dg_ref[:, :d] = (dyv * yh_ref[...] * sh * (1.0 - sh)).astype(BF16)
        dg_ref[:, d:] = (dyv * yg_ref[...] * sg * (1.0 - sg)).astype(BF16)

    return pl.pallas_call(
        body, name="merge_bwd", grid=(t // tr,),
        in_specs=[_row_spec(tr, d)] * 3 + [_view_rows(gh, tr), _view_rows(gg, tr), _ANY],
        out_specs=[_row_spec(tr, d)] * 2 + [pl.BlockSpec((tr, width), lambda i: (i, off // width))],
        out_shape=[jax.ShapeDtypeStruct((t, d), BF16)] * 2 + [jax.ShapeDtypeStruct(dproj.shape, dproj.dtype)],
        input_output_aliases={5: 2},
        compiler_params=_params(_PAR))(dy, yh, yg, gh[0], gg[0], dproj)


def _final_loss(h, g, tgt):
    t, d = h.shape
    tr = _pick(t, ROW_TILE, 8)

    def body(h_ref, g_ref, t_ref, loss_ref, dh_ref, dhb_ref, dg_ref):
        @pl.when(pl.program_id(0) == 0)
        def _():
            dg_ref[...] = jnp.zeros_like(dg_ref)
            loss_ref[...] = jnp.zeros_like(loss_ref)

        xv = h_ref[...]
        gv = g_ref[...]
        r = lax.rsqrt(jnp.mean(xv * xv, axis=1, keepdims=True) + EPS)
        xh = xv * r
        err = xh * gv - t_ref[...]
        loss_ref[...] += 0.5 * jnp.sum(jnp.mean(err * err, axis=1, keepdims=True), axis=0, keepdims=True)
        dy = err * (1.0 / d)
        dg_ref[...] += jnp.sum(dy * xh, axis=0, keepdims=True)
        dxh = dy * gv
        dh = r * (dxh - xh * jnp.mean(dxh * xh, axis=1, keepdims=True))
        dh_ref[...] = dh
        dhb_ref[...] = dh.astype(BF16)

    return pl.pallas_call(
        body, name="final_loss", grid=(t // tr,),
        in_specs=[_row_spec(tr, d), _full_spec((1, d)), _row_spec(tr, d)],
        out_specs=[_full_spec((1, 128)), _row_spec(tr, d), _row_spec(tr, d), _full_spec((1, d))],
        out_shape=[jax.ShapeDtypeStruct((1, 128), F32), jax.ShapeDtypeStruct((t, d), F32),
                   jax.ShapeDtypeStruct((t, d), BF16), jax.ShapeDtypeStruct((1, d), F32)],
        compiler_params=_params(_ARB))(h, g, tgt)


def _hg_consts():
    c = CHUNK
    t = np.arange(c)
    mats, masks = [], []
    for lvl in range(6):
        m = 1 << lvl
        blk = t // m
        mat = np.zeros((c, c), np.float32)
        for tt in range(c):
            b = blk[tt]
            if b % 2 == 1:
                mat[tt, b * m:tt + 1] = 1.0
            else:
                mat[tt, tt + 1:(b + 1) * m] = 1.0
        mats.append(mat)
        same = (t[:, None] // (2 * m)) == (t[None, :] // (2 * m))
        masks.append((same & (blk[:, None] % 2 == 1) & (blk[None, :] % 2 == 0)).astype(np.float32))
    pre = np.tril(np.ones((c, c), np.float32))
    suf = np.triu(np.ones((c, c), np.float32), 1)
    mstack = np.concatenate(mats + [pre, suf], 0)
    masks.append(np.eye(c, dtype=np.float32))
    return (jnp.asarray(mstack, BF16), jnp.asarray(mstack.T.copy(), BF16), jnp.asarray(np.stack(masks), F32),
            jnp.asarray(np.eye(HEAD, dtype=np.float32)))


def _gd_consts():
    c = CHUNK
    incl = np.tril(np.ones((c, c), np.float32))
    strict = np.tril(np.ones((c, c), np.float32), -1)
    eye = np.eye(c, dtype=np.float32)
    masks = np.stack([incl, strict, eye, incl.T.copy()])
    return jnp.asarray(incl, BF16), jnp.asarray(incl.T.copy(), BF16), jnp.asarray(masks, F32)


def _chunks_per_step(nc):
    for cb in (32 // HPS, 2, 1):
        if nc % cb == 0:
            return cb
    return 1


def _hg_prep(hq, hf, lg):
    lb = _sigmoid(lg[0:1, :] - lg[1:2, :])
    sg = _sigmoid(hf)
    sgn = _sigmoid(-hf)
    f = lb + (1.0 - lb) * sg
    lf = jnp.log(f)
    kk = (1.0 - lb) * sgn
    q = _silu(hq) * (HEAD ** -0.5)
    return lb, sg, sgn, f, lf, kk, q


def _mx_each(m, xs):
    hi, lo = _split2_each(xs)
    prods = [jnp.dot(m, jnp.concatenate([h, l], axis=1), preferred_element_type=F32) for h, l in zip(hi, lo)]
    return [p[:, :HEAD] + p[:, HEAD:] for p in prods]


def _hg_scaled(x, ex):
    xb = [_bf(a) for a in x]
    eb = [_bf(e[:6 * CHUNK]) for e in ex]
    return [[a * e[lvl * CHUNK:(lvl + 1) * CHUNK] for lvl in range(6)] for a, e in zip(xb, eb)]


def _hg_scores(q, kk, qe, ke, mask_ref):
    p = [mask_ref[6] * _rowsum(a * b) for a, b in zip(q, kk)]
    for lvl in range(6):
        d = [_dot_nt(a[lvl], b[lvl]) for a, b in zip(qe, ke)]
        p = [x + mask_ref[lvl] * y for x, y in zip(p, d)]
    return p


def _hgrn_fwd(hq, hf, hi, hg, logits, gain, consts):
    t = hq[0].shape[0]
    nc = t // CHUNK
    cb = _chunks_per_step(nc)
    rows = cb * CHUNK
    mstack, _, masks, eye = consts
    tile = pl.BlockSpec((rows, HPS * HEAD), lambda c, g: (c, g))

    def body(hq_ref, hf_ref, hi_ref, hg_ref, lg_ref, gain_ref, m_ref, mask_ref, eye_ref,
             oraw_ref, og_ref, ssave_ref, state):
        c = pl.program_id(0)
        g = pl.program_id(1)

        @pl.when(c == 0)
        def _():
            for hh in range(HPS):
                state[g * HPS + hh] = jnp.zeros((HEAD, HEAD), F32)

        lg_all = lg_ref[...]
        gain_v = gain_ref[...]

        def one(i, carry):
            sl = pl.ds(pl.multiple_of(i * CHUNK, CHUNK), CHUNK)
            hs = range(HPS)
            heads = [g * HPS + hh for hh in hs]
            ln = [slice(hh * HEAD, (hh + 1) * HEAD) for hh in hs]
            preps = [_hg_prep(hq_ref[sl, s], hf_ref[sl, s], lg_all[:, s]) for s in ln]
            lf, kk, q = [p[4] for p in preps], [p[5] for p in preps], [p[6] for p in preps]
            v = [hi_ref[sl, s] for s in ln]
            ex = [jnp.exp(x) for x in _mx_each(m_ref[...], lf)]
            eb = [e[6 * CHUNK:7 * CHUNK] for e in ex]
            esfx = [e[7 * CHUNK:8 * CHUNK] for e in ex]
            qe, ke = _hg_scaled(q, ex), _hg_scaled(kk, ex)
            p = _hg_scores(q, kk, qe, ke, mask_ref)
            s0 = [state[h] for h in heads]
            o = _each(lambda a, e, s, pp, vv: _dot(a * e, s) + _dot(pp, vv), q, eb, s0, p, v)
            eye_v = eye_ref[...]
            s1 = _each(lambda s, e, kx, ef, vv: s * _row_to_col(e[CHUNK - 1:CHUNK, :], eye_v) + _dot_tn(kx * ef, vv),
                       s0, eb, kk, esfx, v)
            for hh in hs:
                ssave_ref[i, hh] = s0[hh]
                state[heads[hh]] = s1[hh]
                oraw_ref[sl, ln[hh]] = o[hh]
                r = lax.rsqrt(jnp.mean(o[hh] * o[hh], axis=1, keepdims=True) + EPS)
                og_ref[sl, ln[hh]] = (o[hh] * r * gain_v * _silu(hg_ref[sl, ln[hh]])).astype(BF16)
            return carry

        lax.fori_loop(0, cb, one, 0, unroll=4)

    return pl.pallas_call(
        body, name="hgrn_fwd", grid=(nc // cb, HG_HEADS // HPS),
        in_specs=[_view_tile(v, rows, HPS * HEAD) for v in (hq, hf, hi, hg)] + [
                  pl.BlockSpec((2, HPS * HEAD), lambda c, g: (0, g)),
                  pl.BlockSpec((1, HEAD), lambda c, g: (0, 0)),
                  pl.BlockSpec(mstack.shape, lambda c, g: (0, 0)),
                  pl.BlockSpec(masks.shape, lambda c, g: (0, 0, 0)),
                  pl.BlockSpec(eye.shape, lambda c, g: (0, 0))],
        out_specs=[tile, tile, pl.BlockSpec((cb, HPS, HEAD, HEAD), lambda c, g: (c, g, 0, 0))],
        out_shape=[jax.ShapeDtypeStruct((t, HG_HEADS * HEAD), F32), jax.ShapeDtypeStruct((t, HG_HEADS * HEAD), BF16),
                   jax.ShapeDtypeStruct((nc, HG_HEADS, HEAD, HEAD), F32)],
        scratch_shapes=[pltpu.VMEM((HG_HEADS, HEAD, HEAD), F32)],
        compiler_params=_params(_ARB, _ARB))(hq[0], hf[0], hi[0], hg[0], logits, gain, mstack, masks, eye)


def _hgrn_bwd(hq, hf, hi, hg, logits, gain, oraw, ssave, dog, consts, into):
    t = hq[0].shape[0]
    dproj, off, width = into
    seg = HG_HEADS * HEAD
    assert HPS == HG_HEADS and width == 4 * seg and off % width == 0
    nc = t // CHUNK
    cb = _chunks_per_step(nc)
    rows = cb * CHUNK
    nb = nc // cb
    mstack, mstack_t, masks, eye = consts
    tile = pl.BlockSpec((rows, HPS * HEAD), lambda c, g: (nb - 1 - c, g))

    def body(hq_ref, hf_ref, hi_ref, hg_ref, lg_ref, gain_ref, oraw_ref, ssave_ref, dog_ref, m_ref, mt_ref,
             mask_ref, eye_ref, _, d_ref, dgain_ref, dlb_ref, dstate):
        c = pl.program_id(0)
        g = pl.program_id(1)

        @pl.when(c == 0)
        def _():
            for hh in range(HPS):
                dstate[g * HPS + hh] = jnp.zeros((HEAD, HEAD), F32)

        @pl.when((c == 0) & (g == 0))
        def _():
            dgain_ref[...] = jnp.zeros_like(dgain_ref)
            dlb_ref[...] = jnp.zeros_like(dlb_ref)

        lg_all = lg_ref[...]
        gain_v = gain_ref[...]
        eye_v = eye_ref[...]
        last_row = (lax.broadcasted_iota(jnp.int32, (CHUNK, HEAD), 0) == CHUNK - 1).astype(F32)

        def one(j, carry):
            i = cb - 1 - j
            sl = pl.ds(pl.multiple_of(i * CHUNK, CHUNK), CHUNK)
            hs = range(HPS)
            heads = [g * HPS + hh for hh in hs]
            ln = [slice(hh * HEAD, (hh + 1) * HEAD) for hh in hs]
            hqv = [hq_ref[sl, s] for s in ln]
            hgv = [hg_ref[sl, s] for s in ln]
            preps = [_hg_prep(a, hf_ref[sl, s], lg_all[:, s]) for a, s in zip(hqv, ln)]
            lb, sg, sgn, f, lf, kk, q = ([p[n] for p in preps] for n in range(7))
            v = [hi_ref[sl, s] for s in ln]
            ex = [jnp.exp(x) for x in _mx_each(m_ref[...], lf)]
            eb = [e[6 * CHUNK:7 * CHUNK] for e in ex]
            esfx = [e[7 * CHUNK:8 * CHUNK] for e in ex]
            qe, ke = _hg_scaled(q, ex), _hg_scaled(kk, ex)
            p = _hg_scores(q, kk, qe, ke, mask_ref)
            s0 = [ssave_ref[i, hh] for hh in hs]
            ds = [dstate[h] for h in heads]

            o = [oraw_ref[sl, s] for s in ln]
            r = [lax.rsqrt(jnp.mean(x * x, axis=1, keepdims=True) + EPS) for x in o]
            on = _each(lambda x, y: x * y, o, r)
            dg_out = [dog_ref[sl, s] for s in ln]
            sgate = [_silu(x) for x in hgv]
            for hh in hs:
                d_ref[sl, slice(3 * seg + hh * HEAD, 3 * seg + (hh + 1) * HEAD)] =(dg_out[hh] * on[hh] * gain_v * _dsilu(hgv[hh])).astype(BF16)
            dgain_ref[...] += sum(jnp.sum(d * s * n, axis=0, keepdims=True) for d, s, n in zip(dg_out, sgate, on))
            don = _each(lambda d, s: d * s * gain_v, dg_out, sgate)
            do = _each(lambda rr, dn, n: rr * (dn - n * jnp.mean(dn * n, axis=1, keepdims=True)), r, don, on)

            dp = _each(_dot_nt, do, v)
            dv = _each(lambda pp, d, kx, ef, s: _dot_tn(pp, d) + _dot(kx * ef, s), p, do, kk, esfx, ds)
            dqb = _each(_dot_nt, do, s0)
            dkx = _each(_dot_nt, v, ds)
            diag = [_rowsum(mask_ref[6] * x) for x in dp]
            dq = _each(lambda a, e, d, kx: a * e + d * kx, dqb, eb, diag, kk)
            dk = _each(lambda a, e, d, qq: a * e + d * qq, dkx, esfx, diag, q)
            dxs = [[] for _ in hs]
            for lvl in range(6):
                el = [e[lvl * CHUNK:(lvl + 1) * CHUNK] for e in ex]
                gm = [mask_ref[lvl] * x for x in dp]
                gm = [_bf(x) for x in gm]
                a1 = _each(lambda m_, kx: _dot(m_, kx[lvl]), gm, ke)
                a2 = _each(lambda m_, qq: _dot_tn(m_, qq[lvl]), gm, qe)
                dq = _each(lambda x, a, e: x + a * e, dq, a1, el)
                dk = _each(lambda x, a, e: x + a * e, dk, a2, el)
                for hh in hs:
                    dxs[hh].append((a1[hh] * q[hh] + a2[hh] * kk[hh]) * el[hh])
            e_end_row = [e[CHUNK - 1:CHUNK, :] for e in eb]
            ds_new = _each(lambda qq, e, d, er, s: _dot_tn(qq * e, d) + _row_to_col(er, eye_v) * s, q, eb, do, e_end_row, ds)
            for hh in hs:
                dstate[heads[hh]] = ds_new[hh]
                dend_row = _col_to_row(_rowsum(s0[hh] * ds[hh]), eye_v)
                dxs[hh].append(dqb[hh] * q[hh] * eb[hh] + last_row * (e_end_row[hh] * dend_row))
                dxs[hh].append(dkx[hh] * kk[hh] * esfx[hh])
            dlf = _mx_each(mt_ref[...], [jnp.concatenate(x, axis=0) for x in dxs])

            for hh in hs:
                d_ref[sl, slice(2 * seg + hh * HEAD, 2 * seg + (hh + 1) * HEAD)] =dv[hh].astype(BF16)
                d_ref[sl, ln[hh]] =(dq[hh] * (HEAD ** -0.5) * _dsilu(hqv[hh])).astype(BF16)
                df = dlf[hh] / f[hh]
                dsig = (1.0 - lb[hh]) * sg[hh] * sgn[hh]
                d_ref[sl, slice(seg + hh * HEAD, seg + (hh + 1) * HEAD)] =((df - dk[hh]) * dsig).astype(BF16)
                dlb_t = jnp.sum(df * sgn[hh] - dk[hh] * sgn[hh], axis=0, keepdims=True)
                dlb_ref[pl.ds(heads[hh], 1), :] += dlb_t * lb[hh] * (1.0 - lb[hh])
            return carry

        lax.fori_loop(0, cb, one, 0, unroll=2)

    outs = [jax.ShapeDtypeStruct(dproj.shape, dproj.dtype),
            jax.ShapeDtypeStruct((1, HEAD), F32), jax.ShapeDtypeStruct((HG_HEADS, HEAD), F32)]
    return pl.pallas_call(
        body, name="hgrn_bwd", grid=(nb, HG_HEADS // HPS),
        in_specs=[_view_tile(v, rows, HPS * HEAD, lambda c: nb - 1 - c) for v in (hq, hf, hi, hg)] + [
                  pl.BlockSpec((2, HPS * HEAD), lambda c, g: (0, g)),
                  pl.BlockSpec((1, HEAD), lambda c, g: (0, 0)), tile,
                  pl.BlockSpec((cb, HPS, HEAD, HEAD), lambda c, g: (nb - 1 - c, g, 0, 0)), tile,
                  pl.BlockSpec(mstack.shape, lambda c, h: (0, 0)),
                  pl.BlockSpec(mstack_t.shape, lambda c, h: (0, 0)),
                  pl.BlockSpec(masks.shape, lambda c, h: (0, 0, 0)),
                  pl.BlockSpec(eye.shape, lambda c, h: (0, 0)), _ANY],
        out_specs=[pl.BlockSpec((rows, width), lambda c, h: (nb - 1 - c, off // width)),
                   pl.BlockSpec((1, HEAD), lambda c, h: (0, 0)),
                   pl.BlockSpec((HG_HEADS, HEAD), lambda c, h: (0, 0))],
        out_shape=outs, scratch_shapes=[pltpu.VMEM((HG_HEADS, HEAD, HEAD), F32)], input_output_aliases={13: 0},
        compiler_params=_params(_ARB, _ARB))(hq[0], hf[0], hi[0], hg[0], logits, gain, oraw, ssave, dog, mstack,
                                             mstack_t, masks, eye, dproj)


CONV_W = 512
CONV_ROWS = 1024


def _per_head(fn, *arrs):
    width = arrs[0].shape[1]
    return jnp.concatenate([fn(*[a[:, j:j + HEAD] for a in arrs]) for j in range(0, width, HEAD)], axis=1)


def _shift_down(xv, halo, d, top_rows):
    if d == 0:
        return xv, xv[0:8]
    main = pltpu.roll(xv, d, 0)
    top = jnp.where(top_rows < d, pltpu.roll(halo, d, 0), main[0:8])
    return main, top


def _conv_parts(x_ref, halo_ref, w_ref, first):
    xv = x_ref[...]
    halo = jnp.where(first, 0.0, halo_ref[...])
    top_rows = lax.broadcasted_iota(jnp.int32, (8, xv.shape[1]), 0)
    shifted = [_shift_down(xv, halo, CONV_K - 1 - j, top_rows) for j in range(CONV_K)]
    w = w_ref[...]
    acc = sum(shifted[j][0] * w[j:j + 1, :] for j in range(CONV_K))
    acc_top = sum(shifted[j][1] * w[j:j + 1, :] for j in range(CONV_K))
    return shifted, acc, acc_top


def _conv_fwd(x, w8, l2scale, name):
    x, off, width = x
    t = x.shape[0]
    o = off // CONV_W
    tr = _pick(t, CONV_ROWS, 8)

    def post(cv):
        s = _silu(cv)
        if l2scale is not None:
            s = _per_head(lambda sh: sh * (lax.rsqrt(_rowsum(sh * sh) + EPS) * l2scale), s)
        return s

    def body(x_ref, halo_ref, w_ref, o_ref):
        _, acc, acc_top = _conv_parts(x_ref, halo_ref, w_ref, pl.program_id(1) == 0)
        o_ref[...] = post(acc)
        o_ref[0:8, :] = post(acc_top)

    return pl.pallas_call(
        body, name=name, grid=(width // CONV_W,t // tr),
        in_specs=[pl.BlockSpec((tr, CONV_W), lambda j, i: (i, o + j)),
                  pl.BlockSpec((8, CONV_W), lambda j, i: (jnp.maximum(i * (tr // 8) - 1, 0), o + j)),
                  pl.BlockSpec((8, CONV_W), lambda j, i: (0, j))],
        out_specs=pl.BlockSpec((tr, CONV_W), lambda j, i: (i, j)),
        out_shape=jax.ShapeDtypeStruct((t, width), F32), compiler_params=_params(_PAR, _PAR))(x, x, w8)


def _conv_bwd_a(x, w8, dy, l2scale, name):
    x, off, width = x
    t = x.shape[0]
    o = off // CONV_W
    tr = _pick(t, CONV_ROWS, 8)

    def l2_bwd(s, dyh):
        r = lax.rsqrt(_rowsum(s * s) + EPS)
        y0 = s * r
        dy0 = dyh * l2scale
        return r * (dy0 - y0 * _rowsum(dy0 * y0))

    def to_dc(cv, dyv):
        if l2scale is not None:
            dyv = _per_head(l2_bwd, _silu(cv), dyv)
        return dyv * _dsilu(cv)

    def body(x_ref, halo_ref, w_ref, dy_ref, dc_ref, dw_ref):
        @pl.when(pl.program_id(1) == 0)
        def _():
            dw_ref[...] = jnp.zeros_like(dw_ref)

        shifted, acc, acc_top = _conv_parts(x_ref, halo_ref, w_ref, pl.program_id(1) == 0)
        dyv = dy_ref[...]
        dc = to_dc(acc, dyv)
        dc_top = to_dc(acc_top, dyv[0:8])
        dc_ref[...] = dc
        dc_ref[0:8, :] = dc_top
        rest = (lax.broadcasted_iota(jnp.int32, dc.shape, 0) >= 8).astype(F32)
        dc_rest = dc * rest
        for j in range(CONV_K):
            dw_ref[j:j + 1, :] += (jnp.sum(dc_rest * shifted[j][0], axis=0, keepdims=True)
                                   + jnp.sum(dc_top * shifted[j][1], axis=0, keepdims=True))

    return pl.pallas_call(
        body, name=name, grid=(width // CONV_W,t // tr),
        in_specs=[pl.BlockSpec((tr, CONV_W), lambda j, i: (i, o + j)),
                  pl.BlockSpec((8, CONV_W), lambda j, i: (jnp.maximum(i * (tr // 8) - 1, 0), o + j)),
                  pl.BlockSpec((8, CONV_W), lambda j, i: (0, j)),
                  pl.BlockSpec((tr, CONV_W), lambda j, i: (i, j))],
        out_specs=[pl.BlockSpec((tr, CONV_W), lambda j, i: (i, j)), pl.BlockSpec((8, CONV_W), lambda j, i: (0, j))],
        out_shape=[jax.ShapeDtypeStruct((t, width), F32), jax.ShapeDtypeStruct((8, width), F32)],
        compiler_params=_params(_PAR, _ARB))(x, x, w8, dy)


def _conv_bwd_b(dc, w8, name, into):
    t, width = dc.shape
    tr = _pick(t, CONV_ROWS, 8)
    nt = t // tr

    dproj, off, into_width = into
    assert into_width == width and off % CONV_W == 0
    o = off // CONV_W

    def body(dc_ref, halo_ref, w_ref, _, dx_ref):
        dcv = dc_ref[...]
        halo = jnp.where(pl.program_id(1) == nt - 1, 0.0, halo_ref[...])
        w = w_ref[...]
        bot_rows = lax.broadcasted_iota(jnp.int32, (8, CONV_W), 0)
        acc = dcv * w[CONV_K - 1:CONV_K, :]
        acc_bot = dcv[tr - 8:tr] * w[CONV_K - 1:CONV_K, :]
        for d in range(1, CONV_K):
            main = pltpu.roll(dcv, tr - d, 0)
            bot = jnp.where(bot_rows >= 8 - d, pltpu.roll(halo, 8 - d, 0), main[tr - 8:tr])
            wj = w[CONV_K - 1 - d:CONV_K - d, :]
            acc = acc + main * wj
            acc_bot = acc_bot + bot * wj
        dx_ref[...] = acc.astype(BF16)
        dx_ref[tr - 16:tr, :] = jnp.concatenate([acc[tr - 16:tr - 8], acc_bot], axis=0).astype(BF16)

    return pl.pallas_call(
        body, name=name, grid=(width // CONV_W,nt),
        in_specs=[pl.BlockSpec((tr, CONV_W), lambda j, i: (i, j)),
                  pl.BlockSpec((8, CONV_W), lambda j, i: (jnp.minimum((i + 1) * (tr // 8), t // 8 - 1), j)),
                  pl.BlockSpec((8, CONV_W), lambda j, i: (0, j)), _ANY],
        out_specs=pl.BlockSpec((tr, CONV_W), lambda j, i: (i, o + j)),
        out_shape=jax.ShapeDtypeStruct(dproj.shape, dproj.dtype), input_output_aliases={3: 0},
        compiler_params=_params(_PAR, _PAR))(dc, dc, w8, dproj)


def _each(f, *lists):
    return [f(*xs) for xs in zip(*lists)]


def _split2_each(xs):
    hi = [_bf(x) for x in xs]
    lo = [_bf(x - h.astype(F32)) for x, h in zip(xs, hi)]
    return hi, lo


def _hp_each(a_split, b_split):
    (ah, al), (bh, bl) = a_split, b_split
    rows = ah[0].shape[0]
    d12 = [jnp.dot(jnp.concatenate([x, y], axis=0), z, preferred_element_type=F32) for x, y, z in zip(ah, al, bh)]
    d3 = [jnp.dot(x, y, preferred_element_type=F32) for x, y in zip(ah, bl)]
    return [d[:rows] + d[rows:] + e for d, e in zip(d12, d3)]


INV_EXACT_STEPS = 1


def _tri_inv_each(a_list, eye):
    ns = [-a for a in a_list]
    ps = [eye + n for n in ns]
    n_split = _split2_each(ns)
    for step in range(5):
        if step < INV_EXACT_STEPS:
            ns = _hp_each(n_split, n_split)
            n_split = _split2_each(ns)
            ps = [p + d for p, d in zip(ps, _hp_each(_split2_each(ps), n_split))]
        else:
            nb = n_split[0] if step == INV_EXACT_STEPS else [_bf(n) for n in ns]
            ns = [jnp.dot(x, x, preferred_element_type=F32) for x in nb]
            nb2 = [_bf(n) for n in ns]
            ps = [p + jnp.dot(_bf(p), y, preferred_element_type=F32) for p, y in zip(ps, nb2)]
    return ps


def _gd_gates(gab, alog, dtb):
    sp_arg = gab + dtb
    return sp_arg, -jnp.exp(alog) * _softplus(sp_arg), _sigmoid(gab)


def _pick_lane(tile, base, head):
    g, hh = head
    col = tile[:, base + hh:base + hh + 1]
    for gi in range(1, GD_HEADS // HPS):
        lane = base + gi * HPS + hh
        col = jnp.where(g == gi, tile[:, lane:lane + 1], col)
    return col


def _gd_chunks(q, k, v, g_all, beta_all, heads, l_ref, mask_ref, tm=None):
    incl, strict, eye, upper = mask_ref[0], mask_ref[1], mask_ref[2], mask_ref[3]
    lmat = l_ref[...]
    gb = [jnp.broadcast_to(_pick_lane(g_all, 0, s), (CHUNK, HEAD)) for s in heads]
    bb = [jnp.broadcast_to(_pick_lane(beta_all, GD_HEADS, s), (CHUNK, HEAD)) for s in heads]
    gam = _mx_each(lmat, gb)
    gam_row = [jnp.sum(x[:, :CHUNK] * upper, axis=0, keepdims=True) for x in gb]
    lm = _each(lambda gm, gr: incl * jnp.exp(jnp.minimum(gm[:, :CHUNK] - gr, 0.0)), gam, gam_row)
    kb = _each(lambda x, b: x * b, k, bb)
    a = _each(lambda x, y, m: strict * _dot_nt(x, y) * m, kb, k, lm)
    if tm is None:
        tm = _tri_inv_each(a, eye)
    eg = [jnp.exp(x) for x in gam]
    vb = _each(lambda x, b: x * b, v, bb)
    kbg = _each(lambda x, e: x * e, kb, eg)
    uw = _each(lambda t_, x, y: _dot(t_, jnp.concatenate([x, y], axis=1)), tm, vb, kbg)
    u = [x[:, :HEAD] for x in uw]
    w = [x[:, HEAD:] for x in uw]
    qk = _each(lambda x, y, m: _dot_nt(x, y) * m, q, k, lm)
    g_end = [x[CHUNK - 1:CHUNK, :] for x in gam]
    ekg = _each(lambda e, x: jnp.exp(e - x), g_end, gam)
    ge = [jnp.exp(e) for e in g_end]
    kg = _each(lambda x, e: x * e, k, ekg)
    qg = _each(lambda x, e: x * e, q, eg)
    names = ("bb", "lm", "kb", "a", "tm", "eg", "vb", "kbg", "u", "w", "qk", "ekg", "ge", "kg", "qg")
    cols = (bb, lm, kb, a, tm, eg, vb, kbg, u, w, qk, ekg, ge, kg, qg)
    return [dict(zip(names, vals)) for vals in zip(*cols)]


def _gd_specs(rows, rev_nb=None):
    def cidx(c):
        return c if rev_nb is None else rev_nb - 1 - c

    qk_tile = pl.BlockSpec((rows, HPS // 2 * HEAD), lambda c, g: (cidx(c), g))
    v_tile = pl.BlockSpec((rows, HPS * HEAD), lambda c, g: (cidx(c), g))
    gab_tile = pl.BlockSpec((rows, HEAD), lambda c, g: (cidx(c), 0))
    return qk_tile, v_tile, gab_tile


def _gdn_fwd(qn, kn, cv, gab, gz, alog, dtb, gain, consts):
    t = qn.shape[0]
    nc = t // CHUNK
    cb = _chunks_per_step(nc)
    rows = cb * CHUNK
    lmat, _, masks = consts
    qk_tile, v_tile, gab_tile = _gd_specs(rows)
    row128 = pl.BlockSpec((1, HEAD), lambda c, h: (0, 0))

    def body(q_ref, k_ref, v_ref, gab_ref, gz_ref, alog_ref, dtb_ref, gain_ref, l_ref, mask_ref,
             oraw_ref, og_ref, ssave_ref, tsave_ref, state):
        c = pl.program_id(0)
        g = pl.program_id(1)

        @pl.when(c == 0)
        def _():
            for hh in range(HPS):
                state[g * HPS + hh] = jnp.zeros((HEAD, HEAD), F32)

        alog = alog_ref[...]
        dtb = dtb_ref[...]
        gain_v = gain_ref[...]

        def one(i, carry):
            sl = pl.ds(pl.multiple_of(i * CHUNK, CHUNK), CHUNK)
            _, g_all, beta_all = _gd_gates(gab_ref[sl, :], alog, dtb)
            heads = [g * HPS + hh for hh in range(HPS)]
            lq = [slice(hh // 2 * HEAD, (hh // 2 + 1) * HEAD) for hh in range(HPS)]
            lv = [slice(hh * HEAD, (hh + 1) * HEAD) for hh in range(HPS)]
            chs = _gd_chunks([q_ref[sl, s] for s in lq], [k_ref[sl, s] for s in lq], [v_ref[sl, s] for s in lv],
                             g_all, beta_all, [(g, hh) for hh in range(HPS)], l_ref, mask_ref)
            s0 = [state[h] for h in heads]
            ws = _each(lambda ch, s: _dot(jnp.concatenate([ch["w"], ch["qg"]], axis=0), s), chs, s0)
            v_new = _each(lambda ch, x: ch["u"] - x[:CHUNK], chs, ws)
            o = _each(lambda ch, x, vn: x[CHUNK:] + _dot(ch["qk"], vn), chs, ws, v_new)
            s1 = _each(lambda ch, s, vn: s * ch["ge"] + _dot_tn(ch["kg"], vn), chs, s0, v_new)
            for hh in range(HPS):
                ssave_ref[i, hh] = s0[hh]
                tsave_ref[i, hh] = chs[hh]["tm"]
                state[heads[hh]] = s1[hh]
                oraw_ref[sl, lv[hh]] = o[hh]
                r = lax.rsqrt(jnp.mean(o[hh] * o[hh], axis=1, keepdims=True) + EPS)
                og_ref[sl, lv[hh]] = (o[hh] * r * gain_v * _silu(gz_ref[sl, lv[hh]])).astype(BF16)
            return carry

        lax.fori_loop(0, cb, one, 0, unroll=4)

    return pl.pallas_call(
        body, name="gdn_fwd", grid=(nc // cb, GD_HEADS // HPS),
        in_specs=[qk_tile, qk_tile, v_tile, gab_tile, _view_tile(gz, rows, HPS * HEAD), row128, row128, row128,
                  pl.BlockSpec(lmat.shape, lambda c, g: (0, 0)),
                  pl.BlockSpec(masks.shape, lambda c, g: (0, 0, 0))],
        out_specs=[v_tile, v_tile, pl.BlockSpec((cb, HPS, HEAD, HEAD), lambda c, g: (c, g, 0, 0)),
                   pl.BlockSpec((cb, HPS, CHUNK, CHUNK), lambda c, g: (c, g, 0, 0))],
        out_shape=[jax.ShapeDtypeStruct((t, GD_HEADS * HEAD), F32), jax.ShapeDtypeStruct((t, GD_HEADS * HEAD), BF16),
                   jax.ShapeDtypeStruct((nc, GD_HEADS, HEAD, HEAD), F32),
                   jax.ShapeDtypeStruct((nc, GD_HEADS, CHUNK, CHUNK), F32)],
        scratch_shapes=[pltpu.VMEM((GD_HEADS, HEAD, HEAD), F32)],
        compiler_params=_params(_ARB, _ARB))(qn, kn, cv, gab, gz[0], alog, dtb, gain, lmat, masks)


def _gdn_bwd(qn, kn, cv, gab, gz, alog, dtb, gain, oraw, ssave, tsave, dog, consts, into):
    t = qn.shape[0]
    dproj, off, width = into
    assert width == GD_HEADS * HEAD and off % (HPS * HEAD) == 0
    nc = t // CHUNK
    cb = _chunks_per_step(nc)
    rows = cb * CHUNK
    nb = nc // cb
    lmat, lmat_t, masks = consts
    qk_tile, v_tile, gab_tile = _gd_specs(rows, nb)
    row128 = pl.BlockSpec((1, HEAD), lambda c, h: (0, 0))

    def body(q_ref, k_ref, v_ref, gab_ref, gz_ref, alog_ref, dtb_ref, gain_ref, oraw_ref, ssave_ref, tsave_ref, dog_ref,
             l_ref, lt_ref, mask_ref, _,
             dq_ref, dk_ref, dv_ref, dgab_ref, dgz_ref, small_ref, dstate):
        c = pl.program_id(0)
        g = pl.program_id(1)

        @pl.when(c == 0)
        def _():
            for hh in range(HPS):
                dstate[g * HPS + hh] = jnp.zeros((HEAD, HEAD), F32)

        @pl.when((c == 0) & (g == 0))
        def _():
            small_ref[...] = jnp.zeros_like(small_ref)

        alog = alog_ref[...]
        dtb = dtb_ref[...]
        gain_v = gain_ref[...]
        lane = lax.broadcasted_iota(jnp.int32, (1, HEAD), 1)
        last_row = (lax.broadcasted_iota(jnp.int32, (CHUNK, HEAD), 0) == CHUNK - 1).astype(F32)

        def one(j, carry):
            i = cb - 1 - j
            sl = pl.ds(pl.multiple_of(i * CHUNK, CHUNK), CHUNK)
            sp_arg, g_all, beta_all = _gd_gates(gab_ref[sl, :], alog, dtb)
            strict, eye = mask_ref[1], mask_ref[2]
            ltm = lt_ref[...]
            hs = range(HPS)
            heads = [g * HPS + hh for hh in hs]
            lq = [slice(hh // 2 * HEAD, (hh // 2 + 1) * HEAD) for hh in hs]
            lv = [slice(hh * HEAD, (hh + 1) * HEAD) for hh in hs]
            q = [q_ref[sl, s] for s in lq]
            k = [k_ref[sl, s] for s in lq]
            v = [v_ref[sl, s] for s in lv]
            gzv = [gz_ref[sl, s] for s in lv]
            chs = _gd_chunks(q, k, v, g_all, beta_all, [(g, hh) for hh in hs], l_ref, mask_ref,
                             tm=[tsave_ref[i, hh] for hh in hs])

            def col(name):
                return [ch[name] for ch in chs]

            def mul(x, y):
                return x * y

            tm, lm, eg, bb = col("tm"), col("lm"), col("eg"), col("bb")
            s0 = [ssave_ref[i, hh] for hh in hs]
            ds = [dstate[h] for h in heads]
            v_new = _each(lambda u, w, s: u - _dot(w, s), col("u"), col("w"), s0)

            o = [oraw_ref[sl, s] for s in lv]
            r = [lax.rsqrt(jnp.mean(x * x, axis=1, keepdims=True) + EPS) for x in o]
            on = _each(mul, o, r)
            dg_out = [dog_ref[sl, s] for s in lv]
            sgate = [_silu(x) for x in gzv]
            for hh in hs:
                dgz_ref[sl, lv[hh]] = (dg_out[hh] * on[hh] * gain_v * _dsilu(gzv[hh])).astype(BF16)
            small_ref[0:1, :] += sum(jnp.sum(d * s * n, axis=0, keepdims=True) for d, s, n in zip(dg_out, sgate, on))
            don = _each(lambda d, s: d * s * gain_v, dg_out, sgate)
            do = _each(lambda rr, dn, n: rr * (dn - n * jnp.mean(dn * n, axis=1, keepdims=True)), r, don, on)

            dv_new = _each(lambda a, d, b, s: _dot_tn(a, d) + _dot(b, s), col("qk"), do, col("kg"), ds)
            dqk = _each(_dot_nt, do, v_new)
            dkg = _each(_dot_nt, v_new, ds)
            dge = _each(lambda s, d: jnp.sum(_rowsum(s * d), axis=0, keepdims=True), s0, ds)
            both = _each(lambda d, dv: jnp.concatenate([d, dv], axis=0), do, dv_new)
            from_s = _each(_dot_nt, both, s0)
            dqg = [x[:CHUNK] for x in from_s]
            dw = [-x[CHUNK:] for x in from_s]
            ds_new = _each(lambda qg, w, bo, ge, s: _dot_tn(jnp.concatenate([qg, -w], axis=0), bo) + ge * s,
                           col("qg"), col("w"), both, col("ge"), ds)
            for hh in hs:
                dstate[heads[hh]] = ds_new[hh]

            side = _each(lambda dv, d: jnp.concatenate([dv, d], axis=1), dv_new, dw)
            back = _each(_dot_tn, tm, side)
            dvb = [x[:, :HEAD] for x in back]
            dkbg = [x[:, HEAD:] for x in back]
            dtm = _each(lambda sd, vb, kbg: _dot_nt(sd, jnp.concatenate([vb, kbg], axis=1)), side, col("vb"), col("kbg"))
            dtt = _each(_dot_nt, dtm, tm)
            da = _each(lambda t_, x: -_dot_tn(t_, x) * strict, tm, dtt)
            dal = _each(mul, da, lm)
            dqk_l = _each(mul, dqk, lm)
            stack = _each(lambda x, y: jnp.concatenate([x, y], axis=0), dal, dqk_l)
            on_k = _each(_dot, stack, k)
            dkb = _each(lambda x, y, e: x[:CHUNK] + y * e, on_k, dkbg, eg)
            dq = _each(lambda x, y, e: x[CHUNK:] + y * e, on_k, dqg, eg)
            dk = _each(lambda st, kb, qq, z, ekg, w_, b: _dot_tn(st, jnp.concatenate([kb, qq], axis=0)) + z * ekg + w_ * b,
                       stack, col("kb"), q, dkg, col("ekg"), dkb, bb)
            gmat = _each(lambda x, a, y, qk: x * a + y * qk, da, col("a"), dqk, col("qk"))
            t_kg = _each(lambda x, y: _rowsum(x * y), dkg, col("kg"))
            dgam = _each(lambda gm, x, qg, t_, y, kbg: (_rowsum(gm) - _row_to_col(jnp.sum(gm, axis=0, keepdims=True), eye)
                                                        + _rowsum(x * qg) - t_ + _rowsum(y * kbg)),
                         gmat, dqg, col("qg"), t_kg, dkbg, col("kbg"))
            dg_end = _each(lambda t_, e, ge: jnp.sum(t_, axis=0, keepdims=True) + e * ge[:, 0:1], t_kg, dge, col("ge"))
            dgam = _each(lambda x, e: x + last_row * e, dgam, dg_end)
            dbeta = _each(lambda x, kk, y, vv: _rowsum(x * kk) + _rowsum(y * vv), dkb, k, dvb, v)
            dg = _mx_each(ltm, dgam)

            for hh in hs:
                dv_ref[sl, lv[hh]] = dvb[hh] * bb[hh]
            fac_g = -jnp.exp(alog) * _sigmoid(sp_arg)
            fac_b = beta_all * (1.0 - beta_all)
            hot_g = [(lane == h).astype(F32) for h in heads]
            hot_b = [(lane == GD_HEADS + h).astype(F32) for h in heads]
            dga = _each(lambda x, hot: x * hot * fac_g, dg, hot_g)
            dgb = _each(lambda x, hot: x * hot * fac_b, dbeta, hot_b)
            small_ref[1:2, :] += sum(jnp.sum(x, axis=0, keepdims=True) for x in dga)
            small_ref[2:3, :] += sum(jnp.sum(x * hot * g_all, axis=0, keepdims=True) for x, hot in zip(dg, hot_g))
            for pair in range(HPS // 2):
                lqp = slice(pair * HEAD, (pair + 1) * HEAD)
                dq_ref[sl, lqp] = dq[2 * pair] + dq[2 * pair + 1]
                dk_ref[sl, lqp] = dk[2 * pair] + dk[2 * pair + 1]
            dgab_ref[sl, :] = sum(a + b for a, b in zip(dga, dgb))
            return carry

        lax.fori_loop(0, cb, one, 0, unroll=4)

    groups = GD_HEADS // HPS
    outs = [jax.ShapeDtypeStruct((t, 1024), F32), jax.ShapeDtypeStruct((t, 1024), F32),
            jax.ShapeDtypeStruct((t, 2048), F32), jax.ShapeDtypeStruct((t, groups * HEAD), F32),
            jax.ShapeDtypeStruct(dproj.shape, dproj.dtype), jax.ShapeDtypeStruct((8, HEAD), F32)]
    dgz_tile = pl.BlockSpec((rows, HPS * HEAD), lambda c, g: (nb - 1 - c, off // (HPS * HEAD) + g))
    return pl.pallas_call(
        body, name="gdn_bwd", grid=(nb, groups),
        in_specs=[qk_tile, qk_tile, v_tile, gab_tile, _view_tile(gz, rows, HPS * HEAD, lambda c: nb - 1 - c),
                  row128, row128, row128, v_tile,
                  pl.BlockSpec((cb, HPS, HEAD, HEAD), lambda c, g: (nb - 1 - c, g, 0, 0)),
                  pl.BlockSpec((cb, HPS, CHUNK, CHUNK), lambda c, g: (nb - 1 - c, g, 0, 0)), v_tile,
                  pl.BlockSpec(lmat.shape, lambda c, g: (0, 0)),
                  pl.BlockSpec(lmat_t.shape, lambda c, g: (0, 0)),
                  pl.BlockSpec(masks.shape, lambda c, g: (0, 0, 0)), _ANY],
        out_specs=[qk_tile, qk_tile, v_tile, pl.BlockSpec((rows, HEAD), lambda c, g: (nb - 1 - c, g)), dgz_tile,
                   pl.BlockSpec((8, HEAD), lambda c, g: (0, 0))],
        out_shape=outs, scratch_shapes=[pltpu.VMEM((GD_HEADS, HEAD, HEAD), F32)], input_output_aliases={15: 4},
        compiler_params=_params(_ARB, _ARB))(qn, kn, cv, gab, gz[0], alog, dtb, gain, oraw, ssave, tsave, dog,
                                             lmat, lmat_t, masks, dproj)


def _fold_groups(wide):
    t, width = wide.shape
    tr = _pick(t, CONV_ROWS, 8)

    def body(w_ref, o_ref):
        acc = w_ref[:, 0:HEAD]
        for j in range(1, width // HEAD):
            acc = acc + w_ref[:, j * HEAD:(j + 1) * HEAD]
        o_ref[...] = acc.astype(BF16)

    return pl.pallas_call(
        body, name="fold_gate_grads", grid=(t // tr,), in_specs=[_row_spec(tr, width)], out_specs=_row_spec(tr, HEAD),
        out_shape=jax.ShapeDtypeStruct((t, HEAD), BF16), compiler_params=_params(_PAR))(wide)


def _adam_math(w, g, m, v):
    m2 = ADAM_B1 * m + (1.0 - ADAM_B1) * g
    v2 = ADAM_B2 * v + (1.0 - ADAM_B2) * (g * g)
    m_hat = m2 / (1.0 - ADAM_B1 ** ADAM_STEP)
    v_hat = v2 / (1.0 - ADAM_B2 ** ADAM_STEP)
    delta = -ADAM_LR * (m_hat / (jnp.sqrt(v_hat) + ADAM_EPS) + ADAM_WD * w)
    return delta, m2, v2


def _adamw(w, g, m, v, name, after=None):
    r, c = w.shape
    tr = r
    for cand in range(8, r + 1, 8):
        if r % cand == 0 and cand * c * 4 <= (2 << 20):
            tr = cand
    if r % 8 != 0:
        tr = r

    def body(w_ref, g_ref, m_ref, v_ref, *rest):
        d_ref, m2_ref, v2_ref = rest[-3:]
        d, m2, v2 = _adam_math(w_ref[...], g_ref[...], m_ref[...], v_ref[...])
        d_ref[...] = d
        m2_ref[...] = m2
        v2_ref[...] = v2

    spec = pl.BlockSpec((tr, c), lambda i: (i, 0))
    extra = [] if after is None else [after]
    return pl.pallas_call(
        body, name=name, grid=(r // tr,), in_specs=[spec] * 4 + [_ANY] * len(extra), out_specs=[spec] * 3,
        out_shape=[jax.ShapeDtypeStruct((r, c), F32)] * 3, compiler_params=_params(_PAR))(w, g, m, v, *extra)


_ANY = pl.BlockSpec(memory_space=pl.ANY)


def _place():
    return lax.axis_index("x"), lax.axis_index("y"), lax.axis_index("c")


def _gather_weights(packs, nchs, name):
    n = len(packs)
    halves = [p.shape[0] // 2 for p in packs]
    base = [sum(nchs[:i]) for i in range(n)]
    total = sum(nchs)
    for p, h, k in zip(packs, halves, nchs):
        assert p.shape[0] == 2 * h and h % k == 0 and (h // k) % 16 == 0

    def body(*refs):
        p_refs, g_refs, (send_sems, recv_sems) = refs[:n], refs[n:2 * n], refs[2 * n:]
        x, y, c = _place()
        sibling = (x, y, 1 - c)
        chips = [(1 - x, y), (x, 1 - y), (1 - x, 1 - y)]
        chunks = [(a, q) for a in range(n) for q in range(nchs[a])]

        def rows_of(a, pc, q):
            ch = halves[a] // nchs[a]
            return pl.ds(pl.multiple_of(pc * halves[a] + q * ch, 16), ch)

        def piece(a, px, py, pc, q):
            return g_refs[a].at[2 * px + py, rows_of(a, pc, q), :]

        def copy(k, src, dst, to):
            return pltpu.make_async_remote_copy(src_ref=src, dst_ref=dst, send_sem=send_sems.at[k],
                                                recv_sem=recv_sems.at[k], device_id=to, device_id_type=MESH)

        def sem_of(j, a, q):
            return j * total + base[a] + q

        first = {(j, a, q): copy(sem_of(j, a, q), p_refs[a].at[rows_of(a, c, q), :], piece(a, x, y, c, q), (*chip, c))
                 for j, chip in enumerate(chips) for a, q in chunks}
        for a, q in chunks:
            for j in range(3):
                first[j, a, q].start()
        passed = {(j, a, q): copy(sem_of(3 + j, a, q), piece(a, *chip, c, q), piece(a, *chip, c, q), sibling)
                  for j, chip in enumerate(chips) for a, q in chunks}
        for a, q in chunks:
            for j, chip in enumerate(chips):
                copy(sem_of(j, a, q), p_refs[a].at[rows_of(a, c, q), :], piece(a, *chip, c, q), (*chip, c)).wait_recv()
                passed[j, a, q].start()
        for a, q in chunks:
            for j, chip in enumerate(chips):
                copy(sem_of(3 + j, a, q), piece(a, *chip, 1 - c, q), piece(a, *chip, 1 - c, q), sibling).wait_recv()
        for key in first:
            first[key].wait_send()
            passed[key].wait_send()

    return pl.pallas_call(
        body, name=name, out_shape=[jax.ShapeDtypeStruct((4,) + p.shape, p.dtype) for p in packs],
        in_specs=[_ANY] * n, out_specs=[_ANY] * n,
        scratch_shapes=[pltpu.SemaphoreType.DMA((6 * total,)), pltpu.SemaphoreType.DMA((6 * total,))])(*packs)


def _swap_with_sibling(arrs, nchs, lead, name, halves=False):
    n = len(arrs)
    jobs = []
    hs = [arr.shape[-2] // (2 if halves else 1) for arr in arrs]
    for a, (h, k) in enumerate(zip(hs, nchs)):
        assert h % k == 0 and (h // k) % 16 == 0
        for s in (range(lead) if lead else [None]):
            jobs += [(a, s, q * (h // k), h // k) for q in range(k)]

    def body(*refs):
        src, dst, (send_sems, recv_sems) = refs[:n], refs[n:2 * n], refs[2 * n:]
        x, y, c = _place()

        def at(ref, s, r0, rows):
            return ref.at[pl.ds(r0, rows), :] if s is None else ref.at[s, pl.ds(r0, rows), :]

        def src_rows(a, r0):
            return pl.multiple_of((1 - c) * hs[a] + r0, 16) if halves else r0

        copies = [pltpu.make_async_remote_copy(
            src_ref=at(src[a], s, src_rows(a, r0), rows), dst_ref=at(dst[a], s, r0, rows), send_sem=send_sems.at[k],
            recv_sem=recv_sems.at[k], device_id=(x, y, 1 - c), device_id_type=MESH)
            for k, (a, s, r0, rows) in enumerate(jobs)]
        for cp in copies:
            cp.start()
        for cp in copies:
            cp.wait()

    shapes = [jax.ShapeDtypeStruct(arr.shape[:-2] + (h, arr.shape[-1]), arr.dtype) for arr, h in zip(arrs, hs)]
    return pl.pallas_call(
        body, name=name, out_shape=shapes, in_specs=[_ANY] * n, out_specs=[_ANY] * n,
        scratch_shapes=[pltpu.SemaphoreType.DMA((len(jobs),)), pltpu.SemaphoreType.DMA((len(jobs),))])(*arrs)


def _add2(full, b, core, name):
    n, rows, w = b.shape
    tr = _pick(rows, 256, 16)
    nblk = rows // tr

    def body(c_ref, a_ref, b_ref, o_ref):
        o_ref[...] = (a_ref[...].astype(F32) + b_ref[...].astype(F32)).astype(BF16)

    spec = pl.BlockSpec((1, tr, w), lambda i, j, c_ref: (i, j, 0))
    grid_spec = pltpu.PrefetchScalarGridSpec(
        num_scalar_prefetch=1, grid=(n, nblk),
        in_specs=[pl.BlockSpec((1, tr, w), lambda i, j, c_ref: (i, c_ref[0] * nblk + j, 0)), spec], out_specs=spec)
    return pl.pallas_call(
        body, name=name, grid_spec=grid_spec, out_shape=jax.ShapeDtypeStruct(b.shape, BF16),
        compiler_params=_params(_PAR, _PAR))(core, full, b)


def _reduce_chips(partials, nchs, name):
    n = len(partials)
    jobs = []
    for a, (arr, k) in enumerate(zip(partials, nchs)):
        h = arr.shape[1]
        assert h % k == 0 and (h // k) % 16 == 0
        jobs += [(a, q * (h // k), h // k) for q in range(k)]

    def body(*refs):
        src, dst, (send_sems, recv_sems) = refs[:n], refs[n:2 * n], refs[2 * n:]
        x, y, c = _place()
        chips = [(1 - x, y), (x, 1 - y), (1 - x, 1 - y)]
        copies = [pltpu.make_async_remote_copy(
            src_ref=src[a].at[2 * px + py, pl.ds(r0, rows), :], dst_ref=dst[a].at[j, pl.ds(r0, rows), :],
            send_sem=send_sems.at[3 * k + j], recv_sem=recv_sems.at[3 * k + j],
            device_id=(px, py, c), device_id_type=MESH)
            for k, (a, r0, rows) in enumerate(jobs) for j, (px, py) in enumerate(chips)]
        for cp in copies:
            cp.start()
        for cp in copies:
            cp.wait()

    return pl.pallas_call(
        body, name=name,
        out_shape=[jax.ShapeDtypeStruct((3,) + p.shape[1:], p.dtype) for p in partials],
        in_specs=[_ANY] * n, out_specs=[_ANY] * n,
        scratch_shapes=[pltpu.SemaphoreType.DMA((3 * len(jobs),)), pltpu.SemaphoreType.DMA((3 * len(jobs),))])(*partials)


_HBM = pl.BlockSpec(memory_space=pltpu.HBM)
_SEM = pl.BlockSpec(memory_space=pltpu.SEMAPHORE)
_DATAFLOW = pltpu.SideEffectType.DATAFLOW_SIDE_EFFECTING


def _ici_jobs(srcs, nchs, kind):
    jobs = []
    for a, (arr, k) in enumerate(zip(srcs, nchs)):
        h = arr.shape[0] // 2 if kind == "gather" else arr.shape[1]
        assert h % k == 0 and (h // k) % 16 == 0
        jobs += [(a, h, q * (h // k), h // k) for q in range(k)]
    return jobs


def _ici_copies(src, land, send_sems, recv_sems, jobs, kind):
    x, y, c = _place()
    chips = [(1 - x, y), (x, 1 - y), (1 - x, 1 - y)]
    copies = []
    for k, (a, h, r0, rows) in enumerate(jobs):
        for j, (px, py) in enumerate(chips):
            if kind == "gather":
                at = pl.ds(pl.multiple_of(c * h + r0, 16), rows)
                s, d = src[a].at[at, :], land[a].at[2 * x + y, at, :]
            else:
                s, d = src[a].at[2 * px + py, pl.ds(r0, rows), :], land[a].at[j, pl.ds(r0, rows), :]
            copies.append(pltpu.make_async_remote_copy(
                src_ref=s, dst_ref=d, send_sem=send_sems.at[3 * k + j], recv_sem=recv_sems.at[3 * k + j],
                device_id=(px, py, c), device_id_type=MESH))
    return copies


def _ici_start(srcs, nchs, kind, name, after=None):
    n = len(srcs)
    extra = [] if after is None else [after]
    jobs = _ici_jobs(srcs, nchs, kind)
    lead = (lambda s: (4,) + s.shape) if kind == "gather" else (lambda s: (3,) + s.shape[1:])
    lands = [lax.empty(lead(s), s.dtype) for s in srcs]

    def body(*refs):
        src, land = refs[:n], refs[n:2 * n]
        send_sems, recv_sems, token = refs[2 * n + len(extra)], refs[2 * n + len(extra) + 1], refs[-1]
        for cp in _ici_copies(src, land, send_sems, recv_sems, jobs, kind):
            cp.start()
        token[...] = jnp.zeros_like(token)

    hbm = [pltpu.HBM(a.shape, a.dtype) for a in srcs + lands]
    outs = pl.pallas_call(
        body, name=name,
        out_shape=[pltpu.SemaphoreType.DMA((3 * len(jobs),)), pltpu.SemaphoreType.DMA((3 * len(jobs),))] + hbm
        + [jax.ShapeDtypeStruct((8, 128), F32)],
        in_specs=[_HBM] * (2 * n) + [_ANY] * len(extra),
        out_specs=[_SEM, _SEM] + [_HBM] * (2 * n) + [pl.BlockSpec(memory_space=pltpu.VMEM)],
        input_output_aliases={i: 2 + i for i in range(2 * n)},
        compiler_params=pltpu.CompilerParams(has_side_effects=_DATAFLOW),
    )(*[pltpu.with_memory_space_constraint(a, pltpu.HBM) for a in srcs + lands], *extra)
    return (outs[0], outs[1], list(outs[2:2 + n]), list(outs[2 + n:2 + 2 * n]), nchs, kind), outs[-1]


def _ici_wait(handle, after, name):
    send_sems, recv_sems, srcs, lands, nchs, kind = handle
    n = len(srcs)
    jobs = _ici_jobs(srcs, nchs, kind)

    def body(*refs):
        src, land = refs[:n], refs[n:2 * n]
        for cp in _ici_copies(src, land, refs[2 * n], refs[2 * n + 1], jobs, kind):
            cp.wait_send()
            cp.wait_recv()

    outs = pl.pallas_call(
        body, name=name, out_shape=[pltpu.HBM(a.shape, a.dtype) for a in srcs + lands],
        in_specs=[_HBM] * (2 * n) + [_SEM, _SEM, _ANY], out_specs=[_HBM] * (2 * n),
        input_output_aliases={i: i for i in range(2 * n)},
        compiler_params=pltpu.CompilerParams(has_side_effects=_DATAFLOW),
    )(*srcs, *lands, send_sems, recv_sems, after)
    return list(outs[:n]), list(outs[n:])


def _pass_to_sibling(gathered, nchs, name):
    n = len(gathered)
    jobs = _ici_jobs([jax.ShapeDtypeStruct(g.shape[1:], g.dtype) for g in gathered], nchs, "gather")

    def body(*refs):
        src, dst, (send_sems, recv_sems) = refs[:n], refs[n:2 * n], refs[2 * n:]
        x, y, c = _place()
        slots = [2 * (1 - x) + y, 2 * x + (1 - y), 2 * (1 - x) + (1 - y)]

        def copy(k, j, pc):
            a, h, r0, rows = jobs[k]
            at = pl.ds(pl.multiple_of(pc * h + r0, 16), rows)
            return pltpu.make_async_remote_copy(
                src_ref=src[a].at[slots[j], at, :], dst_ref=dst[a].at[slots[j], at, :], send_sem=send_sems.at[3 * k + j],
                recv_sem=recv_sems.at[3 * k + j], device_id=(x, y, 1 - c), device_id_type=MESH)

        pairs = [(k, j) for k in range(len(jobs)) for j in range(3)]
        for k, j in pairs:
            copy(k, j, c).start()
        for k, j in pairs:
            copy(k, j, c).wait_send()
            copy(k, j, 1 - c).wait_recv()

    return pl.pallas_call(
        body, name=name, out_shape=[jax.ShapeDtypeStruct(g.shape, g.dtype) for g in gathered],
        in_specs=[_ANY] * n, out_specs=[_ANY] * n, input_output_aliases={i: i for i in range(n)},
        scratch_shapes=[pltpu.SemaphoreType.DMA((3 * len(jobs),)), pltpu.SemaphoreType.DMA((3 * len(jobs),))])(*gathered)


def _add4(own, got, name):
    rows, w = own.shape
    tr = _pick(rows, 128, 16)

    def body(a_ref, b_ref, o_ref):
        o_ref[...] = ((a_ref[...].astype(F32) + b_ref[0].astype(F32)) + b_ref[1].astype(F32)) + b_ref[2].astype(F32)

    return pl.pallas_call(
        body, name=name, grid=(rows // tr,),
        in_specs=[pl.BlockSpec((tr, w), lambda i: (i, 0)), pl.BlockSpec((3, tr, w), lambda i: (0, i, 0))],
        out_specs=pl.BlockSpec((tr, w), lambda i: (i, 0)), out_shape=jax.ShapeDtypeStruct((rows, w), F32),
        compiler_params=_params(_PAR))(own, got)


def _small_sync(gs, ws, ms, vs):
    rows = gs.shape[0]
    vmem = pl.BlockSpec(memory_space=pltpu.VMEM)

    def body(g_ref, w_ref, m_ref, v_ref, sum_ref, d_ref, m2_ref, v2_ref, buf, send_sems, recv_sems):
        x, y, c = _place()
        me = 4 * x + 2 * y + c
        buf[me] = g_ref[...]
        copies = []
        for k in range(1, 8):
            peer = (x ^ (k >> 2), y ^ ((k >> 1) & 1), c ^ (k & 1))
            copies.append(pltpu.make_async_remote_copy(
                src_ref=g_ref, dst_ref=buf.at[me], send_sem=send_sems.at[k - 1], recv_sem=recv_sems.at[k - 1],
                device_id=peer, device_id_type=MESH))
        for cp in copies:
            cp.start()
        for cp in copies:
            cp.wait()
        total = buf[0]
        for i in range(1, 8):
            total = total + buf[i]
        sum_ref[...] = total
        d, m2, v2 = _adam_math(w_ref[...], total, m_ref[...], v_ref[...])
        d_ref[...] = d
        m2_ref[...] = m2
        v2_ref[...] = v2

    shape = jax.ShapeDtypeStruct((rows, 128), F32)
    return pl.pallas_call(
        body, name="small_sync", out_shape=[shape] * 4, in_specs=[vmem] * 4, out_specs=[vmem] * 4,
        scratch_shapes=[pltpu.VMEM((8, rows, 128), F32), pltpu.SemaphoreType.DMA((7,)),
                        pltpu.SemaphoreType.DMA((7,))])(gs, ws, ms, vs)


_GROUPS = {
    "ffn1": dict(cols=("ffn1_w_in", 1408), rows=(("ffn1_w_out", 704, 704),), chunks=(8, 2)),
    "ffn2": dict(cols=("ffn2_w_in", 1408), rows=(("ffn2_w_out", 704, 704),), chunks=(8, 2)),
    "mixer_in": dict(cols=("w_in", 3080), rows=(("gdn_conv_w", CONV_K, 128),), chunks=(8, 1)),
    "mixer_out": dict(cols=None, chunks=(4,),
                      rows=(("w_branch_hgrn", 256, 256), ("w_branch_gdn", 512, 512), ("w_out", 256, 256))),
}


def _group_names(group):
    return ((group["cols"][0],) if group["cols"] else ()) + tuple(r[0] for r in group["rows"])


_BIG_NAMES = tuple(n for g in _GROUPS.values() for n in _group_names(g))


def _pack(parts, lead, group):
    ax = len(lead)
    rows = []
    for n, r, padded in group["rows"]:
        p = parts[n]
        if padded != r:
            p = jnp.tile(p, (1,) * ax + (padded // r, 1))
        rows.append(p)
    stacked = rows[0] if len(rows) == 1 else jnp.concatenate(rows, axis=ax)
    return ([parts[group["cols"][0]]] if group["cols"] else []) + [stacked]


def _unpack(packs, group):
    out, off = ({group["cols"][0]: packs[0]} if group["cols"] else {}), 0
    for n, r, padded in group["rows"]:
        out[n] = packs[-1][..., off:off + r, :]
        off += padded
    return out


def _is_col_sharded(name):
    return name in ("ffn1_w_in", "ffn2_w_in", "w_in", "gdn_conv_w")


def _full_from_shards(name, g):
    if _is_col_sharded(name):
        return jnp.transpose(g, (1, 0, 2)).reshape(g.shape[1], -1)
    return g.reshape(-1, g.shape[2])


def _shards_from_full(name, full):
    if _is_col_sharded(name):
        return jnp.transpose(full.reshape(full.shape[0], 4, -1), (1, 0, 2))
    return full.reshape(4, -1, full.shape[1])


_SMALL = (("ffn1_norm", 8), ("mix_norm", 8), ("hgrn_lb_logits", 16), ("hgrn_out_norm", 8), ("gdn_a_log", 8),
          ("gdn_dt_bias", 8), ("gdn_out_norm", 8), ("ffn2_norm", 8), ("final_norm", 8), ("loss", 8))
_SMALL_ROWS = sum(r for _, r in _SMALL)


def _pack_small(parts):
    out = []
    for name, rows in _SMALL:
        p = parts[name].reshape(-1).astype(F32)
        if p.shape[0] <= 128:
            if p.shape[0] < 128:
                p = jnp.concatenate([p, jnp.zeros((128 - p.shape[0],), F32)])
            p = jnp.broadcast_to(p.reshape(1, 128), (rows, 128))
        out.append(p.reshape(rows, 128))
    return jnp.concatenate(out, axis=0)


def _unpack_small(packed, shapes):
    out, off = {}, 0
    for name, rows in _SMALL:
        n = int(np.prod(shapes[name]))
        out[name] = packed[off:off + rows].reshape(-1)[:n].reshape(shapes[name])
        off += rows
    return out


def _ffn_fwd(x, gain, w_in, w_out, tag):
    n = _rmsnorm_fwd(x, gain, tag + "_norm")
    a, b, hm = _ffn_in_act(n, w_in, tag + "_in")
    out = _mm(hm, w_out, alpha=0.5, res=x, name=tag + "_out")
    return out, (n, a, b)


def _ffn_bwd(x, gain, w_in, w_out, saved, dout, dout_bf, tag):
    n, a, b = saved
    da, db, hm = _ffn_dact(dout_bf, w_out, a, b, tag + "_dact")
    dw_out = _mm(hm, dout_bf, ta=True, alpha=0.5, out_dtype=BF16, name=tag + "_dwout")
    dwa = _mm(n, da, ta=True, out_dtype=BF16, name=tag + "_dwin_a")
    dwb = _mm(n, db, ta=True, out_dtype=BF16, name=tag + "_dwin_b")
    half = D_FF // 2
    dw_in = jnp.stack([dwa[:, :half], dwa[:, half:], dwb[:, :half], dwb[:, half:]])
    dn = _mm(da, w_in, tb=True, name=tag + "_dnorm_a")
    dn = _mm(db, w_in, tb=True, res=dn, b_from=D_FF, name=tag + "_dnorm_b")
    dx, dx_bf, dgain = _rmsnorm_bwd(x, gain, dn, dout, tag + "_dx")
    return dx, dx_bf, dgain, dw_in, dw_out


def _pad_lanes(v):
    return jnp.concatenate([v.reshape(1, -1), jnp.zeros((1, HEAD - v.size), F32)], axis=1)


def _local_step(x, tgt, small, exchange):
    hg_c = _hg_consts()
    gd_c = _gd_consts()
    alog = _pad_lanes(small["gdn_a_log"])
    dtb = _pad_lanes(small["gdn_dt_bias"])
    logits = small["hgrn_lb_logits"]
    hg_gain = small["hgrn_out_norm"].reshape(1, HEAD)
    gd_gain = small["gdn_out_norm"].reshape(1, HEAD)
    g1, gm, g2 = small["ffn1_norm"].reshape(1, -1), small["mix_norm"].reshape(1, -1), small["ffn2_norm"].reshape(1, -1)
    gf = small["final_norm"].reshape(1, -1)
    qscale = HEAD ** -0.5

    w1 = exchange.weights("ffn1")
    started = exchange.prefetch("mixer_in")
    h1, ffn1_saved = _ffn_fwd(x, g1 + started, w1["ffn1_w_in"], w1["ffn1_w_out"], "ffn1")
    u = _rmsnorm_fwd(h1, gm, "mix_norm")
    w = exchange.weights("mixer_in", after=u)
    arrived = w["gdn_conv_w"]
    started = exchange.prefetch("mixer_out", arrived) + exchange.prefetch("ffn2", arrived)
    seg, off = {}, 0
    for name, size in zip(IN_NAMES, IN_SIZES):
        seg[name] = w["w_in"][:, off:off + size]
        off += size
    w_gab = jnp.concatenate([seg["ga"], seg["gb"], jnp.zeros((D_MODEL, HEAD - 32), BF16)], axis=1)
    big_segs = [n for n in IN_NAMES if n not in ("ga", "gb")]
    conv8 = jnp.concatenate([w["gdn_conv_w"].astype(F32), jnp.zeros((8 - CONV_K, 4096), F32)], axis=0)
    conv_q, conv_k, conv_v = conv8[:, :1024], conv8[:, 1024:2048], conv8[:, 2048:]
    w_main = jnp.concatenate([seg[n] for n in big_segs], axis=1)
    proj = _mm(u, w_main, name="proj", tm_max=2048)
    pr, off = {}, 0
    for n in big_segs:
        pr[n] = _view(proj, off, seg[n].shape[1])
        off += seg[n].shape[1]
    gab = _mm(u, w_gab, name="proj_gab")
    oh_raw, oh, s_h = _hgrn_fwd(pr["hq"], pr["hf"], pr["hi"], pr["hg"], logits, hg_gain + started, hg_c)
    qn = _conv_fwd(pr["gq"], conv_q, qscale, "conv_q")
    kn = _conv_fwd(pr["gk"], conv_k, 1.0, "conv_k")
    cv = _conv_fwd(pr["gv"], conv_v, None, "conv_v")
    og_raw, og, s_g, t_g = _gdn_fwd(qn, kn, cv, gab, pr["gz"], alog, dtb, gd_gain, gd_c)
    wo = exchange.weights("mixer_out", after=og)
    yh = _mm(oh, wo["w_branch_hgrn"], out_dtype=BF16, name="branch_h")
    yg = _mm(og, wo["w_branch_gdn"], out_dtype=BF16, name="branch_g")
    ym = _merge_fwd(yh, yg, pr["gate_h"], pr["gate_g"])
    h2 = _mm(ym, wo["w_out"], res=h1, name="mix_out")
    w2 = exchange.weights("ffn2", after=h2)
    h3, ffn2_saved = _ffn_fwd(h2, g2, w2["ffn2_w_in"], w2["ffn2_w_out"], "ffn2")
    loss, dh3, dh3_bf, d_gf = _final_loss(h3, gf, tgt)

    dh2, dh2_bf, d_g2, d_f2in, d_f2out = _ffn_bwd(h2, g2, w2["ffn2_w_in"], w2["ffn2_w_out"], ffn2_saved, dh3, dh3_bf,
                                                  "ffn2")
    started = exchange.reduce("ffn2", {"ffn2_w_in": d_f2in, "ffn2_w_out": d_f2out}, behind=True)
    dym = _mm(dh2_bf, wo["w_out"], tb=True, name="d_merge")
    d_wout = _mm(ym, dh2_bf, ta=True, out_dtype=BF16, name="d_w_out")
    dproj = lax.empty((x.shape[0], w_main.shape[1]), BF16)
    dyh, dyg, dproj = _merge_bwd(dym, yh, yg, pr["gate_h"], pr["gate_g"], _into(dproj, pr["gate_h"][1], 2 * D_MODEL))
    d_wbh = _mm(oh, dyh, ta=True, out_dtype=BF16, name="d_w_branch_h")
    d_wbg = _mm(og, dyg, ta=True, out_dtype=BF16, name="d_w_branch_g")
    started = started + exchange.reduce("mixer_out", {"w_branch_hgrn": d_wbh, "w_branch_gdn": d_wbg, "w_out": d_wout},
                                        behind=True)
    doh = _mm(dyh, wo["w_branch_hgrn"], tb=True, name="d_oh")
    dog = _mm(dyg, wo["w_branch_gdn"], tb=True, name="d_og")
    dproj, d_hg_gain, d_lb0 = _hgrn_bwd(pr["hq"], pr["hf"], pr["hi"], pr["hg"], logits, hg_gain + started, oh_raw,
                                        s_h, doh, hg_c, _into(dproj, pr["hq"][1], 4 * D_MODEL))
    d_qn, d_kn, d_cv, d_gab_wide, dproj, gd_small = _gdn_bwd(qn, kn, cv, gab, pr["gz"], alog, dtb, gd_gain, og_raw,
                                                             s_g, t_g, dog, gd_c, _into(dproj, *pr["gz"][1:]))
    d_gab = _fold_groups(d_gab_wide)
    dc_q, dwc_q = _conv_bwd_a(pr["gq"], conv_q, d_qn, qscale, "dconv_q")
    dc_k, dwc_k = _conv_bwd_a(pr["gk"], conv_k, d_kn, 1.0, "dconv_k")
    dc_v, dwc_v = _conv_bwd_a(pr["gv"], conv_v, d_cv, None, "dconv_v")
    dproj = _conv_bwd_b(dc_q, conv_q, "dconvx_q", _into(dproj, *pr["gq"][1:]))
    dproj = _conv_bwd_b(dc_k, conv_k, "dconvx_k", _into(dproj, *pr["gk"][1:]))
    dproj = _conv_bwd_b(dc_v, conv_v, "dconvx_v", _into(dproj, *pr["gv"][1:]))
    du =_mm(d_gab, w_gab, tb=True, name="du_gab")
    du = _mm(dproj, w_main, tb=True, res=du, name="du")
    d_wmain = _mm(u, dproj, ta=True, out_dtype=BF16, name="dw_main")
    d_wgab = _mm(u, d_gab, ta=True, out_dtype=BF16, name="dw_gab")
    cut = IN_WIDTH // 4
    d_win = jnp.stack([d_wmain[:, :cut], d_wmain[:, cut:2 * cut],
                       jnp.concatenate([d_wmain[:, 2 * cut:8192], d_wgab[:, :32], d_wmain[:, 8192:3 * cut - 32]], axis=1),
                       d_wmain[:, 3 * cut - 32:]])
    d_conv = jnp.concatenate([dwc_q[:CONV_K], dwc_k[:CONV_K], dwc_v[:CONV_K]], axis=1).astype(BF16)
    started = exchange.reduce("mixer_in", {"w_in": d_win, "gdn_conv_w": d_conv}, behind=True)
    dh1, dh1_bf, d_gm = _rmsnorm_bwd(h1, gm + started, du, dh2, "mix_dnorm")
    dx, _, d_g1, d_f1in, d_f1out = _ffn_bwd(x, g1, w1["ffn1_w_in"], w1["ffn1_w_out"], ffn1_saved, dh1, dh1_bf, "ffn1")
    exchange.reduce("ffn1", {"ffn1_w_in": d_f1in, "ffn1_w_out": d_f1out}, behind=True)
    d_lb0 = d_lb0.reshape(1, -1)
    sm = {"ffn1_norm": d_g1, "mix_norm": d_gm, "hgrn_lb_logits": jnp.concatenate([d_lb0, -d_lb0], axis=0),
          "hgrn_out_norm": d_hg_gain, "gdn_a_log": gd_small[2, :16], "gdn_dt_bias": gd_small[1, :16],
          "gdn_out_norm": gd_small[0], "ffn2_norm": d_g2, "final_norm": d_gf, "loss": loss[0, :1]}
    return dx, sm


class _Exchange:
    def __init__(self, wts):
        self.wts = wts
        xi, yi, ci = _place()
        self.chip = 2 * xi + yi
        self.south = ci == 0
        self.core = ci.reshape(1).astype(jnp.int32)
        self.mine = {}
        self.coming = {}
        self.going = {}

    def _packs(self, tag):
        group = _GROUPS[tag]
        return _pack({n: self.wts[n][0].astype(BF16) for n in _group_names(group)}, (), group)

    def prefetch(self, tag, after=None):
        packs = self._packs(tag)
        handle, token = _ici_start(packs, _GROUPS[tag]["chunks"], "gather", "gather_start_" + tag, after)
        self.coming[tag] = handle
        return token[0:1, 0:1]

    def weights(self, tag, after=None):
        group = _GROUPS[tag]
        if tag in self.coming:
            packs, halves = _ici_wait(self.coming.pop(tag), after, "gather_wait_" + tag)
            others = _pass_to_sibling(halves, group["chunks"], "gather_pass_" + tag)
        else:
            packs = self._packs(tag)
            others = _gather_weights(packs, group["chunks"], "gather_" + tag)
        whole = [lax.dynamic_update_index_in_dim(g, p, self.chip, 0) for g, p in zip(others, packs)]
        gathered = _unpack(whole, group)
        return {n: _full_from_shards(n, gathered[n]) for n in _group_names(group)}

    def reduce(self, tag, grads, behind=False):
        group = _GROUPS[tag]
        shards = {n: (grads[n] if grads[n].ndim == 3 else _shards_from_full(n, grads[n])) for n in _group_names(group)}
        gpacks = _pack(shards, (4,), group)
        got = _swap_with_sibling(gpacks, group["chunks"], 4, "reduce_pair_" + tag, halves=True)
        sums = [_add2(a, b, self.core, "add_pair_%s_%d" % (tag, i)) for i, (a, b) in enumerate(zip(gpacks, got))]
        if behind:
            handle, token = _ici_start(sums, group["chunks"], "reduce", "reduce_start_" + tag)
            self.going[tag] = handle
            self.token = token
            return token[0:1, 0:1]
        self._add_chips(tag, sums, _reduce_chips(sums, group["chunks"], "reduce_chips_" + tag))
        return None

    def _add_chips(self, tag, sums, from_chips):
        self.mine[tag] = [_add4(lax.dynamic_index_in_dim(s, self.chip, axis=0, keepdims=False), f,
                                "add_chips_%s_%d" % (tag, i)) for i, (s, f) in enumerate(zip(sums, from_chips))]

    def finish(self, tags, after):
        for tag in tags:
            if tag in self.going:
                self._add_chips(tag, *_ici_wait(self.going.pop(tag), after, "reduce_wait_" + tag))
        mine = [a for t in tags for a in self.mine[t]]
        nchs = [k for t in tags for k in _GROUPS[t]["chunks"]]
        theirs = _swap_with_sibling(mine, nchs, 0, "share_pair_" + tags[0])
        whole = [jnp.concatenate([jnp.where(self.south, a, b), jnp.where(self.south, b, a)], axis=0)
                 for a, b in zip(mine, theirs)]
        reduced, at = {}, 0
        for t in tags:
            n = len(self.mine[t])
            reduced.update(_unpack(whole[at:at + n], _GROUPS[t]))
            at += n
        return reduced


_WEIGHTS = ("ffn1_norm", "ffn1_w_in", "ffn1_w_out", "mix_norm", "w_in", "hgrn_lb_logits", "hgrn_out_norm",
            "gdn_conv_w", "gdn_a_log", "gdn_dt_bias", "gdn_out_norm", "w_branch_hgrn", "w_branch_gdn", "w_out",
            "ffn2_norm", "ffn2_w_in", "ffn2_w_out", "final_norm")


def kernel(x, ffn1_norm, ffn1_w_in, ffn1_w_out, mix_norm, w_in, hgrn_lb_logits, hgrn_out_norm, gdn_conv_w, gdn_a_log, gdn_dt_bias, gdn_out_norm, w_branch_hgrn, w_branch_gdn, w_out, ffn2_norm, ffn2_w_in, ffn2_w_out, final_norm, loss_target, m_ffn1_norm, m_ffn1_w_in, m_ffn1_w_out, m_mix_norm, m_w_in, m_hgrn_lb_logits, m_hgrn_out_norm, m_gdn_conv_w, m_gdn_a_log, m_gdn_dt_bias, m_gdn_out_norm, m_w_branch_hgrn, m_w_branch_gdn, m_w_out, m_ffn2_norm, m_ffn2_w_in, m_ffn2_w_out, m_final_norm, v_ffn1_norm, v_ffn1_w_in, v_ffn1_w_out, v_mix_norm, v_w_in, v_hgrn_lb_logits, v_hgrn_out_norm, v_gdn_conv_w, v_gdn_a_log, v_gdn_dt_bias, v_gdn_out_norm, v_w_branch_hgrn, v_w_branch_gdn, v_w_out, v_ffn2_norm, v_ffn2_w_in, v_ffn2_w_out, v_final_norm):
    args = dict(locals())
    wts = {n: args[n] for n in _WEIGHTS}
    moms = {n: args["m_" + n] for n in _WEIGHTS}
    vars_ = {n: args["v_" + n] for n in _WEIGHTS}

    small = {n: wts[n].astype(F32) for n in _WEIGHTS if n not in _BIG_NAMES}
    exchange = _Exchange(wts)
    dx, small_grads = _local_step(x[0], loss_target[0], small, exchange)

    out_g, out_d, out_m, out_v = {}, {}, {}, {}

    def update(tags, reduced, after):
        for t in tags:
            for n in _group_names(_GROUPS[t]):
                shape = wts[n].shape
                w2 = wts[n].reshape(shape[-2], shape[-1])
                g2 = reduced[n]
                d, m2, v2 = _adamw(w2, g2, moms[n].reshape(w2.shape), vars_[n].reshape(w2.shape), "adamw_" + n, after)
                out_g[n], out_d[n], out_m[n], out_v[n] = (g2.reshape(shape), d.reshape(shape), m2.reshape(shape),
                                                          v2.reshape(shape))
                after = v2
        return after

    early = ("ffn2", "mixer_out", "mixer_in")
    done = update(early, exchange.finish(early, after=dx), exchange.token)
    update(("ffn1",), exchange.finish(("ffn1",), after=done), None)

    small_names = [n for n, _ in _SMALL]
    zero = jnp.zeros((1,), F32)
    shapes = {n: (wts[n].shape if n != "loss" else (1,)) for n in small_names}
    sums, sd, sm_, sv = _small_sync(
        _pack_small(small_grads),
        _pack_small({n: (wts[n] if n != "loss" else zero) for n in small_names}),
        _pack_small({n: (moms[n] if n != "loss" else zero) for n in small_names}),
        _pack_small({n: (vars_[n] if n != "loss" else zero) for n in small_names}))
    sg_u, sd_u, sm_u, sv_u = (_unpack_small(p, shapes) for p in (sums, sd, sm_, sv))
    for n in small_names:
        if n != "loss":
            out_g[n], out_d[n], out_m[n], out_v[n] = sg_u[n], sd_u[n], sm_u[n], sv_u[n]
    loss = sg_u["loss"].reshape(())

    return (loss, dx[None], *[out_g[n] for n in _WEIGHTS], *[out_d[n] for n in _WEIGHTS],
            *[out_m[n] for n in _WEIGHTS], *[out_v[n] for n in _WEIGHTS])
```

```python
import numpy as np

import jax
import jax.numpy as jnp
from jax import lax
from jax.experimental import pallas as pl
from jax.experimental.pallas import tpu as pltpu

F32 = jnp.float32
BF16 = jnp.bfloat16

D_MODEL = 1024
D_FF = 2816
CHUNK = 64
HEAD = 128
HG_HEADS = 8
GD_HEADS = 16
HPS = 8
MM_TM = 1408
MM_TN = 1024
MM_TK = 2048
VMEM_LIMIT = 48 * 1024 * 1024
ROW_TILE = 512
EPS = 1e-6
CONV_K = 4
IN_NAMES = ("hq", "hf", "hi", "hg", "gq", "gk", "gv", "ga", "gb", "gz", "gate_h", "gate_g")
IN_SIZES = (1024, 1024, 1024, 1024, 1024, 1024, 2048, 16, 16, 2048, 1024, 1024)
IN_WIDTH = sum(IN_SIZES)

ADAM_LR = 0.001
ADAM_B1 = 0.9
ADAM_B2 = 0.999
ADAM_EPS = 1e-08
ADAM_WD = 0.01
ADAM_STEP = 10

MESH = pl.DeviceIdType.MESH
_ARB = "arbitrary"
_PAR = "parallel"


def _bf(x):
    return x.astype(BF16)


def _dot(a, b):
    return jnp.dot(_bf(a), _bf(b), preferred_element_type=F32)


def _dot_nt(a, b):
    return lax.dot_general(_bf(a), _bf(b), (((1,), (1,)), ((), ())), preferred_element_type=F32)


def _dot_tn(a, b):
    return lax.dot_general(_bf(a), _bf(b), (((0,), (0,)), ((), ())), preferred_element_type=F32)


def _sigmoid(x):
    return jax.nn.sigmoid(x)


def _silu(x):
    return x * _sigmoid(x)


def _dsilu(x):
    s = _sigmoid(x)
    return s * (1.0 + x * (1.0 - s))


def _softplus(x):
    return jnp.maximum(x, 0.0) + jnp.log(1.0 + jnp.exp(-jnp.abs(x)))


def _rowsum(x):
    return jnp.sum(x, axis=1, keepdims=True)


def _col_to_row(col, eye):
    return jnp.sum(eye * col, axis=0, keepdims=True)


def _row_to_col(row, eye):
    return jnp.sum(eye * row, axis=1, keepdims=True)


def _pick(dim, pref, unit=128):
    if dim <= pref:
        return dim
    t = pref
    while t >= unit:
        if dim % t == 0:
            return t
        t -= unit
    return dim


def _params(*sem):
    return pltpu.CompilerParams(dimension_semantics=tuple(sem), vmem_limit_bytes=VMEM_LIMIT)


def _mm(a, b, *, ta=False, tb=False, alpha=1.0, res=None, out_dtype=F32, name="mm", b_from=0, tm_max=MM_TM):
    m = a.shape[1] if ta else a.shape[0]
    k = a.shape[0] if ta else a.shape[1]
    n = b.shape[0] if tb else b.shape[1]
    assert b_from + k <= (b.shape[1] if tb else b.shape[0])
    tm, tn, tk = _pick(m, tm_max), _pick(n, MM_TN), _pick(k, MM_TK)
    if tn < MM_TN < n and n % MM_TM == 0:
        tn = MM_TM
    nk = k // tk
    assert b_from % tk == 0
    b0 = b_from // tk
    a_spec = pl.BlockSpec((tk, tm), lambda i, j, l: (l, i)) if ta else pl.BlockSpec((tm, tk), lambda i, j, l: (i, l))
    b_spec = (pl.BlockSpec((tn, tk), lambda i, j, l: (j, b0 + l)) if tb
              else pl.BlockSpec((tk, tn), lambda i, j, l: (b0 + l, j)))
    o_spec = pl.BlockSpec((tm, tn), lambda i, j, l: (i, j))
    dims = (((0 if ta else 1,), (1 if tb else 0,)), ((), ()))
    has_res = res is not None

    def finish(r, r_ref, o_ref):
        if alpha != 1.0:
            r = r * alpha
        if has_res:
            r = r + r_ref[...]
        o_ref[...] = r.astype(out_dtype)

    def body(*refs):
        a_ref, b_ref = refs[0], refs[1]
        r_ref = refs[2] if has_res else None
        o_ref = refs[3] if has_res else refs[2]
        part = lax.dot_general(_bf(a_ref[...]), _bf(b_ref[...]), dims, preferred_element_type=F32)
        if nk == 1:
            finish(part, r_ref, o_ref)
            return
        acc = refs[-1]
        step = pl.program_id(2)

        @pl.when(step == 0)
        def _():
            acc[...] = part

        @pl.when(step != 0)
        def _():
            acc[...] += part

        @pl.when(step == nk - 1)
        def _():
            finish(acc[...], r_ref, o_ref)

    ins = [a, b] + ([res] if has_res else [])
    in_specs = [a_spec, b_spec] + ([o_spec] if has_res else [])
    return pl.pallas_call(
        body, name=name, grid=(m // tm, n // tn, nk), in_specs=in_specs, out_specs=o_spec,
        out_shape=jax.ShapeDtypeStruct((m, n), out_dtype),
        scratch_shapes=[pltpu.VMEM((tm, tn), F32)] if nk > 1 else [],
        compiler_params=_params(_PAR, _PAR, _ARB))(*ins)


def _row_spec(tr, w):
    return pl.BlockSpec((tr, w), lambda i: (i, 0))


def _full_spec(shape):
    return pl.BlockSpec(shape, lambda i: tuple(0 for _ in shape))


def _view(arr, off, width):
    return arr, off, width


def _view_rows(view, tr):
    _, off, width = view
    assert off % width == 0
    return pl.BlockSpec((tr, width), lambda i: (i, off // width))


def _view_tile(view, rows, bw, cidx=lambda c: c):
    _, off, width = view
    assert off % bw == 0 and width % bw == 0
    return pl.BlockSpec((rows, bw), lambda c, g: (cidx(c), off // bw + g))


def _rmsnorm_fwd(x, g, name):
    t, d = x.shape
    tr = _pick(t, ROW_TILE, 8)

    def body(x_ref, g_ref, o_ref):
        xv = x_ref[...]
        r = lax.rsqrt(jnp.mean(xv * xv, axis=1, keepdims=True) + EPS)
        o_ref[...] = (xv * r * g_ref[...]).astype(BF16)

    return pl.pallas_call(
        body, name=name, grid=(t // tr,), in_specs=[_row_spec(tr, d), _full_spec((1, d))],
        out_specs=_row_spec(tr, d), out_shape=jax.ShapeDtypeStruct((t, d), BF16),
        compiler_params=_params(_PAR))(x, g)


def _rmsnorm_bwd(x, g, dn, res, name):
    t, d = x.shape
    tr = _pick(t, ROW_TILE, 8)

    def body(x_ref, g_ref, dn_ref, r_ref, dx_ref, dxb_ref, dg_ref):
        @pl.when(pl.program_id(0) == 0)
        def _():
            dg_ref[...] = jnp.zeros_like(dg_ref)

        xv = x_ref[...]
        r = lax.rsqrt(jnp.mean(xv * xv, axis=1, keepdims=True) + EPS)
        xh = xv * r
        dy = dn_ref[...]
        dg_ref[...] += jnp.sum(dy * xh, axis=0, keepdims=True)
        dxh = dy * g_ref[...]
        dx = r_ref[...] + r * (dxh - xh * jnp.mean(dxh * xh, axis=1, keepdims=True))
        dx_ref[...] = dx
        dxb_ref[...] = dx.astype(BF16)

    return pl.pallas_call(
        body, name=name, grid=(t // tr,),
        in_specs=[_row_spec(tr, d), _full_spec((1, d)), _row_spec(tr, d), _row_spec(tr, d)],
        out_specs=[_row_spec(tr, d), _row_spec(tr, d), _full_spec((1, d))],
        out_shape=[jax.ShapeDtypeStruct((t, d), F32), jax.ShapeDtypeStruct((t, d), BF16),
                   jax.ShapeDtypeStruct((1, d), F32)],
        compiler_params=_params(_ARB))(x, g, dn, res)


FFN_TN = 1408
FFN_TM = 512


def _ffn_in_act(n, w_in, name):
    t, d = n.shape
    tm = _pick(t, FFN_TM)
    nf = D_FF // FFN_TN

    def body(n_ref, wa_ref, wb_ref, a_ref, b_ref, hm_ref):
        nv = n_ref[...]
        a = jnp.dot(nv, wa_ref[...], preferred_element_type=F32)
        b = jnp.dot(nv, wb_ref[...], preferred_element_type=F32)
        a_ref[...] = a.astype(BF16)
        b_ref[...] = b.astype(BF16)
        hm_ref[...] = (_silu(a) * b).astype(BF16)

    tile = pl.BlockSpec((tm, FFN_TN), lambda i, j: (i, j))
    return pl.pallas_call(
        body, name=name, grid=(t // tm, nf),
        in_specs=[pl.BlockSpec((tm, d), lambda i, j: (i, 0)), pl.BlockSpec((d, FFN_TN), lambda i, j: (0, j)),
                  pl.BlockSpec((d, FFN_TN), lambda i, j: (0, nf + j))],
        out_specs=[tile, tile, tile], out_shape=[jax.ShapeDtypeStruct((t, D_FF), BF16)] * 3,
        compiler_params=_params(_PAR, _PAR))(n, w_in, w_in)


def _ffn_dact(dout, w_out, a, b, name):
    t, d = dout.shape
    tm = _pick(t, FFN_TM)

    def body(do_ref, w_ref, a_ref, b_ref, da_ref, db_ref, hm_ref):
        dh = 0.5 * _dot_nt(do_ref[...], w_ref[...])
        av = a_ref[...].astype(F32)
        bv = b_ref[...].astype(F32)
        sg = _sigmoid(av)
        sa = av * sg
        da_ref[...] = (dh * bv * (sg * (1.0 + av * (1.0 - sg)))).astype(BF16)
        db_ref[...] = (dh * sa).astype(BF16)
        hm_ref[...] = (sa * bv).astype(BF16)

    tile = pl.BlockSpec((tm, FFN_TN), lambda i, j: (i, j))
    return pl.pallas_call(
        body, name=name, grid=(t // tm, D_FF // FFN_TN),
        in_specs=[pl.BlockSpec((tm, d), lambda i, j: (i, 0)), pl.BlockSpec((FFN_TN, d), lambda i, j: (j, 0)), tile, tile],
        out_specs=[tile, tile, tile], out_shape=[jax.ShapeDtypeStruct((t, D_FF), BF16)] * 3,
        compiler_params=_params(_PAR, _PAR))(dout, w_out, a, b)


def _merge_fwd(yh, yg, gh, gg):
    t, d = yh.shape
    tr = _pick(t, ROW_TILE, 8)

    def body(yh_ref, yg_ref, gh_ref, gg_ref, o_ref):
        o_ref[...] = (_sigmoid(gh_ref[...]) * yh_ref[...] + _sigmoid(gg_ref[...]) * yg_ref[...]).astype(BF16)

    return pl.pallas_call(
        body, name="merge_fwd", grid=(t // tr,),
        in_specs=[_row_spec(tr, d), _row_spec(tr, d), _view_rows(gh, tr), _view_rows(gg, tr)],
        out_specs=_row_spec(tr, d),
        out_shape=jax.ShapeDtypeStruct((t, d), BF16), compiler_params=_params(_PAR))(yh, yg, gh[0], gg[0])


def _into(dproj, off, width):
    return dproj, off, width


def _merge_bwd(dy, yh, yg, gh, gg, into):
    t, d = yh.shape
    tr = _pick(t, ROW_TILE, 8)
    dproj, off, width = into
    assert width == 2 * d and off % width == 0

    def body(dy_ref, yh_ref, yg_ref, gh_ref, gg_ref, _, dyh_ref, dyg_ref, dg_ref):
        dyv = dy_ref[...]
        sh = _sigmoid(gh_ref[...])
        sg = _sigmoid(gg_ref[...])
        dyh_ref[...] = (dyv * sh).astype(BF16)
        dyg_ref[...] = (dyv * sg).astype(BF16)
        dg_ref[:, :d] = (dyv * yh_ref[...] * sh * (1.0 - sh)).astype(BF16)
        dg_ref[:, d:] = (dyv * yg_ref[...] * sg * (1.0 - sg)).astype(BF16)

    return pl.pallas_call(
        body, name="merge_bwd", grid=(t // tr,),
        in_specs=[_row_spec(tr, d)] * 3 + [_view_rows(gh, tr), _view_rows(gg, tr), _ANY],
        out_specs=[_row_spec(tr, d)] * 2 + [pl.BlockSpec((tr, width), lambda i: (i, off // width))],
        out_shape=[jax.ShapeDtypeStruct((t, d), BF16)] * 2 + [jax.ShapeDtypeStruct(dproj.shape, dproj.dtype)],
        input_output_aliases={5: 2},
        compiler_params=_params(_PAR))(dy, yh, yg, gh[0], gg[0], dproj)


def _final_loss(h, g, tgt):
    t, d = h.shape
    tr = _pick(t, ROW_TILE, 8)

    def body(h_ref, g_ref, t_ref, loss_ref, dh_ref, dhb_ref, dg_ref):
        @pl.when(pl.program_id(0) == 0)
        def _():
            dg_ref[...] = jnp.zeros_like(dg_ref)
            loss_ref[...] = jnp.zeros_like(loss_ref)

        xv = h_ref[...]
        gv = g_ref[...]
        r = lax.rsqrt(jnp.mean(xv * xv, axis=1, keepdims=True) + EPS)
        xh = xv * r
        err = xh * gv - t_ref[...]
        loss_ref[...] += 0.5 * jnp.sum(jnp.mean(err * err, axis=1, keepdims=True), axis=0, keepdims=True)
        dy = err * (1.0 / d)
        dg_ref[...] += jnp.sum(dy * xh, axis=0, keepdims=True)
        dxh = dy * gv
        dh = r * (dxh - xh * jnp.mean(dxh * xh, axis=1, keepdims=True))
        dh_ref[...] = dh
        dhb_ref[...] = dh.astype(BF16)

    return pl.pallas_call(
        body, name="final_loss", grid=(t // tr,),
        in_specs=[_row_spec(tr, d), _full_spec((1, d)), _row_spec(tr, d)],
        out_specs=[_full_spec((1, 128)), _row_spec(tr, d), _row_spec(tr, d), _full_spec((1, d))],
        out_shape=[jax.ShapeDtypeStruct((1, 128), F32), jax.ShapeDtypeStruct((t, d), F32),
                   jax.ShapeDtypeStruct((t, d), BF16), jax.ShapeDtypeStruct((1, d), F32)],
        compiler_params=_params(_ARB))(h, g, tgt)


def _hg_consts():
    c = CHUNK
    t = np.arange(c)
    mats, masks = [], []
    for lvl in range(6):
        m = 1 << lvl
        blk = t // m
        mat = np.zeros((c, c), np.float32)
        for tt in range(c):
            b = blk[tt]
            if b % 2 == 1:
                mat[tt, b * m:tt + 1] = 1.0
            else:
                mat[tt, tt + 1:(b + 1) * m] = 1.0
        mats.append(mat)
        same = (t[:, None] // (2 * m)) == (t[None, :] // (2 * m))
        masks.append((same & (blk[:, None] % 2 == 1) & (blk[None, :] % 2 == 0)).astype(np.float32))
    pre = np.tril(np.ones((c, c), np.float32))
    suf = np.triu(np.ones((c, c), np.float32), 1)
    mstack = np.concatenate(mats + [pre, suf], 0)
    masks.append(np.eye(c, dtype=np.float32))
    return (jnp.asarray(mstack, BF16), jnp.asarray(mstack.T.copy(), BF16), jnp.asarray(np.stack(masks), F32),
            jnp.asarray(np.eye(HEAD, dtype=np.float32)))


def _gd_consts():
    c = CHUNK
    incl = np.tril(np.ones((c, c), np.float32))
    strict = np.tril(np.ones((c, c), np.float32), -1)
    eye = np.eye(c, dtype=np.float32)
    masks = np.stack([incl, strict, eye, incl.T.copy()])
    return jnp.asarray(incl, BF16), jnp.asarray(incl.T.copy(), BF16), jnp.asarray(masks, F32)


def _chunks_per_step(nc):
    for cb in (32 // HPS, 2, 1):
        if nc % cb == 0:
            return cb
    return 1


def _hg_prep(hq, hf, lg):
    lb = _sigmoid(lg[0:1, :] - lg[1:2, :])
    sg = _sigmoid(hf)
    sgn = _sigmoid(-hf)
    f = lb + (1.0 - lb) * sg
    lf = jnp.log(f)
    kk = (1.0 - lb) * sgn
    q = _silu(hq) * (HEAD ** -0.5)
    return lb, sg, sgn, f, lf, kk, q


def _mx_each(m, xs):
    hi, lo = _split2_each(xs)
    prods = [jnp.dot(m, jnp.concatenate([h, l], axis=1), preferred_element_type=F32) for h, l in zip(hi, lo)]
    return [p[:, :HEAD] + p[:, HEAD:] for p in prods]


def _hg_scaled(x, ex):
    xb = [_bf(a) for a in x]
    eb = [_bf(e[:6 * CHUNK]) for e in ex]
    return [[a * e[lvl * CHUNK:(lvl + 1) * CHUNK] for lvl in range(6)] for a, e in zip(xb, eb)]


def _hg_scores(q, kk, qe, ke, mask_ref):
    p = [mask_ref[6] * _rowsum(a * b) for a, b in zip(q, kk)]
    for lvl in range(6):
        d = [_dot_nt(a[lvl], b[lvl]) for a, b in zip(qe, ke)]
        p = [x + mask_ref[lvl] * y for x, y in zip(p, d)]
    return p


def _hgrn_fwd(hq, hf, hi, hg, logits, gain, consts):
    t = hq[0].shape[0]
    nc = t // CHUNK
    cb = _chunks_per_step(nc)
    rows = cb * CHUNK
    mstack, _, masks, eye = consts
    tile = pl.BlockSpec((rows, HPS * HEAD), lambda c, g: (c, g))

    def body(hq_ref, hf_ref, hi_ref, hg_ref, lg_ref, gain_ref, m_ref, mask_ref, eye_ref,
             oraw_ref, og_ref, ssave_ref, state):
        c = pl.program_id(0)
        g = pl.program_id(1)

        @pl.when(c == 0)
        def _():
            for hh in range(HPS):
                state[g * HPS + hh] = jnp.zeros((HEAD, HEAD), F32)

        lg_all = lg_ref[...]
        gain_v = gain_ref[...]

        def one(i, carry):
            sl = pl.ds(pl.multiple_of(i * CHUNK, CHUNK), CHUNK)
            hs = range(HPS)
            heads = [g * HPS + hh for hh in hs]
            ln = [slice(hh * HEAD, (hh + 1) * HEAD) for hh in hs]
            preps = [_hg_prep(hq_ref[sl, s], hf_ref[sl, s], lg_all[:, s]) for s in ln]
            lf, kk, q = [p[4] for p in preps], [p[5] for p in preps], [p[6] for p in preps]
            v = [hi_ref[sl, s] for s in ln]
            ex = [jnp.exp(x) for x in _mx_each(m_ref[...], lf)]
            eb = [e[6 * CHUNK:7 * CHUNK] for e in ex]
            esfx = [e[7 * CHUNK:8 * CHUNK] for e in ex]
            qe, ke = _hg_scaled(q, ex), _hg_scaled(kk, ex)
            p = _hg_scores(q, kk, qe, ke, mask_ref)
            s0 = [state[h] for h in heads]
            o = _each(lambda a, e, s, pp, vv: _dot(a * e, s) + _dot(pp, vv), q, eb, s0, p, v)
            eye_v = eye_ref[...]
            s1 = _each(lambda s, e, kx, ef, vv: s * _row_to_col(e[CHUNK - 1:CHUNK, :], eye_v) + _dot_tn(kx * ef, vv),
                       s0, eb, kk, esfx, v)
            for hh in hs:
                ssave_ref[i, hh] = s0[hh]
                state[heads[hh]] = s1[hh]
                oraw_ref[sl, ln[hh]] = o[hh]
                r = lax.rsqrt(jnp.mean(o[hh] * o[hh], axis=1, keepdims=True) + EPS)
                og_ref[sl, ln[hh]] = (o[hh] * r * gain_v * _silu(hg_ref[sl, ln[hh]])).astype(BF16)
            return carry

        lax.fori_loop(0, cb, one, 0, unroll=4)

    return pl.pallas_call(
        body, name="hgrn_fwd", grid=(nc // cb, HG_HEADS // HPS),
        in_specs=[_view_tile(v, rows, HPS * HEAD) for v in (hq, hf, hi, hg)] + [
                  pl.BlockSpec((2, HPS * HEAD), lambda c, g: (0, g)),
                  pl.BlockSpec((1, HEAD), lambda c, g: (0, 0)),
                  pl.BlockSpec(mstack.shape, lambda c, g: (0, 0)),
                  pl.BlockSpec(masks.shape, lambda c, g: (0, 0, 0)),
                  pl.BlockSpec(eye.shape, lambda c, g: (0, 0))],
        out_specs=[tile, tile, pl.BlockSpec((cb, HPS, HEAD, HEAD), lambda c, g: (c, g, 0, 0))],
        out_shape=[jax.ShapeDtypeStruct((t, HG_HEADS * HEAD), F32), jax.ShapeDtypeStruct((t, HG_HEADS * HEAD), BF16),
                   jax.ShapeDtypeStruct((nc, HG_HEADS, HEAD, HEAD), F32)],
        scratch_shapes=[pltpu.VMEM((HG_HEADS, HEAD, HEAD), F32)],
        compiler_params=_params(_ARB, _ARB))(hq[0], hf[0], hi[0], hg[0], logits, gain, mstack, masks, eye)


def _hgrn_bwd(hq, hf, hi, hg, logits, gain, oraw, ssave, dog, consts, into):
    t = hq[0].shape[0]
    dproj, off, width = into
    seg = HG_HEADS * HEAD
    assert HPS == HG_HEADS and width == 4 * seg and off % width == 0
    nc = t // CHUNK
    cb = _chunks_per_step(nc)
    rows = cb * CHUNK
    nb = nc // cb
    mstack, mstack_t, masks, eye = consts
    tile = pl.BlockSpec((rows, HPS * HEAD), lambda c, g: (nb - 1 - c, g))

    def body(hq_ref, hf_ref, hi_ref, hg_ref, lg_ref, gain_ref, oraw_ref, ssave_ref, dog_ref, m_ref, mt_ref,
             mask_ref, eye_ref, _, d_ref, dgain_ref, dlb_ref, dstate):
        c = pl.program_id(0)
        g = pl.program_id(1)

        @pl.when(c == 0)
        def _():
            for hh in range(HPS):
                dstate[g * HPS + hh] = jnp.zeros((HEAD, HEAD), F32)

        @pl.when((c == 0) & (g == 0))
        def _():
            dgain_ref[...] = jnp.zeros_like(dgain_ref)
            dlb_ref[...] = jnp.zeros_like(dlb_ref)

        lg_all = lg_ref[...]
        gain_v = gain_ref[...]
        eye_v = eye_ref[...]
        last_row = (lax.broadcasted_iota(jnp.int32, (CHUNK, HEAD), 0) == CHUNK - 1).astype(F32)

        def one(j, carry):
            i = cb - 1 - j
            sl = pl.ds(pl.multiple_of(i * CHUNK, CHUNK), CHUNK)
            hs = range(HPS)
            heads = [g * HPS + hh for hh in hs]
            ln = [slice(hh * HEAD, (hh + 1) * HEAD) for hh in hs]
            hqv = [hq_ref[sl, s] for s in ln]
            hgv = [hg_ref[sl, s] for s in ln]
            preps = [_hg_prep(a, hf_ref[sl, s], lg_all[:, s]) for a, s in zip(hqv, ln)]
            lb, sg, sgn, f, lf, kk, q = ([p[n] for p in preps] for n in range(7))
            v = [hi_ref[sl, s] for s in ln]
            ex = [jnp.exp(x) for x in _mx_each(m_ref[...], lf)]
            eb = [e[6 * CHUNK:7 * CHUNK] for e in ex]
            esfx = [e[7 * CHUNK:8 * CHUNK] for e in ex]
            qe, ke = _hg_scaled(q, ex), _hg_scaled(kk, ex)
            p = _hg_scores(q, kk, qe, ke, mask_ref)
            s0 = [ssave_ref[i, hh] for hh in hs]
            ds = [dstate[h] for h in heads]

            o = [oraw_ref[sl, s] for s in ln]
            r = [lax.rsqrt(jnp.mean(x * x, axis=1, keepdims=True) + EPS) for x in o]
            on = _each(lambda x, y: x * y, o, r)
            dg_out = [dog_ref[sl, s] for s in ln]
            sgate = [_silu(x) for x in hgv]
            for hh in hs:
                d_ref[sl, slice(3 * seg + hh * HEAD, 3 * seg + (hh + 1) * HEAD)] =(dg_out[hh] * on[hh] * gain_v * _dsilu(hgv[hh])).astype(BF16)
            dgain_ref[...] += sum(jnp.sum(d * s * n, axis=0, keepdims=True) for d, s, n in zip(dg_out, sgate, on))
            don = _each(lambda d, s: d * s * gain_v, dg_out, sgate)
            do = _each(lambda rr, dn, n: rr * (dn - n * jnp.mean(dn * n, axis=1, keepdims=True)), r, don, on)

            dp = _each(_dot_nt, do, v)
            dv = _each(lambda pp, d, kx, ef, s: _dot_tn(pp, d) + _dot(kx * ef, s), p, do, kk, esfx, ds)
            dqb = _each(_dot_nt, do, s0)
            dkx = _each(_dot_nt, v, ds)
            diag = [_rowsum(mask_ref[6] * x) for x in dp]
            dq = _each(lambda a, e, d, kx: a * e + d * kx, dqb, eb, diag, kk)
            dk = _each(lambda a, e, d, qq: a * e + d * qq, dkx, esfx, diag, q)
            dxs = [[] for _ in hs]
            for lvl in range(6):
                el = [e[lvl * CHUNK:(lvl + 1) * CHUNK] for e in ex]
                gm = [mask_ref[lvl] * x for x in dp]
                gm = [_bf(x) for x in gm]
                a1 = _each(lambda m_, kx: _dot(m_, kx[lvl]), gm, ke)
                a2 = _each(lambda m_, qq: _dot_tn(m_, qq[lvl]), gm, qe)
                dq = _each(lambda x, a, e: x + a * e, dq, a1, el)
                dk = _each(lambda x, a, e: x + a * e, dk, a2, el)
                for hh in hs:
                    dxs[hh].append((a1[hh] * q[hh] + a2[hh] * kk[hh]) * el[hh])
            e_end_row = [e[CHUNK - 1:CHUNK, :] for e in eb]
            ds_new = _each(lambda qq, e, d, er, s: _dot_tn(qq * e, d) + _row_to_col(er, eye_v) * s, q, eb, do, e_end_row, ds)
            for hh in hs:
                dstate[heads[hh]] = ds_new[hh]
                dend_row = _col_to_row(_rowsum(s0[hh] * ds[hh]), eye_v)
                dxs[hh].append(dqb[hh] * q[hh] * eb[hh] + last_row * (e_end_row[hh] * dend_row))
                dxs[hh].append(dkx[hh] * kk[hh] * esfx[hh])
            dlf = _mx_each(mt_ref[...], [jnp.concatenate(x, axis=0) for x in dxs])

            for hh in hs:
                d_ref[sl, slice(2 * seg + hh * HEAD, 2 * seg + (hh + 1) * HEAD)] =dv[hh].astype(BF16)
                d_ref[sl, ln[hh]] =(dq[hh] * (HEAD ** -0.5) * _dsilu(hqv[hh])).astype(BF16)
                df = dlf[hh] / f[hh]
                dsig = (1.0 - lb[hh]) * sg[hh] * sgn[hh]
                d_ref[sl, slice(seg + hh * HEAD, seg + (hh + 1) * HEAD)] =((df - dk[hh]) * dsig).astype(BF16)
                dlb_t = jnp.sum(df * sgn[hh] - dk[hh] * sgn[hh], axis=0, keepdims=True)
                dlb_ref[pl.ds(heads[hh], 1), :] += dlb_t * lb[hh] * (1.0 - lb[hh])
            return carry

        lax.fori_loop(0, cb, one, 0, unroll=2)

    outs = [jax.ShapeDtypeStruct(dproj.shape, dproj.dtype),
            jax.ShapeDtypeStruct((1, HEAD), F32), jax.ShapeDtypeStruct((HG_HEADS, HEAD), F32)]
    return pl.pallas_call(
        body, name="hgrn_bwd", grid=(nb, HG_HEADS // HPS),
        in_specs=[_view_tile(v, rows, HPS * HEAD, lambda c: nb - 1 - c) for v in (hq, hf, hi, hg)] + [
                  pl.BlockSpec((2, HPS * HEAD), lambda c, g: (0, g)),
                  pl.BlockSpec((1, HEAD), lambda c, g: (0, 0)), tile,
                  pl.BlockSpec((cb, HPS, HEAD, HEAD), lambda c, g: (nb - 1 - c, g, 0, 0)), tile,
                  pl.BlockSpec(mstack.shape, lambda c, h: (0, 0)),
                  pl.BlockSpec(mstack_t.shape, lambda c, h: (0, 0)),
                  pl.BlockSpec(masks.shape, lambda c, h: (0, 0, 0)),
                  pl.BlockSpec(eye.shape, lambda c, h: (0, 0)), _ANY],
        out_specs=[pl.BlockSpec((rows, width), lambda c, h: (nb - 1 - c, off // width)),
                   pl.BlockSpec((1, HEAD), lambda c, h: (0, 0)),
                   pl.BlockSpec((HG_HEADS, HEAD), lambda c, h: (0, 0))],
        out_shape=outs, scratch_shapes=[pltpu.VMEM((HG_HEADS, HEAD, HEAD), F32)], input_output_aliases={13: 0},
        compiler_params=_params(_ARB, _ARB))(hq[0], hf[0], hi[0], hg[0], logits, gain, oraw, ssave, dog, mstack,
                                             mstack_t, masks, eye, dproj)


CONV_W = 512
CONV_ROWS = 1024


def _per_head(fn, *arrs):
    width = arrs[0].shape[1]
    return jnp.concatenate([fn(*[a[:, j:j + HEAD] for a in arrs]) for j in range(0, width, HEAD)], axis=1)


def _shift_down(xv, halo, d, top_rows):
    if d == 0:
        return xv, xv[0:8]
    main = pltpu.roll(xv, d, 0)
    top = jnp.where(top_rows < d, pltpu.roll(halo, d, 0), main[0:8])
    return main, top


def _conv_parts(x_ref, halo_ref, w_ref, first):
    xv = x_ref[...]
    halo = jnp.where(first, 0.0, halo_ref[...])
    top_rows = lax.broadcasted_iota(jnp.int32, (8, xv.shape[1]), 0)
    shifted = [_shift_down(xv, halo, CONV_K - 1 - j, top_rows) for j in range(CONV_K)]
    w = w_ref[...]
    acc = sum(shifted[j][0] * w[j:j + 1, :] for j in range(CONV_K))
    acc_top = sum(shifted[j][1] * w[j:j + 1, :] for j in range(CONV_K))
    return shifted, acc, acc_top


def _conv_fwd(x, w8, l2scale, name):
    x, off, width = x
    t = x.shape[0]
    o = off // CONV_W
    tr = _pick(t, CONV_ROWS, 8)

    def post(cv):
        s = _silu(cv)
        if l2scale is not None:
            s = _per_head(lambda sh: sh * (lax.rsqrt(_rowsum(sh * sh) + EPS) * l2scale), s)
        return s

    def body(x_ref, halo_ref, w_ref, o_ref):
        _, acc, acc_top = _conv_parts(x_ref, halo_ref, w_ref, pl.program_id(1) == 0)
        o_ref[...] = post(acc)
        o_ref[0:8, :] = post(acc_top)

    return pl.pallas_call(
        body, name=name, grid=(width // CONV_W,t // tr),
        in_specs=[pl.BlockSpec((tr, CONV_W), lambda j, i: (i, o + j)),
                  pl.BlockSpec((8, CONV_W), lambda j, i: (jnp.maximum(i * (tr // 8) - 1, 0), o + j)),
                  pl.BlockSpec((8, CONV_W), lambda j, i: (0, j))],
        out_specs=pl.BlockSpec((tr, CONV_W), lambda j, i: (i, j)),
        out_shape=jax.ShapeDtypeStruct((t, width), F32), compiler_params=_params(_PAR, _PAR))(x, x, w8)


def _conv_bwd_a(x, w8, dy, l2scale, name):
    x, off, width = x
    t = x.shape[0]
    o = off // CONV_W
    tr = _pick(t, CONV_ROWS, 8)

    def l2_bwd(s, dyh):
        r = lax.rsqrt(_rowsum(s * s) + EPS)
        y0 = s * r
        dy0 = dyh * l2scale
        return r * (dy0 - y0 * _rowsum(dy0 * y0))

    def to_dc(cv, dyv):
        if l2scale is not None:
            dyv = _per_head(l2_bwd, _silu(cv), dyv)
        return dyv * _dsilu(cv)

    def body(x_ref, halo_ref, w_ref, dy_ref, dc_ref, dw_ref):
        @pl.when(pl.program_id(1) == 0)
        def _():
            dw_ref[...] = jnp.zeros_like(dw_ref)

        shifted, acc, acc_top = _conv_parts(x_ref, halo_ref, w_ref, pl.program_id(1) == 0)
        dyv = dy_ref[...]
        dc = to_dc(acc, dyv)
        dc_top = to_dc(acc_top, dyv[0:8])
        dc_ref[...] = dc
        dc_ref[0:8, :] = dc_top
        rest = (lax.broadcasted_iota(jnp.int32, dc.shape, 0) >= 8).astype(F32)
        dc_rest = dc * rest
        for j in range(CONV_K):
            dw_ref[j:j + 1, :] += (jnp.sum(dc_rest * shifted[j][0], axis=0, keepdims=True)
                                   + jnp.sum(dc_top * shifted[j][1], axis=0, keepdims=True))

    return pl.pallas_call(
        body, name=name, grid=(width // CONV_W,t // tr),
        in_specs=[pl.BlockSpec((tr, CONV_W), lambda j, i: (i, o + j)),
                  pl.BlockSpec((8, CONV_W), lambda j, i: (jnp.maximum(i * (tr // 8) - 1, 0), o + j)),
                  pl.BlockSpec((8, CONV_W), lambda j, i: (0, j)),
                  pl.BlockSpec((tr, CONV_W), lambda j, i: (i, j))],
        out_specs=[pl.BlockSpec((tr, CONV_W), lambda j, i: (i, j)), pl.BlockSpec((8, CONV_W), lambda j, i: (0, j))],
        out_shape=[jax.ShapeDtypeStruct((t, width), F32), jax.ShapeDtypeStruct((8, width), F32)],
        compiler_params=_params(_PAR, _ARB))(x, x, w8, dy)


def _conv_bwd_b(dc, w8, name, into):
    t, width = dc.shape
    tr = _pick(t, CONV_ROWS, 8)
    nt = t // tr

    dproj, off, into_width = into
    assert into_width == width and off % CONV_W == 0
    o = off // CONV_W

    def body(dc_ref, halo_ref, w_ref, _, dx_ref):
        dcv = dc_ref[...]
        halo = jnp.where(pl.program_id(1) == nt - 1, 0.0, halo_ref[...])
        w = w_ref[...]
        bot_rows = lax.broadcasted_iota(jnp.int32, (8, CONV_W), 0)
        acc = dcv * w[CONV_K - 1:CONV_K, :]
        acc_bot = dcv[tr - 8:tr] * w[CONV_K - 1:CONV_K, :]
        for d in range(1, CONV_K):
            main = pltpu.roll(dcv, tr - d, 0)
            bot = jnp.where(bot_rows >= 8 - d, pltpu.roll(halo, 8 - d, 0), main[tr - 8:tr])
            wj = w[CONV_K - 1 - d:CONV_K - d, :]
            acc = acc + main * wj
            acc_bot = acc_bot + bot * wj
        dx_ref[...] = acc.astype(BF16)
        dx_ref[tr - 16:tr, :] = jnp.concatenate([acc[tr - 16:tr - 8], acc_bot], axis=0).astype(BF16)

    return pl.pallas_call(
        body, name=name, grid=(width // CONV_W,nt),
        in_specs=[pl.BlockSpec((tr, CONV_W), lambda j, i: (i, j)),
                  pl.BlockSpec((8, CONV_W), lambda j, i: (jnp.minimum((i + 1) * (tr // 8), t // 8 - 1), j)),
                  pl.BlockSpec((8, CONV_W), lambda j, i: (0, j)), _ANY],
        out_specs=pl.BlockSpec((tr, CONV_W), lambda j, i: (i, o + j)),
        out_shape=jax.ShapeDtypeStruct(dproj.shape, dproj.dtype), input_output_aliases={3: 0},
        compiler_params=_params(_PAR, _PAR))(dc, dc, w8, dproj)


def _each(f, *lists):
    return [f(*xs) for xs in zip(*lists)]


def _split2_each(xs):
    hi = [_bf(x) for x in xs]
    lo = [_bf(x - h.astype(F32)) for x, h in zip(xs, hi)]
    return hi, lo


def _hp_each(a_split, b_split):
    (ah, al), (bh, bl) = a_split, b_split
    rows = ah[0].shape[0]
    d12 = [jnp.dot(jnp.concatenate([x, y], axis=0), z, preferred_element_type=F32) for x, y, z in zip(ah, al, bh)]
    d3 = [jnp.dot(x, y, preferred_element_type=F32) for x, y in zip(ah, bl)]
    return [d[:rows] + d[rows:] + e for d, e in zip(d12, d3)]


INV_EXACT_STEPS = 0


def _tri_inv_each(a_list, eye):
    ns = [-a for a in a_list]
    ps = [eye + n for n in ns]
    n_split = _split2_each(ns)
    for step in range(5):
        if step < INV_EXACT_STEPS:
            ns = _hp_each(n_split, n_split)
            n_split = _split2_each(ns)
            ps = [p + d for p, d in zip(ps, _hp_each(_split2_each(ps), n_split))]
        else:
            nb = n_split[0] if step == INV_EXACT_STEPS else [_bf(n) for n in ns]
            ns = [jnp.dot(x, x, preferred_element_type=F32) for x in nb]
            nb2 = [_bf(n) for n in ns]
            ps = [p + jnp.dot(_bf(p), y, preferred_element_type=F32) for p, y in zip(ps, nb2)]
    return ps


def _gd_gates(gab, alog, dtb):
    sp_arg = gab + dtb
    return sp_arg, -jnp.exp(alog) * _softplus(sp_arg), _sigmoid(gab)


def _pick_lane(tile, base, head):
    g, hh = head
    col = tile[:, base + hh:base + hh + 1]
    for gi in range(1, GD_HEADS // HPS):
        lane = base + gi * HPS + hh
        col = jnp.where(g == gi, tile[:, lane:lane + 1], col)
    return col


def _gd_chunks(q, k, v, g_all, beta_all, heads, l_ref, mask_ref, tm=None):
    incl, strict, eye, upper = mask_ref[0], mask_ref[1], mask_ref[2], mask_ref[3]
    lmat = l_ref[...]
    gb = [jnp.broadcast_to(_pick_lane(g_all, 0, s), (CHUNK, HEAD)) for s in heads]
    bb = [jnp.broadcast_to(_pick_lane(beta_all, GD_HEADS, s), (CHUNK, HEAD)) for s in heads]
    gam = _mx_each(lmat, gb)
    gam_row = [jnp.sum(x[:, :CHUNK] * upper, axis=0, keepdims=True) for x in gb]
    lm = _each(lambda gm, gr: incl * jnp.exp(jnp.minimum(gm[:, :CHUNK] - gr, 0.0)), gam, gam_row)
    kb = _each(lambda x, b: x * b, k, bb)
    a = _each(lambda x, y, m: strict * _dot_nt(x, y) * m, kb, k, lm)
    if tm is None:
        tm = _tri_inv_each(a, eye)
    eg = [jnp.exp(x) for x in gam]
    vb = _each(lambda x, b: x * b, v, bb)
    kbg = _each(lambda x, e: x * e, kb, eg)
    uw = _each(lambda t_, x, y: _dot(t_, jnp.concatenate([x, y], axis=1)), tm, vb, kbg)
    u = [x[:, :HEAD] for x in uw]
    w = [x[:, HEAD:] for x in uw]
    qk = _each(lambda x, y, m: _dot_nt(x, y) * m, q, k, lm)
    g_end = [x[CHUNK - 1:CHUNK, :] for x in gam]
    ekg = _each(lambda e, x: jnp.exp(e - x), g_end, gam)
    ge = [jnp.exp(e) for e in g_end]
    kg = _each(lambda x, e: x * e, k, ekg)
    qg = _each(lambda x, e: x * e, q, eg)
    names = ("bb", "lm", "kb", "a", "tm", "eg", "vb", "kbg", "u", "w", "qk", "ekg", "ge", "kg", "qg")
    cols = (bb, lm, kb, a, tm, eg, vb, kbg, u, w, qk, ekg, ge, kg, qg)
    return [dict(zip(names, vals)) for vals in zip(*cols)]


def _gd_specs(rows, rev_nb=None):
    def cidx(c):
        return c if rev_nb is None else rev_nb - 1 - c

    qk_tile = pl.BlockSpec((rows, HPS // 2 * HEAD), lambda c, g: (cidx(c), g))
    v_tile = pl.BlockSpec((rows, HPS * HEAD), lambda c, g: (cidx(c), g))
    gab_tile = pl.BlockSpec((rows, HEAD), lambda c, g: (cidx(c), 0))
    return qk_tile, v_tile, gab_tile


def _gdn_fwd(qn, kn, cv, gab, gz, alog, dtb, gain, consts):
    t = qn.shape[0]
    nc = t // CHUNK
    cb = _chunks_per_step(nc)
    rows = cb * CHUNK
    lmat, _, masks = consts
    qk_tile, v_tile, gab_tile = _gd_specs(rows)
    row128 = pl.BlockSpec((1, HEAD), lambda c, h: (0, 0))

    def body(q_ref, k_ref, v_ref, gab_ref, gz_ref, alog_ref, dtb_ref, gain_ref, l_ref, mask_ref,
             oraw_ref, og_ref, ssave_ref, tsave_ref, state):
        c = pl.program_id(0)
        g = pl.program_id(1)

        @pl.when(c == 0)
        def _():
            for hh in range(HPS):
                state[g * HPS + hh] = jnp.zeros((HEAD, HEAD), F32)

        alog = alog_ref[...]
        dtb = dtb_ref[...]
        gain_v = gain_ref[...]

        def one(i, carry):
            sl = pl.ds(pl.multiple_of(i * CHUNK, CHUNK), CHUNK)
            _, g_all, beta_all = _gd_gates(gab_ref[sl, :], alog, dtb)
            heads = [g * HPS + hh for hh in range(HPS)]
            lq = [slice(hh // 2 * HEAD, (hh // 2 + 1) * HEAD) for hh in range(HPS)]
            lv = [slice(hh * HEAD, (hh + 1) * HEAD) for hh in range(HPS)]
            chs = _gd_chunks([q_ref[sl, s] for s in lq], [k_ref[sl, s] for s in lq], [v_ref[sl, s] for s in lv],
                             g_all, beta_all, [(g, hh) for hh in range(HPS)], l_ref, mask_ref)
            s0 = [state[h] for h in heads]
            ws = _each(lambda ch, s: _dot(jnp.concatenate([ch["w"], ch["qg"]], axis=0), s), chs, s0)
            v_new = _each(lambda ch, x: ch["u"] - x[:CHUNK], chs, ws)
            o = _each(lambda ch, x, vn: x[CHUNK:] + _dot(ch["qk"], vn), chs, ws, v_new)
            s1 = _each(lambda ch, s, vn: s * ch["ge"] + _dot_tn(ch["kg"], vn), chs, s0, v_new)
            for hh in range(HPS):
                ssave_ref[i, hh] = s0[hh]
                tsave_ref[i, hh] = chs[hh]["tm"]
                state[heads[hh]] = s1[hh]
                oraw_ref[sl, lv[hh]] = o[hh]
                r = lax.rsqrt(jnp.mean(o[hh] * o[hh], axis=1, keepdims=True) + EPS)
                og_ref[sl, lv[hh]] = (o[hh] * r * gain_v * _silu(gz_ref[sl, lv[hh]])).astype(BF16)
            return carry

        lax.fori_loop(0, cb, one, 0, unroll=4)

    return pl.pallas_call(
        body, name="gdn_fwd", grid=(nc // cb, GD_HEADS // HPS),
        in_specs=[qk_tile, qk_tile, v_tile, gab_tile, _view_tile(gz, rows, HPS * HEAD), row128, row128, row128,
                  pl.BlockSpec(lmat.shape, lambda c, g: (0, 0)),
                  pl.BlockSpec(masks.shape, lambda c, g: (0, 0, 0))],
        out_specs=[v_tile, v_tile, pl.BlockSpec((cb, HPS, HEAD, HEAD), lambda c, g: (c, g, 0, 0)),
                   pl.BlockSpec((cb, HPS, CHUNK, CHUNK), lambda c, g: (c, g, 0, 0))],
        out_shape=[jax.ShapeDtypeStruct((t, GD_HEADS * HEAD), F32), jax.ShapeDtypeStruct((t, GD_HEADS * HEAD), BF16),
                   jax.ShapeDtypeStruct((nc, GD_HEADS, HEAD, HEAD), F32),
                   jax.ShapeDtypeStruct((nc, GD_HEADS, CHUNK, CHUNK), F32)],
        scratch_shapes=[pltpu.VMEM((GD_HEADS, HEAD, HEAD), F32)],
        compiler_params=_params(_ARB, _ARB))(qn, kn, cv, gab, gz[0], alog, dtb, gain, lmat, masks)


def _gdn_bwd(qn, kn, cv, gab, gz, alog, dtb, gain, oraw, ssave, tsave, dog, consts, into):
    t = qn.shape[0]
    dproj, off, width = into
    assert width == GD_HEADS * HEAD and off % (HPS * HEAD) == 0
    nc = t // CHUNK
    cb = _chunks_per_step(nc)
    rows = cb * CHUNK
    nb = nc // cb
    lmat, lmat_t, masks = consts
    qk_tile, v_tile, gab_tile = _gd_specs(rows, nb)
    row128 = pl.BlockSpec((1, HEAD), lambda c, h: (0, 0))

    def body(q_ref, k_ref, v_ref, gab_ref, gz_ref, alog_ref, dtb_ref, gain_ref, oraw_ref, ssave_ref, tsave_ref, dog_ref,
             l_ref, lt_ref, mask_ref, _,
             dq_ref, dk_ref, dv_ref, dgab_ref, dgz_ref, small_ref, dstate):
        c = pl.program_id(0)
        g = pl.program_id(1)

        @pl.when(c == 0)
        def _():
            for hh in range(HPS):
                dstate[g * HPS + hh] = jnp.zeros((HEAD, HEAD), F32)

        @pl.when((c == 0) & (g == 0))
        def _():
            small_ref[...] = jnp.zeros_like(small_ref)

        alog = alog_ref[...]
        dtb = dtb_ref[...]
        gain_v = gain_ref[...]
        lane = lax.broadcasted_iota(jnp.int32, (1, HEAD), 1)
        last_row = (lax.broadcasted_iota(jnp.int32, (CHUNK, HEAD), 0) == CHUNK - 1).astype(F32)

        def one(j, carry):
            i = cb - 1 - j
            sl = pl.ds(pl.multiple_of(i * CHUNK, CHUNK), CHUNK)
            sp_arg, g_all, beta_all = _gd_gates(gab_ref[sl, :], alog, dtb)
            strict, eye = mask_ref[1], mask_ref[2]
            ltm = lt_ref[...]
            hs = range(HPS)
            heads = [g * HPS + hh for hh in hs]
            lq = [slice(hh // 2 * HEAD, (hh // 2 + 1) * HEAD) for hh in hs]
            lv = [slice(hh * HEAD, (hh + 1) * HEAD) for hh in hs]
            q = [q_ref[sl, s] for s in lq]
            k = [k_ref[sl, s] for s in lq]
            v = [v_ref[sl, s] for s in lv]
            gzv = [gz_ref[sl, s] for s in lv]
            chs = _gd_chunks(q, k, v, g_all, beta_all, [(g, hh) for hh in hs], l_ref, mask_ref,
                             tm=[tsave_ref[i, hh] for hh in hs])

            def col(name):
                return [ch[name] for ch in chs]

            def mul(x, y):
                return x * y

            tm, lm, eg, bb = col("tm"), col("lm"), col("eg"), col("bb")
            s0 = [ssave_ref[i, hh] for hh in hs]
            ds = [dstate[h] for h in heads]
            v_new = _each(lambda u, w, s: u - _dot(w, s), col("u"), col("w"), s0)

            o = [oraw_ref[sl, s] for s in lv]
            r = [lax.rsqrt(jnp.mean(x * x, axis=1, keepdims=True) + EPS) for x in o]
            on = _each(mul, o, r)
            dg_out = [dog_ref[sl, s] for s in lv]
            sgate = [_silu(x) for x in gzv]
            for hh in hs:
                dgz_ref[sl, lv[hh]] = (dg_out[hh] * on[hh] * gain_v * _dsilu(gzv[hh])).astype(BF16)
            small_ref[0:1, :] += sum(jnp.sum(d * s * n, axis=0, keepdims=True) for d, s, n in zip(dg_out, sgate, on))
            don = _each(lambda d, s: d * s * gain_v, dg_out, sgate)
            do = _each(lambda rr, dn, n: rr * (dn - n * jnp.mean(dn * n, axis=1, keepdims=True)), r, don, on)

            dv_new = _each(lambda a, d, b, s: _dot_tn(a, d) + _dot(b, s), col("qk"), do, col("kg"), ds)
            dqk = _each(_dot_nt, do, v_new)
            dkg = _each(_dot_nt, v_new, ds)
            dge = _each(lambda s, d: jnp.sum(_rowsum(s * d), axis=0, keepdims=True), s0, ds)
            both = _each(lambda d, dv: jnp.concatenate([d, dv], axis=0), do, dv_new)
            from_s = _each(_dot_nt, both, s0)
            dqg = [x[:CHUNK] for x in from_s]
            dw = [-x[CHUNK:] for x in from_s]
            ds_new = _each(lambda qg, w, bo, ge, s: _dot_tn(jnp.concatenate([qg, -w], axis=0), bo) + ge * s,
                           col("qg"), col("w"), both, col("ge"), ds)
            for hh in hs:
                dstate[heads[hh]] = ds_new[hh]

            side = _each(lambda dv, d: jnp.concatenate([dv, d], axis=1), dv_new, dw)
            back = _each(_dot_tn, tm, side)
            dvb = [x[:, :HEAD] for x in back]
            dkbg = [x[:, HEAD:] for x in back]
            dtm = _each(lambda sd, vb, kbg: _dot_nt(sd, jnp.concatenate([vb, kbg], axis=1)), side, col("vb"), col("kbg"))
            dtt = _each(_dot_nt, dtm, tm)
            da = _each(lambda t_, x: -_dot_tn(t_, x) * strict, tm, dtt)
            dal = _each(mul, da, lm)
            dqk_l = _each(mul, dqk, lm)
            stack = _each(lambda x, y: jnp.concatenate([x, y], axis=0), dal, dqk_l)
            on_k = _each(_dot, stack, k)
            dkb = _each(lambda x, y, e: x[:CHUNK] + y * e, on_k, dkbg, eg)
            dq = _each(lambda x, y, e: x[CHUNK:] + y * e, on_k, dqg, eg)
            dk = _each(lambda st, kb, qq, z, ekg, w_, b: _dot_tn(st, jnp.concatenate([kb, qq], axis=0)) + z * ekg + w_ * b,
                       stack, col("kb"), q, dkg, col("ekg"), dkb, bb)
            gmat = _each(lambda x, a, y, qk: x * a + y * qk, da, col("a"), dqk, col("qk"))
            t_kg = _each(lambda x, y: _rowsum(x * y), dkg, col("kg"))
            dgam = _each(lambda gm, x, qg, t_, y, kbg: (_rowsum(gm) - _row_to_col(jnp.sum(gm, axis=0, keepdims=True), eye)
                                                        + _rowsum(x * qg) - t_ + _rowsum(y * kbg)),
                         gmat, dqg, col("qg"), t_kg, dkbg, col("kbg"))
            dg_end = _each(lambda t_, e, ge: jnp.sum(t_, axis=0, keepdims=True) + e * ge[:, 0:1], t_kg, dge, col("ge"))
            dgam = _each(lambda x, e: x + last_row * e, dgam, dg_end)
            dbeta = _each(lambda x, kk, y, vv: _rowsum(x * kk) + _rowsum(y * vv), dkb, k, dvb, v)
            dg = _mx_each(ltm, dgam)

            for hh in hs:
                dv_ref[sl, lv[hh]] = dvb[hh] * bb[hh]
            fac_g = -jnp.exp(alog) * _sigmoid(sp_arg)
            fac_b = beta_all * (1.0 - beta_all)
            hot_g = [(lane == h).astype(F32) for h in heads]
            hot_b = [(lane == GD_HEADS + h).astype(F32) for h in heads]
            dga = _each(lambda x, hot: x * hot * fac_g, dg, hot_g)
            dgb = _each(lambda x, hot: x * hot * fac_b, dbeta, hot_b)
            small_ref[1:2, :] += sum(jnp.sum(x, axis=0, keepdims=True) for x in dga)
            small_ref[2:3, :] += sum(jnp.sum(x * hot * g_all, axis=0, keepdims=True) for x, hot in zip(dg, hot_g))
            for pair in range(HPS // 2):
                lqp = slice(pair * HEAD, (pair + 1) * HEAD)
                dq_ref[sl, lqp] = dq[2 * pair] + dq[2 * pair + 1]
                dk_ref[sl, lqp] = dk[2 * pair] + dk[2 * pair + 1]
            dgab_ref[sl, :] = sum(a + b for a, b in zip(dga, dgb))
            return carry

        lax.fori_loop(0, cb, one, 0, unroll=4)

    groups = GD_HEADS // HPS
    outs = [jax.ShapeDtypeStruct((t, 1024), F32), jax.ShapeDtypeStruct((t, 1024), F32),
            jax.ShapeDtypeStruct((t, 2048), F32), jax.ShapeDtypeStruct((t, groups * HEAD), F32),
            jax.ShapeDtypeStruct(dproj.shape, dproj.dtype), jax.ShapeDtypeStruct((8, HEAD), F32)]
    dgz_tile = pl.BlockSpec((rows, HPS * HEAD), lambda c, g: (nb - 1 - c, off // (HPS * HEAD) + g))
    return pl.pallas_call(
        body, name="gdn_bwd", grid=(nb, groups),
        in_specs=[qk_tile, qk_tile, v_tile, gab_tile, _view_tile(gz, rows, HPS * HEAD, lambda c: nb - 1 - c),
                  row128, row128, row128, v_tile,
                  pl.BlockSpec((cb, HPS, HEAD, HEAD), lambda c, g: (nb - 1 - c, g, 0, 0)),
                  pl.BlockSpec((cb, HPS, CHUNK, CHUNK), lambda c, g: (nb - 1 - c, g, 0, 0)), v_tile,
                  pl.BlockSpec(lmat.shape, lambda c, g: (0, 0)),
                  pl.BlockSpec(lmat_t.shape, lambda c, g: (0, 0)),
                  pl.BlockSpec(masks.shape, lambda c, g: (0, 0, 0)), _ANY],
        out_specs=[qk_tile, qk_tile, v_tile, pl.BlockSpec((rows, HEAD), lambda c, g: (nb - 1 - c, g)), dgz_tile,
                   pl.BlockSpec((8, HEAD), lambda c, g: (0, 0))],
        out_shape=outs, scratch_shapes=[pltpu.VMEM((GD_HEADS, HEAD, HEAD), F32)], input_output_aliases={15: 4},
        compiler_params=_params(_ARB, _ARB))(qn, kn, cv, gab, gz[0], alog, dtb, gain, oraw, ssave, tsave, dog,
                                             lmat, lmat_t, masks, dproj)


def _fold_groups(wide):
    t, width = wide.shape
    tr = _pick(t, CONV_ROWS, 8)

    def body(w_ref, o_ref):
        acc = w_ref[:, 0:HEAD]
        for j in range(1, width // HEAD):
            acc = acc + w_ref[:, j * HEAD:(j + 1) * HEAD]
        o_ref[...] = acc.astype(BF16)

    return pl.pallas_call(
        body, name="fold_gate_grads", grid=(t // tr,), in_specs=[_row_spec(tr, width)], out_specs=_row_spec(tr, HEAD),
        out_shape=jax.ShapeDtypeStruct((t, HEAD), BF16), compiler_params=_params(_PAR))(wide)


def _adam_math(w, g, m, v):
    m2 = ADAM_B1 * m + (1.0 - ADAM_B1) * g
    v2 = ADAM_B2 * v + (1.0 - ADAM_B2) * (g * g)
    m_hat = m2 / (1.0 - ADAM_B1 ** ADAM_STEP)
    v_hat = v2 / (1.0 - ADAM_B2 ** ADAM_STEP)
    delta = -ADAM_LR * (m_hat / (jnp.sqrt(v_hat) + ADAM_EPS) + ADAM_WD * w)
    return delta, m2, v2


def _adamw(w, g, m, v, name, after=None):
    r, c = w.shape
    tr = r
    for cand in range(8, r + 1, 8):
        if r % cand == 0 and cand * c * 4 <= (2 << 20):
            tr = cand
    if r % 8 != 0:
        tr = r

    def body(w_ref, g_ref, m_ref, v_ref, *rest):
        d_ref, m2_ref, v2_ref = rest[-3:]
        d, m2, v2 = _adam_math(w_ref[...], g_ref[...], m_ref[...], v_ref[...])
        d_ref[...] = d
        m2_ref[...] = m2
        v2_ref[...] = v2

    spec = pl.BlockSpec((tr, c), lambda i: (i, 0))
    extra = [] if after is None else [after]
    return pl.pallas_call(
        body, name=name, grid=(r // tr,), in_specs=[spec] * 4 + [_ANY] * len(extra), out_specs=[spec] * 3,
        out_shape=[jax.ShapeDtypeStruct((r, c), F32)] * 3, compiler_params=_params(_PAR))(w, g, m, v, *extra)


_ANY = pl.BlockSpec(memory_space=pl.ANY)


def _place():
    return lax.axis_index("x"), lax.axis_index("y"), lax.axis_index("c")


def _gather_weights(packs, nchs, name):
    n = len(packs)
    halves = [p.shape[0] // 2 for p in packs]
    base = [sum(nchs[:i]) for i in range(n)]
    total = sum(nchs)
    for p, h, k in zip(packs, halves, nchs):
        assert p.shape[0] == 2 * h and h % k == 0 and (h // k) % 16 == 0

    def body(*refs):
        p_refs, g_refs, (send_sems, recv_sems) = refs[:n], refs[n:2 * n], refs[2 * n:]
        x, y, c = _place()
        sibling = (x, y, 1 - c)
        chips = [(1 - x, y), (x, 1 - y), (1 - x, 1 - y)]
        chunks = [(a, q) for a in range(n) for q in range(nchs[a])]

        def rows_of(a, pc, q):
            ch = halves[a] // nchs[a]
            return pl.ds(pl.multiple_of(pc * halves[a] + q * ch, 16), ch)

        def piece(a, px, py, pc, q):
            return g_refs[a].at[2 * px + py, rows_of(a, pc, q), :]

        def copy(k, src, dst, to):
            return pltpu.make_async_remote_copy(src_ref=src, dst_ref=dst, send_sem=send_sems.at[k],
                                                recv_sem=recv_sems.at[k], device_id=to, device_id_type=MESH)

        def sem_of(j, a, q):
            return j * total + base[a] + q

        first = {(j, a, q): copy(sem_of(j, a, q), p_refs[a].at[rows_of(a, c, q), :], piece(a, x, y, c, q), (*chip, c))
                 for j, chip in enumerate(chips) for a, q in chunks}
        for a, q in chunks:
            for j in range(3):
                first[j, a, q].start()
        passed = {(j, a, q): copy(sem_of(3 + j, a, q), piece(a, *chip, c, q), piece(a, *chip, c, q), sibling)
                  for j, chip in enumerate(chips) for a, q in chunks}
        for a, q in chunks:
            for j, chip in enumerate(chips):
                copy(sem_of(j, a, q), p_refs[a].at[rows_of(a, c, q), :], piece(a, *chip, c, q), (*chip, c)).wait_recv()
                passed[j, a, q].start()
        for a, q in chunks:
            for j, chip in enumerate(chips):
                copy(sem_of(3 + j, a, q), piece(a, *chip, 1 - c, q), piece(a, *chip, 1 - c, q), sibling).wait_recv()
        for key in first:
            first[key].wait_send()
            passed[key].wait_send()

    return pl.pallas_call(
        body, name=name, out_shape=[jax.ShapeDtypeStruct((4,) + p.shape, p.dtype) for p in packs],
        in_specs=[_ANY] * n, out_specs=[_ANY] * n,
        scratch_shapes=[pltpu.SemaphoreType.DMA((6 * total,)), pltpu.SemaphoreType.DMA((6 * total,))])(*packs)


def _swap_with_sibling(arrs, nchs, lead, name, halves=False):
    n = len(arrs)
    jobs = []
    hs = [arr.shape[-2] // (2 if halves else 1) for arr in arrs]
    for a, (h, k) in enumerate(zip(hs, nchs)):
        assert h % k == 0 and (h // k) % 16 == 0
        for s in (range(lead) if lead else [None]):
            jobs += [(a, s, q * (h // k), h // k) for q in range(k)]

    def body(*refs):
        src, dst, (send_sems, recv_sems) = refs[:n], refs[n:2 * n], refs[2 * n:]
        x, y, c = _place()

        def at(ref, s, r0, rows):
            return ref.at[pl.ds(r0, rows), :] if s is None else ref.at[s, pl.ds(r0, rows), :]

        def src_rows(a, r0):
            return pl.multiple_of((1 - c) * hs[a] + r0, 16) if halves else r0

        copies = [pltpu.make_async_remote_copy(
            src_ref=at(src[a], s, src_rows(a, r0), rows), dst_ref=at(dst[a], s, r0, rows), send_sem=send_sems.at[k],
            recv_sem=recv_sems.at[k], device_id=(x, y, 1 - c), device_id_type=MESH)
            for k, (a, s, r0, rows) in enumerate(jobs)]
        for cp in copies:
            cp.start()
        for cp in copies:
            cp.wait()

    shapes = [jax.ShapeDtypeStruct(arr.shape[:-2] + (h, arr.shape[-1]), arr.dtype) for arr, h in zip(arrs, hs)]
    return pl.pallas_call(
        body, name=name, out_shape=shapes, in_specs=[_ANY] * n, out_specs=[_ANY] * n,
        scratch_shapes=[pltpu.SemaphoreType.DMA((len(jobs),)), pltpu.SemaphoreType.DMA((len(jobs),))])(*arrs)


def _add2(full, b, core, name):
    n, rows, w = b.shape
    tr = _pick(rows, 256, 16)
    nblk = rows // tr

    def body(c_ref, a_ref, b_ref, o_ref):
        o_ref[...] = (a_ref[...].astype(F32) + b_ref[...].astype(F32)).astype(BF16)

    spec = pl.BlockSpec((1, tr, w), lambda i, j, c_ref: (i, j, 0))
    grid_spec = pltpu.PrefetchScalarGridSpec(
        num_scalar_prefetch=1, grid=(n, nblk),
        in_specs=[pl.BlockSpec((1, tr, w), lambda i, j, c_ref: (i, c_ref[0] * nblk + j, 0)), spec], out_specs=spec)
    return pl.pallas_call(
        body, name=name, grid_spec=grid_spec, out_shape=jax.ShapeDtypeStruct(b.shape, BF16),
        compiler_params=_params(_PAR, _PAR))(core, full, b)


def _reduce_chips(partials, nchs, name):
    n = len(partials)
    jobs = []
    for a, (arr, k) in enumerate(zip(partials, nchs)):
        h = arr.shape[1]
        assert h % k == 0 and (h // k) % 16 == 0
        jobs += [(a, q * (h // k), h // k) for q in range(k)]

    def body(*refs):
        src, dst, (send_sems, recv_sems) = refs[:n], refs[n:2 * n], refs[2 * n:]
        x, y, c = _place()
        chips = [(1 - x, y), (x, 1 - y), (1 - x, 1 - y)]
        copies = [pltpu.make_async_remote_copy(
            src_ref=src[a].at[2 * px + py, pl.ds(r0, rows), :], dst_ref=dst[a].at[j, pl.ds(r0, rows), :],
            send_sem=send_sems.at[3 * k + j], recv_sem=recv_sems.at[3 * k + j],
            device_id=(px, py, c), device_id_type=MESH)
            for k, (a, r0, rows) in enumerate(jobs) for j, (px, py) in enumerate(chips)]
        for cp in copies:
            cp.start()
        for cp in copies:
            cp.wait()

    return pl.pallas_call(
        body, name=name,
        out_shape=[jax.ShapeDtypeStruct((3,) + p.shape[1:], p.dtype) for p in partials],
        in_specs=[_ANY] * n, out_specs=[_ANY] * n,
        scratch_shapes=[pltpu.SemaphoreType.DMA((3 * len(jobs),)), pltpu.SemaphoreType.DMA((3 * len(jobs),))])(*partials)


_HBM = pl.BlockSpec(memory_space=pltpu.HBM)
_SEM = pl.BlockSpec(memory_space=pltpu.SEMAPHORE)
_DATAFLOW = pltpu.SideEffectType.DATAFLOW_SIDE_EFFECTING


def _ici_jobs(srcs, nchs, kind):
    jobs = []
    for a, (arr, k) in enumerate(zip(srcs, nchs)):
        h = arr.shape[0] // 2 if kind == "gather" else arr.shape[1]
        assert h % k == 0 and (h // k) % 16 == 0
        jobs += [(a, h, q * (h // k), h // k) for q in range(k)]
    return jobs


def _ici_copies(src, land, send_sems, recv_sems, jobs, kind):
    x, y, c = _place()
    chips = [(1 - x, y), (x, 1 - y), (1 - x, 1 - y)]
    copies = []
    for k, (a, h, r0, rows) in enumerate(jobs):
        for j, (px, py) in enumerate(chips):
            if kind == "gather":
                at = pl.ds(pl.multiple_of(c * h + r0, 16), rows)
                s, d = src[a].at[at, :], land[a].at[2 * x + y, at, :]
            else:
                s, d = src[a].at[2 * px + py, pl.ds(r0, rows), :], land[a].at[j, pl.ds(r0, rows), :]
            copies.append(pltpu.make_async_remote_copy(
                src_ref=s, dst_ref=d, send_sem=send_sems.at[3 * k + j], recv_sem=recv_sems.at[3 * k + j],
                device_id=(px, py, c), device_id_type=MESH))
    return copies


def _ici_start(srcs, nchs, kind, name, after=None):
    n = len(srcs)
    extra = [] if after is None else [after]
    jobs = _ici_jobs(srcs, nchs, kind)
    lead = (lambda s: (4,) + s.shape) if kind == "gather" else (lambda s: (3,) + s.shape[1:])
    lands = [lax.empty(lead(s), s.dtype) for s in srcs]

    def body(*refs):
        src, land = refs[:n], refs[n:2 * n]
        send_sems, recv_sems, token = refs[2 * n + len(extra)], refs[2 * n + len(extra) + 1], refs[-1]
        for cp in _ici_copies(src, land, send_sems, recv_sems, jobs, kind):
            cp.start()
        token[...] = jnp.zeros_like(token)

    hbm = [pltpu.HBM(a.shape, a.dtype) for a in srcs + lands]
    outs = pl.pallas_call(
        body, name=name,
        out_shape=[pltpu.SemaphoreType.DMA((3 * len(jobs),)), pltpu.SemaphoreType.DMA((3 * len(jobs),))] + hbm
        + [jax.ShapeDtypeStruct((8, 128), F32)],
        in_specs=[_HBM] * (2 * n) + [_ANY] * len(extra),
        out_specs=[_SEM, _SEM] + [_HBM] * (2 * n) + [pl.BlockSpec(memory_space=pltpu.VMEM)],
        input_output_aliases={i: 2 + i for i in range(2 * n)},
        compiler_params=pltpu.CompilerParams(has_side_effects=_DATAFLOW),
    )(*[pltpu.with_memory_space_constraint(a, pltpu.HBM) for a in srcs + lands], *extra)
    return (outs[0], outs[1], list(outs[2:2 + n]), list(outs[2 + n:2 + 2 * n]), nchs, kind), outs[-1]


def _ici_wait(handle, after, name):
    send_sems, recv_sems, srcs, lands, nchs, kind = handle
    n = len(srcs)
    jobs = _ici_jobs(srcs, nchs, kind)

    def body(*refs):
        src, land = refs[:n], refs[n:2 * n]
        for cp in _ici_copies(src, land, refs[2 * n], refs[2 * n + 1], jobs, kind):
            cp.wait_send()
            cp.wait_recv()

    outs = pl.pallas_call(
        body, name=name, out_shape=[pltpu.HBM(a.shape, a.dtype) for a in srcs + lands],
        in_specs=[_HBM] * (2 * n) + [_SEM, _SEM, _ANY], out_specs=[_HBM] * (2 * n),
        input_output_aliases={i: i for i in range(2 * n)},
        compiler_params=pltpu.CompilerParams(has_side_effects=_DATAFLOW),
    )(*srcs, *lands, send_sems, recv_sems, after)
    return list(outs[:n]), list(outs[n:])


def _pass_to_sibling(gathered, nchs, name):
    n = len(gathered)
    jobs = _ici_jobs([jax.ShapeDtypeStruct(g.shape[1:], g.dtype) for g in gathered], nchs, "gather")

    def body(*refs):
        src, dst, (send_sems, recv_sems) = refs[:n], refs[n:2 * n], refs[2 * n:]
        x, y, c = _place()
        slots = [2 * (1 - x) + y, 2 * x + (1 - y), 2 * (1 - x) + (1 - y)]

        def copy(k, j, pc):
            a, h, r0, rows = jobs[k]
            at = pl.ds(pl.multiple_of(pc * h + r0, 16), rows)
            return pltpu.make_async_remote_copy(
                src_ref=src[a].at[slots[j], at, :], dst_ref=dst[a].at[slots[j], at, :], send_sem=send_sems.at[3 * k + j],
                recv_sem=recv_sems.at[3 * k + j], device_id=(x, y, 1 - c), device_id_type=MESH)

        pairs = [(k, j) for k in range(len(jobs)) for j in range(3)]
        for k, j in pairs:
            copy(k, j, c).start()
        for k, j in pairs:
            copy(k, j, c).wait_send()
            copy(k, j, 1 - c).wait_recv()

    return pl.pallas_call(
        body, name=name, out_shape=[jax.ShapeDtypeStruct(g.shape, g.dtype) for g in gathered],
        in_specs=[_ANY] * n, out_specs=[_ANY] * n, input_output_aliases={i: i for i in range(n)},
        scratch_shapes=[pltpu.SemaphoreType.DMA((3 * len(jobs),)), pltpu.SemaphoreType.DMA((3 * len(jobs),))])(*gathered)


def _add4(own, got, name):
    rows, w = own.shape
    tr = _pick(rows, 128, 16)

    def body(a_ref, b_ref, o_ref):
        o_ref[...] = ((a_ref[...].astype(F32) + b_ref[0].astype(F32)) + b_ref[1].astype(F32)) + b_ref[2].astype(F32)

    return pl.pallas_call(
        body, name=name, grid=(rows // tr,),
        in_specs=[pl.BlockSpec((tr, w), lambda i: (i, 0)), pl.BlockSpec((3, tr, w), lambda i: (0, i, 0))],
        out_specs=pl.BlockSpec((tr, w), lambda i: (i, 0)), out_shape=jax.ShapeDtypeStruct((rows, w), F32),
        compiler_params=_params(_PAR))(own, got)


def _small_sync(gs, ws, ms, vs):
    rows = gs.shape[0]
    vmem = pl.BlockSpec(memory_space=pltpu.VMEM)

    def body(g_ref, w_ref, m_ref, v_ref, sum_ref, d_ref, m2_ref, v2_ref, buf, send_sems, recv_sems):
        x, y, c = _place()
        me = 4 * x + 2 * y + c
        buf[me] = g_ref[...]
        copies = []
        for k in range(1, 8):
            peer = (x ^ (k >> 2), y ^ ((k >> 1) & 1), c ^ (k & 1))
            copies.append(pltpu.make_async_remote_copy(
                src_ref=g_ref, dst_ref=buf.at[me], send_sem=send_sems.at[k - 1], recv_sem=recv_sems.at[k - 1],
                device_id=peer, device_id_type=MESH))
        for cp in copies:
            cp.start()
        for cp in copies:
            cp.wait()
        total = buf[0]
        for i in range(1, 8):
            total = total + buf[i]
        sum_ref[...] = total
        d, m2, v2 = _adam_math(w_ref[...], total, m_ref[...], v_ref[...])
        d_ref[...] = d
        m2_ref[...] = m2
        v2_ref[...] = v2

    shape = jax.ShapeDtypeStruct((rows, 128), F32)
    return pl.pallas_call(
        body, name="small_sync", out_shape=[shape] * 4, in_specs=[vmem] * 4, out_specs=[vmem] * 4,
        scratch_shapes=[pltpu.VMEM((8, rows, 128), F32), pltpu.SemaphoreType.DMA((7,)),
                        pltpu.SemaphoreType.DMA((7,))])(gs, ws, ms, vs)


_GROUPS = {
    "ffn1": dict(cols=("ffn1_w_in", 1408), rows=(("ffn1_w_out", 704, 704),), chunks=(8, 2)),
    "ffn2": dict(cols=("ffn2_w_in", 1408), rows=(("ffn2_w_out", 704, 704),), chunks=(8, 2)),
    "mixer_in": dict(cols=("w_in", 3080), rows=(("gdn_conv_w", CONV_K, 128),), chunks=(8, 1)),
    "mixer_out": dict(cols=None, chunks=(4,),
                      rows=(("w_branch_hgrn", 256, 256), ("w_branch_gdn", 512, 512), ("w_out", 256, 256))),
}


def _group_names(group):
    return ((group["cols"][0],) if group["cols"] else ()) + tuple(r[0] for r in group["rows"])


_BIG_NAMES = tuple(n for g in _GROUPS.values() for n in _group_names(g))


def _pack(parts, lead, group):
    ax = len(lead)
    rows = []
    for n, r, padded in group["rows"]:
        p = parts[n]
        if padded != r:
            p = jnp.tile(p, (1,) * ax + (padded // r, 1))
        rows.append(p)
    stacked = rows[0] if len(rows) == 1 else jnp.concatenate(rows, axis=ax)
    return ([parts[group["cols"][0]]] if group["cols"] else []) + [stacked]


def _unpack(packs, group):
    out, off = ({group["cols"][0]: packs[0]} if group["cols"] else {}), 0
    for n, r, padded in group["rows"]:
        out[n] = packs[-1][..., off:off + r, :]
        off += padded
    return out


def _is_col_sharded(name):
    return name in ("ffn1_w_in", "ffn2_w_in", "w_in", "gdn_conv_w")


def _full_from_shards(name, g):
    if _is_col_sharded(name):
        return jnp.transpose(g, (1, 0, 2)).reshape(g.shape[1], -1)
    return g.reshape(-1, g.shape[2])


def _shards_from_full(name, full):
    if _is_col_sharded(name):
        return jnp.transpose(full.reshape(full.shape[0], 4, -1), (1, 0, 2))
    return full.reshape(4, -1, full.shape[1])


_SMALL = (("ffn1_norm", 8), ("mix_norm", 8), ("hgrn_lb_logits", 16), ("hgrn_out_norm", 8), ("gdn_a_log", 8),
          ("gdn_dt_bias", 8), ("gdn_out_norm", 8), ("ffn2_norm", 8), ("final_norm", 8), ("loss", 8))
_SMALL_ROWS = sum(r for _, r in _SMALL)


def _pack_small(parts):
    out = []
    for name, rows in _SMALL:
        p = parts[name].reshape(-1).astype(F32)
        if p.shape[0] <= 128:
            if p.shape[0] < 128:
                p = jnp.concatenate([p, jnp.zeros((128 - p.shape[0],), F32)])
            p = jnp.broadcast_to(p.reshape(1, 128), (rows, 128))
        out.append(p.reshape(rows, 128))
    return jnp.concatenate(out, axis=0)


def _unpack_small(packed, shapes):
    out, off = {}, 0
    for name, rows in _SMALL:
        n = int(np.prod(shapes[name]))
        out[name] = packed[off:off + rows].reshape(-1)[:n].reshape(shapes[name])
        off += rows
    return out


def _ffn_fwd(x, gain, w_in, w_out, tag):
    n = _rmsnorm_fwd(x, gain, tag + "_norm")
    a, b, hm = _ffn_in_act(n, w_in, tag + "_in")
    out = _mm(hm, w_out, alpha=0.5, res=x, name=tag + "_out")
    return out, (n, a, b)


def _ffn_bwd(x, gain, w_in, w_out, saved, dout, dout_bf, tag):
    n, a, b = saved
    da, db, hm = _ffn_dact(dout_bf, w_out, a, b, tag + "_dact")
    dw_out = _mm(hm, dout_bf, ta=True, alpha=0.5, out_dtype=BF16, name=tag + "_dwout")
    dwa = _mm(n, da, ta=True, out_dtype=BF16, name=tag + "_dwin_a")
    dwb = _mm(n, db, ta=True, out_dtype=BF16, name=tag + "_dwin_b")
    half = D_FF // 2
    dw_in = jnp.stack([dwa[:, :half], dwa[:, half:], dwb[:, :half], dwb[:, half:]])
    dn = _mm(da, w_in, tb=True, name=tag + "_dnorm_a")
    dn = _mm(db, w_in, tb=True, res=dn, b_from=D_FF, name=tag + "_dnorm_b")
    dx, dx_bf, dgain = _rmsnorm_bwd(x, gain, dn, dout, tag + "_dx")
    return dx, dx_bf, dgain, dw_in, dw_out


def _pad_lanes(v):
    return jnp.concatenate([v.reshape(1, -1), jnp.zeros((1, HEAD - v.size), F32)], axis=1)


def _local_step(x, tgt, small, exchange):
    hg_c = _hg_consts()
    gd_c = _gd_consts()
    alog = _pad_lanes(small["gdn_a_log"])
    dtb = _pad_lanes(small["gdn_dt_bias"])
    logits = small["hgrn_lb_logits"]
    hg_gain = small["hgrn_out_norm"].reshape(1, HEAD)
    gd_gain = small["gdn_out_norm"].reshape(1, HEAD)
    g1, gm, g2 = small["ffn1_norm"].reshape(1, -1), small["mix_norm"].reshape(1, -1), small["ffn2_norm"].reshape(1, -1)
    gf = small["final_norm"].reshape(1, -1)
    qscale = HEAD ** -0.5

    w1 = exchange.weights("ffn1")
    started = exchange.prefetch("mixer_in")
    h1, ffn1_saved = _ffn_fwd(x, g1 + started, w1["ffn1_w_in"], w1["ffn1_w_out"], "ffn1")
    u = _rmsnorm_fwd(h1, gm, "mix_norm")
    w = exchange.weights("mixer_in", after=u)
    arrived = w["gdn_conv_w"]
    started = exchange.prefetch("mixer_out", arrived) + exchange.prefetch("ffn2", arrived)
    seg, off = {}, 0
    for name, size in zip(IN_NAMES, IN_SIZES):
        seg[name] = w["w_in"][:, off:off + size]
        off += size
    w_gab = jnp.concatenate([seg["ga"], seg["gb"], jnp.zeros((D_MODEL, HEAD - 32), BF16)], axis=1)
    big_segs = [n for n in IN_NAMES if n not in ("ga", "gb")]
    conv8 = jnp.concatenate([w["gdn_conv_w"].astype(F32), jnp.zeros((8 - CONV_K, 4096), F32)], axis=0)
    conv_q, conv_k, conv_v = conv8[:, :1024], conv8[:, 1024:2048], conv8[:, 2048:]
    w_main = jnp.concatenate([seg[n] for n in big_segs], axis=1)
    proj = _mm(u, w_main, name="proj", tm_max=2048)
    pr, off = {}, 0
    for n in big_segs:
        pr[n] = _view(proj, off, seg[n].shape[1])
        off += seg[n].shape[1]
    gab = _mm(u, w_gab, name="proj_gab")
    oh_raw, oh, s_h = _hgrn_fwd(pr["hq"], pr["hf"], pr["hi"], pr["hg"], logits, hg_gain + started, hg_c)
    qn = _conv_fwd(pr["gq"], conv_q, qscale, "conv_q")
    kn = _conv_fwd(pr["gk"], conv_k, 1.0, "conv_k")
    cv = _conv_fwd(pr["gv"], conv_v, None, "conv_v")
    og_raw, og, s_g, t_g = _gdn_fwd(qn, kn, cv, gab, pr["gz"], alog, dtb, gd_gain, gd_c)
    wo = exchange.weights("mixer_out", after=og)
    yh = _mm(oh, wo["w_branch_hgrn"], out_dtype=BF16, name="branch_h")
    yg = _mm(og, wo["w_branch_gdn"], out_dtype=BF16, name="branch_g")
    ym = _merge_fwd(yh, yg, pr["gate_h"], pr["gate_g"])
    h2 = _mm(ym, wo["w_out"], res=h1, name="mix_out")
    w2 = exchange.weights("ffn2", after=h2)
    h3, ffn2_saved = _ffn_fwd(h2, g2, w2["ffn2_w_in"], w2["ffn2_w_out"], "ffn2")
    loss, dh3, dh3_bf, d_gf = _final_loss(h3, gf, tgt)

    dh2, dh2_bf, d_g2, d_f2in, d_f2out = _ffn_bwd(h2, g2, w2["ffn2_w_in"], w2["ffn2_w_out"], ffn2_saved, dh3, dh3_bf,
                                                  "ffn2")
    started = exchange.reduce("ffn2", {"ffn2_w_in": d_f2in, "ffn2_w_out": d_f2out}, behind=True)
    dym = _mm(dh2_bf, wo["w_out"], tb=True, name="d_merge")
    d_wout = _mm(ym, dh2_bf, ta=True, out_dtype=BF16, name="d_w_out")
    dproj = lax.empty((x.shape[0], w_main.shape[1]), BF16)
    dyh, dyg, dproj = _merge_bwd(dym, yh, yg, pr["gate_h"], pr["gate_g"], _into(dproj, pr["gate_h"][1], 2 * D_MODEL))
    d_wbh = _mm(oh, dyh, ta=True, out_dtype=BF16, name="d_w_branch_h")
    d_wbg = _mm(og, dyg, ta=True, out_dtype=BF16, name="d_w_branch_g")
    started = started + exchange.reduce("mixer_out", {"w_branch_hgrn": d_wbh, "w_branch_gdn": d_wbg, "w_out": d_wout},
                                        behind=True)
    doh = _mm(dyh, wo["w_branch_hgrn"], tb=True, name="d_oh")
    dog = _mm(dyg, wo["w_branch_gdn"], tb=True, name="d_og")
    dproj, d_hg_gain, d_lb0 = _hgrn_bwd(pr["hq"], pr["hf"], pr["hi"], pr["hg"], logits, hg_gain + started, oh_raw,
                                        s_h, doh, hg_c, _into(dproj, pr["hq"][1], 4 * D_MODEL))
    d_qn, d_kn, d_cv, d_gab_wide, dproj, gd_small = _gdn_bwd(qn, kn, cv, gab, pr["gz"], alog, dtb, gd_gain, og_raw,
                                                             s_g, t_g, dog, gd_c, _into(dproj, *pr["gz"][1:]))
    d_gab = _fold_groups(d_gab_wide)
    dc_q, dwc_q = _conv_bwd_a(pr["gq"], conv_q, d_qn, qscale, "dconv_q")
    dc_k, dwc_k = _conv_bwd_a(pr["gk"], conv_k, d_kn, 1.0, "dconv_k")
    dc_v, dwc_v = _conv_bwd_a(pr["gv"], conv_v, d_cv, None, "dconv_v")
    dproj = _conv_bwd_b(dc_q, conv_q, "dconvx_q", _into(dproj, *pr["gq"][1:]))
    dproj = _conv_bwd_b(dc_k, conv_k, "dconvx_k", _into(dproj, *pr["gk"][1:]))
    dproj = _conv_bwd_b(dc_v, conv_v, "dconvx_v", _into(dproj, *pr["gv"][1:]))
    du =_mm(d_gab, w_gab, tb=True, name="du_gab")
    du = _mm(dproj, w_main, tb=True, res=du, name="du")
    d_wmain = _mm(u, dproj, ta=True, out_dtype=BF16, name="dw_main")
    d_wgab = _mm(u, d_gab, ta=True, out_dtype=BF16, name="dw_gab")
    cut = IN_WIDTH // 4
    d_win = jnp.stack([d_wmain[:, :cut], d_wmain[:, cut:2 * cut],
                       jnp.concatenate([d_wmain[:, 2 * cut:8192], d_wgab[:, :32], d_wmain[:, 8192:3 * cut - 32]], axis=1),
                       d_wmain[:, 3 * cut - 32:]])
    d_conv = jnp.concatenate([dwc_q[:CONV_K], dwc_k[:CONV_K], dwc_v[:CONV_K]], axis=1).astype(BF16)
    started = exchange.reduce("mixer_in", {"w_in": d_win, "gdn_conv_w": d_conv}, behind=True)
    dh1, dh1_bf, d_gm = _rmsnorm_bwd(h1, gm + started, du, dh2, "mix_dnorm")
    dx, _, d_g1, d_f1in, d_f1out = _ffn_bwd(x, g1, w1["ffn1_w_in"], w1["ffn1_w_out"], ffn1_saved, dh1, dh1_bf, "ffn1")
    exchange.reduce("ffn1", {"ffn1_w_in": d_f1in, "ffn1_w_out": d_f1out}, behind=True)
    d_lb0 = d_lb0.reshape(1, -1)
    sm = {"ffn1_norm": d_g1, "mix_norm": d_gm, "hgrn_lb_logits": jnp.concatenate([d_lb0, -d_lb0], axis=0),
          "hgrn_out_norm": d_hg_gain, "gdn_a_log": gd_small[2, :16], "gdn_dt_bias": gd_small[1, :16],
          "gdn_out_norm": gd_small[0], "ffn2_norm": d_g2, "final_norm": d_gf, "loss": loss[0, :1]}
    return dx, sm


class _Exchange:
    def __init__(self, wts):
        self.wts = wts
        xi, yi, ci = _place()
        self.chip = 2 * xi + yi
        self.south = ci == 0
        self.core = ci.reshape(1).astype(jnp.int32)
        self.mine = {}
        self.coming = {}
        self.going = {}

    def _packs(self, tag):
        group = _GROUPS[tag]
        return _pack({n: self.wts[n][0].astype(BF16) for n in _group_names(group)}, (), group)

    def prefetch(self, tag, after=None):
        packs = self._packs(tag)
        handle, token = _ici_start(packs, _GROUPS[tag]["chunks"], "gather", "gather_start_" + tag, after)
        self.coming[tag] = handle
        return token[0:1, 0:1]

    def weights(self, tag, after=None):
        group = _GROUPS[tag]
        if tag in self.coming:
            packs, halves = _ici_wait(self.coming.pop(tag), after, "gather_wait_" + tag)
            others = _pass_to_sibling(halves, group["chunks"], "gather_pass_" + tag)
        else:
            packs = self._packs(tag)
            others = _gather_weights(packs, group["chunks"], "gather_" + tag)
        whole = [lax.dynamic_update_index_in_dim(g, p, self.chip, 0) for g, p in zip(others, packs)]
        gathered = _unpack(whole, group)
        return {n: _full_from_shards(n, gathered[n]) for n in _group_names(group)}

    def reduce(self, tag, grads, behind=False):
        group = _GROUPS[tag]
        shards = {n: (grads[n] if grads[n].ndim == 3 else _shards_from_full(n, grads[n])) for n in _group_names(group)}
        gpacks = _pack(shards, (4,), group)
        got = _swap_with_sibling(gpacks, group["chunks"], 4, "reduce_pair_" + tag, halves=True)
        sums = [_add2(a, b, self.core, "add_pair_%s_%d" % (tag, i)) for i, (a, b) in enumerate(zip(gpacks, got))]
        if behind:
            handle, token = _ici_start(sums, group["chunks"], "reduce", "reduce_start_" + tag)
            self.going[tag] = handle
            self.token = token
            return token[0:1, 0:1]
        self._add_chips(tag, sums, _reduce_chips(sums, group["chunks"], "reduce_chips_" + tag))
        return None

    def _add_chips(self, tag, sums, from_chips):
        self.mine[tag] = [_add4(lax.dynamic_index_in_dim(s, self.chip, axis=0, keepdims=False), f,
                                "add_chips_%s_%d" % (tag, i)) for i, (s, f) in enumerate(zip(sums, from_chips))]

    def finish(self, tags, after):
        for tag in tags:
            if tag in self.going:
                self._add_chips(tag, *_ici_wait(self.going.pop(tag), after, "reduce_wait_" + tag))
        mine = [a for t in tags for a in self.mine[t]]
        nchs = [k for t in tags for k in _GROUPS[t]["chunks"]]
        theirs = _swap_with_sibling(mine, nchs, 0, "share_pair_" + tags[0])
        whole = [jnp.concatenate([jnp.where(self.south, a, b), jnp.where(self.south, b, a)], axis=0)
                 for a, b in zip(mine, theirs)]
        reduced, at = {}, 0
        for t in tags:
            n = len(self.mine[t])
            reduced.update(_unpack(whole[at:at + n], _GROUPS[t]))
            at += n
        return reduced


_WEIGHTS = ("ffn1_norm", "ffn1_w_in", "ffn1_w_out", "mix_norm", "w_in", "hgrn_lb_logits", "hgrn_out_norm",
            "gdn_conv_w", "gdn_a_log", "gdn_dt_bias", "gdn_out_norm", "w_branch_hgrn", "w_branch_gdn", "w_out",
            "ffn2_norm", "ffn2_w_in", "ffn2_w_out", "final_norm")


def kernel(x, ffn1_norm, ffn1_w_in, ffn1_w_out, mix_norm, w_in, hgrn_lb_logits, hgrn_out_norm, gdn_conv_w, gdn_a_log, gdn_dt_bias, gdn_out_norm, w_branch_hgrn, w_branch_gdn, w_out, ffn2_norm, ffn2_w_in, ffn2_w_out, final_norm, loss_target, m_ffn1_norm, m_ffn1_w_in, m_ffn1_w_out, m_mix_norm, m_w_in, m_hgrn_lb_logits, m_hgrn_out_norm, m_gdn_conv_w, m_gdn_a_log, m_gdn_dt_bias, m_gdn_out_norm, m_w_branch_hgrn, m_w_branch_gdn, m_w_out, m_ffn2_norm, m_ffn2_w_in, m_ffn2_w_out, m_final_norm, v_ffn1_norm, v_ffn1_w_in, v_ffn1_w_out, v_mix_norm, v_w_in, v_hgrn_lb_logits, v_hgrn_out_norm, v_gdn_conv_w, v_gdn_a_log, v_gdn_dt_bias, v_gdn_out_norm, v_w_branch_hgrn, v_w_branch_gdn, v_w_out, v_ffn2_norm, v_ffn2_w_in, v_ffn2_w_out, v_final_norm):
    args = dict(locals())
    wts = {n: args[n] for n in _WEIGHTS}
    moms = {n: args["m_" + n] for n in _WEIGHTS}
    vars_ = {n: args["v_" + n] for n in _WEIGHTS}

    small = {n: wts[n].astype(F32) for n in _WEIGHTS if n not in _BIG_NAMES}
    exchange = _Exchange(wts)
    dx, small_grads = _local_step(x[0], loss_target[0], small, exchange)

    out_g, out_d, out_m, out_v = {}, {}, {}, {}

    def update(tags, reduced, after):
        for t in tags:
            for n in _group_names(_GROUPS[t]):
                shape = wts[n].shape
                w2 = wts[n].reshape(shape[-2], shape[-1])
                g2 = reduced[n]
                d, m2, v2 = _adamw(w2, g2, moms[n].reshape(w2.shape), vars_[n].reshape(w2.shape), "adamw_" + n, after)
                out_g[n], out_d[n], out_m[n], out_v[n] = (g2.reshape(shape), d.reshape(shape), m2.reshape(shape),
                                                          v2.reshape(shape))
                after = v2
        return after

    early = ("ffn2", "mixer_out", "mixer_in")
    done = update(early, exchange.finish(early, after=dx), exchange.token)
    update(("ffn1",), exchange.finish(("ffn1",), after=done), None)

    small_names = [n for n, _ in _SMALL]
    zero = jnp.zeros((1,), F32)
    shapes = {n: (wts[n].shape if n != "loss" else (1,)) for n in small_names}
    sums, sd, sm_, sv = _small_sync(
        _pack_small(small_grads),
        _pack_small({n: (wts[n] if n != "loss" else zero) for n in small_names}),
        _pack_small({n: (moms[n] if n != "loss" else zero) for n in small_names}),
        _pack_small({n: (vars_[n] if n != "loss" else zero) for n in small_names}))
    sg_u, sd_u, sm_u, sv_u = (_unpack_small(p, shapes) for p in (sums, sd, sm_, sv))
    for n in small_names:
        if n != "loss":
            out_g[n], out_d[n], out_m[n], out_v[n] = sg_u[n], sd_u[n], sm_u[n], sv_u[n]
    loss = sg_u["loss"].reshape(())

    return (loss, dx[None], *[out_g[n] for n in _WEIGHTS], *[out_d[n] for n in _WEIGHTS],
            *[out_m[n] for n in _WEIGHTS], *[out_v[n] for n in _WEIGHTS])
```

```python
import numpy as np

import jax
import jax.numpy as jnp
from jax import lax
from jax.experimental import pallas as pl
from jax.experimental.pallas import tpu as pltpu

F32 = jnp.float32
BF16 = jnp.bfloat16

D_MODEL = 1024
D_FF = 2816
CHUNK = 64
HEAD = 128
HG_HEADS = 8
GD_HEADS = 16
HPS = 8
MM_TM = 1408
MM_TN = 1024
MM_TK = 2048
VMEM_LIMIT = 48 * 1024 * 1024
ROW_TILE = 512
EPS = 1e-6
CONV_K = 4
IN_NAMES = ("hq", "hf", "hi", "hg", "gq", "gk", "gv", "ga", "gb", "gz", "gate_h", "gate_g")
IN_SIZES = (1024, 1024, 1024, 1024, 1024, 1024, 2048, 16, 16, 2048, 1024, 1024)
IN_WIDTH = sum(IN_SIZES)

ADAM_LR = 0.001
ADAM_B1 = 0.9
ADAM_B2 = 0.999
ADAM_EPS = 1e-08
ADAM_WD = 0.01
ADAM_STEP = 10

MESH = pl.DeviceIdType.MESH
_ARB = "arbitrary"
_PAR = "parallel"


def _bf(x):
    return x.astype(BF16)


def _dot(a, b):
    return jnp.dot(_bf(a), _bf(b), preferred_element_type=F32)


def _dot_nt(a, b):
    return lax.dot_general(_bf(a), _bf(b), (((1,), (1,)), ((), ())), preferred_element_type=F32)


def _dot_tn(a, b):
    return lax.dot_general(_bf(a), _bf(b), (((0,), (0,)), ((), ())), preferred_element_type=F32)


def _sigmoid(x):
    return jax.nn.sigmoid(x)


def _silu(x):
    return x * _sigmoid(x)


def _dsilu(x):
    s = _sigmoid(x)
    return s * (1.0 + x * (1.0 - s))


def _softplus(x):
    return jnp.maximum(x, 0.0) + jnp.log(1.0 + jnp.exp(-jnp.abs(x)))


def _rowsum(x):
    return jnp.sum(x, axis=1, keepdims=True)


def _col_to_row(col, eye):
    return jnp.sum(eye * col, axis=0, keepdims=True)


def _row_to_col(row, eye):
    return jnp.sum(eye * row, axis=1, keepdims=True)


def _pick(dim, pref, unit=128):
    if dim <= pref:
        return dim
    t = pref
    while t >= unit:
        if dim % t == 0:
            return t
        t -= unit
    return dim


def _params(*sem):
    return pltpu.CompilerParams(dimension_semantics=tuple(sem), vmem_limit_bytes=VMEM_LIMIT)


def _mm(a, b, *, ta=False, tb=False, alpha=1.0, res=None, out_dtype=F32, name="mm", b_from=0, tm_max=MM_TM):
    m = a.shape[1] if ta else a.shape[0]
    k = a.shape[0] if ta else a.shape[1]
    n = b.shape[0] if tb else b.shape[1]
    assert b_from + k <= (b.shape[1] if tb else b.shape[0])
    tm, tn, tk = _pick(m, tm_max), _pick(n, MM_TN), _pick(k, MM_TK)
    if tn < MM_TN < n and n % MM_TM == 0:
        tn = MM_TM
    nk = k // tk
    assert b_from % tk == 0
    b0 = b_from // tk
    a_spec = pl.BlockSpec((tk, tm), lambda i, j, l: (l, i)) if ta else pl.BlockSpec((tm, tk), lambda i, j, l: (i, l))
    b_spec = (pl.BlockSpec((tn, tk), lambda i, j, l: (j, b0 + l)) if tb
              else pl.BlockSpec((tk, tn), lambda i, j, l: (b0 + l, j)))
    o_spec = pl.BlockSpec((tm, tn), lambda i, j, l: (i, j))
    dims = (((0 if ta else 1,), (1 if tb else 0,)), ((), ()))
    has_res = res is not None

    def finish(r, r_ref, o_ref):
        if alpha != 1.0:
            r = r * alpha
        if has_res:
            r = r + r_ref[...]
        o_ref[...] = r.astype(out_dtype)

    def body(*refs):
        a_ref, b_ref = refs[0], refs[1]
        r_ref = refs[2] if has_res else None
        o_ref = refs[3] if has_res else refs[2]
        part = lax.dot_general(_bf(a_ref[...]), _bf(b_ref[...]), dims, preferred_element_type=F32)
        if nk == 1:
            finish(part, r_ref, o_ref)
            return
        acc = refs[-1]
        step = pl.program_id(2)

        @pl.when(step == 0)
        def _():
            acc[...] = part

        @pl.when(step != 0)
        def _():
            acc[...] += part

        @pl.when(step == nk - 1)
        def _():
            finish(acc[...], r_ref, o_ref)

    ins = [a, b] + ([res] if has_res else [])
    in_specs = [a_spec, b_spec] + ([o_spec] if has_res else [])
    return pl.pallas_call(
        body, name=name, grid=(m // tm, n // tn, nk), in_specs=in_specs, out_specs=o_spec,
        out_shape=jax.ShapeDtypeStruct((m, n), out_dtype),
        scratch_shapes=[pltpu.VMEM((tm, tn), F32)] if nk > 1 else [],
        compiler_params=_params(_PAR, _PAR, _ARB))(*ins)


def _row_spec(tr, w):
    return pl.BlockSpec((tr, w), lambda i: (i, 0))


def _full_spec(shape):
    return pl.BlockSpec(shape, lambda i: tuple(0 for _ in shape))


def _view(arr, off, width):
    return arr, off, width


def _view_rows(view, tr):
    _, off, width = view
    assert off % width == 0
    return pl.BlockSpec((tr, width), lambda i: (i, off // width))


def _view_tile(view, rows, bw, cidx=lambda c: c):
    _, off, width = view
    assert off % bw == 0 and width % bw == 0
    return pl.BlockSpec((rows, bw), lambda c, g: (cidx(c), off // bw + g))


def _rmsnorm_fwd(x, g, name):
    t, d = x.shape
    tr = _pick(t, ROW_TILE, 8)

    def body(x_ref, g_ref, o_ref):
        xv = x_ref[...]
        r = lax.rsqrt(jnp.mean(xv * xv, axis=1, keepdims=True) + EPS)
        o_ref[...] = (xv * r * g_ref[...]).astype(BF16)

    return pl.pallas_call(
        body, name=name, grid=(t // tr,), in_specs=[_row_spec(tr, d), _full_spec((1, d))],
        out_specs=_row_spec(tr, d), out_shape=jax.ShapeDtypeStruct((t, d), BF16),
        compiler_params=_params(_PAR))(x, g)


def _rmsnorm_bwd(x, g, dn, res, name):
    t, d = x.shape
    tr = _pick(t, ROW_TILE, 8)

    def body(x_ref, g_ref, dn_ref, r_ref, dx_ref, dxb_ref, dg_ref):
        @pl.when(pl.program_id(0) == 0)
        def _():
            dg_ref[...] = jnp.zeros_like(dg_ref)

        xv = x_ref[...]
        r = lax.rsqrt(jnp.mean(xv * xv, axis=1, keepdims=True) + EPS)
        xh = xv * r
        dy = dn_ref[...]
        dg_ref[...] += jnp.sum(dy * xh, axis=0, keepdims=True)
        dxh = dy * g_ref[...]
        dx = r_ref[...] + r * (dxh - xh * jnp.mean(dxh * xh, axis=1, keepdims=True))
        dx_ref[...] = dx
        dxb_ref[...] = dx.astype(BF16)

    return pl.pallas_call(
        body, name=name, grid=(t // tr,),
        in_specs=[_row_spec(tr, d), _full_spec((1, d)), _row_spec(tr, d), _row_spec(tr, d)],
        out_specs=[_row_spec(tr, d), _row_spec(tr, d), _full_spec((1, d))],
        out_shape=[jax.ShapeDtypeStruct((t, d), F32), jax.ShapeDtypeStruct((t, d), BF16),
                   jax.ShapeDtypeStruct((1, d), F32)],
        compiler_params=_params(_ARB))(x, g, dn, res)


FFN_TN = 1408
FFN_TM = 512


def _ffn_in_act(n, w_in, name):
    t, d = n.shape
    tm = _pick(t, FFN_TM)
    nf = D_FF // FFN_TN

    def body(n_ref, wa_ref, wb_ref, a_ref, b_ref, hm_ref):
        nv = n_ref[...]
        a = jnp.dot(nv, wa_ref[...], preferred_element_type=F32)
        b = jnp.dot(nv, wb_ref[...], preferred_element_type=F32)
        a_ref[...] = a.astype(BF16)
        b_ref[...] = b.astype(BF16)
        hm_ref[...] = (_silu(a) * b).astype(BF16)

    tile = pl.BlockSpec((tm, FFN_TN), lambda i, j: (i, j))
    return pl.pallas_call(
        body, name=name, grid=(t // tm, nf),
        in_specs=[pl.BlockSpec((tm, d), lambda i, j: (i, 0)), pl.BlockSpec((d, FFN_TN), lambda i, j: (0, j)),
                  pl.BlockSpec((d, FFN_TN), lambda i, j: (0, nf + j))],
        out_specs=[tile, tile, tile], out_shape=[jax.ShapeDtypeStruct((t, D_FF), BF16)] * 3,
        compiler_params=_params(_PAR, _PAR))(n, w_in, w_in)


def _ffn_dact(dout, w_out, a, b, name):
    t, d = dout.shape
    tm = _pick(t, FFN_TM)

    def body(do_ref, w_ref, a_ref, b_ref, da_ref, db_ref, hm_ref):
        dh = 0.5 * _dot_nt(do_ref[...], w_ref[...])
        av = a_ref[...].astype(F32)
        bv = b_ref[...].astype(F32)
        sg = _sigmoid(av)
        sa = av * sg
        da_ref[...] = (dh * bv * (sg * (1.0 + av * (1.0 - sg)))).astype(BF16)
        db_ref[...] = (dh * sa).astype(BF16)
        hm_ref[...] = (sa * bv).astype(BF16)

    tile = pl.BlockSpec((tm, FFN_TN), lambda i, j: (i, j))
    return pl.pallas_call(
        body, name=name, grid=(t // tm, D_FF // FFN_TN),
        in_specs=[pl.BlockSpec((tm, d), lambda i, j: (i, 0)), pl.BlockSpec((FFN_TN, d), lambda i, j: (j, 0)), tile, tile],
        out_specs=[tile, tile, tile], out_shape=[jax.ShapeDtypeStruct((t, D_FF), BF16)] * 3,
        compiler_params=_params(_PAR, _PAR))(dout, w_out, a, b)


def _merge_fwd(yh, yg, gh, gg):
    t, d = yh.shape
    tr = _pick(t, ROW_TILE, 8)

    def body(yh_ref, yg_ref, gh_ref, gg_ref, o_ref):
        o_ref[...] = (_sigmoid(gh_ref[...]) * yh_ref[...] + _sigmoid(gg_ref[...]) * yg_ref[...]).astype(BF16)

    return pl.pallas_call(
        body, name="merge_fwd", grid=(t // tr,),
        in_specs=[_row_spec(tr, d), _row_spec(tr, d), _view_rows(gh, tr), _view_rows(gg, tr)],
        out_specs=_row_spec(tr, d),
        out_shape=jax.ShapeDtypeStruct((t, d), BF16), compiler_params=_params(_PAR))(yh, yg, gh[0], gg[0])


def _into(dproj, off, width):
    return dproj, off, width


def _merge_bwd(dy, yh, yg, gh, gg, into):
    t, d = yh.shape
    tr = _pick(t, ROW_TILE, 8)
    dproj, off, width = into
    assert width == 2 * d and off % width == 0

    def body(dy_ref, yh_ref, yg_ref, gh_ref, gg_ref, _, dyh_ref, dyg_ref, dg_ref):
        dyv = dy_ref[...]
        sh = _sigmoid(gh_ref[...])
        sg = _sigmoid(gg_ref[...])
        dyh_ref[...] = (dyv * sh).astype(BF16)
        dyg_ref[...] = (dyv * sg).astype(BF16)
        dg_ref[:, :d] = (dyv * yh_ref[...] * sh * (1.0 - sh)).astype(BF16)
        dg_ref[:, d:] = (dyv * yg_ref[...] * sg * (1.0 - sg)).astype(BF16)

    return pl.pallas_call(
        body, name="merge_bwd", grid=(t // tr,),
        in_specs=[_row_spec(tr, d)] * 3 + [_view_rows(gh, tr), _view_rows(gg, tr), _ANY],
        out_specs=[_row_spec(tr, d)] * 2 + [pl.BlockSpec((tr, width), lambda i: (i, off // width))],
        out_shape=[jax.ShapeDtypeStruct((t, d), BF16)] * 2 + [jax.ShapeDtypeStruct(dproj.shape, dproj.dtype)],
        input_output_aliases={5: 2},
        compiler_params=_params(_PAR))(dy, yh, yg, gh[0], gg[0], dproj)


def _final_loss(h, g, tgt):
    t, d = h.shape
    tr = _pick(t, ROW_TILE, 8)

    def body(h_ref, g_ref, t_ref, loss_ref, dh_ref, dhb_ref, dg_ref):
        @pl.when(pl.program_id(0) == 0)
        def _():
            dg_ref[...] = jnp.zeros_like(dg_ref)
            loss_ref[...] = jnp.zeros_like(loss_ref)

        xv = h_ref[...]
        gv = g_ref[...]
        r = lax.rsqrt(jnp.mean(xv * xv, axis=1, keepdims=True) + EPS)
        xh = xv * r
        err = xh * gv - t_ref[...]
        loss_ref[...] += 0.5 * jnp.sum(jnp.mean(err * err, axis=1, keepdims=True), axis=0, keepdims=True)
        dy = err * (1.0 / d)
        dg_ref[...] += jnp.sum(dy * xh, axis=0, keepdims=True)
        dxh = dy * gv
        dh = r * (dxh - xh * jnp.mean(dxh * xh, axis=1, keepdims=True))
        dh_ref[...] = dh
        dhb_ref[...] = dh.astype(BF16)

    return pl.pallas_call(
        body, name="final_loss", grid=(t // tr,),
        in_specs=[_row_spec(tr, d), _full_spec((1, d)), _row_spec(tr, d)],
        out_specs=[_full_spec((1, 128)), _row_spec(tr, d), _row_spec(tr, d), _full_spec((1, d))],
        out_shape=[jax.ShapeDtypeStruct((1, 128), F32), jax.ShapeDtypeStruct((t, d), F32),
                   jax.ShapeDtypeStruct((t, d), BF16), jax.ShapeDtypeStruct((1, d), F32)],
        compiler_params=_params(_ARB))(h, g, tgt)


def _hg_consts():
    c = CHUNK
    t = np.arange(c)
    mats, masks = [], []
    for lvl in range(6):
        m = 1 << lvl
        blk = t // m
        mat = np.zeros((c, c), np.float32)
        for tt in range(c):
            b = blk[tt]
            if b % 2 == 1:
                mat[tt, b * m:tt + 1] = 1.0
            else:
                mat[tt, tt + 1:(b + 1) * m] = 1.0
        mats.append(mat)
        same = (t[:, None] // (2 * m)) == (t[None, :] // (2 * m))
        masks.append((same & (blk[:, None] % 2 == 1) & (blk[None, :] % 2 == 0)).astype(np.float32))
    pre = np.tril(np.ones((c, c), np.float32))
    suf = np.triu(np.ones((c, c), np.float32), 1)
    mstack = np.concatenate(mats + [pre, suf], 0)
    masks.append(np.eye(c, dtype=np.float32))
    return (jnp.asarray(mstack, BF16), jnp.asarray(mstack.T.copy(), BF16), jnp.asarray(np.stack(masks), F32),
            jnp.asarray(np.eye(HEAD, dtype=np.float32)))


def _gd_consts():
    c = CHUNK
    incl = np.tril(np.ones((c, c), np.float32))
    strict = np.tril(np.ones((c, c), np.float32), -1)
    eye = np.eye(c, dtype=np.float32)
    masks = np.stack([incl, strict, eye, incl.T.copy()])
    return jnp.asarray(incl, BF16), jnp.asarray(incl.T.copy(), BF16), jnp.asarray(masks, F32)


def _chunks_per_step(nc):
    for cb in (32 // HPS, 2, 1):
        if nc % cb == 0:
            return cb
    return 1


def _hg_prep(hq, hf, lg):
    lb = _sigmoid(lg[0:1, :] - lg[1:2, :])
    sg = _sigmoid(hf)
    sgn = _sigmoid(-hf)
    f = lb + (1.0 - lb) * sg
    lf = jnp.log(f)
    kk = (1.0 - lb) * sgn
    q = _silu(hq) * (HEAD ** -0.5)
    return lb, sg, sgn, f, lf, kk, q


def _mx_each(m, xs):
    hi, lo = _split2_each(xs)
    prods = [jnp.dot(m, jnp.concatenate([h, l], axis=1), preferred_element_type=F32) for h, l in zip(hi, lo)]
    return [p[:, :HEAD] + p[:, HEAD:] for p in prods]


def _hg_scaled(x, ex):
    xb = [_bf(a) for a in x]
    eb = [_bf(e[:6 * CHUNK]) for e in ex]
    return [[a * e[lvl * CHUNK:(lvl + 1) * CHUNK] for lvl in range(6)] for a, e in zip(xb, eb)]


def _hg_scores(q, kk, qe, ke, mask_ref):
    p = [mask_ref[6] * _rowsum(a * b) for a, b in zip(q, kk)]
    for lvl in range(6):
        d = [_dot_nt(a[lvl], b[lvl]) for a, b in zip(qe, ke)]
        p = [x + mask_ref[lvl] * y for x, y in zip(p, d)]
    return p


def _hgrn_fwd(hq, hf, hi, hg, logits, gain, consts):
    t = hq[0].shape[0]
    nc = t // CHUNK
    cb = _chunks_per_step(nc)
    rows = cb * CHUNK
    mstack, _, masks, eye = consts
    tile = pl.BlockSpec((rows, HPS * HEAD), lambda c, g: (c, g))

    def body(hq_ref, hf_ref, hi_ref, hg_ref, lg_ref, gain_ref, m_ref, mask_ref, eye_ref,
             oraw_ref, og_ref, ssave_ref, state):
        c = pl.program_id(0)
        g = pl.program_id(1)

        @pl.when(c == 0)
        def _():
            for hh in range(HPS):
                state[g * HPS + hh] = jnp.zeros((HEAD, HEAD), F32)

        lg_all = lg_ref[...]
        gain_v = gain_ref[...]

        def one(i, carry):
            sl = pl.ds(pl.multiple_of(i * CHUNK, CHUNK), CHUNK)
            hs = range(HPS)
            heads = [g * HPS + hh for hh in hs]
            ln = [slice(hh * HEAD, (hh + 1) * HEAD) for hh in hs]
            preps = [_hg_prep(hq_ref[sl, s], hf_ref[sl, s], lg_all[:, s]) for s in ln]
            lf, kk, q = [p[4] for p in preps], [p[5] for p in preps], [p[6] for p in preps]
            v = [hi_ref[sl, s] for s in ln]
            ex = [jnp.exp(x) for x in _mx_each(m_ref[...], lf)]
            eb = [e[6 * CHUNK:7 * CHUNK] for e in ex]
            esfx = [e[7 * CHUNK:8 * CHUNK] for e in ex]
            qe, ke = _hg_scaled(q, ex), _hg_scaled(kk, ex)
            p = _hg_scores(q, kk, qe, ke, mask_ref)
            s0 = [state[h] for h in heads]
            o = _each(lambda a, e, s, pp, vv: _dot(a * e, s) + _dot(pp, vv), q, eb, s0, p, v)
            eye_v = eye_ref[...]
            s1 = _each(lambda s, e, kx, ef, vv: s * _row_to_col(e[CHUNK - 1:CHUNK, :], eye_v) + _dot_tn(kx * ef, vv),
                       s0, eb, kk, esfx, v)
            for hh in hs:
                ssave_ref[i, hh] = s0[hh]
                state[heads[hh]] = s1[hh]
                oraw_ref[sl, ln[hh]] = o[hh]
                r = lax.rsqrt(jnp.mean(o[hh] * o[hh], axis=1, keepdims=True) + EPS)
                og_ref[sl, ln[hh]] = (o[hh] * r * gain_v * _silu(hg_ref[sl, ln[hh]])).astype(BF16)
            return carry

        lax.fori_loop(0, cb, one, 0, unroll=4)

    return pl.pallas_call(
        body, name="hgrn_fwd", grid=(nc // cb, HG_HEADS // HPS),
        in_specs=[_view_tile(v, rows, HPS * HEAD) for v in (hq, hf, hi, hg)] + [
                  pl.BlockSpec((2, HPS * HEAD), lambda c, g: (0, g)),
                  pl.BlockSpec((1, HEAD), lambda c, g: (0, 0)),
                  pl.BlockSpec(mstack.shape, lambda c, g: (0, 0)),
                  pl.BlockSpec(masks.shape, lambda c, g: (0, 0, 0)),
                  pl.BlockSpec(eye.shape, lambda c, g: (0, 0))],
        out_specs=[tile, tile, pl.BlockSpec((cb, HPS, HEAD, HEAD), lambda c, g: (c, g, 0, 0))],
        out_shape=[jax.ShapeDtypeStruct((t, HG_HEADS * HEAD), F32), jax.ShapeDtypeStruct((t, HG_HEADS * HEAD), BF16),
                   jax.ShapeDtypeStruct((nc, HG_HEADS, HEAD, HEAD), F32)],
        scratch_shapes=[pltpu.VMEM((HG_HEADS, HEAD, HEAD), F32)],
        compiler_params=_params(_ARB, _ARB))(hq[0], hf[0], hi[0], hg[0], logits, gain, mstack, masks, eye)


def _hgrn_bwd(hq, hf, hi, hg, logits, gain, oraw, ssave, dog, consts, into):
    t = hq[0].shape[0]
    dproj, off, width = into
    seg = HG_HEADS * HEAD
    assert HPS == HG_HEADS and width == 4 * seg and off % width == 0
    nc = t // CHUNK
    cb = _chunks_per_step(nc)
    rows = cb * CHUNK
    nb = nc // cb
    mstack, mstack_t, masks, eye = consts
    tile = pl.BlockSpec((rows, HPS * HEAD), lambda c, g: (nb - 1 - c, g))

    def body(hq_ref, hf_ref, hi_ref, hg_ref, lg_ref, gain_ref, oraw_ref, ssave_ref, dog_ref, m_ref, mt_ref,
             mask_ref, eye_ref, _, d_ref, dgain_ref, dlb_ref, dstate):
        c = pl.program_id(0)
        g = pl.program_id(1)

        @pl.when(c == 0)
        def _():
            for hh in range(HPS):
                dstate[g * HPS + hh] = jnp.zeros((HEAD, HEAD), F32)

        @pl.when((c == 0) & (g == 0))
        def _():
            dgain_ref[...] = jnp.zeros_like(dgain_ref)
            dlb_ref[...] = jnp.zeros_like(dlb_ref)

        lg_all = lg_ref[...]
        gain_v = gain_ref[...]
        eye_v = eye_ref[...]
        last_row = (lax.broadcasted_iota(jnp.int32, (CHUNK, HEAD), 0) == CHUNK - 1).astype(F32)

        def one(j, carry):
            i = cb - 1 - j
            sl = pl.ds(pl.multiple_of(i * CHUNK, CHUNK), CHUNK)
            hs = range(HPS)
            heads = [g * HPS + hh for hh in hs]
            ln = [slice(hh * HEAD, (hh + 1) * HEAD) for hh in hs]
            hqv = [hq_ref[sl, s] for s in ln]
            hgv = [hg_ref[sl, s] for s in ln]
            preps = [_hg_prep(a, hf_ref[sl, s], lg_all[:, s]) for a, s in zip(hqv, ln)]
            lb, sg, sgn, f, lf, kk, q = ([p[n] for p in preps] for n in range(7))
            v = [hi_ref[sl, s] for s in ln]
            ex = [jnp.exp(x) for x in _mx_each(m_ref[...], lf)]
            eb = [e[6 * CHUNK:7 * CHUNK] for e in ex]
            esfx = [e[7 * CHUNK:8 * CHUNK] for e in ex]
            qe, ke = _hg_scaled(q, ex), _hg_scaled(kk, ex)
            p = _hg_scores(q, kk, qe, ke, mask_ref)
            s0 = [ssave_ref[i, hh] for hh in hs]
            ds = [dstate[h] for h in heads]

            o = [oraw_ref[sl, s] for s in ln]
            r = [lax.rsqrt(jnp.mean(x * x, axis=1, keepdims=True) + EPS) for x in o]
            on = _each(lambda x, y: x * y, o, r)
            dg_out = [dog_ref[sl, s] for s in ln]
            sgate = [_silu(x) for x in hgv]
            for hh in hs:
                d_ref[sl, slice(3 * seg + hh * HEAD, 3 * seg + (hh + 1) * HEAD)] =(dg_out[hh] * on[hh] * gain_v * _dsilu(hgv[hh])).astype(BF16)
            dgain_ref[...] += sum(jnp.sum(d * s * n, axis=0, keepdims=True) for d, s, n in zip(dg_out, sgate, on))
            don = _each(lambda d, s: d * s * gain_v, dg_out, sgate)
            do = _each(lambda rr, dn, n: rr * (dn - n * jnp.mean(dn * n, axis=1, keepdims=True)), r, don, on)

            dp = _each(_dot_nt, do, v)
            dv = _each(lambda pp, d, kx, ef, s: _dot_tn(pp, d) + _dot(kx * ef, s), p, do, kk, esfx, ds)
            dqb = _each(_dot_nt, do, s0)
            dkx = _each(_dot_nt, v, ds)
            diag = [_rowsum(mask_ref[6] * x) for x in dp]
            dq = _each(lambda a, e, d, kx: a * e + d * kx, dqb, eb, diag, kk)
            dk = _each(lambda a, e, d, qq: a * e + d * qq, dkx, esfx, diag, q)
            dxs = [[] for _ in hs]
            for lvl in range(6):
                el = [e[lvl * CHUNK:(lvl + 1) * CHUNK] for e in ex]
                gm = [mask_ref[lvl] * x for x in dp]
                gm = [_bf(x) for x in gm]
                a1 = _each(lambda m_, kx: _dot(m_, kx[lvl]), gm, ke)
                a2 = _each(lambda m_, qq: _dot_tn(m_, qq[lvl]), gm, qe)
                dq = _each(lambda x, a, e: x + a * e, dq, a1, el)
                dk = _each(lambda x, a, e: x + a * e, dk, a2, el)
                for hh in hs:
                    dxs[hh].append((a1[hh] * q[hh] + a2[hh] * kk[hh]) * el[hh])
            e_end_row = [e[CHUNK - 1:CHUNK, :] for e in eb]
            ds_new = _each(lambda qq, e, d, er, s: _dot_tn(qq * e, d) + _row_to_col(er, eye_v) * s, q, eb, do, e_end_row, ds)
            for hh in hs:
                dstate[heads[hh]] = ds_new[hh]
                dend_row = _col_to_row(_rowsum(s0[hh] * ds[hh]), eye_v)
                dxs[hh].append(dqb[hh] * q[hh] * eb[hh] + last_row * (e_end_row[hh] * dend_row))
                dxs[hh].append(dkx[hh] * kk[hh] * esfx[hh])
            dlf = _mx_each(mt_ref[...], [jnp.concatenate(x, axis=0) for x in dxs])

            for hh in hs:
                d_ref[sl, slice(2 * seg + hh * HEAD, 2 * seg + (hh + 1) * HEAD)] =dv[hh].astype(BF16)
                d_ref[sl, ln[hh]] =(dq[hh] * (HEAD ** -0.5) * _dsilu(hqv[hh])).astype(BF16)
                df = dlf[hh] / f[hh]
                dsig = (1.0 - lb[hh]) * sg[hh] * sgn[hh]
                d_ref[sl, slice(seg + hh * HEAD, seg + (hh + 1) * HEAD)] =((df - dk[hh]) * dsig).astype(BF16)
                dlb_t = jnp.sum(df * sgn[hh] - dk[hh] * sgn[hh], axis=0, keepdims=True)
                dlb_ref[pl.ds(heads[hh], 1), :] += dlb_t * lb[hh] * (1.0 - lb[hh])
            return carry

        lax.fori_loop(0, cb, one, 0, unroll=2)

    outs = [jax.ShapeDtypeStruct(dproj.shape, dproj.dtype),
            jax.ShapeDtypeStruct((1, HEAD), F32), jax.ShapeDtypeStruct((HG_HEADS, HEAD), F32)]
    return pl.pallas_call(
        body, name="hgrn_bwd", grid=(nb, HG_HEADS // HPS),
        in_specs=[_view_tile(v, rows, HPS * HEAD, lambda c: nb - 1 - c) for v in (hq, hf, hi, hg)] + [
                  pl.BlockSpec((2, HPS * HEAD), lambda c, g: (0, g)),
                  pl.BlockSpec((1, HEAD), lambda c, g: (0, 0)), tile,
                  pl.BlockSpec((cb, HPS, HEAD, HEAD), lambda c, g: (nb - 1 - c, g, 0, 0)), tile,
                  pl.BlockSpec(mstack.shape, lambda c, h: (0, 0)),
                  pl.BlockSpec(mstack_t.shape, lambda c, h: (0, 0)),
                  pl.BlockSpec(masks.shape, lambda c, h: (0, 0, 0)),
                  pl.BlockSpec(eye.shape, lambda c, h: (0, 0)), _ANY],
        out_specs=[pl.BlockSpec((rows, width), lambda c, h: (nb - 1 - c, off // width)),
                   pl.BlockSpec((1, HEAD), lambda c, h: (0, 0)),
                   pl.BlockSpec((HG_HEADS, HEAD), lambda c, h: (0, 0))],
        out_shape=outs, scratch_shapes=[pltpu.VMEM((HG_HEADS, HEAD, HEAD), F32)], input_output_aliases={13: 0},
        compiler_params=_params(_ARB, _ARB))(hq[0], hf[0], hi[0], hg[0], logits, gain, oraw, ssave, dog, mstack,
                                             mstack_t, masks, eye, dproj)


CONV_W = 512
CONV_ROWS = 1024


def _per_head(fn, *arrs):
    width = arrs[0].shape[1]
    return jnp.concatenate([fn(*[a[:, j:j + HEAD] for a in arrs]) for j in range(0, width, HEAD)], axis=1)


def _shift_down(xv, halo, d, top_rows):
    if d == 0:
        return xv, xv[0:8]
    main = pltpu.roll(xv, d, 0)
    top = jnp.where(top_rows < d, pltpu.roll(halo, d, 0), main[0:8])
    return main, top


def _conv_parts(x_ref, halo_ref, w_ref, first):
    xv = x_ref[...]
    halo = jnp.where(first, 0.0, halo_ref[...])
    top_rows = lax.broadcasted_iota(jnp.int32, (8, xv.shape[1]), 0)
    shifted = [_shift_down(xv, halo, CONV_K - 1 - j, top_rows) for j in range(CONV_K)]
    w = w_ref[...]
    acc = sum(shifted[j][0] * w[j:j + 1, :] for j in range(CONV_K))
    acc_top = sum(shifted[j][1] * w[j:j + 1, :] for j in range(CONV_K))
    return shifted, acc, acc_top


def _conv_fwd(x, w8, l2scale, name):
    x, off, width = x
    t = x.shape[0]
    o = off // CONV_W
    tr = _pick(t, CONV_ROWS, 8)

    def post(cv):
        s = _silu(cv)
        if l2scale is not None:
            s = _per_head(lambda sh: sh * (lax.rsqrt(_rowsum(sh * sh) + EPS) * l2scale), s)
        return s

    def body(x_ref, halo_ref, w_ref, o_ref):
        _, acc, acc_top = _conv_parts(x_ref, halo_ref, w_ref, pl.program_id(1) == 0)
        o_ref[...] = post(acc)
        o_ref[0:8, :] = post(acc_top)

    return pl.pallas_call(
        body, name=name, grid=(width // CONV_W,t // tr),
        in_specs=[pl.BlockSpec((tr, CONV_W), lambda j, i: (i, o + j)),
                  pl.BlockSpec((8, CONV_W), lambda j, i: (jnp.maximum(i * (tr // 8) - 1, 0), o + j)),
                  pl.BlockSpec((8, CONV_W), lambda j, i: (0, j))],
        out_specs=pl.BlockSpec((tr, CONV_W), lambda j, i: (i, j)),
        out_shape=jax.ShapeDtypeStruct((t, width), F32), compiler_params=_params(_PAR, _PAR))(x, x, w8)


def _conv_bwd_a(x, w8, dy, l2scale, name):
    x, off, width = x
    t = x.shape[0]
    o = off // CONV_W
    tr = _pick(t, CONV_ROWS, 8)

    def l2_bwd(s, dyh):
        r = lax.rsqrt(_rowsum(s * s) + EPS)
        y0 = s * r
        dy0 = dyh * l2scale
        return r * (dy0 - y0 * _rowsum(dy0 * y0))

    def to_dc(cv, dyv):
        if l2scale is not None:
            dyv = _per_head(l2_bwd, _silu(cv), dyv)
        return dyv * _dsilu(cv)

    def body(x_ref, halo_ref, w_ref, dy_ref, dc_ref, dw_ref):
        @pl.when(pl.program_id(1) == 0)
        def _():
            dw_ref[...] = jnp.zeros_like(dw_ref)

        shifted, acc, acc_top = _conv_parts(x_ref, halo_ref, w_ref, pl.program_id(1) == 0)
        dyv = dy_ref[...]
        dc = to_dc(acc, dyv)
        dc_top = to_dc(acc_top, dyv[0:8])
        dc_ref[...] = dc
        dc_ref[0:8, :] = dc_top
        rest = (lax.broadcasted_iota(jnp.int32, dc.shape, 0) >= 8).astype(F32)
        dc_rest = dc * rest
        for j in range(CONV_K):
            dw_ref[j:j + 1, :] += (jnp.sum(dc_rest * shifted[j][0], axis=0, keepdims=True)
                                   + jnp.sum(dc_top * shifted[j][1], axis=0, keepdims=True))

    return pl.pallas_call(
        body, name=name, grid=(width // CONV_W,t // tr),
        in_specs=[pl.BlockSpec((tr, CONV_W), lambda j, i: (i, o + j)),
                  pl.BlockSpec((8, CONV_W), lambda j, i: (jnp.maximum(i * (tr // 8) - 1, 0), o + j)),
                  pl.BlockSpec((8, CONV_W), lambda j, i: (0, j)),
                  pl.BlockSpec((tr, CONV_W), lambda j, i: (i, j))],
        out_specs=[pl.BlockSpec((tr, CONV_W), lambda j, i: (i, j)), pl.BlockSpec((8, CONV_W), lambda j, i: (0, j))],
        out_shape=[jax.ShapeDtypeStruct((t, width), F32), jax.ShapeDtypeStruct((8, width), F32)],
        compiler_params=_params(_PAR, _ARB))(x, x, w8, dy)


def _conv_bwd_b(dc, w8, name, into):
    t, width = dc.shape
    tr = _pick(t, CONV_ROWS, 8)
    nt = t // tr

    dproj, off, into_width = into
    assert into_width == width and off % CONV_W == 0
    o = off // CONV_W

    def body(dc_ref, halo_ref, w_ref, _, dx_ref):
        dcv = dc_ref[...]
        halo = jnp.where(pl.program_id(1) == nt - 1, 0.0, halo_ref[...])
        w = w_ref[...]
        bot_rows = lax.broadcasted_iota(jnp.int32, (8, CONV_W), 0)
        acc = dcv * w[CONV_K - 1:CONV_K, :]
        acc_bot = dcv[tr - 8:tr] * w[CONV_K - 1:CONV_K, :]
        for d in range(1, CONV_K):
            main = pltpu.roll(dcv, tr - d, 0)
            bot = jnp.where(bot_rows >= 8 - d, pltpu.roll(halo, 8 - d, 0), main[tr - 8:tr])
            wj = w[CONV_K - 1 - d:CONV_K - d, :]
            acc = acc + main * wj
            acc_bot = acc_bot + bot * wj
        dx_ref[...] = acc.astype(BF16)
        dx_ref[tr - 16:tr, :] = jnp.concatenate([acc[tr - 16:tr - 8], acc_bot], axis=0).astype(BF16)

    return pl.pallas_call(
        body, name=name, grid=(width // CONV_W,nt),
        in_specs=[pl.BlockSpec((tr, CONV_W), lambda j, i: (i, j)),
                  pl.BlockSpec((8, CONV_W), lambda j, i: (jnp.minimum((i + 1) * (tr // 8), t // 8 - 1), j)),
                  pl.BlockSpec((8, CONV_W), lambda j, i: (0, j)), _ANY],
        out_specs=pl.BlockSpec((tr, CONV_W), lambda j, i: (i, o + j)),
        out_shape=jax.ShapeDtypeStruct(dproj.shape, dproj.dtype), input_output_aliases={3: 0},
        compiler_params=_params(_PAR, _PAR))(dc, dc, w8, dproj)


def _each(f, *lists):
    return [f(*xs) for xs in zip(*lists)]


def _split2_each(xs):
    hi = [_bf(x) for x in xs]
    lo = [_bf(x - h.astype(F32)) for x, h in zip(xs, hi)]
    return hi, lo


def _hp_each(a_split, b_split):
    (ah, al), (bh, bl) = a_split, b_split
    rows = ah[0].shape[0]
    d12 = [jnp.dot(jnp.concatenate([x, y], axis=0), z, preferred_element_type=F32) for x, y, z in zip(ah, al, bh)]
    d3 = [jnp.dot(x, y, preferred_element_type=F32) for x, y in zip(ah, bl)]
    return [d[:rows] + d[rows:] + e for d, e in zip(d12, d3)]


INV_EXACT_STEPS = 0


def _tri_inv_each(a_list, eye):
    ns = [-a for a in a_list]
    ps = [eye + n for n in ns]
    n_split = _split2_each(ns)
    for step in range(5):
        if step < INV_EXACT_STEPS:
            ns = _hp_each(n_split, n_split)
            n_split = _split2_each(ns)
            ps = [p + d for p, d in zip(ps, _hp_each(_split2_each(ps), n_split))]
        else:
            nb = n_split[0] if step == INV_EXACT_STEPS else [_bf(n) for n in ns]
            ns = [jnp.dot(x, x, preferred_element_type=F32) for x in nb]
            nb2 = [_bf(n) for n in ns]
            ps = [p + jnp.dot(_bf(p), y, preferred_element_type=F32) for p, y in zip(ps, nb2)]
    return ps


def _gd_gates(gab, alog, dtb):
    sp_arg = gab + dtb
    return sp_arg, -jnp.exp(alog) * _softplus(sp_arg), _sigmoid(gab)


def _pick_lane(tile, base, head):
    g, hh = head
    col = tile[:, base + hh:base + hh + 1]
    for gi in range(1, GD_HEADS // HPS):
        lane = base + gi * HPS + hh
        col = jnp.where(g == gi, tile[:, lane:lane + 1], col)
    return col


def _gd_chunks(q, k, v, g_all, beta_all, heads, l_ref, mask_ref, tm=None):
    incl, strict, eye, upper = mask_ref[0], mask_ref[1], mask_ref[2], mask_ref[3]
    lmat = l_ref[...]
    gb = [jnp.broadcast_to(_pick_lane(g_all, 0, s), (CHUNK, HEAD)) for s in heads]
    bb = [jnp.broadcast_to(_pick_lane(beta_all, GD_HEADS, s), (CHUNK, HEAD)) for s in heads]
    gam = _mx_each(lmat, gb)
    gam_row = [jnp.sum(x[:, :CHUNK] * upper, axis=0, keepdims=True) for x in gb]
    lm = _each(lambda gm, gr: incl * jnp.exp(jnp.minimum(gm[:, :CHUNK] - gr, 0.0)), gam, gam_row)
    kb = _each(lambda x, b: x * b, k, bb)
    a = _each(lambda x, y, m: strict * _dot_nt(x, y) * m, kb, k, lm)
    if tm is None:
        tm = _tri_inv_each(a, eye)
    eg = [jnp.exp(x) for x in gam]
    vb = _each(lambda x, b: x * b, v, bb)
    kbg = _each(lambda x, e: x * e, kb, eg)
    uw = _each(lambda t_, x, y: _dot(t_, jnp.concatenate([x, y], axis=1)), tm, vb, kbg)
    u = [x[:, :HEAD] for x in uw]
    w = [x[:, HEAD:] for x in uw]
    qk = _each(lambda x, y, m: _dot_nt(x, y) * m, q, k, lm)
    g_end = [x[CHUNK - 1:CHUNK, :] for x in gam]
    ekg = _each(lambda e, x: jnp.exp(e - x), g_end, gam)
    ge = [jnp.exp(e) for e in g_end]
    kg = _each(lambda x, e: x * e, k, ekg)
    qg = _each(lambda x, e: x * e, q, eg)
    names = ("bb", "lm", "kb", "a", "tm", "eg", "vb", "kbg", "u", "w", "qk", "ekg", "ge", "kg", "qg")
    cols = (bb, lm, kb, a, tm, eg, vb, kbg, u, w, qk, ekg, ge, kg, qg)
    return [dict(zip(names, vals)) for vals in zip(*cols)]


def _gd_specs(rows, rev_nb=None):
    def cidx(c):
        return c if rev_nb is None else rev_nb - 1 - c

    qk_tile = pl.BlockSpec((rows, HPS // 2 * HEAD), lambda c, g: (cidx(c), g))
    v_tile = pl.BlockSpec((rows, HPS * HEAD), lambda c, g: (cidx(c), g))
    gab_tile = pl.BlockSpec((rows, HEAD), lambda c, g: (cidx(c), 0))
    return qk_tile, v_tile, gab_tile


def _gdn_fwd(qn, kn, cv, gab, gz, alog, dtb, gain, consts):
    t = qn.shape[0]
    nc = t // CHUNK
    cb = _chunks_per_step(nc)
    rows = cb * CHUNK
    lmat, _, masks = consts
    qk_tile, v_tile, gab_tile = _gd_specs(rows)
    row128 = pl.BlockSpec((1, HEAD), lambda c, h: (0, 0))

    def body(q_ref, k_ref, v_ref, gab_ref, gz_ref, alog_ref, dtb_ref, gain_ref, l_ref, mask_ref,
             oraw_ref, og_ref, ssave_ref, tsave_ref, state):
        c = pl.program_id(0)
        g = pl.program_id(1)

        @pl.when(c == 0)
        def _():
            for hh in range(HPS):
                state[g * HPS + hh] = jnp.zeros((HEAD, HEAD), F32)

        alog = alog_ref[...]
        dtb = dtb_ref[...]
        gain_v = gain_ref[...]

        def one(i, carry):
            sl = pl.ds(pl.multiple_of(i * CHUNK, CHUNK), CHUNK)
            _, g_all, beta_all = _gd_gates(gab_ref[sl, :], alog, dtb)
            heads = [g * HPS + hh for hh in range(HPS)]
            lq = [slice(hh // 2 * HEAD, (hh // 2 + 1) * HEAD) for hh in range(HPS)]
            lv = [slice(hh * HEAD, (hh + 1) * HEAD) for hh in range(HPS)]
            chs = _gd_chunks([q_ref[sl, s] for s in lq], [k_ref[sl, s] for s in lq], [v_ref[sl, s] for s in lv],
                             g_all, beta_all, [(g, hh) for hh in range(HPS)], l_ref, mask_ref)
            s0 = [state[h] for h in heads]
            ws = _each(lambda ch, s: _dot(jnp.concatenate([ch["w"], ch["qg"]], axis=0), s), chs, s0)
            v_new = _each(lambda ch, x: ch["u"] - x[:CHUNK], chs, ws)
            o = _each(lambda ch, x, vn: x[CHUNK:] + _dot(ch["qk"], vn), chs, ws, v_new)
            s1 = _each(lambda ch, s, vn: s * ch["ge"] + _dot_tn(ch["kg"], vn), chs, s0, v_new)
            for hh in range(HPS):
                ssave_ref[i, hh] = s0[hh]
                tsave_ref[i, hh] = chs[hh]["tm"]
                state[heads[hh]] = s1[hh]
                oraw_ref[sl, lv[hh]] = o[hh]
                r = lax.rsqrt(jnp.mean(o[hh] * o[hh], axis=1, keepdims=True) + EPS)
                og_ref[sl, lv[hh]] = (o[hh] * r * gain_v * _silu(gz_ref[sl, lv[hh]])).astype(BF16)
            return carry

        lax.fori_loop(0, cb, one, 0, unroll=4)

    return pl.pallas_call(
        body, name="gdn_fwd", grid=(nc // cb, GD_HEADS // HPS),
        in_specs=[qk_tile, qk_tile, v_tile, gab_tile, _view_tile(gz, rows, HPS * HEAD), row128, row128, row128,
                  pl.BlockSpec(lmat.shape, lambda c, g: (0, 0)),
                  pl.BlockSpec(masks.shape, lambda c, g: (0, 0, 0))],
        out_specs=[v_tile, v_tile, pl.BlockSpec((cb, HPS, HEAD, HEAD), lambda c, g: (c, g, 0, 0)),
                   pl.BlockSpec((cb, HPS, CHUNK, CHUNK), lambda c, g: (c, g, 0, 0))],
        out_shape=[jax.ShapeDtypeStruct((t, GD_HEADS * HEAD), F32), jax.ShapeDtypeStruct((t, GD_HEADS * HEAD), BF16),
                   jax.ShapeDtypeStruct((nc, GD_HEADS, HEAD, HEAD), F32),
                   jax.ShapeDtypeStruct((nc, GD_HEADS, CHUNK, CHUNK), F32)],
        scratch_shapes=[pltpu.VMEM((GD_HEADS, HEAD, HEAD), F32)],
        compiler_params=_params(_ARB, _ARB))(qn, kn, cv, gab, gz[0], alog, dtb, gain, lmat, masks)


def _gdn_bwd(qn, kn, cv, gab, gz, alog, dtb, gain, oraw, ssave, tsave, dog, consts, into):
    t = qn.shape[0]
    dproj, off, width = into
    assert width == GD_HEADS * HEAD and off % (HPS * HEAD) == 0
    nc = t // CHUNK
    cb = _chunks_per_step(nc)
    rows = cb * CHUNK
    nb = nc // cb
    lmat, lmat_t, masks = consts
    qk_tile, v_tile, gab_tile = _gd_specs(rows, nb)
    row128 = pl.BlockSpec((1, HEAD), lambda c, h: (0, 0))

    def body(q_ref, k_ref, v_ref, gab_ref, gz_ref, alog_ref, dtb_ref, gain_ref, oraw_ref, ssave_ref, tsave_ref, dog_ref,
             l_ref, lt_ref, mask_ref, _,
             dq_ref, dk_ref, dv_ref, dgab_ref, dgz_ref, small_ref, dstate):
        c = pl.program_id(0)
        g = pl.program_id(1)

        @pl.when(c == 0)
        def _():
            for hh in range(HPS):
                dstate[g * HPS + hh] = jnp.zeros((HEAD, HEAD), F32)

        @pl.when((c == 0) & (g == 0))
        def _():
            small_ref[...] = jnp.zeros_like(small_ref)

        alog = alog_ref[...]
        dtb = dtb_ref[...]
        gain_v = gain_ref[...]
        lane = lax.broadcasted_iota(jnp.int32, (1, HEAD), 1)
        last_row = (lax.broadcasted_iota(jnp.int32, (CHUNK, HEAD), 0) == CHUNK - 1).astype(F32)

        def one(j, carry):
            i = cb - 1 - j
            sl = pl.ds(pl.multiple_of(i * CHUNK, CHUNK), CHUNK)
            sp_arg, g_all, beta_all = _gd_gates(gab_ref[sl, :], alog, dtb)
            strict, eye = mask_ref[1], mask_ref[2]
            ltm = lt_ref[...]
            hs = range(HPS)
            heads = [g * HPS + hh for hh in hs]
            lq = [slice(hh // 2 * HEAD, (hh // 2 + 1) * HEAD) for hh in hs]
            lv = [slice(hh * HEAD, (hh + 1) * HEAD) for hh in hs]
            q = [q_ref[sl, s] for s in lq]
            k = [k_ref[sl, s] for s in lq]
            v = [v_ref[sl, s] for s in lv]
            gzv = [gz_ref[sl, s] for s in lv]
            chs = _gd_chunks(q, k, v, g_all, beta_all, [(g, hh) for hh in hs], l_ref, mask_ref,
                             tm=[tsave_ref[i, hh] for hh in hs])

            def col(name):
                return [ch[name] for ch in chs]

            def mul(x, y):
                return x * y

            tm, lm, eg, bb = col("tm"), col("lm"), col("eg"), col("bb")
            s0 = [ssave_ref[i, hh] for hh in hs]
            ds = [dstate[h] for h in heads]
            v_new = _each(lambda u, w, s: u - _dot(w, s), col("u"), col("w"), s0)

            o = [oraw_ref[sl, s] for s in lv]
            r = [lax.rsqrt(jnp.mean(x * x, axis=1, keepdims=True) + EPS) for x in o]
            on = _each(mul, o, r)
            dg_out = [dog_ref[sl, s] for s in lv]
            sgate = [_silu(x) for x in gzv]
            for hh in hs:
                dgz_ref[sl, lv[hh]] = (dg_out[hh] * on[hh] * gain_v * _dsilu(gzv[hh])).astype(BF16)
            small_ref[0:1, :] += sum(jnp.sum(d * s * n, axis=0, keepdims=True) for d, s, n in zip(dg_out, sgate, on))
            don = _each(lambda d, s: d * s * gain_v, dg_out, sgate)
            do = _each(lambda rr, dn, n: rr * (dn - n * jnp.mean(dn * n, axis=1, keepdims=True)), r, don, on)

            dv_new = _each(lambda a, d, b, s: _dot_tn(a, d) + _dot(b, s), col("qk"), do, col("kg"), ds)
            dqk = _each(_dot_nt, do, v_new)
            dkg = _each(_dot_nt, v_new, ds)
            dge = _each(lambda s, d: jnp.sum(_rowsum(s * d), axis=0, keepdims=True), s0, ds)
            both = _each(lambda d, dv: jnp.concatenate([d, dv], axis=0), do, dv_new)
            from_s = _each(_dot_nt, both, s0)
            dqg = [x[:CHUNK] for x in from_s]
            dw = [-x[CHUNK:] for x in from_s]
            ds_new = _each(lambda qg, w, bo, ge, s: _dot_tn(jnp.concatenate([qg, -w], axis=0), bo) + ge * s,
                           col("qg"), col("w"), both, col("ge"), ds)
            for hh in hs:
                dstate[heads[hh]] = ds_new[hh]

            side = _each(lambda dv, d: jnp.concatenate([dv, d], axis=1), dv_new, dw)
            back = _each(_dot_tn, tm, side)
            dvb = [x[:, :HEAD] for x in back]
            dkbg = [x[:, HEAD:] for x in back]
            dtm = _each(lambda sd, vb, kbg: _dot_nt(sd, jnp.concatenate([vb, kbg], axis=1)), side, col("vb"), col("kbg"))
            dtt = _each(_dot_nt, dtm, tm)
            da = _each(lambda t_, x: -_dot_tn(t_, x) * strict, tm, dtt)
            dal = _each(mul, da, lm)
            dqk_l = _each(mul, dqk, lm)
            stack = _each(lambda x, y: jnp.concatenate([x, y], axis=0), dal, dqk_l)
            on_k = _each(_dot, stack, k)
            dkb = _each(lambda x, y, e: x[:CHUNK] + y * e, on_k, dkbg, eg)
            dq = _each(lambda x, y, e: x[CHUNK:] + y * e, on_k, dqg, eg)
            dk = _each(lambda st, kb, qq, z, ekg, w_, b: _dot_tn(st, jnp.concatenate([kb, qq], axis=0)) + z * ekg + w_ * b,
                       stack, col("kb"), q, dkg, col("ekg"), dkb, bb)
            gmat = _each(lambda x, a, y, qk: x * a + y * qk, da, col("a"), dqk, col("qk"))
            t_kg = _each(lambda x, y: _rowsum(x * y), dkg, col("kg"))
            dgam = _each(lambda gm, x, qg, t_, y, kbg: (_rowsum(gm) - _row_to_col(jnp.sum(gm, axis=0, keepdims=True), eye)
                                                        + _rowsum(x * qg) - t_ + _rowsum(y * kbg)),
                         gmat, dqg, col("qg"), t_kg, dkbg, col("kbg"))
            dg_end = _each(lambda t_, e, ge: jnp.sum(t_, axis=0, keepdims=True) + e * ge[:, 0:1], t_kg, dge, col("ge"))
            dgam = _each(lambda x, e: x + last_row * e, dgam, dg_end)
            dbeta = _each(lambda x, kk, y, vv: _rowsum(x * kk) + _rowsum(y * vv), dkb, k, dvb, v)
            dg = _mx_each(ltm, dgam)

            for hh in hs:
                dv_ref[sl, lv[hh]] = dvb[hh] * bb[hh]
            fac_g = -jnp.exp(alog) * _sigmoid(sp_arg)
            fac_b = beta_all * (1.0 - beta_all)
            hot_g = [(lane == h).astype(F32) for h in heads]
            hot_b = [(lane == GD_HEADS + h).astype(F32) for h in heads]
            dga = _each(lambda x, hot: x * hot * fac_g, dg, hot_g)
            dgb = _each(lambda x, hot: x * hot * fac_b, dbeta, hot_b)
            small_ref[1:2, :] += sum(jnp.sum(x, axis=0, keepdims=True) for x in dga)
            small_ref[2:3, :] += sum(jnp.sum(x * hot * g_all, axis=0, keepdims=True) for x, hot in zip(dg, hot_g))
            for pair in range(HPS // 2):
                lqp = slice(pair * HEAD, (pair + 1) * HEAD)
                dq_ref[sl, lqp] = dq[2 * pair] + dq[2 * pair + 1]
                dk_ref[sl, lqp] = dk[2 * pair] + dk[2 * pair + 1]
            dgab_ref[sl, :] = sum(a + b for a, b in zip(dga, dgb))
            return carry

        lax.fori_loop(0, cb, one, 0, unroll=4)

    groups = GD_HEADS // HPS
    outs = [jax.ShapeDtypeStruct((t, 1024), F32), jax.ShapeDtypeStruct((t, 1024), F32),
            jax.ShapeDtypeStruct((t, 2048), F32), jax.ShapeDtypeStruct((t, groups * HEAD), F32),
            jax.ShapeDtypeStruct(dproj.shape, dproj.dtype), jax.ShapeDtypeStruct((8, HEAD), F32)]
    dgz_tile = pl.BlockSpec((rows, HPS * HEAD), lambda c, g: (nb - 1 - c, off // (HPS * HEAD) + g))
    return pl.pallas_call(
        body, name="gdn_bwd", grid=(nb, groups),
        in_specs=[qk_tile, qk_tile, v_tile, gab_tile, _view_tile(gz, rows, HPS * HEAD, lambda c: nb - 1 - c),
                  row128, row128, row128, v_tile,
                  pl.BlockSpec((cb, HPS, HEAD, HEAD), lambda c, g: (nb - 1 - c, g, 0, 0)),
                  pl.BlockSpec((cb, HPS, CHUNK, CHUNK), lambda c, g: (nb - 1 - c, g, 0, 0)), v_tile,
                  pl.BlockSpec(lmat.shape, lambda c, g: (0, 0)),
                  pl.BlockSpec(lmat_t.shape, lambda c, g: (0, 0)),
                  pl.BlockSpec(masks.shape, lambda c, g: (0, 0, 0)), _ANY],
        out_specs=[qk_tile, qk_tile, v_tile, pl.BlockSpec((rows, HEAD), lambda c, g: (nb - 1 - c, g)), dgz_tile,
                   pl.BlockSpec((8, HEAD), lambda c, g: (0, 0))],
        out_shape=outs, scratch_shapes=[pltpu.VMEM((GD_HEADS, HEAD, HEAD), F32)], input_output_aliases={15: 4},
        compiler_params=_params(_ARB, _ARB))(qn, kn, cv, gab, gz[0], alog, dtb, gain, oraw, ssave, tsave, dog,
                                             lmat, lmat_t, masks, dproj)


def _fold_groups(wide):
    t, width = wide.shape
    tr = _pick(t, CONV_ROWS, 8)

    def body(w_ref, o_ref):
        acc = w_ref[:, 0:HEAD]
        for j in range(1, width // HEAD):
            acc = acc + w_ref[:, j * HEAD:(j + 1) * HEAD]
        o_ref[...] = acc.astype(BF16)

    return pl.pallas_call(
        body, name="fold_gate_grads", grid=(t // tr,), in_specs=[_row_spec(tr, width)], out_specs=_row_spec(tr, HEAD),
        out_shape=jax.ShapeDtypeStruct((t, HEAD), BF16), compiler_params=_params(_PAR))(wide)


def _adam_math(w, g, m, v):
    m2 = ADAM_B1 * m + (1.0 - ADAM_B1) * g
    v2 = ADAM_B2 * v + (1.0 - ADAM_B2) * (g * g)
    m_hat = m2 / (1.0 - ADAM_B1 ** ADAM_STEP)
    v_hat = v2 / (1.0 - ADAM_B2 ** ADAM_STEP)
    delta = -ADAM_LR * (m_hat / (jnp.sqrt(v_hat) + ADAM_EPS) + ADAM_WD * w)
    return delta, m2, v2


def _adamw(w, g, m, v, name, after=None):
    r, c = w.shape
    tr = r
    for cand in range(8, r + 1, 8):
        if r % cand == 0 and cand * c * 4 <= (2 << 20):
            tr = cand
    if r % 8 != 0:
        tr = r

    def body(w_ref, g_ref, m_ref, v_ref, *rest):
        d_ref, m2_ref, v2_ref = rest[-3:]
        d, m2, v2 = _adam_math(w_ref[...], g_ref[...], m_ref[...], v_ref[...])
        d_ref[...] = d
        m2_ref[...] = m2
        v2_ref[...] = v2

    spec = pl.BlockSpec((tr, c), lambda i: (i, 0))
    extra = [] if after is None else [after]
    return pl.pallas_call(
        body, name=name, grid=(r // tr,), in_specs=[spec] * 4 + [_ANY] * len(extra), out_specs=[spec] * 3,
        out_shape=[jax.ShapeDtypeStruct((r, c), F32)] * 3, compiler_params=_params(_PAR))(w, g, m, v, *extra)


_ANY = pl.BlockSpec(memory_space=pl.ANY)


def _place():
    return lax.axis_index("x"), lax.axis_index("y"), lax.axis_index("c")


def _gather_weights(packs, nchs, name):
    n = len(packs)
    halves = [p.shape[0] // 2 for p in packs]
    base = [sum(nchs[:i]) for i in range(n)]
    total = sum(nchs)
    for p, h, k in zip(packs, halves, nchs):
        assert p.shape[0] == 2 * h and h % k == 0 and (h // k) % 16 == 0

    def body(*refs):
        p_refs, g_refs, (send_sems, recv_sems) = refs[:n], refs[n:2 * n], refs[2 * n:]
        x, y, c = _place()
        sibling = (x, y, 1 - c)
        chips = [(1 - x, y), (x, 1 - y), (1 - x, 1 - y)]
        chunks = [(a, q) for a in range(n) for q in range(nchs[a])]

        def rows_of(a, pc, q):
            ch = halves[a] // nchs[a]
            return pl.ds(pl.multiple_of(pc * halves[a] + q * ch, 16), ch)

        def piece(a, px, py, pc, q):
            return g_refs[a].at[2 * px + py, rows_of(a, pc, q), :]

        def copy(k, src, dst, to):
            return pltpu.make_async_remote_copy(src_ref=src, dst_ref=dst, send_sem=send_sems.at[k],
                                                recv_sem=recv_sems.at[k], device_id=to, device_id_type=MESH)

        def sem_of(j, a, q):
            return j * total + base[a] + q

        first = {(j, a, q): copy(sem_of(j, a, q), p_refs[a].at[rows_of(a, c, q), :], piece(a, x, y, c, q), (*chip, c))
                 for j, chip in enumerate(chips) for a, q in chunks}
        for a, q in chunks:
            for j in range(3):
                first[j, a, q].start()
        passed = {(j, a, q): copy(sem_of(3 + j, a, q), piece(a, *chip, c, q), piece(a, *chip, c, q), sibling)
                  for j, chip in enumerate(chips) for a, q in chunks}
        for a, q in chunks:
            for j, chip in enumerate(chips):
                copy(sem_of(j, a, q), p_refs[a].at[rows_of(a, c, q), :], piece(a, *chip, c, q), (*chip, c)).wait_recv()
                passed[j, a, q].start()
        for a, q in chunks:
            for j, chip in enumerate(chips):
                copy(sem_of(3 + j, a, q), piece(a, *chip, 1 - c, q), piece(a, *chip, 1 - c, q), sibling).wait_recv()
        for key in first:
            first[key].wait_send()
            passed[key].wait_send()

    return pl.pallas_call(
        body, name=name, out_shape=[jax.ShapeDtypeStruct((4,) + p.shape, p.dtype) for p in packs],
        in_specs=[_ANY] * n, out_specs=[_ANY] * n,
        scratch_shapes=[pltpu.SemaphoreType.DMA((6 * total,)), pltpu.SemaphoreType.DMA((6 * total,))])(*packs)


def _swap_with_sibling(arrs, nchs, lead, name, halves=False):
    n = len(arrs)
    jobs = []
    hs = [arr.shape[-2] // (2 if halves else 1) for arr in arrs]
    for a, (h, k) in enumerate(zip(hs, nchs)):
        assert h % k == 0 and (h // k) % 16 == 0
        for s in (range(lead) if lead else [None]):
            jobs += [(a, s, q * (h // k), h // k) for q in range(k)]

    def body(*refs):
        src, dst, (send_sems, recv_sems) = refs[:n], refs[n:2 * n], refs[2 * n:]
        x, y, c = _place()

        def at(ref, s, r0, rows):
            return ref.at[pl.ds(r0, rows), :] if s is None else ref.at[s, pl.ds(r0, rows), :]

        def src_rows(a, r0):
            return pl.multiple_of((1 - c) * hs[a] + r0, 16) if halves else r0

        copies = [pltpu.make_async_remote_copy(
            src_ref=at(src[a], s, src_rows(a, r0), rows), dst_ref=at(dst[a], s, r0, rows), send_sem=send_sems.at[k],
            recv_sem=recv_sems.at[k], device_id=(x, y, 1 - c), device_id_type=MESH)
            for k, (a, s, r0, rows) in enumerate(jobs)]
        for cp in copies:
            cp.start()
        for cp in copies:
            cp.wait()

    shapes = [jax.ShapeDtypeStruct(arr.shape[:-2] + (h, arr.shape[-1]), arr.dtype) for arr, h in zip(arrs, hs)]
    return pl.pallas_call(
        body, name=name, out_shape=shapes, in_specs=[_ANY] * n, out_specs=[_ANY] * n,
        scratch_shapes=[pltpu.SemaphoreType.DMA((len(jobs),)), pltpu.SemaphoreType.DMA((len(jobs),))])(*arrs)


def _add2(full, b, core, name):
    n, rows, w = b.shape
    tr = _pick(rows, 256, 16)
    nblk = rows // tr

    def body(c_ref, a_ref, b_ref, o_ref):
        o_ref[...] = (a_ref[...].astype(F32) + b_ref[...].astype(F32)).astype(BF16)

    spec = pl.BlockSpec((1, tr, w), lambda i, j, c_ref: (i, j, 0))
    grid_spec = pltpu.PrefetchScalarGridSpec(
        num_scalar_prefetch=1, grid=(n, nblk),
        in_specs=[pl.BlockSpec((1, tr, w), lambda i, j, c_ref: (i, c_ref[0] * nblk + j, 0)), spec], out_specs=spec)
    return pl.pallas_call(
        body, name=name, grid_spec=grid_spec, out_shape=jax.ShapeDtypeStruct(b.shape, BF16),
        compiler_params=_params(_PAR, _PAR))(core, full, b)


def _reduce_chips(partials, nchs, name):
    n = len(partials)
    jobs = []
    for a, (arr, k) in enumerate(zip(partials, nchs)):
        h = arr.shape[1]
        assert h % k == 0 and (h // k) % 16 == 0
        jobs += [(a, q * (h // k), h // k) for q in range(k)]

    def body(*refs):
        src, dst, (send_sems, recv_sems) = refs[:n], refs[n:2 * n], refs[2 * n:]
        x, y, c = _place()
        chips = [(1 - x, y), (x, 1 - y), (1 - x, 1 - y)]
        copies = [pltpu.make_async_remote_copy(
            src_ref=src[a].at[2 * px + py, pl.ds(r0, rows), :], dst_ref=dst[a].at[j, pl.ds(r0, rows), :],
            send_sem=send_sems.at[3 * k + j], recv_sem=recv_sems.at[3 * k + j],
            device_id=(px, py, c), device_id_type=MESH)
            for k, (a, r0, rows) in enumerate(jobs) for j, (px, py) in enumerate(chips)]
        for cp in copies:
            cp.start()
        for cp in copies:
            cp.wait()

    return pl.pallas_call(
        body, name=name,
        out_shape=[jax.ShapeDtypeStruct((3,) + p.shape[1:], p.dtype) for p in partials],
        in_specs=[_ANY] * n, out_specs=[_ANY] * n,
        scratch_shapes=[pltpu.SemaphoreType.DMA((3 * len(jobs),)), pltpu.SemaphoreType.DMA((3 * len(jobs),))])(*partials)


_HBM = pl.BlockSpec(memory_space=pltpu.HBM)
_SEM = pl.BlockSpec(memory_space=pltpu.SEMAPHORE)
_DATAFLOW = pltpu.SideEffectType.DATAFLOW_SIDE_EFFECTING


def _ici_jobs(srcs, nchs, kind):
    jobs = []
    for a, (arr, k) in enumerate(zip(srcs, nchs)):
        h = arr.shape[0] // 2 if kind in ("gather", "gather_both") else arr.shape[1]
        assert h % k == 0 and (h // k) % 16 == 0
        jobs += [(a, h, q * (h // k), h // k) for q in range(k)]
    return jobs


def _ici_sems(jobs, kind):
    return (6 if kind == "gather_both" else 3) * len(jobs)


def _ici_copies(src, land, send_sems, recv_sems, jobs, kind, waiting=False):
    x, y, c = _place()
    chips = [(1 - x, y), (x, 1 - y), (1 - x, 1 - y)]
    copies = []
    if kind == "gather_both":
        for k, (a, h, r0, rows) in enumerate(jobs):
            for j, (px, py) in enumerate(chips):
                for t in range(2):
                    half = t if waiting else c
                    at = pl.ds(pl.multiple_of(half * h + r0, 16), rows)
                    slot = 2 * px + py if waiting else 2 * x + y
                    copies.append(pltpu.make_async_remote_copy(
                        src_ref=src[a].at[at, :], dst_ref=land[a].at[slot, at, :], send_sem=send_sems.at[6 * k + 2 * j + t],
                        recv_sem=recv_sems.at[6 * k + 2 * j + half], device_id=(px, py, t), device_id_type=MESH))
        return copies
    for k, (a, h, r0, rows) in enumerate(jobs):
        for j, (px, py) in enumerate(chips):
            if kind == "gather":
                at = pl.ds(pl.multiple_of(c * h + r0, 16), rows)
                s, d = src[a].at[at, :], land[a].at[2 * x + y, at, :]
            else:
                s, d = src[a].at[2 * px + py, pl.ds(r0, rows), :], land[a].at[j, pl.ds(r0, rows), :]
            copies.append(pltpu.make_async_remote_copy(
                src_ref=s, dst_ref=d, send_sem=send_sems.at[3 * k + j], recv_sem=recv_sems.at[3 * k + j],
                device_id=(px, py, c), device_id_type=MESH))
    return copies


def _ici_start(srcs, nchs, kind, name, after=None):
    n = len(srcs)
    extra = [] if after is None else [after]
    jobs = _ici_jobs(srcs, nchs, kind)
    lead = (lambda s: (3,) + s.shape[1:]) if kind == "reduce" else (lambda s: (4,) + s.shape)
    lands = [lax.empty(lead(s), s.dtype) for s in srcs]

    def body(*refs):
        src, land = refs[:n], refs[n:2 * n]
        send_sems, recv_sems, token = refs[2 * n + len(extra)], refs[2 * n + len(extra) + 1], refs[-1]
        for cp in _ici_copies(src, land, send_sems, recv_sems, jobs, kind):
            cp.start()
        token[...] = jnp.zeros_like(token)

    hbm = [pltpu.HBM(a.shape, a.dtype) for a in srcs + lands]
    outs = pl.pallas_call(
        body, name=name,
        out_shape=[pltpu.SemaphoreType.DMA((_ici_sems(jobs, kind),)), pltpu.SemaphoreType.DMA((_ici_sems(jobs, kind),))] + hbm
        + [jax.ShapeDtypeStruct((8, 128), F32)],
        in_specs=[_HBM] * (2 * n) + [_ANY] * len(extra),
        out_specs=[_SEM, _SEM] + [_HBM] * (2 * n) + [pl.BlockSpec(memory_space=pltpu.VMEM)],
        input_output_aliases={i: 2 + i for i in range(2 * n)},
        compiler_params=pltpu.CompilerParams(has_side_effects=_DATAFLOW),
    )(*[pltpu.with_memory_space_constraint(a, pltpu.HBM) for a in srcs + lands], *extra)
    return (outs[0], outs[1], list(outs[2:2 + n]), list(outs[2 + n:2 + 2 * n]), nchs, kind), outs[-1]


def _ici_wait(handle, after, name):
    send_sems, recv_sems, srcs, lands, nchs, kind = handle
    n = len(srcs)
    jobs = _ici_jobs(srcs, nchs, kind)

    def body(*refs):
        src, land = refs[:n], refs[n:2 * n]
        for cp in _ici_copies(src, land, refs[2 * n], refs[2 * n + 1], jobs, kind, waiting=True):
            cp.wait_send()
            cp.wait_recv()

    outs = pl.pallas_call(
        body, name=name, out_shape=[pltpu.HBM(a.shape, a.dtype) for a in srcs + lands],
        in_specs=[_HBM] * (2 * n) + [_SEM, _SEM, _ANY], out_specs=[_HBM] * (2 * n),
        input_output_aliases={i: i for i in range(2 * n)},
        compiler_params=pltpu.CompilerParams(has_side_effects=_DATAFLOW),
    )(*srcs, *lands, send_sems, recv_sems, after)
    return list(outs[:n]), list(outs[n:])


def _pass_to_sibling(gathered, nchs, name):
    n = len(gathered)
    jobs = _ici_jobs([jax.ShapeDtypeStruct(g.shape[1:], g.dtype) for g in gathered], nchs, "gather")

    def body(*refs):
        src, dst, (send_sems, recv_sems) = refs[:n], refs[n:2 * n], refs[2 * n:]
        x, y, c = _place()
        slots = [2 * (1 - x) + y, 2 * x + (1 - y), 2 * (1 - x) + (1 - y)]

        def copy(k, j, pc):
            a, h, r0, rows = jobs[k]
            at = pl.ds(pl.multiple_of(pc * h + r0, 16), rows)
            return pltpu.make_async_remote_copy(
                src_ref=src[a].at[slots[j], at, :], dst_ref=dst[a].at[slots[j], at, :], send_sem=send_sems.at[3 * k + j],
                recv_sem=recv_sems.at[3 * k + j], device_id=(x, y, 1 - c), device_id_type=MESH)

        pairs = [(k, j) for k in range(len(jobs)) for j in range(3)]
        for k, j in pairs:
            copy(k, j, c).start()
        for k, j in pairs:
            copy(k, j, c).wait_send()
            copy(k, j, 1 - c).wait_recv()

    return pl.pallas_call(
        body, name=name, out_shape=[jax.ShapeDtypeStruct(g.shape, g.dtype) for g in gathered],
        in_specs=[_ANY] * n, out_specs=[_ANY] * n, input_output_aliases={i: i for i in range(n)},
        scratch_shapes=[pltpu.SemaphoreType.DMA((3 * len(jobs),)), pltpu.SemaphoreType.DMA((3 * len(jobs),))])(*gathered)


def _add4(own, got, name):
    rows, w = own.shape
    tr = _pick(rows, 128, 16)

    def body(a_ref, b_ref, o_ref):
        o_ref[...] = ((a_ref[...].astype(F32) + b_ref[0].astype(F32)) + b_ref[1].astype(F32)) + b_ref[2].astype(F32)

    return pl.pallas_call(
        body, name=name, grid=(rows // tr,),
        in_specs=[pl.BlockSpec((tr, w), lambda i: (i, 0)), pl.BlockSpec((3, tr, w), lambda i: (0, i, 0))],
        out_specs=pl.BlockSpec((tr, w), lambda i: (i, 0)), out_shape=jax.ShapeDtypeStruct((rows, w), F32),
        compiler_params=_params(_PAR))(own, got)


def _small_sync(gs, ws, ms, vs):
    rows = gs.shape[0]
    vmem = pl.BlockSpec(memory_space=pltpu.VMEM)

    def body(g_ref, w_ref, m_ref, v_ref, sum_ref, d_ref, m2_ref, v2_ref, buf, send_sems, recv_sems):
        x, y, c = _place()
        me = 4 * x + 2 * y + c
        buf[me] = g_ref[...]
        copies = []
        for k in range(1, 8):
            peer = (x ^ (k >> 2), y ^ ((k >> 1) & 1), c ^ (k & 1))
            copies.append(pltpu.make_async_remote_copy(
                src_ref=g_ref, dst_ref=buf.at[me], send_sem=send_sems.at[k - 1], recv_sem=recv_sems.at[k - 1],
                device_id=peer, device_id_type=MESH))
        for cp in copies:
            cp.start()
        for cp in copies:
            cp.wait()
        total = buf[0]
        for i in range(1, 8):
            total = total + buf[i]
        sum_ref[...] = total
        d, m2, v2 = _adam_math(w_ref[...], total, m_ref[...], v_ref[...])
        d_ref[...] = d
        m2_ref[...] = m2
        v2_ref[...] = v2

    shape = jax.ShapeDtypeStruct((rows, 128), F32)
    return pl.pallas_call(
        body, name="small_sync", out_shape=[shape] * 4, in_specs=[vmem] * 4, out_specs=[vmem] * 4,
        scratch_shapes=[pltpu.VMEM((8, rows, 128), F32), pltpu.SemaphoreType.DMA((7,)),
                        pltpu.SemaphoreType.DMA((7,))])(gs, ws, ms, vs)


_GROUPS = {
    "ffn1": dict(cols=("ffn1_w_in", 1408), rows=(("ffn1_w_out", 704, 704),), chunks=(8, 2)),
    "ffn2": dict(cols=("ffn2_w_in", 1408), rows=(("ffn2_w_out", 704, 704),), chunks=(8, 2)),
    "mixer_in": dict(cols=("w_in", 3080), rows=(("gdn_conv_w", CONV_K, 128),), chunks=(8, 1)),
    "mixer_out": dict(cols=None, chunks=(4,),
                      rows=(("w_branch_hgrn", 256, 256), ("w_branch_gdn", 512, 512), ("w_out", 256, 256))),
}


def _group_names(group):
    return ((group["cols"][0],) if group["cols"] else ()) + tuple(r[0] for r in group["rows"])


_BIG_NAMES = tuple(n for g in _GROUPS.values() for n in _group_names(g))


def _pack(parts, lead, group):
    ax = len(lead)
    rows = []
    for n, r, padded in group["rows"]:
        p = parts[n]
        if padded != r:
            p = jnp.tile(p, (1,) * ax + (padded // r, 1))
        rows.append(p)
    stacked = rows[0] if len(rows) == 1 else jnp.concatenate(rows, axis=ax)
    return ([parts[group["cols"][0]]] if group["cols"] else []) + [stacked]


def _unpack(packs, group):
    out, off = ({group["cols"][0]: packs[0]} if group["cols"] else {}), 0
    for n, r, padded in group["rows"]:
        out[n] = packs[-1][..., off:off + r, :]
        off += padded
    return out


def _is_col_sharded(name):
    return name in ("ffn1_w_in", "ffn2_w_in", "w_in", "gdn_conv_w")


def _full_from_shards(name, g):
    if _is_col_sharded(name):
        return jnp.transpose(g, (1, 0, 2)).reshape(g.shape[1], -1)
    return g.reshape(-1, g.shape[2])


def _shards_from_full(name, full):
    if _is_col_sharded(name):
        return jnp.transpose(full.reshape(full.shape[0], 4, -1), (1, 0, 2))
    return full.reshape(4, -1, full.shape[1])


_SMALL = (("ffn1_norm", 8), ("mix_norm", 8), ("hgrn_lb_logits", 16), ("hgrn_out_norm", 8), ("gdn_a_log", 8),
          ("gdn_dt_bias", 8), ("gdn_out_norm", 8), ("ffn2_norm", 8), ("final_norm", 8), ("loss", 8))
_SMALL_ROWS = sum(r for _, r in _SMALL)


def _pack_small(parts):
    out = []
    for name, rows in _SMALL:
        p = parts[name].reshape(-1).astype(F32)
        if p.shape[0] <= 128:
            if p.shape[0] < 128:
                p = jnp.concatenate([p, jnp.zeros((128 - p.shape[0],), F32)])
            p = jnp.broadcast_to(p.reshape(1, 128), (rows, 128))
        out.append(p.reshape(rows, 128))
    return jnp.concatenate(out, axis=0)


def _unpack_small(packed, shapes):
    out, off = {}, 0
    for name, rows in _SMALL:
        n = int(np.prod(shapes[name]))
        out[name] = packed[off:off + rows].reshape(-1)[:n].reshape(shapes[name])
        off += rows
    return out


def _ffn_fwd(x, gain, w_in, w_out, tag):
    n = _rmsnorm_fwd(x, gain, tag + "_norm")
    a, b, hm = _ffn_in_act(n, w_in, tag + "_in")
    out = _mm(hm, w_out, alpha=0.5, res=x, name=tag + "_out")
    return out, (n, a, b)


def _ffn_bwd(x, gain, w_in, w_out, saved, dout, dout_bf, tag):
    n, a, b = saved
    da, db, hm = _ffn_dact(dout_bf, w_out, a, b, tag + "_dact")
    dw_out = _mm(hm, dout_bf, ta=True, alpha=0.5, out_dtype=BF16, name=tag + "_dwout")
    dwa = _mm(n, da, ta=True, out_dtype=BF16, name=tag + "_dwin_a")
    dwb = _mm(n, db, ta=True, out_dtype=BF16, name=tag + "_dwin_b")
    half = D_FF // 2
    dw_in = jnp.stack([dwa[:, :half], dwa[:, half:], dwb[:, :half], dwb[:, half:]])
    dn = _mm(da, w_in, tb=True, name=tag + "_dnorm_a")
    dn = _mm(db, w_in, tb=True, res=dn, b_from=D_FF, name=tag + "_dnorm_b")
    dx, dx_bf, dgain = _rmsnorm_bwd(x, gain, dn, dout, tag + "_dx")
    return dx, dx_bf, dgain, dw_in, dw_out


def _pad_lanes(v):
    return jnp.concatenate([v.reshape(1, -1), jnp.zeros((1, HEAD - v.size), F32)], axis=1)


def _local_step(x, tgt, small, exchange):
    hg_c = _hg_consts()
    gd_c = _gd_consts()
    alog = _pad_lanes(small["gdn_a_log"])
    dtb = _pad_lanes(small["gdn_dt_bias"])
    logits = small["hgrn_lb_logits"]
    hg_gain = small["hgrn_out_norm"].reshape(1, HEAD)
    gd_gain = small["gdn_out_norm"].reshape(1, HEAD)
    g1, gm, g2 = small["ffn1_norm"].reshape(1, -1), small["mix_norm"].reshape(1, -1), small["ffn2_norm"].reshape(1, -1)
    gf = small["final_norm"].reshape(1, -1)
    qscale = HEAD ** -0.5

    w1 = exchange.weights("ffn1")
    started = exchange.prefetch("mixer_in")
    h1, ffn1_saved = _ffn_fwd(x, g1 + started, w1["ffn1_w_in"], w1["ffn1_w_out"], "ffn1")
    u = _rmsnorm_fwd(h1, gm, "mix_norm")
    w = exchange.weights("mixer_in", after=u)
    arrived = w["gdn_conv_w"]
    started = (exchange.prefetch("mixer_out", arrived, to_both=True)
               + exchange.prefetch("ffn2", arrived, to_both=True))
    seg, off = {}, 0
    for name, size in zip(IN_NAMES, IN_SIZES):
        seg[name] = w["w_in"][:, off:off + size]
        off += size
    w_gab = jnp.concatenate([seg["ga"], seg["gb"], jnp.zeros((D_MODEL, HEAD - 32), BF16)], axis=1)
    big_segs = [n for n in IN_NAMES if n not in ("ga", "gb")]
    conv8 = jnp.concatenate([w["gdn_conv_w"].astype(F32), jnp.zeros((8 - CONV_K, 4096), F32)], axis=0)
    conv_q, conv_k, conv_v = conv8[:, :1024], conv8[:, 1024:2048], conv8[:, 2048:]
    w_main = jnp.concatenate([seg[n] for n in big_segs], axis=1)
    proj = _mm(u, w_main, name="proj", tm_max=2048)
    pr, off = {}, 0
    for n in big_segs:
        pr[n] = _view(proj, off, seg[n].shape[1])
        off += seg[n].shape[1]
    gab = _mm(u, w_gab, name="proj_gab")
    oh_raw, oh, s_h = _hgrn_fwd(pr["hq"], pr["hf"], pr["hi"], pr["hg"], logits, hg_gain + started, hg_c)
    qn = _conv_fwd(pr["gq"], conv_q, qscale, "conv_q")
    kn = _conv_fwd(pr["gk"], conv_k, 1.0, "conv_k")
    cv = _conv_fwd(pr["gv"], conv_v, None, "conv_v")
    og_raw, og, s_g, t_g = _gdn_fwd(qn, kn, cv, gab, pr["gz"], alog, dtb, gd_gain, gd_c)
    wo = exchange.weights("mixer_out", after=og)
    yh = _mm(oh, wo["w_branch_hgrn"], out_dtype=BF16, name="branch_h")
    yg = _mm(og, wo["w_branch_gdn"], out_dtype=BF16, name="branch_g")
    ym = _merge_fwd(yh, yg, pr["gate_h"], pr["gate_g"])
    h2 = _mm(ym, wo["w_out"], res=h1, name="mix_out")
    w2 = exchange.weights("ffn2", after=h2)
    h3, ffn2_saved = _ffn_fwd(h2, g2, w2["ffn2_w_in"], w2["ffn2_w_out"], "ffn2")
    loss, dh3, dh3_bf, d_gf = _final_loss(h3, gf, tgt)

    dh2, dh2_bf, d_g2, d_f2in, d_f2out = _ffn_bwd(h2, g2, w2["ffn2_w_in"], w2["ffn2_w_out"], ffn2_saved, dh3, dh3_bf,
                                                  "ffn2")
    started = exchange.reduce("ffn2", {"ffn2_w_in": d_f2in, "ffn2_w_out": d_f2out}, behind=True)
    dym = _mm(dh2_bf, wo["w_out"], tb=True, name="d_merge")
    d_wout = _mm(ym, dh2_bf, ta=True, out_dtype=BF16, name="d_w_out")
    dproj = lax.empty((x.shape[0], w_main.shape[1]), BF16)
    dyh, dyg, dproj = _merge_bwd(dym, yh, yg, pr["gate_h"], pr["gate_g"], _into(dproj, pr["gate_h"][1], 2 * D_MODEL))
    d_wbh = _mm(oh, dyh, ta=True, out_dtype=BF16, name="d_w_branch_h")
    d_wbg = _mm(og, dyg, ta=True, out_dtype=BF16, name="d_w_branch_g")
    started = started + exchange.reduce("mixer_out", {"w_branch_hgrn": d_wbh, "w_branch_gdn": d_wbg, "w_out": d_wout},
                                        behind=True)
    doh = _mm(dyh, wo["w_branch_hgrn"], tb=True, name="d_oh")
    dog = _mm(dyg, wo["w_branch_gdn"], tb=True, name="d_og")
    dproj, d_hg_gain, d_lb0 = _hgrn_bwd(pr["hq"], pr["hf"], pr["hi"], pr["hg"], logits, hg_gain + started, oh_raw,
                                        s_h, doh, hg_c, _into(dproj, pr["hq"][1], 4 * D_MODEL))
    d_qn, d_kn, d_cv, d_gab_wide, dproj, gd_small = _gdn_bwd(qn, kn, cv, gab, pr["gz"], alog, dtb, gd_gain, og_raw,
                                                             s_g, t_g, dog, gd_c, _into(dproj, *pr["gz"][1:]))
    d_gab = _fold_groups(d_gab_wide)
    dc_q, dwc_q = _conv_bwd_a(pr["gq"], conv_q, d_qn, qscale, "dconv_q")
    dc_k, dwc_k = _conv_bwd_a(pr["gk"], conv_k, d_kn, 1.0, "dconv_k")
    dc_v, dwc_v = _conv_bwd_a(pr["gv"], conv_v, d_cv, None, "dconv_v")
    dproj = _conv_bwd_b(dc_q, conv_q, "dconvx_q", _into(dproj, *pr["gq"][1:]))
    dproj = _conv_bwd_b(dc_k, conv_k, "dconvx_k", _into(dproj, *pr["gk"][1:]))
    dproj = _conv_bwd_b(dc_v, conv_v, "dconvx_v", _into(dproj, *pr["gv"][1:]))
    du =_mm(d_gab, w_gab, tb=True, name="du_gab")
    du = _mm(dproj, w_main, tb=True, res=du, name="du")
    d_wmain = _mm(u, dproj, ta=True, out_dtype=BF16, name="dw_main")
    d_wgab = _mm(u, d_gab, ta=True, out_dtype=BF16, name="dw_gab")
    cut = IN_WIDTH // 4
    d_win = jnp.stack([d_wmain[:, :cut], d_wmain[:, cut:2 * cut],
                       jnp.concatenate([d_wmain[:, 2 * cut:8192], d_wgab[:, :32], d_wmain[:, 8192:3 * cut - 32]], axis=1),
                       d_wmain[:, 3 * cut - 32:]])
    d_conv = jnp.concatenate([dwc_q[:CONV_K], dwc_k[:CONV_K], dwc_v[:CONV_K]], axis=1).astype(BF16)
    started = exchange.reduce("mixer_in", {"w_in": d_win, "gdn_conv_w": d_conv}, behind=True)
    dh1, dh1_bf, d_gm = _rmsnorm_bwd(h1, gm + started, du, dh2, "mix_dnorm")
    dx, _, d_g1, d_f1in, d_f1out = _ffn_bwd(x, g1, w1["ffn1_w_in"], w1["ffn1_w_out"], ffn1_saved, dh1, dh1_bf, "ffn1")
    exchange.reduce("ffn1", {"ffn1_w_in": d_f1in, "ffn1_w_out": d_f1out}, behind=True)
    d_lb0 = d_lb0.reshape(1, -1)
    sm = {"ffn1_norm": d_g1, "mix_norm": d_gm, "hgrn_lb_logits": jnp.concatenate([d_lb0, -d_lb0], axis=0),
          "hgrn_out_norm": d_hg_gain, "gdn_a_log": gd_small[2, :16], "gdn_dt_bias": gd_small[1, :16],
          "gdn_out_norm": gd_small[0], "ffn2_norm": d_g2, "final_norm": d_gf, "loss": loss[0, :1]}
    return dx, sm


class _Exchange:
    def __init__(self, wts):
        self.wts = wts
        xi, yi, ci = _place()
        self.chip = 2 * xi + yi
        self.south = ci == 0
        self.core = ci.reshape(1).astype(jnp.int32)
        self.mine = {}
        self.coming = {}
        self.going = {}

    def _packs(self, tag):
        group = _GROUPS[tag]
        return _pack({n: self.wts[n][0].astype(BF16) for n in _group_names(group)}, (), group)

    def prefetch(self, tag, after=None, to_both=False):
        packs = self._packs(tag)
        kind = "gather_both" if to_both else "gather"
        handle, token = _ici_start(packs, _GROUPS[tag]["chunks"], kind, "gather_start_" + tag, after)
        self.coming[tag] = handle
        return token[0:1, 0:1]

    def weights(self, tag, after=None):
        group = _GROUPS[tag]
        if tag in self.coming:
            handle = self.coming.pop(tag)
            packs, others = _ici_wait(handle, after, "gather_wait_" + tag)
            if handle[-1] == "gather":
                others = _pass_to_sibling(others, group["chunks"], "gather_pass_" + tag)
        else:
            packs = self._packs(tag)
            others = _gather_weights(packs, group["chunks"], "gather_" + tag)
        whole = [lax.dynamic_update_index_in_dim(g, p, self.chip, 0) for g, p in zip(others, packs)]
        gathered = _unpack(whole, group)
        return {n: _full_from_shards(n, gathered[n]) for n in _group_names(group)}

    def reduce(self, tag, grads, behind=False):
        group = _GROUPS[tag]
        shards = {n: (grads[n] if grads[n].ndim == 3 else _shards_from_full(n, grads[n])) for n in _group_names(group)}
        gpacks = _pack(shards, (4,), group)
        got = _swap_with_sibling(gpacks, group["chunks"], 4, "reduce_pair_" + tag, halves=True)
        sums = [_add2(a, b, self.core, "add_pair_%s_%d" % (tag, i)) for i, (a, b) in enumerate(zip(gpacks, got))]
        if behind:
            handle, token = _ici_start(sums, group["chunks"], "reduce", "reduce_start_" + tag)
            self.going[tag] = handle
            self.token = token
            return token[0:1, 0:1]
        self._add_chips(tag, sums, _reduce_chips(sums, group["chunks"], "reduce_chips_" + tag))
        return None

    def _add_chips(self, tag, sums, from_chips):
        self.mine[tag] = [_add4(lax.dynamic_index_in_dim(s, self.chip, axis=0, keepdims=False), f,
                                "add_chips_%s_%d" % (tag, i)) for i, (s, f) in enumerate(zip(sums, from_chips))]

    def finish(self, tags, after):
        for tag in tags:
            if tag in self.going:
                self._add_chips(tag, *_ici_wait(self.going.pop(tag), after, "reduce_wait_" + tag))
        mine = [a for t in tags for a in self.mine[t]]
        nchs = [k for t in tags for k in _GROUPS[t]["chunks"]]
        theirs = _swap_with_sibling(mine, nchs, 0, "share_pair_" + tags[0])
        whole = [jnp.concatenate([jnp.where(self.south, a, b), jnp.where(self.south, b, a)], axis=0)
                 for a, b in zip(mine, theirs)]
        reduced, at = {}, 0
        for t in tags:
            n = len(self.mine[t])
            reduced.update(_unpack(whole[at:at + n], _GROUPS[t]))
            at += n
        return reduced


_WEIGHTS = ("ffn1_norm", "ffn1_w_in", "ffn1_w_out", "mix_norm", "w_in", "hgrn_lb_logits", "hgrn_out_norm",
            "gdn_conv_w", "gdn_a_log", "gdn_dt_bias", "gdn_out_norm", "w_branch_hgrn", "w_branch_gdn", "w_out",
            "ffn2_norm", "ffn2_w_in", "ffn2_w_out", "final_norm")


def kernel(x, ffn1_norm, ffn1_w_in, ffn1_w_out, mix_norm, w_in, hgrn_lb_logits, hgrn_out_norm, gdn_conv_w, gdn_a_log, gdn_dt_bias, gdn_out_norm, w_branch_hgrn, w_branch_gdn, w_out, ffn2_norm, ffn2_w_in, ffn2_w_out, final_norm, loss_target, m_ffn1_norm, m_ffn1_w_in, m_ffn1_w_out, m_mix_norm, m_w_in, m_hgrn_lb_logits, m_hgrn_out_norm, m_gdn_conv_w, m_gdn_a_log, m_gdn_dt_bias, m_gdn_out_norm, m_w_branch_hgrn, m_w_branch_gdn, m_w_out, m_ffn2_norm, m_ffn2_w_in, m_ffn2_w_out, m_final_norm, v_ffn1_norm, v_ffn1_w_in, v_ffn1_w_out, v_mix_norm, v_w_in, v_hgrn_lb_logits, v_hgrn_out_norm, v_gdn_conv_w, v_gdn_a_log, v_gdn_dt_bias, v_gdn_out_norm, v_w_branch_hgrn, v_w_branch_gdn, v_w_out, v_ffn2_norm, v_ffn2_w_in, v_ffn2_w_out, v_final_norm):
    args = dict(locals())
    wts = {n: args[n] for n in _WEIGHTS}
    moms = {n: args["m_" + n] for n in _WEIGHTS}
    vars_ = {n: args["v_" + n] for n in _WEIGHTS}

    small = {n: wts[n].astype(F32) for n in _WEIGHTS if n not in _BIG_NAMES}
    exchange = _Exchange(wts)
    dx, small_grads = _local_step(x[0], loss_target[0], small, exchange)

    out_g, out_d, out_m, out_v = {}, {}, {}, {}

    def update(tags, reduced, after):
        for t in tags:
            for n in _group_names(_GROUPS[t]):
                shape = wts[n].shape
                w2 = wts[n].reshape(shape[-2], shape[-1])
                g2 = reduced[n]
                d, m2, v2 = _adamw(w2, g2, moms[n].reshape(w2.shape), vars_[n].reshape(w2.shape), "adamw_" + n, after)
                out_g[n], out_d[n], out_m[n], out_v[n] = (g2.reshape(shape), d.reshape(shape), m2.reshape(shape),
                                                          v2.reshape(shape))
                after = v2
        return after

    early = ("ffn2", "mixer_out", "mixer_in")
    done = update(early, exchange.finish(early, after=dx), exchange.token)
    update(("ffn1",), exchange.finish(("ffn1",), after=done), None)

    small_names = [n for n, _ in _SMALL]
    zero = jnp.zeros((1,), F32)
    shapes = {n: (wts[n].shape if n != "loss" else (1,)) for n in small_names}
    sums, sd, sm_, sv = _small_sync(
        _pack_small(small_grads),
        _pack_small({n: (wts[n] if n != "loss" else zero) for n in small_names}),
        _pack_small({n: (moms[n] if n != "loss" else zero) for n in small_names}),
        _pack_small({n: (vars_[n] if n != "loss" else zero) for n in small_names}))
    sg_u, sd_u, sm_u, sv_u = (_unpack_small(p, shapes) for p in (sums, sd, sm_, sv))
    for n in small_names:
        if n != "loss":
            out_g[n], out_d[n], out_m[n], out_v[n] = sg_u[n], sd_u[n], sm_u[n], sv_u[n]
    loss = sg_u["loss"].reshape(())

    return (loss, dx[None], *[out_g[n] for n in _WEIGHTS], *[out_d[n] for n in _WEIGHTS],
            *[out_m[n] for n in _WEIGHTS], *[out_v[n] for n in _WEIGHTS])
```
